```python
import math
import jax, jax.numpy as jnp
from jax import lax
import numpy as np

D_MODEL = 1024
BATCH = 8
SEQ = 4096
DEPTH = 2

N_A_LAYERS = DEPTH // 2
N_B_LAYERS = DEPTH - N_A_LAYERS
SSM_EXPAND = 2
D_INNER = SSM_EXPAND * D_MODEL
SSM_HEAD_DIM = 64
SSM_HEADS = D_INNER // SSM_HEAD_DIM
SSM_GROUPS = 4
SSM_STATE = 128
SSM_CONV = 4
SSM_CHUNK = 128
GN = SSM_GROUPS * SSM_STATE
CONV_DIM = D_INNER + 2 * GN
IN_PROJ_DIM = D_INNER + CONV_DIM + SSM_HEADS
SB_HEADS = 16
SB_HEAD_DIM = D_MODEL // SB_HEADS
SB_BLOCK = 128
D_FF = 2816
FFN_CONV = 3
EPS = 1e-6

kernel_name = 'yoco_mamba2_stickbreaking_convffn'


def rmsnorm(x, w):
    xf = x.astype(jnp.float32)
    y = xf * lax.rsqrt(jnp.mean(xf * xf, axis=-1, keepdims=True) + EPS)
    return (y * w.astype(jnp.float32)).astype(x.dtype)


def causal_dwconv(x, w, b):
    k_w, s = w.shape[0], x.shape[1]
    xp = jnp.pad(x, ((0, 0), (k_w - 1, 0), (0, 0)))
    out = b
    for j in range(k_w):
        out = out + xp[:, j:j + s, :] * w[j]
    return out


def segsum(a):
    t = a.shape[-1]
    cs = jnp.cumsum(a, axis=-1)
    diff = cs[..., :, None] - cs[..., None, :]
    mask = jnp.tril(jnp.ones((t, t), dtype=bool))
    return jnp.where(mask, diff, -jnp.inf)


def ssd_chunked(xs, dt, a, bm, cm):
    bsz, s, h, p = xs.shape
    g, n = bm.shape[2], bm.shape[3]
    r = h // g
    c, l = s // SSM_CHUNK, SSM_CHUNK
    xd = (xs * dt[..., None]).reshape(bsz, c, l, g, r, p)
    ad = (dt * a).reshape(bsz, c, l, g, r).transpose(0, 3, 4, 1, 2)
    bc = bm.reshape(bsz, c, l, g, n)
    cc = cm.reshape(bsz, c, l, g, n)
    a_cs = jnp.cumsum(ad, axis=-1)
    lmat = jnp.exp(segsum(ad))
    cb = jnp.einsum('bclgn,bcsgn->bcgls', cc, bc)
    y_diag = jnp.einsum('bcgls,bgrcls,bcsgrp->bclgrp', cb, lmat, xd)
    decay_states = jnp.exp(a_cs[..., -1:] - a_cs)
    states = jnp.einsum('bclgn,bgrcl,bclgrp->bcgrpn', bc, decay_states, xd)
    chunk_decay = jnp.exp(a_cs[..., -1])

    def step(carry, inp):
        st, dc = inp
        return carry * dc[..., None, None] + st, carry

    init = jnp.zeros((bsz, g, r, p, n), dtype=states.dtype)
    _, prev = lax.scan(step, init, (jnp.moveaxis(states, 1, 0), jnp.moveaxis(chunk_decay, -1, 0)))
    prev = jnp.moveaxis(prev, 0, 1)
    y_off = jnp.einsum('bclgn,bcgrpn,bgrcl->bclgrp', cc, prev, jnp.exp(a_cs))
    return (y_diag + y_off).reshape(bsz, s, h, p)


def mamba2_mixer(u, w_in, conv_w, conv_b, dt_bias, a_log, d_skip, gate_norm_w, w_out):
    bsz, s, _ = u.shape
    f32 = jnp.float32
    zxbcdt = u @ w_in
    z = zxbcdt[..., :D_INNER]
    xbc = zxbcdt[..., D_INNER:D_INNER + CONV_DIM]
    dt = zxbcdt[..., D_INNER + CONV_DIM:]
    xbc = jax.nn.silu(causal_dwconv(xbc, conv_w, conv_b))
    xs = xbc[..., :D_INNER].reshape(bsz, s, SSM_HEADS, SSM_HEAD_DIM).astype(f32)
    bm = xbc[..., D_INNER:D_INNER + GN].reshape(bsz, s, SSM_GROUPS, SSM_STATE).astype(f32)
    cm = xbc[..., D_INNER + GN:].reshape(bsz, s, SSM_GROUPS, SSM_STATE).astype(f32)
    dt = jax.nn.softplus(dt.astype(f32) + dt_bias.astype(f32))
    a = -jnp.exp(a_log.astype(f32))
    y = ssd_chunked(xs, dt, a, bm, cm)
    y = y + d_skip.astype(f32)[:, None] * xs
    y = y.reshape(bsz, s, D_INNER) * jax.nn.silu(z.astype(f32))
    gsz = D_INNER // SSM_GROUPS
    y = rmsnorm(y.reshape(bsz, s, SSM_GROUPS, gsz), gate_norm_w.reshape(SSM_GROUPS, gsz))
    return y.reshape(bsz, s, D_INNER).astype(u.dtype) @ w_out


def stick_breaking_attention(q, k, v):
    bsz, s, h, d = q.shape
    nblk = s // SB_BLOCK
    scale = 1.0 / math.sqrt(d)
    qb = q.reshape(bsz, nblk, SB_BLOCK, h, d).transpose(1, 0, 3, 2, 4)
    kt = k.transpose(0, 2, 1, 3).astype(jnp.float32)
    vt = v.transpose(0, 2, 1, 3).astype(jnp.float32)
    starts = jnp.arange(nblk, dtype=jnp.int32) * SB_BLOCK
    kpos = jnp.arange(s, dtype=jnp.int32)

    def block(args):
        qblk, i0 = args
        qpos = i0 + jnp.arange(SB_BLOCK, dtype=jnp.int32)
        mask = kpos[None, :] < qpos[:, None]
        zl = jnp.einsum('bhqd,bhkd->bhqk', qblk.astype(jnp.float32), kt) * scale
        log_1m = jnp.where(mask, -jax.nn.softplus(zl), 0.0)
        later = lax.cumsum(log_1m, axis=3, reverse=True) - log_1m
        att = jnp.where(mask, jnp.exp(jax.nn.log_sigmoid(zl) + later), 0.0)
        return jnp.einsum('bhqk,bhkd->bhqd', att, vt)

    o = lax.map(block, (qb, starts))
    return o.transpose(1, 0, 3, 2, 4).reshape(bsz, s, h * d).astype(q.dtype)


def conv_ffn(u, w_up, conv_w, conv_b, w_down):
    hid = causal_dwconv(u @ w_up, conv_w, conv_b)
    gate, val = hid[..., :D_FF], hid[..., D_FF:]
    return (jax.nn.silu(gate) * val) @ w_down


def _fwd_setup_inputs(seed: int = 0) -> dict:
    key = jax.random.key(seed)
    ks = jax.random.split(key, 24)
    f32 = jnp.float32

    def nrm(k, shape, scale):
        return jax.random.normal(k, shape, f32) * scale

    def gain(k, shape):
        return 1.0 + 0.02 * jax.random.normal(k, shape, f32)

    na, nb = N_A_LAYERS, N_B_LAYERS
    dt0 = jnp.exp(jax.random.uniform(ks[5], (na, SSM_HEADS), f32, math.log(1e-3), math.log(1e-1)))
    dt_bias = dt0 + jnp.log(-jnp.expm1(-dt0))
    a_log = jnp.log(jax.random.uniform(ks[6], (na, SSM_HEADS), f32, 1.0, 16.0))
    return {
        'x': jax.random.normal(ks[0], (BATCH, SEQ, D_MODEL), f32),
        'ssm_norm_w': gain(ks[1], (na, D_MODEL)),
        'ssm_in_w': nrm(ks[2], (na, D_MODEL, IN_PROJ_DIM), D_MODEL ** -0.5),
        'ssm_conv_w': nrm(ks[3], (na, SSM_CONV, CONV_DIM), SSM_CONV ** -0.5),
        'ssm_conv_b': nrm(ks[4], (na, CONV_DIM), 0.02),
        'ssm_dt_bias': dt_bias,
        'ssm_a_log': a_log,
        'ssm_d': 1.0 + 0.1 * jax.random.normal(ks[7], (na, SSM_HEADS), f32),
        'ssm_gate_norm_w': gain(ks[8], (na, D_INNER)),
        'ssm_out_w': nrm(ks[9], (na, D_INNER, D_MODEL), D_INNER ** -0.5),
        'kv_norm_w': gain(ks[10], (D_MODEL,)),
        'w_k': nrm(ks[11], (D_MODEL, SB_HEADS * SB_HEAD_DIM), D_MODEL ** -0.5),
        'w_v': nrm(ks[12], (D_MODEL, SB_HEADS * SB_HEAD_DIM), D_MODEL ** -0.5),
        'attn_norm_w': gain(ks[13], (nb, D_MODEL)),
        'w_q': nrm(ks[14], (nb, D_MODEL, SB_HEADS * SB_HEAD_DIM), D_MODEL ** -0.5),
        'w_o': nrm(ks[15], (nb, SB_HEADS * SB_HEAD_DIM, D_MODEL), D_MODEL ** -0.5),
        'ffn_norm_w': gain(ks[16], (DEPTH, D_MODEL)),
        'ffn_up_w': nrm(ks[17], (DEPTH, D_MODEL, 2 * D_FF), D_MODEL ** -0.5),
        'ffn_conv_w': nrm(ks[18], (DEPTH, FFN_CONV, 2 * D_FF), FFN_CONV ** -0.5),
        'ffn_conv_b': nrm(ks[19], (DEPTH, 2 * D_FF), 0.02),
        'ffn_down_w': nrm(ks[20], (DEPTH, D_FF, D_MODEL), D_FF ** -0.5),
        'final_norm_w': gain(ks[21], (D_MODEL,)),
    }


def _fwd_reference(x, ssm_norm_w, ssm_in_w, ssm_conv_w, ssm_conv_b, ssm_dt_bias, ssm_a_log, ssm_d,
              ssm_gate_norm_w, ssm_out_w, kv_norm_w, w_k, w_v, attn_norm_w, w_q, w_o,
              ffn_norm_w, ffn_up_w, ffn_conv_w, ffn_conv_b, ffn_down_w, final_norm_w):
    bsz, s, _ = x.shape
    h = x
    k_shared = None
    v_shared = None
    for layer in range(DEPTH):
        if layer < N_A_LAYERS:
            i = layer
            h = h + mamba2_mixer(rmsnorm(h, ssm_norm_w[i]), ssm_in_w[i], ssm_conv_w[i], ssm_conv_b[i],
                                 ssm_dt_bias[i], ssm_a_log[i], ssm_d[i], ssm_gate_norm_w[i], ssm_out_w[i])
        else:
            i = layer - N_A_LAYERS
            if k_shared is None:
                hk = rmsnorm(h, kv_norm_w)
                k_shared = (hk @ w_k).reshape(bsz, s, SB_HEADS, SB_HEAD_DIM)
                v_shared = (hk @ w_v).reshape(bsz, s, SB_HEADS, SB_HEAD_DIM)
            q = (rmsnorm(h, attn_norm_w[i]) @ w_q[i]).reshape(bsz, s, SB_HEADS, SB_HEAD_DIM)
            h = h + stick_breaking_attention(q, k_shared, v_shared) @ w_o[i]
        h = h + conv_ffn(rmsnorm(h, ffn_norm_w[layer]), ffn_up_w[layer], ffn_conv_w[layer],
                         ffn_conv_b[layer], ffn_down_w[layer])
    return rmsnorm(h, final_norm_w)


import jax as _jax
import jax.numpy as _jnp

TWIN_FORMAT = 'train_step'
FWD_PARAMS = ['x', 'ssm_norm_w', 'ssm_in_w', 'ssm_conv_w', 'ssm_conv_b', 'ssm_dt_bias', 'ssm_a_log', 'ssm_d', 'ssm_gate_norm_w', 'ssm_out_w', 'kv_norm_w', 'w_k', 'w_v', 'attn_norm_w', 'w_q', 'w_o', 'ffn_norm_w', 'ffn_up_w', 'ffn_conv_w', 'ffn_conv_b', 'ffn_down_w', 'final_norm_w']
TWIN_WEIGHTS = ['ssm_norm_w', 'ssm_in_w', 'ssm_conv_w', 'ssm_conv_b', 'ssm_dt_bias', 'ssm_a_log', 'ssm_d', 'ssm_gate_norm_w', 'ssm_out_w', 'kv_norm_w', 'w_k', 'w_v', 'attn_norm_w', 'w_q', 'w_o', 'ffn_norm_w', 'ffn_up_w', 'ffn_conv_w', 'ffn_conv_b', 'ffn_down_w', 'final_norm_w']
TWIN_DIFF_INPUT = 'x'
TWIN_INPUTS = ['x', 'ssm_norm_w', 'ssm_in_w', 'ssm_conv_w', 'ssm_conv_b', 'ssm_dt_bias', 'ssm_a_log', 'ssm_d', 'ssm_gate_norm_w', 'ssm_out_w', 'kv_norm_w', 'w_k', 'w_v', 'attn_norm_w', 'w_q', 'w_o', 'ffn_norm_w', 'ffn_up_w', 'ffn_conv_w', 'ffn_conv_b', 'ffn_down_w', 'final_norm_w', 'loss_target', 'm_ssm_norm_w', 'm_ssm_in_w', 'm_ssm_conv_w', 'm_ssm_conv_b', 'm_ssm_dt_bias', 'm_ssm_a_log', 'm_ssm_d', 'm_ssm_gate_norm_w', 'm_ssm_out_w', 'm_kv_norm_w', 'm_w_k', 'm_w_v', 'm_attn_norm_w', 'm_w_q', 'm_w_o', 'm_ffn_norm_w', 'm_ffn_up_w', 'm_ffn_conv_w', 'm_ffn_conv_b', 'm_ffn_down_w', 'm_final_norm_w', 'v_ssm_norm_w', 'v_ssm_in_w', 'v_ssm_conv_w', 'v_ssm_conv_b', 'v_ssm_dt_bias', 'v_ssm_a_log', 'v_ssm_d', 'v_ssm_gate_norm_w', 'v_ssm_out_w', 'v_kv_norm_w', 'v_w_k', 'v_w_v', 'v_attn_norm_w', 'v_w_q', 'v_w_o', 'v_ffn_norm_w', 'v_ffn_up_w', 'v_ffn_conv_w', 'v_ffn_conv_b', 'v_ffn_down_w', 'v_final_norm_w']
TWIN_OUTPUTS = ['loss', 'grad_x', 'grad_ssm_norm_w', 'grad_ssm_in_w', 'grad_ssm_conv_w', 'grad_ssm_conv_b', 'grad_ssm_dt_bias', 'grad_ssm_a_log', 'grad_ssm_d', 'grad_ssm_gate_norm_w', 'grad_ssm_out_w', 'grad_kv_norm_w', 'grad_w_k', 'grad_w_v', 'grad_attn_norm_w', 'grad_w_q', 'grad_w_o', 'grad_ffn_norm_w', 'grad_ffn_up_w', 'grad_ffn_conv_w', 'grad_ffn_conv_b', 'grad_ffn_down_w', 'grad_final_norm_w', 'delta_ssm_norm_w', 'delta_ssm_in_w', 'delta_ssm_conv_w', 'delta_ssm_conv_b', 'delta_ssm_dt_bias', 'delta_ssm_a_log', 'delta_ssm_d', 'delta_ssm_gate_norm_w', 'delta_ssm_out_w', 'delta_kv_norm_w', 'delta_w_k', 'delta_w_v', 'delta_attn_norm_w', 'delta_w_q', 'delta_w_o', 'delta_ffn_norm_w', 'delta_ffn_up_w', 'delta_ffn_conv_w', 'delta_ffn_conv_b', 'delta_ffn_down_w', 'delta_final_norm_w', 'new_m_ssm_norm_w', 'new_m_ssm_in_w', 'new_m_ssm_conv_w', 'new_m_ssm_conv_b', 'new_m_ssm_dt_bias', 'new_m_ssm_a_log', 'new_m_ssm_d', 'new_m_ssm_gate_norm_w', 'new_m_ssm_out_w', 'new_m_kv_norm_w', 'new_m_w_k', 'new_m_w_v', 'new_m_attn_norm_w', 'new_m_w_q', 'new_m_w_o', 'new_m_ffn_norm_w', 'new_m_ffn_up_w', 'new_m_ffn_conv_w', 'new_m_ffn_conv_b', 'new_m_ffn_down_w', 'new_m_final_norm_w', 'new_v_ssm_norm_w', 'new_v_ssm_in_w', 'new_v_ssm_conv_w', 'new_v_ssm_conv_b', 'new_v_ssm_dt_bias', 'new_v_ssm_a_log', 'new_v_ssm_d', 'new_v_ssm_gate_norm_w', 'new_v_ssm_out_w', 'new_v_kv_norm_w', 'new_v_w_k', 'new_v_w_v', 'new_v_attn_norm_w', 'new_v_w_q', 'new_v_w_o', 'new_v_ffn_norm_w', 'new_v_ffn_up_w', 'new_v_ffn_conv_w', 'new_v_ffn_conv_b', 'new_v_ffn_down_w', 'new_v_final_norm_w']
TWIN_LEAF_KINDS = {'loss': 'loss', 'grad_x': 'grad_x', 'grad_ssm_norm_w': 'grad_w', 'grad_ssm_in_w': 'grad_w', 'grad_ssm_conv_w': 'grad_w', 'grad_ssm_conv_b': 'grad_w', 'grad_ssm_dt_bias': 'grad_w', 'grad_ssm_a_log': 'grad_w', 'grad_ssm_d': 'grad_w', 'grad_ssm_gate_norm_w': 'grad_w', 'grad_ssm_out_w': 'grad_w', 'grad_kv_norm_w': 'grad_w', 'grad_w_k': 'grad_w', 'grad_w_v': 'grad_w', 'grad_attn_norm_w': 'grad_w', 'grad_w_q': 'grad_w', 'grad_w_o': 'grad_w', 'grad_ffn_norm_w': 'grad_w', 'grad_ffn_up_w': 'grad_w', 'grad_ffn_conv_w': 'grad_w', 'grad_ffn_conv_b': 'grad_w', 'grad_ffn_down_w': 'grad_w', 'grad_final_norm_w': 'grad_w', 'delta_ssm_norm_w': 'delta_w', 'delta_ssm_in_w': 'delta_w', 'delta_ssm_conv_w': 'delta_w', 'delta_ssm_conv_b': 'delta_w', 'delta_ssm_dt_bias': 'delta_w', 'delta_ssm_a_log': 'delta_w', 'delta_ssm_d': 'delta_w', 'delta_ssm_gate_norm_w': 'delta_w', 'delta_ssm_out_w': 'delta_w', 'delta_kv_norm_w': 'delta_w', 'delta_w_k': 'delta_w', 'delta_w_v': 'delta_w', 'delta_attn_norm_w': 'delta_w', 'delta_w_q': 'delta_w', 'delta_w_o': 'delta_w', 'delta_ffn_norm_w': 'delta_w', 'delta_ffn_up_w': 'delta_w', 'delta_ffn_conv_w': 'delta_w', 'delta_ffn_conv_b': 'delta_w', 'delta_ffn_down_w': 'delta_w', 'delta_final_norm_w': 'delta_w', 'new_m_ssm_norm_w': 'new_m', 'new_m_ssm_in_w': 'new_m', 'new_m_ssm_conv_w': 'new_m', 'new_m_ssm_conv_b': 'new_m', 'new_m_ssm_dt_bias': 'new_m', 'new_m_ssm_a_log': 'new_m', 'new_m_ssm_d': 'new_m', 'new_m_ssm_gate_norm_w': 'new_m', 'new_m_ssm_out_w': 'new_m', 'new_m_kv_norm_w': 'new_m', 'new_m_w_k': 'new_m', 'new_m_w_v': 'new_m', 'new_m_attn_norm_w': 'new_m', 'new_m_w_q': 'new_m', 'new_m_w_o': 'new_m', 'new_m_ffn_norm_w': 'new_m', 'new_m_ffn_up_w': 'new_m', 'new_m_ffn_conv_w': 'new_m', 'new_m_ffn_conv_b': 'new_m', 'new_m_ffn_down_w': 'new_m', 'new_m_final_norm_w': 'new_m', 'new_v_ssm_norm_w': 'new_v', 'new_v_ssm_in_w': 'new_v', 'new_v_ssm_conv_w': 'new_v', 'new_v_ssm_conv_b': 'new_v', 'new_v_ssm_dt_bias': 'new_v', 'new_v_ssm_a_log': 'new_v', 'new_v_ssm_d': 'new_v', 'new_v_ssm_gate_norm_w': 'new_v', 'new_v_ssm_out_w': 'new_v', 'new_v_kv_norm_w': 'new_v', 'new_v_w_k': 'new_v', 'new_v_w_v': 'new_v', 'new_v_attn_norm_w': 'new_v', 'new_v_w_q': 'new_v', 'new_v_w_o': 'new_v', 'new_v_ffn_norm_w': 'new_v', 'new_v_ffn_up_w': 'new_v', 'new_v_ffn_conv_w': 'new_v', 'new_v_ffn_conv_b': 'new_v', 'new_v_ffn_down_w': 'new_v', 'new_v_final_norm_w': 'new_v'}


def _forward(args):
    return _fwd_reference(*[args[k] for k in FWD_PARAMS])


def _output_shape():
    out = _jax.eval_shape(lambda: _forward(_fwd_setup_inputs(0)))
    return out.shape, out.dtype

N_MICROBATCH = 1
ADAM_LR = 0.001
ADAM_B1 = 0.9
ADAM_B2 = 0.999
ADAM_EPS = 1e-08
ADAM_WD = 0.01
ADAM_STEP = 10
PER_EXAMPLE_BATCH_AXIS = {'x': 0, 'loss_target': 0}
SHARED_INPUTS = []
_WEIGHT_DTYPES = {'ssm_norm_w': _jnp.float32, 'ssm_in_w': _jnp.float32, 'ssm_conv_w': _jnp.float32, 'ssm_conv_b': _jnp.float32, 'ssm_dt_bias': _jnp.float32, 'ssm_a_log': _jnp.float32, 'ssm_d': _jnp.float32, 'ssm_gate_norm_w': _jnp.float32, 'ssm_out_w': _jnp.float32, 'kv_norm_w': _jnp.float32, 'w_k': _jnp.float32, 'w_v': _jnp.float32, 'attn_norm_w': _jnp.float32, 'w_q': _jnp.float32, 'w_o': _jnp.float32, 'ffn_norm_w': _jnp.float32, 'ffn_up_w': _jnp.float32, 'ffn_conv_w': _jnp.float32, 'ffn_conv_b': _jnp.float32, 'ffn_down_w': _jnp.float32, 'final_norm_w': _jnp.float32}
MOMENT_SCALE = {'ssm_norm_w': 2.248148e-01, 'ssm_in_w': 1.000202e-01, 'ssm_conv_w': 9.206803e-02, 'ssm_conv_b': 1.225318e-01, 'ssm_dt_bias': 2.710052e-01, 'ssm_a_log': 4.447519e-01, 'ssm_d': 4.980662e-01, 'ssm_gate_norm_w': 1.038009e-01, 'ssm_out_w': 1.482609e-01, 'kv_norm_w': 8.529283e-02, 'w_k': 3.346443e-02, 'w_v': 7.376899e-02, 'attn_norm_w': 3.509310e-02, 'w_q': 3.344200e-02, 'w_o': 7.373714e-02, 'ffn_norm_w': 9.876697e-02, 'ffn_up_w': 4.171500e-02, 'ffn_conv_w': 4.206791e-02, 'ffn_conv_b': 4.140397e-02, 'ffn_down_w': 6.839148e-02, 'final_norm_w': 3.204760e+01}


def _to_microbatches(a, axis):
    t = _jnp.moveaxis(a, axis, 0)
    t = t.reshape((N_MICROBATCH, t.shape[0] // N_MICROBATCH) + t.shape[1:])
    return _jnp.moveaxis(t, 1, axis + 1)


def setup_inputs(seed: int = 0) -> dict:
    inp = _fwd_setup_inputs(seed)
    key = _jax.random.fold_in(_jax.random.key(seed), 7919)
    shape, _ = _output_shape()
    out = dict(inp)
    out["loss_target"] = _jax.random.normal(_jax.random.fold_in(key, 0), shape, _jnp.float32)
    for i, name in enumerate(TWIN_WEIGHTS):
        w = inp[name].astype(_jnp.float32)
        if MOMENT_SCALE is None:
            s = _jnp.sqrt(_jnp.mean(_jnp.square(w)) + 1e-30)
        else:
            s = MOMENT_SCALE[name]
        km, kv = _jax.random.split(_jax.random.fold_in(key, i + 1))
        out[name] = w
        out["m_" + name] = s * _jax.random.normal(km, w.shape, _jnp.float32)
        out["v_" + name] = (s * s) * _jax.random.uniform(kv, w.shape, _jnp.float32, 0.5, 1.5)
    if N_MICROBATCH > 1:
        for name, axis in PER_EXAMPLE_BATCH_AXIS.items():
            out[name] = _to_microbatches(out[name], axis)
    return {'x': out['x'], 'ssm_norm_w': out['ssm_norm_w'], 'ssm_in_w': out['ssm_in_w'], 'ssm_conv_w': out['ssm_conv_w'], 'ssm_conv_b': out['ssm_conv_b'], 'ssm_dt_bias': out['ssm_dt_bias'], 'ssm_a_log': out['ssm_a_log'], 'ssm_d': out['ssm_d'], 'ssm_gate_norm_w': out['ssm_gate_norm_w'], 'ssm_out_w': out['ssm_out_w'], 'kv_norm_w': out['kv_norm_w'], 'w_k': out['w_k'], 'w_v': out['w_v'], 'attn_norm_w': out['attn_norm_w'], 'w_q': out['w_q'], 'w_o': out['w_o'], 'ffn_norm_w': out['ffn_norm_w'], 'ffn_up_w': out['ffn_up_w'], 'ffn_conv_w': out['ffn_conv_w'], 'ffn_conv_b': out['ffn_conv_b'], 'ffn_down_w': out['ffn_down_w'], 'final_norm_w': out['final_norm_w'], 'loss_target': out['loss_target'], 'm_ssm_norm_w': out['m_ssm_norm_w'], 'm_ssm_in_w': out['m_ssm_in_w'], 'm_ssm_conv_w': out['m_ssm_conv_w'], 'm_ssm_conv_b': out['m_ssm_conv_b'], 'm_ssm_dt_bias': out['m_ssm_dt_bias'], 'm_ssm_a_log': out['m_ssm_a_log'], 'm_ssm_d': out['m_ssm_d'], 'm_ssm_gate_norm_w': out['m_ssm_gate_norm_w'], 'm_ssm_out_w': out['m_ssm_out_w'], 'm_kv_norm_w': out['m_kv_norm_w'], 'm_w_k': out['m_w_k'], 'm_w_v': out['m_w_v'], 'm_attn_norm_w': out['m_attn_norm_w'], 'm_w_q': out['m_w_q'], 'm_w_o': out['m_w_o'], 'm_ffn_norm_w': out['m_ffn_norm_w'], 'm_ffn_up_w': out['m_ffn_up_w'], 'm_ffn_conv_w': out['m_ffn_conv_w'], 'm_ffn_conv_b': out['m_ffn_conv_b'], 'm_ffn_down_w': out['m_ffn_down_w'], 'm_final_norm_w': out['m_final_norm_w'], 'v_ssm_norm_w': out['v_ssm_norm_w'], 'v_ssm_in_w': out['v_ssm_in_w'], 'v_ssm_conv_w': out['v_ssm_conv_w'], 'v_ssm_conv_b': out['v_ssm_conv_b'], 'v_ssm_dt_bias': out['v_ssm_dt_bias'], 'v_ssm_a_log': out['v_ssm_a_log'], 'v_ssm_d': out['v_ssm_d'], 'v_ssm_gate_norm_w': out['v_ssm_gate_norm_w'], 'v_ssm_out_w': out['v_ssm_out_w'], 'v_kv_norm_w': out['v_kv_norm_w'], 'v_w_k': out['v_w_k'], 'v_w_v': out['v_w_v'], 'v_attn_norm_w': out['v_attn_norm_w'], 'v_w_q': out['v_w_q'], 'v_w_o': out['v_w_o'], 'v_ffn_norm_w': out['v_ffn_norm_w'], 'v_ffn_up_w': out['v_ffn_up_w'], 'v_ffn_conv_w': out['v_ffn_conv_w'], 'v_ffn_conv_b': out['v_ffn_conv_b'], 'v_ffn_down_w': out['v_ffn_down_w'], 'v_final_norm_w': out['v_final_norm_w']}


def _loss(weights, diff, rest, loss_target):
    with _jax.named_scope("forward"):
        args = {**rest, TWIN_DIFF_INPUT: diff, **{k: w.astype(_WEIGHT_DTYPES[k]) for k, w in weights.items()}}
        y = _forward(args)
    with _jax.named_scope("loss_head"):
        err = _jnp.square(y.astype(_jnp.float32) - loss_target)
        return 0.5 * _jnp.sum(_jnp.mean(err, axis=-1)) if err.ndim else 0.5 * err


def _adamw(w, g, m, v):
    m = ADAM_B1 * m + (1.0 - ADAM_B1) * g
    v = ADAM_B2 * v + (1.0 - ADAM_B2) * _jnp.square(g)
    m_hat = m / (1.0 - ADAM_B1 ** ADAM_STEP)
    v_hat = v / (1.0 - ADAM_B2 ** ADAM_STEP)
    delta = -ADAM_LR * (m_hat / (_jnp.sqrt(v_hat) + ADAM_EPS) + ADAM_WD * w)
    return delta, m, v


def reference(x, ssm_norm_w, ssm_in_w, ssm_conv_w, ssm_conv_b, ssm_dt_bias, ssm_a_log, ssm_d, ssm_gate_norm_w, ssm_out_w, kv_norm_w, w_k, w_v, attn_norm_w, w_q, w_o, ffn_norm_w, ffn_up_w, ffn_conv_w, ffn_conv_b, ffn_down_w, final_norm_w, loss_target, m_ssm_norm_w, m_ssm_in_w, m_ssm_conv_w, m_ssm_conv_b, m_ssm_dt_bias, m_ssm_a_log, m_ssm_d, m_ssm_gate_norm_w, m_ssm_out_w, m_kv_norm_w, m_w_k, m_w_v, m_attn_norm_w, m_w_q, m_w_o, m_ffn_norm_w, m_ffn_up_w, m_ffn_conv_w, m_ffn_conv_b, m_ffn_down_w, m_final_norm_w, v_ssm_norm_w, v_ssm_in_w, v_ssm_conv_w, v_ssm_conv_b, v_ssm_dt_bias, v_ssm_a_log, v_ssm_d, v_ssm_gate_norm_w, v_ssm_out_w, v_kv_norm_w, v_w_k, v_w_v, v_attn_norm_w, v_w_q, v_w_o, v_ffn_norm_w, v_ffn_up_w, v_ffn_conv_w, v_ffn_conv_b, v_ffn_down_w, v_final_norm_w):
    given = dict(x=x, ssm_norm_w=ssm_norm_w, ssm_in_w=ssm_in_w, ssm_conv_w=ssm_conv_w, ssm_conv_b=ssm_conv_b, ssm_dt_bias=ssm_dt_bias, ssm_a_log=ssm_a_log, ssm_d=ssm_d, ssm_gate_norm_w=ssm_gate_norm_w, ssm_out_w=ssm_out_w, kv_norm_w=kv_norm_w, w_k=w_k, w_v=w_v, attn_norm_w=attn_norm_w, w_q=w_q, w_o=w_o, ffn_norm_w=ffn_norm_w, ffn_up_w=ffn_up_w, ffn_conv_w=ffn_conv_w, ffn_conv_b=ffn_conv_b, ffn_down_w=ffn_down_w, final_norm_w=final_norm_w, loss_target=loss_target, m_ssm_norm_w=m_ssm_norm_w, m_ssm_in_w=m_ssm_in_w, m_ssm_conv_w=m_ssm_conv_w, m_ssm_conv_b=m_ssm_conv_b, m_ssm_dt_bias=m_ssm_dt_bias, m_ssm_a_log=m_ssm_a_log, m_ssm_d=m_ssm_d, m_ssm_gate_norm_w=m_ssm_gate_norm_w, m_ssm_out_w=m_ssm_out_w, m_kv_norm_w=m_kv_norm_w, m_w_k=m_w_k, m_w_v=m_w_v, m_attn_norm_w=m_attn_norm_w, m_w_q=m_w_q, m_w_o=m_w_o, m_ffn_norm_w=m_ffn_norm_w, m_ffn_up_w=m_ffn_up_w, m_ffn_conv_w=m_ffn_conv_w, m_ffn_conv_b=m_ffn_conv_b, m_ffn_down_w=m_ffn_down_w, m_final_norm_w=m_final_norm_w, v_ssm_norm_w=v_ssm_norm_w, v_ssm_in_w=v_ssm_in_w, v_ssm_conv_w=v_ssm_conv_w, v_ssm_conv_b=v_ssm_conv_b, v_ssm_dt_bias=v_ssm_dt_bias, v_ssm_a_log=v_ssm_a_log, v_ssm_d=v_ssm_d, v_ssm_gate_norm_w=v_ssm_gate_norm_w, v_ssm_out_w=v_ssm_out_w, v_kv_norm_w=v_kv_norm_w, v_w_k=v_w_k, v_w_v=v_w_v, v_attn_norm_w=v_attn_norm_w, v_w_q=v_w_q, v_w_o=v_w_o, v_ffn_norm_w=v_ffn_norm_w, v_ffn_up_w=v_ffn_up_w, v_ffn_conv_w=v_ffn_conv_w, v_ffn_conv_b=v_ffn_conv_b, v_ffn_down_w=v_ffn_down_w, v_final_norm_w=v_final_norm_w)
    weights = {n: given[n] for n in TWIN_WEIGHTS}
    shared = {n: given[n] for n in SHARED_INPUTS}
    per_example = {n: given[n] for n in ['x']}
    grad_fn = _jax.value_and_grad(_loss, argnums=(0, 1))

    def one_microbatch(ex, loss_target):
        ex = dict(ex)
        diff = ex.pop(TWIN_DIFF_INPUT)
        return grad_fn(weights, diff, {**shared, **ex}, loss_target)

    if N_MICROBATCH == 1:
        loss, (grad_w, grad_x) = one_microbatch(per_example, given["loss_target"])
    else:
        def body(carry, xs):
            loss_sum, grad_sum = carry
            l_k, (gw_k, gx_k) = one_microbatch(xs[0], xs[1])
            with _jax.named_scope("update"):
                return (loss_sum + l_k, _jax.tree.map(_jnp.add, grad_sum, gw_k)), gx_k

        init = (_jnp.zeros((), _jnp.float32), _jax.tree.map(_jnp.zeros_like, weights))
        (loss, grad_w), grad_x = _jax.lax.scan(body, init, (per_example, given["loss_target"]))
    with _jax.named_scope("update"):
        delta_w, new_m, new_v = {}, {}, {}
        for n in TWIN_WEIGHTS:
            delta_w[n], new_m[n], new_v[n] = _adamw(weights[n], grad_w[n], given["m_" + n], given["v_" + n])
    return (loss, grad_x, *[grad_w[n] for n in TWIN_WEIGHTS], *[delta_w[n] for n in TWIN_WEIGHTS],
            *[new_m[n] for n in TWIN_WEIGHTS], *[new_v[n] for n in TWIN_WEIGHTS])
```

```python
import functools
import math

import jax
import jax.numpy as jnp
from jax import lax
from jax.experimental import pallas as pl
from jax.experimental.pallas import tpu as pltpu

F32 = jnp.float32
BF16 = jnp.bfloat16
EPS = 1e-6

D_MODEL = 1024
D_INNER = 2048
SSM_HEADS = 32
SSM_GROUPS = 4
SSM_STATE = 128
SSM_CONV = 4
SSM_CHUNK = 128
GN = SSM_GROUPS * SSM_STATE
CONV_DIM = D_INNER + 2 * GN
IN_PROJ_DIM = D_INNER + CONV_DIM + SSM_HEADS
IN_PROJ_PAD = 5376
SB_HEADS = 16
SB_HEAD_DIM = 64
SB_BLOCK = 128
D_FF = 2816
FFN_CONV = 3
N_DEV = 8

ADAM_LR = 0.001
ADAM_B1 = 0.9
ADAM_B2 = 0.999
ADAM_EPS = 1e-08
ADAM_WD = 0.01
ADAM_STEP = 10

_MESH = pl.DeviceIdType.MESH
_NT = (((1,), (1,)), ((), ()))
_TN = (((0,), (0,)), ((), ()))
_ANY = pl.BlockSpec(memory_space=pl.ANY)


def _cparams(sem, vmem_mb=48):
    return pltpu.CompilerParams(dimension_semantics=sem, vmem_limit_bytes=vmem_mb * 1024 * 1024)


def _sigmoid(x):
    return 1.0 / (1.0 + jnp.exp(-x))


def _softplus(x):
    return jnp.maximum(x, 0.0) + jnp.log(1.0 + jnp.exp(-jnp.abs(x)))


def _rms_fwd(xv, w):
    r = lax.rsqrt(jnp.mean(xv * xv, axis=-1, keepdims=True) + EPS)
    return xv * r * w


def _mm_fwd(x, w, *, name, norm_w=None, residual=None, out_dtype=F32, tm=512, tn=512):
    M, K = x.shape
    N = w.shape[1]
    tm, tn = min(tm, M), min(tn, N)
    assert M % tm == 0 and N % tn == 0, (name, M, N, tm, tn)
    has_norm, has_res = norm_w is not None, residual is not None

    def body(*refs):
        x_ref, w_ref = refs[0], refs[1]
        p = 2
        nw_ref = r_ref = None
        if has_norm:
            nw_ref = refs[p]
            p += 1
        if has_res:
            r_ref = refs[p]
            p += 1
        o_ref, xn_ref = refs[p], refs[p + 1]

        @pl.when(pl.program_id(1) == 0)
        def _():
            xv = x_ref[...].astype(F32)
            if has_norm:
                xv = _rms_fwd(xv, nw_ref[...])
            xn_ref[...] = xv.astype(BF16)

        acc = jnp.dot(xn_ref[...], w_ref[...], preferred_element_type=F32)
        if has_res:
            acc = acc + r_ref[...]
        o_ref[...] = acc.astype(out_dtype)

    in_specs = [pl.BlockSpec((tm, K), lambda i, j: (i, 0)), pl.BlockSpec((K, tn), lambda i, j: (0, j))]
    args = [x, w]
    if has_norm:
        in_specs.append(pl.BlockSpec((1, K), lambda i, j: (0, 0)))
        args.append(norm_w.reshape(1, K))
    if has_res:
        in_specs.append(pl.BlockSpec((tm, tn), lambda i, j: (i, j)))
        args.append(residual)
    return pl.pallas_call(
        body, name=name, grid=(M // tm, N // tn), in_specs=in_specs,
        out_specs=pl.BlockSpec((tm, tn), lambda i, j: (i, j)),
        out_shape=jax.ShapeDtypeStruct((M, N), out_dtype),
        scratch_shapes=[pltpu.VMEM((tm, K), BF16)],
        compiler_params=_cparams(("parallel", "arbitrary")))(*args)


def _mm_nt(dy, w, *, name, epi=None, out_dtype=F32, tm=512, tn=512, tk=512):
    M, K = dy.shape
    N = w.shape[0]
    tm, tk = min(tm, M), min(tk, K)
    tn = N if epi is not None else min(tn, N)
    assert M % tm == 0 and N % tn == 0 and K % tk == 0, (name, M, N, K, tm, tn, tk)
    nk = K // tk
    has_epi = epi is not None

    def body(*refs):
        if has_epi:
            dy_ref, w_ref, h_ref, nw_ref, r_ref, o_ref, dnw_ref, acc_ref = refs
        else:
            dy_ref, w_ref, o_ref, acc_ref = refs
        i = pl.program_id(0)
        k = pl.program_id(2)

        @pl.when(k == 0)
        def _():
            acc_ref[...] = jnp.zeros_like(acc_ref)

        acc_ref[...] += lax.dot_general(dy_ref[...].astype(BF16), w_ref[...], _NT, preferred_element_type=F32)

        @pl.when(k == nk - 1)
        def _():
            du = acc_ref[...]
            if has_epi:
                hv = h_ref[...]
                r = lax.rsqrt(jnp.mean(hv * hv, axis=-1, keepdims=True) + EPS)
                xhat = hv * r
                dxh = du * nw_ref[...]
                dx = r * (dxh - xhat * jnp.mean(dxh * xhat, axis=-1, keepdims=True))
                o_ref[...] = (r_ref[...] + dx).astype(out_dtype)
                contrib = jnp.sum(du * xhat, axis=0, keepdims=True)

                @pl.when(i == 0)
                def _():
                    dnw_ref[...] = contrib

                @pl.when(i > 0)
                def _():
                    dnw_ref[...] += contrib
            else:
                o_ref[...] = du.astype(out_dtype)

    in_specs = [pl.BlockSpec((tm, tk), lambda i, j, k: (i, k)), pl.BlockSpec((tn, tk), lambda i, j, k: (j, k))]
    args = [dy, w]
    out_specs = [pl.BlockSpec((tm, tn), lambda i, j, k: (i, j))]
    out_shape = [jax.ShapeDtypeStruct((M, N), out_dtype)]
    if has_epi:
        h, nw, res = epi
        in_specs += [pl.BlockSpec((tm, N), lambda i, j, k: (i, 0)), pl.BlockSpec((1, N), lambda i, j, k: (0, 0)),
                     pl.BlockSpec((tm, N), lambda i, j, k: (i, 0))]
        args += [h, nw.reshape(1, N), res]
        out_specs.append(pl.BlockSpec((1, N), lambda i, j, k: (0, 0)))
        out_shape.append(jax.ShapeDtypeStruct((1, N), F32))
    outs = pl.pallas_call(
        body, name=name, grid=(M // tm, N // tn, nk), in_specs=in_specs, out_specs=out_specs, out_shape=out_shape,
        scratch_shapes=[pltpu.VMEM((tm, tn), F32)],
        compiler_params=_cparams(("arbitrary", "arbitrary", "arbitrary")))(*args)
    return (outs[0], outs[1]) if has_epi else outs[0]


def _mm_tn(x, dy, *, name, norm_w=None, out_dtype=BF16, tk1=1024, tn=512, tt=512):
    T, K1 = x.shape
    N = dy.shape[1]
    tk1, tn, tt = min(tk1, K1), min(tn, N), min(tt, T)
    has_norm = norm_w is not None
    assert K1 % tk1 == 0 and N % tn == 0 and T % tt == 0, (name, K1, N, T, tk1, tn, tt)
    assert not has_norm or tk1 == K1
    nt = T // tt

    def body(*refs):
        if has_norm:
            x_ref, dy_ref, nw_ref, o_ref, acc_ref = refs
        else:
            x_ref, dy_ref, o_ref, acc_ref = refs
        t = pl.program_id(2)

        @pl.when(t == 0)
        def _():
            acc_ref[...] = jnp.zeros_like(acc_ref)

        xv = x_ref[...]
        if has_norm:
            xv = _rms_fwd(xv.astype(F32), nw_ref[...])
        acc_ref[...] += lax.dot_general(xv.astype(BF16), dy_ref[...].astype(BF16), _TN, preferred_element_type=F32)

        @pl.when(t == nt - 1)
        def _():
            o_ref[...] = acc_ref[...].astype(out_dtype)

    in_specs = [pl.BlockSpec((tt, tk1), lambda a, b, t: (t, a)), pl.BlockSpec((tt, tn), lambda a, b, t: (t, b))]
    args = [x, dy]
    if has_norm:
        in_specs.append(pl.BlockSpec((1, K1), lambda a, b, t: (0, 0)))
        args.append(norm_w.reshape(1, K1))
    return pl.pallas_call(
        body, name=name, grid=(K1 // tk1, N // tn, nt), in_specs=in_specs,
        out_specs=pl.BlockSpec((tk1, tn), lambda a, b, t: (a, b)),
        out_shape=jax.ShapeDtypeStruct((K1, N), out_dtype),
        scratch_shapes=[pltpu.VMEM((tk1, tn), F32)],
        compiler_params=_cparams(("parallel", "parallel", "arbitrary")))(*args)


def _shift_down(xb, prev8, j):
    main = pltpu.roll(xb, j, 0)
    head = pltpu.roll(xb[0:8], j, 0)
    ph = pltpu.roll(prev8, j, 0)
    row8 = lax.broadcasted_iota(jnp.int32, head.shape, 0)
    head = jnp.where(row8 < j, ph, head)
    return jnp.concatenate([head, main[8:]], axis=0)


def _shift_up(xb, next8, j):
    tt = xb.shape[0]
    main = pltpu.roll(xb, tt - j, 0)
    tail = pltpu.roll(xb[tt - 8:tt], 8 - j, 0)
    nh = pltpu.roll(next8, 8 - j, 0)
    row8 = lax.broadcasted_iota(jnp.int32, tail.shape, 0)
    tail = jnp.where(row8 + j >= 8, nh, tail)
    return jnp.concatenate([main[:tt - 8], tail], axis=0)


def _conv_hid(xb, prev8, w, b_row, K):
    out = b_row
    shifted = []
    for j in range(K):
        sh = K - 1 - j
        xs = xb if sh == 0 else _shift_down(xb, prev8, sh)
        shifted.append(xs)
        out = out + xs * w[j:j + 1, :]
    return out, shifted


def _prev_idx(i, nb8):
    return jnp.maximum(i * nb8 - 1, 0)


def _ssm_conv_fwd(zx, w, b, *, name, tt=512, tc=512):
    T = zx.shape[0]
    tt = min(tt, T)
    C, K = CONV_DIM, SSM_CONV
    cb0, nb8 = D_INNER // tc, tt // 8

    def body(x_ref, p_ref, w_ref, b_ref, o_ref):
        first = (pl.program_id(1) > 0).astype(F32)
        hid, _ = _conv_hid(x_ref[...], p_ref[...] * first, w_ref[...], b_ref[...], K)
        o_ref[...] = hid * _sigmoid(hid)

    return pl.pallas_call(
        body, name=name, grid=(C // tc, T // tt),
        in_specs=[pl.BlockSpec((tt, tc), lambda c, i: (i, c + cb0)),
                  pl.BlockSpec((8, tc), lambda c, i: (_prev_idx(i, nb8), c + cb0)),
                  pl.BlockSpec((K, tc), lambda c, i: (0, c)), pl.BlockSpec((1, tc), lambda c, i: (0, c))],
        out_specs=pl.BlockSpec((tt, tc), lambda c, i: (i, c)),
        out_shape=jax.ShapeDtypeStruct((T, C), F32),
        compiler_params=_cparams(("parallel", "parallel")))(zx, zx, w, b)


def _ssm_conv_bwd_pre(zx, w, b, dout, *, name, tt=512, tc=512):
    T = zx.shape[0]
    tt = min(tt, T)
    C, K = CONV_DIM, SSM_CONV
    cb0, nb8 = D_INNER // tc, tt // 8

    def body(x_ref, p_ref, w_ref, b_ref, d_ref, dh_ref, dw_ref, db_ref):
        t = pl.program_id(1)
        first = (t > 0).astype(F32)
        hid, shifted = _conv_hid(x_ref[...], p_ref[...] * first, w_ref[...], b_ref[...], K)
        sg = _sigmoid(hid)
        dh = d_ref[...] * (sg * (1.0 + hid * (1.0 - sg)))
        dh_ref[...] = dh

        @pl.when(t == 0)
        def _():
            dw_ref[...] = jnp.zeros_like(dw_ref)
            db_ref[...] = jnp.zeros_like(db_ref)

        db_ref[...] += jnp.sum(dh, axis=0, keepdims=True)
        for j in range(K):
            dw_ref[j:j + 1, :] += jnp.sum(dh * shifted[j], axis=0, keepdims=True)

    return pl.pallas_call(
        body, name=name, grid=(C // tc, T // tt),
        in_specs=[pl.BlockSpec((tt, tc), lambda c, i: (i, c + cb0)),
                  pl.BlockSpec((8, tc), lambda c, i: (_prev_idx(i, nb8), c + cb0)),
                  pl.BlockSpec((K, tc), lambda c, i: (0, c)), pl.BlockSpec((1, tc), lambda c, i: (0, c)),
                  pl.BlockSpec((tt, tc), lambda c, i: (i, c))],
        out_specs=[pl.BlockSpec((tt, tc), lambda c, i: (i, c)), pl.BlockSpec((K, tc), lambda c, i: (0, c)),
                   pl.BlockSpec((1, tc), lambda c, i: (0, c))],
        out_shape=[jax.ShapeDtypeStruct((T, C), F32), jax.ShapeDtypeStruct((K, C), F32),
                   jax.ShapeDtypeStruct((1, C), F32)],
        compiler_params=_cparams(("parallel", "arbitrary")))(zx, zx, w, b, dout)


def _conv_bwd_in(dh, w, *, name, K, tt=512, tc=512, out_dtype=BF16):
    T, C = dh.shape
    tt = min(tt, T)
    nb8, nT = tt // 8, T // tt
    last8 = T // 8 - 1

    def body(d_ref, n_ref, w_ref, o_ref):
        notlast = (pl.program_id(1) < nT - 1).astype(F32)
        d = d_ref[...]
        nxt = n_ref[...] * notlast
        w_ = w_ref[...]
        acc = d * w_[K - 1:K, :]
        for sh in range(1, K):
            acc = acc + _shift_up(d, nxt, sh) * w_[K - 1 - sh:K - sh, :]
        o_ref[...] = acc.astype(out_dtype)

    return pl.pallas_call(
        body, name=name, grid=(C // tc, nT),
        in_specs=[pl.BlockSpec((tt, tc), lambda c, i: (i, c)),
                  pl.BlockSpec((8, tc), lambda c, i: (jnp.minimum((i + 1) * nb8, last8), c)),
                  pl.BlockSpec((K, tc), lambda c, i: (0, c))],
        out_specs=pl.BlockSpec((tt, tc), lambda c, i: (i, c)),
        out_shape=jax.ShapeDtypeStruct((T, C), out_dtype),
        compiler_params=_cparams(("parallel", "parallel")))(dh, dh, w)


def _ffn_conv_fwd(a, w, b, *, name, tt=256, tc=1408):
    T = a.shape[0]
    tt = min(tt, T)
    K, nbh, nb8 = FFN_CONV, D_FF // tc, tt // 8

    def body(ag_ref, pg_ref, av_ref, pv_ref, wg_ref, wv_ref, bg_ref, bv_ref, o_ref):
        first = (pl.program_id(1) > 0).astype(F32)
        hg, _ = _conv_hid(ag_ref[...], pg_ref[...] * first, wg_ref[...], bg_ref[...], K)
        hv, _ = _conv_hid(av_ref[...], pv_ref[...] * first, wv_ref[...], bv_ref[...], K)
        o_ref[...] = (hg * _sigmoid(hg) * hv).astype(BF16)

    return pl.pallas_call(
        body, name=name, grid=(nbh, T // tt),
        in_specs=[pl.BlockSpec((tt, tc), lambda c, i: (i, c)),
                  pl.BlockSpec((8, tc), lambda c, i: (_prev_idx(i, nb8), c)),
                  pl.BlockSpec((tt, tc), lambda c, i: (i, c + nbh)),
                  pl.BlockSpec((8, tc), lambda c, i: (_prev_idx(i, nb8), c + nbh)),
                  pl.BlockSpec((K, tc), lambda c, i: (0, c)), pl.BlockSpec((K, tc), lambda c, i: (0, c + nbh)),
                  pl.BlockSpec((1, tc), lambda c, i: (0, c)), pl.BlockSpec((1, tc), lambda c, i: (0, c + nbh))],
        out_specs=pl.BlockSpec((tt, tc), lambda c, i: (i, c)),
        out_shape=jax.ShapeDtypeStruct((T, D_FF), BF16),
        compiler_params=_cparams(("parallel", "parallel")))(a, a, a, a, w, w, b, b)


def _ffn_conv_bwd_pre(a, w, b, dp, *, name, tt=256, tc=1408):
    T = a.shape[0]
    tt = min(tt, T)
    K, nbh, nb8 = FFN_CONV, D_FF // tc, tt // 8

    def body(ao_ref, po_ref, ag_ref, pg_ref, av_ref, pv_ref, wg_ref, wv_ref, bg_ref, bv_ref, dp_ref,
             dh_ref, dw_ref, db_ref):
        j = pl.program_id(0)
        t = pl.program_id(1)
        first = (t > 0).astype(F32)
        hg, _ = _conv_hid(ag_ref[...], pg_ref[...] * first, wg_ref[...], bg_ref[...], K)
        hv, _ = _conv_hid(av_ref[...], pv_ref[...] * first, wv_ref[...], bv_ref[...], K)
        sg = _sigmoid(hg)
        d = dp_ref[...].astype(F32)
        is_gate = (j < nbh).astype(F32)
        dh = d * (is_gate * (hv * (sg * (1.0 + hg * (1.0 - sg)))) + (1.0 - is_gate) * (hg * sg))
        dh_ref[...] = dh
        xo = ao_ref[...]
        po = po_ref[...] * first

        @pl.when(t == 0)
        def _():
            dw_ref[...] = jnp.zeros_like(dw_ref)
            db_ref[...] = jnp.zeros_like(db_ref)

        db_ref[...] += jnp.sum(dh, axis=0, keepdims=True)
        for jj in range(K):
            sh = K - 1 - jj
            xs = xo if sh == 0 else _shift_down(xo, po, sh)
            dw_ref[jj:jj + 1, :] += jnp.sum(dh * xs, axis=0, keepdims=True)

    def gi(c):
        return lax.rem(c, nbh)

    return pl.pallas_call(
        body, name=name, grid=(2 * nbh, T // tt),
        in_specs=[pl.BlockSpec((tt, tc), lambda c, i: (i, c)),
                  pl.BlockSpec((8, tc), lambda c, i: (_prev_idx(i, nb8), c)),
                  pl.BlockSpec((tt, tc), lambda c, i: (i, gi(c))),
                  pl.BlockSpec((8, tc), lambda c, i: (_prev_idx(i, nb8), gi(c))),
                  pl.BlockSpec((tt, tc), lambda c, i: (i, gi(c) + nbh)),
                  pl.BlockSpec((8, tc), lambda c, i: (_prev_idx(i, nb8), gi(c) + nbh)),
                  pl.BlockSpec((K, tc), lambda c, i: (0, gi(c))), pl.BlockSpec((K, tc), lambda c, i: (0, gi(c) + nbh)),
                  pl.BlockSpec((1, tc), lambda c, i: (0, gi(c))), pl.BlockSpec((1, tc), lambda c, i: (0, gi(c) + nbh)),
                  pl.BlockSpec((tt, tc), lambda c, i: (i, gi(c)))],
        out_specs=[pl.BlockSpec((tt, tc), lambda c, i: (i, c)), pl.BlockSpec((K, tc), lambda c, i: (0, c)),
                   pl.BlockSpec((1, tc), lambda c, i: (0, c))],
        out_shape=[jax.ShapeDtypeStruct((T, 2 * D_FF), F32), jax.ShapeDtypeStruct((K, 2 * D_FF), F32),
                   jax.ShapeDtypeStruct((1, 2 * D_FF), F32)],
        compiler_params=_cparams(("parallel", "arbitrary")))(a, a, a, a, a, a, w, w, b, b, dp)


def _cumsum_rows(x):
    L = x.shape[0]
    row = lax.broadcasted_iota(jnp.int32, x.shape, 0)
    k = 1
    while k < L:
        x = x + jnp.where(row >= k, pltpu.roll(x, k, 0), 0.0)
        k *= 2
    return x


def _rcumsum_rows(x):
    L = x.shape[0]
    row = lax.broadcasted_iota(jnp.int32, x.shape, 0)
    k = 1
    while k < L:
        x = x + jnp.where(row < L - k, pltpu.roll(x, L - k, 0), 0.0)
        k *= 2
    return x


def _ssd_common(dt_ref, par_ref):
    par = par_ref[...]
    raw = dt_ref[...] + par[0:1, :]
    dt = _softplus(raw)
    a = -jnp.exp(par[1:2, :])
    cs = _cumsum_rows(dt * a)
    L = cs.shape[0]
    cs_last = cs[L - 1:L, :]
    return raw, dt, a, par[2:3, :], cs, cs.T, jnp.exp(cs), jnp.exp(cs_last - cs), jnp.exp(cs_last)


def _ssd_specs(nc, rev):
    L = SSM_CHUNK

    def ci(c):
        return nc - 1 - c if rev else c

    return [pl.BlockSpec((L, 512), lambda g, c: (ci(c), g)),
            pl.BlockSpec((L, 128), lambda g, c: (ci(c), 16 + g)),
            pl.BlockSpec((L, 128), lambda g, c: (ci(c), 20 + g)),
            pl.BlockSpec((None, L, 128), lambda g, c: (g, ci(c), 0)),
            pl.BlockSpec((None, 8, 128), lambda g, c: (g, 0, 0)),
            pl.BlockSpec((L, 512), lambda g, c: (ci(c), g)),
            pl.BlockSpec((1, 512), lambda g, c: (0, g))], ci


def _ssd_fwd(xbc_c, zx, dtg, par, gnw, *, name):
    T = xbc_c.shape[0]
    L = SSM_CHUNK
    nc = T // L
    in_specs, ci = _ssd_specs(nc, False)

    def body(xs_ref, b_ref, c_ref, dt_ref, par_ref, z_ref, gnw_ref, y_ref, yn_ref, st_ref, h_ref):
        @pl.when(pl.program_id(1) == 0)
        def _():
            h_ref[...] = jnp.zeros_like(h_ref)

        _, dt, _, dsk, cs, csT, ecs, eend, dec = _ssd_common(dt_ref, par_ref)
        Bb = b_ref[...].astype(BF16)
        Cb = c_ref[...].astype(BF16)
        G = lax.dot_general(Cb, Bb, _NT, preferred_element_type=F32)
        row = lax.broadcasted_iota(jnp.int32, (L, L), 0)
        col = lax.broadcasted_iota(jnp.int32, (L, L), 1)
        tril = col <= row
        lo = lax.broadcasted_iota(jnp.int32, (L, 128), 1) < 64
        lo1 = lax.broadcasted_iota(jnp.int32, (1, 128), 1) < 64
        for pp in range(4):
            hA, hB = 2 * pp, 2 * pp + 1

            def sel(m):
                return jnp.where(lo, m[:, hA:hA + 1], m[:, hB:hB + 1])

            def sel1(m):
                return jnp.where(lo1, m[:, hA:hA + 1], m[:, hB:hB + 1])

            X = xs_ref[:, pp * 128:(pp + 1) * 128]
            xd = X * sel(dt)
            xdb = xd.astype(BF16)
            ys = []
            for h in (hA, hB):
                Lm = jnp.where(tril, jnp.exp(jnp.minimum(cs[:, h:h + 1] - csT[h:h + 1, :], 0.0)), 0.0)
                ys.append(jnp.dot((G * Lm).astype(BF16), xdb, preferred_element_type=F32))
            Hp = h_ref[pp]
            st_ref[pp] = Hp
            yoff = jnp.dot(Cb, Hp.astype(BF16), preferred_element_type=F32) * sel(ecs)
            y_ref[:, pp * 128:(pp + 1) * 128] = jnp.where(lo, ys[0], ys[1]) + yoff + sel1(dsk) * X
            S = lax.dot_general(Bb, (xd * sel(eend)).astype(BF16), _TN, preferred_element_type=F32)
            h_ref[pp] = Hp * sel1(dec) + S
        zv = z_ref[...]
        yg = y_ref[...] * (zv * _sigmoid(zv))
        yn_ref[...] = _rms_fwd(yg, gnw_ref[...]).astype(BF16)

    return pl.pallas_call(
        body, name=name, grid=(SSM_GROUPS, nc), in_specs=in_specs,
        out_specs=[pl.BlockSpec((L, 512), lambda g, c: (c, g)), pl.BlockSpec((L, 512), lambda g, c: (c, g)),
                   pl.BlockSpec((None, None, 4, 128, 128), lambda g, c: (g, c, 0, 0, 0))],
        out_shape=[jax.ShapeDtypeStruct((T, D_INNER), F32), jax.ShapeDtypeStruct((T, D_INNER), BF16),
                   jax.ShapeDtypeStruct((SSM_GROUPS, nc, 4, 128, 128), F32)],
        scratch_shapes=[pltpu.VMEM((4, 128, 128), F32)],
        compiler_params=_cparams(("parallel", "arbitrary")))(xbc_c, xbc_c, xbc_c, dtg, par, zx, gnw)


def _ssd_bwd(xbc_c, zx, dtg, par, gnw, y, st, dyn, *, name):
    T = xbc_c.shape[0]
    L = SSM_CHUNK
    nc = T // L
    in_specs, ci = _ssd_specs(nc, True)
    in_specs += [pl.BlockSpec((L, 512), lambda g, c: (ci(c), g)),
                 pl.BlockSpec((None, None, 4, 128, 128), lambda g, c: (g, ci(c), 0, 0, 0)),
                 pl.BlockSpec((L, 512), lambda g, c: (ci(c), g))]

    def body(xs_ref, b_ref, c_ref, dt_ref, par_ref, z_ref, gnw_ref, y_ref, st_ref, dyn_ref,
             dxs_ref, db_ref, dc_ref, dz_ref, ddt_ref, dgnw_ref, dpar_ref, dh_ref):
        @pl.when(pl.program_id(1) == 0)
        def _():
            dh_ref[...] = jnp.zeros_like(dh_ref)
            dgnw_ref[...] = jnp.zeros_like(dgnw_ref)
            dpar_ref[...] = jnp.zeros_like(dpar_ref)

        yv = y_ref[...]
        zv = z_ref[...]
        sg = _sigmoid(zv)
        sz = zv * sg
        yg = yv * sz
        r = lax.rsqrt(jnp.mean(yg * yg, axis=-1, keepdims=True) + EPS)
        yh = yg * r
        dyn = dyn_ref[...].astype(F32)
        dgnw_ref[...] += jnp.sum(dyn * yh, axis=0, keepdims=True)
        dyh = dyn * gnw_ref[...]
        dyg = r * (dyh - yh * jnp.mean(dyh * yh, axis=-1, keepdims=True))
        dY_all = dyg * sz
        dz_ref[...] = (dyg * yv * (sg * (1.0 + zv * (1.0 - sg)))).astype(dz_ref.dtype)

        raw, dt, a, dsk, cs, csT, ecs, eend, dec = _ssd_common(dt_ref, par_ref)
        Bb = b_ref[...].astype(BF16)
        Cb = c_ref[...].astype(BF16)
        G = lax.dot_general(Cb, Bb, _NT, preferred_element_type=F32)
        row = lax.broadcasted_iota(jnp.int32, (L, L), 0)
        col = lax.broadcasted_iota(jnp.int32, (L, L), 1)
        tril = col <= row
        lane = lax.broadcasted_iota(jnp.int32, (L, 128), 1)
        lo = lane < 64
        lane1 = lax.broadcasted_iota(jnp.int32, (1, 128), 1)
        lo1 = lane1 < 64
        rowc = lax.broadcasted_iota(jnp.int32, (L, 1), 0)
        dG = jnp.zeros((L, L), F32)
        dB = jnp.zeros((L, SSM_STATE), F32)
        dC = jnp.zeros((L, SSM_STATE), F32)
        dcs_mat = jnp.zeros((L, 128), F32)
        ddt_mat = jnp.zeros((L, 128), F32)
        dD_row = jnp.zeros((1, 128), F32)

        def tot(m):
            return jnp.sum(jnp.sum(m, axis=1, keepdims=True), axis=0, keepdims=True)

        for pp in range(4):
            hA, hB = 2 * pp, 2 * pp + 1

            def sel(m):
                return jnp.where(lo, m[:, hA:hA + 1], m[:, hB:hB + 1])

            def sel1(m):
                return jnp.where(lo1, m[:, hA:hA + 1], m[:, hB:hB + 1])

            X = xs_ref[:, pp * 128:(pp + 1) * 128]
            dY = dY_all[:, pp * 128:(pp + 1) * 128]
            dtsel = sel(dt)
            xd = X * dtsel
            xdb = xd.astype(BF16)
            dYb = dY.astype(BF16)
            Hp = st_ref[pp]
            Hb = Hp.astype(BF16)
            dHn = dh_ref[pp]
            dHb = dHn.astype(BF16)
            ecs_sel = sel(ecs)
            eend_sel = sel(eend)
            dxd_state = jnp.dot(Bb, dHb, preferred_element_type=F32) * eend_sel
            yoff = jnp.dot(Cb, Hb, preferred_element_type=F32) * ecs_sel
            dYe = (dY * ecs_sel).astype(BF16)
            dC = dC + lax.dot_general(dYe, Hb, _NT, preferred_element_type=F32)
            dB = dB + lax.dot_general((xd * eend_sel).astype(BF16), dHb, _NT, preferred_element_type=F32)
            dh_ref[pp] = dHn * sel1(dec) + lax.dot_general(Cb, dYe, _TN, preferred_element_type=F32)
            q = xd * dxd_state
            dyoff = dY * yoff
            hh = dHn * Hp
            dxd_diag = []
            for h, msk, msk1 in ((hA, lo, lo1), (hB, jnp.logical_not(lo), jnp.logical_not(lo1))):
                Lm = jnp.where(tril, jnp.exp(jnp.minimum(cs[:, h:h + 1] - csT[h:h + 1, :], 0.0)), 0.0)
                M = G * Lm
                dxd_diag.append(lax.dot_general(M.astype(BF16), dYb, _TN, preferred_element_type=F32))
                dM = lax.dot_general(jnp.where(msk, dY, 0.0).astype(BF16), xdb, _NT, preferred_element_type=F32)
                dG = dG + dM * Lm
                W = dM * M
                dcs_h = jnp.sum(W - W.T, axis=1, keepdims=True)
                dcs_h = dcs_h + jnp.sum(jnp.where(msk, dyoff - q, 0.0), axis=1, keepdims=True)
                tail = tot(jnp.where(msk, q, 0.0)) + dec[:, h:h + 1] * tot(jnp.where(msk, hh, 0.0))
                dcs_h = dcs_h + jnp.where(rowc == L - 1, tail, 0.0)
                dcs_mat = dcs_mat + jnp.where(lane == h, dcs_h, 0.0)
            dxd = jnp.where(lo, dxd_diag[0], dxd_diag[1]) + dxd_state
            prod = dxd * X
            dA_ = jnp.sum(jnp.where(lo, prod, 0.0), axis=1, keepdims=True)
            dB_ = jnp.sum(prod, axis=1, keepdims=True) - dA_
            ddt_mat = ddt_mat + jnp.where(lane == hA, dA_, 0.0) + jnp.where(lane == hB, dB_, 0.0)
            dxs_ref[:, pp * 128:(pp + 1) * 128] = dxd * dtsel + sel1(dsk) * dY
            dyx = jnp.sum(dY * X, axis=0, keepdims=True)
            sA = jnp.sum(jnp.where(lo1, dyx, 0.0), axis=1, keepdims=True)
            sB = jnp.sum(dyx, axis=1, keepdims=True) - sA
            dD_row = dD_row + jnp.where(lane1 == hA, sA, 0.0) + jnp.where(lane1 == hB, sB, 0.0)
        dGb = dG.astype(BF16)
        db_ref[...] = dB + lax.dot_general(dGb, Cb, _TN, preferred_element_type=F32)
        dc_ref[...] = dC + jnp.dot(dGb, Bb, preferred_element_type=F32)
        dad = _rcumsum_rows(dcs_mat)
        draw = (a * dad + ddt_mat) * _sigmoid(raw)
        ddt_ref[...] = draw
        dpar_ref[0:1, :] += jnp.sum(draw, axis=0, keepdims=True)
        dpar_ref[1:2, :] += jnp.sum(dt * dad, axis=0, keepdims=True) * a
        dpar_ref[2:3, :] += dD_row

    return pl.pallas_call(
        body, name=name, grid=(SSM_GROUPS, nc), in_specs=in_specs,
        out_specs=[pl.BlockSpec((L, 512), lambda g, c: (ci(c), g)),
                   pl.BlockSpec((L, 128), lambda g, c: (ci(c), g)),
                   pl.BlockSpec((L, 128), lambda g, c: (ci(c), g)),
                   pl.BlockSpec((L, 512), lambda g, c: (ci(c), g)),
                   pl.BlockSpec((None, L, 128), lambda g, c: (g, ci(c), 0)),
                   pl.BlockSpec((1, 512), lambda g, c: (0, g)),
                   pl.BlockSpec((None, 8, 128), lambda g, c: (g, 0, 0))],
        out_shape=[jax.ShapeDtypeStruct((T, D_INNER), F32), jax.ShapeDtypeStruct((T, GN), F32),
                   jax.ShapeDtypeStruct((T, GN), F32), jax.ShapeDtypeStruct((T, D_INNER), BF16),
                   jax.ShapeDtypeStruct((SSM_GROUPS, T, 128), F32), jax.ShapeDtypeStruct((1, D_INNER), F32),
                   jax.ShapeDtypeStruct((SSM_GROUPS, 8, 128), F32)],
        scratch_shapes=[pltpu.VMEM((4, 128, 128), F32)],
        compiler_params=_cparams(("parallel", "arbitrary")))(xbc_c, xbc_c, xbc_c, dtg, par, zx, gnw, y, st, dyn)


def _split_hi_lo(v):
    hi = v.astype(BF16)
    return hi, (v - hi.astype(F32)).astype(BF16)


def _stack_heads(v):
    lo = lax.broadcasted_iota(jnp.int32, v.shape, 1) < 64
    zero = jnp.zeros_like(v)
    return jnp.concatenate([jnp.where(lo, v, zero), jnp.where(lo, zero, v)], axis=0)


def _unstack_heads(v):
    lo = lax.broadcasted_iota(jnp.int32, (SB_BLOCK, 128), 1) < 64
    return jnp.where(lo, v[:SB_BLOCK], v[SB_BLOCK:])


def _sba_tile(qs, kb, i, J):
    Bq = SB_BLOCK
    z = lax.dot_general(qs, kb, _NT, preferred_element_type=F32)
    rowi = lax.broadcasted_iota(jnp.int32, (2 * Bq, Bq), 0)
    qpos = i * Bq + jnp.where(rowi >= Bq, rowi - Bq, rowi)
    kpos = J * Bq + lax.broadcasted_iota(jnp.int32, (2 * Bq, Bq), 1)
    mask = kpos < qpos
    return z, mask, _softplus(z)


def _sba_fwd(q, kv, *, name):
    T = q.shape[0]
    Bq = SB_BLOCK
    nq = T // Bq
    scale = 1.0 / math.sqrt(SB_HEAD_DIM)

    def body(q_ref, k_ref, v_ref, o_ref, lt_ref):
        i = pl.program_id(1)
        qs = _stack_heads(q_ref[...] * scale)
        kk = lax.broadcasted_iota(jnp.int32, (Bq, Bq), 0)
        jj = lax.broadcasted_iota(jnp.int32, (Bq, Bq), 1)
        U = (kk > jj).astype(BF16)

        def step(n, carry):
            c, acc = carry
            J = i - n
            off = pl.multiple_of(J * Bq, Bq)
            kb = k_ref[pl.ds(off, Bq), :]
            vb = v_ref[pl.ds(off, Bq), :]
            z, mask, s = _sba_tile(qs, kb, i, J)
            l = jnp.where(mask, -s, 0.0)
            lhi, llo = _split_hi_lo(l)
            R = c + jnp.dot(lhi, U, preferred_element_type=F32) + jnp.dot(llo, U, preferred_element_type=F32)
            A = jnp.where(mask, jnp.exp(z - s + R), 0.0)
            acc = acc + jnp.dot(A.astype(BF16), vb, preferred_element_type=F32)
            return c + jnp.sum(l, axis=1, keepdims=True), acc

        c, acc = lax.fori_loop(0, i + 1, step, (jnp.zeros((2 * Bq, 1), F32), jnp.zeros((2 * Bq, 128), F32)))
        o_ref[...] = _unstack_heads(acc).astype(BF16)
        lt_ref[...] = _unstack_heads(jnp.broadcast_to(c, (2 * Bq, 128)))

    return pl.pallas_call(
        body, name=name, grid=(SB_HEADS // 2, nq),
        in_specs=[pl.BlockSpec((Bq, 128), lambda p, i: (i, p)), pl.BlockSpec((T, 128), lambda p, i: (0, p)),
                  pl.BlockSpec((T, 128), lambda p, i: (0, p + SB_HEADS // 2))],
        out_specs=[pl.BlockSpec((Bq, 128), lambda p, i: (i, p)), pl.BlockSpec((None, Bq, 128), lambda p, i: (p, i, 0))],
        out_shape=[jax.ShapeDtypeStruct((T, D_MODEL), BF16), jax.ShapeDtypeStruct((SB_HEADS // 2, T, 128), F32)],
        compiler_params=_cparams(("parallel", "parallel")))(q, kv, kv)


def _sba_bwd(q, kv, lt, do, *, name):
    T = q.shape[0]
    Bq = SB_BLOCK
    nq = T // Bq
    scale = 1.0 / math.sqrt(SB_HEAD_DIM)

    def body(q_ref, k_ref, v_ref, lt_ref, do_ref, dq_ref, dk_ref, dv_ref, dk_acc, dv_acc):
        i = pl.program_id(1)

        @pl.when(i == 0)
        def _():
            dk_acc[...] = jnp.zeros_like(dk_acc)
            dv_acc[...] = jnp.zeros_like(dv_acc)

        qs = _stack_heads(q_ref[...] * scale)
        dos = _stack_heads(do_ref[...])
        ltv = lt_ref[...]
        Lt = jnp.concatenate([ltv[:, 0:1], ltv[:, 64:65]], axis=0)
        kk = lax.broadcasted_iota(jnp.int32, (Bq, Bq), 0)
        jj = lax.broadcasted_iota(jnp.int32, (Bq, Bq), 1)
        Uincl = (kk <= jj).astype(BF16)
        Uexcl = (kk < jj).astype(BF16)

        def step(J, carry):
            pc, pe, dq_acc = carry
            off = pl.multiple_of(J * Bq, Bq)
            kb = k_ref[pl.ds(off, Bq), :]
            vb = v_ref[pl.ds(off, Bq), :]
            z, mask, s = _sba_tile(qs, kb, i, J)
            l = jnp.where(mask, -s, 0.0)
            lhi, llo = _split_hi_lo(l)
            P = pc + jnp.dot(lhi, Uincl, preferred_element_type=F32) + jnp.dot(llo, Uincl, preferred_element_type=F32)
            g = z - s
            A = jnp.where(mask, jnp.exp(g + (Lt - P)), 0.0)
            dA = lax.dot_general(dos, vb, _NT, preferred_element_type=F32)
            E = dA * A
            ehi, elo = _split_hi_lo(E)
            PE = pe + jnp.dot(ehi, Uexcl, preferred_element_type=F32) + jnp.dot(elo, Uexcl, preferred_element_type=F32)
            sig = jnp.exp(g)
            dz = jnp.where(mask, E * (1.0 - sig) - PE * sig, 0.0).astype(BF16)
            dq_acc = dq_acc + jnp.dot(dz, kb, preferred_element_type=F32)
            dk_acc[pl.ds(off, Bq), :] += lax.dot_general(dz, qs, _TN, preferred_element_type=F32)
            dv_acc[pl.ds(off, Bq), :] += lax.dot_general(A.astype(BF16), dos, _TN, preferred_element_type=F32)
            return pc + jnp.sum(l, axis=1, keepdims=True), pe + jnp.sum(E, axis=1, keepdims=True), dq_acc

        zc = jnp.zeros((2 * Bq, 1), F32)
        _, _, dq_acc = lax.fori_loop(0, i + 1, step, (zc, zc, jnp.zeros((2 * Bq, 128), F32)))
        dq_ref[...] = (_unstack_heads(dq_acc) * scale).astype(BF16)

        @pl.when(i == nq - 1)
        def _():
            dk_ref[...] = dk_acc[...].astype(BF16)
            dv_ref[...] = dv_acc[...].astype(BF16)

    return pl.pallas_call(
        body, name=name, grid=(SB_HEADS // 2, nq),
        in_specs=[pl.BlockSpec((Bq, 128), lambda p, i: (i, p)), pl.BlockSpec((T, 128), lambda p, i: (0, p)),
                  pl.BlockSpec((T, 128), lambda p, i: (0, p + SB_HEADS // 2)),
                  pl.BlockSpec((None, Bq, 128), lambda p, i: (p, i, 0)), pl.BlockSpec((Bq, 128), lambda p, i: (i, p))],
        out_specs=[pl.BlockSpec((Bq, 128), lambda p, i: (i, p)), pl.BlockSpec((T, 128), lambda p, i: (0, p)),
                   pl.BlockSpec((T, 128), lambda p, i: (0, p))],
        out_shape=[jax.ShapeDtypeStruct((T, D_MODEL), BF16), jax.ShapeDtypeStruct((T, D_MODEL), BF16),
                   jax.ShapeDtypeStruct((T, D_MODEL), BF16)],
        scratch_shapes=[pltpu.VMEM((T, 128), F32), pltpu.VMEM((T, 128), F32)],
        compiler_params=_cparams(("parallel", "arbitrary")))(q, kv, kv, lt, do)


def _loss_head(h, tgt, w, *, name, tt=512):
    T, D = h.shape
    tt = min(tt, T)

    def body(h_ref, t_ref, w_ref, loss_ref, dh_ref, dw_ref):
        i = pl.program_id(0)
        hv = h_ref[...]
        wv = w_ref[...]
        r = lax.rsqrt(jnp.mean(hv * hv, axis=-1, keepdims=True) + EPS)
        xhat = hv * r
        err = xhat * wv - t_ref[...]
        part = 0.5 * jnp.sum(jnp.mean(err * err, axis=-1, keepdims=True), axis=0, keepdims=True)
        dy = err * (1.0 / D)
        dxh = dy * wv
        dh_ref[...] = r * (dxh - xhat * jnp.mean(dxh * xhat, axis=-1, keepdims=True))
        dwc = jnp.sum(dy * xhat, axis=0, keepdims=True)

        @pl.when(i == 0)
        def _():
            loss_ref[...] = jnp.broadcast_to(part, loss_ref.shape)
            dw_ref[...] = dwc

        @pl.when(i > 0)
        def _():
            loss_ref[...] += jnp.broadcast_to(part, loss_ref.shape)
            dw_ref[...] += dwc

    return pl.pallas_call(
        body, name=name, grid=(T // tt,),
        in_specs=[pl.BlockSpec((tt, D), lambda i: (i, 0)), pl.BlockSpec((tt, D), lambda i: (i, 0)),
                  pl.BlockSpec((1, D), lambda i: (0, 0))],
        out_specs=[pl.BlockSpec((1, 128), lambda i: (0, 0)), pl.BlockSpec((tt, D), lambda i: (i, 0)),
                   pl.BlockSpec((1, D), lambda i: (0, 0))],
        out_shape=[jax.ShapeDtypeStruct((1, 128), F32), jax.ShapeDtypeStruct((T, D), F32),
                   jax.ShapeDtypeStruct((1, D), F32)],
        compiler_params=_cparams(("arbitrary",)))(h, tgt, w.reshape(1, D))


def _adamw(parts, w, m, v, *, name, tr=256):
    P, R, C = parts.shape
    tr = min(tr, R)
    assert R % tr == 0, (name, R, tr)
    c1 = 1.0 - ADAM_B1 ** ADAM_STEP
    c2 = 1.0 - ADAM_B2 ** ADAM_STEP

    def body(p_ref, w_ref, m_ref, v_ref, g_ref, d_ref, nm_ref, nv_ref):
        g = p_ref[0].astype(F32)
        for k in range(1, P):
            g = g + p_ref[k].astype(F32)
        mn = ADAM_B1 * m_ref[...] + (1.0 - ADAM_B1) * g
        vn = ADAM_B2 * v_ref[...] + (1.0 - ADAM_B2) * (g * g)
        g_ref[...] = g
        nm_ref[...] = mn
        nv_ref[...] = vn
        d_ref[...] = -ADAM_LR * ((mn / c1) / (jnp.sqrt(vn / c2) + ADAM_EPS) + ADAM_WD * w_ref[...])

    spec = pl.BlockSpec((tr, C), lambda i: (i, 0))
    sds = jax.ShapeDtypeStruct((R, C), F32)
    return pl.pallas_call(
        body, name=name, grid=(R // tr,),
        in_specs=[pl.BlockSpec((P, tr, C), lambda i: (0, i, 0)), spec, spec, spec],
        out_specs=[spec, spec, spec, spec], out_shape=[sds, sds, sds, sds],
        compiler_params=_cparams(("parallel",)))(parts, w, m, v)


def _all_gather(shards, *, name):
    n = len(shards)

    def body(*refs):
        ins, outs = refs[:n], refs[n:2 * n]
        send_sems, recv_sems, local_sems = refs[2 * n:]
        x, y, c = lax.axis_index("x"), lax.axis_index("y"), lax.axis_index("c")
        me, sib = (x, y, c), (x, y, 1 - c)
        chips = [(1 - x, y), (x, 1 - y), (1 - x, 1 - y)]

        def slot(p):
            return 4 * p[0] + 2 * p[1] + p[2]

        def cp(a, k, block, to, src=None):
            dst = outs[a].at[slot(block)]
            return pltpu.make_async_remote_copy(src_ref=dst if src is None else src, dst_ref=dst,
                                                send_sem=send_sems.at[a, k], recv_sem=recv_sems.at[a, k],
                                                device_id=to, device_id_type=_MESH)

        mine = [pltpu.make_async_copy(ins[a], outs[a].at[slot(me)], local_sems.at[a]) for a in range(n)]
        for m in mine:
            m.start()
        first = []
        for a in range(n):
            first.append(cp(a, 0, me, sib, src=ins[a]))
            for j, chip in enumerate(chips):
                first.append(cp(a, 1 + j, me, (*chip, c), src=ins[a]))
        for f in first:
            f.start()
        passed = []
        for j, chip in enumerate(chips):
            for a in range(n):
                cp(a, 1 + j, (*chip, c), me).wait_recv()
                f = cp(a, 4 + j, (*chip, c), sib)
                f.start()
                passed.append(f)
        for a in range(n):
            cp(a, 0, sib, me).wait_recv()
            for j, chip in enumerate(chips):
                cp(a, 4 + j, (*chip, 1 - c), me).wait_recv()
        for f in first + passed:
            f.wait_send()
        for m in mine:
            m.wait()

    return pl.pallas_call(
        body, name=name, in_specs=[_ANY] * n, out_specs=[_ANY] * n,
        out_shape=[jax.ShapeDtypeStruct((N_DEV,) + s.shape, s.dtype) for s in shards],
        scratch_shapes=[pltpu.SemaphoreType.DMA((n, 7)), pltpu.SemaphoreType.DMA((n, 7)),
                        pltpu.SemaphoreType.DMA((n,))])(*shards)


def _exchange(blocks, *, name):
    n = len(blocks)

    def body(*refs):
        ins, outs = refs[:n], refs[n:2 * n]
        send_sems, recv_sems, local_sems = refs[2 * n:]
        x, y, c = lax.axis_index("x"), lax.axis_index("y"), lax.axis_index("c")
        me = 4 * x + 2 * y + c
        mine = [pltpu.make_async_copy(ins[a].at[me], outs[a].at[me], local_sems.at[a]) for a in range(n)]
        for m in mine:
            m.start()
        copies = []
        for r in range(1, N_DEV):
            rx, ry, rc = (r >> 2) & 1, (r >> 1) & 1, r & 1
            px, py, pc = (1 - x if rx else x), (1 - y if ry else y), (1 - c if rc else c)
            peer = 4 * px + 2 * py + pc
            for a in range(n):
                copies.append((pltpu.make_async_remote_copy(
                    src_ref=ins[a].at[peer], dst_ref=outs[a].at[me], send_sem=send_sems.at[a, r - 1],
                    recv_sem=recv_sems.at[a, r - 1], device_id=(px, py, pc), device_id_type=_MESH),
                    pltpu.make_async_remote_copy(
                    src_ref=ins[a].at[peer], dst_ref=outs[a].at[peer], send_sem=send_sems.at[a, r - 1],
                    recv_sem=recv_sems.at[a, r - 1], device_id=(px, py, pc), device_id_type=_MESH)))
        for snd, _ in copies:
            snd.start()
        for _, rcv in copies:
            rcv.wait_recv()
        for snd, _ in copies:
            snd.wait_send()
        for m in mine:
            m.wait()

    return pl.pallas_call(
        body, name=name, in_specs=[_ANY] * n, out_specs=[_ANY] * n,
        out_shape=[jax.ShapeDtypeStruct(b.shape, b.dtype) for b in blocks],
        scratch_shapes=[pltpu.SemaphoreType.DMA((n, 7)), pltpu.SemaphoreType.DMA((n, 7)),
                        pltpu.SemaphoreType.DMA((n,))])(*blocks)


def _ffn_fwd(h, nw, w_up, conv_w, conv_b, w_down, tag):
    a = _mm_fwd(h, w_up, norm_w=nw, name=f"ffn{tag}_up", tn=512)
    p = _ffn_conv_fwd(a, conv_w, conv_b.reshape(1, -1), name=f"ffn{tag}_conv")
    h_out = _mm_fwd(p, w_down, residual=h, name=f"ffn{tag}_down", tn=512)
    return h_out, (a, p)


def _ffn_bwd(dh, h, saved, nw, w_up, conv_w, conv_b, w_down, tag):
    a, p = saved
    g_down = _mm_tn(p, dh, name=f"ffn{tag}_down_wg", tk1=1408, tn=1024)
    dp = _mm_nt(dh, w_down, name=f"ffn{tag}_down_dg", out_dtype=BF16, tn=1408, tk=1024)
    dhid, g_cw, g_cb = _ffn_conv_bwd_pre(a, conv_w, conv_b.reshape(1, -1), dp, name=f"ffn{tag}_conv_bwd")
    da = _conv_bwd_in(dhid, conv_w, K=FFN_CONV, name=f"ffn{tag}_conv_bwd_in", tt=256, tc=1408)
    g_up = _mm_tn(h, da, norm_w=nw, name=f"ffn{tag}_up_wg", tn=1408)
    dh_out, g_nw = _mm_nt(da, w_up, epi=(h, nw, dh), name=f"ffn{tag}_up_dg", tk=512)
    return dh_out, dict(norm=g_nw.reshape(-1), up=g_up, conv_w=g_cw, conv_b=g_cb.reshape(-1), down=g_down)


def _local_step(x, tgt, W):
    T = x.shape[0]
    f = {}
    zx = _mm_fwd(x, W["in_w"], norm_w=W["ssm_norm_w"], name="ssm_in", tn=896)
    xbc_c = _ssm_conv_fwd(zx, W["ssm_conv_w"], W["ssm_conv_b"].reshape(1, -1), name="ssm_conv")
    dt_raw = zx[:, D_INNER + CONV_DIM:IN_PROJ_DIM]
    dtg = jnp.pad(dt_raw.reshape(T, SSM_GROUPS, 8).transpose(1, 0, 2), ((0, 0), (0, 0), (0, 120)))
    par = jnp.stack([W["ssm_dt_bias"].reshape(SSM_GROUPS, 8), W["ssm_a_log"].reshape(SSM_GROUPS, 8),
                     W["ssm_d"].reshape(SSM_GROUPS, 8)], axis=1)
    par = jnp.pad(par, ((0, 0), (0, 5), (0, 120)))
    gnw = W["ssm_gate_norm_w"].reshape(1, D_INNER)
    y, yn, st = _ssd_fwd(xbc_c, zx, dtg, par, gnw, name="ssd_fwd")
    h1 = _mm_fwd(yn, W["ssm_out_w"], residual=x, name="ssm_out", tn=512)
    h2, ffn0 = _ffn_fwd(h1, W["ffn_norm_w"][0], W["ffn_up_w"][0], W["ffn_conv_w"][0], W["ffn_conv_b"][0],
                        W["ffn_down_w"][0], "0")
    q = _mm_fwd(h2, W["w_q"], norm_w=W["attn_norm_w"], out_dtype=BF16, name="attn_q", tn=512)
    kv = _mm_fwd(h2, W["w_kv"], norm_w=W["kv_norm_w"], out_dtype=BF16, name="attn_kv", tn=512)
    o, lt = _sba_fwd(q, kv, name="sba_fwd")
    h3 = _mm_fwd(o, W["w_o"], residual=h2, name="attn_o", tn=512)
    h4, ffn1 = _ffn_fwd(h3, W["ffn_norm_w"][1], W["ffn_up_w"][1], W["ffn_conv_w"][1], W["ffn_conv_b"][1],
                        W["ffn_down_w"][1], "1")
    loss, dh4, g_final = _loss_head(h4, tgt, W["final_norm_w"], name="loss_head")
    dh3, gf1 = _ffn_bwd(dh4, h3, ffn1, W["ffn_norm_w"][1], W["ffn_up_w"][1], W["ffn_conv_w"][1], W["ffn_conv_b"][1],
                        W["ffn_down_w"][1], "1")
    g_wo = _mm_tn(o, dh3, name="attn_o_wg", tn=1024)
    do = _mm_nt(dh3, W["w_o"], name="attn_o_dg", out_dtype=BF16, tn=1024, tk=1024)
    dq, dk, dv = _sba_bwd(q, kv, lt, do, name="sba_bwd")
    g_wq = _mm_tn(h2, dq, norm_w=W["attn_norm_w"], name="attn_q_wg", tn=1024)
    dh2a, g_attn_nw = _mm_nt(dq, W["w_q"], epi=(h2, W["attn_norm_w"], dh3), name="attn_q_dg", tk=1024)
    dkv = jnp.concatenate([dk, dv], axis=1)
    g_wkv = _mm_tn(h2, dkv, norm_w=W["kv_norm_w"], name="attn_kv_wg", tn=1024)
    dh2, g_kv_nw = _mm_nt(dkv, W["w_kv"], epi=(h2, W["kv_norm_w"], dh2a), name="attn_kv_dg", tk=1024)
    dh1, gf0 = _ffn_bwd(dh2, h1, ffn0, W["ffn_norm_w"][0], W["ffn_up_w"][0], W["ffn_conv_w"][0], W["ffn_conv_b"][0],
                        W["ffn_down_w"][0], "0")
    g_out = _mm_tn(yn, dh1, name="ssm_out_wg", tn=1024)
    dyn = _mm_nt(dh1, W["ssm_out_w"], name="ssm_out_dg", out_dtype=BF16, tn=1024, tk=1024)
    dxs, dB, dC, dz, ddt, g_gnw, dpar = _ssd_bwd(xbc_c, zx, dtg, par, gnw, y, st, dyn, name="ssd_bwd")
    dxbc_c = jnp.concatenate([dxs, dB, dC], axis=1)
    dhid, g_scw, g_scb = _ssm_conv_bwd_pre(zx, W["ssm_conv_w"], W["ssm_conv_b"].reshape(1, -1), dxbc_c,
                                           name="ssm_conv_bwd")
    dxbc = _conv_bwd_in(dhid, W["ssm_conv_w"], K=SSM_CONV, name="ssm_conv_bwd_in")
    ddt_t = ddt[:, :, :8].transpose(1, 0, 2).reshape(T, SSM_HEADS).astype(BF16)
    dzx = jnp.concatenate([dz, dxbc, jnp.pad(ddt_t, ((0, 0), (0, IN_PROJ_PAD - IN_PROJ_DIM)))], axis=1)
    g_in = _mm_tn(x, dzx, norm_w=W["ssm_norm_w"], name="ssm_in_wg", tn=896)
    dx, g_ssm_nw = _mm_nt(dzx, W["in_w"], epi=(x, W["ssm_norm_w"], dh1), name="ssm_in_dg", tk=896)
    f["ssm_norm_w"] = g_ssm_nw.reshape(-1)
    f["ssm_in_w"] = g_in[:, :IN_PROJ_DIM]
    f["ssm_conv_w"] = g_scw
    f["ssm_conv_b"] = g_scb.reshape(-1)
    f["ssm_dt_bias"] = dpar[:, 0, :8].reshape(-1)
    f["ssm_a_log"] = dpar[:, 1, :8].reshape(-1)
    f["ssm_d"] = dpar[:, 2, :8].reshape(-1)
    f["ssm_gate_norm_w"] = g_gnw.reshape(-1)
    f["ssm_out_w"] = g_out
    f["kv_norm_w"] = g_kv_nw.reshape(-1)
    f["w_k"] = g_wkv[:, :D_MODEL]
    f["w_v"] = g_wkv[:, D_MODEL:]
    f["attn_norm_w"] = g_attn_nw.reshape(-1)
    f["w_q"] = g_wq
    f["w_o"] = g_wo
    f["ffn_norm_w"] = jnp.stack([gf0["norm"], gf1["norm"]])
    f["ffn_up_w"] = [gf0["up"], gf1["up"]]
    f["ffn_conv_w"] = jnp.stack([gf0["conv_w"], gf1["conv_w"]])
    f["ffn_conv_b"] = jnp.stack([gf0["conv_b"], gf1["conv_b"]])
    f["ffn_down_w"] = [gf0["down"], gf1["down"]]
    f["final_norm_w"] = g_final.reshape(-1)
    return loss, dx, f


_BIG = ["ssm_in_w", "ssm_out_w", "w_k", "w_v", "w_q", "w_o", "ffn_up_w", "ffn_down_w"]
_SMALL_SHARDED = ["ssm_norm_w", "ssm_conv_w", "ssm_conv_b", "ssm_gate_norm_w", "ffn_conv_w"]
_SMALL_REPL = ["ssm_dt_bias", "ssm_a_log", "ssm_d", "kv_norm_w", "attn_norm_w", "ffn_norm_w", "ffn_conv_b",
               "final_norm_w"]
_WEIGHTS = ["ssm_norm_w", "ssm_in_w", "ssm_conv_w", "ssm_conv_b", "ssm_dt_bias", "ssm_a_log", "ssm_d",
            "ssm_gate_norm_w", "ssm_out_w", "kv_norm_w", "w_k", "w_v", "attn_norm_w", "w_q", "w_o", "ffn_norm_w",
            "ffn_up_w", "ffn_conv_w", "ffn_conv_b", "ffn_down_w", "final_norm_w"]


def _as2d(a):
    return a.reshape(-1, a.shape[-1])


def _cols_to_full(g):
    return g.transpose(1, 0, 2).reshape(g.shape[1], N_DEV * g.shape[2])


def _full_to_cols(a):
    R = a.shape[0]
    return a.reshape(R, N_DEV, -1).transpose(1, 0, 2)


def _gather_weights(p):
    names = _BIG + _SMALL_SHARDED
    shards = [_as2d(p[n]).astype(BF16) for n in _BIG] + [_as2d(p[n]) for n in _SMALL_SHARDED]
    got = dict(zip(names, _all_gather(shards, name="gather_weights")))
    W = {n: p[n] for n in _SMALL_REPL}
    in_w = _cols_to_full(got["ssm_in_w"])
    W["in_w"] = jnp.pad(in_w, ((0, 0), (0, IN_PROJ_PAD - IN_PROJ_DIM)))
    W["ssm_out_w"] = got["ssm_out_w"].reshape(D_INNER, D_MODEL)
    W["w_kv"] = jnp.concatenate([got["w_k"].reshape(D_MODEL, D_MODEL), got["w_v"].reshape(D_MODEL, D_MODEL)], axis=1)
    W["w_q"] = got["w_q"].reshape(D_MODEL, D_MODEL)
    W["w_o"] = got["w_o"].reshape(D_MODEL, D_MODEL)
    up = got["ffn_up_w"]
    W["ffn_up_w"] = [_cols_to_full(up[:, l * D_MODEL:(l + 1) * D_MODEL]) for l in range(2)]
    dn = got["ffn_down_w"]
    rs = D_FF // N_DEV
    W["ffn_down_w"] = [dn[:, l * rs:(l + 1) * rs].reshape(D_FF, D_MODEL) for l in range(2)]
    W["ssm_norm_w"] = got["ssm_norm_w"].reshape(D_MODEL)
    W["ssm_conv_w"] = _cols_to_full(got["ssm_conv_w"])
    W["ssm_conv_b"] = got["ssm_conv_b"].reshape(CONV_DIM)
    W["ssm_gate_norm_w"] = got["ssm_gate_norm_w"].reshape(D_INNER)
    fcw = _cols_to_full(got["ffn_conv_w"])
    W["ffn_conv_w"] = fcw.reshape(2, FFN_CONV, 2 * D_FF)
    for n in ("ssm_dt_bias", "ssm_a_log", "ssm_d", "attn_norm_w"):
        W[n] = W[n].reshape(-1)
    return W


def _big_grad_blocks(f):
    rs = D_FF // N_DEV
    return {
        "ssm_in_w": _full_to_cols(f["ssm_in_w"]),
        "ssm_out_w": f["ssm_out_w"].reshape(N_DEV, D_INNER // N_DEV, D_MODEL),
        "w_k": f["w_k"].reshape(N_DEV, D_MODEL // N_DEV, D_MODEL),
        "w_v": f["w_v"].reshape(N_DEV, D_MODEL // N_DEV, D_MODEL),
        "w_q": f["w_q"].reshape(N_DEV, D_MODEL // N_DEV, D_MODEL),
        "w_o": f["w_o"].reshape(N_DEV, D_MODEL // N_DEV, D_MODEL),
        "ffn_up_w": jnp.concatenate([_full_to_cols(g) for g in f["ffn_up_w"]], axis=1),
        "ffn_down_w": jnp.concatenate([g.reshape(N_DEV, rs, D_MODEL) for g in f["ffn_down_w"]], axis=1),
    }


def _pack_small(vals):
    flat = jnp.concatenate([v.reshape(-1).astype(F32) for v in vals])
    n = flat.shape[0]
    rows = -(-n // 1024) * 8
    return jnp.pad(flat, (0, rows * 128 - n)).reshape(rows, 128)


def _unpack_small(packed, shapes):
    flat = packed.reshape(-1)
    out, off = [], 0
    for s in shapes:
        n = math.prod(s)
        out.append(flat[off:off + n].reshape(s))
        off += n
    return out


def kernel(x, ssm_norm_w, ssm_in_w, ssm_conv_w, ssm_conv_b, ssm_dt_bias, ssm_a_log, ssm_d, ssm_gate_norm_w, ssm_out_w, kv_norm_w, w_k, w_v, attn_norm_w, w_q, w_o, ffn_norm_w, ffn_up_w, ffn_conv_w, ffn_conv_b, ffn_down_w, final_norm_w, loss_target, m_ssm_norm_w, m_ssm_in_w, m_ssm_conv_w, m_ssm_conv_b, m_ssm_dt_bias, m_ssm_a_log, m_ssm_d, m_ssm_gate_norm_w, m_ssm_out_w, m_kv_norm_w, m_w_k, m_w_v, m_attn_norm_w, m_w_q, m_w_o, m_ffn_norm_w, m_ffn_up_w, m_ffn_conv_w, m_ffn_conv_b, m_ffn_down_w, m_final_norm_w, v_ssm_norm_w, v_ssm_in_w, v_ssm_conv_w, v_ssm_conv_b, v_ssm_dt_bias, v_ssm_a_log, v_ssm_d, v_ssm_gate_norm_w, v_ssm_out_w, v_kv_norm_w, v_w_k, v_w_v, v_attn_norm_w, v_w_q, v_w_o, v_ffn_norm_w, v_ffn_up_w, v_ffn_conv_w, v_ffn_conv_b, v_ffn_down_w, v_final_norm_w):
    env = dict(locals())
    p = {n: env[n] for n in _WEIGHTS}
    mom = {n: env["m_" + n] for n in _WEIGHTS}
    var = {n: env["v_" + n] for n in _WEIGHTS}
    T = x.shape[1]
    me = 4 * lax.axis_index("x") + 2 * lax.axis_index("y") + lax.axis_index("c")

    W = _gather_weights(p)
    loss_row, dx, f = _local_step(x.reshape(T, D_MODEL), loss_target.reshape(T, D_MODEL), W)
    loss = lax.psum(loss_row[0, 0], ("x", "y", "c"))

    big = _big_grad_blocks(f)
    small_names = _SMALL_REPL + _SMALL_SHARDED
    small_full = _pack_small([f[n] for n in small_names])
    small_bcast = jnp.broadcast_to(small_full[None], (N_DEV,) + small_full.shape)
    got = _exchange([big[n] for n in _BIG] + [small_bcast], name="exchange_grads")
    big_parts = dict(zip(_BIG, got[:-1]))

    zero = jnp.zeros_like(small_full)
    g_small_sum = _adamw(got[-1], zero, zero, zero, name="sum_small_grads", tr=small_full.shape[0])[0]
    full_shapes = [f[n].shape for n in small_names]
    g_small = dict(zip(small_names, _unpack_small(g_small_sum, full_shapes)))
    for n in _SMALL_SHARDED:
        width = p[n].shape[-1]
        g_small[n] = lax.dynamic_slice_in_dim(g_small[n], me * width, width, axis=g_small[n].ndim - 1)

    out_g, out_d, out_m, out_v = {}, {}, {}, {}
    for n in _BIG:
        w2, m2, v2 = _as2d(p[n]), _as2d(mom[n]), _as2d(var[n])
        tr = 352 if n == "ffn_down_w" else 256
        g, d, nm, nv = _adamw(big_parts[n], w2, m2, v2, name="adamw_" + n, tr=tr)
        out_g[n], out_d[n], out_m[n], out_v[n] = (t.reshape(p[n].shape) for t in (g, d, nm, nv))
    sw = _pack_small([p[n] for n in small_names])
    sm = _pack_small([mom[n] for n in small_names])
    sv = _pack_small([var[n] for n in small_names])
    sg = _pack_small([g_small[n] for n in small_names])
    _, d, nm, nv = _adamw(sg[None], sw, sm, sv, name="adamw_small", tr=sw.shape[0])
    shard_shapes = [p[n].shape for n in small_names]
    for n, dd, mm, vv in zip(small_names, _unpack_small(d, shard_shapes), _unpack_small(nm, shard_shapes),
                             _unpack_small(nv, shard_shapes)):
        out_g[n] = g_small[n].reshape(p[n].shape)
        out_d[n], out_m[n], out_v[n] = dd, mm, vv

    return (loss, dx.reshape(x.shape), *[out_g[n] for n in _WEIGHTS], *[out_d[n] for n in _WEIGHTS],
            *[out_m[n] for n in _WEIGHTS], *[out_v[n] for n in _WEIGHTS])
```

```python
import functools
import math

import jax
import jax.numpy as jnp
from jax import lax
from jax.experimental import pallas as pl
from jax.experimental.pallas import tpu as pltpu

F32 = jnp.float32
BF16 = jnp.bfloat16
EPS = 1e-6

D_MODEL = 1024
D_INNER = 2048
SSM_HEADS = 32
SSM_GROUPS = 4
SSM_STATE = 128
SSM_CONV = 4
SSM_CHUNK = 128
GN = SSM_GROUPS * SSM_STATE
CONV_DIM = D_INNER + 2 * GN
IN_PROJ_DIM = D_INNER + CONV_DIM + SSM_HEADS
IN_PROJ_PAD = 5376
SB_HEADS = 16
SB_HEAD_DIM = 64
SB_BLOCK = 128
D_FF = 2816
FFN_CONV = 3
N_DEV = 8

ADAM_LR = 0.001
ADAM_B1 = 0.9
ADAM_B2 = 0.999
ADAM_EPS = 1e-08
ADAM_WD = 0.01
ADAM_STEP = 10

_MESH = pl.DeviceIdType.MESH
_NT = (((1,), (1,)), ((), ()))
_TN = (((0,), (0,)), ((), ()))
_ANY = pl.BlockSpec(memory_space=pl.ANY)


def _cparams(sem, vmem_mb=48):
    return pltpu.CompilerParams(dimension_semantics=sem, vmem_limit_bytes=vmem_mb * 1024 * 1024)


def _sigmoid(x):
    return 1.0 / (1.0 + jnp.exp(-x))


def _softplus(x):
    return jnp.maximum(x, 0.0) + jnp.log(1.0 + jnp.exp(-jnp.abs(x)))


def _rms_fwd(xv, w):
    r = lax.rsqrt(jnp.mean(xv * xv, axis=-1, keepdims=True) + EPS)
    return xv * r * w


def _mm_fwd(x, w, *, name, norm_w=None, residual=None, out_dtype=F32, tm=512, tn=512):
    M, K = x.shape
    N = w.shape[1]
    tm, tn = min(tm, M), min(tn, N)
    assert M % tm == 0 and N % tn == 0, (name, M, N, tm, tn)
    has_norm, has_res = norm_w is not None, residual is not None

    def body(*refs):
        x_ref, w_ref = refs[0], refs[1]
        p = 2
        nw_ref = r_ref = None
        if has_norm:
            nw_ref = refs[p]
            p += 1
        if has_res:
            r_ref = refs[p]
            p += 1
        o_ref, xn_ref = refs[p], refs[p + 1]

        @pl.when(pl.program_id(1) == 0)
        def _():
            xv = x_ref[...].astype(F32)
            if has_norm:
                xv = _rms_fwd(xv, nw_ref[...])
            xn_ref[...] = xv.astype(BF16)

        acc = jnp.dot(xn_ref[...], w_ref[...], preferred_element_type=F32)
        if has_res:
            acc = acc + r_ref[...]
        o_ref[...] = acc.astype(out_dtype)

    in_specs = [pl.BlockSpec((tm, K), lambda i, j: (i, 0)), pl.BlockSpec((K, tn), lambda i, j: (0, j))]
    args = [x, w]
    if has_norm:
        in_specs.append(pl.BlockSpec((1, K), lambda i, j: (0, 0)))
        args.append(norm_w.reshape(1, K))
    if has_res:
        in_specs.append(pl.BlockSpec((tm, tn), lambda i, j: (i, j)))
        args.append(residual)
    return pl.pallas_call(
        body, name=name, grid=(M // tm, N // tn), in_specs=in_specs,
        out_specs=pl.BlockSpec((tm, tn), lambda i, j: (i, j)),
        out_shape=jax.ShapeDtypeStruct((M, N), out_dtype),
        scratch_shapes=[pltpu.VMEM((tm, K), BF16)],
        compiler_params=_cparams(("parallel", "arbitrary")))(*args)


def _mm_nt(dy, w, *, name, epi=None, out_dtype=F32, tm=512, tn=512, tk=512):
    M, K = dy.shape
    N = w.shape[0]
    tm, tk = min(tm, M), min(tk, K)
    tn = N if epi is not None else min(tn, N)
    assert M % tm == 0 and N % tn == 0 and K % tk == 0, (name, M, N, K, tm, tn, tk)
    nk = K // tk
    has_epi = epi is not None

    def body(*refs):
        if has_epi:
            dy_ref, w_ref, h_ref, nw_ref, r_ref, o_ref, dnw_ref, acc_ref = refs
        else:
            dy_ref, w_ref, o_ref, acc_ref = refs
        i = pl.program_id(0)
        k = pl.program_id(2)

        @pl.when(k == 0)
        def _():
            acc_ref[...] = jnp.zeros_like(acc_ref)

        acc_ref[...] += lax.dot_general(dy_ref[...].astype(BF16), w_ref[...], _NT, preferred_element_type=F32)

        @pl.when(k == nk - 1)
        def _():
            du = acc_ref[...]
            if has_epi:
                hv = h_ref[...]
                r = lax.rsqrt(jnp.mean(hv * hv, axis=-1, keepdims=True) + EPS)
                xhat = hv * r
                dxh = du * nw_ref[...]
                dx = r * (dxh - xhat * jnp.mean(dxh * xhat, axis=-1, keepdims=True))
                o_ref[...] = (r_ref[...] + dx).astype(out_dtype)
                contrib = jnp.sum(du * xhat, axis=0, keepdims=True)

                @pl.when(i == 0)
                def _():
                    dnw_ref[...] = contrib

                @pl.when(i > 0)
                def _():
                    dnw_ref[...] += contrib
            else:
                o_ref[...] = du.astype(out_dtype)

    in_specs = [pl.BlockSpec((tm, tk), lambda i, j, k: (i, k)), pl.BlockSpec((tn, tk), lambda i, j, k: (j, k))]
    args = [dy, w]
    out_specs = [pl.BlockSpec((tm, tn), lambda i, j, k: (i, j))]
    out_shape = [jax.ShapeDtypeStruct((M, N), out_dtype)]
    if has_epi:
        h, nw, res = epi
        in_specs += [pl.BlockSpec((tm, N), lambda i, j, k: (i, 0)), pl.BlockSpec((1, N), lambda i, j, k: (0, 0)),
                     pl.BlockSpec((tm, N), lambda i, j, k: (i, 0))]
        args += [h, nw.reshape(1, N), res]
        out_specs.append(pl.BlockSpec((1, N), lambda i, j, k: (0, 0)))
        out_shape.append(jax.ShapeDtypeStruct((1, N), F32))
    outs = pl.pallas_call(
        body, name=name, grid=(M // tm, N // tn, nk), in_specs=in_specs, out_specs=out_specs, out_shape=out_shape,
        scratch_shapes=[pltpu.VMEM((tm, tn), F32)],
        compiler_params=_cparams(("arbitrary", "arbitrary", "arbitrary")))(*args)
    return (outs[0], outs[1]) if has_epi else outs[0]


def _mm_tn(x, dy, *, name, norm_w=None, out_dtype=BF16, tk1=1024, tn=512, tt=512):
    T, K1 = x.shape
    N = dy.shape[1]
    tk1, tn, tt = min(tk1, K1), min(tn, N), min(tt, T)
    has_norm = norm_w is not None
    assert K1 % tk1 == 0 and N % tn == 0 and T % tt == 0, (name, K1, N, T, tk1, tn, tt)
    assert not has_norm or tk1 == K1
    nt = T // tt

    def body(*refs):
        if has_norm:
            x_ref, dy_ref, nw_ref, o_ref, acc_ref = refs
        else:
            x_ref, dy_ref, o_ref, acc_ref = refs
        t = pl.program_id(2)

        @pl.when(t == 0)
        def _():
            acc_ref[...] = jnp.zeros_like(acc_ref)

        xv = x_ref[...]
        if has_norm:
            xv = _rms_fwd(xv.astype(F32), nw_ref[...])
        acc_ref[...] += lax.dot_general(xv.astype(BF16), dy_ref[...].astype(BF16), _TN, preferred_element_type=F32)

        @pl.when(t == nt - 1)
        def _():
            o_ref[...] = acc_ref[...].astype(out_dtype)

    in_specs = [pl.BlockSpec((tt, tk1), lambda a, b, t: (t, a)), pl.BlockSpec((tt, tn), lambda a, b, t: (t, b))]
    args = [x, dy]
    if has_norm:
        in_specs.append(pl.BlockSpec((1, K1), lambda a, b, t: (0, 0)))
        args.append(norm_w.reshape(1, K1))
    return pl.pallas_call(
        body, name=name, grid=(K1 // tk1, N // tn, nt), in_specs=in_specs,
        out_specs=pl.BlockSpec((tk1, tn), lambda a, b, t: (a, b)),
        out_shape=jax.ShapeDtypeStruct((K1, N), out_dtype),
        scratch_shapes=[pltpu.VMEM((tk1, tn), F32)],
        compiler_params=_cparams(("parallel", "parallel", "arbitrary")))(*args)


def _shift_down(xb, prev8, j):
    main = pltpu.roll(xb, j, 0)
    head = pltpu.roll(xb[0:8], j, 0)
    ph = pltpu.roll(prev8, j, 0)
    row8 = lax.broadcasted_iota(jnp.int32, head.shape, 0)
    head = jnp.where(row8 < j, ph, head)
    return jnp.concatenate([head, main[8:]], axis=0)


def _shift_up(xb, next8, j):
    tt = xb.shape[0]
    main = pltpu.roll(xb, tt - j, 0)
    tail = pltpu.roll(xb[tt - 8:tt], 8 - j, 0)
    nh = pltpu.roll(next8, 8 - j, 0)
    row8 = lax.broadcasted_iota(jnp.int32, tail.shape, 0)
    tail = jnp.where(row8 + j >= 8, nh, tail)
    return jnp.concatenate([main[:tt - 8], tail], axis=0)


def _conv_hid(xb, prev8, w, b_row, K):
    out = b_row
    shifted = []
    for j in range(K):
        sh = K - 1 - j
        xs = xb if sh == 0 else _shift_down(xb, prev8, sh)
        shifted.append(xs)
        out = out + xs * w[j:j + 1, :]
    return out, shifted


def _prev_idx(i, nb8):
    return jnp.maximum(i * nb8 - 1, 0)


def _ssm_conv_fwd(zx, w, b, *, name, tt=512, tc=512):
    T = zx.shape[0]
    tt = min(tt, T)
    C, K = CONV_DIM, SSM_CONV
    cb0, nb8 = D_INNER // tc, tt // 8

    def body(x_ref, p_ref, w_ref, b_ref, o_ref):
        first = (pl.program_id(1) > 0).astype(F32)
        hid, _ = _conv_hid(x_ref[...], p_ref[...] * first, w_ref[...], b_ref[...], K)
        o_ref[...] = hid * _sigmoid(hid)

    return pl.pallas_call(
        body, name=name, grid=(C // tc, T // tt),
        in_specs=[pl.BlockSpec((tt, tc), lambda c, i: (i, c + cb0)),
                  pl.BlockSpec((8, tc), lambda c, i: (_prev_idx(i, nb8), c + cb0)),
                  pl.BlockSpec((K, tc), lambda c, i: (0, c)), pl.BlockSpec((1, tc), lambda c, i: (0, c))],
        out_specs=pl.BlockSpec((tt, tc), lambda c, i: (i, c)),
        out_shape=jax.ShapeDtypeStruct((T, C), F32),
        compiler_params=_cparams(("parallel", "parallel")))(zx, zx, w, b)


def _ssm_conv_bwd_pre(zx, w, b, dout, *, name, tt=512, tc=512):
    T = zx.shape[0]
    tt = min(tt, T)
    C, K = CONV_DIM, SSM_CONV
    cb0, nb8 = D_INNER // tc, tt // 8

    def body(x_ref, p_ref, w_ref, b_ref, d_ref, dh_ref, dw_ref, db_ref):
        t = pl.program_id(1)
        first = (t > 0).astype(F32)
        hid, shifted = _conv_hid(x_ref[...], p_ref[...] * first, w_ref[...], b_ref[...], K)
        sg = _sigmoid(hid)
        dh = d_ref[...] * (sg * (1.0 + hid * (1.0 - sg)))
        dh_ref[...] = dh

        @pl.when(t == 0)
        def _():
            dw_ref[...] = jnp.zeros_like(dw_ref)
            db_ref[...] = jnp.zeros_like(db_ref)

        db_ref[...] += jnp.sum(dh, axis=0, keepdims=True)
        for j in range(K):
            dw_ref[j:j + 1, :] += jnp.sum(dh * shifted[j], axis=0, keepdims=True)

    return pl.pallas_call(
        body, name=name, grid=(C // tc, T // tt),
        in_specs=[pl.BlockSpec((tt, tc), lambda c, i: (i, c + cb0)),
                  pl.BlockSpec((8, tc), lambda c, i: (_prev_idx(i, nb8), c + cb0)),
                  pl.BlockSpec((K, tc), lambda c, i: (0, c)), pl.BlockSpec((1, tc), lambda c, i: (0, c)),
                  pl.BlockSpec((tt, tc), lambda c, i: (i, c))],
        out_specs=[pl.BlockSpec((tt, tc), lambda c, i: (i, c)), pl.BlockSpec((K, tc), lambda c, i: (0, c)),
                   pl.BlockSpec((1, tc), lambda c, i: (0, c))],
        out_shape=[jax.ShapeDtypeStruct((T, C), F32), jax.ShapeDtypeStruct((K, C), F32),
                   jax.ShapeDtypeStruct((1, C), F32)],
        compiler_params=_cparams(("parallel", "arbitrary")))(zx, zx, w, b, dout)


def _conv_bwd_in(dh, w, *, name, K, tt=512, tc=512, out_dtype=BF16):
    T, C = dh.shape
    tt = min(tt, T)
    nb8, nT = tt // 8, T // tt
    last8 = T // 8 - 1

    def body(d_ref, n_ref, w_ref, o_ref):
        notlast = (pl.program_id(1) < nT - 1).astype(F32)
        d = d_ref[...]
        nxt = n_ref[...] * notlast
        w_ = w_ref[...]
        acc = d * w_[K - 1:K, :]
        for sh in range(1, K):
            acc = acc + _shift_up(d, nxt, sh) * w_[K - 1 - sh:K - sh, :]
        o_ref[...] = acc.astype(out_dtype)

    return pl.pallas_call(
        body, name=name, grid=(C // tc, nT),
        in_specs=[pl.BlockSpec((tt, tc), lambda c, i: (i, c)),
                  pl.BlockSpec((8, tc), lambda c, i: (jnp.minimum((i + 1) * nb8, last8), c)),
                  pl.BlockSpec((K, tc), lambda c, i: (0, c))],
        out_specs=pl.BlockSpec((tt, tc), lambda c, i: (i, c)),
        out_shape=jax.ShapeDtypeStruct((T, C), out_dtype),
        compiler_params=_cparams(("parallel", "parallel")))(dh, dh, w)


def _ffn_conv_fwd(a, w, b, *, name, tt=256, tc=1408):
    T = a.shape[0]
    tt = min(tt, T)
    K, nbh, nb8 = FFN_CONV, D_FF // tc, tt // 8

    def body(ag_ref, pg_ref, av_ref, pv_ref, wg_ref, wv_ref, bg_ref, bv_ref, o_ref):
        first = (pl.program_id(1) > 0).astype(F32)
        hg, _ = _conv_hid(ag_ref[...], pg_ref[...] * first, wg_ref[...], bg_ref[...], K)
        hv, _ = _conv_hid(av_ref[...], pv_ref[...] * first, wv_ref[...], bv_ref[...], K)
        o_ref[...] = (hg * _sigmoid(hg) * hv).astype(BF16)

    return pl.pallas_call(
        body, name=name, grid=(nbh, T // tt),
        in_specs=[pl.BlockSpec((tt, tc), lambda c, i: (i, c)),
                  pl.BlockSpec((8, tc), lambda c, i: (_prev_idx(i, nb8), c)),
                  pl.BlockSpec((tt, tc), lambda c, i: (i, c + nbh)),
                  pl.BlockSpec((8, tc), lambda c, i: (_prev_idx(i, nb8), c + nbh)),
                  pl.BlockSpec((K, tc), lambda c, i: (0, c)), pl.BlockSpec((K, tc), lambda c, i: (0, c + nbh)),
                  pl.BlockSpec((1, tc), lambda c, i: (0, c)), pl.BlockSpec((1, tc), lambda c, i: (0, c + nbh))],
        out_specs=pl.BlockSpec((tt, tc), lambda c, i: (i, c)),
        out_shape=jax.ShapeDtypeStruct((T, D_FF), BF16),
        compiler_params=_cparams(("parallel", "parallel")))(a, a, a, a, w, w, b, b)


def _ffn_conv_bwd_pre(a, w, b, dp, *, name, tt=256, tc=1408):
    T = a.shape[0]
    tt = min(tt, T)
    K, nbh, nb8 = FFN_CONV, D_FF // tc, tt // 8

    def body(ao_ref, po_ref, ag_ref, pg_ref, av_ref, pv_ref, wg_ref, wv_ref, bg_ref, bv_ref, dp_ref,
             dh_ref, dw_ref, db_ref):
        j = pl.program_id(0)
        t = pl.program_id(1)
        first = (t > 0).astype(F32)
        hg, _ = _conv_hid(ag_ref[...], pg_ref[...] * first, wg_ref[...], bg_ref[...], K)
        hv, _ = _conv_hid(av_ref[...], pv_ref[...] * first, wv_ref[...], bv_ref[...], K)
        sg = _sigmoid(hg)
        d = dp_ref[...].astype(F32)
        is_gate = (j < nbh).astype(F32)
        dh = d * (is_gate * (hv * (sg * (1.0 + hg * (1.0 - sg)))) + (1.0 - is_gate) * (hg * sg))
        dh_ref[...] = dh
        xo = ao_ref[...]
        po = po_ref[...] * first

        @pl.when(t == 0)
        def _():
            dw_ref[...] = jnp.zeros_like(dw_ref)
            db_ref[...] = jnp.zeros_like(db_ref)

        db_ref[...] += jnp.sum(dh, axis=0, keepdims=True)
        for jj in range(K):
            sh = K - 1 - jj
            xs = xo if sh == 0 else _shift_down(xo, po, sh)
            dw_ref[jj:jj + 1, :] += jnp.sum(dh * xs, axis=0, keepdims=True)

    def gi(c):
        return lax.rem(c, nbh)

    return pl.pallas_call(
        body, name=name, grid=(2 * nbh, T // tt),
        in_specs=[pl.BlockSpec((tt, tc), lambda c, i: (i, c)),
                  pl.BlockSpec((8, tc), lambda c, i: (_prev_idx(i, nb8), c)),
                  pl.BlockSpec((tt, tc), lambda c, i: (i, gi(c))),
                  pl.BlockSpec((8, tc), lambda c, i: (_prev_idx(i, nb8), gi(c))),
                  pl.BlockSpec((tt, tc), lambda c, i: (i, gi(c) + nbh)),
                  pl.BlockSpec((8, tc), lambda c, i: (_prev_idx(i, nb8), gi(c) + nbh)),
                  pl.BlockSpec((K, tc), lambda c, i: (0, gi(c))), pl.BlockSpec((K, tc), lambda c, i: (0, gi(c) + nbh)),
                  pl.BlockSpec((1, tc), lambda c, i: (0, gi(c))), pl.BlockSpec((1, tc), lambda c, i: (0, gi(c) + nbh)),
                  pl.BlockSpec((tt, tc), lambda c, i: (i, gi(c)))],
        out_specs=[pl.BlockSpec((tt, tc), lambda c, i: (i, c)), pl.BlockSpec((K, tc), lambda c, i: (0, c)),
                   pl.BlockSpec((1, tc), lambda c, i: (0, c))],
        out_shape=[jax.ShapeDtypeStruct((T, 2 * D_FF), F32), jax.ShapeDtypeStruct((K, 2 * D_FF), F32),
                   jax.ShapeDtypeStruct((1, 2 * D_FF), F32)],
        compiler_params=_cparams(("parallel", "arbitrary")))(a, a, a, a, a, a, w, w, b, b, dp)


def _cumsum_rows(x):
    L = x.shape[0]
    row = lax.broadcasted_iota(jnp.int32, x.shape, 0)
    k = 1
    while k < L:
        x = x + jnp.where(row >= k, pltpu.roll(x, k, 0), 0.0)
        k *= 2
    return x


def _rcumsum_rows(x):
    L = x.shape[0]
    row = lax.broadcasted_iota(jnp.int32, x.shape, 0)
    k = 1
    while k < L:
        x = x + jnp.where(row < L - k, pltpu.roll(x, L - k, 0), 0.0)
        k *= 2
    return x


def _ssd_common(dt_ref, par_ref):
    par = par_ref[...]
    raw = dt_ref[...] + par[0:1, :]
    dt = _softplus(raw)
    a = -jnp.exp(par[1:2, :])
    cs = _cumsum_rows(dt * a)
    L = cs.shape[0]
    cs_last = cs[L - 1:L, :]
    return raw, dt, a, par[2:3, :], cs, cs.T, jnp.exp(cs), jnp.exp(cs_last - cs), jnp.exp(cs_last)


def _ssd_specs(nc, rev):
    L = SSM_CHUNK

    def ci(c):
        return nc - 1 - c if rev else c

    return [pl.BlockSpec((L, 512), lambda g, c: (ci(c), g)),
            pl.BlockSpec((L, 128), lambda g, c: (ci(c), 16 + g)),
            pl.BlockSpec((L, 128), lambda g, c: (ci(c), 20 + g)),
            pl.BlockSpec((None, L, 128), lambda g, c: (g, ci(c), 0)),
            pl.BlockSpec((None, 8, 128), lambda g, c: (g, 0, 0)),
            pl.BlockSpec((L, 512), lambda g, c: (ci(c), g)),
            pl.BlockSpec((1, 512), lambda g, c: (0, g))], ci


def _ssd_fwd(xbc_c, zx, dtg, par, gnw, *, name):
    T = xbc_c.shape[0]
    L = SSM_CHUNK
    nc = T // L
    in_specs, ci = _ssd_specs(nc, False)

    def body(xs_ref, b_ref, c_ref, dt_ref, par_ref, z_ref, gnw_ref, y_ref, yn_ref, st_ref, h_ref):
        @pl.when(pl.program_id(1) == 0)
        def _():
            h_ref[...] = jnp.zeros_like(h_ref)

        _, dt, _, dsk, cs, csT, ecs, eend, dec = _ssd_common(dt_ref, par_ref)
        Bb = b_ref[...].astype(BF16)
        Cb = c_ref[...].astype(BF16)
        G = lax.dot_general(Cb, Bb, _NT, preferred_element_type=F32)
        row = lax.broadcasted_iota(jnp.int32, (L, L), 0)
        col = lax.broadcasted_iota(jnp.int32, (L, L), 1)
        tril = col <= row
        lo = lax.broadcasted_iota(jnp.int32, (L, 128), 1) < 64
        lo1 = lax.broadcasted_iota(jnp.int32, (1, 128), 1) < 64
        for pp in range(4):
            hA, hB = 2 * pp, 2 * pp + 1

            def sel(m):
                return jnp.where(lo, m[:, hA:hA + 1], m[:, hB:hB + 1])

            def sel1(m):
                return jnp.where(lo1, m[:, hA:hA + 1], m[:, hB:hB + 1])

            X = xs_ref[:, pp * 128:(pp + 1) * 128]
            xd = X * sel(dt)
            xdb = xd.astype(BF16)
            ys = []
            for h in (hA, hB):
                Lm = jnp.where(tril, jnp.exp(jnp.minimum(cs[:, h:h + 1] - csT[h:h + 1, :], 0.0)), 0.0)
                ys.append(jnp.dot((G * Lm).astype(BF16), xdb, preferred_element_type=F32))
            Hp = h_ref[pp]
            st_ref[pp] = Hp
            yoff = jnp.dot(Cb, Hp.astype(BF16), preferred_element_type=F32) * sel(ecs)
            y_ref[:, pp * 128:(pp + 1) * 128] = jnp.where(lo, ys[0], ys[1]) + yoff + sel1(dsk) * X
            S = lax.dot_general(Bb, (xd * sel(eend)).astype(BF16), _TN, preferred_element_type=F32)
            h_ref[pp] = Hp * sel1(dec) + S
        zv = z_ref[...]
        yg = y_ref[...] * (zv * _sigmoid(zv))
        yn_ref[...] = _rms_fwd(yg, gnw_ref[...]).astype(BF16)

    return pl.pallas_call(
        body, name=name, grid=(SSM_GROUPS, nc), in_specs=in_specs,
        out_specs=[pl.BlockSpec((L, 512), lambda g, c: (c, g)), pl.BlockSpec((L, 512), lambda g, c: (c, g)),
                   pl.BlockSpec((None, None, 4, 128, 128), lambda g, c: (g, c, 0, 0, 0))],
        out_shape=[jax.ShapeDtypeStruct((T, D_INNER), F32), jax.ShapeDtypeStruct((T, D_INNER), BF16),
                   jax.ShapeDtypeStruct((SSM_GROUPS, nc, 4, 128, 128), F32)],
        scratch_shapes=[pltpu.VMEM((4, 128, 128), F32)],
        compiler_params=_cparams(("parallel", "arbitrary")))(xbc_c, xbc_c, xbc_c, dtg, par, zx, gnw)


def _ssd_bwd(xbc_c, zx, dtg, par, gnw, y, st, dyn, *, name):
    T = xbc_c.shape[0]
    L = SSM_CHUNK
    nc = T // L
    in_specs, ci = _ssd_specs(nc, True)
    in_specs += [pl.BlockSpec((L, 512), lambda g, c: (ci(c), g)),
                 pl.BlockSpec((None, None, 4, 128, 128), lambda g, c: (g, ci(c), 0, 0, 0)),
                 pl.BlockSpec((L, 512), lambda g, c: (ci(c), g))]

    def body(xs_ref, b_ref, c_ref, dt_ref, par_ref, z_ref, gnw_ref, y_ref, st_ref, dyn_ref,
             dxs_ref, db_ref, dc_ref, dz_ref, ddt_ref, dgnw_ref, dpar_ref, dh_ref):
        @pl.when(pl.program_id(1) == 0)
        def _():
            dh_ref[...] = jnp.zeros_like(dh_ref)
            dgnw_ref[...] = jnp.zeros_like(dgnw_ref)
            dpar_ref[...] = jnp.zeros_like(dpar_ref)

        yv = y_ref[...]
        zv = z_ref[...]
        sg = _sigmoid(zv)
        sz = zv * sg
        yg = yv * sz
        r = lax.rsqrt(jnp.mean(yg * yg, axis=-1, keepdims=True) + EPS)
        yh = yg * r
        dyn = dyn_ref[...].astype(F32)
        dgnw_ref[...] += jnp.sum(dyn * yh, axis=0, keepdims=True)
        dyh = dyn * gnw_ref[...]
        dyg = r * (dyh - yh * jnp.mean(dyh * yh, axis=-1, keepdims=True))
        dY_all = dyg * sz
        dz_ref[...] = (dyg * yv * (sg * (1.0 + zv * (1.0 - sg)))).astype(dz_ref.dtype)

        raw, dt, a, dsk, cs, csT, ecs, eend, dec = _ssd_common(dt_ref, par_ref)
        Bb = b_ref[...].astype(BF16)
        Cb = c_ref[...].astype(BF16)
        G = lax.dot_general(Cb, Bb, _NT, preferred_element_type=F32)
        row = lax.broadcasted_iota(jnp.int32, (L, L), 0)
        col = lax.broadcasted_iota(jnp.int32, (L, L), 1)
        tril = col <= row
        lane = lax.broadcasted_iota(jnp.int32, (L, 128), 1)
        lo = lane < 64
        lane1 = lax.broadcasted_iota(jnp.int32, (1, 128), 1)
        lo1 = lane1 < 64
        rowc = lax.broadcasted_iota(jnp.int32, (L, 1), 0)
        dG = jnp.zeros((L, L), F32)
        dB = jnp.zeros((L, SSM_STATE), F32)
        dC = jnp.zeros((L, SSM_STATE), F32)
        dcs_mat = jnp.zeros((L, 128), F32)
        ddt_mat = jnp.zeros((L, 128), F32)
        dD_row = jnp.zeros((1, 128), F32)

        def tot(m):
            return jnp.sum(jnp.sum(m, axis=1, keepdims=True), axis=0, keepdims=True)

        for pp in range(4):
            hA, hB = 2 * pp, 2 * pp + 1

            def sel(m):
                return jnp.where(lo, m[:, hA:hA + 1], m[:, hB:hB + 1])

            def sel1(m):
                return jnp.where(lo1, m[:, hA:hA + 1], m[:, hB:hB + 1])

            X = xs_ref[:, pp * 128:(pp + 1) * 128]
            dY = dY_all[:, pp * 128:(pp + 1) * 128]
            dtsel = sel(dt)
            xd = X * dtsel
            xdb = xd.astype(BF16)
            dYb = dY.astype(BF16)
            Hp = st_ref[pp]
            Hb = Hp.astype(BF16)
            dHn = dh_ref[pp]
            dHb = dHn.astype(BF16)
            ecs_sel = sel(ecs)
            eend_sel = sel(eend)
            dxd_state = jnp.dot(Bb, dHb, preferred_element_type=F32) * eend_sel
            yoff = jnp.dot(Cb, Hb, preferred_element_type=F32) * ecs_sel
            dYe = (dY * ecs_sel).astype(BF16)
            dC = dC + lax.dot_general(dYe, Hb, _NT, preferred_element_type=F32)
            dB = dB + lax.dot_general((xd * eend_sel).astype(BF16), dHb, _NT, preferred_element_type=F32)
            dh_ref[pp] = dHn * sel1(dec) + lax.dot_general(Cb, dYe, _TN, preferred_element_type=F32)
            q = xd * dxd_state
            dyoff = dY * yoff
            hh = dHn * Hp
            dxd_diag = []
            for h, msk, msk1 in ((hA, lo, lo1), (hB, jnp.logical_not(lo), jnp.logical_not(lo1))):
                Lm = jnp.where(tril, jnp.exp(jnp.minimum(cs[:, h:h + 1] - csT[h:h + 1, :], 0.0)), 0.0)
                M = G * Lm
                dxd_diag.append(lax.dot_general(M.astype(BF16), dYb, _TN, preferred_element_type=F32))
                dM = lax.dot_general(jnp.where(msk, dY, 0.0).astype(BF16), xdb, _NT, preferred_element_type=F32)
                dG = dG + dM * Lm
                W = dM * M
                dcs_h = jnp.sum(W - W.T, axis=1, keepdims=True)
                dcs_h = dcs_h + jnp.sum(jnp.where(msk, dyoff - q, 0.0), axis=1, keepdims=True)
                tail = tot(jnp.where(msk, q, 0.0)) + dec[:, h:h + 1] * tot(jnp.where(msk, hh, 0.0))
                dcs_h = dcs_h + jnp.where(rowc == L - 1, tail, 0.0)
                dcs_mat = dcs_mat + jnp.where(lane == h, dcs_h, 0.0)
            dxd = jnp.where(lo, dxd_diag[0], dxd_diag[1]) + dxd_state
            prod = dxd * X
            dA_ = jnp.sum(jnp.where(lo, prod, 0.0), axis=1, keepdims=True)
            dB_ = jnp.sum(prod, axis=1, keepdims=True) - dA_
            ddt_mat = ddt_mat + jnp.where(lane == hA, dA_, 0.0) + jnp.where(lane == hB, dB_, 0.0)
            dxs_ref[:, pp * 128:(pp + 1) * 128] = dxd * dtsel + sel1(dsk) * dY
            dyx = jnp.sum(dY * X, axis=0, keepdims=True)
            sA = jnp.sum(jnp.where(lo1, dyx, 0.0), axis=1, keepdims=True)
            sB = jnp.sum(dyx, axis=1, keepdims=True) - sA
            dD_row = dD_row + jnp.where(lane1 == hA, sA, 0.0) + jnp.where(lane1 == hB, sB, 0.0)
        dGb = dG.astype(BF16)
        db_ref[...] = dB + lax.dot_general(dGb, Cb, _TN, preferred_element_type=F32)
        dc_ref[...] = dC + jnp.dot(dGb, Bb, preferred_element_type=F32)
        dad = _rcumsum_rows(dcs_mat)
        draw = (a * dad + ddt_mat) * _sigmoid(raw)
        ddt_ref[...] = draw
        dpar_ref[0:1, :] += jnp.sum(draw, axis=0, keepdims=True)
        dpar_ref[1:2, :] += jnp.sum(dt * dad, axis=0, keepdims=True) * a
        dpar_ref[2:3, :] += dD_row

    return pl.pallas_call(
        body, name=name, grid=(SSM_GROUPS, nc), in_specs=in_specs,
        out_specs=[pl.BlockSpec((L, 512), lambda g, c: (ci(c), g)),
                   pl.BlockSpec((L, 128), lambda g, c: (ci(c), g)),
                   pl.BlockSpec((L, 128), lambda g, c: (ci(c), g)),
                   pl.BlockSpec((L, 512), lambda g, c: (ci(c), g)),
                   pl.BlockSpec((None, L, 128), lambda g, c: (g, ci(c), 0)),
                   pl.BlockSpec((1, 512), lambda g, c: (0, g)),
                   pl.BlockSpec((None, 8, 128), lambda g, c: (g, 0, 0))],
        out_shape=[jax.ShapeDtypeStruct((T, D_INNER), F32), jax.ShapeDtypeStruct((T, GN), F32),
                   jax.ShapeDtypeStruct((T, GN), F32), jax.ShapeDtypeStruct((T, D_INNER), BF16),
                   jax.ShapeDtypeStruct((SSM_GROUPS, T, 128), F32), jax.ShapeDtypeStruct((1, D_INNER), F32),
                   jax.ShapeDtypeStruct((SSM_GROUPS, 8, 128), F32)],
        scratch_shapes=[pltpu.VMEM((4, 128, 128), F32)],
        compiler_params=_cparams(("parallel", "arbitrary")))(xbc_c, xbc_c, xbc_c, dtg, par, zx, gnw, y, st, dyn)


SB_KEYS = 512


def _hi_lo(v):
    hi = v.astype(BF16)
    return jnp.concatenate([hi, (v - hi.astype(F32)).astype(BF16)], axis=1)


def _tri2(cond):
    kk = lax.broadcasted_iota(jnp.int32, (2 * SB_BLOCK, SB_BLOCK), 0)
    kk = jnp.where(kk >= SB_BLOCK, kk - SB_BLOCK, kk)
    jj = lax.broadcasted_iota(jnp.int32, (2 * SB_BLOCK, SB_BLOCK), 1)
    return cond(kk, jj).astype(BF16)


def _sba_mask(i, J, b):
    Bq = SB_BLOCK
    rowi = lax.broadcasted_iota(jnp.int32, (2 * Bq, Bq), 0)
    qpos = i * Bq + jnp.where(rowi >= Bq, rowi - Bq, rowi)
    kpos = J * SB_KEYS + b * Bq + lax.broadcasted_iota(jnp.int32, (2 * Bq, Bq), 1)
    return kpos < qpos


def _stack_heads(v):
    lo = lax.broadcasted_iota(jnp.int32, v.shape, 1) < 64
    zero = jnp.zeros_like(v)
    return jnp.concatenate([jnp.where(lo, v, zero), jnp.where(lo, zero, v)], axis=0)


def _unstack_heads(v):
    lo = lax.broadcasted_iota(jnp.int32, (SB_BLOCK, 128), 1) < 64
    return jnp.where(lo, v[:SB_BLOCK], v[SB_BLOCK:])


def _sba_fwd(q, kv, *, name):
    T = q.shape[0]
    Bq = SB_BLOCK
    nq = T // Bq
    nsub = SB_KEYS // Bq
    assert T % SB_KEYS == 0
    scale = 1.0 / math.sqrt(SB_HEAD_DIM)

    def body(q_ref, k_ref, v_ref, o_ref, lt_ref):
        i = pl.program_id(1)
        qs = _stack_heads(q_ref[...] * scale)
        U2 = _tri2(lambda k, j: k > j)
        Jd = i // nsub

        def tile(J, c, acc, masked):
            off = pl.multiple_of(J * SB_KEYS, SB_KEYS)
            kb = k_ref[pl.ds(off, SB_KEYS), :]
            vb = v_ref[pl.ds(off, SB_KEYS), :]
            z = lax.dot_general(qs, kb, _NT, preferred_element_type=F32)
            parts = [None] * nsub
            for b in reversed(range(nsub)):
                zb = z[:, b * Bq:(b + 1) * Bq]
                s = _softplus(zb)
                if masked:
                    mask = _sba_mask(i, J, b)
                    l = jnp.where(mask, -s, 0.0)
                else:
                    l = -s
                R = c + jnp.dot(_hi_lo(l), U2, preferred_element_type=F32)
                A = jnp.exp(zb - s + R)
                if masked:
                    A = jnp.where(mask, A, 0.0)
                parts[b] = A.astype(BF16)
                c = c + jnp.sum(l, axis=1, keepdims=True)
            acc = acc + jnp.dot(jnp.concatenate(parts, axis=1), vb, preferred_element_type=F32)
            return c, acc

        c, acc = tile(Jd, jnp.zeros((2 * Bq, 1), F32), jnp.zeros((2 * Bq, 128), F32), True)
        c, acc = lax.fori_loop(0, Jd, lambda n, carry: tile(Jd - 1 - n, carry[0], carry[1], False), (c, acc))
        o_ref[...] = _unstack_heads(acc).astype(BF16)
        lt_ref[...] = _unstack_heads(jnp.broadcast_to(c, (2 * Bq, 128)))

    return pl.pallas_call(
        body, name=name, grid=(SB_HEADS // 2, nq),
        in_specs=[pl.BlockSpec((Bq, 128), lambda p, i: (i, p)), pl.BlockSpec((T, 128), lambda p, i: (0, p)),
                  pl.BlockSpec((T, 128), lambda p, i: (0, p + SB_HEADS // 2))],
        out_specs=[pl.BlockSpec((Bq, 128), lambda p, i: (i, p)), pl.BlockSpec((None, Bq, 128), lambda p, i: (p, i, 0))],
        out_shape=[jax.ShapeDtypeStruct((T, D_MODEL), BF16), jax.ShapeDtypeStruct((SB_HEADS // 2, T, 128), F32)],
        compiler_params=_cparams(("parallel", "parallel")))(q, kv, kv)


def _sba_bwd(q, kv, lt, do, *, name):
    T = q.shape[0]
    Bq = SB_BLOCK
    nq = T // Bq
    nsub = SB_KEYS // Bq
    assert T % SB_KEYS == 0
    scale = 1.0 / math.sqrt(SB_HEAD_DIM)

    def body(q_ref, k_ref, v_ref, lt_ref, do_ref, dq_ref, dk_ref, dv_ref, dk_acc, dv_acc):
        i = pl.program_id(1)

        @pl.when(i == 0)
        def _():
            dk_acc[...] = jnp.zeros_like(dk_acc)
            dv_acc[...] = jnp.zeros_like(dv_acc)

        qs = _stack_heads(q_ref[...] * scale)
        dos = _stack_heads(do_ref[...])
        ltv = lt_ref[...]
        Lt = jnp.concatenate([ltv[:, 0:1], ltv[:, 64:65]], axis=0)
        Uincl = _tri2(lambda k, j: k <= j)
        Uexcl = _tri2(lambda k, j: k < j)
        Jd = i // nsub

        def tile(J, pc, pe, dq_acc, masked):
            off = pl.multiple_of(J * SB_KEYS, SB_KEYS)
            kb = k_ref[pl.ds(off, SB_KEYS), :]
            vb = v_ref[pl.ds(off, SB_KEYS), :]
            z = lax.dot_general(qs, kb, _NT, preferred_element_type=F32)
            dA = lax.dot_general(dos, vb, _NT, preferred_element_type=F32)
            a_parts, dz_parts = [], []
            for b in range(nsub):
                zb = z[:, b * Bq:(b + 1) * Bq]
                s = _softplus(zb)
                if masked:
                    mask = _sba_mask(i, J, b)
                    l = jnp.where(mask, -s, 0.0)
                else:
                    l = -s
                P = pc + jnp.dot(_hi_lo(l), Uincl, preferred_element_type=F32)
                g = zb - s
                A = jnp.exp(g + (Lt - P))
                if masked:
                    A = jnp.where(mask, A, 0.0)
                E = dA[:, b * Bq:(b + 1) * Bq] * A
                PE = pe + jnp.dot(_hi_lo(E), Uexcl, preferred_element_type=F32)
                dz = E - jnp.exp(g) * (E + PE)
                if masked:
                    dz = jnp.where(mask, dz, 0.0)
                a_parts.append(A.astype(BF16))
                dz_parts.append(dz.astype(BF16))
                pc = pc + jnp.sum(l, axis=1, keepdims=True)
                pe = pe + jnp.sum(E, axis=1, keepdims=True)
            dzt = jnp.concatenate(dz_parts, axis=1)
            at = jnp.concatenate(a_parts, axis=1)
            dq_acc = dq_acc + jnp.dot(dzt, kb, preferred_element_type=F32)
            dk_acc[pl.ds(off, SB_KEYS), :] += lax.dot_general(dzt, qs, _TN, preferred_element_type=F32)
            dv_acc[pl.ds(off, SB_KEYS), :] += lax.dot_general(at, dos, _TN, preferred_element_type=F32)
            return pc, pe, dq_acc

        zc = jnp.zeros((2 * Bq, 1), F32)
        carry = lax.fori_loop(0, Jd, lambda J, cr: tile(J, cr[0], cr[1], cr[2], False),
                              (zc, zc, jnp.zeros((2 * Bq, 128), F32)))
        _, _, dq_acc = tile(Jd, carry[0], carry[1], carry[2], True)
        dq_ref[...] = (_unstack_heads(dq_acc) * scale).astype(BF16)

        @pl.when(i == nq - 1)
        def _():
            dk_ref[...] = dk_acc[...].astype(BF16)
            dv_ref[...] = dv_acc[...].astype(BF16)

    return pl.pallas_call(
        body, name=name, grid=(SB_HEADS // 2, nq),
        in_specs=[pl.BlockSpec((Bq, 128), lambda p, i: (i, p)), pl.BlockSpec((T, 128), lambda p, i: (0, p)),
                  pl.BlockSpec((T, 128), lambda p, i: (0, p + SB_HEADS // 2)),
                  pl.BlockSpec((None, Bq, 128), lambda p, i: (p, i, 0)), pl.BlockSpec((Bq, 128), lambda p, i: (i, p))],
        out_specs=[pl.BlockSpec((Bq, 128), lambda p, i: (i, p)), pl.BlockSpec((T, 128), lambda p, i: (0, p)),
                   pl.BlockSpec((T, 128), lambda p, i: (0, p))],
        out_shape=[jax.ShapeDtypeStruct((T, D_MODEL), BF16), jax.ShapeDtypeStruct((T, D_MODEL), BF16),
                   jax.ShapeDtypeStruct((T, D_MODEL), BF16)],
        scratch_shapes=[pltpu.VMEM((T, 128), F32), pltpu.VMEM((T, 128), F32)],
        compiler_params=_cparams(("parallel", "arbitrary")))(q, kv, kv, lt, do)


def _loss_head(h, tgt, w, *, name, tt=512):
    T, D = h.shape
    tt = min(tt, T)

    def body(h_ref, t_ref, w_ref, loss_ref, dh_ref, dw_ref):
        i = pl.program_id(0)
        hv = h_ref[...]
        wv = w_ref[...]
        r = lax.rsqrt(jnp.mean(hv * hv, axis=-1, keepdims=True) + EPS)
        xhat = hv * r
        err = xhat * wv - t_ref[...]
        part = 0.5 * jnp.sum(jnp.mean(err * err, axis=-1, keepdims=True), axis=0, keepdims=True)
        dy = err * (1.0 / D)
        dxh = dy * wv
        dh_ref[...] = r * (dxh - xhat * jnp.mean(dxh * xhat, axis=-1, keepdims=True))
        dwc = jnp.sum(dy * xhat, axis=0, keepdims=True)

        @pl.when(i == 0)
        def _():
            loss_ref[...] = jnp.broadcast_to(part, loss_ref.shape)
            dw_ref[...] = dwc

        @pl.when(i > 0)
        def _():
            loss_ref[...] += jnp.broadcast_to(part, loss_ref.shape)
            dw_ref[...] += dwc

    return pl.pallas_call(
        body, name=name, grid=(T // tt,),
        in_specs=[pl.BlockSpec((tt, D), lambda i: (i, 0)), pl.BlockSpec((tt, D), lambda i: (i, 0)),
                  pl.BlockSpec((1, D), lambda i: (0, 0))],
        out_specs=[pl.BlockSpec((1, 128), lambda i: (0, 0)), pl.BlockSpec((tt, D), lambda i: (i, 0)),
                   pl.BlockSpec((1, D), lambda i: (0, 0))],
        out_shape=[jax.ShapeDtypeStruct((1, 128), F32), jax.ShapeDtypeStruct((T, D), F32),
                   jax.ShapeDtypeStruct((1, D), F32)],
        compiler_params=_cparams(("arbitrary",)))(h, tgt, w.reshape(1, D))


def _adamw(parts, w, m, v, *, name, tr=256):
    P, R, C = parts.shape
    tr = min(tr, R)
    assert R % tr == 0, (name, R, tr)
    c1 = 1.0 - ADAM_B1 ** ADAM_STEP
    c2 = 1.0 - ADAM_B2 ** ADAM_STEP

    def body(p_ref, w_ref, m_ref, v_ref, g_ref, d_ref, nm_ref, nv_ref):
        g = p_ref[0].astype(F32)
        for k in range(1, P):
            g = g + p_ref[k].astype(F32)
        mn = ADAM_B1 * m_ref[...] + (1.0 - ADAM_B1) * g
        vn = ADAM_B2 * v_ref[...] + (1.0 - ADAM_B2) * (g * g)
        g_ref[...] = g
        nm_ref[...] = mn
        nv_ref[...] = vn
        d_ref[...] = -ADAM_LR * ((mn / c1) / (jnp.sqrt(vn / c2) + ADAM_EPS) + ADAM_WD * w_ref[...])

    spec = pl.BlockSpec((tr, C), lambda i: (i, 0))
    sds = jax.ShapeDtypeStruct((R, C), F32)
    return pl.pallas_call(
        body, name=name, grid=(R // tr,),
        in_specs=[pl.BlockSpec((P, tr, C), lambda i: (0, i, 0)), spec, spec, spec],
        out_specs=[spec, spec, spec, spec], out_shape=[sds, sds, sds, sds],
        compiler_params=_cparams(("parallel",)))(parts, w, m, v)


def _all_gather(shards, *, name):
    n = len(shards)

    def body(*refs):
        ins, outs = refs[:n], refs[n:2 * n]
        send_sems, recv_sems, local_sems = refs[2 * n:]
        x, y, c = lax.axis_index("x"), lax.axis_index("y"), lax.axis_index("c")
        me, sib = (x, y, c), (x, y, 1 - c)
        chips = [(1 - x, y), (x, 1 - y), (1 - x, 1 - y)]

        def slot(p):
            return 4 * p[0] + 2 * p[1] + p[2]

        def cp(a, k, block, to, src=None):
            dst = outs[a].at[slot(block)]
            return pltpu.make_async_remote_copy(src_ref=dst if src is None else src, dst_ref=dst,
                                                send_sem=send_sems.at[a, k], recv_sem=recv_sems.at[a, k],
                                                device_id=to, device_id_type=_MESH)

        mine = [pltpu.make_async_copy(ins[a], outs[a].at[slot(me)], local_sems.at[a]) for a in range(n)]
        for m in mine:
            m.start()
        first = []
        for a in range(n):
            first.append(cp(a, 0, me, sib, src=ins[a]))
            for j, chip in enumerate(chips):
                first.append(cp(a, 1 + j, me, (*chip, c), src=ins[a]))
        for f in first:
            f.start()
        passed = []
        for j, chip in enumerate(chips):
            for a in range(n):
                cp(a, 1 + j, (*chip, c), me).wait_recv()
                f = cp(a, 4 + j, (*chip, c), sib)
                f.start()
                passed.append(f)
        for a in range(n):
            cp(a, 0, sib, me).wait_recv()
            for j, chip in enumerate(chips):
                cp(a, 4 + j, (*chip, 1 - c), me).wait_recv()
        for f in first + passed:
            f.wait_send()
        for m in mine:
            m.wait()

    return pl.pallas_call(
        body, name=name, in_specs=[_ANY] * n, out_specs=[_ANY] * n,
        out_shape=[jax.ShapeDtypeStruct((N_DEV,) + s.shape, s.dtype) for s in shards],
        scratch_shapes=[pltpu.SemaphoreType.DMA((n, 7)), pltpu.SemaphoreType.DMA((n, 7)),
                        pltpu.SemaphoreType.DMA((n,))])(*shards)


def _exchange(blocks, *, name):
    n = len(blocks)

    def body(*refs):
        ins, outs = refs[:n], refs[n:2 * n]
        send_sems, recv_sems, local_sems = refs[2 * n:]
        x, y, c = lax.axis_index("x"), lax.axis_index("y"), lax.axis_index("c")
        me = 4 * x + 2 * y + c
        mine = [pltpu.make_async_copy(ins[a].at[me], outs[a].at[me], local_sems.at[a]) for a in range(n)]
        for m in mine:
            m.start()
        copies = []
        for r in range(1, N_DEV):
            rx, ry, rc = (r >> 2) & 1, (r >> 1) & 1, r & 1
            px, py, pc = (1 - x if rx else x), (1 - y if ry else y), (1 - c if rc else c)
            peer = 4 * px + 2 * py + pc
            for a in range(n):
                copies.append((pltpu.make_async_remote_copy(
                    src_ref=ins[a].at[peer], dst_ref=outs[a].at[me], send_sem=send_sems.at[a, r - 1],
                    recv_sem=recv_sems.at[a, r - 1], device_id=(px, py, pc), device_id_type=_MESH),
                    pltpu.make_async_remote_copy(
                    src_ref=ins[a].at[peer], dst_ref=outs[a].at[peer], send_sem=send_sems.at[a, r - 1],
                    recv_sem=recv_sems.at[a, r - 1], device_id=(px, py, pc), device_id_type=_MESH)))
        for snd, _ in copies:
            snd.start()
        for _, rcv in copies:
            rcv.wait_recv()
        for snd, _ in copies:
            snd.wait_send()
        for m in mine:
            m.wait()

    return pl.pallas_call(
        body, name=name, in_specs=[_ANY] * n, out_specs=[_ANY] * n,
        out_shape=[jax.ShapeDtypeStruct(b.shape, b.dtype) for b in blocks],
        scratch_shapes=[pltpu.SemaphoreType.DMA((n, 7)), pltpu.SemaphoreType.DMA((n, 7)),
                        pltpu.SemaphoreType.DMA((n,))])(*blocks)


def _ffn_fwd(h, nw, w_up, conv_w, conv_b, w_down, tag):
    a = _mm_fwd(h, w_up, norm_w=nw, name=f"ffn{tag}_up", tn=512)
    p = _ffn_conv_fwd(a, conv_w, conv_b.reshape(1, -1), name=f"ffn{tag}_conv")
    h_out = _mm_fwd(p, w_down, residual=h, name=f"ffn{tag}_down", tn=512)
    return h_out, (a, p)


def _ffn_bwd(dh, h, saved, nw, w_up, conv_w, conv_b, w_down, tag):
    a, p = saved
    g_down = _mm_tn(p, dh, name=f"ffn{tag}_down_wg", tk1=1408, tn=1024)
    dp = _mm_nt(dh, w_down, name=f"ffn{tag}_down_dg", out_dtype=BF16, tn=1408, tk=1024)
    dhid, g_cw, g_cb = _ffn_conv_bwd_pre(a, conv_w, conv_b.reshape(1, -1), dp, name=f"ffn{tag}_conv_bwd")
    da = _conv_bwd_in(dhid, conv_w, K=FFN_CONV, name=f"ffn{tag}_conv_bwd_in", tt=256, tc=1408)
    g_up = _mm_tn(h, da, norm_w=nw, name=f"ffn{tag}_up_wg", tn=1408)
    dh_out, g_nw = _mm_nt(da, w_up, epi=(h, nw, dh), name=f"ffn{tag}_up_dg", tk=512)
    return dh_out, dict(norm=g_nw.reshape(-1), up=g_up, conv_w=g_cw, conv_b=g_cb.reshape(-1), down=g_down)


def _local_step(x, tgt, W):
    T = x.shape[0]
    f = {}
    zx = _mm_fwd(x, W["in_w"], norm_w=W["ssm_norm_w"], name="ssm_in", tn=896)
    xbc_c = _ssm_conv_fwd(zx, W["ssm_conv_w"], W["ssm_conv_b"].reshape(1, -1), name="ssm_conv")
    dt_raw = zx[:, D_INNER + CONV_DIM:IN_PROJ_DIM]
    dtg = jnp.pad(dt_raw.reshape(T, SSM_GROUPS, 8).transpose(1, 0, 2), ((0, 0), (0, 0), (0, 120)))
    par = jnp.stack([W["ssm_dt_bias"].reshape(SSM_GROUPS, 8), W["ssm_a_log"].reshape(SSM_GROUPS, 8),
                     W["ssm_d"].reshape(SSM_GROUPS, 8)], axis=1)
    par = jnp.pad(par, ((0, 0), (0, 5), (0, 120)))
    gnw = W["ssm_gate_norm_w"].reshape(1, D_INNER)
    y, yn, st = _ssd_fwd(xbc_c, zx, dtg, par, gnw, name="ssd_fwd")
    h1 = _mm_fwd(yn, W["ssm_out_w"], residual=x, name="ssm_out", tn=512)
    h2, ffn0 = _ffn_fwd(h1, W["ffn_norm_w"][0], W["ffn_up_w"][0], W["ffn_conv_w"][0], W["ffn_conv_b"][0],
                        W["ffn_down_w"][0], "0")
    q = _mm_fwd(h2, W["w_q"], norm_w=W["attn_norm_w"], out_dtype=BF16, name="attn_q", tn=512)
    kv = _mm_fwd(h2, W["w_kv"], norm_w=W["kv_norm_w"], out_dtype=BF16, name="attn_kv", tn=512)
    o, lt = _sba_fwd(q, kv, name="sba_fwd")
    h3 = _mm_fwd(o, W["w_o"], residual=h2, name="attn_o", tn=512)
    h4, ffn1 = _ffn_fwd(h3, W["ffn_norm_w"][1], W["ffn_up_w"][1], W["ffn_conv_w"][1], W["ffn_conv_b"][1],
                        W["ffn_down_w"][1], "1")
    loss, dh4, g_final = _loss_head(h4, tgt, W["final_norm_w"], name="loss_head")
    dh3, gf1 = _ffn_bwd(dh4, h3, ffn1, W["ffn_norm_w"][1], W["ffn_up_w"][1], W["ffn_conv_w"][1], W["ffn_conv_b"][1],
                        W["ffn_down_w"][1], "1")
    g_wo = _mm_tn(o, dh3, name="attn_o_wg", tn=1024)
    do = _mm_nt(dh3, W["w_o"], name="attn_o_dg", out_dtype=BF16, tn=1024, tk=1024)
    dq, dk, dv = _sba_bwd(q, kv, lt, do, name="sba_bwd")
    g_wq = _mm_tn(h2, dq, norm_w=W["attn_norm_w"], name="attn_q_wg", tn=1024)
    dh2a, g_attn_nw = _mm_nt(dq, W["w_q"], epi=(h2, W["attn_norm_w"], dh3), name="attn_q_dg", tk=1024)
    dkv = jnp.concatenate([dk, dv], axis=1)
    g_wkv = _mm_tn(h2, dkv, norm_w=W["kv_norm_w"], name="attn_kv_wg", tn=1024)
    dh2, g_kv_nw = _mm_nt(dkv, W["w_kv"], epi=(h2, W["kv_norm_w"], dh2a), name="attn_kv_dg", tk=1024)
    dh1, gf0 = _ffn_bwd(dh2, h1, ffn0, W["ffn_norm_w"][0], W["ffn_up_w"][0], W["ffn_conv_w"][0], W["ffn_conv_b"][0],
                        W["ffn_down_w"][0], "0")
    g_out = _mm_tn(yn, dh1, name="ssm_out_wg", tn=1024)
    dyn = _mm_nt(dh1, W["ssm_out_w"], name="ssm_out_dg", out_dtype=BF16, tn=1024, tk=1024)
    dxs, dB, dC, dz, ddt, g_gnw, dpar = _ssd_bwd(xbc_c, zx, dtg, par, gnw, y, st, dyn, name="ssd_bwd")
    dxbc_c = jnp.concatenate([dxs, dB, dC], axis=1)
    dhid, g_scw, g_scb = _ssm_conv_bwd_pre(zx, W["ssm_conv_w"], W["ssm_conv_b"].reshape(1, -1), dxbc_c,
                                           name="ssm_conv_bwd")
    dxbc = _conv_bwd_in(dhid, W["ssm_conv_w"], K=SSM_CONV, name="ssm_conv_bwd_in")
    ddt_t = ddt[:, :, :8].transpose(1, 0, 2).reshape(T, SSM_HEADS).astype(BF16)
    dzx = jnp.concatenate([dz, dxbc, jnp.pad(ddt_t, ((0, 0), (0, IN_PROJ_PAD - IN_PROJ_DIM)))], axis=1)
    g_in = _mm_tn(x, dzx, norm_w=W["ssm_norm_w"], name="ssm_in_wg", tn=896)
    dx, g_ssm_nw = _mm_nt(dzx, W["in_w"], epi=(x, W["ssm_norm_w"], dh1), name="ssm_in_dg", tk=896)
    f["ssm_norm_w"] = g_ssm_nw.reshape(-1)
    f["ssm_in_w"] = g_in[:, :IN_PROJ_DIM]
    f["ssm_conv_w"] = g_scw
    f["ssm_conv_b"] = g_scb.reshape(-1)
    f["ssm_dt_bias"] = dpar[:, 0, :8].reshape(-1)
    f["ssm_a_log"] = dpar[:, 1, :8].reshape(-1)
    f["ssm_d"] = dpar[:, 2, :8].reshape(-1)
    f["ssm_gate_norm_w"] = g_gnw.reshape(-1)
    f["ssm_out_w"] = g_out
    f["kv_norm_w"] = g_kv_nw.reshape(-1)
    f["w_k"] = g_wkv[:, :D_MODEL]
    f["w_v"] = g_wkv[:, D_MODEL:]
    f["attn_norm_w"] = g_attn_nw.reshape(-1)
    f["w_q"] = g_wq
    f["w_o"] = g_wo
    f["ffn_norm_w"] = jnp.stack([gf0["norm"], gf1["norm"]])
    f["ffn_up_w"] = [gf0["up"], gf1["up"]]
    f["ffn_conv_w"] = jnp.stack([gf0["conv_w"], gf1["conv_w"]])
    f["ffn_conv_b"] = jnp.stack([gf0["conv_b"], gf1["conv_b"]])
    f["ffn_down_w"] = [gf0["down"], gf1["down"]]
    f["final_norm_w"] = g_final.reshape(-1)
    return loss, dx, f


_BIG = ["ssm_in_w", "ssm_out_w", "w_k", "w_v", "w_q", "w_o", "ffn_up_w", "ffn_down_w"]
_SMALL_SHARDED = ["ssm_norm_w", "ssm_conv_w", "ssm_conv_b", "ssm_gate_norm_w", "ffn_conv_w"]
_SMALL_REPL = ["ssm_dt_bias", "ssm_a_log", "ssm_d", "kv_norm_w", "attn_norm_w", "ffn_norm_w", "ffn_conv_b",
               "final_norm_w"]
_WEIGHTS = ["ssm_norm_w", "ssm_in_w", "ssm_conv_w", "ssm_conv_b", "ssm_dt_bias", "ssm_a_log", "ssm_d",
            "ssm_gate_norm_w", "ssm_out_w", "kv_norm_w", "w_k", "w_v", "attn_norm_w", "w_q", "w_o", "ffn_norm_w",
            "ffn_up_w", "ffn_conv_w", "ffn_conv_b", "ffn_down_w", "final_norm_w"]


def _as2d(a):
    return a.reshape(-1, a.shape[-1])


def _cols_to_full(g):
    return g.transpose(1, 0, 2).reshape(g.shape[1], N_DEV * g.shape[2])


def _full_to_cols(a):
    R = a.shape[0]
    return a.reshape(R, N_DEV, -1).transpose(1, 0, 2)


def _gather_weights(p):
    names = _BIG + _SMALL_SHARDED
    shards = [_as2d(p[n]).astype(BF16) for n in _BIG] + [_as2d(p[n]) for n in _SMALL_SHARDED]
    got = dict(zip(names, _all_gather(shards, name="gather_weights")))
    W = {n: p[n] for n in _SMALL_REPL}
    in_w = _cols_to_full(got["ssm_in_w"])
    W["in_w"] = jnp.pad(in_w, ((0, 0), (0, IN_PROJ_PAD - IN_PROJ_DIM)))
    W["ssm_out_w"] = got["ssm_out_w"].reshape(D_INNER, D_MODEL)
    W["w_kv"] = jnp.concatenate([got["w_k"].reshape(D_MODEL, D_MODEL), got["w_v"].reshape(D_MODEL, D_MODEL)], axis=1)
    W["w_q"] = got["w_q"].reshape(D_MODEL, D_MODEL)
    W["w_o"] = got["w_o"].reshape(D_MODEL, D_MODEL)
    up = got["ffn_up_w"]
    W["ffn_up_w"] = [_cols_to_full(up[:, l * D_MODEL:(l + 1) * D_MODEL]) for l in range(2)]
    dn = got["ffn_down_w"]
    rs = D_FF // N_DEV
    W["ffn_down_w"] = [dn[:, l * rs:(l + 1) * rs].reshape(D_FF, D_MODEL) for l in range(2)]
    W["ssm_norm_w"] = got["ssm_norm_w"].reshape(D_MODEL)
    W["ssm_conv_w"] = _cols_to_full(got["ssm_conv_w"])
    W["ssm_conv_b"] = got["ssm_conv_b"].reshape(CONV_DIM)
    W["ssm_gate_norm_w"] = got["ssm_gate_norm_w"].reshape(D_INNER)
    fcw = _cols_to_full(got["ffn_conv_w"])
    W["ffn_conv_w"] = fcw.reshape(2, FFN_CONV, 2 * D_FF)
    for n in ("ssm_dt_bias", "ssm_a_log", "ssm_d", "attn_norm_w"):
        W[n] = W[n].reshape(-1)
    return W


def _big_grad_blocks(f):
    rs = D_FF // N_DEV
    return {
        "ssm_in_w": _full_to_cols(f["ssm_in_w"]),
        "ssm_out_w": f["ssm_out_w"].reshape(N_DEV, D_INNER // N_DEV, D_MODEL),
        "w_k": f["w_k"].reshape(N_DEV, D_MODEL // N_DEV, D_MODEL),
        "w_v": f["w_v"].reshape(N_DEV, D_MODEL // N_DEV, D_MODEL),
        "w_q": f["w_q"].reshape(N_DEV, D_MODEL // N_DEV, D_MODEL),
        "w_o": f["w_o"].reshape(N_DEV, D_MODEL // N_DEV, D_MODEL),
        "ffn_up_w": jnp.concatenate([_full_to_cols(g) for g in f["ffn_up_w"]], axis=1),
        "ffn_down_w": jnp.concatenate([g.reshape(N_DEV, rs, D_MODEL) for g in f["ffn_down_w"]], axis=1),
    }


def _pack_small(vals):
    flat = jnp.concatenate([v.reshape(-1).astype(F32) for v in vals])
    n = flat.shape[0]
    rows = -(-n // 1024) * 8
    return jnp.pad(flat, (0, rows * 128 - n)).reshape(rows, 128)


def _unpack_small(packed, shapes):
    flat = packed.reshape(-1)
    out, off = [], 0
    for s in shapes:
        n = math.prod(s)
        out.append(flat[off:off + n].reshape(s))
        off += n
    return out


def kernel(x, ssm_norm_w, ssm_in_w, ssm_conv_w, ssm_conv_b, ssm_dt_bias, ssm_a_log, ssm_d, ssm_gate_norm_w, ssm_out_w, kv_norm_w, w_k, w_v, attn_norm_w, w_q, w_o, ffn_norm_w, ffn_up_w, ffn_conv_w, ffn_conv_b, ffn_down_w, final_norm_w, loss_target, m_ssm_norm_w, m_ssm_in_w, m_ssm_conv_w, m_ssm_conv_b, m_ssm_dt_bias, m_ssm_a_log, m_ssm_d, m_ssm_gate_norm_w, m_ssm_out_w, m_kv_norm_w, m_w_k, m_w_v, m_attn_norm_w, m_w_q, m_w_o, m_ffn_norm_w, m_ffn_up_w, m_ffn_conv_w, m_ffn_conv_b, m_ffn_down_w, m_final_norm_w, v_ssm_norm_w, v_ssm_in_w, v_ssm_conv_w, v_ssm_conv_b, v_ssm_dt_bias, v_ssm_a_log, v_ssm_d, v_ssm_gate_norm_w, v_ssm_out_w, v_kv_norm_w, v_w_k, v_w_v, v_attn_norm_w, v_w_q, v_w_o, v_ffn_norm_w, v_ffn_up_w, v_ffn_conv_w, v_ffn_conv_b, v_ffn_down_w, v_final_norm_w):
    env = dict(locals())
    p = {n: env[n] for n in _WEIGHTS}
    mom = {n: env["m_" + n] for n in _WEIGHTS}
    var = {n: env["v_" + n] for n in _WEIGHTS}
    T = x.shape[1]
    me = 4 * lax.axis_index("x") + 2 * lax.axis_index("y") + lax.axis_index("c")

    W = _gather_weights(p)
    loss_row, dx, f = _local_step(x.reshape(T, D_MODEL), loss_target.reshape(T, D_MODEL), W)
    loss = lax.psum(loss_row[0, 0], ("x", "y", "c"))

    big = _big_grad_blocks(f)
    small_names = _SMALL_REPL + _SMALL_SHARDED
    small_full = _pack_small([f[n] for n in small_names])
    small_bcast = jnp.broadcast_to(small_full[None], (N_DEV,) + small_full.shape)
    got = _exchange([big[n] for n in _BIG] + [small_bcast], name="exchange_grads")
    big_parts = dict(zip(_BIG, got[:-1]))

    zero = jnp.zeros_like(small_full)
    g_small_sum = _adamw(got[-1], zero, zero, zero, name="sum_small_grads", tr=small_full.shape[0])[0]
    full_shapes = [f[n].shape for n in small_names]
    g_small = dict(zip(small_names, _unpack_small(g_small_sum, full_shapes)))
    for n in _SMALL_SHARDED:
        width = p[n].shape[-1]
        g_small[n] = lax.dynamic_slice_in_dim(g_small[n], me * width, width, axis=g_small[n].ndim - 1)

    out_g, out_d, out_m, out_v = {}, {}, {}, {}
    for n in _BIG:
        w2, m2, v2 = _as2d(p[n]), _as2d(mom[n]), _as2d(var[n])
        tr = 352 if n == "ffn_down_w" else 256
        g, d, nm, nv = _adamw(big_parts[n], w2, m2, v2, name="adamw_" + n, tr=tr)
        out_g[n], out_d[n], out_m[n], out_v[n] = (t.reshape(p[n].shape) for t in (g, d, nm, nv))
    sw = _pack_small([p[n] for n in small_names])
    sm = _pack_small([mom[n] for n in small_names])
    sv = _pack_small([var[n] for n in small_names])
    sg = _pack_small([g_small[n] for n in small_names])
    _, d, nm, nv = _adamw(sg[None], sw, sm, sv, name="adamw_small", tr=sw.shape[0])
    shard_shapes = [p[n].shape for n in small_names]
    for n, dd, mm, vv in zip(small_names, _unpack_small(d, shard_shapes), _unpack_small(nm, shard_shapes),
                             _unpack_small(nv, shard_shapes)):
        out_g[n] = g_small[n].reshape(p[n].shape)
        out_d[n], out_m[n], out_v[n] = dd, mm, vv

    return (loss, dx.reshape(x.shape), *[out_g[n] for n in _WEIGHTS], *[out_d[n] for n in _WEIGHTS],
            *[out_m[n] for n in _WEIGHTS], *[out_v[n] for n in _WEIGHTS])
```

```python
import functools
import math

import jax
import jax.numpy as jnp
from jax import lax
from jax.experimental import pallas as pl
from jax.experimental.pallas import tpu as pltpu

F32 = jnp.float32
BF16 = jnp.bfloat16
EPS = 1e-6

D_MODEL = 1024
D_INNER = 2048
SSM_HEADS = 32
SSM_GROUPS = 4
SSM_STATE = 128
SSM_CONV = 4
SSM_CHUNK = 128
GN = SSM_GROUPS * SSM_STATE
CONV_DIM = D_INNER + 2 * GN
IN_PROJ_DIM = D_INNER + CONV_DIM + SSM_HEADS
IN_PROJ_PAD = 5376
SB_HEADS = 16
SB_HEAD_DIM = 64
SB_BLOCK = 128
D_FF = 2816
FFN_CONV = 3
N_DEV = 8

ADAM_LR = 0.001
ADAM_B1 = 0.9
ADAM_B2 = 0.999
ADAM_EPS = 1e-08
ADAM_WD = 0.01
ADAM_STEP = 10

_MESH = pl.DeviceIdType.MESH
_NT = (((1,), (1,)), ((), ()))
_TN = (((0,), (0,)), ((), ()))
_ANY = pl.BlockSpec(memory_space=pl.ANY)


def _cparams(sem, vmem_mb=48):
    return pltpu.CompilerParams(dimension_semantics=sem, vmem_limit_bytes=vmem_mb * 1024 * 1024)


def _sigmoid(x):
    return 1.0 / (1.0 + jnp.exp(-x))


def _softplus(x):
    return jnp.maximum(x, 0.0) + jnp.log(1.0 + jnp.exp(-jnp.abs(x)))


def _rms_fwd(xv, w):
    r = lax.rsqrt(jnp.mean(xv * xv, axis=-1, keepdims=True) + EPS)
    return xv * r * w


def _mm_fwd(x, w, *, name, norm_w=None, residual=None, out_dtype=F32, tm=512, tn=512):
    M, K = x.shape
    N = w.shape[1]
    tm, tn = min(tm, M), min(tn, N)
    assert M % tm == 0 and N % tn == 0, (name, M, N, tm, tn)
    has_norm, has_res = norm_w is not None, residual is not None

    def body(*refs):
        x_ref, w_ref = refs[0], refs[1]
        p = 2
        nw_ref = r_ref = None
        if has_norm:
            nw_ref = refs[p]
            p += 1
        if has_res:
            r_ref = refs[p]
            p += 1
        o_ref, xn_ref = refs[p], refs[p + 1]

        @pl.when(pl.program_id(1) == 0)
        def _():
            xv = x_ref[...].astype(F32)
            if has_norm:
                xv = _rms_fwd(xv, nw_ref[...])
            xn_ref[...] = xv.astype(BF16)

        acc = jnp.dot(xn_ref[...], w_ref[...], preferred_element_type=F32)
        if has_res:
            acc = acc + r_ref[...]
        o_ref[...] = acc.astype(out_dtype)

    in_specs = [pl.BlockSpec((tm, K), lambda i, j: (i, 0)), pl.BlockSpec((K, tn), lambda i, j: (0, j))]
    args = [x, w]
    if has_norm:
        in_specs.append(pl.BlockSpec((1, K), lambda i, j: (0, 0)))
        args.append(norm_w.reshape(1, K))
    if has_res:
        in_specs.append(pl.BlockSpec((tm, tn), lambda i, j: (i, j)))
        args.append(residual)
    return pl.pallas_call(
        body, name=name, grid=(M // tm, N // tn), in_specs=in_specs,
        out_specs=pl.BlockSpec((tm, tn), lambda i, j: (i, j)),
        out_shape=jax.ShapeDtypeStruct((M, N), out_dtype),
        scratch_shapes=[pltpu.VMEM((tm, K), BF16)],
        compiler_params=_cparams(("parallel", "arbitrary")))(*args)


def _mm_nt(dy, w, *, name, epi=None, out_dtype=F32, tm=512, tn=512, tk=512):
    M, K = dy.shape
    N = w.shape[0]
    tm, tk = min(tm, M), min(tk, K)
    tn = N if epi is not None else min(tn, N)
    assert M % tm == 0 and N % tn == 0 and K % tk == 0, (name, M, N, K, tm, tn, tk)
    nk = K // tk
    has_epi = epi is not None

    def body(*refs):
        if has_epi:
            dy_ref, w_ref, h_ref, nw_ref, r_ref, o_ref, dnw_ref, acc_ref = refs
        else:
            dy_ref, w_ref, o_ref, acc_ref = refs
        i = pl.program_id(0)
        k = pl.program_id(2)

        @pl.when(k == 0)
        def _():
            acc_ref[...] = jnp.zeros_like(acc_ref)

        acc_ref[...] += lax.dot_general(dy_ref[...].astype(BF16), w_ref[...], _NT, preferred_element_type=F32)

        @pl.when(k == nk - 1)
        def _():
            du = acc_ref[...]
            if has_epi:
                hv = h_ref[...]
                r = lax.rsqrt(jnp.mean(hv * hv, axis=-1, keepdims=True) + EPS)
                xhat = hv * r
                dxh = du * nw_ref[...]
                dx = r * (dxh - xhat * jnp.mean(dxh * xhat, axis=-1, keepdims=True))
                o_ref[...] = (r_ref[...] + dx).astype(out_dtype)
                contrib = jnp.sum(du * xhat, axis=0, keepdims=True)

                @pl.when(i == 0)
                def _():
                    dnw_ref[...] = contrib

                @pl.when(i > 0)
                def _():
                    dnw_ref[...] += contrib
            else:
                o_ref[...] = du.astype(out_dtype)

    in_specs = [pl.BlockSpec((tm, tk), lambda i, j, k: (i, k)), pl.BlockSpec((tn, tk), lambda i, j, k: (j, k))]
    args = [dy, w]
    out_specs = [pl.BlockSpec((tm, tn), lambda i, j, k: (i, j))]
    out_shape = [jax.ShapeDtypeStruct((M, N), out_dtype)]
    if has_epi:
        h, nw, res = epi
        in_specs += [pl.BlockSpec((tm, N), lambda i, j, k: (i, 0)), pl.BlockSpec((1, N), lambda i, j, k: (0, 0)),
                     pl.BlockSpec((tm, N), lambda i, j, k: (i, 0))]
        args += [h, nw.reshape(1, N), res]
        out_specs.append(pl.BlockSpec((1, N), lambda i, j, k: (0, 0)))
        out_shape.append(jax.ShapeDtypeStruct((1, N), F32))
    outs = pl.pallas_call(
        body, name=name, grid=(M // tm, N // tn, nk), in_specs=in_specs, out_specs=out_specs, out_shape=out_shape,
        scratch_shapes=[pltpu.VMEM((tm, tn), F32)],
        compiler_params=_cparams(("arbitrary", "arbitrary", "arbitrary")))(*args)
    return (outs[0], outs[1]) if has_epi else outs[0]


def _mm_tn(x, dy, *, name, norm_w=None, out_dtype=BF16, tk1=1024, tn=512, tt=512):
    T, K1 = x.shape
    N = dy.shape[1]
    tk1, tn, tt = min(tk1, K1), min(tn, N), min(tt, T)
    has_norm = norm_w is not None
    assert K1 % tk1 == 0 and N % tn == 0 and T % tt == 0, (name, K1, N, T, tk1, tn, tt)
    assert not has_norm or tk1 == K1
    nt = T // tt

    def body(*refs):
        if has_norm:
            x_ref, dy_ref, nw_ref, o_ref, acc_ref = refs
        else:
            x_ref, dy_ref, o_ref, acc_ref = refs
        t = pl.program_id(2)

        @pl.when(t == 0)
        def _():
            acc_ref[...] = jnp.zeros_like(acc_ref)

        xv = x_ref[...]
        if has_norm:
            xv = _rms_fwd(xv.astype(F32), nw_ref[...])
        acc_ref[...] += lax.dot_general(xv.astype(BF16), dy_ref[...].astype(BF16), _TN, preferred_element_type=F32)

        @pl.when(t == nt - 1)
        def _():
            o_ref[...] = acc_ref[...].astype(out_dtype)

    in_specs = [pl.BlockSpec((tt, tk1), lambda a, b, t: (t, a)), pl.BlockSpec((tt, tn), lambda a, b, t: (t, b))]
    args = [x, dy]
    if has_norm:
        in_specs.append(pl.BlockSpec((1, K1), lambda a, b, t: (0, 0)))
        args.append(norm_w.reshape(1, K1))
    return pl.pallas_call(
        body, name=name, grid=(K1 // tk1, N // tn, nt), in_specs=in_specs,
        out_specs=pl.BlockSpec((tk1, tn), lambda a, b, t: (a, b)),
        out_shape=jax.ShapeDtypeStruct((K1, N), out_dtype),
        scratch_shapes=[pltpu.VMEM((tk1, tn), F32)],
        compiler_params=_cparams(("parallel", "parallel", "arbitrary")))(*args)


def _shift_down(xb, prev8, j):
    main = pltpu.roll(xb, j, 0)
    head = pltpu.roll(xb[0:8], j, 0)
    ph = pltpu.roll(prev8, j, 0)
    row8 = lax.broadcasted_iota(jnp.int32, head.shape, 0)
    head = jnp.where(row8 < j, ph, head)
    return jnp.concatenate([head, main[8:]], axis=0)


def _shift_up(xb, next8, j):
    tt = xb.shape[0]
    main = pltpu.roll(xb, tt - j, 0)
    tail = pltpu.roll(xb[tt - 8:tt], 8 - j, 0)
    nh = pltpu.roll(next8, 8 - j, 0)
    row8 = lax.broadcasted_iota(jnp.int32, tail.shape, 0)
    tail = jnp.where(row8 + j >= 8, nh, tail)
    return jnp.concatenate([main[:tt - 8], tail], axis=0)


def _conv_hid(xb, prev8, w, b_row, K):
    out = b_row
    shifted = []
    for j in range(K):
        sh = K - 1 - j
        xs = xb if sh == 0 else _shift_down(xb, prev8, sh)
        shifted.append(xs)
        out = out + xs * w[j:j + 1, :]
    return out, shifted


def _prev_idx(i, nb8):
    return jnp.maximum(i * nb8 - 1, 0)


def _ssm_conv_fwd(zx, w, b, *, name, tt=512, tc=512):
    T = zx.shape[0]
    tt = min(tt, T)
    C, K = CONV_DIM, SSM_CONV
    cb0, nb8 = D_INNER // tc, tt // 8

    def body(x_ref, p_ref, w_ref, b_ref, o_ref):
        first = (pl.program_id(1) > 0).astype(F32)
        hid, _ = _conv_hid(x_ref[...], p_ref[...] * first, w_ref[...], b_ref[...], K)
        o_ref[...] = hid * _sigmoid(hid)

    return pl.pallas_call(
        body, name=name, grid=(C // tc, T // tt),
        in_specs=[pl.BlockSpec((tt, tc), lambda c, i: (i, c + cb0)),
                  pl.BlockSpec((8, tc), lambda c, i: (_prev_idx(i, nb8), c + cb0)),
                  pl.BlockSpec((K, tc), lambda c, i: (0, c)), pl.BlockSpec((1, tc), lambda c, i: (0, c))],
        out_specs=pl.BlockSpec((tt, tc), lambda c, i: (i, c)),
        out_shape=jax.ShapeDtypeStruct((T, C), F32),
        compiler_params=_cparams(("parallel", "parallel")))(zx, zx, w, b)


def _ssm_conv_bwd_pre(zx, w, b, dout, *, name, tt=512, tc=512):
    T = zx.shape[0]
    tt = min(tt, T)
    C, K = CONV_DIM, SSM_CONV
    cb0, nb8 = D_INNER // tc, tt // 8

    def body(x_ref, p_ref, w_ref, b_ref, d_ref, dh_ref, dw_ref, db_ref):
        t = pl.program_id(1)
        first = (t > 0).astype(F32)
        hid, shifted = _conv_hid(x_ref[...], p_ref[...] * first, w_ref[...], b_ref[...], K)
        sg = _sigmoid(hid)
        dh = d_ref[...] * (sg * (1.0 + hid * (1.0 - sg)))
        dh_ref[...] = dh

        @pl.when(t == 0)
        def _():
            dw_ref[...] = jnp.zeros_like(dw_ref)
            db_ref[...] = jnp.zeros_like(db_ref)

        db_ref[...] += jnp.sum(dh, axis=0, keepdims=True)
        for j in range(K):
            dw_ref[j:j + 1, :] += jnp.sum(dh * shifted[j], axis=0, keepdims=True)

    return pl.pallas_call(
        body, name=name, grid=(C // tc, T // tt),
        in_specs=[pl.BlockSpec((tt, tc), lambda c, i: (i, c + cb0)),
                  pl.BlockSpec((8, tc), lambda c, i: (_prev_idx(i, nb8), c + cb0)),
                  pl.BlockSpec((K, tc), lambda c, i: (0, c)), pl.BlockSpec((1, tc), lambda c, i: (0, c)),
                  pl.BlockSpec((tt, tc), lambda c, i: (i, c))],
        out_specs=[pl.BlockSpec((tt, tc), lambda c, i: (i, c)), pl.BlockSpec((K, tc), lambda c, i: (0, c)),
                   pl.BlockSpec((1, tc), lambda c, i: (0, c))],
        out_shape=[jax.ShapeDtypeStruct((T, C), F32), jax.ShapeDtypeStruct((K, C), F32),
                   jax.ShapeDtypeStruct((1, C), F32)],
        compiler_params=_cparams(("parallel", "arbitrary")))(zx, zx, w, b, dout)


def _conv_bwd_in(dh, w, *, name, K, tt=512, tc=512, out_dtype=BF16):
    T, C = dh.shape
    tt = min(tt, T)
    nb8, nT = tt // 8, T // tt
    last8 = T // 8 - 1

    def body(d_ref, n_ref, w_ref, o_ref):
        notlast = (pl.program_id(1) < nT - 1).astype(F32)
        d = d_ref[...]
        nxt = n_ref[...] * notlast
        w_ = w_ref[...]
        acc = d * w_[K - 1:K, :]
        for sh in range(1, K):
            acc = acc + _shift_up(d, nxt, sh) * w_[K - 1 - sh:K - sh, :]
        o_ref[...] = acc.astype(out_dtype)

    return pl.pallas_call(
        body, name=name, grid=(C // tc, nT),
        in_specs=[pl.BlockSpec((tt, tc), lambda c, i: (i, c)),
                  pl.BlockSpec((8, tc), lambda c, i: (jnp.minimum((i + 1) * nb8, last8), c)),
                  pl.BlockSpec((K, tc), lambda c, i: (0, c))],
        out_specs=pl.BlockSpec((tt, tc), lambda c, i: (i, c)),
        out_shape=jax.ShapeDtypeStruct((T, C), out_dtype),
        compiler_params=_cparams(("parallel", "parallel")))(dh, dh, w)


def _ffn_conv_fwd(a, w, b, *, name, tt=256, tc=1408):
    T = a.shape[0]
    tt = min(tt, T)
    K, nbh, nb8 = FFN_CONV, D_FF // tc, tt // 8

    def body(ag_ref, pg_ref, av_ref, pv_ref, wg_ref, wv_ref, bg_ref, bv_ref, o_ref):
        first = (pl.program_id(1) > 0).astype(F32)
        hg, _ = _conv_hid(ag_ref[...], pg_ref[...] * first, wg_ref[...], bg_ref[...], K)
        hv, _ = _conv_hid(av_ref[...], pv_ref[...] * first, wv_ref[...], bv_ref[...], K)
        o_ref[...] = (hg * _sigmoid(hg) * hv).astype(BF16)

    return pl.pallas_call(
        body, name=name, grid=(nbh, T // tt),
        in_specs=[pl.BlockSpec((tt, tc), lambda c, i: (i, c)),
                  pl.BlockSpec((8, tc), lambda c, i: (_prev_idx(i, nb8), c)),
                  pl.BlockSpec((tt, tc), lambda c, i: (i, c + nbh)),
                  pl.BlockSpec((8, tc), lambda c, i: (_prev_idx(i, nb8), c + nbh)),
                  pl.BlockSpec((K, tc), lambda c, i: (0, c)), pl.BlockSpec((K, tc), lambda c, i: (0, c + nbh)),
                  pl.BlockSpec((1, tc), lambda c, i: (0, c)), pl.BlockSpec((1, tc), lambda c, i: (0, c + nbh))],
        out_specs=pl.BlockSpec((tt, tc), lambda c, i: (i, c)),
        out_shape=jax.ShapeDtypeStruct((T, D_FF), BF16),
        compiler_params=_cparams(("parallel", "parallel")))(a, a, a, a, w, w, b, b)


def _ffn_conv_bwd_pre(a, w, b, dp, *, name, tt=256, tc=1408):
    T = a.shape[0]
    tt = min(tt, T)
    K, nbh, nb8 = FFN_CONV, D_FF // tc, tt // 8

    def body(ao_ref, po_ref, ag_ref, pg_ref, av_ref, pv_ref, wg_ref, wv_ref, bg_ref, bv_ref, dp_ref,
             dh_ref, dw_ref, db_ref):
        j = pl.program_id(0)
        t = pl.program_id(1)
        first = (t > 0).astype(F32)
        hg, _ = _conv_hid(ag_ref[...], pg_ref[...] * first, wg_ref[...], bg_ref[...], K)
        hv, _ = _conv_hid(av_ref[...], pv_ref[...] * first, wv_ref[...], bv_ref[...], K)
        sg = _sigmoid(hg)
        d = dp_ref[...].astype(F32)
        is_gate = (j < nbh).astype(F32)
        dh = d * (is_gate * (hv * (sg * (1.0 + hg * (1.0 - sg)))) + (1.0 - is_gate) * (hg * sg))
        dh_ref[...] = dh
        xo = ao_ref[...]
        po = po_ref[...] * first

        @pl.when(t == 0)
        def _():
            dw_ref[...] = jnp.zeros_like(dw_ref)
            db_ref[...] = jnp.zeros_like(db_ref)

        db_ref[...] += jnp.sum(dh, axis=0, keepdims=True)
        for jj in range(K):
            sh = K - 1 - jj
            xs = xo if sh == 0 else _shift_down(xo, po, sh)
            dw_ref[jj:jj + 1, :] += jnp.sum(dh * xs, axis=0, keepdims=True)

    def gi(c):
        return lax.rem(c, nbh)

    return pl.pallas_call(
        body, name=name, grid=(2 * nbh, T // tt),
        in_specs=[pl.BlockSpec((tt, tc), lambda c, i: (i, c)),
                  pl.BlockSpec((8, tc), lambda c, i: (_prev_idx(i, nb8), c)),
                  pl.BlockSpec((tt, tc), lambda c, i: (i, gi(c))),
                  pl.BlockSpec((8, tc), lambda c, i: (_prev_idx(i, nb8), gi(c))),
                  pl.BlockSpec((tt, tc), lambda c, i: (i, gi(c) + nbh)),
                  pl.BlockSpec((8, tc), lambda c, i: (_prev_idx(i, nb8), gi(c) + nbh)),
                  pl.BlockSpec((K, tc), lambda c, i: (0, gi(c))), pl.BlockSpec((K, tc), lambda c, i: (0, gi(c) + nbh)),
                  pl.BlockSpec((1, tc), lambda c, i: (0, gi(c))), pl.BlockSpec((1, tc), lambda c, i: (0, gi(c) + nbh)),
                  pl.BlockSpec((tt, tc), lambda c, i: (i, gi(c)))],
        out_specs=[pl.BlockSpec((tt, tc), lambda c, i: (i, c)), pl.BlockSpec((K, tc), lambda c, i: (0, c)),
                   pl.BlockSpec((1, tc), lambda c, i: (0, c))],
        out_shape=[jax.ShapeDtypeStruct((T, 2 * D_FF), F32), jax.ShapeDtypeStruct((K, 2 * D_FF), F32),
                   jax.ShapeDtypeStruct((1, 2 * D_FF), F32)],
        compiler_params=_cparams(("parallel", "arbitrary")))(a, a, a, a, a, a, w, w, b, b, dp)


def _cumsum_rows(x):
    L = x.shape[0]
    row = lax.broadcasted_iota(jnp.int32, x.shape, 0)
    k = 1
    while k < L:
        x = x + jnp.where(row >= k, pltpu.roll(x, k, 0), 0.0)
        k *= 2
    return x


def _rcumsum_rows(x):
    L = x.shape[0]
    row = lax.broadcasted_iota(jnp.int32, x.shape, 0)
    k = 1
    while k < L:
        x = x + jnp.where(row < L - k, pltpu.roll(x, L - k, 0), 0.0)
        k *= 2
    return x


def _ssd_common(dt_ref, par_ref):
    par = par_ref[...]
    raw = dt_ref[...] + par[0:1, :]
    dt = _softplus(raw)
    a = -jnp.exp(par[1:2, :])
    cs = _cumsum_rows(dt * a)
    L = cs.shape[0]
    cs_last = cs[L - 1:L, :]
    return raw, dt, a, par[2:3, :], cs, cs.T, jnp.exp(cs), jnp.exp(cs_last - cs), jnp.exp(cs_last)


def _ssd_specs(nc, rev):
    L = SSM_CHUNK

    def ci(c):
        return nc - 1 - c if rev else c

    return [pl.BlockSpec((L, 512), lambda g, c: (ci(c), g)),
            pl.BlockSpec((L, 128), lambda g, c: (ci(c), 16 + g)),
            pl.BlockSpec((L, 128), lambda g, c: (ci(c), 20 + g)),
            pl.BlockSpec((None, L, 128), lambda g, c: (g, ci(c), 0)),
            pl.BlockSpec((None, 8, 128), lambda g, c: (g, 0, 0)),
            pl.BlockSpec((L, 512), lambda g, c: (ci(c), g)),
            pl.BlockSpec((1, 512), lambda g, c: (0, g))], ci


def _ssd_fwd(xbc_c, zx, dtg, par, gnw, *, name):
    T = xbc_c.shape[0]
    L = SSM_CHUNK
    nc = T // L
    in_specs, ci = _ssd_specs(nc, False)

    def body(xs_ref, b_ref, c_ref, dt_ref, par_ref, z_ref, gnw_ref, y_ref, yn_ref, st_ref, h_ref):
        @pl.when(pl.program_id(1) == 0)
        def _():
            h_ref[...] = jnp.zeros_like(h_ref)

        _, dt, _, dsk, cs, csT, ecs, eend, dec = _ssd_common(dt_ref, par_ref)
        Bb = b_ref[...].astype(BF16)
        Cb = c_ref[...].astype(BF16)
        G = lax.dot_general(Cb, Bb, _NT, preferred_element_type=F32)
        row = lax.broadcasted_iota(jnp.int32, (L, L), 0)
        col = lax.broadcasted_iota(jnp.int32, (L, L), 1)
        tril = col <= row
        lo = lax.broadcasted_iota(jnp.int32, (L, 128), 1) < 64
        lo1 = lax.broadcasted_iota(jnp.int32, (1, 128), 1) < 64
        for pp in range(4):
            hA, hB = 2 * pp, 2 * pp + 1

            def sel(m):
                return jnp.where(lo, m[:, hA:hA + 1], m[:, hB:hB + 1])

            def sel1(m):
                return jnp.where(lo1, m[:, hA:hA + 1], m[:, hB:hB + 1])

            X = xs_ref[:, pp * 128:(pp + 1) * 128]
            xd = X * sel(dt)
            xdb = xd.astype(BF16)
            ys = []
            for h in (hA, hB):
                Lm = jnp.where(tril, jnp.exp(jnp.minimum(cs[:, h:h + 1] - csT[h:h + 1, :], 0.0)), 0.0)
                ys.append(jnp.dot((G * Lm).astype(BF16), xdb, preferred_element_type=F32))
            Hp = h_ref[pp]
            st_ref[pp] = Hp
            yoff = jnp.dot(Cb, Hp.astype(BF16), preferred_element_type=F32) * sel(ecs)
            y_ref[:, pp * 128:(pp + 1) * 128] = jnp.where(lo, ys[0], ys[1]) + yoff + sel1(dsk) * X
            S = lax.dot_general(Bb, (xd * sel(eend)).astype(BF16), _TN, preferred_element_type=F32)
            h_ref[pp] = Hp * sel1(dec) + S
        zv = z_ref[...]
        yg = y_ref[...] * (zv * _sigmoid(zv))
        yn_ref[...] = _rms_fwd(yg, gnw_ref[...]).astype(BF16)

    return pl.pallas_call(
        body, name=name, grid=(SSM_GROUPS, nc), in_specs=in_specs,
        out_specs=[pl.BlockSpec((L, 512), lambda g, c: (c, g)), pl.BlockSpec((L, 512), lambda g, c: (c, g)),
                   pl.BlockSpec((None, None, 4, 128, 128), lambda g, c: (g, c, 0, 0, 0))],
        out_shape=[jax.ShapeDtypeStruct((T, D_INNER), F32), jax.ShapeDtypeStruct((T, D_INNER), BF16),
                   jax.ShapeDtypeStruct((SSM_GROUPS, nc, 4, 128, 128), F32)],
        scratch_shapes=[pltpu.VMEM((4, 128, 128), F32)],
        compiler_params=_cparams(("parallel", "arbitrary")))(xbc_c, xbc_c, xbc_c, dtg, par, zx, gnw)


def _ssd_bwd(xbc_c, zx, dtg, par, gnw, y, st, dyn, *, name):
    T = xbc_c.shape[0]
    L = SSM_CHUNK
    nc = T // L
    in_specs, ci = _ssd_specs(nc, True)
    in_specs += [pl.BlockSpec((L, 512), lambda g, c: (ci(c), g)),
                 pl.BlockSpec((None, None, 4, 128, 128), lambda g, c: (g, ci(c), 0, 0, 0)),
                 pl.BlockSpec((L, 512), lambda g, c: (ci(c), g))]

    def body(xs_ref, b_ref, c_ref, dt_ref, par_ref, z_ref, gnw_ref, y_ref, st_ref, dyn_ref,
             dxs_ref, db_ref, dc_ref, dz_ref, ddt_ref, dgnw_ref, dpar_ref, dh_ref):
        @pl.when(pl.program_id(1) == 0)
        def _():
            dh_ref[...] = jnp.zeros_like(dh_ref)
            dgnw_ref[...] = jnp.zeros_like(dgnw_ref)
            dpar_ref[...] = jnp.zeros_like(dpar_ref)

        yv = y_ref[...]
        zv = z_ref[...]
        sg = _sigmoid(zv)
        sz = zv * sg
        yg = yv * sz
        r = lax.rsqrt(jnp.mean(yg * yg, axis=-1, keepdims=True) + EPS)
        yh = yg * r
        dyn = dyn_ref[...].astype(F32)
        dgnw_ref[...] += jnp.sum(dyn * yh, axis=0, keepdims=True)
        dyh = dyn * gnw_ref[...]
        dyg = r * (dyh - yh * jnp.mean(dyh * yh, axis=-1, keepdims=True))
        dY_all = dyg * sz
        dz_ref[...] = (dyg * yv * (sg * (1.0 + zv * (1.0 - sg)))).astype(dz_ref.dtype)

        raw, dt, a, dsk, cs, csT, ecs, eend, dec = _ssd_common(dt_ref, par_ref)
        Bb = b_ref[...].astype(BF16)
        Cb = c_ref[...].astype(BF16)
        G = lax.dot_general(Cb, Bb, _NT, preferred_element_type=F32)
        row = lax.broadcasted_iota(jnp.int32, (L, L), 0)
        col = lax.broadcasted_iota(jnp.int32, (L, L), 1)
        tril = col <= row
        lane = lax.broadcasted_iota(jnp.int32, (L, 128), 1)
        lo = lane < 64
        lane1 = lax.broadcasted_iota(jnp.int32, (1, 128), 1)
        lo1 = lane1 < 64
        rowc = lax.broadcasted_iota(jnp.int32, (L, 1), 0)
        dG = jnp.zeros((L, L), F32)
        dB = jnp.zeros((L, SSM_STATE), F32)
        dC = jnp.zeros((L, SSM_STATE), F32)
        dcs_mat = jnp.zeros((L, 128), F32)
        dcs_t = jnp.zeros((L, L), F32)
        ddt_mat = jnp.zeros((L, 128), F32)
        dD_row = jnp.zeros((1, 128), F32)

        def tot(m):
            return jnp.sum(jnp.sum(m, axis=1, keepdims=True), axis=0, keepdims=True)

        for pp in range(4):
            hA, hB = 2 * pp, 2 * pp + 1

            def sel(m):
                return jnp.where(lo, m[:, hA:hA + 1], m[:, hB:hB + 1])

            def sel1(m):
                return jnp.where(lo1, m[:, hA:hA + 1], m[:, hB:hB + 1])

            X = xs_ref[:, pp * 128:(pp + 1) * 128]
            dY = dY_all[:, pp * 128:(pp + 1) * 128]
            dtsel = sel(dt)
            xd = X * dtsel
            xdb = xd.astype(BF16)
            dYb = dY.astype(BF16)
            Hp = st_ref[pp]
            Hb = Hp.astype(BF16)
            dHn = dh_ref[pp]
            dHb = dHn.astype(BF16)
            ecs_sel = sel(ecs)
            eend_sel = sel(eend)
            dxd_state = jnp.dot(Bb, dHb, preferred_element_type=F32) * eend_sel
            yoff = jnp.dot(Cb, Hb, preferred_element_type=F32) * ecs_sel
            dYe = (dY * ecs_sel).astype(BF16)
            dC = dC + lax.dot_general(dYe, Hb, _NT, preferred_element_type=F32)
            dB = dB + lax.dot_general((xd * eend_sel).astype(BF16), dHb, _NT, preferred_element_type=F32)
            dh_ref[pp] = dHn * sel1(dec) + lax.dot_general(Cb, dYe, _TN, preferred_element_type=F32)
            q = xd * dxd_state
            dyoff = dY * yoff
            hh = dHn * Hp
            dxd_diag = []
            for h, msk, msk1 in ((hA, lo, lo1), (hB, jnp.logical_not(lo), jnp.logical_not(lo1))):
                Lm = jnp.where(tril, jnp.exp(jnp.minimum(cs[:, h:h + 1] - csT[h:h + 1, :], 0.0)), 0.0)
                M = G * Lm
                dxd_diag.append(lax.dot_general(M.astype(BF16), dYb, _TN, preferred_element_type=F32))
                dM = lax.dot_general(jnp.where(msk, dY, 0.0).astype(BF16), xdb, _NT, preferred_element_type=F32)
                dG = dG + dM * Lm
                W = dM * M
                dcs_h = jnp.sum(W, axis=1, keepdims=True)
                dcs_t = dcs_t + jnp.where(row == h, jnp.sum(W, axis=0, keepdims=True), 0.0)
                dcs_h = dcs_h + jnp.sum(jnp.where(msk, dyoff - q, 0.0), axis=1, keepdims=True)
                tail = tot(jnp.where(msk, q, 0.0)) + dec[:, h:h + 1] * tot(jnp.where(msk, hh, 0.0))
                dcs_h = dcs_h + jnp.where(rowc == L - 1, tail, 0.0)
                dcs_mat = dcs_mat + jnp.where(lane == h, dcs_h, 0.0)
            dxd = jnp.where(lo, dxd_diag[0], dxd_diag[1]) + dxd_state
            prod = dxd * X
            dA_ = jnp.sum(jnp.where(lo, prod, 0.0), axis=1, keepdims=True)
            dB_ = jnp.sum(prod, axis=1, keepdims=True) - dA_
            ddt_mat = ddt_mat + jnp.where(lane == hA, dA_, 0.0) + jnp.where(lane == hB, dB_, 0.0)
            dxs_ref[:, pp * 128:(pp + 1) * 128] = dxd * dtsel + sel1(dsk) * dY
            dyx = jnp.sum(dY * X, axis=0, keepdims=True)
            sA = jnp.sum(jnp.where(lo1, dyx, 0.0), axis=1, keepdims=True)
            sB = jnp.sum(dyx, axis=1, keepdims=True) - sA
            dD_row = dD_row + jnp.where(lane1 == hA, sA, 0.0) + jnp.where(lane1 == hB, sB, 0.0)
        dGb = dG.astype(BF16)
        db_ref[...] = dB + lax.dot_general(dGb, Cb, _TN, preferred_element_type=F32)
        dc_ref[...] = dC + jnp.dot(dGb, Bb, preferred_element_type=F32)
        dad = _rcumsum_rows(dcs_mat - dcs_t.T)
        draw = (a * dad + ddt_mat) * _sigmoid(raw)
        ddt_ref[...] = draw
        dpar_ref[0:1, :] += jnp.sum(draw, axis=0, keepdims=True)
        dpar_ref[1:2, :] += jnp.sum(dt * dad, axis=0, keepdims=True) * a
        dpar_ref[2:3, :] += dD_row

    return pl.pallas_call(
        body, name=name, grid=(SSM_GROUPS, nc), in_specs=in_specs,
        out_specs=[pl.BlockSpec((L, 512), lambda g, c: (ci(c), g)),
                   pl.BlockSpec((L, 128), lambda g, c: (ci(c), g)),
                   pl.BlockSpec((L, 128), lambda g, c: (ci(c), g)),
                   pl.BlockSpec((L, 512), lambda g, c: (ci(c), g)),
                   pl.BlockSpec((None, L, 128), lambda g, c: (g, ci(c), 0)),
                   pl.BlockSpec((1, 512), lambda g, c: (0, g)),
                   pl.BlockSpec((None, 8, 128), lambda g, c: (g, 0, 0))],
        out_shape=[jax.ShapeDtypeStruct((T, D_INNER), F32), jax.ShapeDtypeStruct((T, GN), F32),
                   jax.ShapeDtypeStruct((T, GN), F32), jax.ShapeDtypeStruct((T, D_INNER), BF16),
                   jax.ShapeDtypeStruct((SSM_GROUPS, T, 128), F32), jax.ShapeDtypeStruct((1, D_INNER), F32),
                   jax.ShapeDtypeStruct((SSM_GROUPS, 8, 128), F32)],
        scratch_shapes=[pltpu.VMEM((4, 128, 128), F32)],
        compiler_params=_cparams(("parallel", "arbitrary")))(xbc_c, xbc_c, xbc_c, dtg, par, zx, gnw, y, st, dyn)


SB_KEYS = 512


def _hi_lo(v):
    hi = v.astype(BF16)
    return jnp.concatenate([hi, (v - hi.astype(F32)).astype(BF16)], axis=1)


def _tri2(cond):
    kk = lax.broadcasted_iota(jnp.int32, (2 * SB_BLOCK, SB_BLOCK), 0)
    kk = jnp.where(kk >= SB_BLOCK, kk - SB_BLOCK, kk)
    jj = lax.broadcasted_iota(jnp.int32, (2 * SB_BLOCK, SB_BLOCK), 1)
    return cond(kk, jj).astype(BF16)


def _sba_diag_mask():
    Bq = SB_BLOCK
    rowi = lax.broadcasted_iota(jnp.int32, (2 * Bq, Bq), 0)
    return lax.broadcasted_iota(jnp.int32, (2 * Bq, Bq), 1) < jnp.where(rowi >= Bq, rowi - Bq, rowi)


def _sba_sub_fwd(zb, c, U2, mask):
    s = _softplus(zb)
    l = -s if mask is None else jnp.where(mask, -s, 0.0)
    A = jnp.exp(zb - s + (c + jnp.dot(_hi_lo(l), U2, preferred_element_type=F32)))
    if mask is not None:
        A = jnp.where(mask, A, 0.0)
    return A.astype(BF16), c + jnp.sum(l, axis=1, keepdims=True)


def _sba_sub_bwd(zb, dAb, Lt, pc, pe, Uincl, Uexcl, mask):
    s = _softplus(zb)
    l = -s if mask is None else jnp.where(mask, -s, 0.0)
    P = pc + jnp.dot(_hi_lo(l), Uincl, preferred_element_type=F32)
    g = zb - s
    A = jnp.exp(g + (Lt - P))
    if mask is not None:
        A = jnp.where(mask, A, 0.0)
    E = dAb * A
    PE = pe + jnp.dot(_hi_lo(E), Uexcl, preferred_element_type=F32)
    dz = E - jnp.exp(g) * (E + PE)
    if mask is not None:
        dz = jnp.where(mask, dz, 0.0)
    return (A.astype(BF16), dz.astype(BF16), pc + jnp.sum(l, axis=1, keepdims=True),
            pe + jnp.sum(E, axis=1, keepdims=True))


def _stack_heads(v):
    lo = lax.broadcasted_iota(jnp.int32, v.shape, 1) < 64
    zero = jnp.zeros_like(v)
    return jnp.concatenate([jnp.where(lo, v, zero), jnp.where(lo, zero, v)], axis=0)


def _unstack_heads(v):
    lo = lax.broadcasted_iota(jnp.int32, (SB_BLOCK, 128), 1) < 64
    return jnp.where(lo, v[:SB_BLOCK], v[SB_BLOCK:])


def _sba_fwd(q, kv, *, name):
    T = q.shape[0]
    Bq = SB_BLOCK
    nsub = SB_KEYS // Bq
    assert T % SB_KEYS == 0
    scale = 1.0 / math.sqrt(SB_HEAD_DIM)

    def body(q_ref, k_ref, v_ref, o_ref, lt_ref):
        I = pl.program_id(1)
        U2 = _tri2(lambda k, j: k > j)
        dmask = _sba_diag_mask()
        qs = [_stack_heads(q_ref[a * Bq:(a + 1) * Bq, :] * scale) for a in range(nsub)]
        cs, accs = [], []
        for a in range(nsub):
            c = jnp.zeros((2 * Bq, 1), F32)
            acc = jnp.zeros((2 * Bq, 128), F32)
            for b in range(a, -1, -1):
                off = pl.multiple_of(I * SB_KEYS + b * Bq, Bq)
                zb = lax.dot_general(qs[a], k_ref[pl.ds(off, Bq), :], _NT, preferred_element_type=F32)
                A, c = _sba_sub_fwd(zb, c, U2, dmask if b == a else None)
                acc = acc + jnp.dot(A, v_ref[pl.ds(off, Bq), :], preferred_element_type=F32)
            cs.append(c)
            accs.append(acc)
        qs_all = jnp.concatenate(qs, axis=0)

        def step(n, carry):
            c, acc = carry
            off = pl.multiple_of((I - 1 - n) * SB_KEYS, SB_KEYS)
            z = lax.dot_general(qs_all, k_ref[pl.ds(off, SB_KEYS), :], _NT, preferred_element_type=F32)
            parts = [None] * nsub
            for b in reversed(range(nsub)):
                parts[b], c = _sba_sub_fwd(z[:, b * Bq:(b + 1) * Bq], c, U2, None)
            return c, acc + jnp.dot(jnp.concatenate(parts, axis=1), v_ref[pl.ds(off, SB_KEYS), :],
                                    preferred_element_type=F32)

        c, acc = lax.fori_loop(0, I, step, (jnp.concatenate(cs, axis=0), jnp.concatenate(accs, axis=0)))
        for a in range(nsub):
            rows = slice(2 * a * Bq, 2 * (a + 1) * Bq)
            o_ref[a * Bq:(a + 1) * Bq, :] = _unstack_heads(acc[rows]).astype(BF16)
            lt_ref[a * Bq:(a + 1) * Bq, :] = _unstack_heads(jnp.broadcast_to(c[rows], (2 * Bq, 128)))

    return pl.pallas_call(
        body, name=name, grid=(SB_HEADS // 2, T // SB_KEYS),
        in_specs=[pl.BlockSpec((SB_KEYS, 128), lambda p, i: (i, p)), pl.BlockSpec((T, 128), lambda p, i: (0, p)),
                  pl.BlockSpec((T, 128), lambda p, i: (0, p + SB_HEADS // 2))],
        out_specs=[pl.BlockSpec((SB_KEYS, 128), lambda p, i: (i, p)),
                   pl.BlockSpec((None, SB_KEYS, 128), lambda p, i: (p, i, 0))],
        out_shape=[jax.ShapeDtypeStruct((T, D_MODEL), BF16), jax.ShapeDtypeStruct((SB_HEADS // 2, T, 128), F32)],
        compiler_params=_cparams(("parallel", "parallel")))(q, kv, kv)


def _sba_bwd(q, kv, lt, do, *, name):
    T = q.shape[0]
    Bq = SB_BLOCK
    nq = T // SB_KEYS
    nsub = SB_KEYS // Bq
    assert T % SB_KEYS == 0
    scale = 1.0 / math.sqrt(SB_HEAD_DIM)

    def body(q_ref, k_ref, v_ref, lt_ref, do_ref, dq_ref, dk_ref, dv_ref, dk_acc, dv_acc):
        i = pl.program_id(1)

        @pl.when(i == 0)
        def _():
            dk_acc[...] = jnp.zeros_like(dk_acc)
            dv_acc[...] = jnp.zeros_like(dv_acc)

        Uincl = _tri2(lambda k, j: k <= j)
        Uexcl = _tri2(lambda k, j: k < j)
        dmask = _sba_diag_mask()
        qs, dos, lts = [], [], []
        for a in range(nsub):
            rows = slice(a * Bq, (a + 1) * Bq)
            qs.append(_stack_heads(q_ref[rows, :] * scale))
            dos.append(_stack_heads(do_ref[rows, :]))
            lts.append(jnp.concatenate([lt_ref[rows, 0:1], lt_ref[rows, 64:65]], axis=0))
        qs_all = jnp.concatenate(qs, axis=0)
        dos_all = jnp.concatenate(dos, axis=0)
        lt_all = jnp.concatenate(lts, axis=0)

        def step(J, carry):
            pc, pe, dq_acc = carry
            off = pl.multiple_of(J * SB_KEYS, SB_KEYS)
            kb = k_ref[pl.ds(off, SB_KEYS), :]
            z = lax.dot_general(qs_all, kb, _NT, preferred_element_type=F32)
            dA = lax.dot_general(dos_all, v_ref[pl.ds(off, SB_KEYS), :], _NT, preferred_element_type=F32)
            a_parts, dz_parts = [], []
            for b in range(nsub):
                cols = slice(b * Bq, (b + 1) * Bq)
                A, dz, pc, pe = _sba_sub_bwd(z[:, cols], dA[:, cols], lt_all, pc, pe, Uincl, Uexcl, None)
                a_parts.append(A)
                dz_parts.append(dz)
            dzt = jnp.concatenate(dz_parts, axis=1)
            dk_acc[pl.ds(off, SB_KEYS), :] += lax.dot_general(dzt, qs_all, _TN, preferred_element_type=F32)
            dv_acc[pl.ds(off, SB_KEYS), :] += lax.dot_general(jnp.concatenate(a_parts, axis=1), dos_all, _TN,
                                                              preferred_element_type=F32)
            return pc, pe, dq_acc + jnp.dot(dzt, kb, preferred_element_type=F32)

        zc = jnp.zeros((2 * nsub * Bq, 1), F32)
        pc, pe, dq_acc = lax.fori_loop(0, i, step, (zc, zc, jnp.zeros((2 * nsub * Bq, 128), F32)))
        for a in range(nsub):
            rows = slice(2 * a * Bq, 2 * (a + 1) * Bq)
            pca, pea, dqa = pc[rows], pe[rows], dq_acc[rows]
            for b in range(a + 1):
                off = pl.multiple_of(i * SB_KEYS + b * Bq, Bq)
                kb = k_ref[pl.ds(off, Bq), :]
                zb = lax.dot_general(qs[a], kb, _NT, preferred_element_type=F32)
                dAb = lax.dot_general(dos[a], v_ref[pl.ds(off, Bq), :], _NT, preferred_element_type=F32)
                A, dz, pca, pea = _sba_sub_bwd(zb, dAb, lts[a], pca, pea, Uincl, Uexcl, dmask if b == a else None)
                dqa = dqa + jnp.dot(dz, kb, preferred_element_type=F32)
                dk_acc[pl.ds(off, Bq), :] += lax.dot_general(dz, qs[a], _TN, preferred_element_type=F32)
                dv_acc[pl.ds(off, Bq), :] += lax.dot_general(A, dos[a], _TN, preferred_element_type=F32)
            dq_ref[a * Bq:(a + 1) * Bq, :] = (_unstack_heads(dqa) * scale).astype(BF16)

        @pl.when(i == nq - 1)
        def _():
            dk_ref[...] = dk_acc[...].astype(BF16)
            dv_ref[...] = dv_acc[...].astype(BF16)

    return pl.pallas_call(
        body, name=name, grid=(SB_HEADS // 2, nq),
        in_specs=[pl.BlockSpec((SB_KEYS, 128), lambda p, i: (i, p)), pl.BlockSpec((T, 128), lambda p, i: (0, p)),
                  pl.BlockSpec((T, 128), lambda p, i: (0, p + SB_HEADS // 2)),
                  pl.BlockSpec((None, SB_KEYS, 128), lambda p, i: (p, i, 0)),
                  pl.BlockSpec((SB_KEYS, 128), lambda p, i: (i, p))],
        out_specs=[pl.BlockSpec((SB_KEYS, 128), lambda p, i: (i, p)), pl.BlockSpec((T, 128), lambda p, i: (0, p)),
                   pl.BlockSpec((T, 128), lambda p, i: (0, p))],
        out_shape=[jax.ShapeDtypeStruct((T, D_MODEL), BF16), jax.ShapeDtypeStruct((T, D_MODEL), BF16),
                   jax.ShapeDtypeStruct((T, D_MODEL), BF16)],
        scratch_shapes=[pltpu.VMEM((T, 128), F32), pltpu.VMEM((T, 128), F32)],
        compiler_params=_cparams(("parallel", "arbitrary")))(q, kv, kv, lt, do)


def _loss_head(h, tgt, w, *, name, tt=512):
    T, D = h.shape
    tt = min(tt, T)

    def body(h_ref, t_ref, w_ref, loss_ref, dh_ref, dw_ref):
        i = pl.program_id(0)
        hv = h_ref[...]
        wv = w_ref[...]
        r = lax.rsqrt(jnp.mean(hv * hv, axis=-1, keepdims=True) + EPS)
        xhat = hv * r
        err = xhat * wv - t_ref[...]
        part = 0.5 * jnp.sum(jnp.mean(err * err, axis=-1, keepdims=True), axis=0, keepdims=True)
        dy = err * (1.0 / D)
        dxh = dy * wv
        dh_ref[...] = r * (dxh - xhat * jnp.mean(dxh * xhat, axis=-1, keepdims=True))
        dwc = jnp.sum(dy * xhat, axis=0, keepdims=True)

        @pl.when(i == 0)
        def _():
            loss_ref[...] = jnp.broadcast_to(part, loss_ref.shape)
            dw_ref[...] = dwc

        @pl.when(i > 0)
        def _():
            loss_ref[...] += jnp.broadcast_to(part, loss_ref.shape)
            dw_ref[...] += dwc

    return pl.pallas_call(
        body, name=name, grid=(T // tt,),
        in_specs=[pl.BlockSpec((tt, D), lambda i: (i, 0)), pl.BlockSpec((tt, D), lambda i: (i, 0)),
                  pl.BlockSpec((1, D), lambda i: (0, 0))],
        out_specs=[pl.BlockSpec((1, 128), lambda i: (0, 0)), pl.BlockSpec((tt, D), lambda i: (i, 0)),
                   pl.BlockSpec((1, D), lambda i: (0, 0))],
        out_shape=[jax.ShapeDtypeStruct((1, 128), F32), jax.ShapeDtypeStruct((T, D), F32),
                   jax.ShapeDtypeStruct((1, D), F32)],
        compiler_params=_cparams(("arbitrary",)))(h, tgt, w.reshape(1, D))


def _adamw(parts, w, m, v, *, name, tr=256):
    P, R, C = parts.shape
    tr = min(tr, R)
    assert R % tr == 0, (name, R, tr)
    c1 = 1.0 - ADAM_B1 ** ADAM_STEP
    c2 = 1.0 - ADAM_B2 ** ADAM_STEP

    def body(p_ref, w_ref, m_ref, v_ref, g_ref, d_ref, nm_ref, nv_ref):
        g = p_ref[0].astype(F32)
        for k in range(1, P):
            g = g + p_ref[k].astype(F32)
        mn = ADAM_B1 * m_ref[...] + (1.0 - ADAM_B1) * g
        vn = ADAM_B2 * v_ref[...] + (1.0 - ADAM_B2) * (g * g)
        g_ref[...] = g
        nm_ref[...] = mn
        nv_ref[...] = vn
        d_ref[...] = -ADAM_LR * ((mn / c1) / (jnp.sqrt(vn / c2) + ADAM_EPS) + ADAM_WD * w_ref[...])

    spec = pl.BlockSpec((tr, C), lambda i: (i, 0))
    sds = jax.ShapeDtypeStruct((R, C), F32)
    return pl.pallas_call(
        body, name=name, grid=(R // tr,),
        in_specs=[pl.BlockSpec((P, tr, C), lambda i: (0, i, 0)), spec, spec, spec],
        out_specs=[spec, spec, spec, spec], out_shape=[sds, sds, sds, sds],
        compiler_params=_cparams(("parallel",)))(parts, w, m, v)


def _all_gather(shards, *, name):
    n = len(shards)

    def body(*refs):
        ins, outs = refs[:n], refs[n:2 * n]
        send_sems, recv_sems, local_sems = refs[2 * n:]
        x, y, c = lax.axis_index("x"), lax.axis_index("y"), lax.axis_index("c")
        me, sib = (x, y, c), (x, y, 1 - c)
        chips = [(1 - x, y), (x, 1 - y), (1 - x, 1 - y)]

        def slot(p):
            return 4 * p[0] + 2 * p[1] + p[2]

        def cp(a, k, block, to, src=None):
            dst = outs[a].at[slot(block)]
            return pltpu.make_async_remote_copy(src_ref=dst if src is None else src, dst_ref=dst,
                                                send_sem=send_sems.at[a, k], recv_sem=recv_sems.at[a, k],
                                                device_id=to, device_id_type=_MESH)

        mine = [pltpu.make_async_copy(ins[a], outs[a].at[slot(me)], local_sems.at[a]) for a in range(n)]
        for m in mine:
            m.start()
        first = []
        for a in range(n):
            first.append(cp(a, 0, me, sib, src=ins[a]))
            for j, chip in enumerate(chips):
                first.append(cp(a, 1 + j, me, (*chip, c), src=ins[a]))
        for f in first:
            f.start()
        passed = []
        for j, chip in enumerate(chips):
            for a in range(n):
                cp(a, 1 + j, (*chip, c), me).wait_recv()
                f = cp(a, 4 + j, (*chip, c), sib)
                f.start()
                passed.append(f)
        for a in range(n):
            cp(a, 0, sib, me).wait_recv()
            for j, chip in enumerate(chips):
                cp(a, 4 + j, (*chip, 1 - c), me).wait_recv()
        for f in first + passed:
            f.wait_send()
        for m in mine:
            m.wait()

    return pl.pallas_call(
        body, name=name, in_specs=[_ANY] * n, out_specs=[_ANY] * n,
        out_shape=[jax.ShapeDtypeStruct((N_DEV,) + s.shape, s.dtype) for s in shards],
        scratch_shapes=[pltpu.SemaphoreType.DMA((n, 7)), pltpu.SemaphoreType.DMA((n, 7)),
                        pltpu.SemaphoreType.DMA((n,))])(*shards)


def _exchange(blocks, *, name):
    n = len(blocks)

    def body(*refs):
        ins, outs = refs[:n], refs[n:2 * n]
        send_sems, recv_sems, local_sems = refs[2 * n:]
        x, y, c = lax.axis_index("x"), lax.axis_index("y"), lax.axis_index("c")
        me = 4 * x + 2 * y + c
        mine = [pltpu.make_async_copy(ins[a].at[me], outs[a].at[me], local_sems.at[a]) for a in range(n)]
        for m in mine:
            m.start()
        copies = []
        for r in range(1, N_DEV):
            rx, ry, rc = (r >> 2) & 1, (r >> 1) & 1, r & 1
            px, py, pc = (1 - x if rx else x), (1 - y if ry else y), (1 - c if rc else c)
            peer = 4 * px + 2 * py + pc
            for a in range(n):
                copies.append((pltpu.make_async_remote_copy(
                    src_ref=ins[a].at[peer], dst_ref=outs[a].at[me], send_sem=send_sems.at[a, r - 1],
                    recv_sem=recv_sems.at[a, r - 1], device_id=(px, py, pc), device_id_type=_MESH),
                    pltpu.make_async_remote_copy(
                    src_ref=ins[a].at[peer], dst_ref=outs[a].at[peer], send_sem=send_sems.at[a, r - 1],
                    recv_sem=recv_sems.at[a, r - 1], device_id=(px, py, pc), device_id_type=_MESH)))
        for snd, _ in copies:
            snd.start()
        for _, rcv in copies:
            rcv.wait_recv()
        for snd, _ in copies:
            snd.wait_send()
        for m in mine:
            m.wait()

    return pl.pallas_call(
        body, name=name, in_specs=[_ANY] * n, out_specs=[_ANY] * n,
        out_shape=[jax.ShapeDtypeStruct(b.shape, b.dtype) for b in blocks],
        scratch_shapes=[pltpu.SemaphoreType.DMA((n, 7)), pltpu.SemaphoreType.DMA((n, 7)),
                        pltpu.SemaphoreType.DMA((n,))])(*blocks)


def _ffn_fwd(h, nw, w_up, conv_w, conv_b, w_down, tag):
    a = _mm_fwd(h, w_up, norm_w=nw, name=f"ffn{tag}_up", tm=1024, tn=1408)
    p = _ffn_conv_fwd(a, conv_w, conv_b.reshape(1, -1), name=f"ffn{tag}_conv")
    h_out = _mm_fwd(p, w_down, residual=h, name=f"ffn{tag}_down", tm=1024, tn=512)
    return h_out, (a, p)


def _ffn_bwd(dh, h, saved, nw, w_up, conv_w, conv_b, w_down, tag):
    a, p = saved
    g_down = _mm_tn(p, dh, name=f"ffn{tag}_down_wg", tk1=1408, tn=1024)
    dp = _mm_nt(dh, w_down, name=f"ffn{tag}_down_dg", out_dtype=BF16, tm=1024, tn=1408, tk=1024)
    dhid, g_cw, g_cb = _ffn_conv_bwd_pre(a, conv_w, conv_b.reshape(1, -1), dp, name=f"ffn{tag}_conv_bwd")
    da = _conv_bwd_in(dhid, conv_w, K=FFN_CONV, name=f"ffn{tag}_conv_bwd_in", tt=256, tc=1408)
    g_up = _mm_tn(h, da, norm_w=nw, name=f"ffn{tag}_up_wg", tn=1408)
    dh_out, g_nw = _mm_nt(da, w_up, epi=(h, nw, dh), name=f"ffn{tag}_up_dg", tk=1408)
    return dh_out, dict(norm=g_nw.reshape(-1), up=g_up, conv_w=g_cw, conv_b=g_cb.reshape(-1), down=g_down)


def _local_step(x, tgt, W):
    T = x.shape[0]
    f = {}
    zx = _mm_fwd(x, W["in_w"], norm_w=W["ssm_norm_w"], name="ssm_in", tm=1024, tn=896)
    xbc_c = _ssm_conv_fwd(zx, W["ssm_conv_w"], W["ssm_conv_b"].reshape(1, -1), name="ssm_conv")
    dt_raw = zx[:, D_INNER + CONV_DIM:IN_PROJ_DIM]
    dtg = jnp.pad(dt_raw.reshape(T, SSM_GROUPS, 8).transpose(1, 0, 2), ((0, 0), (0, 0), (0, 120)))
    par = jnp.stack([W["ssm_dt_bias"].reshape(SSM_GROUPS, 8), W["ssm_a_log"].reshape(SSM_GROUPS, 8),
                     W["ssm_d"].reshape(SSM_GROUPS, 8)], axis=1)
    par = jnp.pad(par, ((0, 0), (0, 5), (0, 120)))
    gnw = W["ssm_gate_norm_w"].reshape(1, D_INNER)
    y, yn, st = _ssd_fwd(xbc_c, zx, dtg, par, gnw, name="ssd_fwd")
    h1 = _mm_fwd(yn, W["ssm_out_w"], residual=x, name="ssm_out", tm=1024, tn=512)
    h2, ffn0 = _ffn_fwd(h1, W["ffn_norm_w"][0], W["ffn_up_w"][0], W["ffn_conv_w"][0], W["ffn_conv_b"][0],
                        W["ffn_down_w"][0], "0")
    q = _mm_fwd(h2, W["w_q"], norm_w=W["attn_norm_w"], out_dtype=BF16, name="attn_q", tm=1024, tn=1024)
    kv = _mm_fwd(h2, W["w_kv"], norm_w=W["kv_norm_w"], out_dtype=BF16, name="attn_kv", tm=1024, tn=1024)
    o, lt = _sba_fwd(q, kv, name="sba_fwd")
    h3 = _mm_fwd(o, W["w_o"], residual=h2, name="attn_o", tm=1024, tn=512)
    h4, ffn1 = _ffn_fwd(h3, W["ffn_norm_w"][1], W["ffn_up_w"][1], W["ffn_conv_w"][1], W["ffn_conv_b"][1],
                        W["ffn_down_w"][1], "1")
    loss, dh4, g_final = _loss_head(h4, tgt, W["final_norm_w"], name="loss_head")
    dh3, gf1 = _ffn_bwd(dh4, h3, ffn1, W["ffn_norm_w"][1], W["ffn_up_w"][1], W["ffn_conv_w"][1], W["ffn_conv_b"][1],
                        W["ffn_down_w"][1], "1")
    g_wo = _mm_tn(o, dh3, name="attn_o_wg", tn=1024)
    do = _mm_nt(dh3, W["w_o"], name="attn_o_dg", out_dtype=BF16, tn=1024, tk=1024)
    dq, dk, dv = _sba_bwd(q, kv, lt, do, name="sba_bwd")
    g_wq = _mm_tn(h2, dq, norm_w=W["attn_norm_w"], name="attn_q_wg", tn=1024)
    dh2a, g_attn_nw = _mm_nt(dq, W["w_q"], epi=(h2, W["attn_norm_w"], dh3), name="attn_q_dg", tk=1024)
    dkv = jnp.concatenate([dk, dv], axis=1)
    g_wkv = _mm_tn(h2, dkv, norm_w=W["kv_norm_w"], name="attn_kv_wg", tn=1024)
    dh2, g_kv_nw = _mm_nt(dkv, W["w_kv"], epi=(h2, W["kv_norm_w"], dh2a), name="attn_kv_dg", tk=1024)
    dh1, gf0 = _ffn_bwd(dh2, h1, ffn0, W["ffn_norm_w"][0], W["ffn_up_w"][0], W["ffn_conv_w"][0], W["ffn_conv_b"][0],
                        W["ffn_down_w"][0], "0")
    g_out = _mm_tn(yn, dh1, name="ssm_out_wg", tn=1024)
    dyn = _mm_nt(dh1, W["ssm_out_w"], name="ssm_out_dg", out_dtype=BF16, tn=1024, tk=1024)
    dxs, dB, dC, dz, ddt, g_gnw, dpar = _ssd_bwd(xbc_c, zx, dtg, par, gnw, y, st, dyn, name="ssd_bwd")
    dxbc_c = jnp.concatenate([dxs, dB, dC], axis=1)
    dhid, g_scw, g_scb = _ssm_conv_bwd_pre(zx, W["ssm_conv_w"], W["ssm_conv_b"].reshape(1, -1), dxbc_c,
                                           name="ssm_conv_bwd")
    dxbc = _conv_bwd_in(dhid, W["ssm_conv_w"], K=SSM_CONV, name="ssm_conv_bwd_in")
    ddt_t = ddt[:, :, :8].transpose(1, 0, 2).reshape(T, SSM_HEADS).astype(BF16)
    dzx = jnp.concatenate([dz, dxbc, jnp.pad(ddt_t, ((0, 0), (0, IN_PROJ_PAD - IN_PROJ_DIM)))], axis=1)
    g_in = _mm_tn(x, dzx, norm_w=W["ssm_norm_w"], name="ssm_in_wg", tn=896)
    dx, g_ssm_nw = _mm_nt(dzx, W["in_w"], epi=(x, W["ssm_norm_w"], dh1), name="ssm_in_dg", tk=1792)
    f["ssm_norm_w"] = g_ssm_nw.reshape(-1)
    f["ssm_in_w"] = g_in[:, :IN_PROJ_DIM]
    f["ssm_conv_w"] = g_scw
    f["ssm_conv_b"] = g_scb.reshape(-1)
    f["ssm_dt_bias"] = dpar[:, 0, :8].reshape(-1)
    f["ssm_a_log"] = dpar[:, 1, :8].reshape(-1)
    f["ssm_d"] = dpar[:, 2, :8].reshape(-1)
    f["ssm_gate_norm_w"] = g_gnw.reshape(-1)
    f["ssm_out_w"] = g_out
    f["kv_norm_w"] = g_kv_nw.reshape(-1)
    f["w_k"] = g_wkv[:, :D_MODEL]
    f["w_v"] = g_wkv[:, D_MODEL:]
    f["attn_norm_w"] = g_attn_nw.reshape(-1)
    f["w_q"] = g_wq
    f["w_o"] = g_wo
    f["ffn_norm_w"] = jnp.stack([gf0["norm"], gf1["norm"]])
    f["ffn_up_w"] = [gf0["up"], gf1["up"]]
    f["ffn_conv_w"] = jnp.stack([gf0["conv_w"], gf1["conv_w"]])
    f["ffn_conv_b"] = jnp.stack([gf0["conv_b"], gf1["conv_b"]])
    f["ffn_down_w"] = [gf0["down"], gf1["down"]]
    f["final_norm_w"] = g_final.reshape(-1)
    return loss, dx, f


_BIG = ["ssm_in_w", "ssm_out_w", "w_k", "w_v", "w_q", "w_o", "ffn_up_w", "ffn_down_w"]
_SMALL_SHARDED = ["ssm_norm_w", "ssm_conv_w", "ssm_conv_b", "ssm_gate_norm_w", "ffn_conv_w"]
_SMALL_REPL = ["ssm_dt_bias", "ssm_a_log", "ssm_d", "kv_norm_w", "attn_norm_w", "ffn_norm_w", "ffn_conv_b",
               "final_norm_w"]
_WEIGHTS = ["ssm_norm_w", "ssm_in_w", "ssm_conv_w", "ssm_conv_b", "ssm_dt_bias", "ssm_a_log", "ssm_d",
            "ssm_gate_norm_w", "ssm_out_w", "kv_norm_w", "w_k", "w_v", "attn_norm_w", "w_q", "w_o", "ffn_norm_w",
            "ffn_up_w", "ffn_conv_w", "ffn_conv_b", "ffn_down_w", "final_norm_w"]


def _as2d(a):
    return a.reshape(-1, a.shape[-1])


def _cols_to_full(g):
    return g.transpose(1, 0, 2).reshape(g.shape[1], N_DEV * g.shape[2])


def _full_to_cols(a):
    R = a.shape[0]
    return a.reshape(R, N_DEV, -1).transpose(1, 0, 2)


def _gather_weights(p):
    names = _BIG + _SMALL_SHARDED
    shards = [_as2d(p[n]).astype(BF16) for n in _BIG] + [_as2d(p[n]) for n in _SMALL_SHARDED]
    got = dict(zip(names, _all_gather(shards, name="gather_weights")))
    W = {n: p[n] for n in _SMALL_REPL}
    in_w = _cols_to_full(got["ssm_in_w"])
    W["in_w"] = jnp.pad(in_w, ((0, 0), (0, IN_PROJ_PAD - IN_PROJ_DIM)))
    W["ssm_out_w"] = got["ssm_out_w"].reshape(D_INNER, D_MODEL)
    W["w_kv"] = jnp.concatenate([got["w_k"].reshape(D_MODEL, D_MODEL), got["w_v"].reshape(D_MODEL, D_MODEL)], axis=1)
    W["w_q"] = got["w_q"].reshape(D_MODEL, D_MODEL)
    W["w_o"] = got["w_o"].reshape(D_MODEL, D_MODEL)
    up = got["ffn_up_w"]
    W["ffn_up_w"] = [_cols_to_full(up[:, l * D_MODEL:(l + 1) * D_MODEL]) for l in range(2)]
    dn = got["ffn_down_w"]
    rs = D_FF // N_DEV
    W["ffn_down_w"] = [dn[:, l * rs:(l + 1) * rs].reshape(D_FF, D_MODEL) for l in range(2)]
    W["ssm_norm_w"] = got["ssm_norm_w"].reshape(D_MODEL)
    W["ssm_conv_w"] = _cols_to_full(got["ssm_conv_w"])
    W["ssm_conv_b"] = got["ssm_conv_b"].reshape(CONV_DIM)
    W["ssm_gate_norm_w"] = got["ssm_gate_norm_w"].reshape(D_INNER)
    fcw = _cols_to_full(got["ffn_conv_w"])
    W["ffn_conv_w"] = fcw.reshape(2, FFN_CONV, 2 * D_FF)
    for n in ("ssm_dt_bias", "ssm_a_log", "ssm_d", "attn_norm_w"):
        W[n] = W[n].reshape(-1)
    return W


def _big_grad_blocks(f):
    rs = D_FF // N_DEV
    return {
        "ssm_in_w": _full_to_cols(f["ssm_in_w"]),
        "ssm_out_w": f["ssm_out_w"].reshape(N_DEV, D_INNER // N_DEV, D_MODEL),
        "w_k": f["w_k"].reshape(N_DEV, D_MODEL // N_DEV, D_MODEL),
        "w_v": f["w_v"].reshape(N_DEV, D_MODEL // N_DEV, D_MODEL),
        "w_q": f["w_q"].reshape(N_DEV, D_MODEL // N_DEV, D_MODEL),
        "w_o": f["w_o"].reshape(N_DEV, D_MODEL // N_DEV, D_MODEL),
        "ffn_up_w": jnp.concatenate([_full_to_cols(g) for g in f["ffn_up_w"]], axis=1),
        "ffn_down_w": jnp.concatenate([g.reshape(N_DEV, rs, D_MODEL) for g in f["ffn_down_w"]], axis=1),
    }


def _pack_small(vals):
    flat = jnp.concatenate([v.reshape(-1).astype(F32) for v in vals])
    n = flat.shape[0]
    rows = -(-n // 1024) * 8
    return jnp.pad(flat, (0, rows * 128 - n)).reshape(rows, 128)


def _unpack_small(packed, shapes):
    flat = packed.reshape(-1)
    out, off = [], 0
    for s in shapes:
        n = math.prod(s)
        out.append(flat[off:off + n].reshape(s))
        off += n
    return out


def kernel(x, ssm_norm_w, ssm_in_w, ssm_conv_w, ssm_conv_b, ssm_dt_bias, ssm_a_log, ssm_d, ssm_gate_norm_w, ssm_out_w, kv_norm_w, w_k, w_v, attn_norm_w, w_q, w_o, ffn_norm_w, ffn_up_w, ffn_conv_w, ffn_conv_b, ffn_down_w, final_norm_w, loss_target, m_ssm_norm_w, m_ssm_in_w, m_ssm_conv_w, m_ssm_conv_b, m_ssm_dt_bias, m_ssm_a_log, m_ssm_d, m_ssm_gate_norm_w, m_ssm_out_w, m_kv_norm_w, m_w_k, m_w_v, m_attn_norm_w, m_w_q, m_w_o, m_ffn_norm_w, m_ffn_up_w, m_ffn_conv_w, m_ffn_conv_b, m_ffn_down_w, m_final_norm_w, v_ssm_norm_w, v_ssm_in_w, v_ssm_conv_w, v_ssm_conv_b, v_ssm_dt_bias, v_ssm_a_log, v_ssm_d, v_ssm_gate_norm_w, v_ssm_out_w, v_kv_norm_w, v_w_k, v_w_v, v_attn_norm_w, v_w_q, v_w_o, v_ffn_norm_w, v_ffn_up_w, v_ffn_conv_w, v_ffn_conv_b, v_ffn_down_w, v_final_norm_w):
    env = dict(locals())
    p = {n: env[n] for n in _WEIGHTS}
    mom = {n: env["m_" + n] for n in _WEIGHTS}
    var = {n: env["v_" + n] for n in _WEIGHTS}
    T = x.shape[1]
    me = 4 * lax.axis_index("x") + 2 * lax.axis_index("y") + lax.axis_index("c")

    W = _gather_weights(p)
    loss_row, dx, f = _local_step(x.reshape(T, D_MODEL), loss_target.reshape(T, D_MODEL), W)
    loss = lax.psum(loss_row[0, 0], ("x", "y", "c"))

    big = _big_grad_blocks(f)
    small_names = _SMALL_REPL + _SMALL_SHARDED
    small_full = _pack_small([f[n] for n in small_names])
    small_bcast = jnp.broadcast_to(small_full[None], (N_DEV,) + small_full.shape)
    got = _exchange([big[n] for n in _BIG] + [small_bcast], name="exchange_grads")
    big_parts = dict(zip(_BIG, got[:-1]))

    zero = jnp.zeros_like(small_full)
    g_small_sum = _adamw(got[-1], zero, zero, zero, name="sum_small_grads", tr=small_full.shape[0])[0]
    full_shapes = [f[n].shape for n in small_names]
    g_small = dict(zip(small_names, _unpack_small(g_small_sum, full_shapes)))
    for n in _SMALL_SHARDED:
        width = p[n].shape[-1]
        g_small[n] = lax.dynamic_slice_in_dim(g_small[n], me * width, width, axis=g_small[n].ndim - 1)

    out_g, out_d, out_m, out_v = {}, {}, {}, {}
    for n in _BIG:
        w2, m2, v2 = _as2d(p[n]), _as2d(mom[n]), _as2d(var[n])
        tr = 352 if n == "ffn_down_w" else 256
        g, d, nm, nv = _adamw(big_parts[n], w2, m2, v2, name="adamw_" + n, tr=tr)
        out_g[n], out_d[n], out_m[n], out_v[n] = (t.reshape(p[n].shape) for t in (g, d, nm, nv))
    sw = _pack_small([p[n] for n in small_names])
    sm = _pack_small([mom[n] for n in small_names])
    sv = _pack_small([var[n] for n in small_names])
    sg = _pack_small([g_small[n] for n in small_names])
    _, d, nm, nv = _adamw(sg[None], sw, sm, sv, name="adamw_small", tr=sw.shape[0])
    shard_shapes = [p[n].shape for n in small_names]
    for n, dd, mm, vv in zip(small_names, _unpack_small(d, shard_shapes), _unpack_small(nm, shard_shapes),
                             _unpack_small(nv, shard_shapes)):
        out_g[n] = g_small[n].reshape(p[n].shape)
        out_d[n], out_m[n], out_v[n] = dd, mm, vv

    return (loss, dx.reshape(x.shape), *[out_g[n] for n in _WEIGHTS], *[out_d[n] for n in _WEIGHTS],
            *[out_m[n] for n in _WEIGHTS], *[out_v[n] for n in _WEIGHTS])
```

```python
import functools
import math

import jax
import jax.numpy as jnp
from jax import lax
from jax.experimental import pallas as pl
from jax.experimental.pallas import tpu as pltpu

F32 = jnp.float32
BF16 = jnp.bfloat16
EPS = 1e-6

D_MODEL = 1024
D_INNER = 2048
SSM_HEADS = 32
SSM_GROUPS = 4
SSM_STATE = 128
SSM_CONV = 4
SSM_CHUNK = 128
GN = SSM_GROUPS * SSM_STATE
CONV_DIM = D_INNER + 2 * GN
IN_PROJ_DIM = D_INNER + CONV_DIM + SSM_HEADS
IN_PROJ_PAD = 5376
SB_HEADS = 16
SB_HEAD_DIM = 64
SB_BLOCK = 128
D_FF = 2816
FFN_CONV = 3
N_DEV = 8

ADAM_LR = 0.001
ADAM_B1 = 0.9
ADAM_B2 = 0.999
ADAM_EPS = 1e-08
ADAM_WD = 0.01
ADAM_STEP = 10

_MESH = pl.DeviceIdType.MESH
_NT = (((1,), (1,)), ((), ()))
_TN = (((0,), (0,)), ((), ()))
_ANY = pl.BlockSpec(memory_space=pl.ANY)


def _cparams(sem, vmem_mb=48):
    return pltpu.CompilerParams(dimension_semantics=sem, vmem_limit_bytes=vmem_mb * 1024 * 1024)


def _sigmoid(x):
    return 1.0 / (1.0 + jnp.exp(-x))


def _softplus(x):
    return jnp.maximum(x, 0.0) + jnp.log(1.0 + jnp.exp(-jnp.abs(x)))


def _rms_fwd(xv, w):
    r = lax.rsqrt(jnp.mean(xv * xv, axis=-1, keepdims=True) + EPS)
    return xv * r * w


def _mm_fwd(x, w, *, name, norm_w=None, residual=None, out_dtype=F32, tm=512, tn=512):
    M, K = x.shape
    N = w.shape[1]
    tm, tn = min(tm, M), min(tn, N)
    assert M % tm == 0 and N % tn == 0, (name, M, N, tm, tn)
    has_norm, has_res = norm_w is not None, residual is not None

    def body(*refs):
        x_ref, w_ref = refs[0], refs[1]
        p = 2
        nw_ref = r_ref = None
        if has_norm:
            nw_ref = refs[p]
            p += 1
        if has_res:
            r_ref = refs[p]
            p += 1
        o_ref, xn_ref = refs[p], refs[p + 1]

        @pl.when(pl.program_id(1) == 0)
        def _():
            xv = x_ref[...].astype(F32)
            if has_norm:
                xv = _rms_fwd(xv, nw_ref[...])
            xn_ref[...] = xv.astype(BF16)

        acc = jnp.dot(xn_ref[...], w_ref[...], preferred_element_type=F32)
        if has_res:
            acc = acc + r_ref[...]
        o_ref[...] = acc.astype(out_dtype)

    in_specs = [pl.BlockSpec((tm, K), lambda i, j: (i, 0)), pl.BlockSpec((K, tn), lambda i, j: (0, j))]
    args = [x, w]
    if has_norm:
        in_specs.append(pl.BlockSpec((1, K), lambda i, j: (0, 0)))
        args.append(norm_w.reshape(1, K))
    if has_res:
        in_specs.append(pl.BlockSpec((tm, tn), lambda i, j: (i, j)))
        args.append(residual)
    return pl.pallas_call(
        body, name=name, grid=(M // tm, N // tn), in_specs=in_specs,
        out_specs=pl.BlockSpec((tm, tn), lambda i, j: (i, j)),
        out_shape=jax.ShapeDtypeStruct((M, N), out_dtype),
        scratch_shapes=[pltpu.VMEM((tm, K), BF16)],
        compiler_params=_cparams(("parallel", "arbitrary")))(*args)


def _mm_nt(dy, w, *, name, epi=None, out_dtype=F32, tm=512, tn=512, tk=512):
    M, K = dy.shape
    N = w.shape[0]
    tm, tk = min(tm, M), min(tk, K)
    tn = N if epi is not None else min(tn, N)
    assert M % tm == 0 and N % tn == 0 and K % tk == 0, (name, M, N, K, tm, tn, tk)
    nk = K // tk
    has_epi = epi is not None

    def body(*refs):
        if has_epi:
            dy_ref, w_ref, h_ref, nw_ref, r_ref, o_ref, dnw_ref, acc_ref = refs
        else:
            dy_ref, w_ref, o_ref, acc_ref = refs
        i = pl.program_id(0)
        k = pl.program_id(2)

        @pl.when(k == 0)
        def _():
            acc_ref[...] = jnp.zeros_like(acc_ref)

        acc_ref[...] += lax.dot_general(dy_ref[...].astype(BF16), w_ref[...], _NT, preferred_element_type=F32)

        @pl.when(k == nk - 1)
        def _():
            du = acc_ref[...]
            if has_epi:
                hv = h_ref[...]
                r = lax.rsqrt(jnp.mean(hv * hv, axis=-1, keepdims=True) + EPS)
                xhat = hv * r
                dxh = du * nw_ref[...]
                dx = r * (dxh - xhat * jnp.mean(dxh * xhat, axis=-1, keepdims=True))
                o_ref[...] = (r_ref[...] + dx).astype(out_dtype)
                contrib = jnp.sum(du * xhat, axis=0, keepdims=True)

                @pl.when(i == 0)
                def _():
                    dnw_ref[...] = contrib

                @pl.when(i > 0)
                def _():
                    dnw_ref[...] += contrib
            else:
                o_ref[...] = du.astype(out_dtype)

    in_specs = [pl.BlockSpec((tm, tk), lambda i, j, k: (i, k)), pl.BlockSpec((tn, tk), lambda i, j, k: (j, k))]
    args = [dy, w]
    out_specs = [pl.BlockSpec((tm, tn), lambda i, j, k: (i, j))]
    out_shape = [jax.ShapeDtypeStruct((M, N), out_dtype)]
    if has_epi:
        h, nw, res = epi
        in_specs += [pl.BlockSpec((tm, N), lambda i, j, k: (i, 0)), pl.BlockSpec((1, N), lambda i, j, k: (0, 0)),
                     pl.BlockSpec((tm, N), lambda i, j, k: (i, 0))]
        args += [h, nw.reshape(1, N), res]
        out_specs.append(pl.BlockSpec((1, N), lambda i, j, k: (0, 0)))
        out_shape.append(jax.ShapeDtypeStruct((1, N), F32))
    outs = pl.pallas_call(
        body, name=name, grid=(M // tm, N // tn, nk), in_specs=in_specs, out_specs=out_specs, out_shape=out_shape,
        scratch_shapes=[pltpu.VMEM((tm, tn), F32)],
        compiler_params=_cparams(("arbitrary", "arbitrary", "arbitrary")))(*args)
    return (outs[0], outs[1]) if has_epi else outs[0]


def _mm_tn(x, dy, *, name, norm_w=None, out_dtype=BF16, tk1=1024, tn=512, tt=512):
    T, K1 = x.shape
    N = dy.shape[1]
    tk1, tn, tt = min(tk1, K1), min(tn, N), min(tt, T)
    has_norm = norm_w is not None
    assert K1 % tk1 == 0 and N % tn == 0 and T % tt == 0, (name, K1, N, T, tk1, tn, tt)
    assert not has_norm or tk1 == K1
    nt = T // tt

    def body(*refs):
        if has_norm:
            x_ref, dy_ref, nw_ref, o_ref, acc_ref = refs
        else:
            x_ref, dy_ref, o_ref, acc_ref = refs
        t = pl.program_id(2)

        @pl.when(t == 0)
        def _():
            acc_ref[...] = jnp.zeros_like(acc_ref)

        xv = x_ref[...]
        if has_norm:
            xv = _rms_fwd(xv.astype(F32), nw_ref[...])
        acc_ref[...] += lax.dot_general(xv.astype(BF16), dy_ref[...].astype(BF16), _TN, preferred_element_type=F32)

        @pl.when(t == nt - 1)
        def _():
            o_ref[...] = acc_ref[...].astype(out_dtype)

    in_specs = [pl.BlockSpec((tt, tk1), lambda a, b, t: (t, a)), pl.BlockSpec((tt, tn), lambda a, b, t: (t, b))]
    args = [x, dy]
    if has_norm:
        in_specs.append(pl.BlockSpec((1, K1), lambda a, b, t: (0, 0)))
        args.append(norm_w.reshape(1, K1))
    return pl.pallas_call(
        body, name=name, grid=(K1 // tk1, N // tn, nt), in_specs=in_specs,
        out_specs=pl.BlockSpec((tk1, tn), lambda a, b, t: (a, b)),
        out_shape=jax.ShapeDtypeStruct((K1, N), out_dtype),
        scratch_shapes=[pltpu.VMEM((tk1, tn), F32)],
        compiler_params=_cparams(("parallel", "parallel", "arbitrary")))(*args)


def _shift_down(xb, prev8, j):
    main = pltpu.roll(xb, j, 0)
    head = pltpu.roll(xb[0:8], j, 0)
    ph = pltpu.roll(prev8, j, 0)
    row8 = lax.broadcasted_iota(jnp.int32, head.shape, 0)
    head = jnp.where(row8 < j, ph, head)
    return jnp.concatenate([head, main[8:]], axis=0)


def _shift_up(xb, next8, j):
    tt = xb.shape[0]
    main = pltpu.roll(xb, tt - j, 0)
    tail = pltpu.roll(xb[tt - 8:tt], 8 - j, 0)
    nh = pltpu.roll(next8, 8 - j, 0)
    row8 = lax.broadcasted_iota(jnp.int32, tail.shape, 0)
    tail = jnp.where(row8 + j >= 8, nh, tail)
    return jnp.concatenate([main[:tt - 8], tail], axis=0)


def _conv_hid(xb, prev8, w, b_row, K):
    out = b_row
    shifted = []
    for j in range(K):
        sh = K - 1 - j
        xs = xb if sh == 0 else _shift_down(xb, prev8, sh)
        shifted.append(xs)
        out = out + xs * w[j:j + 1, :]
    return out, shifted


def _prev_idx(i, nb8):
    return jnp.maximum(i * nb8 - 1, 0)


def _ssm_conv_fwd(zx, w, b, *, name, tt=512, tc=512):
    T = zx.shape[0]
    tt = min(tt, T)
    C, K = CONV_DIM, SSM_CONV
    cb0, nb8 = D_INNER // tc, tt // 8

    def body(x_ref, p_ref, w_ref, b_ref, o_ref):
        first = (pl.program_id(1) > 0).astype(F32)
        hid, _ = _conv_hid(x_ref[...], p_ref[...] * first, w_ref[...], b_ref[...], K)
        o_ref[...] = hid * _sigmoid(hid)

    return pl.pallas_call(
        body, name=name, grid=(C // tc, T // tt),
        in_specs=[pl.BlockSpec((tt, tc), lambda c, i: (i, c + cb0)),
                  pl.BlockSpec((8, tc), lambda c, i: (_prev_idx(i, nb8), c + cb0)),
                  pl.BlockSpec((K, tc), lambda c, i: (0, c)), pl.BlockSpec((1, tc), lambda c, i: (0, c))],
        out_specs=pl.BlockSpec((tt, tc), lambda c, i: (i, c)),
        out_shape=jax.ShapeDtypeStruct((T, C), F32),
        compiler_params=_cparams(("parallel", "parallel")))(zx, zx, w, b)


def _ssm_conv_bwd_pre(zx, w, b, dout, *, name, tt=512, tc=512):
    T = zx.shape[0]
    tt = min(tt, T)
    C, K = CONV_DIM, SSM_CONV
    cb0, nb8 = D_INNER // tc, tt // 8

    def body(x_ref, p_ref, w_ref, b_ref, d_ref, dh_ref, dw_ref, db_ref):
        t = pl.program_id(1)
        first = (t > 0).astype(F32)
        hid, shifted = _conv_hid(x_ref[...], p_ref[...] * first, w_ref[...], b_ref[...], K)
        sg = _sigmoid(hid)
        dh = d_ref[...] * (sg * (1.0 + hid * (1.0 - sg)))
        dh_ref[...] = dh

        @pl.when(t == 0)
        def _():
            dw_ref[...] = jnp.zeros_like(dw_ref)
            db_ref[...] = jnp.zeros_like(db_ref)

        db_ref[...] += jnp.sum(dh, axis=0, keepdims=True)
        for j in range(K):
            dw_ref[j:j + 1, :] += jnp.sum(dh * shifted[j], axis=0, keepdims=True)

    return pl.pallas_call(
        body, name=name, grid=(C // tc, T // tt),
        in_specs=[pl.BlockSpec((tt, tc), lambda c, i: (i, c + cb0)),
                  pl.BlockSpec((8, tc), lambda c, i: (_prev_idx(i, nb8), c + cb0)),
                  pl.BlockSpec((K, tc), lambda c, i: (0, c)), pl.BlockSpec((1, tc), lambda c, i: (0, c)),
                  pl.BlockSpec((tt, tc), lambda c, i: (i, c))],
        out_specs=[pl.BlockSpec((tt, tc), lambda c, i: (i, c)), pl.BlockSpec((K, tc), lambda c, i: (0, c)),
                   pl.BlockSpec((1, tc), lambda c, i: (0, c))],
        out_shape=[jax.ShapeDtypeStruct((T, C), F32), jax.ShapeDtypeStruct((K, C), F32),
                   jax.ShapeDtypeStruct((1, C), F32)],
        compiler_params=_cparams(("parallel", "arbitrary")))(zx, zx, w, b, dout)


def _conv_bwd_in(dh, w, *, name, K, tt=512, tc=512, out_dtype=BF16):
    T, C = dh.shape
    tt = min(tt, T)
    nb8, nT = tt // 8, T // tt
    last8 = T // 8 - 1

    def body(d_ref, n_ref, w_ref, o_ref):
        notlast = (pl.program_id(1) < nT - 1).astype(F32)
        d = d_ref[...]
        nxt = n_ref[...] * notlast
        w_ = w_ref[...]
        acc = d * w_[K - 1:K, :]
        for sh in range(1, K):
            acc = acc + _shift_up(d, nxt, sh) * w_[K - 1 - sh:K - sh, :]
        o_ref[...] = acc.astype(out_dtype)

    return pl.pallas_call(
        body, name=name, grid=(C // tc, nT),
        in_specs=[pl.BlockSpec((tt, tc), lambda c, i: (i, c)),
                  pl.BlockSpec((8, tc), lambda c, i: (jnp.minimum((i + 1) * nb8, last8), c)),
                  pl.BlockSpec((K, tc), lambda c, i: (0, c))],
        out_specs=pl.BlockSpec((tt, tc), lambda c, i: (i, c)),
        out_shape=jax.ShapeDtypeStruct((T, C), out_dtype),
        compiler_params=_cparams(("parallel", "parallel")))(dh, dh, w)


def _ffn_conv_fwd(a, w, b, *, name, tt=256, tc=1408):
    T = a.shape[0]
    tt = min(tt, T)
    K, nbh, nb8 = FFN_CONV, D_FF // tc, tt // 8

    def body(ag_ref, pg_ref, av_ref, pv_ref, wg_ref, wv_ref, bg_ref, bv_ref, o_ref):
        first = (pl.program_id(1) > 0).astype(F32)
        hg, _ = _conv_hid(ag_ref[...], pg_ref[...] * first, wg_ref[...], bg_ref[...], K)
        hv, _ = _conv_hid(av_ref[...], pv_ref[...] * first, wv_ref[...], bv_ref[...], K)
        o_ref[...] = (hg * _sigmoid(hg) * hv).astype(BF16)

    return pl.pallas_call(
        body, name=name, grid=(nbh, T // tt),
        in_specs=[pl.BlockSpec((tt, tc), lambda c, i: (i, c)),
                  pl.BlockSpec((8, tc), lambda c, i: (_prev_idx(i, nb8), c)),
                  pl.BlockSpec((tt, tc), lambda c, i: (i, c + nbh)),
                  pl.BlockSpec((8, tc), lambda c, i: (_prev_idx(i, nb8), c + nbh)),
                  pl.BlockSpec((K, tc), lambda c, i: (0, c)), pl.BlockSpec((K, tc), lambda c, i: (0, c + nbh)),
                  pl.BlockSpec((1, tc), lambda c, i: (0, c)), pl.BlockSpec((1, tc), lambda c, i: (0, c + nbh))],
        out_specs=pl.BlockSpec((tt, tc), lambda c, i: (i, c)),
        out_shape=jax.ShapeDtypeStruct((T, D_FF), BF16),
        compiler_params=_cparams(("parallel", "parallel")))(a, a, a, a, w, w, b, b)


def _ffn_conv_bwd_pre(a, w, b, dp, *, name, tt=256, tc=1408):
    T = a.shape[0]
    tt = min(tt, T)
    K, nbh, nb8 = FFN_CONV, D_FF // tc, tt // 8

    def body(ao_ref, po_ref, ag_ref, pg_ref, av_ref, pv_ref, wg_ref, wv_ref, bg_ref, bv_ref, dp_ref,
             dh_ref, dw_ref, db_ref):
        j = pl.program_id(0)
        t = pl.program_id(1)
        first = (t > 0).astype(F32)
        hg, _ = _conv_hid(ag_ref[...], pg_ref[...] * first, wg_ref[...], bg_ref[...], K)
        hv, _ = _conv_hid(av_ref[...], pv_ref[...] * first, wv_ref[...], bv_ref[...], K)
        sg = _sigmoid(hg)
        d = dp_ref[...].astype(F32)
        is_gate = (j < nbh).astype(F32)
        dh = d * (is_gate * (hv * (sg * (1.0 + hg * (1.0 - sg)))) + (1.0 - is_gate) * (hg * sg))
        dh_ref[...] = dh
        xo = ao_ref[...]
        po = po_ref[...] * first

        @pl.when(t == 0)
        def _():
            dw_ref[...] = jnp.zeros_like(dw_ref)
            db_ref[...] = jnp.zeros_like(db_ref)

        db_ref[...] += jnp.sum(dh, axis=0, keepdims=True)
        for jj in range(K):
            sh = K - 1 - jj
            xs = xo if sh == 0 else _shift_down(xo, po, sh)
            dw_ref[jj:jj + 1, :] += jnp.sum(dh * xs, axis=0, keepdims=True)

    def gi(c):
        return lax.rem(c, nbh)

    return pl.pallas_call(
        body, name=name, grid=(2 * nbh, T // tt),
        in_specs=[pl.BlockSpec((tt, tc), lambda c, i: (i, c)),
                  pl.BlockSpec((8, tc), lambda c, i: (_prev_idx(i, nb8), c)),
                  pl.BlockSpec((tt, tc), lambda c, i: (i, gi(c))),
                  pl.BlockSpec((8, tc), lambda c, i: (_prev_idx(i, nb8), gi(c))),
                  pl.BlockSpec((tt, tc), lambda c, i: (i, gi(c) + nbh)),
                  pl.BlockSpec((8, tc), lambda c, i: (_prev_idx(i, nb8), gi(c) + nbh)),
                  pl.BlockSpec((K, tc), lambda c, i: (0, gi(c))), pl.BlockSpec((K, tc), lambda c, i: (0, gi(c) + nbh)),
                  pl.BlockSpec((1, tc), lambda c, i: (0, gi(c))), pl.BlockSpec((1, tc), lambda c, i: (0, gi(c) + nbh)),
                  pl.BlockSpec((tt, tc), lambda c, i: (i, gi(c)))],
        out_specs=[pl.BlockSpec((tt, tc), lambda c, i: (i, c)), pl.BlockSpec((K, tc), lambda c, i: (0, c)),
                   pl.BlockSpec((1, tc), lambda c, i: (0, c))],
        out_shape=[jax.ShapeDtypeStruct((T, 2 * D_FF), F32), jax.ShapeDtypeStruct((K, 2 * D_FF), F32),
                   jax.ShapeDtypeStruct((1, 2 * D_FF), F32)],
        compiler_params=_cparams(("parallel", "arbitrary")))(a, a, a, a, a, a, w, w, b, b, dp)


def _cumsum_rows(x):
    L = x.shape[0]
    row = lax.broadcasted_iota(jnp.int32, x.shape, 0)
    k = 1
    while k < L:
        x = x + jnp.where(row >= k, pltpu.roll(x, k, 0), 0.0)
        k *= 2
    return x


def _rcumsum_rows(x):
    L = x.shape[0]
    row = lax.broadcasted_iota(jnp.int32, x.shape, 0)
    k = 1
    while k < L:
        x = x + jnp.where(row < L - k, pltpu.roll(x, L - k, 0), 0.0)
        k *= 2
    return x


def _ssd_common(dt_ref, par_ref):
    par = par_ref[...]
    raw = dt_ref[...] + par[0:1, :]
    dt = _softplus(raw)
    a = -jnp.exp(par[1:2, :])
    cs = _cumsum_rows(dt * a)
    L = cs.shape[0]
    cs_last = cs[L - 1:L, :]
    return raw, dt, a, par[2:3, :], cs, cs.T, jnp.exp(cs), jnp.exp(cs_last - cs), jnp.exp(cs_last)


def _ssd_specs(nc, rev):
    L = SSM_CHUNK

    def ci(c):
        return nc - 1 - c if rev else c

    return [pl.BlockSpec((L, 512), lambda g, c: (ci(c), g)),
            pl.BlockSpec((L, 128), lambda g, c: (ci(c), 16 + g)),
            pl.BlockSpec((L, 128), lambda g, c: (ci(c), 20 + g)),
            pl.BlockSpec((None, L, 128), lambda g, c: (g, ci(c), 0)),
            pl.BlockSpec((None, 8, 128), lambda g, c: (g, 0, 0)),
            pl.BlockSpec((L, 512), lambda g, c: (ci(c), g)),
            pl.BlockSpec((1, 512), lambda g, c: (0, g))], ci


def _ssd_fwd(xbc_c, zx, dtg, par, gnw, *, name):
    T = xbc_c.shape[0]
    L = SSM_CHUNK
    nc = T // L
    in_specs, ci = _ssd_specs(nc, False)

    def body(xs_ref, b_ref, c_ref, dt_ref, par_ref, z_ref, gnw_ref, y_ref, yn_ref, st_ref, h_ref):
        @pl.when(pl.program_id(1) == 0)
        def _():
            h_ref[...] = jnp.zeros_like(h_ref)

        _, dt, _, dsk, cs, csT, ecs, eend, dec = _ssd_common(dt_ref, par_ref)
        Bb = b_ref[...].astype(BF16)
        Cb = c_ref[...].astype(BF16)
        G = lax.dot_general(Cb, Bb, _NT, preferred_element_type=F32)
        row = lax.broadcasted_iota(jnp.int32, (L, L), 0)
        col = lax.broadcasted_iota(jnp.int32, (L, L), 1)
        tril = col <= row
        lo = lax.broadcasted_iota(jnp.int32, (L, 128), 1) < 64
        lo1 = lax.broadcasted_iota(jnp.int32, (1, 128), 1) < 64
        for pp in range(4):
            hA, hB = 2 * pp, 2 * pp + 1

            def sel(m):
                return jnp.where(lo, m[:, hA:hA + 1], m[:, hB:hB + 1])

            def sel1(m):
                return jnp.where(lo1, m[:, hA:hA + 1], m[:, hB:hB + 1])

            X = xs_ref[:, pp * 128:(pp + 1) * 128]
            xd = X * sel(dt)
            xdb = xd.astype(BF16)
            ys = []
            for h in (hA, hB):
                Lm = jnp.where(tril, jnp.exp(jnp.minimum(cs[:, h:h + 1] - csT[h:h + 1, :], 0.0)), 0.0)
                ys.append(jnp.dot((G * Lm).astype(BF16), xdb, preferred_element_type=F32))
            Hp = h_ref[pp]
            st_ref[pp] = Hp
            yoff = jnp.dot(Cb, Hp.astype(BF16), preferred_element_type=F32) * sel(ecs)
            y_ref[:, pp * 128:(pp + 1) * 128] = jnp.where(lo, ys[0], ys[1]) + yoff + sel1(dsk) * X
            S = lax.dot_general(Bb, (xd * sel(eend)).astype(BF16), _TN, preferred_element_type=F32)
            h_ref[pp] = Hp * sel1(dec) + S
        zv = z_ref[...]
        yg = y_ref[...] * (zv * _sigmoid(zv))
        yn_ref[...] = _rms_fwd(yg, gnw_ref[...]).astype(BF16)

    return pl.pallas_call(
        body, name=name, grid=(SSM_GROUPS, nc), in_specs=in_specs,
        out_specs=[pl.BlockSpec((L, 512), lambda g, c: (c, g)), pl.BlockSpec((L, 512), lambda g, c: (c, g)),
                   pl.BlockSpec((None, None, 4, 128, 128), lambda g, c: (g, c, 0, 0, 0))],
        out_shape=[jax.ShapeDtypeStruct((T, D_INNER), F32), jax.ShapeDtypeStruct((T, D_INNER), BF16),
                   jax.ShapeDtypeStruct((SSM_GROUPS, nc, 4, 128, 128), F32)],
        scratch_shapes=[pltpu.VMEM((4, 128, 128), F32)],
        compiler_params=_cparams(("parallel", "arbitrary")))(xbc_c, xbc_c, xbc_c, dtg, par, zx, gnw)


def _ssd_bwd(xbc_c, zx, dtg, par, gnw, y, st, dyn, *, name):
    T = xbc_c.shape[0]
    L = SSM_CHUNK
    nc = T // L
    in_specs, ci = _ssd_specs(nc, True)
    in_specs += [pl.BlockSpec((L, 512), lambda g, c: (ci(c), g)),
                 pl.BlockSpec((None, None, 4, 128, 128), lambda g, c: (g, ci(c), 0, 0, 0)),
                 pl.BlockSpec((L, 512), lambda g, c: (ci(c), g))]

    def body(xs_ref, b_ref, c_ref, dt_ref, par_ref, z_ref, gnw_ref, y_ref, st_ref, dyn_ref,
             dxs_ref, db_ref, dc_ref, dz_ref, ddt_ref, dgnw_ref, dpar_ref, dh_ref):
        @pl.when(pl.program_id(1) == 0)
        def _():
            dh_ref[...] = jnp.zeros_like(dh_ref)
            dgnw_ref[...] = jnp.zeros_like(dgnw_ref)
            dpar_ref[...] = jnp.zeros_like(dpar_ref)

        yv = y_ref[...]
        zv = z_ref[...]
        sg = _sigmoid(zv)
        sz = zv * sg
        yg = yv * sz
        r = lax.rsqrt(jnp.mean(yg * yg, axis=-1, keepdims=True) + EPS)
        yh = yg * r
        dyn = dyn_ref[...].astype(F32)
        dgnw_ref[...] += jnp.sum(dyn * yh, axis=0, keepdims=True)
        dyh = dyn * gnw_ref[...]
        dyg = r * (dyh - yh * jnp.mean(dyh * yh, axis=-1, keepdims=True))
        dY_all = dyg * sz
        dz_ref[...] = (dyg * yv * (sg * (1.0 + zv * (1.0 - sg)))).astype(dz_ref.dtype)

        raw, dt, a, dsk, cs, csT, ecs, eend, dec = _ssd_common(dt_ref, par_ref)
        Bb = b_ref[...].astype(BF16)
        Cb = c_ref[...].astype(BF16)
        G = lax.dot_general(Cb, Bb, _NT, preferred_element_type=F32)
        row = lax.broadcasted_iota(jnp.int32, (L, L), 0)
        col = lax.broadcasted_iota(jnp.int32, (L, L), 1)
        tril = col <= row
        lane = lax.broadcasted_iota(jnp.int32, (L, 128), 1)
        lo = lane < 64
        lane1 = lax.broadcasted_iota(jnp.int32, (1, 128), 1)
        lo1 = lane1 < 64
        rowc = lax.broadcasted_iota(jnp.int32, (L, 1), 0)
        dG = jnp.zeros((L, L), F32)
        dB = jnp.zeros((L, SSM_STATE), F32)
        dC = jnp.zeros((L, SSM_STATE), F32)
        dcs_mat = jnp.zeros((L, 128), F32)
        dcs_t = jnp.zeros((L, L), F32)
        ddt_mat = jnp.zeros((L, 128), F32)
        dD_row = jnp.zeros((1, 128), F32)

        def tot(m):
            return jnp.sum(jnp.sum(m, axis=1, keepdims=True), axis=0, keepdims=True)

        for pp in range(4):
            hA, hB = 2 * pp, 2 * pp + 1

            def sel(m):
                return jnp.where(lo, m[:, hA:hA + 1], m[:, hB:hB + 1])

            def sel1(m):
                return jnp.where(lo1, m[:, hA:hA + 1], m[:, hB:hB + 1])

            X = xs_ref[:, pp * 128:(pp + 1) * 128]
            dY = dY_all[:, pp * 128:(pp + 1) * 128]
            dtsel = sel(dt)
            xd = X * dtsel
            xdb = xd.astype(BF16)
            dYb = dY.astype(BF16)
            Hp = st_ref[pp]
            Hb = Hp.astype(BF16)
            dHn = dh_ref[pp]
            dHb = dHn.astype(BF16)
            ecs_sel = sel(ecs)
            eend_sel = sel(eend)
            dxd_state = jnp.dot(Bb, dHb, preferred_element_type=F32) * eend_sel
            yoff = jnp.dot(Cb, Hb, preferred_element_type=F32) * ecs_sel
            dYe = (dY * ecs_sel).astype(BF16)
            dC = dC + lax.dot_general(dYe, Hb, _NT, preferred_element_type=F32)
            dB = dB + lax.dot_general((xd * eend_sel).astype(BF16), dHb, _NT, preferred_element_type=F32)
            dh_ref[pp] = dHn * sel1(dec) + lax.dot_general(Cb, dYe, _TN, preferred_element_type=F32)
            q = xd * dxd_state
            dyoff = dY * yoff
            hh = dHn * Hp
            dxd_diag = []
            for h, msk, msk1 in ((hA, lo, lo1), (hB, jnp.logical_not(lo), jnp.logical_not(lo1))):
                Lm = jnp.where(tril, jnp.exp(jnp.minimum(cs[:, h:h + 1] - csT[h:h + 1, :], 0.0)), 0.0)
                M = G * Lm
                dxd_diag.append(lax.dot_general(M.astype(BF16), dYb, _TN, preferred_element_type=F32))
                dM = lax.dot_general(jnp.where(msk, dY, 0.0).astype(BF16), xdb, _NT, preferred_element_type=F32)
                dG = dG + dM * Lm
                W = dM * M
                dcs_h = jnp.sum(W, axis=1, keepdims=True)
                dcs_t = dcs_t + jnp.where(row == h, jnp.sum(W, axis=0, keepdims=True), 0.0)
                dcs_h = dcs_h + jnp.sum(jnp.where(msk, dyoff - q, 0.0), axis=1, keepdims=True)
                tail = tot(jnp.where(msk, q, 0.0)) + dec[:, h:h + 1] * tot(jnp.where(msk, hh, 0.0))
                dcs_h = dcs_h + jnp.where(rowc == L - 1, tail, 0.0)
                dcs_mat = dcs_mat + jnp.where(lane == h, dcs_h, 0.0)
            dxd = jnp.where(lo, dxd_diag[0], dxd_diag[1]) + dxd_state
            prod = dxd * X
            dA_ = jnp.sum(jnp.where(lo, prod, 0.0), axis=1, keepdims=True)
            dB_ = jnp.sum(prod, axis=1, keepdims=True) - dA_
            ddt_mat = ddt_mat + jnp.where(lane == hA, dA_, 0.0) + jnp.where(lane == hB, dB_, 0.0)
            dxs_ref[:, pp * 128:(pp + 1) * 128] = dxd * dtsel + sel1(dsk) * dY
            dyx = jnp.sum(dY * X, axis=0, keepdims=True)
            sA = jnp.sum(jnp.where(lo1, dyx, 0.0), axis=1, keepdims=True)
            sB = jnp.sum(dyx, axis=1, keepdims=True) - sA
            dD_row = dD_row + jnp.where(lane1 == hA, sA, 0.0) + jnp.where(lane1 == hB, sB, 0.0)
        dGb = dG.astype(BF16)
        db_ref[...] = dB + lax.dot_general(dGb, Cb, _TN, preferred_element_type=F32)
        dc_ref[...] = dC + jnp.dot(dGb, Bb, preferred_element_type=F32)
        dad = _rcumsum_rows(dcs_mat - dcs_t.T)
        draw = (a * dad + ddt_mat) * _sigmoid(raw)
        ddt_ref[...] = draw
        dpar_ref[0:1, :] += jnp.sum(draw, axis=0, keepdims=True)
        dpar_ref[1:2, :] += jnp.sum(dt * dad, axis=0, keepdims=True) * a
        dpar_ref[2:3, :] += dD_row

    return pl.pallas_call(
        body, name=name, grid=(SSM_GROUPS, nc), in_specs=in_specs,
        out_specs=[pl.BlockSpec((L, 512), lambda g, c: (ci(c), g)),
                   pl.BlockSpec((L, 128), lambda g, c: (ci(c), g)),
                   pl.BlockSpec((L, 128), lambda g, c: (ci(c), g)),
                   pl.BlockSpec((L, 512), lambda g, c: (ci(c), g)),
                   pl.BlockSpec((None, L, 128), lambda g, c: (g, ci(c), 0)),
                   pl.BlockSpec((1, 512), lambda g, c: (0, g)),
                   pl.BlockSpec((None, 8, 128), lambda g, c: (g, 0, 0))],
        out_shape=[jax.ShapeDtypeStruct((T, D_INNER), F32), jax.ShapeDtypeStruct((T, GN), F32),
                   jax.ShapeDtypeStruct((T, GN), F32), jax.ShapeDtypeStruct((T, D_INNER), BF16),
                   jax.ShapeDtypeStruct((SSM_GROUPS, T, 128), F32), jax.ShapeDtypeStruct((1, D_INNER), F32),
                   jax.ShapeDtypeStruct((SSM_GROUPS, 8, 128), F32)],
        scratch_shapes=[pltpu.VMEM((4, 128, 128), F32)],
        compiler_params=_cparams(("parallel", "arbitrary")))(xbc_c, xbc_c, xbc_c, dtg, par, zx, gnw, y, st, dyn)


SB_KEYS = 512


def _hi_lo(v):
    hi = v.astype(BF16)
    return jnp.concatenate([hi, (v - hi.astype(F32)).astype(BF16)], axis=1)


def _tri2(cond):
    kk = lax.broadcasted_iota(jnp.int32, (2 * SB_BLOCK, SB_BLOCK), 0)
    kk = jnp.where(kk >= SB_BLOCK, kk - SB_BLOCK, kk)
    jj = lax.broadcasted_iota(jnp.int32, (2 * SB_BLOCK, SB_BLOCK), 1)
    return cond(kk, jj).astype(BF16)


def _sba_diag_mask():
    Bq = SB_BLOCK
    rowi = lax.broadcasted_iota(jnp.int32, (2 * Bq, Bq), 0)
    return lax.broadcasted_iota(jnp.int32, (2 * Bq, Bq), 1) < jnp.where(rowi >= Bq, rowi - Bq, rowi)


def _sba_sub_fwd(zb, c, U2, mask):
    s = _softplus(zb)
    l = -s if mask is None else jnp.where(mask, -s, 0.0)
    A = jnp.exp(zb - s + (c + jnp.dot(_hi_lo(l), U2, preferred_element_type=F32)))
    if mask is not None:
        A = jnp.where(mask, A, 0.0)
    return A.astype(BF16), c + jnp.sum(l, axis=1, keepdims=True)


def _sba_sub_bwd(zb, dAb, Lt, pc, pe, Uincl, Uexcl, mask):
    s = _softplus(zb)
    l = -s if mask is None else jnp.where(mask, -s, 0.0)
    P = pc + jnp.dot(_hi_lo(l), Uincl, preferred_element_type=F32)
    g = zb - s
    A = jnp.exp(g + (Lt - P))
    if mask is not None:
        A = jnp.where(mask, A, 0.0)
    E = dAb * A
    PE = pe + jnp.dot(_hi_lo(E), Uexcl, preferred_element_type=F32)
    dz = E - jnp.exp(g) * (E + PE)
    if mask is not None:
        dz = jnp.where(mask, dz, 0.0)
    return (A.astype(BF16), dz.astype(BF16), pc + jnp.sum(l, axis=1, keepdims=True),
            pe + jnp.sum(E, axis=1, keepdims=True))


def _stack_heads(v):
    lo = lax.broadcasted_iota(jnp.int32, v.shape, 1) < 64
    zero = jnp.zeros_like(v)
    return jnp.concatenate([jnp.where(lo, v, zero), jnp.where(lo, zero, v)], axis=0)


def _unstack_heads(v):
    lo = lax.broadcasted_iota(jnp.int32, (SB_BLOCK, 128), 1) < 64
    return jnp.where(lo, v[:SB_BLOCK], v[SB_BLOCK:])


def _sba_fwd(q, kv, *, name):
    T = q.shape[0]
    Bq = SB_BLOCK
    nsub = SB_KEYS // Bq
    assert T % SB_KEYS == 0
    scale = 1.0 / math.sqrt(SB_HEAD_DIM)

    def body(q_ref, k_ref, v_ref, o_ref, lt_ref):
        I = pl.program_id(1)
        U2 = _tri2(lambda k, j: k > j)
        dmask = _sba_diag_mask()
        qs = [_stack_heads(q_ref[a * Bq:(a + 1) * Bq, :] * scale) for a in range(nsub)]
        cs, accs = [], []
        for a in range(nsub):
            c = jnp.zeros((2 * Bq, 1), F32)
            acc = jnp.zeros((2 * Bq, 128), F32)
            for b in range(a, -1, -1):
                off = pl.multiple_of(I * SB_KEYS + b * Bq, Bq)
                zb = lax.dot_general(qs[a], k_ref[pl.ds(off, Bq), :], _NT, preferred_element_type=F32)
                A, c = _sba_sub_fwd(zb, c, U2, dmask if b == a else None)
                acc = acc + jnp.dot(A, v_ref[pl.ds(off, Bq), :], preferred_element_type=F32)
            cs.append(c)
            accs.append(acc)
        qs_all = jnp.concatenate(qs, axis=0)

        def step(n, carry):
            c, acc = carry
            off = pl.multiple_of((I - 1 - n) * SB_KEYS, SB_KEYS)
            z = lax.dot_general(qs_all, k_ref[pl.ds(off, SB_KEYS), :], _NT, preferred_element_type=F32)
            parts = [None] * nsub
            for b in reversed(range(nsub)):
                parts[b], c = _sba_sub_fwd(z[:, b * Bq:(b + 1) * Bq], c, U2, None)
            return c, acc + jnp.dot(jnp.concatenate(parts, axis=1), v_ref[pl.ds(off, SB_KEYS), :],
                                    preferred_element_type=F32)

        c, acc = lax.fori_loop(0, I, step, (jnp.concatenate(cs, axis=0), jnp.concatenate(accs, axis=0)))
        for a in range(nsub):
            rows = slice(2 * a * Bq, 2 * (a + 1) * Bq)
            o_ref[a * Bq:(a + 1) * Bq, :] = _unstack_heads(acc[rows]).astype(BF16)
            lt_ref[a * Bq:(a + 1) * Bq, :] = _unstack_heads(jnp.broadcast_to(c[rows], (2 * Bq, 128)))

    return pl.pallas_call(
        body, name=name, grid=(SB_HEADS // 2, T // SB_KEYS),
        in_specs=[pl.BlockSpec((SB_KEYS, 128), lambda p, i: (i, p)), pl.BlockSpec((T, 128), lambda p, i: (0, p)),
                  pl.BlockSpec((T, 128), lambda p, i: (0, p + SB_HEADS // 2))],
        out_specs=[pl.BlockSpec((SB_KEYS, 128), lambda p, i: (i, p)),
                   pl.BlockSpec((None, SB_KEYS, 128), lambda p, i: (p, i, 0))],
        out_shape=[jax.ShapeDtypeStruct((T, D_MODEL), BF16), jax.ShapeDtypeStruct((SB_HEADS // 2, T, 128), F32)],
        compiler_params=_cparams(("parallel", "parallel")))(q, kv, kv)


def _sba_bwd(q, kv, lt, do, *, name):
    T = q.shape[0]
    Bq = SB_BLOCK
    nq = T // SB_KEYS
    nsub = SB_KEYS // Bq
    assert T % SB_KEYS == 0
    scale = 1.0 / math.sqrt(SB_HEAD_DIM)

    def body(q_ref, k_ref, v_ref, lt_ref, do_ref, dq_ref, dk_ref, dv_ref, dk_acc, dv_acc):
        i = pl.program_id(1)

        @pl.when(i == 0)
        def _():
            dk_acc[...] = jnp.zeros_like(dk_acc)
            dv_acc[...] = jnp.zeros_like(dv_acc)

        Uincl = _tri2(lambda k, j: k <= j)
        Uexcl = _tri2(lambda k, j: k < j)
        dmask = _sba_diag_mask()
        qs, dos, lts = [], [], []
        for a in range(nsub):
            rows = slice(a * Bq, (a + 1) * Bq)
            qs.append(_stack_heads(q_ref[rows, :] * scale))
            dos.append(_stack_heads(do_ref[rows, :]))
            lts.append(jnp.concatenate([lt_ref[rows, 0:1], lt_ref[rows, 64:65]], axis=0))
        qs_all = jnp.concatenate(qs, axis=0)
        dos_all = jnp.concatenate(dos, axis=0)
        lt_all = jnp.concatenate(lts, axis=0)

        def step(J, carry):
            pc, pe, dq_acc = carry
            off = pl.multiple_of(J * SB_KEYS, SB_KEYS)
            kb = k_ref[pl.ds(off, SB_KEYS), :]
            z = lax.dot_general(qs_all, kb, _NT, preferred_element_type=F32)
            dA = lax.dot_general(dos_all, v_ref[pl.ds(off, SB_KEYS), :], _NT, preferred_element_type=F32)
            a_parts, dz_parts = [], []
            for b in range(nsub):
                cols = slice(b * Bq, (b + 1) * Bq)
                A, dz, pc, pe = _sba_sub_bwd(z[:, cols], dA[:, cols], lt_all, pc, pe, Uincl, Uexcl, None)
                a_parts.append(A)
                dz_parts.append(dz)
            dzt = jnp.concatenate(dz_parts, axis=1)
            dk_acc[pl.ds(off, SB_KEYS), :] += lax.dot_general(dzt, qs_all, _TN, preferred_element_type=F32)
            dv_acc[pl.ds(off, SB_KEYS), :] += lax.dot_general(jnp.concatenate(a_parts, axis=1), dos_all, _TN,
                                                              preferred_element_type=F32)
            return pc, pe, dq_acc + jnp.dot(dzt, kb, preferred_element_type=F32)

        zc = jnp.zeros((2 * nsub * Bq, 1), F32)
        pc, pe, dq_acc = lax.fori_loop(0, i, step, (zc, zc, jnp.zeros((2 * nsub * Bq, 128), F32)))
        for a in range(nsub):
            rows = slice(2 * a * Bq, 2 * (a + 1) * Bq)
            pca, pea, dqa = pc[rows], pe[rows], dq_acc[rows]
            for b in range(a + 1):
                off = pl.multiple_of(i * SB_KEYS + b * Bq, Bq)
                kb = k_ref[pl.ds(off, Bq), :]
                zb = lax.dot_general(qs[a], kb, _NT, preferred_element_type=F32)
                dAb = lax.dot_general(dos[a], v_ref[pl.ds(off, Bq), :], _NT, preferred_element_type=F32)
                A, dz, pca, pea = _sba_sub_bwd(zb, dAb, lts[a], pca, pea, Uincl, Uexcl, dmask if b == a else None)
                dqa = dqa + jnp.dot(dz, kb, preferred_element_type=F32)
                dk_acc[pl.ds(off, Bq), :] += lax.dot_general(dz, qs[a], _TN, preferred_element_type=F32)
                dv_acc[pl.ds(off, Bq), :] += lax.dot_general(A, dos[a], _TN, preferred_element_type=F32)
            dq_ref[a * Bq:(a + 1) * Bq, :] = (_unstack_heads(dqa) * scale).astype(BF16)

        @pl.when(i == nq - 1)
        def _():
            dk_ref[...] = dk_acc[...].astype(BF16)
            dv_ref[...] = dv_acc[...].astype(BF16)

    return pl.pallas_call(
        body, name=name, grid=(SB_HEADS // 2, nq),
        in_specs=[pl.BlockSpec((SB_KEYS, 128), lambda p, i: (i, p)), pl.BlockSpec((T, 128), lambda p, i: (0, p)),
                  pl.BlockSpec((T, 128), lambda p, i: (0, p + SB_HEADS // 2)),
                  pl.BlockSpec((None, SB_KEYS, 128), lambda p, i: (p, i, 0)),
                  pl.BlockSpec((SB_KEYS, 128), lambda p, i: (i, p))],
        out_specs=[pl.BlockSpec((SB_KEYS, 128), lambda p, i: (i, p)), pl.BlockSpec((T, 128), lambda p, i: (0, p)),
                   pl.BlockSpec((T, 128), lambda p, i: (0, p))],
        out_shape=[jax.ShapeDtypeStruct((T, D_MODEL), BF16), jax.ShapeDtypeStruct((T, D_MODEL), BF16),
                   jax.ShapeDtypeStruct((T, D_MODEL), BF16)],
        scratch_shapes=[pltpu.VMEM((T, 128), F32), pltpu.VMEM((T, 128), F32)],
        compiler_params=_cparams(("parallel", "arbitrary")))(q, kv, kv, lt, do)


def _loss_head(h, tgt, w, *, name, tt=512):
    T, D = h.shape
    tt = min(tt, T)

    def body(h_ref, t_ref, w_ref, loss_ref, dh_ref, dw_ref):
        i = pl.program_id(0)
        hv = h_ref[...]
        wv = w_ref[...]
        r = lax.rsqrt(jnp.mean(hv * hv, axis=-1, keepdims=True) + EPS)
        xhat = hv * r
        err = xhat * wv - t_ref[...]
        part = 0.5 * jnp.sum(jnp.mean(err * err, axis=-1, keepdims=True), axis=0, keepdims=True)
        dy = err * (1.0 / D)
        dxh = dy * wv
        dh_ref[...] = r * (dxh - xhat * jnp.mean(dxh * xhat, axis=-1, keepdims=True))
        dwc = jnp.sum(dy * xhat, axis=0, keepdims=True)

        @pl.when(i == 0)
        def _():
            loss_ref[...] = jnp.broadcast_to(part, loss_ref.shape)
            dw_ref[...] = dwc

        @pl.when(i > 0)
        def _():
            loss_ref[...] += jnp.broadcast_to(part, loss_ref.shape)
            dw_ref[...] += dwc

    return pl.pallas_call(
        body, name=name, grid=(T // tt,),
        in_specs=[pl.BlockSpec((tt, D), lambda i: (i, 0)), pl.BlockSpec((tt, D), lambda i: (i, 0)),
                  pl.BlockSpec((1, D), lambda i: (0, 0))],
        out_specs=[pl.BlockSpec((1, 128), lambda i: (0, 0)), pl.BlockSpec((tt, D), lambda i: (i, 0)),
                   pl.BlockSpec((1, D), lambda i: (0, 0))],
        out_shape=[jax.ShapeDtypeStruct((1, 128), F32), jax.ShapeDtypeStruct((T, D), F32),
                   jax.ShapeDtypeStruct((1, D), F32)],
        compiler_params=_cparams(("arbitrary",)))(h, tgt, w.reshape(1, D))


def _adamw(parts, w, m, v, *, name, tr=256):
    P, R, C = parts.shape
    tr = min(tr, R)
    assert R % tr == 0, (name, R, tr)
    c1 = 1.0 - ADAM_B1 ** ADAM_STEP
    c2 = 1.0 - ADAM_B2 ** ADAM_STEP

    def body(p_ref, w_ref, m_ref, v_ref, g_ref, d_ref, nm_ref, nv_ref):
        g = p_ref[0].astype(F32)
        for k in range(1, P):
            g = g + p_ref[k].astype(F32)
        mn = ADAM_B1 * m_ref[...] + (1.0 - ADAM_B1) * g
        vn = ADAM_B2 * v_ref[...] + (1.0 - ADAM_B2) * (g * g)
        g_ref[...] = g
        nm_ref[...] = mn
        nv_ref[...] = vn
        d_ref[...] = -ADAM_LR * ((mn / c1) / (jnp.sqrt(vn / c2) + ADAM_EPS) + ADAM_WD * w_ref[...])

    spec = pl.BlockSpec((tr, C), lambda i: (i, 0))
    sds = jax.ShapeDtypeStruct((R, C), F32)
    return pl.pallas_call(
        body, name=name, grid=(R // tr,),
        in_specs=[pl.BlockSpec((P, tr, C), lambda i: (0, i, 0)), spec, spec, spec],
        out_specs=[spec, spec, spec, spec], out_shape=[sds, sds, sds, sds],
        compiler_params=_cparams(("parallel",)))(parts, w, m, v)


def _all_gather(shards, *, name):
    n = len(shards)

    def body(*refs):
        ins, outs = refs[:n], refs[n:2 * n]
        send_sems, recv_sems, local_sems = refs[2 * n:]
        x, y, c = lax.axis_index("x"), lax.axis_index("y"), lax.axis_index("c")
        me, sib = (x, y, c), (x, y, 1 - c)
        chips = [(1 - x, y), (x, 1 - y), (1 - x, 1 - y)]

        def slot(p):
            return 4 * p[0] + 2 * p[1] + p[2]

        def cp(a, k, block, to, src=None):
            dst = outs[a].at[slot(block)]
            return pltpu.make_async_remote_copy(src_ref=dst if src is None else src, dst_ref=dst,
                                                send_sem=send_sems.at[a, k], recv_sem=recv_sems.at[a, k],
                                                device_id=to, device_id_type=_MESH)

        mine = [pltpu.make_async_copy(ins[a], outs[a].at[slot(me)], local_sems.at[a]) for a in range(n)]
        for m in mine:
            m.start()
        first = []
        for a in range(n):
            first.append(cp(a, 0, me, sib, src=ins[a]))
            for j, chip in enumerate(chips):
                first.append(cp(a, 1 + j, me, (*chip, c), src=ins[a]))
        for f in first:
            f.start()
        passed = []
        for j, chip in enumerate(chips):
            for a in range(n):
                cp(a, 1 + j, (*chip, c), me).wait_recv()
                f = cp(a, 4 + j, (*chip, c), sib)
                f.start()
                passed.append(f)
        for a in range(n):
            cp(a, 0, sib, me).wait_recv()
            for j, chip in enumerate(chips):
                cp(a, 4 + j, (*chip, 1 - c), me).wait_recv()
        for f in first + passed:
            f.wait_send()
        for m in mine:
            m.wait()

    return pl.pallas_call(
        body, name=name, in_specs=[_ANY] * n, out_specs=[_ANY] * n,
        out_shape=[jax.ShapeDtypeStruct((N_DEV,) + s.shape, s.dtype) for s in shards],
        scratch_shapes=[pltpu.SemaphoreType.DMA((n, 7)), pltpu.SemaphoreType.DMA((n, 7)),
                        pltpu.SemaphoreType.DMA((n,))])(*shards)


def _exchange(blocks, *, name):
    n = len(blocks)

    def body(*refs):
        ins, outs = refs[:n], refs[n:2 * n]
        send_sems, recv_sems, local_sems = refs[2 * n:]
        x, y, c = lax.axis_index("x"), lax.axis_index("y"), lax.axis_index("c")
        me = 4 * x + 2 * y + c
        mine = [pltpu.make_async_copy(ins[a].at[me], outs[a].at[me], local_sems.at[a]) for a in range(n)]
        for m in mine:
            m.start()
        copies = []
        for r in range(1, N_DEV):
            rx, ry, rc = (r >> 2) & 1, (r >> 1) & 1, r & 1
            px, py, pc = (1 - x if rx else x), (1 - y if ry else y), (1 - c if rc else c)
            peer = 4 * px + 2 * py + pc
            for a in range(n):
                copies.append((pltpu.make_async_remote_copy(
                    src_ref=ins[a].at[peer], dst_ref=outs[a].at[me], send_sem=send_sems.at[a, r - 1],
                    recv_sem=recv_sems.at[a, r - 1], device_id=(px, py, pc), device_id_type=_MESH),
                    pltpu.make_async_remote_copy(
                    src_ref=ins[a].at[peer], dst_ref=outs[a].at[peer], send_sem=send_sems.at[a, r - 1],
                    recv_sem=recv_sems.at[a, r - 1], device_id=(px, py, pc), device_id_type=_MESH)))
        for snd, _ in copies:
            snd.start()
        for _, rcv in copies:
            rcv.wait_recv()
        for snd, _ in copies:
            snd.wait_send()
        for m in mine:
            m.wait()

    return pl.pallas_call(
        body, name=name, in_specs=[_ANY] * n, out_specs=[_ANY] * n,
        out_shape=[jax.ShapeDtypeStruct(b.shape, b.dtype) for b in blocks],
        scratch_shapes=[pltpu.SemaphoreType.DMA((n, 7)), pltpu.SemaphoreType.DMA((n, 7)),
                        pltpu.SemaphoreType.DMA((n,))])(*blocks)


_HBM = pl.BlockSpec(memory_space=pltpu.HBM)
_SEM = pl.BlockSpec(memory_space=pltpu.SEMAPHORE)
_EFFECT = pltpu.SideEffectType.DATAFLOW_SIDE_EFFECTING


def _peers():
    x, y, c = lax.axis_index("x"), lax.axis_index("y"), lax.axis_index("c")
    out = []
    for r in range(1, N_DEV):
        px = 1 - x if (r >> 2) & 1 else x
        py = 1 - y if (r >> 1) & 1 else y
        pc = 1 - c if r & 1 else c
        out.append(((px, py, pc), 4 * px + 2 * py + pc))
    return 4 * x + 2 * y + c, out


def _push_copy(src_ref, land_ref, send_sems, recv_sems, a, k, me, peer, peer_slot, scatter, arriving):
    src = src_ref.at[peer_slot] if scatter else src_ref
    return pltpu.make_async_remote_copy(
        src_ref=src, dst_ref=land_ref.at[peer_slot if arriving else me], send_sem=send_sems.at[a * (N_DEV - 1) + k],
        recv_sem=recv_sems.at[a * (N_DEV - 1) + k], device_id=peer, device_id_type=_MESH)


def _push_start(srcs, *, scatter, name):
    n = len(srcs)
    lands = [lax.empty(s.shape if scatter else (N_DEV,) + s.shape, s.dtype) for s in srcs]

    def body(*refs):
        src_refs, land_refs = refs[:n], refs[n:2 * n]
        send_sems, recv_sems = refs[2 * n], refs[2 * n + 1]
        token = refs[-1]
        me, peers = _peers()
        for k, (peer, slot) in enumerate(peers):
            for a in range(n):
                _push_copy(src_refs[a], land_refs[a], send_sems, recv_sems, a, k, me, peer, slot, scatter, False).start()
        token[...] = jnp.zeros_like(token)

    hbm = lambda a: pltpu.HBM(a.shape, a.dtype)
    outs = pl.pallas_call(
        body, name=name,
        out_shape=(pltpu.SemaphoreType.DMA((n * (N_DEV - 1),)), pltpu.SemaphoreType.DMA((n * (N_DEV - 1),)),
                   *[hbm(s) for s in srcs], *[hbm(l) for l in lands], jax.ShapeDtypeStruct((8, 128), F32)),
        in_specs=[_HBM] * (2 * n),
        out_specs=(_SEM, _SEM, *([_HBM] * (2 * n)), pl.BlockSpec(memory_space=pltpu.VMEM)),
        input_output_aliases={i: 2 + i for i in range(2 * n)},
        compiler_params=pltpu.CompilerParams(has_side_effects=_EFFECT),
    )(*[pltpu.with_memory_space_constraint(s, pltpu.HBM) for s in srcs],
      *[pltpu.with_memory_space_constraint(l, pltpu.HBM) for l in lands])
    return dict(send=outs[0], recv=outs[1], srcs=list(outs[2:2 + n]), lands=list(outs[2 + n:2 + 2 * n]),
                token=outs[-1], scatter=scatter, n=n)


def _push_wait(h, after, *, name):
    n, scatter = h["n"], h["scatter"]

    def body(*refs):
        src_refs, land_refs = refs[:n], refs[n:2 * n]
        send_sems, recv_sems = refs[2 * n], refs[2 * n + 1]
        me, peers = _peers()
        for k, (peer, slot) in enumerate(peers):
            for a in range(n):
                cp = _push_copy(src_refs[a], land_refs[a], send_sems, recv_sems, a, k, me, peer, slot, scatter, True)
                cp.wait_send()
                cp.wait_recv()

    hbm = lambda a: pltpu.HBM(a.shape, a.dtype)
    outs = pl.pallas_call(
        body, name=name,
        out_shape=(*[hbm(s) for s in h["srcs"]], *[hbm(l) for l in h["lands"]]),
        in_specs=[_HBM] * (2 * n) + [_SEM, _SEM, _ANY], out_specs=tuple([_HBM] * (2 * n)),
        input_output_aliases={i: i for i in range(2 * n)},
        compiler_params=pltpu.CompilerParams(has_side_effects=_EFFECT),
    )(*h["srcs"], *h["lands"], h["send"], h["recv"], after)
    return list(outs[:n]), list(outs[n:])


def _ffn_fwd(h, nw, w_up, conv_w, conv_b, w_down, tag):
    a = _mm_fwd(h, w_up, norm_w=nw, name=f"ffn{tag}_up", tm=1024, tn=1408)
    p = _ffn_conv_fwd(a, conv_w, conv_b.reshape(1, -1), name=f"ffn{tag}_conv")
    h_out = _mm_fwd(p, w_down, residual=h, name=f"ffn{tag}_down", tm=1024, tn=512)
    return h_out, (a, p)


def _ffn_bwd(dh, h, saved, nw, w_up, conv_w, conv_b, w_down, tag):
    a, p = saved
    g_down = _mm_tn(p, dh, name=f"ffn{tag}_down_wg", tk1=1408, tn=1024)
    dp = _mm_nt(dh, w_down, name=f"ffn{tag}_down_dg", out_dtype=BF16, tm=1024, tn=1408, tk=1024)
    dhid, g_cw, g_cb = _ffn_conv_bwd_pre(a, conv_w, conv_b.reshape(1, -1), dp, name=f"ffn{tag}_conv_bwd")
    da = _conv_bwd_in(dhid, conv_w, K=FFN_CONV, name=f"ffn{tag}_conv_bwd_in", tt=256, tc=1408)
    g_up = _mm_tn(h, da, norm_w=nw, name=f"ffn{tag}_up_wg", tn=1408)
    dh_out, g_nw = _mm_nt(da, w_up, epi=(h, nw, dh), name=f"ffn{tag}_up_dg", tk=1408)
    return dh_out, dict(norm=g_nw.reshape(-1), up=g_up, conv_w=g_cw, conv_b=g_cb.reshape(-1), down=g_down)


def _local_step(x, tgt, W):
    T = x.shape[0]
    f = {}
    zx = _mm_fwd(x, W["in_w"], norm_w=W["ssm_norm_w"], name="ssm_in", tm=1024, tn=896)
    xbc_c = _ssm_conv_fwd(zx, W["ssm_conv_w"], W["ssm_conv_b"].reshape(1, -1), name="ssm_conv")
    dt_raw = zx[:, D_INNER + CONV_DIM:IN_PROJ_DIM]
    dtg = jnp.pad(dt_raw.reshape(T, SSM_GROUPS, 8).transpose(1, 0, 2), ((0, 0), (0, 0), (0, 120)))
    par = jnp.stack([W["ssm_dt_bias"].reshape(SSM_GROUPS, 8), W["ssm_a_log"].reshape(SSM_GROUPS, 8),
                     W["ssm_d"].reshape(SSM_GROUPS, 8)], axis=1)
    par = jnp.pad(par, ((0, 0), (0, 5), (0, 120)))
    gnw = W["ssm_gate_norm_w"].reshape(1, D_INNER)
    y, yn, st = _ssd_fwd(xbc_c, zx, dtg, par, gnw, name="ssd_fwd")
    h1 = _mm_fwd(yn, W["ssm_out_w"], residual=x, name="ssm_out", tm=1024, tn=512)
    h2, ffn0 = _ffn_fwd(h1, W["ffn_norm_w"][0], W["ffn_up_w"][0], W["ffn_conv_w"][0], W["ffn_conv_b"][0],
                        W["ffn_down_w"][0], "0")
    q = _mm_fwd(h2, W["w_q"], norm_w=W["attn_norm_w"], out_dtype=BF16, name="attn_q", tm=1024, tn=1024)
    kv = _mm_fwd(h2, W["w_kv"], norm_w=W["kv_norm_w"], out_dtype=BF16, name="attn_kv", tm=1024, tn=1024)
    o, lt = _sba_fwd(q, kv, name="sba_fwd")
    h3 = _mm_fwd(o, W["w_o"], residual=h2, name="attn_o", tm=1024, tn=512)
    h4, ffn1 = _ffn_fwd(h3, W["ffn_norm_w"][1], W["ffn_up_w"][1], W["ffn_conv_w"][1], W["ffn_conv_b"][1],
                        W["ffn_down_w"][1], "1")
    loss, dh4, g_final = _loss_head(h4, tgt, W["final_norm_w"], name="loss_head")
    dh3, gf1 = _ffn_bwd(dh4, h3, ffn1, W["ffn_norm_w"][1], W["ffn_up_w"][1], W["ffn_conv_w"][1], W["ffn_conv_b"][1],
                        W["ffn_down_w"][1], "1")
    g_wo = _mm_tn(o, dh3, name="attn_o_wg", tn=1024)
    do = _mm_nt(dh3, W["w_o"], name="attn_o_dg", out_dtype=BF16, tn=1024, tk=1024)
    dq, dk, dv = _sba_bwd(q, kv, lt, do, name="sba_bwd")
    g_wq = _mm_tn(h2, dq, norm_w=W["attn_norm_w"], name="attn_q_wg", tn=1024)
    dh2a, g_attn_nw = _mm_nt(dq, W["w_q"], epi=(h2, W["attn_norm_w"], dh3), name="attn_q_dg", tk=1024)
    dkv = jnp.concatenate([dk, dv], axis=1)
    g_wkv = _mm_tn(h2, dkv, norm_w=W["kv_norm_w"], name="attn_kv_wg", tn=1024)
    dh2, g_kv_nw = _mm_nt(dkv, W["w_kv"], epi=(h2, W["kv_norm_w"], dh2a), name="attn_kv_dg", tk=1024)
    dh1, gf0 = _ffn_bwd(dh2, h1, ffn0, W["ffn_norm_w"][0], W["ffn_up_w"][0], W["ffn_conv_w"][0], W["ffn_conv_b"][0],
                        W["ffn_down_w"][0], "0")
    g_out = _mm_tn(yn, dh1, name="ssm_out_wg", tn=1024)
    dyn = _mm_nt(dh1, W["ssm_out_w"], name="ssm_out_dg", out_dtype=BF16, tn=1024, tk=1024)
    dxs, dB, dC, dz, ddt, g_gnw, dpar = _ssd_bwd(xbc_c, zx, dtg, par, gnw, y, st, dyn, name="ssd_bwd")
    dxbc_c = jnp.concatenate([dxs, dB, dC], axis=1)
    dhid, g_scw, g_scb = _ssm_conv_bwd_pre(zx, W["ssm_conv_w"], W["ssm_conv_b"].reshape(1, -1), dxbc_c,
                                           name="ssm_conv_bwd")
    dxbc = _conv_bwd_in(dhid, W["ssm_conv_w"], K=SSM_CONV, name="ssm_conv_bwd_in")
    ddt_t = ddt[:, :, :8].transpose(1, 0, 2).reshape(T, SSM_HEADS).astype(BF16)
    dzx = jnp.concatenate([dz, dxbc, jnp.pad(ddt_t, ((0, 0), (0, IN_PROJ_PAD - IN_PROJ_DIM)))], axis=1)
    g_in = _mm_tn(x, dzx, norm_w=W["ssm_norm_w"], name="ssm_in_wg", tn=896)
    dx, g_ssm_nw = _mm_nt(dzx, W["in_w"], epi=(x, W["ssm_norm_w"], dh1), name="ssm_in_dg", tk=1792)
    f["ssm_norm_w"] = g_ssm_nw.reshape(-1)
    f["ssm_in_w"] = g_in[:, :IN_PROJ_DIM]
    f["ssm_conv_w"] = g_scw
    f["ssm_conv_b"] = g_scb.reshape(-1)
    f["ssm_dt_bias"] = dpar[:, 0, :8].reshape(-1)
    f["ssm_a_log"] = dpar[:, 1, :8].reshape(-1)
    f["ssm_d"] = dpar[:, 2, :8].reshape(-1)
    f["ssm_gate_norm_w"] = g_gnw.reshape(-1)
    f["ssm_out_w"] = g_out
    f["kv_norm_w"] = g_kv_nw.reshape(-1)
    f["w_k"] = g_wkv[:, :D_MODEL]
    f["w_v"] = g_wkv[:, D_MODEL:]
    f["attn_norm_w"] = g_attn_nw.reshape(-1)
    f["w_q"] = g_wq
    f["w_o"] = g_wo
    f["ffn_norm_w"] = jnp.stack([gf0["norm"], gf1["norm"]])
    f["ffn_up_w"] = [gf0["up"], gf1["up"]]
    f["ffn_conv_w"] = jnp.stack([gf0["conv_w"], gf1["conv_w"]])
    f["ffn_conv_b"] = jnp.stack([gf0["conv_b"], gf1["conv_b"]])
    f["ffn_down_w"] = [gf0["down"], gf1["down"]]
    f["final_norm_w"] = g_final.reshape(-1)
    return loss, dx, f


_BIG = ["ssm_in_w", "ssm_out_w", "w_k", "w_v", "w_q", "w_o", "ffn_up_w", "ffn_down_w"]
_SMALL_SHARDED = ["ssm_norm_w", "ssm_conv_w", "ssm_conv_b", "ssm_gate_norm_w", "ffn_conv_w"]
_SMALL_REPL = ["ssm_dt_bias", "ssm_a_log", "ssm_d", "kv_norm_w", "attn_norm_w", "ffn_norm_w", "ffn_conv_b",
               "final_norm_w"]
_WEIGHTS = ["ssm_norm_w", "ssm_in_w", "ssm_conv_w", "ssm_conv_b", "ssm_dt_bias", "ssm_a_log", "ssm_d",
            "ssm_gate_norm_w", "ssm_out_w", "kv_norm_w", "w_k", "w_v", "attn_norm_w", "w_q", "w_o", "ffn_norm_w",
            "ffn_up_w", "ffn_conv_w", "ffn_conv_b", "ffn_down_w", "final_norm_w"]


def _as2d(a):
    return a.reshape(-1, a.shape[-1])


def _cols_to_full(g):
    return g.transpose(1, 0, 2).reshape(g.shape[1], N_DEV * g.shape[2])


def _full_to_cols(a):
    R = a.shape[0]
    return a.reshape(R, N_DEV, -1).transpose(1, 0, 2)


def _gather_weights(p):
    names = _BIG + _SMALL_SHARDED
    shards = [_as2d(p[n]).astype(BF16) for n in _BIG] + [_as2d(p[n]) for n in _SMALL_SHARDED]
    got = dict(zip(names, _all_gather(shards, name="gather_weights")))
    W = {n: p[n] for n in _SMALL_REPL}
    in_w = _cols_to_full(got["ssm_in_w"])
    W["in_w"] = jnp.pad(in_w, ((0, 0), (0, IN_PROJ_PAD - IN_PROJ_DIM)))
    W["ssm_out_w"] = got["ssm_out_w"].reshape(D_INNER, D_MODEL)
    W["w_kv"] = jnp.concatenate([got["w_k"].reshape(D_MODEL, D_MODEL), got["w_v"].reshape(D_MODEL, D_MODEL)], axis=1)
    W["w_q"] = got["w_q"].reshape(D_MODEL, D_MODEL)
    W["w_o"] = got["w_o"].reshape(D_MODEL, D_MODEL)
    up = got["ffn_up_w"]
    W["ffn_up_w"] = [_cols_to_full(up[:, l * D_MODEL:(l + 1) * D_MODEL]) for l in range(2)]
    dn = got["ffn_down_w"]
    rs = D_FF // N_DEV
    W["ffn_down_w"] = [dn[:, l * rs:(l + 1) * rs].reshape(D_FF, D_MODEL) for l in range(2)]
    W["ssm_norm_w"] = got["ssm_norm_w"].reshape(D_MODEL)
    W["ssm_conv_w"] = _cols_to_full(got["ssm_conv_w"])
    W["ssm_conv_b"] = got["ssm_conv_b"].reshape(CONV_DIM)
    W["ssm_gate_norm_w"] = got["ssm_gate_norm_w"].reshape(D_INNER)
    fcw = _cols_to_full(got["ffn_conv_w"])
    W["ffn_conv_w"] = fcw.reshape(2, FFN_CONV, 2 * D_FF)
    for n in ("ssm_dt_bias", "ssm_a_log", "ssm_d", "attn_norm_w"):
        W[n] = W[n].reshape(-1)
    return W


def _big_grad_blocks(f):
    rs = D_FF // N_DEV
    return {
        "ssm_in_w": _full_to_cols(f["ssm_in_w"]),
        "ssm_out_w": f["ssm_out_w"].reshape(N_DEV, D_INNER // N_DEV, D_MODEL),
        "w_k": f["w_k"].reshape(N_DEV, D_MODEL // N_DEV, D_MODEL),
        "w_v": f["w_v"].reshape(N_DEV, D_MODEL // N_DEV, D_MODEL),
        "w_q": f["w_q"].reshape(N_DEV, D_MODEL // N_DEV, D_MODEL),
        "w_o": f["w_o"].reshape(N_DEV, D_MODEL // N_DEV, D_MODEL),
        "ffn_up_w": jnp.concatenate([_full_to_cols(g) for g in f["ffn_up_w"]], axis=1),
        "ffn_down_w": jnp.concatenate([g.reshape(N_DEV, rs, D_MODEL) for g in f["ffn_down_w"]], axis=1),
    }


def _pack_small(vals):
    flat = jnp.concatenate([v.reshape(-1).astype(F32) for v in vals])
    n = flat.shape[0]
    rows = -(-n // 1024) * 8
    return jnp.pad(flat, (0, rows * 128 - n)).reshape(rows, 128)


def _unpack_small(packed, shapes):
    flat = packed.reshape(-1)
    out, off = [], 0
    for s in shapes:
        n = math.prod(s)
        out.append(flat[off:off + n].reshape(s))
        off += n
    return out


def _kernel_v1(x, ssm_norm_w, ssm_in_w, ssm_conv_w, ssm_conv_b, ssm_dt_bias, ssm_a_log, ssm_d, ssm_gate_norm_w, ssm_out_w, kv_norm_w, w_k, w_v, attn_norm_w, w_q, w_o, ffn_norm_w, ffn_up_w, ffn_conv_w, ffn_conv_b, ffn_down_w, final_norm_w, loss_target, m_ssm_norm_w, m_ssm_in_w, m_ssm_conv_w, m_ssm_conv_b, m_ssm_dt_bias, m_ssm_a_log, m_ssm_d, m_ssm_gate_norm_w, m_ssm_out_w, m_kv_norm_w, m_w_k, m_w_v, m_attn_norm_w, m_w_q, m_w_o, m_ffn_norm_w, m_ffn_up_w, m_ffn_conv_w, m_ffn_conv_b, m_ffn_down_w, m_final_norm_w, v_ssm_norm_w, v_ssm_in_w, v_ssm_conv_w, v_ssm_conv_b, v_ssm_dt_bias, v_ssm_a_log, v_ssm_d, v_ssm_gate_norm_w, v_ssm_out_w, v_kv_norm_w, v_w_k, v_w_v, v_attn_norm_w, v_w_q, v_w_o, v_ffn_norm_w, v_ffn_up_w, v_ffn_conv_w, v_ffn_conv_b, v_ffn_down_w, v_final_norm_w):
    env = dict(locals())
    p = {n: env[n] for n in _WEIGHTS}
    mom = {n: env["m_" + n] for n in _WEIGHTS}
    var = {n: env["v_" + n] for n in _WEIGHTS}
    T = x.shape[1]
    me = 4 * lax.axis_index("x") + 2 * lax.axis_index("y") + lax.axis_index("c")

    W = _gather_weights(p)
    loss_row, dx, f = _local_step(x.reshape(T, D_MODEL), loss_target.reshape(T, D_MODEL), W)
    loss = lax.psum(loss_row[0, 0], ("x", "y", "c"))

    big = _big_grad_blocks(f)
    small_names = _SMALL_REPL + _SMALL_SHARDED
    small_full = _pack_small([f[n] for n in small_names])
    small_bcast = jnp.broadcast_to(small_full[None], (N_DEV,) + small_full.shape)
    got = _exchange([big[n] for n in _BIG] + [small_bcast], name="exchange_grads")
    big_parts = dict(zip(_BIG, got[:-1]))

    zero = jnp.zeros_like(small_full)
    g_small_sum = _adamw(got[-1], zero, zero, zero, name="sum_small_grads", tr=small_full.shape[0])[0]
    full_shapes = [f[n].shape for n in small_names]
    g_small = dict(zip(small_names, _unpack_small(g_small_sum, full_shapes)))
    for n in _SMALL_SHARDED:
        width = p[n].shape[-1]
        g_small[n] = lax.dynamic_slice_in_dim(g_small[n], me * width, width, axis=g_small[n].ndim - 1)

    out_g, out_d, out_m, out_v = {}, {}, {}, {}
    for n in _BIG:
        w2, m2, v2 = _as2d(p[n]), _as2d(mom[n]), _as2d(var[n])
        tr = 352 if n == "ffn_down_w" else 256
        g, d, nm, nv = _adamw(big_parts[n], w2, m2, v2, name="adamw_" + n, tr=tr)
        out_g[n], out_d[n], out_m[n], out_v[n] = (t.reshape(p[n].shape) for t in (g, d, nm, nv))
    sw = _pack_small([p[n] for n in small_names])
    sm = _pack_small([mom[n] for n in small_names])
    sv = _pack_small([var[n] for n in small_names])
    sg = _pack_small([g_small[n] for n in small_names])
    _, d, nm, nv = _adamw(sg[None], sw, sm, sv, name="adamw_small", tr=sw.shape[0])
    shard_shapes = [p[n].shape for n in small_names]
    for n, dd, mm, vv in zip(small_names, _unpack_small(d, shard_shapes), _unpack_small(nm, shard_shapes),
                             _unpack_small(nv, shard_shapes)):
        out_g[n] = g_small[n].reshape(p[n].shape)
        out_d[n], out_m[n], out_v[n] = dd, mm, vv

    return (loss, dx.reshape(x.shape), *[out_g[n] for n in _WEIGHTS], *[out_d[n] for n in _WEIGHTS],
            *[out_m[n] for n in _WEIGHTS], *[out_v[n] for n in _WEIGHTS])


def _tie(a, token):
    return a + token[0, 0].astype(a.dtype)


def _local_step2(x, tgt, get_w, put_g):
    T = x.shape[0]
    Ws = get_w("ssm", None)
    fnw, fcw, fcb = Ws["ffn_norm_w"], Ws["ffn_conv_w"], Ws["ffn_conv_b"]
    zx = _mm_fwd(x, Ws["in_w"], norm_w=Ws["ssm_norm_w"], name="ssm_in", tm=1024, tn=896)
    xbc_c = _ssm_conv_fwd(zx, Ws["ssm_conv_w"], Ws["ssm_conv_b"].reshape(1, -1), name="ssm_conv")
    dt_raw = zx[:, D_INNER + CONV_DIM:IN_PROJ_DIM]
    dtg = jnp.pad(dt_raw.reshape(T, SSM_GROUPS, 8).transpose(1, 0, 2), ((0, 0), (0, 0), (0, 120)))
    par = jnp.stack([Ws["ssm_dt_bias"].reshape(SSM_GROUPS, 8), Ws["ssm_a_log"].reshape(SSM_GROUPS, 8),
                     Ws["ssm_d"].reshape(SSM_GROUPS, 8)], axis=1)
    par = jnp.pad(par, ((0, 0), (0, 5), (0, 120)))
    gnw = Ws["ssm_gate_norm_w"].reshape(1, D_INNER)
    y, yn, st = _ssd_fwd(xbc_c, zx, dtg, par, gnw, name="ssd_fwd")
    h1 = _mm_fwd(yn, Ws["ssm_out_w"], residual=x, name="ssm_out", tm=1024, tn=512)
    W0 = get_w("ffn0", h1)
    h2, ffn0 = _ffn_fwd(h1, fnw[0], W0["up"], fcw[0], fcb[0], W0["down"], "0")
    Wr = get_w("rest", h2)
    q = _mm_fwd(h2, Wr["w_q"], norm_w=Ws["attn_norm_w"], out_dtype=BF16, name="attn_q", tm=1024, tn=1024)
    kv = _mm_fwd(h2, Wr["w_kv"], norm_w=Ws["kv_norm_w"], out_dtype=BF16, name="attn_kv", tm=1024, tn=1024)
    o, lt = _sba_fwd(q, kv, name="sba_fwd")
    h3 = _mm_fwd(o, Wr["w_o"], residual=h2, name="attn_o", tm=1024, tn=512)
    h4, ffn1 = _ffn_fwd(h3, fnw[1], Wr["up"], fcw[1], fcb[1], Wr["down"], "1")
    loss, dh4, g_final = _loss_head(h4, tgt, Ws["final_norm_w"], name="loss_head")
    dh3, gf1 = _ffn_bwd(dh4, h3, ffn1, fnw[1], Wr["up"], fcw[1], fcb[1], Wr["down"], "1")
    tok = put_g("ffn1", dict(up=gf1["up"], down=gf1["down"]))
    g_wo = _mm_tn(o, dh3, name="attn_o_wg", tn=1024)
    do = _mm_nt(dh3, _tie(Wr["w_o"], tok), name="attn_o_dg", out_dtype=BF16, tn=1024, tk=1024)
    dq, dk, dv = _sba_bwd(q, kv, lt, do, name="sba_bwd")
    g_wq = _mm_tn(h2, dq, norm_w=Ws["attn_norm_w"], name="attn_q_wg", tn=1024)
    dh2a, g_attn_nw = _mm_nt(dq, Wr["w_q"], epi=(h2, Ws["attn_norm_w"], dh3), name="attn_q_dg", tk=1024)
    dkv = jnp.concatenate([dk, dv], axis=1)
    g_wkv = _mm_tn(h2, dkv, norm_w=Ws["kv_norm_w"], name="attn_kv_wg", tn=1024)
    dh2, g_kv_nw = _mm_nt(dkv, Wr["w_kv"], epi=(h2, Ws["kv_norm_w"], dh2a), name="attn_kv_dg", tk=1024)
    tok = put_g("attn", dict(w_o=g_wo, w_q=g_wq, w_k=g_wkv[:, :D_MODEL], w_v=g_wkv[:, D_MODEL:]))
    dh1, gf0 = _ffn_bwd(dh2, h1, ffn0, fnw[0], W0["up"], fcw[0], _tie(fcb[0], tok), W0["down"], "0")
    tok = put_g("ffn0", dict(up=gf0["up"], down=gf0["down"]))
    g_out = _mm_tn(yn, dh1, name="ssm_out_wg", tn=1024)
    dyn = _mm_nt(dh1, _tie(Ws["ssm_out_w"], tok), name="ssm_out_dg", out_dtype=BF16, tn=1024, tk=1024)
    tok = put_g("ssm_out", dict(ssm_out_w=g_out))
    dxs, dB, dC, dz, ddt, g_gnw, dpar = _ssd_bwd(xbc_c, zx, dtg, par, _tie(gnw, tok), y, st, dyn, name="ssd_bwd")
    dxbc_c = jnp.concatenate([dxs, dB, dC], axis=1)
    dhid, g_scw, g_scb = _ssm_conv_bwd_pre(zx, Ws["ssm_conv_w"], Ws["ssm_conv_b"].reshape(1, -1), dxbc_c,
                                           name="ssm_conv_bwd")
    dxbc = _conv_bwd_in(dhid, Ws["ssm_conv_w"], K=SSM_CONV, name="ssm_conv_bwd_in")
    ddt_t = ddt[:, :, :8].transpose(1, 0, 2).reshape(T, SSM_HEADS).astype(BF16)
    dzx = jnp.concatenate([dz, dxbc, jnp.pad(ddt_t, ((0, 0), (0, IN_PROJ_PAD - IN_PROJ_DIM)))], axis=1)
    g_in = _mm_tn(x, dzx, norm_w=Ws["ssm_norm_w"], name="ssm_in_wg", tn=896)
    dx, g_ssm_nw = _mm_nt(dzx, Ws["in_w"], epi=(x, Ws["ssm_norm_w"], dh1), name="ssm_in_dg", tk=1792)
    f = {
        "ssm_norm_w": g_ssm_nw.reshape(-1), "ssm_in_w": g_in[:, :IN_PROJ_DIM], "ssm_conv_w": g_scw,
        "ssm_conv_b": g_scb.reshape(-1), "ssm_dt_bias": dpar[:, 0, :8].reshape(-1),
        "ssm_a_log": dpar[:, 1, :8].reshape(-1), "ssm_d": dpar[:, 2, :8].reshape(-1),
        "ssm_gate_norm_w": g_gnw.reshape(-1), "kv_norm_w": g_kv_nw.reshape(-1), "attn_norm_w": g_attn_nw.reshape(-1),
        "ffn_norm_w": jnp.stack([gf0["norm"], gf1["norm"]]), "ffn_conv_w": jnp.stack([gf0["conv_w"], gf1["conv_w"]]),
        "ffn_conv_b": jnp.stack([gf0["conv_b"], gf1["conv_b"]]), "final_norm_w": g_final.reshape(-1),
    }
    return loss, dx, f


def kernel(x, ssm_norm_w, ssm_in_w, ssm_conv_w, ssm_conv_b, ssm_dt_bias, ssm_a_log, ssm_d, ssm_gate_norm_w, ssm_out_w, kv_norm_w, w_k, w_v, attn_norm_w, w_q, w_o, ffn_norm_w, ffn_up_w, ffn_conv_w, ffn_conv_b, ffn_down_w, final_norm_w, loss_target, m_ssm_norm_w, m_ssm_in_w, m_ssm_conv_w, m_ssm_conv_b, m_ssm_dt_bias, m_ssm_a_log, m_ssm_d, m_ssm_gate_norm_w, m_ssm_out_w, m_kv_norm_w, m_w_k, m_w_v, m_attn_norm_w, m_w_q, m_w_o, m_ffn_norm_w, m_ffn_up_w, m_ffn_conv_w, m_ffn_conv_b, m_ffn_down_w, m_final_norm_w, v_ssm_norm_w, v_ssm_in_w, v_ssm_conv_w, v_ssm_conv_b, v_ssm_dt_bias, v_ssm_a_log, v_ssm_d, v_ssm_gate_norm_w, v_ssm_out_w, v_kv_norm_w, v_w_k, v_w_v, v_attn_norm_w, v_w_q, v_w_o, v_ffn_norm_w, v_ffn_up_w, v_ffn_conv_w, v_ffn_conv_b, v_ffn_down_w, v_final_norm_w):
    env = dict(locals())
    p = {n: env[n] for n in _WEIGHTS}
    mom = {n: env["m_" + n] for n in _WEIGHTS}
    var = {n: env["v_" + n] for n in _WEIGHTS}
    T = x.shape[1]
    me = 4 * lax.axis_index("x") + 2 * lax.axis_index("y") + lax.axis_index("c")
    rs = D_FF // N_DEV

    def bf2(a):
        return _as2d(a).astype(BF16)

    def with_own(srcs, lands, scatter):
        out = []
        for s, l in zip(srcs, lands):
            own = lax.dynamic_index_in_dim(s, me, 0, keepdims=False) if scatter else s
            out.append(lax.dynamic_update_index_in_dim(l, own, me, 0))
        return out

    a_names = ["ssm_in_w", "ssm_out_w"] + _SMALL_SHARDED
    got_a = dict(zip(a_names, _all_gather([bf2(p["ssm_in_w"]), bf2(p["ssm_out_w"])] + [_as2d(p[n]) for n in _SMALL_SHARDED],
                                          name="gather_ssm")))
    ffn0_names = ["up0", "down0"]
    rest_names = ["w_q", "w_k", "w_v", "w_o", "up1", "down1"]
    shard = {"up0": bf2(p["ffn_up_w"][0]), "down0": bf2(p["ffn_down_w"][0]), "up1": bf2(p["ffn_up_w"][1]),
             "down1": bf2(p["ffn_down_w"][1]), "w_q": bf2(p["w_q"]), "w_k": bf2(p["w_k"]), "w_v": bf2(p["w_v"]),
             "w_o": bf2(p["w_o"])}
    h_ffn0 = _push_start([shard[n] for n in ffn0_names], scatter=False, name="gather_ffn0_start")
    h_rest = _push_start([shard[n] for n in rest_names], scatter=False, name="gather_rest_start")

    def get_w(group, after):
        if group == "ssm":
            W = {n: p[n] for n in _SMALL_REPL}
            for n in ("ssm_dt_bias", "ssm_a_log", "ssm_d", "attn_norm_w"):
                W[n] = W[n].reshape(-1)
            W["in_w"] = jnp.pad(_cols_to_full(got_a["ssm_in_w"]), ((0, 0), (0, IN_PROJ_PAD - IN_PROJ_DIM)))
            W["ssm_out_w"] = got_a["ssm_out_w"].reshape(D_INNER, D_MODEL)
            W["ssm_norm_w"] = _tie(_tie(got_a["ssm_norm_w"].reshape(D_MODEL), h_ffn0["token"]), h_rest["token"])
            W["ssm_conv_w"] = _cols_to_full(got_a["ssm_conv_w"])
            W["ssm_conv_b"] = got_a["ssm_conv_b"].reshape(CONV_DIM)
            W["ssm_gate_norm_w"] = got_a["ssm_gate_norm_w"].reshape(D_INNER)
            W["ffn_conv_w"] = _cols_to_full(got_a["ffn_conv_w"]).reshape(2, FFN_CONV, 2 * D_FF)
            return W
        if group == "ffn0":
            srcs, lands = _push_wait(h_ffn0, after, name="gather_ffn0_wait")
            up, down = with_own(srcs, lands, False)
            return dict(up=_cols_to_full(up), down=down.reshape(D_FF, D_MODEL))
        srcs, lands = _push_wait(h_rest, after, name="gather_rest_wait")
        g = dict(zip(rest_names, with_own(srcs, lands, False)))
        sq = lambda a: a.reshape(D_MODEL, D_MODEL)
        return dict(w_q=sq(g["w_q"]), w_kv=jnp.concatenate([sq(g["w_k"]), sq(g["w_v"])], axis=1), w_o=sq(g["w_o"]),
                    up=_cols_to_full(g["up1"]), down=g["down1"].reshape(D_FF, D_MODEL))

    pending = []

    def put_g(group, g):
        if group in ("ffn0", "ffn1"):
            keys = [("ffn_up_w", int(group[-1])), ("ffn_down_w", int(group[-1]))]
            blocks = [_full_to_cols(g["up"]), g["down"].reshape(N_DEV, rs, D_MODEL)]
        elif group == "attn":
            keys = [(n, None) for n in ("w_o", "w_q", "w_k", "w_v")]
            blocks = [g[n].reshape(N_DEV, D_MODEL // N_DEV, D_MODEL) for n, _ in keys]
        else:
            keys = [("ssm_out_w", None)]
            blocks = [g["ssm_out_w"].reshape(N_DEV, D_INNER // N_DEV, D_MODEL)]
        h = _push_start(blocks, scatter=True, name=f"exchange_{group}_start")
        pending.append((group, keys, h))
        return h["token"]

    loss_row, dx, f = _local_step2(x.reshape(T, D_MODEL), loss_target.reshape(T, D_MODEL), get_w, put_g)
    loss = lax.psum(loss_row[0, 0], ("x", "y", "c"))

    small_names = _SMALL_REPL + _SMALL_SHARDED
    small_full = _pack_small([f[n] for n in small_names])
    small_bcast = jnp.broadcast_to(small_full[None], (N_DEV,) + small_full.shape)
    h = _push_start([_full_to_cols(f["ssm_in_w"]), small_bcast], scatter=True, name="exchange_last_start")
    pending.append(("last", [("ssm_in_w", None), ("small", None)], h))

    res = {}
    small_parts = None
    for group, keys, h in pending:
        srcs, lands = _push_wait(h, dx, name=f"exchange_{group}_wait")
        for (n, layer), parts in zip(keys, with_own(srcs, lands, True)):
            if n == "small":
                small_parts = parts
                continue
            sel = (lambda a: a) if layer is None else (lambda a: a[layer])
            w2, m2, v2 = _as2d(sel(p[n])), _as2d(sel(mom[n])), _as2d(sel(var[n]))
            tr = rs if n == "ffn_down_w" else 256
            res[(n, layer)] = _adamw(parts, w2, m2, v2, name=f"adamw_{n}" + ("" if layer is None else str(layer)), tr=tr)
    out_g, out_d, out_m, out_v = {}, {}, {}, {}
    for n in _BIG:
        if (n, None) in res:
            quad = res[(n, None)]
        else:
            quad = [jnp.stack([res[(n, 0)][k], res[(n, 1)][k]]) for k in range(4)]
        out_g[n], out_d[n], out_m[n], out_v[n] = (t.reshape(p[n].shape) for t in quad)

    zero = jnp.zeros_like(small_full)
    g_small_sum = _adamw(small_parts, zero, zero, zero, name="sum_small_grads", tr=small_full.shape[0])[0]
    g_small = dict(zip(small_names, _unpack_small(g_small_sum, [f[n].shape for n in small_names])))
    for n in _SMALL_SHARDED:
        width = p[n].shape[-1]
        g_small[n] = lax.dynamic_slice_in_dim(g_small[n], me * width, width, axis=g_small[n].ndim - 1)
    sw = _pack_small([p[n] for n in small_names])
    sm = _pack_small([mom[n] for n in small_names])
    sv = _pack_small([var[n] for n in small_names])
    sg = _pack_small([g_small[n] for n in small_names])
    _, d, nm, nv = _adamw(sg[None], sw, sm, sv, name="adamw_small", tr=sw.shape[0])
    shard_shapes = [p[n].shape for n in small_names]
    for n, dd, mm, vv in zip(small_names, _unpack_small(d, shard_shapes), _unpack_small(nm, shard_shapes),
                             _unpack_small(nv, shard_shapes)):
        out_g[n] = g_small[n].reshape(p[n].shape)
        out_d[n], out_m[n], out_v[n] = dd, mm, vv

    return (loss, dx.reshape(x.shape), *[out_g[n] for n in _WEIGHTS], *[out_d[n] for n in _WEIGHTS],
            *[out_m[n] for n in _WEIGHTS], *[out_v[n] for n in _WEIGHTS])
```

```python
import functools
import math

import jax
import jax.numpy as jnp
from jax import lax
from jax.experimental import pallas as pl
from jax.experimental.pallas import tpu as pltpu

F32 = jnp.float32
BF16 = jnp.bfloat16
EPS = 1e-6

D_MODEL = 1024
D_INNER = 2048
SSM_HEADS = 32
SSM_GROUPS = 4
SSM_STATE = 128
SSM_CONV = 4
SSM_CHUNK = 128
GN = SSM_GROUPS * SSM_STATE
CONV_DIM = D_INNER + 2 * GN
IN_PROJ_DIM = D_INNER + CONV_DIM + SSM_HEADS
IN_PROJ_PAD = 5376
SB_HEADS = 16
SB_HEAD_DIM = 64
SB_BLOCK = 128
D_FF = 2816
FFN_CONV = 3
N_DEV = 8

ADAM_LR = 0.001
ADAM_B1 = 0.9
ADAM_B2 = 0.999
ADAM_EPS = 1e-08
ADAM_WD = 0.01
ADAM_STEP = 10

_MESH = pl.DeviceIdType.MESH
_NT = (((1,), (1,)), ((), ()))
_TN = (((0,), (0,)), ((), ()))
_ANY = pl.BlockSpec(memory_space=pl.ANY)


def _cparams(sem, vmem_mb=48):
    return pltpu.CompilerParams(dimension_semantics=sem, vmem_limit_bytes=vmem_mb * 1024 * 1024)


def _sigmoid(x):
    return 1.0 / (1.0 + jnp.exp(-x))


def _softplus(x):
    return jnp.maximum(x, 0.0) + jnp.log(1.0 + jnp.exp(-jnp.abs(x)))


def _rms_fwd(xv, w):
    r = lax.rsqrt(jnp.mean(xv * xv, axis=-1, keepdims=True) + EPS)
    return xv * r * w


def _mm_fwd(x, w, *, name, norm_w=None, residual=None, out_dtype=F32, tm=512, tn=512):
    M, K = x.shape
    N = w.shape[1]
    tm, tn = min(tm, M), min(tn, N)
    assert M % tm == 0 and N % tn == 0, (name, M, N, tm, tn)
    has_norm, has_res = norm_w is not None, residual is not None

    def body(*refs):
        x_ref, w_ref = refs[0], refs[1]
        p = 2
        nw_ref = r_ref = None
        if has_norm:
            nw_ref = refs[p]
            p += 1
        if has_res:
            r_ref = refs[p]
            p += 1
        o_ref, xn_ref = refs[p], refs[p + 1]

        @pl.when(pl.program_id(1) == 0)
        def _():
            xv = x_ref[...].astype(F32)
            if has_norm:
                xv = _rms_fwd(xv, nw_ref[...])
            xn_ref[...] = xv.astype(BF16)

        acc = jnp.dot(xn_ref[...], w_ref[...], preferred_element_type=F32)
        if has_res:
            acc = acc + r_ref[...]
        o_ref[...] = acc.astype(out_dtype)

    in_specs = [pl.BlockSpec((tm, K), lambda i, j: (i, 0)), pl.BlockSpec((K, tn), lambda i, j: (0, j))]
    args = [x, w]
    if has_norm:
        in_specs.append(pl.BlockSpec((1, K), lambda i, j: (0, 0)))
        args.append(norm_w.reshape(1, K))
    if has_res:
        in_specs.append(pl.BlockSpec((tm, tn), lambda i, j: (i, j)))
        args.append(residual)
    return pl.pallas_call(
        body, name=name, grid=(M // tm, N // tn), in_specs=in_specs,
        out_specs=pl.BlockSpec((tm, tn), lambda i, j: (i, j)),
        out_shape=jax.ShapeDtypeStruct((M, N), out_dtype),
        scratch_shapes=[pltpu.VMEM((tm, K), BF16)],
        compiler_params=_cparams(("parallel", "arbitrary")))(*args)


def _mm_nt(dy, w, *, name, epi=None, out_dtype=F32, tm=512, tn=512, tk=512):
    M, K = dy.shape
    N = w.shape[0]
    tm, tk = min(tm, M), min(tk, K)
    tn = N if epi is not None else min(tn, N)
    assert M % tm == 0 and N % tn == 0 and K % tk == 0, (name, M, N, K, tm, tn, tk)
    nk = K // tk
    has_epi = epi is not None

    def body(*refs):
        if has_epi:
            dy_ref, w_ref, h_ref, nw_ref, r_ref, o_ref, dnw_ref, acc_ref = refs
        else:
            dy_ref, w_ref, o_ref, acc_ref = refs
        i = pl.program_id(0)
        k = pl.program_id(2)

        @pl.when(k == 0)
        def _():
            acc_ref[...] = jnp.zeros_like(acc_ref)

        acc_ref[...] += lax.dot_general(dy_ref[...].astype(BF16), w_ref[...], _NT, preferred_element_type=F32)

        @pl.when(k == nk - 1)
        def _():
            du = acc_ref[...]
            if has_epi:
                hv = h_ref[...]
                r = lax.rsqrt(jnp.mean(hv * hv, axis=-1, keepdims=True) + EPS)
                xhat = hv * r
                dxh = du * nw_ref[...]
                dx = r * (dxh - xhat * jnp.mean(dxh * xhat, axis=-1, keepdims=True))
                o_ref[...] = (r_ref[...] + dx).astype(out_dtype)
                contrib = jnp.sum(du * xhat, axis=0, keepdims=True)

                @pl.when(i == 0)
                def _():
                    dnw_ref[...] = contrib

                @pl.when(i > 0)
                def _():
                    dnw_ref[...] += contrib
            else:
                o_ref[...] = du.astype(out_dtype)

    in_specs = [pl.BlockSpec((tm, tk), lambda i, j, k: (i, k)), pl.BlockSpec((tn, tk), lambda i, j, k: (j, k))]
    args = [dy, w]
    out_specs = [pl.BlockSpec((tm, tn), lambda i, j, k: (i, j))]
    out_shape = [jax.ShapeDtypeStruct((M, N), out_dtype)]
    if has_epi:
        h, nw, res = epi
        in_specs += [pl.BlockSpec((tm, N), lambda i, j, k: (i, 0)), pl.BlockSpec((1, N), lambda i, j, k: (0, 0)),
                     pl.BlockSpec((tm, N), lambda i, j, k: (i, 0))]
        args += [h, nw.reshape(1, N), res]
        out_specs.append(pl.BlockSpec((1, N), lambda i, j, k: (0, 0)))
        out_shape.append(jax.ShapeDtypeStruct((1, N), F32))
    outs = pl.pallas_call(
        body, name=name, grid=(M // tm, N // tn, nk), in_specs=in_specs, out_specs=out_specs, out_shape=out_shape,
        scratch_shapes=[pltpu.VMEM((tm, tn), F32)],
        compiler_params=_cparams(("arbitrary", "arbitrary", "arbitrary")))(*args)
    return (outs[0], outs[1]) if has_epi else outs[0]


def _mm_tn(x, dy, *, name, norm_w=None, out_dtype=BF16, tk1=1024, tn=512, tt=512):
    T, K1 = x.shape
    N = dy.shape[1]
    tk1, tn, tt = min(tk1, K1), min(tn, N), min(tt, T)
    has_norm = norm_w is not None
    assert K1 % tk1 == 0 and N % tn == 0 and T % tt == 0, (name, K1, N, T, tk1, tn, tt)
    assert not has_norm or tk1 == K1
    nt = T // tt

    def body(*refs):
        if has_norm:
            x_ref, dy_ref, nw_ref, o_ref, acc_ref = refs
        else:
            x_ref, dy_ref, o_ref, acc_ref = refs
        t = pl.program_id(2)

        @pl.when(t == 0)
        def _():
            acc_ref[...] = jnp.zeros_like(acc_ref)

        xv = x_ref[...]
        if has_norm:
            xv = _rms_fwd(xv.astype(F32), nw_ref[...])
        acc_ref[...] += lax.dot_general(xv.astype(BF16), dy_ref[...].astype(BF16), _TN, preferred_element_type=F32)

        @pl.when(t == nt - 1)
        def _():
            o_ref[...] = acc_ref[...].astype(out_dtype)

    in_specs = [pl.BlockSpec((tt, tk1), lambda a, b, t: (t, a)), pl.BlockSpec((tt, tn), lambda a, b, t: (t, b))]
    args = [x, dy]
    if has_norm:
        in_specs.append(pl.BlockSpec((1, K1), lambda a, b, t: (0, 0)))
        args.append(norm_w.reshape(1, K1))
    return pl.pallas_call(
        body, name=name, grid=(K1 // tk1, N // tn, nt), in_specs=in_specs,
        out_specs=pl.BlockSpec((tk1, tn), lambda a, b, t: (a, b)),
        out_shape=jax.ShapeDtypeStruct((K1, N), out_dtype),
        scratch_shapes=[pltpu.VMEM((tk1, tn), F32)],
        compiler_params=_cparams(("parallel", "parallel", "arbitrary")))(*args)


def _shift_down(xb, prev8, j):
    main = pltpu.roll(xb, j, 0)
    head = pltpu.roll(xb[0:8], j, 0)
    ph = pltpu.roll(prev8, j, 0)
    row8 = lax.broadcasted_iota(jnp.int32, head.shape, 0)
    head = jnp.where(row8 < j, ph, head)
    return jnp.concatenate([head, main[8:]], axis=0)


def _shift_up(xb, next8, j):
    tt = xb.shape[0]
    main = pltpu.roll(xb, tt - j, 0)
    tail = pltpu.roll(xb[tt - 8:tt], 8 - j, 0)
    nh = pltpu.roll(next8, 8 - j, 0)
    row8 = lax.broadcasted_iota(jnp.int32, tail.shape, 0)
    tail = jnp.where(row8 + j >= 8, nh, tail)
    return jnp.concatenate([main[:tt - 8], tail], axis=0)


def _conv_hid(xb, prev8, w, b_row, K):
    out = b_row
    shifted = []
    for j in range(K):
        sh = K - 1 - j
        xs = xb if sh == 0 else _shift_down(xb, prev8, sh)
        shifted.append(xs)
        out = out + xs * w[j:j + 1, :]
    return out, shifted


def _prev_idx(i, nb8):
    return jnp.maximum(i * nb8 - 1, 0)


def _ssm_conv_fwd(zx, w, b, *, name, tt=512, tc=512):
    T = zx.shape[0]
    tt = min(tt, T)
    C, K = CONV_DIM, SSM_CONV
    cb0, nb8 = D_INNER // tc, tt // 8

    def body(x_ref, p_ref, w_ref, b_ref, o_ref):
        first = (pl.program_id(1) > 0).astype(F32)
        hid, _ = _conv_hid(x_ref[...], p_ref[...] * first, w_ref[...], b_ref[...], K)
        o_ref[...] = hid * _sigmoid(hid)

    return pl.pallas_call(
        body, name=name, grid=(C // tc, T // tt),
        in_specs=[pl.BlockSpec((tt, tc), lambda c, i: (i, c + cb0)),
                  pl.BlockSpec((8, tc), lambda c, i: (_prev_idx(i, nb8), c + cb0)),
                  pl.BlockSpec((K, tc), lambda c, i: (0, c)), pl.BlockSpec((1, tc), lambda c, i: (0, c))],
        out_specs=pl.BlockSpec((tt, tc), lambda c, i: (i, c)),
        out_shape=jax.ShapeDtypeStruct((T, C), F32),
        compiler_params=_cparams(("parallel", "parallel")))(zx, zx, w, b)


def _ssm_conv_bwd_pre(zx, w, b, dout, *, name, tt=512, tc=512):
    T = zx.shape[0]
    tt = min(tt, T)
    C, K = CONV_DIM, SSM_CONV
    cb0, nb8 = D_INNER // tc, tt // 8

    def body(x_ref, p_ref, w_ref, b_ref, d_ref, dh_ref, dw_ref, db_ref):
        t = pl.program_id(1)
        first = (t > 0).astype(F32)
        hid, shifted = _conv_hid(x_ref[...], p_ref[...] * first, w_ref[...], b_ref[...], K)
        sg = _sigmoid(hid)
        dh = d_ref[...] * (sg * (1.0 + hid * (1.0 - sg)))
        dh_ref[...] = dh

        @pl.when(t == 0)
        def _():
            dw_ref[...] = jnp.zeros_like(dw_ref)
            db_ref[...] = jnp.zeros_like(db_ref)

        db_ref[...] += jnp.sum(dh, axis=0, keepdims=True)
        for j in range(K):
            dw_ref[j:j + 1, :] += jnp.sum(dh * shifted[j], axis=0, keepdims=True)

    return pl.pallas_call(
        body, name=name, grid=(C // tc, T // tt),
        in_specs=[pl.BlockSpec((tt, tc), lambda c, i: (i, c + cb0)),
                  pl.BlockSpec((8, tc), lambda c, i: (_prev_idx(i, nb8), c + cb0)),
                  pl.BlockSpec((K, tc), lambda c, i: (0, c)), pl.BlockSpec((1, tc), lambda c, i: (0, c)),
                  pl.BlockSpec((tt, tc), lambda c, i: (i, c))],
        out_specs=[pl.BlockSpec((tt, tc), lambda c, i: (i, c)), pl.BlockSpec((K, tc), lambda c, i: (0, c)),
                   pl.BlockSpec((1, tc), lambda c, i: (0, c))],
        out_shape=[jax.ShapeDtypeStruct((T, C), F32), jax.ShapeDtypeStruct((K, C), F32),
                   jax.ShapeDtypeStruct((1, C), F32)],
        compiler_params=_cparams(("parallel", "arbitrary")))(zx, zx, w, b, dout)


def _conv_bwd_in(dh, w, *, name, K, tt=512, tc=512, out_dtype=BF16):
    T, C = dh.shape
    tt = min(tt, T)
    nb8, nT = tt // 8, T // tt
    last8 = T // 8 - 1

    def body(d_ref, n_ref, w_ref, o_ref):
        notlast = (pl.program_id(1) < nT - 1).astype(F32)
        d = d_ref[...]
        nxt = n_ref[...] * notlast
        w_ = w_ref[...]
        acc = d * w_[K - 1:K, :]
        for sh in range(1, K):
            acc = acc + _shift_up(d, nxt, sh) * w_[K - 1 - sh:K - sh, :]
        o_ref[...] = acc.astype(out_dtype)

    return pl.pallas_call(
        body, name=name, grid=(C // tc, nT),
        in_specs=[pl.BlockSpec((tt, tc), lambda c, i: (i, c)),
                  pl.BlockSpec((8, tc), lambda c, i: (jnp.minimum((i + 1) * nb8, last8), c)),
                  pl.BlockSpec((K, tc), lambda c, i: (0, c))],
        out_specs=pl.BlockSpec((tt, tc), lambda c, i: (i, c)),
        out_shape=jax.ShapeDtypeStruct((T, C), out_dtype),
        compiler_params=_cparams(("parallel", "parallel")))(dh, dh, w)


def _ffn_conv_fwd(a, w, b, *, name, tt=256, tc=1408):
    T = a.shape[0]
    tt = min(tt, T)
    K, nbh, nb8 = FFN_CONV, D_FF // tc, tt // 8

    def body(ag_ref, pg_ref, av_ref, pv_ref, wg_ref, wv_ref, bg_ref, bv_ref, o_ref):
        first = (pl.program_id(1) > 0).astype(F32)
        hg, _ = _conv_hid(ag_ref[...], pg_ref[...] * first, wg_ref[...], bg_ref[...], K)
        hv, _ = _conv_hid(av_ref[...], pv_ref[...] * first, wv_ref[...], bv_ref[...], K)
        o_ref[...] = (hg * _sigmoid(hg) * hv).astype(BF16)

    return pl.pallas_call(
        body, name=name, grid=(nbh, T // tt),
        in_specs=[pl.BlockSpec((tt, tc), lambda c, i: (i, c)),
                  pl.BlockSpec((8, tc), lambda c, i: (_prev_idx(i, nb8), c)),
                  pl.BlockSpec((tt, tc), lambda c, i: (i, c + nbh)),
                  pl.BlockSpec((8, tc), lambda c, i: (_prev_idx(i, nb8), c + nbh)),
                  pl.BlockSpec((K, tc), lambda c, i: (0, c)), pl.BlockSpec((K, tc), lambda c, i: (0, c + nbh)),
                  pl.BlockSpec((1, tc), lambda c, i: (0, c)), pl.BlockSpec((1, tc), lambda c, i: (0, c + nbh))],
        out_specs=pl.BlockSpec((tt, tc), lambda c, i: (i, c)),
        out_shape=jax.ShapeDtypeStruct((T, D_FF), BF16),
        compiler_params=_cparams(("parallel", "parallel")))(a, a, a, a, w, w, b, b)


def _ffn_conv_bwd_pre(a, w, b, dp, *, name, tt=256, tc=1408):
    T = a.shape[0]
    tt = min(tt, T)
    K, nbh, nb8 = FFN_CONV, D_FF // tc, tt // 8

    def body(ao_ref, po_ref, ag_ref, pg_ref, av_ref, pv_ref, wg_ref, wv_ref, bg_ref, bv_ref, dp_ref,
             dh_ref, dw_ref, db_ref):
        j = pl.program_id(0)
        t = pl.program_id(1)
        first = (t > 0).astype(F32)
        hg, _ = _conv_hid(ag_ref[...], pg_ref[...] * first, wg_ref[...], bg_ref[...], K)
        hv, _ = _conv_hid(av_ref[...], pv_ref[...] * first, wv_ref[...], bv_ref[...], K)
        sg = _sigmoid(hg)
        d = dp_ref[...].astype(F32)
        is_gate = (j < nbh).astype(F32)
        dh = d * (is_gate * (hv * (sg * (1.0 + hg * (1.0 - sg)))) + (1.0 - is_gate) * (hg * sg))
        dh_ref[...] = dh
        xo = ao_ref[...]
        po = po_ref[...] * first

        @pl.when(t == 0)
        def _():
            dw_ref[...] = jnp.zeros_like(dw_ref)
            db_ref[...] = jnp.zeros_like(db_ref)

        db_ref[...] += jnp.sum(dh, axis=0, keepdims=True)
        for jj in range(K):
            sh = K - 1 - jj
            xs = xo if sh == 0 else _shift_down(xo, po, sh)
            dw_ref[jj:jj + 1, :] += jnp.sum(dh * xs, axis=0, keepdims=True)

    def gi(c):
        return lax.rem(c, nbh)

    return pl.pallas_call(
        body, name=name, grid=(2 * nbh, T // tt),
        in_specs=[pl.BlockSpec((tt, tc), lambda c, i: (i, c)),
                  pl.BlockSpec((8, tc), lambda c, i: (_prev_idx(i, nb8), c)),
                  pl.BlockSpec((tt, tc), lambda c, i: (i, gi(c))),
                  pl.BlockSpec((8, tc), lambda c, i: (_prev_idx(i, nb8), gi(c))),
                  pl.BlockSpec((tt, tc), lambda c, i: (i, gi(c) + nbh)),
                  pl.BlockSpec((8, tc), lambda c, i: (_prev_idx(i, nb8), gi(c) + nbh)),
                  pl.BlockSpec((K, tc), lambda c, i: (0, gi(c))), pl.BlockSpec((K, tc), lambda c, i: (0, gi(c) + nbh)),
                  pl.BlockSpec((1, tc), lambda c, i: (0, gi(c))), pl.BlockSpec((1, tc), lambda c, i: (0, gi(c) + nbh)),
                  pl.BlockSpec((tt, tc), lambda c, i: (i, gi(c)))],
        out_specs=[pl.BlockSpec((tt, tc), lambda c, i: (i, c)), pl.BlockSpec((K, tc), lambda c, i: (0, c)),
                   pl.BlockSpec((1, tc), lambda c, i: (0, c))],
        out_shape=[jax.ShapeDtypeStruct((T, 2 * D_FF), F32), jax.ShapeDtypeStruct((K, 2 * D_FF), F32),
                   jax.ShapeDtypeStruct((1, 2 * D_FF), F32)],
        compiler_params=_cparams(("parallel", "arbitrary")))(a, a, a, a, a, a, w, w, b, b, dp)


def _cumsum_rows(x):
    L = x.shape[0]
    row = lax.broadcasted_iota(jnp.int32, x.shape, 0)
    k = 1
    while k < L:
        x = x + jnp.where(row >= k, pltpu.roll(x, k, 0), 0.0)
        k *= 2
    return x


def _rcumsum_rows(x):
    L = x.shape[0]
    row = lax.broadcasted_iota(jnp.int32, x.shape, 0)
    k = 1
    while k < L:
        x = x + jnp.where(row < L - k, pltpu.roll(x, L - k, 0), 0.0)
        k *= 2
    return x


def _ssd_common(dt_ref, par_ref):
    par = par_ref[...]
    raw = dt_ref[...] + par[0:1, :]
    dt = _softplus(raw)
    a = -jnp.exp(par[1:2, :])
    cs = _cumsum_rows(dt * a)
    L = cs.shape[0]
    cs_last = cs[L - 1:L, :]
    return raw, dt, a, par[2:3, :], cs, cs.T, jnp.exp(cs), jnp.exp(cs_last - cs), jnp.exp(cs_last)


def _ssd_specs(nc, rev):
    L = SSM_CHUNK

    def ci(c):
        return nc - 1 - c if rev else c

    return [pl.BlockSpec((L, 512), lambda g, c: (ci(c), g)),
            pl.BlockSpec((L, 128), lambda g, c: (ci(c), 16 + g)),
            pl.BlockSpec((L, 128), lambda g, c: (ci(c), 20 + g)),
            pl.BlockSpec((None, L, 128), lambda g, c: (g, ci(c), 0)),
            pl.BlockSpec((None, 8, 128), lambda g, c: (g, 0, 0)),
            pl.BlockSpec((L, 512), lambda g, c: (ci(c), g)),
            pl.BlockSpec((1, 512), lambda g, c: (0, g))], ci


def _ssd_fwd(xbc_c, zx, dtg, par, gnw, *, name):
    T = xbc_c.shape[0]
    L = SSM_CHUNK
    nc = T // L
    in_specs, ci = _ssd_specs(nc, False)

    def body(xs_ref, b_ref, c_ref, dt_ref, par_ref, z_ref, gnw_ref, y_ref, yn_ref, st_ref, h_ref):
        @pl.when(pl.program_id(1) == 0)
        def _():
            h_ref[...] = jnp.zeros_like(h_ref)

        _, dt, _, dsk, cs, csT, ecs, eend, dec = _ssd_common(dt_ref, par_ref)
        Bb = b_ref[...].astype(BF16)
        Cb = c_ref[...].astype(BF16)
        G = lax.dot_general(Cb, Bb, _NT, preferred_element_type=F32)
        row = lax.broadcasted_iota(jnp.int32, (L, L), 0)
        col = lax.broadcasted_iota(jnp.int32, (L, L), 1)
        tril = col <= row
        lo = lax.broadcasted_iota(jnp.int32, (L, 128), 1) < 64
        lo1 = lax.broadcasted_iota(jnp.int32, (1, 128), 1) < 64
        for pp in range(4):
            hA, hB = 2 * pp, 2 * pp + 1

            def sel(m):
                return jnp.where(lo, m[:, hA:hA + 1], m[:, hB:hB + 1])

            def sel1(m):
                return jnp.where(lo1, m[:, hA:hA + 1], m[:, hB:hB + 1])

            X = xs_ref[:, pp * 128:(pp + 1) * 128]
            xd = X * sel(dt)
            xdb = xd.astype(BF16)
            ys = []
            for h in (hA, hB):
                Lm = jnp.where(tril, jnp.exp(jnp.minimum(cs[:, h:h + 1] - csT[h:h + 1, :], 0.0)), 0.0)
                ys.append(jnp.dot((G * Lm).astype(BF16), xdb, preferred_element_type=F32))
            Hp = h_ref[pp]
            st_ref[pp] = Hp
            yoff = jnp.dot(Cb, Hp.astype(BF16), preferred_element_type=F32) * sel(ecs)
            y_ref[:, pp * 128:(pp + 1) * 128] = jnp.where(lo, ys[0], ys[1]) + yoff + sel1(dsk) * X
            S = lax.dot_general(Bb, (xd * sel(eend)).astype(BF16), _TN, preferred_element_type=F32)
            h_ref[pp] = Hp * sel1(dec) + S
        zv = z_ref[...]
        yg = y_ref[...] * (zv * _sigmoid(zv))
        yn_ref[...] = _rms_fwd(yg, gnw_ref[...]).astype(BF16)

    return pl.pallas_call(
        body, name=name, grid=(SSM_GROUPS, nc), in_specs=in_specs,
        out_specs=[pl.BlockSpec((L, 512), lambda g, c: (c, g)), pl.BlockSpec((L, 512), lambda g, c: (c, g)),
                   pl.BlockSpec((None, None, 4, 128, 128), lambda g, c: (g, c, 0, 0, 0))],
        out_shape=[jax.ShapeDtypeStruct((T, D_INNER), F32), jax.ShapeDtypeStruct((T, D_INNER), BF16),
                   jax.ShapeDtypeStruct((SSM_GROUPS, nc, 4, 128, 128), F32)],
        scratch_shapes=[pltpu.VMEM((4, 128, 128), F32)],
        compiler_params=_cparams(("parallel", "arbitrary")))(xbc_c, xbc_c, xbc_c, dtg, par, zx, gnw)


def _ssd_bwd(xbc_c, zx, dtg, par, gnw, y, st, dyn, *, name):
    T = xbc_c.shape[0]
    L = SSM_CHUNK
    nc = T // L
    in_specs, ci = _ssd_specs(nc, True)
    in_specs += [pl.BlockSpec((L, 512), lambda g, c: (ci(c), g)),
                 pl.BlockSpec((None, None, 4, 128, 128), lambda g, c: (g, ci(c), 0, 0, 0)),
                 pl.BlockSpec((L, 512), lambda g, c: (ci(c), g))]

    def body(xs_ref, b_ref, c_ref, dt_ref, par_ref, z_ref, gnw_ref, y_ref, st_ref, dyn_ref,
             dxs_ref, db_ref, dc_ref, dz_ref, ddt_ref, dgnw_ref, dpar_ref, dh_ref):
        @pl.when(pl.program_id(1) == 0)
        def _():
            dh_ref[...] = jnp.zeros_like(dh_ref)
            dgnw_ref[...] = jnp.zeros_like(dgnw_ref)
            dpar_ref[...] = jnp.zeros_like(dpar_ref)

        yv = y_ref[...]
        zv = z_ref[...]
        sg = _sigmoid(zv)
        sz = zv * sg
        yg = yv * sz
        r = lax.rsqrt(jnp.mean(yg * yg, axis=-1, keepdims=True) + EPS)
        yh = yg * r
        dyn = dyn_ref[...].astype(F32)
        dgnw_ref[...] += jnp.sum(dyn * yh, axis=0, keepdims=True)
        dyh = dyn * gnw_ref[...]
        dyg = r * (dyh - yh * jnp.mean(dyh * yh, axis=-1, keepdims=True))
        dY_all = dyg * sz
        dz_ref[...] = (dyg * yv * (sg * (1.0 + zv * (1.0 - sg)))).astype(dz_ref.dtype)

        raw, dt, a, dsk, cs, csT, ecs, eend, dec = _ssd_common(dt_ref, par_ref)
        Bb = b_ref[...].astype(BF16)
        Cb = c_ref[...].astype(BF16)
        G = lax.dot_general(Cb, Bb, _NT, preferred_element_type=F32)
        row = lax.broadcasted_iota(jnp.int32, (L, L), 0)
        col = lax.broadcasted_iota(jnp.int32, (L, L), 1)
        tril = col <= row
        lane = lax.broadcasted_iota(jnp.int32, (L, 128), 1)
        lo = lane < 64
        lane1 = lax.broadcasted_iota(jnp.int32, (1, 128), 1)
        lo1 = lane1 < 64
        rowc = lax.broadcasted_iota(jnp.int32, (L, 1), 0)
        dG = jnp.zeros((L, L), F32)
        dB = jnp.zeros((L, SSM_STATE), F32)
        dC = jnp.zeros((L, SSM_STATE), F32)
        dcs_mat = jnp.zeros((L, 128), F32)
        dcs_t = jnp.zeros((L, L), F32)
        ddt_mat = jnp.zeros((L, 128), F32)
        dD_row = jnp.zeros((1, 128), F32)

        def tot(m):
            return jnp.sum(jnp.sum(m, axis=1, keepdims=True), axis=0, keepdims=True)

        for pp in range(4):
            hA, hB = 2 * pp, 2 * pp + 1

            def sel(m):
                return jnp.where(lo, m[:, hA:hA + 1], m[:, hB:hB + 1])

            def sel1(m):
                return jnp.where(lo1, m[:, hA:hA + 1], m[:, hB:hB + 1])

            X = xs_ref[:, pp * 128:(pp + 1) * 128]
            dY = dY_all[:, pp * 128:(pp + 1) * 128]
            dtsel = sel(dt)
            xd = X * dtsel
            xdb = xd.astype(BF16)
            dYb = dY.astype(BF16)
            Hp = st_ref[pp]
            Hb = Hp.astype(BF16)
            dHn = dh_ref[pp]
            dHb = dHn.astype(BF16)
            ecs_sel = sel(ecs)
            eend_sel = sel(eend)
            dxd_state = jnp.dot(Bb, dHb, preferred_element_type=F32) * eend_sel
            yoff = jnp.dot(Cb, Hb, preferred_element_type=F32) * ecs_sel
            dYe = (dY * ecs_sel).astype(BF16)
            dC = dC + lax.dot_general(dYe, Hb, _NT, preferred_element_type=F32)
            dB = dB + lax.dot_general((xd * eend_sel).astype(BF16), dHb, _NT, preferred_element_type=F32)
            dh_ref[pp] = dHn * sel1(dec) + lax.dot_general(Cb, dYe, _TN, preferred_element_type=F32)
            q = xd * dxd_state
            dyoff = dY * yoff
            hh = dHn * Hp
            dxd_diag = []
            for h, msk, msk1 in ((hA, lo, lo1), (hB, jnp.logical_not(lo), jnp.logical_not(lo1))):
                Lm = jnp.where(tril, jnp.exp(jnp.minimum(cs[:, h:h + 1] - csT[h:h + 1, :], 0.0)), 0.0)
                M = G * Lm
                dxd_diag.append(lax.dot_general(M.astype(BF16), dYb, _TN, preferred_element_type=F32))
                dM = lax.dot_general(jnp.where(msk, dY, 0.0).astype(BF16), xdb, _NT, preferred_element_type=F32)
                dG = dG + dM * Lm
                W = dM * M
                dcs_h = jnp.sum(W, axis=1, keepdims=True)
                dcs_t = dcs_t + jnp.where(row == h, jnp.sum(W, axis=0, keepdims=True), 0.0)
                dcs_h = dcs_h + jnp.sum(jnp.where(msk, dyoff - q, 0.0), axis=1, keepdims=True)
                tail = tot(jnp.where(msk, q, 0.0)) + dec[:, h:h + 1] * tot(jnp.where(msk, hh, 0.0))
                dcs_h = dcs_h + jnp.where(rowc == L - 1, tail, 0.0)
                dcs_mat = dcs_mat + jnp.where(lane == h, dcs_h, 0.0)
            dxd = jnp.where(lo, dxd_diag[0], dxd_diag[1]) + dxd_state
            prod = dxd * X
            dA_ = jnp.sum(jnp.where(lo, prod, 0.0), axis=1, keepdims=True)
            dB_ = jnp.sum(prod, axis=1, keepdims=True) - dA_
            ddt_mat = ddt_mat + jnp.where(lane == hA, dA_, 0.0) + jnp.where(lane == hB, dB_, 0.0)
            dxs_ref[:, pp * 128:(pp + 1) * 128] = dxd * dtsel + sel1(dsk) * dY
            dyx = jnp.sum(dY * X, axis=0, keepdims=True)
            sA = jnp.sum(jnp.where(lo1, dyx, 0.0), axis=1, keepdims=True)
            sB = jnp.sum(dyx, axis=1, keepdims=True) - sA
            dD_row = dD_row + jnp.where(lane1 == hA, sA, 0.0) + jnp.where(lane1 == hB, sB, 0.0)
        dGb = dG.astype(BF16)
        db_ref[...] = dB + lax.dot_general(dGb, Cb, _TN, preferred_element_type=F32)
        dc_ref[...] = dC + jnp.dot(dGb, Bb, preferred_element_type=F32)
        dad = _rcumsum_rows(dcs_mat - dcs_t.T)
        draw = (a * dad + ddt_mat) * _sigmoid(raw)
        ddt_ref[...] = draw
        dpar_ref[0:1, :] += jnp.sum(draw, axis=0, keepdims=True)
        dpar_ref[1:2, :] += jnp.sum(dt * dad, axis=0, keepdims=True) * a
        dpar_ref[2:3, :] += dD_row

    return pl.pallas_call(
        body, name=name, grid=(SSM_GROUPS, nc), in_specs=in_specs,
        out_specs=[pl.BlockSpec((L, 512), lambda g, c: (ci(c), g)),
                   pl.BlockSpec((L, 128), lambda g, c: (ci(c), g)),
                   pl.BlockSpec((L, 128), lambda g, c: (ci(c), g)),
                   pl.BlockSpec((L, 512), lambda g, c: (ci(c), g)),
                   pl.BlockSpec((None, L, 128), lambda g, c: (g, ci(c), 0)),
                   pl.BlockSpec((1, 512), lambda g, c: (0, g)),
                   pl.BlockSpec((None, 8, 128), lambda g, c: (g, 0, 0))],
        out_shape=[jax.ShapeDtypeStruct((T, D_INNER), F32), jax.ShapeDtypeStruct((T, GN), F32),
                   jax.ShapeDtypeStruct((T, GN), F32), jax.ShapeDtypeStruct((T, D_INNER), BF16),
                   jax.ShapeDtypeStruct((SSM_GROUPS, T, 128), F32), jax.ShapeDtypeStruct((1, D_INNER), F32),
                   jax.ShapeDtypeStruct((SSM_GROUPS, 8, 128), F32)],
        scratch_shapes=[pltpu.VMEM((4, 128, 128), F32)],
        compiler_params=_cparams(("parallel", "arbitrary")))(xbc_c, xbc_c, xbc_c, dtg, par, zx, gnw, y, st, dyn)


SB_KEYS = 512
SB_SCAN = 256


def _tri(width, cond):
    kk = lax.broadcasted_iota(jnp.int32, (width, width), 0)
    jj = lax.broadcasted_iota(jnp.int32, (width, width), 1)
    return cond(kk, jj).astype(BF16)


def _sba_diag_mask():
    Bq = SB_BLOCK
    rowi = lax.broadcasted_iota(jnp.int32, (2 * Bq, Bq), 0)
    return lax.broadcasted_iota(jnp.int32, (2 * Bq, Bq), 1) < jnp.where(rowi >= Bq, rowi - Bq, rowi)


def _sba_sub_fwd(zb, c, U, mask):
    s = _softplus(zb)
    l = -s if mask is None else jnp.where(mask, -s, 0.0)
    R = c + jnp.dot(l.astype(BF16), U, preferred_element_type=F32)
    A = jnp.exp(zb - s + R)
    if mask is not None:
        A = jnp.where(mask, A, 0.0)
    return A.astype(BF16), R[:, 0:1] + l[:, 0:1]


def _sba_sub_bwd(zb, dAb, Lt, pc, pe, Uincl, Uexcl, mask):
    last = zb.shape[1] - 1
    s = _softplus(zb)
    l = -s if mask is None else jnp.where(mask, -s, 0.0)
    P = pc + jnp.dot(l.astype(BF16), Uincl, preferred_element_type=F32)
    g = zb - s
    A = jnp.exp(g + (Lt - P))
    if mask is not None:
        A = jnp.where(mask, A, 0.0)
    E = dAb * A
    PE = pe + jnp.dot(E.astype(BF16), Uexcl, preferred_element_type=F32)
    dz = E - jnp.exp(g) * (E + PE)
    if mask is not None:
        dz = jnp.where(mask, dz, 0.0)
    return (A.astype(BF16), dz.astype(BF16), P[:, last:last + 1], PE[:, last:last + 1] + E[:, last:last + 1])


def _stack_heads(v):
    lo = lax.broadcasted_iota(jnp.int32, v.shape, 1) < 64
    zero = jnp.zeros_like(v)
    return jnp.concatenate([jnp.where(lo, v, zero), jnp.where(lo, zero, v)], axis=0)


def _unstack_heads(v):
    lo = lax.broadcasted_iota(jnp.int32, (SB_BLOCK, 128), 1) < 64
    return jnp.where(lo, v[:SB_BLOCK], v[SB_BLOCK:])


def _sba_fwd(q, kv, *, name):
    T = q.shape[0]
    Bq = SB_BLOCK
    nsub = SB_KEYS // Bq
    assert T % SB_KEYS == 0
    scale = 1.0 / math.sqrt(SB_HEAD_DIM)

    def body(q_ref, k_ref, v_ref, o_ref, lt_ref):
        I = pl.program_id(1)
        U1 = _tri(Bq, lambda k, j: k > j)
        U2 = _tri(SB_SCAN, lambda k, j: k > j)
        dmask = _sba_diag_mask()
        qs = [_stack_heads(q_ref[a * Bq:(a + 1) * Bq, :] * scale) for a in range(nsub)]
        cs, accs = [], []
        for a in range(nsub):
            c = jnp.zeros((2 * Bq, 1), F32)
            acc = jnp.zeros((2 * Bq, 128), F32)
            for b in range(a, -1, -1):
                off = pl.multiple_of(I * SB_KEYS + b * Bq, Bq)
                zb = lax.dot_general(qs[a], k_ref[pl.ds(off, Bq), :], _NT, preferred_element_type=F32)
                A, c = _sba_sub_fwd(zb, c, U1, dmask if b == a else None)
                acc = acc + jnp.dot(A, v_ref[pl.ds(off, Bq), :], preferred_element_type=F32)
            cs.append(c)
            accs.append(acc)
        qs_all = jnp.concatenate(qs, axis=0)

        def step(n, carry):
            c, acc = carry
            off = pl.multiple_of((I - 1 - n) * SB_KEYS, SB_KEYS)
            z = lax.dot_general(qs_all, k_ref[pl.ds(off, SB_KEYS), :], _NT, preferred_element_type=F32)
            parts = [None] * (SB_KEYS // SB_SCAN)
            for b in reversed(range(SB_KEYS // SB_SCAN)):
                parts[b], c = _sba_sub_fwd(z[:, b * SB_SCAN:(b + 1) * SB_SCAN], c, U2, None)
            return c, acc + jnp.dot(jnp.concatenate(parts, axis=1), v_ref[pl.ds(off, SB_KEYS), :],
                                    preferred_element_type=F32)

        c, acc = lax.fori_loop(0, I, step, (jnp.concatenate(cs, axis=0), jnp.concatenate(accs, axis=0)))
        for a in range(nsub):
            rows = slice(2 * a * Bq, 2 * (a + 1) * Bq)
            o_ref[a * Bq:(a + 1) * Bq, :] = _unstack_heads(acc[rows]).astype(BF16)
            lt_ref[a * Bq:(a + 1) * Bq, :] = _unstack_heads(jnp.broadcast_to(c[rows], (2 * Bq, 128)))

    return pl.pallas_call(
        body, name=name, grid=(SB_HEADS // 2, T // SB_KEYS),
        in_specs=[pl.BlockSpec((SB_KEYS, 128), lambda p, i: (i, p)), pl.BlockSpec((T, 128), lambda p, i: (0, p)),
                  pl.BlockSpec((T, 128), lambda p, i: (0, p + SB_HEADS // 2))],
        out_specs=[pl.BlockSpec((SB_KEYS, 128), lambda p, i: (i, p)),
                   pl.BlockSpec((None, SB_KEYS, 128), lambda p, i: (p, i, 0))],
        out_shape=[jax.ShapeDtypeStruct((T, D_MODEL), BF16), jax.ShapeDtypeStruct((SB_HEADS // 2, T, 128), F32)],
        compiler_params=_cparams(("parallel", "parallel")))(q, kv, kv)


def _sba_bwd(q, kv, lt, do, *, name):
    T = q.shape[0]
    Bq = SB_BLOCK
    nq = T // SB_KEYS
    nsub = SB_KEYS // Bq
    assert T % SB_KEYS == 0
    scale = 1.0 / math.sqrt(SB_HEAD_DIM)

    def body(q_ref, k_ref, v_ref, lt_ref, do_ref, dq_ref, dk_ref, dv_ref, dk_acc, dv_acc):
        i = pl.program_id(1)

        @pl.when(i == 0)
        def _():
            dk_acc[...] = jnp.zeros_like(dk_acc)
            dv_acc[...] = jnp.zeros_like(dv_acc)

        Uincl1 = _tri(Bq, lambda k, j: k <= j)
        Uexcl1 = _tri(Bq, lambda k, j: k < j)
        Uincl2 = _tri(SB_SCAN, lambda k, j: k <= j)
        Uexcl2 = _tri(SB_SCAN, lambda k, j: k < j)
        dmask = _sba_diag_mask()
        qs, dos, lts = [], [], []
        for a in range(nsub):
            rows = slice(a * Bq, (a + 1) * Bq)
            qs.append(_stack_heads(q_ref[rows, :] * scale))
            dos.append(_stack_heads(do_ref[rows, :]))
            lts.append(jnp.concatenate([lt_ref[rows, 0:1], lt_ref[rows, 64:65]], axis=0))
        qs_all = jnp.concatenate(qs, axis=0)
        dos_all = jnp.concatenate(dos, axis=0)
        lt_all = jnp.concatenate(lts, axis=0)

        def step(J, carry):
            pc, pe, dq_acc = carry
            off = pl.multiple_of(J * SB_KEYS, SB_KEYS)
            kb = k_ref[pl.ds(off, SB_KEYS), :]
            z = lax.dot_general(qs_all, kb, _NT, preferred_element_type=F32)
            dA = lax.dot_general(dos_all, v_ref[pl.ds(off, SB_KEYS), :], _NT, preferred_element_type=F32)
            a_parts, dz_parts = [], []
            for b in range(SB_KEYS // SB_SCAN):
                cols = slice(b * SB_SCAN, (b + 1) * SB_SCAN)
                A, dz, pc, pe = _sba_sub_bwd(z[:, cols], dA[:, cols], lt_all, pc, pe, Uincl2, Uexcl2, None)
                a_parts.append(A)
                dz_parts.append(dz)
            dzt = jnp.concatenate(dz_parts, axis=1)
            dk_acc[pl.ds(off, SB_KEYS), :] += lax.dot_general(dzt, qs_all, _TN, preferred_element_type=F32)
            dv_acc[pl.ds(off, SB_KEYS), :] += lax.dot_general(jnp.concatenate(a_parts, axis=1), dos_all, _TN,
                                                              preferred_element_type=F32)
            return pc, pe, dq_acc + jnp.dot(dzt, kb, preferred_element_type=F32)

        zc = jnp.zeros((2 * nsub * Bq, 1), F32)
        pc, pe, dq_acc = lax.fori_loop(0, i, step, (zc, zc, jnp.zeros((2 * nsub * Bq, 128), F32)))
        for a in range(nsub):
            rows = slice(2 * a * Bq, 2 * (a + 1) * Bq)
            pca, pea, dqa = pc[rows], pe[rows], dq_acc[rows]
            for b in range(a + 1):
                off = pl.multiple_of(i * SB_KEYS + b * Bq, Bq)
                kb = k_ref[pl.ds(off, Bq), :]
                zb = lax.dot_general(qs[a], kb, _NT, preferred_element_type=F32)
                dAb = lax.dot_general(dos[a], v_ref[pl.ds(off, Bq), :], _NT, preferred_element_type=F32)
                A, dz, pca, pea = _sba_sub_bwd(zb, dAb, lts[a], pca, pea, Uincl1, Uexcl1, dmask if b == a else None)
                dqa = dqa + jnp.dot(dz, kb, preferred_element_type=F32)
                dk_acc[pl.ds(off, Bq), :] += lax.dot_general(dz, qs[a], _TN, preferred_element_type=F32)
                dv_acc[pl.ds(off, Bq), :] += lax.dot_general(A, dos[a], _TN, preferred_element_type=F32)
            dq_ref[a * Bq:(a + 1) * Bq, :] = (_unstack_heads(dqa) * scale).astype(BF16)

        @pl.when(i == nq - 1)
        def _():
            dk_ref[...] = dk_acc[...].astype(BF16)
            dv_ref[...] = dv_acc[...].astype(BF16)

    return pl.pallas_call(
        body, name=name, grid=(SB_HEADS // 2, nq),
        in_specs=[pl.BlockSpec((SB_KEYS, 128), lambda p, i: (i, p)), pl.BlockSpec((T, 128), lambda p, i: (0, p)),
                  pl.BlockSpec((T, 128), lambda p, i: (0, p + SB_HEADS // 2)),
                  pl.BlockSpec((None, SB_KEYS, 128), lambda p, i: (p, i, 0)),
                  pl.BlockSpec((SB_KEYS, 128), lambda p, i: (i, p))],
        out_specs=[pl.BlockSpec((SB_KEYS, 128), lambda p, i: (i, p)), pl.BlockSpec((T, 128), lambda p, i: (0, p)),
                   pl.BlockSpec((T, 128), lambda p, i: (0, p))],
        out_shape=[jax.ShapeDtypeStruct((T, D_MODEL), BF16), jax.ShapeDtypeStruct((T, D_MODEL), BF16),
                   jax.ShapeDtypeStruct((T, D_MODEL), BF16)],
        scratch_shapes=[pltpu.VMEM((T, 128), F32), pltpu.VMEM((T, 128), F32)],
        compiler_params=_cparams(("parallel", "arbitrary")))(q, kv, kv, lt, do)


def _loss_head(h, tgt, w, *, name, tt=512):
    T, D = h.shape
    tt = min(tt, T)

    def body(h_ref, t_ref, w_ref, loss_ref, dh_ref, dw_ref):
        i = pl.program_id(0)
        hv = h_ref[...]
        wv = w_ref[...]
        r = lax.rsqrt(jnp.mean(hv * hv, axis=-1, keepdims=True) + EPS)
        xhat = hv * r
        err = xhat * wv - t_ref[...]
        part = 0.5 * jnp.sum(jnp.mean(err * err, axis=-1, keepdims=True), axis=0, keepdims=True)
        dy = err * (1.0 / D)
        dxh = dy * wv
        dh_ref[...] = r * (dxh - xhat * jnp.mean(dxh * xhat, axis=-1, keepdims=True))
        dwc = jnp.sum(dy * xhat, axis=0, keepdims=True)

        @pl.when(i == 0)
        def _():
            loss_ref[...] = jnp.broadcast_to(part, loss_ref.shape)
            dw_ref[...] = dwc

        @pl.when(i > 0)
        def _():
            loss_ref[...] += jnp.broadcast_to(part, loss_ref.shape)
            dw_ref[...] += dwc

    return pl.pallas_call(
        body, name=name, grid=(T // tt,),
        in_specs=[pl.BlockSpec((tt, D), lambda i: (i, 0)), pl.BlockSpec((tt, D), lambda i: (i, 0)),
                  pl.BlockSpec((1, D), lambda i: (0, 0))],
        out_specs=[pl.BlockSpec((1, 128), lambda i: (0, 0)), pl.BlockSpec((tt, D), lambda i: (i, 0)),
                   pl.BlockSpec((1, D), lambda i: (0, 0))],
        out_shape=[jax.ShapeDtypeStruct((1, 128), F32), jax.ShapeDtypeStruct((T, D), F32),
                   jax.ShapeDtypeStruct((1, D), F32)],
        compiler_params=_cparams(("arbitrary",)))(h, tgt, w.reshape(1, D))


def _adamw(parts, w, m, v, *, name, tr=256):
    P, R, C = parts.shape
    tr = min(tr, R)
    assert R % tr == 0, (name, R, tr)
    c1 = 1.0 - ADAM_B1 ** ADAM_STEP
    c2 = 1.0 - ADAM_B2 ** ADAM_STEP

    def body(p_ref, w_ref, m_ref, v_ref, g_ref, d_ref, nm_ref, nv_ref):
        g = p_ref[0].astype(F32)
        for k in range(1, P):
            g = g + p_ref[k].astype(F32)
        mn = ADAM_B1 * m_ref[...] + (1.0 - ADAM_B1) * g
        vn = ADAM_B2 * v_ref[...] + (1.0 - ADAM_B2) * (g * g)
        g_ref[...] = g
        nm_ref[...] = mn
        nv_ref[...] = vn
        d_ref[...] = -ADAM_LR * ((mn / c1) / (jnp.sqrt(vn / c2) + ADAM_EPS) + ADAM_WD * w_ref[...])

    spec = pl.BlockSpec((tr, C), lambda i: (i, 0))
    sds = jax.ShapeDtypeStruct((R, C), F32)
    return pl.pallas_call(
        body, name=name, grid=(R // tr,),
        in_specs=[pl.BlockSpec((P, tr, C), lambda i: (0, i, 0)), spec, spec, spec],
        out_specs=[spec, spec, spec, spec], out_shape=[sds, sds, sds, sds],
        compiler_params=_cparams(("parallel",)))(parts, w, m, v)


def _all_gather(shards, *, name):
    n = len(shards)

    def body(*refs):
        ins, outs = refs[:n], refs[n:2 * n]
        send_sems, recv_sems, local_sems = refs[2 * n:]
        x, y, c = lax.axis_index("x"), lax.axis_index("y"), lax.axis_index("c")
        me, sib = (x, y, c), (x, y, 1 - c)
        chips = [(1 - x, y), (x, 1 - y), (1 - x, 1 - y)]

        def slot(p):
            return 4 * p[0] + 2 * p[1] + p[2]

        def cp(a, k, block, to, src=None):
            dst = outs[a].at[slot(block)]
            return pltpu.make_async_remote_copy(src_ref=dst if src is None else src, dst_ref=dst,
                                                send_sem=send_sems.at[a, k], recv_sem=recv_sems.at[a, k],
                                                device_id=to, device_id_type=_MESH)

        mine = [pltpu.make_async_copy(ins[a], outs[a].at[slot(me)], local_sems.at[a]) for a in range(n)]
        for m in mine:
            m.start()
        first = []
        for a in range(n):
            first.append(cp(a, 0, me, sib, src=ins[a]))
            for j, chip in enumerate(chips):
                first.append(cp(a, 1 + j, me, (*chip, c), src=ins[a]))
        for f in first:
            f.start()
        passed = []
        for j, chip in enumerate(chips):
            for a in range(n):
                cp(a, 1 + j, (*chip, c), me).wait_recv()
                f = cp(a, 4 + j, (*chip, c), sib)
                f.start()
                passed.append(f)
        for a in range(n):
            cp(a, 0, sib, me).wait_recv()
            for j, chip in enumerate(chips):
                cp(a, 4 + j, (*chip, 1 - c), me).wait_recv()
        for f in first + passed:
            f.wait_send()
        for m in mine:
            m.wait()

    return pl.pallas_call(
        body, name=name, in_specs=[_ANY] * n, out_specs=[_ANY] * n,
        out_shape=[jax.ShapeDtypeStruct((N_DEV,) + s.shape, s.dtype) for s in shards],
        scratch_shapes=[pltpu.SemaphoreType.DMA((n, 7)), pltpu.SemaphoreType.DMA((n, 7)),
                        pltpu.SemaphoreType.DMA((n,))])(*shards)


def _exchange(blocks, *, name):
    n = len(blocks)

    def body(*refs):
        ins, outs = refs[:n], refs[n:2 * n]
        send_sems, recv_sems, local_sems = refs[2 * n:]
        x, y, c = lax.axis_index("x"), lax.axis_index("y"), lax.axis_index("c")
        me = 4 * x + 2 * y + c
        mine = [pltpu.make_async_copy(ins[a].at[me], outs[a].at[me], local_sems.at[a]) for a in range(n)]
        for m in mine:
            m.start()
        copies = []
        for r in range(1, N_DEV):
            rx, ry, rc = (r >> 2) & 1, (r >> 1) & 1, r & 1
            px, py, pc = (1 - x if rx else x), (1 - y if ry else y), (1 - c if rc else c)
            peer = 4 * px + 2 * py + pc
            for a in range(n):
                copies.append((pltpu.make_async_remote_copy(
                    src_ref=ins[a].at[peer], dst_ref=outs[a].at[me], send_sem=send_sems.at[a, r - 1],
                    recv_sem=recv_sems.at[a, r - 1], device_id=(px, py, pc), device_id_type=_MESH),
                    pltpu.make_async_remote_copy(
                    src_ref=ins[a].at[peer], dst_ref=outs[a].at[peer], send_sem=send_sems.at[a, r - 1],
                    recv_sem=recv_sems.at[a, r - 1], device_id=(px, py, pc), device_id_type=_MESH)))
        for snd, _ in copies:
            snd.start()
        for _, rcv in copies:
            rcv.wait_recv()
        for snd, _ in copies:
            snd.wait_send()
        for m in mine:
            m.wait()

    return pl.pallas_call(
        body, name=name, in_specs=[_ANY] * n, out_specs=[_ANY] * n,
        out_shape=[jax.ShapeDtypeStruct(b.shape, b.dtype) for b in blocks],
        scratch_shapes=[pltpu.SemaphoreType.DMA((n, 7)), pltpu.SemaphoreType.DMA((n, 7)),
                        pltpu.SemaphoreType.DMA((n,))])(*blocks)


_HBM = pl.BlockSpec(memory_space=pltpu.HBM)
_SEM = pl.BlockSpec(memory_space=pltpu.SEMAPHORE)
_EFFECT = pltpu.SideEffectType.DATAFLOW_SIDE_EFFECTING


def _peers():
    x, y, c = lax.axis_index("x"), lax.axis_index("y"), lax.axis_index("c")
    out = []
    for r in range(1, N_DEV):
        px = 1 - x if (r >> 2) & 1 else x
        py = 1 - y if (r >> 1) & 1 else y
        pc = 1 - c if r & 1 else c
        out.append(((px, py, pc), 4 * px + 2 * py + pc))
    return 4 * x + 2 * y + c, out


def _push_copy(src_ref, land_ref, send_sems, recv_sems, a, k, me, peer, peer_slot, scatter, arriving):
    src = src_ref.at[peer_slot] if scatter else src_ref
    return pltpu.make_async_remote_copy(
        src_ref=src, dst_ref=land_ref.at[peer_slot if arriving else me], send_sem=send_sems.at[a * (N_DEV - 1) + k],
        recv_sem=recv_sems.at[a * (N_DEV - 1) + k], device_id=peer, device_id_type=_MESH)


def _push_start(srcs, *, scatter, name):
    n = len(srcs)
    lands = [lax.empty(s.shape if scatter else (N_DEV,) + s.shape, s.dtype) for s in srcs]

    def body(*refs):
        src_refs, land_refs = refs[:n], refs[n:2 * n]
        send_sems, recv_sems = refs[2 * n], refs[2 * n + 1]
        token = refs[-1]
        me, peers = _peers()
        for k, (peer, slot) in enumerate(peers):
            for a in range(n):
                _push_copy(src_refs[a], land_refs[a], send_sems, recv_sems, a, k, me, peer, slot, scatter, False).start()
        token[...] = jnp.zeros_like(token)

    hbm = lambda a: pltpu.HBM(a.shape, a.dtype)
    outs = pl.pallas_call(
        body, name=name,
        out_shape=(pltpu.SemaphoreType.DMA((n * (N_DEV - 1),)), pltpu.SemaphoreType.DMA((n * (N_DEV - 1),)),
                   *[hbm(s) for s in srcs], *[hbm(l) for l in lands], jax.ShapeDtypeStruct((8, 128), F32)),
        in_specs=[_HBM] * (2 * n),
        out_specs=(_SEM, _SEM, *([_HBM] * (2 * n)), pl.BlockSpec(memory_space=pltpu.VMEM)),
        input_output_aliases={i: 2 + i for i in range(2 * n)},
        compiler_params=pltpu.CompilerParams(has_side_effects=_EFFECT),
    )(*[pltpu.with_memory_space_constraint(s, pltpu.HBM) for s in srcs],
      *[pltpu.with_memory_space_constraint(l, pltpu.HBM) for l in lands])
    return dict(send=outs[0], recv=outs[1], srcs=list(outs[2:2 + n]), lands=list(outs[2 + n:2 + 2 * n]),
                token=outs[-1], scatter=scatter, n=n)


def _push_wait(h, after, *, name):
    n, scatter = h["n"], h["scatter"]

    def body(*refs):
        src_refs, land_refs = refs[:n], refs[n:2 * n]
        send_sems, recv_sems = refs[2 * n], refs[2 * n + 1]
        me, peers = _peers()
        for k, (peer, slot) in enumerate(peers):
            for a in range(n):
                cp = _push_copy(src_refs[a], land_refs[a], send_sems, recv_sems, a, k, me, peer, slot, scatter, True)
                cp.wait_send()
                cp.wait_recv()

    hbm = lambda a: pltpu.HBM(a.shape, a.dtype)
    outs = pl.pallas_call(
        body, name=name,
        out_shape=(*[hbm(s) for s in h["srcs"]], *[hbm(l) for l in h["lands"]]),
        in_specs=[_HBM] * (2 * n) + [_SEM, _SEM, _ANY], out_specs=tuple([_HBM] * (2 * n)),
        input_output_aliases={i: i for i in range(2 * n)},
        compiler_params=pltpu.CompilerParams(has_side_effects=_EFFECT),
    )(*h["srcs"], *h["lands"], h["send"], h["recv"], after)
    return list(outs[:n]), list(outs[n:])


def _ffn_fwd(h, nw, w_up, conv_w, conv_b, w_down, tag):
    a = _mm_fwd(h, w_up, norm_w=nw, name=f"ffn{tag}_up", tm=1024, tn=1408)
    p = _ffn_conv_fwd(a, conv_w, conv_b.reshape(1, -1), name=f"ffn{tag}_conv")
    h_out = _mm_fwd(p, w_down, residual=h, name=f"ffn{tag}_down", tm=1024, tn=512)
    return h_out, (a, p)


def _ffn_bwd(dh, h, saved, nw, w_up, conv_w, conv_b, w_down, tag):
    a, p = saved
    g_down = _mm_tn(p, dh, name=f"ffn{tag}_down_wg", tk1=1408, tn=1024)
    dp = _mm_nt(dh, w_down, name=f"ffn{tag}_down_dg", out_dtype=BF16, tm=1024, tn=1408, tk=1024)
    dhid, g_cw, g_cb = _ffn_conv_bwd_pre(a, conv_w, conv_b.reshape(1, -1), dp, name=f"ffn{tag}_conv_bwd")
    da = _conv_bwd_in(dhid, conv_w, K=FFN_CONV, name=f"ffn{tag}_conv_bwd_in", tt=256, tc=1408)
    g_up = _mm_tn(h, da, norm_w=nw, name=f"ffn{tag}_up_wg", tn=1408)
    dh_out, g_nw = _mm_nt(da, w_up, epi=(h, nw, dh), name=f"ffn{tag}_up_dg", tk=1408)
    return dh_out, dict(norm=g_nw.reshape(-1), up=g_up, conv_w=g_cw, conv_b=g_cb.reshape(-1), down=g_down)


def _local_step(x, tgt, W):
    T = x.shape[0]
    f = {}
    zx = _mm_fwd(x, W["in_w"], norm_w=W["ssm_norm_w"], name="ssm_in", tm=1024, tn=896)
    xbc_c = _ssm_conv_fwd(zx, W["ssm_conv_w"], W["ssm_conv_b"].reshape(1, -1), name="ssm_conv")
    dt_raw = zx[:, D_INNER + CONV_DIM:IN_PROJ_DIM]
    dtg = jnp.pad(dt_raw.reshape(T, SSM_GROUPS, 8).transpose(1, 0, 2), ((0, 0), (0, 0), (0, 120)))
    par = jnp.stack([W["ssm_dt_bias"].reshape(SSM_GROUPS, 8), W["ssm_a_log"].reshape(SSM_GROUPS, 8),
                     W["ssm_d"].reshape(SSM_GROUPS, 8)], axis=1)
    par = jnp.pad(par, ((0, 0), (0, 5), (0, 120)))
    gnw = W["ssm_gate_norm_w"].reshape(1, D_INNER)
    y, yn, st = _ssd_fwd(xbc_c, zx, dtg, par, gnw, name="ssd_fwd")
    h1 = _mm_fwd(yn, W["ssm_out_w"], residual=x, name="ssm_out", tm=1024, tn=512)
    h2, ffn0 = _ffn_fwd(h1, W["ffn_norm_w"][0], W["ffn_up_w"][0], W["ffn_conv_w"][0], W["ffn_conv_b"][0],
                        W["ffn_down_w"][0], "0")
    q = _mm_fwd(h2, W["w_q"], norm_w=W["attn_norm_w"], out_dtype=BF16, name="attn_q", tm=1024, tn=1024)
    kv = _mm_fwd(h2, W["w_kv"], norm_w=W["kv_norm_w"], out_dtype=BF16, name="attn_kv", tm=1024, tn=1024)
    o, lt = _sba_fwd(q, kv, name="sba_fwd")
    h3 = _mm_fwd(o, W["w_o"], residual=h2, name="attn_o", tm=1024, tn=512)
    h4, ffn1 = _ffn_fwd(h3, W["ffn_norm_w"][1], W["ffn_up_w"][1], W["ffn_conv_w"][1], W["ffn_conv_b"][1],
                        W["ffn_down_w"][1], "1")
    loss, dh4, g_final = _loss_head(h4, tgt, W["final_norm_w"], name="loss_head")
    dh3, gf1 = _ffn_bwd(dh4, h3, ffn1, W["ffn_norm_w"][1], W["ffn_up_w"][1], W["ffn_conv_w"][1], W["ffn_conv_b"][1],
                        W["ffn_down_w"][1], "1")
    g_wo = _mm_tn(o, dh3, name="attn_o_wg", tn=1024)
    do = _mm_nt(dh3, W["w_o"], name="attn_o_dg", out_dtype=BF16, tn=1024, tk=1024)
    dq, dk, dv = _sba_bwd(q, kv, lt, do, name="sba_bwd")
    g_wq = _mm_tn(h2, dq, norm_w=W["attn_norm_w"], name="attn_q_wg", tn=1024)
    dh2a, g_attn_nw = _mm_nt(dq, W["w_q"], epi=(h2, W["attn_norm_w"], dh3), name="attn_q_dg", tk=1024)
    dkv = jnp.concatenate([dk, dv], axis=1)
    g_wkv = _mm_tn(h2, dkv, norm_w=W["kv_norm_w"], name="attn_kv_wg", tn=1024)
    dh2, g_kv_nw = _mm_nt(dkv, W["w_kv"], epi=(h2, W["kv_norm_w"], dh2a), name="attn_kv_dg", tk=1024)
    dh1, gf0 = _ffn_bwd(dh2, h1, ffn0, W["ffn_norm_w"][0], W["ffn_up_w"][0], W["ffn_conv_w"][0], W["ffn_conv_b"][0],
                        W["ffn_down_w"][0], "0")
    g_out = _mm_tn(yn, dh1, name="ssm_out_wg", tn=1024)
    dyn = _mm_nt(dh1, W["ssm_out_w"], name="ssm_out_dg", out_dtype=BF16, tn=1024, tk=1024)
    dxs, dB, dC, dz, ddt, g_gnw, dpar = _ssd_bwd(xbc_c, zx, dtg, par, gnw, y, st, dyn, name="ssd_bwd")
    dxbc_c = jnp.concatenate([dxs, dB, dC], axis=1)
    dhid, g_scw, g_scb = _ssm_conv_bwd_pre(zx, W["ssm_conv_w"], W["ssm_conv_b"].reshape(1, -1), dxbc_c,
                                           name="ssm_conv_bwd")
    dxbc = _conv_bwd_in(dhid, W["ssm_conv_w"], K=SSM_CONV, name="ssm_conv_bwd_in")
    ddt_t = ddt[:, :, :8].transpose(1, 0, 2).reshape(T, SSM_HEADS).astype(BF16)
    dzx = jnp.concatenate([dz, dxbc, jnp.pad(ddt_t, ((0, 0), (0, IN_PROJ_PAD - IN_PROJ_DIM)))], axis=1)
    g_in = _mm_tn(x, dzx, norm_w=W["ssm_norm_w"], name="ssm_in_wg", tn=896)
    dx, g_ssm_nw = _mm_nt(dzx, W["in_w"], epi=(x, W["ssm_norm_w"], dh1), name="ssm_in_dg", tk=1792)
    f["ssm_norm_w"] = g_ssm_nw.reshape(-1)
    f["ssm_in_w"] = g_in[:, :IN_PROJ_DIM]
    f["ssm_conv_w"] = g_scw
    f["ssm_conv_b"] = g_scb.reshape(-1)
    f["ssm_dt_bias"] = dpar[:, 0, :8].reshape(-1)
    f["ssm_a_log"] = dpar[:, 1, :8].reshape(-1)
    f["ssm_d"] = dpar[:, 2, :8].reshape(-1)
    f["ssm_gate_norm_w"] = g_gnw.reshape(-1)
    f["ssm_out_w"] = g_out
    f["kv_norm_w"] = g_kv_nw.reshape(-1)
    f["w_k"] = g_wkv[:, :D_MODEL]
    f["w_v"] = g_wkv[:, D_MODEL:]
    f["attn_norm_w"] = g_attn_nw.reshape(-1)
    f["w_q"] = g_wq
    f["w_o"] = g_wo
    f["ffn_norm_w"] = jnp.stack([gf0["norm"], gf1["norm"]])
    f["ffn_up_w"] = [gf0["up"], gf1["up"]]
    f["ffn_conv_w"] = jnp.stack([gf0["conv_w"], gf1["conv_w"]])
    f["ffn_conv_b"] = jnp.stack([gf0["conv_b"], gf1["conv_b"]])
    f["ffn_down_w"] = [gf0["down"], gf1["down"]]
    f["final_norm_w"] = g_final.reshape(-1)
    return loss, dx, f


_BIG = ["ssm_in_w", "ssm_out_w", "w_k", "w_v", "w_q", "w_o", "ffn_up_w", "ffn_down_w"]
_SMALL_SHARDED = ["ssm_norm_w", "ssm_conv_w", "ssm_conv_b", "ssm_gate_norm_w", "ffn_conv_w"]
_SMALL_REPL = ["ssm_dt_bias", "ssm_a_log", "ssm_d", "kv_norm_w", "attn_norm_w", "ffn_norm_w", "ffn_conv_b",
               "final_norm_w"]
_WEIGHTS = ["ssm_norm_w", "ssm_in_w", "ssm_conv_w", "ssm_conv_b", "ssm_dt_bias", "ssm_a_log", "ssm_d",
            "ssm_gate_norm_w", "ssm_out_w", "kv_norm_w", "w_k", "w_v", "attn_norm_w", "w_q", "w_o", "ffn_norm_w",
            "ffn_up_w", "ffn_conv_w", "ffn_conv_b", "ffn_down_w", "final_norm_w"]


def _as2d(a):
    return a.reshape(-1, a.shape[-1])


def _cols_to_full(g):
    return g.transpose(1, 0, 2).reshape(g.shape[1], N_DEV * g.shape[2])


def _full_to_cols(a):
    R = a.shape[0]
    return a.reshape(R, N_DEV, -1).transpose(1, 0, 2)


def _gather_weights(p):
    names = _BIG + _SMALL_SHARDED
    shards = [_as2d(p[n]).astype(BF16) for n in _BIG] + [_as2d(p[n]) for n in _SMALL_SHARDED]
    got = dict(zip(names, _all_gather(shards, name="gather_weights")))
    W = {n: p[n] for n in _SMALL_REPL}
    in_w = _cols_to_full(got["ssm_in_w"])
    W["in_w"] = jnp.pad(in_w, ((0, 0), (0, IN_PROJ_PAD - IN_PROJ_DIM)))
    W["ssm_out_w"] = got["ssm_out_w"].reshape(D_INNER, D_MODEL)
    W["w_kv"] = jnp.concatenate([got["w_k"].reshape(D_MODEL, D_MODEL), got["w_v"].reshape(D_MODEL, D_MODEL)], axis=1)
    W["w_q"] = got["w_q"].reshape(D_MODEL, D_MODEL)
    W["w_o"] = got["w_o"].reshape(D_MODEL, D_MODEL)
    up = got["ffn_up_w"]
    W["ffn_up_w"] = [_cols_to_full(up[:, l * D_MODEL:(l + 1) * D_MODEL]) for l in range(2)]
    dn = got["ffn_down_w"]
    rs = D_FF // N_DEV
    W["ffn_down_w"] = [dn[:, l * rs:(l + 1) * rs].reshape(D_FF, D_MODEL) for l in range(2)]
    W["ssm_norm_w"] = got["ssm_norm_w"].reshape(D_MODEL)
    W["ssm_conv_w"] = _cols_to_full(got["ssm_conv_w"])
    W["ssm_conv_b"] = got["ssm_conv_b"].reshape(CONV_DIM)
    W["ssm_gate_norm_w"] = got["ssm_gate_norm_w"].reshape(D_INNER)
    fcw = _cols_to_full(got["ffn_conv_w"])
    W["ffn_conv_w"] = fcw.reshape(2, FFN_CONV, 2 * D_FF)
    for n in ("ssm_dt_bias", "ssm_a_log", "ssm_d", "attn_norm_w"):
        W[n] = W[n].reshape(-1)
    return W


def _big_grad_blocks(f):
    rs = D_FF // N_DEV
    return {
        "ssm_in_w": _full_to_cols(f["ssm_in_w"]),
        "ssm_out_w": f["ssm_out_w"].reshape(N_DEV, D_INNER // N_DEV, D_MODEL),
        "w_k": f["w_k"].reshape(N_DEV, D_MODEL // N_DEV, D_MODEL),
        "w_v": f["w_v"].reshape(N_DEV, D_MODEL // N_DEV, D_MODEL),
        "w_q": f["w_q"].reshape(N_DEV, D_MODEL // N_DEV, D_MODEL),
        "w_o": f["w_o"].reshape(N_DEV, D_MODEL // N_DEV, D_MODEL),
        "ffn_up_w": jnp.concatenate([_full_to_cols(g) for g in f["ffn_up_w"]], axis=1),
        "ffn_down_w": jnp.concatenate([g.reshape(N_DEV, rs, D_MODEL) for g in f["ffn_down_w"]], axis=1),
    }


def _pack_small(vals):
    flat = jnp.concatenate([v.reshape(-1).astype(F32) for v in vals])
    n = flat.shape[0]
    rows = -(-n // 1024) * 8
    return jnp.pad(flat, (0, rows * 128 - n)).reshape(rows, 128)


def _unpack_small(packed, shapes):
    flat = packed.reshape(-1)
    out, off = [], 0
    for s in shapes:
        n = math.prod(s)
        out.append(flat[off:off + n].reshape(s))
        off += n
    return out


def _kernel_v1(x, ssm_norm_w, ssm_in_w, ssm_conv_w, ssm_conv_b, ssm_dt_bias, ssm_a_log, ssm_d, ssm_gate_norm_w, ssm_out_w, kv_norm_w, w_k, w_v, attn_norm_w, w_q, w_o, ffn_norm_w, ffn_up_w, ffn_conv_w, ffn_conv_b, ffn_down_w, final_norm_w, loss_target, m_ssm_norm_w, m_ssm_in_w, m_ssm_conv_w, m_ssm_conv_b, m_ssm_dt_bias, m_ssm_a_log, m_ssm_d, m_ssm_gate_norm_w, m_ssm_out_w, m_kv_norm_w, m_w_k, m_w_v, m_attn_norm_w, m_w_q, m_w_o, m_ffn_norm_w, m_ffn_up_w, m_ffn_conv_w, m_ffn_conv_b, m_ffn_down_w, m_final_norm_w, v_ssm_norm_w, v_ssm_in_w, v_ssm_conv_w, v_ssm_conv_b, v_ssm_dt_bias, v_ssm_a_log, v_ssm_d, v_ssm_gate_norm_w, v_ssm_out_w, v_kv_norm_w, v_w_k, v_w_v, v_attn_norm_w, v_w_q, v_w_o, v_ffn_norm_w, v_ffn_up_w, v_ffn_conv_w, v_ffn_conv_b, v_ffn_down_w, v_final_norm_w):
    env = dict(locals())
    p = {n: env[n] for n in _WEIGHTS}
    mom = {n: env["m_" + n] for n in _WEIGHTS}
    var = {n: env["v_" + n] for n in _WEIGHTS}
    T = x.shape[1]
    me = 4 * lax.axis_index("x") + 2 * lax.axis_index("y") + lax.axis_index("c")

    W = _gather_weights(p)
    loss_row, dx, f = _local_step(x.reshape(T, D_MODEL), loss_target.reshape(T, D_MODEL), W)
    loss = lax.psum(loss_row[0, 0], ("x", "y", "c"))

    big = _big_grad_blocks(f)
    small_names = _SMALL_REPL + _SMALL_SHARDED
    small_full = _pack_small([f[n] for n in small_names])
    small_bcast = jnp.broadcast_to(small_full[None], (N_DEV,) + small_full.shape)
    got = _exchange([big[n] for n in _BIG] + [small_bcast], name="exchange_grads")
    big_parts = dict(zip(_BIG, got[:-1]))

    zero = jnp.zeros_like(small_full)
    g_small_sum = _adamw(got[-1], zero, zero, zero, name="sum_small_grads", tr=small_full.shape[0])[0]
    full_shapes = [f[n].shape for n in small_names]
    g_small = dict(zip(small_names, _unpack_small(g_small_sum, full_shapes)))
    for n in _SMALL_SHARDED:
        width = p[n].shape[-1]
        g_small[n] = lax.dynamic_slice_in_dim(g_small[n], me * width, width, axis=g_small[n].ndim - 1)

    out_g, out_d, out_m, out_v = {}, {}, {}, {}
    for n in _BIG:
        w2, m2, v2 = _as2d(p[n]), _as2d(mom[n]), _as2d(var[n])
        tr = 352 if n == "ffn_down_w" else 256
        g, d, nm, nv = _adamw(big_parts[n], w2, m2, v2, name="adamw_" + n, tr=tr)
        out_g[n], out_d[n], out_m[n], out_v[n] = (t.reshape(p[n].shape) for t in (g, d, nm, nv))
    sw = _pack_small([p[n] for n in small_names])
    sm = _pack_small([mom[n] for n in small_names])
    sv = _pack_small([var[n] for n in small_names])
    sg = _pack_small([g_small[n] for n in small_names])
    _, d, nm, nv = _adamw(sg[None], sw, sm, sv, name="adamw_small", tr=sw.shape[0])
    shard_shapes = [p[n].shape for n in small_names]
    for n, dd, mm, vv in zip(small_names, _unpack_small(d, shard_shapes), _unpack_small(nm, shard_shapes),
                             _unpack_small(nv, shard_shapes)):
        out_g[n] = g_small[n].reshape(p[n].shape)
        out_d[n], out_m[n], out_v[n] = dd, mm, vv

    return (loss, dx.reshape(x.shape), *[out_g[n] for n in _WEIGHTS], *[out_d[n] for n in _WEIGHTS],
            *[out_m[n] for n in _WEIGHTS], *[out_v[n] for n in _WEIGHTS])


def _tie(a, token):
    return a + token[0, 0].astype(a.dtype)


def _local_step2(x, tgt, get_w, put_g):
    T = x.shape[0]
    Ws = get_w("ssm", None)
    fnw, fcw, fcb = Ws["ffn_norm_w"], Ws["ffn_conv_w"], Ws["ffn_conv_b"]
    zx = _mm_fwd(x, Ws["in_w"], norm_w=Ws["ssm_norm_w"], name="ssm_in", tm=1024, tn=896)
    xbc_c = _ssm_conv_fwd(zx, Ws["ssm_conv_w"], Ws["ssm_conv_b"].reshape(1, -1), name="ssm_conv")
    dt_raw = zx[:, D_INNER + CONV_DIM:IN_PROJ_DIM]
    dtg = jnp.pad(dt_raw.reshape(T, SSM_GROUPS, 8).transpose(1, 0, 2), ((0, 0), (0, 0), (0, 120)))
    par = jnp.stack([Ws["ssm_dt_bias"].reshape(SSM_GROUPS, 8), Ws["ssm_a_log"].reshape(SSM_GROUPS, 8),
                     Ws["ssm_d"].reshape(SSM_GROUPS, 8)], axis=1)
    par = jnp.pad(par, ((0, 0), (0, 5), (0, 120)))
    gnw = _tie(Ws["ssm_gate_norm_w"].reshape(1, D_INNER), get_w("rest_start", xbc_c))
    y, yn, st = _ssd_fwd(xbc_c, zx, dtg, par, gnw, name="ssd_fwd")
    W0 = get_w("ffn0", y)
    Ws["ssm_out_w"] = W0["ssm_out_w"]
    h1 = _mm_fwd(yn, Ws["ssm_out_w"], residual=x, name="ssm_out", tm=1024, tn=512)
    h2, ffn0 = _ffn_fwd(h1, fnw[0], W0["up"], fcw[0], fcb[0], W0["down"], "0")
    Wr = get_w("rest", h2)
    q = _mm_fwd(h2, Wr["w_q"], norm_w=Ws["attn_norm_w"], out_dtype=BF16, name="attn_q", tm=1024, tn=1024)
    kv = _mm_fwd(h2, Wr["w_kv"], norm_w=Ws["kv_norm_w"], out_dtype=BF16, name="attn_kv", tm=1024, tn=1024)
    o, lt = _sba_fwd(q, kv, name="sba_fwd")
    h3 = _mm_fwd(o, Wr["w_o"], residual=h2, name="attn_o", tm=1024, tn=512)
    h4, ffn1 = _ffn_fwd(h3, fnw[1], Wr["up"], fcw[1], fcb[1], Wr["down"], "1")
    loss, dh4, g_final = _loss_head(h4, tgt, Ws["final_norm_w"], name="loss_head")
    dh3, gf1 = _ffn_bwd(dh4, h3, ffn1, fnw[1], Wr["up"], fcw[1], fcb[1], Wr["down"], "1")
    tok = put_g("ffn1", dict(up=gf1["up"], down=gf1["down"]))
    g_wo = _mm_tn(o, dh3, name="attn_o_wg", tn=1024)
    do = _mm_nt(dh3, _tie(Wr["w_o"], tok), name="attn_o_dg", out_dtype=BF16, tn=1024, tk=1024)
    dq, dk, dv = _sba_bwd(q, kv, lt, do, name="sba_bwd")
    g_wq = _mm_tn(h2, dq, norm_w=Ws["attn_norm_w"], name="attn_q_wg", tn=1024)
    dh2a, g_attn_nw = _mm_nt(dq, Wr["w_q"], epi=(h2, Ws["attn_norm_w"], dh3), name="attn_q_dg", tk=1024)
    dkv = jnp.concatenate([dk, dv], axis=1)
    g_wkv = _mm_tn(h2, dkv, norm_w=Ws["kv_norm_w"], name="attn_kv_wg", tn=1024)
    dh2, g_kv_nw = _mm_nt(dkv, Wr["w_kv"], epi=(h2, Ws["kv_norm_w"], dh2a), name="attn_kv_dg", tk=1024)
    tok = put_g("attn", dict(w_o=g_wo, w_q=g_wq, w_k=g_wkv[:, :D_MODEL], w_v=g_wkv[:, D_MODEL:]))
    dh1, gf0 = _ffn_bwd(dh2, h1, ffn0, fnw[0], W0["up"], fcw[0], _tie(fcb[0], tok), W0["down"], "0")
    tok = put_g("ffn0", dict(up=gf0["up"], down=gf0["down"]))
    g_out = _mm_tn(yn, dh1, name="ssm_out_wg", tn=1024)
    dyn = _mm_nt(dh1, _tie(Ws["ssm_out_w"], tok), name="ssm_out_dg", out_dtype=BF16, tn=1024, tk=1024)
    tok = put_g("ssm_out", dict(ssm_out_w=g_out))
    dxs, dB, dC, dz, ddt, g_gnw, dpar = _ssd_bwd(xbc_c, zx, dtg, par, _tie(gnw, tok), y, st, dyn, name="ssd_bwd")
    dxbc_c = jnp.concatenate([dxs, dB, dC], axis=1)
    dhid, g_scw, g_scb = _ssm_conv_bwd_pre(zx, Ws["ssm_conv_w"], Ws["ssm_conv_b"].reshape(1, -1), dxbc_c,
                                           name="ssm_conv_bwd")
    dxbc = _conv_bwd_in(dhid, Ws["ssm_conv_w"], K=SSM_CONV, name="ssm_conv_bwd_in")
    ddt_t = ddt[:, :, :8].transpose(1, 0, 2).reshape(T, SSM_HEADS).astype(BF16)
    dzx = jnp.concatenate([dz, dxbc, jnp.pad(ddt_t, ((0, 0), (0, IN_PROJ_PAD - IN_PROJ_DIM)))], axis=1)
    g_in = _mm_tn(x, dzx, norm_w=Ws["ssm_norm_w"], name="ssm_in_wg", tn=896)
    tok = put_g("ssm_in", dict(ssm_in_w=g_in[:, :IN_PROJ_DIM]))
    dx, g_ssm_nw = _mm_nt(dzx, Ws["in_w"], epi=(x, _tie(Ws["ssm_norm_w"], tok), dh1), name="ssm_in_dg", tk=1792)
    f = {
        "ssm_norm_w": g_ssm_nw.reshape(-1), "ssm_conv_w": g_scw,
        "ssm_conv_b": g_scb.reshape(-1), "ssm_dt_bias": dpar[:, 0, :8].reshape(-1),
        "ssm_a_log": dpar[:, 1, :8].reshape(-1), "ssm_d": dpar[:, 2, :8].reshape(-1),
        "ssm_gate_norm_w": g_gnw.reshape(-1), "kv_norm_w": g_kv_nw.reshape(-1), "attn_norm_w": g_attn_nw.reshape(-1),
        "ffn_norm_w": jnp.stack([gf0["norm"], gf1["norm"]]), "ffn_conv_w": jnp.stack([gf0["conv_w"], gf1["conv_w"]]),
        "ffn_conv_b": jnp.stack([gf0["conv_b"], gf1["conv_b"]]), "final_norm_w": g_final.reshape(-1),
    }
    return loss, dx, f


def kernel(x, ssm_norm_w, ssm_in_w, ssm_conv_w, ssm_conv_b, ssm_dt_bias, ssm_a_log, ssm_d, ssm_gate_norm_w, ssm_out_w, kv_norm_w, w_k, w_v, attn_norm_w, w_q, w_o, ffn_norm_w, ffn_up_w, ffn_conv_w, ffn_conv_b, ffn_down_w, final_norm_w, loss_target, m_ssm_norm_w, m_ssm_in_w, m_ssm_conv_w, m_ssm_conv_b, m_ssm_dt_bias, m_ssm_a_log, m_ssm_d, m_ssm_gate_norm_w, m_ssm_out_w, m_kv_norm_w, m_w_k, m_w_v, m_attn_norm_w, m_w_q, m_w_o, m_ffn_norm_w, m_ffn_up_w, m_ffn_conv_w, m_ffn_conv_b, m_ffn_down_w, m_final_norm_w, v_ssm_norm_w, v_ssm_in_w, v_ssm_conv_w, v_ssm_conv_b, v_ssm_dt_bias, v_ssm_a_log, v_ssm_d, v_ssm_gate_norm_w, v_ssm_out_w, v_kv_norm_w, v_w_k, v_w_v, v_attn_norm_w, v_w_q, v_w_o, v_ffn_norm_w, v_ffn_up_w, v_ffn_conv_w, v_ffn_conv_b, v_ffn_down_w, v_final_norm_w):
    env = dict(locals())
    p = {n: env[n] for n in _WEIGHTS}
    mom = {n: env["m_" + n] for n in _WEIGHTS}
    var = {n: env["v_" + n] for n in _WEIGHTS}
    T = x.shape[1]
    me = 4 * lax.axis_index("x") + 2 * lax.axis_index("y") + lax.axis_index("c")
    rs = D_FF // N_DEV

    def bf2(a):
        return _as2d(a).astype(BF16)

    def with_own(srcs, lands, scatter):
        out = []
        for s, l in zip(srcs, lands):
            own = lax.dynamic_index_in_dim(s, me, 0, keepdims=False) if scatter else s
            out.append(lax.dynamic_update_index_in_dim(l, own, me, 0))
        return out

    a_names = ["ssm_in_w"] + _SMALL_SHARDED
    got_a = dict(zip(a_names, _all_gather([bf2(p["ssm_in_w"])] + [_as2d(p[n]) for n in _SMALL_SHARDED],
                                          name="gather_ssm")))
    ffn0_names = ["ssm_out_w", "up0", "down0"]
    rest_names = ["w_q", "w_k", "w_v", "w_o", "up1", "down1"]
    shard = {"up0": bf2(p["ffn_up_w"][0]), "down0": bf2(p["ffn_down_w"][0]), "up1": bf2(p["ffn_up_w"][1]),
             "down1": bf2(p["ffn_down_w"][1]), "w_q": bf2(p["w_q"]), "w_k": bf2(p["w_k"]), "w_v": bf2(p["w_v"]),
             "w_o": bf2(p["w_o"]), "ssm_out_w": bf2(p["ssm_out_w"])}
    h_ffn0 = _push_start([shard[n] for n in ffn0_names], scatter=False, name="gather_ffn0_start")
    handles = {}

    def get_w(group, after):
        if group == "ssm":
            W = {n: p[n] for n in _SMALL_REPL}
            for n in ("ssm_dt_bias", "ssm_a_log", "ssm_d", "attn_norm_w"):
                W[n] = W[n].reshape(-1)
            W["in_w"] = jnp.pad(_cols_to_full(got_a["ssm_in_w"]), ((0, 0), (0, IN_PROJ_PAD - IN_PROJ_DIM)))
            W["ssm_norm_w"] = _tie(got_a["ssm_norm_w"].reshape(D_MODEL), h_ffn0["token"])
            W["ssm_conv_w"] = _cols_to_full(got_a["ssm_conv_w"])
            W["ssm_conv_b"] = got_a["ssm_conv_b"].reshape(CONV_DIM)
            W["ssm_gate_norm_w"] = got_a["ssm_gate_norm_w"].reshape(D_INNER)
            W["ffn_conv_w"] = _cols_to_full(got_a["ffn_conv_w"]).reshape(2, FFN_CONV, 2 * D_FF)
            return W
        if group == "rest_start":
            anchor = after[0, 0]
            first = shard[rest_names[0]] + (jnp.where(jnp.isfinite(anchor), anchor, 0.0) * 0.0).astype(BF16)
            handles["rest"] = _push_start([first] + [shard[n] for n in rest_names[1:]], scatter=False,
                                          name="gather_rest_start")
            return handles["rest"]["token"]
        if group == "ffn0":
            srcs, lands = _push_wait(h_ffn0, after, name="gather_ffn0_wait")
            out, up, down = with_own(srcs, lands, False)
            return dict(ssm_out_w=out.reshape(D_INNER, D_MODEL), up=_cols_to_full(up), down=down.reshape(D_FF, D_MODEL))
        srcs, lands = _push_wait(handles["rest"], after, name="gather_rest_wait")
        g = dict(zip(rest_names, with_own(srcs, lands, False)))
        sq = lambda a: a.reshape(D_MODEL, D_MODEL)
        return dict(w_q=sq(g["w_q"]), w_kv=jnp.concatenate([sq(g["w_k"]), sq(g["w_v"])], axis=1), w_o=sq(g["w_o"]),
                    up=_cols_to_full(g["up1"]), down=g["down1"].reshape(D_FF, D_MODEL))

    pending = []

    def put_g(group, g):
        if group in ("ffn0", "ffn1"):
            keys = [("ffn_up_w", int(group[-1])), ("ffn_down_w", int(group[-1]))]
            blocks = [_full_to_cols(g["up"]), g["down"].reshape(N_DEV, rs, D_MODEL)]
        elif group == "attn":
            keys = [(n, None) for n in ("w_o", "w_q", "w_k", "w_v")]
            blocks = [g[n].reshape(N_DEV, D_MODEL // N_DEV, D_MODEL) for n, _ in keys]
        elif group == "ssm_out":
            keys = [("ssm_out_w", None)]
            blocks = [g["ssm_out_w"].reshape(N_DEV, D_INNER // N_DEV, D_MODEL)]
        else:
            keys = [("ssm_in_w", None)]
            blocks = [_full_to_cols(g["ssm_in_w"])]
        h = _push_start(blocks, scatter=True, name=f"exchange_{group}_start")
        pending.append((group, keys, h))
        return h["token"]

    loss_row, dx, f = _local_step2(x.reshape(T, D_MODEL), loss_target.reshape(T, D_MODEL), get_w, put_g)
    loss = lax.psum(loss_row[0, 0], ("x", "y", "c"))

    small_names = _SMALL_REPL + _SMALL_SHARDED
    small_full = _pack_small([f[n] for n in small_names])
    small_bcast = jnp.broadcast_to(small_full[None], (N_DEV,) + small_full.shape)
    h_small = _push_start([small_bcast], scatter=True, name="exchange_small_start")
    tok = h_small["token"]

    res = {}
    for group, keys, h in pending:
        srcs, lands = _push_wait(h, dx, name=f"exchange_{group}_wait")
        for (n, layer), parts in zip(keys, with_own(srcs, lands, True)):
            sel = (lambda a: a) if layer is None else (lambda a: a[layer])
            w2, m2, v2 = _as2d(sel(p[n])), _as2d(sel(mom[n])), _as2d(sel(var[n]))
            if not res:
                w2 = _tie(w2, tok)
            tr = rs if n == "ffn_down_w" else 256
            res[(n, layer)] = _adamw(parts, w2, m2, v2, name=f"adamw_{n}" + ("" if layer is None else str(layer)), tr=tr)
    srcs, lands = _push_wait(h_small, res[("ssm_in_w", None)][0], name="exchange_small_wait")
    small_parts = with_own(srcs, lands, True)[0]
    out_g, out_d, out_m, out_v = {}, {}, {}, {}
    for n in _BIG:
        if (n, None) in res:
            quad = res[(n, None)]
        else:
            quad = [jnp.stack([res[(n, 0)][k], res[(n, 1)][k]]) for k in range(4)]
        out_g[n], out_d[n], out_m[n], out_v[n] = (t.reshape(p[n].shape) for t in quad)

    zero = jnp.zeros_like(small_full)
    g_small_sum = _adamw(small_parts, zero, zero, zero, name="sum_small_grads", tr=small_full.shape[0])[0]
    g_small = dict(zip(small_names, _unpack_small(g_small_sum, [f[n].shape for n in small_names])))
    for n in _SMALL_SHARDED:
        width = p[n].shape[-1]
        g_small[n] = lax.dynamic_slice_in_dim(g_small[n], me * width, width, axis=g_small[n].ndim - 1)
    sw = _pack_small([p[n] for n in small_names])
    sm = _pack_small([mom[n] for n in small_names])
    sv = _pack_small([var[n] for n in small_names])
    sg = _pack_small([g_small[n] for n in small_names])
    _, d, nm, nv = _adamw(sg[None], sw, sm, sv, name="adamw_small", tr=sw.shape[0])
    shard_shapes = [p[n].shape for n in small_names]
    for n, dd, mm, vv in zip(small_names, _unpack_small(d, shard_shapes), _unpack_small(nm, shard_shapes),
                             _unpack_small(nv, shard_shapes)):
        out_g[n] = g_small[n].reshape(p[n].shape)
        out_d[n], out_m[n], out_v[n] = dd, mm, vv

    return (loss, dx.reshape(x.shape), *[out_g[n] for n in _WEIGHTS], *[out_d[n] for n in _WEIGHTS],
            *[out_m[n] for n in _WEIGHTS], *[out_v[n] for n in _WEIGHTS])
```

```python
import functools
import math

import jax
import jax.numpy as jnp
from jax import lax
from jax.experimental import pallas as pl
from jax.experimental.pallas import tpu as pltpu

F32 = jnp.float32
BF16 = jnp.bfloat16
EPS = 1e-6

D_MODEL = 1024
D_INNER = 2048
SSM_HEADS = 32
SSM_GROUPS = 4
SSM_STATE = 128
SSM_CONV = 4
SSM_CHUNK = 128
GN = SSM_GROUPS * SSM_STATE
CONV_DIM = D_INNER + 2 * GN
IN_PROJ_DIM = D_INNER + CONV_DIM + SSM_HEADS
IN_PROJ_PAD = 5376
SB_HEADS = 16
SB_HEAD_DIM = 64
SB_BLOCK = 128
D_FF = 2816
FFN_CONV = 3
N_DEV = 8

ADAM_LR = 0.001
ADAM_B1 = 0.9
ADAM_B2 = 0.999
ADAM_EPS = 1e-08
ADAM_WD = 0.01
ADAM_STEP = 10

_MESH = pl.DeviceIdType.MESH
_NT = (((1,), (1,)), ((), ()))
_TN = (((0,), (0,)), ((), ()))
_ANY = pl.BlockSpec(memory_space=pl.ANY)


def _cparams(sem, vmem_mb=48):
    return pltpu.CompilerParams(dimension_semantics=sem, vmem_limit_bytes=vmem_mb * 1024 * 1024)


def _sigmoid(x):
    return 1.0 / (1.0 + jnp.exp(-x))


def _softplus(x):
    return jnp.maximum(x, 0.0) + jnp.log(1.0 + jnp.exp(-jnp.abs(x)))


def _rms_fwd(xv, w):
    r = lax.rsqrt(jnp.mean(xv * xv, axis=-1, keepdims=True) + EPS)
    return xv * r * w


def _mm_fwd(x, w, *, name, norm_w=None, residual=None, out_dtype=F32, tm=512, tn=512):
    M, K = x.shape
    N = w.shape[1]
    tm, tn = min(tm, M), min(tn, N)
    assert M % tm == 0 and N % tn == 0, (name, M, N, tm, tn)
    has_norm, has_res = norm_w is not None, residual is not None

    def body(*refs):
        x_ref, w_ref = refs[0], refs[1]
        p = 2
        nw_ref = r_ref = None
        if has_norm:
            nw_ref = refs[p]
            p += 1
        if has_res:
            r_ref = refs[p]
            p += 1
        o_ref, xn_ref = refs[p], refs[p + 1]

        @pl.when(pl.program_id(1) == 0)
        def _():
            xv = x_ref[...].astype(F32)
            if has_norm:
                xv = _rms_fwd(xv, nw_ref[...])
            xn_ref[...] = xv.astype(BF16)

        acc = jnp.dot(xn_ref[...], w_ref[...], preferred_element_type=F32)
        if has_res:
            acc = acc + r_ref[...]
        o_ref[...] = acc.astype(out_dtype)

    in_specs = [pl.BlockSpec((tm, K), lambda i, j: (i, 0)), pl.BlockSpec((K, tn), lambda i, j: (0, j))]
    args = [x, w]
    if has_norm:
        in_specs.append(pl.BlockSpec((1, K), lambda i, j: (0, 0)))
        args.append(norm_w.reshape(1, K))
    if has_res:
        in_specs.append(pl.BlockSpec((tm, tn), lambda i, j: (i, j)))
        args.append(residual)
    return pl.pallas_call(
        body, name=name, grid=(M // tm, N // tn), in_specs=in_specs,
        out_specs=pl.BlockSpec((tm, tn), lambda i, j: (i, j)),
        out_shape=jax.ShapeDtypeStruct((M, N), out_dtype),
        scratch_shapes=[pltpu.VMEM((tm, K), BF16)],
        compiler_params=_cparams(("parallel", "arbitrary")))(*args)


def _mm_nt(dy, w, *, name, epi=None, out_dtype=F32, tm=512, tn=512, tk=512):
    M, K = dy.shape
    N = w.shape[0]
    tm, tk = min(tm, M), min(tk, K)
    tn = N if epi is not None else min(tn, N)
    assert M % tm == 0 and N % tn == 0 and K % tk == 0, (name, M, N, K, tm, tn, tk)
    nk = K // tk
    has_epi = epi is not None

    def body(*refs):
        if has_epi:
            dy_ref, w_ref, h_ref, nw_ref, r_ref, o_ref, dnw_ref, acc_ref = refs
        else:
            dy_ref, w_ref, o_ref, acc_ref = refs
        i = pl.program_id(0)
        k = pl.program_id(2)

        @pl.when(k == 0)
        def _():
            acc_ref[...] = jnp.zeros_like(acc_ref)

        acc_ref[...] += lax.dot_general(dy_ref[...].astype(BF16), w_ref[...], _NT, preferred_element_type=F32)

        @pl.when(k == nk - 1)
        def _():
            du = acc_ref[...]
            if has_epi:
                hv = h_ref[...]
                r = lax.rsqrt(jnp.mean(hv * hv, axis=-1, keepdims=True) + EPS)
                xhat = hv * r
                dxh = du * nw_ref[...]
                dx = r * (dxh - xhat * jnp.mean(dxh * xhat, axis=-1, keepdims=True))
                o_ref[...] = (r_ref[...] + dx).astype(out_dtype)
                contrib = jnp.sum(du * xhat, axis=0, keepdims=True)

                @pl.when(i == 0)
                def _():
                    dnw_ref[...] = contrib

                @pl.when(i > 0)
                def _():
                    dnw_ref[...] += contrib
            else:
                o_ref[...] = du.astype(out_dtype)

    in_specs = [pl.BlockSpec((tm, tk), lambda i, j, k: (i, k)), pl.BlockSpec((tn, tk), lambda i, j, k: (j, k))]
    args = [dy, w]
    out_specs = [pl.BlockSpec((tm, tn), lambda i, j, k: (i, j))]
    out_shape = [jax.ShapeDtypeStruct((M, N), out_dtype)]
    if has_epi:
        h, nw, res = epi
        in_specs += [pl.BlockSpec((tm, N), lambda i, j, k: (i, 0)), pl.BlockSpec((1, N), lambda i, j, k: (0, 0)),
                     pl.BlockSpec((tm, N), lambda i, j, k: (i, 0))]
        args += [h, nw.reshape(1, N), res]
        out_specs.append(pl.BlockSpec((1, N), lambda i, j, k: (0, 0)))
        out_shape.append(jax.ShapeDtypeStruct((1, N), F32))
    outs = pl.pallas_call(
        body, name=name, grid=(M // tm, N // tn, nk), in_specs=in_specs, out_specs=out_specs, out_shape=out_shape,
        scratch_shapes=[pltpu.VMEM((tm, tn), F32)],
        compiler_params=_cparams(("arbitrary", "arbitrary", "arbitrary")))(*args)
    return (outs[0], outs[1]) if has_epi else outs[0]


def _mm_tn(x, dy, *, name, norm_w=None, out_dtype=BF16, tk1=1024, tn=512, tt=512):
    T, K1 = x.shape
    N = dy.shape[1]
    tk1, tn, tt = min(tk1, K1), min(tn, N), min(tt, T)
    has_norm = norm_w is not None
    assert K1 % tk1 == 0 and N % tn == 0 and T % tt == 0, (name, K1, N, T, tk1, tn, tt)
    assert not has_norm or tk1 == K1
    nt = T // tt

    def body(*refs):
        if has_norm:
            x_ref, dy_ref, nw_ref, o_ref, acc_ref = refs
        else:
            x_ref, dy_ref, o_ref, acc_ref = refs
        t = pl.program_id(2)

        @pl.when(t == 0)
        def _():
            acc_ref[...] = jnp.zeros_like(acc_ref)

        xv = x_ref[...]
        if has_norm:
            xv = _rms_fwd(xv.astype(F32), nw_ref[...])
        acc_ref[...] += lax.dot_general(xv.astype(BF16), dy_ref[...].astype(BF16), _TN, preferred_element_type=F32)

        @pl.when(t == nt - 1)
        def _():
            o_ref[...] = acc_ref[...].astype(out_dtype)

    in_specs = [pl.BlockSpec((tt, tk1), lambda a, b, t: (t, a)), pl.BlockSpec((tt, tn), lambda a, b, t: (t, b))]
    args = [x, dy]
    if has_norm:
        in_specs.append(pl.BlockSpec((1, K1), lambda a, b, t: (0, 0)))
        args.append(norm_w.reshape(1, K1))
    return pl.pallas_call(
        body, name=name, grid=(K1 // tk1, N // tn, nt), in_specs=in_specs,
        out_specs=pl.BlockSpec((tk1, tn), lambda a, b, t: (a, b)),
        out_shape=jax.ShapeDtypeStruct((K1, N), out_dtype),
        scratch_shapes=[pltpu.VMEM((tk1, tn), F32)],
        compiler_params=_cparams(("parallel", "parallel", "arbitrary")))(*args)


def _shift_down(xb, prev8, j):
    main = pltpu.roll(xb, j, 0)
    head = pltpu.roll(xb[0:8], j, 0)
    ph = pltpu.roll(prev8, j, 0)
    row8 = lax.broadcasted_iota(jnp.int32, head.shape, 0)
    head = jnp.where(row8 < j, ph, head)
    return jnp.concatenate([head, main[8:]], axis=0)


def _shift_up(xb, next8, j):
    tt = xb.shape[0]
    main = pltpu.roll(xb, tt - j, 0)
    tail = pltpu.roll(xb[tt - 8:tt], 8 - j, 0)
    nh = pltpu.roll(next8, 8 - j, 0)
    row8 = lax.broadcasted_iota(jnp.int32, tail.shape, 0)
    tail = jnp.where(row8 + j >= 8, nh, tail)
    return jnp.concatenate([main[:tt - 8], tail], axis=0)


def _conv_hid(xb, prev8, w, b_row, K):
    out = b_row
    shifted = []
    for j in range(K):
        sh = K - 1 - j
        xs = xb if sh == 0 else _shift_down(xb, prev8, sh)
        shifted.append(xs)
        out = out + xs * w[j:j + 1, :]
    return out, shifted


def _prev_idx(i, nb8):
    return jnp.maximum(i * nb8 - 1, 0)


def _ssm_conv_fwd(zx, w, b, *, name, tt=512, tc=512):
    T = zx.shape[0]
    tt = min(tt, T)
    C, K = CONV_DIM, SSM_CONV
    cb0, nb8 = D_INNER // tc, tt // 8

    def body(x_ref, p_ref, w_ref, b_ref, o_ref):
        first = (pl.program_id(1) > 0).astype(F32)
        hid, _ = _conv_hid(x_ref[...], p_ref[...] * first, w_ref[...], b_ref[...], K)
        o_ref[...] = hid * _sigmoid(hid)

    return pl.pallas_call(
        body, name=name, grid=(C // tc, T // tt),
        in_specs=[pl.BlockSpec((tt, tc), lambda c, i: (i, c + cb0)),
                  pl.BlockSpec((8, tc), lambda c, i: (_prev_idx(i, nb8), c + cb0)),
                  pl.BlockSpec((K, tc), lambda c, i: (0, c)), pl.BlockSpec((1, tc), lambda c, i: (0, c))],
        out_specs=pl.BlockSpec((tt, tc), lambda c, i: (i, c)),
        out_shape=jax.ShapeDtypeStruct((T, C), F32),
        compiler_params=_cparams(("parallel", "parallel")))(zx, zx, w, b)


def _ssm_conv_bwd_pre(zx, w, b, dout, *, name, tt=512, tc=512):
    T = zx.shape[0]
    tt = min(tt, T)
    C, K = CONV_DIM, SSM_CONV
    cb0, nb8 = D_INNER // tc, tt // 8

    def body(x_ref, p_ref, w_ref, b_ref, d_ref, dh_ref, dw_ref, db_ref):
        t = pl.program_id(1)
        first = (t > 0).astype(F32)
        hid, shifted = _conv_hid(x_ref[...], p_ref[...] * first, w_ref[...], b_ref[...], K)
        sg = _sigmoid(hid)
        dh = d_ref[...] * (sg * (1.0 + hid * (1.0 - sg)))
        dh_ref[...] = dh

        @pl.when(t == 0)
        def _():
            dw_ref[...] = jnp.zeros_like(dw_ref)
            db_ref[...] = jnp.zeros_like(db_ref)

        db_ref[...] += jnp.sum(dh, axis=0, keepdims=True)
        for j in range(K):
            dw_ref[j:j + 1, :] += jnp.sum(dh * shifted[j], axis=0, keepdims=True)

    return pl.pallas_call(
        body, name=name, grid=(C // tc, T // tt),
        in_specs=[pl.BlockSpec((tt, tc), lambda c, i: (i, c + cb0)),
                  pl.BlockSpec((8, tc), lambda c, i: (_prev_idx(i, nb8), c + cb0)),
                  pl.BlockSpec((K, tc), lambda c, i: (0, c)), pl.BlockSpec((1, tc), lambda c, i: (0, c)),
                  pl.BlockSpec((tt, tc), lambda c, i: (i, c))],
        out_specs=[pl.BlockSpec((tt, tc), lambda c, i: (i, c)), pl.BlockSpec((K, tc), lambda c, i: (0, c)),
                   pl.BlockSpec((1, tc), lambda c, i: (0, c))],
        out_shape=[jax.ShapeDtypeStruct((T, C), F32), jax.ShapeDtypeStruct((K, C), F32),
                   jax.ShapeDtypeStruct((1, C), F32)],
        compiler_params=_cparams(("parallel", "arbitrary")))(zx, zx, w, b, dout)


def _conv_bwd_in(dh, w, *, name, K, tt=512, tc=512, out_dtype=BF16):
    T, C = dh.shape
    tt = min(tt, T)
    nb8, nT = tt // 8, T // tt
    last8 = T // 8 - 1

    def body(d_ref, n_ref, w_ref, o_ref):
        notlast = (pl.program_id(1) < nT - 1).astype(F32)
        d = d_ref[...]
        nxt = n_ref[...] * notlast
        w_ = w_ref[...]
        acc = d * w_[K - 1:K, :]
        for sh in range(1, K):
            acc = acc + _shift_up(d, nxt, sh) * w_[K - 1 - sh:K - sh, :]
        o_ref[...] = acc.astype(out_dtype)

    return pl.pallas_call(
        body, name=name, grid=(C // tc, nT),
        in_specs=[pl.BlockSpec((tt, tc), lambda c, i: (i, c)),
                  pl.BlockSpec((8, tc), lambda c, i: (jnp.minimum((i + 1) * nb8, last8), c)),
                  pl.BlockSpec((K, tc), lambda c, i: (0, c))],
        out_specs=pl.BlockSpec((tt, tc), lambda c, i: (i, c)),
        out_shape=jax.ShapeDtypeStruct((T, C), out_dtype),
        compiler_params=_cparams(("parallel", "parallel")))(dh, dh, w)


def _ffn_conv_fwd(a, w, b, *, name, tt=256, tc=1408):
    T = a.shape[0]
    tt = min(tt, T)
    K, nbh, nb8 = FFN_CONV, D_FF // tc, tt // 8

    def body(ag_ref, pg_ref, av_ref, pv_ref, wg_ref, wv_ref, bg_ref, bv_ref, o_ref):
        first = (pl.program_id(1) > 0).astype(F32)
        hg, _ = _conv_hid(ag_ref[...], pg_ref[...] * first, wg_ref[...], bg_ref[...], K)
        hv, _ = _conv_hid(av_ref[...], pv_ref[...] * first, wv_ref[...], bv_ref[...], K)
        o_ref[...] = (hg * _sigmoid(hg) * hv).astype(BF16)

    return pl.pallas_call(
        body, name=name, grid=(nbh, T // tt),
        in_specs=[pl.BlockSpec((tt, tc), lambda c, i: (i, c)),
                  pl.BlockSpec((8, tc), lambda c, i: (_prev_idx(i, nb8), c)),
                  pl.BlockSpec((tt, tc), lambda c, i: (i, c + nbh)),
                  pl.BlockSpec((8, tc), lambda c, i: (_prev_idx(i, nb8), c + nbh)),
                  pl.BlockSpec((K, tc), lambda c, i: (0, c)), pl.BlockSpec((K, tc), lambda c, i: (0, c + nbh)),
                  pl.BlockSpec((1, tc), lambda c, i: (0, c)), pl.BlockSpec((1, tc), lambda c, i: (0, c + nbh))],
        out_specs=pl.BlockSpec((tt, tc), lambda c, i: (i, c)),
        out_shape=jax.ShapeDtypeStruct((T, D_FF), BF16),
        compiler_params=_cparams(("parallel", "parallel")))(a, a, a, a, w, w, b, b)


def _ffn_conv_bwd_pre(a, w, b, dp, *, name, tt=256, tc=1408):
    T = a.shape[0]
    tt = min(tt, T)
    K, nbh, nb8 = FFN_CONV, D_FF // tc, tt // 8

    def body(ao_ref, po_ref, ag_ref, pg_ref, av_ref, pv_ref, wg_ref, wv_ref, bg_ref, bv_ref, dp_ref,
             dh_ref, dw_ref, db_ref):
        j = pl.program_id(0)
        t = pl.program_id(1)
        first = (t > 0).astype(F32)
        hg, _ = _conv_hid(ag_ref[...], pg_ref[...] * first, wg_ref[...], bg_ref[...], K)
        hv, _ = _conv_hid(av_ref[...], pv_ref[...] * first, wv_ref[...], bv_ref[...], K)
        sg = _sigmoid(hg)
        d = dp_ref[...].astype(F32)
        is_gate = (j < nbh).astype(F32)
        dh = d * (is_gate * (hv * (sg * (1.0 + hg * (1.0 - sg)))) + (1.0 - is_gate) * (hg * sg))
        dh_ref[...] = dh
        xo = ao_ref[...]
        po = po_ref[...] * first

        @pl.when(t == 0)
        def _():
            dw_ref[...] = jnp.zeros_like(dw_ref)
            db_ref[...] = jnp.zeros_like(db_ref)

        db_ref[...] += jnp.sum(dh, axis=0, keepdims=True)
        for jj in range(K):
            sh = K - 1 - jj
            xs = xo if sh == 0 else _shift_down(xo, po, sh)
            dw_ref[jj:jj + 1, :] += jnp.sum(dh * xs, axis=0, keepdims=True)

    def gi(c):
        return lax.rem(c, nbh)

    return pl.pallas_call(
        body, name=name, grid=(2 * nbh, T // tt),
        in_specs=[pl.BlockSpec((tt, tc), lambda c, i: (i, c)),
                  pl.BlockSpec((8, tc), lambda c, i: (_prev_idx(i, nb8), c)),
                  pl.BlockSpec((tt, tc), lambda c, i: (i, gi(c))),
                  pl.BlockSpec((8, tc), lambda c, i: (_prev_idx(i, nb8), gi(c))),
                  pl.BlockSpec((tt, tc), lambda c, i: (i, gi(c) + nbh)),
                  pl.BlockSpec((8, tc), lambda c, i: (_prev_idx(i, nb8), gi(c) + nbh)),
                  pl.BlockSpec((K, tc), lambda c, i: (0, gi(c))), pl.BlockSpec((K, tc), lambda c, i: (0, gi(c) + nbh)),
                  pl.BlockSpec((1, tc), lambda c, i: (0, gi(c))), pl.BlockSpec((1, tc), lambda c, i: (0, gi(c) + nbh)),
                  pl.BlockSpec((tt, tc), lambda c, i: (i, gi(c)))],
        out_specs=[pl.BlockSpec((tt, tc), lambda c, i: (i, c)), pl.BlockSpec((K, tc), lambda c, i: (0, c)),
                   pl.BlockSpec((1, tc), lambda c, i: (0, c))],
        out_shape=[jax.ShapeDtypeStruct((T, 2 * D_FF), F32), jax.ShapeDtypeStruct((K, 2 * D_FF), F32),
                   jax.ShapeDtypeStruct((1, 2 * D_FF), F32)],
        compiler_params=_cparams(("parallel", "arbitrary")))(a, a, a, a, a, a, w, w, b, b, dp)


def _cumsum_rows(x):
    L = x.shape[0]
    row = lax.broadcasted_iota(jnp.int32, x.shape, 0)
    k = 1
    while k < L:
        x = x + jnp.where(row >= k, pltpu.roll(x, k, 0), 0.0)
        k *= 2
    return x


def _rcumsum_rows(x):
    L = x.shape[0]
    row = lax.broadcasted_iota(jnp.int32, x.shape, 0)
    k = 1
    while k < L:
        x = x + jnp.where(row < L - k, pltpu.roll(x, L - k, 0), 0.0)
        k *= 2
    return x


def _ssd_common(dt_ref, par_ref):
    par = par_ref[...]
    raw = dt_ref[...] + par[0:1, :]
    dt = _softplus(raw)
    a = -jnp.exp(par[1:2, :])
    cs = _cumsum_rows(dt * a)
    L = cs.shape[0]
    cs_last = cs[L - 1:L, :]
    return raw, dt, a, par[2:3, :], cs, cs.T, jnp.exp(cs), jnp.exp(cs_last - cs), jnp.exp(cs_last)


def _ssd_specs(nc, rev):
    L = SSM_CHUNK

    def ci(c):
        return nc - 1 - c if rev else c

    return [pl.BlockSpec((L, 512), lambda g, c: (ci(c), g)),
            pl.BlockSpec((L, 128), lambda g, c: (ci(c), 16 + g)),
            pl.BlockSpec((L, 128), lambda g, c: (ci(c), 20 + g)),
            pl.BlockSpec((None, L, 128), lambda g, c: (g, ci(c), 0)),
            pl.BlockSpec((None, 8, 128), lambda g, c: (g, 0, 0)),
            pl.BlockSpec((L, 512), lambda g, c: (ci(c), g)),
            pl.BlockSpec((1, 512), lambda g, c: (0, g))], ci


def _ssd_fwd(xbc_c, zx, dtg, par, gnw, *, name):
    T = xbc_c.shape[0]
    L = SSM_CHUNK
    nc = T // L
    in_specs, ci = _ssd_specs(nc, False)

    def body(xs_ref, b_ref, c_ref, dt_ref, par_ref, z_ref, gnw_ref, y_ref, yn_ref, st_ref, h_ref):
        @pl.when(pl.program_id(1) == 0)
        def _():
            h_ref[...] = jnp.zeros_like(h_ref)

        _, dt, _, dsk, cs, csT, ecs, eend, dec = _ssd_common(dt_ref, par_ref)
        Bb = b_ref[...].astype(BF16)
        Cb = c_ref[...].astype(BF16)
        G = lax.dot_general(Cb, Bb, _NT, preferred_element_type=F32)
        row = lax.broadcasted_iota(jnp.int32, (L, L), 0)
        col = lax.broadcasted_iota(jnp.int32, (L, L), 1)
        tril = col <= row
        lo = lax.broadcasted_iota(jnp.int32, (L, 128), 1) < 64
        lo1 = lax.broadcasted_iota(jnp.int32, (1, 128), 1) < 64
        for pp in range(4):
            hA, hB = 2 * pp, 2 * pp + 1

            def sel(m):
                return jnp.where(lo, m[:, hA:hA + 1], m[:, hB:hB + 1])

            def sel1(m):
                return jnp.where(lo1, m[:, hA:hA + 1], m[:, hB:hB + 1])

            X = xs_ref[:, pp * 128:(pp + 1) * 128]
            xd = X * sel(dt)
            xdb = xd.astype(BF16)
            ys = []
            for h in (hA, hB):
                Lm = jnp.where(tril, jnp.exp(jnp.minimum(cs[:, h:h + 1] - csT[h:h + 1, :], 0.0)), 0.0)
                ys.append(jnp.dot((G * Lm).astype(BF16), xdb, preferred_element_type=F32))
            Hp = h_ref[pp]
            st_ref[pp] = Hp
            yoff = jnp.dot(Cb, Hp.astype(BF16), preferred_element_type=F32) * sel(ecs)
            y_ref[:, pp * 128:(pp + 1) * 128] = jnp.where(lo, ys[0], ys[1]) + yoff + sel1(dsk) * X
            S = lax.dot_general(Bb, (xd * sel(eend)).astype(BF16), _TN, preferred_element_type=F32)
            h_ref[pp] = Hp * sel1(dec) + S
        zv = z_ref[...]
        yg = y_ref[...] * (zv * _sigmoid(zv))
        yn_ref[...] = _rms_fwd(yg, gnw_ref[...]).astype(BF16)

    return pl.pallas_call(
        body, name=name, grid=(SSM_GROUPS, nc), in_specs=in_specs,
        out_specs=[pl.BlockSpec((L, 512), lambda g, c: (c, g)), pl.BlockSpec((L, 512), lambda g, c: (c, g)),
                   pl.BlockSpec((None, None, 4, 128, 128), lambda g, c: (g, c, 0, 0, 0))],
        out_shape=[jax.ShapeDtypeStruct((T, D_INNER), F32), jax.ShapeDtypeStruct((T, D_INNER), BF16),
                   jax.ShapeDtypeStruct((SSM_GROUPS, nc, 4, 128, 128), F32)],
        scratch_shapes=[pltpu.VMEM((4, 128, 128), F32)],
        compiler_params=_cparams(("parallel", "arbitrary")))(xbc_c, xbc_c, xbc_c, dtg, par, zx, gnw)


def _ssd_bwd(xbc_c, zx, dtg, par, gnw, y, st, dyn, *, name):
    T = xbc_c.shape[0]
    L = SSM_CHUNK
    nc = T // L
    in_specs, ci = _ssd_specs(nc, True)
    in_specs += [pl.BlockSpec((L, 512), lambda g, c: (ci(c), g)),
                 pl.BlockSpec((None, None, 4, 128, 128), lambda g, c: (g, ci(c), 0, 0, 0)),
                 pl.BlockSpec((L, 512), lambda g, c: (ci(c), g))]

    def body(xs_ref, b_ref, c_ref, dt_ref, par_ref, z_ref, gnw_ref, y_ref, st_ref, dyn_ref,
             dxs_ref, db_ref, dc_ref, dz_ref, ddt_ref, dgnw_ref, dpar_ref, dh_ref):
        @pl.when(pl.program_id(1) == 0)
        def _():
            dh_ref[...] = jnp.zeros_like(dh_ref)
            dgnw_ref[...] = jnp.zeros_like(dgnw_ref)
            dpar_ref[...] = jnp.zeros_like(dpar_ref)

        yv = y_ref[...]
        zv = z_ref[...]
        sg = _sigmoid(zv)
        sz = zv * sg
        yg = yv * sz
        r = lax.rsqrt(jnp.mean(yg * yg, axis=-1, keepdims=True) + EPS)
        yh = yg * r
        dyn = dyn_ref[...].astype(F32)
        dgnw_ref[...] += jnp.sum(dyn * yh, axis=0, keepdims=True)
        dyh = dyn * gnw_ref[...]
        dyg = r * (dyh - yh * jnp.mean(dyh * yh, axis=-1, keepdims=True))
        dY_all = dyg * sz
        dz_ref[...] = (dyg * yv * (sg * (1.0 + zv * (1.0 - sg)))).astype(dz_ref.dtype)

        raw, dt, a, dsk, cs, csT, ecs, eend, dec = _ssd_common(dt_ref, par_ref)
        Bb = b_ref[...].astype(BF16)
        Cb = c_ref[...].astype(BF16)
        G = lax.dot_general(Cb, Bb, _NT, preferred_element_type=F32)
        row = lax.broadcasted_iota(jnp.int32, (L, L), 0)
        col = lax.broadcasted_iota(jnp.int32, (L, L), 1)
        tril = col <= row
        lane = lax.broadcasted_iota(jnp.int32, (L, 128), 1)
        lo = lane < 64
        lane1 = lax.broadcasted_iota(jnp.int32, (1, 128), 1)
        lo1 = lane1 < 64
        rowc = lax.broadcasted_iota(jnp.int32, (L, 1), 0)
        dG = jnp.zeros((L, L), F32)
        dB = jnp.zeros((L, SSM_STATE), F32)
        dC = jnp.zeros((L, SSM_STATE), F32)
        dcs_mat = jnp.zeros((L, 128), F32)
        dcs_t = jnp.zeros((L, L), F32)
        ddt_mat = jnp.zeros((L, 128), F32)
        dD_row = jnp.zeros((1, 128), F32)

        def tot(m):
            return jnp.sum(jnp.sum(m, axis=1, keepdims=True), axis=0, keepdims=True)

        for pp in range(4):
            hA, hB = 2 * pp, 2 * pp + 1

            def sel(m):
                return jnp.where(lo, m[:, hA:hA + 1], m[:, hB:hB + 1])

            def sel1(m):
                return jnp.where(lo1, m[:, hA:hA + 1], m[:, hB:hB + 1])

            X = xs_ref[:, pp * 128:(pp + 1) * 128]
            dY = dY_all[:, pp * 128:(pp + 1) * 128]
            dtsel = sel(dt)
            xd = X * dtsel
            xdb = xd.astype(BF16)
            dYb = dY.astype(BF16)
            Hp = st_ref[pp]
            Hb = Hp.astype(BF16)
            dHn = dh_ref[pp]
            dHb = dHn.astype(BF16)
            ecs_sel = sel(ecs)
            eend_sel = sel(eend)
            dxd_state = jnp.dot(Bb, dHb, preferred_element_type=F32) * eend_sel
            yoff = jnp.dot(Cb, Hb, preferred_element_type=F32) * ecs_sel
            dYe = (dY * ecs_sel).astype(BF16)
            dC = dC + lax.dot_general(dYe, Hb, _NT, preferred_element_type=F32)
            dB = dB + lax.dot_general((xd * eend_sel).astype(BF16), dHb, _NT, preferred_element_type=F32)
            dh_ref[pp] = dHn * sel1(dec) + lax.dot_general(Cb, dYe, _TN, preferred_element_type=F32)
            q = xd * dxd_state
            dyoff = dY * yoff
            hh = dHn * Hp
            dxd_diag = []
            for h, msk, msk1 in ((hA, lo, lo1), (hB, jnp.logical_not(lo), jnp.logical_not(lo1))):
                Lm = jnp.where(tril, jnp.exp(jnp.minimum(cs[:, h:h + 1] - csT[h:h + 1, :], 0.0)), 0.0)
                M = G * Lm
                dxd_diag.append(lax.dot_general(M.astype(BF16), dYb, _TN, preferred_element_type=F32))
                dM = lax.dot_general(jnp.where(msk, dY, 0.0).astype(BF16), xdb, _NT, preferred_element_type=F32)
                dG = dG + dM * Lm
                W = dM * M
                dcs_h = jnp.sum(W, axis=1, keepdims=True)
                dcs_t = dcs_t + jnp.where(row == h, jnp.sum(W, axis=0, keepdims=True), 0.0)
                dcs_h = dcs_h + jnp.sum(jnp.where(msk, dyoff - q, 0.0), axis=1, keepdims=True)
                tail = tot(jnp.where(msk, q, 0.0)) + dec[:, h:h + 1] * tot(jnp.where(msk, hh, 0.0))
                dcs_h = dcs_h + jnp.where(rowc == L - 1, tail, 0.0)
                dcs_mat = dcs_mat + jnp.where(lane == h, dcs_h, 0.0)
            dxd = jnp.where(lo, dxd_diag[0], dxd_diag[1]) + dxd_state
            prod = dxd * X
            dA_ = jnp.sum(jnp.where(lo, prod, 0.0), axis=1, keepdims=True)
            dB_ = jnp.sum(prod, axis=1, keepdims=True) - dA_
            ddt_mat = ddt_mat + jnp.where(lane == hA, dA_, 0.0) + jnp.where(lane == hB, dB_, 0.0)
            dxs_ref[:, pp * 128:(pp + 1) * 128] = dxd * dtsel + sel1(dsk) * dY
            dyx = jnp.sum(dY * X, axis=0, keepdims=True)
            sA = jnp.sum(jnp.where(lo1, dyx, 0.0), axis=1, keepdims=True)
            sB = jnp.sum(dyx, axis=1, keepdims=True) - sA
            dD_row = dD_row + jnp.where(lane1 == hA, sA, 0.0) + jnp.where(lane1 == hB, sB, 0.0)
        dGb = dG.astype(BF16)
        db_ref[...] = dB + lax.dot_general(dGb, Cb, _TN, preferred_element_type=F32)
        dc_ref[...] = dC + jnp.dot(dGb, Bb, preferred_element_type=F32)
        dad = _rcumsum_rows(dcs_mat - dcs_t.T)
        draw = (a * dad + ddt_mat) * _sigmoid(raw)
        ddt_ref[...] = draw
        dpar_ref[0:1, :] += jnp.sum(draw, axis=0, keepdims=True)
        dpar_ref[1:2, :] += jnp.sum(dt * dad, axis=0, keepdims=True) * a
        dpar_ref[2:3, :] += dD_row

    return pl.pallas_call(
        body, name=name, grid=(SSM_GROUPS, nc), in_specs=in_specs,
        out_specs=[pl.BlockSpec((L, 512), lambda g, c: (ci(c), g)),
                   pl.BlockSpec((L, 128), lambda g, c: (ci(c), g)),
                   pl.BlockSpec((L, 128), lambda g, c: (ci(c), g)),
                   pl.BlockSpec((L, 512), lambda g, c: (ci(c), g)),
                   pl.BlockSpec((None, L, 128), lambda g, c: (g, ci(c), 0)),
                   pl.BlockSpec((1, 512), lambda g, c: (0, g)),
                   pl.BlockSpec((None, 8, 128), lambda g, c: (g, 0, 0))],
        out_shape=[jax.ShapeDtypeStruct((T, D_INNER), F32), jax.ShapeDtypeStruct((T, GN), F32),
                   jax.ShapeDtypeStruct((T, GN), F32), jax.ShapeDtypeStruct((T, D_INNER), BF16),
                   jax.ShapeDtypeStruct((SSM_GROUPS, T, 128), F32), jax.ShapeDtypeStruct((1, D_INNER), F32),
                   jax.ShapeDtypeStruct((SSM_GROUPS, 8, 128), F32)],
        scratch_shapes=[pltpu.VMEM((4, 128, 128), F32)],
        compiler_params=_cparams(("parallel", "arbitrary")))(xbc_c, xbc_c, xbc_c, dtg, par, zx, gnw, y, st, dyn)


SB_KEYS = 512
SB_SCAN = 256
SB_STRIP = 256


def _tri(width, cond):
    kk = lax.broadcasted_iota(jnp.int32, (width, width), 0)
    jj = lax.broadcasted_iota(jnp.int32, (width, width), 1)
    return cond(kk, jj).astype(BF16)


def _sba_diag_mask():
    Bq = SB_BLOCK
    rowi = lax.broadcasted_iota(jnp.int32, (2 * Bq, Bq), 0)
    return lax.broadcasted_iota(jnp.int32, (2 * Bq, Bq), 1) < jnp.where(rowi >= Bq, rowi - Bq, rowi)


_LOG2E = 1.4426950408889634


def _softplus2(z2):
    return jnp.maximum(z2, 0.0) + jnp.log2(1.0 + jnp.exp2(-jnp.abs(z2)))


def _sba_sub_fwd(zb, c, U, mask):
    z2 = zb * _LOG2E
    s = _softplus2(z2)
    if mask is not None:
        s = jnp.where(mask, s, 0.0)
    R = c + jnp.dot(s.astype(BF16), U, preferred_element_type=F32)
    A = jnp.exp2(z2 - s - R)
    if mask is not None:
        A = jnp.where(mask, A, 0.0)
    return A.astype(BF16), R[:, 0:1] + s[:, 0:1]


def _sba_sub_bwd(zb, dAb, Lt, pc, pe, Uincl, Uexcl, mask):
    last = zb.shape[1] - 1
    z2 = zb * _LOG2E
    s = _softplus2(z2)
    g = z2 - s
    if mask is not None:
        s = jnp.where(mask, s, 0.0)
    P = pc + jnp.dot(s.astype(BF16), Uincl, preferred_element_type=F32)
    A = jnp.exp2(g - (Lt - P))
    if mask is not None:
        A = jnp.where(mask, A, 0.0)
    E = dAb * A
    PE = pe + jnp.dot(E.astype(BF16), Uexcl, preferred_element_type=F32)
    dz = E - jnp.exp2(g) * (E + PE)
    if mask is not None:
        dz = jnp.where(mask, dz, 0.0)
    return (A.astype(BF16), dz.astype(BF16), P[:, last:last + 1], PE[:, last:last + 1] + E[:, last:last + 1])


def _stack_heads(v):
    lo = lax.broadcasted_iota(jnp.int32, v.shape, 1) < 64
    zero = jnp.zeros_like(v)
    return jnp.concatenate([jnp.where(lo, v, zero), jnp.where(lo, zero, v)], axis=0)


def _unstack_heads(v):
    lo = lax.broadcasted_iota(jnp.int32, (SB_BLOCK, 128), 1) < 64
    return jnp.where(lo, v[:SB_BLOCK], v[SB_BLOCK:])


def _sba_rows(a):
    return slice(2 * a * SB_BLOCK, 2 * (a + 1) * SB_BLOCK)


def _sba_diag_case(a, b):
    Bq = SB_BLOCK
    if b * SB_SCAN >= (a + 1) * Bq:
        return "skip"
    if (b + 1) * SB_SCAN <= a * Bq:
        return "full"
    rowi = lax.broadcasted_iota(jnp.int32, (2 * Bq, SB_SCAN), 0)
    qpos = a * Bq + jnp.where(rowi >= Bq, rowi - Bq, rowi)
    return b * SB_SCAN + lax.broadcasted_iota(jnp.int32, (2 * Bq, SB_SCAN), 1) < qpos


def _sba_fwd(q, kv, *, name):
    T = q.shape[0]
    Bq = SB_BLOCK
    nsub = SB_KEYS // Bq
    nscan = SB_KEYS // SB_SCAN
    R = 2 * SB_KEYS
    assert T % SB_KEYS == 0 and SB_STRIP == 2 * Bq
    scale = 1.0 / math.sqrt(SB_HEAD_DIM)

    def body(q_ref, k_ref, v_ref, o_ref, lt_ref, z_s, a_s, c_s, acc_s):
        i = pl.program_id(1)
        U2 = _tri(SB_SCAN, lambda k, j: k > j)
        qs_all = jnp.concatenate([_stack_heads(q_ref[a * Bq:(a + 1) * Bq, :] * scale) for a in range(nsub)], axis=0)
        c_s[...] = jnp.zeros_like(c_s)
        acc_s[...] = jnp.zeros_like(acc_s)

        def scores(J, slot):
            off = pl.multiple_of(J * SB_KEYS, SB_KEYS)
            z_s[slot] = lax.dot_general(qs_all, k_ref[pl.ds(off, SB_KEYS), :], _NT, preferred_element_type=F32)

        def weights(slot, diag):
            for a in range(nsub):
                rows = _sba_rows(a)
                c = c_s[rows, :]
                for b in reversed(range(nscan)):
                    cols = slice(b * SB_SCAN, (b + 1) * SB_SCAN)
                    case = _sba_diag_case(a, b) if diag else "full"
                    if isinstance(case, str) and case == "skip":
                        a_s[slot, rows, cols] = jnp.zeros((2 * Bq, SB_SCAN), BF16)
                        continue
                    A, c = _sba_sub_fwd(z_s[slot, rows, cols], c, U2, None if isinstance(case, str) else case)
                    a_s[slot, rows, cols] = A
                c_s[rows, :] = c

        def values(J, slot):
            off = pl.multiple_of(J * SB_KEYS, SB_KEYS)
            acc_s[...] += jnp.dot(a_s[slot], v_ref[pl.ds(off, SB_KEYS), :], preferred_element_type=F32)

        scores(i, 0)
        weights(0, True)
        scores(jnp.maximum(i - 1, 0), 1)

        def step(t, _):
            slot = lax.rem(t, 2)
            weights(slot, False)
            scores(jnp.maximum(i - t - 1, 0), 1 - slot)
            values(i - t + 1, 1 - slot)
            return 0

        lax.fori_loop(1, i + 1, step, 0)
        values(0, lax.rem(i, 2))
        for a in range(nsub):
            o_ref[a * Bq:(a + 1) * Bq, :] = _unstack_heads(acc_s[_sba_rows(a), :]).astype(BF16)
            lt_ref[a * Bq:(a + 1) * Bq, :] = _unstack_heads(jnp.broadcast_to(c_s[_sba_rows(a), :], (2 * Bq, 128)))

    return pl.pallas_call(
        body, name=name, grid=(SB_HEADS // 2, T // SB_KEYS),
        in_specs=[pl.BlockSpec((SB_KEYS, 128), lambda p, i: (i, p)), pl.BlockSpec((T, 128), lambda p, i: (0, p)),
                  pl.BlockSpec((T, 128), lambda p, i: (0, p + SB_HEADS // 2))],
        out_specs=[pl.BlockSpec((SB_KEYS, 128), lambda p, i: (i, p)),
                   pl.BlockSpec((None, SB_KEYS, 128), lambda p, i: (p, i, 0))],
        out_shape=[jax.ShapeDtypeStruct((T, D_MODEL), BF16), jax.ShapeDtypeStruct((SB_HEADS // 2, T, 128), F32)],
        scratch_shapes=[pltpu.VMEM((2, R, SB_KEYS), F32), pltpu.VMEM((2, R, SB_KEYS), BF16),
                        pltpu.VMEM((R, 1), F32), pltpu.VMEM((R, 128), F32)],
        compiler_params=_cparams(("parallel", "parallel")))(q, kv, kv)


def _sba_bwd(q, kv, lt, do, *, name):
    T = q.shape[0]
    Bq = SB_BLOCK
    nq = T // SB_KEYS
    nsub = SB_KEYS // Bq
    nscan = SB_KEYS // SB_SCAN
    R = 2 * SB_KEYS
    assert T % SB_KEYS == 0 and SB_STRIP == 2 * Bq
    scale = 1.0 / math.sqrt(SB_HEAD_DIM)

    def body(q_ref, k_ref, v_ref, lt_ref, do_ref, dq_ref, dk_ref, dv_ref, dk_acc, dv_acc,
             z_s, da_s, a_s, dz_s, pc_s, pe_s, lt_s):
        i = pl.program_id(1)

        @pl.when(i == 0)
        def _():
            dk_acc[...] = jnp.zeros_like(dk_acc)
            dv_acc[...] = jnp.zeros_like(dv_acc)

        Uincl = _tri(SB_SCAN, lambda k, j: k <= j)
        Uexcl = _tri(SB_SCAN, lambda k, j: k < j)
        qs, dos = [], []
        for a in range(nsub):
            rows = slice(a * Bq, (a + 1) * Bq)
            qs.append(_stack_heads(q_ref[rows, :] * scale))
            dos.append(_stack_heads(do_ref[rows, :]))
            lt_s[_sba_rows(a), :] = jnp.concatenate([lt_ref[rows, 0:1], lt_ref[rows, 64:65]], axis=0)
        qs_all = jnp.concatenate(qs, axis=0)
        dos_all = jnp.concatenate(dos, axis=0)
        pc_s[...] = jnp.zeros_like(pc_s)
        pe_s[...] = jnp.zeros_like(pe_s)
        a_s[1] = jnp.zeros((R, SB_KEYS), BF16)
        dz_s[1] = jnp.zeros((R, SB_KEYS), BF16)

        def scores(J, slot):
            off = pl.multiple_of(J * SB_KEYS, SB_KEYS)
            z_s[slot] = lax.dot_general(qs_all, k_ref[pl.ds(off, SB_KEYS), :], _NT, preferred_element_type=F32)
            da_s[slot] = lax.dot_general(dos_all, v_ref[pl.ds(off, SB_KEYS), :], _NT, preferred_element_type=F32)

        def gradients(slot, diag):
            for a in range(nsub):
                rows = _sba_rows(a)
                pc, pe, Lt = pc_s[rows, :], pe_s[rows, :], lt_s[rows, :]
                for b in range(nscan):
                    cols = slice(b * SB_SCAN, (b + 1) * SB_SCAN)
                    case = _sba_diag_case(a, b) if diag else "full"
                    if isinstance(case, str) and case == "skip":
                        a_s[slot, rows, cols] = jnp.zeros((2 * Bq, SB_SCAN), BF16)
                        dz_s[slot, rows, cols] = jnp.zeros((2 * Bq, SB_SCAN), BF16)
                        continue
                    A, dz, pc, pe = _sba_sub_bwd(z_s[slot, rows, cols], da_s[slot, rows, cols], Lt, pc, pe, Uincl, Uexcl,
                                                 None if isinstance(case, str) else case)
                    a_s[slot, rows, cols] = A
                    dz_s[slot, rows, cols] = dz
                pc_s[rows, :] = pc
                pe_s[rows, :] = pe

        def products(J, slot, dq_acc):
            off = pl.multiple_of(J * SB_KEYS, SB_KEYS)
            dzt = dz_s[slot]
            dk_acc[pl.ds(off, SB_KEYS), :] += lax.dot_general(dzt, qs_all, _TN, preferred_element_type=F32)
            dv_acc[pl.ds(off, SB_KEYS), :] += lax.dot_general(a_s[slot], dos_all, _TN, preferred_element_type=F32)
            return dq_acc + jnp.dot(dzt, k_ref[pl.ds(off, SB_KEYS), :], preferred_element_type=F32)

        scores(0, 0)

        def step(t, dq_acc):
            slot = lax.rem(t, 2)
            gradients(slot, False)
            scores(t + 1, 1 - slot)
            return products(jnp.maximum(t - 1, 0), 1 - slot, dq_acc)

        dq_acc = lax.fori_loop(0, i, step, jnp.zeros((R, 128), F32))
        own = lax.rem(i, 2)
        gradients(own, True)
        dq_acc = products(jnp.maximum(i - 1, 0), 1 - own, dq_acc)
        dq_acc = products(i, own, dq_acc)
        for a in range(nsub):
            dq_ref[a * Bq:(a + 1) * Bq, :] = (_unstack_heads(dq_acc[_sba_rows(a)]) * scale).astype(BF16)

        @pl.when(i == nq - 1)
        def _():
            dk_ref[...] = dk_acc[...].astype(BF16)
            dv_ref[...] = dv_acc[...].astype(BF16)

    return pl.pallas_call(
        body, name=name, grid=(SB_HEADS // 2, nq),
        in_specs=[pl.BlockSpec((SB_KEYS, 128), lambda p, i: (i, p)), pl.BlockSpec((T, 128), lambda p, i: (0, p)),
                  pl.BlockSpec((T, 128), lambda p, i: (0, p + SB_HEADS // 2)),
                  pl.BlockSpec((None, SB_KEYS, 128), lambda p, i: (p, i, 0)),
                  pl.BlockSpec((SB_KEYS, 128), lambda p, i: (i, p))],
        out_specs=[pl.BlockSpec((SB_KEYS, 128), lambda p, i: (i, p)), pl.BlockSpec((T, 128), lambda p, i: (0, p)),
                   pl.BlockSpec((T, 128), lambda p, i: (0, p))],
        out_shape=[jax.ShapeDtypeStruct((T, D_MODEL), BF16), jax.ShapeDtypeStruct((T, D_MODEL), BF16),
                   jax.ShapeDtypeStruct((T, D_MODEL), BF16)],
        scratch_shapes=[pltpu.VMEM((T, 128), F32), pltpu.VMEM((T, 128), F32),
                        pltpu.VMEM((2, R, SB_KEYS), F32), pltpu.VMEM((2, R, SB_KEYS), F32),
                        pltpu.VMEM((2, R, SB_KEYS), BF16), pltpu.VMEM((2, R, SB_KEYS), BF16),
                        pltpu.VMEM((R, 1), F32), pltpu.VMEM((R, 1), F32), pltpu.VMEM((R, 1), F32)],
        compiler_params=_cparams(("parallel", "arbitrary")))(q, kv, kv, lt, do)


def _sba_fwd_old(q, kv, *, name):
    T = q.shape[0]
    Bq = SB_BLOCK
    nsub = SB_KEYS // Bq
    assert T % SB_KEYS == 0
    scale = 1.0 / math.sqrt(SB_HEAD_DIM)

    def body(q_ref, k_ref, v_ref, o_ref, lt_ref):
        I = pl.program_id(1)
        U1 = _tri(Bq, lambda k, j: k > j)
        U2 = _tri(SB_SCAN, lambda k, j: k > j)
        dmask = _sba_diag_mask()
        qs = [_stack_heads(q_ref[a * Bq:(a + 1) * Bq, :] * scale) for a in range(nsub)]
        cs, accs = [], []
        for a in range(nsub):
            c = jnp.zeros((2 * Bq, 1), F32)
            acc = jnp.zeros((2 * Bq, 128), F32)
            for b in range(a, -1, -1):
                off = pl.multiple_of(I * SB_KEYS + b * Bq, Bq)
                zb = lax.dot_general(qs[a], k_ref[pl.ds(off, Bq), :], _NT, preferred_element_type=F32)
                A, c = _sba_sub_fwd(zb, c, U1, dmask if b == a else None)
                acc = acc + jnp.dot(A, v_ref[pl.ds(off, Bq), :], preferred_element_type=F32)
            cs.append(c)
            accs.append(acc)
        qs_all = jnp.concatenate(qs, axis=0)

        def step(n, carry):
            c, acc = carry
            off = pl.multiple_of((I - 1 - n) * SB_KEYS, SB_KEYS)
            z = lax.dot_general(qs_all, k_ref[pl.ds(off, SB_KEYS), :], _NT, preferred_element_type=F32)
            parts = [None] * (SB_KEYS // SB_SCAN)
            for b in reversed(range(SB_KEYS // SB_SCAN)):
                parts[b], c = _sba_sub_fwd(z[:, b * SB_SCAN:(b + 1) * SB_SCAN], c, U2, None)
            return c, acc + jnp.dot(jnp.concatenate(parts, axis=1), v_ref[pl.ds(off, SB_KEYS), :],
                                    preferred_element_type=F32)

        c, acc = lax.fori_loop(0, I, step, (jnp.concatenate(cs, axis=0), jnp.concatenate(accs, axis=0)))
        for a in range(nsub):
            rows = slice(2 * a * Bq, 2 * (a + 1) * Bq)
            o_ref[a * Bq:(a + 1) * Bq, :] = _unstack_heads(acc[rows]).astype(BF16)
            lt_ref[a * Bq:(a + 1) * Bq, :] = _unstack_heads(jnp.broadcast_to(c[rows], (2 * Bq, 128)))

    return pl.pallas_call(
        body, name=name, grid=(SB_HEADS // 2, T // SB_KEYS),
        in_specs=[pl.BlockSpec((SB_KEYS, 128), lambda p, i: (i, p)), pl.BlockSpec((T, 128), lambda p, i: (0, p)),
                  pl.BlockSpec((T, 128), lambda p, i: (0, p + SB_HEADS // 2))],
        out_specs=[pl.BlockSpec((SB_KEYS, 128), lambda p, i: (i, p)),
                   pl.BlockSpec((None, SB_KEYS, 128), lambda p, i: (p, i, 0))],
        out_shape=[jax.ShapeDtypeStruct((T, D_MODEL), BF16), jax.ShapeDtypeStruct((SB_HEADS // 2, T, 128), F32)],
        compiler_params=_cparams(("parallel", "parallel")))(q, kv, kv)


def _sba_bwd_old(q, kv, lt, do, *, name):
    T = q.shape[0]
    Bq = SB_BLOCK
    nq = T // SB_KEYS
    nsub = SB_KEYS // Bq
    assert T % SB_KEYS == 0
    scale = 1.0 / math.sqrt(SB_HEAD_DIM)

    def body(q_ref, k_ref, v_ref, lt_ref, do_ref, dq_ref, dk_ref, dv_ref, dk_acc, dv_acc,
             z_s, da_s, a_s, dz_s, pc_s, pe_s, lt_s):
        i = pl.program_id(1)

        @pl.when(i == 0)
        def _():
            dk_acc[...] = jnp.zeros_like(dk_acc)
            dv_acc[...] = jnp.zeros_like(dv_acc)

        Uincl1 = _tri(Bq, lambda k, j: k <= j)
        Uexcl1 = _tri(Bq, lambda k, j: k < j)
        Uincl2 = _tri(SB_SCAN, lambda k, j: k <= j)
        Uexcl2 = _tri(SB_SCAN, lambda k, j: k < j)
        dmask = _sba_diag_mask()
        qs, dos, lts = [], [], []
        for a in range(nsub):
            rows = slice(a * Bq, (a + 1) * Bq)
            qs.append(_stack_heads(q_ref[rows, :] * scale))
            dos.append(_stack_heads(do_ref[rows, :]))
            lts.append(jnp.concatenate([lt_ref[rows, 0:1], lt_ref[rows, 64:65]], axis=0))
        qs_all = jnp.concatenate(qs, axis=0)
        dos_all = jnp.concatenate(dos, axis=0)
        lt_all = jnp.concatenate(lts, axis=0)

        R = 2 * nsub * Bq
        pc_s[...] = jnp.zeros_like(pc_s)
        pe_s[...] = jnp.zeros_like(pe_s)
        lt_s[...] = lt_all

        def scores(J, slot):
            off = pl.multiple_of(J * SB_KEYS, SB_KEYS)
            z_s[slot] = lax.dot_general(qs_all, k_ref[pl.ds(off, SB_KEYS), :], _NT, preferred_element_type=F32)
            da_s[slot] = lax.dot_general(dos_all, v_ref[pl.ds(off, SB_KEYS), :], _NT, preferred_element_type=F32)

        def elementwise(slot):
            for r in range(R // SB_STRIP):
                rows = slice(r * SB_STRIP, (r + 1) * SB_STRIP)
                pc, pe, Lt = pc_s[rows, :], pe_s[rows, :], lt_s[rows, :]
                for b in range(SB_KEYS // SB_SCAN):
                    cols = slice(b * SB_SCAN, (b + 1) * SB_SCAN)
                    A, dz, pc, pe = _sba_sub_bwd(z_s[slot, rows, cols], da_s[slot, rows, cols], Lt, pc, pe,
                                                 Uincl2, Uexcl2, None)
                    a_s[slot, rows, cols] = A
                    dz_s[slot, rows, cols] = dz
                pc_s[rows, :] = pc
                pe_s[rows, :] = pe

        def outputs(J, slot, dq_acc):
            off = pl.multiple_of(J * SB_KEYS, SB_KEYS)
            dzt = dz_s[slot]
            dk_acc[pl.ds(off, SB_KEYS), :] += lax.dot_general(dzt, qs_all, _TN, preferred_element_type=F32)
            dv_acc[pl.ds(off, SB_KEYS), :] += lax.dot_general(a_s[slot], dos_all, _TN, preferred_element_type=F32)
            return dq_acc + jnp.dot(dzt, k_ref[pl.ds(off, SB_KEYS), :], preferred_element_type=F32)

        a_s[1] = jnp.zeros((R, SB_KEYS), BF16)
        dz_s[1] = jnp.zeros((R, SB_KEYS), BF16)
        last = jnp.maximum(i - 1, 0)
        scores(0, 0)

        def step(J, dq_acc):
            slot = lax.rem(J, 2)
            elementwise(slot)
            scores(jnp.minimum(J + 1, last), 1 - slot)
            return outputs(jnp.maximum(J - 1, 0), 1 - slot, dq_acc)

        dq_acc = lax.fori_loop(0, i, step, jnp.zeros((R, 128), F32))
        dq_acc = outputs(last, lax.rem(i + 1, 2), dq_acc)
        pc, pe = pc_s[...], pe_s[...]
        for a in range(nsub):
            rows = slice(2 * a * Bq, 2 * (a + 1) * Bq)
            pca, pea, dqa = pc[rows], pe[rows], dq_acc[rows]
            for b in range(a + 1):
                off = pl.multiple_of(i * SB_KEYS + b * Bq, Bq)
                kb = k_ref[pl.ds(off, Bq), :]
                zb = lax.dot_general(qs[a], kb, _NT, preferred_element_type=F32)
                dAb = lax.dot_general(dos[a], v_ref[pl.ds(off, Bq), :], _NT, preferred_element_type=F32)
                A, dz, pca, pea = _sba_sub_bwd(zb, dAb, lts[a], pca, pea, Uincl1, Uexcl1, dmask if b == a else None)
                dqa = dqa + jnp.dot(dz, kb, preferred_element_type=F32)
                dk_acc[pl.ds(off, Bq), :] += lax.dot_general(dz, qs[a], _TN, preferred_element_type=F32)
                dv_acc[pl.ds(off, Bq), :] += lax.dot_general(A, dos[a], _TN, preferred_element_type=F32)
            dq_ref[a * Bq:(a + 1) * Bq, :] = (_unstack_heads(dqa) * scale).astype(BF16)

        @pl.when(i == nq - 1)
        def _():
            dk_ref[...] = dk_acc[...].astype(BF16)
            dv_ref[...] = dv_acc[...].astype(BF16)

    return pl.pallas_call(
        body, name=name, grid=(SB_HEADS // 2, nq),
        in_specs=[pl.BlockSpec((SB_KEYS, 128), lambda p, i: (i, p)), pl.BlockSpec((T, 128), lambda p, i: (0, p)),
                  pl.BlockSpec((T, 128), lambda p, i: (0, p + SB_HEADS // 2)),
                  pl.BlockSpec((None, SB_KEYS, 128), lambda p, i: (p, i, 0)),
                  pl.BlockSpec((SB_KEYS, 128), lambda p, i: (i, p))],
        out_specs=[pl.BlockSpec((SB_KEYS, 128), lambda p, i: (i, p)), pl.BlockSpec((T, 128), lambda p, i: (0, p)),
                   pl.BlockSpec((T, 128), lambda p, i: (0, p))],
        out_shape=[jax.ShapeDtypeStruct((T, D_MODEL), BF16), jax.ShapeDtypeStruct((T, D_MODEL), BF16),
                   jax.ShapeDtypeStruct((T, D_MODEL), BF16)],
        scratch_shapes=[pltpu.VMEM((T, 128), F32), pltpu.VMEM((T, 128), F32),
                        pltpu.VMEM((2, 2 * SB_KEYS, SB_KEYS), F32), pltpu.VMEM((2, 2 * SB_KEYS, SB_KEYS), F32),
                        pltpu.VMEM((2, 2 * SB_KEYS, SB_KEYS), BF16), pltpu.VMEM((2, 2 * SB_KEYS, SB_KEYS), BF16),
                        pltpu.VMEM((2 * SB_KEYS, 1), F32), pltpu.VMEM((2 * SB_KEYS, 1), F32),
                        pltpu.VMEM((2 * SB_KEYS, 1), F32)],
        compiler_params=_cparams(("parallel", "arbitrary")))(q, kv, kv, lt, do)


def _loss_head(h, tgt, w, *, name, tt=512):
    T, D = h.shape
    tt = min(tt, T)

    def body(h_ref, t_ref, w_ref, loss_ref, dh_ref, dw_ref):
        i = pl.program_id(0)
        hv = h_ref[...]
        wv = w_ref[...]
        r = lax.rsqrt(jnp.mean(hv * hv, axis=-1, keepdims=True) + EPS)
        xhat = hv * r
        err = xhat * wv - t_ref[...]
        part = 0.5 * jnp.sum(jnp.mean(err * err, axis=-1, keepdims=True), axis=0, keepdims=True)
        dy = err * (1.0 / D)
        dxh = dy * wv
        dh_ref[...] = r * (dxh - xhat * jnp.mean(dxh * xhat, axis=-1, keepdims=True))
        dwc = jnp.sum(dy * xhat, axis=0, keepdims=True)

        @pl.when(i == 0)
        def _():
            loss_ref[...] = jnp.broadcast_to(part, loss_ref.shape)
            dw_ref[...] = dwc

        @pl.when(i > 0)
        def _():
            loss_ref[...] += jnp.broadcast_to(part, loss_ref.shape)
            dw_ref[...] += dwc

    return pl.pallas_call(
        body, name=name, grid=(T // tt,),
        in_specs=[pl.BlockSpec((tt, D), lambda i: (i, 0)), pl.BlockSpec((tt, D), lambda i: (i, 0)),
                  pl.BlockSpec((1, D), lambda i: (0, 0))],
        out_specs=[pl.BlockSpec((1, 128), lambda i: (0, 0)), pl.BlockSpec((tt, D), lambda i: (i, 0)),
                   pl.BlockSpec((1, D), lambda i: (0, 0))],
        out_shape=[jax.ShapeDtypeStruct((1, 128), F32), jax.ShapeDtypeStruct((T, D), F32),
                   jax.ShapeDtypeStruct((1, D), F32)],
        compiler_params=_cparams(("arbitrary",)))(h, tgt, w.reshape(1, D))


def _adamw(parts, w, m, v, *, name, tr=256):
    P, R, C = parts.shape
    tr = min(tr, R)
    assert R % tr == 0, (name, R, tr)
    c1 = 1.0 - ADAM_B1 ** ADAM_STEP
    c2 = 1.0 - ADAM_B2 ** ADAM_STEP

    def body(p_ref, w_ref, m_ref, v_ref, g_ref, d_ref, nm_ref, nv_ref):
        g = p_ref[0].astype(F32)
        for k in range(1, P):
            g = g + p_ref[k].astype(F32)
        mn = ADAM_B1 * m_ref[...] + (1.0 - ADAM_B1) * g
        vn = ADAM_B2 * v_ref[...] + (1.0 - ADAM_B2) * (g * g)
        g_ref[...] = g
        nm_ref[...] = mn
        nv_ref[...] = vn
        d_ref[...] = -ADAM_LR * ((mn / c1) / (jnp.sqrt(vn / c2) + ADAM_EPS) + ADAM_WD * w_ref[...])

    spec = pl.BlockSpec((tr, C), lambda i: (i, 0))
    sds = jax.ShapeDtypeStruct((R, C), F32)
    return pl.pallas_call(
        body, name=name, grid=(R // tr,),
        in_specs=[pl.BlockSpec((P, tr, C), lambda i: (0, i, 0)), spec, spec, spec],
        out_specs=[spec, spec, spec, spec], out_shape=[sds, sds, sds, sds],
        compiler_params=_cparams(("parallel",)))(parts, w, m, v)


def _all_gather(shards, *, name):
    n = len(shards)

    def body(*refs):
        ins, outs = refs[:n], refs[n:2 * n]
        send_sems, recv_sems, local_sems = refs[2 * n:]
        x, y, c = lax.axis_index("x"), lax.axis_index("y"), lax.axis_index("c")
        me, sib = (x, y, c), (x, y, 1 - c)
        chips = [(1 - x, y), (x, 1 - y), (1 - x, 1 - y)]

        def slot(p):
            return 4 * p[0] + 2 * p[1] + p[2]

        def cp(a, k, block, to, src=None):
            dst = outs[a].at[slot(block)]
            return pltpu.make_async_remote_copy(src_ref=dst if src is None else src, dst_ref=dst,
                                                send_sem=send_sems.at[a, k], recv_sem=recv_sems.at[a, k],
                                                device_id=to, device_id_type=_MESH)

        mine = [pltpu.make_async_copy(ins[a], outs[a].at[slot(me)], local_sems.at[a]) for a in range(n)]
        for m in mine:
            m.start()
        first = []
        for a in range(n):
            first.append(cp(a, 0, me, sib, src=ins[a]))
            for j, chip in enumerate(chips):
                first.append(cp(a, 1 + j, me, (*chip, c), src=ins[a]))
        for f in first:
            f.start()
        passed = []
        for j, chip in enumerate(chips):
            for a in range(n):
                cp(a, 1 + j, (*chip, c), me).wait_recv()
                f = cp(a, 4 + j, (*chip, c), sib)
                f.start()
                passed.append(f)
        for a in range(n):
            cp(a, 0, sib, me).wait_recv()
            for j, chip in enumerate(chips):
                cp(a, 4 + j, (*chip, 1 - c), me).wait_recv()
        for f in first + passed:
            f.wait_send()
        for m in mine:
            m.wait()

    return pl.pallas_call(
        body, name=name, in_specs=[_ANY] * n, out_specs=[_ANY] * n,
        out_shape=[jax.ShapeDtypeStruct((N_DEV,) + s.shape, s.dtype) for s in shards],
        scratch_shapes=[pltpu.SemaphoreType.DMA((n, 7)), pltpu.SemaphoreType.DMA((n, 7)),
                        pltpu.SemaphoreType.DMA((n,))])(*shards)


def _exchange(blocks, *, name):
    n = len(blocks)

    def body(*refs):
        ins, outs = refs[:n], refs[n:2 * n]
        send_sems, recv_sems, local_sems = refs[2 * n:]
        x, y, c = lax.axis_index("x"), lax.axis_index("y"), lax.axis_index("c")
        me = 4 * x + 2 * y + c
        mine = [pltpu.make_async_copy(ins[a].at[me], outs[a].at[me], local_sems.at[a]) for a in range(n)]
        for m in mine:
            m.start()
        copies = []
        for r in range(1, N_DEV):
            rx, ry, rc = (r >> 2) & 1, (r >> 1) & 1, r & 1
            px, py, pc = (1 - x if rx else x), (1 - y if ry else y), (1 - c if rc else c)
            peer = 4 * px + 2 * py + pc
            for a in range(n):
                copies.append((pltpu.make_async_remote_copy(
                    src_ref=ins[a].at[peer], dst_ref=outs[a].at[me], send_sem=send_sems.at[a, r - 1],
                    recv_sem=recv_sems.at[a, r - 1], device_id=(px, py, pc), device_id_type=_MESH),
                    pltpu.make_async_remote_copy(
                    src_ref=ins[a].at[peer], dst_ref=outs[a].at[peer], send_sem=send_sems.at[a, r - 1],
                    recv_sem=recv_sems.at[a, r - 1], device_id=(px, py, pc), device_id_type=_MESH)))
        for snd, _ in copies:
            snd.start()
        for _, rcv in copies:
            rcv.wait_recv()
        for snd, _ in copies:
            snd.wait_send()
        for m in mine:
            m.wait()

    return pl.pallas_call(
        body, name=name, in_specs=[_ANY] * n, out_specs=[_ANY] * n,
        out_shape=[jax.ShapeDtypeStruct(b.shape, b.dtype) for b in blocks],
        scratch_shapes=[pltpu.SemaphoreType.DMA((n, 7)), pltpu.SemaphoreType.DMA((n, 7)),
                        pltpu.SemaphoreType.DMA((n,))])(*blocks)


_HBM = pl.BlockSpec(memory_space=pltpu.HBM)
_SEM = pl.BlockSpec(memory_space=pltpu.SEMAPHORE)
_EFFECT = pltpu.SideEffectType.DATAFLOW_SIDE_EFFECTING


def _peers():
    x, y, c = lax.axis_index("x"), lax.axis_index("y"), lax.axis_index("c")
    out = []
    for r in range(1, N_DEV):
        px = 1 - x if (r >> 2) & 1 else x
        py = 1 - y if (r >> 1) & 1 else y
        pc = 1 - c if r & 1 else c
        out.append(((px, py, pc), 4 * px + 2 * py + pc))
    return 4 * x + 2 * y + c, out


def _push_copy(src_ref, land_ref, send_sems, recv_sems, a, k, me, peer, peer_slot, scatter, arriving):
    src = src_ref.at[peer_slot] if scatter else src_ref
    return pltpu.make_async_remote_copy(
        src_ref=src, dst_ref=land_ref.at[peer_slot if arriving else me], send_sem=send_sems.at[a * (N_DEV - 1) + k],
        recv_sem=recv_sems.at[a * (N_DEV - 1) + k], device_id=peer, device_id_type=_MESH)


def _push_start(srcs, *, scatter, name):
    n = len(srcs)
    lands = [lax.empty(s.shape if scatter else (N_DEV,) + s.shape, s.dtype) for s in srcs]

    def body(*refs):
        src_refs, land_refs = refs[:n], refs[n:2 * n]
        send_sems, recv_sems = refs[2 * n], refs[2 * n + 1]
        token = refs[-1]
        me, peers = _peers()
        for k, (peer, slot) in enumerate(peers):
            for a in range(n):
                _push_copy(src_refs[a], land_refs[a], send_sems, recv_sems, a, k, me, peer, slot, scatter, False).start()
        token[...] = jnp.zeros_like(token)

    hbm = lambda a: pltpu.HBM(a.shape, a.dtype)
    outs = pl.pallas_call(
        body, name=name,
        out_shape=(pltpu.SemaphoreType.DMA((n * (N_DEV - 1),)), pltpu.SemaphoreType.DMA((n * (N_DEV - 1),)),
                   *[hbm(s) for s in srcs], *[hbm(l) for l in lands], jax.ShapeDtypeStruct((8, 128), F32)),
        in_specs=[_HBM] * (2 * n),
        out_specs=(_SEM, _SEM, *([_HBM] * (2 * n)), pl.BlockSpec(memory_space=pltpu.VMEM)),
        input_output_aliases={i: 2 + i for i in range(2 * n)},
        compiler_params=pltpu.CompilerParams(has_side_effects=_EFFECT),
    )(*[pltpu.with_memory_space_constraint(s, pltpu.HBM) for s in srcs],
      *[pltpu.with_memory_space_constraint(l, pltpu.HBM) for l in lands])
    return dict(send=outs[0], recv=outs[1], srcs=list(outs[2:2 + n]), lands=list(outs[2 + n:2 + 2 * n]),
                token=outs[-1], scatter=scatter, n=n)


def _push_wait(h, after, *, name):
    n, scatter = h["n"], h["scatter"]

    def body(*refs):
        src_refs, land_refs = refs[:n], refs[n:2 * n]
        send_sems, recv_sems = refs[2 * n], refs[2 * n + 1]
        me, peers = _peers()
        for k, (peer, slot) in enumerate(peers):
            for a in range(n):
                cp = _push_copy(src_refs[a], land_refs[a], send_sems, recv_sems, a, k, me, peer, slot, scatter, True)
                cp.wait_send()
                cp.wait_recv()

    hbm = lambda a: pltpu.HBM(a.shape, a.dtype)
    outs = pl.pallas_call(
        body, name=name,
        out_shape=(*[hbm(s) for s in h["srcs"]], *[hbm(l) for l in h["lands"]]),
        in_specs=[_HBM] * (2 * n) + [_SEM, _SEM, _ANY], out_specs=tuple([_HBM] * (2 * n)),
        input_output_aliases={i: i for i in range(2 * n)},
        compiler_params=pltpu.CompilerParams(has_side_effects=_EFFECT),
    )(*h["srcs"], *h["lands"], h["send"], h["recv"], after)
    return list(outs[:n]), list(outs[n:])


def _ffn_fwd(h, nw, w_up, conv_w, conv_b, w_down, tag):
    a = _mm_fwd(h, w_up, norm_w=nw, name=f"ffn{tag}_up", tm=1024, tn=1408)
    p = _ffn_conv_fwd(a, conv_w, conv_b.reshape(1, -1), name=f"ffn{tag}_conv")
    h_out = _mm_fwd(p, w_down, residual=h, name=f"ffn{tag}_down", tm=1024, tn=512)
    return h_out, (a, p)


def _ffn_bwd(dh, h, saved, nw, w_up, conv_w, conv_b, w_down, tag):
    a, p = saved
    g_down = _mm_tn(p, dh, name=f"ffn{tag}_down_wg", tk1=1408, tn=1024)
    dp = _mm_nt(dh, w_down, name=f"ffn{tag}_down_dg", out_dtype=BF16, tm=1024, tn=1408, tk=1024)
    dhid, g_cw, g_cb = _ffn_conv_bwd_pre(a, conv_w, conv_b.reshape(1, -1), dp, name=f"ffn{tag}_conv_bwd")
    da = _conv_bwd_in(dhid, conv_w, K=FFN_CONV, name=f"ffn{tag}_conv_bwd_in", tt=256, tc=1408)
    g_up = _mm_tn(h, da, norm_w=nw, name=f"ffn{tag}_up_wg", tn=1408)
    dh_out, g_nw = _mm_nt(da, w_up, epi=(h, nw, dh), name=f"ffn{tag}_up_dg", tk=1408)
    return dh_out, dict(norm=g_nw.reshape(-1), up=g_up, conv_w=g_cw, conv_b=g_cb.reshape(-1), down=g_down)


def _local_step(x, tgt, W):
    T = x.shape[0]
    f = {}
    zx = _mm_fwd(x, W["in_w"], norm_w=W["ssm_norm_w"], name="ssm_in", tm=1024, tn=896)
    xbc_c = _ssm_conv_fwd(zx, W["ssm_conv_w"], W["ssm_conv_b"].reshape(1, -1), name="ssm_conv")
    dt_raw = zx[:, D_INNER + CONV_DIM:IN_PROJ_DIM]
    dtg = jnp.pad(dt_raw.reshape(T, SSM_GROUPS, 8).transpose(1, 0, 2), ((0, 0), (0, 0), (0, 120)))
    par = jnp.stack([W["ssm_dt_bias"].reshape(SSM_GROUPS, 8), W["ssm_a_log"].reshape(SSM_GROUPS, 8),
                     W["ssm_d"].reshape(SSM_GROUPS, 8)], axis=1)
    par = jnp.pad(par, ((0, 0), (0, 5), (0, 120)))
    gnw = W["ssm_gate_norm_w"].reshape(1, D_INNER)
    y, yn, st = _ssd_fwd(xbc_c, zx, dtg, par, gnw, name="ssd_fwd")
    h1 = _mm_fwd(yn, W["ssm_out_w"], residual=x, name="ssm_out", tm=1024, tn=512)
    h2, ffn0 = _ffn_fwd(h1, W["ffn_norm_w"][0], W["ffn_up_w"][0], W["ffn_conv_w"][0], W["ffn_conv_b"][0],
                        W["ffn_down_w"][0], "0")
    q = _mm_fwd(h2, W["w_q"], norm_w=W["attn_norm_w"], out_dtype=BF16, name="attn_q", tm=1024, tn=1024)
    kv = _mm_fwd(h2, W["w_kv"], norm_w=W["kv_norm_w"], out_dtype=BF16, name="attn_kv", tm=1024, tn=1024)
    o, lt = _sba_fwd(q, kv, name="sba_fwd")
    h3 = _mm_fwd(o, W["w_o"], residual=h2, name="attn_o", tm=1024, tn=512)
    h4, ffn1 = _ffn_fwd(h3, W["ffn_norm_w"][1], W["ffn_up_w"][1], W["ffn_conv_w"][1], W["ffn_conv_b"][1],
                        W["ffn_down_w"][1], "1")
    loss, dh4, g_final = _loss_head(h4, tgt, W["final_norm_w"], name="loss_head")
    dh3, gf1 = _ffn_bwd(dh4, h3, ffn1, W["ffn_norm_w"][1], W["ffn_up_w"][1], W["ffn_conv_w"][1], W["ffn_conv_b"][1],
                        W["ffn_down_w"][1], "1")
    g_wo = _mm_tn(o, dh3, name="attn_o_wg", tn=1024)
    do = _mm_nt(dh3, W["w_o"], name="attn_o_dg", out_dtype=BF16, tn=1024, tk=1024)
    dq, dk, dv = _sba_bwd(q, kv, lt, do, name="sba_bwd")
    g_wq = _mm_tn(h2, dq, norm_w=W["attn_norm_w"], name="attn_q_wg", tn=1024)
    dh2a, g_attn_nw = _mm_nt(dq, W["w_q"], epi=(h2, W["attn_norm_w"], dh3), name="attn_q_dg", tk=1024)
    dkv = jnp.concatenate([dk, dv], axis=1)
    g_wkv = _mm_tn(h2, dkv, norm_w=W["kv_norm_w"], name="attn_kv_wg", tn=1024)
    dh2, g_kv_nw = _mm_nt(dkv, W["w_kv"], epi=(h2, W["kv_norm_w"], dh2a), name="attn_kv_dg", tk=1024)
    dh1, gf0 = _ffn_bwd(dh2, h1, ffn0, W["ffn_norm_w"][0], W["ffn_up_w"][0], W["ffn_conv_w"][0], W["ffn_conv_b"][0],
                        W["ffn_down_w"][0], "0")
    g_out = _mm_tn(yn, dh1, name="ssm_out_wg", tn=1024)
    dyn = _mm_nt(dh1, W["ssm_out_w"], name="ssm_out_dg", out_dtype=BF16, tn=1024, tk=1024)
    dxs, dB, dC, dz, ddt, g_gnw, dpar = _ssd_bwd(xbc_c, zx, dtg, par, gnw, y, st, dyn, name="ssd_bwd")
    dxbc_c = jnp.concatenate([dxs, dB, dC], axis=1)
    dhid, g_scw, g_scb = _ssm_conv_bwd_pre(zx, W["ssm_conv_w"], W["ssm_conv_b"].reshape(1, -1), dxbc_c,
                                           name="ssm_conv_bwd")
    dxbc = _conv_bwd_in(dhid, W["ssm_conv_w"], K=SSM_CONV, name="ssm_conv_bwd_in")
    ddt_t = ddt[:, :, :8].transpose(1, 0, 2).reshape(T, SSM_HEADS).astype(BF16)
    dzx = jnp.concatenate([dz, dxbc, jnp.pad(ddt_t, ((0, 0), (0, IN_PROJ_PAD - IN_PROJ_DIM)))], axis=1)
    g_in = _mm_tn(x, dzx, norm_w=W["ssm_norm_w"], name="ssm_in_wg", tn=896)
    dx, g_ssm_nw = _mm_nt(dzx, W["in_w"], epi=(x, W["ssm_norm_w"], dh1), name="ssm_in_dg", tk=1792)
    f["ssm_norm_w"] = g_ssm_nw.reshape(-1)
    f["ssm_in_w"] = g_in[:, :IN_PROJ_DIM]
    f["ssm_conv_w"] = g_scw
    f["ssm_conv_b"] = g_scb.reshape(-1)
    f["ssm_dt_bias"] = dpar[:, 0, :8].reshape(-1)
    f["ssm_a_log"] = dpar[:, 1, :8].reshape(-1)
    f["ssm_d"] = dpar[:, 2, :8].reshape(-1)
    f["ssm_gate_norm_w"] = g_gnw.reshape(-1)
    f["ssm_out_w"] = g_out
    f["kv_norm_w"] = g_kv_nw.reshape(-1)
    f["w_k"] = g_wkv[:, :D_MODEL]
    f["w_v"] = g_wkv[:, D_MODEL:]
    f["attn_norm_w"] = g_attn_nw.reshape(-1)
    f["w_q"] = g_wq
    f["w_o"] = g_wo
    f["ffn_norm_w"] = jnp.stack([gf0["norm"], gf1["norm"]])
    f["ffn_up_w"] = [gf0["up"], gf1["up"]]
    f["ffn_conv_w"] = jnp.stack([gf0["conv_w"], gf1["conv_w"]])
    f["ffn_conv_b"] = jnp.stack([gf0["conv_b"], gf1["conv_b"]])
    f["ffn_down_w"] = [gf0["down"], gf1["down"]]
    f["final_norm_w"] = g_final.reshape(-1)
    return loss, dx, f


_BIG = ["ssm_in_w", "ssm_out_w", "w_k", "w_v", "w_q", "w_o", "ffn_up_w", "ffn_down_w"]
_SMALL_SHARDED = ["ssm_norm_w", "ssm_conv_w", "ssm_conv_b", "ssm_gate_norm_w", "ffn_conv_w"]
_SMALL_REPL = ["ssm_dt_bias", "ssm_a_log", "ssm_d", "kv_norm_w", "attn_norm_w", "ffn_norm_w", "ffn_conv_b",
               "final_norm_w"]
_WEIGHTS = ["ssm_norm_w", "ssm_in_w", "ssm_conv_w", "ssm_conv_b", "ssm_dt_bias", "ssm_a_log", "ssm_d",
            "ssm_gate_norm_w", "ssm_out_w", "kv_norm_w", "w_k", "w_v", "attn_norm_w", "w_q", "w_o", "ffn_norm_w",
            "ffn_up_w", "ffn_conv_w", "ffn_conv_b", "ffn_down_w", "final_norm_w"]


def _as2d(a):
    return a.reshape(-1, a.shape[-1])


def _cols_to_full(g):
    return g.transpose(1, 0, 2).reshape(g.shape[1], N_DEV * g.shape[2])


def _full_to_cols(a):
    R = a.shape[0]
    return a.reshape(R, N_DEV, -1).transpose(1, 0, 2)


def _gather_weights(p):
    names = _BIG + _SMALL_SHARDED
    shards = [_as2d(p[n]).astype(BF16) for n in _BIG] + [_as2d(p[n]) for n in _SMALL_SHARDED]
    got = dict(zip(names, _all_gather(shards, name="gather_weights")))
    W = {n: p[n] for n in _SMALL_REPL}
    in_w = _cols_to_full(got["ssm_in_w"])
    W["in_w"] = jnp.pad(in_w, ((0, 0), (0, IN_PROJ_PAD - IN_PROJ_DIM)))
    W["ssm_out_w"] = got["ssm_out_w"].reshape(D_INNER, D_MODEL)
    W["w_kv"] = jnp.concatenate([got["w_k"].reshape(D_MODEL, D_MODEL), got["w_v"].reshape(D_MODEL, D_MODEL)], axis=1)
    W["w_q"] = got["w_q"].reshape(D_MODEL, D_MODEL)
    W["w_o"] = got["w_o"].reshape(D_MODEL, D_MODEL)
    up = got["ffn_up_w"]
    W["ffn_up_w"] = [_cols_to_full(up[:, l * D_MODEL:(l + 1) * D_MODEL]) for l in range(2)]
    dn = got["ffn_down_w"]
    rs = D_FF // N_DEV
    W["ffn_down_w"] = [dn[:, l * rs:(l + 1) * rs].reshape(D_FF, D_MODEL) for l in range(2)]
    W["ssm_norm_w"] = got["ssm_norm_w"].reshape(D_MODEL)
    W["ssm_conv_w"] = _cols_to_full(got["ssm_conv_w"])
    W["ssm_conv_b"] = got["ssm_conv_b"].reshape(CONV_DIM)
    W["ssm_gate_norm_w"] = got["ssm_gate_norm_w"].reshape(D_INNER)
    fcw = _cols_to_full(got["ffn_conv_w"])
    W["ffn_conv_w"] = fcw.reshape(2, FFN_CONV, 2 * D_FF)
    for n in ("ssm_dt_bias", "ssm_a_log", "ssm_d", "attn_norm_w"):
        W[n] = W[n].reshape(-1)
    return W


def _big_grad_blocks(f):
    rs = D_FF // N_DEV
    return {
        "ssm_in_w": _full_to_cols(f["ssm_in_w"]),
        "ssm_out_w": f["ssm_out_w"].reshape(N_DEV, D_INNER // N_DEV, D_MODEL),
        "w_k": f["w_k"].reshape(N_DEV, D_MODEL // N_DEV, D_MODEL),
        "w_v": f["w_v"].reshape(N_DEV, D_MODEL // N_DEV, D_MODEL),
        "w_q": f["w_q"].reshape(N_DEV, D_MODEL // N_DEV, D_MODEL),
        "w_o": f["w_o"].reshape(N_DEV, D_MODEL // N_DEV, D_MODEL),
        "ffn_up_w": jnp.concatenate([_full_to_cols(g) for g in f["ffn_up_w"]], axis=1),
        "ffn_down_w": jnp.concatenate([g.reshape(N_DEV, rs, D_MODEL) for g in f["ffn_down_w"]], axis=1),
    }


def _pack_small(vals):
    flat = jnp.concatenate([v.reshape(-1).astype(F32) for v in vals])
    n = flat.shape[0]
    rows = -(-n // 1024) * 8
    return jnp.pad(flat, (0, rows * 128 - n)).reshape(rows, 128)


def _unpack_small(packed, shapes):
    flat = packed.reshape(-1)
    out, off = [], 0
    for s in shapes:
        n = math.prod(s)
        out.append(flat[off:off + n].reshape(s))
        off += n
    return out


def _kernel_v1(x, ssm_norm_w, ssm_in_w, ssm_conv_w, ssm_conv_b, ssm_dt_bias, ssm_a_log, ssm_d, ssm_gate_norm_w, ssm_out_w, kv_norm_w, w_k, w_v, attn_norm_w, w_q, w_o, ffn_norm_w, ffn_up_w, ffn_conv_w, ffn_conv_b, ffn_down_w, final_norm_w, loss_target, m_ssm_norm_w, m_ssm_in_w, m_ssm_conv_w, m_ssm_conv_b, m_ssm_dt_bias, m_ssm_a_log, m_ssm_d, m_ssm_gate_norm_w, m_ssm_out_w, m_kv_norm_w, m_w_k, m_w_v, m_attn_norm_w, m_w_q, m_w_o, m_ffn_norm_w, m_ffn_up_w, m_ffn_conv_w, m_ffn_conv_b, m_ffn_down_w, m_final_norm_w, v_ssm_norm_w, v_ssm_in_w, v_ssm_conv_w, v_ssm_conv_b, v_ssm_dt_bias, v_ssm_a_log, v_ssm_d, v_ssm_gate_norm_w, v_ssm_out_w, v_kv_norm_w, v_w_k, v_w_v, v_attn_norm_w, v_w_q, v_w_o, v_ffn_norm_w, v_ffn_up_w, v_ffn_conv_w, v_ffn_conv_b, v_ffn_down_w, v_final_norm_w):
    env = dict(locals())
    p = {n: env[n] for n in _WEIGHTS}
    mom = {n: env["m_" + n] for n in _WEIGHTS}
    var = {n: env["v_" + n] for n in _WEIGHTS}
    T = x.shape[1]
    me = 4 * lax.axis_index("x") + 2 * lax.axis_index("y") + lax.axis_index("c")

    W = _gather_weights(p)
    loss_row, dx, f = _local_step(x.reshape(T, D_MODEL), loss_target.reshape(T, D_MODEL), W)
    loss = lax.psum(loss_row[0, 0], ("x", "y", "c"))

    big = _big_grad_blocks(f)
    small_names = _SMALL_REPL + _SMALL_SHARDED
    small_full = _pack_small([f[n] for n in small_names])
    small_bcast = jnp.broadcast_to(small_full[None], (N_DEV,) + small_full.shape)
    got = _exchange([big[n] for n in _BIG] + [small_bcast], name="exchange_grads")
    big_parts = dict(zip(_BIG, got[:-1]))

    zero = jnp.zeros_like(small_full)
    g_small_sum = _adamw(got[-1], zero, zero, zero, name="sum_small_grads", tr=small_full.shape[0])[0]
    full_shapes = [f[n].shape for n in small_names]
    g_small = dict(zip(small_names, _unpack_small(g_small_sum, full_shapes)))
    for n in _SMALL_SHARDED:
        width = p[n].shape[-1]
        g_small[n] = lax.dynamic_slice_in_dim(g_small[n], me * width, width, axis=g_small[n].ndim - 1)

    out_g, out_d, out_m, out_v = {}, {}, {}, {}
    for n in _BIG:
        w2, m2, v2 = _as2d(p[n]), _as2d(mom[n]), _as2d(var[n])
        tr = 352 if n == "ffn_down_w" else 256
        g, d, nm, nv = _adamw(big_parts[n], w2, m2, v2, name="adamw_" + n, tr=tr)
        out_g[n], out_d[n], out_m[n], out_v[n] = (t.reshape(p[n].shape) for t in (g, d, nm, nv))
    sw = _pack_small([p[n] for n in small_names])
    sm = _pack_small([mom[n] for n in small_names])
    sv = _pack_small([var[n] for n in small_names])
    sg = _pack_small([g_small[n] for n in small_names])
    _, d, nm, nv = _adamw(sg[None], sw, sm, sv, name="adamw_small", tr=sw.shape[0])
    shard_shapes = [p[n].shape for n in small_names]
    for n, dd, mm, vv in zip(small_names, _unpack_small(d, shard_shapes), _unpack_small(nm, shard_shapes),
                             _unpack_small(nv, shard_shapes)):
        out_g[n] = g_small[n].reshape(p[n].shape)
        out_d[n], out_m[n], out_v[n] = dd, mm, vv

    return (loss, dx.reshape(x.shape), *[out_g[n] for n in _WEIGHTS], *[out_d[n] for n in _WEIGHTS],
            *[out_m[n] for n in _WEIGHTS], *[out_v[n] for n in _WEIGHTS])


def _tie(a, token):
    return a + token[0, 0].astype(a.dtype)


def _local_step2(x, tgt, get_w, put_g):
    T = x.shape[0]
    Ws = get_w("ssm", None)
    fnw, fcw, fcb = Ws["ffn_norm_w"], Ws["ffn_conv_w"], Ws["ffn_conv_b"]
    zx = _mm_fwd(x, Ws["in_w"], norm_w=Ws["ssm_norm_w"], name="ssm_in", tm=1024, tn=896)
    xbc_c = _ssm_conv_fwd(zx, Ws["ssm_conv_w"], Ws["ssm_conv_b"].reshape(1, -1), name="ssm_conv")
    dt_raw = zx[:, D_INNER + CONV_DIM:IN_PROJ_DIM]
    dtg = jnp.pad(dt_raw.reshape(T, SSM_GROUPS, 8).transpose(1, 0, 2), ((0, 0), (0, 0), (0, 120)))
    par = jnp.stack([Ws["ssm_dt_bias"].reshape(SSM_GROUPS, 8), Ws["ssm_a_log"].reshape(SSM_GROUPS, 8),
                     Ws["ssm_d"].reshape(SSM_GROUPS, 8)], axis=1)
    par = jnp.pad(par, ((0, 0), (0, 5), (0, 120)))
    gnw = _tie(Ws["ssm_gate_norm_w"].reshape(1, D_INNER), get_w("rest_start", xbc_c))
    y, yn, st = _ssd_fwd(xbc_c, zx, dtg, par, gnw, name="ssd_fwd")
    W0 = get_w("ffn0", y)
    Ws["ssm_out_w"] = W0["ssm_out_w"]
    h1 = _mm_fwd(yn, Ws["ssm_out_w"], residual=x, name="ssm_out", tm=1024, tn=512)
    h2, ffn0 = _ffn_fwd(h1, fnw[0], W0["up"], fcw[0], fcb[0], W0["down"], "0")
    Wr = get_w("rest", h2)
    q = _mm_fwd(h2, Wr["w_q"], norm_w=Ws["attn_norm_w"], out_dtype=BF16, name="attn_q", tm=1024, tn=1024)
    kv = _mm_fwd(h2, Wr["w_kv"], norm_w=Ws["kv_norm_w"], out_dtype=BF16, name="attn_kv", tm=1024, tn=1024)
    o, lt = _sba_fwd(q, kv, name="sba_fwd")
    h3 = _mm_fwd(o, Wr["w_o"], residual=h2, name="attn_o", tm=1024, tn=512)
    h4, ffn1 = _ffn_fwd(h3, fnw[1], Wr["up"], fcw[1], fcb[1], Wr["down"], "1")
    loss, dh4, g_final = _loss_head(h4, tgt, Ws["final_norm_w"], name="loss_head")
    dh3, gf1 = _ffn_bwd(dh4, h3, ffn1, fnw[1], Wr["up"], fcw[1], fcb[1], Wr["down"], "1")
    tok = put_g("ffn1", dict(up=gf1["up"], down=gf1["down"]))
    g_wo = _mm_tn(o, dh3, name="attn_o_wg", tn=1024)
    do = _mm_nt(dh3, _tie(Wr["w_o"], tok), name="attn_o_dg", out_dtype=BF16, tn=1024, tk=1024)
    dq, dk, dv = _sba_bwd(q, kv, lt, do, name="sba_bwd")
    g_wq = _mm_tn(h2, dq, norm_w=Ws["attn_norm_w"], name="attn_q_wg", tn=1024)
    dh2a, g_attn_nw = _mm_nt(dq, Wr["w_q"], epi=(h2, Ws["attn_norm_w"], dh3), name="attn_q_dg", tk=1024)
    dkv = jnp.concatenate([dk, dv], axis=1)
    g_wkv = _mm_tn(h2, dkv, norm_w=Ws["kv_norm_w"], name="attn_kv_wg", tn=1024)
    dh2, g_kv_nw = _mm_nt(dkv, Wr["w_kv"], epi=(h2, Ws["kv_norm_w"], dh2a), name="attn_kv_dg", tk=1024)
    tok = put_g("attn", dict(w_o=g_wo, w_q=g_wq, w_k=g_wkv[:, :D_MODEL], w_v=g_wkv[:, D_MODEL:]))
    dh1, gf0 = _ffn_bwd(dh2, h1, ffn0, fnw[0], W0["up"], fcw[0], _tie(fcb[0], tok), W0["down"], "0")
    tok = put_g("ffn0", dict(up=gf0["up"], down=gf0["down"]))
    g_out = _mm_tn(yn, dh1, name="ssm_out_wg", tn=1024)
    dyn = _mm_nt(dh1, _tie(Ws["ssm_out_w"], tok), name="ssm_out_dg", out_dtype=BF16, tn=1024, tk=1024)
    tok = put_g("ssm_out", dict(ssm_out_w=g_out))
    dxs, dB, dC, dz, ddt, g_gnw, dpar = _ssd_bwd(xbc_c, zx, dtg, par, _tie(gnw, tok), y, st, dyn, name="ssd_bwd")
    dxbc_c = jnp.concatenate([dxs, dB, dC], axis=1)
    dhid, g_scw, g_scb = _ssm_conv_bwd_pre(zx, Ws["ssm_conv_w"], Ws["ssm_conv_b"].reshape(1, -1), dxbc_c,
                                           name="ssm_conv_bwd")
    dxbc = _conv_bwd_in(dhid, Ws["ssm_conv_w"], K=SSM_CONV, name="ssm_conv_bwd_in")
    ddt_t = ddt[:, :, :8].transpose(1, 0, 2).reshape(T, SSM_HEADS).astype(BF16)
    dzx = jnp.concatenate([dz, dxbc, jnp.pad(ddt_t, ((0, 0), (0, IN_PROJ_PAD - IN_PROJ_DIM)))], axis=1)
    g_in = _mm_tn(x, dzx, norm_w=Ws["ssm_norm_w"], name="ssm_in_wg", tn=896)
    tok = put_g("ssm_in", dict(ssm_in_w=g_in[:, :IN_PROJ_DIM]))
    dx, g_ssm_nw = _mm_nt(dzx, Ws["in_w"], epi=(x, _tie(Ws["ssm_norm_w"], tok), dh1), name="ssm_in_dg", tk=1792)
    f = {
        "ssm_norm_w": g_ssm_nw.reshape(-1), "ssm_conv_w": g_scw,
        "ssm_conv_b": g_scb.reshape(-1), "ssm_dt_bias": dpar[:, 0, :8].reshape(-1),
        "ssm_a_log": dpar[:, 1, :8].reshape(-1), "ssm_d": dpar[:, 2, :8].reshape(-1),
        "ssm_gate_norm_w": g_gnw.reshape(-1), "kv_norm_w": g_kv_nw.reshape(-1), "attn_norm_w": g_attn_nw.reshape(-1),
        "ffn_norm_w": jnp.stack([gf0["norm"], gf1["norm"]]), "ffn_conv_w": jnp.stack([gf0["conv_w"], gf1["conv_w"]]),
        "ffn_conv_b": jnp.stack([gf0["conv_b"], gf1["conv_b"]]), "final_norm_w": g_final.reshape(-1),
    }
    return loss, dx, f


def kernel(x, ssm_norm_w, ssm_in_w, ssm_conv_w, ssm_conv_b, ssm_dt_bias, ssm_a_log, ssm_d, ssm_gate_norm_w, ssm_out_w, kv_norm_w, w_k, w_v, attn_norm_w, w_q, w_o, ffn_norm_w, ffn_up_w, ffn_conv_w, ffn_conv_b, ffn_down_w, final_norm_w, loss_target, m_ssm_norm_w, m_ssm_in_w, m_ssm_conv_w, m_ssm_conv_b, m_ssm_dt_bias, m_ssm_a_log, m_ssm_d, m_ssm_gate_norm_w, m_ssm_out_w, m_kv_norm_w, m_w_k, m_w_v, m_attn_norm_w, m_w_q, m_w_o, m_ffn_norm_w, m_ffn_up_w, m_ffn_conv_w, m_ffn_conv_b, m_ffn_down_w, m_final_norm_w, v_ssm_norm_w, v_ssm_in_w, v_ssm_conv_w, v_ssm_conv_b, v_ssm_dt_bias, v_ssm_a_log, v_ssm_d, v_ssm_gate_norm_w, v_ssm_out_w, v_kv_norm_w, v_w_k, v_w_v, v_attn_norm_w, v_w_q, v_w_o, v_ffn_norm_w, v_ffn_up_w, v_ffn_conv_w, v_ffn_conv_b, v_ffn_down_w, v_final_norm_w):
    env = dict(locals())
    p = {n: env[n] for n in _WEIGHTS}
    mom = {n: env["m_" + n] for n in _WEIGHTS}
    var = {n: env["v_" + n] for n in _WEIGHTS}
    T = x.shape[1]
    me = 4 * lax.axis_index("x") + 2 * lax.axis_index("y") + lax.axis_index("c")
    rs = D_FF // N_DEV

    def bf2(a):
        return _as2d(a).astype(BF16)

    def with_own(srcs, lands, scatter):
        out = []
        for s, l in zip(srcs, lands):
            own = lax.dynamic_index_in_dim(s, me, 0, keepdims=False) if scatter else s
            out.append(lax.dynamic_update_index_in_dim(l, own, me, 0))
        return out

    a_names = ["ssm_in_w"] + _SMALL_SHARDED
    got_a = dict(zip(a_names, _all_gather([bf2(p["ssm_in_w"])] + [_as2d(p[n]) for n in _SMALL_SHARDED],
                                          name="gather_ssm")))
    ffn0_names = ["ssm_out_w", "up0", "down0"]
    rest_names = ["w_q", "w_k", "w_v", "w_o", "up1", "down1"]
    shard = {"up0": bf2(p["ffn_up_w"][0]), "down0": bf2(p["ffn_down_w"][0]), "up1": bf2(p["ffn_up_w"][1]),
             "down1": bf2(p["ffn_down_w"][1]), "w_q": bf2(p["w_q"]), "w_k": bf2(p["w_k"]), "w_v": bf2(p["w_v"]),
             "w_o": bf2(p["w_o"]), "ssm_out_w": bf2(p["ssm_out_w"])}
    h_ffn0 = _push_start([shard[n] for n in ffn0_names], scatter=False, name="gather_ffn0_start")
    handles = {}

    def get_w(group, after):
        if group == "ssm":
            W = {n: p[n] for n in _SMALL_REPL}
            for n in ("ssm_dt_bias", "ssm_a_log", "ssm_d", "attn_norm_w"):
                W[n] = W[n].reshape(-1)
            W["in_w"] = jnp.pad(_cols_to_full(got_a["ssm_in_w"]), ((0, 0), (0, IN_PROJ_PAD - IN_PROJ_DIM)))
            W["ssm_norm_w"] = _tie(got_a["ssm_norm_w"].reshape(D_MODEL), h_ffn0["token"])
            W["ssm_conv_w"] = _cols_to_full(got_a["ssm_conv_w"])
            W["ssm_conv_b"] = got_a["ssm_conv_b"].reshape(CONV_DIM)
            W["ssm_gate_norm_w"] = got_a["ssm_gate_norm_w"].reshape(D_INNER)
            W["ffn_conv_w"] = _cols_to_full(got_a["ffn_conv_w"]).reshape(2, FFN_CONV, 2 * D_FF)
            return W
        if group == "rest_start":
            anchor = after[0, 0]
            first = shard[rest_names[0]] + (jnp.where(jnp.isfinite(anchor), anchor, 0.0) * 0.0).astype(BF16)
            handles["rest"] = _push_start([first] + [shard[n] for n in rest_names[1:]], scatter=False,
                                          name="gather_rest_start")
            return handles["rest"]["token"]
        if group == "ffn0":
            srcs, lands = _push_wait(h_ffn0, after, name="gather_ffn0_wait")
            out, up, down = with_own(srcs, lands, False)
            return dict(ssm_out_w=out.reshape(D_INNER, D_MODEL), up=_cols_to_full(up), down=down.reshape(D_FF, D_MODEL))
        srcs, lands = _push_wait(handles["rest"], after, name="gather_rest_wait")
        g = dict(zip(rest_names, with_own(srcs, lands, False)))
        sq = lambda a: a.reshape(D_MODEL, D_MODEL)
        return dict(w_q=sq(g["w_q"]), w_kv=jnp.concatenate([sq(g["w_k"]), sq(g["w_v"])], axis=1), w_o=sq(g["w_o"]),
                    up=_cols_to_full(g["up1"]), down=g["down1"].reshape(D_FF, D_MODEL))

    pending = []

    def put_g(group, g):
        if group in ("ffn0", "ffn1"):
            keys = [("ffn_up_w", int(group[-1])), ("ffn_down_w", int(group[-1]))]
            blocks = [_full_to_cols(g["up"]), g["down"].reshape(N_DEV, rs, D_MODEL)]
        elif group == "attn":
            keys = [(n, None) for n in ("w_o", "w_q", "w_k", "w_v")]
            blocks = [g[n].reshape(N_DEV, D_MODEL // N_DEV, D_MODEL) for n, _ in keys]
        elif group == "ssm_out":
            keys = [("ssm_out_w", None)]
            blocks = [g["ssm_out_w"].reshape(N_DEV, D_INNER // N_DEV, D_MODEL)]
        else:
            keys = [("ssm_in_w", None)]
            blocks = [_full_to_cols(g["ssm_in_w"])]
        h = _push_start(blocks, scatter=True, name=f"exchange_{group}_start")
        pending.append((group, keys, h))
        return h["token"]

    loss_row, dx, f = _local_step2(x.reshape(T, D_MODEL), loss_target.reshape(T, D_MODEL), get_w, put_g)
    loss = lax.psum(loss_row[0, 0], ("x", "y", "c"))

    small_names = _SMALL_REPL + _SMALL_SHARDED
    small_full = _pack_small([f[n] for n in small_names])
    small_bcast = jnp.broadcast_to(small_full[None], (N_DEV,) + small_full.shape)
    h_small = _push_start([small_bcast], scatter=True, name="exchange_small_start")
    tok = h_small["token"]

    res = {}
    for group, keys, h in pending:
        srcs, lands = _push_wait(h, dx, name=f"exchange_{group}_wait")
        for (n, layer), parts in zip(keys, with_own(srcs, lands, True)):
            sel = (lambda a: a) if layer is None else (lambda a: a[layer])
            w2, m2, v2 = _as2d(sel(p[n])), _as2d(sel(mom[n])), _as2d(sel(var[n]))
            if not res:
                w2 = _tie(w2, tok)
            tr = rs if n == "ffn_down_w" else 256
            res[(n, layer)] = _adamw(parts, w2, m2, v2, name=f"adamw_{n}" + ("" if layer is None else str(layer)), tr=tr)
    srcs, lands = _push_wait(h_small, res[("ssm_in_w", None)][0], name="exchange_small_wait")
    small_parts = with_own(srcs, lands, True)[0]
    out_g, out_d, out_m, out_v = {}, {}, {}, {}
    for n in _BIG:
        if (n, None) in res:
            quad = res[(n, None)]
        else:
            quad = [jnp.stack([res[(n, 0)][k], res[(n, 1)][k]]) for k in range(4)]
        out_g[n], out_d[n], out_m[n], out_v[n] = (t.reshape(p[n].shape) for t in quad)

    zero = jnp.zeros_like(small_full)
    g_small_sum = _adamw(small_parts, zero, zero, zero, name="sum_small_grads", tr=small_full.shape[0])[0]
    g_small = dict(zip(small_names, _unpack_small(g_small_sum, [f[n].shape for n in small_names])))
    for n in _SMALL_SHARDED:
        width = p[n].shape[-1]
        g_small[n] = lax.dynamic_slice_in_dim(g_small[n], me * width, width, axis=g_small[n].ndim - 1)
    sw = _pack_small([p[n] for n in small_names])
    sm = _pack_small([mom[n] for n in small_names])
    sv = _pack_small([var[n] for n in small_names])
    sg = _pack_small([g_small[n] for n in small_names])
    _, d, nm, nv = _adamw(sg[None], sw, sm, sv, name="adamw_small", tr=sw.shape[0])
    shard_shapes = [p[n].shape for n in small_names]
    for n, dd, mm, vv in zip(small_names, _unpack_small(d, shard_shapes), _unpack_small(nm, shard_shapes),
                             _unpack_small(nv, shard_shapes)):
        out_g[n] = g_small[n].reshape(p[n].shape)
        out_d[n], out_m[n], out_v[n] = dd, mm, vv

    return (loss, dx.reshape(x.shape), *[out_g[n] for n in _WEIGHTS], *[out_d[n] for n in _WEIGHTS],
            *[out_m[n] for n in _WEIGHTS], *[out_v[n] for n in _WEIGHTS])
```

```python
import functools
import math

import jax
import jax.numpy as jnp
from jax import lax
from jax.experimental import pallas as pl
from jax.experimental.pallas import tpu as pltpu

F32 = jnp.float32
BF16 = jnp.bfloat16
EPS = 1e-6

D_MODEL = 1024
D_INNER = 2048
SSM_HEADS = 32
SSM_GROUPS = 4
SSM_STATE = 128
SSM_CONV = 4
SSM_CHUNK = 128
GN = SSM_GROUPS * SSM_STATE
CONV_DIM = D_INNER + 2 * GN
IN_PROJ_DIM = D_INNER + CONV_DIM + SSM_HEADS
IN_PROJ_PAD = 5376
SB_HEADS = 16
SB_HEAD_DIM = 64
SB_BLOCK = 128
D_FF = 2816
FFN_CONV = 3
N_DEV = 8

ADAM_LR = 0.001
ADAM_B1 = 0.9
ADAM_B2 = 0.999
ADAM_EPS = 1e-08
ADAM_WD = 0.01
ADAM_STEP = 10

_MESH = pl.DeviceIdType.MESH
_NT = (((1,), (1,)), ((), ()))
_TN = (((0,), (0,)), ((), ()))
_ANY = pl.BlockSpec(memory_space=pl.ANY)


def _cparams(sem, vmem_mb=48):
    return pltpu.CompilerParams(dimension_semantics=sem, vmem_limit_bytes=vmem_mb * 1024 * 1024)


def _sigmoid(x):
    return 1.0 / (1.0 + jnp.exp(-x))


def _softplus(x):
    return jnp.maximum(x, 0.0) + jnp.log(1.0 + jnp.exp(-jnp.abs(x)))


def _rms_fwd(xv, w):
    r = lax.rsqrt(jnp.mean(xv * xv, axis=-1, keepdims=True) + EPS)
    return xv * r * w


def _mm_fwd(x, w, *, name, norm_w=None, residual=None, out_dtype=F32, tm=512, tn=512):
    M, K = x.shape
    N = w.shape[1]
    tm, tn = min(tm, M), min(tn, N)
    assert M % tm == 0 and N % tn == 0, (name, M, N, tm, tn)
    has_norm, has_res = norm_w is not None, residual is not None

    def body(*refs):
        x_ref, w_ref = refs[0], refs[1]
        p = 2
        nw_ref = r_ref = None
        if has_norm:
            nw_ref = refs[p]
            p += 1
        if has_res:
            r_ref = refs[p]
            p += 1
        o_ref, xn_ref = refs[p], refs[p + 1]

        @pl.when(pl.program_id(1) == 0)
        def _():
            xv = x_ref[...].astype(F32)
            if has_norm:
                xv = _rms_fwd(xv, nw_ref[...])
            xn_ref[...] = xv.astype(BF16)

        acc = jnp.dot(xn_ref[...], w_ref[...], preferred_element_type=F32)
        if has_res:
            acc = acc + r_ref[...]
        o_ref[...] = acc.astype(out_dtype)

    in_specs = [pl.BlockSpec((tm, K), lambda i, j: (i, 0)), pl.BlockSpec((K, tn), lambda i, j: (0, j))]
    args = [x, w]
    if has_norm:
        in_specs.append(pl.BlockSpec((1, K), lambda i, j: (0, 0)))
        args.append(norm_w.reshape(1, K))
    if has_res:
        in_specs.append(pl.BlockSpec((tm, tn), lambda i, j: (i, j)))
        args.append(residual)
    return pl.pallas_call(
        body, name=name, grid=(M // tm, N // tn), in_specs=in_specs,
        out_specs=pl.BlockSpec((tm, tn), lambda i, j: (i, j)),
        out_shape=jax.ShapeDtypeStruct((M, N), out_dtype),
        scratch_shapes=[pltpu.VMEM((tm, K), BF16)],
        compiler_params=_cparams(("parallel", "arbitrary")))(*args)


def _mm_nt(dy, w, *, name, epi=None, out_dtype=F32, tm=512, tn=512, tk=512):
    M, K = dy.shape
    N = w.shape[0]
    tm, tk = min(tm, M), min(tk, K)
    tn = N if epi is not None else min(tn, N)
    assert M % tm == 0 and N % tn == 0 and K % tk == 0, (name, M, N, K, tm, tn, tk)
    nk = K // tk
    has_epi = epi is not None

    def body(*refs):
        if has_epi:
            dy_ref, w_ref, h_ref, nw_ref, r_ref, o_ref, dnw_ref, acc_ref = refs
        else:
            dy_ref, w_ref, o_ref, acc_ref = refs
        i = pl.program_id(0)
        k = pl.program_id(2)

        @pl.when(k == 0)
        def _():
            acc_ref[...] = jnp.zeros_like(acc_ref)

        acc_ref[...] += lax.dot_general(dy_ref[...].astype(BF16), w_ref[...], _NT, preferred_element_type=F32)

        @pl.when(k == nk - 1)
        def _():
            du = acc_ref[...]
            if has_epi:
                hv = h_ref[...]
                r = lax.rsqrt(jnp.mean(hv * hv, axis=-1, keepdims=True) + EPS)
                xhat = hv * r
                dxh = du * nw_ref[...]
                dx = r * (dxh - xhat * jnp.mean(dxh * xhat, axis=-1, keepdims=True))
                o_ref[...] = (r_ref[...] + dx).astype(out_dtype)
                contrib = jnp.sum(du * xhat, axis=0, keepdims=True)

                @pl.when(i == 0)
                def _():
                    dnw_ref[...] = contrib

                @pl.when(i > 0)
                def _():
                    dnw_ref[...] += contrib
            else:
                o_ref[...] = du.astype(out_dtype)

    in_specs = [pl.BlockSpec((tm, tk), lambda i, j, k: (i, k)), pl.BlockSpec((tn, tk), lambda i, j, k: (j, k))]
    args = [dy, w]
    out_specs = [pl.BlockSpec((tm, tn), lambda i, j, k: (i, j))]
    out_shape = [jax.ShapeDtypeStruct((M, N), out_dtype)]
    if has_epi:
        h, nw, res = epi
        in_specs += [pl.BlockSpec((tm, N), lambda i, j, k: (i, 0)), pl.BlockSpec((1, N), lambda i, j, k: (0, 0)),
                     pl.BlockSpec((tm, N), lambda i, j, k: (i, 0))]
        args += [h, nw.reshape(1, N), res]
        out_specs.append(pl.BlockSpec((1, N), lambda i, j, k: (0, 0)))
        out_shape.append(jax.ShapeDtypeStruct((1, N), F32))
    outs = pl.pallas_call(
        body, name=name, grid=(M // tm, N // tn, nk), in_specs=in_specs, out_specs=out_specs, out_shape=out_shape,
        scratch_shapes=[pltpu.VMEM((tm, tn), F32)],
        compiler_params=_cparams(("arbitrary", "arbitrary", "arbitrary")))(*args)
    return (outs[0], outs[1]) if has_epi else outs[0]


def _mm_tn(x, dy, *, name, norm_w=None, out_dtype=BF16, tk1=1024, tn=512, tt=512):
    T, K1 = x.shape
    N = dy.shape[1]
    tk1, tn, tt = min(tk1, K1), min(tn, N), min(tt, T)
    has_norm = norm_w is not None
    assert K1 % tk1 == 0 and N % tn == 0 and T % tt == 0, (name, K1, N, T, tk1, tn, tt)
    assert not has_norm or tk1 == K1
    nt = T // tt

    def body(*refs):
        if has_norm:
            x_ref, dy_ref, nw_ref, o_ref, acc_ref = refs
        else:
            x_ref, dy_ref, o_ref, acc_ref = refs
        t = pl.program_id(2)

        @pl.when(t == 0)
        def _():
            acc_ref[...] = jnp.zeros_like(acc_ref)

        xv = x_ref[...]
        if has_norm:
            xv = _rms_fwd(xv.astype(F32), nw_ref[...])
        acc_ref[...] += lax.dot_general(xv.astype(BF16), dy_ref[...].astype(BF16), _TN, preferred_element_type=F32)

        @pl.when(t == nt - 1)
        def _():
            o_ref[...] = acc_ref[...].astype(out_dtype)

    in_specs = [pl.BlockSpec((tt, tk1), lambda a, b, t: (t, a)), pl.BlockSpec((tt, tn), lambda a, b, t: (t, b))]
    args = [x, dy]
    if has_norm:
        in_specs.append(pl.BlockSpec((1, K1), lambda a, b, t: (0, 0)))
        args.append(norm_w.reshape(1, K1))
    return pl.pallas_call(
        body, name=name, grid=(K1 // tk1, N // tn, nt), in_specs=in_specs,
        out_specs=pl.BlockSpec((tk1, tn), lambda a, b, t: (a, b)),
        out_shape=jax.ShapeDtypeStruct((K1, N), out_dtype),
        scratch_shapes=[pltpu.VMEM((tk1, tn), F32)],
        compiler_params=_cparams(("parallel", "parallel", "arbitrary")))(*args)


def _shift_down(xb, prev8, j):
    main = pltpu.roll(xb, j, 0)
    head = pltpu.roll(xb[0:8], j, 0)
    ph = pltpu.roll(prev8, j, 0)
    row8 = lax.broadcasted_iota(jnp.int32, head.shape, 0)
    head = jnp.where(row8 < j, ph, head)
    return jnp.concatenate([head, main[8:]], axis=0)


def _shift_up(xb, next8, j):
    tt = xb.shape[0]
    main = pltpu.roll(xb, tt - j, 0)
    tail = pltpu.roll(xb[tt - 8:tt], 8 - j, 0)
    nh = pltpu.roll(next8, 8 - j, 0)
    row8 = lax.broadcasted_iota(jnp.int32, tail.shape, 0)
    tail = jnp.where(row8 + j >= 8, nh, tail)
    return jnp.concatenate([main[:tt - 8], tail], axis=0)


def _conv_hid(xb, prev8, w, b_row, K):
    out = b_row
    shifted = []
    for j in range(K):
        sh = K - 1 - j
        xs = xb if sh == 0 else _shift_down(xb, prev8, sh)
        shifted.append(xs)
        out = out + xs * w[j:j + 1, :]
    return out, shifted


def _prev_idx(i, nb8):
    return jnp.maximum(i * nb8 - 1, 0)


def _ssm_conv_fwd(zx, w, b, *, name, tt=512, tc=512):
    T = zx.shape[0]
    tt = min(tt, T)
    C, K = CONV_DIM, SSM_CONV
    cb0, nb8 = D_INNER // tc, tt // 8

    def body(x_ref, p_ref, w_ref, b_ref, o_ref):
        first = (pl.program_id(1) > 0).astype(F32)
        hid, _ = _conv_hid(x_ref[...], p_ref[...] * first, w_ref[...], b_ref[...], K)
        o_ref[...] = hid * _sigmoid(hid)

    return pl.pallas_call(
        body, name=name, grid=(C // tc, T // tt),
        in_specs=[pl.BlockSpec((tt, tc), lambda c, i: (i, c + cb0)),
                  pl.BlockSpec((8, tc), lambda c, i: (_prev_idx(i, nb8), c + cb0)),
                  pl.BlockSpec((K, tc), lambda c, i: (0, c)), pl.BlockSpec((1, tc), lambda c, i: (0, c))],
        out_specs=pl.BlockSpec((tt, tc), lambda c, i: (i, c)),
        out_shape=jax.ShapeDtypeStruct((T, C), F32),
        compiler_params=_cparams(("parallel", "parallel")))(zx, zx, w, b)


def _ssm_conv_bwd_pre(zx, w, b, dout, *, name, tt=512, tc=512):
    T = zx.shape[0]
    tt = min(tt, T)
    C, K = CONV_DIM, SSM_CONV
    cb0, nb8 = D_INNER // tc, tt // 8

    def body(x_ref, p_ref, w_ref, b_ref, d_ref, dh_ref, dw_ref, db_ref):
        t = pl.program_id(1)
        first = (t > 0).astype(F32)
        hid, shifted = _conv_hid(x_ref[...], p_ref[...] * first, w_ref[...], b_ref[...], K)
        sg = _sigmoid(hid)
        dh = d_ref[...] * (sg * (1.0 + hid * (1.0 - sg)))
        dh_ref[...] = dh

        @pl.when(t == 0)
        def _():
            dw_ref[...] = jnp.zeros_like(dw_ref)
            db_ref[...] = jnp.zeros_like(db_ref)

        db_ref[...] += jnp.sum(dh, axis=0, keepdims=True)
        for j in range(K):
            dw_ref[j:j + 1, :] += jnp.sum(dh * shifted[j], axis=0, keepdims=True)

    return pl.pallas_call(
        body, name=name, grid=(C // tc, T // tt),
        in_specs=[pl.BlockSpec((tt, tc), lambda c, i: (i, c + cb0)),
                  pl.BlockSpec((8, tc), lambda c, i: (_prev_idx(i, nb8), c + cb0)),
                  pl.BlockSpec((K, tc), lambda c, i: (0, c)), pl.BlockSpec((1, tc), lambda c, i: (0, c)),
                  pl.BlockSpec((tt, tc), lambda c, i: (i, c))],
        out_specs=[pl.BlockSpec((tt, tc), lambda c, i: (i, c)), pl.BlockSpec((K, tc), lambda c, i: (0, c)),
                   pl.BlockSpec((1, tc), lambda c, i: (0, c))],
        out_shape=[jax.ShapeDtypeStruct((T, C), F32), jax.ShapeDtypeStruct((K, C), F32),
                   jax.ShapeDtypeStruct((1, C), F32)],
        compiler_params=_cparams(("parallel", "arbitrary")))(zx, zx, w, b, dout)


def _conv_bwd_in(dh, w, *, name, K, tt=512, tc=512, out_dtype=BF16):
    T, C = dh.shape
    tt = min(tt, T)
    nb8, nT = tt // 8, T // tt
    last8 = T // 8 - 1

    def body(d_ref, n_ref, w_ref, o_ref):
        notlast = (pl.program_id(1) < nT - 1).astype(F32)
        d = d_ref[...]
        nxt = n_ref[...] * notlast
        w_ = w_ref[...]
        acc = d * w_[K - 1:K, :]
        for sh in range(1, K):
            acc = acc + _shift_up(d, nxt, sh) * w_[K - 1 - sh:K - sh, :]
        o_ref[...] = acc.astype(out_dtype)

    return pl.pallas_call(
        body, name=name, grid=(C // tc, nT),
        in_specs=[pl.BlockSpec((tt, tc), lambda c, i: (i, c)),
                  pl.BlockSpec((8, tc), lambda c, i: (jnp.minimum((i + 1) * nb8, last8), c)),
                  pl.BlockSpec((K, tc), lambda c, i: (0, c))],
        out_specs=pl.BlockSpec((tt, tc), lambda c, i: (i, c)),
        out_shape=jax.ShapeDtypeStruct((T, C), out_dtype),
        compiler_params=_cparams(("parallel", "parallel")))(dh, dh, w)


def _ffn_conv_fwd(a, w, b, *, name, tt=256, tc=1408):
    T = a.shape[0]
    tt = min(tt, T)
    K, nbh, nb8 = FFN_CONV, D_FF // tc, tt // 8

    def body(ag_ref, pg_ref, av_ref, pv_ref, wg_ref, wv_ref, bg_ref, bv_ref, o_ref):
        first = (pl.program_id(1) > 0).astype(F32)
        hg, _ = _conv_hid(ag_ref[...], pg_ref[...] * first, wg_ref[...], bg_ref[...], K)
        hv, _ = _conv_hid(av_ref[...], pv_ref[...] * first, wv_ref[...], bv_ref[...], K)
        o_ref[...] = (hg * _sigmoid(hg) * hv).astype(BF16)

    return pl.pallas_call(
        body, name=name, grid=(nbh, T // tt),
        in_specs=[pl.BlockSpec((tt, tc), lambda c, i: (i, c)),
                  pl.BlockSpec((8, tc), lambda c, i: (_prev_idx(i, nb8), c)),
                  pl.BlockSpec((tt, tc), lambda c, i: (i, c + nbh)),
                  pl.BlockSpec((8, tc), lambda c, i: (_prev_idx(i, nb8), c + nbh)),
                  pl.BlockSpec((K, tc), lambda c, i: (0, c)), pl.BlockSpec((K, tc), lambda c, i: (0, c + nbh)),
                  pl.BlockSpec((1, tc), lambda c, i: (0, c)), pl.BlockSpec((1, tc), lambda c, i: (0, c + nbh))],
        out_specs=pl.BlockSpec((tt, tc), lambda c, i: (i, c)),
        out_shape=jax.ShapeDtypeStruct((T, D_FF), BF16),
        compiler_params=_cparams(("parallel", "parallel")))(a, a, a, a, w, w, b, b)


def _ffn_conv_bwd_pre(a, w, b, dp, *, name, tt=256, tc=1408):
    T = a.shape[0]
    tt = min(tt, T)
    K, nbh, nb8 = FFN_CONV, D_FF // tc, tt // 8

    def body(ao_ref, po_ref, ag_ref, pg_ref, av_ref, pv_ref, wg_ref, wv_ref, bg_ref, bv_ref, dp_ref,
             dh_ref, dw_ref, db_ref):
        j = pl.program_id(0)
        t = pl.program_id(1)
        first = (t > 0).astype(F32)
        hg, _ = _conv_hid(ag_ref[...], pg_ref[...] * first, wg_ref[...], bg_ref[...], K)
        hv, _ = _conv_hid(av_ref[...], pv_ref[...] * first, wv_ref[...], bv_ref[...], K)
        sg = _sigmoid(hg)
        d = dp_ref[...].astype(F32)
        is_gate = (j < nbh).astype(F32)
        dh = d * (is_gate * (hv * (sg * (1.0 + hg * (1.0 - sg)))) + (1.0 - is_gate) * (hg * sg))
        dh_ref[...] = dh
        xo = ao_ref[...]
        po = po_ref[...] * first

        @pl.when(t == 0)
        def _():
            dw_ref[...] = jnp.zeros_like(dw_ref)
            db_ref[...] = jnp.zeros_like(db_ref)

        db_ref[...] += jnp.sum(dh, axis=0, keepdims=True)
        for jj in range(K):
            sh = K - 1 - jj
            xs = xo if sh == 0 else _shift_down(xo, po, sh)
            dw_ref[jj:jj + 1, :] += jnp.sum(dh * xs, axis=0, keepdims=True)

    def gi(c):
        return lax.rem(c, nbh)

    return pl.pallas_call(
        body, name=name, grid=(2 * nbh, T // tt),
        in_specs=[pl.BlockSpec((tt, tc), lambda c, i: (i, c)),
                  pl.BlockSpec((8, tc), lambda c, i: (_prev_idx(i, nb8), c)),
                  pl.BlockSpec((tt, tc), lambda c, i: (i, gi(c))),
                  pl.BlockSpec((8, tc), lambda c, i: (_prev_idx(i, nb8), gi(c))),
                  pl.BlockSpec((tt, tc), lambda c, i: (i, gi(c) + nbh)),
                  pl.BlockSpec((8, tc), lambda c, i: (_prev_idx(i, nb8), gi(c) + nbh)),
                  pl.BlockSpec((K, tc), lambda c, i: (0, gi(c))), pl.BlockSpec((K, tc), lambda c, i: (0, gi(c) + nbh)),
                  pl.BlockSpec((1, tc), lambda c, i: (0, gi(c))), pl.BlockSpec((1, tc), lambda c, i: (0, gi(c) + nbh)),
                  pl.BlockSpec((tt, tc), lambda c, i: (i, gi(c)))],
        out_specs=[pl.BlockSpec((tt, tc), lambda c, i: (i, c)), pl.BlockSpec((K, tc), lambda c, i: (0, c)),
                   pl.BlockSpec((1, tc), lambda c, i: (0, c))],
        out_shape=[jax.ShapeDtypeStruct((T, 2 * D_FF), F32), jax.ShapeDtypeStruct((K, 2 * D_FF), F32),
                   jax.ShapeDtypeStruct((1, 2 * D_FF), F32)],
        compiler_params=_cparams(("parallel", "arbitrary")))(a, a, a, a, a, a, w, w, b, b, dp)


def _cumsum_rows(x):
    L = x.shape[0]
    row = lax.broadcasted_iota(jnp.int32, x.shape, 0)
    k = 1
    while k < L:
        x = x + jnp.where(row >= k, pltpu.roll(x, k, 0), 0.0)
        k *= 2
    return x


def _rcumsum_rows(x):
    L = x.shape[0]
    row = lax.broadcasted_iota(jnp.int32, x.shape, 0)
    k = 1
    while k < L:
        x = x + jnp.where(row < L - k, pltpu.roll(x, L - k, 0), 0.0)
        k *= 2
    return x


def _split_terms(m, n):
    terms, rest = [], m
    for _ in range(n):
        t = rest.astype(BF16)
        terms.append(t)
        rest = rest - t.astype(F32)
    return jnp.concatenate(terms, axis=1)


def _select_dot(m, n_terms, n_out, cond):
    K = m.shape[1]
    k = lax.broadcasted_iota(jnp.int32, (K, n_out), 0)
    j = lax.broadcasted_iota(jnp.int32, (K, n_out), 1)
    sel = cond(k, j).astype(BF16)
    return jnp.dot(_split_terms(m, n_terms), jnp.concatenate([sel] * n_terms, axis=0), preferred_element_type=F32)


def _rowsum_mxu(m):
    return _select_dot(m, 2, 128, lambda k, j: k >= 0)


def _lane_block_sums(m, width):
    shift = width.bit_length() - 1
    return _select_dot(m, 2, 128, lambda k, j: j == jnp.right_shift(k, shift))


def _heads_to_pairs(m):
    return _select_dot(m, 3, 512, lambda k, j: k == jnp.right_shift(j, 6))


def _ssd_common(dt_ref, par_ref):
    par = par_ref[...]
    raw = dt_ref[...] + par[0:1, :]
    dt = _softplus(raw)
    a = -jnp.exp(par[1:2, :])
    cs = _cumsum_rows(dt * a)
    L = cs.shape[0]
    cs_last = cs[L - 1:L, :]
    return raw, dt, a, par[2:3, :], cs, cs.T, jnp.exp(cs), jnp.exp(cs_last - cs), jnp.exp(cs_last)


def _ssd_specs(nc, rev):
    L = SSM_CHUNK

    def ci(c):
        return nc - 1 - c if rev else c

    return [pl.BlockSpec((L, D_INNER), lambda c: (ci(c), 0)),
            pl.BlockSpec((L, GN), lambda c: (ci(c), D_INNER // GN)),
            pl.BlockSpec((L, GN), lambda c: (ci(c), D_INNER // GN + 1)),
            pl.BlockSpec((SSM_GROUPS, L, 128), lambda c: (0, ci(c), 0)),
            pl.BlockSpec((SSM_GROUPS, 8, 128), lambda c: (0, 0, 0)),
            pl.BlockSpec((L, D_INNER), lambda c: (ci(c), 0)),
            pl.BlockSpec((1, D_INNER), lambda c: (0, 0))], ci


def _round_robin(gens):
    live = list(gens)
    while live:
        nxt = []
        for gen in live:
            try:
                next(gen)
                nxt.append(gen)
            except StopIteration:
                pass
        live = nxt


def _group_views(g, wide, narrow, lead):
    return ([r.at[:, g * 512:(g + 1) * 512] for r in wide], [r.at[:, g * 128:(g + 1) * 128] for r in narrow],
            [r.at[g] for r in lead])


def _ssd_fwd(xbc_c, zx, dtg, par, gnw, *, name):
    T = xbc_c.shape[0]
    L = SSM_CHUNK
    nc = T // L
    in_specs, ci = _ssd_specs(nc, False)

    def body(xs_ref, b_ref, c_ref, dt_ref, par_ref, z_ref, gnw_ref, y_ref, yn_ref, st_ref, h_ref):
        @pl.when(pl.program_id(0) == 0)
        def _():
            h_ref[...] = jnp.zeros_like(h_ref)

        gens = []
        for g in range(SSM_GROUPS):
            (xs, z, gw, y, yn), (b, c), (dt, pr, st, h) = _group_views(
                g, [xs_ref, z_ref, gnw_ref, y_ref, yn_ref], [b_ref, c_ref], [dt_ref, par_ref, st_ref, h_ref])
            gens.append(group(xs, b, c, dt, pr, z, gw, y, yn, st, h))
        _round_robin(gens)

    def group(xs_ref, b_ref, c_ref, dt_ref, par_ref, z_ref, gnw_ref, y_ref, yn_ref, st_ref, h_ref):
        _, dt, _, dsk, cs, csT, ecs, eend, dec = _ssd_common(dt_ref, par_ref)
        Bb = b_ref[...].astype(BF16)
        Cb = c_ref[...].astype(BF16)
        G = lax.dot_general(Cb, Bb, _NT, preferred_element_type=F32)
        row = lax.broadcasted_iota(jnp.int32, (L, L), 0)
        col = lax.broadcasted_iota(jnp.int32, (L, L), 1)
        tril = col <= row
        lo = lax.broadcasted_iota(jnp.int32, (L, 128), 1) < 64
        lo1 = lax.broadcasted_iota(jnp.int32, (1, 128), 1) < 64
        dt_x, ecs_x, eend_x = (_heads_to_pairs(m) for m in (dt, ecs, eend))
        for pp in range(4):
            hA, hB = 2 * pp, 2 * pp + 1
            lanes = slice(pp * 128, (pp + 1) * 128)

            def sel1(m):
                return jnp.where(lo1, m[:, hA:hA + 1], m[:, hB:hB + 1])

            X = xs_ref[:, lanes]
            xd = X * dt_x[:, lanes]
            xdb = xd.astype(BF16)
            ys = []
            for h in (hA, hB):
                Lm = jnp.where(tril, jnp.exp(jnp.minimum(cs[:, h:h + 1] - csT[h:h + 1, :], 0.0)), 0.0)
                ys.append(jnp.dot((G * Lm).astype(BF16), xdb, preferred_element_type=F32))
                yield
            Hp = h_ref[pp]
            st_ref[pp] = Hp
            yoff = jnp.dot(Cb, Hp.astype(BF16), preferred_element_type=F32) * ecs_x[:, lanes]
            y_ref[:, lanes] = jnp.where(lo, ys[0], ys[1]) + yoff + sel1(dsk) * X
            S = lax.dot_general(Bb, (xd * eend_x[:, lanes]).astype(BF16), _TN, preferred_element_type=F32)
            h_ref[pp] = Hp * sel1(dec) + S
            yield
        zv = z_ref[...]
        yg = y_ref[...] * (zv * _sigmoid(zv))
        r = jnp.tile(lax.rsqrt(_rowsum_mxu(yg * yg) * (1.0 / 512) + EPS), (1, 4))
        yn_ref[...] = (yg * r * gnw_ref[...]).astype(BF16)

    return pl.pallas_call(
        body, name=name, grid=(nc,), in_specs=in_specs,
        out_specs=[pl.BlockSpec((L, D_INNER), lambda c: (c, 0)), pl.BlockSpec((L, D_INNER), lambda c: (c, 0)),
                   pl.BlockSpec((SSM_GROUPS, None, 4, 128, 128), lambda c: (0, c, 0, 0, 0))],
        out_shape=[jax.ShapeDtypeStruct((T, D_INNER), F32), jax.ShapeDtypeStruct((T, D_INNER), BF16),
                   jax.ShapeDtypeStruct((SSM_GROUPS, nc, 4, 128, 128), F32)],
        scratch_shapes=[pltpu.VMEM((SSM_GROUPS, 4, 128, 128), F32)],
        compiler_params=_cparams(("arbitrary",)))(xbc_c, xbc_c, xbc_c, dtg, par, zx, gnw)


def _ssd_bwd(xbc_c, zx, dtg, par, gnw, y, st, dyn, *, name):
    T = xbc_c.shape[0]
    L = SSM_CHUNK
    nc = T // L
    in_specs, ci = _ssd_specs(nc, True)
    in_specs += [pl.BlockSpec((L, D_INNER), lambda c: (ci(c), 0)),
                 pl.BlockSpec((SSM_GROUPS, None, 4, 128, 128), lambda c: (0, ci(c), 0, 0, 0)),
                 pl.BlockSpec((L, D_INNER), lambda c: (ci(c), 0))]

    def body(xs_ref, b_ref, c_ref, dt_ref, par_ref, z_ref, gnw_ref, y_ref, st_ref, dyn_ref,
             dxs_ref, db_ref, dc_ref, dz_ref, ddt_ref, dgnw_ref, dpar_ref, dh_ref):
        @pl.when(pl.program_id(0) == 0)
        def _():
            dh_ref[...] = jnp.zeros_like(dh_ref)
            dgnw_ref[...] = jnp.zeros_like(dgnw_ref)
            dpar_ref[...] = jnp.zeros_like(dpar_ref)

        gens = []
        for g in range(SSM_GROUPS):
            (xs, z, gw, y, dyn, dxs, dz, dgw), (b, c, db, dc), (dt, pr, st, ddt, dpr, dh) = _group_views(
                g, [xs_ref, z_ref, gnw_ref, y_ref, dyn_ref, dxs_ref, dz_ref, dgnw_ref], [b_ref, c_ref, db_ref, dc_ref],
                [dt_ref, par_ref, st_ref, ddt_ref, dpar_ref, dh_ref])
            gens.append(group(xs, b, c, dt, pr, z, gw, y, st, dyn, dxs, db, dc, dz, ddt, dgw, dpr, dh))
        _round_robin(gens)

    def group(xs_ref, b_ref, c_ref, dt_ref, par_ref, z_ref, gnw_ref, y_ref, st_ref, dyn_ref,
              dxs_ref, db_ref, dc_ref, dz_ref, ddt_ref, dgnw_ref, dpar_ref, dh_ref):
        yv = y_ref[...]
        zv = z_ref[...]
        sg = _sigmoid(zv)
        sz = zv * sg
        yg = yv * sz
        r = jnp.tile(lax.rsqrt(_rowsum_mxu(yg * yg) * (1.0 / 512) + EPS), (1, 4))
        yh = yg * r
        dyn = dyn_ref[...].astype(F32)
        dgnw_ref[...] += jnp.sum(dyn * yh, axis=0, keepdims=True)
        dyh = dyn * gnw_ref[...]
        dyg = r * (dyh - yh * jnp.tile(_rowsum_mxu(dyh * yh) * (1.0 / 512), (1, 4)))
        dY_all = dyg * sz
        dz_ref[...] = (dyg * yv * (sg * (1.0 + zv * (1.0 - sg)))).astype(dz_ref.dtype)

        yield
        raw, dt, a, dsk, cs, csT, ecs, eend, dec = _ssd_common(dt_ref, par_ref)
        Bb = b_ref[...].astype(BF16)
        Cb = c_ref[...].astype(BF16)
        G = lax.dot_general(Cb, Bb, _NT, preferred_element_type=F32)
        row = lax.broadcasted_iota(jnp.int32, (L, L), 0)
        col = lax.broadcasted_iota(jnp.int32, (L, L), 1)
        tril = col <= row
        lo = lax.broadcasted_iota(jnp.int32, (L, 128), 1) < 64
        lane1 = lax.broadcasted_iota(jnp.int32, (1, 128), 1)
        lo1 = lane1 < 64
        rowl = lax.broadcasted_iota(jnp.int32, (L, 128), 0)
        dt_x, ecs_x, eend_x = (_heads_to_pairs(m) for m in (dt, ecs, eend))
        dG = jnp.zeros((L, L), F32)
        dB = jnp.zeros((L, SSM_STATE), F32)
        dC = jnp.zeros((L, SSM_STATE), F32)
        dcs_t = jnp.zeros((L, L), F32)
        tails = jnp.zeros((1, 128), F32)
        dD_row = jnp.zeros((1, 128), F32)
        v_parts, prod_parts = [], []

        def tot(m):
            return jnp.sum(jnp.sum(m, axis=0, keepdims=True), axis=1, keepdims=True)

        for pp in range(4):
            hA, hB = 2 * pp, 2 * pp + 1
            lanes = slice(pp * 128, (pp + 1) * 128)

            def sel1(m):
                return jnp.where(lo1, m[:, hA:hA + 1], m[:, hB:hB + 1])

            X = xs_ref[:, lanes]
            dY = dY_all[:, lanes]
            dtsel = dt_x[:, lanes]
            xd = X * dtsel
            xdb = xd.astype(BF16)
            dYb = dY.astype(BF16)
            Hp = st_ref[pp]
            Hb = Hp.astype(BF16)
            dHn = dh_ref[pp]
            dHb = dHn.astype(BF16)
            ecs_sel = ecs_x[:, lanes]
            eend_sel = eend_x[:, lanes]
            dxd_state = jnp.dot(Bb, dHb, preferred_element_type=F32) * eend_sel
            yoff = jnp.dot(Cb, Hb, preferred_element_type=F32) * ecs_sel
            dYe = (dY * ecs_sel).astype(BF16)
            dC = dC + lax.dot_general(dYe, Hb, _NT, preferred_element_type=F32)
            dB = dB + lax.dot_general((xd * eend_sel).astype(BF16), dHb, _NT, preferred_element_type=F32)
            dh_ref[pp] = dHn * sel1(dec) + lax.dot_general(Cb, dYe, _TN, preferred_element_type=F32)
            q = xd * dxd_state
            dyq = dY * yoff - q
            qcol = jnp.sum(q, axis=0, keepdims=True)
            hcol = jnp.sum(dHn * Hp, axis=0, keepdims=True)
            dxd_diag = []
            for h, msk, msk1 in ((hA, lo, lo1), (hB, jnp.logical_not(lo), jnp.logical_not(lo1))):
                Lm = jnp.where(tril, jnp.exp(jnp.minimum(cs[:, h:h + 1] - csT[h:h + 1, :], 0.0)), 0.0)
                M = G * Lm
                dxd_diag.append(lax.dot_general(M.astype(BF16), dYb, _TN, preferred_element_type=F32))
                dM = lax.dot_general(jnp.where(msk, dY, 0.0).astype(BF16), xdb, _NT, preferred_element_type=F32)
                dG = dG + dM * Lm
                W = dM * M
                dcs_t = dcs_t + jnp.where(row == h, jnp.sum(W, axis=0, keepdims=True), 0.0)
                v_parts.append(W + jnp.where(msk, dyq, 0.0))
                tail = (jnp.sum(jnp.where(msk1, qcol, 0.0), axis=1, keepdims=True)
                        + dec[:, h:h + 1] * jnp.sum(jnp.where(msk1, hcol, 0.0), axis=1, keepdims=True))
                tails = tails + jnp.where(lane1 == h, tail, 0.0)
                yield
            dxd = jnp.where(lo, dxd_diag[0], dxd_diag[1]) + dxd_state
            prod_parts.append(dxd * X)
            dxs_ref[:, lanes] = dxd * dtsel + sel1(dsk) * dY
            dyx = jnp.sum(dY * X, axis=0, keepdims=True)
            sA = jnp.sum(jnp.where(lo1, dyx, 0.0), axis=1, keepdims=True)
            sB = jnp.sum(dyx, axis=1, keepdims=True) - sA
            dD_row = dD_row + jnp.where(lane1 == hA, sA, 0.0) + jnp.where(lane1 == hB, sB, 0.0)
            yield
        dGb = dG.astype(BF16)
        db_ref[...] = dB + lax.dot_general(dGb, Cb, _TN, preferred_element_type=F32)
        dc_ref[...] = dC + jnp.dot(dGb, Bb, preferred_element_type=F32)
        dcs_mat = _lane_block_sums(jnp.concatenate(v_parts, axis=1), 128) + jnp.where(rowl == L - 1, tails, 0.0)
        ddt_mat = _lane_block_sums(jnp.concatenate(prod_parts, axis=1), 64)
        dad = _rcumsum_rows(dcs_mat - dcs_t.T)
        draw = (a * dad + ddt_mat) * _sigmoid(raw)
        ddt_ref[...] = draw
        dpar_ref[0:1, :] += jnp.sum(draw, axis=0, keepdims=True)
        dpar_ref[1:2, :] += jnp.sum(dt * dad, axis=0, keepdims=True) * a
        dpar_ref[2:3, :] += dD_row

    return pl.pallas_call(
        body, name=name, grid=(nc,), in_specs=in_specs,
        out_specs=[pl.BlockSpec((L, D_INNER), lambda c: (ci(c), 0)),
                   pl.BlockSpec((L, GN), lambda c: (ci(c), 0)),
                   pl.BlockSpec((L, GN), lambda c: (ci(c), 0)),
                   pl.BlockSpec((L, D_INNER), lambda c: (ci(c), 0)),
                   pl.BlockSpec((SSM_GROUPS, L, 128), lambda c: (0, ci(c), 0)),
                   pl.BlockSpec((1, D_INNER), lambda c: (0, 0)),
                   pl.BlockSpec((SSM_GROUPS, 8, 128), lambda c: (0, 0, 0))],
        out_shape=[jax.ShapeDtypeStruct((T, D_INNER), F32), jax.ShapeDtypeStruct((T, GN), F32),
                   jax.ShapeDtypeStruct((T, GN), F32), jax.ShapeDtypeStruct((T, D_INNER), BF16),
                   jax.ShapeDtypeStruct((SSM_GROUPS, T, 128), F32), jax.ShapeDtypeStruct((1, D_INNER), F32),
                   jax.ShapeDtypeStruct((SSM_GROUPS, 8, 128), F32)],
        scratch_shapes=[pltpu.VMEM((SSM_GROUPS, 4, 128, 128), F32)],
        compiler_params=_cparams(("arbitrary",)))(xbc_c, xbc_c, xbc_c, dtg, par, zx, gnw, y, st, dyn)


SB_KEYS = 512
SB_SCAN = 256
SB_STRIP = 256


def _tri(width, cond):
    kk = lax.broadcasted_iota(jnp.int32, (width, width), 0)
    jj = lax.broadcasted_iota(jnp.int32, (width, width), 1)
    return cond(kk, jj).astype(BF16)


def _sba_diag_mask():
    Bq = SB_BLOCK
    rowi = lax.broadcasted_iota(jnp.int32, (2 * Bq, Bq), 0)
    return lax.broadcasted_iota(jnp.int32, (2 * Bq, Bq), 1) < jnp.where(rowi >= Bq, rowi - Bq, rowi)


_LOG2E = 1.4426950408889634


def _softplus2(z2):
    return jnp.maximum(z2, 0.0) + jnp.log2(1.0 + jnp.exp2(-jnp.abs(z2)))


def _sba_sub_fwd(zb, c, U, mask):
    z2 = zb * _LOG2E
    s = _softplus2(z2)
    if mask is not None:
        s = jnp.where(mask, s, 0.0)
    R = c + jnp.dot(s.astype(BF16), U, preferred_element_type=F32)
    A = jnp.exp2(z2 - s - R)
    if mask is not None:
        A = jnp.where(mask, A, 0.0)
    return A.astype(BF16), R[:, 0:1] + s[:, 0:1]


def _sba_sub_bwd(zb, dAb, Lt, pc, pe, Uincl, Uexcl, mask):
    last = zb.shape[1] - 1
    z2 = zb * _LOG2E
    s = _softplus2(z2)
    g = z2 - s
    if mask is not None:
        s = jnp.where(mask, s, 0.0)
    P = pc + jnp.dot(s.astype(BF16), Uincl, preferred_element_type=F32)
    A = jnp.exp2(g - (Lt - P))
    if mask is not None:
        A = jnp.where(mask, A, 0.0)
    E = dAb * A
    PE = pe + jnp.dot(E.astype(BF16), Uexcl, preferred_element_type=F32)
    dz = E - jnp.exp2(g) * (E + PE)
    if mask is not None:
        dz = jnp.where(mask, dz, 0.0)
    return (A.astype(BF16), dz.astype(BF16), P[:, last:last + 1], PE[:, last:last + 1] + E[:, last:last + 1])


def _stack_heads(v):
    lo = lax.broadcasted_iota(jnp.int32, v.shape, 1) < 64
    zero = jnp.zeros_like(v)
    return jnp.concatenate([jnp.where(lo, v, zero), jnp.where(lo, zero, v)], axis=0)


def _unstack_heads(v):
    lo = lax.broadcasted_iota(jnp.int32, (SB_BLOCK, 128), 1) < 64
    return jnp.where(lo, v[:SB_BLOCK], v[SB_BLOCK:])


def _sba_rows(a):
    return slice(2 * a * SB_BLOCK, 2 * (a + 1) * SB_BLOCK)


def _sba_diag_case(a, b):
    Bq = SB_BLOCK
    if b * SB_SCAN >= (a + 1) * Bq:
        return "skip"
    if (b + 1) * SB_SCAN <= a * Bq:
        return "full"
    rowi = lax.broadcasted_iota(jnp.int32, (2 * Bq, SB_SCAN), 0)
    qpos = a * Bq + jnp.where(rowi >= Bq, rowi - Bq, rowi)
    return b * SB_SCAN + lax.broadcasted_iota(jnp.int32, (2 * Bq, SB_SCAN), 1) < qpos


def _sba_fwd(q, kv, *, name):
    T = q.shape[0]
    Bq = SB_BLOCK
    nsub = SB_KEYS // Bq
    nscan = SB_KEYS // SB_SCAN
    R = 2 * SB_KEYS
    assert T % SB_KEYS == 0 and SB_STRIP == 2 * Bq
    scale = 1.0 / math.sqrt(SB_HEAD_DIM)

    def body(q_ref, k_ref, v_ref, o_ref, lt_ref, z_s, a_s, c_s, acc_s):
        i = pl.program_id(1)
        U2 = _tri(SB_SCAN, lambda k, j: k > j)
        qs_all = jnp.concatenate([_stack_heads(q_ref[a * Bq:(a + 1) * Bq, :] * scale) for a in range(nsub)], axis=0)
        c_s[...] = jnp.zeros_like(c_s)
        acc_s[...] = jnp.zeros_like(acc_s)

        def scores(J, slot):
            off = pl.multiple_of(J * SB_KEYS, SB_KEYS)
            z_s[slot] = lax.dot_general(qs_all, k_ref[pl.ds(off, SB_KEYS), :], _NT, preferred_element_type=F32)

        def weights(slot, diag):
            for a in range(nsub):
                rows = _sba_rows(a)
                c = c_s[rows, :]
                for b in reversed(range(nscan)):
                    cols = slice(b * SB_SCAN, (b + 1) * SB_SCAN)
                    case = _sba_diag_case(a, b) if diag else "full"
                    if isinstance(case, str) and case == "skip":
                        a_s[slot, rows, cols] = jnp.zeros((2 * Bq, SB_SCAN), BF16)
                        continue
                    A, c = _sba_sub_fwd(z_s[slot, rows, cols], c, U2, None if isinstance(case, str) else case)
                    a_s[slot, rows, cols] = A
                c_s[rows, :] = c

        def values(J, slot):
            off = pl.multiple_of(J * SB_KEYS, SB_KEYS)
            acc_s[...] += jnp.dot(a_s[slot], v_ref[pl.ds(off, SB_KEYS), :], preferred_element_type=F32)

        scores(i, 0)
        weights(0, True)
        scores(jnp.maximum(i - 1, 0), 1)

        def step(t, _):
            slot = lax.rem(t, 2)
            weights(slot, False)
            scores(jnp.maximum(i - t - 1, 0), 1 - slot)
            values(i - t + 1, 1 - slot)
            return 0

        lax.fori_loop(1, i + 1, step, 0)
        values(0, lax.rem(i, 2))
        for a in range(nsub):
            o_ref[a * Bq:(a + 1) * Bq, :] = _unstack_heads(acc_s[_sba_rows(a), :]).astype(BF16)
            lt_ref[a * Bq:(a + 1) * Bq, :] = _unstack_heads(jnp.broadcast_to(c_s[_sba_rows(a), :], (2 * Bq, 128)))

    return pl.pallas_call(
        body, name=name, grid=(SB_HEADS // 2, T // SB_KEYS),
        in_specs=[pl.BlockSpec((SB_KEYS, 128), lambda p, i: (i, p)), pl.BlockSpec((T, 128), lambda p, i: (0, p)),
                  pl.BlockSpec((T, 128), lambda p, i: (0, p + SB_HEADS // 2))],
        out_specs=[pl.BlockSpec((SB_KEYS, 128), lambda p, i: (i, p)),
                   pl.BlockSpec((None, SB_KEYS, 128), lambda p, i: (p, i, 0))],
        out_shape=[jax.ShapeDtypeStruct((T, D_MODEL), BF16), jax.ShapeDtypeStruct((SB_HEADS // 2, T, 128), F32)],
        scratch_shapes=[pltpu.VMEM((2, R, SB_KEYS), F32), pltpu.VMEM((2, R, SB_KEYS), BF16),
                        pltpu.VMEM((R, 1), F32), pltpu.VMEM((R, 128), F32)],
        compiler_params=_cparams(("parallel", "parallel")))(q, kv, kv)


def _sba_bwd(q, kv, lt, do, *, name):
    T = q.shape[0]
    Bq = SB_BLOCK
    nq = T // SB_KEYS
    nsub = SB_KEYS // Bq
    nscan = SB_KEYS // SB_SCAN
    R = 2 * SB_KEYS
    assert T % SB_KEYS == 0 and SB_STRIP == 2 * Bq
    scale = 1.0 / math.sqrt(SB_HEAD_DIM)

    def body(q_ref, k_ref, v_ref, lt_ref, do_ref, dq_ref, dk_ref, dv_ref, dk_acc, dv_acc,
             z_s, da_s, a_s, dz_s, pc_s, pe_s, lt_s):
        i = pl.program_id(1)

        @pl.when(i == 0)
        def _():
            dk_acc[...] = jnp.zeros_like(dk_acc)
            dv_acc[...] = jnp.zeros_like(dv_acc)

        Uincl = _tri(SB_SCAN, lambda k, j: k <= j)
        Uexcl = _tri(SB_SCAN, lambda k, j: k < j)
        qs, dos = [], []
        for a in range(nsub):
            rows = slice(a * Bq, (a + 1) * Bq)
            qs.append(_stack_heads(q_ref[rows, :] * scale))
            dos.append(_stack_heads(do_ref[rows, :]))
            lt_s[_sba_rows(a), :] = jnp.concatenate([lt_ref[rows, 0:1], lt_ref[rows, 64:65]], axis=0)
        qs_all = jnp.concatenate(qs, axis=0)
        dos_all = jnp.concatenate(dos, axis=0)
        pc_s[...] = jnp.zeros_like(pc_s)
        pe_s[...] = jnp.zeros_like(pe_s)
        a_s[1] = jnp.zeros((R, SB_KEYS), BF16)
        dz_s[1] = jnp.zeros((R, SB_KEYS), BF16)

        def scores(J, slot):
            off = pl.multiple_of(J * SB_KEYS, SB_KEYS)
            z_s[slot] = lax.dot_general(qs_all, k_ref[pl.ds(off, SB_KEYS), :], _NT, preferred_element_type=F32)
            da_s[slot] = lax.dot_general(dos_all, v_ref[pl.ds(off, SB_KEYS), :], _NT, preferred_element_type=F32)

        def gradients(slot, diag):
            for a in range(nsub):
                rows = _sba_rows(a)
                pc, pe, Lt = pc_s[rows, :], pe_s[rows, :], lt_s[rows, :]
                for b in range(nscan):
                    cols = slice(b * SB_SCAN, (b + 1) * SB_SCAN)
                    case = _sba_diag_case(a, b) if diag else "full"
                    if isinstance(case, str) and case == "skip":
                        a_s[slot, rows, cols] = jnp.zeros((2 * Bq, SB_SCAN), BF16)
                        dz_s[slot, rows, cols] = jnp.zeros((2 * Bq, SB_SCAN), BF16)
                        continue
                    A, dz, pc, pe = _sba_sub_bwd(z_s[slot, rows, cols], da_s[slot, rows, cols], Lt, pc, pe, Uincl, Uexcl,
                                                 None if isinstance(case, str) else case)
                    a_s[slot, rows, cols] = A
                    dz_s[slot, rows, cols] = dz
                pc_s[rows, :] = pc
                pe_s[rows, :] = pe

        def products(J, slot, dq_acc):
            off = pl.multiple_of(J * SB_KEYS, SB_KEYS)
            dzt = dz_s[slot]
            dk_acc[pl.ds(off, SB_KEYS), :] += lax.dot_general(dzt, qs_all, _TN, preferred_element_type=F32)
            dv_acc[pl.ds(off, SB_KEYS), :] += lax.dot_general(a_s[slot], dos_all, _TN, preferred_element_type=F32)
            return dq_acc + jnp.dot(dzt, k_ref[pl.ds(off, SB_KEYS), :], preferred_element_type=F32)

        scores(0, 0)

        def step(t, dq_acc):
            slot = lax.rem(t, 2)
            gradients(slot, False)
            scores(t + 1, 1 - slot)
            return products(jnp.maximum(t - 1, 0), 1 - slot, dq_acc)

        dq_acc = lax.fori_loop(0, i, step, jnp.zeros((R, 128), F32))
        own = lax.rem(i, 2)
        gradients(own, True)
        dq_acc = products(jnp.maximum(i - 1, 0), 1 - own, dq_acc)
        dq_acc = products(i, own, dq_acc)
        for a in range(nsub):
            dq_ref[a * Bq:(a + 1) * Bq, :] = (_unstack_heads(dq_acc[_sba_rows(a)]) * scale).astype(BF16)

        @pl.when(i == nq - 1)
        def _():
            dk_ref[...] = dk_acc[...].astype(BF16)
            dv_ref[...] = dv_acc[...].astype(BF16)

    return pl.pallas_call(
        body, name=name, grid=(SB_HEADS // 2, nq),
        in_specs=[pl.BlockSpec((SB_KEYS, 128), lambda p, i: (i, p)), pl.BlockSpec((T, 128), lambda p, i: (0, p)),
                  pl.BlockSpec((T, 128), lambda p, i: (0, p + SB_HEADS // 2)),
                  pl.BlockSpec((None, SB_KEYS, 128), lambda p, i: (p, i, 0)),
                  pl.BlockSpec((SB_KEYS, 128), lambda p, i: (i, p))],
        out_specs=[pl.BlockSpec((SB_KEYS, 128), lambda p, i: (i, p)), pl.BlockSpec((T, 128), lambda p, i: (0, p)),
                   pl.BlockSpec((T, 128), lambda p, i: (0, p))],
        out_shape=[jax.ShapeDtypeStruct((T, D_MODEL), BF16), jax.ShapeDtypeStruct((T, D_MODEL), BF16),
                   jax.ShapeDtypeStruct((T, D_MODEL), BF16)],
        scratch_shapes=[pltpu.VMEM((T, 128), F32), pltpu.VMEM((T, 128), F32),
                        pltpu.VMEM((2, R, SB_KEYS), F32), pltpu.VMEM((2, R, SB_KEYS), F32),
                        pltpu.VMEM((2, R, SB_KEYS), BF16), pltpu.VMEM((2, R, SB_KEYS), BF16),
                        pltpu.VMEM((R, 1), F32), pltpu.VMEM((R, 1), F32), pltpu.VMEM((R, 1), F32)],
        compiler_params=_cparams(("parallel", "arbitrary")))(q, kv, kv, lt, do)


def _sba_fwd_old(q, kv, *, name):
    T = q.shape[0]
    Bq = SB_BLOCK
    nsub = SB_KEYS // Bq
    assert T % SB_KEYS == 0
    scale = 1.0 / math.sqrt(SB_HEAD_DIM)

    def body(q_ref, k_ref, v_ref, o_ref, lt_ref):
        I = pl.program_id(1)
        U1 = _tri(Bq, lambda k, j: k > j)
        U2 = _tri(SB_SCAN, lambda k, j: k > j)
        dmask = _sba_diag_mask()
        qs = [_stack_heads(q_ref[a * Bq:(a + 1) * Bq, :] * scale) for a in range(nsub)]
        cs, accs = [], []
        for a in range(nsub):
            c = jnp.zeros((2 * Bq, 1), F32)
            acc = jnp.zeros((2 * Bq, 128), F32)
            for b in range(a, -1, -1):
                off = pl.multiple_of(I * SB_KEYS + b * Bq, Bq)
                zb = lax.dot_general(qs[a], k_ref[pl.ds(off, Bq), :], _NT, preferred_element_type=F32)
                A, c = _sba_sub_fwd(zb, c, U1, dmask if b == a else None)
                acc = acc + jnp.dot(A, v_ref[pl.ds(off, Bq), :], preferred_element_type=F32)
            cs.append(c)
            accs.append(acc)
        qs_all = jnp.concatenate(qs, axis=0)

        def step(n, carry):
            c, acc = carry
            off = pl.multiple_of((I - 1 - n) * SB_KEYS, SB_KEYS)
            z = lax.dot_general(qs_all, k_ref[pl.ds(off, SB_KEYS), :], _NT, preferred_element_type=F32)
            parts = [None] * (SB_KEYS // SB_SCAN)
            for b in reversed(range(SB_KEYS // SB_SCAN)):
                parts[b], c = _sba_sub_fwd(z[:, b * SB_SCAN:(b + 1) * SB_SCAN], c, U2, None)
            return c, acc + jnp.dot(jnp.concatenate(parts, axis=1), v_ref[pl.ds(off, SB_KEYS), :],
                                    preferred_element_type=F32)

        c, acc = lax.fori_loop(0, I, step, (jnp.concatenate(cs, axis=0), jnp.concatenate(accs, axis=0)))
        for a in range(nsub):
            rows = slice(2 * a * Bq, 2 * (a + 1) * Bq)
            o_ref[a * Bq:(a + 1) * Bq, :] = _unstack_heads(acc[rows]).astype(BF16)
            lt_ref[a * Bq:(a + 1) * Bq, :] = _unstack_heads(jnp.broadcast_to(c[rows], (2 * Bq, 128)))

    return pl.pallas_call(
        body, name=name, grid=(SB_HEADS // 2, T // SB_KEYS),
        in_specs=[pl.BlockSpec((SB_KEYS, 128), lambda p, i: (i, p)), pl.BlockSpec((T, 128), lambda p, i: (0, p)),
                  pl.BlockSpec((T, 128), lambda p, i: (0, p + SB_HEADS // 2))],
        out_specs=[pl.BlockSpec((SB_KEYS, 128), lambda p, i: (i, p)),
                   pl.BlockSpec((None, SB_KEYS, 128), lambda p, i: (p, i, 0))],
        out_shape=[jax.ShapeDtypeStruct((T, D_MODEL), BF16), jax.ShapeDtypeStruct((SB_HEADS // 2, T, 128), F32)],
        compiler_params=_cparams(("parallel", "parallel")))(q, kv, kv)


def _sba_bwd_old(q, kv, lt, do, *, name):
    T = q.shape[0]
    Bq = SB_BLOCK
    nq = T // SB_KEYS
    nsub = SB_KEYS // Bq
    assert T % SB_KEYS == 0
    scale = 1.0 / math.sqrt(SB_HEAD_DIM)

    def body(q_ref, k_ref, v_ref, lt_ref, do_ref, dq_ref, dk_ref, dv_ref, dk_acc, dv_acc,
             z_s, da_s, a_s, dz_s, pc_s, pe_s, lt_s):
        i = pl.program_id(1)

        @pl.when(i == 0)
        def _():
            dk_acc[...] = jnp.zeros_like(dk_acc)
            dv_acc[...] = jnp.zeros_like(dv_acc)

        Uincl1 = _tri(Bq, lambda k, j: k <= j)
        Uexcl1 = _tri(Bq, lambda k, j: k < j)
        Uincl2 = _tri(SB_SCAN, lambda k, j: k <= j)
        Uexcl2 = _tri(SB_SCAN, lambda k, j: k < j)
        dmask = _sba_diag_mask()
        qs, dos, lts = [], [], []
        for a in range(nsub):
            rows = slice(a * Bq, (a + 1) * Bq)
            qs.append(_stack_heads(q_ref[rows, :] * scale))
            dos.append(_stack_heads(do_ref[rows, :]))
            lts.append(jnp.concatenate([lt_ref[rows, 0:1], lt_ref[rows, 64:65]], axis=0))
        qs_all = jnp.concatenate(qs, axis=0)
        dos_all = jnp.concatenate(dos, axis=0)
        lt_all = jnp.concatenate(lts, axis=0)

        R = 2 * nsub * Bq
        pc_s[...] = jnp.zeros_like(pc_s)
        pe_s[...] = jnp.zeros_like(pe_s)
        lt_s[...] = lt_all

        def scores(J, slot):
            off = pl.multiple_of(J * SB_KEYS, SB_KEYS)
            z_s[slot] = lax.dot_general(qs_all, k_ref[pl.ds(off, SB_KEYS), :], _NT, preferred_element_type=F32)
            da_s[slot] = lax.dot_general(dos_all, v_ref[pl.ds(off, SB_KEYS), :], _NT, preferred_element_type=F32)

        def elementwise(slot):
            for r in range(R // SB_STRIP):
                rows = slice(r * SB_STRIP, (r + 1) * SB_STRIP)
                pc, pe, Lt = pc_s[rows, :], pe_s[rows, :], lt_s[rows, :]
                for b in range(SB_KEYS // SB_SCAN):
                    cols = slice(b * SB_SCAN, (b + 1) * SB_SCAN)
                    A, dz, pc, pe = _sba_sub_bwd(z_s[slot, rows, cols], da_s[slot, rows, cols], Lt, pc, pe,
                                                 Uincl2, Uexcl2, None)
                    a_s[slot, rows, cols] = A
                    dz_s[slot, rows, cols] = dz
                pc_s[rows, :] = pc
                pe_s[rows, :] = pe

        def outputs(J, slot, dq_acc):
            off = pl.multiple_of(J * SB_KEYS, SB_KEYS)
            dzt = dz_s[slot]
            dk_acc[pl.ds(off, SB_KEYS), :] += lax.dot_general(dzt, qs_all, _TN, preferred_element_type=F32)
            dv_acc[pl.ds(off, SB_KEYS), :] += lax.dot_general(a_s[slot], dos_all, _TN, preferred_element_type=F32)
            return dq_acc + jnp.dot(dzt, k_ref[pl.ds(off, SB_KEYS), :], preferred_element_type=F32)

        a_s[1] = jnp.zeros((R, SB_KEYS), BF16)
        dz_s[1] = jnp.zeros((R, SB_KEYS), BF16)
        last = jnp.maximum(i - 1, 0)
        scores(0, 0)

        def step(J, dq_acc):
            slot = lax.rem(J, 2)
            elementwise(slot)
            scores(jnp.minimum(J + 1, last), 1 - slot)
            return outputs(jnp.maximum(J - 1, 0), 1 - slot, dq_acc)

        dq_acc = lax.fori_loop(0, i, step, jnp.zeros((R, 128), F32))
        dq_acc = outputs(last, lax.rem(i + 1, 2), dq_acc)
        pc, pe = pc_s[...], pe_s[...]
        for a in range(nsub):
            rows = slice(2 * a * Bq, 2 * (a + 1) * Bq)
            pca, pea, dqa = pc[rows], pe[rows], dq_acc[rows]
            for b in range(a + 1):
                off = pl.multiple_of(i * SB_KEYS + b * Bq, Bq)
                kb = k_ref[pl.ds(off, Bq), :]
                zb = lax.dot_general(qs[a], kb, _NT, preferred_element_type=F32)
                dAb = lax.dot_general(dos[a], v_ref[pl.ds(off, Bq), :], _NT, preferred_element_type=F32)
                A, dz, pca, pea = _sba_sub_bwd(zb, dAb, lts[a], pca, pea, Uincl1, Uexcl1, dmask if b == a else None)
                dqa = dqa + jnp.dot(dz, kb, preferred_element_type=F32)
                dk_acc[pl.ds(off, Bq), :] += lax.dot_general(dz, qs[a], _TN, preferred_element_type=F32)
                dv_acc[pl.ds(off, Bq), :] += lax.dot_general(A, dos[a], _TN, preferred_element_type=F32)
            dq_ref[a * Bq:(a + 1) * Bq, :] = (_unstack_heads(dqa) * scale).astype(BF16)

        @pl.when(i == nq - 1)
        def _():
            dk_ref[...] = dk_acc[...].astype(BF16)
            dv_ref[...] = dv_acc[...].astype(BF16)

    return pl.pallas_call(
        body, name=name, grid=(SB_HEADS // 2, nq),
        in_specs=[pl.BlockSpec((SB_KEYS, 128), lambda p, i: (i, p)), pl.BlockSpec((T, 128), lambda p, i: (0, p)),
                  pl.BlockSpec((T, 128), lambda p, i: (0, p + SB_HEADS // 2)),
                  pl.BlockSpec((None, SB_KEYS, 128), lambda p, i: (p, i, 0)),
                  pl.BlockSpec((SB_KEYS, 128), lambda p, i: (i, p))],
        out_specs=[pl.BlockSpec((SB_KEYS, 128), lambda p, i: (i, p)), pl.BlockSpec((T, 128), lambda p, i: (0, p)),
                   pl.BlockSpec((T, 128), lambda p, i: (0, p))],
        out_shape=[jax.ShapeDtypeStruct((T, D_MODEL), BF16), jax.ShapeDtypeStruct((T, D_MODEL), BF16),
                   jax.ShapeDtypeStruct((T, D_MODEL), BF16)],
        scratch_shapes=[pltpu.VMEM((T, 128), F32), pltpu.VMEM((T, 128), F32),
                        pltpu.VMEM((2, 2 * SB_KEYS, SB_KEYS), F32), pltpu.VMEM((2, 2 * SB_KEYS, SB_KEYS), F32),
                        pltpu.VMEM((2, 2 * SB_KEYS, SB_KEYS), BF16), pltpu.VMEM((2, 2 * SB_KEYS, SB_KEYS), BF16),
                        pltpu.VMEM((2 * SB_KEYS, 1), F32), pltpu.VMEM((2 * SB_KEYS, 1), F32),
                        pltpu.VMEM((2 * SB_KEYS, 1), F32)],
        compiler_params=_cparams(("parallel", "arbitrary")))(q, kv, kv, lt, do)


def _loss_head(h, tgt, w, *, name, tt=512):
    T, D = h.shape
    tt = min(tt, T)

    def body(h_ref, t_ref, w_ref, loss_ref, dh_ref, dw_ref):
        i = pl.program_id(0)
        hv = h_ref[...]
        wv = w_ref[...]
        r = lax.rsqrt(jnp.mean(hv * hv, axis=-1, keepdims=True) + EPS)
        xhat = hv * r
        err = xhat * wv - t_ref[...]
        part = 0.5 * jnp.sum(jnp.mean(err * err, axis=-1, keepdims=True), axis=0, keepdims=True)
        dy = err * (1.0 / D)
        dxh = dy * wv
        dh_ref[...] = r * (dxh - xhat * jnp.mean(dxh * xhat, axis=-1, keepdims=True))
        dwc = jnp.sum(dy * xhat, axis=0, keepdims=True)

        @pl.when(i == 0)
        def _():
            loss_ref[...] = jnp.broadcast_to(part, loss_ref.shape)
            dw_ref[...] = dwc

        @pl.when(i > 0)
        def _():
            loss_ref[...] += jnp.broadcast_to(part, loss_ref.shape)
            dw_ref[...] += dwc

    return pl.pallas_call(
        body, name=name, grid=(T // tt,),
        in_specs=[pl.BlockSpec((tt, D), lambda i: (i, 0)), pl.BlockSpec((tt, D), lambda i: (i, 0)),
                  pl.BlockSpec((1, D), lambda i: (0, 0))],
        out_specs=[pl.BlockSpec((1, 128), lambda i: (0, 0)), pl.BlockSpec((tt, D), lambda i: (i, 0)),
                   pl.BlockSpec((1, D), lambda i: (0, 0))],
        out_shape=[jax.ShapeDtypeStruct((1, 128), F32), jax.ShapeDtypeStruct((T, D), F32),
                   jax.ShapeDtypeStruct((1, D), F32)],
        compiler_params=_cparams(("arbitrary",)))(h, tgt, w.reshape(1, D))


def _adamw(parts, w, m, v, *, name, tr=256):
    P, R, C = parts.shape
    tr = min(tr, R)
    assert R % tr == 0, (name, R, tr)
    c1 = 1.0 - ADAM_B1 ** ADAM_STEP
    c2 = 1.0 - ADAM_B2 ** ADAM_STEP

    def body(p_ref, w_ref, m_ref, v_ref, g_ref, d_ref, nm_ref, nv_ref):
        g = p_ref[0].astype(F32)
        for k in range(1, P):
            g = g + p_ref[k].astype(F32)
        mn = ADAM_B1 * m_ref[...] + (1.0 - ADAM_B1) * g
        vn = ADAM_B2 * v_ref[...] + (1.0 - ADAM_B2) * (g * g)
        g_ref[...] = g
        nm_ref[...] = mn
        nv_ref[...] = vn
        d_ref[...] = -ADAM_LR * ((mn / c1) / (jnp.sqrt(vn / c2) + ADAM_EPS) + ADAM_WD * w_ref[...])

    spec = pl.BlockSpec((tr, C), lambda i: (i, 0))
    sds = jax.ShapeDtypeStruct((R, C), F32)
    return pl.pallas_call(
        body, name=name, grid=(R // tr,),
        in_specs=[pl.BlockSpec((P, tr, C), lambda i: (0, i, 0)), spec, spec, spec],
        out_specs=[spec, spec, spec, spec], out_shape=[sds, sds, sds, sds],
        compiler_params=_cparams(("parallel",)))(parts, w, m, v)


def _all_gather(shards, *, name):
    n = len(shards)

    def body(*refs):
        ins, outs = refs[:n], refs[n:2 * n]
        send_sems, recv_sems, local_sems = refs[2 * n:]
        x, y, c = lax.axis_index("x"), lax.axis_index("y"), lax.axis_index("c")
        me, sib = (x, y, c), (x, y, 1 - c)
        chips = [(1 - x, y), (x, 1 - y), (1 - x, 1 - y)]

        def slot(p):
            return 4 * p[0] + 2 * p[1] + p[2]

        def cp(a, k, block, to, src=None):
            dst = outs[a].at[slot(block)]
            return pltpu.make_async_remote_copy(src_ref=dst if src is None else src, dst_ref=dst,
                                                send_sem=send_sems.at[a, k], recv_sem=recv_sems.at[a, k],
                                                device_id=to, device_id_type=_MESH)

        mine = [pltpu.make_async_copy(ins[a], outs[a].at[slot(me)], local_sems.at[a]) for a in range(n)]
        for m in mine:
            m.start()
        first = []
        for a in range(n):
            first.append(cp(a, 0, me, sib, src=ins[a]))
            for j, chip in enumerate(chips):
                first.append(cp(a, 1 + j, me, (*chip, c), src=ins[a]))
        for f in first:
            f.start()
        passed = []
        for j, chip in enumerate(chips):
            for a in range(n):
                cp(a, 1 + j, (*chip, c), me).wait_recv()
                f = cp(a, 4 + j, (*chip, c), sib)
                f.start()
                passed.append(f)
        for a in range(n):
            cp(a, 0, sib, me).wait_recv()
            for j, chip in enumerate(chips):
                cp(a, 4 + j, (*chip, 1 - c), me).wait_recv()
        for f in first + passed:
            f.wait_send()
        for m in mine:
            m.wait()

    return pl.pallas_call(
        body, name=name, in_specs=[_ANY] * n, out_specs=[_ANY] * n,
        out_shape=[jax.ShapeDtypeStruct((N_DEV,) + s.shape, s.dtype) for s in shards],
        scratch_shapes=[pltpu.SemaphoreType.DMA((n, 7)), pltpu.SemaphoreType.DMA((n, 7)),
                        pltpu.SemaphoreType.DMA((n,))])(*shards)


def _exchange(blocks, *, name):
    n = len(blocks)

    def body(*refs):
        ins, outs = refs[:n], refs[n:2 * n]
        send_sems, recv_sems, local_sems = refs[2 * n:]
        x, y, c = lax.axis_index("x"), lax.axis_index("y"), lax.axis_index("c")
        me = 4 * x + 2 * y + c
        mine = [pltpu.make_async_copy(ins[a].at[me], outs[a].at[me], local_sems.at[a]) for a in range(n)]
        for m in mine:
            m.start()
        copies = []
        for r in range(1, N_DEV):
            rx, ry, rc = (r >> 2) & 1, (r >> 1) & 1, r & 1
            px, py, pc = (1 - x if rx else x), (1 - y if ry else y), (1 - c if rc else c)
            peer = 4 * px + 2 * py + pc
            for a in range(n):
                copies.append((pltpu.make_async_remote_copy(
                    src_ref=ins[a].at[peer], dst_ref=outs[a].at[me], send_sem=send_sems.at[a, r - 1],
                    recv_sem=recv_sems.at[a, r - 1], device_id=(px, py, pc), device_id_type=_MESH),
                    pltpu.make_async_remote_copy(
                    src_ref=ins[a].at[peer], dst_ref=outs[a].at[peer], send_sem=send_sems.at[a, r - 1],
                    recv_sem=recv_sems.at[a, r - 1], device_id=(px, py, pc), device_id_type=_MESH)))
        for snd, _ in copies:
            snd.start()
        for _, rcv in copies:
            rcv.wait_recv()
        for snd, _ in copies:
            snd.wait_send()
        for m in mine:
            m.wait()

    return pl.pallas_call(
        body, name=name, in_specs=[_ANY] * n, out_specs=[_ANY] * n,
        out_shape=[jax.ShapeDtypeStruct(b.shape, b.dtype) for b in blocks],
        scratch_shapes=[pltpu.SemaphoreType.DMA((n, 7)), pltpu.SemaphoreType.DMA((n, 7)),
                        pltpu.SemaphoreType.DMA((n,))])(*blocks)


_HBM = pl.BlockSpec(memory_space=pltpu.HBM)
_SEM = pl.BlockSpec(memory_space=pltpu.SEMAPHORE)
_EFFECT = pltpu.SideEffectType.DATAFLOW_SIDE_EFFECTING


def _peers():
    x, y, c = lax.axis_index("x"), lax.axis_index("y"), lax.axis_index("c")
    out = []
    for r in range(1, N_DEV):
        px = 1 - x if (r >> 2) & 1 else x
        py = 1 - y if (r >> 1) & 1 else y
        pc = 1 - c if r & 1 else c
        out.append(((px, py, pc), 4 * px + 2 * py + pc))
    return 4 * x + 2 * y + c, out


def _push_copy(src_ref, land_ref, send_sems, recv_sems, a, k, me, peer, peer_slot, scatter, arriving):
    src = src_ref.at[peer_slot] if scatter else src_ref
    return pltpu.make_async_remote_copy(
        src_ref=src, dst_ref=land_ref.at[peer_slot if arriving else me], send_sem=send_sems.at[a * (N_DEV - 1) + k],
        recv_sem=recv_sems.at[a * (N_DEV - 1) + k], device_id=peer, device_id_type=_MESH)


def _push_start(srcs, *, scatter, name):
    n = len(srcs)
    lands = [lax.empty(s.shape if scatter else (N_DEV,) + s.shape, s.dtype) for s in srcs]

    def body(*refs):
        src_refs, land_refs = refs[:n], refs[n:2 * n]
        send_sems, recv_sems = refs[2 * n], refs[2 * n + 1]
        token = refs[-1]
        me, peers = _peers()
        for k, (peer, slot) in enumerate(peers):
            for a in range(n):
                _push_copy(src_refs[a], land_refs[a], send_sems, recv_sems, a, k, me, peer, slot, scatter, False).start()
        token[...] = jnp.zeros_like(token)

    hbm = lambda a: pltpu.HBM(a.shape, a.dtype)
    outs = pl.pallas_call(
        body, name=name,
        out_shape=(pltpu.SemaphoreType.DMA((n * (N_DEV - 1),)), pltpu.SemaphoreType.DMA((n * (N_DEV - 1),)),
                   *[hbm(s) for s in srcs], *[hbm(l) for l in lands], jax.ShapeDtypeStruct((8, 128), F32)),
        in_specs=[_HBM] * (2 * n),
        out_specs=(_SEM, _SEM, *([_HBM] * (2 * n)), pl.BlockSpec(memory_space=pltpu.VMEM)),
        input_output_aliases={i: 2 + i for i in range(2 * n)},
        compiler_params=pltpu.CompilerParams(has_side_effects=_EFFECT),
    )(*[pltpu.with_memory_space_constraint(s, pltpu.HBM) for s in srcs],
      *[pltpu.with_memory_space_constraint(l, pltpu.HBM) for l in lands])
    return dict(send=outs[0], recv=outs[1], srcs=list(outs[2:2 + n]), lands=list(outs[2 + n:2 + 2 * n]),
                token=outs[-1], scatter=scatter, n=n)


def _push_wait(h, after, *, name):
    n, scatter = h["n"], h["scatter"]

    def body(*refs):
        src_refs, land_refs = refs[:n], refs[n:2 * n]
        send_sems, recv_sems = refs[2 * n], refs[2 * n + 1]
        me, peers = _peers()
        for k, (peer, slot) in enumerate(peers):
            for a in range(n):
                cp = _push_copy(src_refs[a], land_refs[a], send_sems, recv_sems, a, k, me, peer, slot, scatter, True)
                cp.wait_send()
                cp.wait_recv()

    hbm = lambda a: pltpu.HBM(a.shape, a.dtype)
    outs = pl.pallas_call(
        body, name=name,
        out_shape=(*[hbm(s) for s in h["srcs"]], *[hbm(l) for l in h["lands"]]),
        in_specs=[_HBM] * (2 * n) + [_SEM, _SEM, _ANY], out_specs=tuple([_HBM] * (2 * n)),
        input_output_aliases={i: i for i in range(2 * n)},
        compiler_params=pltpu.CompilerParams(has_side_effects=_EFFECT),
    )(*h["srcs"], *h["lands"], h["send"], h["recv"], after)
    return list(outs[:n]), list(outs[n:])


def _ffn_fwd(h, nw, w_up, conv_w, conv_b, w_down, tag):
    a = _mm_fwd(h, w_up, norm_w=nw, name=f"ffn{tag}_up", tm=1024, tn=1408)
    p = _ffn_conv_fwd(a, conv_w, conv_b.reshape(1, -1), name=f"ffn{tag}_conv")
    h_out = _mm_fwd(p, w_down, residual=h, name=f"ffn{tag}_down", tm=1024, tn=512)
    return h_out, (a, p)


def _ffn_bwd(dh, h, saved, nw, w_up, conv_w, conv_b, w_down, tag):
    a, p = saved
    g_down = _mm_tn(p, dh, name=f"ffn{tag}_down_wg", tk1=1408, tn=1024)
    dp = _mm_nt(dh, w_down, name=f"ffn{tag}_down_dg", out_dtype=BF16, tm=1024, tn=1408, tk=1024)
    dhid, g_cw, g_cb = _ffn_conv_bwd_pre(a, conv_w, conv_b.reshape(1, -1), dp, name=f"ffn{tag}_conv_bwd")
    da = _conv_bwd_in(dhid, conv_w, K=FFN_CONV, name=f"ffn{tag}_conv_bwd_in", tt=256, tc=1408)
    g_up = _mm_tn(h, da, norm_w=nw, name=f"ffn{tag}_up_wg", tn=1408)
    dh_out, g_nw = _mm_nt(da, w_up, epi=(h, nw, dh), name=f"ffn{tag}_up_dg", tk=1408)
    return dh_out, dict(norm=g_nw.reshape(-1), up=g_up, conv_w=g_cw, conv_b=g_cb.reshape(-1), down=g_down)


def _local_step(x, tgt, W):
    T = x.shape[0]
    f = {}
    zx = _mm_fwd(x, W["in_w"], norm_w=W["ssm_norm_w"], name="ssm_in", tm=1024, tn=896)
    xbc_c = _ssm_conv_fwd(zx, W["ssm_conv_w"], W["ssm_conv_b"].reshape(1, -1), name="ssm_conv")
    dt_raw = zx[:, D_INNER + CONV_DIM:IN_PROJ_DIM]
    dtg = jnp.pad(dt_raw.reshape(T, SSM_GROUPS, 8).transpose(1, 0, 2), ((0, 0), (0, 0), (0, 120)))
    par = jnp.stack([W["ssm_dt_bias"].reshape(SSM_GROUPS, 8), W["ssm_a_log"].reshape(SSM_GROUPS, 8),
                     W["ssm_d"].reshape(SSM_GROUPS, 8)], axis=1)
    par = jnp.pad(par, ((0, 0), (0, 5), (0, 120)))
    gnw = W["ssm_gate_norm_w"].reshape(1, D_INNER)
    y, yn, st = _ssd_fwd(xbc_c, zx, dtg, par, gnw, name="ssd_fwd")
    h1 = _mm_fwd(yn, W["ssm_out_w"], residual=x, name="ssm_out", tm=1024, tn=512)
    h2, ffn0 = _ffn_fwd(h1, W["ffn_norm_w"][0], W["ffn_up_w"][0], W["ffn_conv_w"][0], W["ffn_conv_b"][0],
                        W["ffn_down_w"][0], "0")
    q = _mm_fwd(h2, W["w_q"], norm_w=W["attn_norm_w"], out_dtype=BF16, name="attn_q", tm=1024, tn=1024)
    kv = _mm_fwd(h2, W["w_kv"], norm_w=W["kv_norm_w"], out_dtype=BF16, name="attn_kv", tm=1024, tn=1024)
    o, lt = _sba_fwd(q, kv, name="sba_fwd")
    h3 = _mm_fwd(o, W["w_o"], residual=h2, name="attn_o", tm=1024, tn=512)
    h4, ffn1 = _ffn_fwd(h3, W["ffn_norm_w"][1], W["ffn_up_w"][1], W["ffn_conv_w"][1], W["ffn_conv_b"][1],
                        W["ffn_down_w"][1], "1")
    loss, dh4, g_final = _loss_head(h4, tgt, W["final_norm_w"], name="loss_head")
    dh3, gf1 = _ffn_bwd(dh4, h3, ffn1, W["ffn_norm_w"][1], W["ffn_up_w"][1], W["ffn_conv_w"][1], W["ffn_conv_b"][1],
                        W["ffn_down_w"][1], "1")
    g_wo = _mm_tn(o, dh3, name="attn_o_wg", tn=1024)
    do = _mm_nt(dh3, W["w_o"], name="attn_o_dg", out_dtype=BF16, tn=1024, tk=1024)
    dq, dk, dv = _sba_bwd(q, kv, lt, do, name="sba_bwd")
    g_wq = _mm_tn(h2, dq, norm_w=W["attn_norm_w"], name="attn_q_wg", tn=1024)
    dh2a, g_attn_nw = _mm_nt(dq, W["w_q"], epi=(h2, W["attn_norm_w"], dh3), name="attn_q_dg", tk=1024)
    dkv = jnp.concatenate([dk, dv], axis=1)
    g_wkv = _mm_tn(h2, dkv, norm_w=W["kv_norm_w"], name="attn_kv_wg", tn=1024)
    dh2, g_kv_nw = _mm_nt(dkv, W["w_kv"], epi=(h2, W["kv_norm_w"], dh2a), name="attn_kv_dg", tk=1024)
    dh1, gf0 = _ffn_bwd(dh2, h1, ffn0, W["ffn_norm_w"][0], W["ffn_up_w"][0], W["ffn_conv_w"][0], W["ffn_conv_b"][0],
                        W["ffn_down_w"][0], "0")
    g_out = _mm_tn(yn, dh1, name="ssm_out_wg", tn=1024)
    dyn = _mm_nt(dh1, W["ssm_out_w"], name="ssm_out_dg", out_dtype=BF16, tn=1024, tk=1024)
    dxs, dB, dC, dz, ddt, g_gnw, dpar = _ssd_bwd(xbc_c, zx, dtg, par, gnw, y, st, dyn, name="ssd_bwd")
    dxbc_c = jnp.concatenate([dxs, dB, dC], axis=1)
    dhid, g_scw, g_scb = _ssm_conv_bwd_pre(zx, W["ssm_conv_w"], W["ssm_conv_b"].reshape(1, -1), dxbc_c,
                                           name="ssm_conv_bwd")
    dxbc = _conv_bwd_in(dhid, W["ssm_conv_w"], K=SSM_CONV, name="ssm_conv_bwd_in")
    ddt_t = ddt[:, :, :8].transpose(1, 0, 2).reshape(T, SSM_HEADS).astype(BF16)
    dzx = jnp.concatenate([dz, dxbc, jnp.pad(ddt_t, ((0, 0), (0, IN_PROJ_PAD - IN_PROJ_DIM)))], axis=1)
    g_in = _mm_tn(x, dzx, norm_w=W["ssm_norm_w"], name="ssm_in_wg", tn=896)
    dx, g_ssm_nw = _mm_nt(dzx, W["in_w"], epi=(x, W["ssm_norm_w"], dh1), name="ssm_in_dg", tk=1792)
    f["ssm_norm_w"] = g_ssm_nw.reshape(-1)
    f["ssm_in_w"] = g_in[:, :IN_PROJ_DIM]
    f["ssm_conv_w"] = g_scw
    f["ssm_conv_b"] = g_scb.reshape(-1)
    f["ssm_dt_bias"] = dpar[:, 0, :8].reshape(-1)
    f["ssm_a_log"] = dpar[:, 1, :8].reshape(-1)
    f["ssm_d"] = dpar[:, 2, :8].reshape(-1)
    f["ssm_gate_norm_w"] = g_gnw.reshape(-1)
    f["ssm_out_w"] = g_out
    f["kv_norm_w"] = g_kv_nw.reshape(-1)
    f["w_k"] = g_wkv[:, :D_MODEL]
    f["w_v"] = g_wkv[:, D_MODEL:]
    f["attn_norm_w"] = g_attn_nw.reshape(-1)
    f["w_q"] = g_wq
    f["w_o"] = g_wo
    f["ffn_norm_w"] = jnp.stack([gf0["norm"], gf1["norm"]])
    f["ffn_up_w"] = [gf0["up"], gf1["up"]]
    f["ffn_conv_w"] = jnp.stack([gf0["conv_w"], gf1["conv_w"]])
    f["ffn_conv_b"] = jnp.stack([gf0["conv_b"], gf1["conv_b"]])
    f["ffn_down_w"] = [gf0["down"], gf1["down"]]
    f["final_norm_w"] = g_final.reshape(-1)
    return loss, dx, f


_BIG = ["ssm_in_w", "ssm_out_w", "w_k", "w_v", "w_q", "w_o", "ffn_up_w", "ffn_down_w"]
_SMALL_SHARDED = ["ssm_norm_w", "ssm_conv_w", "ssm_conv_b", "ssm_gate_norm_w", "ffn_conv_w"]
_SMALL_REPL = ["ssm_dt_bias", "ssm_a_log", "ssm_d", "kv_norm_w", "attn_norm_w", "ffn_norm_w", "ffn_conv_b",
               "final_norm_w"]
_WEIGHTS = ["ssm_norm_w", "ssm_in_w", "ssm_conv_w", "ssm_conv_b", "ssm_dt_bias", "ssm_a_log", "ssm_d",
            "ssm_gate_norm_w", "ssm_out_w", "kv_norm_w", "w_k", "w_v", "attn_norm_w", "w_q", "w_o", "ffn_norm_w",
            "ffn_up_w", "ffn_conv_w", "ffn_conv_b", "ffn_down_w", "final_norm_w"]


def _as2d(a):
    return a.reshape(-1, a.shape[-1])


def _cols_to_full(g):
    return g.transpose(1, 0, 2).reshape(g.shape[1], N_DEV * g.shape[2])


def _full_to_cols(a):
    R = a.shape[0]
    return a.reshape(R, N_DEV, -1).transpose(1, 0, 2)


def _gather_weights(p):
    names = _BIG + _SMALL_SHARDED
    shards = [_as2d(p[n]).astype(BF16) for n in _BIG] + [_as2d(p[n]) for n in _SMALL_SHARDED]
    got = dict(zip(names, _all_gather(shards, name="gather_weights")))
    W = {n: p[n] for n in _SMALL_REPL}
    in_w = _cols_to_full(got["ssm_in_w"])
    W["in_w"] = jnp.pad(in_w, ((0, 0), (0, IN_PROJ_PAD - IN_PROJ_DIM)))
    W["ssm_out_w"] = got["ssm_out_w"].reshape(D_INNER, D_MODEL)
    W["w_kv"] = jnp.concatenate([got["w_k"].reshape(D_MODEL, D_MODEL), got["w_v"].reshape(D_MODEL, D_MODEL)], axis=1)
    W["w_q"] = got["w_q"].reshape(D_MODEL, D_MODEL)
    W["w_o"] = got["w_o"].reshape(D_MODEL, D_MODEL)
    up = got["ffn_up_w"]
    W["ffn_up_w"] = [_cols_to_full(up[:, l * D_MODEL:(l + 1) * D_MODEL]) for l in range(2)]
    dn = got["ffn_down_w"]
    rs = D_FF // N_DEV
    W["ffn_down_w"] = [dn[:, l * rs:(l + 1) * rs].reshape(D_FF, D_MODEL) for l in range(2)]
    W["ssm_norm_w"] = got["ssm_norm_w"].reshape(D_MODEL)
    W["ssm_conv_w"] = _cols_to_full(got["ssm_conv_w"])
    W["ssm_conv_b"] = got["ssm_conv_b"].reshape(CONV_DIM)
    W["ssm_gate_norm_w"] = got["ssm_gate_norm_w"].reshape(D_INNER)
    fcw = _cols_to_full(got["ffn_conv_w"])
    W["ffn_conv_w"] = fcw.reshape(2, FFN_CONV, 2 * D_FF)
    for n in ("ssm_dt_bias", "ssm_a_log", "ssm_d", "attn_norm_w"):
        W[n] = W[n].reshape(-1)
    return W


def _big_grad_blocks(f):
    rs = D_FF // N_DEV
    return {
        "ssm_in_w": _full_to_cols(f["ssm_in_w"]),
        "ssm_out_w": f["ssm_out_w"].reshape(N_DEV, D_INNER // N_DEV, D_MODEL),
        "w_k": f["w_k"].reshape(N_DEV, D_MODEL // N_DEV, D_MODEL),
        "w_v": f["w_v"].reshape(N_DEV, D_MODEL // N_DEV, D_MODEL),
        "w_q": f["w_q"].reshape(N_DEV, D_MODEL // N_DEV, D_MODEL),
        "w_o": f["w_o"].reshape(N_DEV, D_MODEL // N_DEV, D_MODEL),
        "ffn_up_w": jnp.concatenate([_full_to_cols(g) for g in f["ffn_up_w"]], axis=1),
        "ffn_down_w": jnp.concatenate([g.reshape(N_DEV, rs, D_MODEL) for g in f["ffn_down_w"]], axis=1),
    }


def _pack_small(vals):
    flat = jnp.concatenate([v.reshape(-1).astype(F32) for v in vals])
    n = flat.shape[0]
    rows = -(-n // 1024) * 8
    return jnp.pad(flat, (0, rows * 128 - n)).reshape(rows, 128)


def _unpack_small(packed, shapes):
    flat = packed.reshape(-1)
    out, off = [], 0
    for s in shapes:
        n = math.prod(s)
        out.append(flat[off:off + n].reshape(s))
        off += n
    return out


def _kernel_v1(x, ssm_norm_w, ssm_in_w, ssm_conv_w, ssm_conv_b, ssm_dt_bias, ssm_a_log, ssm_d, ssm_gate_norm_w, ssm_out_w, kv_norm_w, w_k, w_v, attn_norm_w, w_q, w_o, ffn_norm_w, ffn_up_w, ffn_conv_w, ffn_conv_b, ffn_down_w, final_norm_w, loss_target, m_ssm_norm_w, m_ssm_in_w, m_ssm_conv_w, m_ssm_conv_b, m_ssm_dt_bias, m_ssm_a_log, m_ssm_d, m_ssm_gate_norm_w, m_ssm_out_w, m_kv_norm_w, m_w_k, m_w_v, m_attn_norm_w, m_w_q, m_w_o, m_ffn_norm_w, m_ffn_up_w, m_ffn_conv_w, m_ffn_conv_b, m_ffn_down_w, m_final_norm_w, v_ssm_norm_w, v_ssm_in_w, v_ssm_conv_w, v_ssm_conv_b, v_ssm_dt_bias, v_ssm_a_log, v_ssm_d, v_ssm_gate_norm_w, v_ssm_out_w, v_kv_norm_w, v_w_k, v_w_v, v_attn_norm_w, v_w_q, v_w_o, v_ffn_norm_w, v_ffn_up_w, v_ffn_conv_w, v_ffn_conv_b, v_ffn_down_w, v_final_norm_w):
    env = dict(locals())
    p = {n: env[n] for n in _WEIGHTS}
    mom = {n: env["m_" + n] for n in _WEIGHTS}
    var = {n: env["v_" + n] for n in _WEIGHTS}
    T = x.shape[1]
    me = 4 * lax.axis_index("x") + 2 * lax.axis_index("y") + lax.axis_index("c")

    W = _gather_weights(p)
    loss_row, dx, f = _local_step(x.reshape(T, D_MODEL), loss_target.reshape(T, D_MODEL), W)
    loss = lax.psum(loss_row[0, 0], ("x", "y", "c"))

    big = _big_grad_blocks(f)
    small_names = _SMALL_REPL + _SMALL_SHARDED
    small_full = _pack_small([f[n] for n in small_names])
    small_bcast = jnp.broadcast_to(small_full[None], (N_DEV,) + small_full.shape)
    got = _exchange([big[n] for n in _BIG] + [small_bcast], name="exchange_grads")
    big_parts = dict(zip(_BIG, got[:-1]))

    zero = jnp.zeros_like(small_full)
    g_small_sum = _adamw(got[-1], zero, zero, zero, name="sum_small_grads", tr=small_full.shape[0])[0]
    full_shapes = [f[n].shape for n in small_names]
    g_small = dict(zip(small_names, _unpack_small(g_small_sum, full_shapes)))
    for n in _SMALL_SHARDED:
        width = p[n].shape[-1]
        g_small[n] = lax.dynamic_slice_in_dim(g_small[n], me * width, width, axis=g_small[n].ndim - 1)

    out_g, out_d, out_m, out_v = {}, {}, {}, {}
    for n in _BIG:
        w2, m2, v2 = _as2d(p[n]), _as2d(mom[n]), _as2d(var[n])
        tr = 352 if n == "ffn_down_w" else 256
        g, d, nm, nv = _adamw(big_parts[n], w2, m2, v2, name="adamw_" + n, tr=tr)
        out_g[n], out_d[n], out_m[n], out_v[n] = (t.reshape(p[n].shape) for t in (g, d, nm, nv))
    sw = _pack_small([p[n] for n in small_names])
    sm = _pack_small([mom[n] for n in small_names])
    sv = _pack_small([var[n] for n in small_names])
    sg = _pack_small([g_small[n] for n in small_names])
    _, d, nm, nv = _adamw(sg[None], sw, sm, sv, name="adamw_small", tr=sw.shape[0])
    shard_shapes = [p[n].shape for n in small_names]
    for n, dd, mm, vv in zip(small_names, _unpack_small(d, shard_shapes), _unpack_small(nm, shard_shapes),
                             _unpack_small(nv, shard_shapes)):
        out_g[n] = g_small[n].reshape(p[n].shape)
        out_d[n], out_m[n], out_v[n] = dd, mm, vv

    return (loss, dx.reshape(x.shape), *[out_g[n] for n in _WEIGHTS], *[out_d[n] for n in _WEIGHTS],
            *[out_m[n] for n in _WEIGHTS], *[out_v[n] for n in _WEIGHTS])


def _tie(a, token):
    return a + token[0, 0].astype(a.dtype)


def _local_step2(x, tgt, get_w, put_g):
    T = x.shape[0]
    Ws = get_w("ssm", None)
    fnw, fcw, fcb = Ws["ffn_norm_w"], Ws["ffn_conv_w"], Ws["ffn_conv_b"]
    zx = _mm_fwd(x, Ws["in_w"], norm_w=Ws["ssm_norm_w"], name="ssm_in", tm=1024, tn=896)
    xbc_c = _ssm_conv_fwd(zx, Ws["ssm_conv_w"], Ws["ssm_conv_b"].reshape(1, -1), name="ssm_conv")
    dt_raw = zx[:, D_INNER + CONV_DIM:IN_PROJ_DIM]
    dtg = jnp.pad(dt_raw.reshape(T, SSM_GROUPS, 8).transpose(1, 0, 2), ((0, 0), (0, 0), (0, 120)))
    par = jnp.stack([Ws["ssm_dt_bias"].reshape(SSM_GROUPS, 8), Ws["ssm_a_log"].reshape(SSM_GROUPS, 8),
                     Ws["ssm_d"].reshape(SSM_GROUPS, 8)], axis=1)
    par = jnp.pad(par, ((0, 0), (0, 5), (0, 120)))
    gnw = _tie(Ws["ssm_gate_norm_w"].reshape(1, D_INNER), get_w("rest_start", xbc_c))
    y, yn, st = _ssd_fwd(xbc_c, zx, dtg, par, gnw, name="ssd_fwd")
    W0 = get_w("ffn0", y)
    Ws["ssm_out_w"] = W0["ssm_out_w"]
    h1 = _mm_fwd(yn, Ws["ssm_out_w"], residual=x, name="ssm_out", tm=1024, tn=512)
    h2, ffn0 = _ffn_fwd(h1, fnw[0], W0["up"], fcw[0], fcb[0], W0["down"], "0")
    Wr = get_w("rest", h2)
    q = _mm_fwd(h2, Wr["w_q"], norm_w=Ws["attn_norm_w"], out_dtype=BF16, name="attn_q", tm=1024, tn=1024)
    kv = _mm_fwd(h2, Wr["w_kv"], norm_w=Ws["kv_norm_w"], out_dtype=BF16, name="attn_kv", tm=1024, tn=1024)
    o, lt = _sba_fwd(q, kv, name="sba_fwd")
    h3 = _mm_fwd(o, Wr["w_o"], residual=h2, name="attn_o", tm=1024, tn=512)
    h4, ffn1 = _ffn_fwd(h3, fnw[1], Wr["up"], fcw[1], fcb[1], Wr["down"], "1")
    loss, dh4, g_final = _loss_head(h4, tgt, Ws["final_norm_w"], name="loss_head")
    dh3, gf1 = _ffn_bwd(dh4, h3, ffn1, fnw[1], Wr["up"], fcw[1], fcb[1], Wr["down"], "1")
    tok = put_g("ffn1", dict(up=gf1["up"], down=gf1["down"]))
    g_wo = _mm_tn(o, dh3, name="attn_o_wg", tn=1024)
    do = _mm_nt(dh3, _tie(Wr["w_o"], tok), name="attn_o_dg", out_dtype=BF16, tn=1024, tk=1024)
    dq, dk, dv = _sba_bwd(q, kv, lt, do, name="sba_bwd")
    g_wq = _mm_tn(h2, dq, norm_w=Ws["attn_norm_w"], name="attn_q_wg", tn=1024)
    dh2a, g_attn_nw = _mm_nt(dq, Wr["w_q"], epi=(h2, Ws["attn_norm_w"], dh3), name="attn_q_dg", tk=1024)
    dkv = jnp.concatenate([dk, dv], axis=1)
    g_wkv = _mm_tn(h2, dkv, norm_w=Ws["kv_norm_w"], name="attn_kv_wg", tn=1024)
    dh2, g_kv_nw = _mm_nt(dkv, Wr["w_kv"], epi=(h2, Ws["kv_norm_w"], dh2a), name="attn_kv_dg", tk=1024)
    tok = put_g("attn", dict(w_o=g_wo, w_q=g_wq, w_k=g_wkv[:, :D_MODEL], w_v=g_wkv[:, D_MODEL:]))
    dh1, gf0 = _ffn_bwd(dh2, h1, ffn0, fnw[0], W0["up"], fcw[0], _tie(fcb[0], tok), W0["down"], "0")
    tok = put_g("ffn0", dict(up=gf0["up"], down=gf0["down"]))
    g_out = _mm_tn(yn, dh1, name="ssm_out_wg", tn=1024)
    dyn = _mm_nt(dh1, _tie(Ws["ssm_out_w"], tok), name="ssm_out_dg", out_dtype=BF16, tn=1024, tk=1024)
    tok = put_g("ssm_out", dict(ssm_out_w=g_out))
    dxs, dB, dC, dz, ddt, g_gnw, dpar = _ssd_bwd(xbc_c, zx, dtg, par, _tie(gnw, tok), y, st, dyn, name="ssd_bwd")
    dxbc_c = jnp.concatenate([dxs, dB, dC], axis=1)
    dhid, g_scw, g_scb = _ssm_conv_bwd_pre(zx, Ws["ssm_conv_w"], Ws["ssm_conv_b"].reshape(1, -1), dxbc_c,
                                           name="ssm_conv_bwd")
    dxbc = _conv_bwd_in(dhid, Ws["ssm_conv_w"], K=SSM_CONV, name="ssm_conv_bwd_in")
    ddt_t = ddt[:, :, :8].transpose(1, 0, 2).reshape(T, SSM_HEADS).astype(BF16)
    dzx = jnp.concatenate([dz, dxbc, jnp.pad(ddt_t, ((0, 0), (0, IN_PROJ_PAD - IN_PROJ_DIM)))], axis=1)
    g_in = _mm_tn(x, dzx, norm_w=Ws["ssm_norm_w"], name="ssm_in_wg", tn=896)
    tok = put_g("ssm_in", dict(ssm_in_w=g_in[:, :IN_PROJ_DIM]))
    dx, g_ssm_nw = _mm_nt(dzx, Ws["in_w"], epi=(x, _tie(Ws["ssm_norm_w"], tok), dh1), name="ssm_in_dg", tk=1792)
    f = {
        "ssm_norm_w": g_ssm_nw.reshape(-1), "ssm_conv_w": g_scw,
        "ssm_conv_b": g_scb.reshape(-1), "ssm_dt_bias": dpar[:, 0, :8].reshape(-1),
        "ssm_a_log": dpar[:, 1, :8].reshape(-1), "ssm_d": dpar[:, 2, :8].reshape(-1),
        "ssm_gate_norm_w": g_gnw.reshape(-1), "kv_norm_w": g_kv_nw.reshape(-1), "attn_norm_w": g_attn_nw.reshape(-1),
        "ffn_norm_w": jnp.stack([gf0["norm"], gf1["norm"]]), "ffn_conv_w": jnp.stack([gf0["conv_w"], gf1["conv_w"]]),
        "ffn_conv_b": jnp.stack([gf0["conv_b"], gf1["conv_b"]]), "final_norm_w": g_final.reshape(-1),
    }
    return loss, dx, f


def kernel(x, ssm_norm_w, ssm_in_w, ssm_conv_w, ssm_conv_b, ssm_dt_bias, ssm_a_log, ssm_d, ssm_gate_norm_w, ssm_out_w, kv_norm_w, w_k, w_v, attn_norm_w, w_q, w_o, ffn_norm_w, ffn_up_w, ffn_conv_w, ffn_conv_b, ffn_down_w, final_norm_w, loss_target, m_ssm_norm_w, m_ssm_in_w, m_ssm_conv_w, m_ssm_conv_b, m_ssm_dt_bias, m_ssm_a_log, m_ssm_d, m_ssm_gate_norm_w, m_ssm_out_w, m_kv_norm_w, m_w_k, m_w_v, m_attn_norm_w, m_w_q, m_w_o, m_ffn_norm_w, m_ffn_up_w, m_ffn_conv_w, m_ffn_conv_b, m_ffn_down_w, m_final_norm_w, v_ssm_norm_w, v_ssm_in_w, v_ssm_conv_w, v_ssm_conv_b, v_ssm_dt_bias, v_ssm_a_log, v_ssm_d, v_ssm_gate_norm_w, v_ssm_out_w, v_kv_norm_w, v_w_k, v_w_v, v_attn_norm_w, v_w_q, v_w_o, v_ffn_norm_w, v_ffn_up_w, v_ffn_conv_w, v_ffn_conv_b, v_ffn_down_w, v_final_norm_w):
    env = dict(locals())
    p = {n: env[n] for n in _WEIGHTS}
    mom = {n: env["m_" + n] for n in _WEIGHTS}
    var = {n: env["v_" + n] for n in _WEIGHTS}
    T = x.shape[1]
    me = 4 * lax.axis_index("x") + 2 * lax.axis_index("y") + lax.axis_index("c")
    rs = D_FF // N_DEV

    def bf2(a):
        return _as2d(a).astype(BF16)

    def with_own(srcs, lands, scatter):
        out = []
        for s, l in zip(srcs, lands):
            own = lax.dynamic_index_in_dim(s, me, 0, keepdims=False) if scatter else s
            out.append(lax.dynamic_update_index_in_dim(l, own, me, 0))
        return out

    a_names = ["ssm_in_w"] + _SMALL_SHARDED
    got_a = dict(zip(a_names, _all_gather([bf2(p["ssm_in_w"])] + [_as2d(p[n]) for n in _SMALL_SHARDED],
                                          name="gather_ssm")))
    ffn0_names = ["ssm_out_w", "up0", "down0"]
    rest_names = ["w_q", "w_k", "w_v", "w_o", "up1", "down1"]
    shard = {"up0": bf2(p["ffn_up_w"][0]), "down0": bf2(p["ffn_down_w"][0]), "up1": bf2(p["ffn_up_w"][1]),
             "down1": bf2(p["ffn_down_w"][1]), "w_q": bf2(p["w_q"]), "w_k": bf2(p["w_k"]), "w_v": bf2(p["w_v"]),
             "w_o": bf2(p["w_o"]), "ssm_out_w": bf2(p["ssm_out_w"])}
    h_ffn0 = _push_start([shard[n] for n in ffn0_names], scatter=False, name="gather_ffn0_start")
    handles = {}

    def get_w(group, after):
        if group == "ssm":
            W = {n: p[n] for n in _SMALL_REPL}
            for n in ("ssm_dt_bias", "ssm_a_log", "ssm_d", "attn_norm_w"):
                W[n] = W[n].reshape(-1)
            W["in_w"] = jnp.pad(_cols_to_full(got_a["ssm_in_w"]), ((0, 0), (0, IN_PROJ_PAD - IN_PROJ_DIM)))
            W["ssm_norm_w"] = _tie(got_a["ssm_norm_w"].reshape(D_MODEL), h_ffn0["token"])
            W["ssm_conv_w"] = _cols_to_full(got_a["ssm_conv_w"])
            W["ssm_conv_b"] = got_a["ssm_conv_b"].reshape(CONV_DIM)
            W["ssm_gate_norm_w"] = got_a["ssm_gate_norm_w"].reshape(D_INNER)
            W["ffn_conv_w"] = _cols_to_full(got_a["ffn_conv_w"]).reshape(2, FFN_CONV, 2 * D_FF)
            return W
        if group == "rest_start":
            anchor = after[0, 0]
            first = shard[rest_names[0]] + (jnp.where(jnp.isfinite(anchor), anchor, 0.0) * 0.0).astype(BF16)
            handles["rest"] = _push_start([first] + [shard[n] for n in rest_names[1:]], scatter=False,
                                          name="gather_rest_start")
            return handles["rest"]["token"]
        if group == "ffn0":
            srcs, lands = _push_wait(h_ffn0, after, name="gather_ffn0_wait")
            out, up, down = with_own(srcs, lands, False)
            return dict(ssm_out_w=out.reshape(D_INNER, D_MODEL), up=_cols_to_full(up), down=down.reshape(D_FF, D_MODEL))
        srcs, lands = _push_wait(handles["rest"], after, name="gather_rest_wait")
        g = dict(zip(rest_names, with_own(srcs, lands, False)))
        sq = lambda a: a.reshape(D_MODEL, D_MODEL)
        return dict(w_q=sq(g["w_q"]), w_kv=jnp.concatenate([sq(g["w_k"]), sq(g["w_v"])], axis=1), w_o=sq(g["w_o"]),
                    up=_cols_to_full(g["up1"]), down=g["down1"].reshape(D_FF, D_MODEL))

    pending = []

    def put_g(group, g):
        if group in ("ffn0", "ffn1"):
            keys = [("ffn_up_w", int(group[-1])), ("ffn_down_w", int(group[-1]))]
            blocks = [_full_to_cols(g["up"]), g["down"].reshape(N_DEV, rs, D_MODEL)]
        elif group == "attn":
            keys = [(n, None) for n in ("w_o", "w_q", "w_k", "w_v")]
            blocks = [g[n].reshape(N_DEV, D_MODEL // N_DEV, D_MODEL) for n, _ in keys]
        elif group == "ssm_out":
            keys = [("ssm_out_w", None)]
            blocks = [g["ssm_out_w"].reshape(N_DEV, D_INNER // N_DEV, D_MODEL)]
        else:
            keys = [("ssm_in_w", None)]
            blocks = [_full_to_cols(g["ssm_in_w"])]
        h = _push_start(blocks, scatter=True, name=f"exchange_{group}_start")
        pending.append((group, keys, h))
        return h["token"]

    loss_row, dx, f = _local_step2(x.reshape(T, D_MODEL), loss_target.reshape(T, D_MODEL), get_w, put_g)
    loss = lax.psum(loss_row[0, 0], ("x", "y", "c"))

    small_names = _SMALL_REPL + _SMALL_SHARDED
    small_full = _pack_small([f[n] for n in small_names])
    small_bcast = jnp.broadcast_to(small_full[None], (N_DEV,) + small_full.shape)
    h_small = _push_start([small_bcast], scatter=True, name="exchange_small_start")
    tok = h_small["token"]

    res = {}
    for group, keys, h in pending:
        srcs, lands = _push_wait(h, dx, name=f"exchange_{group}_wait")
        for (n, layer), parts in zip(keys, with_own(srcs, lands, True)):
            sel = (lambda a: a) if layer is None else (lambda a: a[layer])
            w2, m2, v2 = _as2d(sel(p[n])), _as2d(sel(mom[n])), _as2d(sel(var[n]))
            if not res:
                w2 = _tie(w2, tok)
            tr = rs if n == "ffn_down_w" else 256
            res[(n, layer)] = _adamw(parts, w2, m2, v2, name=f"adamw_{n}" + ("" if layer is None else str(layer)), tr=tr)
    srcs, lands = _push_wait(h_small, res[("ssm_in_w", None)][0], name="exchange_small_wait")
    small_parts = with_own(srcs, lands, True)[0]
    out_g, out_d, out_m, out_v = {}, {}, {}, {}
    for n in _BIG:
        if (n, None) in res:
            quad = res[(n, None)]
        else:
            quad = [jnp.stack([res[(n, 0)][k], res[(n, 1)][k]]) for k in range(4)]
        out_g[n], out_d[n], out_m[n], out_v[n] = (t.reshape(p[n].shape) for t in quad)

    zero = jnp.zeros_like(small_full)
    g_small_sum = _adamw(small_parts, zero, zero, zero, name="sum_small_grads", tr=small_full.shape[0])[0]
    g_small = dict(zip(small_names, _unpack_small(g_small_sum, [f[n].shape for n in small_names])))
    for n in _SMALL_SHARDED:
        width = p[n].shape[-1]
        g_small[n] = lax.dynamic_slice_in_dim(g_small[n], me * width, width, axis=g_small[n].ndim - 1)
    sw = _pack_small([p[n] for n in small_names])
    sm = _pack_small([mom[n] for n in small_names])
    sv = _pack_small([var[n] for n in small_names])
    sg = _pack_small([g_small[n] for n in small_names])
    _, d, nm, nv = _adamw(sg[None], sw, sm, sv, name="adamw_small", tr=sw.shape[0])
    shard_shapes = [p[n].shape for n in small_names]
    for n, dd, mm, vv in zip(small_names, _unpack_small(d, shard_shapes), _unpack_small(nm, shard_shapes),
                             _unpack_small(nv, shard_shapes)):
        out_g[n] = g_small[n].reshape(p[n].shape)
        out_d[n], out_m[n], out_v[n] = dd, mm, vv

    return (loss, dx.reshape(x.shape), *[out_g[n] for n in _WEIGHTS], *[out_d[n] for n in _WEIGHTS],
            *[out_m[n] for n in _WEIGHTS], *[out_v[n] for n in _WEIGHTS])
```

```python
import functools
import math

import jax
import jax.numpy as jnp
from jax import lax
from jax.experimental import pallas as pl
from jax.experimental.pallas import tpu as pltpu

F32 = jnp.float32
BF16 = jnp.bfloat16
EPS = 1e-6

D_MODEL = 1024
D_INNER = 2048
SSM_HEADS = 32
SSM_GROUPS = 4
SSM_STATE = 128
SSM_CONV = 4
SSM_CHUNK = 128
GN = SSM_GROUPS * SSM_STATE
CONV_DIM = D_INNER + 2 * GN
IN_PROJ_DIM = D_INNER + CONV_DIM + SSM_HEADS
IN_PROJ_PAD = 5376
SB_HEADS = 16
SB_HEAD_DIM = 64
SB_BLOCK = 128
D_FF = 2816
FFN_CONV = 3
N_DEV = 8

ADAM_LR = 0.001
ADAM_B1 = 0.9
ADAM_B2 = 0.999
ADAM_EPS = 1e-08
ADAM_WD = 0.01
ADAM_STEP = 10

_MESH = pl.DeviceIdType.MESH
_NT = (((1,), (1,)), ((), ()))
_TN = (((0,), (0,)), ((), ()))
_ANY = pl.BlockSpec(memory_space=pl.ANY)


def _cparams(sem, vmem_mb=48):
    return pltpu.CompilerParams(dimension_semantics=sem, vmem_limit_bytes=vmem_mb * 1024 * 1024)


def _sigmoid(x):
    return 1.0 / (1.0 + jnp.exp(-x))


def _softplus(x):
    return jnp.maximum(x, 0.0) + jnp.log(1.0 + jnp.exp(-jnp.abs(x)))


def _rms_fwd(xv, w):
    r = lax.rsqrt(jnp.mean(xv * xv, axis=-1, keepdims=True) + EPS)
    return xv * r * w


def _mm_fwd(x, w, *, name, norm_w=None, residual=None, out_dtype=F32, tm=512, tn=512, halves=False):
    M, K = x.shape
    N = w.shape[1]
    tm, tn = min(tm, M), min(tn, N)
    assert M % tm == 0 and N % tn == 0, (name, M, N, tm, tn)
    if halves:
        nbh = N // 2 // tn
        assert N // 2 % tn == 0
        out_spec = pl.BlockSpec((None, tm, tn), lambda i, j: (lax.div(j, nbh), i, lax.rem(j, nbh)))
        out_shape = jax.ShapeDtypeStruct((2, M, N // 2), out_dtype)
    else:
        out_spec = pl.BlockSpec((tm, tn), lambda i, j: (i, j))
        out_shape = jax.ShapeDtypeStruct((M, N), out_dtype)
    has_norm, has_res = norm_w is not None, residual is not None

    def body(*refs):
        x_ref, w_ref = refs[0], refs[1]
        p = 2
        nw_ref = r_ref = None
        if has_norm:
            nw_ref = refs[p]
            p += 1
        if has_res:
            r_ref = refs[p]
            p += 1
        o_ref, xn_ref = refs[p], refs[p + 1]

        @pl.when(pl.program_id(1) == 0)
        def _():
            xv = x_ref[...].astype(F32)
            if has_norm:
                xv = _rms_fwd(xv, nw_ref[...])
            xn_ref[...] = xv.astype(BF16)

        acc = jnp.dot(xn_ref[...], w_ref[...], preferred_element_type=F32)
        if has_res:
            acc = acc + r_ref[...]
        o_ref[...] = acc.astype(out_dtype)

    in_specs = [pl.BlockSpec((tm, K), lambda i, j: (i, 0)), pl.BlockSpec((K, tn), lambda i, j: (0, j))]
    args = [x, w]
    if has_norm:
        in_specs.append(pl.BlockSpec((1, K), lambda i, j: (0, 0)))
        args.append(norm_w.reshape(1, K))
    if has_res:
        in_specs.append(pl.BlockSpec((tm, tn), lambda i, j: (i, j)))
        args.append(residual)
    return pl.pallas_call(
        body, name=name, grid=(M // tm, N // tn), in_specs=in_specs,
        out_specs=out_spec, out_shape=out_shape,
        scratch_shapes=[pltpu.VMEM((tm, K), BF16)],
        compiler_params=_cparams(("parallel", "arbitrary")))(*args)


def _mm_nt(dy, w, *, name, epi=None, out_dtype=F32, tm=512, tn=512, tk=512):
    halves = dy.ndim == 3
    M, K = (dy.shape[1], 2 * dy.shape[2]) if halves else dy.shape
    N = w.shape[0]
    tm, tk = min(tm, M), min(tk, K)
    tn = N if epi is not None else min(tn, N)
    assert M % tm == 0 and N % tn == 0 and K % tk == 0, (name, M, N, K, tm, tn, tk)
    nk = K // tk
    has_epi = epi is not None

    def body(*refs):
        if has_epi:
            dy_ref, w_ref, h_ref, nw_ref, r_ref, o_ref, dnw_ref, acc_ref = refs
        else:
            dy_ref, w_ref, o_ref, acc_ref = refs
        i = pl.program_id(0)
        k = pl.program_id(2)

        @pl.when(k == 0)
        def _():
            acc_ref[...] = jnp.zeros_like(acc_ref)

        acc_ref[...] += lax.dot_general(dy_ref[...].astype(BF16), w_ref[...], _NT, preferred_element_type=F32)

        @pl.when(k == nk - 1)
        def _():
            du = acc_ref[...]
            if has_epi:
                hv = h_ref[...]
                r = lax.rsqrt(jnp.mean(hv * hv, axis=-1, keepdims=True) + EPS)
                xhat = hv * r
                dxh = du * nw_ref[...]
                dx = r * (dxh - xhat * jnp.mean(dxh * xhat, axis=-1, keepdims=True))
                o_ref[...] = (r_ref[...] + dx).astype(out_dtype)
                contrib = jnp.sum(du * xhat, axis=0, keepdims=True)

                @pl.when(i == 0)
                def _():
                    dnw_ref[...] = contrib

                @pl.when(i > 0)
                def _():
                    dnw_ref[...] += contrib
            else:
                o_ref[...] = du.astype(out_dtype)

    if halves:
        nkh = K // 2 // tk
        assert K // 2 % tk == 0
        dy_spec = pl.BlockSpec((None, tm, tk), lambda i, j, k: (lax.div(k, nkh), i, lax.rem(k, nkh)))
    else:
        dy_spec = pl.BlockSpec((tm, tk), lambda i, j, k: (i, k))
    in_specs = [dy_spec, pl.BlockSpec((tn, tk), lambda i, j, k: (j, k))]
    args = [dy, w]
    out_specs = [pl.BlockSpec((tm, tn), lambda i, j, k: (i, j))]
    out_shape = [jax.ShapeDtypeStruct((M, N), out_dtype)]
    if has_epi:
        h, nw, res = epi
        in_specs += [pl.BlockSpec((tm, N), lambda i, j, k: (i, 0)), pl.BlockSpec((1, N), lambda i, j, k: (0, 0)),
                     pl.BlockSpec((tm, N), lambda i, j, k: (i, 0))]
        args += [h, nw.reshape(1, N), res]
        out_specs.append(pl.BlockSpec((1, N), lambda i, j, k: (0, 0)))
        out_shape.append(jax.ShapeDtypeStruct((1, N), F32))
    outs = pl.pallas_call(
        body, name=name, grid=(M // tm, N // tn, nk), in_specs=in_specs, out_specs=out_specs, out_shape=out_shape,
        scratch_shapes=[pltpu.VMEM((tm, tn), F32)],
        compiler_params=_cparams(("arbitrary", "arbitrary", "arbitrary")))(*args)
    return (outs[0], outs[1]) if has_epi else outs[0]


def _mm_tn(x, dy, *, name, norm_w=None, out_dtype=BF16, tk1=1024, tn=512, tt=512):
    T, K1 = x.shape
    halves = dy.ndim == 3
    N = 2 * dy.shape[2] if halves else dy.shape[1]
    tk1, tn, tt = min(tk1, K1), min(tn, N), min(tt, T)
    has_norm = norm_w is not None
    assert K1 % tk1 == 0 and N % tn == 0 and T % tt == 0, (name, K1, N, T, tk1, tn, tt)
    assert not has_norm or tk1 == K1
    nt = T // tt

    def body(*refs):
        if has_norm:
            x_ref, dy_ref, nw_ref, o_ref, acc_ref = refs
        else:
            x_ref, dy_ref, o_ref, acc_ref = refs
        t = pl.program_id(2)

        @pl.when(t == 0)
        def _():
            acc_ref[...] = jnp.zeros_like(acc_ref)

        xv = x_ref[...]
        if has_norm:
            xv = _rms_fwd(xv.astype(F32), nw_ref[...])
        acc_ref[...] += lax.dot_general(xv.astype(BF16), dy_ref[...].astype(BF16), _TN, preferred_element_type=F32)

        @pl.when(t == nt - 1)
        def _():
            o_ref[...] = acc_ref[...].astype(out_dtype)

    if halves:
        nbh = N // 2 // tn
        assert N // 2 % tn == 0
        dy_spec = pl.BlockSpec((None, tt, tn), lambda a, b, t: (lax.div(b, nbh), t, lax.rem(b, nbh)))
    else:
        dy_spec = pl.BlockSpec((tt, tn), lambda a, b, t: (t, b))
    in_specs = [pl.BlockSpec((tt, tk1), lambda a, b, t: (t, a)), dy_spec]
    args = [x, dy]
    if has_norm:
        in_specs.append(pl.BlockSpec((1, K1), lambda a, b, t: (0, 0)))
        args.append(norm_w.reshape(1, K1))
    return pl.pallas_call(
        body, name=name, grid=(K1 // tk1, N // tn, nt), in_specs=in_specs,
        out_specs=pl.BlockSpec((tk1, tn), lambda a, b, t: (a, b)),
        out_shape=jax.ShapeDtypeStruct((K1, N), out_dtype),
        scratch_shapes=[pltpu.VMEM((tk1, tn), F32)],
        compiler_params=_cparams(("parallel", "parallel", "arbitrary")))(*args)


def _shift_down(xb, prev8, j):
    main = pltpu.roll(xb, j, 0)
    head = pltpu.roll(xb[0:8], j, 0)
    ph = pltpu.roll(prev8, j, 0)
    row8 = lax.broadcasted_iota(jnp.int32, head.shape, 0)
    head = jnp.where(row8 < j, ph, head)
    return jnp.concatenate([head, main[8:]], axis=0)


def _shift_up(xb, next8, j):
    tt = xb.shape[0]
    main = pltpu.roll(xb, tt - j, 0)
    tail = pltpu.roll(xb[tt - 8:tt], 8 - j, 0)
    nh = pltpu.roll(next8, 8 - j, 0)
    row8 = lax.broadcasted_iota(jnp.int32, tail.shape, 0)
    tail = jnp.where(row8 + j >= 8, nh, tail)
    return jnp.concatenate([main[:tt - 8], tail], axis=0)


def _conv_hid(xb, prev8, w, b_row, K):
    out = b_row
    shifted = []
    for j in range(K):
        sh = K - 1 - j
        xs = xb if sh == 0 else _shift_down(xb, prev8, sh)
        shifted.append(xs)
        out = out + xs * w[j:j + 1, :]
    return out, shifted


def _prev_idx(i, nb8):
    return jnp.maximum(i * nb8 - 1, 0)


def _ssm_conv_fwd(zx, w, b, *, name, tt=512, tc=512):
    T = zx.shape[0]
    tt = min(tt, T)
    C, K = CONV_DIM, SSM_CONV
    cb0, nb8 = D_INNER // tc, tt // 8

    def body(x_ref, p_ref, w_ref, b_ref, o_ref):
        first = (pl.program_id(1) > 0).astype(F32)
        hid, _ = _conv_hid(x_ref[...], p_ref[...] * first, w_ref[...], b_ref[...], K)
        o_ref[...] = hid * _sigmoid(hid)

    return pl.pallas_call(
        body, name=name, grid=(C // tc, T // tt),
        in_specs=[pl.BlockSpec((tt, tc), lambda c, i: (i, c + cb0)),
                  pl.BlockSpec((8, tc), lambda c, i: (_prev_idx(i, nb8), c + cb0)),
                  pl.BlockSpec((K, tc), lambda c, i: (0, c)), pl.BlockSpec((1, tc), lambda c, i: (0, c))],
        out_specs=pl.BlockSpec((tt, tc), lambda c, i: (i, c)),
        out_shape=jax.ShapeDtypeStruct((T, C), F32),
        compiler_params=_cparams(("parallel", "parallel")))(zx, zx, w, b)


def _ssm_conv_bwd_pre(zx, w, b, dout, *, name, tt=512, tc=512):
    T = zx.shape[0]
    tt = min(tt, T)
    C, K = CONV_DIM, SSM_CONV
    cb0, nb8 = D_INNER // tc, tt // 8

    def body(x_ref, p_ref, w_ref, b_ref, d_ref, dh_ref, dw_ref, db_ref):
        t = pl.program_id(1)
        first = (t > 0).astype(F32)
        hid, shifted = _conv_hid(x_ref[...], p_ref[...] * first, w_ref[...], b_ref[...], K)
        sg = _sigmoid(hid)
        dh = d_ref[...] * (sg * (1.0 + hid * (1.0 - sg)))
        dh_ref[...] = dh

        @pl.when(t == 0)
        def _():
            dw_ref[...] = jnp.zeros_like(dw_ref)
            db_ref[...] = jnp.zeros_like(db_ref)

        db_ref[...] += jnp.sum(dh, axis=0, keepdims=True)
        for j in range(K):
            dw_ref[j:j + 1, :] += jnp.sum(dh * shifted[j], axis=0, keepdims=True)

    return pl.pallas_call(
        body, name=name, grid=(C // tc, T // tt),
        in_specs=[pl.BlockSpec((tt, tc), lambda c, i: (i, c + cb0)),
                  pl.BlockSpec((8, tc), lambda c, i: (_prev_idx(i, nb8), c + cb0)),
                  pl.BlockSpec((K, tc), lambda c, i: (0, c)), pl.BlockSpec((1, tc), lambda c, i: (0, c)),
                  pl.BlockSpec((tt, tc), lambda c, i: (i, c))],
        out_specs=[pl.BlockSpec((tt, tc), lambda c, i: (i, c)), pl.BlockSpec((K, tc), lambda c, i: (0, c)),
                   pl.BlockSpec((1, tc), lambda c, i: (0, c))],
        out_shape=[jax.ShapeDtypeStruct((T, C), F32), jax.ShapeDtypeStruct((K, C), F32),
                   jax.ShapeDtypeStruct((1, C), F32)],
        compiler_params=_cparams(("parallel", "arbitrary")))(zx, zx, w, b, dout)


def _conv_bwd_in(dh, w, *, name, K, tt=512, tc=512, out_dtype=BF16):
    T, C = dh.shape
    tt = min(tt, T)
    nb8, nT = tt // 8, T // tt
    last8 = T // 8 - 1

    def body(d_ref, n_ref, w_ref, o_ref):
        notlast = (pl.program_id(1) < nT - 1).astype(F32)
        d = d_ref[...]
        nxt = n_ref[...] * notlast
        w_ = w_ref[...]
        acc = d * w_[K - 1:K, :]
        for sh in range(1, K):
            acc = acc + _shift_up(d, nxt, sh) * w_[K - 1 - sh:K - sh, :]
        o_ref[...] = acc.astype(out_dtype)

    return pl.pallas_call(
        body, name=name, grid=(C // tc, nT),
        in_specs=[pl.BlockSpec((tt, tc), lambda c, i: (i, c)),
                  pl.BlockSpec((8, tc), lambda c, i: (jnp.minimum((i + 1) * nb8, last8), c)),
                  pl.BlockSpec((K, tc), lambda c, i: (0, c))],
        out_specs=pl.BlockSpec((tt, tc), lambda c, i: (i, c)),
        out_shape=jax.ShapeDtypeStruct((T, C), out_dtype),
        compiler_params=_cparams(("parallel", "parallel")))(dh, dh, w)


def _ffn_conv_fwd(a, w, b, *, name, tt=256, tc=1408):
    T = a.shape[0]
    tt = min(tt, T)
    K, nbh, nb8 = FFN_CONV, D_FF // tc, tt // 8

    def body(ag_ref, pg_ref, av_ref, pv_ref, wg_ref, wv_ref, bg_ref, bv_ref, o_ref):
        first = (pl.program_id(1) > 0).astype(F32)
        hg, _ = _conv_hid(ag_ref[...], pg_ref[...] * first, wg_ref[...], bg_ref[...], K)
        hv, _ = _conv_hid(av_ref[...], pv_ref[...] * first, wv_ref[...], bv_ref[...], K)
        o_ref[...] = (hg * _sigmoid(hg) * hv).astype(BF16)

    return pl.pallas_call(
        body, name=name, grid=(nbh, T // tt),
        in_specs=[pl.BlockSpec((tt, tc), lambda c, i: (i, c)),
                  pl.BlockSpec((8, tc), lambda c, i: (_prev_idx(i, nb8), c)),
                  pl.BlockSpec((tt, tc), lambda c, i: (i, c + nbh)),
                  pl.BlockSpec((8, tc), lambda c, i: (_prev_idx(i, nb8), c + nbh)),
                  pl.BlockSpec((K, tc), lambda c, i: (0, c)), pl.BlockSpec((K, tc), lambda c, i: (0, c + nbh)),
                  pl.BlockSpec((1, tc), lambda c, i: (0, c)), pl.BlockSpec((1, tc), lambda c, i: (0, c + nbh))],
        out_specs=pl.BlockSpec((tt, tc), lambda c, i: (i, c)),
        out_shape=jax.ShapeDtypeStruct((T, D_FF), BF16),
        compiler_params=_cparams(("parallel", "parallel")))(a, a, a, a, w, w, b, b)


def _ffn_conv_bwd_pre(a, w, b, dp, *, name, tt=256, tc=1408):
    T = a.shape[0]
    tt = min(tt, T)
    K, nbh, nb8 = FFN_CONV, D_FF // tc, tt // 8

    def body(ao_ref, po_ref, ag_ref, pg_ref, av_ref, pv_ref, wg_ref, wv_ref, bg_ref, bv_ref, dp_ref,
             dh_ref, dw_ref, db_ref):
        j = pl.program_id(0)
        t = pl.program_id(1)
        first = (t > 0).astype(F32)
        hg, _ = _conv_hid(ag_ref[...], pg_ref[...] * first, wg_ref[...], bg_ref[...], K)
        hv, _ = _conv_hid(av_ref[...], pv_ref[...] * first, wv_ref[...], bv_ref[...], K)
        sg = _sigmoid(hg)
        d = dp_ref[...].astype(F32)
        is_gate = (j < nbh).astype(F32)
        dh = d * (is_gate * (hv * (sg * (1.0 + hg * (1.0 - sg)))) + (1.0 - is_gate) * (hg * sg))
        dh_ref[...] = dh
        xo = ao_ref[...]
        po = po_ref[...] * first

        @pl.when(t == 0)
        def _():
            dw_ref[...] = jnp.zeros_like(dw_ref)
            db_ref[...] = jnp.zeros_like(db_ref)

        db_ref[...] += jnp.sum(dh, axis=0, keepdims=True)
        for jj in range(K):
            sh = K - 1 - jj
            xs = xo if sh == 0 else _shift_down(xo, po, sh)
            dw_ref[jj:jj + 1, :] += jnp.sum(dh * xs, axis=0, keepdims=True)

    def gi(c):
        return lax.rem(c, nbh)

    return pl.pallas_call(
        body, name=name, grid=(2 * nbh, T // tt),
        in_specs=[pl.BlockSpec((tt, tc), lambda c, i: (i, c)),
                  pl.BlockSpec((8, tc), lambda c, i: (_prev_idx(i, nb8), c)),
                  pl.BlockSpec((tt, tc), lambda c, i: (i, gi(c))),
                  pl.BlockSpec((8, tc), lambda c, i: (_prev_idx(i, nb8), gi(c))),
                  pl.BlockSpec((tt, tc), lambda c, i: (i, gi(c) + nbh)),
                  pl.BlockSpec((8, tc), lambda c, i: (_prev_idx(i, nb8), gi(c) + nbh)),
                  pl.BlockSpec((K, tc), lambda c, i: (0, gi(c))), pl.BlockSpec((K, tc), lambda c, i: (0, gi(c) + nbh)),
                  pl.BlockSpec((1, tc), lambda c, i: (0, gi(c))), pl.BlockSpec((1, tc), lambda c, i: (0, gi(c) + nbh)),
                  pl.BlockSpec((tt, tc), lambda c, i: (i, gi(c)))],
        out_specs=[pl.BlockSpec((tt, tc), lambda c, i: (i, c)), pl.BlockSpec((K, tc), lambda c, i: (0, c)),
                   pl.BlockSpec((1, tc), lambda c, i: (0, c))],
        out_shape=[jax.ShapeDtypeStruct((T, 2 * D_FF), F32), jax.ShapeDtypeStruct((K, 2 * D_FF), F32),
                   jax.ShapeDtypeStruct((1, 2 * D_FF), F32)],
        compiler_params=_cparams(("parallel", "arbitrary")))(a, a, a, a, a, a, w, w, b, b, dp)


def _ffn_conv_fwd3(a3, w, b, *, name, tt=256, tc=1408):
    T = a3.shape[1]
    tt = min(tt, T)
    K, nbh, n16 = FFN_CONV, D_FF // tc, tt // 16

    def body(a_ref, p_ref, wg_ref, wv_ref, bg_ref, bv_ref, o_ref):
        first = (pl.program_id(1) > 0).astype(F32)
        a = a_ref[...].astype(F32)
        prev = p_ref[...].astype(F32)[:, 8:16, :] * first
        hg, _ = _conv_hid(a[0], prev[0], wg_ref[...], bg_ref[...], K)
        hv, _ = _conv_hid(a[1], prev[1], wv_ref[...], bv_ref[...], K)
        o_ref[...] = (hg * _sigmoid(hg) * hv).astype(BF16)

    return pl.pallas_call(
        body, name=name, grid=(nbh, T // tt),
        in_specs=[pl.BlockSpec((2, tt, tc), lambda c, i: (0, i, c)),
                  pl.BlockSpec((2, 16, tc), lambda c, i: (0, _prev_idx(i, n16), c)),
                  pl.BlockSpec((K, tc), lambda c, i: (0, c)), pl.BlockSpec((K, tc), lambda c, i: (0, c + nbh)),
                  pl.BlockSpec((1, tc), lambda c, i: (0, c)), pl.BlockSpec((1, tc), lambda c, i: (0, c + nbh))],
        out_specs=pl.BlockSpec((tt, tc), lambda c, i: (i, c)),
        out_shape=jax.ShapeDtypeStruct((T, D_FF), BF16),
        compiler_params=_cparams(("parallel", "parallel")))(a3, a3, w, w, b, b)


def _ffn_conv_bwd3(a3, w, b, dp, *, name, tt=256, tc=1408):
    T = a3.shape[1]
    tt = min(tt, T)
    K, nbh, n16 = FFN_CONV, D_FF // tc, tt // 16

    def body(a_ref, p_ref, wg_ref, wv_ref, bg_ref, bv_ref, dp_ref, dh_ref, dw_ref, db_ref):
        t = pl.program_id(1)
        first = (t > 0).astype(F32)
        a = a_ref[...].astype(F32)
        prev = p_ref[...].astype(F32)[:, 8:16, :] * first
        hg, sh_g = _conv_hid(a[0], prev[0], wg_ref[...], bg_ref[...], K)
        hv, sh_v = _conv_hid(a[1], prev[1], wv_ref[...], bv_ref[...], K)
        sg = _sigmoid(hg)
        d = dp_ref[...].astype(F32)
        dhg = d * hv * (sg * (1.0 + hg * (1.0 - sg)))
        dhv = d * (hg * sg)
        dh_ref[0] = dhg.astype(BF16)
        dh_ref[1] = dhv.astype(BF16)

        @pl.when(t == 0)
        def _():
            dw_ref[...] = jnp.zeros_like(dw_ref)
            db_ref[...] = jnp.zeros_like(db_ref)

        db_ref[0] += jnp.sum(dhg, axis=0, keepdims=True)
        db_ref[1] += jnp.sum(dhv, axis=0, keepdims=True)
        for j in range(K):
            dw_ref[0, j:j + 1, :] += jnp.sum(dhg * sh_g[j], axis=0, keepdims=True)
            dw_ref[1, j:j + 1, :] += jnp.sum(dhv * sh_v[j], axis=0, keepdims=True)

    return pl.pallas_call(
        body, name=name, grid=(nbh, T // tt),
        in_specs=[pl.BlockSpec((2, tt, tc), lambda c, i: (0, i, c)),
                  pl.BlockSpec((2, 16, tc), lambda c, i: (0, _prev_idx(i, n16), c)),
                  pl.BlockSpec((K, tc), lambda c, i: (0, c)), pl.BlockSpec((K, tc), lambda c, i: (0, c + nbh)),
                  pl.BlockSpec((1, tc), lambda c, i: (0, c)), pl.BlockSpec((1, tc), lambda c, i: (0, c + nbh)),
                  pl.BlockSpec((tt, tc), lambda c, i: (i, c))],
        out_specs=[pl.BlockSpec((2, tt, tc), lambda c, i: (0, i, c)), pl.BlockSpec((2, K, tc), lambda c, i: (0, 0, c)),
                   pl.BlockSpec((2, 1, tc), lambda c, i: (0, 0, c))],
        out_shape=[jax.ShapeDtypeStruct((2, T, D_FF), BF16), jax.ShapeDtypeStruct((2, K, D_FF), F32),
                   jax.ShapeDtypeStruct((2, 1, D_FF), F32)],
        compiler_params=_cparams(("parallel", "arbitrary")))(a3, a3, w, w, b, b, dp)


def _conv_bwd_in3(dh3, w, *, name, K, tt=256, tc=1408):
    H, T, C = dh3.shape
    tt = min(tt, T)
    nb, n16, nT = C // tc, tt // 16, T // tt
    last16 = T // 16 - 1

    def body(d_ref, n_ref, w_ref, o_ref):
        notlast = (pl.program_id(2) < nT - 1).astype(F32)
        d = d_ref[...].astype(F32)
        nxt = n_ref[...].astype(F32)[0:8, :] * notlast
        w_ = w_ref[...]
        acc = d * w_[K - 1:K, :]
        for sh in range(1, K):
            acc = acc + _shift_up(d, nxt, sh) * w_[K - 1 - sh:K - sh, :]
        o_ref[...] = acc.astype(BF16)

    return pl.pallas_call(
        body, name=name, grid=(H, nb, nT),
        in_specs=[pl.BlockSpec((None, tt, tc), lambda h, c, i: (h, i, c)),
                  pl.BlockSpec((None, 16, tc), lambda h, c, i: (h, jnp.minimum((i + 1) * n16, last16), c)),
                  pl.BlockSpec((K, tc), lambda h, c, i: (0, h * nb + c))],
        out_specs=pl.BlockSpec((None, tt, tc), lambda h, c, i: (h, i, c)),
        out_shape=jax.ShapeDtypeStruct((H, T, C), BF16),
        compiler_params=_cparams(("parallel", "parallel", "parallel")))(dh3, dh3, w)


def _cumsum_rows(x):
    L = x.shape[0]
    row = lax.broadcasted_iota(jnp.int32, x.shape, 0)
    k = 1
    while k < L:
        x = x + jnp.where(row >= k, pltpu.roll(x, k, 0), 0.0)
        k *= 2
    return x


def _rcumsum_rows(x):
    L = x.shape[0]
    row = lax.broadcasted_iota(jnp.int32, x.shape, 0)
    k = 1
    while k < L:
        x = x + jnp.where(row < L - k, pltpu.roll(x, L - k, 0), 0.0)
        k *= 2
    return x


def _split_terms(m, n):
    terms, rest = [], m
    for _ in range(n):
        t = rest.astype(BF16)
        terms.append(t)
        rest = rest - t.astype(F32)
    return jnp.concatenate(terms, axis=1)


def _select_dot(m, n_terms, n_out, cond):
    K = m.shape[1]
    k = lax.broadcasted_iota(jnp.int32, (K, n_out), 0)
    j = lax.broadcasted_iota(jnp.int32, (K, n_out), 1)
    sel = cond(k, j).astype(BF16)
    return jnp.dot(_split_terms(m, n_terms), jnp.concatenate([sel] * n_terms, axis=0), preferred_element_type=F32)


def _rowsum_mxu(m):
    return _select_dot(m, 2, 128, lambda k, j: k >= 0)


def _lane_block_sums(m, width):
    shift = width.bit_length() - 1
    return _select_dot(m, 2, 128, lambda k, j: j == jnp.right_shift(k, shift))


def _heads_to_pairs(m):
    return _select_dot(m, 3, 512, lambda k, j: k == jnp.right_shift(j, 6))


def _ssd_common(dt_ref, par_ref):
    par = par_ref[...]
    raw = dt_ref[...] + par[0:1, :]
    dt = _softplus(raw)
    a = -jnp.exp(par[1:2, :])
    cs = _cumsum_rows(dt * a)
    L = cs.shape[0]
    cs_last = cs[L - 1:L, :]
    return raw, dt, a, par[2:3, :], cs, cs.T, jnp.exp(cs), jnp.exp(cs_last - cs), jnp.exp(cs_last)


def _ssd_specs(nc, rev):
    L = SSM_CHUNK

    def ci(c):
        return nc - 1 - c if rev else c

    return [pl.BlockSpec((L, D_INNER), lambda c: (ci(c), 0)),
            pl.BlockSpec((L, GN), lambda c: (ci(c), D_INNER // GN)),
            pl.BlockSpec((L, GN), lambda c: (ci(c), D_INNER // GN + 1)),
            pl.BlockSpec((SSM_GROUPS, L, 128), lambda c: (0, ci(c), 0)),
            pl.BlockSpec((SSM_GROUPS, 8, 128), lambda c: (0, 0, 0)),
            pl.BlockSpec((L, D_INNER), lambda c: (ci(c), 0)),
            pl.BlockSpec((1, D_INNER), lambda c: (0, 0))], ci


def _round_robin(gens):
    live = list(gens)
    while live:
        nxt = []
        for gen in live:
            try:
                next(gen)
                nxt.append(gen)
            except StopIteration:
                pass
        live = nxt


def _group_views(g, wide, narrow, lead):
    return ([r.at[:, g * 512:(g + 1) * 512] for r in wide], [r.at[:, g * 128:(g + 1) * 128] for r in narrow],
            [r.at[g] for r in lead])


def _ssd_fwd(xbc_c, zx, dtg, par, gnw, *, name):
    T = xbc_c.shape[0]
    L = SSM_CHUNK
    nc = T // L
    in_specs, ci = _ssd_specs(nc, False)

    def body(xs_ref, b_ref, c_ref, dt_ref, par_ref, z_ref, gnw_ref, y_ref, yn_ref, st_ref, h_ref):
        @pl.when(pl.program_id(0) == 0)
        def _():
            h_ref[...] = jnp.zeros_like(h_ref)

        gens = []
        for g in range(SSM_GROUPS):
            (xs, z, gw, y, yn), (b, c), (dt, pr, st, h) = _group_views(
                g, [xs_ref, z_ref, gnw_ref, y_ref, yn_ref], [b_ref, c_ref], [dt_ref, par_ref, st_ref, h_ref])
            gens.append(group(xs, b, c, dt, pr, z, gw, y, yn, st, h))
        _round_robin(gens)

    def group(xs_ref, b_ref, c_ref, dt_ref, par_ref, z_ref, gnw_ref, y_ref, yn_ref, st_ref, h_ref):
        _, dt, _, dsk, cs, csT, ecs, eend, dec = _ssd_common(dt_ref, par_ref)
        Bb = b_ref[...].astype(BF16)
        Cb = c_ref[...].astype(BF16)
        G = lax.dot_general(Cb, Bb, _NT, preferred_element_type=F32)
        row = lax.broadcasted_iota(jnp.int32, (L, L), 0)
        col = lax.broadcasted_iota(jnp.int32, (L, L), 1)
        tril = col <= row
        lo = lax.broadcasted_iota(jnp.int32, (L, 128), 1) < 64
        lo1 = lax.broadcasted_iota(jnp.int32, (1, 128), 1) < 64
        dt_x, ecs_x, eend_x = (_heads_to_pairs(m) for m in (dt, ecs, eend))
        for pp in range(4):
            hA, hB = 2 * pp, 2 * pp + 1
            lanes = slice(pp * 128, (pp + 1) * 128)

            def sel1(m):
                return jnp.where(lo1, m[:, hA:hA + 1], m[:, hB:hB + 1])

            X = xs_ref[:, lanes]
            xd = X * dt_x[:, lanes]
            xdb = xd.astype(BF16)
            ys = []
            for h in (hA, hB):
                Lm = jnp.where(tril, jnp.exp(jnp.minimum(cs[:, h:h + 1] - csT[h:h + 1, :], 0.0)), 0.0)
                ys.append(jnp.dot((G * Lm).astype(BF16), xdb, preferred_element_type=F32))
                yield
            Hp = h_ref[pp]
            st_ref[pp] = Hp
            yoff = jnp.dot(Cb, Hp.astype(BF16), preferred_element_type=F32) * ecs_x[:, lanes]
            y_ref[:, lanes] = jnp.where(lo, ys[0], ys[1]) + yoff + sel1(dsk) * X
            S = lax.dot_general(Bb, (xd * eend_x[:, lanes]).astype(BF16), _TN, preferred_element_type=F32)
            h_ref[pp] = Hp * sel1(dec) + S
            yield
        zv = z_ref[...]
        yg = y_ref[...] * (zv * _sigmoid(zv))
        r = jnp.tile(lax.rsqrt(_rowsum_mxu(yg * yg) * (1.0 / 512) + EPS), (1, 4))
        yn_ref[...] = (yg * r * gnw_ref[...]).astype(BF16)

    return pl.pallas_call(
        body, name=name, grid=(nc,), in_specs=in_specs,
        out_specs=[pl.BlockSpec((L, D_INNER), lambda c: (c, 0)), pl.BlockSpec((L, D_INNER), lambda c: (c, 0)),
                   pl.BlockSpec((SSM_GROUPS, None, 4, 128, 128), lambda c: (0, c, 0, 0, 0))],
        out_shape=[jax.ShapeDtypeStruct((T, D_INNER), F32), jax.ShapeDtypeStruct((T, D_INNER), BF16),
                   jax.ShapeDtypeStruct((SSM_GROUPS, nc, 4, 128, 128), F32)],
        scratch_shapes=[pltpu.VMEM((SSM_GROUPS, 4, 128, 128), F32)],
        compiler_params=_cparams(("arbitrary",)))(xbc_c, xbc_c, xbc_c, dtg, par, zx, gnw)


def _ssd_bwd(xbc_c, zx, dtg, par, gnw, y, st, dyn, *, name):
    T = xbc_c.shape[0]
    L = SSM_CHUNK
    nc = T // L
    in_specs, ci = _ssd_specs(nc, True)
    in_specs += [pl.BlockSpec((L, D_INNER), lambda c: (ci(c), 0)),
                 pl.BlockSpec((SSM_GROUPS, None, 4, 128, 128), lambda c: (0, ci(c), 0, 0, 0)),
                 pl.BlockSpec((L, D_INNER), lambda c: (ci(c), 0))]

    def body(xs_ref, b_ref, c_ref, dt_ref, par_ref, z_ref, gnw_ref, y_ref, st_ref, dyn_ref,
             dxs_ref, db_ref, dc_ref, dz_ref, ddt_ref, dgnw_ref, dpar_ref, dh_ref):
        @pl.when(pl.program_id(0) == 0)
        def _():
            dh_ref[...] = jnp.zeros_like(dh_ref)
            dgnw_ref[...] = jnp.zeros_like(dgnw_ref)
            dpar_ref[...] = jnp.zeros_like(dpar_ref)

        gens = []
        for g in range(SSM_GROUPS):
            (xs, z, gw, y, dyn, dxs, dz, dgw), (b, c, db, dc), (dt, pr, st, ddt, dpr, dh) = _group_views(
                g, [xs_ref, z_ref, gnw_ref, y_ref, dyn_ref, dxs_ref, dz_ref, dgnw_ref], [b_ref, c_ref, db_ref, dc_ref],
                [dt_ref, par_ref, st_ref, ddt_ref, dpar_ref, dh_ref])
            gens.append(group(xs, b, c, dt, pr, z, gw, y, st, dyn, dxs, db, dc, dz, ddt, dgw, dpr, dh))
        _round_robin(gens)

    def group(xs_ref, b_ref, c_ref, dt_ref, par_ref, z_ref, gnw_ref, y_ref, st_ref, dyn_ref,
              dxs_ref, db_ref, dc_ref, dz_ref, ddt_ref, dgnw_ref, dpar_ref, dh_ref):
        yv = y_ref[...]
        zv = z_ref[...]
        sg = _sigmoid(zv)
        sz = zv * sg
        yg = yv * sz
        r = jnp.tile(lax.rsqrt(_rowsum_mxu(yg * yg) * (1.0 / 512) + EPS), (1, 4))
        yh = yg * r
        dyn = dyn_ref[...].astype(F32)
        dgnw_ref[...] += jnp.sum(dyn * yh, axis=0, keepdims=True)
        dyh = dyn * gnw_ref[...]
        dyg = r * (dyh - yh * jnp.tile(_rowsum_mxu(dyh * yh) * (1.0 / 512), (1, 4)))
        dY_all = dyg * sz
        dz_ref[...] = (dyg * yv * (sg * (1.0 + zv * (1.0 - sg)))).astype(dz_ref.dtype)

        yield
        raw, dt, a, dsk, cs, csT, ecs, eend, dec = _ssd_common(dt_ref, par_ref)
        Bb = b_ref[...].astype(BF16)
        Cb = c_ref[...].astype(BF16)
        G = lax.dot_general(Cb, Bb, _NT, preferred_element_type=F32)
        row = lax.broadcasted_iota(jnp.int32, (L, L), 0)
        col = lax.broadcasted_iota(jnp.int32, (L, L), 1)
        tril = col <= row
        lo = lax.broadcasted_iota(jnp.int32, (L, 128), 1) < 64
        lane1 = lax.broadcasted_iota(jnp.int32, (1, 128), 1)
        lo1 = lane1 < 64
        rowl = lax.broadcasted_iota(jnp.int32, (L, 128), 0)
        dt_x, ecs_x, eend_x = (_heads_to_pairs(m) for m in (dt, ecs, eend))
        dG = jnp.zeros((L, L), F32)
        dB = jnp.zeros((L, SSM_STATE), F32)
        dC = jnp.zeros((L, SSM_STATE), F32)
        dcs_t = jnp.zeros((L, L), F32)
        tails = jnp.zeros((1, 128), F32)
        dD_row = jnp.zeros((1, 128), F32)
        v_parts, prod_parts = [], []

        def tot(m):
            return jnp.sum(jnp.sum(m, axis=0, keepdims=True), axis=1, keepdims=True)

        for pp in range(4):
            hA, hB = 2 * pp, 2 * pp + 1
            lanes = slice(pp * 128, (pp + 1) * 128)

            def sel1(m):
                return jnp.where(lo1, m[:, hA:hA + 1], m[:, hB:hB + 1])

            X = xs_ref[:, lanes]
            dY = dY_all[:, lanes]
            dtsel = dt_x[:, lanes]
            xd = X * dtsel
            xdb = xd.astype(BF16)
            dYb = dY.astype(BF16)
            Hp = st_ref[pp]
            Hb = Hp.astype(BF16)
            dHn = dh_ref[pp]
            dHb = dHn.astype(BF16)
            ecs_sel = ecs_x[:, lanes]
            eend_sel = eend_x[:, lanes]
            dxd_state = jnp.dot(Bb, dHb, preferred_element_type=F32) * eend_sel
            yoff = jnp.dot(Cb, Hb, preferred_element_type=F32) * ecs_sel
            dYe = (dY * ecs_sel).astype(BF16)
            dC = dC + lax.dot_general(dYe, Hb, _NT, preferred_element_type=F32)
            dB = dB + lax.dot_general((xd * eend_sel).astype(BF16), dHb, _NT, preferred_element_type=F32)
            dh_ref[pp] = dHn * sel1(dec) + lax.dot_general(Cb, dYe, _TN, preferred_element_type=F32)
            q = xd * dxd_state
            dyq = dY * yoff - q
            qcol = jnp.sum(q, axis=0, keepdims=True)
            hcol = jnp.sum(dHn * Hp, axis=0, keepdims=True)
            dxd_diag = []
            for h, msk, msk1 in ((hA, lo, lo1), (hB, jnp.logical_not(lo), jnp.logical_not(lo1))):
                Lm = jnp.where(tril, jnp.exp(jnp.minimum(cs[:, h:h + 1] - csT[h:h + 1, :], 0.0)), 0.0)
                M = G * Lm
                dxd_diag.append(lax.dot_general(M.astype(BF16), dYb, _TN, preferred_element_type=F32))
                dM = lax.dot_general(jnp.where(msk, dY, 0.0).astype(BF16), xdb, _NT, preferred_element_type=F32)
                dG = dG + dM * Lm
                W = dM * M
                dcs_t = dcs_t + jnp.where(row == h, jnp.sum(W, axis=0, keepdims=True), 0.0)
                v_parts.append(W + jnp.where(msk, dyq, 0.0))
                tail = (jnp.sum(jnp.where(msk1, qcol, 0.0), axis=1, keepdims=True)
                        + dec[:, h:h + 1] * jnp.sum(jnp.where(msk1, hcol, 0.0), axis=1, keepdims=True))
                tails = tails + jnp.where(lane1 == h, tail, 0.0)
                yield
            dxd = jnp.where(lo, dxd_diag[0], dxd_diag[1]) + dxd_state
            prod_parts.append(dxd * X)
            dxs_ref[:, lanes] = dxd * dtsel + sel1(dsk) * dY
            dyx = jnp.sum(dY * X, axis=0, keepdims=True)
            sA = jnp.sum(jnp.where(lo1, dyx, 0.0), axis=1, keepdims=True)
            sB = jnp.sum(dyx, axis=1, keepdims=True) - sA
            dD_row = dD_row + jnp.where(lane1 == hA, sA, 0.0) + jnp.where(lane1 == hB, sB, 0.0)
            yield
        dGb = dG.astype(BF16)
        db_ref[...] = dB + lax.dot_general(dGb, Cb, _TN, preferred_element_type=F32)
        dc_ref[...] = dC + jnp.dot(dGb, Bb, preferred_element_type=F32)
        dcs_mat = _lane_block_sums(jnp.concatenate(v_parts, axis=1), 128) + jnp.where(rowl == L - 1, tails, 0.0)
        ddt_mat = _lane_block_sums(jnp.concatenate(prod_parts, axis=1), 64)
        dad = _rcumsum_rows(dcs_mat - dcs_t.T)
        draw = (a * dad + ddt_mat) * _sigmoid(raw)
        ddt_ref[...] = draw
        dpar_ref[0:1, :] += jnp.sum(draw, axis=0, keepdims=True)
        dpar_ref[1:2, :] += jnp.sum(dt * dad, axis=0, keepdims=True) * a
        dpar_ref[2:3, :] += dD_row

    return pl.pallas_call(
        body, name=name, grid=(nc,), in_specs=in_specs,
        out_specs=[pl.BlockSpec((L, D_INNER), lambda c: (ci(c), 0)),
                   pl.BlockSpec((L, GN), lambda c: (ci(c), 0)),
                   pl.BlockSpec((L, GN), lambda c: (ci(c), 0)),
                   pl.BlockSpec((L, D_INNER), lambda c: (ci(c), 0)),
                   pl.BlockSpec((SSM_GROUPS, L, 128), lambda c: (0, ci(c), 0)),
                   pl.BlockSpec((1, D_INNER), lambda c: (0, 0)),
                   pl.BlockSpec((SSM_GROUPS, 8, 128), lambda c: (0, 0, 0))],
        out_shape=[jax.ShapeDtypeStruct((T, D_INNER), F32), jax.ShapeDtypeStruct((T, GN), F32),
                   jax.ShapeDtypeStruct((T, GN), F32), jax.ShapeDtypeStruct((T, D_INNER), BF16),
                   jax.ShapeDtypeStruct((SSM_GROUPS, T, 128), F32), jax.ShapeDtypeStruct((1, D_INNER), F32),
                   jax.ShapeDtypeStruct((SSM_GROUPS, 8, 128), F32)],
        scratch_shapes=[pltpu.VMEM((SSM_GROUPS, 4, 128, 128), F32)],
        compiler_params=_cparams(("arbitrary",)))(xbc_c, xbc_c, xbc_c, dtg, par, zx, gnw, y, st, dyn)


SB_KEYS = 512
SB_SCAN = 256
SB_STRIP = 256


def _tri(width, cond):
    kk = lax.broadcasted_iota(jnp.int32, (width, width), 0)
    jj = lax.broadcasted_iota(jnp.int32, (width, width), 1)
    return cond(kk, jj).astype(BF16)


def _sba_diag_mask():
    Bq = SB_BLOCK
    rowi = lax.broadcasted_iota(jnp.int32, (2 * Bq, Bq), 0)
    return lax.broadcasted_iota(jnp.int32, (2 * Bq, Bq), 1) < jnp.where(rowi >= Bq, rowi - Bq, rowi)


_LOG2E = 1.4426950408889634


def _softplus2(z2):
    return jnp.maximum(z2, 0.0) + jnp.log2(1.0 + jnp.exp2(-jnp.abs(z2)))


def _sba_sub_fwd(zb, c, U, mask):
    z2 = zb * _LOG2E
    s = _softplus2(z2)
    if mask is not None:
        s = jnp.where(mask, s, 0.0)
    R = c + jnp.dot(s.astype(BF16), U, preferred_element_type=F32)
    A = jnp.exp2(z2 - s - R)
    if mask is not None:
        A = jnp.where(mask, A, 0.0)
    return A.astype(BF16), R[:, 0:1] + s[:, 0:1]


def _sba_sub_bwd(zb, dAb, Lt, pc, pe, Uincl, Uexcl, mask):
    last = zb.shape[1] - 1
    z2 = zb * _LOG2E
    s = _softplus2(z2)
    g = z2 - s
    if mask is not None:
        s = jnp.where(mask, s, 0.0)
    P = pc + jnp.dot(s.astype(BF16), Uincl, preferred_element_type=F32)
    A = jnp.exp2(g - (Lt - P))
    if mask is not None:
        A = jnp.where(mask, A, 0.0)
    E = dAb * A
    PE = pe + jnp.dot(E.astype(BF16), Uexcl, preferred_element_type=F32)
    dz = E - jnp.exp2(g) * (E + PE)
    if mask is not None:
        dz = jnp.where(mask, dz, 0.0)
    return (A.astype(BF16), dz.astype(BF16), P[:, last:last + 1], PE[:, last:last + 1] + E[:, last:last + 1])


def _stack_heads(v):
    lo = lax.broadcasted_iota(jnp.int32, v.shape, 1) < 64
    zero = jnp.zeros_like(v)
    return jnp.concatenate([jnp.where(lo, v, zero), jnp.where(lo, zero, v)], axis=0)


def _unstack_heads(v):
    lo = lax.broadcasted_iota(jnp.int32, (SB_BLOCK, 128), 1) < 64
    return jnp.where(lo, v[:SB_BLOCK], v[SB_BLOCK:])


def _sba_rows(a):
    return slice(2 * a * SB_BLOCK, 2 * (a + 1) * SB_BLOCK)


def _sba_diag_case(a, b):
    Bq = SB_BLOCK
    if b * SB_SCAN >= (a + 1) * Bq:
        return "skip"
    if (b + 1) * SB_SCAN <= a * Bq:
        return "full"
    rowi = lax.broadcasted_iota(jnp.int32, (2 * Bq, SB_SCAN), 0)
    qpos = a * Bq + jnp.where(rowi >= Bq, rowi - Bq, rowi)
    return b * SB_SCAN + lax.broadcasted_iota(jnp.int32, (2 * Bq, SB_SCAN), 1) < qpos


def _sba_fwd(q, kv, *, name):
    T = q.shape[0]
    Bq = SB_BLOCK
    nsub = SB_KEYS // Bq
    nscan = SB_KEYS // SB_SCAN
    R = 2 * SB_KEYS
    assert T % SB_KEYS == 0 and SB_STRIP == 2 * Bq
    scale = 1.0 / math.sqrt(SB_HEAD_DIM)

    def body(q_ref, k_ref, v_ref, o_ref, lt_ref, z_s, a_s, c_s, acc_s):
        i = pl.program_id(1)
        U2 = _tri(SB_SCAN, lambda k, j: k > j)
        qs_all = jnp.concatenate([_stack_heads(q_ref[a * Bq:(a + 1) * Bq, :] * scale) for a in range(nsub)], axis=0)
        c_s[...] = jnp.zeros_like(c_s)
        acc_s[...] = jnp.zeros_like(acc_s)

        def scores(J, slot):
            off = pl.multiple_of(J * SB_KEYS, SB_KEYS)
            z_s[slot] = lax.dot_general(qs_all, k_ref[pl.ds(off, SB_KEYS), :], _NT, preferred_element_type=F32)

        def weights(slot, diag):
            for a in range(nsub):
                rows = _sba_rows(a)
                c = c_s[rows, :]
                for b in reversed(range(nscan)):
                    cols = slice(b * SB_SCAN, (b + 1) * SB_SCAN)
                    case = _sba_diag_case(a, b) if diag else "full"
                    if isinstance(case, str) and case == "skip":
                        a_s[slot, rows, cols] = jnp.zeros((2 * Bq, SB_SCAN), BF16)
                        continue
                    A, c = _sba_sub_fwd(z_s[slot, rows, cols], c, U2, None if isinstance(case, str) else case)
                    a_s[slot, rows, cols] = A
                c_s[rows, :] = c

        def values(J, slot):
            off = pl.multiple_of(J * SB_KEYS, SB_KEYS)
            acc_s[...] += jnp.dot(a_s[slot], v_ref[pl.ds(off, SB_KEYS), :], preferred_element_type=F32)

        scores(i, 0)
        weights(0, True)
        scores(jnp.maximum(i - 1, 0), 1)

        def step(t, _):
            slot = lax.rem(t, 2)
            weights(slot, False)
            scores(jnp.maximum(i - t - 1, 0), 1 - slot)
            values(i - t + 1, 1 - slot)
            return 0

        lax.fori_loop(1, i + 1, step, 0)
        values(0, lax.rem(i, 2))
        for a in range(nsub):
            o_ref[a * Bq:(a + 1) * Bq, :] = _unstack_heads(acc_s[_sba_rows(a), :]).astype(BF16)
            lt_ref[a * Bq:(a + 1) * Bq, :] = _unstack_heads(jnp.broadcast_to(c_s[_sba_rows(a), :], (2 * Bq, 128)))

    return pl.pallas_call(
        body, name=name, grid=(SB_HEADS // 2, T // SB_KEYS),
        in_specs=[pl.BlockSpec((SB_KEYS, 128), lambda p, i: (i, p)), pl.BlockSpec((T, 128), lambda p, i: (0, p)),
                  pl.BlockSpec((T, 128), lambda p, i: (0, p + SB_HEADS // 2))],
        out_specs=[pl.BlockSpec((SB_KEYS, 128), lambda p, i: (i, p)),
                   pl.BlockSpec((None, SB_KEYS, 128), lambda p, i: (p, i, 0))],
        out_shape=[jax.ShapeDtypeStruct((T, D_MODEL), BF16), jax.ShapeDtypeStruct((SB_HEADS // 2, T, 128), F32)],
        scratch_shapes=[pltpu.VMEM((2, R, SB_KEYS), F32), pltpu.VMEM((2, R, SB_KEYS), BF16),
                        pltpu.VMEM((R, 1), F32), pltpu.VMEM((R, 128), F32)],
        compiler_params=_cparams(("parallel", "parallel")))(q, kv, kv)


def _sba_bwd(q, kv, lt, do, *, name):
    T = q.shape[0]
    Bq = SB_BLOCK
    nq = T // SB_KEYS
    nsub = SB_KEYS // Bq
    nscan = SB_KEYS // SB_SCAN
    R = 2 * SB_KEYS
    assert T % SB_KEYS == 0 and SB_STRIP == 2 * Bq
    scale = 1.0 / math.sqrt(SB_HEAD_DIM)

    def body(q_ref, k_ref, v_ref, lt_ref, do_ref, dq_ref, dk_ref, dv_ref, dk_acc, dv_acc,
             z_s, da_s, a_s, dz_s, pc_s, pe_s, lt_s):
        i = pl.program_id(1)

        @pl.when(i == 0)
        def _():
            dk_acc[...] = jnp.zeros_like(dk_acc)
            dv_acc[...] = jnp.zeros_like(dv_acc)

        Uincl = _tri(SB_SCAN, lambda k, j: k <= j)
        Uexcl = _tri(SB_SCAN, lambda k, j: k < j)
        qs, dos = [], []
        for a in range(nsub):
            rows = slice(a * Bq, (a + 1) * Bq)
            qs.append(_stack_heads(q_ref[rows, :] * scale))
            dos.append(_stack_heads(do_ref[rows, :]))
            lt_s[_sba_rows(a), :] = jnp.concatenate([lt_ref[rows, 0:1], lt_ref[rows, 64:65]], axis=0)
        qs_all = jnp.concatenate(qs, axis=0)
        dos_all = jnp.concatenate(dos, axis=0)
        pc_s[...] = jnp.zeros_like(pc_s)
        pe_s[...] = jnp.zeros_like(pe_s)
        a_s[1] = jnp.zeros((R, SB_KEYS), BF16)
        dz_s[1] = jnp.zeros((R, SB_KEYS), BF16)

        def scores(J, slot):
            off = pl.multiple_of(J * SB_KEYS, SB_KEYS)
            z_s[slot] = lax.dot_general(qs_all, k_ref[pl.ds(off, SB_KEYS), :], _NT, preferred_element_type=F32)
            da_s[slot] = lax.dot_general(dos_all, v_ref[pl.ds(off, SB_KEYS), :], _NT, preferred_element_type=F32)

        def gradients(slot, diag):
            for a in range(nsub):
                rows = _sba_rows(a)
                pc, pe, Lt = pc_s[rows, :], pe_s[rows, :], lt_s[rows, :]
                for b in range(nscan):
                    cols = slice(b * SB_SCAN, (b + 1) * SB_SCAN)
                    case = _sba_diag_case(a, b) if diag else "full"
                    if isinstance(case, str) and case == "skip":
                        a_s[slot, rows, cols] = jnp.zeros((2 * Bq, SB_SCAN), BF16)
                        dz_s[slot, rows, cols] = jnp.zeros((2 * Bq, SB_SCAN), BF16)
                        continue
                    A, dz, pc, pe = _sba_sub_bwd(z_s[slot, rows, cols], da_s[slot, rows, cols], Lt, pc, pe, Uincl, Uexcl,
                                                 None if isinstance(case, str) else case)
                    a_s[slot, rows, cols] = A
                    dz_s[slot, rows, cols] = dz
                pc_s[rows, :] = pc
                pe_s[rows, :] = pe

        def products(J, slot, dq_acc):
            off = pl.multiple_of(J * SB_KEYS, SB_KEYS)
            dzt = dz_s[slot]
            dk_acc[pl.ds(off, SB_KEYS), :] += lax.dot_general(dzt, qs_all, _TN, preferred_element_type=F32)
            dv_acc[pl.ds(off, SB_KEYS), :] += lax.dot_general(a_s[slot], dos_all, _TN, preferred_element_type=F32)
            return dq_acc + jnp.dot(dzt, k_ref[pl.ds(off, SB_KEYS), :], preferred_element_type=F32)

        scores(0, 0)

        def step(t, dq_acc):
            slot = lax.rem(t, 2)
            gradients(slot, False)
            scores(t + 1, 1 - slot)
            return products(jnp.maximum(t - 1, 0), 1 - slot, dq_acc)

        dq_acc = lax.fori_loop(0, i, step, jnp.zeros((R, 128), F32))
        own = lax.rem(i, 2)
        gradients(own, True)
        dq_acc = products(jnp.maximum(i - 1, 0), 1 - own, dq_acc)
        dq_acc = products(i, own, dq_acc)
        for a in range(nsub):
            dq_ref[a * Bq:(a + 1) * Bq, :] = (_unstack_heads(dq_acc[_sba_rows(a)]) * scale).astype(BF16)

        @pl.when(i == nq - 1)
        def _():
            dk_ref[...] = dk_acc[...].astype(BF16)
            dv_ref[...] = dv_acc[...].astype(BF16)

    return pl.pallas_call(
        body, name=name, grid=(SB_HEADS // 2, nq),
        in_specs=[pl.BlockSpec((SB_KEYS, 128), lambda p, i: (i, p)), pl.BlockSpec((T, 128), lambda p, i: (0, p)),
                  pl.BlockSpec((T, 128), lambda p, i: (0, p + SB_HEADS // 2)),
                  pl.BlockSpec((None, SB_KEYS, 128), lambda p, i: (p, i, 0)),
                  pl.BlockSpec((SB_KEYS, 128), lambda p, i: (i, p))],
        out_specs=[pl.BlockSpec((SB_KEYS, 128), lambda p, i: (i, p)), pl.BlockSpec((T, 128), lambda p, i: (0, p)),
                   pl.BlockSpec((T, 128), lambda p, i: (0, p))],
        out_shape=[jax.ShapeDtypeStruct((T, D_MODEL), BF16), jax.ShapeDtypeStruct((T, D_MODEL), BF16),
                   jax.ShapeDtypeStruct((T, D_MODEL), BF16)],
        scratch_shapes=[pltpu.VMEM((T, 128), F32), pltpu.VMEM((T, 128), F32),
                        pltpu.VMEM((2, R, SB_KEYS), F32), pltpu.VMEM((2, R, SB_KEYS), F32),
                        pltpu.VMEM((2, R, SB_KEYS), BF16), pltpu.VMEM((2, R, SB_KEYS), BF16),
                        pltpu.VMEM((R, 1), F32), pltpu.VMEM((R, 1), F32), pltpu.VMEM((R, 1), F32)],
        compiler_params=_cparams(("parallel", "arbitrary")))(q, kv, kv, lt, do)


def _sba_fwd_old(q, kv, *, name):
    T = q.shape[0]
    Bq = SB_BLOCK
    nsub = SB_KEYS // Bq
    assert T % SB_KEYS == 0
    scale = 1.0 / math.sqrt(SB_HEAD_DIM)

    def body(q_ref, k_ref, v_ref, o_ref, lt_ref):
        I = pl.program_id(1)
        U1 = _tri(Bq, lambda k, j: k > j)
        U2 = _tri(SB_SCAN, lambda k, j: k > j)
        dmask = _sba_diag_mask()
        qs = [_stack_heads(q_ref[a * Bq:(a + 1) * Bq, :] * scale) for a in range(nsub)]
        cs, accs = [], []
        for a in range(nsub):
            c = jnp.zeros((2 * Bq, 1), F32)
            acc = jnp.zeros((2 * Bq, 128), F32)
            for b in range(a, -1, -1):
                off = pl.multiple_of(I * SB_KEYS + b * Bq, Bq)
                zb = lax.dot_general(qs[a], k_ref[pl.ds(off, Bq), :], _NT, preferred_element_type=F32)
                A, c = _sba_sub_fwd(zb, c, U1, dmask if b == a else None)
                acc = acc + jnp.dot(A, v_ref[pl.ds(off, Bq), :], preferred_element_type=F32)
            cs.append(c)
            accs.append(acc)
        qs_all = jnp.concatenate(qs, axis=0)

        def step(n, carry):
            c, acc = carry
            off = pl.multiple_of((I - 1 - n) * SB_KEYS, SB_KEYS)
            z = lax.dot_general(qs_all, k_ref[pl.ds(off, SB_KEYS), :], _NT, preferred_element_type=F32)
            parts = [None] * (SB_KEYS // SB_SCAN)
            for b in reversed(range(SB_KEYS // SB_SCAN)):
                parts[b], c = _sba_sub_fwd(z[:, b * SB_SCAN:(b + 1) * SB_SCAN], c, U2, None)
            return c, acc + jnp.dot(jnp.concatenate(parts, axis=1), v_ref[pl.ds(off, SB_KEYS), :],
                                    preferred_element_type=F32)

        c, acc = lax.fori_loop(0, I, step, (jnp.concatenate(cs, axis=0), jnp.concatenate(accs, axis=0)))
        for a in range(nsub):
            rows = slice(2 * a * Bq, 2 * (a + 1) * Bq)
            o_ref[a * Bq:(a + 1) * Bq, :] = _unstack_heads(acc[rows]).astype(BF16)
            lt_ref[a * Bq:(a + 1) * Bq, :] = _unstack_heads(jnp.broadcast_to(c[rows], (2 * Bq, 128)))

    return pl.pallas_call(
        body, name=name, grid=(SB_HEADS // 2, T // SB_KEYS),
        in_specs=[pl.BlockSpec((SB_KEYS, 128), lambda p, i: (i, p)), pl.BlockSpec((T, 128), lambda p, i: (0, p)),
                  pl.BlockSpec((T, 128), lambda p, i: (0, p + SB_HEADS // 2))],
        out_specs=[pl.BlockSpec((SB_KEYS, 128), lambda p, i: (i, p)),
                   pl.BlockSpec((None, SB_KEYS, 128), lambda p, i: (p, i, 0))],
        out_shape=[jax.ShapeDtypeStruct((T, D_MODEL), BF16), jax.ShapeDtypeStruct((SB_HEADS // 2, T, 128), F32)],
        compiler_params=_cparams(("parallel", "parallel")))(q, kv, kv)


def _sba_bwd_old(q, kv, lt, do, *, name):
    T = q.shape[0]
    Bq = SB_BLOCK
    nq = T // SB_KEYS
    nsub = SB_KEYS // Bq
    assert T % SB_KEYS == 0
    scale = 1.0 / math.sqrt(SB_HEAD_DIM)

    def body(q_ref, k_ref, v_ref, lt_ref, do_ref, dq_ref, dk_ref, dv_ref, dk_acc, dv_acc,
             z_s, da_s, a_s, dz_s, pc_s, pe_s, lt_s):
        i = pl.program_id(1)

        @pl.when(i == 0)
        def _():
            dk_acc[...] = jnp.zeros_like(dk_acc)
            dv_acc[...] = jnp.zeros_like(dv_acc)

        Uincl1 = _tri(Bq, lambda k, j: k <= j)
        Uexcl1 = _tri(Bq, lambda k, j: k < j)
        Uincl2 = _tri(SB_SCAN, lambda k, j: k <= j)
        Uexcl2 = _tri(SB_SCAN, lambda k, j: k < j)
        dmask = _sba_diag_mask()
        qs, dos, lts = [], [], []
        for a in range(nsub):
            rows = slice(a * Bq, (a + 1) * Bq)
            qs.append(_stack_heads(q_ref[rows, :] * scale))
            dos.append(_stack_heads(do_ref[rows, :]))
            lts.append(jnp.concatenate([lt_ref[rows, 0:1], lt_ref[rows, 64:65]], axis=0))
        qs_all = jnp.concatenate(qs, axis=0)
        dos_all = jnp.concatenate(dos, axis=0)
        lt_all = jnp.concatenate(lts, axis=0)

        R = 2 * nsub * Bq
        pc_s[...] = jnp.zeros_like(pc_s)
        pe_s[...] = jnp.zeros_like(pe_s)
        lt_s[...] = lt_all

        def scores(J, slot):
            off = pl.multiple_of(J * SB_KEYS, SB_KEYS)
            z_s[slot] = lax.dot_general(qs_all, k_ref[pl.ds(off, SB_KEYS), :], _NT, preferred_element_type=F32)
            da_s[slot] = lax.dot_general(dos_all, v_ref[pl.ds(off, SB_KEYS), :], _NT, preferred_element_type=F32)

        def elementwise(slot):
            for r in range(R // SB_STRIP):
                rows = slice(r * SB_STRIP, (r + 1) * SB_STRIP)
                pc, pe, Lt = pc_s[rows, :], pe_s[rows, :], lt_s[rows, :]
                for b in range(SB_KEYS // SB_SCAN):
                    cols = slice(b * SB_SCAN, (b + 1) * SB_SCAN)
                    A, dz, pc, pe = _sba_sub_bwd(z_s[slot, rows, cols], da_s[slot, rows, cols], Lt, pc, pe,
                                                 Uincl2, Uexcl2, None)
                    a_s[slot, rows, cols] = A
                    dz_s[slot, rows, cols] = dz
                pc_s[rows, :] = pc
                pe_s[rows, :] = pe

        def outputs(J, slot, dq_acc):
            off = pl.multiple_of(J * SB_KEYS, SB_KEYS)
            dzt = dz_s[slot]
            dk_acc[pl.ds(off, SB_KEYS), :] += lax.dot_general(dzt, qs_all, _TN, preferred_element_type=F32)
            dv_acc[pl.ds(off, SB_KEYS), :] += lax.dot_general(a_s[slot], dos_all, _TN, preferred_element_type=F32)
            return dq_acc + jnp.dot(dzt, k_ref[pl.ds(off, SB_KEYS), :], preferred_element_type=F32)

        a_s[1] = jnp.zeros((R, SB_KEYS), BF16)
        dz_s[1] = jnp.zeros((R, SB_KEYS), BF16)
        last = jnp.maximum(i - 1, 0)
        scores(0, 0)

        def step(J, dq_acc):
            slot = lax.rem(J, 2)
            elementwise(slot)
            scores(jnp.minimum(J + 1, last), 1 - slot)
            return outputs(jnp.maximum(J - 1, 0), 1 - slot, dq_acc)

        dq_acc = lax.fori_loop(0, i, step, jnp.zeros((R, 128), F32))
        dq_acc = outputs(last, lax.rem(i + 1, 2), dq_acc)
        pc, pe = pc_s[...], pe_s[...]
        for a in range(nsub):
            rows = slice(2 * a * Bq, 2 * (a + 1) * Bq)
            pca, pea, dqa = pc[rows], pe[rows], dq_acc[rows]
            for b in range(a + 1):
                off = pl.multiple_of(i * SB_KEYS + b * Bq, Bq)
                kb = k_ref[pl.ds(off, Bq), :]
                zb = lax.dot_general(qs[a], kb, _NT, preferred_element_type=F32)
                dAb = lax.dot_general(dos[a], v_ref[pl.ds(off, Bq), :], _NT, preferred_element_type=F32)
                A, dz, pca, pea = _sba_sub_bwd(zb, dAb, lts[a], pca, pea, Uincl1, Uexcl1, dmask if b == a else None)
                dqa = dqa + jnp.dot(dz, kb, preferred_element_type=F32)
                dk_acc[pl.ds(off, Bq), :] += lax.dot_general(dz, qs[a], _TN, preferred_element_type=F32)
                dv_acc[pl.ds(off, Bq), :] += lax.dot_general(A, dos[a], _TN, preferred_element_type=F32)
            dq_ref[a * Bq:(a + 1) * Bq, :] = (_unstack_heads(dqa) * scale).astype(BF16)

        @pl.when(i == nq - 1)
        def _():
            dk_ref[...] = dk_acc[...].astype(BF16)
            dv_ref[...] = dv_acc[...].astype(BF16)

    return pl.pallas_call(
        body, name=name, grid=(SB_HEADS // 2, nq),
        in_specs=[pl.BlockSpec((SB_KEYS, 128), lambda p, i: (i, p)), pl.BlockSpec((T, 128), lambda p, i: (0, p)),
                  pl.BlockSpec((T, 128), lambda p, i: (0, p + SB_HEADS // 2)),
                  pl.BlockSpec((None, SB_KEYS, 128), lambda p, i: (p, i, 0)),
                  pl.BlockSpec((SB_KEYS, 128), lambda p, i: (i, p))],
        out_specs=[pl.BlockSpec((SB_KEYS, 128), lambda p, i: (i, p)), pl.BlockSpec((T, 128), lambda p, i: (0, p)),
                   pl.BlockSpec((T, 128), lambda p, i: (0, p))],
        out_shape=[jax.ShapeDtypeStruct((T, D_MODEL), BF16), jax.ShapeDtypeStruct((T, D_MODEL), BF16),
                   jax.ShapeDtypeStruct((T, D_MODEL), BF16)],
        scratch_shapes=[pltpu.VMEM((T, 128), F32), pltpu.VMEM((T, 128), F32),
                        pltpu.VMEM((2, 2 * SB_KEYS, SB_KEYS), F32), pltpu.VMEM((2, 2 * SB_KEYS, SB_KEYS), F32),
                        pltpu.VMEM((2, 2 * SB_KEYS, SB_KEYS), BF16), pltpu.VMEM((2, 2 * SB_KEYS, SB_KEYS), BF16),
                        pltpu.VMEM((2 * SB_KEYS, 1), F32), pltpu.VMEM((2 * SB_KEYS, 1), F32),
                        pltpu.VMEM((2 * SB_KEYS, 1), F32)],
        compiler_params=_cparams(("parallel", "arbitrary")))(q, kv, kv, lt, do)


def _loss_head(h, tgt, w, *, name, tt=512):
    T, D = h.shape
    tt = min(tt, T)

    def body(h_ref, t_ref, w_ref, loss_ref, dh_ref, dw_ref):
        i = pl.program_id(0)
        hv = h_ref[...]
        wv = w_ref[...]
        r = lax.rsqrt(jnp.mean(hv * hv, axis=-1, keepdims=True) + EPS)
        xhat = hv * r
        err = xhat * wv - t_ref[...]
        part = 0.5 * jnp.sum(jnp.mean(err * err, axis=-1, keepdims=True), axis=0, keepdims=True)
        dy = err * (1.0 / D)
        dxh = dy * wv
        dh_ref[...] = r * (dxh - xhat * jnp.mean(dxh * xhat, axis=-1, keepdims=True))
        dwc = jnp.sum(dy * xhat, axis=0, keepdims=True)

        @pl.when(i == 0)
        def _():
            loss_ref[...] = jnp.broadcast_to(part, loss_ref.shape)
            dw_ref[...] = dwc

        @pl.when(i > 0)
        def _():
            loss_ref[...] += jnp.broadcast_to(part, loss_ref.shape)
            dw_ref[...] += dwc

    return pl.pallas_call(
        body, name=name, grid=(T // tt,),
        in_specs=[pl.BlockSpec((tt, D), lambda i: (i, 0)), pl.BlockSpec((tt, D), lambda i: (i, 0)),
                  pl.BlockSpec((1, D), lambda i: (0, 0))],
        out_specs=[pl.BlockSpec((1, 128), lambda i: (0, 0)), pl.BlockSpec((tt, D), lambda i: (i, 0)),
                   pl.BlockSpec((1, D), lambda i: (0, 0))],
        out_shape=[jax.ShapeDtypeStruct((1, 128), F32), jax.ShapeDtypeStruct((T, D), F32),
                   jax.ShapeDtypeStruct((1, D), F32)],
        compiler_params=_cparams(("arbitrary",)))(h, tgt, w.reshape(1, D))


def _adamw(parts, w, m, v, *, name, tr=256):
    P, R, C = parts.shape
    tr = min(tr, R)
    assert R % tr == 0, (name, R, tr)
    c1 = 1.0 - ADAM_B1 ** ADAM_STEP
    c2 = 1.0 - ADAM_B2 ** ADAM_STEP

    def body(p_ref, w_ref, m_ref, v_ref, g_ref, d_ref, nm_ref, nv_ref):
        g = p_ref[0].astype(F32)
        for k in range(1, P):
            g = g + p_ref[k].astype(F32)
        mn = ADAM_B1 * m_ref[...] + (1.0 - ADAM_B1) * g
        vn = ADAM_B2 * v_ref[...] + (1.0 - ADAM_B2) * (g * g)
        g_ref[...] = g
        nm_ref[...] = mn
        nv_ref[...] = vn
        d_ref[...] = -ADAM_LR * ((mn / c1) / (jnp.sqrt(vn / c2) + ADAM_EPS) + ADAM_WD * w_ref[...])

    spec = pl.BlockSpec((tr, C), lambda i: (i, 0))
    sds = jax.ShapeDtypeStruct((R, C), F32)
    return pl.pallas_call(
        body, name=name, grid=(R // tr,),
        in_specs=[pl.BlockSpec((P, tr, C), lambda i: (0, i, 0)), spec, spec, spec],
        out_specs=[spec, spec, spec, spec], out_shape=[sds, sds, sds, sds],
        compiler_params=_cparams(("parallel",)))(parts, w, m, v)


def _all_gather(shards, *, name):
    n = len(shards)

    def body(*refs):
        ins, outs = refs[:n], refs[n:2 * n]
        send_sems, recv_sems, local_sems = refs[2 * n:]
        x, y, c = lax.axis_index("x"), lax.axis_index("y"), lax.axis_index("c")
        me, sib = (x, y, c), (x, y, 1 - c)
        chips = [(1 - x, y), (x, 1 - y), (1 - x, 1 - y)]

        def slot(p):
            return 4 * p[0] + 2 * p[1] + p[2]

        def cp(a, k, block, to, src=None):
            dst = outs[a].at[slot(block)]
            return pltpu.make_async_remote_copy(src_ref=dst if src is None else src, dst_ref=dst,
                                                send_sem=send_sems.at[a, k], recv_sem=recv_sems.at[a, k],
                                                device_id=to, device_id_type=_MESH)

        mine = [pltpu.make_async_copy(ins[a], outs[a].at[slot(me)], local_sems.at[a]) for a in range(n)]
        for m in mine:
            m.start()
        first = []
        for a in range(n):
            first.append(cp(a, 0, me, sib, src=ins[a]))
            for j, chip in enumerate(chips):
                first.append(cp(a, 1 + j, me, (*chip, c), src=ins[a]))
        for f in first:
            f.start()
        passed = []
        for j, chip in enumerate(chips):
            for a in range(n):
                cp(a, 1 + j, (*chip, c), me).wait_recv()
                f = cp(a, 4 + j, (*chip, c), sib)
                f.start()
                passed.append(f)
        for a in range(n):
            cp(a, 0, sib, me).wait_recv()
            for j, chip in enumerate(chips):
                cp(a, 4 + j, (*chip, 1 - c), me).wait_recv()
        for f in first + passed:
            f.wait_send()
        for m in mine:
            m.wait()

    return pl.pallas_call(
        body, name=name, in_specs=[_ANY] * n, out_specs=[_ANY] * n,
        out_shape=[jax.ShapeDtypeStruct((N_DEV,) + s.shape, s.dtype) for s in shards],
        scratch_shapes=[pltpu.SemaphoreType.DMA((n, 7)), pltpu.SemaphoreType.DMA((n, 7)),
                        pltpu.SemaphoreType.DMA((n,))])(*shards)


def _exchange(blocks, *, name):
    n = len(blocks)

    def body(*refs):
        ins, outs = refs[:n], refs[n:2 * n]
        send_sems, recv_sems, local_sems = refs[2 * n:]
        x, y, c = lax.axis_index("x"), lax.axis_index("y"), lax.axis_index("c")
        me = 4 * x + 2 * y + c
        mine = [pltpu.make_async_copy(ins[a].at[me], outs[a].at[me], local_sems.at[a]) for a in range(n)]
        for m in mine:
            m.start()
        copies = []
        for r in range(1, N_DEV):
            rx, ry, rc = (r >> 2) & 1, (r >> 1) & 1, r & 1
            px, py, pc = (1 - x if rx else x), (1 - y if ry else y), (1 - c if rc else c)
            peer = 4 * px + 2 * py + pc
            for a in range(n):
                copies.append((pltpu.make_async_remote_copy(
                    src_ref=ins[a].at[peer], dst_ref=outs[a].at[me], send_sem=send_sems.at[a, r - 1],
                    recv_sem=recv_sems.at[a, r - 1], device_id=(px, py, pc), device_id_type=_MESH),
                    pltpu.make_async_remote_copy(
                    src_ref=ins[a].at[peer], dst_ref=outs[a].at[peer], send_sem=send_sems.at[a, r - 1],
                    recv_sem=recv_sems.at[a, r - 1], device_id=(px, py, pc), device_id_type=_MESH)))
        for snd, _ in copies:
            snd.start()
        for _, rcv in copies:
            rcv.wait_recv()
        for snd, _ in copies:
            snd.wait_send()
        for m in mine:
            m.wait()

    return pl.pallas_call(
        body, name=name, in_specs=[_ANY] * n, out_specs=[_ANY] * n,
        out_shape=[jax.ShapeDtypeStruct(b.shape, b.dtype) for b in blocks],
        scratch_shapes=[pltpu.SemaphoreType.DMA((n, 7)), pltpu.SemaphoreType.DMA((n, 7)),
                        pltpu.SemaphoreType.DMA((n,))])(*blocks)


_HBM = pl.BlockSpec(memory_space=pltpu.HBM)
_SEM = pl.BlockSpec(memory_space=pltpu.SEMAPHORE)
_EFFECT = pltpu.SideEffectType.DATAFLOW_SIDE_EFFECTING


def _peers():
    x, y, c = lax.axis_index("x"), lax.axis_index("y"), lax.axis_index("c")
    out = []
    for r in range(1, N_DEV):
        px = 1 - x if (r >> 2) & 1 else x
        py = 1 - y if (r >> 1) & 1 else y
        pc = 1 - c if r & 1 else c
        out.append(((px, py, pc), 4 * px + 2 * py + pc))
    return 4 * x + 2 * y + c, out


def _push_copy(src_ref, land_ref, send_sems, recv_sems, a, k, me, peer, peer_slot, scatter, arriving):
    src = src_ref.at[peer_slot] if scatter else src_ref
    return pltpu.make_async_remote_copy(
        src_ref=src, dst_ref=land_ref.at[peer_slot if arriving else me], send_sem=send_sems.at[a * (N_DEV - 1) + k],
        recv_sem=recv_sems.at[a * (N_DEV - 1) + k], device_id=peer, device_id_type=_MESH)


def _push_start(srcs, *, scatter, name):
    n = len(srcs)
    lands = [lax.empty(s.shape if scatter else (N_DEV,) + s.shape, s.dtype) for s in srcs]

    def body(*refs):
        src_refs, land_refs = refs[:n], refs[n:2 * n]
        send_sems, recv_sems = refs[2 * n], refs[2 * n + 1]
        token = refs[-1]
        me, peers = _peers()
        for k, (peer, slot) in enumerate(peers):
            for a in range(n):
                _push_copy(src_refs[a], land_refs[a], send_sems, recv_sems, a, k, me, peer, slot, scatter, False).start()
        token[...] = jnp.zeros_like(token)

    hbm = lambda a: pltpu.HBM(a.shape, a.dtype)
    outs = pl.pallas_call(
        body, name=name,
        out_shape=(pltpu.SemaphoreType.DMA((n * (N_DEV - 1),)), pltpu.SemaphoreType.DMA((n * (N_DEV - 1),)),
                   *[hbm(s) for s in srcs], *[hbm(l) for l in lands], jax.ShapeDtypeStruct((8, 128), F32)),
        in_specs=[_HBM] * (2 * n),
        out_specs=(_SEM, _SEM, *([_HBM] * (2 * n)), pl.BlockSpec(memory_space=pltpu.VMEM)),
        input_output_aliases={i: 2 + i for i in range(2 * n)},
        compiler_params=pltpu.CompilerParams(has_side_effects=_EFFECT),
    )(*[pltpu.with_memory_space_constraint(s, pltpu.HBM) for s in srcs],
      *[pltpu.with_memory_space_constraint(l, pltpu.HBM) for l in lands])
    return dict(send=outs[0], recv=outs[1], srcs=list(outs[2:2 + n]), lands=list(outs[2 + n:2 + 2 * n]),
                token=outs[-1], scatter=scatter, n=n)


def _push_wait(h, after, *, name):
    n, scatter = h["n"], h["scatter"]

    def body(*refs):
        src_refs, land_refs = refs[:n], refs[n:2 * n]
        send_sems, recv_sems = refs[2 * n], refs[2 * n + 1]
        me, peers = _peers()
        for k, (peer, slot) in enumerate(peers):
            for a in range(n):
                cp = _push_copy(src_refs[a], land_refs[a], send_sems, recv_sems, a, k, me, peer, slot, scatter, True)
                cp.wait_send()
                cp.wait_recv()

    hbm = lambda a: pltpu.HBM(a.shape, a.dtype)
    outs = pl.pallas_call(
        body, name=name,
        out_shape=(*[hbm(s) for s in h["srcs"]], *[hbm(l) for l in h["lands"]]),
        in_specs=[_HBM] * (2 * n) + [_SEM, _SEM, _ANY], out_specs=tuple([_HBM] * (2 * n)),
        input_output_aliases={i: i for i in range(2 * n)},
        compiler_params=pltpu.CompilerParams(has_side_effects=_EFFECT),
    )(*h["srcs"], *h["lands"], h["send"], h["recv"], after)
    return list(outs[:n]), list(outs[n:])


def _ffn_fwd(h, nw, w_up, conv_w, conv_b, w_down, tag):
    a3 = _mm_fwd(h, w_up, norm_w=nw, name=f"ffn{tag}_up", out_dtype=BF16, halves=True, tm=1024, tn=1408)
    p = _ffn_conv_fwd3(a3, conv_w, conv_b.reshape(1, -1), name=f"ffn{tag}_conv")
    h_out = _mm_fwd(p, w_down, residual=h, name=f"ffn{tag}_down", tm=1024, tn=512)
    return h_out, (a3, p)


def _ffn_bwd(dh, h, saved, nw, w_up, conv_w, conv_b, w_down, tag):
    a3, p = saved
    g_down = _mm_tn(p, dh, name=f"ffn{tag}_down_wg", tk1=1408, tn=1024)
    dp = _mm_nt(dh, w_down, name=f"ffn{tag}_down_dg", out_dtype=BF16, tm=1024, tn=1408, tk=1024)
    dhid3, dw3, db3 = _ffn_conv_bwd3(a3, conv_w, conv_b.reshape(1, -1), dp, name=f"ffn{tag}_conv_bwd")
    da3 = _conv_bwd_in3(dhid3, conv_w, K=FFN_CONV, name=f"ffn{tag}_conv_bwd_in")
    g_up = _mm_tn(h, da3, norm_w=nw, name=f"ffn{tag}_up_wg", tn=1408)
    dh_out, g_nw = _mm_nt(da3, w_up, epi=(h, nw, dh), name=f"ffn{tag}_up_dg", tk=1408)
    g_cw = jnp.concatenate([dw3[0], dw3[1]], axis=1)
    g_cb = jnp.concatenate([db3[0], db3[1]], axis=1)
    return dh_out, dict(norm=g_nw.reshape(-1), up=g_up, conv_w=g_cw, conv_b=g_cb.reshape(-1), down=g_down)


def _local_step(x, tgt, W):
    T = x.shape[0]
    f = {}
    zx = _mm_fwd(x, W["in_w"], norm_w=W["ssm_norm_w"], name="ssm_in", tm=1024, tn=896)
    xbc_c = _ssm_conv_fwd(zx, W["ssm_conv_w"], W["ssm_conv_b"].reshape(1, -1), name="ssm_conv")
    dt_raw = zx[:, D_INNER + CONV_DIM:IN_PROJ_DIM]
    dtg = jnp.pad(dt_raw.reshape(T, SSM_GROUPS, 8).transpose(1, 0, 2), ((0, 0), (0, 0), (0, 120)))
    par = jnp.stack([W["ssm_dt_bias"].reshape(SSM_GROUPS, 8), W["ssm_a_log"].reshape(SSM_GROUPS, 8),
                     W["ssm_d"].reshape(SSM_GROUPS, 8)], axis=1)
    par = jnp.pad(par, ((0, 0), (0, 5), (0, 120)))
    gnw = W["ssm_gate_norm_w"].reshape(1, D_INNER)
    y, yn, st = _ssd_fwd(xbc_c, zx, dtg, par, gnw, name="ssd_fwd")
    h1 = _mm_fwd(yn, W["ssm_out_w"], residual=x, name="ssm_out", tm=1024, tn=512)
    h2, ffn0 = _ffn_fwd(h1, W["ffn_norm_w"][0], W["ffn_up_w"][0], W["ffn_conv_w"][0], W["ffn_conv_b"][0],
                        W["ffn_down_w"][0], "0")
    q = _mm_fwd(h2, W["w_q"], norm_w=W["attn_norm_w"], out_dtype=BF16, name="attn_q", tm=1024, tn=1024)
    kv = _mm_fwd(h2, W["w_kv"], norm_w=W["kv_norm_w"], out_dtype=BF16, name="attn_kv", tm=1024, tn=1024)
    o, lt = _sba_fwd(q, kv, name="sba_fwd")
    h3 = _mm_fwd(o, W["w_o"], residual=h2, name="attn_o", tm=1024, tn=512)
    h4, ffn1 = _ffn_fwd(h3, W["ffn_norm_w"][1], W["ffn_up_w"][1], W["ffn_conv_w"][1], W["ffn_conv_b"][1],
                        W["ffn_down_w"][1], "1")
    loss, dh4, g_final = _loss_head(h4, tgt, W["final_norm_w"], name="loss_head")
    dh3, gf1 = _ffn_bwd(dh4, h3, ffn1, W["ffn_norm_w"][1], W["ffn_up_w"][1], W["ffn_conv_w"][1], W["ffn_conv_b"][1],
                        W["ffn_down_w"][1], "1")
    g_wo = _mm_tn(o, dh3, name="attn_o_wg", tn=1024)
    do = _mm_nt(dh3, W["w_o"], name="attn_o_dg", out_dtype=BF16, tn=1024, tk=1024)
    dq, dk, dv = _sba_bwd(q, kv, lt, do, name="sba_bwd")
    g_wq = _mm_tn(h2, dq, norm_w=W["attn_norm_w"], name="attn_q_wg", tn=1024)
    dh2a, g_attn_nw = _mm_nt(dq, W["w_q"], epi=(h2, W["attn_norm_w"], dh3), name="attn_q_dg", tk=1024)
    dkv = jnp.concatenate([dk, dv], axis=1)
    g_wkv = _mm_tn(h2, dkv, norm_w=W["kv_norm_w"], name="attn_kv_wg", tn=1024)
    dh2, g_kv_nw = _mm_nt(dkv, W["w_kv"], epi=(h2, W["kv_norm_w"], dh2a), name="attn_kv_dg", tk=1024)
    dh1, gf0 = _ffn_bwd(dh2, h1, ffn0, W["ffn_norm_w"][0], W["ffn_up_w"][0], W["ffn_conv_w"][0], W["ffn_conv_b"][0],
                        W["ffn_down_w"][0], "0")
    g_out = _mm_tn(yn, dh1, name="ssm_out_wg", tn=1024)
    dyn = _mm_nt(dh1, W["ssm_out_w"], name="ssm_out_dg", out_dtype=BF16, tn=1024, tk=1024)
    dxs, dB, dC, dz, ddt, g_gnw, dpar = _ssd_bwd(xbc_c, zx, dtg, par, gnw, y, st, dyn, name="ssd_bwd")
    dxbc_c = jnp.concatenate([dxs, dB, dC], axis=1)
    dhid, g_scw, g_scb = _ssm_conv_bwd_pre(zx, W["ssm_conv_w"], W["ssm_conv_b"].reshape(1, -1), dxbc_c,
                                           name="ssm_conv_bwd")
    dxbc = _conv_bwd_in(dhid, W["ssm_conv_w"], K=SSM_CONV, name="ssm_conv_bwd_in")
    ddt_t = ddt[:, :, :8].transpose(1, 0, 2).reshape(T, SSM_HEADS).astype(BF16)
    dzx = jnp.concatenate([dz, dxbc, jnp.pad(ddt_t, ((0, 0), (0, IN_PROJ_PAD - IN_PROJ_DIM)))], axis=1)
    g_in = _mm_tn(x, dzx, norm_w=W["ssm_norm_w"], name="ssm_in_wg", tn=896)
    dx, g_ssm_nw = _mm_nt(dzx, W["in_w"], epi=(x, W["ssm_norm_w"], dh1), name="ssm_in_dg", tk=1792)
    f["ssm_norm_w"] = g_ssm_nw.reshape(-1)
    f["ssm_in_w"] = g_in[:, :IN_PROJ_DIM]
    f["ssm_conv_w"] = g_scw
    f["ssm_conv_b"] = g_scb.reshape(-1)
    f["ssm_dt_bias"] = dpar[:, 0, :8].reshape(-1)
    f["ssm_a_log"] = dpar[:, 1, :8].reshape(-1)
    f["ssm_d"] = dpar[:, 2, :8].reshape(-1)
    f["ssm_gate_norm_w"] = g_gnw.reshape(-1)
    f["ssm_out_w"] = g_out
    f["kv_norm_w"] = g_kv_nw.reshape(-1)
    f["w_k"] = g_wkv[:, :D_MODEL]
    f["w_v"] = g_wkv[:, D_MODEL:]
    f["attn_norm_w"] = g_attn_nw.reshape(-1)
    f["w_q"] = g_wq
    f["w_o"] = g_wo
    f["ffn_norm_w"] = jnp.stack([gf0["norm"], gf1["norm"]])
    f["ffn_up_w"] = [gf0["up"], gf1["up"]]
    f["ffn_conv_w"] = jnp.stack([gf0["conv_w"], gf1["conv_w"]])
    f["ffn_conv_b"] = jnp.stack([gf0["conv_b"], gf1["conv_b"]])
    f["ffn_down_w"] = [gf0["down"], gf1["down"]]
    f["final_norm_w"] = g_final.reshape(-1)
    return loss, dx, f


_BIG = ["ssm_in_w", "ssm_out_w", "w_k", "w_v", "w_q", "w_o", "ffn_up_w", "ffn_down_w"]
_SMALL_SHARDED = ["ssm_norm_w", "ssm_conv_w", "ssm_conv_b", "ssm_gate_norm_w", "ffn_conv_w"]
_SMALL_REPL = ["ssm_dt_bias", "ssm_a_log", "ssm_d", "kv_norm_w", "attn_norm_w", "ffn_norm_w", "ffn_conv_b",
               "final_norm_w"]
_WEIGHTS = ["ssm_norm_w", "ssm_in_w", "ssm_conv_w", "ssm_conv_b", "ssm_dt_bias", "ssm_a_log", "ssm_d",
            "ssm_gate_norm_w", "ssm_out_w", "kv_norm_w", "w_k", "w_v", "attn_norm_w", "w_q", "w_o", "ffn_norm_w",
            "ffn_up_w", "ffn_conv_w", "ffn_conv_b", "ffn_down_w", "final_norm_w"]


def _as2d(a):
    return a.reshape(-1, a.shape[-1])


def _cols_to_full(g):
    return g.transpose(1, 0, 2).reshape(g.shape[1], N_DEV * g.shape[2])


def _full_to_cols(a):
    R = a.shape[0]
    return a.reshape(R, N_DEV, -1).transpose(1, 0, 2)


def _gather_weights(p):
    names = _BIG + _SMALL_SHARDED
    shards = [_as2d(p[n]).astype(BF16) for n in _BIG] + [_as2d(p[n]) for n in _SMALL_SHARDED]
    got = dict(zip(names, _all_gather(shards, name="gather_weights")))
    W = {n: p[n] for n in _SMALL_REPL}
    in_w = _cols_to_full(got["ssm_in_w"])
    W["in_w"] = jnp.pad(in_w, ((0, 0), (0, IN_PROJ_PAD - IN_PROJ_DIM)))
    W["ssm_out_w"] = got["ssm_out_w"].reshape(D_INNER, D_MODEL)
    W["w_kv"] = jnp.concatenate([got["w_k"].reshape(D_MODEL, D_MODEL), got["w_v"].reshape(D_MODEL, D_MODEL)], axis=1)
    W["w_q"] = got["w_q"].reshape(D_MODEL, D_MODEL)
    W["w_o"] = got["w_o"].reshape(D_MODEL, D_MODEL)
    up = got["ffn_up_w"]
    W["ffn_up_w"] = [_cols_to_full(up[:, l * D_MODEL:(l + 1) * D_MODEL]) for l in range(2)]
    dn = got["ffn_down_w"]
    rs = D_FF // N_DEV
    W["ffn_down_w"] = [dn[:, l * rs:(l + 1) * rs].reshape(D_FF, D_MODEL) for l in range(2)]
    W["ssm_norm_w"] = got["ssm_norm_w"].reshape(D_MODEL)
    W["ssm_conv_w"] = _cols_to_full(got["ssm_conv_w"])
    W["ssm_conv_b"] = got["ssm_conv_b"].reshape(CONV_DIM)
    W["ssm_gate_norm_w"] = got["ssm_gate_norm_w"].reshape(D_INNER)
    fcw = _cols_to_full(got["ffn_conv_w"])
    W["ffn_conv_w"] = fcw.reshape(2, FFN_CONV, 2 * D_FF)
    for n in ("ssm_dt_bias", "ssm_a_log", "ssm_d", "attn_norm_w"):
        W[n] = W[n].reshape(-1)
    return W


def _big_grad_blocks(f):
    rs = D_FF // N_DEV
    return {
        "ssm_in_w": _full_to_cols(f["ssm_in_w"]),
        "ssm_out_w": f["ssm_out_w"].reshape(N_DEV, D_INNER // N_DEV, D_MODEL),
        "w_k": f["w_k"].reshape(N_DEV, D_MODEL // N_DEV, D_MODEL),
        "w_v": f["w_v"].reshape(N_DEV, D_MODEL // N_DEV, D_MODEL),
        "w_q": f["w_q"].reshape(N_DEV, D_MODEL // N_DEV, D_MODEL),
        "w_o": f["w_o"].reshape(N_DEV, D_MODEL // N_DEV, D_MODEL),
        "ffn_up_w": jnp.concatenate([_full_to_cols(g) for g in f["ffn_up_w"]], axis=1),
        "ffn_down_w": jnp.concatenate([g.reshape(N_DEV, rs, D_MODEL) for g in f["ffn_down_w"]], axis=1),
    }


def _pack_small(vals):
    flat = jnp.concatenate([v.reshape(-1).astype(F32) for v in vals])
    n = flat.shape[0]
    rows = -(-n // 1024) * 8
    return jnp.pad(flat, (0, rows * 128 - n)).reshape(rows, 128)


def _unpack_small(packed, shapes):
    flat = packed.reshape(-1)
    out, off = [], 0
    for s in shapes:
        n = math.prod(s)
        out.append(flat[off:off + n].reshape(s))
        off += n
    return out


def _kernel_v1(x, ssm_norm_w, ssm_in_w, ssm_conv_w, ssm_conv_b, ssm_dt_bias, ssm_a_log, ssm_d, ssm_gate_norm_w, ssm_out_w, kv_norm_w, w_k, w_v, attn_norm_w, w_q, w_o, ffn_norm_w, ffn_up_w, ffn_conv_w, ffn_conv_b, ffn_down_w, final_norm_w, loss_target, m_ssm_norm_w, m_ssm_in_w, m_ssm_conv_w, m_ssm_conv_b, m_ssm_dt_bias, m_ssm_a_log, m_ssm_d, m_ssm_gate_norm_w, m_ssm_out_w, m_kv_norm_w, m_w_k, m_w_v, m_attn_norm_w, m_w_q, m_w_o, m_ffn_norm_w, m_ffn_up_w, m_ffn_conv_w, m_ffn_conv_b, m_ffn_down_w, m_final_norm_w, v_ssm_norm_w, v_ssm_in_w, v_ssm_conv_w, v_ssm_conv_b, v_ssm_dt_bias, v_ssm_a_log, v_ssm_d, v_ssm_gate_norm_w, v_ssm_out_w, v_kv_norm_w, v_w_k, v_w_v, v_attn_norm_w, v_w_q, v_w_o, v_ffn_norm_w, v_ffn_up_w, v_ffn_conv_w, v_ffn_conv_b, v_ffn_down_w, v_final_norm_w):
    env = dict(locals())
    p = {n: env[n] for n in _WEIGHTS}
    mom = {n: env["m_" + n] for n in _WEIGHTS}
    var = {n: env["v_" + n] for n in _WEIGHTS}
    T = x.shape[1]
    me = 4 * lax.axis_index("x") + 2 * lax.axis_index("y") + lax.axis_index("c")

    W = _gather_weights(p)
    loss_row, dx, f = _local_step(x.reshape(T, D_MODEL), loss_target.reshape(T, D_MODEL), W)
    loss = lax.psum(loss_row[0, 0], ("x", "y", "c"))

    big = _big_grad_blocks(f)
    small_names = _SMALL_REPL + _SMALL_SHARDED
    small_full = _pack_small([f[n] for n in small_names])
    small_bcast = jnp.broadcast_to(small_full[None], (N_DEV,) + small_full.shape)
    got = _exchange([big[n] for n in _BIG] + [small_bcast], name="exchange_grads")
    big_parts = dict(zip(_BIG, got[:-1]))

    zero = jnp.zeros_like(small_full)
    g_small_sum = _adamw(got[-1], zero, zero, zero, name="sum_small_grads", tr=small_full.shape[0])[0]
    full_shapes = [f[n].shape for n in small_names]
    g_small = dict(zip(small_names, _unpack_small(g_small_sum, full_shapes)))
    for n in _SMALL_SHARDED:
        width = p[n].shape[-1]
        g_small[n] = lax.dynamic_slice_in_dim(g_small[n], me * width, width, axis=g_small[n].ndim - 1)

    out_g, out_d, out_m, out_v = {}, {}, {}, {}
    for n in _BIG:
        w2, m2, v2 = _as2d(p[n]), _as2d(mom[n]), _as2d(var[n])
        tr = 352 if n == "ffn_down_w" else 256
        g, d, nm, nv = _adamw(big_parts[n], w2, m2, v2, name="adamw_" + n, tr=tr)
        out_g[n], out_d[n], out_m[n], out_v[n] = (t.reshape(p[n].shape) for t in (g, d, nm, nv))
    sw = _pack_small([p[n] for n in small_names])
    sm = _pack_small([mom[n] for n in small_names])
    sv = _pack_small([var[n] for n in small_names])
    sg = _pack_small([g_small[n] for n in small_names])
    _, d, nm, nv = _adamw(sg[None], sw, sm, sv, name="adamw_small", tr=sw.shape[0])
    shard_shapes = [p[n].shape for n in small_names]
    for n, dd, mm, vv in zip(small_names, _unpack_small(d, shard_shapes), _unpack_small(nm, shard_shapes),
                             _unpack_small(nv, shard_shapes)):
        out_g[n] = g_small[n].reshape(p[n].shape)
        out_d[n], out_m[n], out_v[n] = dd, mm, vv

    return (loss, dx.reshape(x.shape), *[out_g[n] for n in _WEIGHTS], *[out_d[n] for n in _WEIGHTS],
            *[out_m[n] for n in _WEIGHTS], *[out_v[n] for n in _WEIGHTS])


def _tie(a, token):
    return a + token[0, 0].astype(a.dtype)


def _local_step2(x, tgt, get_w, put_g):
    T = x.shape[0]
    Ws = get_w("ssm", None)
    fnw, fcw, fcb = Ws["ffn_norm_w"], Ws["ffn_conv_w"], Ws["ffn_conv_b"]
    zx = _mm_fwd(x, Ws["in_w"], norm_w=Ws["ssm_norm_w"], name="ssm_in", tm=1024, tn=896)
    xbc_c = _ssm_conv_fwd(zx, Ws["ssm_conv_w"], Ws["ssm_conv_b"].reshape(1, -1), name="ssm_conv")
    dt_raw = zx[:, D_INNER + CONV_DIM:IN_PROJ_DIM]
    dtg = jnp.pad(dt_raw.reshape(T, SSM_GROUPS, 8).transpose(1, 0, 2), ((0, 0), (0, 0), (0, 120)))
    par = jnp.stack([Ws["ssm_dt_bias"].reshape(SSM_GROUPS, 8), Ws["ssm_a_log"].reshape(SSM_GROUPS, 8),
                     Ws["ssm_d"].reshape(SSM_GROUPS, 8)], axis=1)
    par = jnp.pad(par, ((0, 0), (0, 5), (0, 120)))
    gnw = _tie(Ws["ssm_gate_norm_w"].reshape(1, D_INNER), get_w("rest_start", xbc_c))
    y, yn, st = _ssd_fwd(xbc_c, zx, dtg, par, gnw, name="ssd_fwd")
    W0 = get_w("ffn0", y)
    Ws["ssm_out_w"] = W0["ssm_out_w"]
    h1 = _mm_fwd(yn, Ws["ssm_out_w"], residual=x, name="ssm_out", tm=1024, tn=512)
    h2, ffn0 = _ffn_fwd(h1, fnw[0], W0["up"], fcw[0], fcb[0], W0["down"], "0")
    Wr = get_w("rest", h2)
    q = _mm_fwd(h2, Wr["w_q"], norm_w=Ws["attn_norm_w"], out_dtype=BF16, name="attn_q", tm=1024, tn=1024)
    kv = _mm_fwd(h2, Wr["w_kv"], norm_w=Ws["kv_norm_w"], out_dtype=BF16, name="attn_kv", tm=1024, tn=1024)
    o, lt = _sba_fwd(q, kv, name="sba_fwd")
    h3 = _mm_fwd(o, Wr["w_o"], residual=h2, name="attn_o", tm=1024, tn=512)
    h4, ffn1 = _ffn_fwd(h3, fnw[1], Wr["up"], fcw[1], fcb[1], Wr["down"], "1")
    loss, dh4, g_final = _loss_head(h4, tgt, Ws["final_norm_w"], name="loss_head")
    dh3, gf1 = _ffn_bwd(dh4, h3, ffn1, fnw[1], Wr["up"], fcw[1], fcb[1], Wr["down"], "1")
    tok = put_g("ffn1", dict(up=gf1["up"], down=gf1["down"]))
    g_wo = _mm_tn(o, dh3, name="attn_o_wg", tn=1024)
    do = _mm_nt(dh3, _tie(Wr["w_o"], tok), name="attn_o_dg", out_dtype=BF16, tn=1024, tk=1024)
    dq, dk, dv = _sba_bwd(q, kv, lt, do, name="sba_bwd")
    g_wq = _mm_tn(h2, dq, norm_w=Ws["attn_norm_w"], name="attn_q_wg", tn=1024)
    dh2a, g_attn_nw = _mm_nt(dq, Wr["w_q"], epi=(h2, Ws["attn_norm_w"], dh3), name="attn_q_dg", tk=1024)
    dkv = jnp.concatenate([dk, dv], axis=1)
    g_wkv = _mm_tn(h2, dkv, norm_w=Ws["kv_norm_w"], name="attn_kv_wg", tn=1024)
    dh2, g_kv_nw = _mm_nt(dkv, Wr["w_kv"], epi=(h2, Ws["kv_norm_w"], dh2a), name="attn_kv_dg", tk=1024)
    tok = put_g("attn", dict(w_o=g_wo, w_q=g_wq, w_k=g_wkv[:, :D_MODEL], w_v=g_wkv[:, D_MODEL:]))
    dh1, gf0 = _ffn_bwd(dh2, h1, ffn0, fnw[0], W0["up"], fcw[0], _tie(fcb[0], tok), W0["down"], "0")
    tok = put_g("ffn0", dict(up=gf0["up"], down=gf0["down"]))
    g_out = _mm_tn(yn, dh1, name="ssm_out_wg", tn=1024)
    dyn = _mm_nt(dh1, _tie(Ws["ssm_out_w"], tok), name="ssm_out_dg", out_dtype=BF16, tn=1024, tk=1024)
    tok = put_g("ssm_out", dict(ssm_out_w=g_out))
    dxs, dB, dC, dz, ddt, g_gnw, dpar = _ssd_bwd(xbc_c, zx, dtg, par, _tie(gnw, tok), y, st, dyn, name="ssd_bwd")
    dxbc_c = jnp.concatenate([dxs, dB, dC], axis=1)
    dhid, g_scw, g_scb = _ssm_conv_bwd_pre(zx, Ws["ssm_conv_w"], Ws["ssm_conv_b"].reshape(1, -1), dxbc_c,
                                           name="ssm_conv_bwd")
    dxbc = _conv_bwd_in(dhid, Ws["ssm_conv_w"], K=SSM_CONV, name="ssm_conv_bwd_in")
    ddt_t = ddt[:, :, :8].transpose(1, 0, 2).reshape(T, SSM_HEADS).astype(BF16)
    dzx = jnp.concatenate([dz, dxbc, jnp.pad(ddt_t, ((0, 0), (0, IN_PROJ_PAD - IN_PROJ_DIM)))], axis=1)
    g_in = _mm_tn(x, dzx, norm_w=Ws["ssm_norm_w"], name="ssm_in_wg", tn=896)
    tok = put_g("ssm_in", dict(ssm_in_w=g_in[:, :IN_PROJ_DIM]))
    dx, g_ssm_nw = _mm_nt(dzx, Ws["in_w"], epi=(x, _tie(Ws["ssm_norm_w"], tok), dh1), name="ssm_in_dg", tk=1792)
    f = {
        "ssm_norm_w": g_ssm_nw.reshape(-1), "ssm_conv_w": g_scw,
        "ssm_conv_b": g_scb.reshape(-1), "ssm_dt_bias": dpar[:, 0, :8].reshape(-1),
        "ssm_a_log": dpar[:, 1, :8].reshape(-1), "ssm_d": dpar[:, 2, :8].reshape(-1),
        "ssm_gate_norm_w": g_gnw.reshape(-1), "kv_norm_w": g_kv_nw.reshape(-1), "attn_norm_w": g_attn_nw.reshape(-1),
        "ffn_norm_w": jnp.stack([gf0["norm"], gf1["norm"]]), "ffn_conv_w": jnp.stack([gf0["conv_w"], gf1["conv_w"]]),
        "ffn_conv_b": jnp.stack([gf0["conv_b"], gf1["conv_b"]]), "final_norm_w": g_final.reshape(-1),
    }
    return loss, dx, f


def kernel(x, ssm_norm_w, ssm_in_w, ssm_conv_w, ssm_conv_b, ssm_dt_bias, ssm_a_log, ssm_d, ssm_gate_norm_w, ssm_out_w, kv_norm_w, w_k, w_v, attn_norm_w, w_q, w_o, ffn_norm_w, ffn_up_w, ffn_conv_w, ffn_conv_b, ffn_down_w, final_norm_w, loss_target, m_ssm_norm_w, m_ssm_in_w, m_ssm_conv_w, m_ssm_conv_b, m_ssm_dt_bias, m_ssm_a_log, m_ssm_d, m_ssm_gate_norm_w, m_ssm_out_w, m_kv_norm_w, m_w_k, m_w_v, m_attn_norm_w, m_w_q, m_w_o, m_ffn_norm_w, m_ffn_up_w, m_ffn_conv_w, m_ffn_conv_b, m_ffn_down_w, m_final_norm_w, v_ssm_norm_w, v_ssm_in_w, v_ssm_conv_w, v_ssm_conv_b, v_ssm_dt_bias, v_ssm_a_log, v_ssm_d, v_ssm_gate_norm_w, v_ssm_out_w, v_kv_norm_w, v_w_k, v_w_v, v_attn_norm_w, v_w_q, v_w_o, v_ffn_norm_w, v_ffn_up_w, v_ffn_conv_w, v_ffn_conv_b, v_ffn_down_w, v_final_norm_w):
    env = dict(locals())
    p = {n: env[n] for n in _WEIGHTS}
    mom = {n: env["m_" + n] for n in _WEIGHTS}
    var = {n: env["v_" + n] for n in _WEIGHTS}
    T = x.shape[1]
    me = 4 * lax.axis_index("x") + 2 * lax.axis_index("y") + lax.axis_index("c")
    rs = D_FF // N_DEV

    def bf2(a):
        return _as2d(a).astype(BF16)

    def with_own(srcs, lands, scatter):
        out = []
        for s, l in zip(srcs, lands):
            own = lax.dynamic_index_in_dim(s, me, 0, keepdims=False) if scatter else s
            out.append(lax.dynamic_update_index_in_dim(l, own, me, 0))
        return out

    a_names = ["ssm_in_w"] + _SMALL_SHARDED
    got_a = dict(zip(a_names, _all_gather([bf2(p["ssm_in_w"])] + [_as2d(p[n]) for n in _SMALL_SHARDED],
                                          name="gather_ssm")))
    ffn0_names = ["ssm_out_w", "up0", "down0"]
    rest_names = ["w_q", "w_k", "w_v", "w_o", "up1", "down1"]
    shard = {"up0": bf2(p["ffn_up_w"][0]), "down0": bf2(p["ffn_down_w"][0]), "up1": bf2(p["ffn_up_w"][1]),
             "down1": bf2(p["ffn_down_w"][1]), "w_q": bf2(p["w_q"]), "w_k": bf2(p["w_k"]), "w_v": bf2(p["w_v"]),
             "w_o": bf2(p["w_o"]), "ssm_out_w": bf2(p["ssm_out_w"])}
    h_ffn0 = _push_start([shard[n] for n in ffn0_names], scatter=False, name="gather_ffn0_start")
    handles = {}

    def get_w(group, after):
        if group == "ssm":
            W = {n: p[n] for n in _SMALL_REPL}
            for n in ("ssm_dt_bias", "ssm_a_log", "ssm_d", "attn_norm_w"):
                W[n] = W[n].reshape(-1)
            W["in_w"] = jnp.pad(_cols_to_full(got_a["ssm_in_w"]), ((0, 0), (0, IN_PROJ_PAD - IN_PROJ_DIM)))
            W["ssm_norm_w"] = _tie(got_a["ssm_norm_w"].reshape(D_MODEL), h_ffn0["token"])
            W["ssm_conv_w"] = _cols_to_full(got_a["ssm_conv_w"])
            W["ssm_conv_b"] = got_a["ssm_conv_b"].reshape(CONV_DIM)
            W["ssm_gate_norm_w"] = got_a["ssm_gate_norm_w"].reshape(D_INNER)
            W["ffn_conv_w"] = _cols_to_full(got_a["ffn_conv_w"]).reshape(2, FFN_CONV, 2 * D_FF)
            return W
        if group == "rest_start":
            anchor = after[0, 0]
            first = shard[rest_names[0]] + (jnp.where(jnp.isfinite(anchor), anchor, 0.0) * 0.0).astype(BF16)
            handles["rest"] = _push_start([first] + [shard[n] for n in rest_names[1:]], scatter=False,
                                          name="gather_rest_start")
            return handles["rest"]["token"]
        if group == "ffn0":
            srcs, lands = _push_wait(h_ffn0, after, name="gather_ffn0_wait")
            out, up, down = with_own(srcs, lands, False)
            return dict(ssm_out_w=out.reshape(D_INNER, D_MODEL), up=_cols_to_full(up), down=down.reshape(D_FF, D_MODEL))
        srcs, lands = _push_wait(handles["rest"], after, name="gather_rest_wait")
        g = dict(zip(rest_names, with_own(srcs, lands, False)))
        sq = lambda a: a.reshape(D_MODEL, D_MODEL)
        return dict(w_q=sq(g["w_q"]), w_kv=jnp.concatenate([sq(g["w_k"]), sq(g["w_v"])], axis=1), w_o=sq(g["w_o"]),
                    up=_cols_to_full(g["up1"]), down=g["down1"].reshape(D_FF, D_MODEL))

    pending = []

    def put_g(group, g):
        if group in ("ffn0", "ffn1"):
            keys = [("ffn_up_w", int(group[-1])), ("ffn_down_w", int(group[-1]))]
            blocks = [_full_to_cols(g["up"]), g["down"].reshape(N_DEV, rs, D_MODEL)]
        elif group == "attn":
            keys = [(n, None) for n in ("w_o", "w_q", "w_k", "w_v")]
            blocks = [g[n].reshape(N_DEV, D_MODEL // N_DEV, D_MODEL) for n, _ in keys]
        elif group == "ssm_out":
            keys = [("ssm_out_w", None)]
            blocks = [g["ssm_out_w"].reshape(N_DEV, D_INNER // N_DEV, D_MODEL)]
        else:
            keys = [("ssm_in_w", None)]
            blocks = [_full_to_cols(g["ssm_in_w"])]
        h = _push_start(blocks, scatter=True, name=f"exchange_{group}_start")
        pending.append((group, keys, h))
        return h["token"]

    loss_row, dx, f = _local_step2(x.reshape(T, D_MODEL), loss_target.reshape(T, D_MODEL), get_w, put_g)
    loss = lax.psum(loss_row[0, 0], ("x", "y", "c"))

    small_names = _SMALL_REPL + _SMALL_SHARDED
    small_full = _pack_small([f[n] for n in small_names])
    small_bcast = jnp.broadcast_to(small_full[None], (N_DEV,) + small_full.shape)
    h_small = _push_start([small_bcast], scatter=True, name="exchange_small_start")
    tok = h_small["token"]

    res = {}
    for group, keys, h in pending:
        srcs, lands = _push_wait(h, dx, name=f"exchange_{group}_wait")
        for (n, layer), parts in zip(keys, with_own(srcs, lands, True)):
            sel = (lambda a: a) if layer is None else (lambda a: a[layer])
            w2, m2, v2 = _as2d(sel(p[n])), _as2d(sel(mom[n])), _as2d(sel(var[n]))
            if not res:
                w2 = _tie(w2, tok)
            tr = rs if n == "ffn_down_w" else 256
            res[(n, layer)] = _adamw(parts, w2, m2, v2, name=f"adamw_{n}" + ("" if layer is None else str(layer)), tr=tr)
    srcs, lands = _push_wait(h_small, res[("ssm_in_w", None)][0], name="exchange_small_wait")
    small_parts = with_own(srcs, lands, True)[0]
    out_g, out_d, out_m, out_v = {}, {}, {}, {}
    for n in _BIG:
        if (n, None) in res:
            quad = res[(n, None)]
        else:
            quad = [jnp.stack([res[(n, 0)][k], res[(n, 1)][k]]) for k in range(4)]
        out_g[n], out_d[n], out_m[n], out_v[n] = (t.reshape(p[n].shape) for t in quad)

    zero = jnp.zeros_like(small_full)
    g_small_sum = _adamw(small_parts, zero, zero, zero, name="sum_small_grads", tr=small_full.shape[0])[0]
    g_small = dict(zip(small_names, _unpack_small(g_small_sum, [f[n].shape for n in small_names])))
    for n in _SMALL_SHARDED:
        width = p[n].shape[-1]
        g_small[n] = lax.dynamic_slice_in_dim(g_small[n], me * width, width, axis=g_small[n].ndim - 1)
    sw = _pack_small([p[n] for n in small_names])
    sm = _pack_small([mom[n] for n in small_names])
    sv = _pack_small([var[n] for n in small_names])
    sg = _pack_small([g_small[n] for n in small_names])
    _, d, nm, nv = _adamw(sg[None], sw, sm, sv, name="adamw_small", tr=sw.shape[0])
    shard_shapes = [p[n].shape for n in small_names]
    for n, dd, mm, vv in zip(small_names, _unpack_small(d, shard_shapes), _unpack_small(nm, shard_shapes),
                             _unpack_small(nv, shard_shapes)):
        out_g[n] = g_small[n].reshape(p[n].shape)
        out_d[n], out_m[n], out_v[n] = dd, mm, vv

    return (loss, dx.reshape(x.shape), *[out_g[n] for n in _WEIGHTS], *[out_d[n] for n in _WEIGHTS],
            *[out_m[n] for n in _WEIGHTS], *[out_v[n] for n in _WEIGHTS])
```

```python
import functools
import math

import jax
import jax.numpy as jnp
from jax import lax
from jax.experimental import pallas as pl
from jax.experimental.pallas import tpu as pltpu

F32 = jnp.float32
BF16 = jnp.bfloat16
EPS = 1e-6

D_MODEL = 1024
D_INNER = 2048
SSM_HEADS = 32
SSM_GROUPS = 4
SSM_STATE = 128
SSM_CONV = 4
SSM_CHUNK = 128
GN = SSM_GROUPS * SSM_STATE
CONV_DIM = D_INNER + 2 * GN
IN_PROJ_DIM = D_INNER + CONV_DIM + SSM_HEADS
IN_PROJ_PAD = 5376
SB_HEADS = 16
SB_HEAD_DIM = 64
SB_BLOCK = 128
D_FF = 2816
FFN_CONV = 3
N_DEV = 8

ADAM_LR = 0.001
ADAM_B1 = 0.9
ADAM_B2 = 0.999
ADAM_EPS = 1e-08
ADAM_WD = 0.01
ADAM_STEP = 10

_MESH = pl.DeviceIdType.MESH
_NT = (((1,), (1,)), ((), ()))
_TN = (((0,), (0,)), ((), ()))
_ANY = pl.BlockSpec(memory_space=pl.ANY)


def _cparams(sem, vmem_mb=48):
    return pltpu.CompilerParams(dimension_semantics=sem, vmem_limit_bytes=vmem_mb * 1024 * 1024)


def _sigmoid(x):
    return 1.0 / (1.0 + jnp.exp(-x))


def _softplus(x):
    return jnp.maximum(x, 0.0) + jnp.log(1.0 + jnp.exp(-jnp.abs(x)))


def _rms_fwd(xv, w):
    r = lax.rsqrt(jnp.mean(xv * xv, axis=-1, keepdims=True) + EPS)
    return xv * r * w


def _mm_fwd(x, w, *, name, norm_w=None, residual=None, out_dtype=F32, tm=512, tn=512, halves=False):
    M, K = x.shape
    N = w.shape[1]
    tm, tn = min(tm, M), min(tn, N)
    assert M % tm == 0 and N % tn == 0, (name, M, N, tm, tn)
    if halves:
        nbh = N // 2 // tn
        assert N // 2 % tn == 0
        out_spec = pl.BlockSpec((None, tm, tn), lambda i, j: (lax.div(j, nbh), i, lax.rem(j, nbh)))
        out_shape = jax.ShapeDtypeStruct((2, M, N // 2), out_dtype)
    else:
        out_spec = pl.BlockSpec((tm, tn), lambda i, j: (i, j))
        out_shape = jax.ShapeDtypeStruct((M, N), out_dtype)
    has_norm, has_res = norm_w is not None, residual is not None

    def body(*refs):
        x_ref, w_ref = refs[0], refs[1]
        p = 2
        nw_ref = r_ref = None
        if has_norm:
            nw_ref = refs[p]
            p += 1
        if has_res:
            r_ref = refs[p]
            p += 1
        o_ref, xn_ref = refs[p], refs[p + 1]

        @pl.when(pl.program_id(1) == 0)
        def _():
            xv = x_ref[...].astype(F32)
            if has_norm:
                xv = _rms_fwd(xv, nw_ref[...])
            xn_ref[...] = xv.astype(BF16)

        acc = jnp.dot(xn_ref[...], w_ref[...], preferred_element_type=F32)
        if has_res:
            acc = acc + r_ref[...]
        o_ref[...] = acc.astype(out_dtype)

    in_specs = [pl.BlockSpec((tm, K), lambda i, j: (i, 0)), pl.BlockSpec((K, tn), lambda i, j: (0, j))]
    args = [x, w]
    if has_norm:
        in_specs.append(pl.BlockSpec((1, K), lambda i, j: (0, 0)))
        args.append(norm_w.reshape(1, K))
    if has_res:
        in_specs.append(pl.BlockSpec((tm, tn), lambda i, j: (i, j)))
        args.append(residual)
    return pl.pallas_call(
        body, name=name, grid=(M // tm, N // tn), in_specs=in_specs,
        out_specs=out_spec, out_shape=out_shape,
        scratch_shapes=[pltpu.VMEM((tm, K), BF16)],
        compiler_params=_cparams(("parallel", "arbitrary")))(*args)


def _mm_nt(dy, w, *, name, epi=None, out_dtype=F32, tm=512, tn=512, tk=512):
    halves = dy.ndim == 3
    M, K = (dy.shape[1], 2 * dy.shape[2]) if halves else dy.shape
    N = w.shape[0]
    tm, tk = min(tm, M), min(tk, K)
    tn = N if epi is not None else min(tn, N)
    assert M % tm == 0 and N % tn == 0 and K % tk == 0, (name, M, N, K, tm, tn, tk)
    nk = K // tk
    has_epi = epi is not None

    def body(*refs):
        if has_epi:
            dy_ref, w_ref, h_ref, nw_ref, r_ref, o_ref, dnw_ref, acc_ref = refs
        else:
            dy_ref, w_ref, o_ref, acc_ref = refs
        i = pl.program_id(0)
        k = pl.program_id(2)

        @pl.when(k == 0)
        def _():
            acc_ref[...] = jnp.zeros_like(acc_ref)

        acc_ref[...] += lax.dot_general(dy_ref[...].astype(BF16), w_ref[...], _NT, preferred_element_type=F32)

        @pl.when(k == nk - 1)
        def _():
            du = acc_ref[...]
            if has_epi:
                hv = h_ref[...]
                r = lax.rsqrt(jnp.mean(hv * hv, axis=-1, keepdims=True) + EPS)
                xhat = hv * r
                dxh = du * nw_ref[...]
                dx = r * (dxh - xhat * jnp.mean(dxh * xhat, axis=-1, keepdims=True))
                o_ref[...] = (r_ref[...] + dx).astype(out_dtype)
                contrib = jnp.sum(du * xhat, axis=0, keepdims=True)

                @pl.when(i == 0)
                def _():
                    dnw_ref[...] = contrib

                @pl.when(i > 0)
                def _():
                    dnw_ref[...] += contrib
            else:
                o_ref[...] = du.astype(out_dtype)

    if halves:
        nkh = K // 2 // tk
        assert K // 2 % tk == 0
        dy_spec = pl.BlockSpec((None, tm, tk), lambda i, j, k: (lax.div(k, nkh), i, lax.rem(k, nkh)))
    else:
        dy_spec = pl.BlockSpec((tm, tk), lambda i, j, k: (i, k))
    in_specs = [dy_spec, pl.BlockSpec((tn, tk), lambda i, j, k: (j, k))]
    args = [dy, w]
    out_specs = [pl.BlockSpec((tm, tn), lambda i, j, k: (i, j))]
    out_shape = [jax.ShapeDtypeStruct((M, N), out_dtype)]
    if has_epi:
        h, nw, res = epi
        in_specs += [pl.BlockSpec((tm, N), lambda i, j, k: (i, 0)), pl.BlockSpec((1, N), lambda i, j, k: (0, 0)),
                     pl.BlockSpec((tm, N), lambda i, j, k: (i, 0))]
        args += [h, nw.reshape(1, N), res]
        out_specs.append(pl.BlockSpec((1, N), lambda i, j, k: (0, 0)))
        out_shape.append(jax.ShapeDtypeStruct((1, N), F32))
    outs = pl.pallas_call(
        body, name=name, grid=(M // tm, N // tn, nk), in_specs=in_specs, out_specs=out_specs, out_shape=out_shape,
        scratch_shapes=[pltpu.VMEM((tm, tn), F32)],
        compiler_params=_cparams(("arbitrary", "arbitrary", "arbitrary")))(*args)
    return (outs[0], outs[1]) if has_epi else outs[0]


def _mm_tn(x, dy, *, name, norm_w=None, out_dtype=BF16, tk1=1024, tn=512, tt=512):
    T, K1 = x.shape
    halves = dy.ndim == 3
    N = 2 * dy.shape[2] if halves else dy.shape[1]
    tk1, tn, tt = min(tk1, K1), min(tn, N), min(tt, T)
    has_norm = norm_w is not None
    assert K1 % tk1 == 0 and N % tn == 0 and T % tt == 0, (name, K1, N, T, tk1, tn, tt)
    assert not has_norm or tk1 == K1
    nt = T // tt

    def body(*refs):
        if has_norm:
            x_ref, dy_ref, nw_ref, o_ref, acc_ref = refs
        else:
            x_ref, dy_ref, o_ref, acc_ref = refs
        t = pl.program_id(2)

        @pl.when(t == 0)
        def _():
            acc_ref[...] = jnp.zeros_like(acc_ref)

        xv = x_ref[...]
        if has_norm:
            xv = _rms_fwd(xv.astype(F32), nw_ref[...])
        acc_ref[...] += lax.dot_general(xv.astype(BF16), dy_ref[...].astype(BF16), _TN, preferred_element_type=F32)

        @pl.when(t == nt - 1)
        def _():
            o_ref[...] = acc_ref[...].astype(out_dtype)

    if halves:
        nbh = N // 2 // tn
        assert N // 2 % tn == 0
        dy_spec = pl.BlockSpec((None, tt, tn), lambda a, b, t: (lax.div(b, nbh), t, lax.rem(b, nbh)))
    else:
        dy_spec = pl.BlockSpec((tt, tn), lambda a, b, t: (t, b))
    in_specs = [pl.BlockSpec((tt, tk1), lambda a, b, t: (t, a)), dy_spec]
    args = [x, dy]
    if has_norm:
        in_specs.append(pl.BlockSpec((1, K1), lambda a, b, t: (0, 0)))
        args.append(norm_w.reshape(1, K1))
    return pl.pallas_call(
        body, name=name, grid=(K1 // tk1, N // tn, nt), in_specs=in_specs,
        out_specs=pl.BlockSpec((tk1, tn), lambda a, b, t: (a, b)),
        out_shape=jax.ShapeDtypeStruct((K1, N), out_dtype),
        scratch_shapes=[pltpu.VMEM((tk1, tn), F32)],
        compiler_params=_cparams(("parallel", "parallel", "arbitrary")))(*args)


def _shift_down(xb, prev8, j):
    main = pltpu.roll(xb, j, 0)
    head = pltpu.roll(xb[0:8], j, 0)
    ph = pltpu.roll(prev8, j, 0)
    row8 = lax.broadcasted_iota(jnp.int32, head.shape, 0)
    head = jnp.where(row8 < j, ph, head)
    return jnp.concatenate([head, main[8:]], axis=0)


def _shift_up(xb, next8, j):
    tt = xb.shape[0]
    main = pltpu.roll(xb, tt - j, 0)
    tail = pltpu.roll(xb[tt - 8:tt], 8 - j, 0)
    nh = pltpu.roll(next8, 8 - j, 0)
    row8 = lax.broadcasted_iota(jnp.int32, tail.shape, 0)
    tail = jnp.where(row8 + j >= 8, nh, tail)
    return jnp.concatenate([main[:tt - 8], tail], axis=0)


def _conv_hid(xb, prev8, w, b_row, K):
    out = b_row
    shifted = []
    for j in range(K):
        sh = K - 1 - j
        xs = xb if sh == 0 else _shift_down(xb, prev8, sh)
        shifted.append(xs)
        out = out + xs * w[j:j + 1, :]
    return out, shifted


def _prev_idx(i, nb8):
    return jnp.maximum(i * nb8 - 1, 0)


def _ssm_conv_fwd(zx, w, b, *, name, tt=512, tc=512):
    T = zx.shape[0]
    tt = min(tt, T)
    C, K = CONV_DIM, SSM_CONV
    cb0, nb8 = D_INNER // tc, tt // 8

    def body(x_ref, p_ref, w_ref, b_ref, o_ref):
        first = (pl.program_id(1) > 0).astype(F32)
        hid, _ = _conv_hid(x_ref[...], p_ref[...] * first, w_ref[...], b_ref[...], K)
        o_ref[...] = hid * _sigmoid(hid)

    return pl.pallas_call(
        body, name=name, grid=(C // tc, T // tt),
        in_specs=[pl.BlockSpec((tt, tc), lambda c, i: (i, c + cb0)),
                  pl.BlockSpec((8, tc), lambda c, i: (_prev_idx(i, nb8), c + cb0)),
                  pl.BlockSpec((K, tc), lambda c, i: (0, c)), pl.BlockSpec((1, tc), lambda c, i: (0, c))],
        out_specs=pl.BlockSpec((tt, tc), lambda c, i: (i, c)),
        out_shape=jax.ShapeDtypeStruct((T, C), F32),
        compiler_params=_cparams(("parallel", "parallel")))(zx, zx, w, b)


def _ssm_conv_bwd_pre(zx, w, b, dout, *, name, tt=512, tc=512):
    T = zx.shape[0]
    tt = min(tt, T)
    C, K = CONV_DIM, SSM_CONV
    cb0, nb8 = D_INNER // tc, tt // 8

    def body(x_ref, p_ref, w_ref, b_ref, d_ref, dh_ref, dw_ref, db_ref):
        t = pl.program_id(1)
        first = (t > 0).astype(F32)
        hid, shifted = _conv_hid(x_ref[...], p_ref[...] * first, w_ref[...], b_ref[...], K)
        sg = _sigmoid(hid)
        dh = d_ref[...] * (sg * (1.0 + hid * (1.0 - sg)))
        dh_ref[...] = dh

        @pl.when(t == 0)
        def _():
            dw_ref[...] = jnp.zeros_like(dw_ref)
            db_ref[...] = jnp.zeros_like(db_ref)

        db_ref[...] += jnp.sum(dh, axis=0, keepdims=True)
        for j in range(K):
            dw_ref[j:j + 1, :] += jnp.sum(dh * shifted[j], axis=0, keepdims=True)

    return pl.pallas_call(
        body, name=name, grid=(C // tc, T // tt),
        in_specs=[pl.BlockSpec((tt, tc), lambda c, i: (i, c + cb0)),
                  pl.BlockSpec((8, tc), lambda c, i: (_prev_idx(i, nb8), c + cb0)),
                  pl.BlockSpec((K, tc), lambda c, i: (0, c)), pl.BlockSpec((1, tc), lambda c, i: (0, c)),
                  pl.BlockSpec((tt, tc), lambda c, i: (i, c))],
        out_specs=[pl.BlockSpec((tt, tc), lambda c, i: (i, c)), pl.BlockSpec((K, tc), lambda c, i: (0, c)),
                   pl.BlockSpec((1, tc), lambda c, i: (0, c))],
        out_shape=[jax.ShapeDtypeStruct((T, C), F32), jax.ShapeDtypeStruct((K, C), F32),
                   jax.ShapeDtypeStruct((1, C), F32)],
        compiler_params=_cparams(("parallel", "arbitrary")))(zx, zx, w, b, dout)


def _conv_bwd_in(dh, w, *, name, K, tt=512, tc=512, out_dtype=BF16):
    T, C = dh.shape
    tt = min(tt, T)
    nb8, nT = tt // 8, T // tt
    last8 = T // 8 - 1

    def body(d_ref, n_ref, w_ref, o_ref):
        notlast = (pl.program_id(1) < nT - 1).astype(F32)
        d = d_ref[...]
        nxt = n_ref[...] * notlast
        w_ = w_ref[...]
        acc = d * w_[K - 1:K, :]
        for sh in range(1, K):
            acc = acc + _shift_up(d, nxt, sh) * w_[K - 1 - sh:K - sh, :]
        o_ref[...] = acc.astype(out_dtype)

    return pl.pallas_call(
        body, name=name, grid=(C // tc, nT),
        in_specs=[pl.BlockSpec((tt, tc), lambda c, i: (i, c)),
                  pl.BlockSpec((8, tc), lambda c, i: (jnp.minimum((i + 1) * nb8, last8), c)),
                  pl.BlockSpec((K, tc), lambda c, i: (0, c))],
        out_specs=pl.BlockSpec((tt, tc), lambda c, i: (i, c)),
        out_shape=jax.ShapeDtypeStruct((T, C), out_dtype),
        compiler_params=_cparams(("parallel", "parallel")))(dh, dh, w)


def _ffn_conv_fwd(a, w, b, *, name, tt=256, tc=1408):
    T = a.shape[0]
    tt = min(tt, T)
    K, nbh, nb8 = FFN_CONV, D_FF // tc, tt // 8

    def body(ag_ref, pg_ref, av_ref, pv_ref, wg_ref, wv_ref, bg_ref, bv_ref, o_ref):
        first = (pl.program_id(1) > 0).astype(F32)
        hg, _ = _conv_hid(ag_ref[...], pg_ref[...] * first, wg_ref[...], bg_ref[...], K)
        hv, _ = _conv_hid(av_ref[...], pv_ref[...] * first, wv_ref[...], bv_ref[...], K)
        o_ref[...] = (hg * _sigmoid(hg) * hv).astype(BF16)

    return pl.pallas_call(
        body, name=name, grid=(nbh, T // tt),
        in_specs=[pl.BlockSpec((tt, tc), lambda c, i: (i, c)),
                  pl.BlockSpec((8, tc), lambda c, i: (_prev_idx(i, nb8), c)),
                  pl.BlockSpec((tt, tc), lambda c, i: (i, c + nbh)),
                  pl.BlockSpec((8, tc), lambda c, i: (_prev_idx(i, nb8), c + nbh)),
                  pl.BlockSpec((K, tc), lambda c, i: (0, c)), pl.BlockSpec((K, tc), lambda c, i: (0, c + nbh)),
                  pl.BlockSpec((1, tc), lambda c, i: (0, c)), pl.BlockSpec((1, tc), lambda c, i: (0, c + nbh))],
        out_specs=pl.BlockSpec((tt, tc), lambda c, i: (i, c)),
        out_shape=jax.ShapeDtypeStruct((T, D_FF), BF16),
        compiler_params=_cparams(("parallel", "parallel")))(a, a, a, a, w, w, b, b)


def _ffn_conv_bwd_pre(a, w, b, dp, *, name, tt=256, tc=1408):
    T = a.shape[0]
    tt = min(tt, T)
    K, nbh, nb8 = FFN_CONV, D_FF // tc, tt // 8

    def body(ao_ref, po_ref, ag_ref, pg_ref, av_ref, pv_ref, wg_ref, wv_ref, bg_ref, bv_ref, dp_ref,
             dh_ref, dw_ref, db_ref):
        j = pl.program_id(0)
        t = pl.program_id(1)
        first = (t > 0).astype(F32)
        hg, _ = _conv_hid(ag_ref[...], pg_ref[...] * first, wg_ref[...], bg_ref[...], K)
        hv, _ = _conv_hid(av_ref[...], pv_ref[...] * first, wv_ref[...], bv_ref[...], K)
        sg = _sigmoid(hg)
        d = dp_ref[...].astype(F32)
        is_gate = (j < nbh).astype(F32)
        dh = d * (is_gate * (hv * (sg * (1.0 + hg * (1.0 - sg)))) + (1.0 - is_gate) * (hg * sg))
        dh_ref[...] = dh
        xo = ao_ref[...]
        po = po_ref[...] * first

        @pl.when(t == 0)
        def _():
            dw_ref[...] = jnp.zeros_like(dw_ref)
            db_ref[...] = jnp.zeros_like(db_ref)

        db_ref[...] += jnp.sum(dh, axis=0, keepdims=True)
        for jj in range(K):
            sh = K - 1 - jj
            xs = xo if sh == 0 else _shift_down(xo, po, sh)
            dw_ref[jj:jj + 1, :] += jnp.sum(dh * xs, axis=0, keepdims=True)

    def gi(c):
        return lax.rem(c, nbh)

    return pl.pallas_call(
        body, name=name, grid=(2 * nbh, T // tt),
        in_specs=[pl.BlockSpec((tt, tc), lambda c, i: (i, c)),
                  pl.BlockSpec((8, tc), lambda c, i: (_prev_idx(i, nb8), c)),
                  pl.BlockSpec((tt, tc), lambda c, i: (i, gi(c))),
                  pl.BlockSpec((8, tc), lambda c, i: (_prev_idx(i, nb8), gi(c))),
                  pl.BlockSpec((tt, tc), lambda c, i: (i, gi(c) + nbh)),
                  pl.BlockSpec((8, tc), lambda c, i: (_prev_idx(i, nb8), gi(c) + nbh)),
                  pl.BlockSpec((K, tc), lambda c, i: (0, gi(c))), pl.BlockSpec((K, tc), lambda c, i: (0, gi(c) + nbh)),
                  pl.BlockSpec((1, tc), lambda c, i: (0, gi(c))), pl.BlockSpec((1, tc), lambda c, i: (0, gi(c) + nbh)),
                  pl.BlockSpec((tt, tc), lambda c, i: (i, gi(c)))],
        out_specs=[pl.BlockSpec((tt, tc), lambda c, i: (i, c)), pl.BlockSpec((K, tc), lambda c, i: (0, c)),
                   pl.BlockSpec((1, tc), lambda c, i: (0, c))],
        out_shape=[jax.ShapeDtypeStruct((T, 2 * D_FF), F32), jax.ShapeDtypeStruct((K, 2 * D_FF), F32),
                   jax.ShapeDtypeStruct((1, 2 * D_FF), F32)],
        compiler_params=_cparams(("parallel", "arbitrary")))(a, a, a, a, a, a, w, w, b, b, dp)


def _ffn_conv_fwd3(a3, w, b, *, name, tt=256, tc=1408):
    T = a3.shape[1]
    tt = min(tt, T)
    K, nbh, n16 = FFN_CONV, D_FF // tc, tt // 16

    def body(a_ref, p_ref, wg_ref, wv_ref, bg_ref, bv_ref, o_ref):
        first = (pl.program_id(1) > 0).astype(F32)
        a = a_ref[...].astype(F32)
        prev = p_ref[...].astype(F32)[:, 8:16, :] * first
        hg, _ = _conv_hid(a[0], prev[0], wg_ref[...], bg_ref[...], K)
        hv, _ = _conv_hid(a[1], prev[1], wv_ref[...], bv_ref[...], K)
        o_ref[...] = (hg * _sigmoid(hg) * hv).astype(BF16)

    return pl.pallas_call(
        body, name=name, grid=(nbh, T // tt),
        in_specs=[pl.BlockSpec((2, tt, tc), lambda c, i: (0, i, c)),
                  pl.BlockSpec((2, 16, tc), lambda c, i: (0, _prev_idx(i, n16), c)),
                  pl.BlockSpec((K, tc), lambda c, i: (0, c)), pl.BlockSpec((K, tc), lambda c, i: (0, c + nbh)),
                  pl.BlockSpec((1, tc), lambda c, i: (0, c)), pl.BlockSpec((1, tc), lambda c, i: (0, c + nbh))],
        out_specs=pl.BlockSpec((tt, tc), lambda c, i: (i, c)),
        out_shape=jax.ShapeDtypeStruct((T, D_FF), BF16),
        compiler_params=_cparams(("parallel", "parallel")))(a3, a3, w, w, b, b)


def _ffn_conv_bwd3(a3, w, b, dp, *, name, tt=256, tc=1408):
    T = a3.shape[1]
    tt = min(tt, T)
    K, nbh, n16 = FFN_CONV, D_FF // tc, tt // 16

    def body(a_ref, p_ref, wg_ref, wv_ref, bg_ref, bv_ref, dp_ref, dh_ref, dw_ref, db_ref):
        t = pl.program_id(1)
        first = (t > 0).astype(F32)
        a = a_ref[...].astype(F32)
        prev = p_ref[...].astype(F32)[:, 8:16, :] * first
        hg, sh_g = _conv_hid(a[0], prev[0], wg_ref[...], bg_ref[...], K)
        hv, sh_v = _conv_hid(a[1], prev[1], wv_ref[...], bv_ref[...], K)
        sg = _sigmoid(hg)
        d = dp_ref[...].astype(F32)
        dhg = d * hv * (sg * (1.0 + hg * (1.0 - sg)))
        dhv = d * (hg * sg)
        dh_ref[0] = dhg.astype(BF16)
        dh_ref[1] = dhv.astype(BF16)

        @pl.when(t == 0)
        def _():
            dw_ref[...] = jnp.zeros_like(dw_ref)
            db_ref[...] = jnp.zeros_like(db_ref)

        db_ref[0] += jnp.sum(dhg, axis=0, keepdims=True)
        db_ref[1] += jnp.sum(dhv, axis=0, keepdims=True)
        for j in range(K):
            dw_ref[0, j:j + 1, :] += jnp.sum(dhg * sh_g[j], axis=0, keepdims=True)
            dw_ref[1, j:j + 1, :] += jnp.sum(dhv * sh_v[j], axis=0, keepdims=True)

    return pl.pallas_call(
        body, name=name, grid=(nbh, T // tt),
        in_specs=[pl.BlockSpec((2, tt, tc), lambda c, i: (0, i, c)),
                  pl.BlockSpec((2, 16, tc), lambda c, i: (0, _prev_idx(i, n16), c)),
                  pl.BlockSpec((K, tc), lambda c, i: (0, c)), pl.BlockSpec((K, tc), lambda c, i: (0, c + nbh)),
                  pl.BlockSpec((1, tc), lambda c, i: (0, c)), pl.BlockSpec((1, tc), lambda c, i: (0, c + nbh)),
                  pl.BlockSpec((tt, tc), lambda c, i: (i, c))],
        out_specs=[pl.BlockSpec((2, tt, tc), lambda c, i: (0, i, c)), pl.BlockSpec((2, K, tc), lambda c, i: (0, 0, c)),
                   pl.BlockSpec((2, 1, tc), lambda c, i: (0, 0, c))],
        out_shape=[jax.ShapeDtypeStruct((2, T, D_FF), BF16), jax.ShapeDtypeStruct((2, K, D_FF), F32),
                   jax.ShapeDtypeStruct((2, 1, D_FF), F32)],
        compiler_params=_cparams(("parallel", "arbitrary")))(a3, a3, w, w, b, b, dp)


def _conv_bwd_in3(dh3, w, *, name, K, tt=256, tc=1408):
    H, T, C = dh3.shape
    tt = min(tt, T)
    nb, n16, nT = C // tc, tt // 16, T // tt
    last16 = T // 16 - 1

    def body(d_ref, n_ref, w_ref, o_ref):
        notlast = (pl.program_id(2) < nT - 1).astype(F32)
        d = d_ref[...].astype(F32)
        nxt = n_ref[...].astype(F32)[0:8, :] * notlast
        w_ = w_ref[...]
        acc = d * w_[K - 1:K, :]
        for sh in range(1, K):
            acc = acc + _shift_up(d, nxt, sh) * w_[K - 1 - sh:K - sh, :]
        o_ref[...] = acc.astype(BF16)

    return pl.pallas_call(
        body, name=name, grid=(H, nb, nT),
        in_specs=[pl.BlockSpec((None, tt, tc), lambda h, c, i: (h, i, c)),
                  pl.BlockSpec((None, 16, tc), lambda h, c, i: (h, jnp.minimum((i + 1) * n16, last16), c)),
                  pl.BlockSpec((K, tc), lambda h, c, i: (0, h * nb + c))],
        out_specs=pl.BlockSpec((None, tt, tc), lambda h, c, i: (h, i, c)),
        out_shape=jax.ShapeDtypeStruct((H, T, C), BF16),
        compiler_params=_cparams(("parallel", "parallel", "parallel")))(dh3, dh3, w)


def _cumsum_rows(x):
    L = x.shape[0]
    row = lax.broadcasted_iota(jnp.int32, x.shape, 0)
    k = 1
    while k < L:
        x = x + jnp.where(row >= k, pltpu.roll(x, k, 0), 0.0)
        k *= 2
    return x


def _rcumsum_rows(x):
    L = x.shape[0]
    row = lax.broadcasted_iota(jnp.int32, x.shape, 0)
    k = 1
    while k < L:
        x = x + jnp.where(row < L - k, pltpu.roll(x, L - k, 0), 0.0)
        k *= 2
    return x


def _split_terms(m, n):
    terms, rest = [], m
    for _ in range(n):
        t = rest.astype(BF16)
        terms.append(t)
        rest = rest - t.astype(F32)
    return jnp.concatenate(terms, axis=1)


def _select_dot(m, n_terms, n_out, cond):
    K = m.shape[1]
    k = lax.broadcasted_iota(jnp.int32, (K, n_out), 0)
    j = lax.broadcasted_iota(jnp.int32, (K, n_out), 1)
    sel = cond(k, j).astype(BF16)
    return jnp.dot(_split_terms(m, n_terms), jnp.concatenate([sel] * n_terms, axis=0), preferred_element_type=F32)


def _rowsum_mxu(m):
    return _select_dot(m, 2, 128, lambda k, j: k >= 0)


def _lane_block_sums(m, width):
    shift = width.bit_length() - 1
    return _select_dot(m, 2, 128, lambda k, j: j == jnp.right_shift(k, shift))


def _heads_to_pairs(m):
    return _select_dot(m, 3, 512, lambda k, j: k == jnp.right_shift(j, 6))


def _ssd_common(dt_ref, par_ref):
    par = par_ref[...]
    raw = dt_ref[...] + par[0:1, :]
    dt = _softplus(raw)
    a = -jnp.exp(par[1:2, :])
    cs = _cumsum_rows(dt * a)
    L = cs.shape[0]
    cs_last = cs[L - 1:L, :]
    return raw, dt, a, par[2:3, :], cs, cs.T, jnp.exp(cs), jnp.exp(cs_last - cs), jnp.exp(cs_last)


def _ssd_specs(nc, rev):
    L = SSM_CHUNK

    def ci(c):
        return nc - 1 - c if rev else c

    return [pl.BlockSpec((L, D_INNER), lambda c: (ci(c), 0)),
            pl.BlockSpec((L, GN), lambda c: (ci(c), D_INNER // GN)),
            pl.BlockSpec((L, GN), lambda c: (ci(c), D_INNER // GN + 1)),
            pl.BlockSpec((SSM_GROUPS, L, 128), lambda c: (0, ci(c), 0)),
            pl.BlockSpec((SSM_GROUPS, 8, 128), lambda c: (0, 0, 0)),
            pl.BlockSpec((L, D_INNER), lambda c: (ci(c), 0)),
            pl.BlockSpec((1, D_INNER), lambda c: (0, 0))], ci


def _round_robin(gens):
    live = list(gens)
    while live:
        nxt = []
        for gen in live:
            try:
                next(gen)
                nxt.append(gen)
            except StopIteration:
                pass
        live = nxt


def _group_views(g, wide, narrow, lead):
    return ([r.at[:, g * 512:(g + 1) * 512] for r in wide], [r.at[:, g * 128:(g + 1) * 128] for r in narrow],
            [r.at[g] for r in lead])


def _ssd_fwd(xbc_c, zx, dtg, par, gnw, *, name):
    T = xbc_c.shape[0]
    L = SSM_CHUNK
    nc = T // L
    in_specs, ci = _ssd_specs(nc, False)

    def body(xs_ref, b_ref, c_ref, dt_ref, par_ref, z_ref, gnw_ref, y_ref, yn_ref, st_ref, h_ref):
        @pl.when(pl.program_id(0) == 0)
        def _():
            h_ref[...] = jnp.zeros_like(h_ref)

        gens = []
        for g in range(SSM_GROUPS):
            (xs, z, gw, y, yn), (b, c), (dt, pr, st, h) = _group_views(
                g, [xs_ref, z_ref, gnw_ref, y_ref, yn_ref], [b_ref, c_ref], [dt_ref, par_ref, st_ref, h_ref])
            gens.append(group(xs, b, c, dt, pr, z, gw, y, yn, st, h))
        _round_robin(gens)

    def group(xs_ref, b_ref, c_ref, dt_ref, par_ref, z_ref, gnw_ref, y_ref, yn_ref, st_ref, h_ref):
        _, dt, _, dsk, cs, csT, ecs, eend, dec = _ssd_common(dt_ref, par_ref)
        Bb = b_ref[...].astype(BF16)
        Cb = c_ref[...].astype(BF16)
        G = lax.dot_general(Cb, Bb, _NT, preferred_element_type=F32)
        row = lax.broadcasted_iota(jnp.int32, (L, L), 0)
        col = lax.broadcasted_iota(jnp.int32, (L, L), 1)
        tril = col <= row
        lo = lax.broadcasted_iota(jnp.int32, (L, 128), 1) < 64
        lo1 = lax.broadcasted_iota(jnp.int32, (1, 128), 1) < 64
        dt_x, ecs_x, eend_x = (_heads_to_pairs(m) for m in (dt, ecs, eend))
        for pp in range(4):
            hA, hB = 2 * pp, 2 * pp + 1
            lanes = slice(pp * 128, (pp + 1) * 128)

            def sel1(m):
                return jnp.where(lo1, m[:, hA:hA + 1], m[:, hB:hB + 1])

            X = xs_ref[:, lanes]
            xd = X * dt_x[:, lanes]
            xdb = xd.astype(BF16)
            ys = []
            for h in (hA, hB):
                Lm = jnp.where(tril, jnp.exp(jnp.minimum(cs[:, h:h + 1] - csT[h:h + 1, :], 0.0)), 0.0)
                ys.append(jnp.dot((G * Lm).astype(BF16), xdb, preferred_element_type=F32))
                yield
            Hp = h_ref[pp]
            st_ref[pp] = Hp
            yoff = jnp.dot(Cb, Hp.astype(BF16), preferred_element_type=F32) * ecs_x[:, lanes]
            y_ref[:, lanes] = jnp.where(lo, ys[0], ys[1]) + yoff + sel1(dsk) * X
            S = lax.dot_general(Bb, (xd * eend_x[:, lanes]).astype(BF16), _TN, preferred_element_type=F32)
            h_ref[pp] = Hp * sel1(dec) + S
            yield
        zv = z_ref[...]
        yg = y_ref[...] * (zv * _sigmoid(zv))
        r = jnp.tile(lax.rsqrt(_rowsum_mxu(yg * yg) * (1.0 / 512) + EPS), (1, 4))
        yn_ref[...] = (yg * r * gnw_ref[...]).astype(BF16)

    return pl.pallas_call(
        body, name=name, grid=(nc,), in_specs=in_specs,
        out_specs=[pl.BlockSpec((L, D_INNER), lambda c: (c, 0)), pl.BlockSpec((L, D_INNER), lambda c: (c, 0)),
                   pl.BlockSpec((SSM_GROUPS, None, 4, 128, 128), lambda c: (0, c, 0, 0, 0))],
        out_shape=[jax.ShapeDtypeStruct((T, D_INNER), F32), jax.ShapeDtypeStruct((T, D_INNER), BF16),
                   jax.ShapeDtypeStruct((SSM_GROUPS, nc, 4, 128, 128), F32)],
        scratch_shapes=[pltpu.VMEM((SSM_GROUPS, 4, 128, 128), F32)],
        compiler_params=_cparams(("arbitrary",)))(xbc_c, xbc_c, xbc_c, dtg, par, zx, gnw)


def _ssd_bwd(xbc_c, zx, dtg, par, gnw, y, st, dyn, *, name):
    T = xbc_c.shape[0]
    L = SSM_CHUNK
    nc = T // L
    in_specs, ci = _ssd_specs(nc, True)
    in_specs += [pl.BlockSpec((L, D_INNER), lambda c: (ci(c), 0)),
                 pl.BlockSpec((SSM_GROUPS, None, 4, 128, 128), lambda c: (0, ci(c), 0, 0, 0)),
                 pl.BlockSpec((L, D_INNER), lambda c: (ci(c), 0))]

    def body(xs_ref, b_ref, c_ref, dt_ref, par_ref, z_ref, gnw_ref, y_ref, st_ref, dyn_ref,
             dxbc_ref, dz_ref, ddt_ref, dgnw_ref, dpar_ref, dh_ref):
        @pl.when(pl.program_id(0) == 0)
        def _():
            dh_ref[...] = jnp.zeros_like(dh_ref)
            dgnw_ref[...] = jnp.zeros_like(dgnw_ref)
            dpar_ref[...] = jnp.zeros_like(dpar_ref)

        dxs_ref = dxbc_ref.at[:, 0:D_INNER]
        db_ref = dxbc_ref.at[:, D_INNER:D_INNER + GN]
        dc_ref = dxbc_ref.at[:, D_INNER + GN:CONV_DIM]

        gens = []
        for g in range(SSM_GROUPS):
            (xs, z, gw, y, dyn, dxs, dz, dgw), (b, c, db, dc), (dt, pr, st, ddt, dpr, dh) = _group_views(
                g, [xs_ref, z_ref, gnw_ref, y_ref, dyn_ref, dxs_ref, dz_ref, dgnw_ref], [b_ref, c_ref, db_ref, dc_ref],
                [dt_ref, par_ref, st_ref, ddt_ref, dpar_ref, dh_ref])
            gens.append(group(xs, b, c, dt, pr, z, gw, y, st, dyn, dxs, db, dc, dz, ddt, dgw, dpr, dh))
        _round_robin(gens)

    def group(xs_ref, b_ref, c_ref, dt_ref, par_ref, z_ref, gnw_ref, y_ref, st_ref, dyn_ref,
              dxs_ref, db_ref, dc_ref, dz_ref, ddt_ref, dgnw_ref, dpar_ref, dh_ref):
        yv = y_ref[...]
        zv = z_ref[...]
        sg = _sigmoid(zv)
        sz = zv * sg
        yg = yv * sz
        r = jnp.tile(lax.rsqrt(_rowsum_mxu(yg * yg) * (1.0 / 512) + EPS), (1, 4))
        yh = yg * r
        dyn = dyn_ref[...].astype(F32)
        dgnw_ref[...] += jnp.sum(dyn * yh, axis=0, keepdims=True)
        dyh = dyn * gnw_ref[...]
        dyg = r * (dyh - yh * jnp.tile(_rowsum_mxu(dyh * yh) * (1.0 / 512), (1, 4)))
        dY_all = dyg * sz
        dz_ref[...] = (dyg * yv * (sg * (1.0 + zv * (1.0 - sg)))).astype(dz_ref.dtype)

        yield
        raw, dt, a, dsk, cs, csT, ecs, eend, dec = _ssd_common(dt_ref, par_ref)
        Bb = b_ref[...].astype(BF16)
        Cb = c_ref[...].astype(BF16)
        G = lax.dot_general(Cb, Bb, _NT, preferred_element_type=F32)
        row = lax.broadcasted_iota(jnp.int32, (L, L), 0)
        col = lax.broadcasted_iota(jnp.int32, (L, L), 1)
        tril = col <= row
        lo = lax.broadcasted_iota(jnp.int32, (L, 128), 1) < 64
        lane1 = lax.broadcasted_iota(jnp.int32, (1, 128), 1)
        lo1 = lane1 < 64
        rowl = lax.broadcasted_iota(jnp.int32, (L, 128), 0)
        dt_x, ecs_x, eend_x = (_heads_to_pairs(m) for m in (dt, ecs, eend))
        dG = jnp.zeros((L, L), F32)
        dB = jnp.zeros((L, SSM_STATE), F32)
        dC = jnp.zeros((L, SSM_STATE), F32)
        dcs_t = jnp.zeros((L, L), F32)
        tails = jnp.zeros((1, 128), F32)
        dD_row = jnp.zeros((1, 128), F32)
        v_parts, prod_parts = [], []

        def tot(m):
            return jnp.sum(jnp.sum(m, axis=0, keepdims=True), axis=1, keepdims=True)

        for pp in range(4):
            hA, hB = 2 * pp, 2 * pp + 1
            lanes = slice(pp * 128, (pp + 1) * 128)

            def sel1(m):
                return jnp.where(lo1, m[:, hA:hA + 1], m[:, hB:hB + 1])

            X = xs_ref[:, lanes]
            dY = dY_all[:, lanes]
            dtsel = dt_x[:, lanes]
            xd = X * dtsel
            xdb = xd.astype(BF16)
            dYb = dY.astype(BF16)
            Hp = st_ref[pp]
            Hb = Hp.astype(BF16)
            dHn = dh_ref[pp]
            dHb = dHn.astype(BF16)
            ecs_sel = ecs_x[:, lanes]
            eend_sel = eend_x[:, lanes]
            dxd_state = jnp.dot(Bb, dHb, preferred_element_type=F32) * eend_sel
            yoff = jnp.dot(Cb, Hb, preferred_element_type=F32) * ecs_sel
            dYe = (dY * ecs_sel).astype(BF16)
            dC = dC + lax.dot_general(dYe, Hb, _NT, preferred_element_type=F32)
            dB = dB + lax.dot_general((xd * eend_sel).astype(BF16), dHb, _NT, preferred_element_type=F32)
            dh_ref[pp] = dHn * sel1(dec) + lax.dot_general(Cb, dYe, _TN, preferred_element_type=F32)
            q = xd * dxd_state
            dyq = dY * yoff - q
            qcol = jnp.sum(q, axis=0, keepdims=True)
            hcol = jnp.sum(dHn * Hp, axis=0, keepdims=True)
            dxd_diag = []
            for h, msk, msk1 in ((hA, lo, lo1), (hB, jnp.logical_not(lo), jnp.logical_not(lo1))):
                Lm = jnp.where(tril, jnp.exp(jnp.minimum(cs[:, h:h + 1] - csT[h:h + 1, :], 0.0)), 0.0)
                M = G * Lm
                dxd_diag.append(lax.dot_general(M.astype(BF16), dYb, _TN, preferred_element_type=F32))
                dM = lax.dot_general(jnp.where(msk, dY, 0.0).astype(BF16), xdb, _NT, preferred_element_type=F32)
                dG = dG + dM * Lm
                W = dM * M
                dcs_t = dcs_t + jnp.where(row == h, jnp.sum(W, axis=0, keepdims=True), 0.0)
                v_parts.append(W + jnp.where(msk, dyq, 0.0))
                tail = (jnp.sum(jnp.where(msk1, qcol, 0.0), axis=1, keepdims=True)
                        + dec[:, h:h + 1] * jnp.sum(jnp.where(msk1, hcol, 0.0), axis=1, keepdims=True))
                tails = tails + jnp.where(lane1 == h, tail, 0.0)
                yield
            dxd = jnp.where(lo, dxd_diag[0], dxd_diag[1]) + dxd_state
            prod_parts.append(dxd * X)
            dxs_ref[:, lanes] = dxd * dtsel + sel1(dsk) * dY
            dyx = jnp.sum(dY * X, axis=0, keepdims=True)
            sA = jnp.sum(jnp.where(lo1, dyx, 0.0), axis=1, keepdims=True)
            sB = jnp.sum(dyx, axis=1, keepdims=True) - sA
            dD_row = dD_row + jnp.where(lane1 == hA, sA, 0.0) + jnp.where(lane1 == hB, sB, 0.0)
            yield
        dGb = dG.astype(BF16)
        db_ref[...] = dB + lax.dot_general(dGb, Cb, _TN, preferred_element_type=F32)
        dc_ref[...] = dC + jnp.dot(dGb, Bb, preferred_element_type=F32)
        dcs_mat = _lane_block_sums(jnp.concatenate(v_parts, axis=1), 128) + jnp.where(rowl == L - 1, tails, 0.0)
        ddt_mat = _lane_block_sums(jnp.concatenate(prod_parts, axis=1), 64)
        dad = _rcumsum_rows(dcs_mat - dcs_t.T)
        draw = (a * dad + ddt_mat) * _sigmoid(raw)
        ddt_ref[...] = draw
        dpar_ref[0:1, :] += jnp.sum(draw, axis=0, keepdims=True)
        dpar_ref[1:2, :] += jnp.sum(dt * dad, axis=0, keepdims=True) * a
        dpar_ref[2:3, :] += dD_row

    return pl.pallas_call(
        body, name=name, grid=(nc,), in_specs=in_specs,
        out_specs=[pl.BlockSpec((L, CONV_DIM), lambda c: (ci(c), 0)),
                   pl.BlockSpec((L, D_INNER), lambda c: (ci(c), 0)),
                   pl.BlockSpec((SSM_GROUPS, L, 128), lambda c: (0, ci(c), 0)),
                   pl.BlockSpec((1, D_INNER), lambda c: (0, 0)),
                   pl.BlockSpec((SSM_GROUPS, 8, 128), lambda c: (0, 0, 0))],
        out_shape=[jax.ShapeDtypeStruct((T, CONV_DIM), F32), jax.ShapeDtypeStruct((T, D_INNER), BF16),
                   jax.ShapeDtypeStruct((SSM_GROUPS, T, 128), F32), jax.ShapeDtypeStruct((1, D_INNER), F32),
                   jax.ShapeDtypeStruct((SSM_GROUPS, 8, 128), F32)],
        scratch_shapes=[pltpu.VMEM((SSM_GROUPS, 4, 128, 128), F32)],
        compiler_params=_cparams(("arbitrary",)))(xbc_c, xbc_c, xbc_c, dtg, par, zx, gnw, y, st, dyn)


SB_KEYS = 512
SB_SCAN = 256
SB_STRIP = 256


def _tri(width, cond):
    kk = lax.broadcasted_iota(jnp.int32, (width, width), 0)
    jj = lax.broadcasted_iota(jnp.int32, (width, width), 1)
    return cond(kk, jj).astype(BF16)


def _sba_diag_mask():
    Bq = SB_BLOCK
    rowi = lax.broadcasted_iota(jnp.int32, (2 * Bq, Bq), 0)
    return lax.broadcasted_iota(jnp.int32, (2 * Bq, Bq), 1) < jnp.where(rowi >= Bq, rowi - Bq, rowi)


_LOG2E = 1.4426950408889634


def _softplus2(z2):
    return jnp.maximum(z2, 0.0) + jnp.log2(1.0 + jnp.exp2(-jnp.abs(z2)))


def _sba_sub_fwd(zb, c, U, mask):
    z2 = zb * _LOG2E
    s = _softplus2(z2)
    if mask is not None:
        s = jnp.where(mask, s, 0.0)
    R = c + jnp.dot(s.astype(BF16), U, preferred_element_type=F32)
    A = jnp.exp2(z2 - s - R)
    if mask is not None:
        A = jnp.where(mask, A, 0.0)
    return A.astype(BF16), R[:, 0:1] + s[:, 0:1]


def _sba_sub_bwd(zb, dAb, Lt, pc, pe, Uincl, Uexcl, mask):
    last = zb.shape[1] - 1
    z2 = zb * _LOG2E
    s = _softplus2(z2)
    g = z2 - s
    if mask is not None:
        s = jnp.where(mask, s, 0.0)
    P = pc + jnp.dot(s.astype(BF16), Uincl, preferred_element_type=F32)
    A = jnp.exp2(g - (Lt - P))
    if mask is not None:
        A = jnp.where(mask, A, 0.0)
    E = dAb * A
    PE = pe + jnp.dot(E.astype(BF16), Uexcl, preferred_element_type=F32)
    dz = E - jnp.exp2(g) * (E + PE)
    if mask is not None:
        dz = jnp.where(mask, dz, 0.0)
    return (A.astype(BF16), dz.astype(BF16), P[:, last:last + 1], PE[:, last:last + 1] + E[:, last:last + 1])


def _stack_heads(v):
    lo = lax.broadcasted_iota(jnp.int32, v.shape, 1) < 64
    zero = jnp.zeros_like(v)
    return jnp.concatenate([jnp.where(lo, v, zero), jnp.where(lo, zero, v)], axis=0)


def _unstack_heads(v):
    lo = lax.broadcasted_iota(jnp.int32, (SB_BLOCK, 128), 1) < 64
    return jnp.where(lo, v[:SB_BLOCK], v[SB_BLOCK:])


def _sba_rows(a):
    return slice(2 * a * SB_BLOCK, 2 * (a + 1) * SB_BLOCK)


def _sba_diag_case(a, b):
    Bq = SB_BLOCK
    if b * SB_SCAN >= (a + 1) * Bq:
        return "skip"
    if (b + 1) * SB_SCAN <= a * Bq:
        return "full"
    rowi = lax.broadcasted_iota(jnp.int32, (2 * Bq, SB_SCAN), 0)
    qpos = a * Bq + jnp.where(rowi >= Bq, rowi - Bq, rowi)
    return b * SB_SCAN + lax.broadcasted_iota(jnp.int32, (2 * Bq, SB_SCAN), 1) < qpos


def _sba_fwd(q, kv, *, name):
    T = q.shape[0]
    Bq = SB_BLOCK
    nsub = SB_KEYS // Bq
    nscan = SB_KEYS // SB_SCAN
    R = 2 * SB_KEYS
    assert T % SB_KEYS == 0 and SB_STRIP == 2 * Bq
    scale = 1.0 / math.sqrt(SB_HEAD_DIM)

    def body(q_ref, k_ref, v_ref, o_ref, lt_ref, z_s, a_s, c_s, acc_s):
        i = pl.program_id(1)
        U2 = _tri(SB_SCAN, lambda k, j: k > j)
        qs_all = jnp.concatenate([_stack_heads(q_ref[a * Bq:(a + 1) * Bq, :] * scale) for a in range(nsub)], axis=0)
        c_s[...] = jnp.zeros_like(c_s)
        acc_s[...] = jnp.zeros_like(acc_s)

        def scores(J, slot):
            off = pl.multiple_of(J * SB_KEYS, SB_KEYS)
            z_s[slot] = lax.dot_general(qs_all, k_ref[pl.ds(off, SB_KEYS), :], _NT, preferred_element_type=F32)

        def weights(slot, diag):
            for a in range(nsub):
                rows = _sba_rows(a)
                c = c_s[rows, :]
                for b in reversed(range(nscan)):
                    cols = slice(b * SB_SCAN, (b + 1) * SB_SCAN)
                    case = _sba_diag_case(a, b) if diag else "full"
                    if isinstance(case, str) and case == "skip":
                        a_s[slot, rows, cols] = jnp.zeros((2 * Bq, SB_SCAN), BF16)
                        continue
                    A, c = _sba_sub_fwd(z_s[slot, rows, cols], c, U2, None if isinstance(case, str) else case)
                    a_s[slot, rows, cols] = A
                c_s[rows, :] = c

        def values(J, slot):
            off = pl.multiple_of(J * SB_KEYS, SB_KEYS)
            acc_s[...] += jnp.dot(a_s[slot], v_ref[pl.ds(off, SB_KEYS), :], preferred_element_type=F32)

        scores(i, 0)
        weights(0, True)
        scores(jnp.maximum(i - 1, 0), 1)

        def step(t, _):
            slot = lax.rem(t, 2)
            weights(slot, False)
            scores(jnp.maximum(i - t - 1, 0), 1 - slot)
            values(i - t + 1, 1 - slot)
            return 0

        lax.fori_loop(1, i + 1, step, 0)
        values(0, lax.rem(i, 2))
        for a in range(nsub):
            o_ref[a * Bq:(a + 1) * Bq, :] = _unstack_heads(acc_s[_sba_rows(a), :]).astype(BF16)
            lt_ref[a * Bq:(a + 1) * Bq, :] = _unstack_heads(jnp.broadcast_to(c_s[_sba_rows(a), :], (2 * Bq, 128)))

    return pl.pallas_call(
        body, name=name, grid=(SB_HEADS // 2, T // SB_KEYS),
        in_specs=[pl.BlockSpec((SB_KEYS, 128), lambda p, i: (i, p)), pl.BlockSpec((T, 128), lambda p, i: (0, p)),
                  pl.BlockSpec((T, 128), lambda p, i: (0, p + SB_HEADS // 2))],
        out_specs=[pl.BlockSpec((SB_KEYS, 128), lambda p, i: (i, p)),
                   pl.BlockSpec((None, SB_KEYS, 128), lambda p, i: (p, i, 0))],
        out_shape=[jax.ShapeDtypeStruct((T, D_MODEL), BF16), jax.ShapeDtypeStruct((SB_HEADS // 2, T, 128), F32)],
        scratch_shapes=[pltpu.VMEM((2, R, SB_KEYS), F32), pltpu.VMEM((2, R, SB_KEYS), BF16),
                        pltpu.VMEM((R, 1), F32), pltpu.VMEM((R, 128), F32)],
        compiler_params=_cparams(("parallel", "parallel")))(q, kv, kv)


def _sba_bwd(q, kv, lt, do, *, name):
    T = q.shape[0]
    Bq = SB_BLOCK
    nq = T // SB_KEYS
    nsub = SB_KEYS // Bq
    nscan = SB_KEYS // SB_SCAN
    R = 2 * SB_KEYS
    assert T % SB_KEYS == 0 and SB_STRIP == 2 * Bq
    scale = 1.0 / math.sqrt(SB_HEAD_DIM)

    def body(q_ref, k_ref, v_ref, lt_ref, do_ref, dq_ref, dk_ref, dv_ref, dk_acc, dv_acc,
             z_s, da_s, a_s, dz_s, pc_s, pe_s, lt_s, dq_s):
        i = pl.program_id(1)

        @pl.when(i == 0)
        def _():
            dk_acc[...] = jnp.zeros_like(dk_acc)
            dv_acc[...] = jnp.zeros_like(dv_acc)

        Uincl = _tri(SB_SCAN, lambda k, j: k <= j)
        Uexcl = _tri(SB_SCAN, lambda k, j: k < j)
        qs, dos = [], []
        for a in range(nsub):
            rows = slice(a * Bq, (a + 1) * Bq)
            qs.append(_stack_heads(q_ref[rows, :] * scale))
            dos.append(_stack_heads(do_ref[rows, :]))
            lt_s[_sba_rows(a), :] = jnp.concatenate([lt_ref[rows, 0:1], lt_ref[rows, 64:65]], axis=0)
        qs_all = jnp.concatenate(qs, axis=0)
        dos_all = jnp.concatenate(dos, axis=0)
        pc_s[...] = jnp.zeros_like(pc_s)
        pe_s[...] = jnp.zeros_like(pe_s)
        a_s[1] = jnp.zeros((R, SB_KEYS), BF16)
        dz_s[1] = jnp.zeros((R, SB_KEYS), BF16)

        def scores(J, slot):
            off = pl.multiple_of(J * SB_KEYS, SB_KEYS)
            z_s[slot] = lax.dot_general(qs_all, k_ref[pl.ds(off, SB_KEYS), :], _NT, preferred_element_type=F32)
            da_s[slot] = lax.dot_general(dos_all, v_ref[pl.ds(off, SB_KEYS), :], _NT, preferred_element_type=F32)

        def gradients(slot, diag):
            for a in range(nsub):
                rows = _sba_rows(a)
                pc, pe, Lt = pc_s[rows, :], pe_s[rows, :], lt_s[rows, :]
                for b in range(nscan):
                    cols = slice(b * SB_SCAN, (b + 1) * SB_SCAN)
                    case = _sba_diag_case(a, b) if diag else "full"
                    if isinstance(case, str) and case == "skip":
                        a_s[slot, rows, cols] = jnp.zeros((2 * Bq, SB_SCAN), BF16)
                        dz_s[slot, rows, cols] = jnp.zeros((2 * Bq, SB_SCAN), BF16)
                        continue
                    A, dz, pc, pe = _sba_sub_bwd(z_s[slot, rows, cols], da_s[slot, rows, cols], Lt, pc, pe, Uincl, Uexcl,
                                                 None if isinstance(case, str) else case)
                    a_s[slot, rows, cols] = A
                    dz_s[slot, rows, cols] = dz
                pc_s[rows, :] = pc
                pe_s[rows, :] = pe

        def products(J, slot):
            off = pl.multiple_of(J * SB_KEYS, SB_KEYS)
            dzt = dz_s[slot]
            dk_acc[pl.ds(off, SB_KEYS), :] += lax.dot_general(dzt, qs_all, _TN, preferred_element_type=F32)
            dv_acc[pl.ds(off, SB_KEYS), :] += lax.dot_general(a_s[slot], dos_all, _TN, preferred_element_type=F32)
            dq_s[...] += jnp.dot(dzt, k_ref[pl.ds(off, SB_KEYS), :], preferred_element_type=F32)

        dq_s[...] = jnp.zeros_like(dq_s)
        scores(0, 0)

        def two_steps(u, _):
            t = 2 * u
            gradients(0, False)
            scores(t + 1, 1)
            products(jnp.maximum(t - 1, 0), 1)
            gradients(1, False)
            scores(t + 2, 0)
            products(t, 0)
            return 0

        lax.fori_loop(0, i // 2, two_steps, 0)
        odd = lax.rem(i, 2) == 1

        @pl.when(jnp.logical_not(odd))
        def _():
            gradients(0, True)
            products(jnp.maximum(i - 1, 0), 1)
            products(i, 0)

        @pl.when(odd)
        def _():
            gradients(0, False)
            scores(i, 1)
            products(jnp.maximum(i - 2, 0), 1)
            gradients(1, True)
            products(i - 1, 0)
            products(i, 1)

        for a in range(nsub):
            dq_ref[a * Bq:(a + 1) * Bq, :] = (_unstack_heads(dq_s[_sba_rows(a), :]) * scale).astype(BF16)

        @pl.when(i == nq - 1)
        def _():
            dk_ref[...] = dk_acc[...].astype(BF16)
            dv_ref[...] = dv_acc[...].astype(BF16)

    return pl.pallas_call(
        body, name=name, grid=(SB_HEADS // 2, nq),
        in_specs=[pl.BlockSpec((SB_KEYS, 128), lambda p, i: (i, p)), pl.BlockSpec((T, 128), lambda p, i: (0, p)),
                  pl.BlockSpec((T, 128), lambda p, i: (0, p + SB_HEADS // 2)),
                  pl.BlockSpec((None, SB_KEYS, 128), lambda p, i: (p, i, 0)),
                  pl.BlockSpec((SB_KEYS, 128), lambda p, i: (i, p))],
        out_specs=[pl.BlockSpec((SB_KEYS, 128), lambda p, i: (i, p)), pl.BlockSpec((T, 128), lambda p, i: (0, p)),
                   pl.BlockSpec((T, 128), lambda p, i: (0, p))],
        out_shape=[jax.ShapeDtypeStruct((T, D_MODEL), BF16), jax.ShapeDtypeStruct((T, D_MODEL), BF16),
                   jax.ShapeDtypeStruct((T, D_MODEL), BF16)],
        scratch_shapes=[pltpu.VMEM((T, 128), F32), pltpu.VMEM((T, 128), F32),
                        pltpu.VMEM((2, R, SB_KEYS), F32), pltpu.VMEM((2, R, SB_KEYS), F32),
                        pltpu.VMEM((2, R, SB_KEYS), BF16), pltpu.VMEM((2, R, SB_KEYS), BF16),
                        pltpu.VMEM((R, 1), F32), pltpu.VMEM((R, 1), F32), pltpu.VMEM((R, 1), F32),
                        pltpu.VMEM((R, 128), F32)],
        compiler_params=_cparams(("parallel", "arbitrary")))(q, kv, kv, lt, do)


def _sba_fwd_old(q, kv, *, name):
    T = q.shape[0]
    Bq = SB_BLOCK
    nsub = SB_KEYS // Bq
    assert T % SB_KEYS == 0
    scale = 1.0 / math.sqrt(SB_HEAD_DIM)

    def body(q_ref, k_ref, v_ref, o_ref, lt_ref):
        I = pl.program_id(1)
        U1 = _tri(Bq, lambda k, j: k > j)
        U2 = _tri(SB_SCAN, lambda k, j: k > j)
        dmask = _sba_diag_mask()
        qs = [_stack_heads(q_ref[a * Bq:(a + 1) * Bq, :] * scale) for a in range(nsub)]
        cs, accs = [], []
        for a in range(nsub):
            c = jnp.zeros((2 * Bq, 1), F32)
            acc = jnp.zeros((2 * Bq, 128), F32)
            for b in range(a, -1, -1):
                off = pl.multiple_of(I * SB_KEYS + b * Bq, Bq)
                zb = lax.dot_general(qs[a], k_ref[pl.ds(off, Bq), :], _NT, preferred_element_type=F32)
                A, c = _sba_sub_fwd(zb, c, U1, dmask if b == a else None)
                acc = acc + jnp.dot(A, v_ref[pl.ds(off, Bq), :], preferred_element_type=F32)
            cs.append(c)
            accs.append(acc)
        qs_all = jnp.concatenate(qs, axis=0)

        def step(n, carry):
            c, acc = carry
            off = pl.multiple_of((I - 1 - n) * SB_KEYS, SB_KEYS)
            z = lax.dot_general(qs_all, k_ref[pl.ds(off, SB_KEYS), :], _NT, preferred_element_type=F32)
            parts = [None] * (SB_KEYS // SB_SCAN)
            for b in reversed(range(SB_KEYS // SB_SCAN)):
                parts[b], c = _sba_sub_fwd(z[:, b * SB_SCAN:(b + 1) * SB_SCAN], c, U2, None)
            return c, acc + jnp.dot(jnp.concatenate(parts, axis=1), v_ref[pl.ds(off, SB_KEYS), :],
                                    preferred_element_type=F32)

        c, acc = lax.fori_loop(0, I, step, (jnp.concatenate(cs, axis=0), jnp.concatenate(accs, axis=0)))
        for a in range(nsub):
            rows = slice(2 * a * Bq, 2 * (a + 1) * Bq)
            o_ref[a * Bq:(a + 1) * Bq, :] = _unstack_heads(acc[rows]).astype(BF16)
            lt_ref[a * Bq:(a + 1) * Bq, :] = _unstack_heads(jnp.broadcast_to(c[rows], (2 * Bq, 128)))

    return pl.pallas_call(
        body, name=name, grid=(SB_HEADS // 2, T // SB_KEYS),
        in_specs=[pl.BlockSpec((SB_KEYS, 128), lambda p, i: (i, p)), pl.BlockSpec((T, 128), lambda p, i: (0, p)),
                  pl.BlockSpec((T, 128), lambda p, i: (0, p + SB_HEADS // 2))],
        out_specs=[pl.BlockSpec((SB_KEYS, 128), lambda p, i: (i, p)),
                   pl.BlockSpec((None, SB_KEYS, 128), lambda p, i: (p, i, 0))],
        out_shape=[jax.ShapeDtypeStruct((T, D_MODEL), BF16), jax.ShapeDtypeStruct((SB_HEADS // 2, T, 128), F32)],
        compiler_params=_cparams(("parallel", "parallel")))(q, kv, kv)


def _sba_bwd_old(q, kv, lt, do, *, name):
    T = q.shape[0]
    Bq = SB_BLOCK
    nq = T // SB_KEYS
    nsub = SB_KEYS // Bq
    assert T % SB_KEYS == 0
    scale = 1.0 / math.sqrt(SB_HEAD_DIM)

    def body(q_ref, k_ref, v_ref, lt_ref, do_ref, dq_ref, dk_ref, dv_ref, dk_acc, dv_acc,
             z_s, da_s, a_s, dz_s, pc_s, pe_s, lt_s):
        i = pl.program_id(1)

        @pl.when(i == 0)
        def _():
            dk_acc[...] = jnp.zeros_like(dk_acc)
            dv_acc[...] = jnp.zeros_like(dv_acc)

        Uincl1 = _tri(Bq, lambda k, j: k <= j)
        Uexcl1 = _tri(Bq, lambda k, j: k < j)
        Uincl2 = _tri(SB_SCAN, lambda k, j: k <= j)
        Uexcl2 = _tri(SB_SCAN, lambda k, j: k < j)
        dmask = _sba_diag_mask()
        qs, dos, lts = [], [], []
        for a in range(nsub):
            rows = slice(a * Bq, (a + 1) * Bq)
            qs.append(_stack_heads(q_ref[rows, :] * scale))
            dos.append(_stack_heads(do_ref[rows, :]))
            lts.append(jnp.concatenate([lt_ref[rows, 0:1], lt_ref[rows, 64:65]], axis=0))
        qs_all = jnp.concatenate(qs, axis=0)
        dos_all = jnp.concatenate(dos, axis=0)
        lt_all = jnp.concatenate(lts, axis=0)

        R = 2 * nsub * Bq
        pc_s[...] = jnp.zeros_like(pc_s)
        pe_s[...] = jnp.zeros_like(pe_s)
        lt_s[...] = lt_all

        def scores(J, slot):
            off = pl.multiple_of(J * SB_KEYS, SB_KEYS)
            z_s[slot] = lax.dot_general(qs_all, k_ref[pl.ds(off, SB_KEYS), :], _NT, preferred_element_type=F32)
            da_s[slot] = lax.dot_general(dos_all, v_ref[pl.ds(off, SB_KEYS), :], _NT, preferred_element_type=F32)

        def elementwise(slot):
            for r in range(R // SB_STRIP):
                rows = slice(r * SB_STRIP, (r + 1) * SB_STRIP)
                pc, pe, Lt = pc_s[rows, :], pe_s[rows, :], lt_s[rows, :]
                for b in range(SB_KEYS // SB_SCAN):
                    cols = slice(b * SB_SCAN, (b + 1) * SB_SCAN)
                    A, dz, pc, pe = _sba_sub_bwd(z_s[slot, rows, cols], da_s[slot, rows, cols], Lt, pc, pe,
                                                 Uincl2, Uexcl2, None)
                    a_s[slot, rows, cols] = A
                    dz_s[slot, rows, cols] = dz
                pc_s[rows, :] = pc
                pe_s[rows, :] = pe

        def outputs(J, slot, dq_acc):
            off = pl.multiple_of(J * SB_KEYS, SB_KEYS)
            dzt = dz_s[slot]
            dk_acc[pl.ds(off, SB_KEYS), :] += lax.dot_general(dzt, qs_all, _TN, preferred_element_type=F32)
            dv_acc[pl.ds(off, SB_KEYS), :] += lax.dot_general(a_s[slot], dos_all, _TN, preferred_element_type=F32)
            return dq_acc + jnp.dot(dzt, k_ref[pl.ds(off, SB_KEYS), :], preferred_element_type=F32)

        a_s[1] = jnp.zeros((R, SB_KEYS), BF16)
        dz_s[1] = jnp.zeros((R, SB_KEYS), BF16)
        last = jnp.maximum(i - 1, 0)
        scores(0, 0)

        def step(J, dq_acc):
            slot = lax.rem(J, 2)
            elementwise(slot)
            scores(jnp.minimum(J + 1, last), 1 - slot)
            return outputs(jnp.maximum(J - 1, 0), 1 - slot, dq_acc)

        dq_acc = lax.fori_loop(0, i, step, jnp.zeros((R, 128), F32))
        dq_acc = outputs(last, lax.rem(i + 1, 2), dq_acc)
        pc, pe = pc_s[...], pe_s[...]
        for a in range(nsub):
            rows = slice(2 * a * Bq, 2 * (a + 1) * Bq)
            pca, pea, dqa = pc[rows], pe[rows], dq_acc[rows]
            for b in range(a + 1):
                off = pl.multiple_of(i * SB_KEYS + b * Bq, Bq)
                kb = k_ref[pl.ds(off, Bq), :]
                zb = lax.dot_general(qs[a], kb, _NT, preferred_element_type=F32)
                dAb = lax.dot_general(dos[a], v_ref[pl.ds(off, Bq), :], _NT, preferred_element_type=F32)
                A, dz, pca, pea = _sba_sub_bwd(zb, dAb, lts[a], pca, pea, Uincl1, Uexcl1, dmask if b == a else None)
                dqa = dqa + jnp.dot(dz, kb, preferred_element_type=F32)
                dk_acc[pl.ds(off, Bq), :] += lax.dot_general(dz, qs[a], _TN, preferred_element_type=F32)
                dv_acc[pl.ds(off, Bq), :] += lax.dot_general(A, dos[a], _TN, preferred_element_type=F32)
            dq_ref[a * Bq:(a + 1) * Bq, :] = (_unstack_heads(dqa) * scale).astype(BF16)

        @pl.when(i == nq - 1)
        def _():
            dk_ref[...] = dk_acc[...].astype(BF16)
            dv_ref[...] = dv_acc[...].astype(BF16)

    return pl.pallas_call(
        body, name=name, grid=(SB_HEADS // 2, nq),
        in_specs=[pl.BlockSpec((SB_KEYS, 128), lambda p, i: (i, p)), pl.BlockSpec((T, 128), lambda p, i: (0, p)),
                  pl.BlockSpec((T, 128), lambda p, i: (0, p + SB_HEADS // 2)),
                  pl.BlockSpec((None, SB_KEYS, 128), lambda p, i: (p, i, 0)),
                  pl.BlockSpec((SB_KEYS, 128), lambda p, i: (i, p))],
        out_specs=[pl.BlockSpec((SB_KEYS, 128), lambda p, i: (i, p)), pl.BlockSpec((T, 128), lambda p, i: (0, p)),
                   pl.BlockSpec((T, 128), lambda p, i: (0, p))],
        out_shape=[jax.ShapeDtypeStruct((T, D_MODEL), BF16), jax.ShapeDtypeStruct((T, D_MODEL), BF16),
                   jax.ShapeDtypeStruct((T, D_MODEL), BF16)],
        scratch_shapes=[pltpu.VMEM((T, 128), F32), pltpu.VMEM((T, 128), F32),
                        pltpu.VMEM((2, 2 * SB_KEYS, SB_KEYS), F32), pltpu.VMEM((2, 2 * SB_KEYS, SB_KEYS), F32),
                        pltpu.VMEM((2, 2 * SB_KEYS, SB_KEYS), BF16), pltpu.VMEM((2, 2 * SB_KEYS, SB_KEYS), BF16),
                        pltpu.VMEM((2 * SB_KEYS, 1), F32), pltpu.VMEM((2 * SB_KEYS, 1), F32),
                        pltpu.VMEM((2 * SB_KEYS, 1), F32)],
        compiler_params=_cparams(("parallel", "arbitrary")))(q, kv, kv, lt, do)


def _loss_head(h, tgt, w, *, name, tt=512):
    T, D = h.shape
    tt = min(tt, T)

    def body(h_ref, t_ref, w_ref, loss_ref, dh_ref, dw_ref):
        i = pl.program_id(0)
        hv = h_ref[...]
        wv = w_ref[...]
        r = lax.rsqrt(jnp.mean(hv * hv, axis=-1, keepdims=True) + EPS)
        xhat = hv * r
        err = xhat * wv - t_ref[...]
        part = 0.5 * jnp.sum(jnp.mean(err * err, axis=-1, keepdims=True), axis=0, keepdims=True)
        dy = err * (1.0 / D)
        dxh = dy * wv
        dh_ref[...] = r * (dxh - xhat * jnp.mean(dxh * xhat, axis=-1, keepdims=True))
        dwc = jnp.sum(dy * xhat, axis=0, keepdims=True)

        @pl.when(i == 0)
        def _():
            loss_ref[...] = jnp.broadcast_to(part, loss_ref.shape)
            dw_ref[...] = dwc

        @pl.when(i > 0)
        def _():
            loss_ref[...] += jnp.broadcast_to(part, loss_ref.shape)
            dw_ref[...] += dwc

    return pl.pallas_call(
        body, name=name, grid=(T // tt,),
        in_specs=[pl.BlockSpec((tt, D), lambda i: (i, 0)), pl.BlockSpec((tt, D), lambda i: (i, 0)),
                  pl.BlockSpec((1, D), lambda i: (0, 0))],
        out_specs=[pl.BlockSpec((1, 128), lambda i: (0, 0)), pl.BlockSpec((tt, D), lambda i: (i, 0)),
                   pl.BlockSpec((1, D), lambda i: (0, 0))],
        out_shape=[jax.ShapeDtypeStruct((1, 128), F32), jax.ShapeDtypeStruct((T, D), F32),
                   jax.ShapeDtypeStruct((1, D), F32)],
        compiler_params=_cparams(("arbitrary",)))(h, tgt, w.reshape(1, D))


def _adamw(parts, w, m, v, *, name, tr=256):
    P, R, C = parts.shape
    tr = min(tr, R)
    assert R % tr == 0, (name, R, tr)
    c1 = 1.0 - ADAM_B1 ** ADAM_STEP
    c2 = 1.0 - ADAM_B2 ** ADAM_STEP

    def body(p_ref, w_ref, m_ref, v_ref, g_ref, d_ref, nm_ref, nv_ref):
        g = p_ref[0].astype(F32)
        for k in range(1, P):
            g = g + p_ref[k].astype(F32)
        mn = ADAM_B1 * m_ref[...] + (1.0 - ADAM_B1) * g
        vn = ADAM_B2 * v_ref[...] + (1.0 - ADAM_B2) * (g * g)
        g_ref[...] = g
        nm_ref[...] = mn
        nv_ref[...] = vn
        d_ref[...] = -ADAM_LR * ((mn / c1) / (jnp.sqrt(vn / c2) + ADAM_EPS) + ADAM_WD * w_ref[...])

    spec = pl.BlockSpec((tr, C), lambda i: (i, 0))
    sds = jax.ShapeDtypeStruct((R, C), F32)
    return pl.pallas_call(
        body, name=name, grid=(R // tr,),
        in_specs=[pl.BlockSpec((P, tr, C), lambda i: (0, i, 0)), spec, spec, spec],
        out_specs=[spec, spec, spec, spec], out_shape=[sds, sds, sds, sds],
        compiler_params=_cparams(("parallel",)))(parts, w, m, v)


def _all_gather(shards, *, name):
    n = len(shards)

    def body(*refs):
        ins, outs = refs[:n], refs[n:2 * n]
        send_sems, recv_sems, local_sems = refs[2 * n:]
        x, y, c = lax.axis_index("x"), lax.axis_index("y"), lax.axis_index("c")
        me, sib = (x, y, c), (x, y, 1 - c)
        chips = [(1 - x, y), (x, 1 - y), (1 - x, 1 - y)]

        def slot(p):
            return 4 * p[0] + 2 * p[1] + p[2]

        def cp(a, k, block, to, src=None):
            dst = outs[a].at[slot(block)]
            return pltpu.make_async_remote_copy(src_ref=dst if src is None else src, dst_ref=dst,
                                                send_sem=send_sems.at[a, k], recv_sem=recv_sems.at[a, k],
                                                device_id=to, device_id_type=_MESH)

        mine = [pltpu.make_async_copy(ins[a], outs[a].at[slot(me)], local_sems.at[a]) for a in range(n)]
        for m in mine:
            m.start()
        first = []
        for a in range(n):
            first.append(cp(a, 0, me, sib, src=ins[a]))
            for j, chip in enumerate(chips):
                first.append(cp(a, 1 + j, me, (*chip, c), src=ins[a]))
        for f in first:
            f.start()
        passed = []
        for j, chip in enumerate(chips):
            for a in range(n):
                cp(a, 1 + j, (*chip, c), me).wait_recv()
                f = cp(a, 4 + j, (*chip, c), sib)
                f.start()
                passed.append(f)
        for a in range(n):
            cp(a, 0, sib, me).wait_recv()
            for j, chip in enumerate(chips):
                cp(a, 4 + j, (*chip, 1 - c), me).wait_recv()
        for f in first + passed:
            f.wait_send()
        for m in mine:
            m.wait()

    return pl.pallas_call(
        body, name=name, in_specs=[_ANY] * n, out_specs=[_ANY] * n,
        out_shape=[jax.ShapeDtypeStruct((N_DEV,) + s.shape, s.dtype) for s in shards],
        scratch_shapes=[pltpu.SemaphoreType.DMA((n, 7)), pltpu.SemaphoreType.DMA((n, 7)),
                        pltpu.SemaphoreType.DMA((n,))])(*shards)


def _exchange(blocks, *, name):
    n = len(blocks)

    def body(*refs):
        ins, outs = refs[:n], refs[n:2 * n]
        send_sems, recv_sems, local_sems = refs[2 * n:]
        x, y, c = lax.axis_index("x"), lax.axis_index("y"), lax.axis_index("c")
        me = 4 * x + 2 * y + c
        mine = [pltpu.make_async_copy(ins[a].at[me], outs[a].at[me], local_sems.at[a]) for a in range(n)]
        for m in mine:
            m.start()
        copies = []
        for r in range(1, N_DEV):
            rx, ry, rc = (r >> 2) & 1, (r >> 1) & 1, r & 1
            px, py, pc = (1 - x if rx else x), (1 - y if ry else y), (1 - c if rc else c)
            peer = 4 * px + 2 * py + pc
            for a in range(n):
                copies.append((pltpu.make_async_remote_copy(
                    src_ref=ins[a].at[peer], dst_ref=outs[a].at[me], send_sem=send_sems.at[a, r - 1],
                    recv_sem=recv_sems.at[a, r - 1], device_id=(px, py, pc), device_id_type=_MESH),
                    pltpu.make_async_remote_copy(
                    src_ref=ins[a].at[peer], dst_ref=outs[a].at[peer], send_sem=send_sems.at[a, r - 1],
                    recv_sem=recv_sems.at[a, r - 1], device_id=(px, py, pc), device_id_type=_MESH)))
        for snd, _ in copies:
            snd.start()
        for _, rcv in copies:
            rcv.wait_recv()
        for snd, _ in copies:
            snd.wait_send()
        for m in mine:
            m.wait()

    return pl.pallas_call(
        body, name=name, in_specs=[_ANY] * n, out_specs=[_ANY] * n,
        out_shape=[jax.ShapeDtypeStruct(b.shape, b.dtype) for b in blocks],
        scratch_shapes=[pltpu.SemaphoreType.DMA((n, 7)), pltpu.SemaphoreType.DMA((n, 7)),
                        pltpu.SemaphoreType.DMA((n,))])(*blocks)


_HBM = pl.BlockSpec(memory_space=pltpu.HBM)
_SEM = pl.BlockSpec(memory_space=pltpu.SEMAPHORE)
_EFFECT = pltpu.SideEffectType.DATAFLOW_SIDE_EFFECTING


def _peers():
    x, y, c = lax.axis_index("x"), lax.axis_index("y"), lax.axis_index("c")
    out = []
    for r in range(1, N_DEV):
        px = 1 - x if (r >> 2) & 1 else x
        py = 1 - y if (r >> 1) & 1 else y
        pc = 1 - c if r & 1 else c
        out.append(((px, py, pc), 4 * px + 2 * py + pc))
    return 4 * x + 2 * y + c, out


def _push_copy(src_ref, land_ref, send_sems, recv_sems, a, k, me, peer, peer_slot, scatter, arriving):
    src = src_ref.at[peer_slot] if scatter else src_ref
    return pltpu.make_async_remote_copy(
        src_ref=src, dst_ref=land_ref.at[peer_slot if arriving else me], send_sem=send_sems.at[a * (N_DEV - 1) + k],
        recv_sem=recv_sems.at[a * (N_DEV - 1) + k], device_id=peer, device_id_type=_MESH)


def _push_start(srcs, *, scatter, name):
    n = len(srcs)
    lands = [lax.empty(s.shape if scatter else (N_DEV,) + s.shape, s.dtype) for s in srcs]

    def body(*refs):
        src_refs, land_refs = refs[:n], refs[n:2 * n]
        send_sems, recv_sems = refs[2 * n], refs[2 * n + 1]
        token = refs[-1]
        me, peers = _peers()
        for k, (peer, slot) in enumerate(peers):
            for a in range(n):
                _push_copy(src_refs[a], land_refs[a], send_sems, recv_sems, a, k, me, peer, slot, scatter, False).start()
        token[...] = jnp.zeros_like(token)

    hbm = lambda a: pltpu.HBM(a.shape, a.dtype)
    outs = pl.pallas_call(
        body, name=name,
        out_shape=(pltpu.SemaphoreType.DMA((n * (N_DEV - 1),)), pltpu.SemaphoreType.DMA((n * (N_DEV - 1),)),
                   *[hbm(s) for s in srcs], *[hbm(l) for l in lands], jax.ShapeDtypeStruct((8, 128), F32)),
        in_specs=[_HBM] * (2 * n),
        out_specs=(_SEM, _SEM, *([_HBM] * (2 * n)), pl.BlockSpec(memory_space=pltpu.VMEM)),
        input_output_aliases={i: 2 + i for i in range(2 * n)},
        compiler_params=pltpu.CompilerParams(has_side_effects=_EFFECT),
    )(*[pltpu.with_memory_space_constraint(s, pltpu.HBM) for s in srcs],
      *[pltpu.with_memory_space_constraint(l, pltpu.HBM) for l in lands])
    return dict(send=outs[0], recv=outs[1], srcs=list(outs[2:2 + n]), lands=list(outs[2 + n:2 + 2 * n]),
                token=outs[-1], scatter=scatter, n=n)


def _push_wait(h, after, *, name):
    n, scatter = h["n"], h["scatter"]

    def body(*refs):
        src_refs, land_refs = refs[:n], refs[n:2 * n]
        send_sems, recv_sems = refs[2 * n], refs[2 * n + 1]
        me, peers = _peers()
        for k, (peer, slot) in enumerate(peers):
            for a in range(n):
                cp = _push_copy(src_refs[a], land_refs[a], send_sems, recv_sems, a, k, me, peer, slot, scatter, True)
                cp.wait_send()
                cp.wait_recv()

    hbm = lambda a: pltpu.HBM(a.shape, a.dtype)
    outs = pl.pallas_call(
        body, name=name,
        out_shape=(*[hbm(s) for s in h["srcs"]], *[hbm(l) for l in h["lands"]]),
        in_specs=[_HBM] * (2 * n) + [_SEM, _SEM, _ANY], out_specs=tuple([_HBM] * (2 * n)),
        input_output_aliases={i: i for i in range(2 * n)},
        compiler_params=pltpu.CompilerParams(has_side_effects=_EFFECT),
    )(*h["srcs"], *h["lands"], h["send"], h["recv"], after)
    return list(outs[:n]), list(outs[n:])


def _ffn_fwd(h, nw, w_up, conv_w, conv_b, w_down, tag):
    a3 = _mm_fwd(h, w_up, norm_w=nw, name=f"ffn{tag}_up", out_dtype=BF16, halves=True, tm=1024, tn=1408)
    p = _ffn_conv_fwd3(a3, conv_w, conv_b.reshape(1, -1), name=f"ffn{tag}_conv")
    h_out = _mm_fwd(p, w_down, residual=h, name=f"ffn{tag}_down", tm=1024, tn=512)
    return h_out, (a3, p)


def _ffn_bwd(dh, h, saved, nw, w_up, conv_w, conv_b, w_down, tag):
    a3, p = saved
    g_down = _mm_tn(p, dh, name=f"ffn{tag}_down_wg", tk1=1408, tn=1024)
    dp = _mm_nt(dh, w_down, name=f"ffn{tag}_down_dg", out_dtype=BF16, tm=1024, tn=1408, tk=1024)
    dhid3, dw3, db3 = _ffn_conv_bwd3(a3, conv_w, conv_b.reshape(1, -1), dp, name=f"ffn{tag}_conv_bwd")
    da3 = _conv_bwd_in3(dhid3, conv_w, K=FFN_CONV, name=f"ffn{tag}_conv_bwd_in")
    g_up = _mm_tn(h, da3, norm_w=nw, name=f"ffn{tag}_up_wg", tn=1408, tt=1024)
    dh_out, g_nw = _mm_nt(da3, w_up, epi=(h, nw, dh), name=f"ffn{tag}_up_dg", tm=1024, tk=1408)
    g_cw = jnp.concatenate([dw3[0], dw3[1]], axis=1)
    g_cb = jnp.concatenate([db3[0], db3[1]], axis=1)
    return dh_out, dict(norm=g_nw.reshape(-1), up=g_up, conv_w=g_cw, conv_b=g_cb.reshape(-1), down=g_down)


def _local_step(x, tgt, W):
    T = x.shape[0]
    f = {}
    zx = _mm_fwd(x, W["in_w"], norm_w=W["ssm_norm_w"], name="ssm_in", tm=1024, tn=896)
    xbc_c = _ssm_conv_fwd(zx, W["ssm_conv_w"], W["ssm_conv_b"].reshape(1, -1), name="ssm_conv")
    dt_raw = zx[:, D_INNER + CONV_DIM:IN_PROJ_DIM]
    dtg = jnp.pad(dt_raw.reshape(T, SSM_GROUPS, 8).transpose(1, 0, 2), ((0, 0), (0, 0), (0, 120)))
    par = jnp.stack([W["ssm_dt_bias"].reshape(SSM_GROUPS, 8), W["ssm_a_log"].reshape(SSM_GROUPS, 8),
                     W["ssm_d"].reshape(SSM_GROUPS, 8)], axis=1)
    par = jnp.pad(par, ((0, 0), (0, 5), (0, 120)))
    gnw = W["ssm_gate_norm_w"].reshape(1, D_INNER)
    y, yn, st = _ssd_fwd(xbc_c, zx, dtg, par, gnw, name="ssd_fwd")
    h1 = _mm_fwd(yn, W["ssm_out_w"], residual=x, name="ssm_out", tm=1024, tn=512)
    h2, ffn0 = _ffn_fwd(h1, W["ffn_norm_w"][0], W["ffn_up_w"][0], W["ffn_conv_w"][0], W["ffn_conv_b"][0],
                        W["ffn_down_w"][0], "0")
    q = _mm_fwd(h2, W["w_q"], norm_w=W["attn_norm_w"], out_dtype=BF16, name="attn_q", tm=1024, tn=1024)
    kv = _mm_fwd(h2, W["w_kv"], norm_w=W["kv_norm_w"], out_dtype=BF16, name="attn_kv", tm=1024, tn=1024)
    o, lt = _sba_fwd(q, kv, name="sba_fwd")
    h3 = _mm_fwd(o, W["w_o"], residual=h2, name="attn_o", tm=1024, tn=512)
    h4, ffn1 = _ffn_fwd(h3, W["ffn_norm_w"][1], W["ffn_up_w"][1], W["ffn_conv_w"][1], W["ffn_conv_b"][1],
                        W["ffn_down_w"][1], "1")
    loss, dh4, g_final = _loss_head(h4, tgt, W["final_norm_w"], name="loss_head")
    dh3, gf1 = _ffn_bwd(dh4, h3, ffn1, W["ffn_norm_w"][1], W["ffn_up_w"][1], W["ffn_conv_w"][1], W["ffn_conv_b"][1],
                        W["ffn_down_w"][1], "1")
    g_wo = _mm_tn(o, dh3, name="attn_o_wg", tn=1024)
    do = _mm_nt(dh3, W["w_o"], name="attn_o_dg", out_dtype=BF16, tn=1024, tk=1024)
    dq, dk, dv = _sba_bwd(q, kv, lt, do, name="sba_bwd")
    g_wq = _mm_tn(h2, dq, norm_w=W["attn_norm_w"], name="attn_q_wg", tn=1024)
    dh2a, g_attn_nw = _mm_nt(dq, W["w_q"], epi=(h2, W["attn_norm_w"], dh3), name="attn_q_dg", tk=1024)
    dkv = jnp.concatenate([dk, dv], axis=1)
    g_wkv = _mm_tn(h2, dkv, norm_w=W["kv_norm_w"], name="attn_kv_wg", tn=1024)
    dh2, g_kv_nw = _mm_nt(dkv, W["w_kv"], epi=(h2, W["kv_norm_w"], dh2a), name="attn_kv_dg", tk=1024)
    dh1, gf0 = _ffn_bwd(dh2, h1, ffn0, W["ffn_norm_w"][0], W["ffn_up_w"][0], W["ffn_conv_w"][0], W["ffn_conv_b"][0],
                        W["ffn_down_w"][0], "0")
    g_out = _mm_tn(yn, dh1, name="ssm_out_wg", tn=1024)
    dyn = _mm_nt(dh1, W["ssm_out_w"], name="ssm_out_dg", out_dtype=BF16, tn=1024, tk=1024)
    dxs, dB, dC, dz, ddt, g_gnw, dpar = _ssd_bwd(xbc_c, zx, dtg, par, gnw, y, st, dyn, name="ssd_bwd")
    dxbc_c = jnp.concatenate([dxs, dB, dC], axis=1)
    dhid, g_scw, g_scb = _ssm_conv_bwd_pre(zx, W["ssm_conv_w"], W["ssm_conv_b"].reshape(1, -1), dxbc_c,
                                           name="ssm_conv_bwd")
    dxbc = _conv_bwd_in(dhid, W["ssm_conv_w"], K=SSM_CONV, name="ssm_conv_bwd_in")
    ddt_t = ddt[:, :, :8].transpose(1, 0, 2).reshape(T, SSM_HEADS).astype(BF16)
    dzx = jnp.concatenate([dz, dxbc, jnp.pad(ddt_t, ((0, 0), (0, IN_PROJ_PAD - IN_PROJ_DIM)))], axis=1)
    g_in = _mm_tn(x, dzx, norm_w=W["ssm_norm_w"], name="ssm_in_wg", tn=896)
    dx, g_ssm_nw = _mm_nt(dzx, W["in_w"], epi=(x, W["ssm_norm_w"], dh1), name="ssm_in_dg", tk=1792)
    f["ssm_norm_w"] = g_ssm_nw.reshape(-1)
    f["ssm_in_w"] = g_in[:, :IN_PROJ_DIM]
    f["ssm_conv_w"] = g_scw
    f["ssm_conv_b"] = g_scb.reshape(-1)
    f["ssm_dt_bias"] = dpar[:, 0, :8].reshape(-1)
    f["ssm_a_log"] = dpar[:, 1, :8].reshape(-1)
    f["ssm_d"] = dpar[:, 2, :8].reshape(-1)
    f["ssm_gate_norm_w"] = g_gnw.reshape(-1)
    f["ssm_out_w"] = g_out
    f["kv_norm_w"] = g_kv_nw.reshape(-1)
    f["w_k"] = g_wkv[:, :D_MODEL]
    f["w_v"] = g_wkv[:, D_MODEL:]
    f["attn_norm_w"] = g_attn_nw.reshape(-1)
    f["w_q"] = g_wq
    f["w_o"] = g_wo
    f["ffn_norm_w"] = jnp.stack([gf0["norm"], gf1["norm"]])
    f["ffn_up_w"] = [gf0["up"], gf1["up"]]
    f["ffn_conv_w"] = jnp.stack([gf0["conv_w"], gf1["conv_w"]])
    f["ffn_conv_b"] = jnp.stack([gf0["conv_b"], gf1["conv_b"]])
    f["ffn_down_w"] = [gf0["down"], gf1["down"]]
    f["final_norm_w"] = g_final.reshape(-1)
    return loss, dx, f


_BIG = ["ssm_in_w", "ssm_out_w", "w_k", "w_v", "w_q", "w_o", "ffn_up_w", "ffn_down_w"]
_SMALL_SHARDED = ["ssm_norm_w", "ssm_conv_w", "ssm_conv_b", "ssm_gate_norm_w", "ffn_conv_w"]
_SMALL_REPL = ["ssm_dt_bias", "ssm_a_log", "ssm_d", "kv_norm_w", "attn_norm_w", "ffn_norm_w", "ffn_conv_b",
               "final_norm_w"]
_WEIGHTS = ["ssm_norm_w", "ssm_in_w", "ssm_conv_w", "ssm_conv_b", "ssm_dt_bias", "ssm_a_log", "ssm_d",
            "ssm_gate_norm_w", "ssm_out_w", "kv_norm_w", "w_k", "w_v", "attn_norm_w", "w_q", "w_o", "ffn_norm_w",
            "ffn_up_w", "ffn_conv_w", "ffn_conv_b", "ffn_down_w", "final_norm_w"]


def _as2d(a):
    return a.reshape(-1, a.shape[-1])


def _cols_to_full(g):
    return g.transpose(1, 0, 2).reshape(g.shape[1], N_DEV * g.shape[2])


def _full_to_cols(a):
    R = a.shape[0]
    return a.reshape(R, N_DEV, -1).transpose(1, 0, 2)


def _gather_weights(p):
    names = _BIG + _SMALL_SHARDED
    shards = [_as2d(p[n]).astype(BF16) for n in _BIG] + [_as2d(p[n]) for n in _SMALL_SHARDED]
    got = dict(zip(names, _all_gather(shards, name="gather_weights")))
    W = {n: p[n] for n in _SMALL_REPL}
    in_w = _cols_to_full(got["ssm_in_w"])
    W["in_w"] = jnp.pad(in_w, ((0, 0), (0, IN_PROJ_PAD - IN_PROJ_DIM)))
    W["ssm_out_w"] = got["ssm_out_w"].reshape(D_INNER, D_MODEL)
    W["w_kv"] = jnp.concatenate([got["w_k"].reshape(D_MODEL, D_MODEL), got["w_v"].reshape(D_MODEL, D_MODEL)], axis=1)
    W["w_q"] = got["w_q"].reshape(D_MODEL, D_MODEL)
    W["w_o"] = got["w_o"].reshape(D_MODEL, D_MODEL)
    up = got["ffn_up_w"]
    W["ffn_up_w"] = [_cols_to_full(up[:, l * D_MODEL:(l + 1) * D_MODEL]) for l in range(2)]
    dn = got["ffn_down_w"]
    rs = D_FF // N_DEV
    W["ffn_down_w"] = [dn[:, l * rs:(l + 1) * rs].reshape(D_FF, D_MODEL) for l in range(2)]
    W["ssm_norm_w"] = got["ssm_norm_w"].reshape(D_MODEL)
    W["ssm_conv_w"] = _cols_to_full(got["ssm_conv_w"])
    W["ssm_conv_b"] = got["ssm_conv_b"].reshape(CONV_DIM)
    W["ssm_gate_norm_w"] = got["ssm_gate_norm_w"].reshape(D_INNER)
    fcw = _cols_to_full(got["ffn_conv_w"])
    W["ffn_conv_w"] = fcw.reshape(2, FFN_CONV, 2 * D_FF)
    for n in ("ssm_dt_bias", "ssm_a_log", "ssm_d", "attn_norm_w"):
        W[n] = W[n].reshape(-1)
    return W


def _big_grad_blocks(f):
    rs = D_FF // N_DEV
    return {
        "ssm_in_w": _full_to_cols(f["ssm_in_w"]),
        "ssm_out_w": f["ssm_out_w"].reshape(N_DEV, D_INNER // N_DEV, D_MODEL),
        "w_k": f["w_k"].reshape(N_DEV, D_MODEL // N_DEV, D_MODEL),
        "w_v": f["w_v"].reshape(N_DEV, D_MODEL // N_DEV, D_MODEL),
        "w_q": f["w_q"].reshape(N_DEV, D_MODEL // N_DEV, D_MODEL),
        "w_o": f["w_o"].reshape(N_DEV, D_MODEL // N_DEV, D_MODEL),
        "ffn_up_w": jnp.concatenate([_full_to_cols(g) for g in f["ffn_up_w"]], axis=1),
        "ffn_down_w": jnp.concatenate([g.reshape(N_DEV, rs, D_MODEL) for g in f["ffn_down_w"]], axis=1),
    }


def _pack_small(vals):
    flat = jnp.concatenate([v.reshape(-1).astype(F32) for v in vals])
    n = flat.shape[0]
    rows = -(-n // 1024) * 8
    return jnp.pad(flat, (0, rows * 128 - n)).reshape(rows, 128)


def _unpack_small(packed, shapes):
    flat = packed.reshape(-1)
    out, off = [], 0
    for s in shapes:
        n = math.prod(s)
        out.append(flat[off:off + n].reshape(s))
        off += n
    return out


def _kernel_v1(x, ssm_norm_w, ssm_in_w, ssm_conv_w, ssm_conv_b, ssm_dt_bias, ssm_a_log, ssm_d, ssm_gate_norm_w, ssm_out_w, kv_norm_w, w_k, w_v, attn_norm_w, w_q, w_o, ffn_norm_w, ffn_up_w, ffn_conv_w, ffn_conv_b, ffn_down_w, final_norm_w, loss_target, m_ssm_norm_w, m_ssm_in_w, m_ssm_conv_w, m_ssm_conv_b, m_ssm_dt_bias, m_ssm_a_log, m_ssm_d, m_ssm_gate_norm_w, m_ssm_out_w, m_kv_norm_w, m_w_k, m_w_v, m_attn_norm_w, m_w_q, m_w_o, m_ffn_norm_w, m_ffn_up_w, m_ffn_conv_w, m_ffn_conv_b, m_ffn_down_w, m_final_norm_w, v_ssm_norm_w, v_ssm_in_w, v_ssm_conv_w, v_ssm_conv_b, v_ssm_dt_bias, v_ssm_a_log, v_ssm_d, v_ssm_gate_norm_w, v_ssm_out_w, v_kv_norm_w, v_w_k, v_w_v, v_attn_norm_w, v_w_q, v_w_o, v_ffn_norm_w, v_ffn_up_w, v_ffn_conv_w, v_ffn_conv_b, v_ffn_down_w, v_final_norm_w):
    env = dict(locals())
    p = {n: env[n] for n in _WEIGHTS}
    mom = {n: env["m_" + n] for n in _WEIGHTS}
    var = {n: env["v_" + n] for n in _WEIGHTS}
    T = x.shape[1]
    me = 4 * lax.axis_index("x") + 2 * lax.axis_index("y") + lax.axis_index("c")

    W = _gather_weights(p)
    loss_row, dx, f = _local_step(x.reshape(T, D_MODEL), loss_target.reshape(T, D_MODEL), W)
    loss = lax.psum(loss_row[0, 0], ("x", "y", "c"))

    big = _big_grad_blocks(f)
    small_names = _SMALL_REPL + _SMALL_SHARDED
    small_full = _pack_small([f[n] for n in small_names])
    small_bcast = jnp.broadcast_to(small_full[None], (N_DEV,) + small_full.shape)
    got = _exchange([big[n] for n in _BIG] + [small_bcast], name="exchange_grads")
    big_parts = dict(zip(_BIG, got[:-1]))

    zero = jnp.zeros_like(small_full)
    g_small_sum = _adamw(got[-1], zero, zero, zero, name="sum_small_grads", tr=small_full.shape[0])[0]
    full_shapes = [f[n].shape for n in small_names]
    g_small = dict(zip(small_names, _unpack_small(g_small_sum, full_shapes)))
    for n in _SMALL_SHARDED:
        width = p[n].shape[-1]
        g_small[n] = lax.dynamic_slice_in_dim(g_small[n], me * width, width, axis=g_small[n].ndim - 1)

    out_g, out_d, out_m, out_v = {}, {}, {}, {}
    for n in _BIG:
        w2, m2, v2 = _as2d(p[n]), _as2d(mom[n]), _as2d(var[n])
        tr = 352 if n == "ffn_down_w" else 256
        g, d, nm, nv = _adamw(big_parts[n], w2, m2, v2, name="adamw_" + n, tr=tr)
        out_g[n], out_d[n], out_m[n], out_v[n] = (t.reshape(p[n].shape) for t in (g, d, nm, nv))
    sw = _pack_small([p[n] for n in small_names])
    sm = _pack_small([mom[n] for n in small_names])
    sv = _pack_small([var[n] for n in small_names])
    sg = _pack_small([g_small[n] for n in small_names])
    _, d, nm, nv = _adamw(sg[None], sw, sm, sv, name="adamw_small", tr=sw.shape[0])
    shard_shapes = [p[n].shape for n in small_names]
    for n, dd, mm, vv in zip(small_names, _unpack_small(d, shard_shapes), _unpack_small(nm, shard_shapes),
                             _unpack_small(nv, shard_shapes)):
        out_g[n] = g_small[n].reshape(p[n].shape)
        out_d[n], out_m[n], out_v[n] = dd, mm, vv

    return (loss, dx.reshape(x.shape), *[out_g[n] for n in _WEIGHTS], *[out_d[n] for n in _WEIGHTS],
            *[out_m[n] for n in _WEIGHTS], *[out_v[n] for n in _WEIGHTS])


def _tie(a, token):
    return a + token[0, 0].astype(a.dtype)


def _local_step2(x, tgt, get_w, put_g):
    T = x.shape[0]
    Ws = get_w("ssm", None)
    fnw, fcw, fcb = Ws["ffn_norm_w"], Ws["ffn_conv_w"], Ws["ffn_conv_b"]
    zx = _mm_fwd(x, Ws["in_w"], norm_w=Ws["ssm_norm_w"], name="ssm_in", tm=1024, tn=896)
    xbc_c = _ssm_conv_fwd(zx, Ws["ssm_conv_w"], Ws["ssm_conv_b"].reshape(1, -1), name="ssm_conv")
    dt_raw = zx[:, D_INNER + CONV_DIM:IN_PROJ_DIM]
    dtg = jnp.pad(dt_raw.reshape(T, SSM_GROUPS, 8).transpose(1, 0, 2), ((0, 0), (0, 0), (0, 120)))
    par = jnp.stack([Ws["ssm_dt_bias"].reshape(SSM_GROUPS, 8), Ws["ssm_a_log"].reshape(SSM_GROUPS, 8),
                     Ws["ssm_d"].reshape(SSM_GROUPS, 8)], axis=1)
    par = jnp.pad(par, ((0, 0), (0, 5), (0, 120)))
    gnw = _tie(Ws["ssm_gate_norm_w"].reshape(1, D_INNER), get_w("rest_start", xbc_c))
    y, yn, st = _ssd_fwd(xbc_c, zx, dtg, par, gnw, name="ssd_fwd")
    W0 = get_w("ffn0", y)
    Ws["ssm_out_w"] = W0["ssm_out_w"]
    h1 = _mm_fwd(yn, Ws["ssm_out_w"], residual=x, name="ssm_out", tm=1024, tn=512)
    h2, ffn0 = _ffn_fwd(h1, fnw[0], W0["up"], fcw[0], fcb[0], W0["down"], "0")
    Wr = get_w("rest", h2)
    q = _mm_fwd(h2, Wr["w_q"], norm_w=Ws["attn_norm_w"], out_dtype=BF16, name="attn_q", tm=1024, tn=1024)
    kv = _mm_fwd(h2, Wr["w_kv"], norm_w=Ws["kv_norm_w"], out_dtype=BF16, name="attn_kv", tm=1024, tn=1024)
    o, lt = _sba_fwd(q, kv, name="sba_fwd")
    h3 = _mm_fwd(o, Wr["w_o"], residual=h2, name="attn_o", tm=1024, tn=512)
    h4, ffn1 = _ffn_fwd(h3, fnw[1], Wr["up"], fcw[1], fcb[1], Wr["down"], "1")
    loss, dh4, g_final = _loss_head(h4, tgt, Ws["final_norm_w"], name="loss_head")
    dh3, gf1 = _ffn_bwd(dh4, h3, ffn1, fnw[1], Wr["up"], fcw[1], fcb[1], Wr["down"], "1")
    tok = put_g("ffn1", dict(up=gf1["up"], down=gf1["down"]))
    g_wo = _mm_tn(o, dh3, name="attn_o_wg", tn=1024)
    do = _mm_nt(dh3, _tie(Wr["w_o"], tok), name="attn_o_dg", out_dtype=BF16, tn=1024, tk=1024)
    dq, dk, dv = _sba_bwd(q, kv, lt, do, name="sba_bwd")
    g_wq = _mm_tn(h2, dq, norm_w=Ws["attn_norm_w"], name="attn_q_wg", tn=1024, tt=1024)
    dh2a, g_attn_nw = _mm_nt(dq, Wr["w_q"], epi=(h2, Ws["attn_norm_w"], dh3), name="attn_q_dg", tm=1024, tk=1024)
    dkv = jnp.concatenate([dk, dv], axis=1)
    g_wkv = _mm_tn(h2, dkv, norm_w=Ws["kv_norm_w"], name="attn_kv_wg", tn=1024, tt=1024)
    dh2, g_kv_nw = _mm_nt(dkv, Wr["w_kv"], epi=(h2, Ws["kv_norm_w"], dh2a), name="attn_kv_dg", tm=1024, tk=1024)
    tok = put_g("attn", dict(w_o=g_wo, w_q=g_wq, w_k=g_wkv[:, :D_MODEL], w_v=g_wkv[:, D_MODEL:]))
    dh1, gf0 = _ffn_bwd(dh2, h1, ffn0, fnw[0], W0["up"], fcw[0], _tie(fcb[0], tok), W0["down"], "0")
    tok = put_g("ffn0", dict(up=gf0["up"], down=gf0["down"]))
    g_out = _mm_tn(yn, dh1, name="ssm_out_wg", tn=1024)
    dyn = _mm_nt(dh1, _tie(Ws["ssm_out_w"], tok), name="ssm_out_dg", out_dtype=BF16, tn=1024, tk=1024)
    tok = put_g("ssm_out", dict(ssm_out_w=g_out))
    dxbc_c, dz, ddt, g_gnw, dpar = _ssd_bwd(xbc_c, zx, dtg, par, _tie(gnw, tok), y, st, dyn, name="ssd_bwd")
    dhid, g_scw, g_scb = _ssm_conv_bwd_pre(zx, Ws["ssm_conv_w"], Ws["ssm_conv_b"].reshape(1, -1), dxbc_c,
                                           name="ssm_conv_bwd")
    dxbc = _conv_bwd_in(dhid, Ws["ssm_conv_w"], K=SSM_CONV, name="ssm_conv_bwd_in")
    ddt_t = ddt[:, :, :8].transpose(1, 0, 2).reshape(T, SSM_HEADS).astype(BF16)
    dzx = jnp.concatenate([dz, dxbc, jnp.pad(ddt_t, ((0, 0), (0, IN_PROJ_PAD - IN_PROJ_DIM)))], axis=1)
    g_in = _mm_tn(x, dzx, norm_w=Ws["ssm_norm_w"], name="ssm_in_wg", tn=896, tt=1024)
    tok = put_g("ssm_in", dict(ssm_in_w=g_in[:, :IN_PROJ_DIM]))
    dx, g_ssm_nw = _mm_nt(dzx, Ws["in_w"], epi=(x, _tie(Ws["ssm_norm_w"], tok), dh1), name="ssm_in_dg", tm=1024, tk=896)
    f = {
        "ssm_norm_w": g_ssm_nw.reshape(-1), "ssm_conv_w": g_scw,
        "ssm_conv_b": g_scb.reshape(-1), "ssm_dt_bias": dpar[:, 0, :8].reshape(-1),
        "ssm_a_log": dpar[:, 1, :8].reshape(-1), "ssm_d": dpar[:, 2, :8].reshape(-1),
        "ssm_gate_norm_w": g_gnw.reshape(-1), "kv_norm_w": g_kv_nw.reshape(-1), "attn_norm_w": g_attn_nw.reshape(-1),
        "ffn_norm_w": jnp.stack([gf0["norm"], gf1["norm"]]), "ffn_conv_w": jnp.stack([gf0["conv_w"], gf1["conv_w"]]),
        "ffn_conv_b": jnp.stack([gf0["conv_b"], gf1["conv_b"]]), "final_norm_w": g_final.reshape(-1),
    }
    return loss, dx, f


def kernel(x, ssm_norm_w, ssm_in_w, ssm_conv_w, ssm_conv_b, ssm_dt_bias, ssm_a_log, ssm_d, ssm_gate_norm_w, ssm_out_w, kv_norm_w, w_k, w_v, attn_norm_w, w_q, w_o, ffn_norm_w, ffn_up_w, ffn_conv_w, ffn_conv_b, ffn_down_w, final_norm_w, loss_target, m_ssm_norm_w, m_ssm_in_w, m_ssm_conv_w, m_ssm_conv_b, m_ssm_dt_bias, m_ssm_a_log, m_ssm_d, m_ssm_gate_norm_w, m_ssm_out_w, m_kv_norm_w, m_w_k, m_w_v, m_attn_norm_w, m_w_q, m_w_o, m_ffn_norm_w, m_ffn_up_w, m_ffn_conv_w, m_ffn_conv_b, m_ffn_down_w, m_final_norm_w, v_ssm_norm_w, v_ssm_in_w, v_ssm_conv_w, v_ssm_conv_b, v_ssm_dt_bias, v_ssm_a_log, v_ssm_d, v_ssm_gate_norm_w, v_ssm_out_w, v_kv_norm_w, v_w_k, v_w_v, v_attn_norm_w, v_w_q, v_w_o, v_ffn_norm_w, v_ffn_up_w, v_ffn_conv_w, v_ffn_conv_b, v_ffn_down_w, v_final_norm_w):
    env = dict(locals())
    p = {n: env[n] for n in _WEIGHTS}
    mom = {n: env["m_" + n] for n in _WEIGHTS}
    var = {n: env["v_" + n] for n in _WEIGHTS}
    T = x.shape[1]
    me = 4 * lax.axis_index("x") + 2 * lax.axis_index("y") + lax.axis_index("c")
    rs = D_FF // N_DEV

    def bf2(a):
        return _as2d(a).astype(BF16)

    def with_own(srcs, lands, scatter):
        out = []
        for s, l in zip(srcs, lands):
            own = lax.dynamic_index_in_dim(s, me, 0, keepdims=False) if scatter else s
            out.append(lax.dynamic_update_index_in_dim(l, own, me, 0))
        return out

    a_names = ["ssm_in_w"] + _SMALL_SHARDED
    got_a = dict(zip(a_names, _all_gather([bf2(p["ssm_in_w"])] + [_as2d(p[n]) for n in _SMALL_SHARDED],
                                          name="gather_ssm")))
    ffn0_names = ["ssm_out_w", "up0", "down0"]
    rest_names = ["w_q", "w_k", "w_v", "w_o", "up1", "down1"]
    shard = {"up0": bf2(p["ffn_up_w"][0]), "down0": bf2(p["ffn_down_w"][0]), "up1": bf2(p["ffn_up_w"][1]),
             "down1": bf2(p["ffn_down_w"][1]), "w_q": bf2(p["w_q"]), "w_k": bf2(p["w_k"]), "w_v": bf2(p["w_v"]),
             "w_o": bf2(p["w_o"]), "ssm_out_w": bf2(p["ssm_out_w"])}
    h_ffn0 = _push_start([shard[n] for n in ffn0_names], scatter=False, name="gather_ffn0_start")
    handles = {}

    def get_w(group, after):
        if group == "ssm":
            W = {n: p[n] for n in _SMALL_REPL}
            for n in ("ssm_dt_bias", "ssm_a_log", "ssm_d", "attn_norm_w"):
                W[n] = W[n].reshape(-1)
            W["in_w"] = jnp.pad(_cols_to_full(got_a["ssm_in_w"]), ((0, 0), (0, IN_PROJ_PAD - IN_PROJ_DIM)))
            W["ssm_norm_w"] = _tie(got_a["ssm_norm_w"].reshape(D_MODEL), h_ffn0["token"])
            W["ssm_conv_w"] = _cols_to_full(got_a["ssm_conv_w"])
            W["ssm_conv_b"] = got_a["ssm_conv_b"].reshape(CONV_DIM)
            W["ssm_gate_norm_w"] = got_a["ssm_gate_norm_w"].reshape(D_INNER)
            W["ffn_conv_w"] = _cols_to_full(got_a["ffn_conv_w"]).reshape(2, FFN_CONV, 2 * D_FF)
            return W
        if group == "rest_start":
            anchor = after[0, 0]
            first = shard[rest_names[0]] + (jnp.where(jnp.isfinite(anchor), anchor, 0.0) * 0.0).astype(BF16)
            handles["rest"] = _push_start([first] + [shard[n] for n in rest_names[1:]], scatter=False,
                                          name="gather_rest_start")
            return handles["rest"]["token"]
        if group == "ffn0":
            srcs, lands = _push_wait(h_ffn0, after, name="gather_ffn0_wait")
            out, up, down = with_own(srcs, lands, False)
            return dict(ssm_out_w=out.reshape(D_INNER, D_MODEL), up=_cols_to_full(up), down=down.reshape(D_FF, D_MODEL))
        srcs, lands = _push_wait(handles["rest"], after, name="gather_rest_wait")
        g = dict(zip(rest_names, with_own(srcs, lands, False)))
        sq = lambda a: a.reshape(D_MODEL, D_MODEL)
        return dict(w_q=sq(g["w_q"]), w_kv=jnp.concatenate([sq(g["w_k"]), sq(g["w_v"])], axis=1), w_o=sq(g["w_o"]),
                    up=_cols_to_full(g["up1"]), down=g["down1"].reshape(D_FF, D_MODEL))

    pending = []

    def put_g(group, g):
        if group in ("ffn0", "ffn1"):
            keys = [("ffn_up_w", int(group[-1])), ("ffn_down_w", int(group[-1]))]
            blocks = [_full_to_cols(g["up"]), g["down"].reshape(N_DEV, rs, D_MODEL)]
        elif group == "attn":
            keys = [(n, None) for n in ("w_o", "w_q", "w_k", "w_v")]
            blocks = [g[n].reshape(N_DEV, D_MODEL // N_DEV, D_MODEL) for n, _ in keys]
        elif group == "ssm_out":
            keys = [("ssm_out_w", None)]
            blocks = [g["ssm_out_w"].reshape(N_DEV, D_INNER // N_DEV, D_MODEL)]
        else:
            keys = [("ssm_in_w", None)]
            blocks = [_full_to_cols(g["ssm_in_w"])]
        h = _push_start(blocks, scatter=True, name=f"exchange_{group}_start")
        pending.append((group, keys, h))
        return h["token"]

    loss_row, dx, f = _local_step2(x.reshape(T, D_MODEL), loss_target.reshape(T, D_MODEL), get_w, put_g)
    loss = lax.psum(loss_row[0, 0], ("x", "y", "c"))

    small_names = _SMALL_REPL + _SMALL_SHARDED
    small_full = _pack_small([f[n] for n in small_names])
    small_bcast = jnp.broadcast_to(small_full[None], (N_DEV,) + small_full.shape)
    h_small = _push_start([small_bcast], scatter=True, name="exchange_small_start")
    tok = h_small["token"]

    res = {}
    for group, keys, h in pending:
        srcs, lands = _push_wait(h, dx, name=f"exchange_{group}_wait")
        for (n, layer), parts in zip(keys, with_own(srcs, lands, True)):
            sel = (lambda a: a) if layer is None else (lambda a: a[layer])
            w2, m2, v2 = _as2d(sel(p[n])), _as2d(sel(mom[n])), _as2d(sel(var[n]))
            if not res:
                w2 = _tie(w2, tok)
            tr = rs if n == "ffn_down_w" else 256
            res[(n, layer)] = _adamw(parts, w2, m2, v2, name=f"adamw_{n}" + ("" if layer is None else str(layer)), tr=tr)
    srcs, lands = _push_wait(h_small, res[("ssm_in_w", None)][0], name="exchange_small_wait")
    small_parts = with_own(srcs, lands, True)[0]
    out_g, out_d, out_m, out_v = {}, {}, {}, {}
    for n in _BIG:
        if (n, None) in res:
            quad = res[(n, None)]
        else:
            quad = [jnp.stack([res[(n, 0)][k], res[(n, 1)][k]]) for k in range(4)]
        out_g[n], out_d[n], out_m[n], out_v[n] = (t.reshape(p[n].shape) for t in quad)

    zero = jnp.zeros_like(small_full)
    g_small_sum = _adamw(small_parts, zero, zero, zero, name="sum_small_grads", tr=small_full.shape[0])[0]
    g_small = dict(zip(small_names, _unpack_small(g_small_sum, [f[n].shape for n in small_names])))
    for n in _SMALL_SHARDED:
        width = p[n].shape[-1]
        g_small[n] = lax.dynamic_slice_in_dim(g_small[n], me * width, width, axis=g_small[n].ndim - 1)
    sw = _pack_small([p[n] for n in small_names])
    sm = _pack_small([mom[n] for n in small_names])
    sv = _pack_small([var[n] for n in small_names])
    sg = _pack_small([g_small[n] for n in small_names])
    _, d, nm, nv = _adamw(sg[None], sw, sm, sv, name="adamw_small", tr=sw.shape[0])
    shard_shapes = [p[n].shape for n in small_names]
    for n, dd, mm, vv in zip(small_names, _unpack_small(d, shard_shapes), _unpack_small(nm, shard_shapes),
                             _unpack_small(nv, shard_shapes)):
        out_g[n] = g_small[n].reshape(p[n].shape)
        out_d[n], out_m[n], out_v[n] = dd, mm, vv

    return (loss, dx.reshape(x.shape), *[out_g[n] for n in _WEIGHTS], *[out_d[n] for n in _WEIGHTS],
            *[out_m[n] for n in _WEIGHTS], *[out_v[n] for n in _WEIGHTS])
```

```python
import functools
import math

import jax
import jax.numpy as jnp
from jax import lax
from jax.experimental import pallas as pl
from jax.experimental.pallas import tpu as pltpu

F32 = jnp.float32
BF16 = jnp.bfloat16
EPS = 1e-6

D_MODEL = 1024
D_INNER = 2048
SSM_HEADS = 32
SSM_GROUPS = 4
SSM_STATE = 128
SSM_CONV = 4
SSM_CHUNK = 128
GN = SSM_GROUPS * SSM_STATE
CONV_DIM = D_INNER + 2 * GN
IN_PROJ_DIM = D_INNER + CONV_DIM + SSM_HEADS
IN_PROJ_PAD = 5376
SB_HEADS = 16
SB_HEAD_DIM = 64
SB_BLOCK = 128
D_FF = 2816
FFN_CONV = 3
N_DEV = 8

ADAM_LR = 0.001
ADAM_B1 = 0.9
ADAM_B2 = 0.999
ADAM_EPS = 1e-08
ADAM_WD = 0.01
ADAM_STEP = 10

_MESH = pl.DeviceIdType.MESH
_NT = (((1,), (1,)), ((), ()))
_TN = (((0,), (0,)), ((), ()))
_ANY = pl.BlockSpec(memory_space=pl.ANY)


def _cparams(sem, vmem_mb=48):
    return pltpu.CompilerParams(dimension_semantics=sem, vmem_limit_bytes=vmem_mb * 1024 * 1024)


def _sigmoid(x):
    return 1.0 / (1.0 + jnp.exp(-x))


def _softplus(x):
    return jnp.maximum(x, 0.0) + jnp.log(1.0 + jnp.exp(-jnp.abs(x)))


def _rms_fwd(xv, w):
    r = lax.rsqrt(jnp.mean(xv * xv, axis=-1, keepdims=True) + EPS)
    return xv * r * w


def _mm_fwd(x, w, *, name, norm_w=None, residual=None, out_dtype=F32, tm=512, tn=512, halves=False):
    M, K = x.shape
    N = w.shape[1]
    tm, tn = min(tm, M), min(tn, N)
    assert M % tm == 0 and N % tn == 0, (name, M, N, tm, tn)
    if halves:
        nbh = N // 2 // tn
        assert N // 2 % tn == 0
        out_spec = pl.BlockSpec((None, tm, tn), lambda i, j: (lax.div(j, nbh), i, lax.rem(j, nbh)))
        out_shape = jax.ShapeDtypeStruct((2, M, N // 2), out_dtype)
    else:
        out_spec = pl.BlockSpec((tm, tn), lambda i, j: (i, j))
        out_shape = jax.ShapeDtypeStruct((M, N), out_dtype)
    has_norm, has_res = norm_w is not None, residual is not None

    def body(*refs):
        x_ref, w_ref = refs[0], refs[1]
        p = 2
        nw_ref = r_ref = None
        if has_norm:
            nw_ref = refs[p]
            p += 1
        if has_res:
            r_ref = refs[p]
            p += 1
        o_ref, xn_ref = refs[p], refs[p + 1]

        @pl.when(pl.program_id(1) == 0)
        def _():
            xv = x_ref[...].astype(F32)
            if has_norm:
                xv = _rms_fwd(xv, nw_ref[...])
            xn_ref[...] = xv.astype(BF16)

        acc = jnp.dot(xn_ref[...], w_ref[...], preferred_element_type=F32)
        if has_res:
            acc = acc + r_ref[...]
        o_ref[...] = acc.astype(out_dtype)

    in_specs = [pl.BlockSpec((tm, K), lambda i, j: (i, 0)), pl.BlockSpec((K, tn), lambda i, j: (0, j))]
    args = [x, w]
    if has_norm:
        in_specs.append(pl.BlockSpec((1, K), lambda i, j: (0, 0)))
        args.append(norm_w.reshape(1, K))
    if has_res:
        in_specs.append(pl.BlockSpec((tm, tn), lambda i, j: (i, j)))
        args.append(residual)
    return pl.pallas_call(
        body, name=name, grid=(M // tm, N // tn), in_specs=in_specs,
        out_specs=out_spec, out_shape=out_shape,
        scratch_shapes=[pltpu.VMEM((tm, K), BF16)],
        compiler_params=_cparams(("parallel", "arbitrary")))(*args)


def _mm_nt(dy, w, *, name, epi=None, out_dtype=F32, tm=512, tn=512, tk=512):
    halves = dy.ndim == 3
    M, K = (dy.shape[1], 2 * dy.shape[2]) if halves else dy.shape
    N = w.shape[0]
    tm, tk = min(tm, M), min(tk, K)
    tn = N if epi is not None else min(tn, N)
    assert M % tm == 0 and N % tn == 0 and K % tk == 0, (name, M, N, K, tm, tn, tk)
    nk = K // tk
    has_epi = epi is not None

    def body(*refs):
        if has_epi:
            dy_ref, w_ref, h_ref, nw_ref, r_ref, o_ref, dnw_ref, acc_ref = refs
        else:
            dy_ref, w_ref, o_ref, acc_ref = refs
        i = pl.program_id(0)
        k = pl.program_id(2)

        @pl.when(k == 0)
        def _():
            acc_ref[...] = jnp.zeros_like(acc_ref)

        acc_ref[...] += lax.dot_general(dy_ref[...].astype(BF16), w_ref[...], _NT, preferred_element_type=F32)

        @pl.when(k == nk - 1)
        def _():
            du = acc_ref[...]
            if has_epi:
                hv = h_ref[...]
                r = lax.rsqrt(jnp.mean(hv * hv, axis=-1, keepdims=True) + EPS)
                xhat = hv * r
                dxh = du * nw_ref[...]
                dx = r * (dxh - xhat * jnp.mean(dxh * xhat, axis=-1, keepdims=True))
                o_ref[...] = (r_ref[...] + dx).astype(out_dtype)
                contrib = jnp.sum(du * xhat, axis=0, keepdims=True)

                @pl.when(i == 0)
                def _():
                    dnw_ref[...] = contrib

                @pl.when(i > 0)
                def _():
                    dnw_ref[...] += contrib
            else:
                o_ref[...] = du.astype(out_dtype)

    if halves:
        nkh = K // 2 // tk
        assert K // 2 % tk == 0
        dy_spec = pl.BlockSpec((None, tm, tk), lambda i, j, k: (lax.div(k, nkh), i, lax.rem(k, nkh)))
    else:
        dy_spec = pl.BlockSpec((tm, tk), lambda i, j, k: (i, k))
    in_specs = [dy_spec, pl.BlockSpec((tn, tk), lambda i, j, k: (j, k))]
    args = [dy, w]
    out_specs = [pl.BlockSpec((tm, tn), lambda i, j, k: (i, j))]
    out_shape = [jax.ShapeDtypeStruct((M, N), out_dtype)]
    if has_epi:
        h, nw, res = epi
        in_specs += [pl.BlockSpec((tm, N), lambda i, j, k: (i, 0)), pl.BlockSpec((1, N), lambda i, j, k: (0, 0)),
                     pl.BlockSpec((tm, N), lambda i, j, k: (i, 0))]
        args += [h, nw.reshape(1, N), res]
        out_specs.append(pl.BlockSpec((1, N), lambda i, j, k: (0, 0)))
        out_shape.append(jax.ShapeDtypeStruct((1, N), F32))
    outs = pl.pallas_call(
        body, name=name, grid=(M // tm, N // tn, nk), in_specs=in_specs, out_specs=out_specs, out_shape=out_shape,
        scratch_shapes=[pltpu.VMEM((tm, tn), F32)],
        compiler_params=_cparams(("arbitrary", "arbitrary", "arbitrary")))(*args)
    return (outs[0], outs[1]) if has_epi else outs[0]


def _mm_tn(x, dy, *, name, norm_w=None, out_dtype=BF16, tk1=1024, tn=512, tt=512):
    T, K1 = x.shape
    halves = dy.ndim == 3
    N = 2 * dy.shape[2] if halves else dy.shape[1]
    tk1, tn, tt = min(tk1, K1), min(tn, N), min(tt, T)
    has_norm = norm_w is not None
    assert K1 % tk1 == 0 and N % tn == 0 and T % tt == 0, (name, K1, N, T, tk1, tn, tt)
    assert not has_norm or tk1 == K1
    nt = T // tt

    def body(*refs):
        if has_norm:
            x_ref, dy_ref, nw_ref, o_ref, acc_ref = refs
        else:
            x_ref, dy_ref, o_ref, acc_ref = refs
        t = pl.program_id(2)

        @pl.when(t == 0)
        def _():
            acc_ref[...] = jnp.zeros_like(acc_ref)

        xv = x_ref[...]
        if has_norm:
            xv = _rms_fwd(xv.astype(F32), nw_ref[...])
        acc_ref[...] += lax.dot_general(xv.astype(BF16), dy_ref[...].astype(BF16), _TN, preferred_element_type=F32)

        @pl.when(t == nt - 1)
        def _():
            o_ref[...] = acc_ref[...].astype(out_dtype)

    if halves:
        nbh = N // 2 // tn
        assert N // 2 % tn == 0
        dy_spec = pl.BlockSpec((None, tt, tn), lambda a, b, t: (lax.div(b, nbh), t, lax.rem(b, nbh)))
    else:
        dy_spec = pl.BlockSpec((tt, tn), lambda a, b, t: (t, b))
    in_specs = [pl.BlockSpec((tt, tk1), lambda a, b, t: (t, a)), dy_spec]
    args = [x, dy]
    if has_norm:
        in_specs.append(pl.BlockSpec((1, K1), lambda a, b, t: (0, 0)))
        args.append(norm_w.reshape(1, K1))
    return pl.pallas_call(
        body, name=name, grid=(K1 // tk1, N // tn, nt), in_specs=in_specs,
        out_specs=pl.BlockSpec((tk1, tn), lambda a, b, t: (a, b)),
        out_shape=jax.ShapeDtypeStruct((K1, N), out_dtype),
        scratch_shapes=[pltpu.VMEM((tk1, tn), F32)],
        compiler_params=_cparams(("parallel", "parallel", "arbitrary")))(*args)


def _shift_down(xb, prev8, j):
    main = pltpu.roll(xb, j, 0)
    head = pltpu.roll(xb[0:8], j, 0)
    ph = pltpu.roll(prev8, j, 0)
    row8 = lax.broadcasted_iota(jnp.int32, head.shape, 0)
    head = jnp.where(row8 < j, ph, head)
    return jnp.concatenate([head, main[8:]], axis=0)


def _shift_up(xb, next8, j):
    tt = xb.shape[0]
    main = pltpu.roll(xb, tt - j, 0)
    tail = pltpu.roll(xb[tt - 8:tt], 8 - j, 0)
    nh = pltpu.roll(next8, 8 - j, 0)
    row8 = lax.broadcasted_iota(jnp.int32, tail.shape, 0)
    tail = jnp.where(row8 + j >= 8, nh, tail)
    return jnp.concatenate([main[:tt - 8], tail], axis=0)


def _conv_hid(xb, prev8, w, b_row, K):
    out = b_row
    shifted = []
    for j in range(K):
        sh = K - 1 - j
        xs = xb if sh == 0 else _shift_down(xb, prev8, sh)
        shifted.append(xs)
        out = out + xs * w[j:j + 1, :]
    return out, shifted


def _prev_idx(i, nb8):
    return jnp.maximum(i * nb8 - 1, 0)


def _ssm_conv_fwd(zx, w, b, *, name, tt=512, tc=512):
    T = zx.shape[0]
    tt = min(tt, T)
    C, K = CONV_DIM, SSM_CONV
    cb0, nb8 = D_INNER // tc, tt // 8

    def body(x_ref, p_ref, w_ref, b_ref, o_ref):
        first = (pl.program_id(1) > 0).astype(F32)
        hid, _ = _conv_hid(x_ref[...], p_ref[...] * first, w_ref[...], b_ref[...], K)
        o_ref[...] = hid * _sigmoid(hid)

    return pl.pallas_call(
        body, name=name, grid=(C // tc, T // tt),
        in_specs=[pl.BlockSpec((tt, tc), lambda c, i: (i, c + cb0)),
                  pl.BlockSpec((8, tc), lambda c, i: (_prev_idx(i, nb8), c + cb0)),
                  pl.BlockSpec((K, tc), lambda c, i: (0, c)), pl.BlockSpec((1, tc), lambda c, i: (0, c))],
        out_specs=pl.BlockSpec((tt, tc), lambda c, i: (i, c)),
        out_shape=jax.ShapeDtypeStruct((T, C), F32),
        compiler_params=_cparams(("parallel", "parallel")))(zx, zx, w, b)


def _ssm_conv_bwd_pre(zx, w, b, dout, *, name, tt=512, tc=512):
    T = zx.shape[0]
    tt = min(tt, T)
    C, K = CONV_DIM, SSM_CONV
    cb0, nb8 = D_INNER // tc, tt // 8

    def body(x_ref, p_ref, w_ref, b_ref, d_ref, dh_ref, dw_ref, db_ref):
        t = pl.program_id(1)
        first = (t > 0).astype(F32)
        hid, shifted = _conv_hid(x_ref[...], p_ref[...] * first, w_ref[...], b_ref[...], K)
        sg = _sigmoid(hid)
        dh = d_ref[...] * (sg * (1.0 + hid * (1.0 - sg)))
        dh_ref[...] = dh

        @pl.when(t == 0)
        def _():
            dw_ref[...] = jnp.zeros_like(dw_ref)
            db_ref[...] = jnp.zeros_like(db_ref)

        db_ref[...] += jnp.sum(dh, axis=0, keepdims=True)
        for j in range(K):
            dw_ref[j:j + 1, :] += jnp.sum(dh * shifted[j], axis=0, keepdims=True)

    return pl.pallas_call(
        body, name=name, grid=(C // tc, T // tt),
        in_specs=[pl.BlockSpec((tt, tc), lambda c, i: (i, c + cb0)),
                  pl.BlockSpec((8, tc), lambda c, i: (_prev_idx(i, nb8), c + cb0)),
                  pl.BlockSpec((K, tc), lambda c, i: (0, c)), pl.BlockSpec((1, tc), lambda c, i: (0, c)),
                  pl.BlockSpec((tt, tc), lambda c, i: (i, c))],
        out_specs=[pl.BlockSpec((tt, tc), lambda c, i: (i, c)), pl.BlockSpec((K, tc), lambda c, i: (0, c)),
                   pl.BlockSpec((1, tc), lambda c, i: (0, c))],
        out_shape=[jax.ShapeDtypeStruct((T, C), F32), jax.ShapeDtypeStruct((K, C), F32),
                   jax.ShapeDtypeStruct((1, C), F32)],
        compiler_params=_cparams(("parallel", "arbitrary")))(zx, zx, w, b, dout)


def _conv_bwd_in(dh, w, *, name, K, tt=512, tc=512, out_dtype=BF16):
    T, C = dh.shape
    tt = min(tt, T)
    nb8, nT = tt // 8, T // tt
    last8 = T // 8 - 1

    def body(d_ref, n_ref, w_ref, o_ref):
        notlast = (pl.program_id(1) < nT - 1).astype(F32)
        d = d_ref[...]
        nxt = n_ref[...] * notlast
        w_ = w_ref[...]
        acc = d * w_[K - 1:K, :]
        for sh in range(1, K):
            acc = acc + _shift_up(d, nxt, sh) * w_[K - 1 - sh:K - sh, :]
        o_ref[...] = acc.astype(out_dtype)

    return pl.pallas_call(
        body, name=name, grid=(C // tc, nT),
        in_specs=[pl.BlockSpec((tt, tc), lambda c, i: (i, c)),
                  pl.BlockSpec((8, tc), lambda c, i: (jnp.minimum((i + 1) * nb8, last8), c)),
                  pl.BlockSpec((K, tc), lambda c, i: (0, c))],
        out_specs=pl.BlockSpec((tt, tc), lambda c, i: (i, c)),
        out_shape=jax.ShapeDtypeStruct((T, C), out_dtype),
        compiler_params=_cparams(("parallel", "parallel")))(dh, dh, w)


def _ffn_conv_fwd(a, w, b, *, name, tt=256, tc=1408):
    T = a.shape[0]
    tt = min(tt, T)
    K, nbh, nb8 = FFN_CONV, D_FF // tc, tt // 8

    def body(ag_ref, pg_ref, av_ref, pv_ref, wg_ref, wv_ref, bg_ref, bv_ref, o_ref):
        first = (pl.program_id(1) > 0).astype(F32)
        hg, _ = _conv_hid(ag_ref[...], pg_ref[...] * first, wg_ref[...], bg_ref[...], K)
        hv, _ = _conv_hid(av_ref[...], pv_ref[...] * first, wv_ref[...], bv_ref[...], K)
        o_ref[...] = (hg * _sigmoid(hg) * hv).astype(BF16)

    return pl.pallas_call(
        body, name=name, grid=(nbh, T // tt),
        in_specs=[pl.BlockSpec((tt, tc), lambda c, i: (i, c)),
                  pl.BlockSpec((8, tc), lambda c, i: (_prev_idx(i, nb8), c)),
                  pl.BlockSpec((tt, tc), lambda c, i: (i, c + nbh)),
                  pl.BlockSpec((8, tc), lambda c, i: (_prev_idx(i, nb8), c + nbh)),
                  pl.BlockSpec((K, tc), lambda c, i: (0, c)), pl.BlockSpec((K, tc), lambda c, i: (0, c + nbh)),
                  pl.BlockSpec((1, tc), lambda c, i: (0, c)), pl.BlockSpec((1, tc), lambda c, i: (0, c + nbh))],
        out_specs=pl.BlockSpec((tt, tc), lambda c, i: (i, c)),
        out_shape=jax.ShapeDtypeStruct((T, D_FF), BF16),
        compiler_params=_cparams(("parallel", "parallel")))(a, a, a, a, w, w, b, b)


def _ffn_conv_bwd_pre(a, w, b, dp, *, name, tt=256, tc=1408):
    T = a.shape[0]
    tt = min(tt, T)
    K, nbh, nb8 = FFN_CONV, D_FF // tc, tt // 8

    def body(ao_ref, po_ref, ag_ref, pg_ref, av_ref, pv_ref, wg_ref, wv_ref, bg_ref, bv_ref, dp_ref,
             dh_ref, dw_ref, db_ref):
        j = pl.program_id(0)
        t = pl.program_id(1)
        first = (t > 0).astype(F32)
        hg, _ = _conv_hid(ag_ref[...], pg_ref[...] * first, wg_ref[...], bg_ref[...], K)
        hv, _ = _conv_hid(av_ref[...], pv_ref[...] * first, wv_ref[...], bv_ref[...], K)
        sg = _sigmoid(hg)
        d = dp_ref[...].astype(F32)
        is_gate = (j < nbh).astype(F32)
        dh = d * (is_gate * (hv * (sg * (1.0 + hg * (1.0 - sg)))) + (1.0 - is_gate) * (hg * sg))
        dh_ref[...] = dh
        xo = ao_ref[...]
        po = po_ref[...] * first

        @pl.when(t == 0)
        def _():
            dw_ref[...] = jnp.zeros_like(dw_ref)
            db_ref[...] = jnp.zeros_like(db_ref)

        db_ref[...] += jnp.sum(dh, axis=0, keepdims=True)
        for jj in range(K):
            sh = K - 1 - jj
            xs = xo if sh == 0 else _shift_down(xo, po, sh)
            dw_ref[jj:jj + 1, :] += jnp.sum(dh * xs, axis=0, keepdims=True)

    def gi(c):
        return lax.rem(c, nbh)

    return pl.pallas_call(
        body, name=name, grid=(2 * nbh, T // tt),
        in_specs=[pl.BlockSpec((tt, tc), lambda c, i: (i, c)),
                  pl.BlockSpec((8, tc), lambda c, i: (_prev_idx(i, nb8), c)),
                  pl.BlockSpec((tt, tc), lambda c, i: (i, gi(c))),
                  pl.BlockSpec((8, tc), lambda c, i: (_prev_idx(i, nb8), gi(c))),
                  pl.BlockSpec((tt, tc), lambda c, i: (i, gi(c) + nbh)),
                  pl.BlockSpec((8, tc), lambda c, i: (_prev_idx(i, nb8), gi(c) + nbh)),
                  pl.BlockSpec((K, tc), lambda c, i: (0, gi(c))), pl.BlockSpec((K, tc), lambda c, i: (0, gi(c) + nbh)),
                  pl.BlockSpec((1, tc), lambda c, i: (0, gi(c))), pl.BlockSpec((1, tc), lambda c, i: (0, gi(c) + nbh)),
                  pl.BlockSpec((tt, tc), lambda c, i: (i, gi(c)))],
        out_specs=[pl.BlockSpec((tt, tc), lambda c, i: (i, c)), pl.BlockSpec((K, tc), lambda c, i: (0, c)),
                   pl.BlockSpec((1, tc), lambda c, i: (0, c))],
        out_shape=[jax.ShapeDtypeStruct((T, 2 * D_FF), F32), jax.ShapeDtypeStruct((K, 2 * D_FF), F32),
                   jax.ShapeDtypeStruct((1, 2 * D_FF), F32)],
        compiler_params=_cparams(("parallel", "arbitrary")))(a, a, a, a, a, a, w, w, b, b, dp)


def _ffn_conv_fwd3(a3, w, b, *, name, tt=256, tc=1408):
    T = a3.shape[1]
    tt = min(tt, T)
    K, nbh, n16 = FFN_CONV, D_FF // tc, tt // 16

    def body(a_ref, p_ref, wg_ref, wv_ref, bg_ref, bv_ref, o_ref):
        first = (pl.program_id(1) > 0).astype(F32)
        a = a_ref[...].astype(F32)
        prev = p_ref[...].astype(F32)[:, 8:16, :] * first
        hg, _ = _conv_hid(a[0], prev[0], wg_ref[...], bg_ref[...], K)
        hv, _ = _conv_hid(a[1], prev[1], wv_ref[...], bv_ref[...], K)
        o_ref[...] = (hg * _sigmoid(hg) * hv).astype(BF16)

    return pl.pallas_call(
        body, name=name, grid=(nbh, T // tt),
        in_specs=[pl.BlockSpec((2, tt, tc), lambda c, i: (0, i, c)),
                  pl.BlockSpec((2, 16, tc), lambda c, i: (0, _prev_idx(i, n16), c)),
                  pl.BlockSpec((K, tc), lambda c, i: (0, c)), pl.BlockSpec((K, tc), lambda c, i: (0, c + nbh)),
                  pl.BlockSpec((1, tc), lambda c, i: (0, c)), pl.BlockSpec((1, tc), lambda c, i: (0, c + nbh))],
        out_specs=pl.BlockSpec((tt, tc), lambda c, i: (i, c)),
        out_shape=jax.ShapeDtypeStruct((T, D_FF), BF16),
        compiler_params=_cparams(("parallel", "parallel")))(a3, a3, w, w, b, b)


def _ffn_conv_bwd3(a3, w, b, dp, *, name, tt=256, tc=1408):
    T = a3.shape[1]
    tt = min(tt, T)
    K, nbh, n16 = FFN_CONV, D_FF // tc, tt // 16

    def body(a_ref, p_ref, wg_ref, wv_ref, bg_ref, bv_ref, dp_ref, dh_ref, dw_ref, db_ref):
        t = pl.program_id(1)
        first = (t > 0).astype(F32)
        a = a_ref[...].astype(F32)
        prev = p_ref[...].astype(F32)[:, 8:16, :] * first
        hg, sh_g = _conv_hid(a[0], prev[0], wg_ref[...], bg_ref[...], K)
        hv, sh_v = _conv_hid(a[1], prev[1], wv_ref[...], bv_ref[...], K)
        sg = _sigmoid(hg)
        d = dp_ref[...].astype(F32)
        dhg = d * hv * (sg * (1.0 + hg * (1.0 - sg)))
        dhv = d * (hg * sg)
        dh_ref[0] = dhg.astype(BF16)
        dh_ref[1] = dhv.astype(BF16)

        @pl.when(t == 0)
        def _():
            dw_ref[...] = jnp.zeros_like(dw_ref)
            db_ref[...] = jnp.zeros_like(db_ref)

        db_ref[0] += jnp.sum(dhg, axis=0, keepdims=True)
        db_ref[1] += jnp.sum(dhv, axis=0, keepdims=True)
        for j in range(K):
            dw_ref[0, j:j + 1, :] += jnp.sum(dhg * sh_g[j], axis=0, keepdims=True)
            dw_ref[1, j:j + 1, :] += jnp.sum(dhv * sh_v[j], axis=0, keepdims=True)

    return pl.pallas_call(
        body, name=name, grid=(nbh, T // tt),
        in_specs=[pl.BlockSpec((2, tt, tc), lambda c, i: (0, i, c)),
                  pl.BlockSpec((2, 16, tc), lambda c, i: (0, _prev_idx(i, n16), c)),
                  pl.BlockSpec((K, tc), lambda c, i: (0, c)), pl.BlockSpec((K, tc), lambda c, i: (0, c + nbh)),
                  pl.BlockSpec((1, tc), lambda c, i: (0, c)), pl.BlockSpec((1, tc), lambda c, i: (0, c + nbh)),
                  pl.BlockSpec((tt, tc), lambda c, i: (i, c))],
        out_specs=[pl.BlockSpec((2, tt, tc), lambda c, i: (0, i, c)), pl.BlockSpec((2, K, tc), lambda c, i: (0, 0, c)),
                   pl.BlockSpec((2, 1, tc), lambda c, i: (0, 0, c))],
        out_shape=[jax.ShapeDtypeStruct((2, T, D_FF), BF16), jax.ShapeDtypeStruct((2, K, D_FF), F32),
                   jax.ShapeDtypeStruct((2, 1, D_FF), F32)],
        compiler_params=_cparams(("parallel", "arbitrary")))(a3, a3, w, w, b, b, dp)


def _conv_bwd_in3(dh3, w, *, name, K, tt=256, tc=1408):
    H, T, C = dh3.shape
    tt = min(tt, T)
    nb, n16, nT = C // tc, tt // 16, T // tt
    last16 = T // 16 - 1

    def body(d_ref, n_ref, w_ref, o_ref):
        notlast = (pl.program_id(2) < nT - 1).astype(F32)
        d = d_ref[...].astype(F32)
        nxt = n_ref[...].astype(F32)[0:8, :] * notlast
        w_ = w_ref[...]
        acc = d * w_[K - 1:K, :]
        for sh in range(1, K):
            acc = acc + _shift_up(d, nxt, sh) * w_[K - 1 - sh:K - sh, :]
        o_ref[...] = acc.astype(BF16)

    return pl.pallas_call(
        body, name=name, grid=(H, nb, nT),
        in_specs=[pl.BlockSpec((None, tt, tc), lambda h, c, i: (h, i, c)),
                  pl.BlockSpec((None, 16, tc), lambda h, c, i: (h, jnp.minimum((i + 1) * n16, last16), c)),
                  pl.BlockSpec((K, tc), lambda h, c, i: (0, h * nb + c))],
        out_specs=pl.BlockSpec((None, tt, tc), lambda h, c, i: (h, i, c)),
        out_shape=jax.ShapeDtypeStruct((H, T, C), BF16),
        compiler_params=_cparams(("parallel", "parallel", "parallel")))(dh3, dh3, w)


def _cumsum_rows(x):
    L = x.shape[0]
    row = lax.broadcasted_iota(jnp.int32, x.shape, 0)
    k = 1
    while k < L:
        x = x + jnp.where(row >= k, pltpu.roll(x, k, 0), 0.0)
        k *= 2
    return x


def _rcumsum_rows(x):
    L = x.shape[0]
    row = lax.broadcasted_iota(jnp.int32, x.shape, 0)
    k = 1
    while k < L:
        x = x + jnp.where(row < L - k, pltpu.roll(x, L - k, 0), 0.0)
        k *= 2
    return x


def _split_terms(m, n):
    terms, rest = [], m
    for _ in range(n):
        t = rest.astype(BF16)
        terms.append(t)
        rest = rest - t.astype(F32)
    return jnp.concatenate(terms, axis=1)


def _select_dot(m, n_terms, n_out, cond):
    K = m.shape[1]
    k = lax.broadcasted_iota(jnp.int32, (K, n_out), 0)
    j = lax.broadcasted_iota(jnp.int32, (K, n_out), 1)
    sel = cond(k, j).astype(BF16)
    return jnp.dot(_split_terms(m, n_terms), jnp.concatenate([sel] * n_terms, axis=0), preferred_element_type=F32)


def _rowsum_mxu(m):
    return _select_dot(m, 2, 128, lambda k, j: k >= 0)


def _lane_block_sums(m, width):
    shift = width.bit_length() - 1
    return _select_dot(m, 2, 128, lambda k, j: j == jnp.right_shift(k, shift))


def _heads_to_pairs(m):
    return _select_dot(m, 3, 512, lambda k, j: k == jnp.right_shift(j, 6))


def _ssd_common(dt_ref, par_ref):
    par = par_ref[...]
    raw = dt_ref[...] + par[0:1, :]
    dt = _softplus(raw)
    a = -jnp.exp(par[1:2, :])
    cs = _cumsum_rows(dt * a)
    L = cs.shape[0]
    cs_last = cs[L - 1:L, :]
    return raw, dt, a, par[2:3, :], cs, cs.T, jnp.exp(cs), jnp.exp(cs_last - cs), jnp.exp(cs_last)


def _ssd_specs(nc, rev):
    L = SSM_CHUNK

    def ci(c):
        return nc - 1 - c if rev else c

    return [pl.BlockSpec((L, D_INNER), lambda c: (ci(c), 0)),
            pl.BlockSpec((L, GN), lambda c: (ci(c), D_INNER // GN)),
            pl.BlockSpec((L, GN), lambda c: (ci(c), D_INNER // GN + 1)),
            pl.BlockSpec((SSM_GROUPS, L, 128), lambda c: (0, ci(c), 0)),
            pl.BlockSpec((SSM_GROUPS, 8, 128), lambda c: (0, 0, 0)),
            pl.BlockSpec((L, D_INNER), lambda c: (ci(c), 0)),
            pl.BlockSpec((1, D_INNER), lambda c: (0, 0))], ci


def _round_robin(gens):
    live = list(gens)
    while live:
        nxt = []
        for gen in live:
            try:
                next(gen)
                nxt.append(gen)
            except StopIteration:
                pass
        live = nxt


def _group_views(g, wide, narrow, lead):
    return ([r.at[:, g * 512:(g + 1) * 512] for r in wide], [r.at[:, g * 128:(g + 1) * 128] for r in narrow],
            [r.at[g] for r in lead])


def _ssd_fwd(xbc_c, zx, dtg, par, gnw, *, name):
    T = xbc_c.shape[0]
    L = SSM_CHUNK
    nc = T // L
    in_specs, ci = _ssd_specs(nc, False)

    def body(xs_ref, b_ref, c_ref, dt_ref, par_ref, z_ref, gnw_ref, y_ref, yn_ref, st_ref, h_ref):
        @pl.when(pl.program_id(0) == 0)
        def _():
            h_ref[...] = jnp.zeros_like(h_ref)

        gens = []
        for g in range(SSM_GROUPS):
            (xs, z, gw, y, yn), (b, c), (dt, pr, st, h) = _group_views(
                g, [xs_ref, z_ref, gnw_ref, y_ref, yn_ref], [b_ref, c_ref], [dt_ref, par_ref, st_ref, h_ref])
            gens.append(group(xs, b, c, dt, pr, z, gw, y, yn, st, h))
        _round_robin(gens)

    def group(xs_ref, b_ref, c_ref, dt_ref, par_ref, z_ref, gnw_ref, y_ref, yn_ref, st_ref, h_ref):
        _, dt, _, dsk, cs, csT, ecs, eend, dec = _ssd_common(dt_ref, par_ref)
        Bb = b_ref[...].astype(BF16)
        Cb = c_ref[...].astype(BF16)
        G = lax.dot_general(Cb, Bb, _NT, preferred_element_type=F32)
        row = lax.broadcasted_iota(jnp.int32, (L, L), 0)
        col = lax.broadcasted_iota(jnp.int32, (L, L), 1)
        tril = col <= row
        lo = lax.broadcasted_iota(jnp.int32, (L, 128), 1) < 64
        lo1 = lax.broadcasted_iota(jnp.int32, (1, 128), 1) < 64
        dt_x, ecs_x, eend_x = (_heads_to_pairs(m) for m in (dt, ecs, eend))
        for pp in range(4):
            hA, hB = 2 * pp, 2 * pp + 1
            lanes = slice(pp * 128, (pp + 1) * 128)

            def sel1(m):
                return jnp.where(lo1, m[:, hA:hA + 1], m[:, hB:hB + 1])

            X = xs_ref[:, lanes]
            xd = X * dt_x[:, lanes]
            xdb = xd.astype(BF16)
            ys = []
            for h in (hA, hB):
                Lm = jnp.where(tril, jnp.exp(jnp.minimum(cs[:, h:h + 1] - csT[h:h + 1, :], 0.0)), 0.0)
                ys.append(jnp.dot((G * Lm).astype(BF16), xdb, preferred_element_type=F32))
                yield
            Hp = h_ref[pp]
            st_ref[pp] = Hp
            yoff = jnp.dot(Cb, Hp.astype(BF16), preferred_element_type=F32) * ecs_x[:, lanes]
            y_ref[:, lanes] = jnp.where(lo, ys[0], ys[1]) + yoff + sel1(dsk) * X
            S = lax.dot_general(Bb, (xd * eend_x[:, lanes]).astype(BF16), _TN, preferred_element_type=F32)
            h_ref[pp] = Hp * sel1(dec) + S
            yield
        zv = z_ref[...]
        yg = y_ref[...] * (zv * _sigmoid(zv))
        r = jnp.tile(lax.rsqrt(_rowsum_mxu(yg * yg) * (1.0 / 512) + EPS), (1, 4))
        yn_ref[...] = (yg * r * gnw_ref[...]).astype(BF16)

    return pl.pallas_call(
        body, name=name, grid=(nc,), in_specs=in_specs,
        out_specs=[pl.BlockSpec((L, D_INNER), lambda c: (c, 0)), pl.BlockSpec((L, D_INNER), lambda c: (c, 0)),
                   pl.BlockSpec((SSM_GROUPS, None, 4, 128, 128), lambda c: (0, c, 0, 0, 0))],
        out_shape=[jax.ShapeDtypeStruct((T, D_INNER), F32), jax.ShapeDtypeStruct((T, D_INNER), BF16),
                   jax.ShapeDtypeStruct((SSM_GROUPS, nc, 4, 128, 128), F32)],
        scratch_shapes=[pltpu.VMEM((SSM_GROUPS, 4, 128, 128), F32)],
        compiler_params=_cparams(("arbitrary",)))(xbc_c, xbc_c, xbc_c, dtg, par, zx, gnw)


def _ssd_bwd(xbc_c, zx, dtg, par, gnw, y, st, dyn, *, name):
    T = xbc_c.shape[0]
    L = SSM_CHUNK
    nc = T // L
    in_specs, ci = _ssd_specs(nc, True)
    in_specs += [pl.BlockSpec((L, D_INNER), lambda c: (ci(c), 0)),
                 pl.BlockSpec((SSM_GROUPS, None, 4, 128, 128), lambda c: (0, ci(c), 0, 0, 0)),
                 pl.BlockSpec((L, D_INNER), lambda c: (ci(c), 0))]

    def body(xs_ref, b_ref, c_ref, dt_ref, par_ref, z_ref, gnw_ref, y_ref, st_ref, dyn_ref,
             dxbc_ref, dz_ref, ddt_ref, dgnw_ref, dpar_ref, dh_ref):
        @pl.when(pl.program_id(0) == 0)
        def _():
            dh_ref[...] = jnp.zeros_like(dh_ref)
            dgnw_ref[...] = jnp.zeros_like(dgnw_ref)
            dpar_ref[...] = jnp.zeros_like(dpar_ref)

        dxs_ref = dxbc_ref.at[:, 0:D_INNER]
        db_ref = dxbc_ref.at[:, D_INNER:D_INNER + GN]
        dc_ref = dxbc_ref.at[:, D_INNER + GN:CONV_DIM]

        gens = []
        for g in range(SSM_GROUPS):
            (xs, z, gw, y, dyn, dxs, dz, dgw), (b, c, db, dc), (dt, pr, st, ddt, dpr, dh) = _group_views(
                g, [xs_ref, z_ref, gnw_ref, y_ref, dyn_ref, dxs_ref, dz_ref, dgnw_ref], [b_ref, c_ref, db_ref, dc_ref],
                [dt_ref, par_ref, st_ref, ddt_ref, dpar_ref, dh_ref])
            gens.append(group(xs, b, c, dt, pr, z, gw, y, st, dyn, dxs, db, dc, dz, ddt, dgw, dpr, dh))
        _round_robin(gens)

    def group(xs_ref, b_ref, c_ref, dt_ref, par_ref, z_ref, gnw_ref, y_ref, st_ref, dyn_ref,
              dxs_ref, db_ref, dc_ref, dz_ref, ddt_ref, dgnw_ref, dpar_ref, dh_ref):
        yv = y_ref[...]
        zv = z_ref[...]
        sg = _sigmoid(zv)
        sz = zv * sg
        yg = yv * sz
        r = jnp.tile(lax.rsqrt(_rowsum_mxu(yg * yg) * (1.0 / 512) + EPS), (1, 4))
        yh = yg * r
        dyn = dyn_ref[...].astype(F32)
        dgnw_ref[...] += jnp.sum(dyn * yh, axis=0, keepdims=True)
        dyh = dyn * gnw_ref[...]
        dyg = r * (dyh - yh * jnp.tile(_rowsum_mxu(dyh * yh) * (1.0 / 512), (1, 4)))
        dY_all = dyg * sz
        dz_ref[...] = (dyg * yv * (sg * (1.0 + zv * (1.0 - sg)))).astype(dz_ref.dtype)

        yield
        raw, dt, a, dsk, cs, csT, ecs, eend, dec = _ssd_common(dt_ref, par_ref)
        Bb = b_ref[...].astype(BF16)
        Cb = c_ref[...].astype(BF16)
        G = lax.dot_general(Cb, Bb, _NT, preferred_element_type=F32)
        row = lax.broadcasted_iota(jnp.int32, (L, L), 0)
        col = lax.broadcasted_iota(jnp.int32, (L, L), 1)
        tril = col <= row
        lo = lax.broadcasted_iota(jnp.int32, (L, 128), 1) < 64
        lane1 = lax.broadcasted_iota(jnp.int32, (1, 128), 1)
        lo1 = lane1 < 64
        rowl = lax.broadcasted_iota(jnp.int32, (L, 128), 0)
        dt_x, ecs_x, eend_x = (_heads_to_pairs(m) for m in (dt, ecs, eend))
        dG = jnp.zeros((L, L), F32)
        dB = jnp.zeros((L, SSM_STATE), F32)
        dC = jnp.zeros((L, SSM_STATE), F32)
        dcs_t = jnp.zeros((L, L), F32)
        tails = jnp.zeros((1, 128), F32)
        dD_row = jnp.zeros((1, 128), F32)
        v_parts, prod_parts = [], []

        def tot(m):
            return jnp.sum(jnp.sum(m, axis=0, keepdims=True), axis=1, keepdims=True)

        for pp in range(4):
            hA, hB = 2 * pp, 2 * pp + 1
            lanes = slice(pp * 128, (pp + 1) * 128)

            def sel1(m):
                return jnp.where(lo1, m[:, hA:hA + 1], m[:, hB:hB + 1])

            X = xs_ref[:, lanes]
            dY = dY_all[:, lanes]
            dtsel = dt_x[:, lanes]
            xd = X * dtsel
            xdb = xd.astype(BF16)
            dYb = dY.astype(BF16)
            Hp = st_ref[pp]
            Hb = Hp.astype(BF16)
            dHn = dh_ref[pp]
            dHb = dHn.astype(BF16)
            ecs_sel = ecs_x[:, lanes]
            eend_sel = eend_x[:, lanes]
            dxd_state = jnp.dot(Bb, dHb, preferred_element_type=F32) * eend_sel
            yoff = jnp.dot(Cb, Hb, preferred_element_type=F32) * ecs_sel
            dYe = (dY * ecs_sel).astype(BF16)
            dC = dC + lax.dot_general(dYe, Hb, _NT, preferred_element_type=F32)
            dB = dB + lax.dot_general((xd * eend_sel).astype(BF16), dHb, _NT, preferred_element_type=F32)
            dh_ref[pp] = dHn * sel1(dec) + lax.dot_general(Cb, dYe, _TN, preferred_element_type=F32)
            q = xd * dxd_state
            dyq = dY * yoff - q
            qcol = jnp.sum(q, axis=0, keepdims=True)
            hcol = jnp.sum(dHn * Hp, axis=0, keepdims=True)
            dxd_diag = []
            for h, msk, msk1 in ((hA, lo, lo1), (hB, jnp.logical_not(lo), jnp.logical_not(lo1))):
                Lm = jnp.where(tril, jnp.exp(jnp.minimum(cs[:, h:h + 1] - csT[h:h + 1, :], 0.0)), 0.0)
                M = G * Lm
                dxd_diag.append(lax.dot_general(M.astype(BF16), dYb, _TN, preferred_element_type=F32))
                dM = lax.dot_general(jnp.where(msk, dY, 0.0).astype(BF16), xdb, _NT, preferred_element_type=F32)
                dG = dG + dM * Lm
                W = dM * M
                dcs_t = dcs_t + jnp.where(row == h, jnp.sum(W, axis=0, keepdims=True), 0.0)
                v_parts.append(W + jnp.where(msk, dyq, 0.0))
                tail = (jnp.sum(jnp.where(msk1, qcol, 0.0), axis=1, keepdims=True)
                        + dec[:, h:h + 1] * jnp.sum(jnp.where(msk1, hcol, 0.0), axis=1, keepdims=True))
                tails = tails + jnp.where(lane1 == h, tail, 0.0)
                yield
            dxd = jnp.where(lo, dxd_diag[0], dxd_diag[1]) + dxd_state
            prod_parts.append(dxd * X)
            dxs_ref[:, lanes] = dxd * dtsel + sel1(dsk) * dY
            dyx = jnp.sum(dY * X, axis=0, keepdims=True)
            sA = jnp.sum(jnp.where(lo1, dyx, 0.0), axis=1, keepdims=True)
            sB = jnp.sum(dyx, axis=1, keepdims=True) - sA
            dD_row = dD_row + jnp.where(lane1 == hA, sA, 0.0) + jnp.where(lane1 == hB, sB, 0.0)
            yield
        dGb = dG.astype(BF16)
        db_ref[...] = dB + lax.dot_general(dGb, Cb, _TN, preferred_element_type=F32)
        dc_ref[...] = dC + jnp.dot(dGb, Bb, preferred_element_type=F32)
        dcs_mat = _lane_block_sums(jnp.concatenate(v_parts, axis=1), 128) + jnp.where(rowl == L - 1, tails, 0.0)
        ddt_mat = _lane_block_sums(jnp.concatenate(prod_parts, axis=1), 64)
        dad = _rcumsum_rows(dcs_mat - dcs_t.T)
        draw = (a * dad + ddt_mat) * _sigmoid(raw)
        ddt_ref[...] = draw
        dpar_ref[0:1, :] += jnp.sum(draw, axis=0, keepdims=True)
        dpar_ref[1:2, :] += jnp.sum(dt * dad, axis=0, keepdims=True) * a
        dpar_ref[2:3, :] += dD_row

    return pl.pallas_call(
        body, name=name, grid=(nc,), in_specs=in_specs,
        out_specs=[pl.BlockSpec((L, CONV_DIM), lambda c: (ci(c), 0)),
                   pl.BlockSpec((L, D_INNER), lambda c: (ci(c), 0)),
                   pl.BlockSpec((SSM_GROUPS, L, 128), lambda c: (0, ci(c), 0)),
                   pl.BlockSpec((1, D_INNER), lambda c: (0, 0)),
                   pl.BlockSpec((SSM_GROUPS, 8, 128), lambda c: (0, 0, 0))],
        out_shape=[jax.ShapeDtypeStruct((T, CONV_DIM), F32), jax.ShapeDtypeStruct((T, D_INNER), BF16),
                   jax.ShapeDtypeStruct((SSM_GROUPS, T, 128), F32), jax.ShapeDtypeStruct((1, D_INNER), F32),
                   jax.ShapeDtypeStruct((SSM_GROUPS, 8, 128), F32)],
        scratch_shapes=[pltpu.VMEM((SSM_GROUPS, 4, 128, 128), F32)],
        compiler_params=_cparams(("arbitrary",)))(xbc_c, xbc_c, xbc_c, dtg, par, zx, gnw, y, st, dyn)


SB_KEYS = 512
SB_SCAN = 256
SB_STRIP = 256


def _tri(width, cond):
    kk = lax.broadcasted_iota(jnp.int32, (width, width), 0)
    jj = lax.broadcasted_iota(jnp.int32, (width, width), 1)
    return cond(kk, jj).astype(BF16)


def _sba_diag_mask():
    Bq = SB_BLOCK
    rowi = lax.broadcasted_iota(jnp.int32, (2 * Bq, Bq), 0)
    return lax.broadcasted_iota(jnp.int32, (2 * Bq, Bq), 1) < jnp.where(rowi >= Bq, rowi - Bq, rowi)


_LOG2E = 1.4426950408889634


def _softplus2(z2):
    return jnp.maximum(z2, 0.0) + jnp.log2(1.0 + jnp.exp2(-jnp.abs(z2)))


def _sba_sub_fwd(zb, c, U, mask):
    z2 = zb * _LOG2E
    s = _softplus2(z2)
    if mask is not None:
        s = jnp.where(mask, s, 0.0)
    R = c + jnp.dot(s.astype(BF16), U, preferred_element_type=F32)
    A = jnp.exp2(z2 - s - R)
    if mask is not None:
        A = jnp.where(mask, A, 0.0)
    return A.astype(BF16), R[:, 0:1] + s[:, 0:1]


def _sba_sub_bwd(zb, dAb, Lt, pc, pe, Uincl, Uexcl, mask):
    last = zb.shape[1] - 1
    z2 = zb * _LOG2E
    s = _softplus2(z2)
    g = z2 - s
    if mask is not None:
        s = jnp.where(mask, s, 0.0)
    P = pc + jnp.dot(s.astype(BF16), Uincl, preferred_element_type=F32)
    A = jnp.exp2(g - (Lt - P))
    if mask is not None:
        A = jnp.where(mask, A, 0.0)
    E = dAb * A
    PE = pe + jnp.dot(E.astype(BF16), Uexcl, preferred_element_type=F32)
    dz = E - jnp.exp2(g) * (E + PE)
    if mask is not None:
        dz = jnp.where(mask, dz, 0.0)
    return (A.astype(BF16), dz.astype(BF16), P[:, last:last + 1], PE[:, last:last + 1] + E[:, last:last + 1])


def _stack_heads(v):
    lo = lax.broadcasted_iota(jnp.int32, v.shape, 1) < 64
    zero = jnp.zeros_like(v)
    return jnp.concatenate([jnp.where(lo, v, zero), jnp.where(lo, zero, v)], axis=0)


def _unstack_heads(v):
    lo = lax.broadcasted_iota(jnp.int32, (SB_BLOCK, 128), 1) < 64
    return jnp.where(lo, v[:SB_BLOCK], v[SB_BLOCK:])


def _sba_rows(a):
    return slice(2 * a * SB_BLOCK, 2 * (a + 1) * SB_BLOCK)


def _sba_diag_case(a, b):
    Bq = SB_BLOCK
    if b * SB_SCAN >= (a + 1) * Bq:
        return "skip"
    if (b + 1) * SB_SCAN <= a * Bq:
        return "full"
    rowi = lax.broadcasted_iota(jnp.int32, (2 * Bq, SB_SCAN), 0)
    qpos = a * Bq + jnp.where(rowi >= Bq, rowi - Bq, rowi)
    return b * SB_SCAN + lax.broadcasted_iota(jnp.int32, (2 * Bq, SB_SCAN), 1) < qpos


def _sba_fwd(q, kv, *, name):
    T = q.shape[0]
    Bq = SB_BLOCK
    nsub = SB_KEYS // Bq
    nscan = SB_KEYS // SB_SCAN
    R = 2 * SB_KEYS
    assert T % SB_KEYS == 0 and SB_STRIP == 2 * Bq
    scale = 1.0 / math.sqrt(SB_HEAD_DIM)

    def body(q_ref, k_ref, v_ref, o_ref, lt_ref, z_s, a_s, c_s, acc_s):
        i = pl.program_id(1)
        U2 = _tri(SB_SCAN, lambda k, j: k > j)
        qs_all = jnp.concatenate([_stack_heads(q_ref[a * Bq:(a + 1) * Bq, :] * scale) for a in range(nsub)], axis=0)
        c_s[...] = jnp.zeros_like(c_s)
        acc_s[...] = jnp.zeros_like(acc_s)

        def scores(J, slot):
            off = pl.multiple_of(J * SB_KEYS, SB_KEYS)
            z_s[slot] = lax.dot_general(qs_all, k_ref[pl.ds(off, SB_KEYS), :], _NT, preferred_element_type=F32)

        def weights(slot, diag):
            for a in range(nsub):
                rows = _sba_rows(a)
                c = c_s[rows, :]
                for b in reversed(range(nscan)):
                    cols = slice(b * SB_SCAN, (b + 1) * SB_SCAN)
                    case = _sba_diag_case(a, b) if diag else "full"
                    if isinstance(case, str) and case == "skip":
                        a_s[slot, rows, cols] = jnp.zeros((2 * Bq, SB_SCAN), BF16)
                        continue
                    A, c = _sba_sub_fwd(z_s[slot, rows, cols], c, U2, None if isinstance(case, str) else case)
                    a_s[slot, rows, cols] = A
                c_s[rows, :] = c

        def values(J, slot):
            off = pl.multiple_of(J * SB_KEYS, SB_KEYS)
            acc_s[...] += jnp.dot(a_s[slot], v_ref[pl.ds(off, SB_KEYS), :], preferred_element_type=F32)

        scores(i, 0)
        weights(0, True)
        scores(jnp.maximum(i - 1, 0), 1)

        def two_steps(u, _):
            t = 2 * u + 1
            weights(1, False)
            scores(jnp.maximum(i - t - 1, 0), 0)
            values(i - t + 1, 0)
            weights(0, False)
            scores(jnp.maximum(i - t - 2, 0), 1)
            values(i - t, 1)
            return 0

        lax.fori_loop(0, i // 2, two_steps, 0)
        odd = lax.rem(i, 2) == 1

        @pl.when(jnp.logical_not(odd))
        def _():
            values(0, 0)

        @pl.when(odd)
        def _():
            weights(1, False)
            values(1, 0)
            values(0, 1)
        for a in range(nsub):
            o_ref[a * Bq:(a + 1) * Bq, :] = _unstack_heads(acc_s[_sba_rows(a), :]).astype(BF16)
            lt_ref[a * Bq:(a + 1) * Bq, :] = _unstack_heads(jnp.broadcast_to(c_s[_sba_rows(a), :], (2 * Bq, 128)))

    return pl.pallas_call(
        body, name=name, grid=(SB_HEADS // 2, T // SB_KEYS),
        in_specs=[pl.BlockSpec((SB_KEYS, 128), lambda p, i: (i, p)), pl.BlockSpec((T, 128), lambda p, i: (0, p)),
                  pl.BlockSpec((T, 128), lambda p, i: (0, p + SB_HEADS // 2))],
        out_specs=[pl.BlockSpec((SB_KEYS, 128), lambda p, i: (i, p)),
                   pl.BlockSpec((None, SB_KEYS, 128), lambda p, i: (p, i, 0))],
        out_shape=[jax.ShapeDtypeStruct((T, D_MODEL), BF16), jax.ShapeDtypeStruct((SB_HEADS // 2, T, 128), F32)],
        scratch_shapes=[pltpu.VMEM((2, R, SB_KEYS), F32), pltpu.VMEM((2, R, SB_KEYS), BF16),
                        pltpu.VMEM((R, 1), F32), pltpu.VMEM((R, 128), F32)],
        compiler_params=_cparams(("parallel", "parallel")))(q, kv, kv)


def _sba_bwd(q, kv, lt, do, *, name):
    T = q.shape[0]
    Bq = SB_BLOCK
    nq = T // SB_KEYS
    nsub = SB_KEYS // Bq
    nscan = SB_KEYS // SB_SCAN
    R = 2 * SB_KEYS
    assert T % SB_KEYS == 0 and SB_STRIP == 2 * Bq
    scale = 1.0 / math.sqrt(SB_HEAD_DIM)

    def body(q_ref, k_ref, v_ref, lt_ref, do_ref, dq_ref, dk_ref, dv_ref, dk_acc, dv_acc,
             z_s, da_s, a_s, dz_s, pc_s, pe_s, lt_s, dq_s):
        i = pl.program_id(1)

        @pl.when(i == 0)
        def _():
            dk_acc[...] = jnp.zeros_like(dk_acc)
            dv_acc[...] = jnp.zeros_like(dv_acc)

        Uincl = _tri(SB_SCAN, lambda k, j: k <= j)
        Uexcl = _tri(SB_SCAN, lambda k, j: k < j)
        qs, dos = [], []
        for a in range(nsub):
            rows = slice(a * Bq, (a + 1) * Bq)
            qs.append(_stack_heads(q_ref[rows, :] * scale))
            dos.append(_stack_heads(do_ref[rows, :]))
            lt_s[_sba_rows(a), :] = jnp.concatenate([lt_ref[rows, 0:1], lt_ref[rows, 64:65]], axis=0)
        qs_all = jnp.concatenate(qs, axis=0)
        dos_all = jnp.concatenate(dos, axis=0)
        pc_s[...] = jnp.zeros_like(pc_s)
        pe_s[...] = jnp.zeros_like(pe_s)
        a_s[1] = jnp.zeros((R, SB_KEYS), BF16)
        dz_s[1] = jnp.zeros((R, SB_KEYS), BF16)

        def scores(J, slot):
            off = pl.multiple_of(J * SB_KEYS, SB_KEYS)
            z_s[slot] = lax.dot_general(qs_all, k_ref[pl.ds(off, SB_KEYS), :], _NT, preferred_element_type=F32)
            da_s[slot] = lax.dot_general(dos_all, v_ref[pl.ds(off, SB_KEYS), :], _NT, preferred_element_type=F32)

        def gradients(slot, diag):
            for a in range(nsub):
                rows = _sba_rows(a)
                pc, pe, Lt = pc_s[rows, :], pe_s[rows, :], lt_s[rows, :]
                for b in range(nscan):
                    cols = slice(b * SB_SCAN, (b + 1) * SB_SCAN)
                    case = _sba_diag_case(a, b) if diag else "full"
                    if isinstance(case, str) and case == "skip":
                        a_s[slot, rows, cols] = jnp.zeros((2 * Bq, SB_SCAN), BF16)
                        dz_s[slot, rows, cols] = jnp.zeros((2 * Bq, SB_SCAN), BF16)
                        continue
                    A, dz, pc, pe = _sba_sub_bwd(z_s[slot, rows, cols], da_s[slot, rows, cols], Lt, pc, pe, Uincl, Uexcl,
                                                 None if isinstance(case, str) else case)
                    a_s[slot, rows, cols] = A
                    dz_s[slot, rows, cols] = dz
                pc_s[rows, :] = pc
                pe_s[rows, :] = pe

        def products(J, slot):
            off = pl.multiple_of(J * SB_KEYS, SB_KEYS)
            dzt = dz_s[slot]
            dk_acc[pl.ds(off, SB_KEYS), :] += lax.dot_general(dzt, qs_all, _TN, preferred_element_type=F32)
            dv_acc[pl.ds(off, SB_KEYS), :] += lax.dot_general(a_s[slot], dos_all, _TN, preferred_element_type=F32)
            dq_s[...] += jnp.dot(dzt, k_ref[pl.ds(off, SB_KEYS), :], preferred_element_type=F32)

        dq_s[...] = jnp.zeros_like(dq_s)
        scores(0, 0)

        def two_steps(u, _):
            t = 2 * u
            gradients(0, False)
            scores(t + 1, 1)
            products(jnp.maximum(t - 1, 0), 1)
            gradients(1, False)
            scores(t + 2, 0)
            products(t, 0)
            return 0

        lax.fori_loop(0, i // 2, two_steps, 0)
        odd = lax.rem(i, 2) == 1

        @pl.when(jnp.logical_not(odd))
        def _():
            gradients(0, True)
            products(jnp.maximum(i - 1, 0), 1)
            products(i, 0)

        @pl.when(odd)
        def _():
            gradients(0, False)
            scores(i, 1)
            products(jnp.maximum(i - 2, 0), 1)
            gradients(1, True)
            products(i - 1, 0)
            products(i, 1)

        for a in range(nsub):
            dq_ref[a * Bq:(a + 1) * Bq, :] = (_unstack_heads(dq_s[_sba_rows(a), :]) * scale).astype(BF16)

        @pl.when(i == nq - 1)
        def _():
            dk_ref[...] = dk_acc[...].astype(BF16)
            dv_ref[...] = dv_acc[...].astype(BF16)

    return pl.pallas_call(
        body, name=name, grid=(SB_HEADS // 2, nq),
        in_specs=[pl.BlockSpec((SB_KEYS, 128), lambda p, i: (i, p)), pl.BlockSpec((T, 128), lambda p, i: (0, p)),
                  pl.BlockSpec((T, 128), lambda p, i: (0, p + SB_HEADS // 2)),
                  pl.BlockSpec((None, SB_KEYS, 128), lambda p, i: (p, i, 0)),
                  pl.BlockSpec((SB_KEYS, 128), lambda p, i: (i, p))],
        out_specs=[pl.BlockSpec((SB_KEYS, 128), lambda p, i: (i, p)), pl.BlockSpec((T, 128), lambda p, i: (0, p)),
                   pl.BlockSpec((T, 128), lambda p, i: (0, p))],
        out_shape=[jax.ShapeDtypeStruct((T, D_MODEL), BF16), jax.ShapeDtypeStruct((T, D_MODEL), BF16),
                   jax.ShapeDtypeStruct((T, D_MODEL), BF16)],
        scratch_shapes=[pltpu.VMEM((T, 128), F32), pltpu.VMEM((T, 128), F32),
                        pltpu.VMEM((2, R, SB_KEYS), F32), pltpu.VMEM((2, R, SB_KEYS), F32),
                        pltpu.VMEM((2, R, SB_KEYS), BF16), pltpu.VMEM((2, R, SB_KEYS), BF16),
                        pltpu.VMEM((R, 1), F32), pltpu.VMEM((R, 1), F32), pltpu.VMEM((R, 1), F32),
                        pltpu.VMEM((R, 128), F32)],
        compiler_params=_cparams(("parallel", "arbitrary")))(q, kv, kv, lt, do)


def _sba_fwd_old(q, kv, *, name):
    T = q.shape[0]
    Bq = SB_BLOCK
    nsub = SB_KEYS // Bq
    assert T % SB_KEYS == 0
    scale = 1.0 / math.sqrt(SB_HEAD_DIM)

    def body(q_ref, k_ref, v_ref, o_ref, lt_ref):
        I = pl.program_id(1)
        U1 = _tri(Bq, lambda k, j: k > j)
        U2 = _tri(SB_SCAN, lambda k, j: k > j)
        dmask = _sba_diag_mask()
        qs = [_stack_heads(q_ref[a * Bq:(a + 1) * Bq, :] * scale) for a in range(nsub)]
        cs, accs = [], []
        for a in range(nsub):
            c = jnp.zeros((2 * Bq, 1), F32)
            acc = jnp.zeros((2 * Bq, 128), F32)
            for b in range(a, -1, -1):
                off = pl.multiple_of(I * SB_KEYS + b * Bq, Bq)
                zb = lax.dot_general(qs[a], k_ref[pl.ds(off, Bq), :], _NT, preferred_element_type=F32)
                A, c = _sba_sub_fwd(zb, c, U1, dmask if b == a else None)
                acc = acc + jnp.dot(A, v_ref[pl.ds(off, Bq), :], preferred_element_type=F32)
            cs.append(c)
            accs.append(acc)
        qs_all = jnp.concatenate(qs, axis=0)

        def step(n, carry):
            c, acc = carry
            off = pl.multiple_of((I - 1 - n) * SB_KEYS, SB_KEYS)
            z = lax.dot_general(qs_all, k_ref[pl.ds(off, SB_KEYS), :], _NT, preferred_element_type=F32)
            parts = [None] * (SB_KEYS // SB_SCAN)
            for b in reversed(range(SB_KEYS // SB_SCAN)):
                parts[b], c = _sba_sub_fwd(z[:, b * SB_SCAN:(b + 1) * SB_SCAN], c, U2, None)
            return c, acc + jnp.dot(jnp.concatenate(parts, axis=1), v_ref[pl.ds(off, SB_KEYS), :],
                                    preferred_element_type=F32)

        c, acc = lax.fori_loop(0, I, step, (jnp.concatenate(cs, axis=0), jnp.concatenate(accs, axis=0)))
        for a in range(nsub):
            rows = slice(2 * a * Bq, 2 * (a + 1) * Bq)
            o_ref[a * Bq:(a + 1) * Bq, :] = _unstack_heads(acc[rows]).astype(BF16)
            lt_ref[a * Bq:(a + 1) * Bq, :] = _unstack_heads(jnp.broadcast_to(c[rows], (2 * Bq, 128)))

    return pl.pallas_call(
        body, name=name, grid=(SB_HEADS // 2, T // SB_KEYS),
        in_specs=[pl.BlockSpec((SB_KEYS, 128), lambda p, i: (i, p)), pl.BlockSpec((T, 128), lambda p, i: (0, p)),
                  pl.BlockSpec((T, 128), lambda p, i: (0, p + SB_HEADS // 2))],
        out_specs=[pl.BlockSpec((SB_KEYS, 128), lambda p, i: (i, p)),
                   pl.BlockSpec((None, SB_KEYS, 128), lambda p, i: (p, i, 0))],
        out_shape=[jax.ShapeDtypeStruct((T, D_MODEL), BF16), jax.ShapeDtypeStruct((SB_HEADS // 2, T, 128), F32)],
        compiler_params=_cparams(("parallel", "parallel")))(q, kv, kv)


def _sba_bwd_old(q, kv, lt, do, *, name):
    T = q.shape[0]
    Bq = SB_BLOCK
    nq = T // SB_KEYS
    nsub = SB_KEYS // Bq
    assert T % SB_KEYS == 0
    scale = 1.0 / math.sqrt(SB_HEAD_DIM)

    def body(q_ref, k_ref, v_ref, lt_ref, do_ref, dq_ref, dk_ref, dv_ref, dk_acc, dv_acc,
             z_s, da_s, a_s, dz_s, pc_s, pe_s, lt_s):
        i = pl.program_id(1)

        @pl.when(i == 0)
        def _():
            dk_acc[...] = jnp.zeros_like(dk_acc)
            dv_acc[...] = jnp.zeros_like(dv_acc)

        Uincl1 = _tri(Bq, lambda k, j: k <= j)
        Uexcl1 = _tri(Bq, lambda k, j: k < j)
        Uincl2 = _tri(SB_SCAN, lambda k, j: k <= j)
        Uexcl2 = _tri(SB_SCAN, lambda k, j: k < j)
        dmask = _sba_diag_mask()
        qs, dos, lts = [], [], []
        for a in range(nsub):
            rows = slice(a * Bq, (a + 1) * Bq)
            qs.append(_stack_heads(q_ref[rows, :] * scale))
            dos.append(_stack_heads(do_ref[rows, :]))
            lts.append(jnp.concatenate([lt_ref[rows, 0:1], lt_ref[rows, 64:65]], axis=0))
        qs_all = jnp.concatenate(qs, axis=0)
        dos_all = jnp.concatenate(dos, axis=0)
        lt_all = jnp.concatenate(lts, axis=0)

        R = 2 * nsub * Bq
        pc_s[...] = jnp.zeros_like(pc_s)
        pe_s[...] = jnp.zeros_like(pe_s)
        lt_s[...] = lt_all

        def scores(J, slot):
            off = pl.multiple_of(J * SB_KEYS, SB_KEYS)
            z_s[slot] = lax.dot_general(qs_all, k_ref[pl.ds(off, SB_KEYS), :], _NT, preferred_element_type=F32)
            da_s[slot] = lax.dot_general(dos_all, v_ref[pl.ds(off, SB_KEYS), :], _NT, preferred_element_type=F32)

        def elementwise(slot):
            for r in range(R // SB_STRIP):
                rows = slice(r * SB_STRIP, (r + 1) * SB_STRIP)
                pc, pe, Lt = pc_s[rows, :], pe_s[rows, :], lt_s[rows, :]
                for b in range(SB_KEYS // SB_SCAN):
                    cols = slice(b * SB_SCAN, (b + 1) * SB_SCAN)
                    A, dz, pc, pe = _sba_sub_bwd(z_s[slot, rows, cols], da_s[slot, rows, cols], Lt, pc, pe,
                                                 Uincl2, Uexcl2, None)
                    a_s[slot, rows, cols] = A
                    dz_s[slot, rows, cols] = dz
                pc_s[rows, :] = pc
                pe_s[rows, :] = pe

        def outputs(J, slot, dq_acc):
            off = pl.multiple_of(J * SB_KEYS, SB_KEYS)
            dzt = dz_s[slot]
            dk_acc[pl.ds(off, SB_KEYS), :] += lax.dot_general(dzt, qs_all, _TN, preferred_element_type=F32)
            dv_acc[pl.ds(off, SB_KEYS), :] += lax.dot_general(a_s[slot], dos_all, _TN, preferred_element_type=F32)
            return dq_acc + jnp.dot(dzt, k_ref[pl.ds(off, SB_KEYS), :], preferred_element_type=F32)

        a_s[1] = jnp.zeros((R, SB_KEYS), BF16)
        dz_s[1] = jnp.zeros((R, SB_KEYS), BF16)
        last = jnp.maximum(i - 1, 0)
        scores(0, 0)

        def step(J, dq_acc):
            slot = lax.rem(J, 2)
            elementwise(slot)
            scores(jnp.minimum(J + 1, last), 1 - slot)
            return outputs(jnp.maximum(J - 1, 0), 1 - slot, dq_acc)

        dq_acc = lax.fori_loop(0, i, step, jnp.zeros((R, 128), F32))
        dq_acc = outputs(last, lax.rem(i + 1, 2), dq_acc)
        pc, pe = pc_s[...], pe_s[...]
        for a in range(nsub):
            rows = slice(2 * a * Bq, 2 * (a + 1) * Bq)
            pca, pea, dqa = pc[rows], pe[rows], dq_acc[rows]
            for b in range(a + 1):
                off = pl.multiple_of(i * SB_KEYS + b * Bq, Bq)
                kb = k_ref[pl.ds(off, Bq), :]
                zb = lax.dot_general(qs[a], kb, _NT, preferred_element_type=F32)
                dAb = lax.dot_general(dos[a], v_ref[pl.ds(off, Bq), :], _NT, preferred_element_type=F32)
                A, dz, pca, pea = _sba_sub_bwd(zb, dAb, lts[a], pca, pea, Uincl1, Uexcl1, dmask if b == a else None)
                dqa = dqa + jnp.dot(dz, kb, preferred_element_type=F32)
                dk_acc[pl.ds(off, Bq), :] += lax.dot_general(dz, qs[a], _TN, preferred_element_type=F32)
                dv_acc[pl.ds(off, Bq), :] += lax.dot_general(A, dos[a], _TN, preferred_element_type=F32)
            dq_ref[a * Bq:(a + 1) * Bq, :] = (_unstack_heads(dqa) * scale).astype(BF16)

        @pl.when(i == nq - 1)
        def _():
            dk_ref[...] = dk_acc[...].astype(BF16)
            dv_ref[...] = dv_acc[...].astype(BF16)

    return pl.pallas_call(
        body, name=name, grid=(SB_HEADS // 2, nq),
        in_specs=[pl.BlockSpec((SB_KEYS, 128), lambda p, i: (i, p)), pl.BlockSpec((T, 128), lambda p, i: (0, p)),
                  pl.BlockSpec((T, 128), lambda p, i: (0, p + SB_HEADS // 2)),
                  pl.BlockSpec((None, SB_KEYS, 128), lambda p, i: (p, i, 0)),
                  pl.BlockSpec((SB_KEYS, 128), lambda p, i: (i, p))],
        out_specs=[pl.BlockSpec((SB_KEYS, 128), lambda p, i: (i, p)), pl.BlockSpec((T, 128), lambda p, i: (0, p)),
                   pl.BlockSpec((T, 128), lambda p, i: (0, p))],
        out_shape=[jax.ShapeDtypeStruct((T, D_MODEL), BF16), jax.ShapeDtypeStruct((T, D_MODEL), BF16),
                   jax.ShapeDtypeStruct((T, D_MODEL), BF16)],
        scratch_shapes=[pltpu.VMEM((T, 128), F32), pltpu.VMEM((T, 128), F32),
                        pltpu.VMEM((2, 2 * SB_KEYS, SB_KEYS), F32), pltpu.VMEM((2, 2 * SB_KEYS, SB_KEYS), F32),
                        pltpu.VMEM((2, 2 * SB_KEYS, SB_KEYS), BF16), pltpu.VMEM((2, 2 * SB_KEYS, SB_KEYS), BF16),
                        pltpu.VMEM((2 * SB_KEYS, 1), F32), pltpu.VMEM((2 * SB_KEYS, 1), F32),
                        pltpu.VMEM((2 * SB_KEYS, 1), F32)],
        compiler_params=_cparams(("parallel", "arbitrary")))(q, kv, kv, lt, do)


def _loss_head(h, tgt, w, *, name, tt=512):
    T, D = h.shape
    tt = min(tt, T)

    def body(h_ref, t_ref, w_ref, loss_ref, dh_ref, dw_ref):
        i = pl.program_id(0)
        hv = h_ref[...]
        wv = w_ref[...]
        r = lax.rsqrt(jnp.mean(hv * hv, axis=-1, keepdims=True) + EPS)
        xhat = hv * r
        err = xhat * wv - t_ref[...]
        part = 0.5 * jnp.sum(jnp.mean(err * err, axis=-1, keepdims=True), axis=0, keepdims=True)
        dy = err * (1.0 / D)
        dxh = dy * wv
        dh_ref[...] = r * (dxh - xhat * jnp.mean(dxh * xhat, axis=-1, keepdims=True))
        dwc = jnp.sum(dy * xhat, axis=0, keepdims=True)

        @pl.when(i == 0)
        def _():
            loss_ref[...] = jnp.broadcast_to(part, loss_ref.shape)
            dw_ref[...] = dwc

        @pl.when(i > 0)
        def _():
            loss_ref[...] += jnp.broadcast_to(part, loss_ref.shape)
            dw_ref[...] += dwc

    return pl.pallas_call(
        body, name=name, grid=(T // tt,),
        in_specs=[pl.BlockSpec((tt, D), lambda i: (i, 0)), pl.BlockSpec((tt, D), lambda i: (i, 0)),
                  pl.BlockSpec((1, D), lambda i: (0, 0))],
        out_specs=[pl.BlockSpec((1, 128), lambda i: (0, 0)), pl.BlockSpec((tt, D), lambda i: (i, 0)),
                   pl.BlockSpec((1, D), lambda i: (0, 0))],
        out_shape=[jax.ShapeDtypeStruct((1, 128), F32), jax.ShapeDtypeStruct((T, D), F32),
                   jax.ShapeDtypeStruct((1, D), F32)],
        compiler_params=_cparams(("arbitrary",)))(h, tgt, w.reshape(1, D))


def _adamw(parts, w, m, v, *, name, tr=256):
    P, R, C = parts.shape
    tr = min(tr, R)
    assert R % tr == 0, (name, R, tr)
    c1 = 1.0 - ADAM_B1 ** ADAM_STEP
    c2 = 1.0 - ADAM_B2 ** ADAM_STEP

    def body(p_ref, w_ref, m_ref, v_ref, g_ref, d_ref, nm_ref, nv_ref):
        g = p_ref[0].astype(F32)
        for k in range(1, P):
            g = g + p_ref[k].astype(F32)
        mn = ADAM_B1 * m_ref[...] + (1.0 - ADAM_B1) * g
        vn = ADAM_B2 * v_ref[...] + (1.0 - ADAM_B2) * (g * g)
        g_ref[...] = g
        nm_ref[...] = mn
        nv_ref[...] = vn
        d_ref[...] = -ADAM_LR * ((mn / c1) / (jnp.sqrt(vn / c2) + ADAM_EPS) + ADAM_WD * w_ref[...])

    spec = pl.BlockSpec((tr, C), lambda i: (i, 0))
    sds = jax.ShapeDtypeStruct((R, C), F32)
    return pl.pallas_call(
        body, name=name, grid=(R // tr,),
        in_specs=[pl.BlockSpec((P, tr, C), lambda i: (0, i, 0)), spec, spec, spec],
        out_specs=[spec, spec, spec, spec], out_shape=[sds, sds, sds, sds],
        compiler_params=_cparams(("parallel",)))(parts, w, m, v)


def _all_gather(shards, *, name):
    n = len(shards)

    def body(*refs):
        ins, outs = refs[:n], refs[n:2 * n]
        send_sems, recv_sems, local_sems = refs[2 * n:]
        x, y, c = lax.axis_index("x"), lax.axis_index("y"), lax.axis_index("c")
        me, sib = (x, y, c), (x, y, 1 - c)
        chips = [(1 - x, y), (x, 1 - y), (1 - x, 1 - y)]

        def slot(p):
            return 4 * p[0] + 2 * p[1] + p[2]

        def cp(a, k, block, to, src=None):
            dst = outs[a].at[slot(block)]
            return pltpu.make_async_remote_copy(src_ref=dst if src is None else src, dst_ref=dst,
                                                send_sem=send_sems.at[a, k], recv_sem=recv_sems.at[a, k],
                                                device_id=to, device_id_type=_MESH)

        mine = [pltpu.make_async_copy(ins[a], outs[a].at[slot(me)], local_sems.at[a]) for a in range(n)]
        for m in mine:
            m.start()
        first = []
        for a in range(n):
            first.append(cp(a, 0, me, sib, src=ins[a]))
            for j, chip in enumerate(chips):
                first.append(cp(a, 1 + j, me, (*chip, c), src=ins[a]))
        for f in first:
            f.start()
        passed = []
        for j, chip in enumerate(chips):
            for a in range(n):
                cp(a, 1 + j, (*chip, c), me).wait_recv()
                f = cp(a, 4 + j, (*chip, c), sib)
                f.start()
                passed.append(f)
        for a in range(n):
            cp(a, 0, sib, me).wait_recv()
            for j, chip in enumerate(chips):
                cp(a, 4 + j, (*chip, 1 - c), me).wait_recv()
        for f in first + passed:
            f.wait_send()
        for m in mine:
            m.wait()

    return pl.pallas_call(
        body, name=name, in_specs=[_ANY] * n, out_specs=[_ANY] * n,
        out_shape=[jax.ShapeDtypeStruct((N_DEV,) + s.shape, s.dtype) for s in shards],
        scratch_shapes=[pltpu.SemaphoreType.DMA((n, 7)), pltpu.SemaphoreType.DMA((n, 7)),
                        pltpu.SemaphoreType.DMA((n,))])(*shards)


def _exchange(blocks, *, name):
    n = len(blocks)

    def body(*refs):
        ins, outs = refs[:n], refs[n:2 * n]
        send_sems, recv_sems, local_sems = refs[2 * n:]
        x, y, c = lax.axis_index("x"), lax.axis_index("y"), lax.axis_index("c")
        me = 4 * x + 2 * y + c
        mine = [pltpu.make_async_copy(ins[a].at[me], outs[a].at[me], local_sems.at[a]) for a in range(n)]
        for m in mine:
            m.start()
        copies = []
        for r in range(1, N_DEV):
            rx, ry, rc = (r >> 2) & 1, (r >> 1) & 1, r & 1
            px, py, pc = (1 - x if rx else x), (1 - y if ry else y), (1 - c if rc else c)
            peer = 4 * px + 2 * py + pc
            for a in range(n):
                copies.append((pltpu.make_async_remote_copy(
                    src_ref=ins[a].at[peer], dst_ref=outs[a].at[me], send_sem=send_sems.at[a, r - 1],
                    recv_sem=recv_sems.at[a, r - 1], device_id=(px, py, pc), device_id_type=_MESH),
                    pltpu.make_async_remote_copy(
                    src_ref=ins[a].at[peer], dst_ref=outs[a].at[peer], send_sem=send_sems.at[a, r - 1],
                    recv_sem=recv_sems.at[a, r - 1], device_id=(px, py, pc), device_id_type=_MESH)))
        for snd, _ in copies:
            snd.start()
        for _, rcv in copies:
            rcv.wait_recv()
        for snd, _ in copies:
            snd.wait_send()
        for m in mine:
            m.wait()

    return pl.pallas_call(
        body, name=name, in_specs=[_ANY] * n, out_specs=[_ANY] * n,
        out_shape=[jax.ShapeDtypeStruct(b.shape, b.dtype) for b in blocks],
        scratch_shapes=[pltpu.SemaphoreType.DMA((n, 7)), pltpu.SemaphoreType.DMA((n, 7)),
                        pltpu.SemaphoreType.DMA((n,))])(*blocks)


_HBM = pl.BlockSpec(memory_space=pltpu.HBM)
_SEM = pl.BlockSpec(memory_space=pltpu.SEMAPHORE)
_EFFECT = pltpu.SideEffectType.DATAFLOW_SIDE_EFFECTING


def _peers():
    x, y, c = lax.axis_index("x"), lax.axis_index("y"), lax.axis_index("c")
    out = []
    for r in range(1, N_DEV):
        px = 1 - x if (r >> 2) & 1 else x
        py = 1 - y if (r >> 1) & 1 else y
        pc = 1 - c if r & 1 else c
        out.append(((px, py, pc), 4 * px + 2 * py + pc))
    return 4 * x + 2 * y + c, out


def _push_copy(src_ref, land_ref, send_sems, recv_sems, a, k, me, peer, peer_slot, scatter, arriving):
    src = src_ref.at[peer_slot] if scatter else src_ref
    return pltpu.make_async_remote_copy(
        src_ref=src, dst_ref=land_ref.at[peer_slot if arriving else me], send_sem=send_sems.at[a * (N_DEV - 1) + k],
        recv_sem=recv_sems.at[a * (N_DEV - 1) + k], device_id=peer, device_id_type=_MESH)


def _push_start(srcs, *, scatter, name):
    n = len(srcs)
    lands = [lax.empty(s.shape if scatter else (N_DEV,) + s.shape, s.dtype) for s in srcs]

    def body(*refs):
        src_refs, land_refs = refs[:n], refs[n:2 * n]
        send_sems, recv_sems = refs[2 * n], refs[2 * n + 1]
        token = refs[-1]
        me, peers = _peers()
        for k, (peer, slot) in enumerate(peers):
            for a in range(n):
                _push_copy(src_refs[a], land_refs[a], send_sems, recv_sems, a, k, me, peer, slot, scatter, False).start()
        token[...] = jnp.zeros_like(token)

    hbm = lambda a: pltpu.HBM(a.shape, a.dtype)
    outs = pl.pallas_call(
        body, name=name,
        out_shape=(pltpu.SemaphoreType.DMA((n * (N_DEV - 1),)), pltpu.SemaphoreType.DMA((n * (N_DEV - 1),)),
                   *[hbm(s) for s in srcs], *[hbm(l) for l in lands], jax.ShapeDtypeStruct((8, 128), F32)),
        in_specs=[_HBM] * (2 * n),
        out_specs=(_SEM, _SEM, *([_HBM] * (2 * n)), pl.BlockSpec(memory_space=pltpu.VMEM)),
        input_output_aliases={i: 2 + i for i in range(2 * n)},
        compiler_params=pltpu.CompilerParams(has_side_effects=_EFFECT),
    )(*[pltpu.with_memory_space_constraint(s, pltpu.HBM) for s in srcs],
      *[pltpu.with_memory_space_constraint(l, pltpu.HBM) for l in lands])
    return dict(send=outs[0], recv=outs[1], srcs=list(outs[2:2 + n]), lands=list(outs[2 + n:2 + 2 * n]),
                token=outs[-1], scatter=scatter, n=n)


def _push_wait(h, after, *, name):
    n, scatter = h["n"], h["scatter"]

    def body(*refs):
        src_refs, land_refs = refs[:n], refs[n:2 * n]
        send_sems, recv_sems = refs[2 * n], refs[2 * n + 1]
        me, peers = _peers()
        for k, (peer, slot) in enumerate(peers):
            for a in range(n):
                cp = _push_copy(src_refs[a], land_refs[a], send_sems, recv_sems, a, k, me, peer, slot, scatter, True)
                cp.wait_send()
                cp.wait_recv()

    hbm = lambda a: pltpu.HBM(a.shape, a.dtype)
    outs = pl.pallas_call(
        body, name=name,
        out_shape=(*[hbm(s) for s in h["srcs"]], *[hbm(l) for l in h["lands"]]),
        in_specs=[_HBM] * (2 * n) + [_SEM, _SEM, _ANY], out_specs=tuple([_HBM] * (2 * n)),
        input_output_aliases={i: i for i in range(2 * n)},
        compiler_params=pltpu.CompilerParams(has_side_effects=_EFFECT),
    )(*h["srcs"], *h["lands"], h["send"], h["recv"], after)
    return list(outs[:n]), list(outs[n:])


def _ffn_fwd(h, nw, w_up, conv_w, conv_b, w_down, tag):
    a3 = _mm_fwd(h, w_up, norm_w=nw, name=f"ffn{tag}_up", out_dtype=BF16, halves=True, tm=1024, tn=1408)
    p = _ffn_conv_fwd3(a3, conv_w, conv_b.reshape(1, -1), name=f"ffn{tag}_conv")
    h_out = _mm_fwd(p, w_down, residual=h, name=f"ffn{tag}_down", tm=1024, tn=512)
    return h_out, (a3, p)


def _ffn_bwd(dh, h, saved, nw, w_up, conv_w, conv_b, w_down, tag):
    a3, p = saved
    g_down = _mm_tn(p, dh, name=f"ffn{tag}_down_wg", tk1=1408, tn=1024)
    dp = _mm_nt(dh, w_down, name=f"ffn{tag}_down_dg", out_dtype=BF16, tm=1024, tn=1408, tk=1024)
    dhid3, dw3, db3 = _ffn_conv_bwd3(a3, conv_w, conv_b.reshape(1, -1), dp, name=f"ffn{tag}_conv_bwd")
    da3 = _conv_bwd_in3(dhid3, conv_w, K=FFN_CONV, name=f"ffn{tag}_conv_bwd_in")
    g_up = _mm_tn(h, da3, norm_w=nw, name=f"ffn{tag}_up_wg", tn=1408, tt=1024)
    dh_out, g_nw = _mm_nt(da3, w_up, epi=(h, nw, dh), name=f"ffn{tag}_up_dg", tm=1024, tk=1408)
    g_cw = jnp.concatenate([dw3[0], dw3[1]], axis=1)
    g_cb = jnp.concatenate([db3[0], db3[1]], axis=1)
    return dh_out, dict(norm=g_nw.reshape(-1), up=g_up, conv_w=g_cw, conv_b=g_cb.reshape(-1), down=g_down)


def _local_step(x, tgt, W):
    T = x.shape[0]
    f = {}
    zx = _mm_fwd(x, W["in_w"], norm_w=W["ssm_norm_w"], name="ssm_in", tm=1024, tn=896)
    xbc_c = _ssm_conv_fwd(zx, W["ssm_conv_w"], W["ssm_conv_b"].reshape(1, -1), name="ssm_conv")
    dt_raw = zx[:, D_INNER + CONV_DIM:IN_PROJ_DIM]
    dtg = jnp.pad(dt_raw.reshape(T, SSM_GROUPS, 8).transpose(1, 0, 2), ((0, 0), (0, 0), (0, 120)))
    par = jnp.stack([W["ssm_dt_bias"].reshape(SSM_GROUPS, 8), W["ssm_a_log"].reshape(SSM_GROUPS, 8),
                     W["ssm_d"].reshape(SSM_GROUPS, 8)], axis=1)
    par = jnp.pad(par, ((0, 0), (0, 5), (0, 120)))
    gnw = W["ssm_gate_norm_w"].reshape(1, D_INNER)
    y, yn, st = _ssd_fwd(xbc_c, zx, dtg, par, gnw, name="ssd_fwd")
    h1 = _mm_fwd(yn, W["ssm_out_w"], residual=x, name="ssm_out", tm=1024, tn=512)
    h2, ffn0 = _ffn_fwd(h1, W["ffn_norm_w"][0], W["ffn_up_w"][0], W["ffn_conv_w"][0], W["ffn_conv_b"][0],
                        W["ffn_down_w"][0], "0")
    q = _mm_fwd(h2, W["w_q"], norm_w=W["attn_norm_w"], out_dtype=BF16, name="attn_q", tm=1024, tn=1024)
    kv = _mm_fwd(h2, W["w_kv"], norm_w=W["kv_norm_w"], out_dtype=BF16, name="attn_kv", tm=1024, tn=1024)
    o, lt = _sba_fwd(q, kv, name="sba_fwd")
    h3 = _mm_fwd(o, W["w_o"], residual=h2, name="attn_o", tm=1024, tn=512)
    h4, ffn1 = _ffn_fwd(h3, W["ffn_norm_w"][1], W["ffn_up_w"][1], W["ffn_conv_w"][1], W["ffn_conv_b"][1],
                        W["ffn_down_w"][1], "1")
    loss, dh4, g_final = _loss_head(h4, tgt, W["final_norm_w"], name="loss_head")
    dh3, gf1 = _ffn_bwd(dh4, h3, ffn1, W["ffn_norm_w"][1], W["ffn_up_w"][1], W["ffn_conv_w"][1], W["ffn_conv_b"][1],
                        W["ffn_down_w"][1], "1")
    g_wo = _mm_tn(o, dh3, name="attn_o_wg", tn=1024)
    do = _mm_nt(dh3, W["w_o"], name="attn_o_dg", out_dtype=BF16, tn=1024, tk=1024)
    dq, dk, dv = _sba_bwd(q, kv, lt, do, name="sba_bwd")
    g_wq = _mm_tn(h2, dq, norm_w=W["attn_norm_w"], name="attn_q_wg", tn=1024)
    dh2a, g_attn_nw = _mm_nt(dq, W["w_q"], epi=(h2, W["attn_norm_w"], dh3), name="attn_q_dg", tk=1024)
    dkv = jnp.concatenate([dk, dv], axis=1)
    g_wkv = _mm_tn(h2, dkv, norm_w=W["kv_norm_w"], name="attn_kv_wg", tn=1024)
    dh2, g_kv_nw = _mm_nt(dkv, W["w_kv"], epi=(h2, W["kv_norm_w"], dh2a), name="attn_kv_dg", tk=1024)
    dh1, gf0 = _ffn_bwd(dh2, h1, ffn0, W["ffn_norm_w"][0], W["ffn_up_w"][0], W["ffn_conv_w"][0], W["ffn_conv_b"][0],
                        W["ffn_down_w"][0], "0")
    g_out = _mm_tn(yn, dh1, name="ssm_out_wg", tn=1024)
    dyn = _mm_nt(dh1, W["ssm_out_w"], name="ssm_out_dg", out_dtype=BF16, tn=1024, tk=1024)
    dxs, dB, dC, dz, ddt, g_gnw, dpar = _ssd_bwd(xbc_c, zx, dtg, par, gnw, y, st, dyn, name="ssd_bwd")
    dxbc_c = jnp.concatenate([dxs, dB, dC], axis=1)
    dhid, g_scw, g_scb = _ssm_conv_bwd_pre(zx, W["ssm_conv_w"], W["ssm_conv_b"].reshape(1, -1), dxbc_c,
                                           name="ssm_conv_bwd")
    dxbc = _conv_bwd_in(dhid, W["ssm_conv_w"], K=SSM_CONV, name="ssm_conv_bwd_in")
    ddt_t = ddt[:, :, :8].transpose(1, 0, 2).reshape(T, SSM_HEADS).astype(BF16)
    dzx = jnp.concatenate([dz, dxbc, jnp.pad(ddt_t, ((0, 0), (0, IN_PROJ_PAD - IN_PROJ_DIM)))], axis=1)
    g_in = _mm_tn(x, dzx, norm_w=W["ssm_norm_w"], name="ssm_in_wg", tn=896)
    dx, g_ssm_nw = _mm_nt(dzx, W["in_w"], epi=(x, W["ssm_norm_w"], dh1), name="ssm_in_dg", tk=1792)
    f["ssm_norm_w"] = g_ssm_nw.reshape(-1)
    f["ssm_in_w"] = g_in[:, :IN_PROJ_DIM]
    f["ssm_conv_w"] = g_scw
    f["ssm_conv_b"] = g_scb.reshape(-1)
    f["ssm_dt_bias"] = dpar[:, 0, :8].reshape(-1)
    f["ssm_a_log"] = dpar[:, 1, :8].reshape(-1)
    f["ssm_d"] = dpar[:, 2, :8].reshape(-1)
    f["ssm_gate_norm_w"] = g_gnw.reshape(-1)
    f["ssm_out_w"] = g_out
    f["kv_norm_w"] = g_kv_nw.reshape(-1)
    f["w_k"] = g_wkv[:, :D_MODEL]
    f["w_v"] = g_wkv[:, D_MODEL:]
    f["attn_norm_w"] = g_attn_nw.reshape(-1)
    f["w_q"] = g_wq
    f["w_o"] = g_wo
    f["ffn_norm_w"] = jnp.stack([gf0["norm"], gf1["norm"]])
    f["ffn_up_w"] = [gf0["up"], gf1["up"]]
    f["ffn_conv_w"] = jnp.stack([gf0["conv_w"], gf1["conv_w"]])
    f["ffn_conv_b"] = jnp.stack([gf0["conv_b"], gf1["conv_b"]])
    f["ffn_down_w"] = [gf0["down"], gf1["down"]]
    f["final_norm_w"] = g_final.reshape(-1)
    return loss, dx, f


_BIG = ["ssm_in_w", "ssm_out_w", "w_k", "w_v", "w_q", "w_o", "ffn_up_w", "ffn_down_w"]
_SMALL_SHARDED = ["ssm_norm_w", "ssm_conv_w", "ssm_conv_b", "ssm_gate_norm_w", "ffn_conv_w"]
_SMALL_REPL = ["ssm_dt_bias", "ssm_a_log", "ssm_d", "kv_norm_w", "attn_norm_w", "ffn_norm_w", "ffn_conv_b",
               "final_norm_w"]
_WEIGHTS = ["ssm_norm_w", "ssm_in_w", "ssm_conv_w", "ssm_conv_b", "ssm_dt_bias", "ssm_a_log", "ssm_d",
            "ssm_gate_norm_w", "ssm_out_w", "kv_norm_w", "w_k", "w_v", "attn_norm_w", "w_q", "w_o", "ffn_norm_w",
            "ffn_up_w", "ffn_conv_w", "ffn_conv_b", "ffn_down_w", "final_norm_w"]


def _as2d(a):
    return a.reshape(-1, a.shape[-1])


def _cols_to_full(g):
    return g.transpose(1, 0, 2).reshape(g.shape[1], N_DEV * g.shape[2])


def _full_to_cols(a):
    R = a.shape[0]
    return a.reshape(R, N_DEV, -1).transpose(1, 0, 2)


def _gather_weights(p):
    names = _BIG + _SMALL_SHARDED
    shards = [_as2d(p[n]).astype(BF16) for n in _BIG] + [_as2d(p[n]) for n in _SMALL_SHARDED]
    got = dict(zip(names, _all_gather(shards, name="gather_weights")))
    W = {n: p[n] for n in _SMALL_REPL}
    in_w = _cols_to_full(got["ssm_in_w"])
    W["in_w"] = jnp.pad(in_w, ((0, 0), (0, IN_PROJ_PAD - IN_PROJ_DIM)))
    W["ssm_out_w"] = got["ssm_out_w"].reshape(D_INNER, D_MODEL)
    W["w_kv"] = jnp.concatenate([got["w_k"].reshape(D_MODEL, D_MODEL), got["w_v"].reshape(D_MODEL, D_MODEL)], axis=1)
    W["w_q"] = got["w_q"].reshape(D_MODEL, D_MODEL)
    W["w_o"] = got["w_o"].reshape(D_MODEL, D_MODEL)
    up = got["ffn_up_w"]
    W["ffn_up_w"] = [_cols_to_full(up[:, l * D_MODEL:(l + 1) * D_MODEL]) for l in range(2)]
    dn = got["ffn_down_w"]
    rs = D_FF // N_DEV
    W["ffn_down_w"] = [dn[:, l * rs:(l + 1) * rs].reshape(D_FF, D_MODEL) for l in range(2)]
    W["ssm_norm_w"] = got["ssm_norm_w"].reshape(D_MODEL)
    W["ssm_conv_w"] = _cols_to_full(got["ssm_conv_w"])
    W["ssm_conv_b"] = got["ssm_conv_b"].reshape(CONV_DIM)
    W["ssm_gate_norm_w"] = got["ssm_gate_norm_w"].reshape(D_INNER)
    fcw = _cols_to_full(got["ffn_conv_w"])
    W["ffn_conv_w"] = fcw.reshape(2, FFN_CONV, 2 * D_FF)
    for n in ("ssm_dt_bias", "ssm_a_log", "ssm_d", "attn_norm_w"):
        W[n] = W[n].reshape(-1)
    return W


def _big_grad_blocks(f):
    rs = D_FF // N_DEV
    return {
        "ssm_in_w": _full_to_cols(f["ssm_in_w"]),
        "ssm_out_w": f["ssm_out_w"].reshape(N_DEV, D_INNER // N_DEV, D_MODEL),
        "w_k": f["w_k"].reshape(N_DEV, D_MODEL // N_DEV, D_MODEL),
        "w_v": f["w_v"].reshape(N_DEV, D_MODEL // N_DEV, D_MODEL),
        "w_q": f["w_q"].reshape(N_DEV, D_MODEL // N_DEV, D_MODEL),
        "w_o": f["w_o"].reshape(N_DEV, D_MODEL // N_DEV, D_MODEL),
        "ffn_up_w": jnp.concatenate([_full_to_cols(g) for g in f["ffn_up_w"]], axis=1),
        "ffn_down_w": jnp.concatenate([g.reshape(N_DEV, rs, D_MODEL) for g in f["ffn_down_w"]], axis=1),
    }


def _pack_small(vals):
    flat = jnp.concatenate([v.reshape(-1).astype(F32) for v in vals])
    n = flat.shape[0]
    rows = -(-n // 1024) * 8
    return jnp.pad(flat, (0, rows * 128 - n)).reshape(rows, 128)


def _unpack_small(packed, shapes):
    flat = packed.reshape(-1)
    out, off = [], 0
    for s in shapes:
        n = math.prod(s)
        out.append(flat[off:off + n].reshape(s))
        off += n
    return out


def _kernel_v1(x, ssm_norm_w, ssm_in_w, ssm_conv_w, ssm_conv_b, ssm_dt_bias, ssm_a_log, ssm_d, ssm_gate_norm_w, ssm_out_w, kv_norm_w, w_k, w_v, attn_norm_w, w_q, w_o, ffn_norm_w, ffn_up_w, ffn_conv_w, ffn_conv_b, ffn_down_w, final_norm_w, loss_target, m_ssm_norm_w, m_ssm_in_w, m_ssm_conv_w, m_ssm_conv_b, m_ssm_dt_bias, m_ssm_a_log, m_ssm_d, m_ssm_gate_norm_w, m_ssm_out_w, m_kv_norm_w, m_w_k, m_w_v, m_attn_norm_w, m_w_q, m_w_o, m_ffn_norm_w, m_ffn_up_w, m_ffn_conv_w, m_ffn_conv_b, m_ffn_down_w, m_final_norm_w, v_ssm_norm_w, v_ssm_in_w, v_ssm_conv_w, v_ssm_conv_b, v_ssm_dt_bias, v_ssm_a_log, v_ssm_d, v_ssm_gate_norm_w, v_ssm_out_w, v_kv_norm_w, v_w_k, v_w_v, v_attn_norm_w, v_w_q, v_w_o, v_ffn_norm_w, v_ffn_up_w, v_ffn_conv_w, v_ffn_conv_b, v_ffn_down_w, v_final_norm_w):
    env = dict(locals())
    p = {n: env[n] for n in _WEIGHTS}
    mom = {n: env["m_" + n] for n in _WEIGHTS}
    var = {n: env["v_" + n] for n in _WEIGHTS}
    T = x.shape[1]
    me = 4 * lax.axis_index("x") + 2 * lax.axis_index("y") + lax.axis_index("c")

    W = _gather_weights(p)
    loss_row, dx, f = _local_step(x.reshape(T, D_MODEL), loss_target.reshape(T, D_MODEL), W)
    loss = lax.psum(loss_row[0, 0], ("x", "y", "c"))

    big = _big_grad_blocks(f)
    small_names = _SMALL_REPL + _SMALL_SHARDED
    small_full = _pack_small([f[n] for n in small_names])
    small_bcast = jnp.broadcast_to(small_full[None], (N_DEV,) + small_full.shape)
    got = _exchange([big[n] for n in _BIG] + [small_bcast], name="exchange_grads")
    big_parts = dict(zip(_BIG, got[:-1]))

    zero = jnp.zeros_like(small_full)
    g_small_sum = _adamw(got[-1], zero, zero, zero, name="sum_small_grads", tr=small_full.shape[0])[0]
    full_shapes = [f[n].shape for n in small_names]
    g_small = dict(zip(small_names, _unpack_small(g_small_sum, full_shapes)))
    for n in _SMALL_SHARDED:
        width = p[n].shape[-1]
        g_small[n] = lax.dynamic_slice_in_dim(g_small[n], me * width, width, axis=g_small[n].ndim - 1)

    out_g, out_d, out_m, out_v = {}, {}, {}, {}
    for n in _BIG:
        w2, m2, v2 = _as2d(p[n]), _as2d(mom[n]), _as2d(var[n])
        tr = 352 if n == "ffn_down_w" else 256
        g, d, nm, nv = _adamw(big_parts[n], w2, m2, v2, name="adamw_" + n, tr=tr)
        out_g[n], out_d[n], out_m[n], out_v[n] = (t.reshape(p[n].shape) for t in (g, d, nm, nv))
    sw = _pack_small([p[n] for n in small_names])
    sm = _pack_small([mom[n] for n in small_names])
    sv = _pack_small([var[n] for n in small_names])
    sg = _pack_small([g_small[n] for n in small_names])
    _, d, nm, nv = _adamw(sg[None], sw, sm, sv, name="adamw_small", tr=sw.shape[0])
    shard_shapes = [p[n].shape for n in small_names]
    for n, dd, mm, vv in zip(small_names, _unpack_small(d, shard_shapes), _unpack_small(nm, shard_shapes),
                             _unpack_small(nv, shard_shapes)):
        out_g[n] = g_small[n].reshape(p[n].shape)
        out_d[n], out_m[n], out_v[n] = dd, mm, vv

    return (loss, dx.reshape(x.shape), *[out_g[n] for n in _WEIGHTS], *[out_d[n] for n in _WEIGHTS],
            *[out_m[n] for n in _WEIGHTS], *[out_v[n] for n in _WEIGHTS])


def _tie(a, token):
    return a + token[0, 0].astype(a.dtype)


def _local_step2(x, tgt, get_w, put_g):
    T = x.shape[0]
    Ws = get_w("ssm", None)
    fnw, fcw, fcb = Ws["ffn_norm_w"], Ws["ffn_conv_w"], Ws["ffn_conv_b"]
    zx = _mm_fwd(x, Ws["in_w"], norm_w=Ws["ssm_norm_w"], name="ssm_in", tm=1024, tn=896)
    xbc_c = _ssm_conv_fwd(zx, Ws["ssm_conv_w"], Ws["ssm_conv_b"].reshape(1, -1), name="ssm_conv")
    dt_raw = zx[:, D_INNER + CONV_DIM:IN_PROJ_DIM]
    dtg = jnp.pad(dt_raw.reshape(T, SSM_GROUPS, 8).transpose(1, 0, 2), ((0, 0), (0, 0), (0, 120)))
    par = jnp.stack([Ws["ssm_dt_bias"].reshape(SSM_GROUPS, 8), Ws["ssm_a_log"].reshape(SSM_GROUPS, 8),
                     Ws["ssm_d"].reshape(SSM_GROUPS, 8)], axis=1)
    par = jnp.pad(par, ((0, 0), (0, 5), (0, 120)))
    gnw = _tie(Ws["ssm_gate_norm_w"].reshape(1, D_INNER), get_w("rest_start", xbc_c))
    y, yn, st = _ssd_fwd(xbc_c, zx, dtg, par, gnw, name="ssd_fwd")
    W0 = get_w("ffn0", y)
    Ws["ssm_out_w"] = W0["ssm_out_w"]
    h1 = _mm_fwd(yn, Ws["ssm_out_w"], residual=x, name="ssm_out", tm=1024, tn=512)
    h2, ffn0 = _ffn_fwd(h1, fnw[0], W0["up"], fcw[0], fcb[0], W0["down"], "0")
    Wr = get_w("rest", h2)
    q = _mm_fwd(h2, Wr["w_q"], norm_w=Ws["attn_norm_w"], out_dtype=BF16, name="attn_q", tm=1024, tn=1024)
    kv = _mm_fwd(h2, Wr["w_kv"], norm_w=Ws["kv_norm_w"], out_dtype=BF16, name="attn_kv", tm=1024, tn=1024)
    o, lt = _sba_fwd(q, kv, name="sba_fwd")
    h3 = _mm_fwd(o, Wr["w_o"], residual=h2, name="attn_o", tm=1024, tn=512)
    h4, ffn1 = _ffn_fwd(h3, fnw[1], Wr["up"], fcw[1], fcb[1], Wr["down"], "1")
    loss, dh4, g_final = _loss_head(h4, tgt, Ws["final_norm_w"], name="loss_head")
    dh3, gf1 = _ffn_bwd(dh4, h3, ffn1, fnw[1], Wr["up"], fcw[1], fcb[1], Wr["down"], "1")
    tok = put_g("ffn1", dict(up=gf1["up"], down=gf1["down"]))
    g_wo = _mm_tn(o, dh3, name="attn_o_wg", tn=1024)
    do = _mm_nt(dh3, _tie(Wr["w_o"], tok), name="attn_o_dg", out_dtype=BF16, tn=1024, tk=1024)
    dq, dk, dv = _sba_bwd(q, kv, lt, do, name="sba_bwd")
    g_wq = _mm_tn(h2, dq, norm_w=Ws["attn_norm_w"], name="attn_q_wg", tn=1024, tt=1024)
    dh2a, g_attn_nw = _mm_nt(dq, Wr["w_q"], epi=(h2, Ws["attn_norm_w"], dh3), name="attn_q_dg", tm=1024, tk=1024)
    dkv = jnp.concatenate([dk, dv], axis=1)
    g_wkv = _mm_tn(h2, dkv, norm_w=Ws["kv_norm_w"], name="attn_kv_wg", tn=1024, tt=1024)
    dh2, g_kv_nw = _mm_nt(dkv, Wr["w_kv"], epi=(h2, Ws["kv_norm_w"], dh2a), name="attn_kv_dg", tm=1024, tk=1024)
    tok = put_g("attn", dict(w_o=g_wo, w_q=g_wq, w_k=g_wkv[:, :D_MODEL], w_v=g_wkv[:, D_MODEL:]))
    dh1, gf0 = _ffn_bwd(dh2, h1, ffn0, fnw[0], W0["up"], fcw[0], _tie(fcb[0], tok), W0["down"], "0")
    tok = put_g("ffn0", dict(up=gf0["up"], down=gf0["down"]))
    g_out = _mm_tn(yn, dh1, name="ssm_out_wg", tn=1024)
    dyn = _mm_nt(dh1, _tie(Ws["ssm_out_w"], tok), name="ssm_out_dg", out_dtype=BF16, tn=1024, tk=1024)
    tok = put_g("ssm_out", dict(ssm_out_w=g_out))
    dxbc_c, dz, ddt, g_gnw, dpar = _ssd_bwd(xbc_c, zx, dtg, par, _tie(gnw, tok), y, st, dyn, name="ssd_bwd")
    dhid, g_scw, g_scb = _ssm_conv_bwd_pre(zx, Ws["ssm_conv_w"], Ws["ssm_conv_b"].reshape(1, -1), dxbc_c,
                                           name="ssm_conv_bwd")
    dxbc = _conv_bwd_in(dhid, Ws["ssm_conv_w"], K=SSM_CONV, name="ssm_conv_bwd_in")
    ddt_t = ddt[:, :, :8].transpose(1, 0, 2).reshape(T, SSM_HEADS).astype(BF16)
    dzx = jnp.concatenate([dz, dxbc, jnp.pad(ddt_t, ((0, 0), (0, IN_PROJ_PAD - IN_PROJ_DIM)))], axis=1)
    g_in = _mm_tn(x, dzx, norm_w=Ws["ssm_norm_w"], name="ssm_in_wg", tn=896, tt=1024)
    tok = put_g("ssm_in", dict(ssm_in_w=g_in[:, :IN_PROJ_DIM]))
    dx, g_ssm_nw = _mm_nt(dzx, Ws["in_w"], epi=(x, _tie(Ws["ssm_norm_w"], tok), dh1), name="ssm_in_dg", tm=1024, tk=896)
    f = {
        "ssm_norm_w": g_ssm_nw.reshape(-1), "ssm_conv_w": g_scw,
        "ssm_conv_b": g_scb.reshape(-1), "ssm_dt_bias": dpar[:, 0, :8].reshape(-1),
        "ssm_a_log": dpar[:, 1, :8].reshape(-1), "ssm_d": dpar[:, 2, :8].reshape(-1),
        "ssm_gate_norm_w": g_gnw.reshape(-1), "kv_norm_w": g_kv_nw.reshape(-1), "attn_norm_w": g_attn_nw.reshape(-1),
        "ffn_norm_w": jnp.stack([gf0["norm"], gf1["norm"]]), "ffn_conv_w": jnp.stack([gf0["conv_w"], gf1["conv_w"]]),
        "ffn_conv_b": jnp.stack([gf0["conv_b"], gf1["conv_b"]]), "final_norm_w": g_final.reshape(-1),
    }
    return loss, dx, f


def kernel(x, ssm_norm_w, ssm_in_w, ssm_conv_w, ssm_conv_b, ssm_dt_bias, ssm_a_log, ssm_d, ssm_gate_norm_w, ssm_out_w, kv_norm_w, w_k, w_v, attn_norm_w, w_q, w_o, ffn_norm_w, ffn_up_w, ffn_conv_w, ffn_conv_b, ffn_down_w, final_norm_w, loss_target, m_ssm_norm_w, m_ssm_in_w, m_ssm_conv_w, m_ssm_conv_b, m_ssm_dt_bias, m_ssm_a_log, m_ssm_d, m_ssm_gate_norm_w, m_ssm_out_w, m_kv_norm_w, m_w_k, m_w_v, m_attn_norm_w, m_w_q, m_w_o, m_ffn_norm_w, m_ffn_up_w, m_ffn_conv_w, m_ffn_conv_b, m_ffn_down_w, m_final_norm_w, v_ssm_norm_w, v_ssm_in_w, v_ssm_conv_w, v_ssm_conv_b, v_ssm_dt_bias, v_ssm_a_log, v_ssm_d, v_ssm_gate_norm_w, v_ssm_out_w, v_kv_norm_w, v_w_k, v_w_v, v_attn_norm_w, v_w_q, v_w_o, v_ffn_norm_w, v_ffn_up_w, v_ffn_conv_w, v_ffn_conv_b, v_ffn_down_w, v_final_norm_w):
    env = dict(locals())
    p = {n: env[n] for n in _WEIGHTS}
    mom = {n: env["m_" + n] for n in _WEIGHTS}
    var = {n: env["v_" + n] for n in _WEIGHTS}
    T = x.shape[1]
    me = 4 * lax.axis_index("x") + 2 * lax.axis_index("y") + lax.axis_index("c")
    rs = D_FF // N_DEV

    def bf2(a):
        return _as2d(a).astype(BF16)

    def with_own(srcs, lands, scatter):
        out = []
        for s, l in zip(srcs, lands):
            own = lax.dynamic_index_in_dim(s, me, 0, keepdims=False) if scatter else s
            out.append(lax.dynamic_update_index_in_dim(l, own, me, 0))
        return out

    a_names = ["ssm_in_w"] + _SMALL_SHARDED
    got_a = dict(zip(a_names, _all_gather([bf2(p["ssm_in_w"])] + [_as2d(p[n]) for n in _SMALL_SHARDED],
                                          name="gather_ssm")))
    ffn0_names = ["ssm_out_w", "up0", "down0"]
    rest_names = ["w_q", "w_k", "w_v", "w_o", "up1", "down1"]
    shard = {"up0": bf2(p["ffn_up_w"][0]), "down0": bf2(p["ffn_down_w"][0]), "up1": bf2(p["ffn_up_w"][1]),
             "down1": bf2(p["ffn_down_w"][1]), "w_q": bf2(p["w_q"]), "w_k": bf2(p["w_k"]), "w_v": bf2(p["w_v"]),
             "w_o": bf2(p["w_o"]), "ssm_out_w": bf2(p["ssm_out_w"])}
    h_ffn0 = _push_start([shard[n] for n in ffn0_names], scatter=False, name="gather_ffn0_start")
    handles = {}

    def get_w(group, after):
        if group == "ssm":
            W = {n: p[n] for n in _SMALL_REPL}
            for n in ("ssm_dt_bias", "ssm_a_log", "ssm_d", "attn_norm_w"):
                W[n] = W[n].reshape(-1)
            W["in_w"] = jnp.pad(_cols_to_full(got_a["ssm_in_w"]), ((0, 0), (0, IN_PROJ_PAD - IN_PROJ_DIM)))
            W["ssm_norm_w"] = _tie(got_a["ssm_norm_w"].reshape(D_MODEL), h_ffn0["token"])
            W["ssm_conv_w"] = _cols_to_full(got_a["ssm_conv_w"])
            W["ssm_conv_b"] = got_a["ssm_conv_b"].reshape(CONV_DIM)
            W["ssm_gate_norm_w"] = got_a["ssm_gate_norm_w"].reshape(D_INNER)
            W["ffn_conv_w"] = _cols_to_full(got_a["ffn_conv_w"]).reshape(2, FFN_CONV, 2 * D_FF)
            return W
        if group == "rest_start":
            anchor = after[0, 0]
            first = shard[rest_names[0]] + (jnp.where(jnp.isfinite(anchor), anchor, 0.0) * 0.0).astype(BF16)
            handles["rest"] = _push_start([first] + [shard[n] for n in rest_names[1:]], scatter=False,
                                          name="gather_rest_start")
            return handles["rest"]["token"]
        if group == "ffn0":
            srcs, lands = _push_wait(h_ffn0, after, name="gather_ffn0_wait")
            out, up, down = with_own(srcs, lands, False)
            return dict(ssm_out_w=out.reshape(D_INNER, D_MODEL), up=_cols_to_full(up), down=down.reshape(D_FF, D_MODEL))
        srcs, lands = _push_wait(handles["rest"], after, name="gather_rest_wait")
        g = dict(zip(rest_names, with_own(srcs, lands, False)))
        sq = lambda a: a.reshape(D_MODEL, D_MODEL)
        return dict(w_q=sq(g["w_q"]), w_kv=jnp.concatenate([sq(g["w_k"]), sq(g["w_v"])], axis=1), w_o=sq(g["w_o"]),
                    up=_cols_to_full(g["up1"]), down=g["down1"].reshape(D_FF, D_MODEL))

    pending = []

    def put_g(group, g):
        if group in ("ffn0", "ffn1"):
            keys = [("ffn_up_w", int(group[-1])), ("ffn_down_w", int(group[-1]))]
            blocks = [_full_to_cols(g["up"]), g["down"].reshape(N_DEV, rs, D_MODEL)]
        elif group == "attn":
            keys = [(n, None) for n in ("w_o", "w_q", "w_k", "w_v")]
            blocks = [g[n].reshape(N_DEV, D_MODEL // N_DEV, D_MODEL) for n, _ in keys]
        elif group == "ssm_out":
            keys = [("ssm_out_w", None)]
            blocks = [g["ssm_out_w"].reshape(N_DEV, D_INNER // N_DEV, D_MODEL)]
        else:
            keys = [("ssm_in_w", None)]
            blocks = [_full_to_cols(g["ssm_in_w"])]
        h = _push_start(blocks, scatter=True, name=f"exchange_{group}_start")
        pending.append((group, keys, h))
        return h["token"]

    loss_row, dx, f = _local_step2(x.reshape(T, D_MODEL), loss_target.reshape(T, D_MODEL), get_w, put_g)
    loss = lax.psum(loss_row[0, 0], ("x", "y", "c"))

    small_names = _SMALL_REPL + _SMALL_SHARDED
    small_full = _pack_small([f[n] for n in small_names])
    small_bcast = jnp.broadcast_to(small_full[None], (N_DEV,) + small_full.shape)
    h_small = _push_start([small_bcast], scatter=True, name="exchange_small_start")
    tok = h_small["token"]

    res = {}
    for group, keys, h in pending:
        srcs, lands = _push_wait(h, dx, name=f"exchange_{group}_wait")
        for (n, layer), parts in zip(keys, with_own(srcs, lands, True)):
            sel = (lambda a: a) if layer is None else (lambda a: a[layer])
            w2, m2, v2 = _as2d(sel(p[n])), _as2d(sel(mom[n])), _as2d(sel(var[n]))
            if not res:
                w2 = _tie(w2, tok)
            tr = rs if n == "ffn_down_w" else 256
            res[(n, layer)] = _adamw(parts, w2, m2, v2, name=f"adamw_{n}" + ("" if layer is None else str(layer)), tr=tr)
    srcs, lands = _push_wait(h_small, res[("ssm_in_w", None)][0], name="exchange_small_wait")
    small_parts = with_own(srcs, lands, True)[0]
    out_g, out_d, out_m, out_v = {}, {}, {}, {}
    for n in _BIG:
        if (n, None) in res:
            quad = res[(n, None)]
        else:
            quad = [jnp.stack([res[(n, 0)][k], res[(n, 1)][k]]) for k in range(4)]
        out_g[n], out_d[n], out_m[n], out_v[n] = (t.reshape(p[n].shape) for t in quad)

    zero = jnp.zeros_like(small_full)
    g_small_sum = _adamw(small_parts, zero, zero, zero, name="sum_small_grads", tr=small_full.shape[0])[0]
    g_small = dict(zip(small_names, _unpack_small(g_small_sum, [f[n].shape for n in small_names])))
    for n in _SMALL_SHARDED:
        width = p[n].shape[-1]
        g_small[n] = lax.dynamic_slice_in_dim(g_small[n], me * width, width, axis=g_small[n].ndim - 1)
    sw = _pack_small([p[n] for n in small_names])
    sm = _pack_small([mom[n] for n in small_names])
    sv = _pack_small([var[n] for n in small_names])
    sg = _pack_small([g_small[n] for n in small_names])
    _, d, nm, nv = _adamw(sg[None], sw, sm, sv, name="adamw_small", tr=sw.shape[0])
    shard_shapes = [p[n].shape for n in small_names]
    for n, dd, mm, vv in zip(small_names, _unpack_small(d, shard_shapes), _unpack_small(nm, shard_shapes),
                             _unpack_small(nv, shard_shapes)):
        out_g[n] = g_small[n].reshape(p[n].shape)
        out_d[n], out_m[n], out_v[n] = dd, mm, vv

    return (loss, dx.reshape(x.shape), *[out_g[n] for n in _WEIGHTS], *[out_d[n] for n in _WEIGHTS],
            *[out_m[n] for n in _WEIGHTS], *[out_v[n] for n in _WEIGHTS])
```

```python
import functools
import math

import jax
import jax.numpy as jnp
from jax import lax
from jax.experimental import pallas as pl
from jax.experimental.pallas import tpu as pltpu

F32 = jnp.float32
BF16 = jnp.bfloat16
EPS = 1e-6

D_MODEL = 1024
D_INNER = 2048
SSM_HEADS = 32
SSM_GROUPS = 4
SSM_STATE = 128
SSM_CONV = 4
SSM_CHUNK = 128
GN = SSM_GROUPS * SSM_STATE
CONV_DIM = D_INNER + 2 * GN
IN_PROJ_DIM = D_INNER + CONV_DIM + SSM_HEADS
IN_PROJ_PAD = 5376
SB_HEADS = 16
SB_HEAD_DIM = 64
SB_BLOCK = 128
D_FF = 2816
FFN_CONV = 3
N_DEV = 8

ADAM_LR = 0.001
ADAM_B1 = 0.9
ADAM_B2 = 0.999
ADAM_EPS = 1e-08
ADAM_WD = 0.01
ADAM_STEP = 10

_MESH = pl.DeviceIdType.MESH
_NT = (((1,), (1,)), ((), ()))
_TN = (((0,), (0,)), ((), ()))
_ANY = pl.BlockSpec(memory_space=pl.ANY)


def _cparams(sem, vmem_mb=48):
    return pltpu.CompilerParams(dimension_semantics=sem, vmem_limit_bytes=vmem_mb * 1024 * 1024)


def _sigmoid(x):
    return 1.0 / (1.0 + jnp.exp(-x))


def _softplus(x):
    return jnp.maximum(x, 0.0) + jnp.log(1.0 + jnp.exp(-jnp.abs(x)))


def _rms_fwd(xv, w):
    r = lax.rsqrt(jnp.mean(xv * xv, axis=-1, keepdims=True) + EPS)
    return xv * r * w


def _mm_fwd(x, w, *, name, norm_w=None, residual=None, out_dtype=F32, tm=512, tn=512, halves=False):
    M, K = x.shape
    N = w.shape[1]
    tm, tn = min(tm, M), min(tn, N)
    assert M % tm == 0 and N % tn == 0, (name, M, N, tm, tn)
    if halves:
        nbh = N // 2 // tn
        assert N // 2 % tn == 0
        out_spec = pl.BlockSpec((None, tm, tn), lambda i, j: (lax.div(j, nbh), i, lax.rem(j, nbh)))
        out_shape = jax.ShapeDtypeStruct((2, M, N // 2), out_dtype)
    else:
        out_spec = pl.BlockSpec((tm, tn), lambda i, j: (i, j))
        out_shape = jax.ShapeDtypeStruct((M, N), out_dtype)
    has_norm, has_res = norm_w is not None, residual is not None

    def body(*refs):
        x_ref, w_ref = refs[0], refs[1]
        p = 2
        nw_ref = r_ref = None
        if has_norm:
            nw_ref = refs[p]
            p += 1
        if has_res:
            r_ref = refs[p]
            p += 1
        o_ref, xn_ref = refs[p], refs[p + 1]

        @pl.when(pl.program_id(1) == 0)
        def _():
            xv = x_ref[...].astype(F32)
            if has_norm:
                xv = _rms_fwd(xv, nw_ref[...])
            xn_ref[...] = xv.astype(BF16)

        acc = jnp.dot(xn_ref[...], w_ref[...], preferred_element_type=F32)
        if has_res:
            acc = acc + r_ref[...]
        o_ref[...] = acc.astype(out_dtype)

    in_specs = [pl.BlockSpec((tm, K), lambda i, j: (i, 0)), pl.BlockSpec((K, tn), lambda i, j: (0, j))]
    args = [x, w]
    if has_norm:
        in_specs.append(pl.BlockSpec((1, K), lambda i, j: (0, 0)))
        args.append(norm_w.reshape(1, K))
    if has_res:
        in_specs.append(pl.BlockSpec((tm, tn), lambda i, j: (i, j)))
        args.append(residual)
    return pl.pallas_call(
        body, name=name, grid=(M // tm, N // tn), in_specs=in_specs,
        out_specs=out_spec, out_shape=out_shape,
        scratch_shapes=[pltpu.VMEM((tm, K), BF16)],
        compiler_params=_cparams(("parallel", "arbitrary")))(*args)


def _mm_nt(dy, w, *, name, epi=None, out_dtype=F32, tm=512, tn=512, tk=512):
    halves = dy.ndim == 3
    M, K = (dy.shape[1], 2 * dy.shape[2]) if halves else dy.shape
    N = w.shape[0]
    tm, tk = min(tm, M), min(tk, K)
    tn = N if epi is not None else min(tn, N)
    assert M % tm == 0 and N % tn == 0 and K % tk == 0, (name, M, N, K, tm, tn, tk)
    nk = K // tk
    has_epi = epi is not None

    def body(*refs):
        if has_epi:
            dy_ref, w_ref, h_ref, nw_ref, r_ref, o_ref, dnw_ref, acc_ref = refs
        else:
            dy_ref, w_ref, o_ref, acc_ref = refs
        i = pl.program_id(0)
        k = pl.program_id(2)

        @pl.when(k == 0)
        def _():
            acc_ref[...] = jnp.zeros_like(acc_ref)

        acc_ref[...] += lax.dot_general(dy_ref[...].astype(BF16), w_ref[...], _NT, preferred_element_type=F32)

        @pl.when(k == nk - 1)
        def _():
            du = acc_ref[...]
            if has_epi:
                hv = h_ref[...]
                r = lax.rsqrt(jnp.mean(hv * hv, axis=-1, keepdims=True) + EPS)
                xhat = hv * r
                dxh = du * nw_ref[...]
                dx = r * (dxh - xhat * jnp.mean(dxh * xhat, axis=-1, keepdims=True))
                o_ref[...] = (r_ref[...] + dx).astype(out_dtype)
                contrib = jnp.sum(du * xhat, axis=0, keepdims=True)

                @pl.when(i == 0)
                def _():
                    dnw_ref[...] = contrib

                @pl.when(i > 0)
                def _():
                    dnw_ref[...] += contrib
            else:
                o_ref[...] = du.astype(out_dtype)

    if halves:
        nkh = K // 2 // tk
        assert K // 2 % tk == 0
        dy_spec = pl.BlockSpec((None, tm, tk), lambda i, j, k: (lax.div(k, nkh), i, lax.rem(k, nkh)))
    else:
        dy_spec = pl.BlockSpec((tm, tk), lambda i, j, k: (i, k))
    in_specs = [dy_spec, pl.BlockSpec((tn, tk), lambda i, j, k: (j, k))]
    args = [dy, w]
    out_specs = [pl.BlockSpec((tm, tn), lambda i, j, k: (i, j))]
    out_shape = [jax.ShapeDtypeStruct((M, N), out_dtype)]
    if has_epi:
        h, nw, res = epi
        in_specs += [pl.BlockSpec((tm, N), lambda i, j, k: (i, 0)), pl.BlockSpec((1, N), lambda i, j, k: (0, 0)),
                     pl.BlockSpec((tm, N), lambda i, j, k: (i, 0))]
        args += [h, nw.reshape(1, N), res]
        out_specs.append(pl.BlockSpec((1, N), lambda i, j, k: (0, 0)))
        out_shape.append(jax.ShapeDtypeStruct((1, N), F32))
    outs = pl.pallas_call(
        body, name=name, grid=(M // tm, N // tn, nk), in_specs=in_specs, out_specs=out_specs, out_shape=out_shape,
        scratch_shapes=[pltpu.VMEM((tm, tn), F32)],
        compiler_params=_cparams(("arbitrary", "arbitrary", "arbitrary")))(*args)
    return (outs[0], outs[1]) if has_epi else outs[0]


def _mm_tn(x, dy, *, name, norm_w=None, out_dtype=BF16, tk1=1024, tn=512, tt=512):
    T, K1 = x.shape
    halves = dy.ndim == 3
    N = 2 * dy.shape[2] if halves else dy.shape[1]
    tk1, tn, tt = min(tk1, K1), min(tn, N), min(tt, T)
    has_norm = norm_w is not None
    assert K1 % tk1 == 0 and N % tn == 0 and T % tt == 0, (name, K1, N, T, tk1, tn, tt)
    assert not has_norm or tk1 == K1
    nt = T // tt

    def body(*refs):
        if has_norm:
            x_ref, dy_ref, nw_ref, o_ref, acc_ref = refs
        else:
            x_ref, dy_ref, o_ref, acc_ref = refs
        t = pl.program_id(2)

        @pl.when(t == 0)
        def _():
            acc_ref[...] = jnp.zeros_like(acc_ref)

        xv = x_ref[...]
        if has_norm:
            xv = _rms_fwd(xv.astype(F32), nw_ref[...])
        acc_ref[...] += lax.dot_general(xv.astype(BF16), dy_ref[...].astype(BF16), _TN, preferred_element_type=F32)

        @pl.when(t == nt - 1)
        def _():
            o_ref[...] = acc_ref[...].astype(out_dtype)

    if halves:
        nbh = N // 2 // tn
        assert N // 2 % tn == 0
        dy_spec = pl.BlockSpec((None, tt, tn), lambda a, b, t: (lax.div(b, nbh), t, lax.rem(b, nbh)))
    else:
        dy_spec = pl.BlockSpec((tt, tn), lambda a, b, t: (t, b))
    in_specs = [pl.BlockSpec((tt, tk1), lambda a, b, t: (t, a)), dy_spec]
    args = [x, dy]
    if has_norm:
        in_specs.append(pl.BlockSpec((1, K1), lambda a, b, t: (0, 0)))
        args.append(norm_w.reshape(1, K1))
    return pl.pallas_call(
        body, name=name, grid=(K1 // tk1, N // tn, nt), in_specs=in_specs,
        out_specs=pl.BlockSpec((tk1, tn), lambda a, b, t: (a, b)),
        out_shape=jax.ShapeDtypeStruct((K1, N), out_dtype),
        scratch_shapes=[pltpu.VMEM((tk1, tn), F32)],
        compiler_params=_cparams(("parallel", "parallel", "arbitrary")))(*args)


def _shift_down(xb, prev8, j):
    main = pltpu.roll(xb, j, 0)
    head = pltpu.roll(xb[0:8], j, 0)
    ph = pltpu.roll(prev8, j, 0)
    row8 = lax.broadcasted_iota(jnp.int32, head.shape, 0)
    head = jnp.where(row8 < j, ph, head)
    return jnp.concatenate([head, main[8:]], axis=0)


def _shift_up(xb, next8, j):
    tt = xb.shape[0]
    main = pltpu.roll(xb, tt - j, 0)
    tail = pltpu.roll(xb[tt - 8:tt], 8 - j, 0)
    nh = pltpu.roll(next8, 8 - j, 0)
    row8 = lax.broadcasted_iota(jnp.int32, tail.shape, 0)
    tail = jnp.where(row8 + j >= 8, nh, tail)
    return jnp.concatenate([main[:tt - 8], tail], axis=0)


def _conv_hid(xb, prev8, w, b_row, K):
    out = b_row
    shifted = []
    for j in range(K):
        sh = K - 1 - j
        xs = xb if sh == 0 else _shift_down(xb, prev8, sh)
        shifted.append(xs)
        out = out + xs * w[j:j + 1, :]
    return out, shifted


def _prev_idx(i, nb8):
    return jnp.maximum(i * nb8 - 1, 0)


def _ssm_conv_fwd(zx, w, b, *, name, tt=512, tc=512):
    T = zx.shape[0]
    tt = min(tt, T)
    C, K = CONV_DIM, SSM_CONV
    cb0, nb8 = D_INNER // tc, tt // 8

    def body(x_ref, p_ref, w_ref, b_ref, o_ref):
        first = (pl.program_id(1) > 0).astype(F32)
        hid, _ = _conv_hid(x_ref[...], p_ref[...] * first, w_ref[...], b_ref[...], K)
        o_ref[...] = hid * _sigmoid(hid)

    return pl.pallas_call(
        body, name=name, grid=(C // tc, T // tt),
        in_specs=[pl.BlockSpec((tt, tc), lambda c, i: (i, c + cb0)),
                  pl.BlockSpec((8, tc), lambda c, i: (_prev_idx(i, nb8), c + cb0)),
                  pl.BlockSpec((K, tc), lambda c, i: (0, c)), pl.BlockSpec((1, tc), lambda c, i: (0, c))],
        out_specs=pl.BlockSpec((tt, tc), lambda c, i: (i, c)),
        out_shape=jax.ShapeDtypeStruct((T, C), F32),
        compiler_params=_cparams(("parallel", "parallel")))(zx, zx, w, b)


def _ssm_conv_bwd_pre(zx, w, b, dout, *, name, tt=512, tc=512):
    T = zx.shape[0]
    tt = min(tt, T)
    C, K = CONV_DIM, SSM_CONV
    cb0, nb8 = D_INNER // tc, tt // 8

    def body(x_ref, p_ref, w_ref, b_ref, d_ref, dh_ref, dw_ref, db_ref):
        t = pl.program_id(1)
        first = (t > 0).astype(F32)
        hid, shifted = _conv_hid(x_ref[...], p_ref[...] * first, w_ref[...], b_ref[...], K)
        sg = _sigmoid(hid)
        dh = d_ref[...] * (sg * (1.0 + hid * (1.0 - sg)))
        dh_ref[...] = dh

        @pl.when(t == 0)
        def _():
            dw_ref[...] = jnp.zeros_like(dw_ref)
            db_ref[...] = jnp.zeros_like(db_ref)

        db_ref[...] += jnp.sum(dh, axis=0, keepdims=True)
        for j in range(K):
            dw_ref[j:j + 1, :] += jnp.sum(dh * shifted[j], axis=0, keepdims=True)

    return pl.pallas_call(
        body, name=name, grid=(C // tc, T // tt),
        in_specs=[pl.BlockSpec((tt, tc), lambda c, i: (i, c + cb0)),
                  pl.BlockSpec((8, tc), lambda c, i: (_prev_idx(i, nb8), c + cb0)),
                  pl.BlockSpec((K, tc), lambda c, i: (0, c)), pl.BlockSpec((1, tc), lambda c, i: (0, c)),
                  pl.BlockSpec((tt, tc), lambda c, i: (i, c))],
        out_specs=[pl.BlockSpec((tt, tc), lambda c, i: (i, c)), pl.BlockSpec((K, tc), lambda c, i: (0, c)),
                   pl.BlockSpec((1, tc), lambda c, i: (0, c))],
        out_shape=[jax.ShapeDtypeStruct((T, C), F32), jax.ShapeDtypeStruct((K, C), F32),
                   jax.ShapeDtypeStruct((1, C), F32)],
        compiler_params=_cparams(("parallel", "arbitrary")))(zx, zx, w, b, dout)


def _conv_bwd_in(dh, w, *, name, K, tt=512, tc=512, out_dtype=BF16):
    T, C = dh.shape
    tt = min(tt, T)
    nb8, nT = tt // 8, T // tt
    last8 = T // 8 - 1

    def body(d_ref, n_ref, w_ref, o_ref):
        notlast = (pl.program_id(1) < nT - 1).astype(F32)
        d = d_ref[...]
        nxt = n_ref[...] * notlast
        w_ = w_ref[...]
        acc = d * w_[K - 1:K, :]
        for sh in range(1, K):
            acc = acc + _shift_up(d, nxt, sh) * w_[K - 1 - sh:K - sh, :]
        o_ref[...] = acc.astype(out_dtype)

    return pl.pallas_call(
        body, name=name, grid=(C // tc, nT),
        in_specs=[pl.BlockSpec((tt, tc), lambda c, i: (i, c)),
                  pl.BlockSpec((8, tc), lambda c, i: (jnp.minimum((i + 1) * nb8, last8), c)),
                  pl.BlockSpec((K, tc), lambda c, i: (0, c))],
        out_specs=pl.BlockSpec((tt, tc), lambda c, i: (i, c)),
        out_shape=jax.ShapeDtypeStruct((T, C), out_dtype),
        compiler_params=_cparams(("parallel", "parallel")))(dh, dh, w)


def _ffn_conv_fwd(a, w, b, *, name, tt=256, tc=1408):
    T = a.shape[0]
    tt = min(tt, T)
    K, nbh, nb8 = FFN_CONV, D_FF // tc, tt // 8

    def body(ag_ref, pg_ref, av_ref, pv_ref, wg_ref, wv_ref, bg_ref, bv_ref, o_ref):
        first = (pl.program_id(1) > 0).astype(F32)
        hg, _ = _conv_hid(ag_ref[...], pg_ref[...] * first, wg_ref[...], bg_ref[...], K)
        hv, _ = _conv_hid(av_ref[...], pv_ref[...] * first, wv_ref[...], bv_ref[...], K)
        o_ref[...] = (hg * _sigmoid(hg) * hv).astype(BF16)

    return pl.pallas_call(
        body, name=name, grid=(nbh, T // tt),
        in_specs=[pl.BlockSpec((tt, tc), lambda c, i: (i, c)),
                  pl.BlockSpec((8, tc), lambda c, i: (_prev_idx(i, nb8), c)),
                  pl.BlockSpec((tt, tc), lambda c, i: (i, c + nbh)),
                  pl.BlockSpec((8, tc), lambda c, i: (_prev_idx(i, nb8), c + nbh)),
                  pl.BlockSpec((K, tc), lambda c, i: (0, c)), pl.BlockSpec((K, tc), lambda c, i: (0, c + nbh)),
                  pl.BlockSpec((1, tc), lambda c, i: (0, c)), pl.BlockSpec((1, tc), lambda c, i: (0, c + nbh))],
        out_specs=pl.BlockSpec((tt, tc), lambda c, i: (i, c)),
        out_shape=jax.ShapeDtypeStruct((T, D_FF), BF16),
        compiler_params=_cparams(("parallel", "parallel")))(a, a, a, a, w, w, b, b)


def _ffn_conv_bwd_pre(a, w, b, dp, *, name, tt=256, tc=1408):
    T = a.shape[0]
    tt = min(tt, T)
    K, nbh, nb8 = FFN_CONV, D_FF // tc, tt // 8

    def body(ao_ref, po_ref, ag_ref, pg_ref, av_ref, pv_ref, wg_ref, wv_ref, bg_ref, bv_ref, dp_ref,
             dh_ref, dw_ref, db_ref):
        j = pl.program_id(0)
        t = pl.program_id(1)
        first = (t > 0).astype(F32)
        hg, _ = _conv_hid(ag_ref[...], pg_ref[...] * first, wg_ref[...], bg_ref[...], K)
        hv, _ = _conv_hid(av_ref[...], pv_ref[...] * first, wv_ref[...], bv_ref[...], K)
        sg = _sigmoid(hg)
        d = dp_ref[...].astype(F32)
        is_gate = (j < nbh).astype(F32)
        dh = d * (is_gate * (hv * (sg * (1.0 + hg * (1.0 - sg)))) + (1.0 - is_gate) * (hg * sg))
        dh_ref[...] = dh
        xo = ao_ref[...]
        po = po_ref[...] * first

        @pl.when(t == 0)
        def _():
            dw_ref[...] = jnp.zeros_like(dw_ref)
            db_ref[...] = jnp.zeros_like(db_ref)

        db_ref[...] += jnp.sum(dh, axis=0, keepdims=True)
        for jj in range(K):
            sh = K - 1 - jj
            xs = xo if sh == 0 else _shift_down(xo, po, sh)
            dw_ref[jj:jj + 1, :] += jnp.sum(dh * xs, axis=0, keepdims=True)

    def gi(c):
        return lax.rem(c, nbh)

    return pl.pallas_call(
        body, name=name, grid=(2 * nbh, T // tt),
        in_specs=[pl.BlockSpec((tt, tc), lambda c, i: (i, c)),
                  pl.BlockSpec((8, tc), lambda c, i: (_prev_idx(i, nb8), c)),
                  pl.BlockSpec((tt, tc), lambda c, i: (i, gi(c))),
                  pl.BlockSpec((8, tc), lambda c, i: (_prev_idx(i, nb8), gi(c))),
                  pl.BlockSpec((tt, tc), lambda c, i: (i, gi(c) + nbh)),
                  pl.BlockSpec((8, tc), lambda c, i: (_prev_idx(i, nb8), gi(c) + nbh)),
                  pl.BlockSpec((K, tc), lambda c, i: (0, gi(c))), pl.BlockSpec((K, tc), lambda c, i: (0, gi(c) + nbh)),
                  pl.BlockSpec((1, tc), lambda c, i: (0, gi(c))), pl.BlockSpec((1, tc), lambda c, i: (0, gi(c) + nbh)),
                  pl.BlockSpec((tt, tc), lambda c, i: (i, gi(c)))],
        out_specs=[pl.BlockSpec((tt, tc), lambda c, i: (i, c)), pl.BlockSpec((K, tc), lambda c, i: (0, c)),
                   pl.BlockSpec((1, tc), lambda c, i: (0, c))],
        out_shape=[jax.ShapeDtypeStruct((T, 2 * D_FF), F32), jax.ShapeDtypeStruct((K, 2 * D_FF), F32),
                   jax.ShapeDtypeStruct((1, 2 * D_FF), F32)],
        compiler_params=_cparams(("parallel", "arbitrary")))(a, a, a, a, a, a, w, w, b, b, dp)


def _ffn_conv_fwd3(a3, w, b, *, name, tt=256, tc=1408):
    T = a3.shape[1]
    tt = min(tt, T)
    K, nbh, n16 = FFN_CONV, D_FF // tc, tt // 16

    def body(a_ref, p_ref, wg_ref, wv_ref, bg_ref, bv_ref, o_ref):
        first = (pl.program_id(1) > 0).astype(F32)
        a = a_ref[...].astype(F32)
        prev = p_ref[...].astype(F32)[:, 8:16, :] * first
        hg, _ = _conv_hid(a[0], prev[0], wg_ref[...], bg_ref[...], K)
        hv, _ = _conv_hid(a[1], prev[1], wv_ref[...], bv_ref[...], K)
        o_ref[...] = (hg * _sigmoid(hg) * hv).astype(BF16)

    return pl.pallas_call(
        body, name=name, grid=(nbh, T // tt),
        in_specs=[pl.BlockSpec((2, tt, tc), lambda c, i: (0, i, c)),
                  pl.BlockSpec((2, 16, tc), lambda c, i: (0, _prev_idx(i, n16), c)),
                  pl.BlockSpec((K, tc), lambda c, i: (0, c)), pl.BlockSpec((K, tc), lambda c, i: (0, c + nbh)),
                  pl.BlockSpec((1, tc), lambda c, i: (0, c)), pl.BlockSpec((1, tc), lambda c, i: (0, c + nbh))],
        out_specs=pl.BlockSpec((tt, tc), lambda c, i: (i, c)),
        out_shape=jax.ShapeDtypeStruct((T, D_FF), BF16),
        compiler_params=_cparams(("parallel", "parallel")))(a3, a3, w, w, b, b)


def _ffn_conv_bwd3(a3, w, b, dp, *, name, tt=256, tc=1408):
    T = a3.shape[1]
    tt = min(tt, T)
    K, nbh, n16 = FFN_CONV, D_FF // tc, tt // 16

    def body(a_ref, p_ref, wg_ref, wv_ref, bg_ref, bv_ref, dp_ref, dh_ref, dw_ref, db_ref):
        t = pl.program_id(1)
        first = (t > 0).astype(F32)
        a = a_ref[...].astype(F32)
        prev = p_ref[...].astype(F32)[:, 8:16, :] * first
        hg, sh_g = _conv_hid(a[0], prev[0], wg_ref[...], bg_ref[...], K)
        hv, sh_v = _conv_hid(a[1], prev[1], wv_ref[...], bv_ref[...], K)
        sg = _sigmoid(hg)
        d = dp_ref[...].astype(F32)
        dhg = d * hv * (sg * (1.0 + hg * (1.0 - sg)))
        dhv = d * (hg * sg)
        dh_ref[0] = dhg.astype(BF16)
        dh_ref[1] = dhv.astype(BF16)

        @pl.when(t == 0)
        def _():
            dw_ref[...] = jnp.zeros_like(dw_ref)
            db_ref[...] = jnp.zeros_like(db_ref)

        db_ref[0] += jnp.sum(dhg, axis=0, keepdims=True)
        db_ref[1] += jnp.sum(dhv, axis=0, keepdims=True)
        for j in range(K):
            dw_ref[0, j:j + 1, :] += jnp.sum(dhg * sh_g[j], axis=0, keepdims=True)
            dw_ref[1, j:j + 1, :] += jnp.sum(dhv * sh_v[j], axis=0, keepdims=True)

    return pl.pallas_call(
        body, name=name, grid=(nbh, T // tt),
        in_specs=[pl.BlockSpec((2, tt, tc), lambda c, i: (0, i, c)),
                  pl.BlockSpec((2, 16, tc), lambda c, i: (0, _prev_idx(i, n16), c)),
                  pl.BlockSpec((K, tc), lambda c, i: (0, c)), pl.BlockSpec((K, tc), lambda c, i: (0, c + nbh)),
                  pl.BlockSpec((1, tc), lambda c, i: (0, c)), pl.BlockSpec((1, tc), lambda c, i: (0, c + nbh)),
                  pl.BlockSpec((tt, tc), lambda c, i: (i, c))],
        out_specs=[pl.BlockSpec((2, tt, tc), lambda c, i: (0, i, c)), pl.BlockSpec((2, K, tc), lambda c, i: (0, 0, c)),
                   pl.BlockSpec((2, 1, tc), lambda c, i: (0, 0, c))],
        out_shape=[jax.ShapeDtypeStruct((2, T, D_FF), BF16), jax.ShapeDtypeStruct((2, K, D_FF), F32),
                   jax.ShapeDtypeStruct((2, 1, D_FF), F32)],
        compiler_params=_cparams(("parallel", "arbitrary")))(a3, a3, w, w, b, b, dp)


def _conv_bwd_in3(dh3, w, *, name, K, tt=256, tc=1408):
    H, T, C = dh3.shape
    tt = min(tt, T)
    nb, n16, nT = C // tc, tt // 16, T // tt
    last16 = T // 16 - 1

    def body(d_ref, n_ref, w_ref, o_ref):
        notlast = (pl.program_id(2) < nT - 1).astype(F32)
        d = d_ref[...].astype(F32)
        nxt = n_ref[...].astype(F32)[0:8, :] * notlast
        w_ = w_ref[...]
        acc = d * w_[K - 1:K, :]
        for sh in range(1, K):
            acc = acc + _shift_up(d, nxt, sh) * w_[K - 1 - sh:K - sh, :]
        o_ref[...] = acc.astype(BF16)

    return pl.pallas_call(
        body, name=name, grid=(H, nb, nT),
        in_specs=[pl.BlockSpec((None, tt, tc), lambda h, c, i: (h, i, c)),
                  pl.BlockSpec((None, 16, tc), lambda h, c, i: (h, jnp.minimum((i + 1) * n16, last16), c)),
                  pl.BlockSpec((K, tc), lambda h, c, i: (0, h * nb + c))],
        out_specs=pl.BlockSpec((None, tt, tc), lambda h, c, i: (h, i, c)),
        out_shape=jax.ShapeDtypeStruct((H, T, C), BF16),
        compiler_params=_cparams(("parallel", "parallel", "parallel")))(dh3, dh3, w)


def _cumsum_rows(x):
    L = x.shape[0]
    row = lax.broadcasted_iota(jnp.int32, x.shape, 0)
    k = 1
    while k < L:
        x = x + jnp.where(row >= k, pltpu.roll(x, k, 0), 0.0)
        k *= 2
    return x


def _rcumsum_rows(x):
    L = x.shape[0]
    row = lax.broadcasted_iota(jnp.int32, x.shape, 0)
    k = 1
    while k < L:
        x = x + jnp.where(row < L - k, pltpu.roll(x, L - k, 0), 0.0)
        k *= 2
    return x


def _split_terms(m, n):
    terms, rest = [], m
    for _ in range(n):
        t = rest.astype(BF16)
        terms.append(t)
        rest = rest - t.astype(F32)
    return jnp.concatenate(terms, axis=1)


def _select_dot(m, n_terms, n_out, cond):
    K = m.shape[1]
    k = lax.broadcasted_iota(jnp.int32, (K, n_out), 0)
    j = lax.broadcasted_iota(jnp.int32, (K, n_out), 1)
    sel = cond(k, j).astype(BF16)
    return jnp.dot(_split_terms(m, n_terms), jnp.concatenate([sel] * n_terms, axis=0), preferred_element_type=F32)


def _rowsum_mxu(m):
    return _select_dot(m, 2, 128, lambda k, j: k >= 0)


def _lane_block_sums(m, width):
    shift = width.bit_length() - 1
    return _select_dot(m, 2, 128, lambda k, j: j == jnp.right_shift(k, shift))


def _heads_to_pairs(m):
    return _select_dot(m, 3, 512, lambda k, j: k == jnp.right_shift(j, 6))


def _ssd_common(dt_ref, par_ref):
    par = par_ref[...]
    raw = dt_ref[...] + par[0:1, :]
    dt = _softplus(raw)
    a = -jnp.exp(par[1:2, :])
    cs = _cumsum_rows(dt * a)
    L = cs.shape[0]
    cs_last = cs[L - 1:L, :]
    return raw, dt, a, par[2:3, :], cs, cs.T, jnp.exp(cs), jnp.exp(cs_last - cs), jnp.exp(cs_last)


def _ssd_specs(nc, rev):
    L = SSM_CHUNK

    def ci(c):
        return nc - 1 - c if rev else c

    return [pl.BlockSpec((L, D_INNER), lambda c: (ci(c), 0)),
            pl.BlockSpec((L, GN), lambda c: (ci(c), D_INNER // GN)),
            pl.BlockSpec((L, GN), lambda c: (ci(c), D_INNER // GN + 1)),
            pl.BlockSpec((SSM_GROUPS, L, 128), lambda c: (0, ci(c), 0)),
            pl.BlockSpec((SSM_GROUPS, 8, 128), lambda c: (0, 0, 0)),
            pl.BlockSpec((L, D_INNER), lambda c: (ci(c), 0)),
            pl.BlockSpec((1, D_INNER), lambda c: (0, 0))], ci


def _round_robin(gens):
    live = list(gens)
    while live:
        nxt = []
        for gen in live:
            try:
                next(gen)
                nxt.append(gen)
            except StopIteration:
                pass
        live = nxt


def _group_views(g, wide, narrow, lead):
    return ([r.at[:, g * 512:(g + 1) * 512] for r in wide], [r.at[:, g * 128:(g + 1) * 128] for r in narrow],
            [r.at[g] for r in lead])


def _ssd_fwd(xbc_c, zx, dtg, par, gnw, *, name):
    T = xbc_c.shape[0]
    L = SSM_CHUNK
    nc = T // L
    in_specs, ci = _ssd_specs(nc, False)

    def body(xs_ref, b_ref, c_ref, dt_ref, par_ref, z_ref, gnw_ref, y_ref, yn_ref, st_ref, h_ref):
        @pl.when(pl.program_id(0) == 0)
        def _():
            h_ref[...] = jnp.zeros_like(h_ref)

        gens = []
        for g in range(SSM_GROUPS):
            (xs, z, gw, y, yn), (b, c), (dt, pr, st, h) = _group_views(
                g, [xs_ref, z_ref, gnw_ref, y_ref, yn_ref], [b_ref, c_ref], [dt_ref, par_ref, st_ref, h_ref])
            gens.append(group(xs, b, c, dt, pr, z, gw, y, yn, st, h))
        _round_robin(gens)

    def group(xs_ref, b_ref, c_ref, dt_ref, par_ref, z_ref, gnw_ref, y_ref, yn_ref, st_ref, h_ref):
        _, dt, _, dsk, cs, csT, ecs, eend, dec = _ssd_common(dt_ref, par_ref)
        Bb = b_ref[...].astype(BF16)
        Cb = c_ref[...].astype(BF16)
        G = lax.dot_general(Cb, Bb, _NT, preferred_element_type=F32)
        row = lax.broadcasted_iota(jnp.int32, (L, L), 0)
        col = lax.broadcasted_iota(jnp.int32, (L, L), 1)
        tril = col <= row
        lo = lax.broadcasted_iota(jnp.int32, (L, 128), 1) < 64
        lo1 = lax.broadcasted_iota(jnp.int32, (1, 128), 1) < 64
        dt_x, ecs_x, eend_x = (_heads_to_pairs(m) for m in (dt, ecs, eend))
        for pp in range(4):
            hA, hB = 2 * pp, 2 * pp + 1
            lanes = slice(pp * 128, (pp + 1) * 128)

            def sel1(m):
                return jnp.where(lo1, m[:, hA:hA + 1], m[:, hB:hB + 1])

            X = xs_ref[:, lanes]
            xd = X * dt_x[:, lanes]
            xdb = xd.astype(BF16)
            ys = []
            for h in (hA, hB):
                Lm = jnp.where(tril, jnp.exp(jnp.minimum(cs[:, h:h + 1] - csT[h:h + 1, :], 0.0)), 0.0)
                ys.append(jnp.dot((G * Lm).astype(BF16), xdb, preferred_element_type=F32))
                yield
            Hp = h_ref[pp]
            st_ref[pp] = Hp
            yoff = jnp.dot(Cb, Hp.astype(BF16), preferred_element_type=F32) * ecs_x[:, lanes]
            y_ref[:, lanes] = jnp.where(lo, ys[0], ys[1]) + yoff + sel1(dsk) * X
            S = lax.dot_general(Bb, (xd * eend_x[:, lanes]).astype(BF16), _TN, preferred_element_type=F32)
            h_ref[pp] = Hp * sel1(dec) + S
            yield
        zv = z_ref[...]
        yg = y_ref[...] * (zv * _sigmoid(zv))
        r = jnp.tile(lax.rsqrt(_rowsum_mxu(yg * yg) * (1.0 / 512) + EPS), (1, 4))
        yn_ref[...] = (yg * r * gnw_ref[...]).astype(BF16)

    return pl.pallas_call(
        body, name=name, grid=(nc,), in_specs=in_specs,
        out_specs=[pl.BlockSpec((L, D_INNER), lambda c: (c, 0)), pl.BlockSpec((L, D_INNER), lambda c: (c, 0)),
                   pl.BlockSpec((SSM_GROUPS, None, 4, 128, 128), lambda c: (0, c, 0, 0, 0))],
        out_shape=[jax.ShapeDtypeStruct((T, D_INNER), F32), jax.ShapeDtypeStruct((T, D_INNER), BF16),
                   jax.ShapeDtypeStruct((SSM_GROUPS, nc, 4, 128, 128), F32)],
        scratch_shapes=[pltpu.VMEM((SSM_GROUPS, 4, 128, 128), F32)],
        compiler_params=_cparams(("arbitrary",)))(xbc_c, xbc_c, xbc_c, dtg, par, zx, gnw)


def _ssd_bwd(xbc_c, zx, dtg, par, gnw, y, st, dyn, *, name):
    T = xbc_c.shape[0]
    L = SSM_CHUNK
    nc = T // L
    in_specs, ci = _ssd_specs(nc, True)
    in_specs += [pl.BlockSpec((L, D_INNER), lambda c: (ci(c), 0)),
                 pl.BlockSpec((SSM_GROUPS, None, 4, 128, 128), lambda c: (0, ci(c), 0, 0, 0)),
                 pl.BlockSpec((L, D_INNER), lambda c: (ci(c), 0))]

    def body(xs_ref, b_ref, c_ref, dt_ref, par_ref, z_ref, gnw_ref, y_ref, st_ref, dyn_ref,
             dxbc_ref, dz_ref, ddt_ref, dgnw_ref, dpar_ref, dh_ref):
        @pl.when(pl.program_id(0) == 0)
        def _():
            dh_ref[...] = jnp.zeros_like(dh_ref)
            dgnw_ref[...] = jnp.zeros_like(dgnw_ref)
            dpar_ref[...] = jnp.zeros_like(dpar_ref)

        dxs_ref = dxbc_ref.at[:, 0:D_INNER]
        db_ref = dxbc_ref.at[:, D_INNER:D_INNER + GN]
        dc_ref = dxbc_ref.at[:, D_INNER + GN:CONV_DIM]

        gens = []
        for g in range(SSM_GROUPS):
            (xs, z, gw, y, dyn, dxs, dz, dgw), (b, c, db, dc), (dt, pr, st, ddt, dpr, dh) = _group_views(
                g, [xs_ref, z_ref, gnw_ref, y_ref, dyn_ref, dxs_ref, dz_ref, dgnw_ref], [b_ref, c_ref, db_ref, dc_ref],
                [dt_ref, par_ref, st_ref, ddt_ref, dpar_ref, dh_ref])
            gens.append(group(xs, b, c, dt, pr, z, gw, y, st, dyn, dxs, db, dc, dz, ddt, dgw, dpr, dh))
        _round_robin(gens)

    def group(xs_ref, b_ref, c_ref, dt_ref, par_ref, z_ref, gnw_ref, y_ref, st_ref, dyn_ref,
              dxs_ref, db_ref, dc_ref, dz_ref, ddt_ref, dgnw_ref, dpar_ref, dh_ref):
        yv = y_ref[...]
        zv = z_ref[...]
        sg = _sigmoid(zv)
        sz = zv * sg
        yg = yv * sz
        r = jnp.tile(lax.rsqrt(_rowsum_mxu(yg * yg) * (1.0 / 512) + EPS), (1, 4))
        yh = yg * r
        dyn = dyn_ref[...].astype(F32)
        dgnw_ref[...] += jnp.sum(dyn * yh, axis=0, keepdims=True)
        dyh = dyn * gnw_ref[...]
        dyg = r * (dyh - yh * jnp.tile(_rowsum_mxu(dyh * yh) * (1.0 / 512), (1, 4)))
        dY_all = dyg * sz
        dz_ref[...] = (dyg * yv * (sg * (1.0 + zv * (1.0 - sg)))).astype(dz_ref.dtype)

        yield
        raw, dt, a, dsk, cs, csT, ecs, eend, dec = _ssd_common(dt_ref, par_ref)
        Bb = b_ref[...].astype(BF16)
        Cb = c_ref[...].astype(BF16)
        G = lax.dot_general(Cb, Bb, _NT, preferred_element_type=F32)
        row = lax.broadcasted_iota(jnp.int32, (L, L), 0)
        col = lax.broadcasted_iota(jnp.int32, (L, L), 1)
        tril = col <= row
        lo = lax.broadcasted_iota(jnp.int32, (L, 128), 1) < 64
        lane1 = lax.broadcasted_iota(jnp.int32, (1, 128), 1)
        lo1 = lane1 < 64
        rowl = lax.broadcasted_iota(jnp.int32, (L, 128), 0)
        dt_x, ecs_x, eend_x = (_heads_to_pairs(m) for m in (dt, ecs, eend))
        dG = jnp.zeros((L, L), F32)
        dB = jnp.zeros((L, SSM_STATE), F32)
        dC = jnp.zeros((L, SSM_STATE), F32)
        dcs_t = jnp.zeros((L, L), F32)
        tails = jnp.zeros((1, 128), F32)
        dD_row = jnp.zeros((1, 128), F32)
        v_parts, prod_parts = [], []

        def tot(m):
            return jnp.sum(jnp.sum(m, axis=0, keepdims=True), axis=1, keepdims=True)

        for pp in range(4):
            hA, hB = 2 * pp, 2 * pp + 1
            lanes = slice(pp * 128, (pp + 1) * 128)

            def sel1(m):
                return jnp.where(lo1, m[:, hA:hA + 1], m[:, hB:hB + 1])

            X = xs_ref[:, lanes]
            dY = dY_all[:, lanes]
            dtsel = dt_x[:, lanes]
            xd = X * dtsel
            xdb = xd.astype(BF16)
            dYb = dY.astype(BF16)
            Hp = st_ref[pp]
            Hb = Hp.astype(BF16)
            dHn = dh_ref[pp]
            dHb = dHn.astype(BF16)
            ecs_sel = ecs_x[:, lanes]
            eend_sel = eend_x[:, lanes]
            dxd_state = jnp.dot(Bb, dHb, preferred_element_type=F32) * eend_sel
            yoff = jnp.dot(Cb, Hb, preferred_element_type=F32) * ecs_sel
            dYe = (dY * ecs_sel).astype(BF16)
            dC = dC + lax.dot_general(dYe, Hb, _NT, preferred_element_type=F32)
            dB = dB + lax.dot_general((xd * eend_sel).astype(BF16), dHb, _NT, preferred_element_type=F32)
            dh_ref[pp] = dHn * sel1(dec) + lax.dot_general(Cb, dYe, _TN, preferred_element_type=F32)
            q = xd * dxd_state
            dyq = dY * yoff - q
            qcol = jnp.sum(q, axis=0, keepdims=True)
            hcol = jnp.sum(dHn * Hp, axis=0, keepdims=True)
            dxd_diag = []
            for h, msk, msk1 in ((hA, lo, lo1), (hB, jnp.logical_not(lo), jnp.logical_not(lo1))):
                Lm = jnp.where(tril, jnp.exp(jnp.minimum(cs[:, h:h + 1] - csT[h:h + 1, :], 0.0)), 0.0)
                M = G * Lm
                dxd_diag.append(lax.dot_general(M.astype(BF16), dYb, _TN, preferred_element_type=F32))
                dM = lax.dot_general(jnp.where(msk, dY, 0.0).astype(BF16), xdb, _NT, preferred_element_type=F32)
                dG = dG + dM * Lm
                W = dM * M
                dcs_t = dcs_t + jnp.where(row == h, jnp.sum(W, axis=0, keepdims=True), 0.0)
                v_parts.append(W + jnp.where(msk, dyq, 0.0))
                tail = (jnp.sum(jnp.where(msk1, qcol, 0.0), axis=1, keepdims=True)
                        + dec[:, h:h + 1] * jnp.sum(jnp.where(msk1, hcol, 0.0), axis=1, keepdims=True))
                tails = tails + jnp.where(lane1 == h, tail, 0.0)
                yield
            dxd = jnp.where(lo, dxd_diag[0], dxd_diag[1]) + dxd_state
            prod_parts.append(dxd * X)
            dxs_ref[:, lanes] = dxd * dtsel + sel1(dsk) * dY
            dyx = jnp.sum(dY * X, axis=0, keepdims=True)
            sA = jnp.sum(jnp.where(lo1, dyx, 0.0), axis=1, keepdims=True)
            sB = jnp.sum(dyx, axis=1, keepdims=True) - sA
            dD_row = dD_row + jnp.where(lane1 == hA, sA, 0.0) + jnp.where(lane1 == hB, sB, 0.0)
            yield
        dGb = dG.astype(BF16)
        db_ref[...] = dB + lax.dot_general(dGb, Cb, _TN, preferred_element_type=F32)
        dc_ref[...] = dC + jnp.dot(dGb, Bb, preferred_element_type=F32)
        dcs_mat = _lane_block_sums(jnp.concatenate(v_parts, axis=1), 128) + jnp.where(rowl == L - 1, tails, 0.0)
        ddt_mat = _lane_block_sums(jnp.concatenate(prod_parts, axis=1), 64)
        dad = _rcumsum_rows(dcs_mat - dcs_t.T)
        draw = (a * dad + ddt_mat) * _sigmoid(raw)
        ddt_ref[...] = draw
        dpar_ref[0:1, :] += jnp.sum(draw, axis=0, keepdims=True)
        dpar_ref[1:2, :] += jnp.sum(dt * dad, axis=0, keepdims=True) * a
        dpar_ref[2:3, :] += dD_row

    return pl.pallas_call(
        body, name=name, grid=(nc,), in_specs=in_specs,
        out_specs=[pl.BlockSpec((L, CONV_DIM), lambda c: (ci(c), 0)),
                   pl.BlockSpec((L, D_INNER), lambda c: (ci(c), 0)),
                   pl.BlockSpec((SSM_GROUPS, L, 128), lambda c: (0, ci(c), 0)),
                   pl.BlockSpec((1, D_INNER), lambda c: (0, 0)),
                   pl.BlockSpec((SSM_GROUPS, 8, 128), lambda c: (0, 0, 0))],
        out_shape=[jax.ShapeDtypeStruct((T, CONV_DIM), F32), jax.ShapeDtypeStruct((T, D_INNER), BF16),
                   jax.ShapeDtypeStruct((SSM_GROUPS, T, 128), F32), jax.ShapeDtypeStruct((1, D_INNER), F32),
                   jax.ShapeDtypeStruct((SSM_GROUPS, 8, 128), F32)],
        scratch_shapes=[pltpu.VMEM((SSM_GROUPS, 4, 128, 128), F32)],
        compiler_params=_cparams(("arbitrary",)))(xbc_c, xbc_c, xbc_c, dtg, par, zx, gnw, y, st, dyn)


SB_KEYS = 512
SB_SCAN = 256
SB_STRIP = 256


def _tri(width, cond):
    kk = lax.broadcasted_iota(jnp.int32, (width, width), 0)
    jj = lax.broadcasted_iota(jnp.int32, (width, width), 1)
    return cond(kk, jj).astype(BF16)


def _sba_diag_mask():
    Bq = SB_BLOCK
    rowi = lax.broadcasted_iota(jnp.int32, (2 * Bq, Bq), 0)
    return lax.broadcasted_iota(jnp.int32, (2 * Bq, Bq), 1) < jnp.where(rowi >= Bq, rowi - Bq, rowi)


_LOG2E = 1.4426950408889634


def _softplus2(z2):
    return jnp.maximum(z2, 0.0) + jnp.log2(1.0 + jnp.exp2(-jnp.abs(z2)))


def _sba_sub_fwd(zb, c, U, mask):
    z2 = zb * _LOG2E
    s = _softplus2(z2)
    if mask is not None:
        s = jnp.where(mask, s, 0.0)
    R = c + jnp.dot(s.astype(BF16), U, preferred_element_type=F32)
    A = jnp.exp2(z2 - s - R)
    if mask is not None:
        A = jnp.where(mask, A, 0.0)
    return A.astype(BF16), R[:, 0:1] + s[:, 0:1]


def _sba_sub_bwd(zb, dAb, Lt, pc, pe, Uincl, Uexcl, mask):
    last = zb.shape[1] - 1
    z2 = zb * _LOG2E
    s = _softplus2(z2)
    g = z2 - s
    if mask is not None:
        s = jnp.where(mask, s, 0.0)
    P = pc + jnp.dot(s.astype(BF16), Uincl, preferred_element_type=F32)
    A = jnp.exp2(g - (Lt - P))
    if mask is not None:
        A = jnp.where(mask, A, 0.0)
    E = dAb * A
    PE = pe + jnp.dot(E.astype(BF16), Uexcl, preferred_element_type=F32)
    dz = E - jnp.exp2(g) * (E + PE)
    if mask is not None:
        dz = jnp.where(mask, dz, 0.0)
    return (A.astype(BF16), dz.astype(BF16), P[:, last:last + 1], PE[:, last:last + 1] + E[:, last:last + 1])


def _stack_heads(v):
    lo = lax.broadcasted_iota(jnp.int32, v.shape, 1) < 64
    zero = jnp.zeros_like(v)
    return jnp.concatenate([jnp.where(lo, v, zero), jnp.where(lo, zero, v)], axis=0)


def _unstack_heads(v):
    lo = lax.broadcasted_iota(jnp.int32, (SB_BLOCK, 128), 1) < 64
    return jnp.where(lo, v[:SB_BLOCK], v[SB_BLOCK:])


def _sba_rows(a):
    return slice(2 * a * SB_BLOCK, 2 * (a + 1) * SB_BLOCK)


def _sba_diag_case(a, b):
    Bq = SB_BLOCK
    if b * SB_SCAN >= (a + 1) * Bq:
        return "skip"
    if (b + 1) * SB_SCAN <= a * Bq:
        return "full"
    rowi = lax.broadcasted_iota(jnp.int32, (2 * Bq, SB_SCAN), 0)
    qpos = a * Bq + jnp.where(rowi >= Bq, rowi - Bq, rowi)
    return b * SB_SCAN + lax.broadcasted_iota(jnp.int32, (2 * Bq, SB_SCAN), 1) < qpos


def _sba_fwd(q, kv, *, name):
    T = q.shape[0]
    Bq = SB_BLOCK
    nsub = SB_KEYS // Bq
    nscan = SB_KEYS // SB_SCAN
    R = 2 * SB_KEYS
    assert T % SB_KEYS == 0 and SB_STRIP == 2 * Bq
    scale = 1.0 / math.sqrt(SB_HEAD_DIM)

    def body(q_ref, k_ref, v_ref, o_ref, lt_ref, z_s, a_s, c_s, acc_s):
        i = pl.program_id(1)
        U2 = _tri(SB_SCAN, lambda k, j: k > j)
        qs_all = jnp.concatenate([_stack_heads(q_ref[a * Bq:(a + 1) * Bq, :] * scale) for a in range(nsub)], axis=0)
        c_s[...] = jnp.zeros_like(c_s)
        acc_s[...] = jnp.zeros_like(acc_s)

        def scores(J, slot):
            off = pl.multiple_of(J * SB_KEYS, SB_KEYS)
            z_s[slot] = lax.dot_general(qs_all, k_ref[pl.ds(off, SB_KEYS), :], _NT, preferred_element_type=F32)

        def weights(slot, diag):
            for a in range(nsub):
                rows = _sba_rows(a)
                c = c_s[rows, :]
                for b in reversed(range(nscan)):
                    cols = slice(b * SB_SCAN, (b + 1) * SB_SCAN)
                    case = _sba_diag_case(a, b) if diag else "full"
                    if isinstance(case, str) and case == "skip":
                        a_s[slot, rows, cols] = jnp.zeros((2 * Bq, SB_SCAN), BF16)
                        continue
                    A, c = _sba_sub_fwd(z_s[slot, rows, cols], c, U2, None if isinstance(case, str) else case)
                    a_s[slot, rows, cols] = A
                c_s[rows, :] = c

        def values(J, slot):
            off = pl.multiple_of(J * SB_KEYS, SB_KEYS)
            acc_s[...] += jnp.dot(a_s[slot], v_ref[pl.ds(off, SB_KEYS), :], preferred_element_type=F32)

        scores(i, 0)
        weights(0, True)
        scores(jnp.maximum(i - 1, 0), 1)

        def two_steps(u, _):
            t = 2 * u + 1
            weights(1, False)
            scores(jnp.maximum(i - t - 1, 0), 0)
            values(i - t + 1, 0)
            weights(0, False)
            scores(jnp.maximum(i - t - 2, 0), 1)
            values(i - t, 1)
            return 0

        lax.fori_loop(0, i // 2, two_steps, 0)
        odd = lax.rem(i, 2) == 1

        @pl.when(jnp.logical_not(odd))
        def _():
            values(0, 0)

        @pl.when(odd)
        def _():
            weights(1, False)
            values(1, 0)
            values(0, 1)
        for a in range(nsub):
            o_ref[a * Bq:(a + 1) * Bq, :] = _unstack_heads(acc_s[_sba_rows(a), :]).astype(BF16)
            lt_ref[a * Bq:(a + 1) * Bq, :] = _unstack_heads(jnp.broadcast_to(c_s[_sba_rows(a), :], (2 * Bq, 128)))

    return pl.pallas_call(
        body, name=name, grid=(SB_HEADS // 2, T // SB_KEYS),
        in_specs=[pl.BlockSpec((SB_KEYS, 128), lambda p, i: (i, p)), pl.BlockSpec((T, 128), lambda p, i: (0, p)),
                  pl.BlockSpec((T, 128), lambda p, i: (0, p + SB_HEADS // 2))],
        out_specs=[pl.BlockSpec((SB_KEYS, 128), lambda p, i: (i, p)),
                   pl.BlockSpec((None, SB_KEYS, 128), lambda p, i: (p, i, 0))],
        out_shape=[jax.ShapeDtypeStruct((T, D_MODEL), BF16), jax.ShapeDtypeStruct((SB_HEADS // 2, T, 128), F32)],
        scratch_shapes=[pltpu.VMEM((2, R, SB_KEYS), F32), pltpu.VMEM((2, R, SB_KEYS), BF16),
                        pltpu.VMEM((R, 1), F32), pltpu.VMEM((R, 128), F32)],
        compiler_params=_cparams(("parallel", "parallel")))(q, kv, kv)


def _sba_bwd(q, kv, lt, do, *, name):
    T = q.shape[0]
    Bq = SB_BLOCK
    nq = T // SB_KEYS
    nsub = SB_KEYS // Bq
    nscan = SB_KEYS // SB_SCAN
    R = 2 * SB_KEYS
    assert T % SB_KEYS == 0 and SB_STRIP == 2 * Bq
    scale = 1.0 / math.sqrt(SB_HEAD_DIM)

    def body(q_ref, k_ref, v_ref, lt_ref, do_ref, dq_ref, dk_ref, dv_ref, dk_acc, dv_acc,
             z_s, da_s, a_s, dz_s, pc_s, pe_s, lt_s, dq_s):
        i = pl.program_id(1)

        @pl.when(i == 0)
        def _():
            dk_acc[...] = jnp.zeros_like(dk_acc)
            dv_acc[...] = jnp.zeros_like(dv_acc)

        Uincl = _tri(SB_SCAN, lambda k, j: k <= j)
        Uexcl = _tri(SB_SCAN, lambda k, j: k < j)
        qs, dos = [], []
        for a in range(nsub):
            rows = slice(a * Bq, (a + 1) * Bq)
            qs.append(_stack_heads(q_ref[rows, :] * scale))
            dos.append(_stack_heads(do_ref[rows, :]))
            lt_s[_sba_rows(a), :] = jnp.concatenate([lt_ref[rows, 0:1], lt_ref[rows, 64:65]], axis=0)
        qs_all = jnp.concatenate(qs, axis=0)
        dos_all = jnp.concatenate(dos, axis=0)
        pc_s[...] = jnp.zeros_like(pc_s)
        pe_s[...] = jnp.zeros_like(pe_s)
        a_s[1] = jnp.zeros((R, SB_KEYS), BF16)
        dz_s[1] = jnp.zeros((R, SB_KEYS), BF16)

        def scores(J, slot):
            off = pl.multiple_of(J * SB_KEYS, SB_KEYS)
            z_s[slot] = lax.dot_general(qs_all, k_ref[pl.ds(off, SB_KEYS), :], _NT, preferred_element_type=F32)
            da_s[slot] = lax.dot_general(dos_all, v_ref[pl.ds(off, SB_KEYS), :], _NT, preferred_element_type=F32)

        def gradients(slot, diag):
            for a in range(nsub):
                rows = _sba_rows(a)
                pc, pe, Lt = pc_s[rows, :], pe_s[rows, :], lt_s[rows, :]
                for b in range(nscan):
                    cols = slice(b * SB_SCAN, (b + 1) * SB_SCAN)
                    case = _sba_diag_case(a, b) if diag else "full"
                    if isinstance(case, str) and case == "skip":
                        a_s[slot, rows, cols] = jnp.zeros((2 * Bq, SB_SCAN), BF16)
                        dz_s[slot, rows, cols] = jnp.zeros((2 * Bq, SB_SCAN), BF16)
                        continue
                    A, dz, pc, pe = _sba_sub_bwd(z_s[slot, rows, cols], da_s[slot, rows, cols], Lt, pc, pe, Uincl, Uexcl,
                                                 None if isinstance(case, str) else case)
                    a_s[slot, rows, cols] = A
                    dz_s[slot, rows, cols] = dz
                pc_s[rows, :] = pc
                pe_s[rows, :] = pe

        def products(J, slot):
            off = pl.multiple_of(J * SB_KEYS, SB_KEYS)
            dzt = dz_s[slot]
            dk_acc[pl.ds(off, SB_KEYS), :] += lax.dot_general(dzt, qs_all, _TN, preferred_element_type=F32)
            dv_acc[pl.ds(off, SB_KEYS), :] += lax.dot_general(a_s[slot], dos_all, _TN, preferred_element_type=F32)
            dq_s[...] += jnp.dot(dzt, k_ref[pl.ds(off, SB_KEYS), :], preferred_element_type=F32)

        dq_s[...] = jnp.zeros_like(dq_s)
        scores(0, 0)

        def two_steps(u, _):
            t = 2 * u
            gradients(0, False)
            scores(t + 1, 1)
            products(jnp.maximum(t - 1, 0), 1)
            gradients(1, False)
            scores(t + 2, 0)
            products(t, 0)
            return 0

        lax.fori_loop(0, i // 2, two_steps, 0)
        odd = lax.rem(i, 2) == 1

        @pl.when(jnp.logical_not(odd))
        def _():
            gradients(0, True)
            products(jnp.maximum(i - 1, 0), 1)
            products(i, 0)

        @pl.when(odd)
        def _():
            gradients(0, False)
            scores(i, 1)
            products(jnp.maximum(i - 2, 0), 1)
            gradients(1, True)
            products(i - 1, 0)
            products(i, 1)

        for a in range(nsub):
            dq_ref[a * Bq:(a + 1) * Bq, :] = (_unstack_heads(dq_s[_sba_rows(a), :]) * scale).astype(BF16)

        @pl.when(i == nq - 1)
        def _():
            dk_ref[...] = dk_acc[...].astype(BF16)
            dv_ref[...] = dv_acc[...].astype(BF16)

    return pl.pallas_call(
        body, name=name, grid=(SB_HEADS // 2, nq),
        in_specs=[pl.BlockSpec((SB_KEYS, 128), lambda p, i: (i, p)), pl.BlockSpec((T, 128), lambda p, i: (0, p)),
                  pl.BlockSpec((T, 128), lambda p, i: (0, p + SB_HEADS // 2)),
                  pl.BlockSpec((None, SB_KEYS, 128), lambda p, i: (p, i, 0)),
                  pl.BlockSpec((SB_KEYS, 128), lambda p, i: (i, p))],
        out_specs=[pl.BlockSpec((SB_KEYS, 128), lambda p, i: (i, p)), pl.BlockSpec((T, 128), lambda p, i: (0, p)),
                   pl.BlockSpec((T, 128), lambda p, i: (0, p))],
        out_shape=[jax.ShapeDtypeStruct((T, D_MODEL), BF16), jax.ShapeDtypeStruct((T, D_MODEL), BF16),
                   jax.ShapeDtypeStruct((T, D_MODEL), BF16)],
        scratch_shapes=[pltpu.VMEM((T, 128), F32), pltpu.VMEM((T, 128), F32),
                        pltpu.VMEM((2, R, SB_KEYS), F32), pltpu.VMEM((2, R, SB_KEYS), F32),
                        pltpu.VMEM((2, R, SB_KEYS), BF16), pltpu.VMEM((2, R, SB_KEYS), BF16),
                        pltpu.VMEM((R, 1), F32), pltpu.VMEM((R, 1), F32), pltpu.VMEM((R, 1), F32),
                        pltpu.VMEM((R, 128), F32)],
        compiler_params=_cparams(("parallel", "arbitrary")))(q, kv, kv, lt, do)


def _sba_fwd_old(q, kv, *, name):
    T = q.shape[0]
    Bq = SB_BLOCK
    nsub = SB_KEYS // Bq
    assert T % SB_KEYS == 0
    scale = 1.0 / math.sqrt(SB_HEAD_DIM)

    def body(q_ref, k_ref, v_ref, o_ref, lt_ref):
        I = pl.program_id(1)
        U1 = _tri(Bq, lambda k, j: k > j)
        U2 = _tri(SB_SCAN, lambda k, j: k > j)
        dmask = _sba_diag_mask()
        qs = [_stack_heads(q_ref[a * Bq:(a + 1) * Bq, :] * scale) for a in range(nsub)]
        cs, accs = [], []
        for a in range(nsub):
            c = jnp.zeros((2 * Bq, 1), F32)
            acc = jnp.zeros((2 * Bq, 128), F32)
            for b in range(a, -1, -1):
                off = pl.multiple_of(I * SB_KEYS + b * Bq, Bq)
                zb = lax.dot_general(qs[a], k_ref[pl.ds(off, Bq), :], _NT, preferred_element_type=F32)
                A, c = _sba_sub_fwd(zb, c, U1, dmask if b == a else None)
                acc = acc + jnp.dot(A, v_ref[pl.ds(off, Bq), :], preferred_element_type=F32)
            cs.append(c)
            accs.append(acc)
        qs_all = jnp.concatenate(qs, axis=0)

        def step(n, carry):
            c, acc = carry
            off = pl.multiple_of((I - 1 - n) * SB_KEYS, SB_KEYS)
            z = lax.dot_general(qs_all, k_ref[pl.ds(off, SB_KEYS), :], _NT, preferred_element_type=F32)
            parts = [None] * (SB_KEYS // SB_SCAN)
            for b in reversed(range(SB_KEYS // SB_SCAN)):
                parts[b], c = _sba_sub_fwd(z[:, b * SB_SCAN:(b + 1) * SB_SCAN], c, U2, None)
            return c, acc + jnp.dot(jnp.concatenate(parts, axis=1), v_ref[pl.ds(off, SB_KEYS), :],
                                    preferred_element_type=F32)

        c, acc = lax.fori_loop(0, I, step, (jnp.concatenate(cs, axis=0), jnp.concatenate(accs, axis=0)))
        for a in range(nsub):
            rows = slice(2 * a * Bq, 2 * (a + 1) * Bq)
            o_ref[a * Bq:(a + 1) * Bq, :] = _unstack_heads(acc[rows]).astype(BF16)
            lt_ref[a * Bq:(a + 1) * Bq, :] = _unstack_heads(jnp.broadcast_to(c[rows], (2 * Bq, 128)))

    return pl.pallas_call(
        body, name=name, grid=(SB_HEADS // 2, T // SB_KEYS),
        in_specs=[pl.BlockSpec((SB_KEYS, 128), lambda p, i: (i, p)), pl.BlockSpec((T, 128), lambda p, i: (0, p)),
                  pl.BlockSpec((T, 128), lambda p, i: (0, p + SB_HEADS // 2))],
        out_specs=[pl.BlockSpec((SB_KEYS, 128), lambda p, i: (i, p)),
                   pl.BlockSpec((None, SB_KEYS, 128), lambda p, i: (p, i, 0))],
        out_shape=[jax.ShapeDtypeStruct((T, D_MODEL), BF16), jax.ShapeDtypeStruct((SB_HEADS // 2, T, 128), F32)],
        compiler_params=_cparams(("parallel", "parallel")))(q, kv, kv)


def _sba_bwd_old(q, kv, lt, do, *, name):
    T = q.shape[0]
    Bq = SB_BLOCK
    nq = T // SB_KEYS
    nsub = SB_KEYS // Bq
    assert T % SB_KEYS == 0
    scale = 1.0 / math.sqrt(SB_HEAD_DIM)

    def body(q_ref, k_ref, v_ref, lt_ref, do_ref, dq_ref, dk_ref, dv_ref, dk_acc, dv_acc,
             z_s, da_s, a_s, dz_s, pc_s, pe_s, lt_s):
        i = pl.program_id(1)

        @pl.when(i == 0)
        def _():
            dk_acc[...] = jnp.zeros_like(dk_acc)
            dv_acc[...] = jnp.zeros_like(dv_acc)

        Uincl1 = _tri(Bq, lambda k, j: k <= j)
        Uexcl1 = _tri(Bq, lambda k, j: k < j)
        Uincl2 = _tri(SB_SCAN, lambda k, j: k <= j)
        Uexcl2 = _tri(SB_SCAN, lambda k, j: k < j)
        dmask = _sba_diag_mask()
        qs, dos, lts = [], [], []
        for a in range(nsub):
            rows = slice(a * Bq, (a + 1) * Bq)
            qs.append(_stack_heads(q_ref[rows, :] * scale))
            dos.append(_stack_heads(do_ref[rows, :]))
            lts.append(jnp.concatenate([lt_ref[rows, 0:1], lt_ref[rows, 64:65]], axis=0))
        qs_all = jnp.concatenate(qs, axis=0)
        dos_all = jnp.concatenate(dos, axis=0)
        lt_all = jnp.concatenate(lts, axis=0)

        R = 2 * nsub * Bq
        pc_s[...] = jnp.zeros_like(pc_s)
        pe_s[...] = jnp.zeros_like(pe_s)
        lt_s[...] = lt_all

        def scores(J, slot):
            off = pl.multiple_of(J * SB_KEYS, SB_KEYS)
            z_s[slot] = lax.dot_general(qs_all, k_ref[pl.ds(off, SB_KEYS), :], _NT, preferred_element_type=F32)
            da_s[slot] = lax.dot_general(dos_all, v_ref[pl.ds(off, SB_KEYS), :], _NT, preferred_element_type=F32)

        def elementwise(slot):
            for r in range(R // SB_STRIP):
                rows = slice(r * SB_STRIP, (r + 1) * SB_STRIP)
                pc, pe, Lt = pc_s[rows, :], pe_s[rows, :], lt_s[rows, :]
                for b in range(SB_KEYS // SB_SCAN):
                    cols = slice(b * SB_SCAN, (b + 1) * SB_SCAN)
                    A, dz, pc, pe = _sba_sub_bwd(z_s[slot, rows, cols], da_s[slot, rows, cols], Lt, pc, pe,
                                                 Uincl2, Uexcl2, None)
                    a_s[slot, rows, cols] = A
                    dz_s[slot, rows, cols] = dz
                pc_s[rows, :] = pc
                pe_s[rows, :] = pe

        def outputs(J, slot, dq_acc):
            off = pl.multiple_of(J * SB_KEYS, SB_KEYS)
            dzt = dz_s[slot]
            dk_acc[pl.ds(off, SB_KEYS), :] += lax.dot_general(dzt, qs_all, _TN, preferred_element_type=F32)
            dv_acc[pl.ds(off, SB_KEYS), :] += lax.dot_general(a_s[slot], dos_all, _TN, preferred_element_type=F32)
            return dq_acc + jnp.dot(dzt, k_ref[pl.ds(off, SB_KEYS), :], preferred_element_type=F32)

        a_s[1] = jnp.zeros((R, SB_KEYS), BF16)
        dz_s[1] = jnp.zeros((R, SB_KEYS), BF16)
        last = jnp.maximum(i - 1, 0)
        scores(0, 0)

        def step(J, dq_acc):
            slot = lax.rem(J, 2)
            elementwise(slot)
            scores(jnp.minimum(J + 1, last), 1 - slot)
            return outputs(jnp.maximum(J - 1, 0), 1 - slot, dq_acc)

        dq_acc = lax.fori_loop(0, i, step, jnp.zeros((R, 128), F32))
        dq_acc = outputs(last, lax.rem(i + 1, 2), dq_acc)
        pc, pe = pc_s[...], pe_s[...]
        for a in range(nsub):
            rows = slice(2 * a * Bq, 2 * (a + 1) * Bq)
            pca, pea, dqa = pc[rows], pe[rows], dq_acc[rows]
            for b in range(a + 1):
                off = pl.multiple_of(i * SB_KEYS + b * Bq, Bq)
                kb = k_ref[pl.ds(off, Bq), :]
                zb = lax.dot_general(qs[a], kb, _NT, preferred_element_type=F32)
                dAb = lax.dot_general(dos[a], v_ref[pl.ds(off, Bq), :], _NT, preferred_element_type=F32)
                A, dz, pca, pea = _sba_sub_bwd(zb, dAb, lts[a], pca, pea, Uincl1, Uexcl1, dmask if b == a else None)
                dqa = dqa + jnp.dot(dz, kb, preferred_element_type=F32)
                dk_acc[pl.ds(off, Bq), :] += lax.dot_general(dz, qs[a], _TN, preferred_element_type=F32)
                dv_acc[pl.ds(off, Bq), :] += lax.dot_general(A, dos[a], _TN, preferred_element_type=F32)
            dq_ref[a * Bq:(a + 1) * Bq, :] = (_unstack_heads(dqa) * scale).astype(BF16)

        @pl.when(i == nq - 1)
        def _():
            dk_ref[...] = dk_acc[...].astype(BF16)
            dv_ref[...] = dv_acc[...].astype(BF16)

    return pl.pallas_call(
        body, name=name, grid=(SB_HEADS // 2, nq),
        in_specs=[pl.BlockSpec((SB_KEYS, 128), lambda p, i: (i, p)), pl.BlockSpec((T, 128), lambda p, i: (0, p)),
                  pl.BlockSpec((T, 128), lambda p, i: (0, p + SB_HEADS // 2)),
                  pl.BlockSpec((None, SB_KEYS, 128), lambda p, i: (p, i, 0)),
                  pl.BlockSpec((SB_KEYS, 128), lambda p, i: (i, p))],
        out_specs=[pl.BlockSpec((SB_KEYS, 128), lambda p, i: (i, p)), pl.BlockSpec((T, 128), lambda p, i: (0, p)),
                   pl.BlockSpec((T, 128), lambda p, i: (0, p))],
        out_shape=[jax.ShapeDtypeStruct((T, D_MODEL), BF16), jax.ShapeDtypeStruct((T, D_MODEL), BF16),
                   jax.ShapeDtypeStruct((T, D_MODEL), BF16)],
        scratch_shapes=[pltpu.VMEM((T, 128), F32), pltpu.VMEM((T, 128), F32),
                        pltpu.VMEM((2, 2 * SB_KEYS, SB_KEYS), F32), pltpu.VMEM((2, 2 * SB_KEYS, SB_KEYS), F32),
                        pltpu.VMEM((2, 2 * SB_KEYS, SB_KEYS), BF16), pltpu.VMEM((2, 2 * SB_KEYS, SB_KEYS), BF16),
                        pltpu.VMEM((2 * SB_KEYS, 1), F32), pltpu.VMEM((2 * SB_KEYS, 1), F32),
                        pltpu.VMEM((2 * SB_KEYS, 1), F32)],
        compiler_params=_cparams(("parallel", "arbitrary")))(q, kv, kv, lt, do)


def _loss_head(h, tgt, w, *, name, tt=512):
    T, D = h.shape
    tt = min(tt, T)

    def body(h_ref, t_ref, w_ref, loss_ref, dh_ref, dw_ref):
        i = pl.program_id(0)
        hv = h_ref[...]
        wv = w_ref[...]
        r = lax.rsqrt(jnp.mean(hv * hv, axis=-1, keepdims=True) + EPS)
        xhat = hv * r
        err = xhat * wv - t_ref[...]
        part = 0.5 * jnp.sum(jnp.mean(err * err, axis=-1, keepdims=True), axis=0, keepdims=True)
        dy = err * (1.0 / D)
        dxh = dy * wv
        dh_ref[...] = r * (dxh - xhat * jnp.mean(dxh * xhat, axis=-1, keepdims=True))
        dwc = jnp.sum(dy * xhat, axis=0, keepdims=True)

        @pl.when(i == 0)
        def _():
            loss_ref[...] = jnp.broadcast_to(part, loss_ref.shape)
            dw_ref[...] = dwc

        @pl.when(i > 0)
        def _():
            loss_ref[...] += jnp.broadcast_to(part, loss_ref.shape)
            dw_ref[...] += dwc

    return pl.pallas_call(
        body, name=name, grid=(T // tt,),
        in_specs=[pl.BlockSpec((tt, D), lambda i: (i, 0)), pl.BlockSpec((tt, D), lambda i: (i, 0)),
                  pl.BlockSpec((1, D), lambda i: (0, 0))],
        out_specs=[pl.BlockSpec((1, 128), lambda i: (0, 0)), pl.BlockSpec((tt, D), lambda i: (i, 0)),
                   pl.BlockSpec((1, D), lambda i: (0, 0))],
        out_shape=[jax.ShapeDtypeStruct((1, 128), F32), jax.ShapeDtypeStruct((T, D), F32),
                   jax.ShapeDtypeStruct((1, D), F32)],
        compiler_params=_cparams(("arbitrary",)))(h, tgt, w.reshape(1, D))


def _adamw(parts, w, m, v, *, name, tr=256):
    P, R, C = parts.shape
    tr = min(tr, R)
    assert R % tr == 0, (name, R, tr)
    c1 = 1.0 - ADAM_B1 ** ADAM_STEP
    c2 = 1.0 - ADAM_B2 ** ADAM_STEP

    def body(p_ref, w_ref, m_ref, v_ref, g_ref, d_ref, nm_ref, nv_ref):
        g = p_ref[0].astype(F32)
        for k in range(1, P):
            g = g + p_ref[k].astype(F32)
        mn = ADAM_B1 * m_ref[...] + (1.0 - ADAM_B1) * g
        vn = ADAM_B2 * v_ref[...] + (1.0 - ADAM_B2) * (g * g)
        g_ref[...] = g
        nm_ref[...] = mn
        nv_ref[...] = vn
        d_ref[...] = -ADAM_LR * ((mn / c1) / (jnp.sqrt(vn / c2) + ADAM_EPS) + ADAM_WD * w_ref[...])

    spec = pl.BlockSpec((tr, C), lambda i: (i, 0))
    sds = jax.ShapeDtypeStruct((R, C), F32)
    return pl.pallas_call(
        body, name=name, grid=(R // tr,),
        in_specs=[pl.BlockSpec((P, tr, C), lambda i: (0, i, 0)), spec, spec, spec],
        out_specs=[spec, spec, spec, spec], out_shape=[sds, sds, sds, sds],
        compiler_params=_cparams(("parallel",)))(parts, w, m, v)


def _all_gather(shards, *, name):
    n = len(shards)

    def body(*refs):
        ins, outs = refs[:n], refs[n:2 * n]
        send_sems, recv_sems, local_sems = refs[2 * n:]
        x, y, c = lax.axis_index("x"), lax.axis_index("y"), lax.axis_index("c")
        me, sib = (x, y, c), (x, y, 1 - c)
        chips = [(1 - x, y), (x, 1 - y), (1 - x, 1 - y)]

        def slot(p):
            return 4 * p[0] + 2 * p[1] + p[2]

        def cp(a, k, block, to, src=None):
            dst = outs[a].at[slot(block)]
            return pltpu.make_async_remote_copy(src_ref=dst if src is None else src, dst_ref=dst,
                                                send_sem=send_sems.at[a, k], recv_sem=recv_sems.at[a, k],
                                                device_id=to, device_id_type=_MESH)

        mine = [pltpu.make_async_copy(ins[a], outs[a].at[slot(me)], local_sems.at[a]) for a in range(n)]
        for m in mine:
            m.start()
        first = []
        for a in range(n):
            first.append(cp(a, 0, me, sib, src=ins[a]))
            for j, chip in enumerate(chips):
                first.append(cp(a, 1 + j, me, (*chip, c), src=ins[a]))
        for f in first:
            f.start()
        passed = []
        for j, chip in enumerate(chips):
            for a in range(n):
                cp(a, 1 + j, (*chip, c), me).wait_recv()
                f = cp(a, 4 + j, (*chip, c), sib)
                f.start()
                passed.append(f)
        for a in range(n):
            cp(a, 0, sib, me).wait_recv()
            for j, chip in enumerate(chips):
                cp(a, 4 + j, (*chip, 1 - c), me).wait_recv()
        for f in first + passed:
            f.wait_send()
        for m in mine:
            m.wait()

    return pl.pallas_call(
        body, name=name, in_specs=[_ANY] * n, out_specs=[_ANY] * n,
        out_shape=[jax.ShapeDtypeStruct((N_DEV,) + s.shape, s.dtype) for s in shards],
        scratch_shapes=[pltpu.SemaphoreType.DMA((n, 7)), pltpu.SemaphoreType.DMA((n, 7)),
                        pltpu.SemaphoreType.DMA((n,))])(*shards)


def _exchange(blocks, *, name):
    n = len(blocks)

    def body(*refs):
        ins, outs = refs[:n], refs[n:2 * n]
        send_sems, recv_sems, local_sems = refs[2 * n:]
        x, y, c = lax.axis_index("x"), lax.axis_index("y"), lax.axis_index("c")
        me = 4 * x + 2 * y + c
        mine = [pltpu.make_async_copy(ins[a].at[me], outs[a].at[me], local_sems.at[a]) for a in range(n)]
        for m in mine:
            m.start()
        copies = []
        for r in range(1, N_DEV):
            rx, ry, rc = (r >> 2) & 1, (r >> 1) & 1, r & 1
            px, py, pc = (1 - x if rx else x), (1 - y if ry else y), (1 - c if rc else c)
            peer = 4 * px + 2 * py + pc
            for a in range(n):
                copies.append((pltpu.make_async_remote_copy(
                    src_ref=ins[a].at[peer], dst_ref=outs[a].at[me], send_sem=send_sems.at[a, r - 1],
                    recv_sem=recv_sems.at[a, r - 1], device_id=(px, py, pc), device_id_type=_MESH),
                    pltpu.make_async_remote_copy(
                    src_ref=ins[a].at[peer], dst_ref=outs[a].at[peer], send_sem=send_sems.at[a, r - 1],
                    recv_sem=recv_sems.at[a, r - 1], device_id=(px, py, pc), device_id_type=_MESH)))
        for snd, _ in copies:
            snd.start()
        for _, rcv in copies:
            rcv.wait_recv()
        for snd, _ in copies:
            snd.wait_send()
        for m in mine:
            m.wait()

    return pl.pallas_call(
        body, name=name, in_specs=[_ANY] * n, out_specs=[_ANY] * n,
        out_shape=[jax.ShapeDtypeStruct(b.shape, b.dtype) for b in blocks],
        scratch_shapes=[pltpu.SemaphoreType.DMA((n, 7)), pltpu.SemaphoreType.DMA((n, 7)),
                        pltpu.SemaphoreType.DMA((n,))])(*blocks)


_HBM = pl.BlockSpec(memory_space=pltpu.HBM)
_SEM = pl.BlockSpec(memory_space=pltpu.SEMAPHORE)
_EFFECT = pltpu.SideEffectType.DATAFLOW_SIDE_EFFECTING


def _peers():
    x, y, c = lax.axis_index("x"), lax.axis_index("y"), lax.axis_index("c")
    out = []
    for r in range(1, N_DEV):
        px = 1 - x if (r >> 2) & 1 else x
        py = 1 - y if (r >> 1) & 1 else y
        pc = 1 - c if r & 1 else c
        out.append(((px, py, pc), 4 * px + 2 * py + pc))
    return 4 * x + 2 * y + c, out


def _push_copy(src_ref, land_ref, send_sems, recv_sems, a, k, me, peer, peer_slot, scatter, arriving):
    src = src_ref.at[peer_slot] if scatter else src_ref
    return pltpu.make_async_remote_copy(
        src_ref=src, dst_ref=land_ref.at[peer_slot if arriving else me], send_sem=send_sems.at[a * (N_DEV - 1) + k],
        recv_sem=recv_sems.at[a * (N_DEV - 1) + k], device_id=peer, device_id_type=_MESH)


def _push_start(srcs, *, scatter, name):
    n = len(srcs)
    lands = [lax.empty(s.shape if scatter else (N_DEV,) + s.shape, s.dtype) for s in srcs]

    def body(*refs):
        src_refs, land_refs = refs[:n], refs[n:2 * n]
        send_sems, recv_sems = refs[2 * n], refs[2 * n + 1]
        token = refs[-1]
        me, peers = _peers()
        for k, (peer, slot) in enumerate(peers):
            for a in range(n):
                _push_copy(src_refs[a], land_refs[a], send_sems, recv_sems, a, k, me, peer, slot, scatter, False).start()
        token[...] = jnp.zeros_like(token)

    hbm = lambda a: pltpu.HBM(a.shape, a.dtype)
    outs = pl.pallas_call(
        body, name=name,
        out_shape=(pltpu.SemaphoreType.DMA((n * (N_DEV - 1),)), pltpu.SemaphoreType.DMA((n * (N_DEV - 1),)),
                   *[hbm(s) for s in srcs], *[hbm(l) for l in lands], jax.ShapeDtypeStruct((8, 128), F32)),
        in_specs=[_HBM] * (2 * n),
        out_specs=(_SEM, _SEM, *([_HBM] * (2 * n)), pl.BlockSpec(memory_space=pltpu.VMEM)),
        input_output_aliases={i: 2 + i for i in range(2 * n)},
        compiler_params=pltpu.CompilerParams(has_side_effects=_EFFECT),
    )(*[pltpu.with_memory_space_constraint(s, pltpu.HBM) for s in srcs],
      *[pltpu.with_memory_space_constraint(l, pltpu.HBM) for l in lands])
    return dict(send=outs[0], recv=outs[1], srcs=list(outs[2:2 + n]), lands=list(outs[2 + n:2 + 2 * n]),
                token=outs[-1], scatter=scatter, n=n)


def _push_wait(h, after, *, name):
    n, scatter = h["n"], h["scatter"]

    def body(*refs):
        src_refs, land_refs = refs[:n], refs[n:2 * n]
        send_sems, recv_sems = refs[2 * n], refs[2 * n + 1]
        me, peers = _peers()
        for k, (peer, slot) in enumerate(peers):
            for a in range(n):
                cp = _push_copy(src_refs[a], land_refs[a], send_sems, recv_sems, a, k, me, peer, slot, scatter, True)
                cp.wait_send()
                cp.wait_recv()

    hbm = lambda a: pltpu.HBM(a.shape, a.dtype)
    outs = pl.pallas_call(
        body, name=name,
        out_shape=(*[hbm(s) for s in h["srcs"]], *[hbm(l) for l in h["lands"]]),
        in_specs=[_HBM] * (2 * n) + [_SEM, _SEM, _ANY], out_specs=tuple([_HBM] * (2 * n)),
        input_output_aliases={i: i for i in range(2 * n)},
        compiler_params=pltpu.CompilerParams(has_side_effects=_EFFECT),
    )(*h["srcs"], *h["lands"], h["send"], h["recv"], after)
    return list(outs[:n]), list(outs[n:])


def _ffn_fwd(h, nw, w_up, conv_w, conv_b, w_down, tag):
    a3 = _mm_fwd(h, w_up, norm_w=nw, name=f"ffn{tag}_up", out_dtype=BF16, halves=True, tm=1024, tn=2816)
    p = _ffn_conv_fwd3(a3, conv_w, conv_b.reshape(1, -1), name=f"ffn{tag}_conv")
    h_out = _mm_fwd(p, w_down, residual=h, name=f"ffn{tag}_down", tm=1024, tn=512)
    return h_out, (a3, p)


def _ffn_bwd(dh, h, saved, nw, w_up, conv_w, conv_b, w_down, tag):
    a3, p = saved
    g_down = _mm_tn(p, dh, name=f"ffn{tag}_down_wg", tk1=1408, tn=1024)
    dp = _mm_nt(dh, w_down, name=f"ffn{tag}_down_dg", out_dtype=BF16, tm=512, tn=2816, tk=1024)
    dhid3, dw3, db3 = _ffn_conv_bwd3(a3, conv_w, conv_b.reshape(1, -1), dp, name=f"ffn{tag}_conv_bwd")
    da3 = _conv_bwd_in3(dhid3, conv_w, K=FFN_CONV, name=f"ffn{tag}_conv_bwd_in")
    g_up = _mm_tn(h, da3, norm_w=nw, name=f"ffn{tag}_up_wg", tn=1408, tt=1024)
    dh_out, g_nw = _mm_nt(da3, w_up, epi=(h, nw, dh), name=f"ffn{tag}_up_dg", tm=1024, tk=1408)
    g_cw = jnp.concatenate([dw3[0], dw3[1]], axis=1)
    g_cb = jnp.concatenate([db3[0], db3[1]], axis=1)
    return dh_out, dict(norm=g_nw.reshape(-1), up=g_up, conv_w=g_cw, conv_b=g_cb.reshape(-1), down=g_down)


def _local_step(x, tgt, W):
    T = x.shape[0]
    f = {}
    zx = _mm_fwd(x, W["in_w"], norm_w=W["ssm_norm_w"], name="ssm_in", tm=1024, tn=896)
    xbc_c = _ssm_conv_fwd(zx, W["ssm_conv_w"], W["ssm_conv_b"].reshape(1, -1), name="ssm_conv")
    dt_raw = zx[:, D_INNER + CONV_DIM:IN_PROJ_DIM]
    dtg = jnp.pad(dt_raw.reshape(T, SSM_GROUPS, 8).transpose(1, 0, 2), ((0, 0), (0, 0), (0, 120)))
    par = jnp.stack([W["ssm_dt_bias"].reshape(SSM_GROUPS, 8), W["ssm_a_log"].reshape(SSM_GROUPS, 8),
                     W["ssm_d"].reshape(SSM_GROUPS, 8)], axis=1)
    par = jnp.pad(par, ((0, 0), (0, 5), (0, 120)))
    gnw = W["ssm_gate_norm_w"].reshape(1, D_INNER)
    y, yn, st = _ssd_fwd(xbc_c, zx, dtg, par, gnw, name="ssd_fwd")
    h1 = _mm_fwd(yn, W["ssm_out_w"], residual=x, name="ssm_out", tm=1024, tn=512)
    h2, ffn0 = _ffn_fwd(h1, W["ffn_norm_w"][0], W["ffn_up_w"][0], W["ffn_conv_w"][0], W["ffn_conv_b"][0],
                        W["ffn_down_w"][0], "0")
    q = _mm_fwd(h2, W["w_q"], norm_w=W["attn_norm_w"], out_dtype=BF16, name="attn_q", tm=1024, tn=1024)
    kv = _mm_fwd(h2, W["w_kv"], norm_w=W["kv_norm_w"], out_dtype=BF16, name="attn_kv", tm=1024, tn=1024)
    o, lt = _sba_fwd(q, kv, name="sba_fwd")
    h3 = _mm_fwd(o, W["w_o"], residual=h2, name="attn_o", tm=1024, tn=512)
    h4, ffn1 = _ffn_fwd(h3, W["ffn_norm_w"][1], W["ffn_up_w"][1], W["ffn_conv_w"][1], W["ffn_conv_b"][1],
                        W["ffn_down_w"][1], "1")
    loss, dh4, g_final = _loss_head(h4, tgt, W["final_norm_w"], name="loss_head")
    dh3, gf1 = _ffn_bwd(dh4, h3, ffn1, W["ffn_norm_w"][1], W["ffn_up_w"][1], W["ffn_conv_w"][1], W["ffn_conv_b"][1],
                        W["ffn_down_w"][1], "1")
    g_wo = _mm_tn(o, dh3, name="attn_o_wg", tn=1024)
    do = _mm_nt(dh3, W["w_o"], name="attn_o_dg", out_dtype=BF16, tn=1024, tk=1024)
    dq, dk, dv = _sba_bwd(q, kv, lt, do, name="sba_bwd")
    g_wq = _mm_tn(h2, dq, norm_w=W["attn_norm_w"], name="attn_q_wg", tn=1024)
    dh2a, g_attn_nw = _mm_nt(dq, W["w_q"], epi=(h2, W["attn_norm_w"], dh3), name="attn_q_dg", tk=1024)
    dkv = jnp.concatenate([dk, dv], axis=1)
    g_wkv = _mm_tn(h2, dkv, norm_w=W["kv_norm_w"], name="attn_kv_wg", tn=1024)
    dh2, g_kv_nw = _mm_nt(dkv, W["w_kv"], epi=(h2, W["kv_norm_w"], dh2a), name="attn_kv_dg", tk=1024)
    dh1, gf0 = _ffn_bwd(dh2, h1, ffn0, W["ffn_norm_w"][0], W["ffn_up_w"][0], W["ffn_conv_w"][0], W["ffn_conv_b"][0],
                        W["ffn_down_w"][0], "0")
    g_out = _mm_tn(yn, dh1, name="ssm_out_wg", tn=1024)
    dyn = _mm_nt(dh1, W["ssm_out_w"], name="ssm_out_dg", out_dtype=BF16, tn=1024, tk=1024)
    dxs, dB, dC, dz, ddt, g_gnw, dpar = _ssd_bwd(xbc_c, zx, dtg, par, gnw, y, st, dyn, name="ssd_bwd")
    dxbc_c = jnp.concatenate([dxs, dB, dC], axis=1)
    dhid, g_scw, g_scb = _ssm_conv_bwd_pre(zx, W["ssm_conv_w"], W["ssm_conv_b"].reshape(1, -1), dxbc_c,
                                           name="ssm_conv_bwd")
    dxbc = _conv_bwd_in(dhid, W["ssm_conv_w"], K=SSM_CONV, name="ssm_conv_bwd_in")
    ddt_t = ddt[:, :, :8].transpose(1, 0, 2).reshape(T, SSM_HEADS).astype(BF16)
    dzx = jnp.concatenate([dz, dxbc, jnp.pad(ddt_t, ((0, 0), (0, IN_PROJ_PAD - IN_PROJ_DIM)))], axis=1)
    g_in = _mm_tn(x, dzx, norm_w=W["ssm_norm_w"], name="ssm_in_wg", tn=896)
    dx, g_ssm_nw = _mm_nt(dzx, W["in_w"], epi=(x, W["ssm_norm_w"], dh1), name="ssm_in_dg", tk=1792)
    f["ssm_norm_w"] = g_ssm_nw.reshape(-1)
    f["ssm_in_w"] = g_in[:, :IN_PROJ_DIM]
    f["ssm_conv_w"] = g_scw
    f["ssm_conv_b"] = g_scb.reshape(-1)
    f["ssm_dt_bias"] = dpar[:, 0, :8].reshape(-1)
    f["ssm_a_log"] = dpar[:, 1, :8].reshape(-1)
    f["ssm_d"] = dpar[:, 2, :8].reshape(-1)
    f["ssm_gate_norm_w"] = g_gnw.reshape(-1)
    f["ssm_out_w"] = g_out
    f["kv_norm_w"] = g_kv_nw.reshape(-1)
    f["w_k"] = g_wkv[:, :D_MODEL]
    f["w_v"] = g_wkv[:, D_MODEL:]
    f["attn_norm_w"] = g_attn_nw.reshape(-1)
    f["w_q"] = g_wq
    f["w_o"] = g_wo
    f["ffn_norm_w"] = jnp.stack([gf0["norm"], gf1["norm"]])
    f["ffn_up_w"] = [gf0["up"], gf1["up"]]
    f["ffn_conv_w"] = jnp.stack([gf0["conv_w"], gf1["conv_w"]])
    f["ffn_conv_b"] = jnp.stack([gf0["conv_b"], gf1["conv_b"]])
    f["ffn_down_w"] = [gf0["down"], gf1["down"]]
    f["final_norm_w"] = g_final.reshape(-1)
    return loss, dx, f


_BIG = ["ssm_in_w", "ssm_out_w", "w_k", "w_v", "w_q", "w_o", "ffn_up_w", "ffn_down_w"]
_SMALL_SHARDED = ["ssm_norm_w", "ssm_conv_w", "ssm_conv_b", "ssm_gate_norm_w", "ffn_conv_w"]
_SMALL_REPL = ["ssm_dt_bias", "ssm_a_log", "ssm_d", "kv_norm_w", "attn_norm_w", "ffn_norm_w", "ffn_conv_b",
               "final_norm_w"]
_WEIGHTS = ["ssm_norm_w", "ssm_in_w", "ssm_conv_w", "ssm_conv_b", "ssm_dt_bias", "ssm_a_log", "ssm_d",
            "ssm_gate_norm_w", "ssm_out_w", "kv_norm_w", "w_k", "w_v", "attn_norm_w", "w_q", "w_o", "ffn_norm_w",
            "ffn_up_w", "ffn_conv_w", "ffn_conv_b", "ffn_down_w", "final_norm_w"]


def _as2d(a):
    return a.reshape(-1, a.shape[-1])


def _cols_to_full(g):
    return g.transpose(1, 0, 2).reshape(g.shape[1], N_DEV * g.shape[2])


def _full_to_cols(a):
    R = a.shape[0]
    return a.reshape(R, N_DEV, -1).transpose(1, 0, 2)


def _gather_weights(p):
    names = _BIG + _SMALL_SHARDED
    shards = [_as2d(p[n]).astype(BF16) for n in _BIG] + [_as2d(p[n]) for n in _SMALL_SHARDED]
    got = dict(zip(names, _all_gather(shards, name="gather_weights")))
    W = {n: p[n] for n in _SMALL_REPL}
    in_w = _cols_to_full(got["ssm_in_w"])
    W["in_w"] = jnp.pad(in_w, ((0, 0), (0, IN_PROJ_PAD - IN_PROJ_DIM)))
    W["ssm_out_w"] = got["ssm_out_w"].reshape(D_INNER, D_MODEL)
    W["w_kv"] = jnp.concatenate([got["w_k"].reshape(D_MODEL, D_MODEL), got["w_v"].reshape(D_MODEL, D_MODEL)], axis=1)
    W["w_q"] = got["w_q"].reshape(D_MODEL, D_MODEL)
    W["w_o"] = got["w_o"].reshape(D_MODEL, D_MODEL)
    up = got["ffn_up_w"]
    W["ffn_up_w"] = [_cols_to_full(up[:, l * D_MODEL:(l + 1) * D_MODEL]) for l in range(2)]
    dn = got["ffn_down_w"]
    rs = D_FF // N_DEV
    W["ffn_down_w"] = [dn[:, l * rs:(l + 1) * rs].reshape(D_FF, D_MODEL) for l in range(2)]
    W["ssm_norm_w"] = got["ssm_norm_w"].reshape(D_MODEL)
    W["ssm_conv_w"] = _cols_to_full(got["ssm_conv_w"])
    W["ssm_conv_b"] = got["ssm_conv_b"].reshape(CONV_DIM)
    W["ssm_gate_norm_w"] = got["ssm_gate_norm_w"].reshape(D_INNER)
    fcw = _cols_to_full(got["ffn_conv_w"])
    W["ffn_conv_w"] = fcw.reshape(2, FFN_CONV, 2 * D_FF)
    for n in ("ssm_dt_bias", "ssm_a_log", "ssm_d", "attn_norm_w"):
        W[n] = W[n].reshape(-1)
    return W


def _big_grad_blocks(f):
    rs = D_FF // N_DEV
    return {
        "ssm_in_w": _full_to_cols(f["ssm_in_w"]),
        "ssm_out_w": f["ssm_out_w"].reshape(N_DEV, D_INNER // N_DEV, D_MODEL),
        "w_k": f["w_k"].reshape(N_DEV, D_MODEL // N_DEV, D_MODEL),
        "w_v": f["w_v"].reshape(N_DEV, D_MODEL // N_DEV, D_MODEL),
        "w_q": f["w_q"].reshape(N_DEV, D_MODEL // N_DEV, D_MODEL),
        "w_o": f["w_o"].reshape(N_DEV, D_MODEL // N_DEV, D_MODEL),
        "ffn_up_w": jnp.concatenate([_full_to_cols(g) for g in f["ffn_up_w"]], axis=1),
        "ffn_down_w": jnp.concatenate([g.reshape(N_DEV, rs, D_MODEL) for g in f["ffn_down_w"]], axis=1),
    }


def _pack_small(vals):
    flat = jnp.concatenate([v.reshape(-1).astype(F32) for v in vals])
    n = flat.shape[0]
    rows = -(-n // 1024) * 8
    return jnp.pad(flat, (0, rows * 128 - n)).reshape(rows, 128)


def _unpack_small(packed, shapes):
    flat = packed.reshape(-1)
    out, off = [], 0
    for s in shapes:
        n = math.prod(s)
        out.append(flat[off:off + n].reshape(s))
        off += n
    return out


def _kernel_v1(x, ssm_norm_w, ssm_in_w, ssm_conv_w, ssm_conv_b, ssm_dt_bias, ssm_a_log, ssm_d, ssm_gate_norm_w, ssm_out_w, kv_norm_w, w_k, w_v, attn_norm_w, w_q, w_o, ffn_norm_w, ffn_up_w, ffn_conv_w, ffn_conv_b, ffn_down_w, final_norm_w, loss_target, m_ssm_norm_w, m_ssm_in_w, m_ssm_conv_w, m_ssm_conv_b, m_ssm_dt_bias, m_ssm_a_log, m_ssm_d, m_ssm_gate_norm_w, m_ssm_out_w, m_kv_norm_w, m_w_k, m_w_v, m_attn_norm_w, m_w_q, m_w_o, m_ffn_norm_w, m_ffn_up_w, m_ffn_conv_w, m_ffn_conv_b, m_ffn_down_w, m_final_norm_w, v_ssm_norm_w, v_ssm_in_w, v_ssm_conv_w, v_ssm_conv_b, v_ssm_dt_bias, v_ssm_a_log, v_ssm_d, v_ssm_gate_norm_w, v_ssm_out_w, v_kv_norm_w, v_w_k, v_w_v, v_attn_norm_w, v_w_q, v_w_o, v_ffn_norm_w, v_ffn_up_w, v_ffn_conv_w, v_ffn_conv_b, v_ffn_down_w, v_final_norm_w):
    env = dict(locals())
    p = {n: env[n] for n in _WEIGHTS}
    mom = {n: env["m_" + n] for n in _WEIGHTS}
    var = {n: env["v_" + n] for n in _WEIGHTS}
    T = x.shape[1]
    me = 4 * lax.axis_index("x") + 2 * lax.axis_index("y") + lax.axis_index("c")

    W = _gather_weights(p)
    loss_row, dx, f = _local_step(x.reshape(T, D_MODEL), loss_target.reshape(T, D_MODEL), W)
    loss = lax.psum(loss_row[0, 0], ("x", "y", "c"))

    big = _big_grad_blocks(f)
    small_names = _SMALL_REPL + _SMALL_SHARDED
    small_full = _pack_small([f[n] for n in small_names])
    small_bcast = jnp.broadcast_to(small_full[None], (N_DEV,) + small_full.shape)
    got = _exchange([big[n] for n in _BIG] + [small_bcast], name="exchange_grads")
    big_parts = dict(zip(_BIG, got[:-1]))

    zero = jnp.zeros_like(small_full)
    g_small_sum = _adamw(got[-1], zero, zero, zero, name="sum_small_grads", tr=small_full.shape[0])[0]
    full_shapes = [f[n].shape for n in small_names]
    g_small = dict(zip(small_names, _unpack_small(g_small_sum, full_shapes)))
    for n in _SMALL_SHARDED:
        width = p[n].shape[-1]
        g_small[n] = lax.dynamic_slice_in_dim(g_small[n], me * width, width, axis=g_small[n].ndim - 1)

    out_g, out_d, out_m, out_v = {}, {}, {}, {}
    for n in _BIG:
        w2, m2, v2 = _as2d(p[n]), _as2d(mom[n]), _as2d(var[n])
        tr = 352 if n == "ffn_down_w" else 256
        g, d, nm, nv = _adamw(big_parts[n], w2, m2, v2, name="adamw_" + n, tr=tr)
        out_g[n], out_d[n], out_m[n], out_v[n] = (t.reshape(p[n].shape) for t in (g, d, nm, nv))
    sw = _pack_small([p[n] for n in small_names])
    sm = _pack_small([mom[n] for n in small_names])
    sv = _pack_small([var[n] for n in small_names])
    sg = _pack_small([g_small[n] for n in small_names])
    _, d, nm, nv = _adamw(sg[None], sw, sm, sv, name="adamw_small", tr=sw.shape[0])
    shard_shapes = [p[n].shape for n in small_names]
    for n, dd, mm, vv in zip(small_names, _unpack_small(d, shard_shapes), _unpack_small(nm, shard_shapes),
                             _unpack_small(nv, shard_shapes)):
        out_g[n] = g_small[n].reshape(p[n].shape)
        out_d[n], out_m[n], out_v[n] = dd, mm, vv

    return (loss, dx.reshape(x.shape), *[out_g[n] for n in _WEIGHTS], *[out_d[n] for n in _WEIGHTS],
            *[out_m[n] for n in _WEIGHTS], *[out_v[n] for n in _WEIGHTS])


def _tie(a, token):
    return a + token[0, 0].astype(a.dtype)


def _local_step2(x, tgt, get_w, put_g):
    T = x.shape[0]
    Ws = get_w("ssm", None)
    fnw, fcw, fcb = Ws["ffn_norm_w"], Ws["ffn_conv_w"], Ws["ffn_conv_b"]
    zx = _mm_fwd(x, Ws["in_w"], norm_w=Ws["ssm_norm_w"], name="ssm_in", tm=1024, tn=1792)
    xbc_c = _ssm_conv_fwd(zx, Ws["ssm_conv_w"], Ws["ssm_conv_b"].reshape(1, -1), name="ssm_conv")
    dt_raw = zx[:, D_INNER + CONV_DIM:IN_PROJ_DIM]
    dtg = jnp.pad(dt_raw.reshape(T, SSM_GROUPS, 8).transpose(1, 0, 2), ((0, 0), (0, 0), (0, 120)))
    par = jnp.stack([Ws["ssm_dt_bias"].reshape(SSM_GROUPS, 8), Ws["ssm_a_log"].reshape(SSM_GROUPS, 8),
                     Ws["ssm_d"].reshape(SSM_GROUPS, 8)], axis=1)
    par = jnp.pad(par, ((0, 0), (0, 5), (0, 120)))
    gnw = _tie(Ws["ssm_gate_norm_w"].reshape(1, D_INNER), get_w("rest_start", xbc_c))
    y, yn, st = _ssd_fwd(xbc_c, zx, dtg, par, gnw, name="ssd_fwd")
    W0 = get_w("ffn0", y)
    Ws["ssm_out_w"] = W0["ssm_out_w"]
    h1 = _mm_fwd(yn, Ws["ssm_out_w"], residual=x, name="ssm_out", tm=1024, tn=512)
    h2, ffn0 = _ffn_fwd(h1, fnw[0], W0["up"], fcw[0], fcb[0], W0["down"], "0")
    Wr = get_w("rest", h2)
    q = _mm_fwd(h2, Wr["w_q"], norm_w=Ws["attn_norm_w"], out_dtype=BF16, name="attn_q", tm=1024, tn=1024)
    kv = _mm_fwd(h2, Wr["w_kv"], norm_w=Ws["kv_norm_w"], out_dtype=BF16, name="attn_kv", tm=1024, tn=1024)
    o, lt = _sba_fwd(q, kv, name="sba_fwd")
    h3 = _mm_fwd(o, Wr["w_o"], residual=h2, name="attn_o", tm=1024, tn=512)
    h4, ffn1 = _ffn_fwd(h3, fnw[1], Wr["up"], fcw[1], fcb[1], Wr["down"], "1")
    loss, dh4, g_final = _loss_head(h4, tgt, Ws["final_norm_w"], name="loss_head")
    dh3, gf1 = _ffn_bwd(dh4, h3, ffn1, fnw[1], Wr["up"], fcw[1], fcb[1], Wr["down"], "1")
    tok = put_g("ffn1", dict(up=gf1["up"], down=gf1["down"]))
    g_wo = _mm_tn(o, dh3, name="attn_o_wg", tn=1024)
    do = _mm_nt(dh3, _tie(Wr["w_o"], tok), name="attn_o_dg", out_dtype=BF16, tn=1024, tk=1024)
    dq, dk, dv = _sba_bwd(q, kv, lt, do, name="sba_bwd")
    g_wq = _mm_tn(h2, dq, norm_w=Ws["attn_norm_w"], name="attn_q_wg", tn=1024, tt=1024)
    dh2a, g_attn_nw = _mm_nt(dq, Wr["w_q"], epi=(h2, Ws["attn_norm_w"], dh3), name="attn_q_dg", tm=1024, tk=1024)
    dkv = jnp.concatenate([dk, dv], axis=1)
    g_wkv = _mm_tn(h2, dkv, norm_w=Ws["kv_norm_w"], name="attn_kv_wg", tn=1024, tt=1024)
    dh2, g_kv_nw = _mm_nt(dkv, Wr["w_kv"], epi=(h2, Ws["kv_norm_w"], dh2a), name="attn_kv_dg", tm=1024, tk=1024)
    tok = put_g("attn", dict(w_o=g_wo, w_q=g_wq, w_k=g_wkv[:, :D_MODEL], w_v=g_wkv[:, D_MODEL:]))
    dh1, gf0 = _ffn_bwd(dh2, h1, ffn0, fnw[0], W0["up"], fcw[0], _tie(fcb[0], tok), W0["down"], "0")
    tok = put_g("ffn0", dict(up=gf0["up"], down=gf0["down"]))
    g_out = _mm_tn(yn, dh1, name="ssm_out_wg", tn=1024)
    dyn = _mm_nt(dh1, _tie(Ws["ssm_out_w"], tok), name="ssm_out_dg", out_dtype=BF16, tn=1024, tk=1024)
    tok = put_g("ssm_out", dict(ssm_out_w=g_out))
    dxbc_c, dz, ddt, g_gnw, dpar = _ssd_bwd(xbc_c, zx, dtg, par, _tie(gnw, tok), y, st, dyn, name="ssd_bwd")
    dhid, g_scw, g_scb = _ssm_conv_bwd_pre(zx, Ws["ssm_conv_w"], Ws["ssm_conv_b"].reshape(1, -1), dxbc_c,
                                           name="ssm_conv_bwd")
    dxbc = _conv_bwd_in(dhid, Ws["ssm_conv_w"], K=SSM_CONV, name="ssm_conv_bwd_in")
    ddt_t = ddt[:, :, :8].transpose(1, 0, 2).reshape(T, SSM_HEADS).astype(BF16)
    dzx = jnp.concatenate([dz, dxbc, jnp.pad(ddt_t, ((0, 0), (0, IN_PROJ_PAD - IN_PROJ_DIM)))], axis=1)
    g_in = _mm_tn(x, dzx, norm_w=Ws["ssm_norm_w"], name="ssm_in_wg", tn=1792, tt=1024)
    tok = put_g("ssm_in", dict(ssm_in_w=g_in[:, :IN_PROJ_DIM]))
    dx, g_ssm_nw = _mm_nt(dzx, Ws["in_w"], epi=(x, _tie(Ws["ssm_norm_w"], tok), dh1), name="ssm_in_dg", tm=1024, tk=896)
    f = {
        "ssm_norm_w": g_ssm_nw.reshape(-1), "ssm_conv_w": g_scw,
        "ssm_conv_b": g_scb.reshape(-1), "ssm_dt_bias": dpar[:, 0, :8].reshape(-1),
        "ssm_a_log": dpar[:, 1, :8].reshape(-1), "ssm_d": dpar[:, 2, :8].reshape(-1),
        "ssm_gate_norm_w": g_gnw.reshape(-1), "kv_norm_w": g_kv_nw.reshape(-1), "attn_norm_w": g_attn_nw.reshape(-1),
        "ffn_norm_w": jnp.stack([gf0["norm"], gf1["norm"]]), "ffn_conv_w": jnp.stack([gf0["conv_w"], gf1["conv_w"]]),
        "ffn_conv_b": jnp.stack([gf0["conv_b"], gf1["conv_b"]]), "final_norm_w": g_final.reshape(-1),
    }
    return loss, dx, f


def kernel(x, ssm_norm_w, ssm_in_w, ssm_conv_w, ssm_conv_b, ssm_dt_bias, ssm_a_log, ssm_d, ssm_gate_norm_w, ssm_out_w, kv_norm_w, w_k, w_v, attn_norm_w, w_q, w_o, ffn_norm_w, ffn_up_w, ffn_conv_w, ffn_conv_b, ffn_down_w, final_norm_w, loss_target, m_ssm_norm_w, m_ssm_in_w, m_ssm_conv_w, m_ssm_conv_b, m_ssm_dt_bias, m_ssm_a_log, m_ssm_d, m_ssm_gate_norm_w, m_ssm_out_w, m_kv_norm_w, m_w_k, m_w_v, m_attn_norm_w, m_w_q, m_w_o, m_ffn_norm_w, m_ffn_up_w, m_ffn_conv_w, m_ffn_conv_b, m_ffn_down_w, m_final_norm_w, v_ssm_norm_w, v_ssm_in_w, v_ssm_conv_w, v_ssm_conv_b, v_ssm_dt_bias, v_ssm_a_log, v_ssm_d, v_ssm_gate_norm_w, v_ssm_out_w, v_kv_norm_w, v_w_k, v_w_v, v_attn_norm_w, v_w_q, v_w_o, v_ffn_norm_w, v_ffn_up_w, v_ffn_conv_w, v_ffn_conv_b, v_ffn_down_w, v_final_norm_w):
    env = dict(locals())
    p = {n: env[n] for n in _WEIGHTS}
    mom = {n: env["m_" + n] for n in _WEIGHTS}
    var = {n: env["v_" + n] for n in _WEIGHTS}
    T = x.shape[1]
    me = 4 * lax.axis_index("x") + 2 * lax.axis_index("y") + lax.axis_index("c")
    rs = D_FF // N_DEV

    def bf2(a):
        return _as2d(a).astype(BF16)

    def with_own(srcs, lands, scatter):
        out = []
        for s, l in zip(srcs, lands):
            own = lax.dynamic_index_in_dim(s, me, 0, keepdims=False) if scatter else s
            out.append(lax.dynamic_update_index_in_dim(l, own, me, 0))
        return out

    a_names = ["ssm_in_w"] + _SMALL_SHARDED
    got_a = dict(zip(a_names, _all_gather([bf2(p["ssm_in_w"])] + [_as2d(p[n]) for n in _SMALL_SHARDED],
                                          name="gather_ssm")))
    ffn0_names = ["ssm_out_w", "up0", "down0"]
    rest_names = ["w_q", "w_k", "w_v", "w_o", "up1", "down1"]
    shard = {"up0": bf2(p["ffn_up_w"][0]), "down0": bf2(p["ffn_down_w"][0]), "up1": bf2(p["ffn_up_w"][1]),
             "down1": bf2(p["ffn_down_w"][1]), "w_q": bf2(p["w_q"]), "w_k": bf2(p["w_k"]), "w_v": bf2(p["w_v"]),
             "w_o": bf2(p["w_o"]), "ssm_out_w": bf2(p["ssm_out_w"])}
    h_ffn0 = _push_start([shard[n] for n in ffn0_names], scatter=False, name="gather_ffn0_start")
    handles = {}

    def get_w(group, after):
        if group == "ssm":
            W = {n: p[n] for n in _SMALL_REPL}
            for n in ("ssm_dt_bias", "ssm_a_log", "ssm_d", "attn_norm_w"):
                W[n] = W[n].reshape(-1)
            W["in_w"] = jnp.pad(_cols_to_full(got_a["ssm_in_w"]), ((0, 0), (0, IN_PROJ_PAD - IN_PROJ_DIM)))
            W["ssm_norm_w"] = _tie(got_a["ssm_norm_w"].reshape(D_MODEL), h_ffn0["token"])
            W["ssm_conv_w"] = _cols_to_full(got_a["ssm_conv_w"])
            W["ssm_conv_b"] = got_a["ssm_conv_b"].reshape(CONV_DIM)
            W["ssm_gate_norm_w"] = got_a["ssm_gate_norm_w"].reshape(D_INNER)
            W["ffn_conv_w"] = _cols_to_full(got_a["ffn_conv_w"]).reshape(2, FFN_CONV, 2 * D_FF)
            return W
        if group == "rest_start":
            anchor = after[0, 0]
            first = shard[rest_names[0]] + (jnp.where(jnp.isfinite(anchor), anchor, 0.0) * 0.0).astype(BF16)
            handles["rest"] = _push_start([first] + [shard[n] for n in rest_names[1:]], scatter=False,
                                          name="gather_rest_start")
            return handles["rest"]["token"]
        if group == "ffn0":
            srcs, lands = _push_wait(h_ffn0, after, name="gather_ffn0_wait")
            out, up, down = with_own(srcs, lands, False)
            return dict(ssm_out_w=out.reshape(D_INNER, D_MODEL), up=_cols_to_full(up), down=down.reshape(D_FF, D_MODEL))
        srcs, lands = _push_wait(handles["rest"], after, name="gather_rest_wait")
        g = dict(zip(rest_names, with_own(srcs, lands, False)))
        sq = lambda a: a.reshape(D_MODEL, D_MODEL)
        return dict(w_q=sq(g["w_q"]), w_kv=jnp.concatenate([sq(g["w_k"]), sq(g["w_v"])], axis=1), w_o=sq(g["w_o"]),
                    up=_cols_to_full(g["up1"]), down=g["down1"].reshape(D_FF, D_MODEL))

    pending = []

    def put_g(group, g):
        if group in ("ffn0", "ffn1"):
            keys = [("ffn_up_w", int(group[-1])), ("ffn_down_w", int(group[-1]))]
            blocks = [_full_to_cols(g["up"]), g["down"].reshape(N_DEV, rs, D_MODEL)]
        elif group == "attn":
            keys = [(n, None) for n in ("w_o", "w_q", "w_k", "w_v")]
            blocks = [g[n].reshape(N_DEV, D_MODEL // N_DEV, D_MODEL) for n, _ in keys]
        elif group == "ssm_out":
            keys = [("ssm_out_w", None)]
            blocks = [g["ssm_out_w"].reshape(N_DEV, D_INNER // N_DEV, D_MODEL)]
        else:
            keys = [("ssm_in_w", None)]
            blocks = [_full_to_cols(g["ssm_in_w"])]
        h = _push_start(blocks, scatter=True, name=f"exchange_{group}_start")
        pending.append((group, keys, h))
        return h["token"]

    loss_row, dx, f = _local_step2(x.reshape(T, D_MODEL), loss_target.reshape(T, D_MODEL), get_w, put_g)

    small_names = _SMALL_REPL + _SMALL_SHARDED
    small_full = _pack_small([f[n] for n in small_names] + [loss_row[0, 0:1]])
    small_bcast = jnp.broadcast_to(small_full[None], (N_DEV,) + small_full.shape)
    h_small = _push_start([small_bcast], scatter=True, name="exchange_small_start")
    tok = h_small["token"]

    res = {}
    for group, keys, h in pending:
        srcs, lands = _push_wait(h, dx, name=f"exchange_{group}_wait")
        for (n, layer), parts in zip(keys, with_own(srcs, lands, True)):
            sel = (lambda a: a) if layer is None else (lambda a: a[layer])
            w2, m2, v2 = _as2d(sel(p[n])), _as2d(sel(mom[n])), _as2d(sel(var[n]))
            if not res:
                w2 = _tie(w2, tok)
            tr = rs if n == "ffn_down_w" else 256
            res[(n, layer)] = _adamw(parts, w2, m2, v2, name=f"adamw_{n}" + ("" if layer is None else str(layer)), tr=tr)
    srcs, lands = _push_wait(h_small, res[("ssm_in_w", None)][0], name="exchange_small_wait")
    small_parts = with_own(srcs, lands, True)[0]
    out_g, out_d, out_m, out_v = {}, {}, {}, {}
    for n in _BIG:
        if (n, None) in res:
            quad = res[(n, None)]
        else:
            quad = [jnp.stack([res[(n, 0)][k], res[(n, 1)][k]]) for k in range(4)]
        out_g[n], out_d[n], out_m[n], out_v[n] = (t.reshape(p[n].shape) for t in quad)

    zero = jnp.zeros_like(small_full)
    g_small_sum = _adamw(small_parts, zero, zero, zero, name="sum_small_grads", tr=small_full.shape[0])[0]
    *small_sums, loss_sum = _unpack_small(g_small_sum, [f[n].shape for n in small_names] + [(1,)])
    loss = loss_sum[0]
    g_small = dict(zip(small_names, small_sums))
    for n in _SMALL_SHARDED:
        width = p[n].shape[-1]
        g_small[n] = lax.dynamic_slice_in_dim(g_small[n], me * width, width, axis=g_small[n].ndim - 1)
    sw = _pack_small([p[n] for n in small_names])
    sm = _pack_small([mom[n] for n in small_names])
    sv = _pack_small([var[n] for n in small_names])
    sg = _pack_small([g_small[n] for n in small_names])
    _, d, nm, nv = _adamw(sg[None], sw, sm, sv, name="adamw_small", tr=sw.shape[0])
    shard_shapes = [p[n].shape for n in small_names]
    for n, dd, mm, vv in zip(small_names, _unpack_small(d, shard_shapes), _unpack_small(nm, shard_shapes),
                             _unpack_small(nv, shard_shapes)):
        out_g[n] = g_small[n].reshape(p[n].shape)
        out_d[n], out_m[n], out_v[n] = dd, mm, vv

    return (loss, dx.reshape(x.shape), *[out_g[n] for n in _WEIGHTS], *[out_d[n] for n in _WEIGHTS],
            *[out_m[n] for n in _WEIGHTS], *[out_v[n] for n in _WEIGHTS])
```

```python
import functools
import math

import jax
import jax.numpy as jnp
from jax import lax
from jax.experimental import pallas as pl
from jax.experimental.pallas import tpu as pltpu

F32 = jnp.float32
BF16 = jnp.bfloat16
EPS = 1e-6

D_MODEL = 1024
D_INNER = 2048
SSM_HEADS = 32
SSM_GROUPS = 4
SSM_STATE = 128
SSM_CONV = 4
SSM_CHUNK = 128
GN = SSM_GROUPS * SSM_STATE
CONV_DIM = D_INNER + 2 * GN
IN_PROJ_DIM = D_INNER + CONV_DIM + SSM_HEADS
IN_PROJ_PAD = 5376
SB_HEADS = 16
SB_HEAD_DIM = 64
SB_BLOCK = 128
D_FF = 2816
FFN_CONV = 3
N_DEV = 8

ADAM_LR = 0.001
ADAM_B1 = 0.9
ADAM_B2 = 0.999
ADAM_EPS = 1e-08
ADAM_WD = 0.01
ADAM_STEP = 10

_MESH = pl.DeviceIdType.MESH
_NT = (((1,), (1,)), ((), ()))
_TN = (((0,), (0,)), ((), ()))
_ANY = pl.BlockSpec(memory_space=pl.ANY)


def _cparams(sem, vmem_mb=48):
    return pltpu.CompilerParams(dimension_semantics=sem, vmem_limit_bytes=vmem_mb * 1024 * 1024)


def _sigmoid(x):
    return 1.0 / (1.0 + jnp.exp(-x))


def _softplus(x):
    return jnp.maximum(x, 0.0) + jnp.log(1.0 + jnp.exp(-jnp.abs(x)))


def _rms_fwd(xv, w):
    r = lax.rsqrt(jnp.mean(xv * xv, axis=-1, keepdims=True) + EPS)
    return xv * r * w


def _mm_fwd(x, w, *, name, norm_w=None, residual=None, out_dtype=F32, tm=512, tn=512, halves=False):
    M, K = x.shape
    N = w.shape[1]
    tm, tn = min(tm, M), min(tn, N)
    assert M % tm == 0 and N % tn == 0, (name, M, N, tm, tn)
    if halves:
        nbh = N // 2 // tn
        assert N // 2 % tn == 0
        out_spec = pl.BlockSpec((None, tm, tn), lambda i, j: (lax.div(j, nbh), i, lax.rem(j, nbh)))
        out_shape = jax.ShapeDtypeStruct((2, M, N // 2), out_dtype)
    else:
        out_spec = pl.BlockSpec((tm, tn), lambda i, j: (i, j))
        out_shape = jax.ShapeDtypeStruct((M, N), out_dtype)
    has_norm, has_res = norm_w is not None, residual is not None

    def body(*refs):
        x_ref, w_ref = refs[0], refs[1]
        p = 2
        nw_ref = r_ref = None
        if has_norm:
            nw_ref = refs[p]
            p += 1
        if has_res:
            r_ref = refs[p]
            p += 1
        o_ref, xn_ref = refs[p], refs[p + 1]

        @pl.when(pl.program_id(1) == 0)
        def _():
            xv = x_ref[...].astype(F32)
            if has_norm:
                xv = _rms_fwd(xv, nw_ref[...])
            xn_ref[...] = xv.astype(BF16)

        acc = jnp.dot(xn_ref[...], w_ref[...], preferred_element_type=F32)
        if has_res:
            acc = acc + r_ref[...]
        o_ref[...] = acc.astype(out_dtype)

    in_specs = [pl.BlockSpec((tm, K), lambda i, j: (i, 0)), pl.BlockSpec((K, tn), lambda i, j: (0, j))]
    args = [x, w]
    if has_norm:
        in_specs.append(pl.BlockSpec((1, K), lambda i, j: (0, 0)))
        args.append(norm_w.reshape(1, K))
    if has_res:
        in_specs.append(pl.BlockSpec((tm, tn), lambda i, j: (i, j)))
        args.append(residual)
    return pl.pallas_call(
        body, name=name, grid=(M // tm, N // tn), in_specs=in_specs,
        out_specs=out_spec, out_shape=out_shape,
        scratch_shapes=[pltpu.VMEM((tm, K), BF16)],
        compiler_params=_cparams(("parallel", "arbitrary")))(*args)


def _mm_nt(dy, w, *, name, epi=None, out_dtype=F32, tm=512, tn=512, tk=512):
    halves = dy.ndim == 3
    M, K = (dy.shape[1], 2 * dy.shape[2]) if halves else dy.shape
    N = w.shape[0]
    tm, tk = min(tm, M), min(tk, K)
    tn = N if epi is not None else min(tn, N)
    assert M % tm == 0 and N % tn == 0 and K % tk == 0, (name, M, N, K, tm, tn, tk)
    nk = K // tk
    has_epi = epi is not None

    def body(*refs):
        if has_epi:
            dy_ref, w_ref, h_ref, nw_ref, r_ref, o_ref, dnw_ref, acc_ref = refs
        else:
            dy_ref, w_ref, o_ref, acc_ref = refs
        i = pl.program_id(0)
        k = pl.program_id(2)

        @pl.when(k == 0)
        def _():
            acc_ref[...] = jnp.zeros_like(acc_ref)

        acc_ref[...] += lax.dot_general(dy_ref[...].astype(BF16), w_ref[...], _NT, preferred_element_type=F32)

        @pl.when(k == nk - 1)
        def _():
            du = acc_ref[...]
            if has_epi:
                hv = h_ref[...]
                r = lax.rsqrt(jnp.mean(hv * hv, axis=-1, keepdims=True) + EPS)
                xhat = hv * r
                dxh = du * nw_ref[...]
                dx = r * (dxh - xhat * jnp.mean(dxh * xhat, axis=-1, keepdims=True))
                o_ref[...] = (r_ref[...] + dx).astype(out_dtype)
                contrib = jnp.sum(du * xhat, axis=0, keepdims=True)

                @pl.when(i == 0)
                def _():
                    dnw_ref[...] = contrib

                @pl.when(i > 0)
                def _():
                    dnw_ref[...] += contrib
            else:
                o_ref[...] = du.astype(out_dtype)

    if halves:
        nkh = K // 2 // tk
        assert K // 2 % tk == 0
        dy_spec = pl.BlockSpec((None, tm, tk), lambda i, j, k: (lax.div(k, nkh), i, lax.rem(k, nkh)))
    else:
        dy_spec = pl.BlockSpec((tm, tk), lambda i, j, k: (i, k))
    in_specs = [dy_spec, pl.BlockSpec((tn, tk), lambda i, j, k: (j, k))]
    args = [dy, w]
    out_specs = [pl.BlockSpec((tm, tn), lambda i, j, k: (i, j))]
    out_shape = [jax.ShapeDtypeStruct((M, N), out_dtype)]
    if has_epi:
        h, nw, res = epi
        in_specs += [pl.BlockSpec((tm, N), lambda i, j, k: (i, 0)), pl.BlockSpec((1, N), lambda i, j, k: (0, 0)),
                     pl.BlockSpec((tm, N), lambda i, j, k: (i, 0))]
        args += [h, nw.reshape(1, N), res]
        out_specs.append(pl.BlockSpec((1, N), lambda i, j, k: (0, 0)))
        out_shape.append(jax.ShapeDtypeStruct((1, N), F32))
    outs = pl.pallas_call(
        body, name=name, grid=(M // tm, N // tn, nk), in_specs=in_specs, out_specs=out_specs, out_shape=out_shape,
        scratch_shapes=[pltpu.VMEM((tm, tn), F32)],
        compiler_params=_cparams(("arbitrary", "arbitrary", "arbitrary")))(*args)
    return (outs[0], outs[1]) if has_epi else outs[0]


def _mm_tn(x, dy, *, name, norm_w=None, out_dtype=BF16, tk1=1024, tn=512, tt=512):
    T, K1 = x.shape
    halves = dy.ndim == 3
    N = 2 * dy.shape[2] if halves else dy.shape[1]
    tk1, tn, tt = min(tk1, K1), min(tn, N), min(tt, T)
    has_norm = norm_w is not None
    assert K1 % tk1 == 0 and N % tn == 0 and T % tt == 0, (name, K1, N, T, tk1, tn, tt)
    assert not has_norm or tk1 == K1
    nt = T // tt

    def body(*refs):
        if has_norm:
            x_ref, dy_ref, nw_ref, o_ref, acc_ref = refs
        else:
            x_ref, dy_ref, o_ref, acc_ref = refs
        t = pl.program_id(2)

        @pl.when(t == 0)
        def _():
            acc_ref[...] = jnp.zeros_like(acc_ref)

        xv = x_ref[...]
        if has_norm:
            xv = _rms_fwd(xv.astype(F32), nw_ref[...])
        acc_ref[...] += lax.dot_general(xv.astype(BF16), dy_ref[...].astype(BF16), _TN, preferred_element_type=F32)

        @pl.when(t == nt - 1)
        def _():
            o_ref[...] = acc_ref[...].astype(out_dtype)

    if halves:
        nbh = N // 2 // tn
        assert N // 2 % tn == 0
        dy_spec = pl.BlockSpec((None, tt, tn), lambda a, b, t: (lax.div(b, nbh), t, lax.rem(b, nbh)))
    else:
        dy_spec = pl.BlockSpec((tt, tn), lambda a, b, t: (t, b))
    in_specs = [pl.BlockSpec((tt, tk1), lambda a, b, t: (t, a)), dy_spec]
    args = [x, dy]
    if has_norm:
        in_specs.append(pl.BlockSpec((1, K1), lambda a, b, t: (0, 0)))
        args.append(norm_w.reshape(1, K1))
    return pl.pallas_call(
        body, name=name, grid=(K1 // tk1, N // tn, nt), in_specs=in_specs,
        out_specs=pl.BlockSpec((tk1, tn), lambda a, b, t: (a, b)),
        out_shape=jax.ShapeDtypeStruct((K1, N), out_dtype),
        scratch_shapes=[pltpu.VMEM((tk1, tn), F32)],
        compiler_params=_cparams(("parallel", "parallel", "arbitrary")))(*args)


def _shift_down(xb, prev8, j):
    main = pltpu.roll(xb, j, 0)
    head = pltpu.roll(xb[0:8], j, 0)
    ph = pltpu.roll(prev8, j, 0)
    row8 = lax.broadcasted_iota(jnp.int32, head.shape, 0)
    head = jnp.where(row8 < j, ph, head)
    return jnp.concatenate([head, main[8:]], axis=0)


def _shift_up(xb, next8, j):
    tt = xb.shape[0]
    main = pltpu.roll(xb, tt - j, 0)
    tail = pltpu.roll(xb[tt - 8:tt], 8 - j, 0)
    nh = pltpu.roll(next8, 8 - j, 0)
    row8 = lax.broadcasted_iota(jnp.int32, tail.shape, 0)
    tail = jnp.where(row8 + j >= 8, nh, tail)
    return jnp.concatenate([main[:tt - 8], tail], axis=0)


def _conv_hid(xb, prev8, w, b_row, K):
    out = b_row
    shifted = []
    for j in range(K):
        sh = K - 1 - j
        xs = xb if sh == 0 else _shift_down(xb, prev8, sh)
        shifted.append(xs)
        out = out + xs * w[j:j + 1, :]
    return out, shifted


def _prev_idx(i, nb8):
    return jnp.maximum(i * nb8 - 1, 0)


def _ssm_conv_fwd(zx, w, b, *, name, tt=512, tc=512):
    T = zx.shape[0]
    tt = min(tt, T)
    C, K = CONV_DIM, SSM_CONV
    cb0, nb8 = D_INNER // tc, tt // 8

    def body(x_ref, p_ref, w_ref, b_ref, o_ref):
        first = (pl.program_id(1) > 0).astype(F32)
        hid, _ = _conv_hid(x_ref[...], p_ref[...] * first, w_ref[...], b_ref[...], K)
        o_ref[...] = hid * _sigmoid(hid)

    return pl.pallas_call(
        body, name=name, grid=(C // tc, T // tt),
        in_specs=[pl.BlockSpec((tt, tc), lambda c, i: (i, c + cb0)),
                  pl.BlockSpec((8, tc), lambda c, i: (_prev_idx(i, nb8), c + cb0)),
                  pl.BlockSpec((K, tc), lambda c, i: (0, c)), pl.BlockSpec((1, tc), lambda c, i: (0, c))],
        out_specs=pl.BlockSpec((tt, tc), lambda c, i: (i, c)),
        out_shape=jax.ShapeDtypeStruct((T, C), F32),
        compiler_params=_cparams(("parallel", "parallel")))(zx, zx, w, b)


def _ssm_conv_bwd_pre(zx, w, b, dout, *, name, tt=512, tc=512):
    T = zx.shape[0]
    tt = min(tt, T)
    C, K = CONV_DIM, SSM_CONV
    cb0, nb8 = D_INNER // tc, tt // 8

    def body(x_ref, p_ref, w_ref, b_ref, d_ref, dh_ref, dw_ref, db_ref):
        t = pl.program_id(1)
        first = (t > 0).astype(F32)
        hid, shifted = _conv_hid(x_ref[...], p_ref[...] * first, w_ref[...], b_ref[...], K)
        sg = _sigmoid(hid)
        dh = d_ref[...] * (sg * (1.0 + hid * (1.0 - sg)))
        dh_ref[...] = dh

        @pl.when(t == 0)
        def _():
            dw_ref[...] = jnp.zeros_like(dw_ref)
            db_ref[...] = jnp.zeros_like(db_ref)

        db_ref[...] += jnp.sum(dh, axis=0, keepdims=True)
        for j in range(K):
            dw_ref[j:j + 1, :] += jnp.sum(dh * shifted[j], axis=0, keepdims=True)

    return pl.pallas_call(
        body, name=name, grid=(C // tc, T // tt),
        in_specs=[pl.BlockSpec((tt, tc), lambda c, i: (i, c + cb0)),
                  pl.BlockSpec((8, tc), lambda c, i: (_prev_idx(i, nb8), c + cb0)),
                  pl.BlockSpec((K, tc), lambda c, i: (0, c)), pl.BlockSpec((1, tc), lambda c, i: (0, c)),
                  pl.BlockSpec((tt, tc), lambda c, i: (i, c))],
        out_specs=[pl.BlockSpec((tt, tc), lambda c, i: (i, c)), pl.BlockSpec((K, tc), lambda c, i: (0, c)),
                   pl.BlockSpec((1, tc), lambda c, i: (0, c))],
        out_shape=[jax.ShapeDtypeStruct((T, C), F32), jax.ShapeDtypeStruct((K, C), F32),
                   jax.ShapeDtypeStruct((1, C), F32)],
        compiler_params=_cparams(("parallel", "arbitrary")))(zx, zx, w, b, dout)


def _conv_bwd_in(dh, w, *, name, K, tt=512, tc=512, out_dtype=BF16):
    T, C = dh.shape
    tt = min(tt, T)
    nb8, nT = tt // 8, T // tt
    last8 = T // 8 - 1

    def body(d_ref, n_ref, w_ref, o_ref):
        notlast = (pl.program_id(1) < nT - 1).astype(F32)
        d = d_ref[...]
        nxt = n_ref[...] * notlast
        w_ = w_ref[...]
        acc = d * w_[K - 1:K, :]
        for sh in range(1, K):
            acc = acc + _shift_up(d, nxt, sh) * w_[K - 1 - sh:K - sh, :]
        o_ref[...] = acc.astype(out_dtype)

    return pl.pallas_call(
        body, name=name, grid=(C // tc, nT),
        in_specs=[pl.BlockSpec((tt, tc), lambda c, i: (i, c)),
                  pl.BlockSpec((8, tc), lambda c, i: (jnp.minimum((i + 1) * nb8, last8), c)),
                  pl.BlockSpec((K, tc), lambda c, i: (0, c))],
        out_specs=pl.BlockSpec((tt, tc), lambda c, i: (i, c)),
        out_shape=jax.ShapeDtypeStruct((T, C), out_dtype),
        compiler_params=_cparams(("parallel", "parallel")))(dh, dh, w)


def _ffn_conv_fwd(a, w, b, *, name, tt=256, tc=1408):
    T = a.shape[0]
    tt = min(tt, T)
    K, nbh, nb8 = FFN_CONV, D_FF // tc, tt // 8

    def body(ag_ref, pg_ref, av_ref, pv_ref, wg_ref, wv_ref, bg_ref, bv_ref, o_ref):
        first = (pl.program_id(1) > 0).astype(F32)
        hg, _ = _conv_hid(ag_ref[...], pg_ref[...] * first, wg_ref[...], bg_ref[...], K)
        hv, _ = _conv_hid(av_ref[...], pv_ref[...] * first, wv_ref[...], bv_ref[...], K)
        o_ref[...] = (hg * _sigmoid(hg) * hv).astype(BF16)

    return pl.pallas_call(
        body, name=name, grid=(nbh, T // tt),
        in_specs=[pl.BlockSpec((tt, tc), lambda c, i: (i, c)),
                  pl.BlockSpec((8, tc), lambda c, i: (_prev_idx(i, nb8), c)),
                  pl.BlockSpec((tt, tc), lambda c, i: (i, c + nbh)),
                  pl.BlockSpec((8, tc), lambda c, i: (_prev_idx(i, nb8), c + nbh)),
                  pl.BlockSpec((K, tc), lambda c, i: (0, c)), pl.BlockSpec((K, tc), lambda c, i: (0, c + nbh)),
                  pl.BlockSpec((1, tc), lambda c, i: (0, c)), pl.BlockSpec((1, tc), lambda c, i: (0, c + nbh))],
        out_specs=pl.BlockSpec((tt, tc), lambda c, i: (i, c)),
        out_shape=jax.ShapeDtypeStruct((T, D_FF), BF16),
        compiler_params=_cparams(("parallel", "parallel")))(a, a, a, a, w, w, b, b)


def _ffn_conv_bwd_pre(a, w, b, dp, *, name, tt=256, tc=1408):
    T = a.shape[0]
    tt = min(tt, T)
    K, nbh, nb8 = FFN_CONV, D_FF // tc, tt // 8

    def body(ao_ref, po_ref, ag_ref, pg_ref, av_ref, pv_ref, wg_ref, wv_ref, bg_ref, bv_ref, dp_ref,
             dh_ref, dw_ref, db_ref):
        j = pl.program_id(0)
        t = pl.program_id(1)
        first = (t > 0).astype(F32)
        hg, _ = _conv_hid(ag_ref[...], pg_ref[...] * first, wg_ref[...], bg_ref[...], K)
        hv, _ = _conv_hid(av_ref[...], pv_ref[...] * first, wv_ref[...], bv_ref[...], K)
        sg = _sigmoid(hg)
        d = dp_ref[...].astype(F32)
        is_gate = (j < nbh).astype(F32)
        dh = d * (is_gate * (hv * (sg * (1.0 + hg * (1.0 - sg)))) + (1.0 - is_gate) * (hg * sg))
        dh_ref[...] = dh
        xo = ao_ref[...]
        po = po_ref[...] * first

        @pl.when(t == 0)
        def _():
            dw_ref[...] = jnp.zeros_like(dw_ref)
            db_ref[...] = jnp.zeros_like(db_ref)

        db_ref[...] += jnp.sum(dh, axis=0, keepdims=True)
        for jj in range(K):
            sh = K - 1 - jj
            xs = xo if sh == 0 else _shift_down(xo, po, sh)
            dw_ref[jj:jj + 1, :] += jnp.sum(dh * xs, axis=0, keepdims=True)

    def gi(c):
        return lax.rem(c, nbh)

    return pl.pallas_call(
        body, name=name, grid=(2 * nbh, T // tt),
        in_specs=[pl.BlockSpec((tt, tc), lambda c, i: (i, c)),
                  pl.BlockSpec((8, tc), lambda c, i: (_prev_idx(i, nb8), c)),
                  pl.BlockSpec((tt, tc), lambda c, i: (i, gi(c))),
                  pl.BlockSpec((8, tc), lambda c, i: (_prev_idx(i, nb8), gi(c))),
                  pl.BlockSpec((tt, tc), lambda c, i: (i, gi(c) + nbh)),
                  pl.BlockSpec((8, tc), lambda c, i: (_prev_idx(i, nb8), gi(c) + nbh)),
                  pl.BlockSpec((K, tc), lambda c, i: (0, gi(c))), pl.BlockSpec((K, tc), lambda c, i: (0, gi(c) + nbh)),
                  pl.BlockSpec((1, tc), lambda c, i: (0, gi(c))), pl.BlockSpec((1, tc), lambda c, i: (0, gi(c) + nbh)),
                  pl.BlockSpec((tt, tc), lambda c, i: (i, gi(c)))],
        out_specs=[pl.BlockSpec((tt, tc), lambda c, i: (i, c)), pl.BlockSpec((K, tc), lambda c, i: (0, c)),
                   pl.BlockSpec((1, tc), lambda c, i: (0, c))],
        out_shape=[jax.ShapeDtypeStruct((T, 2 * D_FF), F32), jax.ShapeDtypeStruct((K, 2 * D_FF), F32),
                   jax.ShapeDtypeStruct((1, 2 * D_FF), F32)],
        compiler_params=_cparams(("parallel", "arbitrary")))(a, a, a, a, a, a, w, w, b, b, dp)


def _ffn_conv_fwd3(a3, w, b, *, name, tt=256, tc=1408):
    T = a3.shape[1]
    tt = min(tt, T)
    K, nbh, n16 = FFN_CONV, D_FF // tc, tt // 16

    def body(a_ref, p_ref, wg_ref, wv_ref, bg_ref, bv_ref, o_ref):
        first = (pl.program_id(1) > 0).astype(F32)
        a = a_ref[...].astype(F32)
        prev = p_ref[...].astype(F32)[:, 8:16, :] * first
        hg, _ = _conv_hid(a[0], prev[0], wg_ref[...], bg_ref[...], K)
        hv, _ = _conv_hid(a[1], prev[1], wv_ref[...], bv_ref[...], K)
        o_ref[...] = (hg * _sigmoid(hg) * hv).astype(BF16)

    return pl.pallas_call(
        body, name=name, grid=(nbh, T // tt),
        in_specs=[pl.BlockSpec((2, tt, tc), lambda c, i: (0, i, c)),
                  pl.BlockSpec((2, 16, tc), lambda c, i: (0, _prev_idx(i, n16), c)),
                  pl.BlockSpec((K, tc), lambda c, i: (0, c)), pl.BlockSpec((K, tc), lambda c, i: (0, c + nbh)),
                  pl.BlockSpec((1, tc), lambda c, i: (0, c)), pl.BlockSpec((1, tc), lambda c, i: (0, c + nbh))],
        out_specs=pl.BlockSpec((tt, tc), lambda c, i: (i, c)),
        out_shape=jax.ShapeDtypeStruct((T, D_FF), BF16),
        compiler_params=_cparams(("parallel", "parallel")))(a3, a3, w, w, b, b)


def _ffn_conv_bwd3(a3, w, b, dp, *, name, tt=256, tc=1408):
    T = a3.shape[1]
    tt = min(tt, T)
    K, nbh, n16 = FFN_CONV, D_FF // tc, tt // 16

    def body(a_ref, p_ref, wg_ref, wv_ref, bg_ref, bv_ref, dp_ref, dh_ref, dw_ref, db_ref):
        t = pl.program_id(1)
        first = (t > 0).astype(F32)
        a = a_ref[...].astype(F32)
        prev = p_ref[...].astype(F32)[:, 8:16, :] * first
        hg, sh_g = _conv_hid(a[0], prev[0], wg_ref[...], bg_ref[...], K)
        hv, sh_v = _conv_hid(a[1], prev[1], wv_ref[...], bv_ref[...], K)
        sg = _sigmoid(hg)
        d = dp_ref[...].astype(F32)
        dhg = d * hv * (sg * (1.0 + hg * (1.0 - sg)))
        dhv = d * (hg * sg)
        dh_ref[0] = dhg.astype(BF16)
        dh_ref[1] = dhv.astype(BF16)

        @pl.when(t == 0)
        def _():
            dw_ref[...] = jnp.zeros_like(dw_ref)
            db_ref[...] = jnp.zeros_like(db_ref)

        db_ref[0] += jnp.sum(dhg, axis=0, keepdims=True)
        db_ref[1] += jnp.sum(dhv, axis=0, keepdims=True)
        for j in range(K):
            dw_ref[0, j:j + 1, :] += jnp.sum(dhg * sh_g[j], axis=0, keepdims=True)
            dw_ref[1, j:j + 1, :] += jnp.sum(dhv * sh_v[j], axis=0, keepdims=True)

    return pl.pallas_call(
        body, name=name, grid=(nbh, T // tt),
        in_specs=[pl.BlockSpec((2, tt, tc), lambda c, i: (0, i, c)),
                  pl.BlockSpec((2, 16, tc), lambda c, i: (0, _prev_idx(i, n16), c)),
                  pl.BlockSpec((K, tc), lambda c, i: (0, c)), pl.BlockSpec((K, tc), lambda c, i: (0, c + nbh)),
                  pl.BlockSpec((1, tc), lambda c, i: (0, c)), pl.BlockSpec((1, tc), lambda c, i: (0, c + nbh)),
                  pl.BlockSpec((tt, tc), lambda c, i: (i, c))],
        out_specs=[pl.BlockSpec((2, tt, tc), lambda c, i: (0, i, c)), pl.BlockSpec((2, K, tc), lambda c, i: (0, 0, c)),
                   pl.BlockSpec((2, 1, tc), lambda c, i: (0, 0, c))],
        out_shape=[jax.ShapeDtypeStruct((2, T, D_FF), BF16), jax.ShapeDtypeStruct((2, K, D_FF), F32),
                   jax.ShapeDtypeStruct((2, 1, D_FF), F32)],
        compiler_params=_cparams(("parallel", "arbitrary")))(a3, a3, w, w, b, b, dp)


def _conv_bwd_in3(dh3, w, *, name, K, tt=256, tc=1408):
    H, T, C = dh3.shape
    tt = min(tt, T)
    nb, n16, nT = C // tc, tt // 16, T // tt
    last16 = T // 16 - 1

    def body(d_ref, n_ref, w_ref, o_ref):
        notlast = (pl.program_id(2) < nT - 1).astype(F32)
        d = d_ref[...].astype(F32)
        nxt = n_ref[...].astype(F32)[0:8, :] * notlast
        w_ = w_ref[...]
        acc = d * w_[K - 1:K, :]
        for sh in range(1, K):
            acc = acc + _shift_up(d, nxt, sh) * w_[K - 1 - sh:K - sh, :]
        o_ref[...] = acc.astype(BF16)

    return pl.pallas_call(
        body, name=name, grid=(H, nb, nT),
        in_specs=[pl.BlockSpec((None, tt, tc), lambda h, c, i: (h, i, c)),
                  pl.BlockSpec((None, 16, tc), lambda h, c, i: (h, jnp.minimum((i + 1) * n16, last16), c)),
                  pl.BlockSpec((K, tc), lambda h, c, i: (0, h * nb + c))],
        out_specs=pl.BlockSpec((None, tt, tc), lambda h, c, i: (h, i, c)),
        out_shape=jax.ShapeDtypeStruct((H, T, C), BF16),
        compiler_params=_cparams(("parallel", "parallel", "parallel")))(dh3, dh3, w)


def _cumsum_rows(x):
    L = x.shape[0]
    row = lax.broadcasted_iota(jnp.int32, x.shape, 0)
    k = 1
    while k < L:
        x = x + jnp.where(row >= k, pltpu.roll(x, k, 0), 0.0)
        k *= 2
    return x


def _rcumsum_rows(x):
    L = x.shape[0]
    row = lax.broadcasted_iota(jnp.int32, x.shape, 0)
    k = 1
    while k < L:
        x = x + jnp.where(row < L - k, pltpu.roll(x, L - k, 0), 0.0)
        k *= 2
    return x


def _split_terms(m, n):
    terms, rest = [], m
    for _ in range(n):
        t = rest.astype(BF16)
        terms.append(t)
        rest = rest - t.astype(F32)
    return jnp.concatenate(terms, axis=1)


def _select_dot(m, n_terms, n_out, cond):
    K = m.shape[1]
    k = lax.broadcasted_iota(jnp.int32, (K, n_out), 0)
    j = lax.broadcasted_iota(jnp.int32, (K, n_out), 1)
    sel = cond(k, j).astype(BF16)
    return jnp.dot(_split_terms(m, n_terms), jnp.concatenate([sel] * n_terms, axis=0), preferred_element_type=F32)


def _rowsum_mxu(m):
    return _select_dot(m, 2, 128, lambda k, j: k >= 0)


def _lane_block_sums(m, width):
    shift = width.bit_length() - 1
    return _select_dot(m, 2, 128, lambda k, j: j == jnp.right_shift(k, shift))


def _heads_to_pairs(m):
    return _select_dot(m, 3, 512, lambda k, j: k == jnp.right_shift(j, 6))


def _ssd_common(dt_ref, par_ref):
    par = par_ref[...]
    raw = dt_ref[...] + par[0:1, :]
    dt = _softplus(raw)
    a = -jnp.exp(par[1:2, :])
    cs = _cumsum_rows(dt * a)
    L = cs.shape[0]
    cs_last = cs[L - 1:L, :]
    return raw, dt, a, par[2:3, :], cs, cs.T, jnp.exp(cs), jnp.exp(cs_last - cs), jnp.exp(cs_last)


def _ssd_specs(nc, rev):
    L = SSM_CHUNK

    def ci(c):
        return nc - 1 - c if rev else c

    return [pl.BlockSpec((L, D_INNER), lambda c: (ci(c), 0)),
            pl.BlockSpec((L, GN), lambda c: (ci(c), D_INNER // GN)),
            pl.BlockSpec((L, GN), lambda c: (ci(c), D_INNER // GN + 1)),
            pl.BlockSpec((SSM_GROUPS, L, 128), lambda c: (0, ci(c), 0)),
            pl.BlockSpec((SSM_GROUPS, 8, 128), lambda c: (0, 0, 0)),
            pl.BlockSpec((L, D_INNER), lambda c: (ci(c), 0)),
            pl.BlockSpec((1, D_INNER), lambda c: (0, 0))], ci


def _round_robin(gens):
    live = list(gens)
    while live:
        nxt = []
        for gen in live:
            try:
                next(gen)
                nxt.append(gen)
            except StopIteration:
                pass
        live = nxt


def _group_views(g, wide, narrow, lead):
    return ([r.at[:, g * 512:(g + 1) * 512] for r in wide], [r.at[:, g * 128:(g + 1) * 128] for r in narrow],
            [r.at[g] for r in lead])


def _ssd_fwd(xbc_c, zx, dtg, par, gnw, *, name):
    T = xbc_c.shape[0]
    L = SSM_CHUNK
    nc = T // L
    in_specs, ci = _ssd_specs(nc, False)

    def body(xs_ref, b_ref, c_ref, dt_ref, par_ref, z_ref, gnw_ref, y_ref, yn_ref, st_ref, h_ref):
        @pl.when(pl.program_id(0) == 0)
        def _():
            h_ref[...] = jnp.zeros_like(h_ref)

        gens = []
        for g in range(SSM_GROUPS):
            (xs, z, gw, y, yn), (b, c), (dt, pr, st, h) = _group_views(
                g, [xs_ref, z_ref, gnw_ref, y_ref, yn_ref], [b_ref, c_ref], [dt_ref, par_ref, st_ref, h_ref])
            gens.append(group(xs, b, c, dt, pr, z, gw, y, yn, st, h))
        _round_robin(gens)

    def group(xs_ref, b_ref, c_ref, dt_ref, par_ref, z_ref, gnw_ref, y_ref, yn_ref, st_ref, h_ref):
        _, dt, _, dsk, cs, csT, ecs, eend, dec = _ssd_common(dt_ref, par_ref)
        Bb = b_ref[...].astype(BF16)
        Cb = c_ref[...].astype(BF16)
        G = lax.dot_general(Cb, Bb, _NT, preferred_element_type=F32)
        row = lax.broadcasted_iota(jnp.int32, (L, L), 0)
        col = lax.broadcasted_iota(jnp.int32, (L, L), 1)
        tril = col <= row
        lo = lax.broadcasted_iota(jnp.int32, (L, 128), 1) < 64
        lo1 = lax.broadcasted_iota(jnp.int32, (1, 128), 1) < 64
        dt_x, ecs_x, eend_x = (_heads_to_pairs(m) for m in (dt, ecs, eend))
        for pp in range(4):
            hA, hB = 2 * pp, 2 * pp + 1
            lanes = slice(pp * 128, (pp + 1) * 128)

            def sel1(m):
                return jnp.where(lo1, m[:, hA:hA + 1], m[:, hB:hB + 1])

            X = xs_ref[:, lanes]
            xd = X * dt_x[:, lanes]
            xdb = xd.astype(BF16)
            ys = []
            for h in (hA, hB):
                Lm = jnp.where(tril, jnp.exp(jnp.minimum(cs[:, h:h + 1] - csT[h:h + 1, :], 0.0)), 0.0)
                ys.append(jnp.dot((G * Lm).astype(BF16), xdb, preferred_element_type=F32))
                yield
            Hp = h_ref[pp]
            st_ref[pp] = Hp
            yoff = jnp.dot(Cb, Hp.astype(BF16), preferred_element_type=F32) * ecs_x[:, lanes]
            y_ref[:, lanes] = jnp.where(lo, ys[0], ys[1]) + yoff + sel1(dsk) * X
            S = lax.dot_general(Bb, (xd * eend_x[:, lanes]).astype(BF16), _TN, preferred_element_type=F32)
            h_ref[pp] = Hp * sel1(dec) + S
            yield
        zv = z_ref[...]
        yg = y_ref[...] * (zv * _sigmoid(zv))
        r = jnp.tile(lax.rsqrt(_rowsum_mxu(yg * yg) * (1.0 / 512) + EPS), (1, 4))
        yn_ref[...] = (yg * r * gnw_ref[...]).astype(BF16)

    return pl.pallas_call(
        body, name=name, grid=(nc,), in_specs=in_specs,
        out_specs=[pl.BlockSpec((L, D_INNER), lambda c: (c, 0)), pl.BlockSpec((L, D_INNER), lambda c: (c, 0)),
                   pl.BlockSpec((SSM_GROUPS, None, 4, 128, 128), lambda c: (0, c, 0, 0, 0))],
        out_shape=[jax.ShapeDtypeStruct((T, D_INNER), F32), jax.ShapeDtypeStruct((T, D_INNER), BF16),
                   jax.ShapeDtypeStruct((SSM_GROUPS, nc, 4, 128, 128), F32)],
        scratch_shapes=[pltpu.VMEM((SSM_GROUPS, 4, 128, 128), F32)],
        compiler_params=_cparams(("arbitrary",)))(xbc_c, xbc_c, xbc_c, dtg, par, zx, gnw)


def _ssd_bwd(xbc_c, zx, dtg, par, gnw, y, st, dyn, *, name):
    T = xbc_c.shape[0]
    L = SSM_CHUNK
    nc = T // L
    in_specs, ci = _ssd_specs(nc, True)
    in_specs += [pl.BlockSpec((L, D_INNER), lambda c: (ci(c), 0)),
                 pl.BlockSpec((SSM_GROUPS, None, 4, 128, 128), lambda c: (0, ci(c), 0, 0, 0)),
                 pl.BlockSpec((L, D_INNER), lambda c: (ci(c), 0))]

    def body(xs_ref, b_ref, c_ref, dt_ref, par_ref, z_ref, gnw_ref, y_ref, st_ref, dyn_ref,
             dxbc_ref, dz_ref, ddt_ref, dgnw_ref, dpar_ref, dh_ref):
        @pl.when(pl.program_id(0) == 0)
        def _():
            dh_ref[...] = jnp.zeros_like(dh_ref)
            dgnw_ref[...] = jnp.zeros_like(dgnw_ref)
            dpar_ref[...] = jnp.zeros_like(dpar_ref)

        dxs_ref = dxbc_ref.at[:, 0:D_INNER]
        db_ref = dxbc_ref.at[:, D_INNER:D_INNER + GN]
        dc_ref = dxbc_ref.at[:, D_INNER + GN:CONV_DIM]

        gens = []
        for g in range(SSM_GROUPS):
            (xs, z, gw, y, dyn, dxs, dz, dgw), (b, c, db, dc), (dt, pr, st, ddt, dpr, dh) = _group_views(
                g, [xs_ref, z_ref, gnw_ref, y_ref, dyn_ref, dxs_ref, dz_ref, dgnw_ref], [b_ref, c_ref, db_ref, dc_ref],
                [dt_ref, par_ref, st_ref, ddt_ref, dpar_ref, dh_ref])
            gens.append(group(xs, b, c, dt, pr, z, gw, y, st, dyn, dxs, db, dc, dz, ddt, dgw, dpr, dh))
        _round_robin(gens)

    def group(xs_ref, b_ref, c_ref, dt_ref, par_ref, z_ref, gnw_ref, y_ref, st_ref, dyn_ref,
              dxs_ref, db_ref, dc_ref, dz_ref, ddt_ref, dgnw_ref, dpar_ref, dh_ref):
        yv = y_ref[...]
        zv = z_ref[...]
        sg = _sigmoid(zv)
        sz = zv * sg
        yg = yv * sz
        r = jnp.tile(lax.rsqrt(_rowsum_mxu(yg * yg) * (1.0 / 512) + EPS), (1, 4))
        yh = yg * r
        dyn = dyn_ref[...].astype(F32)
        dgnw_ref[...] += jnp.sum(dyn * yh, axis=0, keepdims=True)
        dyh = dyn * gnw_ref[...]
        dyg = r * (dyh - yh * jnp.tile(_rowsum_mxu(dyh * yh) * (1.0 / 512), (1, 4)))
        dY_all = dyg * sz
        dz_ref[...] = (dyg * yv * (sg * (1.0 + zv * (1.0 - sg)))).astype(dz_ref.dtype)

        yield
        raw, dt, a, dsk, cs, csT, ecs, eend, dec = _ssd_common(dt_ref, par_ref)
        Bb = b_ref[...].astype(BF16)
        Cb = c_ref[...].astype(BF16)
        G = lax.dot_general(Cb, Bb, _NT, preferred_element_type=F32)
        row = lax.broadcasted_iota(jnp.int32, (L, L), 0)
        col = lax.broadcasted_iota(jnp.int32, (L, L), 1)
        tril = col <= row
        lo = lax.broadcasted_iota(jnp.int32, (L, 128), 1) < 64
        lane1 = lax.broadcasted_iota(jnp.int32, (1, 128), 1)
        lo1 = lane1 < 64
        rowl = lax.broadcasted_iota(jnp.int32, (L, 128), 0)
        dt_x, ecs_x, eend_x = (_heads_to_pairs(m) for m in (dt, ecs, eend))
        dG = jnp.zeros((L, L), F32)
        dB = jnp.zeros((L, SSM_STATE), F32)
        dC = jnp.zeros((L, SSM_STATE), F32)
        dcs_t = jnp.zeros((L, L), F32)
        tails = jnp.zeros((1, 128), F32)
        dD_row = jnp.zeros((1, 128), F32)
        v_parts, prod_parts = [], []

        def tot(m):
            return jnp.sum(jnp.sum(m, axis=0, keepdims=True), axis=1, keepdims=True)

        for pp in range(4):
            hA, hB = 2 * pp, 2 * pp + 1
            lanes = slice(pp * 128, (pp + 1) * 128)

            def sel1(m):
                return jnp.where(lo1, m[:, hA:hA + 1], m[:, hB:hB + 1])

            X = xs_ref[:, lanes]
            dY = dY_all[:, lanes]
            dtsel = dt_x[:, lanes]
            xd = X * dtsel
            xdb = xd.astype(BF16)
            dYb = dY.astype(BF16)
            Hp = st_ref[pp]
            Hb = Hp.astype(BF16)
            dHn = dh_ref[pp]
            dHb = dHn.astype(BF16)
            ecs_sel = ecs_x[:, lanes]
            eend_sel = eend_x[:, lanes]
            dxd_state = jnp.dot(Bb, dHb, preferred_element_type=F32) * eend_sel
            yoff = jnp.dot(Cb, Hb, preferred_element_type=F32) * ecs_sel
            dYe = (dY * ecs_sel).astype(BF16)
            dC = dC + lax.dot_general(dYe, Hb, _NT, preferred_element_type=F32)
            dB = dB + lax.dot_general((xd * eend_sel).astype(BF16), dHb, _NT, preferred_element_type=F32)
            dh_ref[pp] = dHn * sel1(dec) + lax.dot_general(Cb, dYe, _TN, preferred_element_type=F32)
            q = xd * dxd_state
            dyq = dY * yoff - q
            qcol = jnp.sum(q, axis=0, keepdims=True)
            hcol = jnp.sum(dHn * Hp, axis=0, keepdims=True)
            dxd_diag = []
            for h, msk, msk1 in ((hA, lo, lo1), (hB, jnp.logical_not(lo), jnp.logical_not(lo1))):
                Lm = jnp.where(tril, jnp.exp(jnp.minimum(cs[:, h:h + 1] - csT[h:h + 1, :], 0.0)), 0.0)
                M = G * Lm
                dxd_diag.append(lax.dot_general(M.astype(BF16), dYb, _TN, preferred_element_type=F32))
                dM = lax.dot_general(jnp.where(msk, dY, 0.0).astype(BF16), xdb, _NT, preferred_element_type=F32)
                dG = dG + dM * Lm
                W = dM * M
                dcs_t = dcs_t + jnp.where(row == h, jnp.sum(W, axis=0, keepdims=True), 0.0)
                v_parts.append(W + jnp.where(msk, dyq, 0.0))
                tail = (jnp.sum(jnp.where(msk1, qcol, 0.0), axis=1, keepdims=True)
                        + dec[:, h:h + 1] * jnp.sum(jnp.where(msk1, hcol, 0.0), axis=1, keepdims=True))
                tails = tails + jnp.where(lane1 == h, tail, 0.0)
                yield
            dxd = jnp.where(lo, dxd_diag[0], dxd_diag[1]) + dxd_state
            prod_parts.append(dxd * X)
            dxs_ref[:, lanes] = dxd * dtsel + sel1(dsk) * dY
            dyx = jnp.sum(dY * X, axis=0, keepdims=True)
            sA = jnp.sum(jnp.where(lo1, dyx, 0.0), axis=1, keepdims=True)
            sB = jnp.sum(dyx, axis=1, keepdims=True) - sA
            dD_row = dD_row + jnp.where(lane1 == hA, sA, 0.0) + jnp.where(lane1 == hB, sB, 0.0)
            yield
        dGb = dG.astype(BF16)
        db_ref[...] = dB + lax.dot_general(dGb, Cb, _TN, preferred_element_type=F32)
        dc_ref[...] = dC + jnp.dot(dGb, Bb, preferred_element_type=F32)
        dcs_mat = _lane_block_sums(jnp.concatenate(v_parts, axis=1), 128) + jnp.where(rowl == L - 1, tails, 0.0)
        ddt_mat = _lane_block_sums(jnp.concatenate(prod_parts, axis=1), 64)
        dad = _rcumsum_rows(dcs_mat - dcs_t.T)
        draw = (a * dad + ddt_mat) * _sigmoid(raw)
        ddt_ref[...] = draw
        dpar_ref[0:1, :] += jnp.sum(draw, axis=0, keepdims=True)
        dpar_ref[1:2, :] += jnp.sum(dt * dad, axis=0, keepdims=True) * a
        dpar_ref[2:3, :] += dD_row

    return pl.pallas_call(
        body, name=name, grid=(nc,), in_specs=in_specs,
        out_specs=[pl.BlockSpec((L, CONV_DIM), lambda c: (ci(c), 0)),
                   pl.BlockSpec((L, D_INNER), lambda c: (ci(c), 0)),
                   pl.BlockSpec((SSM_GROUPS, L, 128), lambda c: (0, ci(c), 0)),
                   pl.BlockSpec((1, D_INNER), lambda c: (0, 0)),
                   pl.BlockSpec((SSM_GROUPS, 8, 128), lambda c: (0, 0, 0))],
        out_shape=[jax.ShapeDtypeStruct((T, CONV_DIM), F32), jax.ShapeDtypeStruct((T, D_INNER), BF16),
                   jax.ShapeDtypeStruct((SSM_GROUPS, T, 128), F32), jax.ShapeDtypeStruct((1, D_INNER), F32),
                   jax.ShapeDtypeStruct((SSM_GROUPS, 8, 128), F32)],
        scratch_shapes=[pltpu.VMEM((SSM_GROUPS, 4, 128, 128), F32)],
        compiler_params=_cparams(("arbitrary",)))(xbc_c, xbc_c, xbc_c, dtg, par, zx, gnw, y, st, dyn)


SB_KEYS = 512
SB_SCAN = 256
SB_STRIP = 256


def _tri(width, cond):
    kk = lax.broadcasted_iota(jnp.int32, (width, width), 0)
    jj = lax.broadcasted_iota(jnp.int32, (width, width), 1)
    return cond(kk, jj).astype(BF16)


def _sba_diag_mask():
    Bq = SB_BLOCK
    rowi = lax.broadcasted_iota(jnp.int32, (2 * Bq, Bq), 0)
    return lax.broadcasted_iota(jnp.int32, (2 * Bq, Bq), 1) < jnp.where(rowi >= Bq, rowi - Bq, rowi)


_LOG2E = 1.4426950408889634


def _softplus2(z2):
    return jnp.maximum(z2, 0.0) + jnp.log2(1.0 + jnp.exp2(-jnp.abs(z2)))


def _sba_sub_fwd(zb, c, U, mask):
    z2 = zb * _LOG2E
    s = _softplus2(z2)
    if mask is not None:
        s = jnp.where(mask, s, 0.0)
    R = c + jnp.dot(s.astype(BF16), U, preferred_element_type=F32)
    A = jnp.exp2(z2 - s - R)
    if mask is not None:
        A = jnp.where(mask, A, 0.0)
    return A.astype(BF16), R[:, 0:1] + s[:, 0:1]


def _sba_sub_bwd(zb, dAb, Lt, pc, pe, Uincl, Uexcl, mask):
    last = zb.shape[1] - 1
    z2 = zb * _LOG2E
    s = _softplus2(z2)
    g = z2 - s
    if mask is not None:
        s = jnp.where(mask, s, 0.0)
    P = pc + jnp.dot(s.astype(BF16), Uincl, preferred_element_type=F32)
    A = jnp.exp2(g - (Lt - P))
    if mask is not None:
        A = jnp.where(mask, A, 0.0)
    E = dAb * A
    PE = pe + jnp.dot(E.astype(BF16), Uexcl, preferred_element_type=F32)
    dz = E - jnp.exp2(g) * (E + PE)
    if mask is not None:
        dz = jnp.where(mask, dz, 0.0)
    return (A.astype(BF16), dz.astype(BF16), P[:, last:last + 1], PE[:, last:last + 1] + E[:, last:last + 1])


def _stack_heads(v):
    lo = lax.broadcasted_iota(jnp.int32, v.shape, 1) < 64
    zero = jnp.zeros_like(v)
    return jnp.concatenate([jnp.where(lo, v, zero), jnp.where(lo, zero, v)], axis=0)


def _unstack_heads(v):
    lo = lax.broadcasted_iota(jnp.int32, (SB_BLOCK, 128), 1) < 64
    return jnp.where(lo, v[:SB_BLOCK], v[SB_BLOCK:])


def _sba_rows(a):
    return slice(2 * a * SB_BLOCK, 2 * (a + 1) * SB_BLOCK)


def _sba_diag_case(a, b):
    Bq = SB_BLOCK
    if b * SB_SCAN >= (a + 1) * Bq:
        return "skip"
    if (b + 1) * SB_SCAN <= a * Bq:
        return "full"
    rowi = lax.broadcasted_iota(jnp.int32, (2 * Bq, SB_SCAN), 0)
    qpos = a * Bq + jnp.where(rowi >= Bq, rowi - Bq, rowi)
    return b * SB_SCAN + lax.broadcasted_iota(jnp.int32, (2 * Bq, SB_SCAN), 1) < qpos


def _sba_fwd(q, kv, *, name):
    T = q.shape[0]
    Bq = SB_BLOCK
    nsub = SB_KEYS // Bq
    nscan = SB_KEYS // SB_SCAN
    R = 2 * SB_KEYS
    assert T % SB_KEYS == 0 and SB_STRIP == 2 * Bq
    scale = 1.0 / math.sqrt(SB_HEAD_DIM)

    def body(q_ref, k_ref, v_ref, o_ref, lt_ref, z_s, a_s, c_s, acc_s):
        i = pl.program_id(1)
        U2 = _tri(SB_SCAN, lambda k, j: k > j)
        qs_all = jnp.concatenate([_stack_heads(q_ref[a * Bq:(a + 1) * Bq, :] * scale) for a in range(nsub)], axis=0)
        c_s[...] = jnp.zeros_like(c_s)
        acc_s[...] = jnp.zeros_like(acc_s)

        def scores(J, slot):
            off = pl.multiple_of(J * SB_KEYS, SB_KEYS)
            z_s[slot] = lax.dot_general(qs_all, k_ref[pl.ds(off, SB_KEYS), :], _NT, preferred_element_type=F32)

        def weights(slot, diag):
            for a in range(nsub):
                rows = _sba_rows(a)
                c = c_s[rows, :]
                for b in reversed(range(nscan)):
                    cols = slice(b * SB_SCAN, (b + 1) * SB_SCAN)
                    case = _sba_diag_case(a, b) if diag else "full"
                    if isinstance(case, str) and case == "skip":
                        a_s[slot, rows, cols] = jnp.zeros((2 * Bq, SB_SCAN), BF16)
                        continue
                    A, c = _sba_sub_fwd(z_s[slot, rows, cols], c, U2, None if isinstance(case, str) else case)
                    a_s[slot, rows, cols] = A
                c_s[rows, :] = c

        def values(J, slot):
            off = pl.multiple_of(J * SB_KEYS, SB_KEYS)
            acc_s[...] += jnp.dot(a_s[slot], v_ref[pl.ds(off, SB_KEYS), :], preferred_element_type=F32)

        scores(i, 0)
        weights(0, True)
        scores(jnp.maximum(i - 1, 0), 1)

        def two_steps(u, _):
            t = 2 * u + 1
            weights(1, False)
            scores(jnp.maximum(i - t - 1, 0), 0)
            values(i - t + 1, 0)
            weights(0, False)
            scores(jnp.maximum(i - t - 2, 0), 1)
            values(i - t, 1)
            return 0

        lax.fori_loop(0, i // 2, two_steps, 0)
        odd = lax.rem(i, 2) == 1

        @pl.when(jnp.logical_not(odd))
        def _():
            values(0, 0)

        @pl.when(odd)
        def _():
            weights(1, False)
            values(1, 0)
            values(0, 1)
        for a in range(nsub):
            o_ref[a * Bq:(a + 1) * Bq, :] = _unstack_heads(acc_s[_sba_rows(a), :]).astype(BF16)
            lt_ref[a * Bq:(a + 1) * Bq, :] = _unstack_heads(jnp.broadcast_to(c_s[_sba_rows(a), :], (2 * Bq, 128)))

    return pl.pallas_call(
        body, name=name, grid=(SB_HEADS // 2, T // SB_KEYS),
        in_specs=[pl.BlockSpec((SB_KEYS, 128), lambda p, i: (i, p)), pl.BlockSpec((T, 128), lambda p, i: (0, p)),
                  pl.BlockSpec((T, 128), lambda p, i: (0, p + SB_HEADS // 2))],
        out_specs=[pl.BlockSpec((SB_KEYS, 128), lambda p, i: (i, p)),
                   pl.BlockSpec((None, SB_KEYS, 128), lambda p, i: (p, i, 0))],
        out_shape=[jax.ShapeDtypeStruct((T, D_MODEL), BF16), jax.ShapeDtypeStruct((SB_HEADS // 2, T, 128), F32)],
        scratch_shapes=[pltpu.VMEM((2, R, SB_KEYS), F32), pltpu.VMEM((2, R, SB_KEYS), BF16),
                        pltpu.VMEM((R, 1), F32), pltpu.VMEM((R, 128), F32)],
        compiler_params=_cparams(("parallel", "parallel")))(q, kv, kv)


def _sba_bwd(q, kv, lt, do, *, name):
    T = q.shape[0]
    Bq = SB_BLOCK
    nq = T // SB_KEYS
    nsub = SB_KEYS // Bq
    nscan = SB_KEYS // SB_SCAN
    R = 2 * SB_KEYS
    assert T % SB_KEYS == 0 and SB_STRIP == 2 * Bq
    scale = 1.0 / math.sqrt(SB_HEAD_DIM)

    def body(q_ref, k_ref, v_ref, lt_ref, do_ref, dq_ref, dk_ref, dv_ref, dk_acc, dv_acc,
             z_s, da_s, a_s, dz_s, pc_s, pe_s, lt_s, dq_s):
        i = pl.program_id(1)

        @pl.when(i == 0)
        def _():
            dk_acc[...] = jnp.zeros_like(dk_acc)
            dv_acc[...] = jnp.zeros_like(dv_acc)

        Uincl = _tri(SB_SCAN, lambda k, j: k <= j)
        Uexcl = _tri(SB_SCAN, lambda k, j: k < j)
        qs, dos = [], []
        for a in range(nsub):
            rows = slice(a * Bq, (a + 1) * Bq)
            qs.append(_stack_heads(q_ref[rows, :] * scale))
            dos.append(_stack_heads(do_ref[rows, :]))
            lt_s[_sba_rows(a), :] = jnp.concatenate([lt_ref[rows, 0:1], lt_ref[rows, 64:65]], axis=0)
        qs_all = jnp.concatenate(qs, axis=0)
        dos_all = jnp.concatenate(dos, axis=0)
        pc_s[...] = jnp.zeros_like(pc_s)
        pe_s[...] = jnp.zeros_like(pe_s)
        a_s[1] = jnp.zeros((R, SB_KEYS), BF16)
        dz_s[1] = jnp.zeros((R, SB_KEYS), BF16)

        def scores(J, slot):
            off = pl.multiple_of(J * SB_KEYS, SB_KEYS)
            z_s[slot] = lax.dot_general(qs_all, k_ref[pl.ds(off, SB_KEYS), :], _NT, preferred_element_type=F32)
            da_s[slot] = lax.dot_general(dos_all, v_ref[pl.ds(off, SB_KEYS), :], _NT, preferred_element_type=F32)

        def gradients(slot, diag):
            for a in range(nsub):
                rows = _sba_rows(a)
                pc, pe, Lt = pc_s[rows, :], pe_s[rows, :], lt_s[rows, :]
                for b in range(nscan):
                    cols = slice(b * SB_SCAN, (b + 1) * SB_SCAN)
                    case = _sba_diag_case(a, b) if diag else "full"
                    if isinstance(case, str) and case == "skip":
                        a_s[slot, rows, cols] = jnp.zeros((2 * Bq, SB_SCAN), BF16)
                        dz_s[slot, rows, cols] = jnp.zeros((2 * Bq, SB_SCAN), BF16)
                        continue
                    A, dz, pc, pe = _sba_sub_bwd(z_s[slot, rows, cols], da_s[slot, rows, cols], Lt, pc, pe, Uincl, Uexcl,
                                                 None if isinstance(case, str) else case)
                    a_s[slot, rows, cols] = A
                    dz_s[slot, rows, cols] = dz
                pc_s[rows, :] = pc
                pe_s[rows, :] = pe

        def products(J, slot):
            off = pl.multiple_of(J * SB_KEYS, SB_KEYS)
            dzt = dz_s[slot]
            dk_acc[pl.ds(off, SB_KEYS), :] += lax.dot_general(dzt, qs_all, _TN, preferred_element_type=F32)
            dv_acc[pl.ds(off, SB_KEYS), :] += lax.dot_general(a_s[slot], dos_all, _TN, preferred_element_type=F32)
            dq_s[...] += jnp.dot(dzt, k_ref[pl.ds(off, SB_KEYS), :], preferred_element_type=F32)

        dq_s[...] = jnp.zeros_like(dq_s)
        scores(0, 0)

        def two_steps(u, _):
            t = 2 * u
            gradients(0, False)
            scores(t + 1, 1)
            products(jnp.maximum(t - 1, 0), 1)
            gradients(1, False)
            scores(t + 2, 0)
            products(t, 0)
            return 0

        lax.fori_loop(0, i // 2, two_steps, 0)
        odd = lax.rem(i, 2) == 1

        @pl.when(jnp.logical_not(odd))
        def _():
            gradients(0, True)
            products(jnp.maximum(i - 1, 0), 1)
            products(i, 0)

        @pl.when(odd)
        def _():
            gradients(0, False)
            scores(i, 1)
            products(jnp.maximum(i - 2, 0), 1)
            gradients(1, True)
            products(i - 1, 0)
            products(i, 1)

        for a in range(nsub):
            dq_ref[a * Bq:(a + 1) * Bq, :] = (_unstack_heads(dq_s[_sba_rows(a), :]) * scale).astype(BF16)

        @pl.when(i == nq - 1)
        def _():
            dk_ref[...] = dk_acc[...].astype(BF16)
            dv_ref[...] = dv_acc[...].astype(BF16)

    return pl.pallas_call(
        body, name=name, grid=(SB_HEADS // 2, nq),
        in_specs=[pl.BlockSpec((SB_KEYS, 128), lambda p, i: (i, p)), pl.BlockSpec((T, 128), lambda p, i: (0, p)),
                  pl.BlockSpec((T, 128), lambda p, i: (0, p + SB_HEADS // 2)),
                  pl.BlockSpec((None, SB_KEYS, 128), lambda p, i: (p, i, 0)),
                  pl.BlockSpec((SB_KEYS, 128), lambda p, i: (i, p))],
        out_specs=[pl.BlockSpec((SB_KEYS, 128), lambda p, i: (i, p)), pl.BlockSpec((T, 128), lambda p, i: (0, p)),
                   pl.BlockSpec((T, 128), lambda p, i: (0, p))],
        out_shape=[jax.ShapeDtypeStruct((T, D_MODEL), BF16), jax.ShapeDtypeStruct((T, D_MODEL), BF16),
                   jax.ShapeDtypeStruct((T, D_MODEL), BF16)],
        scratch_shapes=[pltpu.VMEM((T, 128), F32), pltpu.VMEM((T, 128), F32),
                        pltpu.VMEM((2, R, SB_KEYS), F32), pltpu.VMEM((2, R, SB_KEYS), F32),
                        pltpu.VMEM((2, R, SB_KEYS), BF16), pltpu.VMEM((2, R, SB_KEYS), BF16),
                        pltpu.VMEM((R, 1), F32), pltpu.VMEM((R, 1), F32), pltpu.VMEM((R, 1), F32),
                        pltpu.VMEM((R, 128), F32)],
        compiler_params=_cparams(("parallel", "arbitrary")))(q, kv, kv, lt, do)


def _sba_fwd_old(q, kv, *, name):
    T = q.shape[0]
    Bq = SB_BLOCK
    nsub = SB_KEYS // Bq
    assert T % SB_KEYS == 0
    scale = 1.0 / math.sqrt(SB_HEAD_DIM)

    def body(q_ref, k_ref, v_ref, o_ref, lt_ref):
        I = pl.program_id(1)
        U1 = _tri(Bq, lambda k, j: k > j)
        U2 = _tri(SB_SCAN, lambda k, j: k > j)
        dmask = _sba_diag_mask()
        qs = [_stack_heads(q_ref[a * Bq:(a + 1) * Bq, :] * scale) for a in range(nsub)]
        cs, accs = [], []
        for a in range(nsub):
            c = jnp.zeros((2 * Bq, 1), F32)
            acc = jnp.zeros((2 * Bq, 128), F32)
            for b in range(a, -1, -1):
                off = pl.multiple_of(I * SB_KEYS + b * Bq, Bq)
                zb = lax.dot_general(qs[a], k_ref[pl.ds(off, Bq), :], _NT, preferred_element_type=F32)
                A, c = _sba_sub_fwd(zb, c, U1, dmask if b == a else None)
                acc = acc + jnp.dot(A, v_ref[pl.ds(off, Bq), :], preferred_element_type=F32)
            cs.append(c)
            accs.append(acc)
        qs_all = jnp.concatenate(qs, axis=0)

        def step(n, carry):
            c, acc = carry
            off = pl.multiple_of((I - 1 - n) * SB_KEYS, SB_KEYS)
            z = lax.dot_general(qs_all, k_ref[pl.ds(off, SB_KEYS), :], _NT, preferred_element_type=F32)
            parts = [None] * (SB_KEYS // SB_SCAN)
            for b in reversed(range(SB_KEYS // SB_SCAN)):
                parts[b], c = _sba_sub_fwd(z[:, b * SB_SCAN:(b + 1) * SB_SCAN], c, U2, None)
            return c, acc + jnp.dot(jnp.concatenate(parts, axis=1), v_ref[pl.ds(off, SB_KEYS), :],
                                    preferred_element_type=F32)

        c, acc = lax.fori_loop(0, I, step, (jnp.concatenate(cs, axis=0), jnp.concatenate(accs, axis=0)))
        for a in range(nsub):
            rows = slice(2 * a * Bq, 2 * (a + 1) * Bq)
            o_ref[a * Bq:(a + 1) * Bq, :] = _unstack_heads(acc[rows]).astype(BF16)
            lt_ref[a * Bq:(a + 1) * Bq, :] = _unstack_heads(jnp.broadcast_to(c[rows], (2 * Bq, 128)))

    return pl.pallas_call(
        body, name=name, grid=(SB_HEADS // 2, T // SB_KEYS),
        in_specs=[pl.BlockSpec((SB_KEYS, 128), lambda p, i: (i, p)), pl.BlockSpec((T, 128), lambda p, i: (0, p)),
                  pl.BlockSpec((T, 128), lambda p, i: (0, p + SB_HEADS // 2))],
        out_specs=[pl.BlockSpec((SB_KEYS, 128), lambda p, i: (i, p)),
                   pl.BlockSpec((None, SB_KEYS, 128), lambda p, i: (p, i, 0))],
        out_shape=[jax.ShapeDtypeStruct((T, D_MODEL), BF16), jax.ShapeDtypeStruct((SB_HEADS // 2, T, 128), F32)],
        compiler_params=_cparams(("parallel", "parallel")))(q, kv, kv)


def _sba_bwd_old(q, kv, lt, do, *, name):
    T = q.shape[0]
    Bq = SB_BLOCK
    nq = T // SB_KEYS
    nsub = SB_KEYS // Bq
    assert T % SB_KEYS == 0
    scale = 1.0 / math.sqrt(SB_HEAD_DIM)

    def body(q_ref, k_ref, v_ref, lt_ref, do_ref, dq_ref, dk_ref, dv_ref, dk_acc, dv_acc,
             z_s, da_s, a_s, dz_s, pc_s, pe_s, lt_s):
        i = pl.program_id(1)

        @pl.when(i == 0)
        def _():
            dk_acc[...] = jnp.zeros_like(dk_acc)
            dv_acc[...] = jnp.zeros_like(dv_acc)

        Uincl1 = _tri(Bq, lambda k, j: k <= j)
        Uexcl1 = _tri(Bq, lambda k, j: k < j)
        Uincl2 = _tri(SB_SCAN, lambda k, j: k <= j)
        Uexcl2 = _tri(SB_SCAN, lambda k, j: k < j)
        dmask = _sba_diag_mask()
        qs, dos, lts = [], [], []
        for a in range(nsub):
            rows = slice(a * Bq, (a + 1) * Bq)
            qs.append(_stack_heads(q_ref[rows, :] * scale))
            dos.append(_stack_heads(do_ref[rows, :]))
            lts.append(jnp.concatenate([lt_ref[rows, 0:1], lt_ref[rows, 64:65]], axis=0))
        qs_all = jnp.concatenate(qs, axis=0)
        dos_all = jnp.concatenate(dos, axis=0)
        lt_all = jnp.concatenate(lts, axis=0)

        R = 2 * nsub * Bq
        pc_s[...] = jnp.zeros_like(pc_s)
        pe_s[...] = jnp.zeros_like(pe_s)
        lt_s[...] = lt_all

        def scores(J, slot):
            off = pl.multiple_of(J * SB_KEYS, SB_KEYS)
            z_s[slot] = lax.dot_general(qs_all, k_ref[pl.ds(off, SB_KEYS), :], _NT, preferred_element_type=F32)
            da_s[slot] = lax.dot_general(dos_all, v_ref[pl.ds(off, SB_KEYS), :], _NT, preferred_element_type=F32)

        def elementwise(slot):
            for r in range(R // SB_STRIP):
                rows = slice(r * SB_STRIP, (r + 1) * SB_STRIP)
                pc, pe, Lt = pc_s[rows, :], pe_s[rows, :], lt_s[rows, :]
                for b in range(SB_KEYS // SB_SCAN):
                    cols = slice(b * SB_SCAN, (b + 1) * SB_SCAN)
                    A, dz, pc, pe = _sba_sub_bwd(z_s[slot, rows, cols], da_s[slot, rows, cols], Lt, pc, pe,
                                                 Uincl2, Uexcl2, None)
                    a_s[slot, rows, cols] = A
                    dz_s[slot, rows, cols] = dz
                pc_s[rows, :] = pc
                pe_s[rows, :] = pe

        def outputs(J, slot, dq_acc):
            off = pl.multiple_of(J * SB_KEYS, SB_KEYS)
            dzt = dz_s[slot]
            dk_acc[pl.ds(off, SB_KEYS), :] += lax.dot_general(dzt, qs_all, _TN, preferred_element_type=F32)
            dv_acc[pl.ds(off, SB_KEYS), :] += lax.dot_general(a_s[slot], dos_all, _TN, preferred_element_type=F32)
            return dq_acc + jnp.dot(dzt, k_ref[pl.ds(off, SB_KEYS), :], preferred_element_type=F32)

        a_s[1] = jnp.zeros((R, SB_KEYS), BF16)
        dz_s[1] = jnp.zeros((R, SB_KEYS), BF16)
        last = jnp.maximum(i - 1, 0)
        scores(0, 0)

        def step(J, dq_acc):
            slot = lax.rem(J, 2)
            elementwise(slot)
            scores(jnp.minimum(J + 1, last), 1 - slot)
            return outputs(jnp.maximum(J - 1, 0), 1 - slot, dq_acc)

        dq_acc = lax.fori_loop(0, i, step, jnp.zeros((R, 128), F32))
        dq_acc = outputs(last, lax.rem(i + 1, 2), dq_acc)
        pc, pe = pc_s[...], pe_s[...]
        for a in range(nsub):
            rows = slice(2 * a * Bq, 2 * (a + 1) * Bq)
            pca, pea, dqa = pc[rows], pe[rows], dq_acc[rows]
            for b in range(a + 1):
                off = pl.multiple_of(i * SB_KEYS + b * Bq, Bq)
                kb = k_ref[pl.ds(off, Bq), :]
                zb = lax.dot_general(qs[a], kb, _NT, preferred_element_type=F32)
                dAb = lax.dot_general(dos[a], v_ref[pl.ds(off, Bq), :], _NT, preferred_element_type=F32)
                A, dz, pca, pea = _sba_sub_bwd(zb, dAb, lts[a], pca, pea, Uincl1, Uexcl1, dmask if b == a else None)
                dqa = dqa + jnp.dot(dz, kb, preferred_element_type=F32)
                dk_acc[pl.ds(off, Bq), :] += lax.dot_general(dz, qs[a], _TN, preferred_element_type=F32)
                dv_acc[pl.ds(off, Bq), :] += lax.dot_general(A, dos[a], _TN, preferred_element_type=F32)
            dq_ref[a * Bq:(a + 1) * Bq, :] = (_unstack_heads(dqa) * scale).astype(BF16)

        @pl.when(i == nq - 1)
        def _():
            dk_ref[...] = dk_acc[...].astype(BF16)
            dv_ref[...] = dv_acc[...].astype(BF16)

    return pl.pallas_call(
        body, name=name, grid=(SB_HEADS // 2, nq),
        in_specs=[pl.BlockSpec((SB_KEYS, 128), lambda p, i: (i, p)), pl.BlockSpec((T, 128), lambda p, i: (0, p)),
                  pl.BlockSpec((T, 128), lambda p, i: (0, p + SB_HEADS // 2)),
                  pl.BlockSpec((None, SB_KEYS, 128), lambda p, i: (p, i, 0)),
                  pl.BlockSpec((SB_KEYS, 128), lambda p, i: (i, p))],
        out_specs=[pl.BlockSpec((SB_KEYS, 128), lambda p, i: (i, p)), pl.BlockSpec((T, 128), lambda p, i: (0, p)),
                   pl.BlockSpec((T, 128), lambda p, i: (0, p))],
        out_shape=[jax.ShapeDtypeStruct((T, D_MODEL), BF16), jax.ShapeDtypeStruct((T, D_MODEL), BF16),
                   jax.ShapeDtypeStruct((T, D_MODEL), BF16)],
        scratch_shapes=[pltpu.VMEM((T, 128), F32), pltpu.VMEM((T, 128), F32),
                        pltpu.VMEM((2, 2 * SB_KEYS, SB_KEYS), F32), pltpu.VMEM((2, 2 * SB_KEYS, SB_KEYS), F32),
                        pltpu.VMEM((2, 2 * SB_KEYS, SB_KEYS), BF16), pltpu.VMEM((2, 2 * SB_KEYS, SB_KEYS), BF16),
                        pltpu.VMEM((2 * SB_KEYS, 1), F32), pltpu.VMEM((2 * SB_KEYS, 1), F32),
                        pltpu.VMEM((2 * SB_KEYS, 1), F32)],
        compiler_params=_cparams(("parallel", "arbitrary")))(q, kv, kv, lt, do)


def _loss_head(h, tgt, w, *, name, tt=512):
    T, D = h.shape
    tt = min(tt, T)

    def body(h_ref, t_ref, w_ref, loss_ref, dh_ref, dw_ref):
        i = pl.program_id(0)
        hv = h_ref[...]
        wv = w_ref[...]
        r = lax.rsqrt(jnp.mean(hv * hv, axis=-1, keepdims=True) + EPS)
        xhat = hv * r
        err = xhat * wv - t_ref[...]
        part = 0.5 * jnp.sum(jnp.mean(err * err, axis=-1, keepdims=True), axis=0, keepdims=True)
        dy = err * (1.0 / D)
        dxh = dy * wv
        dh_ref[...] = r * (dxh - xhat * jnp.mean(dxh * xhat, axis=-1, keepdims=True))
        dwc = jnp.sum(dy * xhat, axis=0, keepdims=True)

        @pl.when(i == 0)
        def _():
            loss_ref[...] = jnp.broadcast_to(part, loss_ref.shape)
            dw_ref[...] = dwc

        @pl.when(i > 0)
        def _():
            loss_ref[...] += jnp.broadcast_to(part, loss_ref.shape)
            dw_ref[...] += dwc

    return pl.pallas_call(
        body, name=name, grid=(T // tt,),
        in_specs=[pl.BlockSpec((tt, D), lambda i: (i, 0)), pl.BlockSpec((tt, D), lambda i: (i, 0)),
                  pl.BlockSpec((1, D), lambda i: (0, 0))],
        out_specs=[pl.BlockSpec((1, 128), lambda i: (0, 0)), pl.BlockSpec((tt, D), lambda i: (i, 0)),
                   pl.BlockSpec((1, D), lambda i: (0, 0))],
        out_shape=[jax.ShapeDtypeStruct((1, 128), F32), jax.ShapeDtypeStruct((T, D), F32),
                   jax.ShapeDtypeStruct((1, D), F32)],
        compiler_params=_cparams(("arbitrary",)))(h, tgt, w.reshape(1, D))


def _adamw(parts, w, m, v, *, name, tr=256):
    plist = list(parts) if isinstance(parts, (list, tuple)) else [parts]
    P, _, C = plist[0].shape
    R = sum(a.shape[1] for a in plist)
    tr = min(tr, R)
    assert all(a.shape[1] % tr == 0 for a in plist), (name, R, tr)
    nbs = [a.shape[1] // tr for a in plist]
    offs = [sum(nbs[:l]) for l in range(len(nbs))]
    c1 = 1.0 - ADAM_B1 ** ADAM_STEP
    c2 = 1.0 - ADAM_B2 ** ADAM_STEP

    def body(*refs):
        p_refs = refs[:len(plist)]
        w_ref, m_ref, v_ref, g_ref, d_ref, nm_ref, nv_ref = refs[len(plist):]
        i = pl.program_id(0)
        g = None
        for l, p_ref in enumerate(p_refs):
            gl = p_ref[0].astype(F32)
            for k in range(1, P):
                gl = gl + p_ref[k].astype(F32)
            g = gl if g is None else jnp.where(i >= offs[l], gl, g)
        mn = ADAM_B1 * m_ref[...] + (1.0 - ADAM_B1) * g
        vn = ADAM_B2 * v_ref[...] + (1.0 - ADAM_B2) * (g * g)
        g_ref[...] = g
        nm_ref[...] = mn
        nv_ref[...] = vn
        d_ref[...] = -ADAM_LR * ((mn / c1) / (jnp.sqrt(vn / c2) + ADAM_EPS) + ADAM_WD * w_ref[...])

    spec = pl.BlockSpec((tr, C), lambda i: (i, 0))
    sds = jax.ShapeDtypeStruct((R, C), F32)
    return pl.pallas_call(
        body, name=name, grid=(R // tr,),
        in_specs=[pl.BlockSpec((P, tr, C), functools.partial(lambda i, o, n: (0, jnp.clip(i - o, 0, n - 1), 0), o=o, n=n))
                  for o, n in zip(offs, nbs)] + [spec, spec, spec],
        out_specs=[spec, spec, spec, spec], out_shape=[sds, sds, sds, sds],
        compiler_params=_cparams(("parallel",)))(*plist, w, m, v)


def _all_gather(shards, *, name):
    n = len(shards)

    def body(*refs):
        ins, outs = refs[:n], refs[n:2 * n]
        send_sems, recv_sems, local_sems = refs[2 * n:]
        x, y, c = lax.axis_index("x"), lax.axis_index("y"), lax.axis_index("c")
        me, sib = (x, y, c), (x, y, 1 - c)
        chips = [(1 - x, y), (x, 1 - y), (1 - x, 1 - y)]

        def slot(p):
            return 4 * p[0] + 2 * p[1] + p[2]

        def cp(a, k, block, to, src=None):
            dst = outs[a].at[slot(block)]
            return pltpu.make_async_remote_copy(src_ref=dst if src is None else src, dst_ref=dst,
                                                send_sem=send_sems.at[a, k], recv_sem=recv_sems.at[a, k],
                                                device_id=to, device_id_type=_MESH)

        mine = [pltpu.make_async_copy(ins[a], outs[a].at[slot(me)], local_sems.at[a]) for a in range(n)]
        for m in mine:
            m.start()
        first = []
        for a in range(n):
            first.append(cp(a, 0, me, sib, src=ins[a]))
            for j, chip in enumerate(chips):
                first.append(cp(a, 1 + j, me, (*chip, c), src=ins[a]))
        for f in first:
            f.start()
        passed = []
        for j, chip in enumerate(chips):
            for a in range(n):
                cp(a, 1 + j, (*chip, c), me).wait_recv()
                f = cp(a, 4 + j, (*chip, c), sib)
                f.start()
                passed.append(f)
        for a in range(n):
            cp(a, 0, sib, me).wait_recv()
            for j, chip in enumerate(chips):
                cp(a, 4 + j, (*chip, 1 - c), me).wait_recv()
        for f in first + passed:
            f.wait_send()
        for m in mine:
            m.wait()

    return pl.pallas_call(
        body, name=name, in_specs=[_ANY] * n, out_specs=[_ANY] * n,
        out_shape=[jax.ShapeDtypeStruct((N_DEV,) + s.shape, s.dtype) for s in shards],
        scratch_shapes=[pltpu.SemaphoreType.DMA((n, 7)), pltpu.SemaphoreType.DMA((n, 7)),
                        pltpu.SemaphoreType.DMA((n,))])(*shards)


def _exchange(blocks, *, name):
    n = len(blocks)

    def body(*refs):
        ins, outs = refs[:n], refs[n:2 * n]
        send_sems, recv_sems, local_sems = refs[2 * n:]
        x, y, c = lax.axis_index("x"), lax.axis_index("y"), lax.axis_index("c")
        me = 4 * x + 2 * y + c
        mine = [pltpu.make_async_copy(ins[a].at[me], outs[a].at[me], local_sems.at[a]) for a in range(n)]
        for m in mine:
            m.start()
        copies = []
        for r in range(1, N_DEV):
            rx, ry, rc = (r >> 2) & 1, (r >> 1) & 1, r & 1
            px, py, pc = (1 - x if rx else x), (1 - y if ry else y), (1 - c if rc else c)
            peer = 4 * px + 2 * py + pc
            for a in range(n):
                copies.append((pltpu.make_async_remote_copy(
                    src_ref=ins[a].at[peer], dst_ref=outs[a].at[me], send_sem=send_sems.at[a, r - 1],
                    recv_sem=recv_sems.at[a, r - 1], device_id=(px, py, pc), device_id_type=_MESH),
                    pltpu.make_async_remote_copy(
                    src_ref=ins[a].at[peer], dst_ref=outs[a].at[peer], send_sem=send_sems.at[a, r - 1],
                    recv_sem=recv_sems.at[a, r - 1], device_id=(px, py, pc), device_id_type=_MESH)))
        for snd, _ in copies:
            snd.start()
        for _, rcv in copies:
            rcv.wait_recv()
        for snd, _ in copies:
            snd.wait_send()
        for m in mine:
            m.wait()

    return pl.pallas_call(
        body, name=name, in_specs=[_ANY] * n, out_specs=[_ANY] * n,
        out_shape=[jax.ShapeDtypeStruct(b.shape, b.dtype) for b in blocks],
        scratch_shapes=[pltpu.SemaphoreType.DMA((n, 7)), pltpu.SemaphoreType.DMA((n, 7)),
                        pltpu.SemaphoreType.DMA((n,))])(*blocks)


_HBM = pl.BlockSpec(memory_space=pltpu.HBM)
_SEM = pl.BlockSpec(memory_space=pltpu.SEMAPHORE)
_EFFECT = pltpu.SideEffectType.DATAFLOW_SIDE_EFFECTING


def _peers():
    x, y, c = lax.axis_index("x"), lax.axis_index("y"), lax.axis_index("c")
    out = []
    for r in range(1, N_DEV):
        px = 1 - x if (r >> 2) & 1 else x
        py = 1 - y if (r >> 1) & 1 else y
        pc = 1 - c if r & 1 else c
        out.append(((px, py, pc), 4 * px + 2 * py + pc))
    return 4 * x + 2 * y + c, out


def _push_copy(src_ref, land_ref, send_sems, recv_sems, a, k, me, peer, peer_slot, scatter, arriving):
    src = src_ref.at[peer_slot] if scatter else src_ref
    return pltpu.make_async_remote_copy(
        src_ref=src, dst_ref=land_ref.at[peer_slot if arriving else me], send_sem=send_sems.at[a * (N_DEV - 1) + k],
        recv_sem=recv_sems.at[a * (N_DEV - 1) + k], device_id=peer, device_id_type=_MESH)


def _push_start(srcs, *, scatter, name):
    n = len(srcs)
    lands = [lax.empty(s.shape if scatter else (N_DEV,) + s.shape, s.dtype) for s in srcs]

    def body(*refs):
        src_refs, land_refs = refs[:n], refs[n:2 * n]
        send_sems, recv_sems = refs[2 * n], refs[2 * n + 1]
        token = refs[-1]
        me, peers = _peers()
        for k, (peer, slot) in enumerate(peers):
            for a in range(n):
                _push_copy(src_refs[a], land_refs[a], send_sems, recv_sems, a, k, me, peer, slot, scatter, False).start()
        token[...] = jnp.zeros_like(token)

    hbm = lambda a: pltpu.HBM(a.shape, a.dtype)
    outs = pl.pallas_call(
        body, name=name,
        out_shape=(pltpu.SemaphoreType.DMA((n * (N_DEV - 1),)), pltpu.SemaphoreType.DMA((n * (N_DEV - 1),)),
                   *[hbm(s) for s in srcs], *[hbm(l) for l in lands], jax.ShapeDtypeStruct((8, 128), F32)),
        in_specs=[_HBM] * (2 * n),
        out_specs=(_SEM, _SEM, *([_HBM] * (2 * n)), pl.BlockSpec(memory_space=pltpu.VMEM)),
        input_output_aliases={i: 2 + i for i in range(2 * n)},
        compiler_params=pltpu.CompilerParams(has_side_effects=_EFFECT),
    )(*[pltpu.with_memory_space_constraint(s, pltpu.HBM) for s in srcs],
      *[pltpu.with_memory_space_constraint(l, pltpu.HBM) for l in lands])
    return dict(send=outs[0], recv=outs[1], srcs=list(outs[2:2 + n]), lands=list(outs[2 + n:2 + 2 * n]),
                token=outs[-1], scatter=scatter, n=n)


def _push_wait(h, after, *, name):
    n, scatter = h["n"], h["scatter"]

    def body(*refs):
        src_refs, land_refs = refs[:n], refs[n:2 * n]
        send_sems, recv_sems = refs[2 * n], refs[2 * n + 1]
        me, peers = _peers()
        for k, (peer, slot) in enumerate(peers):
            for a in range(n):
                cp = _push_copy(src_refs[a], land_refs[a], send_sems, recv_sems, a, k, me, peer, slot, scatter, True)
                cp.wait_send()
                cp.wait_recv()

    hbm = lambda a: pltpu.HBM(a.shape, a.dtype)
    outs = pl.pallas_call(
        body, name=name,
        out_shape=(*[hbm(s) for s in h["srcs"]], *[hbm(l) for l in h["lands"]]),
        in_specs=[_HBM] * (2 * n) + [_SEM, _SEM, _ANY], out_specs=tuple([_HBM] * (2 * n)),
        input_output_aliases={i: i for i in range(2 * n)},
        compiler_params=pltpu.CompilerParams(has_side_effects=_EFFECT),
    )(*h["srcs"], *h["lands"], h["send"], h["recv"], after)
    return list(outs[:n]), list(outs[n:])


def _ffn_fwd(h, nw, w_up, conv_w, conv_b, w_down, tag):
    a3 = _mm_fwd(h, w_up, norm_w=nw, name=f"ffn{tag}_up", out_dtype=BF16, halves=True, tm=1024, tn=2816)
    p = _ffn_conv_fwd3(a3, conv_w, conv_b.reshape(1, -1), name=f"ffn{tag}_conv")
    h_out = _mm_fwd(p, w_down, residual=h, name=f"ffn{tag}_down", tm=1024, tn=512)
    return h_out, (a3, p)


def _ffn_bwd(dh, h, saved, nw, w_up, conv_w, conv_b, w_down, tag):
    a3, p = saved
    g_down = _mm_tn(p, dh, name=f"ffn{tag}_down_wg", tk1=1408, tn=1024)
    dp = _mm_nt(dh, w_down, name=f"ffn{tag}_down_dg", out_dtype=BF16, tm=512, tn=2816, tk=1024)
    dhid3, dw3, db3 = _ffn_conv_bwd3(a3, conv_w, conv_b.reshape(1, -1), dp, name=f"ffn{tag}_conv_bwd")
    da3 = _conv_bwd_in3(dhid3, conv_w, K=FFN_CONV, name=f"ffn{tag}_conv_bwd_in")
    g_up = _mm_tn(h, da3, norm_w=nw, name=f"ffn{tag}_up_wg", tn=2816, tt=1024)
    dh_out, g_nw = _mm_nt(da3, w_up, epi=(h, nw, dh), name=f"ffn{tag}_up_dg", tm=1024, tk=1408)
    g_cw = jnp.concatenate([dw3[0], dw3[1]], axis=1)
    g_cb = jnp.concatenate([db3[0], db3[1]], axis=1)
    return dh_out, dict(norm=g_nw.reshape(-1), up=g_up, conv_w=g_cw, conv_b=g_cb.reshape(-1), down=g_down)


def _local_step(x, tgt, W):
    T = x.shape[0]
    f = {}
    zx = _mm_fwd(x, W["in_w"], norm_w=W["ssm_norm_w"], name="ssm_in", tm=1024, tn=896)
    xbc_c = _ssm_conv_fwd(zx, W["ssm_conv_w"], W["ssm_conv_b"].reshape(1, -1), name="ssm_conv")
    dt_raw = zx[:, D_INNER + CONV_DIM:IN_PROJ_DIM]
    dtg = jnp.pad(dt_raw.reshape(T, SSM_GROUPS, 8).transpose(1, 0, 2), ((0, 0), (0, 0), (0, 120)))
    par = jnp.stack([W["ssm_dt_bias"].reshape(SSM_GROUPS, 8), W["ssm_a_log"].reshape(SSM_GROUPS, 8),
                     W["ssm_d"].reshape(SSM_GROUPS, 8)], axis=1)
    par = jnp.pad(par, ((0, 0), (0, 5), (0, 120)))
    gnw = W["ssm_gate_norm_w"].reshape(1, D_INNER)
    y, yn, st = _ssd_fwd(xbc_c, zx, dtg, par, gnw, name="ssd_fwd")
    h1 = _mm_fwd(yn, W["ssm_out_w"], residual=x, name="ssm_out", tm=1024, tn=512)
    h2, ffn0 = _ffn_fwd(h1, W["ffn_norm_w"][0], W["ffn_up_w"][0], W["ffn_conv_w"][0], W["ffn_conv_b"][0],
                        W["ffn_down_w"][0], "0")
    q = _mm_fwd(h2, W["w_q"], norm_w=W["attn_norm_w"], out_dtype=BF16, name="attn_q", tm=1024, tn=1024)
    kv = _mm_fwd(h2, W["w_kv"], norm_w=W["kv_norm_w"], out_dtype=BF16, name="attn_kv", tm=1024, tn=1024)
    o, lt = _sba_fwd(q, kv, name="sba_fwd")
    h3 = _mm_fwd(o, W["w_o"], residual=h2, name="attn_o", tm=1024, tn=512)
    h4, ffn1 = _ffn_fwd(h3, W["ffn_norm_w"][1], W["ffn_up_w"][1], W["ffn_conv_w"][1], W["ffn_conv_b"][1],
                        W["ffn_down_w"][1], "1")
    loss, dh4, g_final = _loss_head(h4, tgt, W["final_norm_w"], name="loss_head")
    dh3, gf1 = _ffn_bwd(dh4, h3, ffn1, W["ffn_norm_w"][1], W["ffn_up_w"][1], W["ffn_conv_w"][1], W["ffn_conv_b"][1],
                        W["ffn_down_w"][1], "1")
    g_wo = _mm_tn(o, dh3, name="attn_o_wg", tn=1024)
    do = _mm_nt(dh3, W["w_o"], name="attn_o_dg", out_dtype=BF16, tn=1024, tk=1024)
    dq, dk, dv = _sba_bwd(q, kv, lt, do, name="sba_bwd")
    g_wq = _mm_tn(h2, dq, norm_w=W["attn_norm_w"], name="attn_q_wg", tn=1024)
    dh2a, g_attn_nw = _mm_nt(dq, W["w_q"], epi=(h2, W["attn_norm_w"], dh3), name="attn_q_dg", tk=1024)
    dkv = jnp.concatenate([dk, dv], axis=1)
    g_wkv = _mm_tn(h2, dkv, norm_w=W["kv_norm_w"], name="attn_kv_wg", tn=1024)
    dh2, g_kv_nw = _mm_nt(dkv, W["w_kv"], epi=(h2, W["kv_norm_w"], dh2a), name="attn_kv_dg", tk=1024)
    dh1, gf0 = _ffn_bwd(dh2, h1, ffn0, W["ffn_norm_w"][0], W["ffn_up_w"][0], W["ffn_conv_w"][0], W["ffn_conv_b"][0],
                        W["ffn_down_w"][0], "0")
    g_out = _mm_tn(yn, dh1, name="ssm_out_wg", tn=1024)
    dyn = _mm_nt(dh1, W["ssm_out_w"], name="ssm_out_dg", out_dtype=BF16, tn=1024, tk=1024)
    dxs, dB, dC, dz, ddt, g_gnw, dpar = _ssd_bwd(xbc_c, zx, dtg, par, gnw, y, st, dyn, name="ssd_bwd")
    dxbc_c = jnp.concatenate([dxs, dB, dC], axis=1)
    dhid, g_scw, g_scb = _ssm_conv_bwd_pre(zx, W["ssm_conv_w"], W["ssm_conv_b"].reshape(1, -1), dxbc_c,
                                           name="ssm_conv_bwd")
    dxbc = _conv_bwd_in(dhid, W["ssm_conv_w"], K=SSM_CONV, name="ssm_conv_bwd_in")
    ddt_t = ddt[:, :, :8].transpose(1, 0, 2).reshape(T, SSM_HEADS).astype(BF16)
    dzx = jnp.concatenate([dz, dxbc, jnp.pad(ddt_t, ((0, 0), (0, IN_PROJ_PAD - IN_PROJ_DIM)))], axis=1)
    g_in = _mm_tn(x, dzx, norm_w=W["ssm_norm_w"], name="ssm_in_wg", tn=896)
    dx, g_ssm_nw = _mm_nt(dzx, W["in_w"], epi=(x, W["ssm_norm_w"], dh1), name="ssm_in_dg", tk=1792)
    f["ssm_norm_w"] = g_ssm_nw.reshape(-1)
    f["ssm_in_w"] = g_in[:, :IN_PROJ_DIM]
    f["ssm_conv_w"] = g_scw
    f["ssm_conv_b"] = g_scb.reshape(-1)
    f["ssm_dt_bias"] = dpar[:, 0, :8].reshape(-1)
    f["ssm_a_log"] = dpar[:, 1, :8].reshape(-1)
    f["ssm_d"] = dpar[:, 2, :8].reshape(-1)
    f["ssm_gate_norm_w"] = g_gnw.reshape(-1)
    f["ssm_out_w"] = g_out
    f["kv_norm_w"] = g_kv_nw.reshape(-1)
    f["w_k"] = g_wkv[:, :D_MODEL]
    f["w_v"] = g_wkv[:, D_MODEL:]
    f["attn_norm_w"] = g_attn_nw.reshape(-1)
    f["w_q"] = g_wq
    f["w_o"] = g_wo
    f["ffn_norm_w"] = jnp.stack([gf0["norm"], gf1["norm"]])
    f["ffn_up_w"] = [gf0["up"], gf1["up"]]
    f["ffn_conv_w"] = jnp.stack([gf0["conv_w"], gf1["conv_w"]])
    f["ffn_conv_b"] = jnp.stack([gf0["conv_b"], gf1["conv_b"]])
    f["ffn_down_w"] = [gf0["down"], gf1["down"]]
    f["final_norm_w"] = g_final.reshape(-1)
    return loss, dx, f


_BIG = ["ssm_in_w", "ssm_out_w", "w_k", "w_v", "w_q", "w_o", "ffn_up_w", "ffn_down_w"]
_SMALL_SHARDED = ["ssm_norm_w", "ssm_conv_w", "ssm_conv_b", "ssm_gate_norm_w", "ffn_conv_w"]
_SMALL_REPL = ["ssm_dt_bias", "ssm_a_log", "ssm_d", "kv_norm_w", "attn_norm_w", "ffn_norm_w", "ffn_conv_b",
               "final_norm_w"]
_WEIGHTS = ["ssm_norm_w", "ssm_in_w", "ssm_conv_w", "ssm_conv_b", "ssm_dt_bias", "ssm_a_log", "ssm_d",
            "ssm_gate_norm_w", "ssm_out_w", "kv_norm_w", "w_k", "w_v", "attn_norm_w", "w_q", "w_o", "ffn_norm_w",
            "ffn_up_w", "ffn_conv_w", "ffn_conv_b", "ffn_down_w", "final_norm_w"]


def _as2d(a):
    return a.reshape(-1, a.shape[-1])


def _cols_to_full(g):
    return g.transpose(1, 0, 2).reshape(g.shape[1], N_DEV * g.shape[2])


def _full_to_cols(a):
    R = a.shape[0]
    return a.reshape(R, N_DEV, -1).transpose(1, 0, 2)


def _gather_weights(p):
    names = _BIG + _SMALL_SHARDED
    shards = [_as2d(p[n]).astype(BF16) for n in _BIG] + [_as2d(p[n]) for n in _SMALL_SHARDED]
    got = dict(zip(names, _all_gather(shards, name="gather_weights")))
    W = {n: p[n] for n in _SMALL_REPL}
    in_w = _cols_to_full(got["ssm_in_w"])
    W["in_w"] = jnp.pad(in_w, ((0, 0), (0, IN_PROJ_PAD - IN_PROJ_DIM)))
    W["ssm_out_w"] = got["ssm_out_w"].reshape(D_INNER, D_MODEL)
    W["w_kv"] = jnp.concatenate([got["w_k"].reshape(D_MODEL, D_MODEL), got["w_v"].reshape(D_MODEL, D_MODEL)], axis=1)
    W["w_q"] = got["w_q"].reshape(D_MODEL, D_MODEL)
    W["w_o"] = got["w_o"].reshape(D_MODEL, D_MODEL)
    up = got["ffn_up_w"]
    W["ffn_up_w"] = [_cols_to_full(up[:, l * D_MODEL:(l + 1) * D_MODEL]) for l in range(2)]
    dn = got["ffn_down_w"]
    rs = D_FF // N_DEV
    W["ffn_down_w"] = [dn[:, l * rs:(l + 1) * rs].reshape(D_FF, D_MODEL) for l in range(2)]
    W["ssm_norm_w"] = got["ssm_norm_w"].reshape(D_MODEL)
    W["ssm_conv_w"] = _cols_to_full(got["ssm_conv_w"])
    W["ssm_conv_b"] = got["ssm_conv_b"].reshape(CONV_DIM)
    W["ssm_gate_norm_w"] = got["ssm_gate_norm_w"].reshape(D_INNER)
    fcw = _cols_to_full(got["ffn_conv_w"])
    W["ffn_conv_w"] = fcw.reshape(2, FFN_CONV, 2 * D_FF)
    for n in ("ssm_dt_bias", "ssm_a_log", "ssm_d", "attn_norm_w"):
        W[n] = W[n].reshape(-1)
    return W


def _big_grad_blocks(f):
    rs = D_FF // N_DEV
    return {
        "ssm_in_w": _full_to_cols(f["ssm_in_w"]),
        "ssm_out_w": f["ssm_out_w"].reshape(N_DEV, D_INNER // N_DEV, D_MODEL),
        "w_k": f["w_k"].reshape(N_DEV, D_MODEL // N_DEV, D_MODEL),
        "w_v": f["w_v"].reshape(N_DEV, D_MODEL // N_DEV, D_MODEL),
        "w_q": f["w_q"].reshape(N_DEV, D_MODEL // N_DEV, D_MODEL),
        "w_o": f["w_o"].reshape(N_DEV, D_MODEL // N_DEV, D_MODEL),
        "ffn_up_w": jnp.concatenate([_full_to_cols(g) for g in f["ffn_up_w"]], axis=1),
        "ffn_down_w": jnp.concatenate([g.reshape(N_DEV, rs, D_MODEL) for g in f["ffn_down_w"]], axis=1),
    }


def _pack_small(vals):
    flat = jnp.concatenate([v.reshape(-1).astype(F32) for v in vals])
    n = flat.shape[0]
    rows = -(-n // 1024) * 8
    return jnp.pad(flat, (0, rows * 128 - n)).reshape(rows, 128)


def _unpack_small(packed, shapes):
    flat = packed.reshape(-1)
    out, off = [], 0
    for s in shapes:
        n = math.prod(s)
        out.append(flat[off:off + n].reshape(s))
        off += n
    return out


def _kernel_v1(x, ssm_norm_w, ssm_in_w, ssm_conv_w, ssm_conv_b, ssm_dt_bias, ssm_a_log, ssm_d, ssm_gate_norm_w, ssm_out_w, kv_norm_w, w_k, w_v, attn_norm_w, w_q, w_o, ffn_norm_w, ffn_up_w, ffn_conv_w, ffn_conv_b, ffn_down_w, final_norm_w, loss_target, m_ssm_norm_w, m_ssm_in_w, m_ssm_conv_w, m_ssm_conv_b, m_ssm_dt_bias, m_ssm_a_log, m_ssm_d, m_ssm_gate_norm_w, m_ssm_out_w, m_kv_norm_w, m_w_k, m_w_v, m_attn_norm_w, m_w_q, m_w_o, m_ffn_norm_w, m_ffn_up_w, m_ffn_conv_w, m_ffn_conv_b, m_ffn_down_w, m_final_norm_w, v_ssm_norm_w, v_ssm_in_w, v_ssm_conv_w, v_ssm_conv_b, v_ssm_dt_bias, v_ssm_a_log, v_ssm_d, v_ssm_gate_norm_w, v_ssm_out_w, v_kv_norm_w, v_w_k, v_w_v, v_attn_norm_w, v_w_q, v_w_o, v_ffn_norm_w, v_ffn_up_w, v_ffn_conv_w, v_ffn_conv_b, v_ffn_down_w, v_final_norm_w):
    env = dict(locals())
    p = {n: env[n] for n in _WEIGHTS}
    mom = {n: env["m_" + n] for n in _WEIGHTS}
    var = {n: env["v_" + n] for n in _WEIGHTS}
    T = x.shape[1]
    me = 4 * lax.axis_index("x") + 2 * lax.axis_index("y") + lax.axis_index("c")

    W = _gather_weights(p)
    loss_row, dx, f = _local_step(x.reshape(T, D_MODEL), loss_target.reshape(T, D_MODEL), W)
    loss = lax.psum(loss_row[0, 0], ("x", "y", "c"))

    big = _big_grad_blocks(f)
    small_names = _SMALL_REPL + _SMALL_SHARDED
    small_full = _pack_small([f[n] for n in small_names])
    small_bcast = jnp.broadcast_to(small_full[None], (N_DEV,) + small_full.shape)
    got = _exchange([big[n] for n in _BIG] + [small_bcast], name="exchange_grads")
    big_parts = dict(zip(_BIG, got[:-1]))

    zero = jnp.zeros_like(small_full)
    g_small_sum = _adamw(got[-1], zero, zero, zero, name="sum_small_grads", tr=small_full.shape[0])[0]
    full_shapes = [f[n].shape for n in small_names]
    g_small = dict(zip(small_names, _unpack_small(g_small_sum, full_shapes)))
    for n in _SMALL_SHARDED:
        width = p[n].shape[-1]
        g_small[n] = lax.dynamic_slice_in_dim(g_small[n], me * width, width, axis=g_small[n].ndim - 1)

    out_g, out_d, out_m, out_v = {}, {}, {}, {}
    for n in _BIG:
        w2, m2, v2 = _as2d(p[n]), _as2d(mom[n]), _as2d(var[n])
        tr = 352 if n == "ffn_down_w" else 256
        g, d, nm, nv = _adamw(big_parts[n], w2, m2, v2, name="adamw_" + n, tr=tr)
        out_g[n], out_d[n], out_m[n], out_v[n] = (t.reshape(p[n].shape) for t in (g, d, nm, nv))
    sw = _pack_small([p[n] for n in small_names])
    sm = _pack_small([mom[n] for n in small_names])
    sv = _pack_small([var[n] for n in small_names])
    sg = _pack_small([g_small[n] for n in small_names])
    _, d, nm, nv = _adamw(sg[None], sw, sm, sv, name="adamw_small", tr=sw.shape[0])
    shard_shapes = [p[n].shape for n in small_names]
    for n, dd, mm, vv in zip(small_names, _unpack_small(d, shard_shapes), _unpack_small(nm, shard_shapes),
                             _unpack_small(nv, shard_shapes)):
        out_g[n] = g_small[n].reshape(p[n].shape)
        out_d[n], out_m[n], out_v[n] = dd, mm, vv

    return (loss, dx.reshape(x.shape), *[out_g[n] for n in _WEIGHTS], *[out_d[n] for n in _WEIGHTS],
            *[out_m[n] for n in _WEIGHTS], *[out_v[n] for n in _WEIGHTS])


def _tie(a, token):
    return a + token[0, 0].astype(a.dtype)


def _local_step2(x, tgt, get_w, put_g):
    T = x.shape[0]
    Ws = get_w("ssm", None)
    fnw, fcw, fcb = Ws["ffn_norm_w"], Ws["ffn_conv_w"], Ws["ffn_conv_b"]
    zx = _mm_fwd(x, Ws["in_w"], norm_w=Ws["ssm_norm_w"], name="ssm_in", tm=1024, tn=1792)
    xbc_c = _ssm_conv_fwd(zx, Ws["ssm_conv_w"], Ws["ssm_conv_b"].reshape(1, -1), name="ssm_conv")
    dt_raw = zx[:, D_INNER + CONV_DIM:IN_PROJ_DIM]
    dtg = jnp.pad(dt_raw.reshape(T, SSM_GROUPS, 8).transpose(1, 0, 2), ((0, 0), (0, 0), (0, 120)))
    par = jnp.stack([Ws["ssm_dt_bias"].reshape(SSM_GROUPS, 8), Ws["ssm_a_log"].reshape(SSM_GROUPS, 8),
                     Ws["ssm_d"].reshape(SSM_GROUPS, 8)], axis=1)
    par = jnp.pad(par, ((0, 0), (0, 5), (0, 120)))
    gnw = _tie(Ws["ssm_gate_norm_w"].reshape(1, D_INNER), get_w("rest_start", xbc_c))
    y, yn, st = _ssd_fwd(xbc_c, zx, dtg, par, gnw, name="ssd_fwd")
    W0 = get_w("ffn0", y)
    Ws["ssm_out_w"] = W0["ssm_out_w"]
    h1 = _mm_fwd(yn, Ws["ssm_out_w"], residual=x, name="ssm_out", tm=1024, tn=512)
    h2, ffn0 = _ffn_fwd(h1, fnw[0], W0["up"], fcw[0], fcb[0], W0["down"], "0")
    Wr = get_w("rest", h2)
    q = _mm_fwd(h2, Wr["w_q"], norm_w=Ws["attn_norm_w"], out_dtype=BF16, name="attn_q", tm=1024, tn=1024)
    kv = _mm_fwd(h2, Wr["w_kv"], norm_w=Ws["kv_norm_w"], out_dtype=BF16, name="attn_kv", tm=1024, tn=1024)
    o, lt = _sba_fwd(q, kv, name="sba_fwd")
    h3 = _mm_fwd(o, Wr["w_o"], residual=h2, name="attn_o", tm=1024, tn=512)
    h4, ffn1 = _ffn_fwd(h3, fnw[1], Wr["up"], fcw[1], fcb[1], Wr["down"], "1")
    loss, dh4, g_final = _loss_head(h4, tgt, Ws["final_norm_w"], name="loss_head")
    dh3, gf1 = _ffn_bwd(dh4, h3, ffn1, fnw[1], Wr["up"], fcw[1], fcb[1], Wr["down"], "1")
    tok = put_g("ffn1", dict(up=gf1["up"], down=gf1["down"]))
    g_wo = _mm_tn(o, dh3, name="attn_o_wg", tn=1024)
    do = _mm_nt(dh3, _tie(Wr["w_o"], tok), name="attn_o_dg", out_dtype=BF16, tn=1024, tk=1024)
    dq, dk, dv = _sba_bwd(q, kv, lt, do, name="sba_bwd")
    g_wq = _mm_tn(h2, dq, norm_w=Ws["attn_norm_w"], name="attn_q_wg", tn=1024, tt=1024)
    dh2a, g_attn_nw = _mm_nt(dq, Wr["w_q"], epi=(h2, Ws["attn_norm_w"], dh3), name="attn_q_dg", tm=1024, tk=1024)
    dkv = jnp.concatenate([dk, dv], axis=1)
    g_wkv = _mm_tn(h2, dkv, norm_w=Ws["kv_norm_w"], name="attn_kv_wg", tn=1024, tt=1024)
    dh2, g_kv_nw = _mm_nt(dkv, Wr["w_kv"], epi=(h2, Ws["kv_norm_w"], dh2a), name="attn_kv_dg", tm=1024, tk=1024)
    tok = put_g("attn", dict(w_o=g_wo, w_q=g_wq, w_k=g_wkv[:, :D_MODEL], w_v=g_wkv[:, D_MODEL:]))
    dh1, gf0 = _ffn_bwd(dh2, h1, ffn0, fnw[0], W0["up"], fcw[0], _tie(fcb[0], tok), W0["down"], "0")
    tok = put_g("ffn0", dict(up=gf0["up"], down=gf0["down"]))
    g_out = _mm_tn(yn, dh1, name="ssm_out_wg", tn=1024)
    dyn = _mm_nt(dh1, _tie(Ws["ssm_out_w"], tok), name="ssm_out_dg", out_dtype=BF16, tn=1024, tk=1024)
    tok = put_g("ssm_out", dict(ssm_out_w=g_out))
    dxbc_c, dz, ddt, g_gnw, dpar = _ssd_bwd(xbc_c, zx, dtg, par, _tie(gnw, tok), y, st, dyn, name="ssd_bwd")
    dhid, g_scw, g_scb = _ssm_conv_bwd_pre(zx, Ws["ssm_conv_w"], Ws["ssm_conv_b"].reshape(1, -1), dxbc_c,
                                           name="ssm_conv_bwd")
    dxbc = _conv_bwd_in(dhid, Ws["ssm_conv_w"], K=SSM_CONV, name="ssm_conv_bwd_in")
    ddt_t = ddt[:, :, :8].transpose(1, 0, 2).reshape(T, SSM_HEADS).astype(BF16)
    dzx = jnp.concatenate([dz, dxbc, jnp.pad(ddt_t, ((0, 0), (0, IN_PROJ_PAD - IN_PROJ_DIM)))], axis=1)
    g_in = _mm_tn(x, dzx, norm_w=Ws["ssm_norm_w"], name="ssm_in_wg", tn=1792, tt=1024)
    tok = put_g("ssm_in", dict(ssm_in_w=g_in[:, :IN_PROJ_DIM]))
    dx, g_ssm_nw = _mm_nt(dzx, Ws["in_w"], epi=(x, _tie(Ws["ssm_norm_w"], tok), dh1), name="ssm_in_dg", tm=1024, tk=1792)
    f = {
        "ssm_norm_w": g_ssm_nw.reshape(-1), "ssm_conv_w": g_scw,
        "ssm_conv_b": g_scb.reshape(-1), "ssm_dt_bias": dpar[:, 0, :8].reshape(-1),
        "ssm_a_log": dpar[:, 1, :8].reshape(-1), "ssm_d": dpar[:, 2, :8].reshape(-1),
        "ssm_gate_norm_w": g_gnw.reshape(-1), "kv_norm_w": g_kv_nw.reshape(-1), "attn_norm_w": g_attn_nw.reshape(-1),
        "ffn_norm_w": jnp.stack([gf0["norm"], gf1["norm"]]), "ffn_conv_w": jnp.stack([gf0["conv_w"], gf1["conv_w"]]),
        "ffn_conv_b": jnp.stack([gf0["conv_b"], gf1["conv_b"]]), "final_norm_w": g_final.reshape(-1),
    }
    return loss, dx, f


def kernel(x, ssm_norm_w, ssm_in_w, ssm_conv_w, ssm_conv_b, ssm_dt_bias, ssm_a_log, ssm_d, ssm_gate_norm_w, ssm_out_w, kv_norm_w, w_k, w_v, attn_norm_w, w_q, w_o, ffn_norm_w, ffn_up_w, ffn_conv_w, ffn_conv_b, ffn_down_w, final_norm_w, loss_target, m_ssm_norm_w, m_ssm_in_w, m_ssm_conv_w, m_ssm_conv_b, m_ssm_dt_bias, m_ssm_a_log, m_ssm_d, m_ssm_gate_norm_w, m_ssm_out_w, m_kv_norm_w, m_w_k, m_w_v, m_attn_norm_w, m_w_q, m_w_o, m_ffn_norm_w, m_ffn_up_w, m_ffn_conv_w, m_ffn_conv_b, m_ffn_down_w, m_final_norm_w, v_ssm_norm_w, v_ssm_in_w, v_ssm_conv_w, v_ssm_conv_b, v_ssm_dt_bias, v_ssm_a_log, v_ssm_d, v_ssm_gate_norm_w, v_ssm_out_w, v_kv_norm_w, v_w_k, v_w_v, v_attn_norm_w, v_w_q, v_w_o, v_ffn_norm_w, v_ffn_up_w, v_ffn_conv_w, v_ffn_conv_b, v_ffn_down_w, v_final_norm_w):
    env = dict(locals())
    p = {n: env[n] for n in _WEIGHTS}
    mom = {n: env["m_" + n] for n in _WEIGHTS}
    var = {n: env["v_" + n] for n in _WEIGHTS}
    T = x.shape[1]
    me = 4 * lax.axis_index("x") + 2 * lax.axis_index("y") + lax.axis_index("c")
    rs = D_FF // N_DEV

    def bf2(a):
        return _as2d(a).astype(BF16)

    def with_own(srcs, lands, scatter):
        out = []
        for s, l in zip(srcs, lands):
            own = lax.dynamic_index_in_dim(s, me, 0, keepdims=False) if scatter else s
            out.append(lax.dynamic_update_index_in_dim(l, own, me, 0))
        return out

    a_names = ["ssm_in_w"] + _SMALL_SHARDED
    got_a = dict(zip(a_names, _all_gather([bf2(p["ssm_in_w"])] + [_as2d(p[n]) for n in _SMALL_SHARDED],
                                          name="gather_ssm")))
    ffn0_names = ["ssm_out_w", "up0", "down0"]
    rest_names = ["w_q", "w_k", "w_v", "w_o", "up1", "down1"]
    shard = {"up0": bf2(p["ffn_up_w"][0]), "down0": bf2(p["ffn_down_w"][0]), "up1": bf2(p["ffn_up_w"][1]),
             "down1": bf2(p["ffn_down_w"][1]), "w_q": bf2(p["w_q"]), "w_k": bf2(p["w_k"]), "w_v": bf2(p["w_v"]),
             "w_o": bf2(p["w_o"]), "ssm_out_w": bf2(p["ssm_out_w"])}
    h_ffn0 = _push_start([shard[n] for n in ffn0_names], scatter=False, name="gather_ffn0_start")
    handles = {}

    def get_w(group, after):
        if group == "ssm":
            W = {n: p[n] for n in _SMALL_REPL}
            for n in ("ssm_dt_bias", "ssm_a_log", "ssm_d", "attn_norm_w"):
                W[n] = W[n].reshape(-1)
            W["in_w"] = jnp.pad(_cols_to_full(got_a["ssm_in_w"]), ((0, 0), (0, IN_PROJ_PAD - IN_PROJ_DIM)))
            W["ssm_norm_w"] = _tie(got_a["ssm_norm_w"].reshape(D_MODEL), h_ffn0["token"])
            W["ssm_conv_w"] = _cols_to_full(got_a["ssm_conv_w"])
            W["ssm_conv_b"] = got_a["ssm_conv_b"].reshape(CONV_DIM)
            W["ssm_gate_norm_w"] = got_a["ssm_gate_norm_w"].reshape(D_INNER)
            W["ffn_conv_w"] = _cols_to_full(got_a["ffn_conv_w"]).reshape(2, FFN_CONV, 2 * D_FF)
            return W
        if group == "rest_start":
            anchor = after[0, 0]
            first = shard[rest_names[0]] + (jnp.where(jnp.isfinite(anchor), anchor, 0.0) * 0.0).astype(BF16)
            handles["rest"] = _push_start([first] + [shard[n] for n in rest_names[1:]], scatter=False,
                                          name="gather_rest_start")
            return handles["rest"]["token"]
        if group == "ffn0":
            srcs, lands = _push_wait(h_ffn0, after, name="gather_ffn0_wait")
            out, up, down = with_own(srcs, lands, False)
            return dict(ssm_out_w=out.reshape(D_INNER, D_MODEL), up=_cols_to_full(up), down=down.reshape(D_FF, D_MODEL))
        srcs, lands = _push_wait(handles["rest"], after, name="gather_rest_wait")
        g = dict(zip(rest_names, with_own(srcs, lands, False)))
        sq = lambda a: a.reshape(D_MODEL, D_MODEL)
        return dict(w_q=sq(g["w_q"]), w_kv=jnp.concatenate([sq(g["w_k"]), sq(g["w_v"])], axis=1), w_o=sq(g["w_o"]),
                    up=_cols_to_full(g["up1"]), down=g["down1"].reshape(D_FF, D_MODEL))

    pending = []

    def put_g(group, g):
        if group in ("ffn0", "ffn1"):
            keys = [("ffn_up_w", int(group[-1])), ("ffn_down_w", int(group[-1]))]
            blocks = [_full_to_cols(g["up"]), g["down"].reshape(N_DEV, rs, D_MODEL)]
        elif group == "attn":
            keys = [(n, None) for n in ("w_o", "w_q", "w_k", "w_v")]
            blocks = [g[n].reshape(N_DEV, D_MODEL // N_DEV, D_MODEL) for n, _ in keys]
        elif group == "ssm_out":
            keys = [("ssm_out_w", None)]
            blocks = [g["ssm_out_w"].reshape(N_DEV, D_INNER // N_DEV, D_MODEL)]
        else:
            keys = [("ssm_in_w", None)]
            blocks = [_full_to_cols(g["ssm_in_w"])]
        h = _push_start(blocks, scatter=True, name=f"exchange_{group}_start")
        pending.append((group, keys, h))
        return h["token"]

    loss_row, dx, f = _local_step2(x.reshape(T, D_MODEL), loss_target.reshape(T, D_MODEL), get_w, put_g)

    small_names = _SMALL_REPL + _SMALL_SHARDED
    small_full = _pack_small([f[n] for n in small_names] + [loss_row[0, 0:1]])
    small_bcast = jnp.broadcast_to(small_full[None], (N_DEV,) + small_full.shape)
    h_small = _push_start([small_bcast], scatter=True, name="exchange_small_start")
    tok = h_small["token"]

    arrived, res = {}, {}
    for group, keys, h in pending:
        srcs, lands = _push_wait(h, dx, name=f"exchange_{group}_wait")
        arrived.update(zip(keys, with_own(srcs, lands, True)))
        for n in _BIG:
            layered = (n, 0) in arrived or (n, 1) in arrived
            if n in res or not ((n, None) in arrived or ((n, 0) in arrived and (n, 1) in arrived)):
                continue
            parts = [arrived[(n, 0)], arrived[(n, 1)]] if layered else arrived[(n, None)]
            w2, m2, v2 = _as2d(p[n]), _as2d(mom[n]), _as2d(var[n])
            if not res:
                w2 = _tie(w2, tok)
            res[n] = _adamw(parts, w2, m2, v2, name=f"adamw_{n}", tr=rs if n == "ffn_down_w" else 256)
    srcs, lands = _push_wait(h_small, res["ssm_in_w"][0], name="exchange_small_wait")
    small_parts = with_own(srcs, lands, True)[0]
    out_g, out_d, out_m, out_v = {}, {}, {}, {}
    for n in _BIG:
        out_g[n], out_d[n], out_m[n], out_v[n] = (t.reshape(p[n].shape) for t in res[n])

    zero = jnp.zeros_like(small_full)
    g_small_sum = _adamw(small_parts, zero, zero, zero, name="sum_small_grads", tr=small_full.shape[0])[0]
    *small_sums, loss_sum = _unpack_small(g_small_sum, [f[n].shape for n in small_names] + [(1,)])
    loss = loss_sum[0]
    g_small = dict(zip(small_names, small_sums))
    for n in _SMALL_SHARDED:
        width = p[n].shape[-1]
        g_small[n] = lax.dynamic_slice_in_dim(g_small[n], me * width, width, axis=g_small[n].ndim - 1)
    sw = _pack_small([p[n] for n in small_names])
    sm = _pack_small([mom[n] for n in small_names])
    sv = _pack_small([var[n] for n in small_names])
    sg = _pack_small([g_small[n] for n in small_names])
    _, d, nm, nv = _adamw(sg[None], sw, sm, sv, name="adamw_small", tr=sw.shape[0])
    shard_shapes = [p[n].shape for n in small_names]
    for n, dd, mm, vv in zip(small_names, _unpack_small(d, shard_shapes), _unpack_small(nm, shard_shapes),
                             _unpack_small(nv, shard_shapes)):
        out_g[n] = g_small[n].reshape(p[n].shape)
        out_d[n], out_m[n], out_v[n] = dd, mm, vv

    return (loss, dx.reshape(x.shape), *[out_g[n] for n in _WEIGHTS], *[out_d[n] for n in _WEIGHTS],
            *[out_m[n] for n in _WEIGHTS], *[out_v[n] for n in _WEIGHTS])
```

```python
import functools
import math

import jax
import jax.numpy as jnp
from jax import lax
from jax.experimental import pallas as pl
from jax.experimental.pallas import tpu as pltpu

F32 = jnp.float32
BF16 = jnp.bfloat16
EPS = 1e-6

D_MODEL = 1024
D_INNER = 2048
SSM_HEADS = 32
SSM_GROUPS = 4
SSM_STATE = 128
SSM_CONV = 4
SSM_CHUNK = 128
GN = SSM_GROUPS * SSM_STATE
CONV_DIM = D_INNER + 2 * GN
IN_PROJ_DIM = D_INNER + CONV_DIM + SSM_HEADS
IN_PROJ_PAD = 5376
SB_HEADS = 16
SB_HEAD_DIM = 64
SB_BLOCK = 128
D_FF = 2816
FFN_CONV = 3
N_DEV = 8

ADAM_LR = 0.001
ADAM_B1 = 0.9
ADAM_B2 = 0.999
ADAM_EPS = 1e-08
ADAM_WD = 0.01
ADAM_STEP = 10

_MESH = pl.DeviceIdType.MESH
_NT = (((1,), (1,)), ((), ()))
_TN = (((0,), (0,)), ((), ()))
_ANY = pl.BlockSpec(memory_space=pl.ANY)


def _cparams(sem, vmem_mb=48):
    return pltpu.CompilerParams(dimension_semantics=sem, vmem_limit_bytes=vmem_mb * 1024 * 1024)


def _sigmoid(x):
    return 1.0 / (1.0 + jnp.exp(-x))


def _softplus(x):
    return jnp.maximum(x, 0.0) + jnp.log(1.0 + jnp.exp(-jnp.abs(x)))


def _rms_fwd(xv, w):
    r = lax.rsqrt(jnp.mean(xv * xv, axis=-1, keepdims=True) + EPS)
    return xv * r * w


def _mm_fwd(x, w, *, name, norm_w=None, residual=None, out_dtype=F32, tm=512, tn=512, halves=False):
    M, K = x.shape
    N = w.shape[1]
    tm, tn = min(tm, M), min(tn, N)
    assert M % tm == 0 and N % tn == 0, (name, M, N, tm, tn)
    if halves:
        nbh = N // 2 // tn
        assert N // 2 % tn == 0
        out_spec = pl.BlockSpec((None, tm, tn), lambda i, j: (lax.div(j, nbh), i, lax.rem(j, nbh)))
        out_shape = jax.ShapeDtypeStruct((2, M, N // 2), out_dtype)
    else:
        out_spec = pl.BlockSpec((tm, tn), lambda i, j: (i, j))
        out_shape = jax.ShapeDtypeStruct((M, N), out_dtype)
    has_norm, has_res = norm_w is not None, residual is not None

    def body(*refs):
        x_ref, w_ref = refs[0], refs[1]
        p = 2
        nw_ref = r_ref = None
        if has_norm:
            nw_ref = refs[p]
            p += 1
        if has_res:
            r_ref = refs[p]
            p += 1
        o_ref, xn_ref = refs[p], refs[p + 1]

        @pl.when(pl.program_id(1) == 0)
        def _():
            xv = x_ref[...].astype(F32)
            if has_norm:
                xv = _rms_fwd(xv, nw_ref[...])
            xn_ref[...] = xv.astype(BF16)

        acc = jnp.dot(xn_ref[...], w_ref[...], preferred_element_type=F32)
        if has_res:
            acc = acc + r_ref[...]
        o_ref[...] = acc.astype(out_dtype)

    in_specs = [pl.BlockSpec((tm, K), lambda i, j: (i, 0)), pl.BlockSpec((K, tn), lambda i, j: (0, j))]
    args = [x, w]
    if has_norm:
        in_specs.append(pl.BlockSpec((1, K), lambda i, j: (0, 0)))
        args.append(norm_w.reshape(1, K))
    if has_res:
        in_specs.append(pl.BlockSpec((tm, tn), lambda i, j: (i, j)))
        args.append(residual)
    return pl.pallas_call(
        body, name=name, grid=(M // tm, N // tn), in_specs=in_specs,
        out_specs=out_spec, out_shape=out_shape,
        scratch_shapes=[pltpu.VMEM((tm, K), BF16)],
        compiler_params=_cparams(("parallel", "arbitrary")))(*args)


def _mm_nt(dy, w, *, name, epi=None, out_dtype=F32, tm=512, tn=512, tk=512):
    halves = dy.ndim == 3
    M, K = (dy.shape[1], 2 * dy.shape[2]) if halves else dy.shape
    N = w.shape[0]
    tm, tk = min(tm, M), min(tk, K)
    tn = N if epi is not None else min(tn, N)
    assert M % tm == 0 and N % tn == 0 and K % tk == 0, (name, M, N, K, tm, tn, tk)
    nk = K // tk
    has_epi = epi is not None

    def body(*refs):
        if has_epi:
            dy_ref, w_ref, h_ref, nw_ref, r_ref, o_ref, dnw_ref, acc_ref = refs
        else:
            dy_ref, w_ref, o_ref, acc_ref = refs
        i = pl.program_id(0)
        k = pl.program_id(2)

        @pl.when(k == 0)
        def _():
            acc_ref[...] = jnp.zeros_like(acc_ref)

        acc_ref[...] += lax.dot_general(dy_ref[...].astype(BF16), w_ref[...], _NT, preferred_element_type=F32)

        @pl.when(k == nk - 1)
        def _():
            du = acc_ref[...]
            if has_epi:
                hv = h_ref[...]
                r = lax.rsqrt(jnp.mean(hv * hv, axis=-1, keepdims=True) + EPS)
                xhat = hv * r
                dxh = du * nw_ref[...]
                dx = r * (dxh - xhat * jnp.mean(dxh * xhat, axis=-1, keepdims=True))
                o_ref[...] = (r_ref[...] + dx).astype(out_dtype)
                contrib = jnp.sum(du * xhat, axis=0, keepdims=True)

                @pl.when(i == 0)
                def _():
                    dnw_ref[...] = contrib

                @pl.when(i > 0)
                def _():
                    dnw_ref[...] += contrib
            else:
                o_ref[...] = du.astype(out_dtype)

    if halves:
        nkh = K // 2 // tk
        assert K // 2 % tk == 0
        dy_spec = pl.BlockSpec((None, tm, tk), lambda i, j, k: (lax.div(k, nkh), i, lax.rem(k, nkh)))
    else:
        dy_spec = pl.BlockSpec((tm, tk), lambda i, j, k: (i, k))
    in_specs = [dy_spec, pl.BlockSpec((tn, tk), lambda i, j, k: (j, k))]
    args = [dy, w]
    out_specs = [pl.BlockSpec((tm, tn), lambda i, j, k: (i, j))]
    out_shape = [jax.ShapeDtypeStruct((M, N), out_dtype)]
    if has_epi:
        h, nw, res = epi
        in_specs += [pl.BlockSpec((tm, N), lambda i, j, k: (i, 0)), pl.BlockSpec((1, N), lambda i, j, k: (0, 0)),
                     pl.BlockSpec((tm, N), lambda i, j, k: (i, 0))]
        args += [h, nw.reshape(1, N), res]
        out_specs.append(pl.BlockSpec((1, N), lambda i, j, k: (0, 0)))
        out_shape.append(jax.ShapeDtypeStruct((1, N), F32))
    outs = pl.pallas_call(
        body, name=name, grid=(M // tm, N // tn, nk), in_specs=in_specs, out_specs=out_specs, out_shape=out_shape,
        scratch_shapes=[pltpu.VMEM((tm, tn), F32)],
        compiler_params=_cparams(("arbitrary", "arbitrary", "arbitrary")))(*args)
    return (outs[0], outs[1]) if has_epi else outs[0]


def _mm_tn(x, dy, *, name, norm_w=None, out_dtype=BF16, tk1=1024, tn=512, tt=512):
    T, K1 = x.shape
    halves = dy.ndim == 3
    N = 2 * dy.shape[2] if halves else dy.shape[1]
    tk1, tn, tt = min(tk1, K1), min(tn, N), min(tt, T)
    has_norm = norm_w is not None
    assert K1 % tk1 == 0 and N % tn == 0 and T % tt == 0, (name, K1, N, T, tk1, tn, tt)
    assert not has_norm or tk1 == K1
    nt = T // tt

    def body(*refs):
        if has_norm:
            x_ref, dy_ref, nw_ref, o_ref, acc_ref = refs
        else:
            x_ref, dy_ref, o_ref, acc_ref = refs
        t = pl.program_id(2)

        @pl.when(t == 0)
        def _():
            acc_ref[...] = jnp.zeros_like(acc_ref)

        xv = x_ref[...]
        if has_norm:
            xv = _rms_fwd(xv.astype(F32), nw_ref[...])
        acc_ref[...] += lax.dot_general(xv.astype(BF16), dy_ref[...].astype(BF16), _TN, preferred_element_type=F32)

        @pl.when(t == nt - 1)
        def _():
            o_ref[...] = acc_ref[...].astype(out_dtype)

    if halves:
        nbh = N // 2 // tn
        assert N // 2 % tn == 0
        dy_spec = pl.BlockSpec((None, tt, tn), lambda a, b, t: (lax.div(b, nbh), t, lax.rem(b, nbh)))
    else:
        dy_spec = pl.BlockSpec((tt, tn), lambda a, b, t: (t, b))
    in_specs = [pl.BlockSpec((tt, tk1), lambda a, b, t: (t, a)), dy_spec]
    args = [x, dy]
    if has_norm:
        in_specs.append(pl.BlockSpec((1, K1), lambda a, b, t: (0, 0)))
        args.append(norm_w.reshape(1, K1))
    return pl.pallas_call(
        body, name=name, grid=(K1 // tk1, N // tn, nt), in_specs=in_specs,
        out_specs=pl.BlockSpec((tk1, tn), lambda a, b, t: (a, b)),
        out_shape=jax.ShapeDtypeStruct((K1, N), out_dtype),
        scratch_shapes=[pltpu.VMEM((tk1, tn), F32)],
        compiler_params=_cparams(("parallel", "parallel", "arbitrary")))(*args)


def _shift_down(xb, prev8, j):
    main = pltpu.roll(xb, j, 0)
    head = pltpu.roll(xb[0:8], j, 0)
    ph = pltpu.roll(prev8, j, 0)
    row8 = lax.broadcasted_iota(jnp.int32, head.shape, 0)
    head = jnp.where(row8 < j, ph, head)
    return jnp.concatenate([head, main[8:]], axis=0)


def _shift_up(xb, next8, j):
    tt = xb.shape[0]
    main = pltpu.roll(xb, tt - j, 0)
    tail = pltpu.roll(xb[tt - 8:tt], 8 - j, 0)
    nh = pltpu.roll(next8, 8 - j, 0)
    row8 = lax.broadcasted_iota(jnp.int32, tail.shape, 0)
    tail = jnp.where(row8 + j >= 8, nh, tail)
    return jnp.concatenate([main[:tt - 8], tail], axis=0)


def _conv_hid(xb, prev8, w, b_row, K):
    out = b_row
    shifted = []
    for j in range(K):
        sh = K - 1 - j
        xs = xb if sh == 0 else _shift_down(xb, prev8, sh)
        shifted.append(xs)
        out = out + xs * w[j:j + 1, :]
    return out, shifted


def _prev_idx(i, nb8):
    return jnp.maximum(i * nb8 - 1, 0)


def _ssm_conv_fwd(zx, w, b, *, name, tt=512, tc=512):
    T = zx.shape[0]
    tt = min(tt, T)
    C, K = CONV_DIM, SSM_CONV
    cb0, nb8 = D_INNER // tc, tt // 8

    def body(x_ref, p_ref, w_ref, b_ref, o_ref):
        first = (pl.program_id(1) > 0).astype(F32)
        hid, _ = _conv_hid(x_ref[...], p_ref[...] * first, w_ref[...], b_ref[...], K)
        o_ref[...] = hid * _sigmoid(hid)

    return pl.pallas_call(
        body, name=name, grid=(C // tc, T // tt),
        in_specs=[pl.BlockSpec((tt, tc), lambda c, i: (i, c + cb0)),
                  pl.BlockSpec((8, tc), lambda c, i: (_prev_idx(i, nb8), c + cb0)),
                  pl.BlockSpec((K, tc), lambda c, i: (0, c)), pl.BlockSpec((1, tc), lambda c, i: (0, c))],
        out_specs=pl.BlockSpec((tt, tc), lambda c, i: (i, c)),
        out_shape=jax.ShapeDtypeStruct((T, C), F32),
        compiler_params=_cparams(("parallel", "parallel")))(zx, zx, w, b)


def _ssm_conv_bwd_pre(zx, w, b, dout, *, name, tt=512, tc=512):
    T = zx.shape[0]
    tt = min(tt, T)
    C, K = CONV_DIM, SSM_CONV
    cb0, nb8 = D_INNER // tc, tt // 8

    def body(x_ref, p_ref, w_ref, b_ref, d_ref, dh_ref, dw_ref, db_ref):
        t = pl.program_id(1)
        first = (t > 0).astype(F32)
        hid, shifted = _conv_hid(x_ref[...], p_ref[...] * first, w_ref[...], b_ref[...], K)
        sg = _sigmoid(hid)
        dh = d_ref[...] * (sg * (1.0 + hid * (1.0 - sg)))
        dh_ref[...] = dh

        @pl.when(t == 0)
        def _():
            dw_ref[...] = jnp.zeros_like(dw_ref)
            db_ref[...] = jnp.zeros_like(db_ref)

        db_ref[...] += jnp.sum(dh, axis=0, keepdims=True)
        for j in range(K):
            dw_ref[j:j + 1, :] += jnp.sum(dh * shifted[j], axis=0, keepdims=True)

    return pl.pallas_call(
        body, name=name, grid=(C // tc, T // tt),
        in_specs=[pl.BlockSpec((tt, tc), lambda c, i: (i, c + cb0)),
                  pl.BlockSpec((8, tc), lambda c, i: (_prev_idx(i, nb8), c + cb0)),
                  pl.BlockSpec((K, tc), lambda c, i: (0, c)), pl.BlockSpec((1, tc), lambda c, i: (0, c)),
                  pl.BlockSpec((tt, tc), lambda c, i: (i, c))],
        out_specs=[pl.BlockSpec((tt, tc), lambda c, i: (i, c)), pl.BlockSpec((K, tc), lambda c, i: (0, c)),
                   pl.BlockSpec((1, tc), lambda c, i: (0, c))],
        out_shape=[jax.ShapeDtypeStruct((T, C), F32), jax.ShapeDtypeStruct((K, C), F32),
                   jax.ShapeDtypeStruct((1, C), F32)],
        compiler_params=_cparams(("parallel", "arbitrary")))(zx, zx, w, b, dout)


def _put_cols(buf, src, col0, *, name, tt=512):
    T, C = src.shape
    tt = min(tt, T)

    def body(s_ref, _, o_ref):
        o_ref[...] = s_ref[...]

    return pl.pallas_call(
        body, name=name, grid=(T // tt,),
        in_specs=[pl.BlockSpec((tt, C), lambda i: (i, 0)), _ANY],
        out_specs=pl.BlockSpec((tt, C), lambda i: (i, col0 // C)),
        out_shape=jax.ShapeDtypeStruct(buf.shape, buf.dtype), input_output_aliases={1: 0},
        compiler_params=_cparams(("parallel",)))(src, buf)


def _conv_bwd_in(dh, w, *, name, K, tt=512, tc=512, out_dtype=BF16, into=None):
    T, C = dh.shape
    tt = min(tt, T)
    nb8, nT = tt // 8, T // tt
    last8 = T // 8 - 1
    cb0 = 0 if into is None else into[1] // tc

    def body(d_ref, n_ref, w_ref, *rest):
        o_ref = rest[-1]
        notlast = (pl.program_id(1) < nT - 1).astype(F32)
        d = d_ref[...]
        nxt = n_ref[...] * notlast
        w_ = w_ref[...]
        acc = d * w_[K - 1:K, :]
        for sh in range(1, K):
            acc = acc + _shift_up(d, nxt, sh) * w_[K - 1 - sh:K - sh, :]
        o_ref[...] = acc.astype(out_dtype)

    in_specs = [pl.BlockSpec((tt, tc), lambda c, i: (i, c)),
                pl.BlockSpec((8, tc), lambda c, i: (jnp.minimum((i + 1) * nb8, last8), c)),
                pl.BlockSpec((K, tc), lambda c, i: (0, c))]
    args = [dh, dh, w]
    if into is None:
        out_shape, alias = jax.ShapeDtypeStruct((T, C), out_dtype), {}
    else:
        assert into[0].dtype == out_dtype and into[1] % tc == 0
        in_specs.append(_ANY)
        args.append(into[0])
        out_shape, alias = jax.ShapeDtypeStruct(into[0].shape, out_dtype), {3: 0}
    return pl.pallas_call(
        body, name=name, grid=(C // tc, nT), in_specs=in_specs,
        out_specs=pl.BlockSpec((tt, tc), lambda c, i: (i, c + cb0)),
        out_shape=out_shape, input_output_aliases=alias,
        compiler_params=_cparams(("parallel", "parallel")))(*args)


def _ffn_conv_fwd(a, w, b, *, name, tt=256, tc=1408):
    T = a.shape[0]
    tt = min(tt, T)
    K, nbh, nb8 = FFN_CONV, D_FF // tc, tt // 8

    def body(ag_ref, pg_ref, av_ref, pv_ref, wg_ref, wv_ref, bg_ref, bv_ref, o_ref):
        first = (pl.program_id(1) > 0).astype(F32)
        hg, _ = _conv_hid(ag_ref[...], pg_ref[...] * first, wg_ref[...], bg_ref[...], K)
        hv, _ = _conv_hid(av_ref[...], pv_ref[...] * first, wv_ref[...], bv_ref[...], K)
        o_ref[...] = (hg * _sigmoid(hg) * hv).astype(BF16)

    return pl.pallas_call(
        body, name=name, grid=(nbh, T // tt),
        in_specs=[pl.BlockSpec((tt, tc), lambda c, i: (i, c)),
                  pl.BlockSpec((8, tc), lambda c, i: (_prev_idx(i, nb8), c)),
                  pl.BlockSpec((tt, tc), lambda c, i: (i, c + nbh)),
                  pl.BlockSpec((8, tc), lambda c, i: (_prev_idx(i, nb8), c + nbh)),
                  pl.BlockSpec((K, tc), lambda c, i: (0, c)), pl.BlockSpec((K, tc), lambda c, i: (0, c + nbh)),
                  pl.BlockSpec((1, tc), lambda c, i: (0, c)), pl.BlockSpec((1, tc), lambda c, i: (0, c + nbh))],
        out_specs=pl.BlockSpec((tt, tc), lambda c, i: (i, c)),
        out_shape=jax.ShapeDtypeStruct((T, D_FF), BF16),
        compiler_params=_cparams(("parallel", "parallel")))(a, a, a, a, w, w, b, b)


def _ffn_conv_bwd_pre(a, w, b, dp, *, name, tt=256, tc=1408):
    T = a.shape[0]
    tt = min(tt, T)
    K, nbh, nb8 = FFN_CONV, D_FF // tc, tt // 8

    def body(ao_ref, po_ref, ag_ref, pg_ref, av_ref, pv_ref, wg_ref, wv_ref, bg_ref, bv_ref, dp_ref,
             dh_ref, dw_ref, db_ref):
        j = pl.program_id(0)
        t = pl.program_id(1)
        first = (t > 0).astype(F32)
        hg, _ = _conv_hid(ag_ref[...], pg_ref[...] * first, wg_ref[...], bg_ref[...], K)
        hv, _ = _conv_hid(av_ref[...], pv_ref[...] * first, wv_ref[...], bv_ref[...], K)
        sg = _sigmoid(hg)
        d = dp_ref[...].astype(F32)
        is_gate = (j < nbh).astype(F32)
        dh = d * (is_gate * (hv * (sg * (1.0 + hg * (1.0 - sg)))) + (1.0 - is_gate) * (hg * sg))
        dh_ref[...] = dh
        xo = ao_ref[...]
        po = po_ref[...] * first

        @pl.when(t == 0)
        def _():
            dw_ref[...] = jnp.zeros_like(dw_ref)
            db_ref[...] = jnp.zeros_like(db_ref)

        db_ref[...] += jnp.sum(dh, axis=0, keepdims=True)
        for jj in range(K):
            sh = K - 1 - jj
            xs = xo if sh == 0 else _shift_down(xo, po, sh)
            dw_ref[jj:jj + 1, :] += jnp.sum(dh * xs, axis=0, keepdims=True)

    def gi(c):
        return lax.rem(c, nbh)

    return pl.pallas_call(
        body, name=name, grid=(2 * nbh, T // tt),
        in_specs=[pl.BlockSpec((tt, tc), lambda c, i: (i, c)),
                  pl.BlockSpec((8, tc), lambda c, i: (_prev_idx(i, nb8), c)),
                  pl.BlockSpec((tt, tc), lambda c, i: (i, gi(c))),
                  pl.BlockSpec((8, tc), lambda c, i: (_prev_idx(i, nb8), gi(c))),
                  pl.BlockSpec((tt, tc), lambda c, i: (i, gi(c) + nbh)),
                  pl.BlockSpec((8, tc), lambda c, i: (_prev_idx(i, nb8), gi(c) + nbh)),
                  pl.BlockSpec((K, tc), lambda c, i: (0, gi(c))), pl.BlockSpec((K, tc), lambda c, i: (0, gi(c) + nbh)),
                  pl.BlockSpec((1, tc), lambda c, i: (0, gi(c))), pl.BlockSpec((1, tc), lambda c, i: (0, gi(c) + nbh)),
                  pl.BlockSpec((tt, tc), lambda c, i: (i, gi(c)))],
        out_specs=[pl.BlockSpec((tt, tc), lambda c, i: (i, c)), pl.BlockSpec((K, tc), lambda c, i: (0, c)),
                   pl.BlockSpec((1, tc), lambda c, i: (0, c))],
        out_shape=[jax.ShapeDtypeStruct((T, 2 * D_FF), F32), jax.ShapeDtypeStruct((K, 2 * D_FF), F32),
                   jax.ShapeDtypeStruct((1, 2 * D_FF), F32)],
        compiler_params=_cparams(("parallel", "arbitrary")))(a, a, a, a, a, a, w, w, b, b, dp)


def _ffn_conv_fwd3(a3, w, b, *, name, tt=256, tc=1408):
    T = a3.shape[1]
    tt = min(tt, T)
    K, nbh, n16 = FFN_CONV, D_FF // tc, tt // 16

    def body(a_ref, p_ref, wg_ref, wv_ref, bg_ref, bv_ref, o_ref):
        first = (pl.program_id(1) > 0).astype(F32)
        a = a_ref[...].astype(F32)
        prev = p_ref[...].astype(F32)[:, 8:16, :] * first
        hg, _ = _conv_hid(a[0], prev[0], wg_ref[...], bg_ref[...], K)
        hv, _ = _conv_hid(a[1], prev[1], wv_ref[...], bv_ref[...], K)
        o_ref[...] = (hg * _sigmoid(hg) * hv).astype(BF16)

    return pl.pallas_call(
        body, name=name, grid=(nbh, T // tt),
        in_specs=[pl.BlockSpec((2, tt, tc), lambda c, i: (0, i, c)),
                  pl.BlockSpec((2, 16, tc), lambda c, i: (0, _prev_idx(i, n16), c)),
                  pl.BlockSpec((K, tc), lambda c, i: (0, c)), pl.BlockSpec((K, tc), lambda c, i: (0, c + nbh)),
                  pl.BlockSpec((1, tc), lambda c, i: (0, c)), pl.BlockSpec((1, tc), lambda c, i: (0, c + nbh))],
        out_specs=pl.BlockSpec((tt, tc), lambda c, i: (i, c)),
        out_shape=jax.ShapeDtypeStruct((T, D_FF), BF16),
        compiler_params=_cparams(("parallel", "parallel")))(a3, a3, w, w, b, b)


def _ffn_conv_bwd3(a3, w, b, dp, *, name, tt=256, tc=1408):
    T = a3.shape[1]
    tt = min(tt, T)
    K, nbh, n16 = FFN_CONV, D_FF // tc, tt // 16

    def body(a_ref, p_ref, wg_ref, wv_ref, bg_ref, bv_ref, dp_ref, dh_ref, dw_ref, db_ref):
        t = pl.program_id(1)
        first = (t > 0).astype(F32)
        a = a_ref[...].astype(F32)
        prev = p_ref[...].astype(F32)[:, 8:16, :] * first
        hg, sh_g = _conv_hid(a[0], prev[0], wg_ref[...], bg_ref[...], K)
        hv, sh_v = _conv_hid(a[1], prev[1], wv_ref[...], bv_ref[...], K)
        sg = _sigmoid(hg)
        d = dp_ref[...].astype(F32)
        dhg = d * hv * (sg * (1.0 + hg * (1.0 - sg)))
        dhv = d * (hg * sg)
        dh_ref[0] = dhg.astype(BF16)
        dh_ref[1] = dhv.astype(BF16)

        @pl.when(t == 0)
        def _():
            dw_ref[...] = jnp.zeros_like(dw_ref)
            db_ref[...] = jnp.zeros_like(db_ref)

        db_ref[0] += jnp.sum(dhg, axis=0, keepdims=True)
        db_ref[1] += jnp.sum(dhv, axis=0, keepdims=True)
        for j in range(K):
            dw_ref[0, j:j + 1, :] += jnp.sum(dhg * sh_g[j], axis=0, keepdims=True)
            dw_ref[1, j:j + 1, :] += jnp.sum(dhv * sh_v[j], axis=0, keepdims=True)

    return pl.pallas_call(
        body, name=name, grid=(nbh, T // tt),
        in_specs=[pl.BlockSpec((2, tt, tc), lambda c, i: (0, i, c)),
                  pl.BlockSpec((2, 16, tc), lambda c, i: (0, _prev_idx(i, n16), c)),
                  pl.BlockSpec((K, tc), lambda c, i: (0, c)), pl.BlockSpec((K, tc), lambda c, i: (0, c + nbh)),
                  pl.BlockSpec((1, tc), lambda c, i: (0, c)), pl.BlockSpec((1, tc), lambda c, i: (0, c + nbh)),
                  pl.BlockSpec((tt, tc), lambda c, i: (i, c))],
        out_specs=[pl.BlockSpec((2, tt, tc), lambda c, i: (0, i, c)), pl.BlockSpec((2, K, tc), lambda c, i: (0, 0, c)),
                   pl.BlockSpec((2, 1, tc), lambda c, i: (0, 0, c))],
        out_shape=[jax.ShapeDtypeStruct((2, T, D_FF), BF16), jax.ShapeDtypeStruct((2, K, D_FF), F32),
                   jax.ShapeDtypeStruct((2, 1, D_FF), F32)],
        compiler_params=_cparams(("parallel", "arbitrary")))(a3, a3, w, w, b, b, dp)


def _conv_bwd_in3(dh3, w, *, name, K, tt=256, tc=1408):
    H, T, C = dh3.shape
    tt = min(tt, T)
    nb, n16, nT = C // tc, tt // 16, T // tt
    last16 = T // 16 - 1

    def body(d_ref, n_ref, w_ref, o_ref):
        notlast = (pl.program_id(2) < nT - 1).astype(F32)
        d = d_ref[...].astype(F32)
        nxt = n_ref[...].astype(F32)[0:8, :] * notlast
        w_ = w_ref[...]
        acc = d * w_[K - 1:K, :]
        for sh in range(1, K):
            acc = acc + _shift_up(d, nxt, sh) * w_[K - 1 - sh:K - sh, :]
        o_ref[...] = acc.astype(BF16)

    return pl.pallas_call(
        body, name=name, grid=(H, nb, nT),
        in_specs=[pl.BlockSpec((None, tt, tc), lambda h, c, i: (h, i, c)),
                  pl.BlockSpec((None, 16, tc), lambda h, c, i: (h, jnp.minimum((i + 1) * n16, last16), c)),
                  pl.BlockSpec((K, tc), lambda h, c, i: (0, h * nb + c))],
        out_specs=pl.BlockSpec((None, tt, tc), lambda h, c, i: (h, i, c)),
        out_shape=jax.ShapeDtypeStruct((H, T, C), BF16),
        compiler_params=_cparams(("parallel", "parallel", "parallel")))(dh3, dh3, w)


def _cumsum_rows(x):
    L = x.shape[0]
    row = lax.broadcasted_iota(jnp.int32, x.shape, 0)
    k = 1
    while k < L:
        x = x + jnp.where(row >= k, pltpu.roll(x, k, 0), 0.0)
        k *= 2
    return x


def _rcumsum_rows(x):
    L = x.shape[0]
    row = lax.broadcasted_iota(jnp.int32, x.shape, 0)
    k = 1
    while k < L:
        x = x + jnp.where(row < L - k, pltpu.roll(x, L - k, 0), 0.0)
        k *= 2
    return x


def _split_terms(m, n):
    terms, rest = [], m
    for _ in range(n):
        t = rest.astype(BF16)
        terms.append(t)
        rest = rest - t.astype(F32)
    return jnp.concatenate(terms, axis=1)


def _select_dot(m, n_terms, n_out, cond):
    K = m.shape[1]
    k = lax.broadcasted_iota(jnp.int32, (K, n_out), 0)
    j = lax.broadcasted_iota(jnp.int32, (K, n_out), 1)
    sel = cond(k, j).astype(BF16)
    return jnp.dot(_split_terms(m, n_terms), jnp.concatenate([sel] * n_terms, axis=0), preferred_element_type=F32)


def _rowsum_mxu(m):
    return _select_dot(m, 2, 128, lambda k, j: k >= 0)


def _lane_block_sums(m, width):
    shift = width.bit_length() - 1
    return _select_dot(m, 2, 128, lambda k, j: j == jnp.right_shift(k, shift))


def _heads_to_pairs(m):
    return _select_dot(m, 3, 512, lambda k, j: k == jnp.right_shift(j, 6))


def _ssd_common(dt_ref, par_ref):
    par = par_ref[...]
    raw = dt_ref[...] + par[0:1, :]
    dt = _softplus(raw)
    a = -jnp.exp(par[1:2, :])
    cs = _cumsum_rows(dt * a)
    L = cs.shape[0]
    cs_last = cs[L - 1:L, :]
    return raw, dt, a, par[2:3, :], cs, cs.T, jnp.exp(cs), jnp.exp(cs_last - cs), jnp.exp(cs_last)


def _ssd_specs(nc, rev):
    L = SSM_CHUNK

    def ci(c):
        return nc - 1 - c if rev else c

    return [pl.BlockSpec((L, D_INNER), lambda c: (ci(c), 0)),
            pl.BlockSpec((L, GN), lambda c: (ci(c), D_INNER // GN)),
            pl.BlockSpec((L, GN), lambda c: (ci(c), D_INNER // GN + 1)),
            pl.BlockSpec((SSM_GROUPS, L, 128), lambda c: (0, ci(c), 0)),
            pl.BlockSpec((SSM_GROUPS, 8, 128), lambda c: (0, 0, 0)),
            pl.BlockSpec((L, D_INNER), lambda c: (ci(c), 0)),
            pl.BlockSpec((1, D_INNER), lambda c: (0, 0))], ci


def _round_robin(gens):
    live = list(gens)
    while live:
        nxt = []
        for gen in live:
            try:
                next(gen)
                nxt.append(gen)
            except StopIteration:
                pass
        live = nxt


def _group_views(g, wide, narrow, lead):
    return ([r.at[:, g * 512:(g + 1) * 512] for r in wide], [r.at[:, g * 128:(g + 1) * 128] for r in narrow],
            [r.at[g] for r in lead])


def _ssd_fwd(xbc_c, zx, dtg, par, gnw, *, name):
    T = xbc_c.shape[0]
    L = SSM_CHUNK
    nc = T // L
    in_specs, ci = _ssd_specs(nc, False)

    def body(xs_ref, b_ref, c_ref, dt_ref, par_ref, z_ref, gnw_ref, y_ref, yn_ref, st_ref, h_ref):
        @pl.when(pl.program_id(0) == 0)
        def _():
            h_ref[...] = jnp.zeros_like(h_ref)

        gens = []
        for g in range(SSM_GROUPS):
            (xs, z, gw, y, yn), (b, c), (dt, pr, st, h) = _group_views(
                g, [xs_ref, z_ref, gnw_ref, y_ref, yn_ref], [b_ref, c_ref], [dt_ref, par_ref, st_ref, h_ref])
            gens.append(group(xs, b, c, dt, pr, z, gw, y, yn, st, h))
        _round_robin(gens)

    def group(xs_ref, b_ref, c_ref, dt_ref, par_ref, z_ref, gnw_ref, y_ref, yn_ref, st_ref, h_ref):
        _, dt, _, dsk, cs, csT, ecs, eend, dec = _ssd_common(dt_ref, par_ref)
        Bb = b_ref[...].astype(BF16)
        Cb = c_ref[...].astype(BF16)
        G = lax.dot_general(Cb, Bb, _NT, preferred_element_type=F32)
        row = lax.broadcasted_iota(jnp.int32, (L, L), 0)
        col = lax.broadcasted_iota(jnp.int32, (L, L), 1)
        tril = col <= row
        lo = lax.broadcasted_iota(jnp.int32, (L, 128), 1) < 64
        lo1 = lax.broadcasted_iota(jnp.int32, (1, 128), 1) < 64
        dt_x, ecs_x, eend_x = (_heads_to_pairs(m) for m in (dt, ecs, eend))
        for pp in range(4):
            hA, hB = 2 * pp, 2 * pp + 1
            lanes = slice(pp * 128, (pp + 1) * 128)

            def sel1(m):
                return jnp.where(lo1, m[:, hA:hA + 1], m[:, hB:hB + 1])

            X = xs_ref[:, lanes]
            xd = X * dt_x[:, lanes]
            xdb = xd.astype(BF16)
            ys = []
            for h in (hA, hB):
                Lm = jnp.where(tril, jnp.exp(jnp.minimum(cs[:, h:h + 1] - csT[h:h + 1, :], 0.0)), 0.0)
                ys.append(jnp.dot((G * Lm).astype(BF16), xdb, preferred_element_type=F32))
                yield
            Hp = h_ref[pp]
            st_ref[pp] = Hp
            yoff = jnp.dot(Cb, Hp.astype(BF16), preferred_element_type=F32) * ecs_x[:, lanes]
            y_ref[:, lanes] = jnp.where(lo, ys[0], ys[1]) + yoff + sel1(dsk) * X
            S = lax.dot_general(Bb, (xd * eend_x[:, lanes]).astype(BF16), _TN, preferred_element_type=F32)
            h_ref[pp] = Hp * sel1(dec) + S
            yield
        zv = z_ref[...]
        yg = y_ref[...] * (zv * _sigmoid(zv))
        r = jnp.tile(lax.rsqrt(_rowsum_mxu(yg * yg) * (1.0 / 512) + EPS), (1, 4))
        yn_ref[...] = (yg * r * gnw_ref[...]).astype(BF16)

    return pl.pallas_call(
        body, name=name, grid=(nc,), in_specs=in_specs,
        out_specs=[pl.BlockSpec((L, D_INNER), lambda c: (c, 0)), pl.BlockSpec((L, D_INNER), lambda c: (c, 0)),
                   pl.BlockSpec((SSM_GROUPS, None, 4, 128, 128), lambda c: (0, c, 0, 0, 0))],
        out_shape=[jax.ShapeDtypeStruct((T, D_INNER), F32), jax.ShapeDtypeStruct((T, D_INNER), BF16),
                   jax.ShapeDtypeStruct((SSM_GROUPS, nc, 4, 128, 128), F32)],
        scratch_shapes=[pltpu.VMEM((SSM_GROUPS, 4, 128, 128), F32)],
        compiler_params=_cparams(("arbitrary",)))(xbc_c, xbc_c, xbc_c, dtg, par, zx, gnw)


def _ssd_bwd(xbc_c, zx, dtg, par, gnw, y, st, dyn, *, name):
    T = xbc_c.shape[0]
    L = SSM_CHUNK
    nc = T // L
    in_specs, ci = _ssd_specs(nc, True)
    in_specs += [pl.BlockSpec((L, D_INNER), lambda c: (ci(c), 0)),
                 pl.BlockSpec((SSM_GROUPS, None, 4, 128, 128), lambda c: (0, ci(c), 0, 0, 0)),
                 pl.BlockSpec((L, D_INNER), lambda c: (ci(c), 0))]

    def body(xs_ref, b_ref, c_ref, dt_ref, par_ref, z_ref, gnw_ref, y_ref, st_ref, dyn_ref,
             dxbc_ref, dz_ref, ddt_ref, dgnw_ref, dpar_ref, dh_ref):
        @pl.when(pl.program_id(0) == 0)
        def _():
            dh_ref[...] = jnp.zeros_like(dh_ref)
            dgnw_ref[...] = jnp.zeros_like(dgnw_ref)
            dpar_ref[...] = jnp.zeros_like(dpar_ref)

        dxs_ref = dxbc_ref.at[:, 0:D_INNER]
        db_ref = dxbc_ref.at[:, D_INNER:D_INNER + GN]
        dc_ref = dxbc_ref.at[:, D_INNER + GN:CONV_DIM]

        gens = []
        for g in range(SSM_GROUPS):
            (xs, z, gw, y, dyn, dxs, dz, dgw), (b, c, db, dc), (dt, pr, st, ddt, dpr, dh) = _group_views(
                g, [xs_ref, z_ref, gnw_ref, y_ref, dyn_ref, dxs_ref, dz_ref, dgnw_ref], [b_ref, c_ref, db_ref, dc_ref],
                [dt_ref, par_ref, st_ref, ddt_ref, dpar_ref, dh_ref])
            gens.append(group(xs, b, c, dt, pr, z, gw, y, st, dyn, dxs, db, dc, dz, ddt, dgw, dpr, dh))
        _round_robin(gens)

    def group(xs_ref, b_ref, c_ref, dt_ref, par_ref, z_ref, gnw_ref, y_ref, st_ref, dyn_ref,
              dxs_ref, db_ref, dc_ref, dz_ref, ddt_ref, dgnw_ref, dpar_ref, dh_ref):
        yv = y_ref[...]
        zv = z_ref[...]
        sg = _sigmoid(zv)
        sz = zv * sg
        yg = yv * sz
        r = jnp.tile(lax.rsqrt(_rowsum_mxu(yg * yg) * (1.0 / 512) + EPS), (1, 4))
        yh = yg * r
        dyn = dyn_ref[...].astype(F32)
        dgnw_ref[...] += jnp.sum(dyn * yh, axis=0, keepdims=True)
        dyh = dyn * gnw_ref[...]
        dyg = r * (dyh - yh * jnp.tile(_rowsum_mxu(dyh * yh) * (1.0 / 512), (1, 4)))
        dY_all = dyg * sz
        dz_ref[...] = (dyg * yv * (sg * (1.0 + zv * (1.0 - sg)))).astype(dz_ref.dtype)

        yield
        raw, dt, a, dsk, cs, csT, ecs, eend, dec = _ssd_common(dt_ref, par_ref)
        Bb = b_ref[...].astype(BF16)
        Cb = c_ref[...].astype(BF16)
        G = lax.dot_general(Cb, Bb, _NT, preferred_element_type=F32)
        row = lax.broadcasted_iota(jnp.int32, (L, L), 0)
        col = lax.broadcasted_iota(jnp.int32, (L, L), 1)
        tril = col <= row
        lo = lax.broadcasted_iota(jnp.int32, (L, 128), 1) < 64
        lane1 = lax.broadcasted_iota(jnp.int32, (1, 128), 1)
        lo1 = lane1 < 64
        rowl = lax.broadcasted_iota(jnp.int32, (L, 128), 0)
        dt_x, ecs_x, eend_x = (_heads_to_pairs(m) for m in (dt, ecs, eend))
        dG = jnp.zeros((L, L), F32)
        dB = jnp.zeros((L, SSM_STATE), F32)
        dC = jnp.zeros((L, SSM_STATE), F32)
        dcs_t = jnp.zeros((L, L), F32)
        tails = jnp.zeros((1, 128), F32)
        dD_row = jnp.zeros((1, 128), F32)
        v_parts, prod_parts = [], []

        def tot(m):
            return jnp.sum(jnp.sum(m, axis=0, keepdims=True), axis=1, keepdims=True)

        for pp in range(4):
            hA, hB = 2 * pp, 2 * pp + 1
            lanes = slice(pp * 128, (pp + 1) * 128)

            def sel1(m):
                return jnp.where(lo1, m[:, hA:hA + 1], m[:, hB:hB + 1])

            X = xs_ref[:, lanes]
            dY = dY_all[:, lanes]
            dtsel = dt_x[:, lanes]
            xd = X * dtsel
            xdb = xd.astype(BF16)
            dYb = dY.astype(BF16)
            Hp = st_ref[pp]
            Hb = Hp.astype(BF16)
            dHn = dh_ref[pp]
            dHb = dHn.astype(BF16)
            ecs_sel = ecs_x[:, lanes]
            eend_sel = eend_x[:, lanes]
            dxd_state = jnp.dot(Bb, dHb, preferred_element_type=F32) * eend_sel
            yoff = jnp.dot(Cb, Hb, preferred_element_type=F32) * ecs_sel
            dYe = (dY * ecs_sel).astype(BF16)
            dC = dC + lax.dot_general(dYe, Hb, _NT, preferred_element_type=F32)
            dB = dB + lax.dot_general((xd * eend_sel).astype(BF16), dHb, _NT, preferred_element_type=F32)
            dh_ref[pp] = dHn * sel1(dec) + lax.dot_general(Cb, dYe, _TN, preferred_element_type=F32)
            q = xd * dxd_state
            dyq = dY * yoff - q
            qcol = jnp.sum(q, axis=0, keepdims=True)
            hcol = jnp.sum(dHn * Hp, axis=0, keepdims=True)
            dxd_diag = []
            for h, msk, msk1 in ((hA, lo, lo1), (hB, jnp.logical_not(lo), jnp.logical_not(lo1))):
                Lm = jnp.where(tril, jnp.exp(jnp.minimum(cs[:, h:h + 1] - csT[h:h + 1, :], 0.0)), 0.0)
                M = G * Lm
                dxd_diag.append(lax.dot_general(M.astype(BF16), dYb, _TN, preferred_element_type=F32))
                dM = lax.dot_general(jnp.where(msk, dY, 0.0).astype(BF16), xdb, _NT, preferred_element_type=F32)
                dG = dG + dM * Lm
                W = dM * M
                dcs_t = dcs_t + jnp.where(row == h, jnp.sum(W, axis=0, keepdims=True), 0.0)
                v_parts.append(W + jnp.where(msk, dyq, 0.0))
                tail = (jnp.sum(jnp.where(msk1, qcol, 0.0), axis=1, keepdims=True)
                        + dec[:, h:h + 1] * jnp.sum(jnp.where(msk1, hcol, 0.0), axis=1, keepdims=True))
                tails = tails + jnp.where(lane1 == h, tail, 0.0)
                yield
            dxd = jnp.where(lo, dxd_diag[0], dxd_diag[1]) + dxd_state
            prod_parts.append(dxd * X)
            dxs_ref[:, lanes] = dxd * dtsel + sel1(dsk) * dY
            dyx = jnp.sum(dY * X, axis=0, keepdims=True)
            sA = jnp.sum(jnp.where(lo1, dyx, 0.0), axis=1, keepdims=True)
            sB = jnp.sum(dyx, axis=1, keepdims=True) - sA
            dD_row = dD_row + jnp.where(lane1 == hA, sA, 0.0) + jnp.where(lane1 == hB, sB, 0.0)
            yield
        dGb = dG.astype(BF16)
        db_ref[...] = dB + lax.dot_general(dGb, Cb, _TN, preferred_element_type=F32)
        dc_ref[...] = dC + jnp.dot(dGb, Bb, preferred_element_type=F32)
        dcs_mat = _lane_block_sums(jnp.concatenate(v_parts, axis=1), 128) + jnp.where(rowl == L - 1, tails, 0.0)
        ddt_mat = _lane_block_sums(jnp.concatenate(prod_parts, axis=1), 64)
        dad = _rcumsum_rows(dcs_mat - dcs_t.T)
        draw = (a * dad + ddt_mat) * _sigmoid(raw)
        ddt_ref[...] = draw
        dpar_ref[0:1, :] += jnp.sum(draw, axis=0, keepdims=True)
        dpar_ref[1:2, :] += jnp.sum(dt * dad, axis=0, keepdims=True) * a
        dpar_ref[2:3, :] += dD_row

    return pl.pallas_call(
        body, name=name, grid=(nc,), in_specs=in_specs,
        out_specs=[pl.BlockSpec((L, CONV_DIM), lambda c: (ci(c), 0)),
                   pl.BlockSpec((L, D_INNER), lambda c: (ci(c), 0)),
                   pl.BlockSpec((SSM_GROUPS, L, 128), lambda c: (0, ci(c), 0)),
                   pl.BlockSpec((1, D_INNER), lambda c: (0, 0)),
                   pl.BlockSpec((SSM_GROUPS, 8, 128), lambda c: (0, 0, 0))],
        out_shape=[jax.ShapeDtypeStruct((T, CONV_DIM), F32), jax.ShapeDtypeStruct((T, IN_PROJ_PAD), BF16),
                   jax.ShapeDtypeStruct((SSM_GROUPS, T, 128), F32), jax.ShapeDtypeStruct((1, D_INNER), F32),
                   jax.ShapeDtypeStruct((SSM_GROUPS, 8, 128), F32)],
        scratch_shapes=[pltpu.VMEM((SSM_GROUPS, 4, 128, 128), F32)],
        compiler_params=_cparams(("arbitrary",)))(xbc_c, xbc_c, xbc_c, dtg, par, zx, gnw, y, st, dyn)


SB_KEYS = 512
SB_SCAN = 256
SB_STRIP = 256


def _tri(width, cond):
    kk = lax.broadcasted_iota(jnp.int32, (width, width), 0)
    jj = lax.broadcasted_iota(jnp.int32, (width, width), 1)
    return cond(kk, jj).astype(BF16)


def _sba_diag_mask():
    Bq = SB_BLOCK
    rowi = lax.broadcasted_iota(jnp.int32, (2 * Bq, Bq), 0)
    return lax.broadcasted_iota(jnp.int32, (2 * Bq, Bq), 1) < jnp.where(rowi >= Bq, rowi - Bq, rowi)


_LOG2E = 1.4426950408889634


def _softplus2(z2):
    return jnp.maximum(z2, 0.0) + jnp.log2(1.0 + jnp.exp2(-jnp.abs(z2)))


def _sba_sub_fwd(zb, c, U, mask):
    z2 = zb * _LOG2E
    s = _softplus2(z2)
    if mask is not None:
        s = jnp.where(mask, s, 0.0)
    R = c + jnp.dot(s.astype(BF16), U, preferred_element_type=F32)
    A = jnp.exp2(z2 - s - R)
    if mask is not None:
        A = jnp.where(mask, A, 0.0)
    return A.astype(BF16), R[:, 0:1] + s[:, 0:1]


def _sba_sub_bwd(zb, dAb, Lt, pc, pe, Uincl, Uexcl, mask):
    last = zb.shape[1] - 1
    z2 = zb * _LOG2E
    s = _softplus2(z2)
    g = z2 - s
    if mask is not None:
        s = jnp.where(mask, s, 0.0)
    P = pc + jnp.dot(s.astype(BF16), Uincl, preferred_element_type=F32)
    A = jnp.exp2(g - (Lt - P))
    if mask is not None:
        A = jnp.where(mask, A, 0.0)
    E = dAb * A
    PE = pe + jnp.dot(E.astype(BF16), Uexcl, preferred_element_type=F32)
    dz = E - jnp.exp2(g) * (E + PE)
    if mask is not None:
        dz = jnp.where(mask, dz, 0.0)
    return (A.astype(BF16), dz.astype(BF16), P[:, last:last + 1], PE[:, last:last + 1] + E[:, last:last + 1])


def _stack_heads(v):
    lo = lax.broadcasted_iota(jnp.int32, v.shape, 1) < 64
    zero = jnp.zeros_like(v)
    return jnp.concatenate([jnp.where(lo, v, zero), jnp.where(lo, zero, v)], axis=0)


def _unstack_heads(v):
    lo = lax.broadcasted_iota(jnp.int32, (SB_BLOCK, 128), 1) < 64
    return jnp.where(lo, v[:SB_BLOCK], v[SB_BLOCK:])


def _sba_rows(a):
    return slice(2 * a * SB_BLOCK, 2 * (a + 1) * SB_BLOCK)


def _sba_diag_case(a, b):
    Bq = SB_BLOCK
    if b * SB_SCAN >= (a + 1) * Bq:
        return "skip"
    if (b + 1) * SB_SCAN <= a * Bq:
        return "full"
    rowi = lax.broadcasted_iota(jnp.int32, (2 * Bq, SB_SCAN), 0)
    qpos = a * Bq + jnp.where(rowi >= Bq, rowi - Bq, rowi)
    return b * SB_SCAN + lax.broadcasted_iota(jnp.int32, (2 * Bq, SB_SCAN), 1) < qpos


def _sba_fwd(q, kv, *, name):
    T = q.shape[0]
    Bq = SB_BLOCK
    nsub = SB_KEYS // Bq
    nscan = SB_KEYS // SB_SCAN
    R = 2 * SB_KEYS
    assert T % SB_KEYS == 0 and SB_STRIP == 2 * Bq
    scale = 1.0 / math.sqrt(SB_HEAD_DIM)

    def body(q_ref, k_ref, v_ref, o_ref, lt_ref, z_s, a_s, c_s, acc_s):
        i = pl.program_id(1)
        U2 = _tri(SB_SCAN, lambda k, j: k > j)
        qs_all = jnp.concatenate([_stack_heads(q_ref[a * Bq:(a + 1) * Bq, :] * scale) for a in range(nsub)], axis=0)
        c_s[...] = jnp.zeros_like(c_s)
        acc_s[...] = jnp.zeros_like(acc_s)

        def scores(J, slot):
            off = pl.multiple_of(J * SB_KEYS, SB_KEYS)
            z_s[slot] = lax.dot_general(qs_all, k_ref[pl.ds(off, SB_KEYS), :], _NT, preferred_element_type=F32)

        def weights(slot, diag):
            for a in range(nsub):
                rows = _sba_rows(a)
                c = c_s[rows, :]
                for b in reversed(range(nscan)):
                    cols = slice(b * SB_SCAN, (b + 1) * SB_SCAN)
                    case = _sba_diag_case(a, b) if diag else "full"
                    if isinstance(case, str) and case == "skip":
                        a_s[slot, rows, cols] = jnp.zeros((2 * Bq, SB_SCAN), BF16)
                        continue
                    A, c = _sba_sub_fwd(z_s[slot, rows, cols], c, U2, None if isinstance(case, str) else case)
                    a_s[slot, rows, cols] = A
                c_s[rows, :] = c

        def values(J, slot):
            off = pl.multiple_of(J * SB_KEYS, SB_KEYS)
            acc_s[...] += jnp.dot(a_s[slot], v_ref[pl.ds(off, SB_KEYS), :], preferred_element_type=F32)

        scores(i, 0)
        weights(0, True)
        scores(jnp.maximum(i - 1, 0), 1)

        def two_steps(u, _):
            t = 2 * u + 1
            weights(1, False)
            scores(jnp.maximum(i - t - 1, 0), 0)
            values(i - t + 1, 0)
            weights(0, False)
            scores(jnp.maximum(i - t - 2, 0), 1)
            values(i - t, 1)
            return 0

        lax.fori_loop(0, i // 2, two_steps, 0)
        odd = lax.rem(i, 2) == 1

        @pl.when(jnp.logical_not(odd))
        def _():
            values(0, 0)

        @pl.when(odd)
        def _():
            weights(1, False)
            values(1, 0)
            values(0, 1)
        for a in range(nsub):
            o_ref[a * Bq:(a + 1) * Bq, :] = _unstack_heads(acc_s[_sba_rows(a), :]).astype(BF16)
            lt_ref[a * Bq:(a + 1) * Bq, :] = _unstack_heads(jnp.broadcast_to(c_s[_sba_rows(a), :], (2 * Bq, 128)))

    return pl.pallas_call(
        body, name=name, grid=(SB_HEADS // 2, T // SB_KEYS),
        in_specs=[pl.BlockSpec((SB_KEYS, 128), lambda p, i: (i, p)), pl.BlockSpec((T, 128), lambda p, i: (0, p)),
                  pl.BlockSpec((T, 128), lambda p, i: (0, p + SB_HEADS // 2))],
        out_specs=[pl.BlockSpec((SB_KEYS, 128), lambda p, i: (i, p)),
                   pl.BlockSpec((None, SB_KEYS, 128), lambda p, i: (p, i, 0))],
        out_shape=[jax.ShapeDtypeStruct((T, D_MODEL), BF16), jax.ShapeDtypeStruct((SB_HEADS // 2, T, 128), F32)],
        scratch_shapes=[pltpu.VMEM((2, R, SB_KEYS), F32), pltpu.VMEM((2, R, SB_KEYS), BF16),
                        pltpu.VMEM((R, 1), F32), pltpu.VMEM((R, 128), F32)],
        compiler_params=_cparams(("parallel", "parallel")))(q, kv, kv)


def _sba_bwd(q, kv, lt, do, *, name):
    T = q.shape[0]
    Bq = SB_BLOCK
    nq = T // SB_KEYS
    nsub = SB_KEYS // Bq
    nscan = SB_KEYS // SB_SCAN
    R = 2 * SB_KEYS
    assert T % SB_KEYS == 0 and SB_STRIP == 2 * Bq
    scale = 1.0 / math.sqrt(SB_HEAD_DIM)

    def body(q_ref, k_ref, v_ref, lt_ref, do_ref, dq_ref, dk_ref, dv_ref, dk_acc, dv_acc,
             z_s, da_s, a_s, dz_s, pc_s, pe_s, lt_s, dq_s):
        i = pl.program_id(1)

        @pl.when(i == 0)
        def _():
            dk_acc[...] = jnp.zeros_like(dk_acc)
            dv_acc[...] = jnp.zeros_like(dv_acc)

        Uincl = _tri(SB_SCAN, lambda k, j: k <= j)
        Uexcl = _tri(SB_SCAN, lambda k, j: k < j)
        qs, dos = [], []
        for a in range(nsub):
            rows = slice(a * Bq, (a + 1) * Bq)
            qs.append(_stack_heads(q_ref[rows, :] * scale))
            dos.append(_stack_heads(do_ref[rows, :]))
            lt_s[_sba_rows(a), :] = jnp.concatenate([lt_ref[rows, 0:1], lt_ref[rows, 64:65]], axis=0)
        qs_all = jnp.concatenate(qs, axis=0)
        dos_all = jnp.concatenate(dos, axis=0)
        pc_s[...] = jnp.zeros_like(pc_s)
        pe_s[...] = jnp.zeros_like(pe_s)
        a_s[1] = jnp.zeros((R, SB_KEYS), BF16)
        dz_s[1] = jnp.zeros((R, SB_KEYS), BF16)

        def scores(J, slot):
            off = pl.multiple_of(J * SB_KEYS, SB_KEYS)
            z_s[slot] = lax.dot_general(qs_all, k_ref[pl.ds(off, SB_KEYS), :], _NT, preferred_element_type=F32)
            da_s[slot] = lax.dot_general(dos_all, v_ref[pl.ds(off, SB_KEYS), :], _NT, preferred_element_type=F32)

        def gradients(slot, diag):
            for a in range(nsub):
                rows = _sba_rows(a)
                pc, pe, Lt = pc_s[rows, :], pe_s[rows, :], lt_s[rows, :]
                for b in range(nscan):
                    cols = slice(b * SB_SCAN, (b + 1) * SB_SCAN)
                    case = _sba_diag_case(a, b) if diag else "full"
                    if isinstance(case, str) and case == "skip":
                        a_s[slot, rows, cols] = jnp.zeros((2 * Bq, SB_SCAN), BF16)
                        dz_s[slot, rows, cols] = jnp.zeros((2 * Bq, SB_SCAN), BF16)
                        continue
                    A, dz, pc, pe = _sba_sub_bwd(z_s[slot, rows, cols], da_s[slot, rows, cols], Lt, pc, pe, Uincl, Uexcl,
                                                 None if isinstance(case, str) else case)
                    a_s[slot, rows, cols] = A
                    dz_s[slot, rows, cols] = dz
                pc_s[rows, :] = pc
                pe_s[rows, :] = pe

        def products(J, slot):
            off = pl.multiple_of(J * SB_KEYS, SB_KEYS)
            dzt = dz_s[slot]
            dk_acc[pl.ds(off, SB_KEYS), :] += lax.dot_general(dzt, qs_all, _TN, preferred_element_type=F32)
            dv_acc[pl.ds(off, SB_KEYS), :] += lax.dot_general(a_s[slot], dos_all, _TN, preferred_element_type=F32)
            dq_s[...] += jnp.dot(dzt, k_ref[pl.ds(off, SB_KEYS), :], preferred_element_type=F32)

        dq_s[...] = jnp.zeros_like(dq_s)
        scores(0, 0)

        def two_steps(u, _):
            t = 2 * u
            gradients(0, False)
            scores(t + 1, 1)
            products(jnp.maximum(t - 1, 0), 1)
            gradients(1, False)
            scores(t + 2, 0)
            products(t, 0)
            return 0

        lax.fori_loop(0, i // 2, two_steps, 0)
        odd = lax.rem(i, 2) == 1

        @pl.when(jnp.logical_not(odd))
        def _():
            gradients(0, True)
            products(jnp.maximum(i - 1, 0), 1)
            products(i, 0)

        @pl.when(odd)
        def _():
            gradients(0, False)
            scores(i, 1)
            products(jnp.maximum(i - 2, 0), 1)
            gradients(1, True)
            products(i - 1, 0)
            products(i, 1)

        for a in range(nsub):
            dq_ref[a * Bq:(a + 1) * Bq, :] = (_unstack_heads(dq_s[_sba_rows(a), :]) * scale).astype(BF16)

        @pl.when(i == nq - 1)
        def _():
            dk_ref[...] = dk_acc[...].astype(BF16)
            dv_ref[...] = dv_acc[...].astype(BF16)

    return pl.pallas_call(
        body, name=name, grid=(SB_HEADS // 2, nq),
        in_specs=[pl.BlockSpec((SB_KEYS, 128), lambda p, i: (i, p)), pl.BlockSpec((T, 128), lambda p, i: (0, p)),
                  pl.BlockSpec((T, 128), lambda p, i: (0, p + SB_HEADS // 2)),
                  pl.BlockSpec((None, SB_KEYS, 128), lambda p, i: (p, i, 0)),
                  pl.BlockSpec((SB_KEYS, 128), lambda p, i: (i, p))],
        out_specs=[pl.BlockSpec((SB_KEYS, 128), lambda p, i: (i, p)), pl.BlockSpec((T, 128), lambda p, i: (0, p)),
                   pl.BlockSpec((T, 128), lambda p, i: (0, p))],
        out_shape=[jax.ShapeDtypeStruct((T, D_MODEL), BF16), jax.ShapeDtypeStruct((T, D_MODEL), BF16),
                   jax.ShapeDtypeStruct((T, D_MODEL), BF16)],
        scratch_shapes=[pltpu.VMEM((T, 128), F32), pltpu.VMEM((T, 128), F32),
                        pltpu.VMEM((2, R, SB_KEYS), F32), pltpu.VMEM((2, R, SB_KEYS), F32),
                        pltpu.VMEM((2, R, SB_KEYS), BF16), pltpu.VMEM((2, R, SB_KEYS), BF16),
                        pltpu.VMEM((R, 1), F32), pltpu.VMEM((R, 1), F32), pltpu.VMEM((R, 1), F32),
                        pltpu.VMEM((R, 128), F32)],
        compiler_params=_cparams(("parallel", "arbitrary")))(q, kv, kv, lt, do)


def _sba_fwd_old(q, kv, *, name):
    T = q.shape[0]
    Bq = SB_BLOCK
    nsub = SB_KEYS // Bq
    assert T % SB_KEYS == 0
    scale = 1.0 / math.sqrt(SB_HEAD_DIM)

    def body(q_ref, k_ref, v_ref, o_ref, lt_ref):
        I = pl.program_id(1)
        U1 = _tri(Bq, lambda k, j: k > j)
        U2 = _tri(SB_SCAN, lambda k, j: k > j)
        dmask = _sba_diag_mask()
        qs = [_stack_heads(q_ref[a * Bq:(a + 1) * Bq, :] * scale) for a in range(nsub)]
        cs, accs = [], []
        for a in range(nsub):
            c = jnp.zeros((2 * Bq, 1), F32)
            acc = jnp.zeros((2 * Bq, 128), F32)
            for b in range(a, -1, -1):
                off = pl.multiple_of(I * SB_KEYS + b * Bq, Bq)
                zb = lax.dot_general(qs[a], k_ref[pl.ds(off, Bq), :], _NT, preferred_element_type=F32)
                A, c = _sba_sub_fwd(zb, c, U1, dmask if b == a else None)
                acc = acc + jnp.dot(A, v_ref[pl.ds(off, Bq), :], preferred_element_type=F32)
            cs.append(c)
            accs.append(acc)
        qs_all = jnp.concatenate(qs, axis=0)

        def step(n, carry):
            c, acc = carry
            off = pl.multiple_of((I - 1 - n) * SB_KEYS, SB_KEYS)
            z = lax.dot_general(qs_all, k_ref[pl.ds(off, SB_KEYS), :], _NT, preferred_element_type=F32)
            parts = [None] * (SB_KEYS // SB_SCAN)
            for b in reversed(range(SB_KEYS // SB_SCAN)):
                parts[b], c = _sba_sub_fwd(z[:, b * SB_SCAN:(b + 1) * SB_SCAN], c, U2, None)
            return c, acc + jnp.dot(jnp.concatenate(parts, axis=1), v_ref[pl.ds(off, SB_KEYS), :],
                                    preferred_element_type=F32)

        c, acc = lax.fori_loop(0, I, step, (jnp.concatenate(cs, axis=0), jnp.concatenate(accs, axis=0)))
        for a in range(nsub):
            rows = slice(2 * a * Bq, 2 * (a + 1) * Bq)
            o_ref[a * Bq:(a + 1) * Bq, :] = _unstack_heads(acc[rows]).astype(BF16)
            lt_ref[a * Bq:(a + 1) * Bq, :] = _unstack_heads(jnp.broadcast_to(c[rows], (2 * Bq, 128)))

    return pl.pallas_call(
        body, name=name, grid=(SB_HEADS // 2, T // SB_KEYS),
        in_specs=[pl.BlockSpec((SB_KEYS, 128), lambda p, i: (i, p)), pl.BlockSpec((T, 128), lambda p, i: (0, p)),
                  pl.BlockSpec((T, 128), lambda p, i: (0, p + SB_HEADS // 2))],
        out_specs=[pl.BlockSpec((SB_KEYS, 128), lambda p, i: (i, p)),
                   pl.BlockSpec((None, SB_KEYS, 128), lambda p, i: (p, i, 0))],
        out_shape=[jax.ShapeDtypeStruct((T, D_MODEL), BF16), jax.ShapeDtypeStruct((SB_HEADS // 2, T, 128), F32)],
        compiler_params=_cparams(("parallel", "parallel")))(q, kv, kv)


def _sba_bwd_old(q, kv, lt, do, *, name):
    T = q.shape[0]
    Bq = SB_BLOCK
    nq = T // SB_KEYS
    nsub = SB_KEYS // Bq
    assert T % SB_KEYS == 0
    scale = 1.0 / math.sqrt(SB_HEAD_DIM)

    def body(q_ref, k_ref, v_ref, lt_ref, do_ref, dq_ref, dk_ref, dv_ref, dk_acc, dv_acc,
             z_s, da_s, a_s, dz_s, pc_s, pe_s, lt_s):
        i = pl.program_id(1)

        @pl.when(i == 0)
        def _():
            dk_acc[...] = jnp.zeros_like(dk_acc)
            dv_acc[...] = jnp.zeros_like(dv_acc)

        Uincl1 = _tri(Bq, lambda k, j: k <= j)
        Uexcl1 = _tri(Bq, lambda k, j: k < j)
        Uincl2 = _tri(SB_SCAN, lambda k, j: k <= j)
        Uexcl2 = _tri(SB_SCAN, lambda k, j: k < j)
        dmask = _sba_diag_mask()
        qs, dos, lts = [], [], []
        for a in range(nsub):
            rows = slice(a * Bq, (a + 1) * Bq)
            qs.append(_stack_heads(q_ref[rows, :] * scale))
            dos.append(_stack_heads(do_ref[rows, :]))
            lts.append(jnp.concatenate([lt_ref[rows, 0:1], lt_ref[rows, 64:65]], axis=0))
        qs_all = jnp.concatenate(qs, axis=0)
        dos_all = jnp.concatenate(dos, axis=0)
        lt_all = jnp.concatenate(lts, axis=0)

        R = 2 * nsub * Bq
        pc_s[...] = jnp.zeros_like(pc_s)
        pe_s[...] = jnp.zeros_like(pe_s)
        lt_s[...] = lt_all

        def scores(J, slot):
            off = pl.multiple_of(J * SB_KEYS, SB_KEYS)
            z_s[slot] = lax.dot_general(qs_all, k_ref[pl.ds(off, SB_KEYS), :], _NT, preferred_element_type=F32)
            da_s[slot] = lax.dot_general(dos_all, v_ref[pl.ds(off, SB_KEYS), :], _NT, preferred_element_type=F32)

        def elementwise(slot):
            for r in range(R // SB_STRIP):
                rows = slice(r * SB_STRIP, (r + 1) * SB_STRIP)
                pc, pe, Lt = pc_s[rows, :], pe_s[rows, :], lt_s[rows, :]
                for b in range(SB_KEYS // SB_SCAN):
                    cols = slice(b * SB_SCAN, (b + 1) * SB_SCAN)
                    A, dz, pc, pe = _sba_sub_bwd(z_s[slot, rows, cols], da_s[slot, rows, cols], Lt, pc, pe,
                                                 Uincl2, Uexcl2, None)
                    a_s[slot, rows, cols] = A
                    dz_s[slot, rows, cols] = dz
                pc_s[rows, :] = pc
                pe_s[rows, :] = pe

        def outputs(J, slot, dq_acc):
            off = pl.multiple_of(J * SB_KEYS, SB_KEYS)
            dzt = dz_s[slot]
            dk_acc[pl.ds(off, SB_KEYS), :] += lax.dot_general(dzt, qs_all, _TN, preferred_element_type=F32)
            dv_acc[pl.ds(off, SB_KEYS), :] += lax.dot_general(a_s[slot], dos_all, _TN, preferred_element_type=F32)
            return dq_acc + jnp.dot(dzt, k_ref[pl.ds(off, SB_KEYS), :], preferred_element_type=F32)

        a_s[1] = jnp.zeros((R, SB_KEYS), BF16)
        dz_s[1] = jnp.zeros((R, SB_KEYS), BF16)
        last = jnp.maximum(i - 1, 0)
        scores(0, 0)

        def step(J, dq_acc):
            slot = lax.rem(J, 2)
            elementwise(slot)
            scores(jnp.minimum(J + 1, last), 1 - slot)
            return outputs(jnp.maximum(J - 1, 0), 1 - slot, dq_acc)

        dq_acc = lax.fori_loop(0, i, step, jnp.zeros((R, 128), F32))
        dq_acc = outputs(last, lax.rem(i + 1, 2), dq_acc)
        pc, pe = pc_s[...], pe_s[...]
        for a in range(nsub):
            rows = slice(2 * a * Bq, 2 * (a + 1) * Bq)
            pca, pea, dqa = pc[rows], pe[rows], dq_acc[rows]
            for b in range(a + 1):
                off = pl.multiple_of(i * SB_KEYS + b * Bq, Bq)
                kb = k_ref[pl.ds(off, Bq), :]
                zb = lax.dot_general(qs[a], kb, _NT, preferred_element_type=F32)
                dAb = lax.dot_general(dos[a], v_ref[pl.ds(off, Bq), :], _NT, preferred_element_type=F32)
                A, dz, pca, pea = _sba_sub_bwd(zb, dAb, lts[a], pca, pea, Uincl1, Uexcl1, dmask if b == a else None)
                dqa = dqa + jnp.dot(dz, kb, preferred_element_type=F32)
                dk_acc[pl.ds(off, Bq), :] += lax.dot_general(dz, qs[a], _TN, preferred_element_type=F32)
                dv_acc[pl.ds(off, Bq), :] += lax.dot_general(A, dos[a], _TN, preferred_element_type=F32)
            dq_ref[a * Bq:(a + 1) * Bq, :] = (_unstack_heads(dqa) * scale).astype(BF16)

        @pl.when(i == nq - 1)
        def _():
            dk_ref[...] = dk_acc[...].astype(BF16)
            dv_ref[...] = dv_acc[...].astype(BF16)

    return pl.pallas_call(
        body, name=name, grid=(SB_HEADS // 2, nq),
        in_specs=[pl.BlockSpec((SB_KEYS, 128), lambda p, i: (i, p)), pl.BlockSpec((T, 128), lambda p, i: (0, p)),
                  pl.BlockSpec((T, 128), lambda p, i: (0, p + SB_HEADS // 2)),
                  pl.BlockSpec((None, SB_KEYS, 128), lambda p, i: (p, i, 0)),
                  pl.BlockSpec((SB_KEYS, 128), lambda p, i: (i, p))],
        out_specs=[pl.BlockSpec((SB_KEYS, 128), lambda p, i: (i, p)), pl.BlockSpec((T, 128), lambda p, i: (0, p)),
                   pl.BlockSpec((T, 128), lambda p, i: (0, p))],
        out_shape=[jax.ShapeDtypeStruct((T, D_MODEL), BF16), jax.ShapeDtypeStruct((T, D_MODEL), BF16),
                   jax.ShapeDtypeStruct((T, D_MODEL), BF16)],
        scratch_shapes=[pltpu.VMEM((T, 128), F32), pltpu.VMEM((T, 128), F32),
                        pltpu.VMEM((2, 2 * SB_KEYS, SB_KEYS), F32), pltpu.VMEM((2, 2 * SB_KEYS, SB_KEYS), F32),
                        pltpu.VMEM((2, 2 * SB_KEYS, SB_KEYS), BF16), pltpu.VMEM((2, 2 * SB_KEYS, SB_KEYS), BF16),
                        pltpu.VMEM((2 * SB_KEYS, 1), F32), pltpu.VMEM((2 * SB_KEYS, 1), F32),
                        pltpu.VMEM((2 * SB_KEYS, 1), F32)],
        compiler_params=_cparams(("parallel", "arbitrary")))(q, kv, kv, lt, do)


def _loss_head(h, tgt, w, *, name, tt=512):
    T, D = h.shape
    tt = min(tt, T)

    def body(h_ref, t_ref, w_ref, loss_ref, dh_ref, dw_ref):
        i = pl.program_id(0)
        hv = h_ref[...]
        wv = w_ref[...]
        r = lax.rsqrt(jnp.mean(hv * hv, axis=-1, keepdims=True) + EPS)
        xhat = hv * r
        err = xhat * wv - t_ref[...]
        part = 0.5 * jnp.sum(jnp.mean(err * err, axis=-1, keepdims=True), axis=0, keepdims=True)
        dy = err * (1.0 / D)
        dxh = dy * wv
        dh_ref[...] = r * (dxh - xhat * jnp.mean(dxh * xhat, axis=-1, keepdims=True))
        dwc = jnp.sum(dy * xhat, axis=0, keepdims=True)

        @pl.when(i == 0)
        def _():
            loss_ref[...] = jnp.broadcast_to(part, loss_ref.shape)
            dw_ref[...] = dwc

        @pl.when(i > 0)
        def _():
            loss_ref[...] += jnp.broadcast_to(part, loss_ref.shape)
            dw_ref[...] += dwc

    return pl.pallas_call(
        body, name=name, grid=(T // tt,),
        in_specs=[pl.BlockSpec((tt, D), lambda i: (i, 0)), pl.BlockSpec((tt, D), lambda i: (i, 0)),
                  pl.BlockSpec((1, D), lambda i: (0, 0))],
        out_specs=[pl.BlockSpec((1, 128), lambda i: (0, 0)), pl.BlockSpec((tt, D), lambda i: (i, 0)),
                   pl.BlockSpec((1, D), lambda i: (0, 0))],
        out_shape=[jax.ShapeDtypeStruct((1, 128), F32), jax.ShapeDtypeStruct((T, D), F32),
                   jax.ShapeDtypeStruct((1, D), F32)],
        compiler_params=_cparams(("arbitrary",)))(h, tgt, w.reshape(1, D))


def _adamw(parts, w, m, v, *, name, tr=256):
    plist = list(parts) if isinstance(parts, (list, tuple)) else [parts]
    P, _, C = plist[0].shape
    R = sum(a.shape[1] for a in plist)
    tr = min(tr, R)
    assert all(a.shape[1] % tr == 0 for a in plist), (name, R, tr)
    nbs = [a.shape[1] // tr for a in plist]
    offs = [sum(nbs[:l]) for l in range(len(nbs))]
    c1 = 1.0 - ADAM_B1 ** ADAM_STEP
    c2 = 1.0 - ADAM_B2 ** ADAM_STEP

    def body(*refs):
        p_refs = refs[:len(plist)]
        w_ref, m_ref, v_ref, g_ref, d_ref, nm_ref, nv_ref = refs[len(plist):]
        i = pl.program_id(0)
        g = None
        for l, p_ref in enumerate(p_refs):
            gl = p_ref[0].astype(F32)
            for k in range(1, P):
                gl = gl + p_ref[k].astype(F32)
            g = gl if g is None else jnp.where(i >= offs[l], gl, g)
        mn = ADAM_B1 * m_ref[...] + (1.0 - ADAM_B1) * g
        vn = ADAM_B2 * v_ref[...] + (1.0 - ADAM_B2) * (g * g)
        g_ref[...] = g
        nm_ref[...] = mn
        nv_ref[...] = vn
        d_ref[...] = -ADAM_LR * ((mn / c1) / (jnp.sqrt(vn / c2) + ADAM_EPS) + ADAM_WD * w_ref[...])

    spec = pl.BlockSpec((tr, C), lambda i: (i, 0))
    sds = jax.ShapeDtypeStruct((R, C), F32)
    return pl.pallas_call(
        body, name=name, grid=(R // tr,),
        in_specs=[pl.BlockSpec((P, tr, C), functools.partial(lambda i, o, n: (0, jnp.clip(i - o, 0, n - 1), 0), o=o, n=n))
                  for o, n in zip(offs, nbs)] + [spec, spec, spec],
        out_specs=[spec, spec, spec, spec], out_shape=[sds, sds, sds, sds],
        compiler_params=_cparams(("parallel",)))(*plist, w, m, v)


def _all_gather(shards, *, name):
    n = len(shards)

    def body(*refs):
        ins, outs = refs[:n], refs[n:2 * n]
        send_sems, recv_sems, local_sems = refs[2 * n:]
        x, y, c = lax.axis_index("x"), lax.axis_index("y"), lax.axis_index("c")
        me, sib = (x, y, c), (x, y, 1 - c)
        chips = [(1 - x, y), (x, 1 - y), (1 - x, 1 - y)]

        def slot(p):
            return 4 * p[0] + 2 * p[1] + p[2]

        def cp(a, k, block, to, src=None):
            dst = outs[a].at[slot(block)]
            return pltpu.make_async_remote_copy(src_ref=dst if src is None else src, dst_ref=dst,
                                                send_sem=send_sems.at[a, k], recv_sem=recv_sems.at[a, k],
                                                device_id=to, device_id_type=_MESH)

        mine = [pltpu.make_async_copy(ins[a], outs[a].at[slot(me)], local_sems.at[a]) for a in range(n)]
        for m in mine:
            m.start()
        first = []
        for a in range(n):
            first.append(cp(a, 0, me, sib, src=ins[a]))
            for j, chip in enumerate(chips):
                first.append(cp(a, 1 + j, me, (*chip, c), src=ins[a]))
        for f in first:
            f.start()
        passed = []
        for j, chip in enumerate(chips):
            for a in range(n):
                cp(a, 1 + j, (*chip, c), me).wait_recv()
                f = cp(a, 4 + j, (*chip, c), sib)
                f.start()
                passed.append(f)
        for a in range(n):
            cp(a, 0, sib, me).wait_recv()
            for j, chip in enumerate(chips):
                cp(a, 4 + j, (*chip, 1 - c), me).wait_recv()
        for f in first + passed:
            f.wait_send()
        for m in mine:
            m.wait()

    return pl.pallas_call(
        body, name=name, in_specs=[_ANY] * n, out_specs=[_ANY] * n,
        out_shape=[jax.ShapeDtypeStruct((N_DEV,) + s.shape, s.dtype) for s in shards],
        scratch_shapes=[pltpu.SemaphoreType.DMA((n, 7)), pltpu.SemaphoreType.DMA((n, 7)),
                        pltpu.SemaphoreType.DMA((n,))])(*shards)


def _exchange(blocks, *, name):
    n = len(blocks)

    def body(*refs):
        ins, outs = refs[:n], refs[n:2 * n]
        send_sems, recv_sems, local_sems = refs[2 * n:]
        x, y, c = lax.axis_index("x"), lax.axis_index("y"), lax.axis_index("c")
        me = 4 * x + 2 * y + c
        mine = [pltpu.make_async_copy(ins[a].at[me], outs[a].at[me], local_sems.at[a]) for a in range(n)]
        for m in mine:
            m.start()
        copies = []
        for r in range(1, N_DEV):
            rx, ry, rc = (r >> 2) & 1, (r >> 1) & 1, r & 1
            px, py, pc = (1 - x if rx else x), (1 - y if ry else y), (1 - c if rc else c)
            peer = 4 * px + 2 * py + pc
            for a in range(n):
                copies.append((pltpu.make_async_remote_copy(
                    src_ref=ins[a].at[peer], dst_ref=outs[a].at[me], send_sem=send_sems.at[a, r - 1],
                    recv_sem=recv_sems.at[a, r - 1], device_id=(px, py, pc), device_id_type=_MESH),
                    pltpu.make_async_remote_copy(
                    src_ref=ins[a].at[peer], dst_ref=outs[a].at[peer], send_sem=send_sems.at[a, r - 1],
                    recv_sem=recv_sems.at[a, r - 1], device_id=(px, py, pc), device_id_type=_MESH)))
        for snd, _ in copies:
            snd.start()
        for _, rcv in copies:
            rcv.wait_recv()
        for snd, _ in copies:
            snd.wait_send()
        for m in mine:
            m.wait()

    return pl.pallas_call(
        body, name=name, in_specs=[_ANY] * n, out_specs=[_ANY] * n,
        out_shape=[jax.ShapeDtypeStruct(b.shape, b.dtype) for b in blocks],
        scratch_shapes=[pltpu.SemaphoreType.DMA((n, 7)), pltpu.SemaphoreType.DMA((n, 7)),
                        pltpu.SemaphoreType.DMA((n,))])(*blocks)


_HBM = pl.BlockSpec(memory_space=pltpu.HBM)
_SEM = pl.BlockSpec(memory_space=pltpu.SEMAPHORE)
_EFFECT = pltpu.SideEffectType.DATAFLOW_SIDE_EFFECTING


def _peers():
    x, y, c = lax.axis_index("x"), lax.axis_index("y"), lax.axis_index("c")
    out = []
    for r in range(1, N_DEV):
        px = 1 - x if (r >> 2) & 1 else x
        py = 1 - y if (r >> 1) & 1 else y
        pc = 1 - c if r & 1 else c
        out.append(((px, py, pc), 4 * px + 2 * py + pc))
    return 4 * x + 2 * y + c, out


def _push_copy(src_ref, land_ref, send_sems, recv_sems, a, k, me, peer, peer_slot, scatter, arriving):
    src = src_ref.at[peer_slot] if scatter else src_ref
    return pltpu.make_async_remote_copy(
        src_ref=src, dst_ref=land_ref.at[peer_slot if arriving else me], send_sem=send_sems.at[a * (N_DEV - 1) + k],
        recv_sem=recv_sems.at[a * (N_DEV - 1) + k], device_id=peer, device_id_type=_MESH)


def _push_start(srcs, *, scatter, name):
    n = len(srcs)
    lands = [lax.empty(s.shape if scatter else (N_DEV,) + s.shape, s.dtype) for s in srcs]

    def body(*refs):
        src_refs, land_refs = refs[:n], refs[n:2 * n]
        send_sems, recv_sems = refs[2 * n], refs[2 * n + 1]
        token = refs[-1]
        me, peers = _peers()
        for k, (peer, slot) in enumerate(peers):
            for a in range(n):
                _push_copy(src_refs[a], land_refs[a], send_sems, recv_sems, a, k, me, peer, slot, scatter, False).start()
        token[...] = jnp.zeros_like(token)

    hbm = lambda a: pltpu.HBM(a.shape, a.dtype)
    outs = pl.pallas_call(
        body, name=name,
        out_shape=(pltpu.SemaphoreType.DMA((n * (N_DEV - 1),)), pltpu.SemaphoreType.DMA((n * (N_DEV - 1),)),
                   *[hbm(s) for s in srcs], *[hbm(l) for l in lands], jax.ShapeDtypeStruct((8, 128), F32)),
        in_specs=[_HBM] * (2 * n),
        out_specs=(_SEM, _SEM, *([_HBM] * (2 * n)), pl.BlockSpec(memory_space=pltpu.VMEM)),
        input_output_aliases={i: 2 + i for i in range(2 * n)},
        compiler_params=pltpu.CompilerParams(has_side_effects=_EFFECT),
    )(*[pltpu.with_memory_space_constraint(s, pltpu.HBM) for s in srcs],
      *[pltpu.with_memory_space_constraint(l, pltpu.HBM) for l in lands])
    return dict(send=outs[0], recv=outs[1], srcs=list(outs[2:2 + n]), lands=list(outs[2 + n:2 + 2 * n]),
                token=outs[-1], scatter=scatter, n=n)


def _push_wait(h, after, *, name):
    n, scatter = h["n"], h["scatter"]

    def body(*refs):
        src_refs, land_refs = refs[:n], refs[n:2 * n]
        send_sems, recv_sems = refs[2 * n], refs[2 * n + 1]
        me, peers = _peers()
        for k, (peer, slot) in enumerate(peers):
            for a in range(n):
                cp = _push_copy(src_refs[a], land_refs[a], send_sems, recv_sems, a, k, me, peer, slot, scatter, True)
                cp.wait_send()
                cp.wait_recv()

    hbm = lambda a: pltpu.HBM(a.shape, a.dtype)
    outs = pl.pallas_call(
        body, name=name,
        out_shape=(*[hbm(s) for s in h["srcs"]], *[hbm(l) for l in h["lands"]]),
        in_specs=[_HBM] * (2 * n) + [_SEM, _SEM, _ANY], out_specs=tuple([_HBM] * (2 * n)),
        input_output_aliases={i: i for i in range(2 * n)},
        compiler_params=pltpu.CompilerParams(has_side_effects=_EFFECT),
    )(*h["srcs"], *h["lands"], h["send"], h["recv"], after)
    return list(outs[:n]), list(outs[n:])


def _ffn_fwd(h, nw, w_up, conv_w, conv_b, w_down, tag):
    a3 = _mm_fwd(h, w_up, norm_w=nw, name=f"ffn{tag}_up", out_dtype=BF16, halves=True, tm=1024, tn=2816)
    p = _ffn_conv_fwd3(a3, conv_w, conv_b.reshape(1, -1), name=f"ffn{tag}_conv")
    h_out = _mm_fwd(p, w_down, residual=h, name=f"ffn{tag}_down", tm=1024, tn=512)
    return h_out, (a3, p)


def _ffn_bwd(dh, h, saved, nw, w_up, conv_w, conv_b, w_down, tag):
    a3, p = saved
    g_down = _mm_tn(p, dh, name=f"ffn{tag}_down_wg", tk1=1408, tn=1024)
    dp = _mm_nt(dh, w_down, name=f"ffn{tag}_down_dg", out_dtype=BF16, tm=512, tn=2816, tk=1024)
    dhid3, dw3, db3 = _ffn_conv_bwd3(a3, conv_w, conv_b.reshape(1, -1), dp, name=f"ffn{tag}_conv_bwd")
    da3 = _conv_bwd_in3(dhid3, conv_w, K=FFN_CONV, name=f"ffn{tag}_conv_bwd_in")
    g_up = _mm_tn(h, da3, norm_w=nw, name=f"ffn{tag}_up_wg", tn=2816, tt=1024)
    dh_out, g_nw = _mm_nt(da3, w_up, epi=(h, nw, dh), name=f"ffn{tag}_up_dg", tm=1024, tk=1408)
    g_cw = jnp.concatenate([dw3[0], dw3[1]], axis=1)
    g_cb = jnp.concatenate([db3[0], db3[1]], axis=1)
    return dh_out, dict(norm=g_nw.reshape(-1), up=g_up, conv_w=g_cw, conv_b=g_cb.reshape(-1), down=g_down)


def _local_step(x, tgt, W):
    T = x.shape[0]
    f = {}
    zx = _mm_fwd(x, W["in_w"], norm_w=W["ssm_norm_w"], name="ssm_in", tm=1024, tn=896)
    xbc_c = _ssm_conv_fwd(zx, W["ssm_conv_w"], W["ssm_conv_b"].reshape(1, -1), name="ssm_conv")
    dt_raw = zx[:, D_INNER + CONV_DIM:IN_PROJ_DIM]
    dtg = jnp.pad(dt_raw.reshape(T, SSM_GROUPS, 8).transpose(1, 0, 2), ((0, 0), (0, 0), (0, 120)))
    par = jnp.stack([W["ssm_dt_bias"].reshape(SSM_GROUPS, 8), W["ssm_a_log"].reshape(SSM_GROUPS, 8),
                     W["ssm_d"].reshape(SSM_GROUPS, 8)], axis=1)
    par = jnp.pad(par, ((0, 0), (0, 5), (0, 120)))
    gnw = W["ssm_gate_norm_w"].reshape(1, D_INNER)
    y, yn, st = _ssd_fwd(xbc_c, zx, dtg, par, gnw, name="ssd_fwd")
    h1 = _mm_fwd(yn, W["ssm_out_w"], residual=x, name="ssm_out", tm=1024, tn=512)
    h2, ffn0 = _ffn_fwd(h1, W["ffn_norm_w"][0], W["ffn_up_w"][0], W["ffn_conv_w"][0], W["ffn_conv_b"][0],
                        W["ffn_down_w"][0], "0")
    q = _mm_fwd(h2, W["w_q"], norm_w=W["attn_norm_w"], out_dtype=BF16, name="attn_q", tm=1024, tn=1024)
    kv = _mm_fwd(h2, W["w_kv"], norm_w=W["kv_norm_w"], out_dtype=BF16, name="attn_kv", tm=1024, tn=1024)
    o, lt = _sba_fwd(q, kv, name="sba_fwd")
    h3 = _mm_fwd(o, W["w_o"], residual=h2, name="attn_o", tm=1024, tn=512)
    h4, ffn1 = _ffn_fwd(h3, W["ffn_norm_w"][1], W["ffn_up_w"][1], W["ffn_conv_w"][1], W["ffn_conv_b"][1],
                        W["ffn_down_w"][1], "1")
    loss, dh4, g_final = _loss_head(h4, tgt, W["final_norm_w"], name="loss_head")
    dh3, gf1 = _ffn_bwd(dh4, h3, ffn1, W["ffn_norm_w"][1], W["ffn_up_w"][1], W["ffn_conv_w"][1], W["ffn_conv_b"][1],
                        W["ffn_down_w"][1], "1")
    g_wo = _mm_tn(o, dh3, name="attn_o_wg", tn=1024)
    do = _mm_nt(dh3, W["w_o"], name="attn_o_dg", out_dtype=BF16, tn=1024, tk=1024)
    dq, dk, dv = _sba_bwd(q, kv, lt, do, name="sba_bwd")
    g_wq = _mm_tn(h2, dq, norm_w=W["attn_norm_w"], name="attn_q_wg", tn=1024)
    dh2a, g_attn_nw = _mm_nt(dq, W["w_q"], epi=(h2, W["attn_norm_w"], dh3), name="attn_q_dg", tk=1024)
    dkv = jnp.concatenate([dk, dv], axis=1)
    g_wkv = _mm_tn(h2, dkv, norm_w=W["kv_norm_w"], name="attn_kv_wg", tn=1024)
    dh2, g_kv_nw = _mm_nt(dkv, W["w_kv"], epi=(h2, W["kv_norm_w"], dh2a), name="attn_kv_dg", tk=1024)
    dh1, gf0 = _ffn_bwd(dh2, h1, ffn0, W["ffn_norm_w"][0], W["ffn_up_w"][0], W["ffn_conv_w"][0], W["ffn_conv_b"][0],
                        W["ffn_down_w"][0], "0")
    g_out = _mm_tn(yn, dh1, name="ssm_out_wg", tn=1024)
    dyn = _mm_nt(dh1, W["ssm_out_w"], name="ssm_out_dg", out_dtype=BF16, tn=1024, tk=1024)
    dxs, dB, dC, dz, ddt, g_gnw, dpar = _ssd_bwd(xbc_c, zx, dtg, par, gnw, y, st, dyn, name="ssd_bwd")
    dxbc_c = jnp.concatenate([dxs, dB, dC], axis=1)
    dhid, g_scw, g_scb = _ssm_conv_bwd_pre(zx, W["ssm_conv_w"], W["ssm_conv_b"].reshape(1, -1), dxbc_c,
                                           name="ssm_conv_bwd")
    dxbc = _conv_bwd_in(dhid, W["ssm_conv_w"], K=SSM_CONV, name="ssm_conv_bwd_in")
    ddt_t = ddt[:, :, :8].transpose(1, 0, 2).reshape(T, SSM_HEADS).astype(BF16)
    dzx = jnp.concatenate([dz, dxbc, jnp.pad(ddt_t, ((0, 0), (0, IN_PROJ_PAD - IN_PROJ_DIM)))], axis=1)
    g_in = _mm_tn(x, dzx, norm_w=W["ssm_norm_w"], name="ssm_in_wg", tn=896)
    dx, g_ssm_nw = _mm_nt(dzx, W["in_w"], epi=(x, W["ssm_norm_w"], dh1), name="ssm_in_dg", tk=1792)
    f["ssm_norm_w"] = g_ssm_nw.reshape(-1)
    f["ssm_in_w"] = g_in[:, :IN_PROJ_DIM]
    f["ssm_conv_w"] = g_scw
    f["ssm_conv_b"] = g_scb.reshape(-1)
    f["ssm_dt_bias"] = dpar[:, 0, :8].reshape(-1)
    f["ssm_a_log"] = dpar[:, 1, :8].reshape(-1)
    f["ssm_d"] = dpar[:, 2, :8].reshape(-1)
    f["ssm_gate_norm_w"] = g_gnw.reshape(-1)
    f["ssm_out_w"] = g_out
    f["kv_norm_w"] = g_kv_nw.reshape(-1)
    f["w_k"] = g_wkv[:, :D_MODEL]
    f["w_v"] = g_wkv[:, D_MODEL:]
    f["attn_norm_w"] = g_attn_nw.reshape(-1)
    f["w_q"] = g_wq
    f["w_o"] = g_wo
    f["ffn_norm_w"] = jnp.stack([gf0["norm"], gf1["norm"]])
    f["ffn_up_w"] = [gf0["up"], gf1["up"]]
    f["ffn_conv_w"] = jnp.stack([gf0["conv_w"], gf1["conv_w"]])
    f["ffn_conv_b"] = jnp.stack([gf0["conv_b"], gf1["conv_b"]])
    f["ffn_down_w"] = [gf0["down"], gf1["down"]]
    f["final_norm_w"] = g_final.reshape(-1)
    return loss, dx, f


_BIG = ["ssm_in_w", "ssm_out_w", "w_k", "w_v", "w_q", "w_o", "ffn_up_w", "ffn_down_w"]
_SMALL_SHARDED = ["ssm_norm_w", "ssm_conv_w", "ssm_conv_b", "ssm_gate_norm_w", "ffn_conv_w"]
_SMALL_REPL = ["ssm_dt_bias", "ssm_a_log", "ssm_d", "kv_norm_w", "attn_norm_w", "ffn_norm_w", "ffn_conv_b",
               "final_norm_w"]
_WEIGHTS = ["ssm_norm_w", "ssm_in_w", "ssm_conv_w", "ssm_conv_b", "ssm_dt_bias", "ssm_a_log", "ssm_d",
            "ssm_gate_norm_w", "ssm_out_w", "kv_norm_w", "w_k", "w_v", "attn_norm_w", "w_q", "w_o", "ffn_norm_w",
            "ffn_up_w", "ffn_conv_w", "ffn_conv_b", "ffn_down_w", "final_norm_w"]


def _as2d(a):
    return a.reshape(-1, a.shape[-1])


def _cols_to_full(g):
    return g.transpose(1, 0, 2).reshape(g.shape[1], N_DEV * g.shape[2])


def _full_to_cols(a):
    R = a.shape[0]
    return a.reshape(R, N_DEV, -1).transpose(1, 0, 2)


def _gather_weights(p):
    names = _BIG + _SMALL_SHARDED
    shards = [_as2d(p[n]).astype(BF16) for n in _BIG] + [_as2d(p[n]) for n in _SMALL_SHARDED]
    got = dict(zip(names, _all_gather(shards, name="gather_weights")))
    W = {n: p[n] for n in _SMALL_REPL}
    in_w = _cols_to_full(got["ssm_in_w"])
    W["in_w"] = jnp.pad(in_w, ((0, 0), (0, IN_PROJ_PAD - IN_PROJ_DIM)))
    W["ssm_out_w"] = got["ssm_out_w"].reshape(D_INNER, D_MODEL)
    W["w_kv"] = jnp.concatenate([got["w_k"].reshape(D_MODEL, D_MODEL), got["w_v"].reshape(D_MODEL, D_MODEL)], axis=1)
    W["w_q"] = got["w_q"].reshape(D_MODEL, D_MODEL)
    W["w_o"] = got["w_o"].reshape(D_MODEL, D_MODEL)
    up = got["ffn_up_w"]
    W["ffn_up_w"] = [_cols_to_full(up[:, l * D_MODEL:(l + 1) * D_MODEL]) for l in range(2)]
    dn = got["ffn_down_w"]
    rs = D_FF // N_DEV
    W["ffn_down_w"] = [dn[:, l * rs:(l + 1) * rs].reshape(D_FF, D_MODEL) for l in range(2)]
    W["ssm_norm_w"] = got["ssm_norm_w"].reshape(D_MODEL)
    W["ssm_conv_w"] = _cols_to_full(got["ssm_conv_w"])
    W["ssm_conv_b"] = got["ssm_conv_b"].reshape(CONV_DIM)
    W["ssm_gate_norm_w"] = got["ssm_gate_norm_w"].reshape(D_INNER)
    fcw = _cols_to_full(got["ffn_conv_w"])
    W["ffn_conv_w"] = fcw.reshape(2, FFN_CONV, 2 * D_FF)
    for n in ("ssm_dt_bias", "ssm_a_log", "ssm_d", "attn_norm_w"):
        W[n] = W[n].reshape(-1)
    return W


def _big_grad_blocks(f):
    rs = D_FF // N_DEV
    return {
        "ssm_in_w": _full_to_cols(f["ssm_in_w"]),
        "ssm_out_w": f["ssm_out_w"].reshape(N_DEV, D_INNER // N_DEV, D_MODEL),
        "w_k": f["w_k"].reshape(N_DEV, D_MODEL // N_DEV, D_MODEL),
        "w_v": f["w_v"].reshape(N_DEV, D_MODEL // N_DEV, D_MODEL),
        "w_q": f["w_q"].reshape(N_DEV, D_MODEL // N_DEV, D_MODEL),
        "w_o": f["w_o"].reshape(N_DEV, D_MODEL // N_DEV, D_MODEL),
        "ffn_up_w": jnp.concatenate([_full_to_cols(g) for g in f["ffn_up_w"]], axis=1),
        "ffn_down_w": jnp.concatenate([g.reshape(N_DEV, rs, D_MODEL) for g in f["ffn_down_w"]], axis=1),
    }


def _pack_small(vals):
    flat = jnp.concatenate([v.reshape(-1).astype(F32) for v in vals])
    n = flat.shape[0]
    rows = -(-n // 1024) * 8
    return jnp.pad(flat, (0, rows * 128 - n)).reshape(rows, 128)


def _unpack_small(packed, shapes):
    flat = packed.reshape(-1)
    out, off = [], 0
    for s in shapes:
        n = math.prod(s)
        out.append(flat[off:off + n].reshape(s))
        off += n
    return out


def _kernel_v1(x, ssm_norm_w, ssm_in_w, ssm_conv_w, ssm_conv_b, ssm_dt_bias, ssm_a_log, ssm_d, ssm_gate_norm_w, ssm_out_w, kv_norm_w, w_k, w_v, attn_norm_w, w_q, w_o, ffn_norm_w, ffn_up_w, ffn_conv_w, ffn_conv_b, ffn_down_w, final_norm_w, loss_target, m_ssm_norm_w, m_ssm_in_w, m_ssm_conv_w, m_ssm_conv_b, m_ssm_dt_bias, m_ssm_a_log, m_ssm_d, m_ssm_gate_norm_w, m_ssm_out_w, m_kv_norm_w, m_w_k, m_w_v, m_attn_norm_w, m_w_q, m_w_o, m_ffn_norm_w, m_ffn_up_w, m_ffn_conv_w, m_ffn_conv_b, m_ffn_down_w, m_final_norm_w, v_ssm_norm_w, v_ssm_in_w, v_ssm_conv_w, v_ssm_conv_b, v_ssm_dt_bias, v_ssm_a_log, v_ssm_d, v_ssm_gate_norm_w, v_ssm_out_w, v_kv_norm_w, v_w_k, v_w_v, v_attn_norm_w, v_w_q, v_w_o, v_ffn_norm_w, v_ffn_up_w, v_ffn_conv_w, v_ffn_conv_b, v_ffn_down_w, v_final_norm_w):
    env = dict(locals())
    p = {n: env[n] for n in _WEIGHTS}
    mom = {n: env["m_" + n] for n in _WEIGHTS}
    var = {n: env["v_" + n] for n in _WEIGHTS}
    T = x.shape[1]
    me = 4 * lax.axis_index("x") + 2 * lax.axis_index("y") + lax.axis_index("c")

    W = _gather_weights(p)
    loss_row, dx, f = _local_step(x.reshape(T, D_MODEL), loss_target.reshape(T, D_MODEL), W)
    loss = lax.psum(loss_row[0, 0], ("x", "y", "c"))

    big = _big_grad_blocks(f)
    small_names = _SMALL_REPL + _SMALL_SHARDED
    small_full = _pack_small([f[n] for n in small_names])
    small_bcast = jnp.broadcast_to(small_full[None], (N_DEV,) + small_full.shape)
    got = _exchange([big[n] for n in _BIG] + [small_bcast], name="exchange_grads")
    big_parts = dict(zip(_BIG, got[:-1]))

    zero = jnp.zeros_like(small_full)
    g_small_sum = _adamw(got[-1], zero, zero, zero, name="sum_small_grads", tr=small_full.shape[0])[0]
    full_shapes = [f[n].shape for n in small_names]
    g_small = dict(zip(small_names, _unpack_small(g_small_sum, full_shapes)))
    for n in _SMALL_SHARDED:
        width = p[n].shape[-1]
        g_small[n] = lax.dynamic_slice_in_dim(g_small[n], me * width, width, axis=g_small[n].ndim - 1)

    out_g, out_d, out_m, out_v = {}, {}, {}, {}
    for n in _BIG:
        w2, m2, v2 = _as2d(p[n]), _as2d(mom[n]), _as2d(var[n])
        tr = 352 if n == "ffn_down_w" else 256
        g, d, nm, nv = _adamw(big_parts[n], w2, m2, v2, name="adamw_" + n, tr=tr)
        out_g[n], out_d[n], out_m[n], out_v[n] = (t.reshape(p[n].shape) for t in (g, d, nm, nv))
    sw = _pack_small([p[n] for n in small_names])
    sm = _pack_small([mom[n] for n in small_names])
    sv = _pack_small([var[n] for n in small_names])
    sg = _pack_small([g_small[n] for n in small_names])
    _, d, nm, nv = _adamw(sg[None], sw, sm, sv, name="adamw_small", tr=sw.shape[0])
    shard_shapes = [p[n].shape for n in small_names]
    for n, dd, mm, vv in zip(small_names, _unpack_small(d, shard_shapes), _unpack_small(nm, shard_shapes),
                             _unpack_small(nv, shard_shapes)):
        out_g[n] = g_small[n].reshape(p[n].shape)
        out_d[n], out_m[n], out_v[n] = dd, mm, vv

    return (loss, dx.reshape(x.shape), *[out_g[n] for n in _WEIGHTS], *[out_d[n] for n in _WEIGHTS],
            *[out_m[n] for n in _WEIGHTS], *[out_v[n] for n in _WEIGHTS])


def _tie(a, token):
    return a + token[0, 0].astype(a.dtype)


def _local_step2(x, tgt, get_w, put_g):
    T = x.shape[0]
    Ws = get_w("ssm", None)
    fnw, fcw, fcb = Ws["ffn_norm_w"], Ws["ffn_conv_w"], Ws["ffn_conv_b"]
    zx = _mm_fwd(x, Ws["in_w"], norm_w=Ws["ssm_norm_w"], name="ssm_in", tm=1024, tn=1792)
    xbc_c = _ssm_conv_fwd(zx, Ws["ssm_conv_w"], Ws["ssm_conv_b"].reshape(1, -1), name="ssm_conv")
    dt_raw = zx[:, D_INNER + CONV_DIM:IN_PROJ_DIM]
    dtg = jnp.pad(dt_raw.reshape(T, SSM_GROUPS, 8).transpose(1, 0, 2), ((0, 0), (0, 0), (0, 120)))
    par = jnp.stack([Ws["ssm_dt_bias"].reshape(SSM_GROUPS, 8), Ws["ssm_a_log"].reshape(SSM_GROUPS, 8),
                     Ws["ssm_d"].reshape(SSM_GROUPS, 8)], axis=1)
    par = jnp.pad(par, ((0, 0), (0, 5), (0, 120)))
    gnw = _tie(Ws["ssm_gate_norm_w"].reshape(1, D_INNER), get_w("rest_start", xbc_c))
    y, yn, st = _ssd_fwd(xbc_c, zx, dtg, par, gnw, name="ssd_fwd")
    W0 = get_w("ffn0", y)
    Ws["ssm_out_w"] = W0["ssm_out_w"]
    h1 = _mm_fwd(yn, Ws["ssm_out_w"], residual=x, name="ssm_out", tm=1024, tn=512)
    h2, ffn0 = _ffn_fwd(h1, fnw[0], W0["up"], fcw[0], fcb[0], W0["down"], "0")
    Wr = get_w("rest", h2)
    q = _mm_fwd(h2, Wr["w_q"], norm_w=Ws["attn_norm_w"], out_dtype=BF16, name="attn_q", tm=1024, tn=1024)
    kv = _mm_fwd(h2, Wr["w_kv"], norm_w=Ws["kv_norm_w"], out_dtype=BF16, name="attn_kv", tm=1024, tn=1024)
    o, lt = _sba_fwd(q, kv, name="sba_fwd")
    h3 = _mm_fwd(o, Wr["w_o"], residual=h2, name="attn_o", tm=1024, tn=512)
    h4, ffn1 = _ffn_fwd(h3, fnw[1], Wr["up"], fcw[1], fcb[1], Wr["down"], "1")
    loss, dh4, g_final = _loss_head(h4, tgt, Ws["final_norm_w"], name="loss_head")
    dh3, gf1 = _ffn_bwd(dh4, h3, ffn1, fnw[1], Wr["up"], fcw[1], fcb[1], Wr["down"], "1")
    tok = put_g("ffn1", dict(up=gf1["up"], down=gf1["down"]))
    g_wo = _mm_tn(o, dh3, name="attn_o_wg", tn=1024)
    do = _mm_nt(dh3, _tie(Wr["w_o"], tok), name="attn_o_dg", out_dtype=BF16, tn=1024, tk=1024)
    dq, dk, dv = _sba_bwd(q, kv, lt, do, name="sba_bwd")
    g_wq = _mm_tn(h2, dq, norm_w=Ws["attn_norm_w"], name="attn_q_wg", tn=1024, tt=1024)
    dh2a, g_attn_nw = _mm_nt(dq, Wr["w_q"], epi=(h2, Ws["attn_norm_w"], dh3), name="attn_q_dg", tm=1024, tk=1024)
    dkv = jnp.concatenate([dk, dv], axis=1)
    g_wkv = _mm_tn(h2, dkv, norm_w=Ws["kv_norm_w"], name="attn_kv_wg", tn=1024, tt=1024)
    dh2, g_kv_nw = _mm_nt(dkv, Wr["w_kv"], epi=(h2, Ws["kv_norm_w"], dh2a), name="attn_kv_dg", tm=1024, tk=1024)
    tok = put_g("attn", dict(w_o=g_wo, w_q=g_wq, w_k=g_wkv[:, :D_MODEL], w_v=g_wkv[:, D_MODEL:]))
    dh1, gf0 = _ffn_bwd(dh2, h1, ffn0, fnw[0], W0["up"], fcw[0], _tie(fcb[0], tok), W0["down"], "0")
    tok = put_g("ffn0", dict(up=gf0["up"], down=gf0["down"]))
    g_out = _mm_tn(yn, dh1, name="ssm_out_wg", tn=1024)
    dyn = _mm_nt(dh1, _tie(Ws["ssm_out_w"], tok), name="ssm_out_dg", out_dtype=BF16, tn=1024, tk=1024)
    tok = put_g("ssm_out", dict(ssm_out_w=g_out))
    dxbc_c, dz, ddt, g_gnw, dpar = _ssd_bwd(xbc_c, zx, dtg, par, _tie(gnw, tok), y, st, dyn, name="ssd_bwd")
    dhid, g_scw, g_scb = _ssm_conv_bwd_pre(zx, Ws["ssm_conv_w"], Ws["ssm_conv_b"].reshape(1, -1), dxbc_c,
                                           name="ssm_conv_bwd")
    dzx = _conv_bwd_in(dhid, Ws["ssm_conv_w"], K=SSM_CONV, name="ssm_conv_bwd_in", into=(dz, D_INNER))
    ddt_t = ddt[:, :, :8].transpose(1, 0, 2).reshape(T, SSM_HEADS).astype(BF16)
    dzx = _put_cols(dzx, jnp.pad(ddt_t, ((0, 0), (0, IN_PROJ_PAD - IN_PROJ_DIM))), D_INNER + CONV_DIM, name="ssm_ddt_cols")
    g_in = _mm_tn(x, dzx, norm_w=Ws["ssm_norm_w"], name="ssm_in_wg", tn=1792, tt=1024)
    tok = put_g("ssm_in", dict(ssm_in_w=g_in[:, :IN_PROJ_DIM]))
    dx, g_ssm_nw = _mm_nt(dzx, Ws["in_w"], epi=(x, _tie(Ws["ssm_norm_w"], tok), dh1), name="ssm_in_dg", tm=1024, tk=1792)
    f = {
        "ssm_norm_w": g_ssm_nw.reshape(-1), "ssm_conv_w": g_scw,
        "ssm_conv_b": g_scb.reshape(-1), "ssm_dt_bias": dpar[:, 0, :8].reshape(-1),
        "ssm_a_log": dpar[:, 1, :8].reshape(-1), "ssm_d": dpar[:, 2, :8].reshape(-1),
        "ssm_gate_norm_w": g_gnw.reshape(-1), "kv_norm_w": g_kv_nw.reshape(-1), "attn_norm_w": g_attn_nw.reshape(-1),
        "ffn_norm_w": jnp.stack([gf0["norm"], gf1["norm"]]), "ffn_conv_w": jnp.stack([gf0["conv_w"], gf1["conv_w"]]),
        "ffn_conv_b": jnp.stack([gf0["conv_b"], gf1["conv_b"]]), "final_norm_w": g_final.reshape(-1),
    }
    return loss, dx, f


def kernel(x, ssm_norm_w, ssm_in_w, ssm_conv_w, ssm_conv_b, ssm_dt_bias, ssm_a_log, ssm_d, ssm_gate_norm_w, ssm_out_w, kv_norm_w, w_k, w_v, attn_norm_w, w_q, w_o, ffn_norm_w, ffn_up_w, ffn_conv_w, ffn_conv_b, ffn_down_w, final_norm_w, loss_target, m_ssm_norm_w, m_ssm_in_w, m_ssm_conv_w, m_ssm_conv_b, m_ssm_dt_bias, m_ssm_a_log, m_ssm_d, m_ssm_gate_norm_w, m_ssm_out_w, m_kv_norm_w, m_w_k, m_w_v, m_attn_norm_w, m_w_q, m_w_o, m_ffn_norm_w, m_ffn_up_w, m_ffn_conv_w, m_ffn_conv_b, m_ffn_down_w, m_final_norm_w, v_ssm_norm_w, v_ssm_in_w, v_ssm_conv_w, v_ssm_conv_b, v_ssm_dt_bias, v_ssm_a_log, v_ssm_d, v_ssm_gate_norm_w, v_ssm_out_w, v_kv_norm_w, v_w_k, v_w_v, v_attn_norm_w, v_w_q, v_w_o, v_ffn_norm_w, v_ffn_up_w, v_ffn_conv_w, v_ffn_conv_b, v_ffn_down_w, v_final_norm_w):
    env = dict(locals())
    p = {n: env[n] for n in _WEIGHTS}
    mom = {n: env["m_" + n] for n in _WEIGHTS}
    var = {n: env["v_" + n] for n in _WEIGHTS}
    T = x.shape[1]
    me = 4 * lax.axis_index("x") + 2 * lax.axis_index("y") + lax.axis_index("c")
    rs = D_FF // N_DEV

    def bf2(a):
        return _as2d(a).astype(BF16)

    def with_own(srcs, lands, scatter):
        out = []
        for s, l in zip(srcs, lands):
            own = lax.dynamic_index_in_dim(s, me, 0, keepdims=False) if scatter else s
            out.append(lax.dynamic_update_index_in_dim(l, own, me, 0))
        return out

    a_names = ["ssm_in_w"] + _SMALL_SHARDED
    got_a = dict(zip(a_names, _all_gather([bf2(p["ssm_in_w"])] + [_as2d(p[n]) for n in _SMALL_SHARDED],
                                          name="gather_ssm")))
    ffn0_names = ["ssm_out_w", "up0", "down0"]
    rest_names = ["w_q", "w_k", "w_v", "w_o", "up1", "down1"]
    shard = {"up0": bf2(p["ffn_up_w"][0]), "down0": bf2(p["ffn_down_w"][0]), "up1": bf2(p["ffn_up_w"][1]),
             "down1": bf2(p["ffn_down_w"][1]), "w_q": bf2(p["w_q"]), "w_k": bf2(p["w_k"]), "w_v": bf2(p["w_v"]),
             "w_o": bf2(p["w_o"]), "ssm_out_w": bf2(p["ssm_out_w"])}
    h_ffn0 = _push_start([shard[n] for n in ffn0_names], scatter=False, name="gather_ffn0_start")
    handles = {}

    def get_w(group, after):
        if group == "ssm":
            W = {n: p[n] for n in _SMALL_REPL}
            for n in ("ssm_dt_bias", "ssm_a_log", "ssm_d", "attn_norm_w"):
                W[n] = W[n].reshape(-1)
            W["in_w"] = jnp.pad(_cols_to_full(got_a["ssm_in_w"]), ((0, 0), (0, IN_PROJ_PAD - IN_PROJ_DIM)))
            W["ssm_norm_w"] = _tie(got_a["ssm_norm_w"].reshape(D_MODEL), h_ffn0["token"])
            W["ssm_conv_w"] = _cols_to_full(got_a["ssm_conv_w"])
            W["ssm_conv_b"] = got_a["ssm_conv_b"].reshape(CONV_DIM)
            W["ssm_gate_norm_w"] = got_a["ssm_gate_norm_w"].reshape(D_INNER)
            W["ffn_conv_w"] = _cols_to_full(got_a["ffn_conv_w"]).reshape(2, FFN_CONV, 2 * D_FF)
            return W
        if group == "rest_start":
            anchor = after[0, 0]
            first = shard[rest_names[0]] + (jnp.where(jnp.isfinite(anchor), anchor, 0.0) * 0.0).astype(BF16)
            handles["rest"] = _push_start([first] + [shard[n] for n in rest_names[1:]], scatter=False,
                                          name="gather_rest_start")
            return handles["rest"]["token"]
        if group == "ffn0":
            srcs, lands = _push_wait(h_ffn0, after, name="gather_ffn0_wait")
            out, up, down = with_own(srcs, lands, False)
            return dict(ssm_out_w=out.reshape(D_INNER, D_MODEL), up=_cols_to_full(up), down=down.reshape(D_FF, D_MODEL))
        srcs, lands = _push_wait(handles["rest"], after, name="gather_rest_wait")
        g = dict(zip(rest_names, with_own(srcs, lands, False)))
        sq = lambda a: a.reshape(D_MODEL, D_MODEL)
        return dict(w_q=sq(g["w_q"]), w_kv=jnp.concatenate([sq(g["w_k"]), sq(g["w_v"])], axis=1), w_o=sq(g["w_o"]),
                    up=_cols_to_full(g["up1"]), down=g["down1"].reshape(D_FF, D_MODEL))

    pending = []

    def put_g(group, g):
        if group in ("ffn0", "ffn1"):
            keys = [("ffn_up_w", int(group[-1])), ("ffn_down_w", int(group[-1]))]
            blocks = [_full_to_cols(g["up"]), g["down"].reshape(N_DEV, rs, D_MODEL)]
        elif group == "attn":
            keys = [(n, None) for n in ("w_o", "w_q", "w_k", "w_v")]
            blocks = [g[n].reshape(N_DEV, D_MODEL // N_DEV, D_MODEL) for n, _ in keys]
        elif group == "ssm_out":
            keys = [("ssm_out_w", None)]
            blocks = [g["ssm_out_w"].reshape(N_DEV, D_INNER // N_DEV, D_MODEL)]
        else:
            keys = [("ssm_in_w", None)]
            blocks = [_full_to_cols(g["ssm_in_w"])]
        h = _push_start(blocks, scatter=True, name=f"exchange_{group}_start")
        pending.append((group, keys, h))
        return h["token"]

    loss_row, dx, f = _local_step2(x.reshape(T, D_MODEL), loss_target.reshape(T, D_MODEL), get_w, put_g)

    small_names = _SMALL_REPL + _SMALL_SHARDED
    small_full = _pack_small([f[n] for n in small_names] + [loss_row[0, 0:1]])
    small_bcast = jnp.broadcast_to(small_full[None], (N_DEV,) + small_full.shape)
    h_small = _push_start([small_bcast], scatter=True, name="exchange_small_start")
    tok = h_small["token"]

    arrived, res = {}, {}
    after = dx
    for group, keys, h in pending:
        srcs, lands = _push_wait(h, after, name=f"exchange_{group}_wait")
        arrived.update(zip(keys, with_own(srcs, lands, True)))
        for n in _BIG:
            layered = (n, 0) in arrived or (n, 1) in arrived
            if n in res or not ((n, None) in arrived or ((n, 0) in arrived and (n, 1) in arrived)):
                continue
            parts = [arrived[(n, 0)], arrived[(n, 1)]] if layered else arrived[(n, None)]
            w2, m2, v2 = _as2d(p[n]), _as2d(mom[n]), _as2d(var[n])
            if not res:
                w2 = _tie(w2, tok)
            res[n] = _adamw(parts, w2, m2, v2, name=f"adamw_{n}", tr=rs if n == "ffn_down_w" else 256)
            after = res[n][0]
    srcs, lands = _push_wait(h_small, after, name="exchange_small_wait")
    small_parts = with_own(srcs, lands, True)[0]
    out_g, out_d, out_m, out_v = {}, {}, {}, {}
    for n in _BIG:
        out_g[n], out_d[n], out_m[n], out_v[n] = (t.reshape(p[n].shape) for t in res[n])

    zero = jnp.zeros_like(small_full)
    g_small_sum = _adamw(small_parts, zero, zero, zero, name="sum_small_grads", tr=small_full.shape[0])[0]
    *small_sums, loss_sum = _unpack_small(g_small_sum, [f[n].shape for n in small_names] + [(1,)])
    loss = loss_sum[0]
    g_small = dict(zip(small_names, small_sums))
    for n in _SMALL_SHARDED:
        width = p[n].shape[-1]
        g_small[n] = lax.dynamic_slice_in_dim(g_small[n], me * width, width, axis=g_small[n].ndim - 1)
    sw = _pack_small([p[n] for n in small_names])
    sm = _pack_small([mom[n] for n in small_names])
    sv = _pack_small([var[n] for n in small_names])
    sg = _pack_small([g_small[n] for n in small_names])
    _, d, nm, nv = _adamw(sg[None], sw, sm, sv, name="adamw_small", tr=sw.shape[0])
    shard_shapes = [p[n].shape for n in small_names]
    for n, dd, mm, vv in zip(small_names, _unpack_small(d, shard_shapes), _unpack_small(nm, shard_shapes),
                             _unpack_small(nv, shard_shapes)):
        out_g[n] = g_small[n].reshape(p[n].shape)
        out_d[n], out_m[n], out_v[n] = dd, mm, vv

    return (loss, dx.reshape(x.shape), *[out_g[n] for n in _WEIGHTS], *[out_d[n] for n in _WEIGHTS],
            *[out_m[n] for n in _WEIGHTS], *[out_v[n] for n in _WEIGHTS])
```

```python
import functools
import math

import jax
import jax.numpy as jnp
from jax import lax
from jax.experimental import pallas as pl
from jax.experimental.pallas import tpu as pltpu

F32 = jnp.float32
BF16 = jnp.bfloat16
EPS = 1e-6

D_MODEL = 1024
D_INNER = 2048
SSM_HEADS = 32
SSM_GROUPS = 4
SSM_STATE = 128
SSM_CONV = 4
SSM_CHUNK = 128
GN = SSM_GROUPS * SSM_STATE
CONV_DIM = D_INNER + 2 * GN
IN_PROJ_DIM = D_INNER + CONV_DIM + SSM_HEADS
IN_PROJ_PAD = 5376
SB_HEADS = 16
SB_HEAD_DIM = 64
SB_BLOCK = 128
D_FF = 2816
FFN_CONV = 3
N_DEV = 8

ADAM_LR = 0.001
ADAM_B1 = 0.9
ADAM_B2 = 0.999
ADAM_EPS = 1e-08
ADAM_WD = 0.01
ADAM_STEP = 10

_MESH = pl.DeviceIdType.MESH
_NT = (((1,), (1,)), ((), ()))
_TN = (((0,), (0,)), ((), ()))
_ANY = pl.BlockSpec(memory_space=pl.ANY)


def _cparams(sem, vmem_mb=48):
    return pltpu.CompilerParams(dimension_semantics=sem, vmem_limit_bytes=vmem_mb * 1024 * 1024)


def _sigmoid(x):
    return 0.5 * jnp.tanh(0.5 * x) + 0.5


def _softplus(x):
    return jnp.maximum(x, 0.0) + jnp.log(1.0 + jnp.exp(-jnp.abs(x)))


def _rms_fwd(xv, w):
    r = lax.rsqrt(jnp.mean(xv * xv, axis=-1, keepdims=True) + EPS)
    return xv * r * w


def _mm_fwd(x, w, *, name, norm_w=None, residual=None, out_dtype=F32, tm=512, tn=512, halves=False):
    M, K = x.shape
    N = w.shape[1]
    tm, tn = min(tm, M), min(tn, N)
    assert M % tm == 0 and N % tn == 0, (name, M, N, tm, tn)
    if halves:
        nbh = N // 2 // tn
        assert N // 2 % tn == 0
        out_spec = pl.BlockSpec((None, tm, tn), lambda i, j: (lax.div(j, nbh), i, lax.rem(j, nbh)))
        out_shape = jax.ShapeDtypeStruct((2, M, N // 2), out_dtype)
    else:
        out_spec = pl.BlockSpec((tm, tn), lambda i, j: (i, j))
        out_shape = jax.ShapeDtypeStruct((M, N), out_dtype)
    has_norm, has_res = norm_w is not None, residual is not None

    def body(*refs):
        x_ref, w_ref = refs[0], refs[1]
        p = 2
        nw_ref = r_ref = None
        if has_norm:
            nw_ref = refs[p]
            p += 1
        if has_res:
            r_ref = refs[p]
            p += 1
        o_ref = refs[p]
        xv = x_ref[...]
        if has_norm:
            xv = _rms_fwd(xv.astype(F32), nw_ref[...])
        acc = jnp.dot(xv.astype(BF16), w_ref[...], preferred_element_type=F32)
        if has_res:
            acc = acc + r_ref[...]
        o_ref[...] = acc.astype(out_dtype)

    in_specs = [pl.BlockSpec((tm, K), lambda i, j: (i, 0)), pl.BlockSpec((K, tn), lambda i, j: (0, j))]
    args = [x, w]
    if has_norm:
        in_specs.append(pl.BlockSpec((1, K), lambda i, j: (0, 0)))
        args.append(norm_w.reshape(1, K))
    if has_res:
        in_specs.append(pl.BlockSpec((tm, tn), lambda i, j: (i, j)))
        args.append(residual)
    return pl.pallas_call(
        body, name=name, grid=(M // tm, N // tn), in_specs=in_specs,
        out_specs=out_spec, out_shape=out_shape,
        compiler_params=_cparams(("parallel", "parallel")))(*args)


def _mm_nt(dy, w, *, name, epi=None, out_dtype=F32, tm=512, tn=512, tk=512):
    halves = dy.ndim == 3
    M, K = (dy.shape[1], 2 * dy.shape[2]) if halves else dy.shape
    N = w.shape[0]
    tm, tk = min(tm, M), min(tk, K)
    tn = N if epi is not None else min(tn, N)
    assert M % tm == 0 and N % tn == 0 and K % tk == 0, (name, M, N, K, tm, tn, tk)
    nk = K // tk
    has_epi = epi is not None

    def body(*refs):
        if has_epi:
            dy_ref, w_ref, h_ref, nw_ref, r_ref, o_ref, dnw_ref, acc_ref = refs
        else:
            dy_ref, w_ref, o_ref, acc_ref = refs
        i = pl.program_id(0)
        k = pl.program_id(2)

        @pl.when(k == 0)
        def _():
            acc_ref[...] = jnp.zeros_like(acc_ref)

        acc_ref[...] += lax.dot_general(dy_ref[...].astype(BF16), w_ref[...], _NT, preferred_element_type=F32)

        @pl.when(k == nk - 1)
        def _():
            du = acc_ref[...]
            if has_epi:
                hv = h_ref[...]
                r = lax.rsqrt(jnp.mean(hv * hv, axis=-1, keepdims=True) + EPS)
                xhat = hv * r
                dxh = du * nw_ref[...]
                dx = r * (dxh - xhat * jnp.mean(dxh * xhat, axis=-1, keepdims=True))
                o_ref[...] = (r_ref[...] + dx).astype(out_dtype)
                contrib = jnp.sum(du * xhat, axis=0, keepdims=True)

                @pl.when(i == 0)
                def _():
                    dnw_ref[...] = contrib

                @pl.when(i > 0)
                def _():
                    dnw_ref[...] += contrib
            else:
                o_ref[...] = du.astype(out_dtype)

    if halves:
        nkh = K // 2 // tk
        assert K // 2 % tk == 0
        dy_spec = pl.BlockSpec((None, tm, tk), lambda i, j, k: (lax.div(k, nkh), i, lax.rem(k, nkh)))
    else:
        dy_spec = pl.BlockSpec((tm, tk), lambda i, j, k: (i, k))
    in_specs = [dy_spec, pl.BlockSpec((tn, tk), lambda i, j, k: (j, k))]
    args = [dy, w]
    out_specs = [pl.BlockSpec((tm, tn), lambda i, j, k: (i, j))]
    out_shape = [jax.ShapeDtypeStruct((M, N), out_dtype)]
    if has_epi:
        h, nw, res = epi
        in_specs += [pl.BlockSpec((tm, N), lambda i, j, k: (i, 0)), pl.BlockSpec((1, N), lambda i, j, k: (0, 0)),
                     pl.BlockSpec((tm, N), lambda i, j, k: (i, 0))]
        args += [h, nw.reshape(1, N), res]
        out_specs.append(pl.BlockSpec((1, N), lambda i, j, k: (0, 0)))
        out_shape.append(jax.ShapeDtypeStruct((1, N), F32))
    outs = pl.pallas_call(
        body, name=name, grid=(M // tm, N // tn, nk), in_specs=in_specs, out_specs=out_specs, out_shape=out_shape,
        scratch_shapes=[pltpu.VMEM((tm, tn), F32)],
        compiler_params=_cparams(("arbitrary", "arbitrary", "arbitrary")))(*args)
    return (outs[0], outs[1]) if has_epi else outs[0]


def _mm_tn(x, dy, *, name, norm_w=None, out_dtype=BF16, tk1=1024, tn=512, tt=512):
    T, K1 = x.shape
    halves = dy.ndim == 3
    N = 2 * dy.shape[2] if halves else dy.shape[1]
    tk1, tn, tt = min(tk1, K1), min(tn, N), min(tt, T)
    has_norm = norm_w is not None
    assert K1 % tk1 == 0 and N % tn == 0 and T % tt == 0, (name, K1, N, T, tk1, tn, tt)
    assert not has_norm or tk1 == K1
    nt = T // tt

    def body(*refs):
        if has_norm:
            x_ref, dy_ref, nw_ref, o_ref, acc_ref = refs
        else:
            x_ref, dy_ref, o_ref, acc_ref = refs
        t = pl.program_id(2)

        @pl.when(t == 0)
        def _():
            acc_ref[...] = jnp.zeros_like(acc_ref)

        xv = x_ref[...]
        if has_norm:
            xv = _rms_fwd(xv.astype(F32), nw_ref[...])
        acc_ref[...] += lax.dot_general(xv.astype(BF16), dy_ref[...].astype(BF16), _TN, preferred_element_type=F32)

        @pl.when(t == nt - 1)
        def _():
            o_ref[...] = acc_ref[...].astype(out_dtype)

    if halves:
        nbh = N // 2 // tn
        assert N // 2 % tn == 0
        dy_spec = pl.BlockSpec((None, tt, tn), lambda a, b, t: (lax.div(b, nbh), t, lax.rem(b, nbh)))
    else:
        dy_spec = pl.BlockSpec((tt, tn), lambda a, b, t: (t, b))
    in_specs = [pl.BlockSpec((tt, tk1), lambda a, b, t: (t, a)), dy_spec]
    args = [x, dy]
    if has_norm:
        in_specs.append(pl.BlockSpec((1, K1), lambda a, b, t: (0, 0)))
        args.append(norm_w.reshape(1, K1))
    return pl.pallas_call(
        body, name=name, grid=(K1 // tk1, N // tn, nt), in_specs=in_specs,
        out_specs=pl.BlockSpec((tk1, tn), lambda a, b, t: (a, b)),
        out_shape=jax.ShapeDtypeStruct((K1, N), out_dtype),
        scratch_shapes=[pltpu.VMEM((tk1, tn), F32)],
        compiler_params=_cparams(("parallel", "parallel", "arbitrary")))(*args)


def _shift_down(xb, prev8, j):
    main = pltpu.roll(xb, j, 0)
    head = pltpu.roll(xb[0:8], j, 0)
    ph = pltpu.roll(prev8, j, 0)
    row8 = lax.broadcasted_iota(jnp.int32, head.shape, 0)
    head = jnp.where(row8 < j, ph, head)
    return jnp.concatenate([head, main[8:]], axis=0)


def _shift_up(xb, next8, j):
    tt = xb.shape[0]
    main = pltpu.roll(xb, tt - j, 0)
    tail = pltpu.roll(xb[tt - 8:tt], 8 - j, 0)
    nh = pltpu.roll(next8, 8 - j, 0)
    row8 = lax.broadcasted_iota(jnp.int32, tail.shape, 0)
    tail = jnp.where(row8 + j >= 8, nh, tail)
    return jnp.concatenate([main[:tt - 8], tail], axis=0)


def _conv_hid(xb, prev8, w, b_row, K):
    out = b_row
    shifted = []
    for j in range(K):
        sh = K - 1 - j
        xs = xb if sh == 0 else _shift_down(xb, prev8, sh)
        shifted.append(xs)
        out = out + xs * w[j:j + 1, :]
    return out, shifted


def _prev_idx(i, nb8):
    return jnp.maximum(i * nb8 - 1, 0)


def _ssm_conv_fwd(zx, w, b, *, name, tt=512, tc=512):
    T = zx.shape[0]
    tt = min(tt, T)
    C, K = CONV_DIM, SSM_CONV
    cb0, nb8 = D_INNER // tc, tt // 8

    def body(x_ref, p_ref, w_ref, b_ref, o_ref):
        first = (pl.program_id(1) > 0).astype(F32)
        hid, _ = _conv_hid(x_ref[...], p_ref[...] * first, w_ref[...], b_ref[...], K)
        o_ref[...] = hid * _sigmoid(hid)

    return pl.pallas_call(
        body, name=name, grid=(C // tc, T // tt),
        in_specs=[pl.BlockSpec((tt, tc), lambda c, i: (i, c + cb0)),
                  pl.BlockSpec((8, tc), lambda c, i: (_prev_idx(i, nb8), c + cb0)),
                  pl.BlockSpec((K, tc), lambda c, i: (0, c)), pl.BlockSpec((1, tc), lambda c, i: (0, c))],
        out_specs=pl.BlockSpec((tt, tc), lambda c, i: (i, c)),
        out_shape=jax.ShapeDtypeStruct((T, C), F32),
        compiler_params=_cparams(("parallel", "parallel")))(zx, zx, w, b)


def _ssm_conv_bwd_pre(zx, w, b, dout, *, name, tt=512, tc=512):
    T = zx.shape[0]
    tt = min(tt, T)
    C, K = CONV_DIM, SSM_CONV
    cb0, nb8 = D_INNER // tc, tt // 8

    def body(x_ref, p_ref, w_ref, b_ref, d_ref, dh_ref, dw_ref, db_ref):
        t = pl.program_id(1)
        first = (t > 0).astype(F32)
        hid, shifted = _conv_hid(x_ref[...], p_ref[...] * first, w_ref[...], b_ref[...], K)
        sg = _sigmoid(hid)
        dh = d_ref[...] * (sg * (1.0 + hid * (1.0 - sg)))
        dh_ref[...] = dh

        @pl.when(t == 0)
        def _():
            dw_ref[...] = jnp.zeros_like(dw_ref)
            db_ref[...] = jnp.zeros_like(db_ref)

        db_ref[...] += jnp.sum(dh, axis=0, keepdims=True)
        for j in range(K):
            dw_ref[j:j + 1, :] += jnp.sum(dh * shifted[j], axis=0, keepdims=True)

    return pl.pallas_call(
        body, name=name, grid=(C // tc, T // tt),
        in_specs=[pl.BlockSpec((tt, tc), lambda c, i: (i, c + cb0)),
                  pl.BlockSpec((8, tc), lambda c, i: (_prev_idx(i, nb8), c + cb0)),
                  pl.BlockSpec((K, tc), lambda c, i: (0, c)), pl.BlockSpec((1, tc), lambda c, i: (0, c)),
                  pl.BlockSpec((tt, tc), lambda c, i: (i, c))],
        out_specs=[pl.BlockSpec((tt, tc), lambda c, i: (i, c)), pl.BlockSpec((K, tc), lambda c, i: (0, c)),
                   pl.BlockSpec((1, tc), lambda c, i: (0, c))],
        out_shape=[jax.ShapeDtypeStruct((T, C), F32), jax.ShapeDtypeStruct((K, C), F32),
                   jax.ShapeDtypeStruct((1, C), F32)],
        compiler_params=_cparams(("parallel", "arbitrary")))(zx, zx, w, b, dout)


def _put_cols(buf, src, col0, *, name, tt=512):
    T, C = src.shape
    tt = min(tt, T)

    def body(s_ref, _, o_ref):
        o_ref[...] = s_ref[...]

    return pl.pallas_call(
        body, name=name, grid=(T // tt,),
        in_specs=[pl.BlockSpec((tt, C), lambda i: (i, 0)), _ANY],
        out_specs=pl.BlockSpec((tt, C), lambda i: (i, col0 // C)),
        out_shape=jax.ShapeDtypeStruct(buf.shape, buf.dtype), input_output_aliases={1: 0},
        compiler_params=_cparams(("parallel",)))(src, buf)


def _conv_bwd_in(dh, w, *, name, K, tt=512, tc=512, out_dtype=BF16, into=None):
    T, C = dh.shape
    tt = min(tt, T)
    nb8, nT = tt // 8, T // tt
    last8 = T // 8 - 1
    cb0 = 0 if into is None else into[1] // tc

    def body(d_ref, n_ref, w_ref, *rest):
        o_ref = rest[-1]
        notlast = (pl.program_id(1) < nT - 1).astype(F32)
        d = d_ref[...]
        nxt = n_ref[...] * notlast
        w_ = w_ref[...]
        acc = d * w_[K - 1:K, :]
        for sh in range(1, K):
            acc = acc + _shift_up(d, nxt, sh) * w_[K - 1 - sh:K - sh, :]
        o_ref[...] = acc.astype(out_dtype)

    in_specs = [pl.BlockSpec((tt, tc), lambda c, i: (i, c)),
                pl.BlockSpec((8, tc), lambda c, i: (jnp.minimum((i + 1) * nb8, last8), c)),
                pl.BlockSpec((K, tc), lambda c, i: (0, c))]
    args = [dh, dh, w]
    if into is None:
        out_shape, alias = jax.ShapeDtypeStruct((T, C), out_dtype), {}
    else:
        assert into[0].dtype == out_dtype and into[1] % tc == 0
        in_specs.append(_ANY)
        args.append(into[0])
        out_shape, alias = jax.ShapeDtypeStruct(into[0].shape, out_dtype), {3: 0}
    return pl.pallas_call(
        body, name=name, grid=(C // tc, nT), in_specs=in_specs,
        out_specs=pl.BlockSpec((tt, tc), lambda c, i: (i, c + cb0)),
        out_shape=out_shape, input_output_aliases=alias,
        compiler_params=_cparams(("parallel", "parallel")))(*args)


def _ffn_conv_fwd(a, w, b, *, name, tt=256, tc=1408):
    T = a.shape[0]
    tt = min(tt, T)
    K, nbh, nb8 = FFN_CONV, D_FF // tc, tt // 8

    def body(ag_ref, pg_ref, av_ref, pv_ref, wg_ref, wv_ref, bg_ref, bv_ref, o_ref):
        first = (pl.program_id(1) > 0).astype(F32)
        hg, _ = _conv_hid(ag_ref[...], pg_ref[...] * first, wg_ref[...], bg_ref[...], K)
        hv, _ = _conv_hid(av_ref[...], pv_ref[...] * first, wv_ref[...], bv_ref[...], K)
        o_ref[...] = (hg * _sigmoid(hg) * hv).astype(BF16)

    return pl.pallas_call(
        body, name=name, grid=(nbh, T // tt),
        in_specs=[pl.BlockSpec((tt, tc), lambda c, i: (i, c)),
                  pl.BlockSpec((8, tc), lambda c, i: (_prev_idx(i, nb8), c)),
                  pl.BlockSpec((tt, tc), lambda c, i: (i, c + nbh)),
                  pl.BlockSpec((8, tc), lambda c, i: (_prev_idx(i, nb8), c + nbh)),
                  pl.BlockSpec((K, tc), lambda c, i: (0, c)), pl.BlockSpec((K, tc), lambda c, i: (0, c + nbh)),
                  pl.BlockSpec((1, tc), lambda c, i: (0, c)), pl.BlockSpec((1, tc), lambda c, i: (0, c + nbh))],
        out_specs=pl.BlockSpec((tt, tc), lambda c, i: (i, c)),
        out_shape=jax.ShapeDtypeStruct((T, D_FF), BF16),
        compiler_params=_cparams(("parallel", "parallel")))(a, a, a, a, w, w, b, b)


def _ffn_conv_bwd_pre(a, w, b, dp, *, name, tt=256, tc=1408):
    T = a.shape[0]
    tt = min(tt, T)
    K, nbh, nb8 = FFN_CONV, D_FF // tc, tt // 8

    def body(ao_ref, po_ref, ag_ref, pg_ref, av_ref, pv_ref, wg_ref, wv_ref, bg_ref, bv_ref, dp_ref,
             dh_ref, dw_ref, db_ref):
        j = pl.program_id(0)
        t = pl.program_id(1)
        first = (t > 0).astype(F32)
        hg, _ = _conv_hid(ag_ref[...], pg_ref[...] * first, wg_ref[...], bg_ref[...], K)
        hv, _ = _conv_hid(av_ref[...], pv_ref[...] * first, wv_ref[...], bv_ref[...], K)
        sg = _sigmoid(hg)
        d = dp_ref[...].astype(F32)
        is_gate = (j < nbh).astype(F32)
        dh = d * (is_gate * (hv * (sg * (1.0 + hg * (1.0 - sg)))) + (1.0 - is_gate) * (hg * sg))
        dh_ref[...] = dh
        xo = ao_ref[...]
        po = po_ref[...] * first

        @pl.when(t == 0)
        def _():
            dw_ref[...] = jnp.zeros_like(dw_ref)
            db_ref[...] = jnp.zeros_like(db_ref)

        db_ref[...] += jnp.sum(dh, axis=0, keepdims=True)
        for jj in range(K):
            sh = K - 1 - jj
            xs = xo if sh == 0 else _shift_down(xo, po, sh)
            dw_ref[jj:jj + 1, :] += jnp.sum(dh * xs, axis=0, keepdims=True)

    def gi(c):
        return lax.rem(c, nbh)

    return pl.pallas_call(
        body, name=name, grid=(2 * nbh, T // tt),
        in_specs=[pl.BlockSpec((tt, tc), lambda c, i: (i, c)),
                  pl.BlockSpec((8, tc), lambda c, i: (_prev_idx(i, nb8), c)),
                  pl.BlockSpec((tt, tc), lambda c, i: (i, gi(c))),
                  pl.BlockSpec((8, tc), lambda c, i: (_prev_idx(i, nb8), gi(c))),
                  pl.BlockSpec((tt, tc), lambda c, i: (i, gi(c) + nbh)),
                  pl.BlockSpec((8, tc), lambda c, i: (_prev_idx(i, nb8), gi(c) + nbh)),
                  pl.BlockSpec((K, tc), lambda c, i: (0, gi(c))), pl.BlockSpec((K, tc), lambda c, i: (0, gi(c) + nbh)),
                  pl.BlockSpec((1, tc), lambda c, i: (0, gi(c))), pl.BlockSpec((1, tc), lambda c, i: (0, gi(c) + nbh)),
                  pl.BlockSpec((tt, tc), lambda c, i: (i, gi(c)))],
        out_specs=[pl.BlockSpec((tt, tc), lambda c, i: (i, c)), pl.BlockSpec((K, tc), lambda c, i: (0, c)),
                   pl.BlockSpec((1, tc), lambda c, i: (0, c))],
        out_shape=[jax.ShapeDtypeStruct((T, 2 * D_FF), F32), jax.ShapeDtypeStruct((K, 2 * D_FF), F32),
                   jax.ShapeDtypeStruct((1, 2 * D_FF), F32)],
        compiler_params=_cparams(("parallel", "arbitrary")))(a, a, a, a, a, a, w, w, b, b, dp)


def _ffn_conv_fwd3(a3, w, b, *, name, tt=256, tc=1408):
    T = a3.shape[1]
    tt = min(tt, T)
    K, nbh, n16 = FFN_CONV, D_FF // tc, tt // 16

    def body(a_ref, p_ref, wg_ref, wv_ref, bg_ref, bv_ref, o_ref):
        first = (pl.program_id(1) > 0).astype(F32)
        a = a_ref[...].astype(F32)
        prev = p_ref[...].astype(F32)[:, 8:16, :] * first
        hg, _ = _conv_hid(a[0], prev[0], wg_ref[...], bg_ref[...], K)
        hv, _ = _conv_hid(a[1], prev[1], wv_ref[...], bv_ref[...], K)
        o_ref[...] = (hg * _sigmoid(hg) * hv).astype(BF16)

    return pl.pallas_call(
        body, name=name, grid=(nbh, T // tt),
        in_specs=[pl.BlockSpec((2, tt, tc), lambda c, i: (0, i, c)),
                  pl.BlockSpec((2, 16, tc), lambda c, i: (0, _prev_idx(i, n16), c)),
                  pl.BlockSpec((K, tc), lambda c, i: (0, c)), pl.BlockSpec((K, tc), lambda c, i: (0, c + nbh)),
                  pl.BlockSpec((1, tc), lambda c, i: (0, c)), pl.BlockSpec((1, tc), lambda c, i: (0, c + nbh))],
        out_specs=pl.BlockSpec((tt, tc), lambda c, i: (i, c)),
        out_shape=jax.ShapeDtypeStruct((T, D_FF), BF16),
        compiler_params=_cparams(("parallel", "parallel")))(a3, a3, w, w, b, b)


def _ffn_conv_bwd3(a3, w, b, dp, *, name, tt=256, tc=1408):
    T = a3.shape[1]
    tt = min(tt, T)
    K, nbh, n16 = FFN_CONV, D_FF // tc, tt // 16

    def body(a_ref, p_ref, wg_ref, wv_ref, bg_ref, bv_ref, dp_ref, dh_ref, dw_ref, db_ref):
        t = pl.program_id(1)
        first = (t > 0).astype(F32)
        a = a_ref[...].astype(F32)
        prev = p_ref[...].astype(F32)[:, 8:16, :] * first
        hg, sh_g = _conv_hid(a[0], prev[0], wg_ref[...], bg_ref[...], K)
        hv, sh_v = _conv_hid(a[1], prev[1], wv_ref[...], bv_ref[...], K)
        sg = _sigmoid(hg)
        d = dp_ref[...].astype(F32)
        dhg = d * hv * (sg * (1.0 + hg * (1.0 - sg)))
        dhv = d * (hg * sg)
        dh_ref[0] = dhg.astype(BF16)
        dh_ref[1] = dhv.astype(BF16)

        @pl.when(t == 0)
        def _():
            dw_ref[...] = jnp.zeros_like(dw_ref)
            db_ref[...] = jnp.zeros_like(db_ref)

        db_ref[0] += jnp.sum(dhg, axis=0, keepdims=True)
        db_ref[1] += jnp.sum(dhv, axis=0, keepdims=True)
        for j in range(K):
            dw_ref[0, j:j + 1, :] += jnp.sum(dhg * sh_g[j], axis=0, keepdims=True)
            dw_ref[1, j:j + 1, :] += jnp.sum(dhv * sh_v[j], axis=0, keepdims=True)

    return pl.pallas_call(
        body, name=name, grid=(nbh, T // tt),
        in_specs=[pl.BlockSpec((2, tt, tc), lambda c, i: (0, i, c)),
                  pl.BlockSpec((2, 16, tc), lambda c, i: (0, _prev_idx(i, n16), c)),
                  pl.BlockSpec((K, tc), lambda c, i: (0, c)), pl.BlockSpec((K, tc), lambda c, i: (0, c + nbh)),
                  pl.BlockSpec((1, tc), lambda c, i: (0, c)), pl.BlockSpec((1, tc), lambda c, i: (0, c + nbh)),
                  pl.BlockSpec((tt, tc), lambda c, i: (i, c))],
        out_specs=[pl.BlockSpec((2, tt, tc), lambda c, i: (0, i, c)), pl.BlockSpec((2, K, tc), lambda c, i: (0, 0, c)),
                   pl.BlockSpec((2, 1, tc), lambda c, i: (0, 0, c))],
        out_shape=[jax.ShapeDtypeStruct((2, T, D_FF), BF16), jax.ShapeDtypeStruct((2, K, D_FF), F32),
                   jax.ShapeDtypeStruct((2, 1, D_FF), F32)],
        compiler_params=_cparams(("parallel", "arbitrary")))(a3, a3, w, w, b, b, dp)


def _conv_bwd_in3(dh3, w, *, name, K, tt=256, tc=1408):
    H, T, C = dh3.shape
    tt = min(tt, T)
    nb, n16, nT = C // tc, tt // 16, T // tt
    last16 = T // 16 - 1

    def body(d_ref, n_ref, w_ref, o_ref):
        notlast = (pl.program_id(2) < nT - 1).astype(F32)
        d = d_ref[...].astype(F32)
        nxt = n_ref[...].astype(F32)[0:8, :] * notlast
        w_ = w_ref[...]
        acc = d * w_[K - 1:K, :]
        for sh in range(1, K):
            acc = acc + _shift_up(d, nxt, sh) * w_[K - 1 - sh:K - sh, :]
        o_ref[...] = acc.astype(BF16)

    return pl.pallas_call(
        body, name=name, grid=(H, nb, nT),
        in_specs=[pl.BlockSpec((None, tt, tc), lambda h, c, i: (h, i, c)),
                  pl.BlockSpec((None, 16, tc), lambda h, c, i: (h, jnp.minimum((i + 1) * n16, last16), c)),
                  pl.BlockSpec((K, tc), lambda h, c, i: (0, h * nb + c))],
        out_specs=pl.BlockSpec((None, tt, tc), lambda h, c, i: (h, i, c)),
        out_shape=jax.ShapeDtypeStruct((H, T, C), BF16),
        compiler_params=_cparams(("parallel", "parallel", "parallel")))(dh3, dh3, w)


def _cumsum_rows(x):
    L = x.shape[0]
    row = lax.broadcasted_iota(jnp.int32, x.shape, 0)
    k = 1
    while k < L:
        x = x + jnp.where(row >= k, pltpu.roll(x, k, 0), 0.0)
        k *= 2
    return x


def _rcumsum_rows(x):
    L = x.shape[0]
    row = lax.broadcasted_iota(jnp.int32, x.shape, 0)
    k = 1
    while k < L:
        x = x + jnp.where(row < L - k, pltpu.roll(x, L - k, 0), 0.0)
        k *= 2
    return x


def _split_terms(m, n):
    terms, rest = [], m
    for _ in range(n):
        t = rest.astype(BF16)
        terms.append(t)
        rest = rest - t.astype(F32)
    return jnp.concatenate(terms, axis=1)


def _select_dot(m, n_terms, n_out, cond):
    K = m.shape[1]
    k = lax.broadcasted_iota(jnp.int32, (K, n_out), 0)
    j = lax.broadcasted_iota(jnp.int32, (K, n_out), 1)
    sel = cond(k, j).astype(BF16)
    return jnp.dot(_split_terms(m, n_terms), jnp.concatenate([sel] * n_terms, axis=0), preferred_element_type=F32)


def _rowsum_mxu(m):
    return _select_dot(m, 2, 128, lambda k, j: k >= 0)


def _lane_block_sums(m, width):
    shift = width.bit_length() - 1
    return _select_dot(m, 2, 128, lambda k, j: j == jnp.right_shift(k, shift))


def _heads_to_pairs(m):
    return _select_dot(m, 3, 512, lambda k, j: k == jnp.right_shift(j, 6))


def _ssd_common(dt_ref, par_ref):
    par = par_ref[...]
    raw = dt_ref[...] + par[0:1, :]
    dt = _softplus(raw)
    a = -jnp.exp(par[1:2, :])
    cs = _cumsum_rows(dt * a)
    L = cs.shape[0]
    cs_last = cs[L - 1:L, :]
    return raw, dt, a, par[2:3, :], cs, cs.T, jnp.exp(cs), jnp.exp(cs_last - cs), jnp.exp(cs_last)


def _ssd_specs(nc, rev):
    L = SSM_CHUNK

    def ci(c):
        return nc - 1 - c if rev else c

    return [pl.BlockSpec((L, D_INNER), lambda c: (ci(c), 0)),
            pl.BlockSpec((L, GN), lambda c: (ci(c), D_INNER // GN)),
            pl.BlockSpec((L, GN), lambda c: (ci(c), D_INNER // GN + 1)),
            pl.BlockSpec((SSM_GROUPS, L, 128), lambda c: (0, ci(c), 0)),
            pl.BlockSpec((SSM_GROUPS, 8, 128), lambda c: (0, 0, 0)),
            pl.BlockSpec((L, D_INNER), lambda c: (ci(c), 0)),
            pl.BlockSpec((1, D_INNER), lambda c: (0, 0))], ci


def _round_robin(gens):
    live = list(gens)
    while live:
        nxt = []
        for gen in live:
            try:
                next(gen)
                nxt.append(gen)
            except StopIteration:
                pass
        live = nxt


def _group_views(g, wide, narrow, lead):
    return ([r.at[:, g * 512:(g + 1) * 512] for r in wide], [r.at[:, g * 128:(g + 1) * 128] for r in narrow],
            [r.at[g] for r in lead])


def _ssd_fwd(xbc_c, zx, dtg, par, gnw, *, name):
    T = xbc_c.shape[0]
    L = SSM_CHUNK
    nc = T // L
    in_specs, ci = _ssd_specs(nc, False)

    def body(xs_ref, b_ref, c_ref, dt_ref, par_ref, z_ref, gnw_ref, y_ref, yn_ref, st_ref, h_ref):
        @pl.when(pl.program_id(0) == 0)
        def _():
            h_ref[...] = jnp.zeros_like(h_ref)

        gens = []
        for g in range(SSM_GROUPS):
            (xs, z, gw, y, yn), (b, c), (dt, pr, st, h) = _group_views(
                g, [xs_ref, z_ref, gnw_ref, y_ref, yn_ref], [b_ref, c_ref], [dt_ref, par_ref, st_ref, h_ref])
            gens.append(group(xs, b, c, dt, pr, z, gw, y, yn, st, h))
        _round_robin(gens)

    def group(xs_ref, b_ref, c_ref, dt_ref, par_ref, z_ref, gnw_ref, y_ref, yn_ref, st_ref, h_ref):
        _, dt, _, dsk, cs, csT, ecs, eend, dec = _ssd_common(dt_ref, par_ref)
        Bb = b_ref[...].astype(BF16)
        Cb = c_ref[...].astype(BF16)
        G = lax.dot_general(Cb, Bb, _NT, preferred_element_type=F32)
        row = lax.broadcasted_iota(jnp.int32, (L, L), 0)
        col = lax.broadcasted_iota(jnp.int32, (L, L), 1)
        tril = col <= row
        lo = lax.broadcasted_iota(jnp.int32, (L, 128), 1) < 64
        lo1 = lax.broadcasted_iota(jnp.int32, (1, 128), 1) < 64
        dt_x, ecs_x, eend_x = (_heads_to_pairs(m) for m in (dt, ecs, eend))
        for pp in range(4):
            hA, hB = 2 * pp, 2 * pp + 1
            lanes = slice(pp * 128, (pp + 1) * 128)

            def sel1(m):
                return jnp.where(lo1, m[:, hA:hA + 1], m[:, hB:hB + 1])

            X = xs_ref[:, lanes]
            xd = X * dt_x[:, lanes]
            xdb = xd.astype(BF16)
            ys = []
            for h in (hA, hB):
                Lm = jnp.where(tril, jnp.exp(jnp.minimum(cs[:, h:h + 1] - csT[h:h + 1, :], 0.0)), 0.0)
                ys.append(jnp.dot((G * Lm).astype(BF16), xdb, preferred_element_type=F32))
                yield
            Hp = h_ref[pp]
            st_ref[pp] = Hp
            yoff = jnp.dot(Cb, Hp.astype(BF16), preferred_element_type=F32) * ecs_x[:, lanes]
            y_ref[:, lanes] = jnp.where(lo, ys[0], ys[1]) + yoff + sel1(dsk) * X
            S = lax.dot_general(Bb, (xd * eend_x[:, lanes]).astype(BF16), _TN, preferred_element_type=F32)
            h_ref[pp] = Hp * sel1(dec) + S
            yield
        zv = z_ref[...]
        yg = y_ref[...] * (zv * _sigmoid(zv))
        r = jnp.tile(lax.rsqrt(_rowsum_mxu(yg * yg) * (1.0 / 512) + EPS), (1, 4))
        yn_ref[...] = (yg * r * gnw_ref[...]).astype(BF16)

    return pl.pallas_call(
        body, name=name, grid=(nc,), in_specs=in_specs,
        out_specs=[pl.BlockSpec((L, D_INNER), lambda c: (c, 0)), pl.BlockSpec((L, D_INNER), lambda c: (c, 0)),
                   pl.BlockSpec((SSM_GROUPS, None, 4, 128, 128), lambda c: (0, c, 0, 0, 0))],
        out_shape=[jax.ShapeDtypeStruct((T, D_INNER), F32), jax.ShapeDtypeStruct((T, D_INNER), BF16),
                   jax.ShapeDtypeStruct((SSM_GROUPS, nc, 4, 128, 128), F32)],
        scratch_shapes=[pltpu.VMEM((SSM_GROUPS, 4, 128, 128), F32)],
        compiler_params=_cparams(("arbitrary",)))(xbc_c, xbc_c, xbc_c, dtg, par, zx, gnw)


def _ssd_bwd(xbc_c, zx, dtg, par, gnw, y, st, dyn, *, name):
    T = xbc_c.shape[0]
    L = SSM_CHUNK
    nc = T // L
    in_specs, ci = _ssd_specs(nc, True)
    in_specs += [pl.BlockSpec((L, D_INNER), lambda c: (ci(c), 0)),
                 pl.BlockSpec((SSM_GROUPS, None, 4, 128, 128), lambda c: (0, ci(c), 0, 0, 0)),
                 pl.BlockSpec((L, D_INNER), lambda c: (ci(c), 0))]

    def body(xs_ref, b_ref, c_ref, dt_ref, par_ref, z_ref, gnw_ref, y_ref, st_ref, dyn_ref,
             dxbc_ref, dz_ref, ddt_ref, dgnw_ref, dpar_ref, dh_ref):
        @pl.when(pl.program_id(0) == 0)
        def _():
            dh_ref[...] = jnp.zeros_like(dh_ref)
            dgnw_ref[...] = jnp.zeros_like(dgnw_ref)
            dpar_ref[...] = jnp.zeros_like(dpar_ref)

        dxs_ref = dxbc_ref.at[:, 0:D_INNER]
        db_ref = dxbc_ref.at[:, D_INNER:D_INNER + GN]
        dc_ref = dxbc_ref.at[:, D_INNER + GN:CONV_DIM]

        gens = []
        for g in range(SSM_GROUPS):
            (xs, z, gw, y, dyn, dxs, dz, dgw), (b, c, db, dc), (dt, pr, st, ddt, dpr, dh) = _group_views(
                g, [xs_ref, z_ref, gnw_ref, y_ref, dyn_ref, dxs_ref, dz_ref, dgnw_ref], [b_ref, c_ref, db_ref, dc_ref],
                [dt_ref, par_ref, st_ref, ddt_ref, dpar_ref, dh_ref])
            gens.append(group(xs, b, c, dt, pr, z, gw, y, st, dyn, dxs, db, dc, dz, ddt, dgw, dpr, dh))
        _round_robin(gens)

    def group(xs_ref, b_ref, c_ref, dt_ref, par_ref, z_ref, gnw_ref, y_ref, st_ref, dyn_ref,
              dxs_ref, db_ref, dc_ref, dz_ref, ddt_ref, dgnw_ref, dpar_ref, dh_ref):
        yv = y_ref[...]
        zv = z_ref[...]
        sg = _sigmoid(zv)
        sz = zv * sg
        yg = yv * sz
        r = jnp.tile(lax.rsqrt(_rowsum_mxu(yg * yg) * (1.0 / 512) + EPS), (1, 4))
        yh = yg * r
        dyn = dyn_ref[...].astype(F32)
        dgnw_ref[...] += jnp.sum(dyn * yh, axis=0, keepdims=True)
        dyh = dyn * gnw_ref[...]
        dyg = r * (dyh - yh * jnp.tile(_rowsum_mxu(dyh * yh) * (1.0 / 512), (1, 4)))
        dY_all = dyg * sz
        dz_ref[...] = (dyg * yv * (sg * (1.0 + zv * (1.0 - sg)))).astype(dz_ref.dtype)

        yield
        raw, dt, a, dsk, cs, csT, ecs, eend, dec = _ssd_common(dt_ref, par_ref)
        Bb = b_ref[...].astype(BF16)
        Cb = c_ref[...].astype(BF16)
        G = lax.dot_general(Cb, Bb, _NT, preferred_element_type=F32)
        row = lax.broadcasted_iota(jnp.int32, (L, L), 0)
        col = lax.broadcasted_iota(jnp.int32, (L, L), 1)
        tril = col <= row
        lo = lax.broadcasted_iota(jnp.int32, (L, 128), 1) < 64
        lane1 = lax.broadcasted_iota(jnp.int32, (1, 128), 1)
        lo1 = lane1 < 64
        rowl = lax.broadcasted_iota(jnp.int32, (L, 128), 0)
        dt_x, ecs_x, eend_x = (_heads_to_pairs(m) for m in (dt, ecs, eend))
        dG = jnp.zeros((L, L), F32)
        dB = jnp.zeros((L, SSM_STATE), F32)
        dC = jnp.zeros((L, SSM_STATE), F32)
        dcs_t = jnp.zeros((L, L), F32)
        tails = jnp.zeros((1, 128), F32)
        dD_row = jnp.zeros((1, 128), F32)
        v_parts, prod_parts = [], []

        def tot(m):
            return jnp.sum(jnp.sum(m, axis=0, keepdims=True), axis=1, keepdims=True)

        for pp in range(4):
            hA, hB = 2 * pp, 2 * pp + 1
            lanes = slice(pp * 128, (pp + 1) * 128)

            def sel1(m):
                return jnp.where(lo1, m[:, hA:hA + 1], m[:, hB:hB + 1])

            X = xs_ref[:, lanes]
            dY = dY_all[:, lanes]
            dtsel = dt_x[:, lanes]
            xd = X * dtsel
            xdb = xd.astype(BF16)
            dYb = dY.astype(BF16)
            Hp = st_ref[pp]
            Hb = Hp.astype(BF16)
            dHn = dh_ref[pp]
            dHb = dHn.astype(BF16)
            ecs_sel = ecs_x[:, lanes]
            eend_sel = eend_x[:, lanes]
            dxd_state = jnp.dot(Bb, dHb, preferred_element_type=F32) * eend_sel
            yoff = jnp.dot(Cb, Hb, preferred_element_type=F32) * ecs_sel
            dYe = (dY * ecs_sel).astype(BF16)
            dC = dC + lax.dot_general(dYe, Hb, _NT, preferred_element_type=F32)
            dB = dB + lax.dot_general((xd * eend_sel).astype(BF16), dHb, _NT, preferred_element_type=F32)
            dh_ref[pp] = dHn * sel1(dec) + lax.dot_general(Cb, dYe, _TN, preferred_element_type=F32)
            q = xd * dxd_state
            dyq = dY * yoff - q
            qcol = jnp.sum(q, axis=0, keepdims=True)
            hcol = jnp.sum(dHn * Hp, axis=0, keepdims=True)
            dxd_diag = []
            for h, msk, msk1 in ((hA, lo, lo1), (hB, jnp.logical_not(lo), jnp.logical_not(lo1))):
                Lm = jnp.where(tril, jnp.exp(jnp.minimum(cs[:, h:h + 1] - csT[h:h + 1, :], 0.0)), 0.0)
                M = G * Lm
                dxd_diag.append(lax.dot_general(M.astype(BF16), dYb, _TN, preferred_element_type=F32))
                dM = lax.dot_general(jnp.where(msk, dY, 0.0).astype(BF16), xdb, _NT, preferred_element_type=F32)
                dG = dG + dM * Lm
                W = dM * M
                dcs_t = dcs_t + jnp.where(row == h, jnp.sum(W, axis=0, keepdims=True), 0.0)
                v_parts.append(W + jnp.where(msk, dyq, 0.0))
                tail = (jnp.sum(jnp.where(msk1, qcol, 0.0), axis=1, keepdims=True)
                        + dec[:, h:h + 1] * jnp.sum(jnp.where(msk1, hcol, 0.0), axis=1, keepdims=True))
                tails = tails + jnp.where(lane1 == h, tail, 0.0)
                yield
            dxd = jnp.where(lo, dxd_diag[0], dxd_diag[1]) + dxd_state
            prod_parts.append(dxd * X)
            dxs_ref[:, lanes] = dxd * dtsel + sel1(dsk) * dY
            dyx = jnp.sum(dY * X, axis=0, keepdims=True)
            sA = jnp.sum(jnp.where(lo1, dyx, 0.0), axis=1, keepdims=True)
            sB = jnp.sum(dyx, axis=1, keepdims=True) - sA
            dD_row = dD_row + jnp.where(lane1 == hA, sA, 0.0) + jnp.where(lane1 == hB, sB, 0.0)
            yield
        dGb = dG.astype(BF16)
        db_ref[...] = dB + lax.dot_general(dGb, Cb, _TN, preferred_element_type=F32)
        dc_ref[...] = dC + jnp.dot(dGb, Bb, preferred_element_type=F32)
        dcs_mat = _lane_block_sums(jnp.concatenate(v_parts, axis=1), 128) + jnp.where(rowl == L - 1, tails, 0.0)
        ddt_mat = _lane_block_sums(jnp.concatenate(prod_parts, axis=1), 64)
        dad = _rcumsum_rows(dcs_mat - dcs_t.T)
        draw = (a * dad + ddt_mat) * _sigmoid(raw)
        ddt_ref[...] = draw
        dpar_ref[0:1, :] += jnp.sum(draw, axis=0, keepdims=True)
        dpar_ref[1:2, :] += jnp.sum(dt * dad, axis=0, keepdims=True) * a
        dpar_ref[2:3, :] += dD_row

    return pl.pallas_call(
        body, name=name, grid=(nc,), in_specs=in_specs,
        out_specs=[pl.BlockSpec((L, CONV_DIM), lambda c: (ci(c), 0)),
                   pl.BlockSpec((L, D_INNER), lambda c: (ci(c), 0)),
                   pl.BlockSpec((SSM_GROUPS, L, 128), lambda c: (0, ci(c), 0)),
                   pl.BlockSpec((1, D_INNER), lambda c: (0, 0)),
                   pl.BlockSpec((SSM_GROUPS, 8, 128), lambda c: (0, 0, 0))],
        out_shape=[jax.ShapeDtypeStruct((T, CONV_DIM), F32), jax.ShapeDtypeStruct((T, IN_PROJ_PAD), BF16),
                   jax.ShapeDtypeStruct((SSM_GROUPS, T, 128), F32), jax.ShapeDtypeStruct((1, D_INNER), F32),
                   jax.ShapeDtypeStruct((SSM_GROUPS, 8, 128), F32)],
        scratch_shapes=[pltpu.VMEM((SSM_GROUPS, 4, 128, 128), F32)],
        compiler_params=_cparams(("arbitrary",)))(xbc_c, xbc_c, xbc_c, dtg, par, zx, gnw, y, st, dyn)


SB_KEYS = 512
SB_SCAN = 256
SB_STRIP = 256


def _tri(width, cond):
    kk = lax.broadcasted_iota(jnp.int32, (width, width), 0)
    jj = lax.broadcasted_iota(jnp.int32, (width, width), 1)
    return cond(kk, jj).astype(BF16)


def _sba_diag_mask():
    Bq = SB_BLOCK
    rowi = lax.broadcasted_iota(jnp.int32, (2 * Bq, Bq), 0)
    return lax.broadcasted_iota(jnp.int32, (2 * Bq, Bq), 1) < jnp.where(rowi >= Bq, rowi - Bq, rowi)


_LOG2E = 1.4426950408889634


def _softplus2(z2):
    return jnp.maximum(z2, 0.0) + jnp.log2(1.0 + jnp.exp2(-jnp.abs(z2)))


def _sba_sub_fwd(zb, c, U, mask):
    z2 = zb * _LOG2E
    s = _softplus2(z2)
    if mask is not None:
        s = jnp.where(mask, s, 0.0)
    R = c + jnp.dot(s.astype(BF16), U, preferred_element_type=F32)
    A = jnp.exp2(z2 - s - R)
    if mask is not None:
        A = jnp.where(mask, A, 0.0)
    return A.astype(BF16), R[:, 0:1] + s[:, 0:1]


def _sba_sub_bwd(zb, dAb, Lt, pc, pe, Uincl, Uexcl, mask):
    last = zb.shape[1] - 1
    z2 = zb * _LOG2E
    s = _softplus2(z2)
    g = z2 - s
    if mask is not None:
        s = jnp.where(mask, s, 0.0)
    P = pc + jnp.dot(s.astype(BF16), Uincl, preferred_element_type=F32)
    A = jnp.exp2(g - (Lt - P))
    if mask is not None:
        A = jnp.where(mask, A, 0.0)
    E = dAb * A
    PE = pe + jnp.dot(E.astype(BF16), Uexcl, preferred_element_type=F32)
    dz = E - jnp.exp2(g) * (E + PE)
    if mask is not None:
        dz = jnp.where(mask, dz, 0.0)
    return (A.astype(BF16), dz.astype(BF16), P[:, last:last + 1], PE[:, last:last + 1] + E[:, last:last + 1])


def _stack_heads(v):
    lo = lax.broadcasted_iota(jnp.int32, v.shape, 1) < 64
    zero = jnp.zeros_like(v)
    return jnp.concatenate([jnp.where(lo, v, zero), jnp.where(lo, zero, v)], axis=0)


def _unstack_heads(v):
    lo = lax.broadcasted_iota(jnp.int32, (SB_BLOCK, 128), 1) < 64
    return jnp.where(lo, v[:SB_BLOCK], v[SB_BLOCK:])


def _sba_rows(a):
    return slice(2 * a * SB_BLOCK, 2 * (a + 1) * SB_BLOCK)


def _sba_diag_case(a, b):
    Bq = SB_BLOCK
    if b * SB_SCAN >= (a + 1) * Bq:
        return "skip"
    if (b + 1) * SB_SCAN <= a * Bq:
        return "full"
    rowi = lax.broadcasted_iota(jnp.int32, (2 * Bq, SB_SCAN), 0)
    qpos = a * Bq + jnp.where(rowi >= Bq, rowi - Bq, rowi)
    return b * SB_SCAN + lax.broadcasted_iota(jnp.int32, (2 * Bq, SB_SCAN), 1) < qpos


def _sba_fwd(q, kv, *, name):
    T = q.shape[0]
    Bq = SB_BLOCK
    nsub = SB_KEYS // Bq
    nscan = SB_KEYS // SB_SCAN
    R = 2 * SB_KEYS
    assert T % SB_KEYS == 0 and SB_STRIP == 2 * Bq
    scale = 1.0 / math.sqrt(SB_HEAD_DIM)

    def body(q_ref, k_ref, v_ref, o_ref, lt_ref, z_s, a_s, c_s, acc_s):
        i = pl.program_id(1)
        U2 = _tri(SB_SCAN, lambda k, j: k > j)
        qs_all = jnp.concatenate([_stack_heads(q_ref[a * Bq:(a + 1) * Bq, :] * scale) for a in range(nsub)], axis=0)
        c_s[...] = jnp.zeros_like(c_s)
        acc_s[...] = jnp.zeros_like(acc_s)

        def scores(J, slot):
            off = pl.multiple_of(J * SB_KEYS, SB_KEYS)
            z_s[slot] = lax.dot_general(qs_all, k_ref[pl.ds(off, SB_KEYS), :], _NT, preferred_element_type=F32)

        def weights(slot, diag):
            for a in range(nsub):
                rows = _sba_rows(a)
                c = c_s[rows, :]
                for b in reversed(range(nscan)):
                    cols = slice(b * SB_SCAN, (b + 1) * SB_SCAN)
                    case = _sba_diag_case(a, b) if diag else "full"
                    if isinstance(case, str) and case == "skip":
                        a_s[slot, rows, cols] = jnp.zeros((2 * Bq, SB_SCAN), BF16)
                        continue
                    A, c = _sba_sub_fwd(z_s[slot, rows, cols], c, U2, None if isinstance(case, str) else case)
                    a_s[slot, rows, cols] = A
                c_s[rows, :] = c

        def values(J, slot):
            off = pl.multiple_of(J * SB_KEYS, SB_KEYS)
            acc_s[...] += jnp.dot(a_s[slot], v_ref[pl.ds(off, SB_KEYS), :], preferred_element_type=F32)

        scores(i, 0)
        weights(0, True)
        scores(jnp.maximum(i - 1, 0), 1)

        def two_steps(u, _):
            t = 2 * u + 1
            weights(1, False)
            scores(jnp.maximum(i - t - 1, 0), 0)
            values(i - t + 1, 0)
            weights(0, False)
            scores(jnp.maximum(i - t - 2, 0), 1)
            values(i - t, 1)
            return 0

        lax.fori_loop(0, i // 2, two_steps, 0)
        odd = lax.rem(i, 2) == 1

        @pl.when(jnp.logical_not(odd))
        def _():
            values(0, 0)

        @pl.when(odd)
        def _():
            weights(1, False)
            values(1, 0)
            values(0, 1)
        for a in range(nsub):
            o_ref[a * Bq:(a + 1) * Bq, :] = _unstack_heads(acc_s[_sba_rows(a), :]).astype(BF16)
            lt_ref[a * Bq:(a + 1) * Bq, :] = _unstack_heads(jnp.broadcast_to(c_s[_sba_rows(a), :], (2 * Bq, 128)))

    return pl.pallas_call(
        body, name=name, grid=(SB_HEADS // 2, T // SB_KEYS),
        in_specs=[pl.BlockSpec((SB_KEYS, 128), lambda p, i: (i, p)), pl.BlockSpec((T, 128), lambda p, i: (0, p)),
                  pl.BlockSpec((T, 128), lambda p, i: (0, p + SB_HEADS // 2))],
        out_specs=[pl.BlockSpec((SB_KEYS, 128), lambda p, i: (i, p)),
                   pl.BlockSpec((None, SB_KEYS, 128), lambda p, i: (p, i, 0))],
        out_shape=[jax.ShapeDtypeStruct((T, D_MODEL), BF16), jax.ShapeDtypeStruct((SB_HEADS // 2, T, 128), F32)],
        scratch_shapes=[pltpu.VMEM((2, R, SB_KEYS), F32), pltpu.VMEM((2, R, SB_KEYS), BF16),
                        pltpu.VMEM((R, 1), F32), pltpu.VMEM((R, 128), F32)],
        compiler_params=_cparams(("parallel", "parallel")))(q, kv, kv)


def _sba_bwd(q, kv, lt, do, *, name):
    T = q.shape[0]
    Bq = SB_BLOCK
    nq = T // SB_KEYS
    nsub = SB_KEYS // Bq
    nscan = SB_KEYS // SB_SCAN
    R = 2 * SB_KEYS
    assert T % SB_KEYS == 0 and SB_STRIP == 2 * Bq
    scale = 1.0 / math.sqrt(SB_HEAD_DIM)

    def body(q_ref, k_ref, v_ref, lt_ref, do_ref, dq_ref, dk_ref, dv_ref, dk_acc, dv_acc,
             z_s, da_s, a_s, dz_s, pc_s, pe_s, lt_s, dq_s):
        i = pl.program_id(1)

        @pl.when(i == 0)
        def _():
            dk_acc[...] = jnp.zeros_like(dk_acc)
            dv_acc[...] = jnp.zeros_like(dv_acc)

        Uincl = _tri(SB_SCAN, lambda k, j: k <= j)
        Uexcl = _tri(SB_SCAN, lambda k, j: k < j)
        qs, dos = [], []
        for a in range(nsub):
            rows = slice(a * Bq, (a + 1) * Bq)
            qs.append(_stack_heads(q_ref[rows, :] * scale))
            dos.append(_stack_heads(do_ref[rows, :]))
            lt_s[_sba_rows(a), :] = jnp.concatenate([lt_ref[rows, 0:1], lt_ref[rows, 64:65]], axis=0)
        qs_all = jnp.concatenate(qs, axis=0)
        dos_all = jnp.concatenate(dos, axis=0)
        pc_s[...] = jnp.zeros_like(pc_s)
        pe_s[...] = jnp.zeros_like(pe_s)
        a_s[1] = jnp.zeros((R, SB_KEYS), BF16)
        dz_s[1] = jnp.zeros((R, SB_KEYS), BF16)

        def scores(J, slot):
            off = pl.multiple_of(J * SB_KEYS, SB_KEYS)
            z_s[slot] = lax.dot_general(qs_all, k_ref[pl.ds(off, SB_KEYS), :], _NT, preferred_element_type=F32)
            da_s[slot] = lax.dot_general(dos_all, v_ref[pl.ds(off, SB_KEYS), :], _NT, preferred_element_type=F32)

        def gradients(slot, diag):
            for a in range(nsub):
                rows = _sba_rows(a)
                pc, pe, Lt = pc_s[rows, :], pe_s[rows, :], lt_s[rows, :]
                for b in range(nscan):
                    cols = slice(b * SB_SCAN, (b + 1) * SB_SCAN)
                    case = _sba_diag_case(a, b) if diag else "full"
                    if isinstance(case, str) and case == "skip":
                        a_s[slot, rows, cols] = jnp.zeros((2 * Bq, SB_SCAN), BF16)
                        dz_s[slot, rows, cols] = jnp.zeros((2 * Bq, SB_SCAN), BF16)
                        continue
                    A, dz, pc, pe = _sba_sub_bwd(z_s[slot, rows, cols], da_s[slot, rows, cols], Lt, pc, pe, Uincl, Uexcl,
                                                 None if isinstance(case, str) else case)
                    a_s[slot, rows, cols] = A
                    dz_s[slot, rows, cols] = dz
                pc_s[rows, :] = pc
                pe_s[rows, :] = pe

        def products(J, slot):
            off = pl.multiple_of(J * SB_KEYS, SB_KEYS)
            dzt = dz_s[slot]
            dk_acc[pl.ds(off, SB_KEYS), :] += lax.dot_general(dzt, qs_all, _TN, preferred_element_type=F32)
            dv_acc[pl.ds(off, SB_KEYS), :] += lax.dot_general(a_s[slot], dos_all, _TN, preferred_element_type=F32)
            dq_s[...] += jnp.dot(dzt, k_ref[pl.ds(off, SB_KEYS), :], preferred_element_type=F32)

        dq_s[...] = jnp.zeros_like(dq_s)
        scores(0, 0)

        def two_steps(u, _):
            t = 2 * u
            gradients(0, False)
            scores(t + 1, 1)
            products(jnp.maximum(t - 1, 0), 1)
            gradients(1, False)
            scores(t + 2, 0)
            products(t, 0)
            return 0

        lax.fori_loop(0, i // 2, two_steps, 0)
        odd = lax.rem(i, 2) == 1

        @pl.when(jnp.logical_not(odd))
        def _():
            gradients(0, True)
            products(jnp.maximum(i - 1, 0), 1)
            products(i, 0)

        @pl.when(odd)
        def _():
            gradients(0, False)
            scores(i, 1)
            products(jnp.maximum(i - 2, 0), 1)
            gradients(1, True)
            products(i - 1, 0)
            products(i, 1)

        for a in range(nsub):
            dq_ref[a * Bq:(a + 1) * Bq, :] = (_unstack_heads(dq_s[_sba_rows(a), :]) * scale).astype(BF16)

        @pl.when(i == nq - 1)
        def _():
            dk_ref[...] = dk_acc[...].astype(BF16)
            dv_ref[...] = dv_acc[...].astype(BF16)

    return pl.pallas_call(
        body, name=name, grid=(SB_HEADS // 2, nq),
        in_specs=[pl.BlockSpec((SB_KEYS, 128), lambda p, i: (i, p)), pl.BlockSpec((T, 128), lambda p, i: (0, p)),
                  pl.BlockSpec((T, 128), lambda p, i: (0, p + SB_HEADS // 2)),
                  pl.BlockSpec((None, SB_KEYS, 128), lambda p, i: (p, i, 0)),
                  pl.BlockSpec((SB_KEYS, 128), lambda p, i: (i, p))],
        out_specs=[pl.BlockSpec((SB_KEYS, 128), lambda p, i: (i, p)), pl.BlockSpec((T, 128), lambda p, i: (0, p)),
                   pl.BlockSpec((T, 128), lambda p, i: (0, p))],
        out_shape=[jax.ShapeDtypeStruct((T, D_MODEL), BF16), jax.ShapeDtypeStruct((T, D_MODEL), BF16),
                   jax.ShapeDtypeStruct((T, D_MODEL), BF16)],
        scratch_shapes=[pltpu.VMEM((T, 128), F32), pltpu.VMEM((T, 128), F32),
                        pltpu.VMEM((2, R, SB_KEYS), F32), pltpu.VMEM((2, R, SB_KEYS), F32),
                        pltpu.VMEM((2, R, SB_KEYS), BF16), pltpu.VMEM((2, R, SB_KEYS), BF16),
                        pltpu.VMEM((R, 1), F32), pltpu.VMEM((R, 1), F32), pltpu.VMEM((R, 1), F32),
                        pltpu.VMEM((R, 128), F32)],
        compiler_params=_cparams(("parallel", "arbitrary")))(q, kv, kv, lt, do)


def _sba_fwd_old(q, kv, *, name):
    T = q.shape[0]
    Bq = SB_BLOCK
    nsub = SB_KEYS // Bq
    assert T % SB_KEYS == 0
    scale = 1.0 / math.sqrt(SB_HEAD_DIM)

    def body(q_ref, k_ref, v_ref, o_ref, lt_ref):
        I = pl.program_id(1)
        U1 = _tri(Bq, lambda k, j: k > j)
        U2 = _tri(SB_SCAN, lambda k, j: k > j)
        dmask = _sba_diag_mask()
        qs = [_stack_heads(q_ref[a * Bq:(a + 1) * Bq, :] * scale) for a in range(nsub)]
        cs, accs = [], []
        for a in range(nsub):
            c = jnp.zeros((2 * Bq, 1), F32)
            acc = jnp.zeros((2 * Bq, 128), F32)
            for b in range(a, -1, -1):
                off = pl.multiple_of(I * SB_KEYS + b * Bq, Bq)
                zb = lax.dot_general(qs[a], k_ref[pl.ds(off, Bq), :], _NT, preferred_element_type=F32)
                A, c = _sba_sub_fwd(zb, c, U1, dmask if b == a else None)
                acc = acc + jnp.dot(A, v_ref[pl.ds(off, Bq), :], preferred_element_type=F32)
            cs.append(c)
            accs.append(acc)
        qs_all = jnp.concatenate(qs, axis=0)

        def step(n, carry):
            c, acc = carry
            off = pl.multiple_of((I - 1 - n) * SB_KEYS, SB_KEYS)
            z = lax.dot_general(qs_all, k_ref[pl.ds(off, SB_KEYS), :], _NT, preferred_element_type=F32)
            parts = [None] * (SB_KEYS // SB_SCAN)
            for b in reversed(range(SB_KEYS // SB_SCAN)):
                parts[b], c = _sba_sub_fwd(z[:, b * SB_SCAN:(b + 1) * SB_SCAN], c, U2, None)
            return c, acc + jnp.dot(jnp.concatenate(parts, axis=1), v_ref[pl.ds(off, SB_KEYS), :],
                                    preferred_element_type=F32)

        c, acc = lax.fori_loop(0, I, step, (jnp.concatenate(cs, axis=0), jnp.concatenate(accs, axis=0)))
        for a in range(nsub):
            rows = slice(2 * a * Bq, 2 * (a + 1) * Bq)
            o_ref[a * Bq:(a + 1) * Bq, :] = _unstack_heads(acc[rows]).astype(BF16)
            lt_ref[a * Bq:(a + 1) * Bq, :] = _unstack_heads(jnp.broadcast_to(c[rows], (2 * Bq, 128)))

    return pl.pallas_call(
        body, name=name, grid=(SB_HEADS // 2, T // SB_KEYS),
        in_specs=[pl.BlockSpec((SB_KEYS, 128), lambda p, i: (i, p)), pl.BlockSpec((T, 128), lambda p, i: (0, p)),
                  pl.BlockSpec((T, 128), lambda p, i: (0, p + SB_HEADS // 2))],
        out_specs=[pl.BlockSpec((SB_KEYS, 128), lambda p, i: (i, p)),
                   pl.BlockSpec((None, SB_KEYS, 128), lambda p, i: (p, i, 0))],
        out_shape=[jax.ShapeDtypeStruct((T, D_MODEL), BF16), jax.ShapeDtypeStruct((SB_HEADS // 2, T, 128), F32)],
        compiler_params=_cparams(("parallel", "parallel")))(q, kv, kv)


def _sba_bwd_old(q, kv, lt, do, *, name):
    T = q.shape[0]
    Bq = SB_BLOCK
    nq = T // SB_KEYS
    nsub = SB_KEYS // Bq
    assert T % SB_KEYS == 0
    scale = 1.0 / math.sqrt(SB_HEAD_DIM)

    def body(q_ref, k_ref, v_ref, lt_ref, do_ref, dq_ref, dk_ref, dv_ref, dk_acc, dv_acc,
             z_s, da_s, a_s, dz_s, pc_s, pe_s, lt_s):
        i = pl.program_id(1)

        @pl.when(i == 0)
        def _():
            dk_acc[...] = jnp.zeros_like(dk_acc)
            dv_acc[...] = jnp.zeros_like(dv_acc)

        Uincl1 = _tri(Bq, lambda k, j: k <= j)
        Uexcl1 = _tri(Bq, lambda k, j: k < j)
        Uincl2 = _tri(SB_SCAN, lambda k, j: k <= j)
        Uexcl2 = _tri(SB_SCAN, lambda k, j: k < j)
        dmask = _sba_diag_mask()
        qs, dos, lts = [], [], []
        for a in range(nsub):
            rows = slice(a * Bq, (a + 1) * Bq)
            qs.append(_stack_heads(q_ref[rows, :] * scale))
            dos.append(_stack_heads(do_ref[rows, :]))
            lts.append(jnp.concatenate([lt_ref[rows, 0:1], lt_ref[rows, 64:65]], axis=0))
        qs_all = jnp.concatenate(qs, axis=0)
        dos_all = jnp.concatenate(dos, axis=0)
        lt_all = jnp.concatenate(lts, axis=0)

        R = 2 * nsub * Bq
        pc_s[...] = jnp.zeros_like(pc_s)
        pe_s[...] = jnp.zeros_like(pe_s)
        lt_s[...] = lt_all

        def scores(J, slot):
            off = pl.multiple_of(J * SB_KEYS, SB_KEYS)
            z_s[slot] = lax.dot_general(qs_all, k_ref[pl.ds(off, SB_KEYS), :], _NT, preferred_element_type=F32)
            da_s[slot] = lax.dot_general(dos_all, v_ref[pl.ds(off, SB_KEYS), :], _NT, preferred_element_type=F32)

        def elementwise(slot):
            for r in range(R // SB_STRIP):
                rows = slice(r * SB_STRIP, (r + 1) * SB_STRIP)
                pc, pe, Lt = pc_s[rows, :], pe_s[rows, :], lt_s[rows, :]
                for b in range(SB_KEYS // SB_SCAN):
                    cols = slice(b * SB_SCAN, (b + 1) * SB_SCAN)
                    A, dz, pc, pe = _sba_sub_bwd(z_s[slot, rows, cols], da_s[slot, rows, cols], Lt, pc, pe,
                                                 Uincl2, Uexcl2, None)
                    a_s[slot, rows, cols] = A
                    dz_s[slot, rows, cols] = dz
                pc_s[rows, :] = pc
                pe_s[rows, :] = pe

        def outputs(J, slot, dq_acc):
            off = pl.multiple_of(J * SB_KEYS, SB_KEYS)
            dzt = dz_s[slot]
            dk_acc[pl.ds(off, SB_KEYS), :] += lax.dot_general(dzt, qs_all, _TN, preferred_element_type=F32)
            dv_acc[pl.ds(off, SB_KEYS), :] += lax.dot_general(a_s[slot], dos_all, _TN, preferred_element_type=F32)
            return dq_acc + jnp.dot(dzt, k_ref[pl.ds(off, SB_KEYS), :], preferred_element_type=F32)

        a_s[1] = jnp.zeros((R, SB_KEYS), BF16)
        dz_s[1] = jnp.zeros((R, SB_KEYS), BF16)
        last = jnp.maximum(i - 1, 0)
        scores(0, 0)

        def step(J, dq_acc):
            slot = lax.rem(J, 2)
            elementwise(slot)
            scores(jnp.minimum(J + 1, last), 1 - slot)
            return outputs(jnp.maximum(J - 1, 0), 1 - slot, dq_acc)

        dq_acc = lax.fori_loop(0, i, step, jnp.zeros((R, 128), F32))
        dq_acc = outputs(last, lax.rem(i + 1, 2), dq_acc)
        pc, pe = pc_s[...], pe_s[...]
        for a in range(nsub):
            rows = slice(2 * a * Bq, 2 * (a + 1) * Bq)
            pca, pea, dqa = pc[rows], pe[rows], dq_acc[rows]
            for b in range(a + 1):
                off = pl.multiple_of(i * SB_KEYS + b * Bq, Bq)
                kb = k_ref[pl.ds(off, Bq), :]
                zb = lax.dot_general(qs[a], kb, _NT, preferred_element_type=F32)
                dAb = lax.dot_general(dos[a], v_ref[pl.ds(off, Bq), :], _NT, preferred_element_type=F32)
                A, dz, pca, pea = _sba_sub_bwd(zb, dAb, lts[a], pca, pea, Uincl1, Uexcl1, dmask if b == a else None)
                dqa = dqa + jnp.dot(dz, kb, preferred_element_type=F32)
                dk_acc[pl.ds(off, Bq), :] += lax.dot_general(dz, qs[a], _TN, preferred_element_type=F32)
                dv_acc[pl.ds(off, Bq), :] += lax.dot_general(A, dos[a], _TN, preferred_element_type=F32)
            dq_ref[a * Bq:(a + 1) * Bq, :] = (_unstack_heads(dqa) * scale).astype(BF16)

        @pl.when(i == nq - 1)
        def _():
            dk_ref[...] = dk_acc[...].astype(BF16)
            dv_ref[...] = dv_acc[...].astype(BF16)

    return pl.pallas_call(
        body, name=name, grid=(SB_HEADS // 2, nq),
        in_specs=[pl.BlockSpec((SB_KEYS, 128), lambda p, i: (i, p)), pl.BlockSpec((T, 128), lambda p, i: (0, p)),
                  pl.BlockSpec((T, 128), lambda p, i: (0, p + SB_HEADS // 2)),
                  pl.BlockSpec((None, SB_KEYS, 128), lambda p, i: (p, i, 0)),
                  pl.BlockSpec((SB_KEYS, 128), lambda p, i: (i, p))],
        out_specs=[pl.BlockSpec((SB_KEYS, 128), lambda p, i: (i, p)), pl.BlockSpec((T, 128), lambda p, i: (0, p)),
                   pl.BlockSpec((T, 128), lambda p, i: (0, p))],
        out_shape=[jax.ShapeDtypeStruct((T, D_MODEL), BF16), jax.ShapeDtypeStruct((T, D_MODEL), BF16),
                   jax.ShapeDtypeStruct((T, D_MODEL), BF16)],
        scratch_shapes=[pltpu.VMEM((T, 128), F32), pltpu.VMEM((T, 128), F32),
                        pltpu.VMEM((2, 2 * SB_KEYS, SB_KEYS), F32), pltpu.VMEM((2, 2 * SB_KEYS, SB_KEYS), F32),
                        pltpu.VMEM((2, 2 * SB_KEYS, SB_KEYS), BF16), pltpu.VMEM((2, 2 * SB_KEYS, SB_KEYS), BF16),
                        pltpu.VMEM((2 * SB_KEYS, 1), F32), pltpu.VMEM((2 * SB_KEYS, 1), F32),
                        pltpu.VMEM((2 * SB_KEYS, 1), F32)],
        compiler_params=_cparams(("parallel", "arbitrary")))(q, kv, kv, lt, do)


def _loss_head(h, tgt, w, *, name, tt=512):
    T, D = h.shape
    tt = min(tt, T)

    def body(h_ref, t_ref, w_ref, loss_ref, dh_ref, dw_ref):
        i = pl.program_id(0)
        hv = h_ref[...]
        wv = w_ref[...]
        r = lax.rsqrt(jnp.mean(hv * hv, axis=-1, keepdims=True) + EPS)
        xhat = hv * r
        err = xhat * wv - t_ref[...]
        part = 0.5 * jnp.sum(jnp.mean(err * err, axis=-1, keepdims=True), axis=0, keepdims=True)
        dy = err * (1.0 / D)
        dxh = dy * wv
        dh_ref[...] = r * (dxh - xhat * jnp.mean(dxh * xhat, axis=-1, keepdims=True))
        dwc = jnp.sum(dy * xhat, axis=0, keepdims=True)

        @pl.when(i == 0)
        def _():
            loss_ref[...] = jnp.broadcast_to(part, loss_ref.shape)
            dw_ref[...] = dwc

        @pl.when(i > 0)
        def _():
            loss_ref[...] += jnp.broadcast_to(part, loss_ref.shape)
            dw_ref[...] += dwc

    return pl.pallas_call(
        body, name=name, grid=(T // tt,),
        in_specs=[pl.BlockSpec((tt, D), lambda i: (i, 0)), pl.BlockSpec((tt, D), lambda i: (i, 0)),
                  pl.BlockSpec((1, D), lambda i: (0, 0))],
        out_specs=[pl.BlockSpec((1, 128), lambda i: (0, 0)), pl.BlockSpec((tt, D), lambda i: (i, 0)),
                   pl.BlockSpec((1, D), lambda i: (0, 0))],
        out_shape=[jax.ShapeDtypeStruct((1, 128), F32), jax.ShapeDtypeStruct((T, D), F32),
                   jax.ShapeDtypeStruct((1, D), F32)],
        compiler_params=_cparams(("arbitrary",)))(h, tgt, w.reshape(1, D))


def _adamw(parts, w, m, v, *, name, tr=256):
    plist = list(parts) if isinstance(parts, (list, tuple)) else [parts]
    P, _, C = plist[0].shape
    R = sum(a.shape[1] for a in plist)
    tr = min(tr, R)
    assert all(a.shape[1] % tr == 0 for a in plist), (name, R, tr)
    nbs = [a.shape[1] // tr for a in plist]
    offs = [sum(nbs[:l]) for l in range(len(nbs))]
    c1 = 1.0 - ADAM_B1 ** ADAM_STEP
    c2 = 1.0 - ADAM_B2 ** ADAM_STEP

    def body(*refs):
        p_refs = refs[:len(plist)]
        w_ref, m_ref, v_ref, g_ref, d_ref, nm_ref, nv_ref = refs[len(plist):]
        i = pl.program_id(0)
        g = None
        for l, p_ref in enumerate(p_refs):
            gl = p_ref[0].astype(F32)
            for k in range(1, P):
                gl = gl + p_ref[k].astype(F32)
            g = gl if g is None else jnp.where(i >= offs[l], gl, g)
        mn = ADAM_B1 * m_ref[...] + (1.0 - ADAM_B1) * g
        vn = ADAM_B2 * v_ref[...] + (1.0 - ADAM_B2) * (g * g)
        g_ref[...] = g
        nm_ref[...] = mn
        nv_ref[...] = vn
        d_ref[...] = -ADAM_LR * ((mn / c1) / (jnp.sqrt(vn / c2) + ADAM_EPS) + ADAM_WD * w_ref[...])

    spec = pl.BlockSpec((tr, C), lambda i: (i, 0))
    sds = jax.ShapeDtypeStruct((R, C), F32)
    return pl.pallas_call(
        body, name=name, grid=(R // tr,),
        in_specs=[pl.BlockSpec((P, tr, C), functools.partial(lambda i, o, n: (0, jnp.clip(i - o, 0, n - 1), 0), o=o, n=n))
                  for o, n in zip(offs, nbs)] + [spec, spec, spec],
        out_specs=[spec, spec, spec, spec], out_shape=[sds, sds, sds, sds],
        compiler_params=_cparams(("parallel",)))(*plist, w, m, v)


def _all_gather(shards, *, name):
    n = len(shards)

    def body(*refs):
        ins, outs = refs[:n], refs[n:2 * n]
        send_sems, recv_sems, local_sems = refs[2 * n:]
        x, y, c = lax.axis_index("x"), lax.axis_index("y"), lax.axis_index("c")
        me, sib = (x, y, c), (x, y, 1 - c)
        chips = [(1 - x, y), (x, 1 - y), (1 - x, 1 - y)]

        def slot(p):
            return 4 * p[0] + 2 * p[1] + p[2]

        def cp(a, k, block, to, src=None):
            dst = outs[a].at[slot(block)]
            return pltpu.make_async_remote_copy(src_ref=dst if src is None else src, dst_ref=dst,
                                                send_sem=send_sems.at[a, k], recv_sem=recv_sems.at[a, k],
                                                device_id=to, device_id_type=_MESH)

        mine = [pltpu.make_async_copy(ins[a], outs[a].at[slot(me)], local_sems.at[a]) for a in range(n)]
        for m in mine:
            m.start()
        first = []
        for a in range(n):
            first.append(cp(a, 0, me, sib, src=ins[a]))
            for j, chip in enumerate(chips):
                first.append(cp(a, 1 + j, me, (*chip, c), src=ins[a]))
        for f in first:
            f.start()
        passed = []
        for j, chip in enumerate(chips):
            for a in range(n):
                cp(a, 1 + j, (*chip, c), me).wait_recv()
                f = cp(a, 4 + j, (*chip, c), sib)
                f.start()
                passed.append(f)
        for a in range(n):
            cp(a, 0, sib, me).wait_recv()
            for j, chip in enumerate(chips):
                cp(a, 4 + j, (*chip, 1 - c), me).wait_recv()
        for f in first + passed:
            f.wait_send()
        for m in mine:
            m.wait()

    return pl.pallas_call(
        body, name=name, in_specs=[_ANY] * n, out_specs=[_ANY] * n,
        out_shape=[jax.ShapeDtypeStruct((N_DEV,) + s.shape, s.dtype) for s in shards],
        scratch_shapes=[pltpu.SemaphoreType.DMA((n, 7)), pltpu.SemaphoreType.DMA((n, 7)),
                        pltpu.SemaphoreType.DMA((n,))])(*shards)


def _exchange(blocks, *, name):
    n = len(blocks)

    def body(*refs):
        ins, outs = refs[:n], refs[n:2 * n]
        send_sems, recv_sems, local_sems = refs[2 * n:]
        x, y, c = lax.axis_index("x"), lax.axis_index("y"), lax.axis_index("c")
        me = 4 * x + 2 * y + c
        mine = [pltpu.make_async_copy(ins[a].at[me], outs[a].at[me], local_sems.at[a]) for a in range(n)]
        for m in mine:
            m.start()
        copies = []
        for r in range(1, N_DEV):
            rx, ry, rc = (r >> 2) & 1, (r >> 1) & 1, r & 1
            px, py, pc = (1 - x if rx else x), (1 - y if ry else y), (1 - c if rc else c)
            peer = 4 * px + 2 * py + pc
            for a in range(n):
                copies.append((pltpu.make_async_remote_copy(
                    src_ref=ins[a].at[peer], dst_ref=outs[a].at[me], send_sem=send_sems.at[a, r - 1],
                    recv_sem=recv_sems.at[a, r - 1], device_id=(px, py, pc), device_id_type=_MESH),
                    pltpu.make_async_remote_copy(
                    src_ref=ins[a].at[peer], dst_ref=outs[a].at[peer], send_sem=send_sems.at[a, r - 1],
                    recv_sem=recv_sems.at[a, r - 1], device_id=(px, py, pc), device_id_type=_MESH)))
        for snd, _ in copies:
            snd.start()
        for _, rcv in copies:
            rcv.wait_recv()
        for snd, _ in copies:
            snd.wait_send()
        for m in mine:
            m.wait()

    return pl.pallas_call(
        body, name=name, in_specs=[_ANY] * n, out_specs=[_ANY] * n,
        out_shape=[jax.ShapeDtypeStruct(b.shape, b.dtype) for b in blocks],
        scratch_shapes=[pltpu.SemaphoreType.DMA((n, 7)), pltpu.SemaphoreType.DMA((n, 7)),
                        pltpu.SemaphoreType.DMA((n,))])(*blocks)


_HBM = pl.BlockSpec(memory_space=pltpu.HBM)
_SEM = pl.BlockSpec(memory_space=pltpu.SEMAPHORE)
_EFFECT = pltpu.SideEffectType.DATAFLOW_SIDE_EFFECTING


def _peers():
    x, y, c = lax.axis_index("x"), lax.axis_index("y"), lax.axis_index("c")
    out = []
    for r in range(1, N_DEV):
        px = 1 - x if (r >> 2) & 1 else x
        py = 1 - y if (r >> 1) & 1 else y
        pc = 1 - c if r & 1 else c
        out.append(((px, py, pc), 4 * px + 2 * py + pc))
    return 4 * x + 2 * y + c, out


def _push_copy(src_ref, land_ref, send_sems, recv_sems, a, k, me, peer, peer_slot, scatter, arriving):
    src = src_ref.at[peer_slot] if scatter else src_ref
    return pltpu.make_async_remote_copy(
        src_ref=src, dst_ref=land_ref.at[peer_slot if arriving else me], send_sem=send_sems.at[a * (N_DEV - 1) + k],
        recv_sem=recv_sems.at[a * (N_DEV - 1) + k], device_id=peer, device_id_type=_MESH)


def _push_start(srcs, *, scatter, name):
    n = len(srcs)
    lands = [lax.empty(s.shape if scatter else (N_DEV,) + s.shape, s.dtype) for s in srcs]

    def body(*refs):
        src_refs, land_refs = refs[:n], refs[n:2 * n]
        send_sems, recv_sems = refs[2 * n], refs[2 * n + 1]
        token = refs[-1]
        me, peers = _peers()
        for k, (peer, slot) in enumerate(peers):
            for a in range(n):
                _push_copy(src_refs[a], land_refs[a], send_sems, recv_sems, a, k, me, peer, slot, scatter, False).start()
        token[...] = jnp.zeros_like(token)

    hbm = lambda a: pltpu.HBM(a.shape, a.dtype)
    outs = pl.pallas_call(
        body, name=name,
        out_shape=(pltpu.SemaphoreType.DMA((n * (N_DEV - 1),)), pltpu.SemaphoreType.DMA((n * (N_DEV - 1),)),
                   *[hbm(s) for s in srcs], *[hbm(l) for l in lands], jax.ShapeDtypeStruct((8, 128), F32)),
        in_specs=[_HBM] * (2 * n),
        out_specs=(_SEM, _SEM, *([_HBM] * (2 * n)), pl.BlockSpec(memory_space=pltpu.VMEM)),
        input_output_aliases={i: 2 + i for i in range(2 * n)},
        compiler_params=pltpu.CompilerParams(has_side_effects=_EFFECT),
    )(*[pltpu.with_memory_space_constraint(s, pltpu.HBM) for s in srcs],
      *[pltpu.with_memory_space_constraint(l, pltpu.HBM) for l in lands])
    return dict(send=outs[0], recv=outs[1], srcs=list(outs[2:2 + n]), lands=list(outs[2 + n:2 + 2 * n]),
                token=outs[-1], scatter=scatter, n=n)


def _push_wait(h, after, *, name):
    n, scatter = h["n"], h["scatter"]

    def body(*refs):
        src_refs, land_refs = refs[:n], refs[n:2 * n]
        send_sems, recv_sems = refs[2 * n], refs[2 * n + 1]
        me, peers = _peers()
        for k, (peer, slot) in enumerate(peers):
            for a in range(n):
                cp = _push_copy(src_refs[a], land_refs[a], send_sems, recv_sems, a, k, me, peer, slot, scatter, True)
                cp.wait_send()
                cp.wait_recv()

    hbm = lambda a: pltpu.HBM(a.shape, a.dtype)
    outs = pl.pallas_call(
        body, name=name,
        out_shape=(*[hbm(s) for s in h["srcs"]], *[hbm(l) for l in h["lands"]]),
        in_specs=[_HBM] * (2 * n) + [_SEM, _SEM, _ANY], out_specs=tuple([_HBM] * (2 * n)),
        input_output_aliases={i: i for i in range(2 * n)},
        compiler_params=pltpu.CompilerParams(has_side_effects=_EFFECT),
    )(*h["srcs"], *h["lands"], h["send"], h["recv"], after)
    return list(outs[:n]), list(outs[n:])


def _ffn_fwd(h, nw, w_up, conv_w, conv_b, w_down, tag):
    a3 = _mm_fwd(h, w_up, norm_w=nw, name=f"ffn{tag}_up", out_dtype=BF16, halves=True, tm=1024, tn=2816)
    p = _ffn_conv_fwd3(a3, conv_w, conv_b.reshape(1, -1), name=f"ffn{tag}_conv")
    h_out = _mm_fwd(p, w_down, residual=h, name=f"ffn{tag}_down", tm=1024, tn=512)
    return h_out, (a3, p)


def _ffn_bwd(dh, h, saved, nw, w_up, conv_w, conv_b, w_down, tag):
    a3, p = saved
    g_down = _mm_tn(p, dh, name=f"ffn{tag}_down_wg", tk1=1408, tn=1024)
    dp = _mm_nt(dh, w_down, name=f"ffn{tag}_down_dg", out_dtype=BF16, tm=512, tn=2816, tk=1024)
    dhid3, dw3, db3 = _ffn_conv_bwd3(a3, conv_w, conv_b.reshape(1, -1), dp, name=f"ffn{tag}_conv_bwd")
    da3 = _conv_bwd_in3(dhid3, conv_w, K=FFN_CONV, name=f"ffn{tag}_conv_bwd_in")
    g_up = _mm_tn(h, da3, norm_w=nw, name=f"ffn{tag}_up_wg", tn=2816, tt=1024)
    dh_out, g_nw = _mm_nt(da3, w_up, epi=(h, nw, dh), name=f"ffn{tag}_up_dg", tm=1024, tk=1408)
    g_cw = jnp.concatenate([dw3[0], dw3[1]], axis=1)
    g_cb = jnp.concatenate([db3[0], db3[1]], axis=1)
    return dh_out, dict(norm=g_nw.reshape(-1), up=g_up, conv_w=g_cw, conv_b=g_cb.reshape(-1), down=g_down)


def _local_step(x, tgt, W):
    T = x.shape[0]
    f = {}
    zx = _mm_fwd(x, W["in_w"], norm_w=W["ssm_norm_w"], name="ssm_in", tm=1024, tn=896)
    xbc_c = _ssm_conv_fwd(zx, W["ssm_conv_w"], W["ssm_conv_b"].reshape(1, -1), name="ssm_conv")
    dt_raw = zx[:, D_INNER + CONV_DIM:IN_PROJ_DIM]
    dtg = jnp.pad(dt_raw.reshape(T, SSM_GROUPS, 8).transpose(1, 0, 2), ((0, 0), (0, 0), (0, 120)))
    par = jnp.stack([W["ssm_dt_bias"].reshape(SSM_GROUPS, 8), W["ssm_a_log"].reshape(SSM_GROUPS, 8),
                     W["ssm_d"].reshape(SSM_GROUPS, 8)], axis=1)
    par = jnp.pad(par, ((0, 0), (0, 5), (0, 120)))
    gnw = W["ssm_gate_norm_w"].reshape(1, D_INNER)
    y, yn, st = _ssd_fwd(xbc_c, zx, dtg, par, gnw, name="ssd_fwd")
    h1 = _mm_fwd(yn, W["ssm_out_w"], residual=x, name="ssm_out", tm=1024, tn=512)
    h2, ffn0 = _ffn_fwd(h1, W["ffn_norm_w"][0], W["ffn_up_w"][0], W["ffn_conv_w"][0], W["ffn_conv_b"][0],
                        W["ffn_down_w"][0], "0")
    q = _mm_fwd(h2, W["w_q"], norm_w=W["attn_norm_w"], out_dtype=BF16, name="attn_q", tm=1024, tn=1024)
    kv = _mm_fwd(h2, W["w_kv"], norm_w=W["kv_norm_w"], out_dtype=BF16, name="attn_kv", tm=1024, tn=1024)
    o, lt = _sba_fwd(q, kv, name="sba_fwd")
    h3 = _mm_fwd(o, W["w_o"], residual=h2, name="attn_o", tm=1024, tn=512)
    h4, ffn1 = _ffn_fwd(h3, W["ffn_norm_w"][1], W["ffn_up_w"][1], W["ffn_conv_w"][1], W["ffn_conv_b"][1],
                        W["ffn_down_w"][1], "1")
    loss, dh4, g_final = _loss_head(h4, tgt, W["final_norm_w"], name="loss_head")
    dh3, gf1 = _ffn_bwd(dh4, h3, ffn1, W["ffn_norm_w"][1], W["ffn_up_w"][1], W["ffn_conv_w"][1], W["ffn_conv_b"][1],
                        W["ffn_down_w"][1], "1")
    g_wo = _mm_tn(o, dh3, name="attn_o_wg", tn=1024)
    do = _mm_nt(dh3, W["w_o"], name="attn_o_dg", out_dtype=BF16, tn=1024, tk=1024)
    dq, dk, dv = _sba_bwd(q, kv, lt, do, name="sba_bwd")
    g_wq = _mm_tn(h2, dq, norm_w=W["attn_norm_w"], name="attn_q_wg", tn=1024)
    dh2a, g_attn_nw = _mm_nt(dq, W["w_q"], epi=(h2, W["attn_norm_w"], dh3), name="attn_q_dg", tk=1024)
    dkv = jnp.concatenate([dk, dv], axis=1)
    g_wkv = _mm_tn(h2, dkv, norm_w=W["kv_norm_w"], name="attn_kv_wg", tn=1024)
    dh2, g_kv_nw = _mm_nt(dkv, W["w_kv"], epi=(h2, W["kv_norm_w"], dh2a), name="attn_kv_dg", tk=1024)
    dh1, gf0 = _ffn_bwd(dh2, h1, ffn0, W["ffn_norm_w"][0], W["ffn_up_w"][0], W["ffn_conv_w"][0], W["ffn_conv_b"][0],
                        W["ffn_down_w"][0], "0")
    g_out = _mm_tn(yn, dh1, name="ssm_out_wg", tn=1024)
    dyn = _mm_nt(dh1, W["ssm_out_w"], name="ssm_out_dg", out_dtype=BF16, tn=1024, tk=1024)
    dxs, dB, dC, dz, ddt, g_gnw, dpar = _ssd_bwd(xbc_c, zx, dtg, par, gnw, y, st, dyn, name="ssd_bwd")
    dxbc_c = jnp.concatenate([dxs, dB, dC], axis=1)
    dhid, g_scw, g_scb = _ssm_conv_bwd_pre(zx, W["ssm_conv_w"], W["ssm_conv_b"].reshape(1, -1), dxbc_c,
                                           name="ssm_conv_bwd")
    dxbc = _conv_bwd_in(dhid, W["ssm_conv_w"], K=SSM_CONV, name="ssm_conv_bwd_in")
    ddt_t = ddt[:, :, :8].transpose(1, 0, 2).reshape(T, SSM_HEADS).astype(BF16)
    dzx = jnp.concatenate([dz, dxbc, jnp.pad(ddt_t, ((0, 0), (0, IN_PROJ_PAD - IN_PROJ_DIM)))], axis=1)
    g_in = _mm_tn(x, dzx, norm_w=W["ssm_norm_w"], name="ssm_in_wg", tn=896)
    dx, g_ssm_nw = _mm_nt(dzx, W["in_w"], epi=(x, W["ssm_norm_w"], dh1), name="ssm_in_dg", tk=1792)
    f["ssm_norm_w"] = g_ssm_nw.reshape(-1)
    f["ssm_in_w"] = g_in[:, :IN_PROJ_DIM]
    f["ssm_conv_w"] = g_scw
    f["ssm_conv_b"] = g_scb.reshape(-1)
    f["ssm_dt_bias"] = dpar[:, 0, :8].reshape(-1)
    f["ssm_a_log"] = dpar[:, 1, :8].reshape(-1)
    f["ssm_d"] = dpar[:, 2, :8].reshape(-1)
    f["ssm_gate_norm_w"] = g_gnw.reshape(-1)
    f["ssm_out_w"] = g_out
    f["kv_norm_w"] = g_kv_nw.reshape(-1)
    f["w_k"] = g_wkv[:, :D_MODEL]
    f["w_v"] = g_wkv[:, D_MODEL:]
    f["attn_norm_w"] = g_attn_nw.reshape(-1)
    f["w_q"] = g_wq
    f["w_o"] = g_wo
    f["ffn_norm_w"] = jnp.stack([gf0["norm"], gf1["norm"]])
    f["ffn_up_w"] = [gf0["up"], gf1["up"]]
    f["ffn_conv_w"] = jnp.stack([gf0["conv_w"], gf1["conv_w"]])
    f["ffn_conv_b"] = jnp.stack([gf0["conv_b"], gf1["conv_b"]])
    f["ffn_down_w"] = [gf0["down"], gf1["down"]]
    f["final_norm_w"] = g_final.reshape(-1)
    return loss, dx, f


_BIG = ["ssm_in_w", "ssm_out_w", "w_k", "w_v", "w_q", "w_o", "ffn_up_w", "ffn_down_w"]
_SMALL_SHARDED = ["ssm_norm_w", "ssm_conv_w", "ssm_conv_b", "ssm_gate_norm_w", "ffn_conv_w"]
_SMALL_REPL = ["ssm_dt_bias", "ssm_a_log", "ssm_d", "kv_norm_w", "attn_norm_w", "ffn_norm_w", "ffn_conv_b",
               "final_norm_w"]
_WEIGHTS = ["ssm_norm_w", "ssm_in_w", "ssm_conv_w", "ssm_conv_b", "ssm_dt_bias", "ssm_a_log", "ssm_d",
            "ssm_gate_norm_w", "ssm_out_w", "kv_norm_w", "w_k", "w_v", "attn_norm_w", "w_q", "w_o", "ffn_norm_w",
            "ffn_up_w", "ffn_conv_w", "ffn_conv_b", "ffn_down_w", "final_norm_w"]


def _as2d(a):
    return a.reshape(-1, a.shape[-1])


def _cols_to_full(g):
    return g.transpose(1, 0, 2).reshape(g.shape[1], N_DEV * g.shape[2])


def _full_to_cols(a):
    R = a.shape[0]
    return a.reshape(R, N_DEV, -1).transpose(1, 0, 2)


def _gather_weights(p):
    names = _BIG + _SMALL_SHARDED
    shards = [_as2d(p[n]).astype(BF16) for n in _BIG] + [_as2d(p[n]) for n in _SMALL_SHARDED]
    got = dict(zip(names, _all_gather(shards, name="gather_weights")))
    W = {n: p[n] for n in _SMALL_REPL}
    in_w = _cols_to_full(got["ssm_in_w"])
    W["in_w"] = jnp.pad(in_w, ((0, 0), (0, IN_PROJ_PAD - IN_PROJ_DIM)))
    W["ssm_out_w"] = got["ssm_out_w"].reshape(D_INNER, D_MODEL)
    W["w_kv"] = jnp.concatenate([got["w_k"].reshape(D_MODEL, D_MODEL), got["w_v"].reshape(D_MODEL, D_MODEL)], axis=1)
    W["w_q"] = got["w_q"].reshape(D_MODEL, D_MODEL)
    W["w_o"] = got["w_o"].reshape(D_MODEL, D_MODEL)
    up = got["ffn_up_w"]
    W["ffn_up_w"] = [_cols_to_full(up[:, l * D_MODEL:(l + 1) * D_MODEL]) for l in range(2)]
    dn = got["ffn_down_w"]
    rs = D_FF // N_DEV
    W["ffn_down_w"] = [dn[:, l * rs:(l + 1) * rs].reshape(D_FF, D_MODEL) for l in range(2)]
    W["ssm_norm_w"] = got["ssm_norm_w"].reshape(D_MODEL)
    W["ssm_conv_w"] = _cols_to_full(got["ssm_conv_w"])
    W["ssm_conv_b"] = got["ssm_conv_b"].reshape(CONV_DIM)
    W["ssm_gate_norm_w"] = got["ssm_gate_norm_w"].reshape(D_INNER)
    fcw = _cols_to_full(got["ffn_conv_w"])
    W["ffn_conv_w"] = fcw.reshape(2, FFN_CONV, 2 * D_FF)
    for n in ("ssm_dt_bias", "ssm_a_log", "ssm_d", "attn_norm_w"):
        W[n] = W[n].reshape(-1)
    return W


def _big_grad_blocks(f):
    rs = D_FF // N_DEV
    return {
        "ssm_in_w": _full_to_cols(f["ssm_in_w"]),
        "ssm_out_w": f["ssm_out_w"].reshape(N_DEV, D_INNER // N_DEV, D_MODEL),
        "w_k": f["w_k"].reshape(N_DEV, D_MODEL // N_DEV, D_MODEL),
        "w_v": f["w_v"].reshape(N_DEV, D_MODEL // N_DEV, D_MODEL),
        "w_q": f["w_q"].reshape(N_DEV, D_MODEL // N_DEV, D_MODEL),
        "w_o": f["w_o"].reshape(N_DEV, D_MODEL // N_DEV, D_MODEL),
        "ffn_up_w": jnp.concatenate([_full_to_cols(g) for g in f["ffn_up_w"]], axis=1),
        "ffn_down_w": jnp.concatenate([g.reshape(N_DEV, rs, D_MODEL) for g in f["ffn_down_w"]], axis=1),
    }


def _pack_small(vals):
    flat = jnp.concatenate([v.reshape(-1).astype(F32) for v in vals])
    n = flat.shape[0]
    rows = -(-n // 1024) * 8
    return jnp.pad(flat, (0, rows * 128 - n)).reshape(rows, 128)


def _unpack_small(packed, shapes):
    flat = packed.reshape(-1)
    out, off = [], 0
    for s in shapes:
        n = math.prod(s)
        out.append(flat[off:off + n].reshape(s))
        off += n
    return out


def _kernel_v1(x, ssm_norm_w, ssm_in_w, ssm_conv_w, ssm_conv_b, ssm_dt_bias, ssm_a_log, ssm_d, ssm_gate_norm_w, ssm_out_w, kv_norm_w, w_k, w_v, attn_norm_w, w_q, w_o, ffn_norm_w, ffn_up_w, ffn_conv_w, ffn_conv_b, ffn_down_w, final_norm_w, loss_target, m_ssm_norm_w, m_ssm_in_w, m_ssm_conv_w, m_ssm_conv_b, m_ssm_dt_bias, m_ssm_a_log, m_ssm_d, m_ssm_gate_norm_w, m_ssm_out_w, m_kv_norm_w, m_w_k, m_w_v, m_attn_norm_w, m_w_q, m_w_o, m_ffn_norm_w, m_ffn_up_w, m_ffn_conv_w, m_ffn_conv_b, m_ffn_down_w, m_final_norm_w, v_ssm_norm_w, v_ssm_in_w, v_ssm_conv_w, v_ssm_conv_b, v_ssm_dt_bias, v_ssm_a_log, v_ssm_d, v_ssm_gate_norm_w, v_ssm_out_w, v_kv_norm_w, v_w_k, v_w_v, v_attn_norm_w, v_w_q, v_w_o, v_ffn_norm_w, v_ffn_up_w, v_ffn_conv_w, v_ffn_conv_b, v_ffn_down_w, v_final_norm_w):
    env = dict(locals())
    p = {n: env[n] for n in _WEIGHTS}
    mom = {n: env["m_" + n] for n in _WEIGHTS}
    var = {n: env["v_" + n] for n in _WEIGHTS}
    T = x.shape[1]
    me = 4 * lax.axis_index("x") + 2 * lax.axis_index("y") + lax.axis_index("c")

    W = _gather_weights(p)
    loss_row, dx, f = _local_step(x.reshape(T, D_MODEL), loss_target.reshape(T, D_MODEL), W)
    loss = lax.psum(loss_row[0, 0], ("x", "y", "c"))

    big = _big_grad_blocks(f)
    small_names = _SMALL_REPL + _SMALL_SHARDED
    small_full = _pack_small([f[n] for n in small_names])
    small_bcast = jnp.broadcast_to(small_full[None], (N_DEV,) + small_full.shape)
    got = _exchange([big[n] for n in _BIG] + [small_bcast], name="exchange_grads")
    big_parts = dict(zip(_BIG, got[:-1]))

    zero = jnp.zeros_like(small_full)
    g_small_sum = _adamw(got[-1], zero, zero, zero, name="sum_small_grads", tr=small_full.shape[0])[0]
    full_shapes = [f[n].shape for n in small_names]
    g_small = dict(zip(small_names, _unpack_small(g_small_sum, full_shapes)))
    for n in _SMALL_SHARDED:
        width = p[n].shape[-1]
        g_small[n] = lax.dynamic_slice_in_dim(g_small[n], me * width, width, axis=g_small[n].ndim - 1)

    out_g, out_d, out_m, out_v = {}, {}, {}, {}
    for n in _BIG:
        w2, m2, v2 = _as2d(p[n]), _as2d(mom[n]), _as2d(var[n])
        tr = 352 if n == "ffn_down_w" else 256
        g, d, nm, nv = _adamw(big_parts[n], w2, m2, v2, name="adamw_" + n, tr=tr)
        out_g[n], out_d[n], out_m[n], out_v[n] = (t.reshape(p[n].shape) for t in (g, d, nm, nv))
    sw = _pack_small([p[n] for n in small_names])
    sm = _pack_small([mom[n] for n in small_names])
    sv = _pack_small([var[n] for n in small_names])
    sg = _pack_small([g_small[n] for n in small_names])
    _, d, nm, nv = _adamw(sg[None], sw, sm, sv, name="adamw_small", tr=sw.shape[0])
    shard_shapes = [p[n].shape for n in small_names]
    for n, dd, mm, vv in zip(small_names, _unpack_small(d, shard_shapes), _unpack_small(nm, shard_shapes),
                             _unpack_small(nv, shard_shapes)):
        out_g[n] = g_small[n].reshape(p[n].shape)
        out_d[n], out_m[n], out_v[n] = dd, mm, vv

    return (loss, dx.reshape(x.shape), *[out_g[n] for n in _WEIGHTS], *[out_d[n] for n in _WEIGHTS],
            *[out_m[n] for n in _WEIGHTS], *[out_v[n] for n in _WEIGHTS])


def _tie(a, token):
    return a + token[0, 0].astype(a.dtype)


def _local_step2(x, tgt, get_w, put_g):
    T = x.shape[0]
    Ws = get_w("ssm", None)
    fnw, fcw, fcb = Ws["ffn_norm_w"], Ws["ffn_conv_w"], Ws["ffn_conv_b"]
    zx = _mm_fwd(x, Ws["in_w"], norm_w=Ws["ssm_norm_w"], name="ssm_in", tm=1024, tn=1792)
    xbc_c = _ssm_conv_fwd(zx, Ws["ssm_conv_w"], Ws["ssm_conv_b"].reshape(1, -1), name="ssm_conv")
    dt_raw = zx[:, D_INNER + CONV_DIM:IN_PROJ_DIM]
    dtg = jnp.pad(dt_raw.reshape(T, SSM_GROUPS, 8).transpose(1, 0, 2), ((0, 0), (0, 0), (0, 120)))
    par = jnp.stack([Ws["ssm_dt_bias"].reshape(SSM_GROUPS, 8), Ws["ssm_a_log"].reshape(SSM_GROUPS, 8),
                     Ws["ssm_d"].reshape(SSM_GROUPS, 8)], axis=1)
    par = jnp.pad(par, ((0, 0), (0, 5), (0, 120)))
    gnw = _tie(Ws["ssm_gate_norm_w"].reshape(1, D_INNER), get_w("rest_start", xbc_c))
    y, yn, st = _ssd_fwd(xbc_c, zx, dtg, par, gnw, name="ssd_fwd")
    W0 = get_w("ffn0", y)
    Ws["ssm_out_w"] = W0["ssm_out_w"]
    h1 = _mm_fwd(yn, Ws["ssm_out_w"], residual=x, name="ssm_out", tm=1024, tn=512)
    h2, ffn0 = _ffn_fwd(h1, fnw[0], W0["up"], fcw[0], fcb[0], W0["down"], "0")
    Wr = get_w("rest", h2)
    q = _mm_fwd(h2, Wr["w_q"], norm_w=Ws["attn_norm_w"], out_dtype=BF16, name="attn_q", tm=1024, tn=1024)
    kv = _mm_fwd(h2, Wr["w_kv"], norm_w=Ws["kv_norm_w"], out_dtype=BF16, name="attn_kv", tm=1024, tn=1024)
    o, lt = _sba_fwd(q, kv, name="sba_fwd")
    h3 = _mm_fwd(o, Wr["w_o"], residual=h2, name="attn_o", tm=1024, tn=512)
    h4, ffn1 = _ffn_fwd(h3, fnw[1], Wr["up"], fcw[1], fcb[1], Wr["down"], "1")
    loss, dh4, g_final = _loss_head(h4, tgt, Ws["final_norm_w"], name="loss_head")
    dh3, gf1 = _ffn_bwd(dh4, h3, ffn1, fnw[1], Wr["up"], fcw[1], fcb[1], Wr["down"], "1")
    tok = put_g("ffn1", dict(up=gf1["up"], down=gf1["down"]))
    g_wo = _mm_tn(o, dh3, name="attn_o_wg", tn=1024)
    do = _mm_nt(dh3, _tie(Wr["w_o"], tok), name="attn_o_dg", out_dtype=BF16, tn=1024, tk=1024)
    dq, dk, dv = _sba_bwd(q, kv, lt, do, name="sba_bwd")
    g_wq = _mm_tn(h2, dq, norm_w=Ws["attn_norm_w"], name="attn_q_wg", tn=1024, tt=1024)
    dh2a, g_attn_nw = _mm_nt(dq, Wr["w_q"], epi=(h2, Ws["attn_norm_w"], dh3), name="attn_q_dg", tm=1024, tk=1024)
    dkv = jnp.concatenate([dk, dv], axis=1)
    g_wkv = _mm_tn(h2, dkv, norm_w=Ws["kv_norm_w"], name="attn_kv_wg", tn=1024, tt=1024)
    dh2, g_kv_nw = _mm_nt(dkv, Wr["w_kv"], epi=(h2, Ws["kv_norm_w"], dh2a), name="attn_kv_dg", tm=1024, tk=1024)
    tok = put_g("attn", dict(w_o=g_wo, w_q=g_wq, w_k=g_wkv[:, :D_MODEL], w_v=g_wkv[:, D_MODEL:]))
    dh1, gf0 = _ffn_bwd(dh2, h1, ffn0, fnw[0], W0["up"], fcw[0], _tie(fcb[0], tok), W0["down"], "0")
    tok = put_g("ffn0", dict(up=gf0["up"], down=gf0["down"]))
    g_out = _mm_tn(yn, dh1, name="ssm_out_wg", tn=1024)
    dyn = _mm_nt(dh1, _tie(Ws["ssm_out_w"], tok), name="ssm_out_dg", out_dtype=BF16, tn=1024, tk=1024)
    tok = put_g("ssm_out", dict(ssm_out_w=g_out))
    dxbc_c, dz, ddt, g_gnw, dpar = _ssd_bwd(xbc_c, zx, dtg, par, _tie(gnw, tok), y, st, dyn, name="ssd_bwd")
    dhid, g_scw, g_scb = _ssm_conv_bwd_pre(zx, Ws["ssm_conv_w"], Ws["ssm_conv_b"].reshape(1, -1), dxbc_c,
                                           name="ssm_conv_bwd")
    dzx = _conv_bwd_in(dhid, Ws["ssm_conv_w"], K=SSM_CONV, name="ssm_conv_bwd_in", into=(dz, D_INNER))
    ddt_t = ddt[:, :, :8].transpose(1, 0, 2).reshape(T, SSM_HEADS).astype(BF16)
    dzx = _put_cols(dzx, jnp.pad(ddt_t, ((0, 0), (0, IN_PROJ_PAD - IN_PROJ_DIM))), D_INNER + CONV_DIM, name="ssm_ddt_cols")
    g_in = _mm_tn(x, dzx, norm_w=Ws["ssm_norm_w"], name="ssm_in_wg", tn=1792, tt=1024)
    tok = put_g("ssm_in", dict(ssm_in_w=g_in[:, :IN_PROJ_DIM]))
    dx, g_ssm_nw = _mm_nt(dzx, Ws["in_w"], epi=(x, _tie(Ws["ssm_norm_w"], tok), dh1), name="ssm_in_dg", tm=1024, tk=1792)
    f = {
        "ssm_norm_w": g_ssm_nw.reshape(-1), "ssm_conv_w": g_scw,
        "ssm_conv_b": g_scb.reshape(-1), "ssm_dt_bias": dpar[:, 0, :8].reshape(-1),
        "ssm_a_log": dpar[:, 1, :8].reshape(-1), "ssm_d": dpar[:, 2, :8].reshape(-1),
        "ssm_gate_norm_w": g_gnw.reshape(-1), "kv_norm_w": g_kv_nw.reshape(-1), "attn_norm_w": g_attn_nw.reshape(-1),
        "ffn_norm_w": jnp.stack([gf0["norm"], gf1["norm"]]), "ffn_conv_w": jnp.stack([gf0["conv_w"], gf1["conv_w"]]),
        "ffn_conv_b": jnp.stack([gf0["conv_b"], gf1["conv_b"]]), "final_norm_w": g_final.reshape(-1),
    }
    return loss, dx, f


def kernel(x, ssm_norm_w, ssm_in_w, ssm_conv_w, ssm_conv_b, ssm_dt_bias, ssm_a_log, ssm_d, ssm_gate_norm_w, ssm_out_w, kv_norm_w, w_k, w_v, attn_norm_w, w_q, w_o, ffn_norm_w, ffn_up_w, ffn_conv_w, ffn_conv_b, ffn_down_w, final_norm_w, loss_target, m_ssm_norm_w, m_ssm_in_w, m_ssm_conv_w, m_ssm_conv_b, m_ssm_dt_bias, m_ssm_a_log, m_ssm_d, m_ssm_gate_norm_w, m_ssm_out_w, m_kv_norm_w, m_w_k, m_w_v, m_attn_norm_w, m_w_q, m_w_o, m_ffn_norm_w, m_ffn_up_w, m_ffn_conv_w, m_ffn_conv_b, m_ffn_down_w, m_final_norm_w, v_ssm_norm_w, v_ssm_in_w, v_ssm_conv_w, v_ssm_conv_b, v_ssm_dt_bias, v_ssm_a_log, v_ssm_d, v_ssm_gate_norm_w, v_ssm_out_w, v_kv_norm_w, v_w_k, v_w_v, v_attn_norm_w, v_w_q, v_w_o, v_ffn_norm_w, v_ffn_up_w, v_ffn_conv_w, v_ffn_conv_b, v_ffn_down_w, v_final_norm_w):
    env = dict(locals())
    p = {n: env[n] for n in _WEIGHTS}
    mom = {n: env["m_" + n] for n in _WEIGHTS}
    var = {n: env["v_" + n] for n in _WEIGHTS}
    T = x.shape[1]
    me = 4 * lax.axis_index("x") + 2 * lax.axis_index("y") + lax.axis_index("c")
    rs = D_FF // N_DEV

    def bf2(a):
        return _as2d(a).astype(BF16)

    def with_own(srcs, lands, scatter):
        out = []
        for s, l in zip(srcs, lands):
            own = lax.dynamic_index_in_dim(s, me, 0, keepdims=False) if scatter else s
            out.append(lax.dynamic_update_index_in_dim(l, own, me, 0))
        return out

    a_names = ["ssm_in_w"] + _SMALL_SHARDED
    got_a = dict(zip(a_names, _all_gather([bf2(p["ssm_in_w"])] + [_as2d(p[n]) for n in _SMALL_SHARDED],
                                          name="gather_ssm")))
    ffn0_names = ["ssm_out_w", "up0", "down0"]
    rest_names = ["w_q", "w_k", "w_v", "w_o", "up1", "down1"]
    shard = {"up0": bf2(p["ffn_up_w"][0]), "down0": bf2(p["ffn_down_w"][0]), "up1": bf2(p["ffn_up_w"][1]),
             "down1": bf2(p["ffn_down_w"][1]), "w_q": bf2(p["w_q"]), "w_k": bf2(p["w_k"]), "w_v": bf2(p["w_v"]),
             "w_o": bf2(p["w_o"]), "ssm_out_w": bf2(p["ssm_out_w"])}
    h_ffn0 = _push_start([shard[n] for n in ffn0_names], scatter=False, name="gather_ffn0_start")
    handles = {}

    def get_w(group, after):
        if group == "ssm":
            W = {n: p[n] for n in _SMALL_REPL}
            for n in ("ssm_dt_bias", "ssm_a_log", "ssm_d", "attn_norm_w"):
                W[n] = W[n].reshape(-1)
            W["in_w"] = jnp.pad(_cols_to_full(got_a["ssm_in_w"]), ((0, 0), (0, IN_PROJ_PAD - IN_PROJ_DIM)))
            W["ssm_norm_w"] = _tie(got_a["ssm_norm_w"].reshape(D_MODEL), h_ffn0["token"])
            W["ssm_conv_w"] = _cols_to_full(got_a["ssm_conv_w"])
            W["ssm_conv_b"] = got_a["ssm_conv_b"].reshape(CONV_DIM)
            W["ssm_gate_norm_w"] = got_a["ssm_gate_norm_w"].reshape(D_INNER)
            W["ffn_conv_w"] = _cols_to_full(got_a["ffn_conv_w"]).reshape(2, FFN_CONV, 2 * D_FF)
            return W
        if group == "rest_start":
            anchor = after[0, 0]
            first = shard[rest_names[0]] + (jnp.where(jnp.isfinite(anchor), anchor, 0.0) * 0.0).astype(BF16)
            handles["rest"] = _push_start([first] + [shard[n] for n in rest_names[1:]], scatter=False,
                                          name="gather_rest_start")
            return handles["rest"]["token"]
        if group == "ffn0":
            srcs, lands = _push_wait(h_ffn0, after, name="gather_ffn0_wait")
            out, up, down = with_own(srcs, lands, False)
            return dict(ssm_out_w=out.reshape(D_INNER, D_MODEL), up=_cols_to_full(up), down=down.reshape(D_FF, D_MODEL))
        srcs, lands = _push_wait(handles["rest"], after, name="gather_rest_wait")
        g = dict(zip(rest_names, with_own(srcs, lands, False)))
        sq = lambda a: a.reshape(D_MODEL, D_MODEL)
        return dict(w_q=sq(g["w_q"]), w_kv=jnp.concatenate([sq(g["w_k"]), sq(g["w_v"])], axis=1), w_o=sq(g["w_o"]),
                    up=_cols_to_full(g["up1"]), down=g["down1"].reshape(D_FF, D_MODEL))

    pending = []

    def put_g(group, g):
        if group in ("ffn0", "ffn1"):
            keys = [("ffn_up_w", int(group[-1])), ("ffn_down_w", int(group[-1]))]
            blocks = [_full_to_cols(g["up"]), g["down"].reshape(N_DEV, rs, D_MODEL)]
        elif group == "attn":
            keys = [(n, None) for n in ("w_o", "w_q", "w_k", "w_v")]
            blocks = [g[n].reshape(N_DEV, D_MODEL // N_DEV, D_MODEL) for n, _ in keys]
        elif group == "ssm_out":
            keys = [("ssm_out_w", None)]
            blocks = [g["ssm_out_w"].reshape(N_DEV, D_INNER // N_DEV, D_MODEL)]
        else:
            keys = [("ssm_in_w", None)]
            blocks = [_full_to_cols(g["ssm_in_w"])]
        h = _push_start(blocks, scatter=True, name=f"exchange_{group}_start")
        pending.append((group, keys, h))
        return h["token"]

    loss_row, dx, f = _local_step2(x.reshape(T, D_MODEL), loss_target.reshape(T, D_MODEL), get_w, put_g)

    small_names = _SMALL_REPL + _SMALL_SHARDED
    small_full = _pack_small([f[n] for n in small_names] + [loss_row[0, 0:1]])
    small_bcast = jnp.broadcast_to(small_full[None], (N_DEV,) + small_full.shape)
    h_small = _push_start([small_bcast], scatter=True, name="exchange_small_start")
    tok = h_small["token"]

    arrived, res = {}, {}
    after = dx
    for group, keys, h in pending:
        srcs, lands = _push_wait(h, after, name=f"exchange_{group}_wait")
        arrived.update(zip(keys, with_own(srcs, lands, True)))
        for n in _BIG:
            layered = (n, 0) in arrived or (n, 1) in arrived
            if n in res or not ((n, None) in arrived or ((n, 0) in arrived and (n, 1) in arrived)):
                continue
            parts = [arrived[(n, 0)], arrived[(n, 1)]] if layered else arrived[(n, None)]
            w2, m2, v2 = _as2d(p[n]), _as2d(mom[n]), _as2d(var[n])
            if not res:
                w2 = _tie(w2, tok)
            res[n] = _adamw(parts, w2, m2, v2, name=f"adamw_{n}", tr=rs if n == "ffn_down_w" else 256)
            after = res[n][0]
    srcs, lands = _push_wait(h_small, after, name="exchange_small_wait")
    small_parts = with_own(srcs, lands, True)[0]
    out_g, out_d, out_m, out_v = {}, {}, {}, {}
    for n in _BIG:
        out_g[n], out_d[n], out_m[n], out_v[n] = (t.reshape(p[n].shape) for t in res[n])

    zero = jnp.zeros_like(small_full)
    g_small_sum = _adamw(small_parts, zero, zero, zero, name="sum_small_grads", tr=small_full.shape[0])[0]
    *small_sums, loss_sum = _unpack_small(g_small_sum, [f[n].shape for n in small_names] + [(1,)])
    loss = loss_sum[0]
    g_small = dict(zip(small_names, small_sums))
    for n in _SMALL_SHARDED:
        width = p[n].shape[-1]
        g_small[n] = lax.dynamic_slice_in_dim(g_small[n], me * width, width, axis=g_small[n].ndim - 1)
    sw = _pack_small([p[n] for n in small_names])
    sm = _pack_small([mom[n] for n in small_names])
    sv = _pack_small([var[n] for n in small_names])
    sg = _pack_small([g_small[n] for n in small_names])
    _, d, nm, nv = _adamw(sg[None], sw, sm, sv, name="adamw_small", tr=sw.shape[0])
    shard_shapes = [p[n].shape for n in small_names]
    for n, dd, mm, vv in zip(small_names, _unpack_small(d, shard_shapes), _unpack_small(nm, shard_shapes),
                             _unpack_small(nv, shard_shapes)):
        out_g[n] = g_small[n].reshape(p[n].shape)
        out_d[n], out_m[n], out_v[n] = dd, mm, vv

    return (loss, dx.reshape(x.shape), *[out_g[n] for n in _WEIGHTS], *[out_d[n] for n in _WEIGHTS],
            *[out_m[n] for n in _WEIGHTS], *[out_v[n] for n in _WEIGHTS])
```

```python
import functools
import math

import jax
import jax.numpy as jnp
from jax import lax
from jax.experimental import pallas as pl
from jax.experimental.pallas import tpu as pltpu

F32 = jnp.float32
BF16 = jnp.bfloat16
EPS = 1e-6

D_MODEL = 1024
D_INNER = 2048
SSM_HEADS = 32
SSM_GROUPS = 4
SSM_STATE = 128
SSM_CONV = 4
SSM_CHUNK = 128
GN = SSM_GROUPS * SSM_STATE
CONV_DIM = D_INNER + 2 * GN
IN_PROJ_DIM = D_INNER + CONV_DIM + SSM_HEADS
IN_PROJ_PAD = 5376
SB_HEADS = 16
SB_HEAD_DIM = 64
SB_BLOCK = 128
D_FF = 2816
FFN_CONV = 3
N_DEV = 8

ADAM_LR = 0.001
ADAM_B1 = 0.9
ADAM_B2 = 0.999
ADAM_EPS = 1e-08
ADAM_WD = 0.01
ADAM_STEP = 10

_MESH = pl.DeviceIdType.MESH
_NT = (((1,), (1,)), ((), ()))
_TN = (((0,), (0,)), ((), ()))
_ANY = pl.BlockSpec(memory_space=pl.ANY)


def _cparams(sem, vmem_mb=48):
    return pltpu.CompilerParams(dimension_semantics=sem, vmem_limit_bytes=vmem_mb * 1024 * 1024)


def _sigmoid(x):
    return 0.5 * jnp.tanh(0.5 * x) + 0.5


def _softplus(x):
    return jnp.maximum(x, 0.0) + jnp.log(1.0 + jnp.exp(-jnp.abs(x)))


def _rms_fwd(xv, w):
    r = lax.rsqrt(jnp.mean(xv * xv, axis=-1, keepdims=True) + EPS)
    return xv * r * w


def _mm_fwd(x, w, *, name, norm_w=None, residual=None, out_dtype=F32, tm=512, tn=512, halves=False):
    M, K = x.shape
    N = w.shape[1]
    tm, tn = min(tm, M), min(tn, N)
    assert M % tm == 0 and N % tn == 0, (name, M, N, tm, tn)
    if halves:
        nbh = N // 2 // tn
        assert N // 2 % tn == 0
        out_spec = pl.BlockSpec((None, tm, tn), lambda i, j: (lax.div(j, nbh), i, lax.rem(j, nbh)))
        out_shape = jax.ShapeDtypeStruct((2, M, N // 2), out_dtype)
    else:
        out_spec = pl.BlockSpec((tm, tn), lambda i, j: (i, j))
        out_shape = jax.ShapeDtypeStruct((M, N), out_dtype)
    has_norm, has_res = norm_w is not None, residual is not None

    def body(*refs):
        x_ref, w_ref = refs[0], refs[1]
        p = 2
        nw_ref = r_ref = None
        if has_norm:
            nw_ref = refs[p]
            p += 1
        if has_res:
            r_ref = refs[p]
            p += 1
        o_ref = refs[p]
        xv = x_ref[...]
        if has_norm:
            xv = _rms_fwd(xv.astype(F32), nw_ref[...])
        acc = jnp.dot(xv.astype(BF16), w_ref[...], preferred_element_type=F32)
        if has_res:
            acc = acc + r_ref[...]
        o_ref[...] = acc.astype(out_dtype)

    in_specs = [pl.BlockSpec((tm, K), lambda i, j: (i, 0)), pl.BlockSpec((K, tn), lambda i, j: (0, j))]
    args = [x, w]
    if has_norm:
        in_specs.append(pl.BlockSpec((1, K), lambda i, j: (0, 0)))
        args.append(norm_w.reshape(1, K))
    if has_res:
        in_specs.append(pl.BlockSpec((tm, tn), lambda i, j: (i, j)))
        args.append(residual)
    return pl.pallas_call(
        body, name=name, grid=(M // tm, N // tn), in_specs=in_specs,
        out_specs=out_spec, out_shape=out_shape,
        compiler_params=_cparams(("parallel", "parallel")))(*args)


def _mm_nt(dy, w, *, name, epi=None, out_dtype=F32, tm=512, tn=512, tk=512):
    halves = dy.ndim == 3
    M, K = (dy.shape[1], 2 * dy.shape[2]) if halves else dy.shape
    N = w.shape[0]
    tm, tk = min(tm, M), min(tk, K)
    tn = N if epi is not None else min(tn, N)
    assert M % tm == 0 and N % tn == 0 and K % tk == 0, (name, M, N, K, tm, tn, tk)
    nk = K // tk
    has_epi = epi is not None

    def body(*refs):
        if has_epi:
            dy_ref, w_ref, h_ref, nw_ref, r_ref, o_ref, dnw_ref, acc_ref = refs
        else:
            dy_ref, w_ref, o_ref, acc_ref = refs
        i = pl.program_id(0)
        k = pl.program_id(2)

        @pl.when(k == 0)
        def _():
            acc_ref[...] = jnp.zeros_like(acc_ref)

        def partial_product():
            return lax.dot_general(dy_ref[...].astype(BF16), w_ref[...], _NT, preferred_element_type=F32)

        @pl.when(k < nk - 1)
        def _():
            acc_ref[...] += partial_product()

        @pl.when(k == nk - 1)
        def _():
            du = acc_ref[...] + partial_product()
            if has_epi:
                hv = h_ref[...]
                r = lax.rsqrt(jnp.mean(hv * hv, axis=-1, keepdims=True) + EPS)
                xhat = hv * r
                dxh = du * nw_ref[...]
                dx = r * (dxh - xhat * jnp.mean(dxh * xhat, axis=-1, keepdims=True))
                o_ref[...] = (r_ref[...] + dx).astype(out_dtype)
                contrib = jnp.sum(du * xhat, axis=0, keepdims=True)

                @pl.when(i == 0)
                def _():
                    dnw_ref[...] = contrib

                @pl.when(i > 0)
                def _():
                    dnw_ref[...] += contrib
            else:
                o_ref[...] = du.astype(out_dtype)

    if halves:
        nkh = K // 2 // tk
        assert K // 2 % tk == 0
        dy_spec = pl.BlockSpec((None, tm, tk), lambda i, j, k: (lax.div(k, nkh), i, lax.rem(k, nkh)))
    else:
        dy_spec = pl.BlockSpec((tm, tk), lambda i, j, k: (i, k))
    in_specs = [dy_spec, pl.BlockSpec((tn, tk), lambda i, j, k: (j, k))]
    args = [dy, w]
    out_specs = [pl.BlockSpec((tm, tn), lambda i, j, k: (i, j))]
    out_shape = [jax.ShapeDtypeStruct((M, N), out_dtype)]
    if has_epi:
        h, nw, res = epi
        in_specs += [pl.BlockSpec((tm, N), lambda i, j, k: (i, 0)), pl.BlockSpec((1, N), lambda i, j, k: (0, 0)),
                     pl.BlockSpec((tm, N), lambda i, j, k: (i, 0))]
        args += [h, nw.reshape(1, N), res]
        out_specs.append(pl.BlockSpec((1, N), lambda i, j, k: (0, 0)))
        out_shape.append(jax.ShapeDtypeStruct((1, N), F32))
    outs = pl.pallas_call(
        body, name=name, grid=(M // tm, N // tn, nk), in_specs=in_specs, out_specs=out_specs, out_shape=out_shape,
        scratch_shapes=[pltpu.VMEM((tm, tn), F32)],
        compiler_params=_cparams(("arbitrary", "arbitrary", "arbitrary"), vmem_mb=56 if has_epi else 48))(*args)
    return (outs[0], outs[1]) if has_epi else outs[0]


def _mm_tn(x, dy, *, name, norm_w=None, out_dtype=BF16, tk1=1024, tn=512, tt=512):
    T, K1 = x.shape
    halves = dy.ndim == 3
    N = 2 * dy.shape[2] if halves else dy.shape[1]
    tk1, tn, tt = min(tk1, K1), min(tn, N), min(tt, T)
    has_norm = norm_w is not None
    assert K1 % tk1 == 0 and N % tn == 0 and T % tt == 0, (name, K1, N, T, tk1, tn, tt)
    assert not has_norm or tk1 == K1
    nt = T // tt

    def body(*refs):
        if has_norm:
            x_ref, dy_ref, nw_ref, o_ref, acc_ref = refs
        else:
            x_ref, dy_ref, o_ref, acc_ref = refs
        t = pl.program_id(2)

        @pl.when(t == 0)
        def _():
            acc_ref[...] = jnp.zeros_like(acc_ref)

        xv = x_ref[...]
        if has_norm:
            xv = _rms_fwd(xv.astype(F32), nw_ref[...])
        acc_ref[...] += lax.dot_general(xv.astype(BF16), dy_ref[...].astype(BF16), _TN, preferred_element_type=F32)

        @pl.when(t == nt - 1)
        def _():
            o_ref[...] = acc_ref[...].astype(out_dtype)

    if halves:
        nbh = N // 2 // tn
        assert N // 2 % tn == 0
        dy_spec = pl.BlockSpec((None, tt, tn), lambda a, b, t: (lax.div(b, nbh), t, lax.rem(b, nbh)))
    else:
        dy_spec = pl.BlockSpec((tt, tn), lambda a, b, t: (t, b))
    in_specs = [pl.BlockSpec((tt, tk1), lambda a, b, t: (t, a)), dy_spec]
    args = [x, dy]
    if has_norm:
        in_specs.append(pl.BlockSpec((1, K1), lambda a, b, t: (0, 0)))
        args.append(norm_w.reshape(1, K1))
    return pl.pallas_call(
        body, name=name, grid=(K1 // tk1, N // tn, nt), in_specs=in_specs,
        out_specs=pl.BlockSpec((tk1, tn), lambda a, b, t: (a, b)),
        out_shape=jax.ShapeDtypeStruct((K1, N), out_dtype),
        scratch_shapes=[pltpu.VMEM((tk1, tn), F32)],
        compiler_params=_cparams(("parallel", "parallel", "arbitrary")))(*args)


def _shift_down(xb, prev8, j):
    main = pltpu.roll(xb, j, 0)
    head = pltpu.roll(xb[0:8], j, 0)
    ph = pltpu.roll(prev8, j, 0)
    row8 = lax.broadcasted_iota(jnp.int32, head.shape, 0)
    head = jnp.where(row8 < j, ph, head)
    return jnp.concatenate([head, main[8:]], axis=0)


def _shift_up(xb, next8, j):
    tt = xb.shape[0]
    main = pltpu.roll(xb, tt - j, 0)
    tail = pltpu.roll(xb[tt - 8:tt], 8 - j, 0)
    nh = pltpu.roll(next8, 8 - j, 0)
    row8 = lax.broadcasted_iota(jnp.int32, tail.shape, 0)
    tail = jnp.where(row8 + j >= 8, nh, tail)
    return jnp.concatenate([main[:tt - 8], tail], axis=0)


def _conv_hid(xb, prev8, w, b_row, K):
    out = b_row
    shifted = []
    for j in range(K):
        sh = K - 1 - j
        xs = xb if sh == 0 else _shift_down(xb, prev8, sh)
        shifted.append(xs)
        out = out + xs * w[j:j + 1, :]
    return out, shifted


def _prev_idx(i, nb8):
    return jnp.maximum(i * nb8 - 1, 0)


def _ssm_conv_fwd(zx, w, b, *, name, tt=512, tc=512):
    T = zx.shape[0]
    tt = min(tt, T)
    C, K = CONV_DIM, SSM_CONV
    cb0, nb8 = D_INNER // tc, tt // 8

    def body(x_ref, p_ref, w_ref, b_ref, o_ref):
        first = (pl.program_id(1) > 0).astype(F32)
        hid, _ = _conv_hid(x_ref[...], p_ref[...] * first, w_ref[...], b_ref[...], K)
        o_ref[...] = hid * _sigmoid(hid)

    return pl.pallas_call(
        body, name=name, grid=(C // tc, T // tt),
        in_specs=[pl.BlockSpec((tt, tc), lambda c, i: (i, c + cb0)),
                  pl.BlockSpec((8, tc), lambda c, i: (_prev_idx(i, nb8), c + cb0)),
                  pl.BlockSpec((K, tc), lambda c, i: (0, c)), pl.BlockSpec((1, tc), lambda c, i: (0, c))],
        out_specs=pl.BlockSpec((tt, tc), lambda c, i: (i, c)),
        out_shape=jax.ShapeDtypeStruct((T, C), F32),
        compiler_params=_cparams(("parallel", "parallel")))(zx, zx, w, b)


def _ssm_conv_bwd_pre(zx, w, b, dout, *, name, tt=512, tc=512):
    T = zx.shape[0]
    tt = min(tt, T)
    C, K = CONV_DIM, SSM_CONV
    cb0, nb8 = D_INNER // tc, tt // 8

    def body(x_ref, p_ref, w_ref, b_ref, d_ref, dh_ref, dw_ref, db_ref):
        t = pl.program_id(1)
        first = (t > 0).astype(F32)
        hid, shifted = _conv_hid(x_ref[...], p_ref[...] * first, w_ref[...], b_ref[...], K)
        sg = _sigmoid(hid)
        dh = d_ref[...] * (sg * (1.0 + hid * (1.0 - sg)))
        dh_ref[...] = dh

        @pl.when(t == 0)
        def _():
            dw_ref[...] = jnp.zeros_like(dw_ref)
            db_ref[...] = jnp.zeros_like(db_ref)

        db_ref[...] += jnp.sum(dh, axis=0, keepdims=True)
        for j in range(K):
            dw_ref[j:j + 1, :] += jnp.sum(dh * shifted[j], axis=0, keepdims=True)

    return pl.pallas_call(
        body, name=name, grid=(C // tc, T // tt),
        in_specs=[pl.BlockSpec((tt, tc), lambda c, i: (i, c + cb0)),
                  pl.BlockSpec((8, tc), lambda c, i: (_prev_idx(i, nb8), c + cb0)),
                  pl.BlockSpec((K, tc), lambda c, i: (0, c)), pl.BlockSpec((1, tc), lambda c, i: (0, c)),
                  pl.BlockSpec((tt, tc), lambda c, i: (i, c))],
        out_specs=[pl.BlockSpec((tt, tc), lambda c, i: (i, c)), pl.BlockSpec((K, tc), lambda c, i: (0, c)),
                   pl.BlockSpec((1, tc), lambda c, i: (0, c))],
        out_shape=[jax.ShapeDtypeStruct((T, C), F32), jax.ShapeDtypeStruct((K, C), F32),
                   jax.ShapeDtypeStruct((1, C), F32)],
        compiler_params=_cparams(("parallel", "arbitrary")))(zx, zx, w, b, dout)


def _put_cols(buf, src, col0, *, name, tt=512):
    T, C = src.shape
    tt = min(tt, T)

    def body(s_ref, _, o_ref):
        o_ref[...] = s_ref[...]

    return pl.pallas_call(
        body, name=name, grid=(T // tt,),
        in_specs=[pl.BlockSpec((tt, C), lambda i: (i, 0)), _ANY],
        out_specs=pl.BlockSpec((tt, C), lambda i: (i, col0 // C)),
        out_shape=jax.ShapeDtypeStruct(buf.shape, buf.dtype), input_output_aliases={1: 0},
        compiler_params=_cparams(("parallel",)))(src, buf)


def _conv_bwd_in(dh, w, *, name, K, tt=512, tc=512, out_dtype=BF16, into=None):
    T, C = dh.shape
    tt = min(tt, T)
    nb8, nT = tt // 8, T // tt
    last8 = T // 8 - 1
    cb0 = 0 if into is None else into[1] // tc

    def body(d_ref, n_ref, w_ref, *rest):
        o_ref = rest[-1]
        notlast = (pl.program_id(1) < nT - 1).astype(F32)
        d = d_ref[...]
        nxt = n_ref[...] * notlast
        w_ = w_ref[...]
        acc = d * w_[K - 1:K, :]
        for sh in range(1, K):
            acc = acc + _shift_up(d, nxt, sh) * w_[K - 1 - sh:K - sh, :]
        o_ref[...] = acc.astype(out_dtype)

    in_specs = [pl.BlockSpec((tt, tc), lambda c, i: (i, c)),
                pl.BlockSpec((8, tc), lambda c, i: (jnp.minimum((i + 1) * nb8, last8), c)),
                pl.BlockSpec((K, tc), lambda c, i: (0, c))]
    args = [dh, dh, w]
    if into is None:
        out_shape, alias = jax.ShapeDtypeStruct((T, C), out_dtype), {}
    else:
        assert into[0].dtype == out_dtype and into[1] % tc == 0
        in_specs.append(_ANY)
        args.append(into[0])
        out_shape, alias = jax.ShapeDtypeStruct(into[0].shape, out_dtype), {3: 0}
    return pl.pallas_call(
        body, name=name, grid=(C // tc, nT), in_specs=in_specs,
        out_specs=pl.BlockSpec((tt, tc), lambda c, i: (i, c + cb0)),
        out_shape=out_shape, input_output_aliases=alias,
        compiler_params=_cparams(("parallel", "parallel")))(*args)


def _ffn_conv_fwd(a, w, b, *, name, tt=256, tc=1408):
    T = a.shape[0]
    tt = min(tt, T)
    K, nbh, nb8 = FFN_CONV, D_FF // tc, tt // 8

    def body(ag_ref, pg_ref, av_ref, pv_ref, wg_ref, wv_ref, bg_ref, bv_ref, o_ref):
        first = (pl.program_id(1) > 0).astype(F32)
        hg, _ = _conv_hid(ag_ref[...], pg_ref[...] * first, wg_ref[...], bg_ref[...], K)
        hv, _ = _conv_hid(av_ref[...], pv_ref[...] * first, wv_ref[...], bv_ref[...], K)
        o_ref[...] = (hg * _sigmoid(hg) * hv).astype(BF16)

    return pl.pallas_call(
        body, name=name, grid=(nbh, T // tt),
        in_specs=[pl.BlockSpec((tt, tc), lambda c, i: (i, c)),
                  pl.BlockSpec((8, tc), lambda c, i: (_prev_idx(i, nb8), c)),
                  pl.BlockSpec((tt, tc), lambda c, i: (i, c + nbh)),
                  pl.BlockSpec((8, tc), lambda c, i: (_prev_idx(i, nb8), c + nbh)),
                  pl.BlockSpec((K, tc), lambda c, i: (0, c)), pl.BlockSpec((K, tc), lambda c, i: (0, c + nbh)),
                  pl.BlockSpec((1, tc), lambda c, i: (0, c)), pl.BlockSpec((1, tc), lambda c, i: (0, c + nbh))],
        out_specs=pl.BlockSpec((tt, tc), lambda c, i: (i, c)),
        out_shape=jax.ShapeDtypeStruct((T, D_FF), BF16),
        compiler_params=_cparams(("parallel", "parallel")))(a, a, a, a, w, w, b, b)


def _ffn_conv_bwd_pre(a, w, b, dp, *, name, tt=256, tc=1408):
    T = a.shape[0]
    tt = min(tt, T)
    K, nbh, nb8 = FFN_CONV, D_FF // tc, tt // 8

    def body(ao_ref, po_ref, ag_ref, pg_ref, av_ref, pv_ref, wg_ref, wv_ref, bg_ref, bv_ref, dp_ref,
             dh_ref, dw_ref, db_ref):
        j = pl.program_id(0)
        t = pl.program_id(1)
        first = (t > 0).astype(F32)
        hg, _ = _conv_hid(ag_ref[...], pg_ref[...] * first, wg_ref[...], bg_ref[...], K)
        hv, _ = _conv_hid(av_ref[...], pv_ref[...] * first, wv_ref[...], bv_ref[...], K)
        sg = _sigmoid(hg)
        d = dp_ref[...].astype(F32)
        is_gate = (j < nbh).astype(F32)
        dh = d * (is_gate * (hv * (sg * (1.0 + hg * (1.0 - sg)))) + (1.0 - is_gate) * (hg * sg))
        dh_ref[...] = dh
        xo = ao_ref[...]
        po = po_ref[...] * first

        @pl.when(t == 0)
        def _():
            dw_ref[...] = jnp.zeros_like(dw_ref)
            db_ref[...] = jnp.zeros_like(db_ref)

        db_ref[...] += jnp.sum(dh, axis=0, keepdims=True)
        for jj in range(K):
            sh = K - 1 - jj
            xs = xo if sh == 0 else _shift_down(xo, po, sh)
            dw_ref[jj:jj + 1, :] += jnp.sum(dh * xs, axis=0, keepdims=True)

    def gi(c):
        return lax.rem(c, nbh)

    return pl.pallas_call(
        body, name=name, grid=(2 * nbh, T // tt),
        in_specs=[pl.BlockSpec((tt, tc), lambda c, i: (i, c)),
                  pl.BlockSpec((8, tc), lambda c, i: (_prev_idx(i, nb8), c)),
                  pl.BlockSpec((tt, tc), lambda c, i: (i, gi(c))),
                  pl.BlockSpec((8, tc), lambda c, i: (_prev_idx(i, nb8), gi(c))),
                  pl.BlockSpec((tt, tc), lambda c, i: (i, gi(c) + nbh)),
                  pl.BlockSpec((8, tc), lambda c, i: (_prev_idx(i, nb8), gi(c) + nbh)),
                  pl.BlockSpec((K, tc), lambda c, i: (0, gi(c))), pl.BlockSpec((K, tc), lambda c, i: (0, gi(c) + nbh)),
                  pl.BlockSpec((1, tc), lambda c, i: (0, gi(c))), pl.BlockSpec((1, tc), lambda c, i: (0, gi(c) + nbh)),
                  pl.BlockSpec((tt, tc), lambda c, i: (i, gi(c)))],
        out_specs=[pl.BlockSpec((tt, tc), lambda c, i: (i, c)), pl.BlockSpec((K, tc), lambda c, i: (0, c)),
                   pl.BlockSpec((1, tc), lambda c, i: (0, c))],
        out_shape=[jax.ShapeDtypeStruct((T, 2 * D_FF), F32), jax.ShapeDtypeStruct((K, 2 * D_FF), F32),
                   jax.ShapeDtypeStruct((1, 2 * D_FF), F32)],
        compiler_params=_cparams(("parallel", "arbitrary")))(a, a, a, a, a, a, w, w, b, b, dp)


def _ffn_conv_fwd3(a3, w, b, *, name, tt=256, tc=1408):
    T = a3.shape[1]
    tt = min(tt, T)
    K, nbh, n16 = FFN_CONV, D_FF // tc, tt // 16

    def body(a_ref, p_ref, wg_ref, wv_ref, bg_ref, bv_ref, o_ref):
        first = (pl.program_id(1) > 0).astype(F32)
        a = a_ref[...].astype(F32)
        prev = p_ref[...].astype(F32)[:, 8:16, :] * first
        hg, _ = _conv_hid(a[0], prev[0], wg_ref[...], bg_ref[...], K)
        hv, _ = _conv_hid(a[1], prev[1], wv_ref[...], bv_ref[...], K)
        o_ref[...] = (hg * _sigmoid(hg) * hv).astype(BF16)

    return pl.pallas_call(
        body, name=name, grid=(nbh, T // tt),
        in_specs=[pl.BlockSpec((2, tt, tc), lambda c, i: (0, i, c)),
                  pl.BlockSpec((2, 16, tc), lambda c, i: (0, _prev_idx(i, n16), c)),
                  pl.BlockSpec((K, tc), lambda c, i: (0, c)), pl.BlockSpec((K, tc), lambda c, i: (0, c + nbh)),
                  pl.BlockSpec((1, tc), lambda c, i: (0, c)), pl.BlockSpec((1, tc), lambda c, i: (0, c + nbh))],
        out_specs=pl.BlockSpec((tt, tc), lambda c, i: (i, c)),
        out_shape=jax.ShapeDtypeStruct((T, D_FF), BF16),
        compiler_params=_cparams(("parallel", "parallel")))(a3, a3, w, w, b, b)


def _ffn_conv_bwd3(a3, w, b, dp, *, name, tt=256, tc=1408):
    T = a3.shape[1]
    tt = min(tt, T)
    K, nbh, n16 = FFN_CONV, D_FF // tc, tt // 16

    def body(a_ref, p_ref, wg_ref, wv_ref, bg_ref, bv_ref, dp_ref, dh_ref, dw_ref, db_ref):
        t = pl.program_id(1)
        first = (t > 0).astype(F32)
        a = a_ref[...].astype(F32)
        prev = p_ref[...].astype(F32)[:, 8:16, :] * first
        hg, sh_g = _conv_hid(a[0], prev[0], wg_ref[...], bg_ref[...], K)
        hv, sh_v = _conv_hid(a[1], prev[1], wv_ref[...], bv_ref[...], K)
        sg = _sigmoid(hg)
        d = dp_ref[...].astype(F32)
        dhg = d * hv * (sg * (1.0 + hg * (1.0 - sg)))
        dhv = d * (hg * sg)
        dh_ref[0] = dhg.astype(BF16)
        dh_ref[1] = dhv.astype(BF16)

        @pl.when(t == 0)
        def _():
            dw_ref[...] = jnp.zeros_like(dw_ref)
            db_ref[...] = jnp.zeros_like(db_ref)

        db_ref[0] += jnp.sum(dhg, axis=0, keepdims=True)
        db_ref[1] += jnp.sum(dhv, axis=0, keepdims=True)
        for j in range(K):
            dw_ref[0, j:j + 1, :] += jnp.sum(dhg * sh_g[j], axis=0, keepdims=True)
            dw_ref[1, j:j + 1, :] += jnp.sum(dhv * sh_v[j], axis=0, keepdims=True)

    return pl.pallas_call(
        body, name=name, grid=(nbh, T // tt),
        in_specs=[pl.BlockSpec((2, tt, tc), lambda c, i: (0, i, c)),
                  pl.BlockSpec((2, 16, tc), lambda c, i: (0, _prev_idx(i, n16), c)),
                  pl.BlockSpec((K, tc), lambda c, i: (0, c)), pl.BlockSpec((K, tc), lambda c, i: (0, c + nbh)),
                  pl.BlockSpec((1, tc), lambda c, i: (0, c)), pl.BlockSpec((1, tc), lambda c, i: (0, c + nbh)),
                  pl.BlockSpec((tt, tc), lambda c, i: (i, c))],
        out_specs=[pl.BlockSpec((2, tt, tc), lambda c, i: (0, i, c)), pl.BlockSpec((2, K, tc), lambda c, i: (0, 0, c)),
                   pl.BlockSpec((2, 1, tc), lambda c, i: (0, 0, c))],
        out_shape=[jax.ShapeDtypeStruct((2, T, D_FF), BF16), jax.ShapeDtypeStruct((2, K, D_FF), F32),
                   jax.ShapeDtypeStruct((2, 1, D_FF), F32)],
        compiler_params=_cparams(("parallel", "arbitrary")))(a3, a3, w, w, b, b, dp)


def _conv_bwd_in3(dh3, w, *, name, K, tt=256, tc=1408):
    H, T, C = dh3.shape
    tt = min(tt, T)
    nb, n16, nT = C // tc, tt // 16, T // tt
    last16 = T // 16 - 1

    def body(d_ref, n_ref, w_ref, o_ref):
        notlast = (pl.program_id(2) < nT - 1).astype(F32)
        d = d_ref[...].astype(F32)
        nxt = n_ref[...].astype(F32)[0:8, :] * notlast
        w_ = w_ref[...]
        acc = d * w_[K - 1:K, :]
        for sh in range(1, K):
            acc = acc + _shift_up(d, nxt, sh) * w_[K - 1 - sh:K - sh, :]
        o_ref[...] = acc.astype(BF16)

    return pl.pallas_call(
        body, name=name, grid=(H, nb, nT),
        in_specs=[pl.BlockSpec((None, tt, tc), lambda h, c, i: (h, i, c)),
                  pl.BlockSpec((None, 16, tc), lambda h, c, i: (h, jnp.minimum((i + 1) * n16, last16), c)),
                  pl.BlockSpec((K, tc), lambda h, c, i: (0, h * nb + c))],
        out_specs=pl.BlockSpec((None, tt, tc), lambda h, c, i: (h, i, c)),
        out_shape=jax.ShapeDtypeStruct((H, T, C), BF16),
        compiler_params=_cparams(("parallel", "parallel", "parallel")))(dh3, dh3, w)


def _cumsum_rows(x):
    L = x.shape[0]
    row = lax.broadcasted_iota(jnp.int32, x.shape, 0)
    k = 1
    while k < L:
        x = x + jnp.where(row >= k, pltpu.roll(x, k, 0), 0.0)
        k *= 2
    return x


def _rcumsum_rows(x):
    L = x.shape[0]
    row = lax.broadcasted_iota(jnp.int32, x.shape, 0)
    k = 1
    while k < L:
        x = x + jnp.where(row < L - k, pltpu.roll(x, L - k, 0), 0.0)
        k *= 2
    return x


def _split_terms(m, n):
    terms, rest = [], m
    for _ in range(n):
        t = rest.astype(BF16)
        terms.append(t)
        rest = rest - t.astype(F32)
    return jnp.concatenate(terms, axis=1)


def _select_dot(m, n_terms, n_out, cond):
    K = m.shape[1]
    k = lax.broadcasted_iota(jnp.int32, (K, n_out), 0)
    j = lax.broadcasted_iota(jnp.int32, (K, n_out), 1)
    sel = cond(k, j).astype(BF16)
    return jnp.dot(_split_terms(m, n_terms), jnp.concatenate([sel] * n_terms, axis=0), preferred_element_type=F32)


def _rowsum_mxu(m):
    return _select_dot(m, 2, 128, lambda k, j: k >= 0)


def _lane_block_sums(m, width):
    shift = width.bit_length() - 1
    return _select_dot(m, 2, 128, lambda k, j: j == jnp.right_shift(k, shift))


def _heads_to_pairs(m):
    return _select_dot(m, 3, 512, lambda k, j: k == jnp.right_shift(j, 6))


def _ssd_common(dt_ref, par_ref):
    par = par_ref[...]
    raw = dt_ref[...] + par[0:1, :]
    dt = _softplus(raw)
    a = -jnp.exp(par[1:2, :])
    cs = _cumsum_rows(dt * a)
    L = cs.shape[0]
    cs_last = cs[L - 1:L, :]
    return raw, dt, a, par[2:3, :], cs, cs.T, jnp.exp(cs), jnp.exp(cs_last - cs), jnp.exp(cs_last)


def _ssd_specs(nc, rev):
    L = SSM_CHUNK

    def ci(c):
        return nc - 1 - c if rev else c

    return [pl.BlockSpec((L, D_INNER), lambda c: (ci(c), 0)),
            pl.BlockSpec((L, GN), lambda c: (ci(c), D_INNER // GN)),
            pl.BlockSpec((L, GN), lambda c: (ci(c), D_INNER // GN + 1)),
            pl.BlockSpec((SSM_GROUPS, L, 128), lambda c: (0, ci(c), 0)),
            pl.BlockSpec((SSM_GROUPS, 8, 128), lambda c: (0, 0, 0)),
            pl.BlockSpec((L, D_INNER), lambda c: (ci(c), 0)),
            pl.BlockSpec((1, D_INNER), lambda c: (0, 0))], ci


def _round_robin(gens):
    live = list(gens)
    while live:
        nxt = []
        for gen in live:
            try:
                next(gen)
                nxt.append(gen)
            except StopIteration:
                pass
        live = nxt


def _group_views(g, wide, narrow, lead):
    return ([r.at[:, g * 512:(g + 1) * 512] for r in wide], [r.at[:, g * 128:(g + 1) * 128] for r in narrow],
            [r.at[g] for r in lead])


def _ssd_fwd(xbc_c, zx, dtg, par, gnw, *, name):
    T = xbc_c.shape[0]
    L = SSM_CHUNK
    nc = T // L
    in_specs, ci = _ssd_specs(nc, False)

    def body(xs_ref, b_ref, c_ref, dt_ref, par_ref, z_ref, gnw_ref, y_ref, yn_ref, st_ref, h_ref):
        @pl.when(pl.program_id(0) == 0)
        def _():
            h_ref[...] = jnp.zeros_like(h_ref)

        gens = []
        for g in range(SSM_GROUPS):
            (xs, z, gw, y, yn), (b, c), (dt, pr, st, h) = _group_views(
                g, [xs_ref, z_ref, gnw_ref, y_ref, yn_ref], [b_ref, c_ref], [dt_ref, par_ref, st_ref, h_ref])
            gens.append(group(xs, b, c, dt, pr, z, gw, y, yn, st, h))
        _round_robin(gens)

    def group(xs_ref, b_ref, c_ref, dt_ref, par_ref, z_ref, gnw_ref, y_ref, yn_ref, st_ref, h_ref):
        _, dt, _, dsk, cs, csT, ecs, eend, dec = _ssd_common(dt_ref, par_ref)
        Bb = b_ref[...].astype(BF16)
        Cb = c_ref[...].astype(BF16)
        G = lax.dot_general(Cb, Bb, _NT, preferred_element_type=F32)
        row = lax.broadcasted_iota(jnp.int32, (L, L), 0)
        col = lax.broadcasted_iota(jnp.int32, (L, L), 1)
        tril = col <= row
        lo = lax.broadcasted_iota(jnp.int32, (L, 128), 1) < 64
        lo1 = lax.broadcasted_iota(jnp.int32, (1, 128), 1) < 64
        dt_x, ecs_x, eend_x = (_heads_to_pairs(m) for m in (dt, ecs, eend))
        for pp in range(4):
            hA, hB = 2 * pp, 2 * pp + 1
            lanes = slice(pp * 128, (pp + 1) * 128)

            def sel1(m):
                return jnp.where(lo1, m[:, hA:hA + 1], m[:, hB:hB + 1])

            X = xs_ref[:, lanes]
            xd = X * dt_x[:, lanes]
            xdb = xd.astype(BF16)
            ys = []
            for h in (hA, hB):
                Lm = jnp.where(tril, jnp.exp(jnp.minimum(cs[:, h:h + 1] - csT[h:h + 1, :], 0.0)), 0.0)
                ys.append(jnp.dot((G * Lm).astype(BF16), xdb, preferred_element_type=F32))
                yield
            Hp = h_ref[pp]
            st_ref[pp] = Hp
            yoff = jnp.dot(Cb, Hp.astype(BF16), preferred_element_type=F32) * ecs_x[:, lanes]
            y_ref[:, lanes] = jnp.where(lo, ys[0], ys[1]) + yoff + sel1(dsk) * X
            S = lax.dot_general(Bb, (xd * eend_x[:, lanes]).astype(BF16), _TN, preferred_element_type=F32)
            h_ref[pp] = Hp * sel1(dec) + S
            yield
        zv = z_ref[...]
        yg = y_ref[...] * (zv * _sigmoid(zv))
        r = jnp.tile(lax.rsqrt(_rowsum_mxu(yg * yg) * (1.0 / 512) + EPS), (1, 4))
        yn_ref[...] = (yg * r * gnw_ref[...]).astype(BF16)

    return pl.pallas_call(
        body, name=name, grid=(nc,), in_specs=in_specs,
        out_specs=[pl.BlockSpec((L, D_INNER), lambda c: (c, 0)), pl.BlockSpec((L, D_INNER), lambda c: (c, 0)),
                   pl.BlockSpec((SSM_GROUPS, None, 4, 128, 128), lambda c: (0, c, 0, 0, 0))],
        out_shape=[jax.ShapeDtypeStruct((T, D_INNER), F32), jax.ShapeDtypeStruct((T, D_INNER), BF16),
                   jax.ShapeDtypeStruct((SSM_GROUPS, nc, 4, 128, 128), F32)],
        scratch_shapes=[pltpu.VMEM((SSM_GROUPS, 4, 128, 128), F32)],
        compiler_params=_cparams(("arbitrary",)))(xbc_c, xbc_c, xbc_c, dtg, par, zx, gnw)


def _ssd_bwd(xbc_c, zx, dtg, par, gnw, y, st, dyn, *, name):
    T = xbc_c.shape[0]
    L = SSM_CHUNK
    nc = T // L
    in_specs, ci = _ssd_specs(nc, True)
    in_specs += [pl.BlockSpec((L, D_INNER), lambda c: (ci(c), 0)),
                 pl.BlockSpec((SSM_GROUPS, None, 4, 128, 128), lambda c: (0, ci(c), 0, 0, 0)),
                 pl.BlockSpec((L, D_INNER), lambda c: (ci(c), 0))]

    def body(xs_ref, b_ref, c_ref, dt_ref, par_ref, z_ref, gnw_ref, y_ref, st_ref, dyn_ref,
             dxbc_ref, dz_ref, ddt_ref, dgnw_ref, dpar_ref, dh_ref):
        @pl.when(pl.program_id(0) == 0)
        def _():
            dh_ref[...] = jnp.zeros_like(dh_ref)
            dgnw_ref[...] = jnp.zeros_like(dgnw_ref)
            dpar_ref[...] = jnp.zeros_like(dpar_ref)

        dxs_ref = dxbc_ref.at[:, 0:D_INNER]
        db_ref = dxbc_ref.at[:, D_INNER:D_INNER + GN]
        dc_ref = dxbc_ref.at[:, D_INNER + GN:CONV_DIM]

        gens = []
        for g in range(SSM_GROUPS):
            (xs, z, gw, y, dyn, dxs, dz, dgw), (b, c, db, dc), (dt, pr, st, ddt, dpr, dh) = _group_views(
                g, [xs_ref, z_ref, gnw_ref, y_ref, dyn_ref, dxs_ref, dz_ref, dgnw_ref], [b_ref, c_ref, db_ref, dc_ref],
                [dt_ref, par_ref, st_ref, ddt_ref, dpar_ref, dh_ref])
            gens.append(group(xs, b, c, dt, pr, z, gw, y, st, dyn, dxs, db, dc, dz, ddt, dgw, dpr, dh))
        _round_robin(gens)

    def group(xs_ref, b_ref, c_ref, dt_ref, par_ref, z_ref, gnw_ref, y_ref, st_ref, dyn_ref,
              dxs_ref, db_ref, dc_ref, dz_ref, ddt_ref, dgnw_ref, dpar_ref, dh_ref):
        yv = y_ref[...]
        zv = z_ref[...]
        sg = _sigmoid(zv)
        sz = zv * sg
        yg = yv * sz
        r = jnp.tile(lax.rsqrt(_rowsum_mxu(yg * yg) * (1.0 / 512) + EPS), (1, 4))
        yh = yg * r
        dyn = dyn_ref[...].astype(F32)
        dgnw_ref[...] += jnp.sum(dyn * yh, axis=0, keepdims=True)
        dyh = dyn * gnw_ref[...]
        dyg = r * (dyh - yh * jnp.tile(_rowsum_mxu(dyh * yh) * (1.0 / 512), (1, 4)))
        dY_all = dyg * sz
        dz_ref[...] = (dyg * yv * (sg * (1.0 + zv * (1.0 - sg)))).astype(dz_ref.dtype)

        yield
        raw, dt, a, dsk, cs, csT, ecs, eend, dec = _ssd_common(dt_ref, par_ref)
        Bb = b_ref[...].astype(BF16)
        Cb = c_ref[...].astype(BF16)
        G = lax.dot_general(Cb, Bb, _NT, preferred_element_type=F32)
        row = lax.broadcasted_iota(jnp.int32, (L, L), 0)
        col = lax.broadcasted_iota(jnp.int32, (L, L), 1)
        tril = col <= row
        lo = lax.broadcasted_iota(jnp.int32, (L, 128), 1) < 64
        lane1 = lax.broadcasted_iota(jnp.int32, (1, 128), 1)
        lo1 = lane1 < 64
        rowl = lax.broadcasted_iota(jnp.int32, (L, 128), 0)
        dt_x, ecs_x, eend_x = (_heads_to_pairs(m) for m in (dt, ecs, eend))
        dG = jnp.zeros((L, L), F32)
        dB = jnp.zeros((L, SSM_STATE), F32)
        dC = jnp.zeros((L, SSM_STATE), F32)
        dcs_t = jnp.zeros((L, L), F32)
        tails = jnp.zeros((1, 128), F32)
        dD_row = jnp.zeros((1, 128), F32)
        v_parts, prod_parts = [], []

        def tot(m):
            return jnp.sum(jnp.sum(m, axis=0, keepdims=True), axis=1, keepdims=True)

        for pp in range(4):
            hA, hB = 2 * pp, 2 * pp + 1
            lanes = slice(pp * 128, (pp + 1) * 128)

            def sel1(m):
                return jnp.where(lo1, m[:, hA:hA + 1], m[:, hB:hB + 1])

            X = xs_ref[:, lanes]
            dY = dY_all[:, lanes]
            dtsel = dt_x[:, lanes]
            xd = X * dtsel
            xdb = xd.astype(BF16)
            dYb = dY.astype(BF16)
            Hp = st_ref[pp]
            Hb = Hp.astype(BF16)
            dHn = dh_ref[pp]
            dHb = dHn.astype(BF16)
            ecs_sel = ecs_x[:, lanes]
            eend_sel = eend_x[:, lanes]
            dxd_state = jnp.dot(Bb, dHb, preferred_element_type=F32) * eend_sel
            yoff = jnp.dot(Cb, Hb, preferred_element_type=F32) * ecs_sel
            dYe = (dY * ecs_sel).astype(BF16)
            dC = dC + lax.dot_general(dYe, Hb, _NT, preferred_element_type=F32)
            dB = dB + lax.dot_general((xd * eend_sel).astype(BF16), dHb, _NT, preferred_element_type=F32)
            dh_ref[pp] = dHn * sel1(dec) + lax.dot_general(Cb, dYe, _TN, preferred_element_type=F32)
            q = xd * dxd_state
            dyq = dY * yoff - q
            qcol = jnp.sum(q, axis=0, keepdims=True)
            hcol = jnp.sum(dHn * Hp, axis=0, keepdims=True)
            dxd_diag = []
            for h, msk, msk1 in ((hA, lo, lo1), (hB, jnp.logical_not(lo), jnp.logical_not(lo1))):
                Lm = jnp.where(tril, jnp.exp(jnp.minimum(cs[:, h:h + 1] - csT[h:h + 1, :], 0.0)), 0.0)
                M = G * Lm
                dxd_diag.append(lax.dot_general(M.astype(BF16), dYb, _TN, preferred_element_type=F32))
                dM = lax.dot_general(jnp.where(msk, dY, 0.0).astype(BF16), xdb, _NT, preferred_element_type=F32)
                dG = dG + dM * Lm
                W = dM * M
                dcs_t = dcs_t + jnp.where(row == h, jnp.sum(W, axis=0, keepdims=True), 0.0)
                v_parts.append(W + jnp.where(msk, dyq, 0.0))
                tail = (jnp.sum(jnp.where(msk1, qcol, 0.0), axis=1, keepdims=True)
                        + dec[:, h:h + 1] * jnp.sum(jnp.where(msk1, hcol, 0.0), axis=1, keepdims=True))
                tails = tails + jnp.where(lane1 == h, tail, 0.0)
                yield
            dxd = jnp.where(lo, dxd_diag[0], dxd_diag[1]) + dxd_state
            prod_parts.append(dxd * X)
            dxs_ref[:, lanes] = dxd * dtsel + sel1(dsk) * dY
            dyx = jnp.sum(dY * X, axis=0, keepdims=True)
            sA = jnp.sum(jnp.where(lo1, dyx, 0.0), axis=1, keepdims=True)
            sB = jnp.sum(dyx, axis=1, keepdims=True) - sA
            dD_row = dD_row + jnp.where(lane1 == hA, sA, 0.0) + jnp.where(lane1 == hB, sB, 0.0)
            yield
        dGb = dG.astype(BF16)
        db_ref[...] = dB + lax.dot_general(dGb, Cb, _TN, preferred_element_type=F32)
        dc_ref[...] = dC + jnp.dot(dGb, Bb, preferred_element_type=F32)
        dcs_mat = _lane_block_sums(jnp.concatenate(v_parts, axis=1), 128) + jnp.where(rowl == L - 1, tails, 0.0)
        ddt_mat = _lane_block_sums(jnp.concatenate(prod_parts, axis=1), 64)
        dad = _rcumsum_rows(dcs_mat - dcs_t.T)
        draw = (a * dad + ddt_mat) * _sigmoid(raw)
        ddt_ref[...] = draw
        dpar_ref[0:1, :] += jnp.sum(draw, axis=0, keepdims=True)
        dpar_ref[1:2, :] += jnp.sum(dt * dad, axis=0, keepdims=True) * a
        dpar_ref[2:3, :] += dD_row

    return pl.pallas_call(
        body, name=name, grid=(nc,), in_specs=in_specs,
        out_specs=[pl.BlockSpec((L, CONV_DIM), lambda c: (ci(c), 0)),
                   pl.BlockSpec((L, D_INNER), lambda c: (ci(c), 0)),
                   pl.BlockSpec((SSM_GROUPS, L, 128), lambda c: (0, ci(c), 0)),
                   pl.BlockSpec((1, D_INNER), lambda c: (0, 0)),
                   pl.BlockSpec((SSM_GROUPS, 8, 128), lambda c: (0, 0, 0))],
        out_shape=[jax.ShapeDtypeStruct((T, CONV_DIM), F32), jax.ShapeDtypeStruct((T, IN_PROJ_PAD), BF16),
                   jax.ShapeDtypeStruct((SSM_GROUPS, T, 128), F32), jax.ShapeDtypeStruct((1, D_INNER), F32),
                   jax.ShapeDtypeStruct((SSM_GROUPS, 8, 128), F32)],
        scratch_shapes=[pltpu.VMEM((SSM_GROUPS, 4, 128, 128), F32)],
        compiler_params=_cparams(("arbitrary",)))(xbc_c, xbc_c, xbc_c, dtg, par, zx, gnw, y, st, dyn)


SB_KEYS = 512
SB_SCAN = 256
SB_STRIP = 256


def _tri(width, cond):
    kk = lax.broadcasted_iota(jnp.int32, (width, width), 0)
    jj = lax.broadcasted_iota(jnp.int32, (width, width), 1)
    return cond(kk, jj).astype(BF16)


def _sba_diag_mask():
    Bq = SB_BLOCK
    rowi = lax.broadcasted_iota(jnp.int32, (2 * Bq, Bq), 0)
    return lax.broadcasted_iota(jnp.int32, (2 * Bq, Bq), 1) < jnp.where(rowi >= Bq, rowi - Bq, rowi)


_LOG2E = 1.4426950408889634


def _softplus2(z2):
    return jnp.maximum(z2, 0.0) + jnp.log2(1.0 + jnp.exp2(-jnp.abs(z2)))


def _sba_sub_fwd(zb, c, U, mask):
    z2 = zb * _LOG2E
    s = _softplus2(z2)
    if mask is not None:
        s = jnp.where(mask, s, 0.0)
    R = c + jnp.dot(s.astype(BF16), U, preferred_element_type=F32)
    A = jnp.exp2(z2 - s - R)
    if mask is not None:
        A = jnp.where(mask, A, 0.0)
    return A.astype(BF16), R[:, 0:1] + s[:, 0:1]


def _sba_sub_bwd(zb, dAb, Lt, pc, pe, Uincl, Uexcl, mask):
    last = zb.shape[1] - 1
    z2 = zb * _LOG2E
    s = _softplus2(z2)
    g = z2 - s
    if mask is not None:
        s = jnp.where(mask, s, 0.0)
    P = pc + jnp.dot(s.astype(BF16), Uincl, preferred_element_type=F32)
    A = jnp.exp2(g - (Lt - P))
    if mask is not None:
        A = jnp.where(mask, A, 0.0)
    E = dAb * A
    PE = pe + jnp.dot(E.astype(BF16), Uexcl, preferred_element_type=F32)
    dz = E - jnp.exp2(g) * (E + PE)
    if mask is not None:
        dz = jnp.where(mask, dz, 0.0)
    return (A.astype(BF16), dz.astype(BF16), P[:, last:last + 1], PE[:, last:last + 1] + E[:, last:last + 1])


def _stack_heads(v):
    lo = lax.broadcasted_iota(jnp.int32, v.shape, 1) < 64
    zero = jnp.zeros_like(v)
    return jnp.concatenate([jnp.where(lo, v, zero), jnp.where(lo, zero, v)], axis=0)


def _unstack_heads(v):
    lo = lax.broadcasted_iota(jnp.int32, (SB_BLOCK, 128), 1) < 64
    return jnp.where(lo, v[:SB_BLOCK], v[SB_BLOCK:])


def _sba_rows(a):
    return slice(2 * a * SB_BLOCK, 2 * (a + 1) * SB_BLOCK)


def _sba_diag_case(a, b):
    Bq = SB_BLOCK
    if b * SB_SCAN >= (a + 1) * Bq:
        return "skip"
    if (b + 1) * SB_SCAN <= a * Bq:
        return "full"
    rowi = lax.broadcasted_iota(jnp.int32, (2 * Bq, SB_SCAN), 0)
    qpos = a * Bq + jnp.where(rowi >= Bq, rowi - Bq, rowi)
    return b * SB_SCAN + lax.broadcasted_iota(jnp.int32, (2 * Bq, SB_SCAN), 1) < qpos


def _sba_fwd(q, kv, *, name):
    T = q.shape[0]
    Bq = SB_BLOCK
    nsub = SB_KEYS // Bq
    nscan = SB_KEYS // SB_SCAN
    R = 2 * SB_KEYS
    assert T % SB_KEYS == 0 and SB_STRIP == 2 * Bq
    scale = 1.0 / math.sqrt(SB_HEAD_DIM)

    def body(q_ref, k_ref, v_ref, o_ref, lt_ref, z_s, a_s, c_s, acc_s):
        i = pl.program_id(1)
        U2 = _tri(SB_SCAN, lambda k, j: k > j)
        qs_all = jnp.concatenate([_stack_heads(q_ref[a * Bq:(a + 1) * Bq, :] * scale) for a in range(nsub)], axis=0)
        c_s[...] = jnp.zeros_like(c_s)
        acc_s[...] = jnp.zeros_like(acc_s)

        def scores(J, slot):
            off = pl.multiple_of(J * SB_KEYS, SB_KEYS)
            z_s[slot] = lax.dot_general(qs_all, k_ref[pl.ds(off, SB_KEYS), :], _NT, preferred_element_type=F32)

        def weights(slot, diag):
            for a in range(nsub):
                rows = _sba_rows(a)
                c = c_s[rows, :]
                for b in reversed(range(nscan)):
                    cols = slice(b * SB_SCAN, (b + 1) * SB_SCAN)
                    case = _sba_diag_case(a, b) if diag else "full"
                    if isinstance(case, str) and case == "skip":
                        a_s[slot, rows, cols] = jnp.zeros((2 * Bq, SB_SCAN), BF16)
                        continue
                    A, c = _sba_sub_fwd(z_s[slot, rows, cols], c, U2, None if isinstance(case, str) else case)
                    a_s[slot, rows, cols] = A
                c_s[rows, :] = c

        def values(J, slot):
            off = pl.multiple_of(J * SB_KEYS, SB_KEYS)
            acc_s[...] += jnp.dot(a_s[slot], v_ref[pl.ds(off, SB_KEYS), :], preferred_element_type=F32)

        scores(i, 0)
        weights(0, True)
        scores(jnp.maximum(i - 1, 0), 1)

        def two_steps(u, _):
            t = 2 * u + 1
            weights(1, False)
            scores(jnp.maximum(i - t - 1, 0), 0)
            values(i - t + 1, 0)
            weights(0, False)
            scores(jnp.maximum(i - t - 2, 0), 1)
            values(i - t, 1)
            return 0

        lax.fori_loop(0, i // 2, two_steps, 0)
        odd = lax.rem(i, 2) == 1

        @pl.when(jnp.logical_not(odd))
        def _():
            values(0, 0)

        @pl.when(odd)
        def _():
            weights(1, False)
            values(1, 0)
            values(0, 1)
        for a in range(nsub):
            o_ref[a * Bq:(a + 1) * Bq, :] = _unstack_heads(acc_s[_sba_rows(a), :]).astype(BF16)
            lt_ref[a * Bq:(a + 1) * Bq, :] = _unstack_heads(jnp.broadcast_to(c_s[_sba_rows(a), :], (2 * Bq, 128)))

    return pl.pallas_call(
        body, name=name, grid=(SB_HEADS // 2, T // SB_KEYS),
        in_specs=[pl.BlockSpec((SB_KEYS, 128), lambda p, i: (i, p)), pl.BlockSpec((T, 128), lambda p, i: (0, p)),
                  pl.BlockSpec((T, 128), lambda p, i: (0, p + SB_HEADS // 2))],
        out_specs=[pl.BlockSpec((SB_KEYS, 128), lambda p, i: (i, p)),
                   pl.BlockSpec((None, SB_KEYS, 128), lambda p, i: (p, i, 0))],
        out_shape=[jax.ShapeDtypeStruct((T, D_MODEL), BF16), jax.ShapeDtypeStruct((SB_HEADS // 2, T, 128), F32)],
        scratch_shapes=[pltpu.VMEM((2, R, SB_KEYS), F32), pltpu.VMEM((2, R, SB_KEYS), BF16),
                        pltpu.VMEM((R, 1), F32), pltpu.VMEM((R, 128), F32)],
        compiler_params=_cparams(("parallel", "parallel")))(q, kv, kv)


def _sba_bwd(q, kv, lt, do, *, name):
    T = q.shape[0]
    Bq = SB_BLOCK
    nq = T // SB_KEYS
    nsub = SB_KEYS // Bq
    nscan = SB_KEYS // SB_SCAN
    R = 2 * SB_KEYS
    assert T % SB_KEYS == 0 and SB_STRIP == 2 * Bq
    scale = 1.0 / math.sqrt(SB_HEAD_DIM)

    def body(q_ref, k_ref, v_ref, lt_ref, do_ref, dq_ref, dk_ref, dv_ref, dk_acc, dv_acc,
             z_s, da_s, a_s, dz_s, pc_s, pe_s, lt_s, dq_s):
        i = pl.program_id(1)

        @pl.when(i == 0)
        def _():
            dk_acc[...] = jnp.zeros_like(dk_acc)
            dv_acc[...] = jnp.zeros_like(dv_acc)

        Uincl = _tri(SB_SCAN, lambda k, j: k <= j)
        Uexcl = _tri(SB_SCAN, lambda k, j: k < j)
        qs, dos = [], []
        for a in range(nsub):
            rows = slice(a * Bq, (a + 1) * Bq)
            qs.append(_stack_heads(q_ref[rows, :] * scale))
            dos.append(_stack_heads(do_ref[rows, :]))
            lt_s[_sba_rows(a), :] = jnp.concatenate([lt_ref[rows, 0:1], lt_ref[rows, 64:65]], axis=0)
        qs_all = jnp.concatenate(qs, axis=0)
        dos_all = jnp.concatenate(dos, axis=0)
        pc_s[...] = jnp.zeros_like(pc_s)
        pe_s[...] = jnp.zeros_like(pe_s)
        a_s[1] = jnp.zeros((R, SB_KEYS), BF16)
        dz_s[1] = jnp.zeros((R, SB_KEYS), BF16)

        def scores(J, slot):
            off = pl.multiple_of(J * SB_KEYS, SB_KEYS)
            z_s[slot] = lax.dot_general(qs_all, k_ref[pl.ds(off, SB_KEYS), :], _NT, preferred_element_type=F32)
            da_s[slot] = lax.dot_general(dos_all, v_ref[pl.ds(off, SB_KEYS), :], _NT, preferred_element_type=F32)

        def gradients(slot, diag):
            for a in range(nsub):
                rows = _sba_rows(a)
                pc, pe, Lt = pc_s[rows, :], pe_s[rows, :], lt_s[rows, :]
                for b in range(nscan):
                    cols = slice(b * SB_SCAN, (b + 1) * SB_SCAN)
                    case = _sba_diag_case(a, b) if diag else "full"
                    if isinstance(case, str) and case == "skip":
                        a_s[slot, rows, cols] = jnp.zeros((2 * Bq, SB_SCAN), BF16)
                        dz_s[slot, rows, cols] = jnp.zeros((2 * Bq, SB_SCAN), BF16)
                        continue
                    A, dz, pc, pe = _sba_sub_bwd(z_s[slot, rows, cols], da_s[slot, rows, cols], Lt, pc, pe, Uincl, Uexcl,
                                                 None if isinstance(case, str) else case)
                    a_s[slot, rows, cols] = A
                    dz_s[slot, rows, cols] = dz
                pc_s[rows, :] = pc
                pe_s[rows, :] = pe

        def products(J, slot):
            off = pl.multiple_of(J * SB_KEYS, SB_KEYS)
            dzt = dz_s[slot]
            dk_acc[pl.ds(off, SB_KEYS), :] += lax.dot_general(dzt, qs_all, _TN, preferred_element_type=F32)
            dv_acc[pl.ds(off, SB_KEYS), :] += lax.dot_general(a_s[slot], dos_all, _TN, preferred_element_type=F32)
            dq_s[...] += jnp.dot(dzt, k_ref[pl.ds(off, SB_KEYS), :], preferred_element_type=F32)

        dq_s[...] = jnp.zeros_like(dq_s)
        scores(0, 0)

        def two_steps(u, _):
            t = 2 * u
            gradients(0, False)
            scores(t + 1, 1)
            products(jnp.maximum(t - 1, 0), 1)
            gradients(1, False)
            scores(t + 2, 0)
            products(t, 0)
            return 0

        lax.fori_loop(0, i // 2, two_steps, 0)
        odd = lax.rem(i, 2) == 1

        @pl.when(jnp.logical_not(odd))
        def _():
            gradients(0, True)
            products(jnp.maximum(i - 1, 0), 1)
            products(i, 0)

        @pl.when(odd)
        def _():
            gradients(0, False)
            scores(i, 1)
            products(jnp.maximum(i - 2, 0), 1)
            gradients(1, True)
            products(i - 1, 0)
            products(i, 1)

        for a in range(nsub):
            dq_ref[a * Bq:(a + 1) * Bq, :] = (_unstack_heads(dq_s[_sba_rows(a), :]) * scale).astype(BF16)

        @pl.when(i == nq - 1)
        def _():
            dk_ref[...] = dk_acc[...].astype(BF16)
            dv_ref[...] = dv_acc[...].astype(BF16)

    return pl.pallas_call(
        body, name=name, grid=(SB_HEADS // 2, nq),
        in_specs=[pl.BlockSpec((SB_KEYS, 128), lambda p, i: (i, p)), pl.BlockSpec((T, 128), lambda p, i: (0, p)),
                  pl.BlockSpec((T, 128), lambda p, i: (0, p + SB_HEADS // 2)),
                  pl.BlockSpec((None, SB_KEYS, 128), lambda p, i: (p, i, 0)),
                  pl.BlockSpec((SB_KEYS, 128), lambda p, i: (i, p))],
        out_specs=[pl.BlockSpec((SB_KEYS, 128), lambda p, i: (i, p)), pl.BlockSpec((T, 128), lambda p, i: (0, p)),
                   pl.BlockSpec((T, 128), lambda p, i: (0, p))],
        out_shape=[jax.ShapeDtypeStruct((T, D_MODEL), BF16), jax.ShapeDtypeStruct((T, D_MODEL), BF16),
                   jax.ShapeDtypeStruct((T, D_MODEL), BF16)],
        scratch_shapes=[pltpu.VMEM((T, 128), F32), pltpu.VMEM((T, 128), F32),
                        pltpu.VMEM((2, R, SB_KEYS), F32), pltpu.VMEM((2, R, SB_KEYS), F32),
                        pltpu.VMEM((2, R, SB_KEYS), BF16), pltpu.VMEM((2, R, SB_KEYS), BF16),
                        pltpu.VMEM((R, 1), F32), pltpu.VMEM((R, 1), F32), pltpu.VMEM((R, 1), F32),
                        pltpu.VMEM((R, 128), F32)],
        compiler_params=_cparams(("parallel", "arbitrary")))(q, kv, kv, lt, do)


def _sba_fwd_old(q, kv, *, name):
    T = q.shape[0]
    Bq = SB_BLOCK
    nsub = SB_KEYS // Bq
    assert T % SB_KEYS == 0
    scale = 1.0 / math.sqrt(SB_HEAD_DIM)

    def body(q_ref, k_ref, v_ref, o_ref, lt_ref):
        I = pl.program_id(1)
        U1 = _tri(Bq, lambda k, j: k > j)
        U2 = _tri(SB_SCAN, lambda k, j: k > j)
        dmask = _sba_diag_mask()
        qs = [_stack_heads(q_ref[a * Bq:(a + 1) * Bq, :] * scale) for a in range(nsub)]
        cs, accs = [], []
        for a in range(nsub):
            c = jnp.zeros((2 * Bq, 1), F32)
            acc = jnp.zeros((2 * Bq, 128), F32)
            for b in range(a, -1, -1):
                off = pl.multiple_of(I * SB_KEYS + b * Bq, Bq)
                zb = lax.dot_general(qs[a], k_ref[pl.ds(off, Bq), :], _NT, preferred_element_type=F32)
                A, c = _sba_sub_fwd(zb, c, U1, dmask if b == a else None)
                acc = acc + jnp.dot(A, v_ref[pl.ds(off, Bq), :], preferred_element_type=F32)
            cs.append(c)
            accs.append(acc)
        qs_all = jnp.concatenate(qs, axis=0)

        def step(n, carry):
            c, acc = carry
            off = pl.multiple_of((I - 1 - n) * SB_KEYS, SB_KEYS)
            z = lax.dot_general(qs_all, k_ref[pl.ds(off, SB_KEYS), :], _NT, preferred_element_type=F32)
            parts = [None] * (SB_KEYS // SB_SCAN)
            for b in reversed(range(SB_KEYS // SB_SCAN)):
                parts[b], c = _sba_sub_fwd(z[:, b * SB_SCAN:(b + 1) * SB_SCAN], c, U2, None)
            return c, acc + jnp.dot(jnp.concatenate(parts, axis=1), v_ref[pl.ds(off, SB_KEYS), :],
                                    preferred_element_type=F32)

        c, acc = lax.fori_loop(0, I, step, (jnp.concatenate(cs, axis=0), jnp.concatenate(accs, axis=0)))
        for a in range(nsub):
            rows = slice(2 * a * Bq, 2 * (a + 1) * Bq)
            o_ref[a * Bq:(a + 1) * Bq, :] = _unstack_heads(acc[rows]).astype(BF16)
            lt_ref[a * Bq:(a + 1) * Bq, :] = _unstack_heads(jnp.broadcast_to(c[rows], (2 * Bq, 128)))

    return pl.pallas_call(
        body, name=name, grid=(SB_HEADS // 2, T // SB_KEYS),
        in_specs=[pl.BlockSpec((SB_KEYS, 128), lambda p, i: (i, p)), pl.BlockSpec((T, 128), lambda p, i: (0, p)),
                  pl.BlockSpec((T, 128), lambda p, i: (0, p + SB_HEADS // 2))],
        out_specs=[pl.BlockSpec((SB_KEYS, 128), lambda p, i: (i, p)),
                   pl.BlockSpec((None, SB_KEYS, 128), lambda p, i: (p, i, 0))],
        out_shape=[jax.ShapeDtypeStruct((T, D_MODEL), BF16), jax.ShapeDtypeStruct((SB_HEADS // 2, T, 128), F32)],
        compiler_params=_cparams(("parallel", "parallel")))(q, kv, kv)


def _sba_bwd_old(q, kv, lt, do, *, name):
    T = q.shape[0]
    Bq = SB_BLOCK
    nq = T // SB_KEYS
    nsub = SB_KEYS // Bq
    assert T % SB_KEYS == 0
    scale = 1.0 / math.sqrt(SB_HEAD_DIM)

    def body(q_ref, k_ref, v_ref, lt_ref, do_ref, dq_ref, dk_ref, dv_ref, dk_acc, dv_acc,
             z_s, da_s, a_s, dz_s, pc_s, pe_s, lt_s):
        i = pl.program_id(1)

        @pl.when(i == 0)
        def _():
            dk_acc[...] = jnp.zeros_like(dk_acc)
            dv_acc[...] = jnp.zeros_like(dv_acc)

        Uincl1 = _tri(Bq, lambda k, j: k <= j)
        Uexcl1 = _tri(Bq, lambda k, j: k < j)
        Uincl2 = _tri(SB_SCAN, lambda k, j: k <= j)
        Uexcl2 = _tri(SB_SCAN, lambda k, j: k < j)
        dmask = _sba_diag_mask()
        qs, dos, lts = [], [], []
        for a in range(nsub):
            rows = slice(a * Bq, (a + 1) * Bq)
            qs.append(_stack_heads(q_ref[rows, :] * scale))
            dos.append(_stack_heads(do_ref[rows, :]))
            lts.append(jnp.concatenate([lt_ref[rows, 0:1], lt_ref[rows, 64:65]], axis=0))
        qs_all = jnp.concatenate(qs, axis=0)
        dos_all = jnp.concatenate(dos, axis=0)
        lt_all = jnp.concatenate(lts, axis=0)

        R = 2 * nsub * Bq
        pc_s[...] = jnp.zeros_like(pc_s)
        pe_s[...] = jnp.zeros_like(pe_s)
        lt_s[...] = lt_all

        def scores(J, slot):
            off = pl.multiple_of(J * SB_KEYS, SB_KEYS)
            z_s[slot] = lax.dot_general(qs_all, k_ref[pl.ds(off, SB_KEYS), :], _NT, preferred_element_type=F32)
            da_s[slot] = lax.dot_general(dos_all, v_ref[pl.ds(off, SB_KEYS), :], _NT, preferred_element_type=F32)

        def elementwise(slot):
            for r in range(R // SB_STRIP):
                rows = slice(r * SB_STRIP, (r + 1) * SB_STRIP)
                pc, pe, Lt = pc_s[rows, :], pe_s[rows, :], lt_s[rows, :]
                for b in range(SB_KEYS // SB_SCAN):
                    cols = slice(b * SB_SCAN, (b + 1) * SB_SCAN)
                    A, dz, pc, pe = _sba_sub_bwd(z_s[slot, rows, cols], da_s[slot, rows, cols], Lt, pc, pe,
                                                 Uincl2, Uexcl2, None)
                    a_s[slot, rows, cols] = A
                    dz_s[slot, rows, cols] = dz
                pc_s[rows, :] = pc
                pe_s[rows, :] = pe

        def outputs(J, slot, dq_acc):
            off = pl.multiple_of(J * SB_KEYS, SB_KEYS)
            dzt = dz_s[slot]
            dk_acc[pl.ds(off, SB_KEYS), :] += lax.dot_general(dzt, qs_all, _TN, preferred_element_type=F32)
            dv_acc[pl.ds(off, SB_KEYS), :] += lax.dot_general(a_s[slot], dos_all, _TN, preferred_element_type=F32)
            return dq_acc + jnp.dot(dzt, k_ref[pl.ds(off, SB_KEYS), :], preferred_element_type=F32)

        a_s[1] = jnp.zeros((R, SB_KEYS), BF16)
        dz_s[1] = jnp.zeros((R, SB_KEYS), BF16)
        last = jnp.maximum(i - 1, 0)
        scores(0, 0)

        def step(J, dq_acc):
            slot = lax.rem(J, 2)
            elementwise(slot)
            scores(jnp.minimum(J + 1, last), 1 - slot)
            return outputs(jnp.maximum(J - 1, 0), 1 - slot, dq_acc)

        dq_acc = lax.fori_loop(0, i, step, jnp.zeros((R, 128), F32))
        dq_acc = outputs(last, lax.rem(i + 1, 2), dq_acc)
        pc, pe = pc_s[...], pe_s[...]
        for a in range(nsub):
            rows = slice(2 * a * Bq, 2 * (a + 1) * Bq)
            pca, pea, dqa = pc[rows], pe[rows], dq_acc[rows]
            for b in range(a + 1):
                off = pl.multiple_of(i * SB_KEYS + b * Bq, Bq)
                kb = k_ref[pl.ds(off, Bq), :]
                zb = lax.dot_general(qs[a], kb, _NT, preferred_element_type=F32)
                dAb = lax.dot_general(dos[a], v_ref[pl.ds(off, Bq), :], _NT, preferred_element_type=F32)
                A, dz, pca, pea = _sba_sub_bwd(zb, dAb, lts[a], pca, pea, Uincl1, Uexcl1, dmask if b == a else None)
                dqa = dqa + jnp.dot(dz, kb, preferred_element_type=F32)
                dk_acc[pl.ds(off, Bq), :] += lax.dot_general(dz, qs[a], _TN, preferred_element_type=F32)
                dv_acc[pl.ds(off, Bq), :] += lax.dot_general(A, dos[a], _TN, preferred_element_type=F32)
            dq_ref[a * Bq:(a + 1) * Bq, :] = (_unstack_heads(dqa) * scale).astype(BF16)

        @pl.when(i == nq - 1)
        def _():
            dk_ref[...] = dk_acc[...].astype(BF16)
            dv_ref[...] = dv_acc[...].astype(BF16)

    return pl.pallas_call(
        body, name=name, grid=(SB_HEADS // 2, nq),
        in_specs=[pl.BlockSpec((SB_KEYS, 128), lambda p, i: (i, p)), pl.BlockSpec((T, 128), lambda p, i: (0, p)),
                  pl.BlockSpec((T, 128), lambda p, i: (0, p + SB_HEADS // 2)),
                  pl.BlockSpec((None, SB_KEYS, 128), lambda p, i: (p, i, 0)),
                  pl.BlockSpec((SB_KEYS, 128), lambda p, i: (i, p))],
        out_specs=[pl.BlockSpec((SB_KEYS, 128), lambda p, i: (i, p)), pl.BlockSpec((T, 128), lambda p, i: (0, p)),
                   pl.BlockSpec((T, 128), lambda p, i: (0, p))],
        out_shape=[jax.ShapeDtypeStruct((T, D_MODEL), BF16), jax.ShapeDtypeStruct((T, D_MODEL), BF16),
                   jax.ShapeDtypeStruct((T, D_MODEL), BF16)],
        scratch_shapes=[pltpu.VMEM((T, 128), F32), pltpu.VMEM((T, 128), F32),
                        pltpu.VMEM((2, 2 * SB_KEYS, SB_KEYS), F32), pltpu.VMEM((2, 2 * SB_KEYS, SB_KEYS), F32),
                        pltpu.VMEM((2, 2 * SB_KEYS, SB_KEYS), BF16), pltpu.VMEM((2, 2 * SB_KEYS, SB_KEYS), BF16),
                        pltpu.VMEM((2 * SB_KEYS, 1), F32), pltpu.VMEM((2 * SB_KEYS, 1), F32),
                        pltpu.VMEM((2 * SB_KEYS, 1), F32)],
        compiler_params=_cparams(("parallel", "arbitrary")))(q, kv, kv, lt, do)


def _loss_head(h, tgt, w, *, name, tt=512):
    T, D = h.shape
    tt = min(tt, T)

    def body(h_ref, t_ref, w_ref, loss_ref, dh_ref, dw_ref):
        i = pl.program_id(0)
        hv = h_ref[...]
        wv = w_ref[...]
        r = lax.rsqrt(jnp.mean(hv * hv, axis=-1, keepdims=True) + EPS)
        xhat = hv * r
        err = xhat * wv - t_ref[...]
        part = 0.5 * jnp.sum(jnp.mean(err * err, axis=-1, keepdims=True), axis=0, keepdims=True)
        dy = err * (1.0 / D)
        dxh = dy * wv
        dh_ref[...] = r * (dxh - xhat * jnp.mean(dxh * xhat, axis=-1, keepdims=True))
        dwc = jnp.sum(dy * xhat, axis=0, keepdims=True)

        @pl.when(i == 0)
        def _():
            loss_ref[...] = jnp.broadcast_to(part, loss_ref.shape)
            dw_ref[...] = dwc

        @pl.when(i > 0)
        def _():
            loss_ref[...] += jnp.broadcast_to(part, loss_ref.shape)
            dw_ref[...] += dwc

    return pl.pallas_call(
        body, name=name, grid=(T // tt,),
        in_specs=[pl.BlockSpec((tt, D), lambda i: (i, 0)), pl.BlockSpec((tt, D), lambda i: (i, 0)),
                  pl.BlockSpec((1, D), lambda i: (0, 0))],
        out_specs=[pl.BlockSpec((1, 128), lambda i: (0, 0)), pl.BlockSpec((tt, D), lambda i: (i, 0)),
                   pl.BlockSpec((1, D), lambda i: (0, 0))],
        out_shape=[jax.ShapeDtypeStruct((1, 128), F32), jax.ShapeDtypeStruct((T, D), F32),
                   jax.ShapeDtypeStruct((1, D), F32)],
        compiler_params=_cparams(("arbitrary",)))(h, tgt, w.reshape(1, D))


def _adamw(parts, w, m, v, *, name, tr=256):
    plist = list(parts) if isinstance(parts, (list, tuple)) else [parts]
    P, _, C = plist[0].shape
    R = sum(a.shape[1] for a in plist)
    tr = min(tr, R)
    assert all(a.shape[1] % tr == 0 for a in plist), (name, R, tr)
    nbs = [a.shape[1] // tr for a in plist]
    offs = [sum(nbs[:l]) for l in range(len(nbs))]
    c1 = 1.0 - ADAM_B1 ** ADAM_STEP
    c2 = 1.0 - ADAM_B2 ** ADAM_STEP

    def body(*refs):
        p_refs = refs[:len(plist)]
        w_ref, m_ref, v_ref, g_ref, d_ref, nm_ref, nv_ref = refs[len(plist):]
        i = pl.program_id(0)
        g = None
        for l, p_ref in enumerate(p_refs):
            gl = p_ref[0].astype(F32)
            for k in range(1, P):
                gl = gl + p_ref[k].astype(F32)
            g = gl if g is None else jnp.where(i >= offs[l], gl, g)
        mn = ADAM_B1 * m_ref[...] + (1.0 - ADAM_B1) * g
        vn = ADAM_B2 * v_ref[...] + (1.0 - ADAM_B2) * (g * g)
        g_ref[...] = g
        nm_ref[...] = mn
        nv_ref[...] = vn
        d_ref[...] = -ADAM_LR * ((mn / c1) / (jnp.sqrt(vn / c2) + ADAM_EPS) + ADAM_WD * w_ref[...])

    spec = pl.BlockSpec((tr, C), lambda i: (i, 0))
    sds = jax.ShapeDtypeStruct((R, C), F32)
    return pl.pallas_call(
        body, name=name, grid=(R // tr,),
        in_specs=[pl.BlockSpec((P, tr, C), functools.partial(lambda i, o, n: (0, jnp.clip(i - o, 0, n - 1), 0), o=o, n=n))
                  for o, n in zip(offs, nbs)] + [spec, spec, spec],
        out_specs=[spec, spec, spec, spec], out_shape=[sds, sds, sds, sds],
        compiler_params=_cparams(("parallel",)))(*plist, w, m, v)


def _all_gather(shards, *, name):
    n = len(shards)

    def body(*refs):
        ins, outs = refs[:n], refs[n:2 * n]
        send_sems, recv_sems, local_sems = refs[2 * n:]
        x, y, c = lax.axis_index("x"), lax.axis_index("y"), lax.axis_index("c")
        me, sib = (x, y, c), (x, y, 1 - c)
        chips = [(1 - x, y), (x, 1 - y), (1 - x, 1 - y)]

        def slot(p):
            return 4 * p[0] + 2 * p[1] + p[2]

        def cp(a, k, block, to, src=None):
            dst = outs[a].at[slot(block)]
            return pltpu.make_async_remote_copy(src_ref=dst if src is None else src, dst_ref=dst,
                                                send_sem=send_sems.at[a, k], recv_sem=recv_sems.at[a, k],
                                                device_id=to, device_id_type=_MESH)

        mine = [pltpu.make_async_copy(ins[a], outs[a].at[slot(me)], local_sems.at[a]) for a in range(n)]
        for m in mine:
            m.start()
        first = []
        for a in range(n):
            first.append(cp(a, 0, me, sib, src=ins[a]))
            for j, chip in enumerate(chips):
                first.append(cp(a, 1 + j, me, (*chip, c), src=ins[a]))
        for f in first:
            f.start()
        passed = []
        for j, chip in enumerate(chips):
            for a in range(n):
                cp(a, 1 + j, (*chip, c), me).wait_recv()
                f = cp(a, 4 + j, (*chip, c), sib)
                f.start()
                passed.append(f)
        for a in range(n):
            cp(a, 0, sib, me).wait_recv()
            for j, chip in enumerate(chips):
                cp(a, 4 + j, (*chip, 1 - c), me).wait_recv()
        for f in first + passed:
            f.wait_send()
        for m in mine:
            m.wait()

    return pl.pallas_call(
        body, name=name, in_specs=[_ANY] * n, out_specs=[_ANY] * n,
        out_shape=[jax.ShapeDtypeStruct((N_DEV,) + s.shape, s.dtype) for s in shards],
        scratch_shapes=[pltpu.SemaphoreType.DMA((n, 7)), pltpu.SemaphoreType.DMA((n, 7)),
                        pltpu.SemaphoreType.DMA((n,))])(*shards)


def _exchange(blocks, *, name):
    n = len(blocks)

    def body(*refs):
        ins, outs = refs[:n], refs[n:2 * n]
        send_sems, recv_sems, local_sems = refs[2 * n:]
        x, y, c = lax.axis_index("x"), lax.axis_index("y"), lax.axis_index("c")
        me = 4 * x + 2 * y + c
        mine = [pltpu.make_async_copy(ins[a].at[me], outs[a].at[me], local_sems.at[a]) for a in range(n)]
        for m in mine:
            m.start()
        copies = []
        for r in range(1, N_DEV):
            rx, ry, rc = (r >> 2) & 1, (r >> 1) & 1, r & 1
            px, py, pc = (1 - x if rx else x), (1 - y if ry else y), (1 - c if rc else c)
            peer = 4 * px + 2 * py + pc
            for a in range(n):
                copies.append((pltpu.make_async_remote_copy(
                    src_ref=ins[a].at[peer], dst_ref=outs[a].at[me], send_sem=send_sems.at[a, r - 1],
                    recv_sem=recv_sems.at[a, r - 1], device_id=(px, py, pc), device_id_type=_MESH),
                    pltpu.make_async_remote_copy(
                    src_ref=ins[a].at[peer], dst_ref=outs[a].at[peer], send_sem=send_sems.at[a, r - 1],
                    recv_sem=recv_sems.at[a, r - 1], device_id=(px, py, pc), device_id_type=_MESH)))
        for snd, _ in copies:
            snd.start()
        for _, rcv in copies:
            rcv.wait_recv()
        for snd, _ in copies:
            snd.wait_send()
        for m in mine:
            m.wait()

    return pl.pallas_call(
        body, name=name, in_specs=[_ANY] * n, out_specs=[_ANY] * n,
        out_shape=[jax.ShapeDtypeStruct(b.shape, b.dtype) for b in blocks],
        scratch_shapes=[pltpu.SemaphoreType.DMA((n, 7)), pltpu.SemaphoreType.DMA((n, 7)),
                        pltpu.SemaphoreType.DMA((n,))])(*blocks)


_HBM = pl.BlockSpec(memory_space=pltpu.HBM)
_SEM = pl.BlockSpec(memory_space=pltpu.SEMAPHORE)
_EFFECT = pltpu.SideEffectType.DATAFLOW_SIDE_EFFECTING


def _peers():
    x, y, c = lax.axis_index("x"), lax.axis_index("y"), lax.axis_index("c")
    out = []
    for r in range(1, N_DEV):
        px = 1 - x if (r >> 2) & 1 else x
        py = 1 - y if (r >> 1) & 1 else y
        pc = 1 - c if r & 1 else c
        out.append(((px, py, pc), 4 * px + 2 * py + pc))
    return 4 * x + 2 * y + c, out


def _push_copy(src_ref, land_ref, send_sems, recv_sems, a, k, me, peer, peer_slot, scatter, arriving):
    src = src_ref.at[peer_slot] if scatter else src_ref
    return pltpu.make_async_remote_copy(
        src_ref=src, dst_ref=land_ref.at[peer_slot if arriving else me], send_sem=send_sems.at[a * (N_DEV - 1) + k],
        recv_sem=recv_sems.at[a * (N_DEV - 1) + k], device_id=peer, device_id_type=_MESH)


def _push_start(srcs, *, scatter, name):
    n = len(srcs)
    lands = [lax.empty(s.shape if scatter else (N_DEV,) + s.shape, s.dtype) for s in srcs]

    def body(*refs):
        src_refs, land_refs = refs[:n], refs[n:2 * n]
        send_sems, recv_sems = refs[2 * n], refs[2 * n + 1]
        token = refs[-1]
        me, peers = _peers()
        for k, (peer, slot) in enumerate(peers):
            for a in range(n):
                _push_copy(src_refs[a], land_refs[a], send_sems, recv_sems, a, k, me, peer, slot, scatter, False).start()
        token[...] = jnp.zeros_like(token)

    hbm = lambda a: pltpu.HBM(a.shape, a.dtype)
    outs = pl.pallas_call(
        body, name=name,
        out_shape=(pltpu.SemaphoreType.DMA((n * (N_DEV - 1),)), pltpu.SemaphoreType.DMA((n * (N_DEV - 1),)),
                   *[hbm(s) for s in srcs], *[hbm(l) for l in lands], jax.ShapeDtypeStruct((8, 128), F32)),
        in_specs=[_HBM] * (2 * n),
        out_specs=(_SEM, _SEM, *([_HBM] * (2 * n)), pl.BlockSpec(memory_space=pltpu.VMEM)),
        input_output_aliases={i: 2 + i for i in range(2 * n)},
        compiler_params=pltpu.CompilerParams(has_side_effects=_EFFECT),
    )(*[pltpu.with_memory_space_constraint(s, pltpu.HBM) for s in srcs],
      *[pltpu.with_memory_space_constraint(l, pltpu.HBM) for l in lands])
    return dict(send=outs[0], recv=outs[1], srcs=list(outs[2:2 + n]), lands=list(outs[2 + n:2 + 2 * n]),
                token=outs[-1], scatter=scatter, n=n)


def _push_wait(h, after, *, name):
    n, scatter = h["n"], h["scatter"]

    def body(*refs):
        src_refs, land_refs = refs[:n], refs[n:2 * n]
        send_sems, recv_sems = refs[2 * n], refs[2 * n + 1]
        me, peers = _peers()
        for k, (peer, slot) in enumerate(peers):
            for a in range(n):
                cp = _push_copy(src_refs[a], land_refs[a], send_sems, recv_sems, a, k, me, peer, slot, scatter, True)
                cp.wait_send()
                cp.wait_recv()

    hbm = lambda a: pltpu.HBM(a.shape, a.dtype)
    outs = pl.pallas_call(
        body, name=name,
        out_shape=(*[hbm(s) for s in h["srcs"]], *[hbm(l) for l in h["lands"]]),
        in_specs=[_HBM] * (2 * n) + [_SEM, _SEM, _ANY], out_specs=tuple([_HBM] * (2 * n)),
        input_output_aliases={i: i for i in range(2 * n)},
        compiler_params=pltpu.CompilerParams(has_side_effects=_EFFECT),
    )(*h["srcs"], *h["lands"], h["send"], h["recv"], after)
    return list(outs[:n]), list(outs[n:])


def _ffn_fwd(h, nw, w_up, conv_w, conv_b, w_down, tag):
    a3 = _mm_fwd(h, w_up, norm_w=nw, name=f"ffn{tag}_up", out_dtype=BF16, halves=True, tm=1024, tn=2816)
    p = _ffn_conv_fwd3(a3, conv_w, conv_b.reshape(1, -1), name=f"ffn{tag}_conv")
    h_out = _mm_fwd(p, w_down, residual=h, name=f"ffn{tag}_down", tm=1024, tn=512)
    return h_out, (a3, p)


def _ffn_bwd(dh, h, saved, nw, w_up, conv_w, conv_b, w_down, tag):
    a3, p = saved
    g_down = _mm_tn(p, dh, name=f"ffn{tag}_down_wg", tk1=1408, tn=1024)
    dp = _mm_nt(dh, w_down, name=f"ffn{tag}_down_dg", out_dtype=BF16, tm=512, tn=2816, tk=1024)
    dhid3, dw3, db3 = _ffn_conv_bwd3(a3, conv_w, conv_b.reshape(1, -1), dp, name=f"ffn{tag}_conv_bwd")
    da3 = _conv_bwd_in3(dhid3, conv_w, K=FFN_CONV, name=f"ffn{tag}_conv_bwd_in")
    g_up = _mm_tn(h, da3, norm_w=nw, name=f"ffn{tag}_up_wg", tn=2816, tt=1024)
    dh_out, g_nw = _mm_nt(da3, w_up, epi=(h, nw, dh), name=f"ffn{tag}_up_dg", tm=1024, tk=1408)
    g_cw = jnp.concatenate([dw3[0], dw3[1]], axis=1)
    g_cb = jnp.concatenate([db3[0], db3[1]], axis=1)
    return dh_out, dict(norm=g_nw.reshape(-1), up=g_up, conv_w=g_cw, conv_b=g_cb.reshape(-1), down=g_down)


def _local_step(x, tgt, W):
    T = x.shape[0]
    f = {}
    zx = _mm_fwd(x, W["in_w"], norm_w=W["ssm_norm_w"], name="ssm_in", tm=1024, tn=896)
    xbc_c = _ssm_conv_fwd(zx, W["ssm_conv_w"], W["ssm_conv_b"].reshape(1, -1), name="ssm_conv")
    dt_raw = zx[:, D_INNER + CONV_DIM:IN_PROJ_DIM]
    dtg = jnp.pad(dt_raw.reshape(T, SSM_GROUPS, 8).transpose(1, 0, 2), ((0, 0), (0, 0), (0, 120)))
    par = jnp.stack([W["ssm_dt_bias"].reshape(SSM_GROUPS, 8), W["ssm_a_log"].reshape(SSM_GROUPS, 8),
                     W["ssm_d"].reshape(SSM_GROUPS, 8)], axis=1)
    par = jnp.pad(par, ((0, 0), (0, 5), (0, 120)))
    gnw = W["ssm_gate_norm_w"].reshape(1, D_INNER)
    y, yn, st = _ssd_fwd(xbc_c, zx, dtg, par, gnw, name="ssd_fwd")
    h1 = _mm_fwd(yn, W["ssm_out_w"], residual=x, name="ssm_out", tm=1024, tn=512)
    h2, ffn0 = _ffn_fwd(h1, W["ffn_norm_w"][0], W["ffn_up_w"][0], W["ffn_conv_w"][0], W["ffn_conv_b"][0],
                        W["ffn_down_w"][0], "0")
    q = _mm_fwd(h2, W["w_q"], norm_w=W["attn_norm_w"], out_dtype=BF16, name="attn_q", tm=1024, tn=1024)
    kv = _mm_fwd(h2, W["w_kv"], norm_w=W["kv_norm_w"], out_dtype=BF16, name="attn_kv", tm=1024, tn=1024)
    o, lt = _sba_fwd(q, kv, name="sba_fwd")
    h3 = _mm_fwd(o, W["w_o"], residual=h2, name="attn_o", tm=1024, tn=512)
    h4, ffn1 = _ffn_fwd(h3, W["ffn_norm_w"][1], W["ffn_up_w"][1], W["ffn_conv_w"][1], W["ffn_conv_b"][1],
                        W["ffn_down_w"][1], "1")
    loss, dh4, g_final = _loss_head(h4, tgt, W["final_norm_w"], name="loss_head")
    dh3, gf1 = _ffn_bwd(dh4, h3, ffn1, W["ffn_norm_w"][1], W["ffn_up_w"][1], W["ffn_conv_w"][1], W["ffn_conv_b"][1],
                        W["ffn_down_w"][1], "1")
    g_wo = _mm_tn(o, dh3, name="attn_o_wg", tn=1024)
    do = _mm_nt(dh3, W["w_o"], name="attn_o_dg", out_dtype=BF16, tn=1024, tk=1024)
    dq, dk, dv = _sba_bwd(q, kv, lt, do, name="sba_bwd")
    g_wq = _mm_tn(h2, dq, norm_w=W["attn_norm_w"], name="attn_q_wg", tn=1024)
    dh2a, g_attn_nw = _mm_nt(dq, W["w_q"], epi=(h2, W["attn_norm_w"], dh3), name="attn_q_dg", tk=1024)
    dkv = jnp.concatenate([dk, dv], axis=1)
    g_wkv = _mm_tn(h2, dkv, norm_w=W["kv_norm_w"], name="attn_kv_wg", tn=1024)
    dh2, g_kv_nw = _mm_nt(dkv, W["w_kv"], epi=(h2, W["kv_norm_w"], dh2a), name="attn_kv_dg", tk=1024)
    dh1, gf0 = _ffn_bwd(dh2, h1, ffn0, W["ffn_norm_w"][0], W["ffn_up_w"][0], W["ffn_conv_w"][0], W["ffn_conv_b"][0],
                        W["ffn_down_w"][0], "0")
    g_out = _mm_tn(yn, dh1, name="ssm_out_wg", tn=1024)
    dyn = _mm_nt(dh1, W["ssm_out_w"], name="ssm_out_dg", out_dtype=BF16, tn=1024, tk=1024)
    dxs, dB, dC, dz, ddt, g_gnw, dpar = _ssd_bwd(xbc_c, zx, dtg, par, gnw, y, st, dyn, name="ssd_bwd")
    dxbc_c = jnp.concatenate([dxs, dB, dC], axis=1)
    dhid, g_scw, g_scb = _ssm_conv_bwd_pre(zx, W["ssm_conv_w"], W["ssm_conv_b"].reshape(1, -1), dxbc_c,
                                           name="ssm_conv_bwd")
    dxbc = _conv_bwd_in(dhid, W["ssm_conv_w"], K=SSM_CONV, name="ssm_conv_bwd_in")
    ddt_t = ddt[:, :, :8].transpose(1, 0, 2).reshape(T, SSM_HEADS).astype(BF16)
    dzx = jnp.concatenate([dz, dxbc, jnp.pad(ddt_t, ((0, 0), (0, IN_PROJ_PAD - IN_PROJ_DIM)))], axis=1)
    g_in = _mm_tn(x, dzx, norm_w=W["ssm_norm_w"], name="ssm_in_wg", tn=896)
    dx, g_ssm_nw = _mm_nt(dzx, W["in_w"], epi=(x, W["ssm_norm_w"], dh1), name="ssm_in_dg", tk=1792)
    f["ssm_norm_w"] = g_ssm_nw.reshape(-1)
    f["ssm_in_w"] = g_in[:, :IN_PROJ_DIM]
    f["ssm_conv_w"] = g_scw
    f["ssm_conv_b"] = g_scb.reshape(-1)
    f["ssm_dt_bias"] = dpar[:, 0, :8].reshape(-1)
    f["ssm_a_log"] = dpar[:, 1, :8].reshape(-1)
    f["ssm_d"] = dpar[:, 2, :8].reshape(-1)
    f["ssm_gate_norm_w"] = g_gnw.reshape(-1)
    f["ssm_out_w"] = g_out
    f["kv_norm_w"] = g_kv_nw.reshape(-1)
    f["w_k"] = g_wkv[:, :D_MODEL]
    f["w_v"] = g_wkv[:, D_MODEL:]
    f["attn_norm_w"] = g_attn_nw.reshape(-1)
    f["w_q"] = g_wq
    f["w_o"] = g_wo
    f["ffn_norm_w"] = jnp.stack([gf0["norm"], gf1["norm"]])
    f["ffn_up_w"] = [gf0["up"], gf1["up"]]
    f["ffn_conv_w"] = jnp.stack([gf0["conv_w"], gf1["conv_w"]])
    f["ffn_conv_b"] = jnp.stack([gf0["conv_b"], gf1["conv_b"]])
    f["ffn_down_w"] = [gf0["down"], gf1["down"]]
    f["final_norm_w"] = g_final.reshape(-1)
    return loss, dx, f


_BIG = ["ssm_in_w", "ssm_out_w", "w_k", "w_v", "w_q", "w_o", "ffn_up_w", "ffn_down_w"]
_SMALL_SHARDED = ["ssm_norm_w", "ssm_conv_w", "ssm_conv_b", "ssm_gate_norm_w", "ffn_conv_w"]
_SMALL_REPL = ["ssm_dt_bias", "ssm_a_log", "ssm_d", "kv_norm_w", "attn_norm_w", "ffn_norm_w", "ffn_conv_b",
               "final_norm_w"]
_WEIGHTS = ["ssm_norm_w", "ssm_in_w", "ssm_conv_w", "ssm_conv_b", "ssm_dt_bias", "ssm_a_log", "ssm_d",
            "ssm_gate_norm_w", "ssm_out_w", "kv_norm_w", "w_k", "w_v", "attn_norm_w", "w_q", "w_o", "ffn_norm_w",
            "ffn_up_w", "ffn_conv_w", "ffn_conv_b", "ffn_down_w", "final_norm_w"]


def _as2d(a):
    return a.reshape(-1, a.shape[-1])


def _cols_to_full(g):
    return g.transpose(1, 0, 2).reshape(g.shape[1], N_DEV * g.shape[2])


def _full_to_cols(a):
    R = a.shape[0]
    return a.reshape(R, N_DEV, -1).transpose(1, 0, 2)


def _gather_weights(p):
    names = _BIG + _SMALL_SHARDED
    shards = [_as2d(p[n]).astype(BF16) for n in _BIG] + [_as2d(p[n]) for n in _SMALL_SHARDED]
    got = dict(zip(names, _all_gather(shards, name="gather_weights")))
    W = {n: p[n] for n in _SMALL_REPL}
    in_w = _cols_to_full(got["ssm_in_w"])
    W["in_w"] = jnp.pad(in_w, ((0, 0), (0, IN_PROJ_PAD - IN_PROJ_DIM)))
    W["ssm_out_w"] = got["ssm_out_w"].reshape(D_INNER, D_MODEL)
    W["w_kv"] = jnp.concatenate([got["w_k"].reshape(D_MODEL, D_MODEL), got["w_v"].reshape(D_MODEL, D_MODEL)], axis=1)
    W["w_q"] = got["w_q"].reshape(D_MODEL, D_MODEL)
    W["w_o"] = got["w_o"].reshape(D_MODEL, D_MODEL)
    up = got["ffn_up_w"]
    W["ffn_up_w"] = [_cols_to_full(up[:, l * D_MODEL:(l + 1) * D_MODEL]) for l in range(2)]
    dn = got["ffn_down_w"]
    rs = D_FF // N_DEV
    W["ffn_down_w"] = [dn[:, l * rs:(l + 1) * rs].reshape(D_FF, D_MODEL) for l in range(2)]
    W["ssm_norm_w"] = got["ssm_norm_w"].reshape(D_MODEL)
    W["ssm_conv_w"] = _cols_to_full(got["ssm_conv_w"])
    W["ssm_conv_b"] = got["ssm_conv_b"].reshape(CONV_DIM)
    W["ssm_gate_norm_w"] = got["ssm_gate_norm_w"].reshape(D_INNER)
    fcw = _cols_to_full(got["ffn_conv_w"])
    W["ffn_conv_w"] = fcw.reshape(2, FFN_CONV, 2 * D_FF)
    for n in ("ssm_dt_bias", "ssm_a_log", "ssm_d", "attn_norm_w"):
        W[n] = W[n].reshape(-1)
    return W


def _big_grad_blocks(f):
    rs = D_FF // N_DEV
    return {
        "ssm_in_w": _full_to_cols(f["ssm_in_w"]),
        "ssm_out_w": f["ssm_out_w"].reshape(N_DEV, D_INNER // N_DEV, D_MODEL),
        "w_k": f["w_k"].reshape(N_DEV, D_MODEL // N_DEV, D_MODEL),
        "w_v": f["w_v"].reshape(N_DEV, D_MODEL // N_DEV, D_MODEL),
        "w_q": f["w_q"].reshape(N_DEV, D_MODEL // N_DEV, D_MODEL),
        "w_o": f["w_o"].reshape(N_DEV, D_MODEL // N_DEV, D_MODEL),
        "ffn_up_w": jnp.concatenate([_full_to_cols(g) for g in f["ffn_up_w"]], axis=1),
        "ffn_down_w": jnp.concatenate([g.reshape(N_DEV, rs, D_MODEL) for g in f["ffn_down_w"]], axis=1),
    }


def _pack_small(vals):
    flat = jnp.concatenate([v.reshape(-1).astype(F32) for v in vals])
    n = flat.shape[0]
    rows = -(-n // 1024) * 8
    return jnp.pad(flat, (0, rows * 128 - n)).reshape(rows, 128)


def _unpack_small(packed, shapes):
    flat = packed.reshape(-1)
    out, off = [], 0
    for s in shapes:
        n = math.prod(s)
        out.append(flat[off:off + n].reshape(s))
        off += n
    return out


def _kernel_v1(x, ssm_norm_w, ssm_in_w, ssm_conv_w, ssm_conv_b, ssm_dt_bias, ssm_a_log, ssm_d, ssm_gate_norm_w, ssm_out_w, kv_norm_w, w_k, w_v, attn_norm_w, w_q, w_o, ffn_norm_w, ffn_up_w, ffn_conv_w, ffn_conv_b, ffn_down_w, final_norm_w, loss_target, m_ssm_norm_w, m_ssm_in_w, m_ssm_conv_w, m_ssm_conv_b, m_ssm_dt_bias, m_ssm_a_log, m_ssm_d, m_ssm_gate_norm_w, m_ssm_out_w, m_kv_norm_w, m_w_k, m_w_v, m_attn_norm_w, m_w_q, m_w_o, m_ffn_norm_w, m_ffn_up_w, m_ffn_conv_w, m_ffn_conv_b, m_ffn_down_w, m_final_norm_w, v_ssm_norm_w, v_ssm_in_w, v_ssm_conv_w, v_ssm_conv_b, v_ssm_dt_bias, v_ssm_a_log, v_ssm_d, v_ssm_gate_norm_w, v_ssm_out_w, v_kv_norm_w, v_w_k, v_w_v, v_attn_norm_w, v_w_q, v_w_o, v_ffn_norm_w, v_ffn_up_w, v_ffn_conv_w, v_ffn_conv_b, v_ffn_down_w, v_final_norm_w):
    env = dict(locals())
    p = {n: env[n] for n in _WEIGHTS}
    mom = {n: env["m_" + n] for n in _WEIGHTS}
    var = {n: env["v_" + n] for n in _WEIGHTS}
    T = x.shape[1]
    me = 4 * lax.axis_index("x") + 2 * lax.axis_index("y") + lax.axis_index("c")

    W = _gather_weights(p)
    loss_row, dx, f = _local_step(x.reshape(T, D_MODEL), loss_target.reshape(T, D_MODEL), W)
    loss = lax.psum(loss_row[0, 0], ("x", "y", "c"))

    big = _big_grad_blocks(f)
    small_names = _SMALL_REPL + _SMALL_SHARDED
    small_full = _pack_small([f[n] for n in small_names])
    small_bcast = jnp.broadcast_to(small_full[None], (N_DEV,) + small_full.shape)
    got = _exchange([big[n] for n in _BIG] + [small_bcast], name="exchange_grads")
    big_parts = dict(zip(_BIG, got[:-1]))

    zero = jnp.zeros_like(small_full)
    g_small_sum = _adamw(got[-1], zero, zero, zero, name="sum_small_grads", tr=small_full.shape[0])[0]
    full_shapes = [f[n].shape for n in small_names]
    g_small = dict(zip(small_names, _unpack_small(g_small_sum, full_shapes)))
    for n in _SMALL_SHARDED:
        width = p[n].shape[-1]
        g_small[n] = lax.dynamic_slice_in_dim(g_small[n], me * width, width, axis=g_small[n].ndim - 1)

    out_g, out_d, out_m, out_v = {}, {}, {}, {}
    for n in _BIG:
        w2, m2, v2 = _as2d(p[n]), _as2d(mom[n]), _as2d(var[n])
        tr = 352 if n == "ffn_down_w" else 256
        g, d, nm, nv = _adamw(big_parts[n], w2, m2, v2, name="adamw_" + n, tr=tr)
        out_g[n], out_d[n], out_m[n], out_v[n] = (t.reshape(p[n].shape) for t in (g, d, nm, nv))
    sw = _pack_small([p[n] for n in small_names])
    sm = _pack_small([mom[n] for n in small_names])
    sv = _pack_small([var[n] for n in small_names])
    sg = _pack_small([g_small[n] for n in small_names])
    _, d, nm, nv = _adamw(sg[None], sw, sm, sv, name="adamw_small", tr=sw.shape[0])
    shard_shapes = [p[n].shape for n in small_names]
    for n, dd, mm, vv in zip(small_names, _unpack_small(d, shard_shapes), _unpack_small(nm, shard_shapes),
                             _unpack_small(nv, shard_shapes)):
        out_g[n] = g_small[n].reshape(p[n].shape)
        out_d[n], out_m[n], out_v[n] = dd, mm, vv

    return (loss, dx.reshape(x.shape), *[out_g[n] for n in _WEIGHTS], *[out_d[n] for n in _WEIGHTS],
            *[out_m[n] for n in _WEIGHTS], *[out_v[n] for n in _WEIGHTS])


def _tie(a, token):
    return a + token[0, 0].astype(a.dtype)


def _local_step2(x, tgt, get_w, put_g):
    T = x.shape[0]
    Ws = get_w("ssm", None)
    fnw, fcw, fcb = Ws["ffn_norm_w"], Ws["ffn_conv_w"], Ws["ffn_conv_b"]
    zx = _mm_fwd(x, Ws["in_w"], norm_w=Ws["ssm_norm_w"], name="ssm_in", tm=1024, tn=1792)
    xbc_c = _ssm_conv_fwd(zx, Ws["ssm_conv_w"], Ws["ssm_conv_b"].reshape(1, -1), name="ssm_conv")
    dt_raw = zx[:, D_INNER + CONV_DIM:IN_PROJ_DIM]
    dtg = jnp.pad(dt_raw.reshape(T, SSM_GROUPS, 8).transpose(1, 0, 2), ((0, 0), (0, 0), (0, 120)))
    par = jnp.stack([Ws["ssm_dt_bias"].reshape(SSM_GROUPS, 8), Ws["ssm_a_log"].reshape(SSM_GROUPS, 8),
                     Ws["ssm_d"].reshape(SSM_GROUPS, 8)], axis=1)
    par = jnp.pad(par, ((0, 0), (0, 5), (0, 120)))
    gnw = _tie(Ws["ssm_gate_norm_w"].reshape(1, D_INNER), get_w("rest_start", xbc_c))
    y, yn, st = _ssd_fwd(xbc_c, zx, dtg, par, gnw, name="ssd_fwd")
    W0 = get_w("ffn0", y)
    Ws["ssm_out_w"] = W0["ssm_out_w"]
    h1 = _mm_fwd(yn, Ws["ssm_out_w"], residual=x, name="ssm_out", tm=1024, tn=512)
    h2, ffn0 = _ffn_fwd(h1, fnw[0], W0["up"], fcw[0], fcb[0], W0["down"], "0")
    Wr = get_w("rest", h2)
    q = _mm_fwd(h2, Wr["w_q"], norm_w=Ws["attn_norm_w"], out_dtype=BF16, name="attn_q", tm=1024, tn=1024)
    kv = _mm_fwd(h2, Wr["w_kv"], norm_w=Ws["kv_norm_w"], out_dtype=BF16, name="attn_kv", tm=1024, tn=1024)
    o, lt = _sba_fwd(q, kv, name="sba_fwd")
    h3 = _mm_fwd(o, Wr["w_o"], residual=h2, name="attn_o", tm=1024, tn=512)
    h4, ffn1 = _ffn_fwd(h3, fnw[1], Wr["up"], fcw[1], fcb[1], Wr["down"], "1")
    loss, dh4, g_final = _loss_head(h4, tgt, Ws["final_norm_w"], name="loss_head")
    dh3, gf1 = _ffn_bwd(dh4, h3, ffn1, fnw[1], Wr["up"], fcw[1], fcb[1], Wr["down"], "1")
    tok = put_g("ffn1", dict(up=gf1["up"], down=gf1["down"]))
    g_wo = _mm_tn(o, dh3, name="attn_o_wg", tn=1024)
    do = _mm_nt(dh3, _tie(Wr["w_o"], tok), name="attn_o_dg", out_dtype=BF16, tn=1024, tk=1024)
    dq, dk, dv = _sba_bwd(q, kv, lt, do, name="sba_bwd")
    g_wq = _mm_tn(h2, dq, norm_w=Ws["attn_norm_w"], name="attn_q_wg", tn=1024, tt=1024)
    dh2a, g_attn_nw = _mm_nt(dq, Wr["w_q"], epi=(h2, Ws["attn_norm_w"], dh3), name="attn_q_dg", tm=1024, tk=1024)
    dkv = jnp.concatenate([dk, dv], axis=1)
    g_wkv = _mm_tn(h2, dkv, norm_w=Ws["kv_norm_w"], name="attn_kv_wg", tn=1024, tt=1024)
    dh2, g_kv_nw = _mm_nt(dkv, Wr["w_kv"], epi=(h2, Ws["kv_norm_w"], dh2a), name="attn_kv_dg", tm=1024, tk=1024)
    tok = put_g("attn", dict(w_o=g_wo, w_q=g_wq, w_k=g_wkv[:, :D_MODEL], w_v=g_wkv[:, D_MODEL:]))
    dh1, gf0 = _ffn_bwd(dh2, h1, ffn0, fnw[0], W0["up"], fcw[0], _tie(fcb[0], tok), W0["down"], "0")
    tok = put_g("ffn0", dict(up=gf0["up"], down=gf0["down"]))
    g_out = _mm_tn(yn, dh1, name="ssm_out_wg", tn=1024)
    dyn = _mm_nt(dh1, _tie(Ws["ssm_out_w"], tok), name="ssm_out_dg", out_dtype=BF16, tn=1024, tk=1024)
    tok = put_g("ssm_out", dict(ssm_out_w=g_out))
    dxbc_c, dz, ddt, g_gnw, dpar = _ssd_bwd(xbc_c, zx, dtg, par, _tie(gnw, tok), y, st, dyn, name="ssd_bwd")
    dhid, g_scw, g_scb = _ssm_conv_bwd_pre(zx, Ws["ssm_conv_w"], Ws["ssm_conv_b"].reshape(1, -1), dxbc_c,
                                           name="ssm_conv_bwd")
    dzx = _conv_bwd_in(dhid, Ws["ssm_conv_w"], K=SSM_CONV, name="ssm_conv_bwd_in", into=(dz, D_INNER))
    ddt_t = ddt[:, :, :8].transpose(1, 0, 2).reshape(T, SSM_HEADS).astype(BF16)
    dzx = _put_cols(dzx, jnp.pad(ddt_t, ((0, 0), (0, IN_PROJ_PAD - IN_PROJ_DIM))), D_INNER + CONV_DIM, name="ssm_ddt_cols")
    g_in = _mm_tn(x, dzx, norm_w=Ws["ssm_norm_w"], name="ssm_in_wg", tn=1792, tt=1024)
    tok = put_g("ssm_in", dict(ssm_in_w=g_in[:, :IN_PROJ_DIM]))
    dx, g_ssm_nw = _mm_nt(dzx, Ws["in_w"], epi=(x, _tie(Ws["ssm_norm_w"], tok), dh1), name="ssm_in_dg", tm=1024, tk=1792)
    f = {
        "ssm_norm_w": g_ssm_nw.reshape(-1), "ssm_conv_w": g_scw,
        "ssm_conv_b": g_scb.reshape(-1), "ssm_dt_bias": dpar[:, 0, :8].reshape(-1),
        "ssm_a_log": dpar[:, 1, :8].reshape(-1), "ssm_d": dpar[:, 2, :8].reshape(-1),
        "ssm_gate_norm_w": g_gnw.reshape(-1), "kv_norm_w": g_kv_nw.reshape(-1), "attn_norm_w": g_attn_nw.reshape(-1),
        "ffn_norm_w": jnp.stack([gf0["norm"], gf1["norm"]]), "ffn_conv_w": jnp.stack([gf0["conv_w"], gf1["conv_w"]]),
        "ffn_conv_b": jnp.stack([gf0["conv_b"], gf1["conv_b"]]), "final_norm_w": g_final.reshape(-1),
    }
    return loss, dx, f


def kernel(x, ssm_norm_w, ssm_in_w, ssm_conv_w, ssm_conv_b, ssm_dt_bias, ssm_a_log, ssm_d, ssm_gate_norm_w, ssm_out_w, kv_norm_w, w_k, w_v, attn_norm_w, w_q, w_o, ffn_norm_w, ffn_up_w, ffn_conv_w, ffn_conv_b, ffn_down_w, final_norm_w, loss_target, m_ssm_norm_w, m_ssm_in_w, m_ssm_conv_w, m_ssm_conv_b, m_ssm_dt_bias, m_ssm_a_log, m_ssm_d, m_ssm_gate_norm_w, m_ssm_out_w, m_kv_norm_w, m_w_k, m_w_v, m_attn_norm_w, m_w_q, m_w_o, m_ffn_norm_w, m_ffn_up_w, m_ffn_conv_w, m_ffn_conv_b, m_ffn_down_w, m_final_norm_w, v_ssm_norm_w, v_ssm_in_w, v_ssm_conv_w, v_ssm_conv_b, v_ssm_dt_bias, v_ssm_a_log, v_ssm_d, v_ssm_gate_norm_w, v_ssm_out_w, v_kv_norm_w, v_w_k, v_w_v, v_attn_norm_w, v_w_q, v_w_o, v_ffn_norm_w, v_ffn_up_w, v_ffn_conv_w, v_ffn_conv_b, v_ffn_down_w, v_final_norm_w):
    env = dict(locals())
    p = {n: env[n] for n in _WEIGHTS}
    mom = {n: env["m_" + n] for n in _WEIGHTS}
    var = {n: env["v_" + n] for n in _WEIGHTS}
    T = x.shape[1]
    me = 4 * lax.axis_index("x") + 2 * lax.axis_index("y") + lax.axis_index("c")
    rs = D_FF // N_DEV

    def bf2(a):
        return _as2d(a).astype(BF16)

    def with_own(srcs, lands, scatter):
        out = []
        for s, l in zip(srcs, lands):
            own = lax.dynamic_index_in_dim(s, me, 0, keepdims=False) if scatter else s
            out.append(lax.dynamic_update_index_in_dim(l, own, me, 0))
        return out

    a_names = ["ssm_in_w"] + _SMALL_SHARDED
    got_a = dict(zip(a_names, _all_gather([bf2(p["ssm_in_w"])] + [_as2d(p[n]) for n in _SMALL_SHARDED],
                                          name="gather_ssm")))
    ffn0_names = ["ssm_out_w", "up0", "down0"]
    rest_names = ["w_q", "w_k", "w_v", "w_o", "up1", "down1"]
    shard = {"up0": bf2(p["ffn_up_w"][0]), "down0": bf2(p["ffn_down_w"][0]), "up1": bf2(p["ffn_up_w"][1]),
             "down1": bf2(p["ffn_down_w"][1]), "w_q": bf2(p["w_q"]), "w_k": bf2(p["w_k"]), "w_v": bf2(p["w_v"]),
             "w_o": bf2(p["w_o"]), "ssm_out_w": bf2(p["ssm_out_w"])}
    h_ffn0 = _push_start([shard[n] for n in ffn0_names], scatter=False, name="gather_ffn0_start")
    handles = {}

    def get_w(group, after):
        if group == "ssm":
            W = {n: p[n] for n in _SMALL_REPL}
            for n in ("ssm_dt_bias", "ssm_a_log", "ssm_d", "attn_norm_w"):
                W[n] = W[n].reshape(-1)
            W["in_w"] = jnp.pad(_cols_to_full(got_a["ssm_in_w"]), ((0, 0), (0, IN_PROJ_PAD - IN_PROJ_DIM)))
            W["ssm_norm_w"] = _tie(got_a["ssm_norm_w"].reshape(D_MODEL), h_ffn0["token"])
            W["ssm_conv_w"] = _cols_to_full(got_a["ssm_conv_w"])
            W["ssm_conv_b"] = got_a["ssm_conv_b"].reshape(CONV_DIM)
            W["ssm_gate_norm_w"] = got_a["ssm_gate_norm_w"].reshape(D_INNER)
            W["ffn_conv_w"] = _cols_to_full(got_a["ffn_conv_w"]).reshape(2, FFN_CONV, 2 * D_FF)
            return W
        if group == "rest_start":
            anchor = after[0, 0]
            first = shard[rest_names[0]] + (jnp.where(jnp.isfinite(anchor), anchor, 0.0) * 0.0).astype(BF16)
            handles["rest"] = _push_start([first] + [shard[n] for n in rest_names[1:]], scatter=False,
                                          name="gather_rest_start")
            return handles["rest"]["token"]
        if group == "ffn0":
            srcs, lands = _push_wait(h_ffn0, after, name="gather_ffn0_wait")
            out, up, down = with_own(srcs, lands, False)
            return dict(ssm_out_w=out.reshape(D_INNER, D_MODEL), up=_cols_to_full(up), down=down.reshape(D_FF, D_MODEL))
        srcs, lands = _push_wait(handles["rest"], after, name="gather_rest_wait")
        g = dict(zip(rest_names, with_own(srcs, lands, False)))
        sq = lambda a: a.reshape(D_MODEL, D_MODEL)
        return dict(w_q=sq(g["w_q"]), w_kv=jnp.concatenate([sq(g["w_k"]), sq(g["w_v"])], axis=1), w_o=sq(g["w_o"]),
                    up=_cols_to_full(g["up1"]), down=g["down1"].reshape(D_FF, D_MODEL))

    pending = []

    def put_g(group, g):
        if group in ("ffn0", "ffn1"):
            keys = [("ffn_up_w", int(group[-1])), ("ffn_down_w", int(group[-1]))]
            blocks = [_full_to_cols(g["up"]), g["down"].reshape(N_DEV, rs, D_MODEL)]
        elif group == "attn":
            keys = [(n, None) for n in ("w_o", "w_q", "w_k", "w_v")]
            blocks = [g[n].reshape(N_DEV, D_MODEL // N_DEV, D_MODEL) for n, _ in keys]
        elif group == "ssm_out":
            keys = [("ssm_out_w", None)]
            blocks = [g["ssm_out_w"].reshape(N_DEV, D_INNER // N_DEV, D_MODEL)]
        else:
            keys = [("ssm_in_w", None)]
            blocks = [_full_to_cols(g["ssm_in_w"])]
        h = _push_start(blocks, scatter=True, name=f"exchange_{group}_start")
        pending.append((group, keys, h))
        return h["token"]

    loss_row, dx, f = _local_step2(x.reshape(T, D_MODEL), loss_target.reshape(T, D_MODEL), get_w, put_g)

    small_names = _SMALL_REPL + _SMALL_SHARDED
    small_full = _pack_small([f[n] for n in small_names] + [loss_row[0, 0:1]])
    small_bcast = jnp.broadcast_to(small_full[None], (N_DEV,) + small_full.shape)
    h_small = _push_start([small_bcast], scatter=True, name="exchange_small_start")
    tok = h_small["token"]

    arrived, res = {}, {}
    after = dx
    for group, keys, h in pending:
        srcs, lands = _push_wait(h, after, name=f"exchange_{group}_wait")
        arrived.update(zip(keys, with_own(srcs, lands, True)))
        for n in _BIG:
            layered = (n, 0) in arrived or (n, 1) in arrived
            if n in res or not ((n, None) in arrived or ((n, 0) in arrived and (n, 1) in arrived)):
                continue
            parts = [arrived[(n, 0)], arrived[(n, 1)]] if layered else arrived[(n, None)]
            w2, m2, v2 = _as2d(p[n]), _as2d(mom[n]), _as2d(var[n])
            if not res:
                w2 = _tie(w2, tok)
            res[n] = _adamw(parts, w2, m2, v2, name=f"adamw_{n}", tr=rs if n == "ffn_down_w" else 256)
            after = res[n][0]
    srcs, lands = _push_wait(h_small, after, name="exchange_small_wait")
    small_parts = with_own(srcs, lands, True)[0]
    out_g, out_d, out_m, out_v = {}, {}, {}, {}
    for n in _BIG:
        out_g[n], out_d[n], out_m[n], out_v[n] = (t.reshape(p[n].shape) for t in res[n])

    zero = jnp.zeros_like(small_full)
    g_small_sum = _adamw(small_parts, zero, zero, zero, name="sum_small_grads", tr=small_full.shape[0])[0]
    *small_sums, loss_sum = _unpack_small(g_small_sum, [f[n].shape for n in small_names] + [(1,)])
    loss = loss_sum[0]
    g_small = dict(zip(small_names, small_sums))
    for n in _SMALL_SHARDED:
        width = p[n].shape[-1]
        g_small[n] = lax.dynamic_slice_in_dim(g_small[n], me * width, width, axis=g_small[n].ndim - 1)
    sw = _pack_small([p[n] for n in small_names])
    sm = _pack_small([mom[n] for n in small_names])
    sv = _pack_small([var[n] for n in small_names])
    sg = _pack_small([g_small[n] for n in small_names])
    _, d, nm, nv = _adamw(sg[None], sw, sm, sv, name="adamw_small", tr=sw.shape[0])
    shard_shapes = [p[n].shape for n in small_names]
    for n, dd, mm, vv in zip(small_names, _unpack_small(d, shard_shapes), _unpack_small(nm, shard_shapes),
                             _unpack_small(nv, shard_shapes)):
        out_g[n] = g_small[n].reshape(p[n].shape)
        out_d[n], out_m[n], out_v[n] = dd, mm, vv

    return (loss, dx.reshape(x.shape), *[out_g[n] for n in _WEIGHTS], *[out_d[n] for n in _WEIGHTS],
            *[out_m[n] for n in _WEIGHTS], *[out_v[n] for n in _WEIGHTS])
```

```python
import functools
import math

import jax
import jax.numpy as jnp
from jax import lax
from jax.experimental import pallas as pl
from jax.experimental.pallas import tpu as pltpu

F32 = jnp.float32
BF16 = jnp.bfloat16
EPS = 1e-6

D_MODEL = 1024
D_INNER = 2048
SSM_HEADS = 32
SSM_GROUPS = 4
SSM_STATE = 128
SSM_CONV = 4
SSM_CHUNK = 128
GN = SSM_GROUPS * SSM_STATE
CONV_DIM = D_INNER + 2 * GN
IN_PROJ_DIM = D_INNER + CONV_DIM + SSM_HEADS
IN_PROJ_PAD = 5376
SB_HEADS = 16
SB_HEAD_DIM = 64
SB_BLOCK = 128
D_FF = 2816
FFN_CONV = 3
N_DEV = 8

ADAM_LR = 0.001
ADAM_B1 = 0.9
ADAM_B2 = 0.999
ADAM_EPS = 1e-08
ADAM_WD = 0.01
ADAM_STEP = 10

_MESH = pl.DeviceIdType.MESH
_NT = (((1,), (1,)), ((), ()))
_TN = (((0,), (0,)), ((), ()))
_ANY = pl.BlockSpec(memory_space=pl.ANY)


def _cparams(sem, vmem_mb=48):
    return pltpu.CompilerParams(dimension_semantics=sem, vmem_limit_bytes=vmem_mb * 1024 * 1024)


def _sigmoid(x):
    return 0.5 * jnp.tanh(0.5 * x) + 0.5


def _softplus(x):
    return jnp.maximum(x, 0.0) + jnp.log(1.0 + jnp.exp(-jnp.abs(x)))


def _rms_fwd(xv, w):
    r = lax.rsqrt(jnp.mean(xv * xv, axis=-1, keepdims=True) + EPS)
    return xv * r * w


def _mm_fwd(x, w, *, name, norm_w=None, residual=None, out_dtype=F32, tm=512, tn=512, halves=False, w_t=False):
    M, K = x.shape
    N = w.shape[0] if w_t else w.shape[1]
    tm, tn = min(tm, M), min(tn, N)
    assert M % tm == 0 and N % tn == 0, (name, M, N, tm, tn)
    if halves:
        nbh = N // 2 // tn
        assert N // 2 % tn == 0
        out_spec = pl.BlockSpec((None, tm, tn), lambda i, j: (lax.div(j, nbh), i, lax.rem(j, nbh)))
        out_shape = jax.ShapeDtypeStruct((2, M, N // 2), out_dtype)
    else:
        out_spec = pl.BlockSpec((tm, tn), lambda i, j: (i, j))
        out_shape = jax.ShapeDtypeStruct((M, N), out_dtype)
    has_norm, has_res = norm_w is not None, residual is not None

    def body(*refs):
        x_ref, w_ref = refs[0], refs[1]
        p = 2
        nw_ref = r_ref = None
        if has_norm:
            nw_ref = refs[p]
            p += 1
        if has_res:
            r_ref = refs[p]
            p += 1
        o_ref = refs[p]
        xv = x_ref[...]
        if has_norm:
            xv = _rms_fwd(xv.astype(F32), nw_ref[...])
        acc = lax.dot_general(xv.astype(BF16), w_ref[...], _NT if w_t else (((1,), (0,)), ((), ())),
                              preferred_element_type=F32)
        if has_res:
            acc = acc + r_ref[...]
        o_ref[...] = acc.astype(out_dtype)

    w_spec = pl.BlockSpec((tn, K), lambda i, j: (j, 0)) if w_t else pl.BlockSpec((K, tn), lambda i, j: (0, j))
    in_specs = [pl.BlockSpec((tm, K), lambda i, j: (i, 0)), w_spec]
    args = [x, w]
    if has_norm:
        in_specs.append(pl.BlockSpec((1, K), lambda i, j: (0, 0)))
        args.append(norm_w.reshape(1, K))
    if has_res:
        in_specs.append(pl.BlockSpec((tm, tn), lambda i, j: (i, j)))
        args.append(residual)
    return pl.pallas_call(
        body, name=name, grid=(M // tm, N // tn), in_specs=in_specs,
        out_specs=out_spec, out_shape=out_shape,
        compiler_params=_cparams(("parallel", "parallel")))(*args)


def _mm_nt(dy, w, *, name, epi=None, out_dtype=F32, tm=512, tn=512, tk=512, w_t=False):
    halves = dy.ndim == 3
    M, K = (dy.shape[1], 2 * dy.shape[2]) if halves else dy.shape
    N = w.shape[1] if w_t else w.shape[0]
    tm, tk = min(tm, M), min(tk, K)
    tn = N if epi is not None else min(tn, N)
    assert M % tm == 0 and N % tn == 0 and K % tk == 0, (name, M, N, K, tm, tn, tk)
    nk = K // tk
    has_epi = epi is not None

    def body(*refs):
        if has_epi:
            dy_ref, w_ref, h_ref, nw_ref, r_ref, o_ref, dnw_ref, acc_ref = refs
        else:
            dy_ref, w_ref, o_ref, acc_ref = refs
        i = pl.program_id(0)
        k = pl.program_id(2)

        @pl.when(k == 0)
        def _():
            acc_ref[...] = jnp.zeros_like(acc_ref)

        acc_ref[...] += lax.dot_general(dy_ref[...].astype(BF16), w_ref[...], (((1,), (0,)), ((), ())) if w_t else _NT,
                                        preferred_element_type=F32)

        @pl.when(k == nk - 1)
        def _():
            du = acc_ref[...]
            if has_epi:
                hv = h_ref[...]
                r = lax.rsqrt(jnp.mean(hv * hv, axis=-1, keepdims=True) + EPS)
                xhat = hv * r
                dxh = du * nw_ref[...]
                dx = r * (dxh - xhat * jnp.mean(dxh * xhat, axis=-1, keepdims=True))
                o_ref[...] = (r_ref[...] + dx).astype(out_dtype)
                contrib = jnp.sum(du * xhat, axis=0, keepdims=True)

                @pl.when(i == 0)
                def _():
                    dnw_ref[...] = contrib

                @pl.when(i > 0)
                def _():
                    dnw_ref[...] += contrib
            else:
                o_ref[...] = du.astype(out_dtype)

    if halves:
        nkh = K // 2 // tk
        assert K // 2 % tk == 0
        dy_spec = pl.BlockSpec((None, tm, tk), lambda i, j, k: (lax.div(k, nkh), i, lax.rem(k, nkh)))
    else:
        dy_spec = pl.BlockSpec((tm, tk), lambda i, j, k: (i, k))
    w_spec = pl.BlockSpec((tk, tn), lambda i, j, k: (k, j)) if w_t else pl.BlockSpec((tn, tk), lambda i, j, k: (j, k))
    in_specs = [dy_spec, w_spec]
    args = [dy, w]
    out_specs = [pl.BlockSpec((tm, tn), lambda i, j, k: (i, j))]
    out_shape = [jax.ShapeDtypeStruct((M, N), out_dtype)]
    if has_epi:
        h, nw, res = epi
        in_specs += [pl.BlockSpec((tm, N), lambda i, j, k: (i, 0)), pl.BlockSpec((1, N), lambda i, j, k: (0, 0)),
                     pl.BlockSpec((tm, N), lambda i, j, k: (i, 0))]
        args += [h, nw.reshape(1, N), res]
        out_specs.append(pl.BlockSpec((1, N), lambda i, j, k: (0, 0)))
        out_shape.append(jax.ShapeDtypeStruct((1, N), F32))
    outs = pl.pallas_call(
        body, name=name, grid=(M // tm, N // tn, nk), in_specs=in_specs, out_specs=out_specs, out_shape=out_shape,
        scratch_shapes=[pltpu.VMEM((tm, tn), F32)],
        compiler_params=_cparams(("arbitrary", "arbitrary", "arbitrary")))(*args)
    return (outs[0], outs[1]) if has_epi else outs[0]


def _mm_tn(x, dy, *, name, norm_w=None, out_dtype=BF16, tk1=1024, tn=512, tt=512):
    T, K1 = x.shape
    halves = dy.ndim == 3
    N = 2 * dy.shape[2] if halves else dy.shape[1]
    tk1, tn, tt = min(tk1, K1), min(tn, N), min(tt, T)
    has_norm = norm_w is not None
    assert K1 % tk1 == 0 and N % tn == 0 and T % tt == 0, (name, K1, N, T, tk1, tn, tt)
    assert not has_norm or tk1 == K1
    nt = T // tt

    def body(*refs):
        if has_norm:
            x_ref, dy_ref, nw_ref, o_ref, acc_ref = refs
        else:
            x_ref, dy_ref, o_ref, acc_ref = refs
        t = pl.program_id(2)

        @pl.when(t == 0)
        def _():
            acc_ref[...] = jnp.zeros_like(acc_ref)

        xv = x_ref[...]
        if has_norm:
            xv = _rms_fwd(xv.astype(F32), nw_ref[...])
        acc_ref[...] += lax.dot_general(xv.astype(BF16), dy_ref[...].astype(BF16), _TN, preferred_element_type=F32)

        @pl.when(t == nt - 1)
        def _():
            o_ref[...] = acc_ref[...].astype(out_dtype)

    if halves:
        nbh = N // 2 // tn
        assert N // 2 % tn == 0
        dy_spec = pl.BlockSpec((None, tt, tn), lambda a, b, t: (lax.div(b, nbh), t, lax.rem(b, nbh)))
    else:
        dy_spec = pl.BlockSpec((tt, tn), lambda a, b, t: (t, b))
    in_specs = [pl.BlockSpec((tt, tk1), lambda a, b, t: (t, a)), dy_spec]
    args = [x, dy]
    if has_norm:
        in_specs.append(pl.BlockSpec((1, K1), lambda a, b, t: (0, 0)))
        args.append(norm_w.reshape(1, K1))
    return pl.pallas_call(
        body, name=name, grid=(K1 // tk1, N // tn, nt), in_specs=in_specs,
        out_specs=pl.BlockSpec((tk1, tn), lambda a, b, t: (a, b)),
        out_shape=jax.ShapeDtypeStruct((K1, N), out_dtype),
        scratch_shapes=[pltpu.VMEM((tk1, tn), F32)],
        compiler_params=_cparams(("parallel", "parallel", "arbitrary")))(*args)


def _mm_tn_t(dy, x, *, name, norm_w, out_dtype=BF16, tn=1408, tt=1024):
    T, K1 = x.shape
    halves = dy.ndim == 3
    N = 2 * dy.shape[2] if halves else dy.shape[1]
    tn, tt = min(tn, N), min(tt, T)
    assert N % tn == 0 and T % tt == 0, (name, N, T, tn, tt)
    nt = T // tt

    def body(dy_ref, x_ref, nw_ref, o_ref, acc_ref):
        t = pl.program_id(1)

        @pl.when(t == 0)
        def _():
            acc_ref[...] = jnp.zeros_like(acc_ref)

        xn = _rms_fwd(x_ref[...].astype(F32), nw_ref[...]).astype(BF16)
        acc_ref[...] += lax.dot_general(dy_ref[...].astype(BF16), xn, _TN, preferred_element_type=F32)

        @pl.when(t == nt - 1)
        def _():
            o_ref[...] = acc_ref[...].astype(out_dtype)

    if halves:
        nbh = N // 2 // tn
        assert N // 2 % tn == 0
        dy_spec = pl.BlockSpec((None, tt, tn), lambda b, t: (lax.div(b, nbh), t, lax.rem(b, nbh)))
    else:
        dy_spec = pl.BlockSpec((tt, tn), lambda b, t: (t, b))
    return pl.pallas_call(
        body, name=name, grid=(N // tn, nt),
        in_specs=[dy_spec, pl.BlockSpec((tt, K1), lambda b, t: (t, 0)), pl.BlockSpec((1, K1), lambda b, t: (0, 0))],
        out_specs=pl.BlockSpec((tn, K1), lambda b, t: (b, 0)),
        out_shape=jax.ShapeDtypeStruct((N, K1), out_dtype),
        scratch_shapes=[pltpu.VMEM((tn, K1), F32)],
        compiler_params=_cparams(("parallel", "arbitrary")))(dy, x, norm_w.reshape(1, K1))


def _shift_down(xb, prev8, j):
    main = pltpu.roll(xb, j, 0)
    head = pltpu.roll(xb[0:8], j, 0)
    ph = pltpu.roll(prev8, j, 0)
    row8 = lax.broadcasted_iota(jnp.int32, head.shape, 0)
    head = jnp.where(row8 < j, ph, head)
    return jnp.concatenate([head, main[8:]], axis=0)


def _shift_up(xb, next8, j):
    tt = xb.shape[0]
    main = pltpu.roll(xb, tt - j, 0)
    tail = pltpu.roll(xb[tt - 8:tt], 8 - j, 0)
    nh = pltpu.roll(next8, 8 - j, 0)
    row8 = lax.broadcasted_iota(jnp.int32, tail.shape, 0)
    tail = jnp.where(row8 + j >= 8, nh, tail)
    return jnp.concatenate([main[:tt - 8], tail], axis=0)


def _conv_hid(xb, prev8, w, b_row, K):
    out = b_row
    shifted = []
    for j in range(K):
        sh = K - 1 - j
        xs = xb if sh == 0 else _shift_down(xb, prev8, sh)
        shifted.append(xs)
        out = out + xs * w[j:j + 1, :]
    return out, shifted


def _prev_idx(i, nb8):
    return jnp.maximum(i * nb8 - 1, 0)


def _ssm_conv_fwd(zx, w, b, *, name, tt=512, tc=512):
    T = zx.shape[0]
    tt = min(tt, T)
    C, K = CONV_DIM, SSM_CONV
    cb0, nb8 = D_INNER // tc, tt // 8

    def body(x_ref, p_ref, w_ref, b_ref, o_ref):
        first = (pl.program_id(1) > 0).astype(F32)
        hid, _ = _conv_hid(x_ref[...], p_ref[...] * first, w_ref[...], b_ref[...], K)
        o_ref[...] = hid * _sigmoid(hid)

    return pl.pallas_call(
        body, name=name, grid=(C // tc, T // tt),
        in_specs=[pl.BlockSpec((tt, tc), lambda c, i: (i, c + cb0)),
                  pl.BlockSpec((8, tc), lambda c, i: (_prev_idx(i, nb8), c + cb0)),
                  pl.BlockSpec((K, tc), lambda c, i: (0, c)), pl.BlockSpec((1, tc), lambda c, i: (0, c))],
        out_specs=pl.BlockSpec((tt, tc), lambda c, i: (i, c)),
        out_shape=jax.ShapeDtypeStruct((T, C), F32),
        compiler_params=_cparams(("parallel", "parallel")))(zx, zx, w, b)


def _ssm_conv_bwd_pre(zx, w, b, dout, *, name, tt=512, tc=512):
    T = zx.shape[0]
    tt = min(tt, T)
    C, K = CONV_DIM, SSM_CONV
    cb0, nb8 = D_INNER // tc, tt // 8

    def body(x_ref, p_ref, w_ref, b_ref, d_ref, dh_ref, dw_ref, db_ref):
        t = pl.program_id(1)
        first = (t > 0).astype(F32)
        hid, shifted = _conv_hid(x_ref[...], p_ref[...] * first, w_ref[...], b_ref[...], K)
        sg = _sigmoid(hid)
        dh = d_ref[...] * (sg * (1.0 + hid * (1.0 - sg)))
        dh_ref[...] = dh

        @pl.when(t == 0)
        def _():
            dw_ref[...] = jnp.zeros_like(dw_ref)
            db_ref[...] = jnp.zeros_like(db_ref)

        db_ref[...] += jnp.sum(dh, axis=0, keepdims=True)
        for j in range(K):
            dw_ref[j:j + 1, :] += jnp.sum(dh * shifted[j], axis=0, keepdims=True)

    return pl.pallas_call(
        body, name=name, grid=(C // tc, T // tt),
        in_specs=[pl.BlockSpec((tt, tc), lambda c, i: (i, c + cb0)),
                  pl.BlockSpec((8, tc), lambda c, i: (_prev_idx(i, nb8), c + cb0)),
                  pl.BlockSpec((K, tc), lambda c, i: (0, c)), pl.BlockSpec((1, tc), lambda c, i: (0, c)),
                  pl.BlockSpec((tt, tc), lambda c, i: (i, c))],
        out_specs=[pl.BlockSpec((tt, tc), lambda c, i: (i, c)), pl.BlockSpec((K, tc), lambda c, i: (0, c)),
                   pl.BlockSpec((1, tc), lambda c, i: (0, c))],
        out_shape=[jax.ShapeDtypeStruct((T, C), F32), jax.ShapeDtypeStruct((K, C), F32),
                   jax.ShapeDtypeStruct((1, C), F32)],
        compiler_params=_cparams(("parallel", "arbitrary")))(zx, zx, w, b, dout)


def _put_cols(buf, src, col0, *, name, tt=512):
    T, C = src.shape
    tt = min(tt, T)

    def body(s_ref, _, o_ref):
        o_ref[...] = s_ref[...]

    return pl.pallas_call(
        body, name=name, grid=(T // tt,),
        in_specs=[pl.BlockSpec((tt, C), lambda i: (i, 0)), _ANY],
        out_specs=pl.BlockSpec((tt, C), lambda i: (i, col0 // C)),
        out_shape=jax.ShapeDtypeStruct(buf.shape, buf.dtype), input_output_aliases={1: 0},
        compiler_params=_cparams(("parallel",)))(src, buf)


def _conv_bwd_in(dh, w, *, name, K, tt=512, tc=512, out_dtype=BF16, into=None):
    T, C = dh.shape
    tt = min(tt, T)
    nb8, nT = tt // 8, T // tt
    last8 = T // 8 - 1
    cb0 = 0 if into is None else into[1] // tc

    def body(d_ref, n_ref, w_ref, *rest):
        o_ref = rest[-1]
        notlast = (pl.program_id(1) < nT - 1).astype(F32)
        d = d_ref[...]
        nxt = n_ref[...] * notlast
        w_ = w_ref[...]
        acc = d * w_[K - 1:K, :]
        for sh in range(1, K):
            acc = acc + _shift_up(d, nxt, sh) * w_[K - 1 - sh:K - sh, :]
        o_ref[...] = acc.astype(out_dtype)

    in_specs = [pl.BlockSpec((tt, tc), lambda c, i: (i, c)),
                pl.BlockSpec((8, tc), lambda c, i: (jnp.minimum((i + 1) * nb8, last8), c)),
                pl.BlockSpec((K, tc), lambda c, i: (0, c))]
    args = [dh, dh, w]
    if into is None:
        out_shape, alias = jax.ShapeDtypeStruct((T, C), out_dtype), {}
    else:
        assert into[0].dtype == out_dtype and into[1] % tc == 0
        in_specs.append(_ANY)
        args.append(into[0])
        out_shape, alias = jax.ShapeDtypeStruct(into[0].shape, out_dtype), {3: 0}
    return pl.pallas_call(
        body, name=name, grid=(C // tc, nT), in_specs=in_specs,
        out_specs=pl.BlockSpec((tt, tc), lambda c, i: (i, c + cb0)),
        out_shape=out_shape, input_output_aliases=alias,
        compiler_params=_cparams(("parallel", "parallel")))(*args)


def _ffn_conv_fwd3(a3, w, b, *, name, tt=256, tc=1408):
    T = a3.shape[1]
    tt = min(tt, T)
    K, nbh, n16 = FFN_CONV, D_FF // tc, tt // 16

    def body(a_ref, p_ref, wg_ref, wv_ref, bg_ref, bv_ref, o_ref):
        first = (pl.program_id(1) > 0).astype(F32)
        a = a_ref[...].astype(F32)
        prev = p_ref[...].astype(F32)[:, 8:16, :] * first
        hg, _ = _conv_hid(a[0], prev[0], wg_ref[...], bg_ref[...], K)
        hv, _ = _conv_hid(a[1], prev[1], wv_ref[...], bv_ref[...], K)
        o_ref[...] = (hg * _sigmoid(hg) * hv).astype(BF16)

    return pl.pallas_call(
        body, name=name, grid=(nbh, T // tt),
        in_specs=[pl.BlockSpec((2, tt, tc), lambda c, i: (0, i, c)),
                  pl.BlockSpec((2, 16, tc), lambda c, i: (0, _prev_idx(i, n16), c)),
                  pl.BlockSpec((K, tc), lambda c, i: (0, c)), pl.BlockSpec((K, tc), lambda c, i: (0, c + nbh)),
                  pl.BlockSpec((1, tc), lambda c, i: (0, c)), pl.BlockSpec((1, tc), lambda c, i: (0, c + nbh))],
        out_specs=pl.BlockSpec((tt, tc), lambda c, i: (i, c)),
        out_shape=jax.ShapeDtypeStruct((T, D_FF), BF16),
        compiler_params=_cparams(("parallel", "parallel")))(a3, a3, w, w, b, b)


def _ffn_conv_bwd3(a3, w, b, dp, *, name, tt=256, tc=1408):
    T = a3.shape[1]
    tt = min(tt, T)
    K, nbh, n16 = FFN_CONV, D_FF // tc, tt // 16

    def body(a_ref, p_ref, wg_ref, wv_ref, bg_ref, bv_ref, dp_ref, dh_ref, dw_ref, db_ref):
        t = pl.program_id(1)
        first = (t > 0).astype(F32)
        a = a_ref[...].astype(F32)
        prev = p_ref[...].astype(F32)[:, 8:16, :] * first
        hg, sh_g = _conv_hid(a[0], prev[0], wg_ref[...], bg_ref[...], K)
        hv, sh_v = _conv_hid(a[1], prev[1], wv_ref[...], bv_ref[...], K)
        sg = _sigmoid(hg)
        d = dp_ref[...].astype(F32)
        dhg = d * hv * (sg * (1.0 + hg * (1.0 - sg)))
        dhv = d * (hg * sg)
        dh_ref[0] = dhg.astype(BF16)
        dh_ref[1] = dhv.astype(BF16)

        @pl.when(t == 0)
        def _():
            dw_ref[...] = jnp.zeros_like(dw_ref)
            db_ref[...] = jnp.zeros_like(db_ref)

        db_ref[0] += jnp.sum(dhg, axis=0, keepdims=True)
        db_ref[1] += jnp.sum(dhv, axis=0, keepdims=True)
        for j in range(K):
            dw_ref[0, j:j + 1, :] += jnp.sum(dhg * sh_g[j], axis=0, keepdims=True)
            dw_ref[1, j:j + 1, :] += jnp.sum(dhv * sh_v[j], axis=0, keepdims=True)

    return pl.pallas_call(
        body, name=name, grid=(nbh, T // tt),
        in_specs=[pl.BlockSpec((2, tt, tc), lambda c, i: (0, i, c)),
                  pl.BlockSpec((2, 16, tc), lambda c, i: (0, _prev_idx(i, n16), c)),
                  pl.BlockSpec((K, tc), lambda c, i: (0, c)), pl.BlockSpec((K, tc), lambda c, i: (0, c + nbh)),
                  pl.BlockSpec((1, tc), lambda c, i: (0, c)), pl.BlockSpec((1, tc), lambda c, i: (0, c + nbh)),
                  pl.BlockSpec((tt, tc), lambda c, i: (i, c))],
        out_specs=[pl.BlockSpec((2, tt, tc), lambda c, i: (0, i, c)), pl.BlockSpec((2, K, tc), lambda c, i: (0, 0, c)),
                   pl.BlockSpec((2, 1, tc), lambda c, i: (0, 0, c))],
        out_shape=[jax.ShapeDtypeStruct((2, T, D_FF), BF16), jax.ShapeDtypeStruct((2, K, D_FF), F32),
                   jax.ShapeDtypeStruct((2, 1, D_FF), F32)],
        compiler_params=_cparams(("parallel", "arbitrary")))(a3, a3, w, w, b, b, dp)


def _conv_bwd_in3(dh3, w, *, name, K, tt=256, tc=1408):
    H, T, C = dh3.shape
    tt = min(tt, T)
    nb, n16, nT = C // tc, tt // 16, T // tt
    last16 = T // 16 - 1

    def body(d_ref, n_ref, w_ref, o_ref):
        notlast = (pl.program_id(2) < nT - 1).astype(F32)
        d = d_ref[...].astype(F32)
        nxt = n_ref[...].astype(F32)[0:8, :] * notlast
        w_ = w_ref[...]
        acc = d * w_[K - 1:K, :]
        for sh in range(1, K):
            acc = acc + _shift_up(d, nxt, sh) * w_[K - 1 - sh:K - sh, :]
        o_ref[...] = acc.astype(BF16)

    return pl.pallas_call(
        body, name=name, grid=(H, nb, nT),
        in_specs=[pl.BlockSpec((None, tt, tc), lambda h, c, i: (h, i, c)),
                  pl.BlockSpec((None, 16, tc), lambda h, c, i: (h, jnp.minimum((i + 1) * n16, last16), c)),
                  pl.BlockSpec((K, tc), lambda h, c, i: (0, h * nb + c))],
        out_specs=pl.BlockSpec((None, tt, tc), lambda h, c, i: (h, i, c)),
        out_shape=jax.ShapeDtypeStruct((H, T, C), BF16),
        compiler_params=_cparams(("parallel", "parallel", "parallel")))(dh3, dh3, w)


def _cumsum_rows(x):
    L = x.shape[0]
    row = lax.broadcasted_iota(jnp.int32, x.shape, 0)
    k = 1
    while k < L:
        x = x + jnp.where(row >= k, pltpu.roll(x, k, 0), 0.0)
        k *= 2
    return x


def _rcumsum_rows(x):
    L = x.shape[0]
    row = lax.broadcasted_iota(jnp.int32, x.shape, 0)
    k = 1
    while k < L:
        x = x + jnp.where(row < L - k, pltpu.roll(x, L - k, 0), 0.0)
        k *= 2
    return x


def _split_terms(m, n):
    terms, rest = [], m
    for _ in range(n):
        t = rest.astype(BF16)
        terms.append(t)
        rest = rest - t.astype(F32)
    return jnp.concatenate(terms, axis=1)


def _select_dot(m, n_terms, n_out, cond):
    K = m.shape[1]
    k = lax.broadcasted_iota(jnp.int32, (K, n_out), 0)
    j = lax.broadcasted_iota(jnp.int32, (K, n_out), 1)
    sel = cond(k, j).astype(BF16)
    return jnp.dot(_split_terms(m, n_terms), jnp.concatenate([sel] * n_terms, axis=0), preferred_element_type=F32)


def _rowsum_mxu(m):
    return _select_dot(m, 2, 128, lambda k, j: k >= 0)


def _lane_block_sums(m, width):
    shift = width.bit_length() - 1
    return _select_dot(m, 2, 128, lambda k, j: j == jnp.right_shift(k, shift))


def _heads_to_pairs(m):
    return _select_dot(m, 3, 512, lambda k, j: k == jnp.right_shift(j, 6))


def _ssd_common(dt_ref, par_ref):
    par = par_ref[...]
    raw = dt_ref[...] + par[0:1, :]
    dt = _softplus(raw)
    a = -jnp.exp(par[1:2, :])
    cs = _cumsum_rows(dt * a)
    L = cs.shape[0]
    cs_last = cs[L - 1:L, :]
    return raw, dt, a, par[2:3, :], cs, cs.T, jnp.exp(cs), jnp.exp(cs_last - cs), jnp.exp(cs_last)


def _ssd_specs(nc, rev):
    L = SSM_CHUNK

    def ci(c):
        return nc - 1 - c if rev else c

    return [pl.BlockSpec((L, D_INNER), lambda c: (ci(c), 0)),
            pl.BlockSpec((L, GN), lambda c: (ci(c), D_INNER // GN)),
            pl.BlockSpec((L, GN), lambda c: (ci(c), D_INNER // GN + 1)),
            pl.BlockSpec((SSM_GROUPS, L, 128), lambda c: (0, ci(c), 0)),
            pl.BlockSpec((SSM_GROUPS, 8, 128), lambda c: (0, 0, 0)),
            pl.BlockSpec((L, D_INNER), lambda c: (ci(c), 0)),
            pl.BlockSpec((1, D_INNER), lambda c: (0, 0))], ci


def _round_robin(gens):
    live = list(gens)
    while live:
        nxt = []
        for gen in live:
            try:
                next(gen)
                nxt.append(gen)
            except StopIteration:
                pass
        live = nxt


def _group_views(g, wide, narrow, lead):
    return ([r.at[:, g * 512:(g + 1) * 512] for r in wide], [r.at[:, g * 128:(g + 1) * 128] for r in narrow],
            [r.at[g] for r in lead])


def _ssd_fwd(xbc_c, zx, dtg, par, gnw, *, name):
    T = xbc_c.shape[0]
    L = SSM_CHUNK
    nc = T // L
    in_specs, ci = _ssd_specs(nc, False)

    def body(xs_ref, b_ref, c_ref, dt_ref, par_ref, z_ref, gnw_ref, y_ref, yn_ref, st_ref, h_ref):
        @pl.when(pl.program_id(0) == 0)
        def _():
            h_ref[...] = jnp.zeros_like(h_ref)

        gens = []
        for g in range(SSM_GROUPS):
            (xs, z, gw, y, yn), (b, c), (dt, pr, st, h) = _group_views(
                g, [xs_ref, z_ref, gnw_ref, y_ref, yn_ref], [b_ref, c_ref], [dt_ref, par_ref, st_ref, h_ref])
            gens.append(group(xs, b, c, dt, pr, z, gw, y, yn, st, h))
        _round_robin(gens)

    def group(xs_ref, b_ref, c_ref, dt_ref, par_ref, z_ref, gnw_ref, y_ref, yn_ref, st_ref, h_ref):
        _, dt, _, dsk, cs, csT, ecs, eend, dec = _ssd_common(dt_ref, par_ref)
        Bb = b_ref[...].astype(BF16)
        Cb = c_ref[...].astype(BF16)
        G = lax.dot_general(Cb, Bb, _NT, preferred_element_type=F32)
        row = lax.broadcasted_iota(jnp.int32, (L, L), 0)
        col = lax.broadcasted_iota(jnp.int32, (L, L), 1)
        tril = col <= row
        lo = lax.broadcasted_iota(jnp.int32, (L, 128), 1) < 64
        lo1 = lax.broadcasted_iota(jnp.int32, (1, 128), 1) < 64
        dt_x, ecs_x, eend_x = (_heads_to_pairs(m) for m in (dt, ecs, eend))
        for pp in range(4):
            hA, hB = 2 * pp, 2 * pp + 1
            lanes = slice(pp * 128, (pp + 1) * 128)

            def sel1(m):
                return jnp.where(lo1, m[:, hA:hA + 1], m[:, hB:hB + 1])

            X = xs_ref[:, lanes]
            xd = X * dt_x[:, lanes]
            xdb = xd.astype(BF16)
            ys = []
            for h in (hA, hB):
                Lm = jnp.where(tril, jnp.exp(jnp.minimum(cs[:, h:h + 1] - csT[h:h + 1, :], 0.0)), 0.0)
                ys.append(jnp.dot((G * Lm).astype(BF16), xdb, preferred_element_type=F32))
                yield
            Hp = h_ref[pp]
            st_ref[pp] = Hp
            yoff = jnp.dot(Cb, Hp.astype(BF16), preferred_element_type=F32) * ecs_x[:, lanes]
            y_ref[:, lanes] = jnp.where(lo, ys[0], ys[1]) + yoff + sel1(dsk) * X
            S = lax.dot_general(Bb, (xd * eend_x[:, lanes]).astype(BF16), _TN, preferred_element_type=F32)
            h_ref[pp] = Hp * sel1(dec) + S
            yield
        zv = z_ref[...]
        yg = y_ref[...] * (zv * _sigmoid(zv))
        r = jnp.tile(lax.rsqrt(_rowsum_mxu(yg * yg) * (1.0 / 512) + EPS), (1, 4))
        yn_ref[...] = (yg * r * gnw_ref[...]).astype(BF16)

    return pl.pallas_call(
        body, name=name, grid=(nc,), in_specs=in_specs,
        out_specs=[pl.BlockSpec((L, D_INNER), lambda c: (c, 0)), pl.BlockSpec((L, D_INNER), lambda c: (c, 0)),
                   pl.BlockSpec((SSM_GROUPS, None, 4, 128, 128), lambda c: (0, c, 0, 0, 0))],
        out_shape=[jax.ShapeDtypeStruct((T, D_INNER), F32), jax.ShapeDtypeStruct((T, D_INNER), BF16),
                   jax.ShapeDtypeStruct((SSM_GROUPS, nc, 4, 128, 128), F32)],
        scratch_shapes=[pltpu.VMEM((SSM_GROUPS, 4, 128, 128), F32)],
        compiler_params=_cparams(("arbitrary",)))(xbc_c, xbc_c, xbc_c, dtg, par, zx, gnw)


def _ssd_bwd(xbc_c, zx, dtg, par, gnw, y, st, dyn, *, name):
    T = xbc_c.shape[0]
    L = SSM_CHUNK
    nc = T // L
    in_specs, ci = _ssd_specs(nc, True)
    in_specs += [pl.BlockSpec((L, D_INNER), lambda c: (ci(c), 0)),
                 pl.BlockSpec((SSM_GROUPS, None, 4, 128, 128), lambda c: (0, ci(c), 0, 0, 0)),
                 pl.BlockSpec((L, D_INNER), lambda c: (ci(c), 0))]

    def body(xs_ref, b_ref, c_ref, dt_ref, par_ref, z_ref, gnw_ref, y_ref, st_ref, dyn_ref,
             dxbc_ref, dz_ref, ddt_ref, dgnw_ref, dpar_ref, dh_ref):
        @pl.when(pl.program_id(0) == 0)
        def _():
            dh_ref[...] = jnp.zeros_like(dh_ref)
            dgnw_ref[...] = jnp.zeros_like(dgnw_ref)
            dpar_ref[...] = jnp.zeros_like(dpar_ref)

        dxs_ref = dxbc_ref.at[:, 0:D_INNER]
        db_ref = dxbc_ref.at[:, D_INNER:D_INNER + GN]
        dc_ref = dxbc_ref.at[:, D_INNER + GN:CONV_DIM]

        gens = []
        for g in range(SSM_GROUPS):
            (xs, z, gw, y, dyn, dxs, dz, dgw), (b, c, db, dc), (dt, pr, st, ddt, dpr, dh) = _group_views(
                g, [xs_ref, z_ref, gnw_ref, y_ref, dyn_ref, dxs_ref, dz_ref, dgnw_ref], [b_ref, c_ref, db_ref, dc_ref],
                [dt_ref, par_ref, st_ref, ddt_ref, dpar_ref, dh_ref])
            gens.append(group(xs, b, c, dt, pr, z, gw, y, st, dyn, dxs, db, dc, dz, ddt, dgw, dpr, dh))
        _round_robin(gens)

    def group(xs_ref, b_ref, c_ref, dt_ref, par_ref, z_ref, gnw_ref, y_ref, st_ref, dyn_ref,
              dxs_ref, db_ref, dc_ref, dz_ref, ddt_ref, dgnw_ref, dpar_ref, dh_ref):
        yv = y_ref[...]
        zv = z_ref[...]
        sg = _sigmoid(zv)
        sz = zv * sg
        yg = yv * sz
        r = jnp.tile(lax.rsqrt(_rowsum_mxu(yg * yg) * (1.0 / 512) + EPS), (1, 4))
        yh = yg * r
        dyn = dyn_ref[...].astype(F32)
        dgnw_ref[...] += jnp.sum(dyn * yh, axis=0, keepdims=True)
        dyh = dyn * gnw_ref[...]
        dyg = r * (dyh - yh * jnp.tile(_rowsum_mxu(dyh * yh) * (1.0 / 512), (1, 4)))
        dY_all = dyg * sz
        dz_ref[...] = (dyg * yv * (sg * (1.0 + zv * (1.0 - sg)))).astype(dz_ref.dtype)

        yield
        raw, dt, a, dsk, cs, csT, ecs, eend, dec = _ssd_common(dt_ref, par_ref)
        Bb = b_ref[...].astype(BF16)
        Cb = c_ref[...].astype(BF16)
        G = lax.dot_general(Cb, Bb, _NT, preferred_element_type=F32)
        row = lax.broadcasted_iota(jnp.int32, (L, L), 0)
        col = lax.broadcasted_iota(jnp.int32, (L, L), 1)
        tril = col <= row
        lo = lax.broadcasted_iota(jnp.int32, (L, 128), 1) < 64
        lane1 = lax.broadcasted_iota(jnp.int32, (1, 128), 1)
        lo1 = lane1 < 64
        rowl = lax.broadcasted_iota(jnp.int32, (L, 128), 0)
        dt_x, ecs_x, eend_x = (_heads_to_pairs(m) for m in (dt, ecs, eend))
        dG = jnp.zeros((L, L), F32)
        dB = jnp.zeros((L, SSM_STATE), F32)
        dC = jnp.zeros((L, SSM_STATE), F32)
        dcs_t = jnp.zeros((L, L), F32)
        tails = jnp.zeros((1, 128), F32)
        dD_row = jnp.zeros((1, 128), F32)
        v_parts, prod_parts = [], []

        def tot(m):
            return jnp.sum(jnp.sum(m, axis=0, keepdims=True), axis=1, keepdims=True)

        for pp in range(4):
            hA, hB = 2 * pp, 2 * pp + 1
            lanes = slice(pp * 128, (pp + 1) * 128)

            def sel1(m):
                return jnp.where(lo1, m[:, hA:hA + 1], m[:, hB:hB + 1])

            X = xs_ref[:, lanes]
            dY = dY_all[:, lanes]
            dtsel = dt_x[:, lanes]
            xd = X * dtsel
            xdb = xd.astype(BF16)
            dYb = dY.astype(BF16)
            Hp = st_ref[pp]
            Hb = Hp.astype(BF16)
            dHn = dh_ref[pp]
            dHb = dHn.astype(BF16)
            ecs_sel = ecs_x[:, lanes]
            eend_sel = eend_x[:, lanes]
            dxd_state = jnp.dot(Bb, dHb, preferred_element_type=F32) * eend_sel
            yoff = jnp.dot(Cb, Hb, preferred_element_type=F32) * ecs_sel
            dYe = (dY * ecs_sel).astype(BF16)
            dC = dC + lax.dot_general(dYe, Hb, _NT, preferred_element_type=F32)
            dB = dB + lax.dot_general((xd * eend_sel).astype(BF16), dHb, _NT, preferred_element_type=F32)
            dh_ref[pp] = dHn * sel1(dec) + lax.dot_general(Cb, dYe, _TN, preferred_element_type=F32)
            q = xd * dxd_state
            dyq = dY * yoff - q
            qcol = jnp.sum(q, axis=0, keepdims=True)
            hcol = jnp.sum(dHn * Hp, axis=0, keepdims=True)
            dxd_diag = []
            for h, msk, msk1 in ((hA, lo, lo1), (hB, jnp.logical_not(lo), jnp.logical_not(lo1))):
                Lm = jnp.where(tril, jnp.exp(jnp.minimum(cs[:, h:h + 1] - csT[h:h + 1, :], 0.0)), 0.0)
                M = G * Lm
                dxd_diag.append(lax.dot_general(M.astype(BF16), dYb, _TN, preferred_element_type=F32))
                dM = lax.dot_general(jnp.where(msk, dY, 0.0).astype(BF16), xdb, _NT, preferred_element_type=F32)
                dG = dG + dM * Lm
                W = dM * M
                dcs_t = dcs_t + jnp.where(row == h, jnp.sum(W, axis=0, keepdims=True), 0.0)
                v_parts.append(W + jnp.where(msk, dyq, 0.0))
                tail = (jnp.sum(jnp.where(msk1, qcol, 0.0), axis=1, keepdims=True)
                        + dec[:, h:h + 1] * jnp.sum(jnp.where(msk1, hcol, 0.0), axis=1, keepdims=True))
                tails = tails + jnp.where(lane1 == h, tail, 0.0)
                yield
            dxd = jnp.where(lo, dxd_diag[0], dxd_diag[1]) + dxd_state
            prod_parts.append(dxd * X)
            dxs_ref[:, lanes] = dxd * dtsel + sel1(dsk) * dY
            dyx = jnp.sum(dY * X, axis=0, keepdims=True)
            sA = jnp.sum(jnp.where(lo1, dyx, 0.0), axis=1, keepdims=True)
            sB = jnp.sum(dyx, axis=1, keepdims=True) - sA
            dD_row = dD_row + jnp.where(lane1 == hA, sA, 0.0) + jnp.where(lane1 == hB, sB, 0.0)
            yield
        dGb = dG.astype(BF16)
        db_ref[...] = dB + lax.dot_general(dGb, Cb, _TN, preferred_element_type=F32)
        dc_ref[...] = dC + jnp.dot(dGb, Bb, preferred_element_type=F32)
        dcs_mat = _lane_block_sums(jnp.concatenate(v_parts, axis=1), 128) + jnp.where(rowl == L - 1, tails, 0.0)
        ddt_mat = _lane_block_sums(jnp.concatenate(prod_parts, axis=1), 64)
        dad = _rcumsum_rows(dcs_mat - dcs_t.T)
        draw = (a * dad + ddt_mat) * _sigmoid(raw)
        ddt_ref[...] = draw
        dpar_ref[0:1, :] += jnp.sum(draw, axis=0, keepdims=True)
        dpar_ref[1:2, :] += jnp.sum(dt * dad, axis=0, keepdims=True) * a
        dpar_ref[2:3, :] += dD_row

    return pl.pallas_call(
        body, name=name, grid=(nc,), in_specs=in_specs,
        out_specs=[pl.BlockSpec((L, CONV_DIM), lambda c: (ci(c), 0)),
                   pl.BlockSpec((L, D_INNER), lambda c: (ci(c), 0)),
                   pl.BlockSpec((SSM_GROUPS, L, 128), lambda c: (0, ci(c), 0)),
                   pl.BlockSpec((1, D_INNER), lambda c: (0, 0)),
                   pl.BlockSpec((SSM_GROUPS, 8, 128), lambda c: (0, 0, 0))],
        out_shape=[jax.ShapeDtypeStruct((T, CONV_DIM), F32), jax.ShapeDtypeStruct((T, IN_PROJ_PAD), BF16),
                   jax.ShapeDtypeStruct((SSM_GROUPS, T, 128), F32), jax.ShapeDtypeStruct((1, D_INNER), F32),
                   jax.ShapeDtypeStruct((SSM_GROUPS, 8, 128), F32)],
        scratch_shapes=[pltpu.VMEM((SSM_GROUPS, 4, 128, 128), F32)],
        compiler_params=_cparams(("arbitrary",)))(xbc_c, xbc_c, xbc_c, dtg, par, zx, gnw, y, st, dyn)


SB_KEYS = 512
SB_SCAN = 256
SB_STRIP = 256


def _tri(width, cond):
    kk = lax.broadcasted_iota(jnp.int32, (width, width), 0)
    jj = lax.broadcasted_iota(jnp.int32, (width, width), 1)
    return cond(kk, jj).astype(BF16)


_LOG2E = 1.4426950408889634


def _softplus2(z2):
    return jnp.maximum(z2, 0.0) + jnp.log2(1.0 + jnp.exp2(-jnp.abs(z2)))


def _sba_sub_fwd(zb, c, U, mask):
    z2 = zb * _LOG2E
    s = _softplus2(z2)
    if mask is not None:
        s = jnp.where(mask, s, 0.0)
    R = c + jnp.dot(s.astype(BF16), U, preferred_element_type=F32)
    A = jnp.exp2(z2 - s - R)
    if mask is not None:
        A = jnp.where(mask, A, 0.0)
    return A.astype(BF16), R[:, 0:1] + s[:, 0:1]


def _sba_sub_bwd(zb, dAb, Lt, pc, pe, Uincl, Uexcl, mask):
    last = zb.shape[1] - 1
    z2 = zb * _LOG2E
    s = _softplus2(z2)
    g = z2 - s
    if mask is not None:
        s = jnp.where(mask, s, 0.0)
    P = pc + jnp.dot(s.astype(BF16), Uincl, preferred_element_type=F32)
    A = jnp.exp2(g - (Lt - P))
    if mask is not None:
        A = jnp.where(mask, A, 0.0)
    E = dAb * A
    PE = pe + jnp.dot(E.astype(BF16), Uexcl, preferred_element_type=F32)
    dz = E - jnp.exp2(g) * (E + PE)
    if mask is not None:
        dz = jnp.where(mask, dz, 0.0)
    return (A.astype(BF16), dz.astype(BF16), P[:, last:last + 1], PE[:, last:last + 1] + E[:, last:last + 1])


def _stack_heads(v):
    lo = lax.broadcasted_iota(jnp.int32, v.shape, 1) < 64
    zero = jnp.zeros_like(v)
    return jnp.concatenate([jnp.where(lo, v, zero), jnp.where(lo, zero, v)], axis=0)


def _unstack_heads(v):
    lo = lax.broadcasted_iota(jnp.int32, (SB_BLOCK, 128), 1) < 64
    return jnp.where(lo, v[:SB_BLOCK], v[SB_BLOCK:])


def _sba_rows(a):
    return slice(2 * a * SB_BLOCK, 2 * (a + 1) * SB_BLOCK)


def _sba_diag_case(a, b):
    Bq = SB_BLOCK
    if b * SB_SCAN >= (a + 1) * Bq:
        return "skip"
    if (b + 1) * SB_SCAN <= a * Bq:
        return "full"
    rowi = lax.broadcasted_iota(jnp.int32, (2 * Bq, SB_SCAN), 0)
    qpos = a * Bq + jnp.where(rowi >= Bq, rowi - Bq, rowi)
    return b * SB_SCAN + lax.broadcasted_iota(jnp.int32, (2 * Bq, SB_SCAN), 1) < qpos


def _sba_fwd(q, kv, *, name):
    T = q.shape[0]
    Bq = SB_BLOCK
    nsub = SB_KEYS // Bq
    nscan = SB_KEYS // SB_SCAN
    R = 2 * SB_KEYS
    assert T % SB_KEYS == 0 and SB_STRIP == 2 * Bq
    scale = 1.0 / math.sqrt(SB_HEAD_DIM)

    def body(q_ref, k_ref, v_ref, o_ref, lt_ref, z_s, a_s, c_s, acc_s):
        i = pl.program_id(1)
        U2 = _tri(SB_SCAN, lambda k, j: k > j)
        qs_all = jnp.concatenate([_stack_heads(q_ref[a * Bq:(a + 1) * Bq, :] * scale) for a in range(nsub)], axis=0)
        c_s[...] = jnp.zeros_like(c_s)
        acc_s[...] = jnp.zeros_like(acc_s)

        def scores(J, slot):
            off = pl.multiple_of(J * SB_KEYS, SB_KEYS)
            z_s[slot] = lax.dot_general(qs_all, k_ref[pl.ds(off, SB_KEYS), :], _NT, preferred_element_type=F32)

        def weights(slot, diag):
            for a in range(nsub):
                rows = _sba_rows(a)
                c = c_s[rows, :]
                for b in reversed(range(nscan)):
                    cols = slice(b * SB_SCAN, (b + 1) * SB_SCAN)
                    case = _sba_diag_case(a, b) if diag else "full"
                    if isinstance(case, str) and case == "skip":
                        a_s[slot, rows, cols] = jnp.zeros((2 * Bq, SB_SCAN), BF16)
                        continue
                    A, c = _sba_sub_fwd(z_s[slot, rows, cols], c, U2, None if isinstance(case, str) else case)
                    a_s[slot, rows, cols] = A
                c_s[rows, :] = c

        def values(J, slot):
            off = pl.multiple_of(J * SB_KEYS, SB_KEYS)
            acc_s[...] += jnp.dot(a_s[slot], v_ref[pl.ds(off, SB_KEYS), :], preferred_element_type=F32)

        scores(i, 0)
        weights(0, True)
        scores(jnp.maximum(i - 1, 0), 1)

        def two_steps(u, _):
            t = 2 * u + 1
            weights(1, False)
            scores(jnp.maximum(i - t - 1, 0), 0)
            values(i - t + 1, 0)
            weights(0, False)
            scores(jnp.maximum(i - t - 2, 0), 1)
            values(i - t, 1)
            return 0

        lax.fori_loop(0, i // 2, two_steps, 0)
        odd = lax.rem(i, 2) == 1

        @pl.when(jnp.logical_not(odd))
        def _():
            values(0, 0)

        @pl.when(odd)
        def _():
            weights(1, False)
            values(1, 0)
            values(0, 1)
        for a in range(nsub):
            o_ref[a * Bq:(a + 1) * Bq, :] = _unstack_heads(acc_s[_sba_rows(a), :]).astype(BF16)
            lt_ref[a * Bq:(a + 1) * Bq, :] = _unstack_heads(jnp.broadcast_to(c_s[_sba_rows(a), :], (2 * Bq, 128)))

    return pl.pallas_call(
        body, name=name, grid=(SB_HEADS // 2, T // SB_KEYS),
        in_specs=[pl.BlockSpec((SB_KEYS, 128), lambda p, i: (i, p)), pl.BlockSpec((T, 128), lambda p, i: (0, p)),
                  pl.BlockSpec((T, 128), lambda p, i: (0, p + SB_HEADS // 2))],
        out_specs=[pl.BlockSpec((SB_KEYS, 128), lambda p, i: (i, p)),
                   pl.BlockSpec((None, SB_KEYS, 128), lambda p, i: (p, i, 0))],
        out_shape=[jax.ShapeDtypeStruct((T, D_MODEL), BF16), jax.ShapeDtypeStruct((SB_HEADS // 2, T, 128), F32)],
        scratch_shapes=[pltpu.VMEM((2, R, SB_KEYS), F32), pltpu.VMEM((2, R, SB_KEYS), BF16),
                        pltpu.VMEM((R, 1), F32), pltpu.VMEM((R, 128), F32)],
        compiler_params=_cparams(("parallel", "parallel")))(q, kv, kv)


def _sba_bwd(q, kv, lt, do, *, name):
    T = q.shape[0]
    Bq = SB_BLOCK
    nq = T // SB_KEYS
    nsub = SB_KEYS // Bq
    nscan = SB_KEYS // SB_SCAN
    R = 2 * SB_KEYS
    assert T % SB_KEYS == 0 and SB_STRIP == 2 * Bq
    scale = 1.0 / math.sqrt(SB_HEAD_DIM)

    def body(q_ref, k_ref, v_ref, lt_ref, do_ref, dq_ref, dk_ref, dv_ref, dk_acc, dv_acc,
             z_s, da_s, a_s, dz_s, pc_s, pe_s, lt_s, dq_s):
        i = pl.program_id(1)

        @pl.when(i == 0)
        def _():
            dk_acc[...] = jnp.zeros_like(dk_acc)
            dv_acc[...] = jnp.zeros_like(dv_acc)

        Uincl = _tri(SB_SCAN, lambda k, j: k <= j)
        Uexcl = _tri(SB_SCAN, lambda k, j: k < j)
        qs, dos = [], []
        for a in range(nsub):
            rows = slice(a * Bq, (a + 1) * Bq)
            qs.append(_stack_heads(q_ref[rows, :] * scale))
            dos.append(_stack_heads(do_ref[rows, :]))
            lt_s[_sba_rows(a), :] = jnp.concatenate([lt_ref[rows, 0:1], lt_ref[rows, 64:65]], axis=0)
        qs_all = jnp.concatenate(qs, axis=0)
        dos_all = jnp.concatenate(dos, axis=0)
        pc_s[...] = jnp.zeros_like(pc_s)
        pe_s[...] = jnp.zeros_like(pe_s)
        a_s[1] = jnp.zeros((R, SB_KEYS), BF16)
        dz_s[1] = jnp.zeros((R, SB_KEYS), BF16)

        def scores(J, slot):
            off = pl.multiple_of(J * SB_KEYS, SB_KEYS)
            z_s[slot] = lax.dot_general(qs_all, k_ref[pl.ds(off, SB_KEYS), :], _NT, preferred_element_type=F32)
            da_s[slot] = lax.dot_general(dos_all, v_ref[pl.ds(off, SB_KEYS), :], _NT, preferred_element_type=F32)

        def gradients(slot, diag):
            for a in range(nsub):
                rows = _sba_rows(a)
                pc, pe, Lt = pc_s[rows, :], pe_s[rows, :], lt_s[rows, :]
                for b in range(nscan):
                    cols = slice(b * SB_SCAN, (b + 1) * SB_SCAN)
                    case = _sba_diag_case(a, b) if diag else "full"
                    if isinstance(case, str) and case == "skip":
                        a_s[slot, rows, cols] = jnp.zeros((2 * Bq, SB_SCAN), BF16)
                        dz_s[slot, rows, cols] = jnp.zeros((2 * Bq, SB_SCAN), BF16)
                        continue
                    A, dz, pc, pe = _sba_sub_bwd(z_s[slot, rows, cols], da_s[slot, rows, cols], Lt, pc, pe, Uincl, Uexcl,
                                                 None if isinstance(case, str) else case)
                    a_s[slot, rows, cols] = A
                    dz_s[slot, rows, cols] = dz
                pc_s[rows, :] = pc
                pe_s[rows, :] = pe

        def products(J, slot):
            off = pl.multiple_of(J * SB_KEYS, SB_KEYS)
            dzt = dz_s[slot]
            dk_acc[pl.ds(off, SB_KEYS), :] += lax.dot_general(dzt, qs_all, _TN, preferred_element_type=F32)
            dv_acc[pl.ds(off, SB_KEYS), :] += lax.dot_general(a_s[slot], dos_all, _TN, preferred_element_type=F32)
            dq_s[...] += jnp.dot(dzt, k_ref[pl.ds(off, SB_KEYS), :], preferred_element_type=F32)

        dq_s[...] = jnp.zeros_like(dq_s)
        scores(0, 0)

        def two_steps(u, _):
            t = 2 * u
            gradients(0, False)
            scores(t + 1, 1)
            products(jnp.maximum(t - 1, 0), 1)
            gradients(1, False)
            scores(t + 2, 0)
            products(t, 0)
            return 0

        lax.fori_loop(0, i // 2, two_steps, 0)
        odd = lax.rem(i, 2) == 1

        @pl.when(jnp.logical_not(odd))
        def _():
            gradients(0, True)
            products(jnp.maximum(i - 1, 0), 1)
            products(i, 0)

        @pl.when(odd)
        def _():
            gradients(0, False)
            scores(i, 1)
            products(jnp.maximum(i - 2, 0), 1)
            gradients(1, True)
            products(i - 1, 0)
            products(i, 1)

        for a in range(nsub):
            dq_ref[a * Bq:(a + 1) * Bq, :] = (_unstack_heads(dq_s[_sba_rows(a), :]) * scale).astype(BF16)

        @pl.when(i == nq - 1)
        def _():
            dk_ref[...] = dk_acc[...].astype(BF16)
            dv_ref[...] = dv_acc[...].astype(BF16)

    return pl.pallas_call(
        body, name=name, grid=(SB_HEADS // 2, nq),
        in_specs=[pl.BlockSpec((SB_KEYS, 128), lambda p, i: (i, p)), pl.BlockSpec((T, 128), lambda p, i: (0, p)),
                  pl.BlockSpec((T, 128), lambda p, i: (0, p + SB_HEADS // 2)),
                  pl.BlockSpec((None, SB_KEYS, 128), lambda p, i: (p, i, 0)),
                  pl.BlockSpec((SB_KEYS, 128), lambda p, i: (i, p))],
        out_specs=[pl.BlockSpec((SB_KEYS, 128), lambda p, i: (i, p)), pl.BlockSpec((T, 128), lambda p, i: (0, p)),
                   pl.BlockSpec((T, 128), lambda p, i: (0, p))],
        out_shape=[jax.ShapeDtypeStruct((T, D_MODEL), BF16), jax.ShapeDtypeStruct((T, D_MODEL), BF16),
                   jax.ShapeDtypeStruct((T, D_MODEL), BF16)],
        scratch_shapes=[pltpu.VMEM((T, 128), F32), pltpu.VMEM((T, 128), F32),
                        pltpu.VMEM((2, R, SB_KEYS), F32), pltpu.VMEM((2, R, SB_KEYS), F32),
                        pltpu.VMEM((2, R, SB_KEYS), BF16), pltpu.VMEM((2, R, SB_KEYS), BF16),
                        pltpu.VMEM((R, 1), F32), pltpu.VMEM((R, 1), F32), pltpu.VMEM((R, 1), F32),
                        pltpu.VMEM((R, 128), F32)],
        compiler_params=_cparams(("parallel", "arbitrary")))(q, kv, kv, lt, do)


def _loss_head(h, tgt, w, *, name, tt=512):
    T, D = h.shape
    tt = min(tt, T)

    def body(h_ref, t_ref, w_ref, loss_ref, dh_ref, dw_ref):
        i = pl.program_id(0)
        hv = h_ref[...]
        wv = w_ref[...]
        r = lax.rsqrt(jnp.mean(hv * hv, axis=-1, keepdims=True) + EPS)
        xhat = hv * r
        err = xhat * wv - t_ref[...]
        part = 0.5 * jnp.sum(jnp.mean(err * err, axis=-1, keepdims=True), axis=0, keepdims=True)
        dy = err * (1.0 / D)
        dxh = dy * wv
        dh_ref[...] = r * (dxh - xhat * jnp.mean(dxh * xhat, axis=-1, keepdims=True))
        dwc = jnp.sum(dy * xhat, axis=0, keepdims=True)

        @pl.when(i == 0)
        def _():
            loss_ref[...] = jnp.broadcast_to(part, loss_ref.shape)
            dw_ref[...] = dwc

        @pl.when(i > 0)
        def _():
            loss_ref[...] += jnp.broadcast_to(part, loss_ref.shape)
            dw_ref[...] += dwc

    return pl.pallas_call(
        body, name=name, grid=(T // tt,),
        in_specs=[pl.BlockSpec((tt, D), lambda i: (i, 0)), pl.BlockSpec((tt, D), lambda i: (i, 0)),
                  pl.BlockSpec((1, D), lambda i: (0, 0))],
        out_specs=[pl.BlockSpec((1, 128), lambda i: (0, 0)), pl.BlockSpec((tt, D), lambda i: (i, 0)),
                   pl.BlockSpec((1, D), lambda i: (0, 0))],
        out_shape=[jax.ShapeDtypeStruct((1, 128), F32), jax.ShapeDtypeStruct((T, D), F32),
                   jax.ShapeDtypeStruct((1, D), F32)],
        compiler_params=_cparams(("arbitrary",)))(h, tgt, w.reshape(1, D))


def _adamw(parts, w, m, v, *, name, tr=256, tc=None):
    plist = list(parts) if isinstance(parts, (list, tuple)) else [parts]
    P, _, C = plist[0].shape
    R = sum(a.shape[1] for a in plist)
    tr = min(tr, R)
    tc = C if tc is None else tc
    assert all(a.shape[1] % tr == 0 for a in plist) and C % tc == 0, (name, R, C, tr, tc)
    nbs = [a.shape[1] // tr for a in plist]
    offs = [sum(nbs[:l]) for l in range(len(nbs))]
    c1 = 1.0 - ADAM_B1 ** ADAM_STEP
    c2 = 1.0 - ADAM_B2 ** ADAM_STEP

    def body(*refs):
        p_refs = refs[:len(plist)]
        w_ref, m_ref, v_ref, g_ref, d_ref, nm_ref, nv_ref = refs[len(plist):]
        i = pl.program_id(0)
        g = None
        for l, p_ref in enumerate(p_refs):
            gl = p_ref[0].astype(F32)
            for k in range(1, P):
                gl = gl + p_ref[k].astype(F32)
            g = gl if g is None else jnp.where(i >= offs[l], gl, g)
        mn = ADAM_B1 * m_ref[...] + (1.0 - ADAM_B1) * g
        vn = ADAM_B2 * v_ref[...] + (1.0 - ADAM_B2) * (g * g)
        g_ref[...] = g
        nm_ref[...] = mn
        nv_ref[...] = vn
        d_ref[...] = -ADAM_LR * ((mn / c1) / (jnp.sqrt(vn / c2) + ADAM_EPS) + ADAM_WD * w_ref[...])

    spec = pl.BlockSpec((tr, tc), lambda i, j: (i, j))
    sds = jax.ShapeDtypeStruct((R, C), F32)
    return pl.pallas_call(
        body, name=name, grid=(R // tr, C // tc),
        in_specs=[pl.BlockSpec((P, tr, tc), functools.partial(lambda i, j, o, n: (0, jnp.clip(i - o, 0, n - 1), j), o=o, n=n))
                  for o, n in zip(offs, nbs)] + [spec, spec, spec],
        out_specs=[spec, spec, spec, spec], out_shape=[sds, sds, sds, sds],
        compiler_params=_cparams(("parallel", "parallel")))(*plist, w, m, v)


def _all_gather(shards, *, name):
    n = len(shards)

    def body(*refs):
        ins, outs = refs[:n], refs[n:2 * n]
        send_sems, recv_sems, local_sems = refs[2 * n:]
        x, y, c = lax.axis_index("x"), lax.axis_index("y"), lax.axis_index("c")
        me, sib = (x, y, c), (x, y, 1 - c)
        chips = [(1 - x, y), (x, 1 - y), (1 - x, 1 - y)]

        def slot(p):
            return 4 * p[0] + 2 * p[1] + p[2]

        def cp(a, k, block, to, src=None):
            dst = outs[a].at[slot(block)]
            return pltpu.make_async_remote_copy(src_ref=dst if src is None else src, dst_ref=dst,
                                                send_sem=send_sems.at[a, k], recv_sem=recv_sems.at[a, k],
                                                device_id=to, device_id_type=_MESH)

        mine = [pltpu.make_async_copy(ins[a], outs[a].at[slot(me)], local_sems.at[a]) for a in range(n)]
        for m in mine:
            m.start()
        first = []
        for a in range(n):
            first.append(cp(a, 0, me, sib, src=ins[a]))
            for j, chip in enumerate(chips):
                first.append(cp(a, 1 + j, me, (*chip, c), src=ins[a]))
        for f in first:
            f.start()
        passed = []
        for j, chip in enumerate(chips):
            for a in range(n):
                cp(a, 1 + j, (*chip, c), me).wait_recv()
                f = cp(a, 4 + j, (*chip, c), sib)
                f.start()
                passed.append(f)
        for a in range(n):
            cp(a, 0, sib, me).wait_recv()
            for j, chip in enumerate(chips):
                cp(a, 4 + j, (*chip, 1 - c), me).wait_recv()
        for f in first + passed:
            f.wait_send()
        for m in mine:
            m.wait()

    return pl.pallas_call(
        body, name=name, in_specs=[_ANY] * n, out_specs=[_ANY] * n,
        out_shape=[jax.ShapeDtypeStruct((N_DEV,) + s.shape, s.dtype) for s in shards],
        scratch_shapes=[pltpu.SemaphoreType.DMA((n, 7)), pltpu.SemaphoreType.DMA((n, 7)),
                        pltpu.SemaphoreType.DMA((n,))])(*shards)


_HBM = pl.BlockSpec(memory_space=pltpu.HBM)
_SEM = pl.BlockSpec(memory_space=pltpu.SEMAPHORE)
_EFFECT = pltpu.SideEffectType.DATAFLOW_SIDE_EFFECTING


def _peers():
    x, y, c = lax.axis_index("x"), lax.axis_index("y"), lax.axis_index("c")
    out = []
    for r in range(1, N_DEV):
        px = 1 - x if (r >> 2) & 1 else x
        py = 1 - y if (r >> 1) & 1 else y
        pc = 1 - c if r & 1 else c
        out.append(((px, py, pc), 4 * px + 2 * py + pc))
    return 4 * x + 2 * y + c, out


def _push_copy(src_ref, land_ref, send_sems, recv_sems, a, k, me, peer, peer_slot, scatter, arriving):
    src = src_ref.at[peer_slot] if scatter else src_ref
    return pltpu.make_async_remote_copy(
        src_ref=src, dst_ref=land_ref.at[peer_slot if arriving else me], send_sem=send_sems.at[a * (N_DEV - 1) + k],
        recv_sem=recv_sems.at[a * (N_DEV - 1) + k], device_id=peer, device_id_type=_MESH)


def _push_start(srcs, *, scatter, name):
    n = len(srcs)
    lands = [lax.empty(s.shape if scatter else (N_DEV,) + s.shape, s.dtype) for s in srcs]

    def body(*refs):
        src_refs, land_refs = refs[:n], refs[n:2 * n]
        send_sems, recv_sems = refs[2 * n], refs[2 * n + 1]
        token = refs[-1]
        me, peers = _peers()
        for k, (peer, slot) in enumerate(peers):
            for a in range(n):
                _push_copy(src_refs[a], land_refs[a], send_sems, recv_sems, a, k, me, peer, slot, scatter, False).start()
        token[...] = jnp.zeros_like(token)

    hbm = lambda a: pltpu.HBM(a.shape, a.dtype)
    outs = pl.pallas_call(
        body, name=name,
        out_shape=(pltpu.SemaphoreType.DMA((n * (N_DEV - 1),)), pltpu.SemaphoreType.DMA((n * (N_DEV - 1),)),
                   *[hbm(s) for s in srcs], *[hbm(l) for l in lands], jax.ShapeDtypeStruct((8, 128), F32)),
        in_specs=[_HBM] * (2 * n),
        out_specs=(_SEM, _SEM, *([_HBM] * (2 * n)), pl.BlockSpec(memory_space=pltpu.VMEM)),
        input_output_aliases={i: 2 + i for i in range(2 * n)},
        compiler_params=pltpu.CompilerParams(has_side_effects=_EFFECT),
    )(*[pltpu.with_memory_space_constraint(s, pltpu.HBM) for s in srcs],
      *[pltpu.with_memory_space_constraint(l, pltpu.HBM) for l in lands])
    return dict(send=outs[0], recv=outs[1], srcs=list(outs[2:2 + n]), lands=list(outs[2 + n:2 + 2 * n]),
                token=outs[-1], scatter=scatter, n=n)


def _push_wait(h, after, *, name):
    n, scatter = h["n"], h["scatter"]

    def body(*refs):
        src_refs, land_refs = refs[:n], refs[n:2 * n]
        send_sems, recv_sems = refs[2 * n], refs[2 * n + 1]
        me, peers = _peers()
        for k, (peer, slot) in enumerate(peers):
            for a in range(n):
                cp = _push_copy(src_refs[a], land_refs[a], send_sems, recv_sems, a, k, me, peer, slot, scatter, True)
                cp.wait_send()
                cp.wait_recv()

    hbm = lambda a: pltpu.HBM(a.shape, a.dtype)
    outs = pl.pallas_call(
        body, name=name,
        out_shape=(*[hbm(s) for s in h["srcs"]], *[hbm(l) for l in h["lands"]]),
        in_specs=[_HBM] * (2 * n) + [_SEM, _SEM, _ANY], out_specs=tuple([_HBM] * (2 * n)),
        input_output_aliases={i: i for i in range(2 * n)},
        compiler_params=pltpu.CompilerParams(has_side_effects=_EFFECT),
    )(*h["srcs"], *h["lands"], h["send"], h["recv"], after)
    return list(outs[:n]), list(outs[n:])


def _ffn_fwd(h, nw, w_up, conv_w, conv_b, w_down, tag):
    a3 = _mm_fwd(h, w_up, norm_w=nw, name=f"ffn{tag}_up", out_dtype=BF16, halves=True, w_t=True, tm=1024, tn=2816)
    p = _ffn_conv_fwd3(a3, conv_w, conv_b.reshape(1, -1), name=f"ffn{tag}_conv")
    h_out = _mm_fwd(p, w_down, residual=h, name=f"ffn{tag}_down", tm=1024, tn=512)
    return h_out, (a3, p)


def _ffn_bwd(dh, h, saved, nw, w_up, conv_w, conv_b, w_down, tag):
    a3, p = saved
    g_down = _mm_tn(p, dh, name=f"ffn{tag}_down_wg", tk1=1408, tn=1024)
    dp = _mm_nt(dh, w_down, name=f"ffn{tag}_down_dg", out_dtype=BF16, tm=512, tn=2816, tk=1024)
    dhid3, dw3, db3 = _ffn_conv_bwd3(a3, conv_w, conv_b.reshape(1, -1), dp, name=f"ffn{tag}_conv_bwd")
    da3 = _conv_bwd_in3(dhid3, conv_w, K=FFN_CONV, name=f"ffn{tag}_conv_bwd_in")
    g_up = _mm_tn_t(da3, h, norm_w=nw, name=f"ffn{tag}_up_wg", tn=1408, tt=1024)
    dh_out, g_nw = _mm_nt(da3, w_up, epi=(h, nw, dh), name=f"ffn{tag}_up_dg", w_t=True, tm=1024, tk=1408)
    g_cw = jnp.concatenate([dw3[0], dw3[1]], axis=1)
    g_cb = jnp.concatenate([db3[0], db3[1]], axis=1)
    return dh_out, dict(norm=g_nw.reshape(-1), up=g_up, conv_w=g_cw, conv_b=g_cb.reshape(-1), down=g_down)


_BIG = ["ssm_in_w", "ssm_out_w", "w_k", "w_v", "w_q", "w_o", "ffn_up_w", "ffn_down_w"]
_SMALL_SHARDED = ["ssm_norm_w", "ssm_conv_w", "ssm_conv_b", "ssm_gate_norm_w", "ffn_conv_w"]
_SMALL_REPL = ["ssm_dt_bias", "ssm_a_log", "ssm_d", "kv_norm_w", "attn_norm_w", "ffn_norm_w", "ffn_conv_b",
               "final_norm_w"]
_WEIGHTS = ["ssm_norm_w", "ssm_in_w", "ssm_conv_w", "ssm_conv_b", "ssm_dt_bias", "ssm_a_log", "ssm_d",
            "ssm_gate_norm_w", "ssm_out_w", "kv_norm_w", "w_k", "w_v", "attn_norm_w", "w_q", "w_o", "ffn_norm_w",
            "ffn_up_w", "ffn_conv_w", "ffn_conv_b", "ffn_down_w", "final_norm_w"]


def _as2d(a):
    return a.reshape(-1, a.shape[-1])


def _cols_to_full(g):
    return g.transpose(1, 0, 2).reshape(g.shape[1], N_DEV * g.shape[2])


def _pack_small(vals):
    flat = jnp.concatenate([v.reshape(-1).astype(F32) for v in vals])
    n = flat.shape[0]
    rows = -(-n // 1024) * 8
    return jnp.pad(flat, (0, rows * 128 - n)).reshape(rows, 128)


def _unpack_small(packed, shapes):
    flat = packed.reshape(-1)
    out, off = [], 0
    for s in shapes:
        n = math.prod(s)
        out.append(flat[off:off + n].reshape(s))
        off += n
    return out


def _tie(a, token):
    return a + token[0, 0].astype(a.dtype)


def _local_step(x, tgt, get_w, put_g):
    T = x.shape[0]
    Ws = get_w("ssm", None)
    fnw, fcw, fcb = Ws["ffn_norm_w"], Ws["ffn_conv_w"], Ws["ffn_conv_b"]
    zx = _mm_fwd(x, Ws["in_w"], norm_w=Ws["ssm_norm_w"], name="ssm_in", w_t=True, tm=1024, tn=1792)
    xbc_c = _ssm_conv_fwd(zx, Ws["ssm_conv_w"], Ws["ssm_conv_b"].reshape(1, -1), name="ssm_conv")
    dt_raw = zx[:, D_INNER + CONV_DIM:IN_PROJ_DIM]
    dtg = jnp.pad(dt_raw.reshape(T, SSM_GROUPS, 8).transpose(1, 0, 2), ((0, 0), (0, 0), (0, 120)))
    par = jnp.stack([Ws["ssm_dt_bias"].reshape(SSM_GROUPS, 8), Ws["ssm_a_log"].reshape(SSM_GROUPS, 8),
                     Ws["ssm_d"].reshape(SSM_GROUPS, 8)], axis=1)
    par = jnp.pad(par, ((0, 0), (0, 5), (0, 120)))
    gnw = _tie(Ws["ssm_gate_norm_w"].reshape(1, D_INNER), get_w("rest_start", xbc_c))
    y, yn, st = _ssd_fwd(xbc_c, zx, dtg, par, gnw, name="ssd_fwd")
    W0 = get_w("ffn0", y)
    Ws["ssm_out_w"] = W0["ssm_out_w"]
    h1 = _mm_fwd(yn, Ws["ssm_out_w"], residual=x, name="ssm_out", tm=1024, tn=512)
    h2, ffn0 = _ffn_fwd(h1, fnw[0], W0["up"], fcw[0], fcb[0], W0["down"], "0")
    Wr = get_w("rest", h2)
    q = _mm_fwd(h2, Wr["w_q"], norm_w=Ws["attn_norm_w"], out_dtype=BF16, name="attn_q", tm=1024, tn=1024)
    kv = _mm_fwd(h2, Wr["w_kv"], norm_w=Ws["kv_norm_w"], out_dtype=BF16, name="attn_kv", tm=1024, tn=1024)
    o, lt = _sba_fwd(q, kv, name="sba_fwd")
    h3 = _mm_fwd(o, Wr["w_o"], residual=h2, name="attn_o", tm=1024, tn=512)
    h4, ffn1 = _ffn_fwd(h3, fnw[1], Wr["up"], fcw[1], fcb[1], Wr["down"], "1")
    loss, dh4, g_final = _loss_head(h4, tgt, Ws["final_norm_w"], name="loss_head")
    dh3, gf1 = _ffn_bwd(dh4, h3, ffn1, fnw[1], Wr["up"], fcw[1], fcb[1], Wr["down"], "1")
    tok = put_g("ffn1", dict(up=gf1["up"], down=gf1["down"]))
    g_wo = _mm_tn(o, dh3, name="attn_o_wg", tn=1024)
    do = _mm_nt(dh3, _tie(Wr["w_o"], tok), name="attn_o_dg", out_dtype=BF16, tn=1024, tk=1024)
    dq, dk, dv = _sba_bwd(q, kv, lt, do, name="sba_bwd")
    g_wq = _mm_tn(h2, dq, norm_w=Ws["attn_norm_w"], name="attn_q_wg", tn=1024, tt=1024)
    dh2a, g_attn_nw = _mm_nt(dq, Wr["w_q"], epi=(h2, Ws["attn_norm_w"], dh3), name="attn_q_dg", tm=1024, tk=1024)
    dkv = jnp.concatenate([dk, dv], axis=1)
    g_wkv = _mm_tn(h2, dkv, norm_w=Ws["kv_norm_w"], name="attn_kv_wg", tn=1024, tt=1024)
    dh2, g_kv_nw = _mm_nt(dkv, Wr["w_kv"], epi=(h2, Ws["kv_norm_w"], dh2a), name="attn_kv_dg", tm=1024, tk=1024)
    tok = put_g("attn", dict(w_o=g_wo, w_q=g_wq, w_k=g_wkv[:, :D_MODEL], w_v=g_wkv[:, D_MODEL:]))
    dh1, gf0 = _ffn_bwd(dh2, h1, ffn0, fnw[0], W0["up"], fcw[0], _tie(fcb[0], tok), W0["down"], "0")
    tok = put_g("ffn0", dict(up=gf0["up"], down=gf0["down"]))
    g_out = _mm_tn(yn, dh1, name="ssm_out_wg", tn=1024)
    dyn = _mm_nt(dh1, _tie(Ws["ssm_out_w"], tok), name="ssm_out_dg", out_dtype=BF16, tn=1024, tk=1024)
    tok = put_g("ssm_out", dict(ssm_out_w=g_out))
    dxbc_c, dz, ddt, g_gnw, dpar = _ssd_bwd(xbc_c, zx, dtg, par, _tie(gnw, tok), y, st, dyn, name="ssd_bwd")
    dhid, g_scw, g_scb = _ssm_conv_bwd_pre(zx, Ws["ssm_conv_w"], Ws["ssm_conv_b"].reshape(1, -1), dxbc_c,
                                           name="ssm_conv_bwd")
    dzx = _conv_bwd_in(dhid, Ws["ssm_conv_w"], K=SSM_CONV, name="ssm_conv_bwd_in", into=(dz, D_INNER))
    ddt_t = ddt[:, :, :8].transpose(1, 0, 2).reshape(T, SSM_HEADS).astype(BF16)
    dzx = _put_cols(dzx, jnp.pad(ddt_t, ((0, 0), (0, IN_PROJ_PAD - IN_PROJ_DIM))), D_INNER + CONV_DIM, name="ssm_ddt_cols")
    g_in = _mm_tn_t(dzx, x, norm_w=Ws["ssm_norm_w"], name="ssm_in_wg", tn=1792, tt=1024)
    tok = put_g("ssm_in", dict(ssm_in_w=g_in[:IN_PROJ_DIM]))
    dx, g_ssm_nw = _mm_nt(dzx, Ws["in_w"], epi=(x, _tie(Ws["ssm_norm_w"], tok), dh1), name="ssm_in_dg", w_t=True,
                          tm=1024, tk=1792)
    f = {
        "ssm_norm_w": g_ssm_nw.reshape(-1), "ssm_conv_w": g_scw,
        "ssm_conv_b": g_scb.reshape(-1), "ssm_dt_bias": dpar[:, 0, :8].reshape(-1),
        "ssm_a_log": dpar[:, 1, :8].reshape(-1), "ssm_d": dpar[:, 2, :8].reshape(-1),
        "ssm_gate_norm_w": g_gnw.reshape(-1), "kv_norm_w": g_kv_nw.reshape(-1), "attn_norm_w": g_attn_nw.reshape(-1),
        "ffn_norm_w": jnp.stack([gf0["norm"], gf1["norm"]]), "ffn_conv_w": jnp.stack([gf0["conv_w"], gf1["conv_w"]]),
        "ffn_conv_b": jnp.stack([gf0["conv_b"], gf1["conv_b"]]), "final_norm_w": g_final.reshape(-1),
    }
    return loss, dx, f


def kernel(x, ssm_norm_w, ssm_in_w, ssm_conv_w, ssm_conv_b, ssm_dt_bias, ssm_a_log, ssm_d, ssm_gate_norm_w, ssm_out_w, kv_norm_w, w_k, w_v, attn_norm_w, w_q, w_o, ffn_norm_w, ffn_up_w, ffn_conv_w, ffn_conv_b, ffn_down_w, final_norm_w, loss_target, m_ssm_norm_w, m_ssm_in_w, m_ssm_conv_w, m_ssm_conv_b, m_ssm_dt_bias, m_ssm_a_log, m_ssm_d, m_ssm_gate_norm_w, m_ssm_out_w, m_kv_norm_w, m_w_k, m_w_v, m_attn_norm_w, m_w_q, m_w_o, m_ffn_norm_w, m_ffn_up_w, m_ffn_conv_w, m_ffn_conv_b, m_ffn_down_w, m_final_norm_w, v_ssm_norm_w, v_ssm_in_w, v_ssm_conv_w, v_ssm_conv_b, v_ssm_dt_bias, v_ssm_a_log, v_ssm_d, v_ssm_gate_norm_w, v_ssm_out_w, v_kv_norm_w, v_w_k, v_w_v, v_attn_norm_w, v_w_q, v_w_o, v_ffn_norm_w, v_ffn_up_w, v_ffn_conv_w, v_ffn_conv_b, v_ffn_down_w, v_final_norm_w):
    env = dict(locals())
    p = {n: env[n] for n in _WEIGHTS}
    mom = {n: env["m_" + n] for n in _WEIGHTS}
    var = {n: env["v_" + n] for n in _WEIGHTS}
    T = x.shape[1]
    me = 4 * lax.axis_index("x") + 2 * lax.axis_index("y") + lax.axis_index("c")
    rs = D_FF // N_DEV

    def bf2(a):
        return _as2d(a).astype(BF16)

    _T = ("ssm_in_w", "ffn_up_w")

    def t2d(a):
        return jnp.swapaxes(a, -1, -2).reshape(-1, a.shape[-2])

    def from_t2d(a, like):
        return jnp.swapaxes(a.reshape(like.shape[:-2] + (like.shape[-1], like.shape[-2])), -1, -2)

    n_in, n_up = p["ssm_in_w"].shape[-1], p["ffn_up_w"].shape[-1]

    def with_own(srcs, lands, scatter):
        out = []
        for s, l in zip(srcs, lands):
            own = lax.dynamic_index_in_dim(s, me, 0, keepdims=False) if scatter else s
            out.append(lax.dynamic_update_index_in_dim(l, own, me, 0))
        return out

    a_names = ["ssm_in_w"] + _SMALL_SHARDED
    got_a = dict(zip(a_names, _all_gather([t2d(p["ssm_in_w"]).astype(BF16)] + [_as2d(p[n]) for n in _SMALL_SHARDED],
                                          name="gather_ssm")))
    ffn0_names = ["ssm_out_w", "up0", "down0"]
    rest_names = ["w_q", "w_k", "w_v", "w_o", "up1", "down1"]
    up_t = jnp.swapaxes(p["ffn_up_w"], -1, -2).astype(BF16)
    shard = {"up0": up_t[0], "down0": bf2(p["ffn_down_w"][0]), "up1": up_t[1],
             "down1": bf2(p["ffn_down_w"][1]), "w_q": bf2(p["w_q"]), "w_k": bf2(p["w_k"]), "w_v": bf2(p["w_v"]),
             "w_o": bf2(p["w_o"]), "ssm_out_w": bf2(p["ssm_out_w"])}
    h_ffn0 = _push_start([shard[n] for n in ffn0_names], scatter=False, name="gather_ffn0_start")
    handles = {}

    def get_w(group, after):
        if group == "ssm":
            W = {n: p[n] for n in _SMALL_REPL}
            for n in ("ssm_dt_bias", "ssm_a_log", "ssm_d", "attn_norm_w"):
                W[n] = W[n].reshape(-1)
            W["in_w"] = jnp.pad(got_a["ssm_in_w"].reshape(IN_PROJ_DIM, D_MODEL), ((0, IN_PROJ_PAD - IN_PROJ_DIM), (0, 0)))
            W["ssm_norm_w"] = _tie(got_a["ssm_norm_w"].reshape(D_MODEL), h_ffn0["token"])
            W["ssm_conv_w"] = _cols_to_full(got_a["ssm_conv_w"])
            W["ssm_conv_b"] = got_a["ssm_conv_b"].reshape(CONV_DIM)
            W["ssm_gate_norm_w"] = got_a["ssm_gate_norm_w"].reshape(D_INNER)
            W["ffn_conv_w"] = _cols_to_full(got_a["ffn_conv_w"]).reshape(2, FFN_CONV, 2 * D_FF)
            return W
        if group == "rest_start":
            anchor = after[0, 0]
            first = shard[rest_names[0]] + (jnp.where(jnp.isfinite(anchor), anchor, 0.0) * 0.0).astype(BF16)
            handles["rest"] = _push_start([first] + [shard[n] for n in rest_names[1:]], scatter=False,
                                          name="gather_rest_start")
            return handles["rest"]["token"]
        if group == "ffn0":
            srcs, lands = _push_wait(h_ffn0, after, name="gather_ffn0_wait")
            out, up, down = with_own(srcs, lands, False)
            return dict(ssm_out_w=out.reshape(D_INNER, D_MODEL), up=up.reshape(2 * D_FF, D_MODEL),
                        down=down.reshape(D_FF, D_MODEL))
        srcs, lands = _push_wait(handles["rest"], after, name="gather_rest_wait")
        g = dict(zip(rest_names, with_own(srcs, lands, False)))
        sq = lambda a: a.reshape(D_MODEL, D_MODEL)
        return dict(w_q=sq(g["w_q"]), w_kv=jnp.concatenate([sq(g["w_k"]), sq(g["w_v"])], axis=1), w_o=sq(g["w_o"]),
                    up=g["up1"].reshape(2 * D_FF, D_MODEL), down=g["down1"].reshape(D_FF, D_MODEL))

    pending = []

    def put_g(group, g):
        if group in ("ffn0", "ffn1"):
            keys = [("ffn_up_w", int(group[-1])), ("ffn_down_w", int(group[-1]))]
            blocks = [g["up"].reshape(N_DEV, n_up, D_MODEL), g["down"].reshape(N_DEV, rs, D_MODEL)]
        elif group == "attn":
            keys = [(n, None) for n in ("w_o", "w_q", "w_k", "w_v")]
            blocks = [g[n].reshape(N_DEV, D_MODEL // N_DEV, D_MODEL) for n, _ in keys]
        elif group == "ssm_out":
            keys = [("ssm_out_w", None)]
            blocks = [g["ssm_out_w"].reshape(N_DEV, D_INNER // N_DEV, D_MODEL)]
        else:
            keys = [("ssm_in_w", None)]
            blocks = [g["ssm_in_w"].reshape(N_DEV, n_in, D_MODEL)]
        h = _push_start(blocks, scatter=True, name=f"exchange_{group}_start")
        pending.append((group, keys, h))
        return h["token"]

    loss_row, dx, f = _local_step(x.reshape(T, D_MODEL), loss_target.reshape(T, D_MODEL), get_w, put_g)

    small_names = _SMALL_REPL + _SMALL_SHARDED
    small_full = _pack_small([f[n] for n in small_names] + [loss_row[0, 0:1]])
    small_bcast = jnp.broadcast_to(small_full[None], (N_DEV,) + small_full.shape)
    h_small = _push_start([small_bcast], scatter=True, name="exchange_small_start")
    tok = h_small["token"]

    arrived, res = {}, {}
    after = dx
    for group, keys, h in pending:
        srcs, lands = _push_wait(h, after, name=f"exchange_{group}_wait")
        arrived.update(zip(keys, with_own(srcs, lands, True)))
        for n in _BIG:
            layered = (n, 0) in arrived or (n, 1) in arrived
            if n in res or not ((n, None) in arrived or ((n, 0) in arrived and (n, 1) in arrived)):
                continue
            parts = [arrived[(n, 0)], arrived[(n, 1)]] if layered else arrived[(n, None)]
            w2, m2, v2 = ((t2d if n in _T else _as2d)(a[n]) for a in (p, mom, var))
            if not res:
                w2 = _tie(w2, tok)
            tiles = {"ffn_down_w": dict(tr=rs), "ffn_up_w": dict(tr=n_up // 2), "ssm_in_w": dict(tr=n_in, tc=256)}
            res[n] = _adamw(parts, w2, m2, v2, name=f"adamw_{n}", **tiles.get(n, dict(tr=256)))
            after = res[n][0]
    srcs, lands = _push_wait(h_small, after, name="exchange_small_wait")
    small_parts = with_own(srcs, lands, True)[0]
    out_g, out_d, out_m, out_v = {}, {}, {}, {}
    for n in _BIG:
        out_g[n], out_d[n], out_m[n], out_v[n] = (from_t2d(t, p[n]) if n in _T else t.reshape(p[n].shape) for t in res[n])

    zero = jnp.zeros_like(small_full)
    g_small_sum = _adamw(small_parts, zero, zero, zero, name="sum_small_grads", tr=small_full.shape[0])[0]
    *small_sums, loss_sum = _unpack_small(g_small_sum, [f[n].shape for n in small_names] + [(1,)])
    loss = loss_sum[0]
    g_small = dict(zip(small_names, small_sums))
    for n in _SMALL_SHARDED:
        width = p[n].shape[-1]
        g_small[n] = lax.dynamic_slice_in_dim(g_small[n], me * width, width, axis=g_small[n].ndim - 1)
    sw = _pack_small([p[n] for n in small_names])
    sm = _pack_small([mom[n] for n in small_names])
    sv = _pack_small([var[n] for n in small_names])
    sg = _pack_small([g_small[n] for n in small_names])
    _, d, nm, nv = _adamw(sg[None], sw, sm, sv, name="adamw_small", tr=sw.shape[0])
    shard_shapes = [p[n].shape for n in small_names]
    for n, dd, mm, vv in zip(small_names, _unpack_small(d, shard_shapes), _unpack_small(nm, shard_shapes),
                             _unpack_small(nv, shard_shapes)):
        out_g[n] = g_small[n].reshape(p[n].shape)
        out_d[n], out_m[n], out_v[n] = dd, mm, vv

    return (loss, dx.reshape(x.shape), *[out_g[n] for n in _WEIGHTS], *[out_d[n] for n in _WEIGHTS],
            *[out_m[n] for n in _WEIGHTS], *[out_v[n] for n in _WEIGHTS])
```

```python
import functools
import math

import jax
import jax.numpy as jnp
from jax import lax
from jax.experimental import pallas as pl
from jax.experimental.pallas import tpu as pltpu

F32 = jnp.float32
BF16 = jnp.bfloat16
EPS = 1e-6

D_MODEL = 1024
D_INNER = 2048
SSM_HEADS = 32
SSM_GROUPS = 4
SSM_STATE = 128
SSM_CONV = 4
SSM_CHUNK = 128
GN = SSM_GROUPS * SSM_STATE
CONV_DIM = D_INNER + 2 * GN
IN_PROJ_DIM = D_INNER + CONV_DIM + SSM_HEADS
IN_PROJ_PAD = 5376
SB_HEADS = 16
SB_HEAD_DIM = 64
SB_BLOCK = 128
D_FF = 2816
FFN_CONV = 3
N_DEV = 8

ADAM_LR = 0.001
ADAM_B1 = 0.9
ADAM_B2 = 0.999
ADAM_EPS = 1e-08
ADAM_WD = 0.01
ADAM_STEP = 10

_MESH = pl.DeviceIdType.MESH
_NT = (((1,), (1,)), ((), ()))
_TN = (((0,), (0,)), ((), ()))
_ANY = pl.BlockSpec(memory_space=pl.ANY)


def _cparams(sem, vmem_mb=48):
    return pltpu.CompilerParams(dimension_semantics=sem, vmem_limit_bytes=vmem_mb * 1024 * 1024)


def _sigmoid(x):
    return 0.5 * jnp.tanh(0.5 * x) + 0.5


def _softplus(x):
    return jnp.maximum(x, 0.0) + jnp.log(1.0 + jnp.exp(-jnp.abs(x)))


def _rms_fwd(xv, w):
    r = lax.rsqrt(jnp.mean(xv * xv, axis=-1, keepdims=True) + EPS)
    return xv * r * w


def _mm_fwd(x, w, *, name, norm_w=None, residual=None, out_dtype=F32, tm=512, tn=512, halves=False, w_t=False):
    M, K = x.shape
    N = w.shape[0] if w_t else w.shape[1]
    tm, tn = min(tm, M), min(tn, N)
    assert M % tm == 0 and N % tn == 0, (name, M, N, tm, tn)
    if halves:
        nbh = N // 2 // tn
        assert N // 2 % tn == 0
        out_spec = pl.BlockSpec((None, tm, tn), lambda i, j: (lax.div(j, nbh), i, lax.rem(j, nbh)))
        out_shape = jax.ShapeDtypeStruct((2, M, N // 2), out_dtype)
    else:
        out_spec = pl.BlockSpec((tm, tn), lambda i, j: (i, j))
        out_shape = jax.ShapeDtypeStruct((M, N), out_dtype)
    has_norm, has_res = norm_w is not None, residual is not None

    def body(*refs):
        x_ref, w_ref = refs[0], refs[1]
        p = 2
        nw_ref = r_ref = None
        if has_norm:
            nw_ref = refs[p]
            p += 1
        if has_res:
            r_ref = refs[p]
            p += 1
        o_ref = refs[p]
        xv = x_ref[...]
        if has_norm:
            xv = _rms_fwd(xv.astype(F32), nw_ref[...])
        acc = lax.dot_general(xv.astype(BF16), w_ref[...], _NT if w_t else (((1,), (0,)), ((), ())),
                              preferred_element_type=F32)
        if has_res:
            acc = acc + r_ref[...]
        o_ref[...] = acc.astype(out_dtype)

    w_spec = pl.BlockSpec((tn, K), lambda i, j: (j, 0)) if w_t else pl.BlockSpec((K, tn), lambda i, j: (0, j))
    in_specs = [pl.BlockSpec((tm, K), lambda i, j: (i, 0)), w_spec]
    args = [x, w]
    if has_norm:
        in_specs.append(pl.BlockSpec((1, K), lambda i, j: (0, 0)))
        args.append(norm_w.reshape(1, K))
    if has_res:
        in_specs.append(pl.BlockSpec((tm, tn), lambda i, j: (i, j)))
        args.append(residual)
    return pl.pallas_call(
        body, name=name, grid=(M // tm, N // tn), in_specs=in_specs,
        out_specs=out_spec, out_shape=out_shape,
        compiler_params=_cparams(("parallel", "parallel")))(*args)


def _mm_nt(dy, w, *, name, epi=None, out_dtype=F32, tm=512, tn=512, tk=512, w_t=False):
    halves = dy.ndim == 3
    M, K = (dy.shape[1], 2 * dy.shape[2]) if halves else dy.shape
    N = w.shape[1] if w_t else w.shape[0]
    tm, tk = min(tm, M), min(tk, K)
    tn = N if epi is not None else min(tn, N)
    assert M % tm == 0 and N % tn == 0 and K % tk == 0, (name, M, N, K, tm, tn, tk)
    nk = K // tk
    has_epi = epi is not None

    def body(*refs):
        if has_epi:
            dy_ref, w_ref, h_ref, nw_ref, r_ref, o_ref, dnw_ref, acc_ref = refs
        else:
            dy_ref, w_ref, o_ref, acc_ref = refs
        i = pl.program_id(0)
        k = pl.program_id(2)

        @pl.when(k == 0)
        def _():
            acc_ref[...] = jnp.zeros_like(acc_ref)

        acc_ref[...] += lax.dot_general(dy_ref[...].astype(BF16), w_ref[...], (((1,), (0,)), ((), ())) if w_t else _NT,
                                        preferred_element_type=F32)

        @pl.when(k == nk - 1)
        def _():
            du = acc_ref[...]
            if has_epi:
                hv = h_ref[...]
                r = lax.rsqrt(jnp.mean(hv * hv, axis=-1, keepdims=True) + EPS)
                xhat = hv * r
                dxh = du * nw_ref[...]
                dx = r * (dxh - xhat * jnp.mean(dxh * xhat, axis=-1, keepdims=True))
                o_ref[...] = (r_ref[...] + dx).astype(out_dtype)
                contrib = jnp.sum(du * xhat, axis=0, keepdims=True)

                @pl.when(i == 0)
                def _():
                    dnw_ref[...] = contrib

                @pl.when(i > 0)
                def _():
                    dnw_ref[...] += contrib
            else:
                o_ref[...] = du.astype(out_dtype)

    if halves:
        nkh = K // 2 // tk
        assert K // 2 % tk == 0
        dy_spec = pl.BlockSpec((None, tm, tk), lambda i, j, k: (lax.div(k, nkh), i, lax.rem(k, nkh)))
    else:
        dy_spec = pl.BlockSpec((tm, tk), lambda i, j, k: (i, k))
    w_spec = pl.BlockSpec((tk, tn), lambda i, j, k: (k, j)) if w_t else pl.BlockSpec((tn, tk), lambda i, j, k: (j, k))
    in_specs = [dy_spec, w_spec]
    args = [dy, w]
    out_specs = [pl.BlockSpec((tm, tn), lambda i, j, k: (i, j))]
    out_shape = [jax.ShapeDtypeStruct((M, N), out_dtype)]
    if has_epi:
        h, nw, res = epi
        in_specs += [pl.BlockSpec((tm, N), lambda i, j, k: (i, 0)), pl.BlockSpec((1, N), lambda i, j, k: (0, 0)),
                     pl.BlockSpec((tm, N), lambda i, j, k: (i, 0))]
        args += [h, nw.reshape(1, N), res]
        out_specs.append(pl.BlockSpec((1, N), lambda i, j, k: (0, 0)))
        out_shape.append(jax.ShapeDtypeStruct((1, N), F32))
    outs = pl.pallas_call(
        body, name=name, grid=(M // tm, N // tn, nk), in_specs=in_specs, out_specs=out_specs, out_shape=out_shape,
        scratch_shapes=[pltpu.VMEM((tm, tn), F32)],
        compiler_params=_cparams(("arbitrary", "arbitrary", "arbitrary")))(*args)
    return (outs[0], outs[1]) if has_epi else outs[0]


def _mm_tn(x, dy, *, name, norm_w=None, out_dtype=BF16, tk1=1024, tn=512, tt=512):
    T, K1 = x.shape
    halves = dy.ndim == 3
    N = 2 * dy.shape[2] if halves else dy.shape[1]
    tk1, tn, tt = min(tk1, K1), min(tn, N), min(tt, T)
    has_norm = norm_w is not None
    assert K1 % tk1 == 0 and N % tn == 0 and T % tt == 0, (name, K1, N, T, tk1, tn, tt)
    assert not has_norm or tk1 == K1
    nt = T // tt

    def body(*refs):
        if has_norm:
            x_ref, dy_ref, nw_ref, o_ref, acc_ref = refs
        else:
            x_ref, dy_ref, o_ref, acc_ref = refs
        t = pl.program_id(2)

        @pl.when(t == 0)
        def _():
            acc_ref[...] = jnp.zeros_like(acc_ref)

        xv = x_ref[...]
        if has_norm:
            xv = _rms_fwd(xv.astype(F32), nw_ref[...])
        acc_ref[...] += lax.dot_general(xv.astype(BF16), dy_ref[...].astype(BF16), _TN, preferred_element_type=F32)

        @pl.when(t == nt - 1)
        def _():
            o_ref[...] = acc_ref[...].astype(out_dtype)

    if halves:
        nbh = N // 2 // tn
        assert N // 2 % tn == 0
        dy_spec = pl.BlockSpec((None, tt, tn), lambda a, b, t: (lax.div(b, nbh), t, lax.rem(b, nbh)))
    else:
        dy_spec = pl.BlockSpec((tt, tn), lambda a, b, t: (t, b))
    in_specs = [pl.BlockSpec((tt, tk1), lambda a, b, t: (t, a)), dy_spec]
    args = [x, dy]
    if has_norm:
        in_specs.append(pl.BlockSpec((1, K1), lambda a, b, t: (0, 0)))
        args.append(norm_w.reshape(1, K1))
    return pl.pallas_call(
        body, name=name, grid=(K1 // tk1, N // tn, nt), in_specs=in_specs,
        out_specs=pl.BlockSpec((tk1, tn), lambda a, b, t: (a, b)),
        out_shape=jax.ShapeDtypeStruct((K1, N), out_dtype),
        scratch_shapes=[pltpu.VMEM((tk1, tn), F32)],
        compiler_params=_cparams(("parallel", "parallel", "arbitrary")))(*args)


def _mm_tn_t(dy, x, *, name, norm_w, out_dtype=BF16, tn=1408, tt=1024):
    T, K1 = x.shape
    halves = dy.ndim == 3
    N = 2 * dy.shape[2] if halves else dy.shape[1]
    tn, tt = min(tn, N), min(tt, T)
    assert N % tn == 0 and T % tt == 0, (name, N, T, tn, tt)
    nt = T // tt

    def body(dy_ref, x_ref, nw_ref, o_ref, acc_ref):
        t = pl.program_id(1)

        @pl.when(t == 0)
        def _():
            acc_ref[...] = jnp.zeros_like(acc_ref)

        xn = _rms_fwd(x_ref[...].astype(F32), nw_ref[...]).astype(BF16)
        acc_ref[...] += lax.dot_general(dy_ref[...].astype(BF16), xn, _TN, preferred_element_type=F32)

        @pl.when(t == nt - 1)
        def _():
            o_ref[...] = acc_ref[...].astype(out_dtype)

    if halves:
        nbh = N // 2 // tn
        assert N // 2 % tn == 0
        dy_spec = pl.BlockSpec((None, tt, tn), lambda b, t: (lax.div(b, nbh), t, lax.rem(b, nbh)))
    else:
        dy_spec = pl.BlockSpec((tt, tn), lambda b, t: (t, b))
    return pl.pallas_call(
        body, name=name, grid=(N // tn, nt),
        in_specs=[dy_spec, pl.BlockSpec((tt, K1), lambda b, t: (t, 0)), pl.BlockSpec((1, K1), lambda b, t: (0, 0))],
        out_specs=pl.BlockSpec((tn, K1), lambda b, t: (b, 0)),
        out_shape=jax.ShapeDtypeStruct((N, K1), out_dtype),
        scratch_shapes=[pltpu.VMEM((tn, K1), F32)],
        compiler_params=_cparams(("parallel", "arbitrary")))(dy, x, norm_w.reshape(1, K1))


def _shift_down(xb, prev8, j):
    main = pltpu.roll(xb, j, 0)
    head = pltpu.roll(xb[0:8], j, 0)
    ph = pltpu.roll(prev8, j, 0)
    row8 = lax.broadcasted_iota(jnp.int32, head.shape, 0)
    head = jnp.where(row8 < j, ph, head)
    return jnp.concatenate([head, main[8:]], axis=0)


def _shift_up(xb, next8, j):
    tt = xb.shape[0]
    main = pltpu.roll(xb, tt - j, 0)
    tail = pltpu.roll(xb[tt - 8:tt], 8 - j, 0)
    nh = pltpu.roll(next8, 8 - j, 0)
    row8 = lax.broadcasted_iota(jnp.int32, tail.shape, 0)
    tail = jnp.where(row8 + j >= 8, nh, tail)
    return jnp.concatenate([main[:tt - 8], tail], axis=0)


def _conv_hid(xb, prev8, w, b_row, K):
    out = b_row
    shifted = []
    for j in range(K):
        sh = K - 1 - j
        xs = xb if sh == 0 else _shift_down(xb, prev8, sh)
        shifted.append(xs)
        out = out + xs * w[j:j + 1, :]
    return out, shifted


def _prev_idx(i, nb8):
    return jnp.maximum(i * nb8 - 1, 0)


def _ssm_conv_fwd(zx, w, b, *, name, tt=512, tc=512):
    T = zx.shape[0]
    tt = min(tt, T)
    C, K = CONV_DIM, SSM_CONV
    cb0, nb8 = D_INNER // tc, tt // 8

    def body(x_ref, p_ref, w_ref, b_ref, o_ref):
        first = (pl.program_id(1) > 0).astype(F32)
        hid, _ = _conv_hid(x_ref[...], p_ref[...] * first, w_ref[...], b_ref[...], K)
        o_ref[...] = hid * _sigmoid(hid)

    return pl.pallas_call(
        body, name=name, grid=(C // tc, T // tt),
        in_specs=[pl.BlockSpec((tt, tc), lambda c, i: (i, c + cb0)),
                  pl.BlockSpec((8, tc), lambda c, i: (_prev_idx(i, nb8), c + cb0)),
                  pl.BlockSpec((K, tc), lambda c, i: (0, c)), pl.BlockSpec((1, tc), lambda c, i: (0, c))],
        out_specs=pl.BlockSpec((tt, tc), lambda c, i: (i, c)),
        out_shape=jax.ShapeDtypeStruct((T, C), F32),
        compiler_params=_cparams(("parallel", "parallel")))(zx, zx, w, b)


def _ssm_conv_bwd_pre(zx, w, b, dout, *, name, tt=512, tc=512):
    T = zx.shape[0]
    tt = min(tt, T)
    C, K = CONV_DIM, SSM_CONV
    cb0, nb8 = D_INNER // tc, tt // 8

    def body(x_ref, p_ref, w_ref, b_ref, d_ref, dh_ref, dw_ref, db_ref):
        t = pl.program_id(1)
        first = (t > 0).astype(F32)
        hid, shifted = _conv_hid(x_ref[...], p_ref[...] * first, w_ref[...], b_ref[...], K)
        sg = _sigmoid(hid)
        dh = d_ref[...] * (sg * (1.0 + hid * (1.0 - sg)))
        dh_ref[...] = dh

        @pl.when(t == 0)
        def _():
            dw_ref[...] = jnp.zeros_like(dw_ref)
            db_ref[...] = jnp.zeros_like(db_ref)

        db_ref[...] += jnp.sum(dh, axis=0, keepdims=True)
        for j in range(K):
            dw_ref[j:j + 1, :] += jnp.sum(dh * shifted[j], axis=0, keepdims=True)

    return pl.pallas_call(
        body, name=name, grid=(C // tc, T // tt),
        in_specs=[pl.BlockSpec((tt, tc), lambda c, i: (i, c + cb0)),
                  pl.BlockSpec((8, tc), lambda c, i: (_prev_idx(i, nb8), c + cb0)),
                  pl.BlockSpec((K, tc), lambda c, i: (0, c)), pl.BlockSpec((1, tc), lambda c, i: (0, c)),
                  pl.BlockSpec((tt, tc), lambda c, i: (i, c))],
        out_specs=[pl.BlockSpec((tt, tc), lambda c, i: (i, c)), pl.BlockSpec((K, tc), lambda c, i: (0, c)),
                   pl.BlockSpec((1, tc), lambda c, i: (0, c))],
        out_shape=[jax.ShapeDtypeStruct((T, C), F32), jax.ShapeDtypeStruct((K, C), F32),
                   jax.ShapeDtypeStruct((1, C), F32)],
        compiler_params=_cparams(("parallel", "arbitrary")))(zx, zx, w, b, dout)


def _put_cols(buf, src, col0, *, name, tt=512):
    T, C = src.shape
    tt = min(tt, T)

    def body(s_ref, _, o_ref):
        o_ref[...] = s_ref[...]

    return pl.pallas_call(
        body, name=name, grid=(T // tt,),
        in_specs=[pl.BlockSpec((tt, C), lambda i: (i, 0)), _ANY],
        out_specs=pl.BlockSpec((tt, C), lambda i: (i, col0 // C)),
        out_shape=jax.ShapeDtypeStruct(buf.shape, buf.dtype), input_output_aliases={1: 0},
        compiler_params=_cparams(("parallel",)))(src, buf)


def _conv_bwd_in(dh, w, *, name, K, tt=512, tc=512, out_dtype=BF16, into=None):
    T, C = dh.shape
    tt = min(tt, T)
    nb8, nT = tt // 8, T // tt
    last8 = T // 8 - 1
    cb0 = 0 if into is None else into[1] // tc

    def body(d_ref, n_ref, w_ref, *rest):
        o_ref = rest[-1]
        notlast = (pl.program_id(1) < nT - 1).astype(F32)
        d = d_ref[...]
        nxt = n_ref[...] * notlast
        w_ = w_ref[...]
        acc = d * w_[K - 1:K, :]
        for sh in range(1, K):
            acc = acc + _shift_up(d, nxt, sh) * w_[K - 1 - sh:K - sh, :]
        o_ref[...] = acc.astype(out_dtype)

    in_specs = [pl.BlockSpec((tt, tc), lambda c, i: (i, c)),
                pl.BlockSpec((8, tc), lambda c, i: (jnp.minimum((i + 1) * nb8, last8), c)),
                pl.BlockSpec((K, tc), lambda c, i: (0, c))]
    args = [dh, dh, w]
    if into is None:
        out_shape, alias = jax.ShapeDtypeStruct((T, C), out_dtype), {}
    else:
        assert into[0].dtype == out_dtype and into[1] % tc == 0
        in_specs.append(_ANY)
        args.append(into[0])
        out_shape, alias = jax.ShapeDtypeStruct(into[0].shape, out_dtype), {3: 0}
    return pl.pallas_call(
        body, name=name, grid=(C // tc, nT), in_specs=in_specs,
        out_specs=pl.BlockSpec((tt, tc), lambda c, i: (i, c + cb0)),
        out_shape=out_shape, input_output_aliases=alias,
        compiler_params=_cparams(("parallel", "parallel")))(*args)


def _ffn_conv_fwd3(a3, w, b, *, name, tt=256, tc=1408):
    T = a3.shape[1]
    tt = min(tt, T)
    K, nbh, n16 = FFN_CONV, D_FF // tc, tt // 16

    def body(a_ref, p_ref, wg_ref, wv_ref, bg_ref, bv_ref, o_ref):
        first = (pl.program_id(1) > 0).astype(F32)
        a = a_ref[...].astype(F32)
        prev = p_ref[...].astype(F32)[:, 8:16, :] * first
        hg, _ = _conv_hid(a[0], prev[0], wg_ref[...], bg_ref[...], K)
        hv, _ = _conv_hid(a[1], prev[1], wv_ref[...], bv_ref[...], K)
        o_ref[...] = (hg * _sigmoid(hg) * hv).astype(BF16)

    return pl.pallas_call(
        body, name=name, grid=(nbh, T // tt),
        in_specs=[pl.BlockSpec((2, tt, tc), lambda c, i: (0, i, c)),
                  pl.BlockSpec((2, 16, tc), lambda c, i: (0, _prev_idx(i, n16), c)),
                  pl.BlockSpec((K, tc), lambda c, i: (0, c)), pl.BlockSpec((K, tc), lambda c, i: (0, c + nbh)),
                  pl.BlockSpec((1, tc), lambda c, i: (0, c)), pl.BlockSpec((1, tc), lambda c, i: (0, c + nbh))],
        out_specs=pl.BlockSpec((tt, tc), lambda c, i: (i, c)),
        out_shape=jax.ShapeDtypeStruct((T, D_FF), BF16),
        compiler_params=_cparams(("parallel", "parallel")))(a3, a3, w, w, b, b)


def _ffn_conv_bwd3(a3, w, b, dp, *, name, tt=256, tc=1408):
    T = a3.shape[1]
    tt = min(tt, T)
    K, nbh, n16 = FFN_CONV, D_FF // tc, tt // 16

    def body(a_ref, p_ref, wg_ref, wv_ref, bg_ref, bv_ref, dp_ref, dh_ref, dw_ref, db_ref):
        t = pl.program_id(1)
        first = (t > 0).astype(F32)
        a = a_ref[...].astype(F32)
        prev = p_ref[...].astype(F32)[:, 8:16, :] * first
        hg, sh_g = _conv_hid(a[0], prev[0], wg_ref[...], bg_ref[...], K)
        hv, sh_v = _conv_hid(a[1], prev[1], wv_ref[...], bv_ref[...], K)
        sg = _sigmoid(hg)
        d = dp_ref[...].astype(F32)
        dhg = d * hv * (sg * (1.0 + hg * (1.0 - sg)))
        dhv = d * (hg * sg)
        dh_ref[0] = dhg.astype(BF16)
        dh_ref[1] = dhv.astype(BF16)

        @pl.when(t == 0)
        def _():
            dw_ref[...] = jnp.zeros_like(dw_ref)
            db_ref[...] = jnp.zeros_like(db_ref)

        db_ref[0] += jnp.sum(dhg, axis=0, keepdims=True)
        db_ref[1] += jnp.sum(dhv, axis=0, keepdims=True)
        for j in range(K):
            dw_ref[0, j:j + 1, :] += jnp.sum(dhg * sh_g[j], axis=0, keepdims=True)
            dw_ref[1, j:j + 1, :] += jnp.sum(dhv * sh_v[j], axis=0, keepdims=True)

    return pl.pallas_call(
        body, name=name, grid=(nbh, T // tt),
        in_specs=[pl.BlockSpec((2, tt, tc), lambda c, i: (0, i, c)),
                  pl.BlockSpec((2, 16, tc), lambda c, i: (0, _prev_idx(i, n16), c)),
                  pl.BlockSpec((K, tc), lambda c, i: (0, c)), pl.BlockSpec((K, tc), lambda c, i: (0, c + nbh)),
                  pl.BlockSpec((1, tc), lambda c, i: (0, c)), pl.BlockSpec((1, tc), lambda c, i: (0, c + nbh)),
                  pl.BlockSpec((tt, tc), lambda c, i: (i, c))],
        out_specs=[pl.BlockSpec((2, tt, tc), lambda c, i: (0, i, c)), pl.BlockSpec((2, K, tc), lambda c, i: (0, 0, c)),
                   pl.BlockSpec((2, 1, tc), lambda c, i: (0, 0, c))],
        out_shape=[jax.ShapeDtypeStruct((2, T, D_FF), BF16), jax.ShapeDtypeStruct((2, K, D_FF), F32),
                   jax.ShapeDtypeStruct((2, 1, D_FF), F32)],
        compiler_params=_cparams(("parallel", "arbitrary")))(a3, a3, w, w, b, b, dp)


def _conv_bwd_in3(dh3, w, *, name, K, tt=256, tc=1408):
    H, T, C = dh3.shape
    tt = min(tt, T)
    nb, n16, nT = C // tc, tt // 16, T // tt
    last16 = T // 16 - 1

    def body(d_ref, n_ref, w_ref, o_ref):
        notlast = (pl.program_id(2) < nT - 1).astype(F32)
        d = d_ref[...].astype(F32)
        nxt = n_ref[...].astype(F32)[0:8, :] * notlast
        w_ = w_ref[...]
        acc = d * w_[K - 1:K, :]
        for sh in range(1, K):
            acc = acc + _shift_up(d, nxt, sh) * w_[K - 1 - sh:K - sh, :]
        o_ref[...] = acc.astype(BF16)

    return pl.pallas_call(
        body, name=name, grid=(H, nb, nT),
        in_specs=[pl.BlockSpec((None, tt, tc), lambda h, c, i: (h, i, c)),
                  pl.BlockSpec((None, 16, tc), lambda h, c, i: (h, jnp.minimum((i + 1) * n16, last16), c)),
                  pl.BlockSpec((K, tc), lambda h, c, i: (0, h * nb + c))],
        out_specs=pl.BlockSpec((None, tt, tc), lambda h, c, i: (h, i, c)),
        out_shape=jax.ShapeDtypeStruct((H, T, C), BF16),
        compiler_params=_cparams(("parallel", "parallel", "parallel")))(dh3, dh3, w)


def _cumsum_rows(x):
    L = x.shape[0]
    row = lax.broadcasted_iota(jnp.int32, x.shape, 0)
    k = 1
    while k < L:
        x = x + jnp.where(row >= k, pltpu.roll(x, k, 0), 0.0)
        k *= 2
    return x


def _rcumsum_rows(x):
    L = x.shape[0]
    row = lax.broadcasted_iota(jnp.int32, x.shape, 0)
    k = 1
    while k < L:
        x = x + jnp.where(row < L - k, pltpu.roll(x, L - k, 0), 0.0)
        k *= 2
    return x


def _split_terms(m, n):
    terms, rest = [], m
    for _ in range(n):
        t = rest.astype(BF16)
        terms.append(t)
        rest = rest - t.astype(F32)
    return jnp.concatenate(terms, axis=1)


def _select_dot(m, n_terms, n_out, cond):
    K = m.shape[1]
    k = lax.broadcasted_iota(jnp.int32, (K, n_out), 0)
    j = lax.broadcasted_iota(jnp.int32, (K, n_out), 1)
    sel = cond(k, j).astype(BF16)
    return jnp.dot(_split_terms(m, n_terms), jnp.concatenate([sel] * n_terms, axis=0), preferred_element_type=F32)


def _rowsum_mxu(m):
    return _select_dot(m, 2, 128, lambda k, j: k >= 0)


def _lane_block_sums(m, width):
    shift = width.bit_length() - 1
    return _select_dot(m, 2, 128, lambda k, j: j == jnp.right_shift(k, shift))


def _heads_to_pairs(m):
    return _select_dot(m, 3, 512, lambda k, j: k == jnp.right_shift(j, 6))


def _ssd_common(dt_ref, par_ref):
    par = par_ref[...]
    raw = dt_ref[...] + par[0:1, :]
    dt = _softplus(raw)
    a = -jnp.exp(par[1:2, :])
    cs = _cumsum_rows(dt * a)
    L = cs.shape[0]
    cs_last = cs[L - 1:L, :]
    return raw, dt, a, par[2:3, :], cs, cs.T, jnp.exp(cs), jnp.exp(cs_last - cs), jnp.exp(cs_last)


def _ssd_specs(nc, rev):
    L = SSM_CHUNK

    def ci(c):
        return nc - 1 - c if rev else c

    return [pl.BlockSpec((L, D_INNER), lambda c: (ci(c), 0)),
            pl.BlockSpec((L, GN), lambda c: (ci(c), D_INNER // GN)),
            pl.BlockSpec((L, GN), lambda c: (ci(c), D_INNER // GN + 1)),
            pl.BlockSpec((SSM_GROUPS, L, 128), lambda c: (0, ci(c), 0)),
            pl.BlockSpec((SSM_GROUPS, 8, 128), lambda c: (0, 0, 0)),
            pl.BlockSpec((L, D_INNER), lambda c: (ci(c), 0)),
            pl.BlockSpec((1, D_INNER), lambda c: (0, 0))], ci


def _round_robin(gens):
    live = list(gens)
    while live:
        nxt = []
        for gen in live:
            try:
                next(gen)
                nxt.append(gen)
            except StopIteration:
                pass
        live = nxt


def _group_views(g, wide, narrow, lead):
    return ([r.at[:, g * 512:(g + 1) * 512] for r in wide], [r.at[:, g * 128:(g + 1) * 128] for r in narrow],
            [r.at[g] for r in lead])


def _ssd_fwd(xbc_c, zx, dtg, par, gnw, *, name):
    T = xbc_c.shape[0]
    L = SSM_CHUNK
    nc = T // L
    in_specs, ci = _ssd_specs(nc, False)

    def body(xs_ref, b_ref, c_ref, dt_ref, par_ref, z_ref, gnw_ref, y_ref, yn_ref, st_ref, h_ref):
        @pl.when(pl.program_id(0) == 0)
        def _():
            h_ref[...] = jnp.zeros_like(h_ref)

        gens = []
        for g in range(SSM_GROUPS):
            (xs, z, gw, y, yn), (b, c), (dt, pr, st, h) = _group_views(
                g, [xs_ref, z_ref, gnw_ref, y_ref, yn_ref], [b_ref, c_ref], [dt_ref, par_ref, st_ref, h_ref])
            gens.append(group(xs, b, c, dt, pr, z, gw, y, yn, st, h))
        _round_robin(gens)

    def group(xs_ref, b_ref, c_ref, dt_ref, par_ref, z_ref, gnw_ref, y_ref, yn_ref, st_ref, h_ref):
        _, dt, _, dsk, cs, csT, ecs, eend, dec = _ssd_common(dt_ref, par_ref)
        Bb = b_ref[...].astype(BF16)
        Cb = c_ref[...].astype(BF16)
        G = lax.dot_general(Cb, Bb, _NT, preferred_element_type=F32)
        row = lax.broadcasted_iota(jnp.int32, (L, L), 0)
        col = lax.broadcasted_iota(jnp.int32, (L, L), 1)
        tril = col <= row
        lo = lax.broadcasted_iota(jnp.int32, (L, 128), 1) < 64
        lo1 = lax.broadcasted_iota(jnp.int32, (1, 128), 1) < 64
        dt_x, ecs_x, eend_x = (_heads_to_pairs(m) for m in (dt, ecs, eend))
        for pp in range(4):
            hA, hB = 2 * pp, 2 * pp + 1
            lanes = slice(pp * 128, (pp + 1) * 128)

            def sel1(m):
                return jnp.where(lo1, m[:, hA:hA + 1], m[:, hB:hB + 1])

            X = xs_ref[:, lanes]
            xd = X * dt_x[:, lanes]
            xdb = xd.astype(BF16)
            ys = []
            for h in (hA, hB):
                Lm = jnp.where(tril, jnp.exp(jnp.minimum(cs[:, h:h + 1] - csT[h:h + 1, :], 0.0)), 0.0)
                ys.append(jnp.dot((G * Lm).astype(BF16), xdb, preferred_element_type=F32))
                yield
            Hp = h_ref[pp]
            st_ref[pp] = Hp
            yoff = jnp.dot(Cb, Hp.astype(BF16), preferred_element_type=F32) * ecs_x[:, lanes]
            y_ref[:, lanes] = jnp.where(lo, ys[0], ys[1]) + yoff + sel1(dsk) * X
            S = lax.dot_general(Bb, (xd * eend_x[:, lanes]).astype(BF16), _TN, preferred_element_type=F32)
            h_ref[pp] = Hp * sel1(dec) + S
            yield
        zv = z_ref[...]
        yg = y_ref[...] * (zv * _sigmoid(zv))
        r = jnp.tile(lax.rsqrt(_rowsum_mxu(yg * yg) * (1.0 / 512) + EPS), (1, 4))
        yn_ref[...] = (yg * r * gnw_ref[...]).astype(BF16)

    return pl.pallas_call(
        body, name=name, grid=(nc,), in_specs=in_specs,
        out_specs=[pl.BlockSpec((L, D_INNER), lambda c: (c, 0)), pl.BlockSpec((L, D_INNER), lambda c: (c, 0)),
                   pl.BlockSpec((SSM_GROUPS, None, 4, 128, 128), lambda c: (0, c, 0, 0, 0))],
        out_shape=[jax.ShapeDtypeStruct((T, D_INNER), F32), jax.ShapeDtypeStruct((T, D_INNER), BF16),
                   jax.ShapeDtypeStruct((SSM_GROUPS, nc, 4, 128, 128), F32)],
        scratch_shapes=[pltpu.VMEM((SSM_GROUPS, 4, 128, 128), F32)],
        compiler_params=_cparams(("arbitrary",)))(xbc_c, xbc_c, xbc_c, dtg, par, zx, gnw)


def _ssd_bwd(xbc_c, zx, dtg, par, gnw, y, st, dyn, *, name):
    T = xbc_c.shape[0]
    L = SSM_CHUNK
    nc = T // L
    in_specs, ci = _ssd_specs(nc, True)
    in_specs += [pl.BlockSpec((L, D_INNER), lambda c: (ci(c), 0)),
                 pl.BlockSpec((SSM_GROUPS, None, 4, 128, 128), lambda c: (0, ci(c), 0, 0, 0)),
                 pl.BlockSpec((L, D_INNER), lambda c: (ci(c), 0))]

    def body(xs_ref, b_ref, c_ref, dt_ref, par_ref, z_ref, gnw_ref, y_ref, st_ref, dyn_ref,
             dxbc_ref, dz_ref, ddt_ref, dgnw_ref, dpar_ref, dh_ref):
        @pl.when(pl.program_id(0) == 0)
        def _():
            dh_ref[...] = jnp.zeros_like(dh_ref)
            dgnw_ref[...] = jnp.zeros_like(dgnw_ref)
            dpar_ref[...] = jnp.zeros_like(dpar_ref)

        dxs_ref = dxbc_ref.at[:, 0:D_INNER]
        db_ref = dxbc_ref.at[:, D_INNER:D_INNER + GN]
        dc_ref = dxbc_ref.at[:, D_INNER + GN:CONV_DIM]

        gens = []
        for g in range(SSM_GROUPS):
            (xs, z, gw, y, dyn, dxs, dz, dgw), (b, c, db, dc), (dt, pr, st, ddt, dpr, dh) = _group_views(
                g, [xs_ref, z_ref, gnw_ref, y_ref, dyn_ref, dxs_ref, dz_ref, dgnw_ref], [b_ref, c_ref, db_ref, dc_ref],
                [dt_ref, par_ref, st_ref, ddt_ref, dpar_ref, dh_ref])
            gens.append(group(xs, b, c, dt, pr, z, gw, y, st, dyn, dxs, db, dc, dz, ddt, dgw, dpr, dh))
        _round_robin(gens)

    def group(xs_ref, b_ref, c_ref, dt_ref, par_ref, z_ref, gnw_ref, y_ref, st_ref, dyn_ref,
              dxs_ref, db_ref, dc_ref, dz_ref, ddt_ref, dgnw_ref, dpar_ref, dh_ref):
        yv = y_ref[...]
        zv = z_ref[...]
        sg = _sigmoid(zv)
        sz = zv * sg
        yg = yv * sz
        r = jnp.tile(lax.rsqrt(_rowsum_mxu(yg * yg) * (1.0 / 512) + EPS), (1, 4))
        yh = yg * r
        dyn = dyn_ref[...].astype(F32)
        dgnw_ref[...] += jnp.sum(dyn * yh, axis=0, keepdims=True)
        dyh = dyn * gnw_ref[...]
        dyg = r * (dyh - yh * jnp.tile(_rowsum_mxu(dyh * yh) * (1.0 / 512), (1, 4)))
        dY_all = dyg * sz
        dz_ref[...] = (dyg * yv * (sg * (1.0 + zv * (1.0 - sg)))).astype(dz_ref.dtype)

        yield
        raw, dt, a, dsk, cs, csT, ecs, eend, dec = _ssd_common(dt_ref, par_ref)
        Bb = b_ref[...].astype(BF16)
        Cb = c_ref[...].astype(BF16)
        G = lax.dot_general(Cb, Bb, _NT, preferred_element_type=F32)
        row = lax.broadcasted_iota(jnp.int32, (L, L), 0)
        col = lax.broadcasted_iota(jnp.int32, (L, L), 1)
        tril = col <= row
        lo = lax.broadcasted_iota(jnp.int32, (L, 128), 1) < 64
        lane1 = lax.broadcasted_iota(jnp.int32, (1, 128), 1)
        lo1 = lane1 < 64
        rowl = lax.broadcasted_iota(jnp.int32, (L, 128), 0)
        dt_x, ecs_x, eend_x = (_heads_to_pairs(m) for m in (dt, ecs, eend))
        dG = jnp.zeros((L, L), F32)
        dB = jnp.zeros((L, SSM_STATE), F32)
        dC = jnp.zeros((L, SSM_STATE), F32)
        dcs_t = jnp.zeros((L, L), F32)
        tails = jnp.zeros((1, 128), F32)
        dD_row = jnp.zeros((1, 128), F32)
        v_parts, prod_parts = [], []

        def tot(m):
            return jnp.sum(jnp.sum(m, axis=0, keepdims=True), axis=1, keepdims=True)

        for pp in range(4):
            hA, hB = 2 * pp, 2 * pp + 1
            lanes = slice(pp * 128, (pp + 1) * 128)

            def sel1(m):
                return jnp.where(lo1, m[:, hA:hA + 1], m[:, hB:hB + 1])

            X = xs_ref[:, lanes]
            dY = dY_all[:, lanes]
            dtsel = dt_x[:, lanes]
            xd = X * dtsel
            xdb = xd.astype(BF16)
            dYb = dY.astype(BF16)
            Hp = st_ref[pp]
            Hb = Hp.astype(BF16)
            dHn = dh_ref[pp]
            dHb = dHn.astype(BF16)
            ecs_sel = ecs_x[:, lanes]
            eend_sel = eend_x[:, lanes]
            dxd_state = jnp.dot(Bb, dHb, preferred_element_type=F32) * eend_sel
            yoff = jnp.dot(Cb, Hb, preferred_element_type=F32) * ecs_sel
            dYe = (dY * ecs_sel).astype(BF16)
            dC = dC + lax.dot_general(dYe, Hb, _NT, preferred_element_type=F32)
            dB = dB + lax.dot_general((xd * eend_sel).astype(BF16), dHb, _NT, preferred_element_type=F32)
            dh_ref[pp] = dHn * sel1(dec) + lax.dot_general(Cb, dYe, _TN, preferred_element_type=F32)
            q = xd * dxd_state
            dyq = dY * yoff - q
            qcol = jnp.sum(q, axis=0, keepdims=True)
            hcol = jnp.sum(dHn * Hp, axis=0, keepdims=True)
            dxd_diag = []
            for h, msk, msk1 in ((hA, lo, lo1), (hB, jnp.logical_not(lo), jnp.logical_not(lo1))):
                Lm = jnp.where(tril, jnp.exp(jnp.minimum(cs[:, h:h + 1] - csT[h:h + 1, :], 0.0)), 0.0)
                M = G * Lm
                dxd_diag.append(lax.dot_general(M.astype(BF16), dYb, _TN, preferred_element_type=F32))
                dM = lax.dot_general(jnp.where(msk, dY, 0.0).astype(BF16), xdb, _NT, preferred_element_type=F32)
                dG = dG + dM * Lm
                W = dM * M
                dcs_t = dcs_t + jnp.where(row == h, jnp.sum(W, axis=0, keepdims=True), 0.0)
                v_parts.append(W + jnp.where(msk, dyq, 0.0))
                tail = (jnp.sum(jnp.where(msk1, qcol, 0.0), axis=1, keepdims=True)
                        + dec[:, h:h + 1] * jnp.sum(jnp.where(msk1, hcol, 0.0), axis=1, keepdims=True))
                tails = tails + jnp.where(lane1 == h, tail, 0.0)
                yield
            dxd = jnp.where(lo, dxd_diag[0], dxd_diag[1]) + dxd_state
            prod_parts.append(dxd * X)
            dxs_ref[:, lanes] = dxd * dtsel + sel1(dsk) * dY
            dyx = jnp.sum(dY * X, axis=0, keepdims=True)
            sA = jnp.sum(jnp.where(lo1, dyx, 0.0), axis=1, keepdims=True)
            sB = jnp.sum(dyx, axis=1, keepdims=True) - sA
            dD_row = dD_row + jnp.where(lane1 == hA, sA, 0.0) + jnp.where(lane1 == hB, sB, 0.0)
            yield
        dGb = dG.astype(BF16)
        db_ref[...] = dB + lax.dot_general(dGb, Cb, _TN, preferred_element_type=F32)
        dc_ref[...] = dC + jnp.dot(dGb, Bb, preferred_element_type=F32)
        dcs_mat = _lane_block_sums(jnp.concatenate(v_parts, axis=1), 128) + jnp.where(rowl == L - 1, tails, 0.0)
        ddt_mat = _lane_block_sums(jnp.concatenate(prod_parts, axis=1), 64)
        dad = _rcumsum_rows(dcs_mat - dcs_t.T)
        draw = (a * dad + ddt_mat) * _sigmoid(raw)
        ddt_ref[...] = draw
        dpar_ref[0:1, :] += jnp.sum(draw, axis=0, keepdims=True)
        dpar_ref[1:2, :] += jnp.sum(dt * dad, axis=0, keepdims=True) * a
        dpar_ref[2:3, :] += dD_row

    return pl.pallas_call(
        body, name=name, grid=(nc,), in_specs=in_specs,
        out_specs=[pl.BlockSpec((L, CONV_DIM), lambda c: (ci(c), 0)),
                   pl.BlockSpec((L, D_INNER), lambda c: (ci(c), 0)),
                   pl.BlockSpec((SSM_GROUPS, L, 128), lambda c: (0, ci(c), 0)),
                   pl.BlockSpec((1, D_INNER), lambda c: (0, 0)),
                   pl.BlockSpec((SSM_GROUPS, 8, 128), lambda c: (0, 0, 0))],
        out_shape=[jax.ShapeDtypeStruct((T, CONV_DIM), F32), jax.ShapeDtypeStruct((T, IN_PROJ_PAD), BF16),
                   jax.ShapeDtypeStruct((SSM_GROUPS, T, 128), F32), jax.ShapeDtypeStruct((1, D_INNER), F32),
                   jax.ShapeDtypeStruct((SSM_GROUPS, 8, 128), F32)],
        scratch_shapes=[pltpu.VMEM((SSM_GROUPS, 4, 128, 128), F32)],
        compiler_params=_cparams(("arbitrary",)))(xbc_c, xbc_c, xbc_c, dtg, par, zx, gnw, y, st, dyn)


SB_KEYS = 512
SB_SCAN = 256
SB_STRIP = 256


def _tri(width, cond):
    kk = lax.broadcasted_iota(jnp.int32, (width, width), 0)
    jj = lax.broadcasted_iota(jnp.int32, (width, width), 1)
    return cond(kk, jj).astype(BF16)


_LOG2E = 1.4426950408889634


def _softplus2(z2):
    return jnp.maximum(z2, 0.0) + jnp.log2(1.0 + jnp.exp2(-jnp.abs(z2)))


def _sba_sub_fwd(zb, c, U, mask):
    z2 = zb * _LOG2E
    s = _softplus2(z2)
    if mask is not None:
        s = jnp.where(mask, s, 0.0)
    R = c + jnp.dot(s.astype(BF16), U, preferred_element_type=F32)
    A = jnp.exp2(z2 - s - R)
    if mask is not None:
        A = jnp.where(mask, A, 0.0)
    return A.astype(BF16), R[:, 0:1] + s[:, 0:1]


def _sba_sub_bwd(zb, dAb, Lt, pc, pe, Uincl, Uexcl, mask):
    last = zb.shape[1] - 1
    z2 = zb * _LOG2E
    s = _softplus2(z2)
    g = z2 - s
    if mask is not None:
        s = jnp.where(mask, s, 0.0)
    P = pc + jnp.dot(s.astype(BF16), Uincl, preferred_element_type=F32)
    A = jnp.exp2(g - (Lt - P))
    if mask is not None:
        A = jnp.where(mask, A, 0.0)
    E = dAb * A
    PE = pe + jnp.dot(E.astype(BF16), Uexcl, preferred_element_type=F32)
    dz = E - jnp.exp2(g) * (E + PE)
    if mask is not None:
        dz = jnp.where(mask, dz, 0.0)
    return (A.astype(BF16), dz.astype(BF16), P[:, last:last + 1], PE[:, last:last + 1] + E[:, last:last + 1])


def _stack_heads(v):
    lo = lax.broadcasted_iota(jnp.int32, v.shape, 1) < 64
    zero = jnp.zeros_like(v)
    return jnp.concatenate([jnp.where(lo, v, zero), jnp.where(lo, zero, v)], axis=0)


def _unstack_heads(v):
    lo = lax.broadcasted_iota(jnp.int32, (SB_BLOCK, 128), 1) < 64
    return jnp.where(lo, v[:SB_BLOCK], v[SB_BLOCK:])


def _sba_rows(a):
    return slice(2 * a * SB_BLOCK, 2 * (a + 1) * SB_BLOCK)


def _sba_diag_case(a, b):
    Bq = SB_BLOCK
    if b * SB_SCAN >= (a + 1) * Bq:
        return "skip"
    if (b + 1) * SB_SCAN <= a * Bq:
        return "full"
    rowi = lax.broadcasted_iota(jnp.int32, (2 * Bq, SB_SCAN), 0)
    qpos = a * Bq + jnp.where(rowi >= Bq, rowi - Bq, rowi)
    return b * SB_SCAN + lax.broadcasted_iota(jnp.int32, (2 * Bq, SB_SCAN), 1) < qpos


def _sba_fwd(q, kv, *, name):
    T = q.shape[0]
    Bq = SB_BLOCK
    nsub = SB_KEYS // Bq
    nscan = SB_KEYS // SB_SCAN
    R = 2 * SB_KEYS
    assert T % SB_KEYS == 0 and SB_STRIP == 2 * Bq
    scale = 1.0 / math.sqrt(SB_HEAD_DIM)

    def body(q_ref, k_ref, v_ref, o_ref, lt_ref, z_s, a_s, c_s, acc_s):
        i = pl.program_id(1)
        U2 = _tri(SB_SCAN, lambda k, j: k > j)
        qs_all = jnp.concatenate([_stack_heads(q_ref[a * Bq:(a + 1) * Bq, :] * scale) for a in range(nsub)], axis=0)
        c_s[...] = jnp.zeros_like(c_s)
        acc_s[...] = jnp.zeros_like(acc_s)

        def scores(J, slot):
            off = pl.multiple_of(J * SB_KEYS, SB_KEYS)
            z_s[slot] = lax.dot_general(qs_all, k_ref[pl.ds(off, SB_KEYS), :], _NT, preferred_element_type=F32)

        def weights(slot, diag):
            for a in range(nsub):
                rows = _sba_rows(a)
                c = c_s[rows, :]
                for b in reversed(range(nscan)):
                    cols = slice(b * SB_SCAN, (b + 1) * SB_SCAN)
                    case = _sba_diag_case(a, b) if diag else "full"
                    if isinstance(case, str) and case == "skip":
                        a_s[slot, rows, cols] = jnp.zeros((2 * Bq, SB_SCAN), BF16)
                        continue
                    A, c = _sba_sub_fwd(z_s[slot, rows, cols], c, U2, None if isinstance(case, str) else case)
                    a_s[slot, rows, cols] = A
                c_s[rows, :] = c

        def values(J, slot):
            off = pl.multiple_of(J * SB_KEYS, SB_KEYS)
            acc_s[...] += jnp.dot(a_s[slot], v_ref[pl.ds(off, SB_KEYS), :], preferred_element_type=F32)

        scores(i, 0)
        weights(0, True)
        scores(jnp.maximum(i - 1, 0), 1)

        def two_steps(u, _):
            t = 2 * u + 1
            weights(1, False)
            scores(jnp.maximum(i - t - 1, 0), 0)
            values(i - t + 1, 0)
            weights(0, False)
            scores(jnp.maximum(i - t - 2, 0), 1)
            values(i - t, 1)
            return 0

        lax.fori_loop(0, i // 2, two_steps, 0)
        odd = lax.rem(i, 2) == 1

        @pl.when(jnp.logical_not(odd))
        def _():
            values(0, 0)

        @pl.when(odd)
        def _():
            weights(1, False)
            values(1, 0)
            values(0, 1)
        for a in range(nsub):
            o_ref[a * Bq:(a + 1) * Bq, :] = _unstack_heads(acc_s[_sba_rows(a), :]).astype(BF16)
            lt_ref[a * Bq:(a + 1) * Bq, :] = _unstack_heads(jnp.broadcast_to(c_s[_sba_rows(a), :], (2 * Bq, 128)))

    return pl.pallas_call(
        body, name=name, grid=(SB_HEADS // 2, T // SB_KEYS),
        in_specs=[pl.BlockSpec((SB_KEYS, 128), lambda p, i: (i, p)), pl.BlockSpec((T, 128), lambda p, i: (0, p)),
                  pl.BlockSpec((T, 128), lambda p, i: (0, p + SB_HEADS // 2))],
        out_specs=[pl.BlockSpec((SB_KEYS, 128), lambda p, i: (i, p)),
                   pl.BlockSpec((None, SB_KEYS, 128), lambda p, i: (p, i, 0))],
        out_shape=[jax.ShapeDtypeStruct((T, D_MODEL), BF16), jax.ShapeDtypeStruct((SB_HEADS // 2, T, 128), F32)],
        scratch_shapes=[pltpu.VMEM((2, R, SB_KEYS), F32), pltpu.VMEM((2, R, SB_KEYS), BF16),
                        pltpu.VMEM((R, 1), F32), pltpu.VMEM((R, 128), F32)],
        compiler_params=_cparams(("parallel", "parallel")))(q, kv, kv)


def _sba_bwd(q, kv, lt, do, *, name):
    T = q.shape[0]
    Bq = SB_BLOCK
    nq = T // SB_KEYS
    nsub = SB_KEYS // Bq
    nscan = SB_KEYS // SB_SCAN
    R = 2 * SB_KEYS
    assert T % SB_KEYS == 0 and SB_STRIP == 2 * Bq
    scale = 1.0 / math.sqrt(SB_HEAD_DIM)

    def body(q_ref, k_ref, v_ref, lt_ref, do_ref, dq_ref, dk_ref, dv_ref, dk_acc, dv_acc,
             z_s, da_s, a_s, dz_s, pc_s, pe_s, lt_s, dq_s):
        i = pl.program_id(1)

        @pl.when(i == 0)
        def _():
            dk_acc[...] = jnp.zeros_like(dk_acc)
            dv_acc[...] = jnp.zeros_like(dv_acc)

        Uincl = _tri(SB_SCAN, lambda k, j: k <= j)
        Uexcl = _tri(SB_SCAN, lambda k, j: k < j)
        qs, dos = [], []
        for a in range(nsub):
            rows = slice(a * Bq, (a + 1) * Bq)
            qs.append(_stack_heads(q_ref[rows, :] * scale))
            dos.append(_stack_heads(do_ref[rows, :]))
            lt_s[_sba_rows(a), :] = jnp.concatenate([lt_ref[rows, 0:1], lt_ref[rows, 64:65]], axis=0)
        qs_all = jnp.concatenate(qs, axis=0)
        dos_all = jnp.concatenate(dos, axis=0)
        pc_s[...] = jnp.zeros_like(pc_s)
        pe_s[...] = jnp.zeros_like(pe_s)
        a_s[1] = jnp.zeros((R, SB_KEYS), BF16)
        dz_s[1] = jnp.zeros((R, SB_KEYS), BF16)

        def scores(J, slot):
            off = pl.multiple_of(J * SB_KEYS, SB_KEYS)
            z_s[slot] = lax.dot_general(qs_all, k_ref[pl.ds(off, SB_KEYS), :], _NT, preferred_element_type=F32)
            da_s[slot] = lax.dot_general(dos_all, v_ref[pl.ds(off, SB_KEYS), :], _NT, preferred_element_type=F32)

        def gradients(slot, diag):
            for a in range(nsub):
                rows = _sba_rows(a)
                pc, pe, Lt = pc_s[rows, :], pe_s[rows, :], lt_s[rows, :]
                for b in range(nscan):
                    cols = slice(b * SB_SCAN, (b + 1) * SB_SCAN)
                    case = _sba_diag_case(a, b) if diag else "full"
                    if isinstance(case, str) and case == "skip":
                        a_s[slot, rows, cols] = jnp.zeros((2 * Bq, SB_SCAN), BF16)
                        dz_s[slot, rows, cols] = jnp.zeros((2 * Bq, SB_SCAN), BF16)
                        continue
                    A, dz, pc, pe = _sba_sub_bwd(z_s[slot, rows, cols], da_s[slot, rows, cols], Lt, pc, pe, Uincl, Uexcl,
                                                 None if isinstance(case, str) else case)
                    a_s[slot, rows, cols] = A
                    dz_s[slot, rows, cols] = dz
                pc_s[rows, :] = pc
                pe_s[rows, :] = pe

        def products(J, slot):
            off = pl.multiple_of(J * SB_KEYS, SB_KEYS)
            dzt = dz_s[slot]
            dk_acc[pl.ds(off, SB_KEYS), :] += lax.dot_general(dzt, qs_all, _TN, preferred_element_type=F32)
            dv_acc[pl.ds(off, SB_KEYS), :] += lax.dot_general(a_s[slot], dos_all, _TN, preferred_element_type=F32)
            dq_s[...] += jnp.dot(dzt, k_ref[pl.ds(off, SB_KEYS), :], preferred_element_type=F32)

        dq_s[...] = jnp.zeros_like(dq_s)
        scores(0, 0)

        def two_steps(u, _):
            t = 2 * u
            gradients(0, False)
            scores(t + 1, 1)
            products(jnp.maximum(t - 1, 0), 1)
            gradients(1, False)
            scores(t + 2, 0)
            products(t, 0)
            return 0

        lax.fori_loop(0, i // 2, two_steps, 0)
        odd = lax.rem(i, 2) == 1

        @pl.when(jnp.logical_not(odd))
        def _():
            gradients(0, True)
            products(jnp.maximum(i - 1, 0), 1)
            products(i, 0)

        @pl.when(odd)
        def _():
            gradients(0, False)
            scores(i, 1)
            products(jnp.maximum(i - 2, 0), 1)
            gradients(1, True)
            products(i - 1, 0)
            products(i, 1)

        for a in range(nsub):
            dq_ref[a * Bq:(a + 1) * Bq, :] = (_unstack_heads(dq_s[_sba_rows(a), :]) * scale).astype(BF16)

        @pl.when(i == nq - 1)
        def _():
            dk_ref[...] = dk_acc[...].astype(BF16)
            dv_ref[...] = dv_acc[...].astype(BF16)

    return pl.pallas_call(
        body, name=name, grid=(SB_HEADS // 2, nq),
        in_specs=[pl.BlockSpec((SB_KEYS, 128), lambda p, i: (i, p)), pl.BlockSpec((T, 128), lambda p, i: (0, p)),
                  pl.BlockSpec((T, 128), lambda p, i: (0, p + SB_HEADS // 2)),
                  pl.BlockSpec((None, SB_KEYS, 128), lambda p, i: (p, i, 0)),
                  pl.BlockSpec((SB_KEYS, 128), lambda p, i: (i, p))],
        out_specs=[pl.BlockSpec((SB_KEYS, 128), lambda p, i: (i, p)), pl.BlockSpec((T, 128), lambda p, i: (0, p)),
                   pl.BlockSpec((T, 128), lambda p, i: (0, p))],
        out_shape=[jax.ShapeDtypeStruct((T, D_MODEL), BF16), jax.ShapeDtypeStruct((T, D_MODEL), BF16),
                   jax.ShapeDtypeStruct((T, D_MODEL), BF16)],
        scratch_shapes=[pltpu.VMEM((T, 128), F32), pltpu.VMEM((T, 128), F32),
                        pltpu.VMEM((2, R, SB_KEYS), F32), pltpu.VMEM((2, R, SB_KEYS), F32),
                        pltpu.VMEM((2, R, SB_KEYS), BF16), pltpu.VMEM((2, R, SB_KEYS), BF16),
                        pltpu.VMEM((R, 1), F32), pltpu.VMEM((R, 1), F32), pltpu.VMEM((R, 1), F32),
                        pltpu.VMEM((R, 128), F32)],
        compiler_params=_cparams(("parallel", "arbitrary")))(q, kv, kv, lt, do)


def _loss_head(h, tgt, w, *, name, tt=512):
    T, D = h.shape
    tt = min(tt, T)

    def body(h_ref, t_ref, w_ref, loss_ref, dh_ref, dw_ref):
        i = pl.program_id(0)
        hv = h_ref[...]
        wv = w_ref[...]
        r = lax.rsqrt(jnp.mean(hv * hv, axis=-1, keepdims=True) + EPS)
        xhat = hv * r
        err = xhat * wv - t_ref[...]
        part = 0.5 * jnp.sum(jnp.mean(err * err, axis=-1, keepdims=True), axis=0, keepdims=True)
        dy = err * (1.0 / D)
        dxh = dy * wv
        dh_ref[...] = r * (dxh - xhat * jnp.mean(dxh * xhat, axis=-1, keepdims=True))
        dwc = jnp.sum(dy * xhat, axis=0, keepdims=True)

        @pl.when(i == 0)
        def _():
            loss_ref[...] = jnp.broadcast_to(part, loss_ref.shape)
            dw_ref[...] = dwc

        @pl.when(i > 0)
        def _():
            loss_ref[...] += jnp.broadcast_to(part, loss_ref.shape)
            dw_ref[...] += dwc

    return pl.pallas_call(
        body, name=name, grid=(T // tt,),
        in_specs=[pl.BlockSpec((tt, D), lambda i: (i, 0)), pl.BlockSpec((tt, D), lambda i: (i, 0)),
                  pl.BlockSpec((1, D), lambda i: (0, 0))],
        out_specs=[pl.BlockSpec((1, 128), lambda i: (0, 0)), pl.BlockSpec((tt, D), lambda i: (i, 0)),
                   pl.BlockSpec((1, D), lambda i: (0, 0))],
        out_shape=[jax.ShapeDtypeStruct((1, 128), F32), jax.ShapeDtypeStruct((T, D), F32),
                   jax.ShapeDtypeStruct((1, D), F32)],
        compiler_params=_cparams(("arbitrary",)))(h, tgt, w.reshape(1, D))


def _adamw(parts, w, m, v, *, name, tr=256, tc=None):
    plist = list(parts) if isinstance(parts, (list, tuple)) else [parts]
    P, _, C = plist[0].shape
    R = sum(a.shape[1] for a in plist)
    tr = min(tr, R)
    tc = C if tc is None else tc
    assert all(a.shape[1] % tr == 0 for a in plist) and C % tc == 0, (name, R, C, tr, tc)
    nbs = [a.shape[1] // tr for a in plist]
    offs = [sum(nbs[:l]) for l in range(len(nbs))]
    c1 = 1.0 - ADAM_B1 ** ADAM_STEP
    c2 = 1.0 - ADAM_B2 ** ADAM_STEP

    def body(*refs):
        p_refs = refs[:len(plist)]
        w_ref, m_ref, v_ref, g_ref, d_ref, nm_ref, nv_ref = refs[len(plist):]
        i = pl.program_id(0)
        g = None
        for l, p_ref in enumerate(p_refs):
            gl = p_ref[0].astype(F32)
            for k in range(1, P):
                gl = gl + p_ref[k].astype(F32)
            g = gl if g is None else jnp.where(i >= offs[l], gl, g)
        mn = ADAM_B1 * m_ref[...] + (1.0 - ADAM_B1) * g
        vn = ADAM_B2 * v_ref[...] + (1.0 - ADAM_B2) * (g * g)
        g_ref[...] = g
        nm_ref[...] = mn
        nv_ref[...] = vn
        d_ref[...] = -ADAM_LR * ((mn / c1) / (jnp.sqrt(vn / c2) + ADAM_EPS) + ADAM_WD * w_ref[...])

    spec = pl.BlockSpec((tr, tc), lambda i, j: (i, j))
    sds = jax.ShapeDtypeStruct((R, C), F32)
    return pl.pallas_call(
        body, name=name, grid=(R // tr, C // tc),
        in_specs=[pl.BlockSpec((P, tr, tc), functools.partial(lambda i, j, o, n: (0, jnp.clip(i - o, 0, n - 1), j), o=o, n=n))
                  for o, n in zip(offs, nbs)] + [spec, spec, spec],
        out_specs=[spec, spec, spec, spec], out_shape=[sds, sds, sds, sds],
        compiler_params=_cparams(("parallel", "parallel")))(*plist, w, m, v)


def _all_gather(shards, *, name):
    n = len(shards)

    def body(*refs):
        ins, outs = refs[:n], refs[n:2 * n]
        send_sems, recv_sems, local_sems = refs[2 * n:]
        x, y, c = lax.axis_index("x"), lax.axis_index("y"), lax.axis_index("c")
        me, sib = (x, y, c), (x, y, 1 - c)
        chips = [(1 - x, y), (x, 1 - y), (1 - x, 1 - y)]

        def slot(p):
            return 4 * p[0] + 2 * p[1] + p[2]

        def cp(a, k, block, to, src=None):
            dst = outs[a].at[slot(block)]
            return pltpu.make_async_remote_copy(src_ref=dst if src is None else src, dst_ref=dst,
                                                send_sem=send_sems.at[a, k], recv_sem=recv_sems.at[a, k],
                                                device_id=to, device_id_type=_MESH)

        mine = [pltpu.make_async_copy(ins[a], outs[a].at[slot(me)], local_sems.at[a]) for a in range(n)]
        for m in mine:
            m.start()
        first = []
        for a in range(n):
            first.append(cp(a, 0, me, sib, src=ins[a]))
            for j, chip in enumerate(chips):
                first.append(cp(a, 1 + j, me, (*chip, c), src=ins[a]))
        for f in first:
            f.start()
        passed = []
        for j, chip in enumerate(chips):
            for a in range(n):
                cp(a, 1 + j, (*chip, c), me).wait_recv()
                f = cp(a, 4 + j, (*chip, c), sib)
                f.start()
                passed.append(f)
        for a in range(n):
            cp(a, 0, sib, me).wait_recv()
            for j, chip in enumerate(chips):
                cp(a, 4 + j, (*chip, 1 - c), me).wait_recv()
        for f in first + passed:
            f.wait_send()
        for m in mine:
            m.wait()

    return pl.pallas_call(
        body, name=name, in_specs=[_ANY] * n, out_specs=[_ANY] * n,
        out_shape=[jax.ShapeDtypeStruct((N_DEV,) + s.shape, s.dtype) for s in shards],
        scratch_shapes=[pltpu.SemaphoreType.DMA((n, 7)), pltpu.SemaphoreType.DMA((n, 7)),
                        pltpu.SemaphoreType.DMA((n,))])(*shards)


_HBM = pl.BlockSpec(memory_space=pltpu.HBM)
_SEM = pl.BlockSpec(memory_space=pltpu.SEMAPHORE)
_EFFECT = pltpu.SideEffectType.DATAFLOW_SIDE_EFFECTING


def _peers():
    x, y, c = lax.axis_index("x"), lax.axis_index("y"), lax.axis_index("c")
    out = []
    for r in range(1, N_DEV):
        px = 1 - x if (r >> 2) & 1 else x
        py = 1 - y if (r >> 1) & 1 else y
        pc = 1 - c if r & 1 else c
        out.append(((px, py, pc), 4 * px + 2 * py + pc))
    return 4 * x + 2 * y + c, out


def _push_copy(src_ref, land_ref, send_sems, recv_sems, a, k, me, peer, peer_slot, scatter, arriving):
    src = src_ref.at[peer_slot] if scatter else src_ref
    return pltpu.make_async_remote_copy(
        src_ref=src, dst_ref=land_ref.at[peer_slot if arriving else me], send_sem=send_sems.at[a * (N_DEV - 1) + k],
        recv_sem=recv_sems.at[a * (N_DEV - 1) + k], device_id=peer, device_id_type=_MESH)


def _push_start(srcs, *, scatter, name):
    n = len(srcs)
    lands = [lax.empty(s.shape if scatter else (N_DEV,) + s.shape, s.dtype) for s in srcs]

    def body(*refs):
        src_refs, land_refs = refs[:n], refs[n:2 * n]
        send_sems, recv_sems = refs[2 * n], refs[2 * n + 1]
        token = refs[-1]
        me, peers = _peers()
        for k, (peer, slot) in enumerate(peers):
            for a in range(n):
                _push_copy(src_refs[a], land_refs[a], send_sems, recv_sems, a, k, me, peer, slot, scatter, False).start()
        token[...] = jnp.zeros_like(token)

    hbm = lambda a: pltpu.HBM(a.shape, a.dtype)
    outs = pl.pallas_call(
        body, name=name,
        out_shape=(pltpu.SemaphoreType.DMA((n * (N_DEV - 1),)), pltpu.SemaphoreType.DMA((n * (N_DEV - 1),)),
                   *[hbm(s) for s in srcs], *[hbm(l) for l in lands], jax.ShapeDtypeStruct((8, 128), F32)),
        in_specs=[_HBM] * (2 * n),
        out_specs=(_SEM, _SEM, *([_HBM] * (2 * n)), pl.BlockSpec(memory_space=pltpu.VMEM)),
        input_output_aliases={i: 2 + i for i in range(2 * n)},
        compiler_params=pltpu.CompilerParams(has_side_effects=_EFFECT),
    )(*[pltpu.with_memory_space_constraint(s, pltpu.HBM) for s in srcs],
      *[pltpu.with_memory_space_constraint(l, pltpu.HBM) for l in lands])
    return dict(send=outs[0], recv=outs[1], srcs=list(outs[2:2 + n]), lands=list(outs[2 + n:2 + 2 * n]),
                token=outs[-1], scatter=scatter, n=n)


def _push_wait(h, after, *, name):
    n, scatter = h["n"], h["scatter"]

    def body(*refs):
        src_refs, land_refs = refs[:n], refs[n:2 * n]
        send_sems, recv_sems = refs[2 * n], refs[2 * n + 1]
        me, peers = _peers()
        for k, (peer, slot) in enumerate(peers):
            for a in range(n):
                cp = _push_copy(src_refs[a], land_refs[a], send_sems, recv_sems, a, k, me, peer, slot, scatter, True)
                cp.wait_send()
                cp.wait_recv()

    hbm = lambda a: pltpu.HBM(a.shape, a.dtype)
    outs = pl.pallas_call(
        body, name=name,
        out_shape=(*[hbm(s) for s in h["srcs"]], *[hbm(l) for l in h["lands"]]),
        in_specs=[_HBM] * (2 * n) + [_SEM, _SEM, _ANY], out_specs=tuple([_HBM] * (2 * n)),
        input_output_aliases={i: i for i in range(2 * n)},
        compiler_params=pltpu.CompilerParams(has_side_effects=_EFFECT),
    )(*h["srcs"], *h["lands"], h["send"], h["recv"], after)
    return list(outs[:n]), list(outs[n:])


def _ffn_fwd(h, nw, w_up, conv_w, conv_b, w_down, tag):
    a3 = _mm_fwd(h, w_up, norm_w=nw, name=f"ffn{tag}_up", out_dtype=BF16, halves=True, w_t=True, tm=1024, tn=2816)
    p = _ffn_conv_fwd3(a3, conv_w, conv_b.reshape(1, -1), name=f"ffn{tag}_conv")
    h_out = _mm_fwd(p, w_down, residual=h, name=f"ffn{tag}_down", tm=1024, tn=512)
    return h_out, (a3, p)


def _ffn_bwd(dh, h, saved, nw, w_up, conv_w, conv_b, w_down, tag):
    a3, p = saved
    g_down = _mm_tn(p, dh, name=f"ffn{tag}_down_wg", tk1=1408, tn=1024)
    dp = _mm_nt(dh, w_down, name=f"ffn{tag}_down_dg", out_dtype=BF16, tm=512, tn=2816, tk=1024)
    dhid3, dw3, db3 = _ffn_conv_bwd3(a3, conv_w, conv_b.reshape(1, -1), dp, name=f"ffn{tag}_conv_bwd")
    da3 = _conv_bwd_in3(dhid3, conv_w, K=FFN_CONV, name=f"ffn{tag}_conv_bwd_in")
    g_up = _mm_tn_t(da3, h, norm_w=nw, name=f"ffn{tag}_up_wg", tn=1408, tt=1024)
    dh_out, g_nw = _mm_nt(da3, w_up, epi=(h, nw, dh), name=f"ffn{tag}_up_dg", w_t=True, tm=1024, tk=1408)
    g_cw = jnp.concatenate([dw3[0], dw3[1]], axis=1)
    g_cb = jnp.concatenate([db3[0], db3[1]], axis=1)
    return dh_out, dict(norm=g_nw.reshape(-1), up=g_up, conv_w=g_cw, conv_b=g_cb.reshape(-1), down=g_down)


_BIG = ["ssm_in_w", "ssm_out_w", "w_k", "w_v", "w_q", "w_o", "ffn_up_w", "ffn_down_w"]
_SMALL_SHARDED = ["ssm_norm_w", "ssm_conv_w", "ssm_conv_b", "ssm_gate_norm_w", "ffn_conv_w"]
_SMALL_REPL = ["ssm_dt_bias", "ssm_a_log", "ssm_d", "kv_norm_w", "attn_norm_w", "ffn_norm_w", "ffn_conv_b",
               "final_norm_w"]
_WEIGHTS = ["ssm_norm_w", "ssm_in_w", "ssm_conv_w", "ssm_conv_b", "ssm_dt_bias", "ssm_a_log", "ssm_d",
            "ssm_gate_norm_w", "ssm_out_w", "kv_norm_w", "w_k", "w_v", "attn_norm_w", "w_q", "w_o", "ffn_norm_w",
            "ffn_up_w", "ffn_conv_w", "ffn_conv_b", "ffn_down_w", "final_norm_w"]


def _as2d(a):
    return a.reshape(-1, a.shape[-1])


def _cols_to_full(g):
    return g.transpose(1, 0, 2).reshape(g.shape[1], N_DEV * g.shape[2])


def _pack_small(vals):
    flat = jnp.concatenate([v.reshape(-1).astype(F32) for v in vals])
    n = flat.shape[0]
    rows = -(-n // 1024) * 8
    return jnp.pad(flat, (0, rows * 128 - n)).reshape(rows, 128)


def _unpack_small(packed, shapes):
    flat = packed.reshape(-1)
    out, off = [], 0
    for s in shapes:
        n = math.prod(s)
        out.append(flat[off:off + n].reshape(s))
        off += n
    return out


def _tie(a, token):
    return a + token[0, 0].astype(a.dtype)


def _local_step(x, tgt, get_w, put_g):
    T = x.shape[0]
    Ws = get_w("ssm", None)
    fnw, fcw, fcb = Ws["ffn_norm_w"], Ws["ffn_conv_w"], Ws["ffn_conv_b"]
    zx = _mm_fwd(x, Ws["in_w"], norm_w=Ws["ssm_norm_w"], name="ssm_in", w_t=True, tm=1024, tn=1792)
    xbc_c = _ssm_conv_fwd(zx, Ws["ssm_conv_w"], Ws["ssm_conv_b"].reshape(1, -1), name="ssm_conv")
    dt_raw = zx[:, D_INNER + CONV_DIM:IN_PROJ_DIM]
    dtg = jnp.pad(dt_raw.reshape(T, SSM_GROUPS, 8).transpose(1, 0, 2), ((0, 0), (0, 0), (0, 120)))
    par = jnp.stack([Ws["ssm_dt_bias"].reshape(SSM_GROUPS, 8), Ws["ssm_a_log"].reshape(SSM_GROUPS, 8),
                     Ws["ssm_d"].reshape(SSM_GROUPS, 8)], axis=1)
    par = jnp.pad(par, ((0, 0), (0, 5), (0, 120)))
    gnw = _tie(Ws["ssm_gate_norm_w"].reshape(1, D_INNER), get_w("rest_start", xbc_c))
    y, yn, st = _ssd_fwd(xbc_c, zx, dtg, par, gnw, name="ssd_fwd")
    W0 = get_w("ffn0", y)
    Ws["ssm_out_w"] = W0["ssm_out_w"]
    h1 = _mm_fwd(yn, Ws["ssm_out_w"], residual=x, name="ssm_out", tm=1024, tn=512)
    h2, ffn0 = _ffn_fwd(h1, fnw[0], W0["up"], fcw[0], fcb[0], W0["down"], "0")
    Wr = get_w("rest", h2)
    q = _mm_fwd(h2, Wr["w_q"], norm_w=Ws["attn_norm_w"], out_dtype=BF16, name="attn_q", tm=1024, tn=1024)
    kv = _mm_fwd(h2, Wr["w_kv"], norm_w=Ws["kv_norm_w"], out_dtype=BF16, name="attn_kv", tm=1024, tn=1024)
    o, lt = _sba_fwd(q, kv, name="sba_fwd")
    h3 = _mm_fwd(o, Wr["w_o"], residual=h2, name="attn_o", tm=1024, tn=512)
    h4, ffn1 = _ffn_fwd(h3, fnw[1], Wr["up"], fcw[1], fcb[1], Wr["down"], "1")
    loss, dh4, g_final = _loss_head(h4, tgt, Ws["final_norm_w"], name="loss_head")
    dh3, gf1 = _ffn_bwd(dh4, h3, ffn1, fnw[1], Wr["up"], fcw[1], fcb[1], Wr["down"], "1")
    tok = put_g("ffn1", dict(up=gf1["up"], down=gf1["down"]))
    g_wo = _mm_tn(o, dh3, name="attn_o_wg", tn=1024)
    do = _mm_nt(dh3, _tie(Wr["w_o"], tok), name="attn_o_dg", out_dtype=BF16, tn=1024, tk=1024)
    dq, dk, dv = _sba_bwd(q, kv, lt, do, name="sba_bwd")
    g_wq = _mm_tn(h2, dq, norm_w=Ws["attn_norm_w"], name="attn_q_wg", tn=1024, tt=1024)
    dh2a, g_attn_nw = _mm_nt(dq, Wr["w_q"], epi=(h2, Ws["attn_norm_w"], dh3), name="attn_q_dg", tm=1024, tk=1024)
    dkv = jnp.concatenate([dk, dv], axis=1)
    g_wkv = _mm_tn(h2, dkv, norm_w=Ws["kv_norm_w"], name="attn_kv_wg", tn=1024, tt=1024)
    dh2, g_kv_nw = _mm_nt(dkv, Wr["w_kv"], epi=(h2, Ws["kv_norm_w"], dh2a), name="attn_kv_dg", tm=1024, tk=1024)
    tok = put_g("attn", dict(w_o=g_wo, w_q=g_wq, w_k=g_wkv[:, :D_MODEL], w_v=g_wkv[:, D_MODEL:]))
    dh1, gf0 = _ffn_bwd(dh2, h1, ffn0, fnw[0], W0["up"], fcw[0], _tie(fcb[0], tok), W0["down"], "0")
    tok = put_g("ffn0", dict(up=gf0["up"], down=gf0["down"]))
    g_out = _mm_tn(yn, dh1, name="ssm_out_wg", tn=1024)
    dyn = _mm_nt(dh1, _tie(Ws["ssm_out_w"], tok), name="ssm_out_dg", out_dtype=BF16, tn=1024, tk=1024)
    tok = put_g("ssm_out", dict(ssm_out_w=g_out))
    dxbc_c, dz, ddt, g_gnw, dpar = _ssd_bwd(xbc_c, zx, dtg, par, _tie(gnw, tok), y, st, dyn, name="ssd_bwd")
    dhid, g_scw, g_scb = _ssm_conv_bwd_pre(zx, Ws["ssm_conv_w"], Ws["ssm_conv_b"].reshape(1, -1), dxbc_c,
                                           name="ssm_conv_bwd")
    dzx = _conv_bwd_in(dhid, Ws["ssm_conv_w"], K=SSM_CONV, name="ssm_conv_bwd_in", into=(dz, D_INNER))
    ddt_t = ddt[:, :, :8].transpose(1, 0, 2).reshape(T, SSM_HEADS).astype(BF16)
    dzx = _put_cols(dzx, jnp.pad(ddt_t, ((0, 0), (0, IN_PROJ_PAD - IN_PROJ_DIM))), D_INNER + CONV_DIM, name="ssm_ddt_cols")
    g_in = _mm_tn_t(dzx, x, norm_w=Ws["ssm_norm_w"], name="ssm_in_wg", tn=1792, tt=1024)
    tok = put_g("ssm_in", dict(ssm_in_w=g_in[:IN_PROJ_DIM]))
    dx, g_ssm_nw = _mm_nt(dzx, Ws["in_w"], epi=(x, _tie(Ws["ssm_norm_w"], tok), dh1), name="ssm_in_dg", w_t=True,
                          tm=1024, tk=1792)
    f = {
        "ssm_norm_w": g_ssm_nw.reshape(-1), "ssm_conv_w": g_scw,
        "ssm_conv_b": g_scb.reshape(-1), "ssm_dt_bias": dpar[:, 0, :8].reshape(-1),
        "ssm_a_log": dpar[:, 1, :8].reshape(-1), "ssm_d": dpar[:, 2, :8].reshape(-1),
        "ssm_gate_norm_w": g_gnw.reshape(-1), "kv_norm_w": g_kv_nw.reshape(-1), "attn_norm_w": g_attn_nw.reshape(-1),
        "ffn_norm_w": jnp.stack([gf0["norm"], gf1["norm"]]), "ffn_conv_w": jnp.stack([gf0["conv_w"], gf1["conv_w"]]),
        "ffn_conv_b": jnp.stack([gf0["conv_b"], gf1["conv_b"]]), "final_norm_w": g_final.reshape(-1),
    }
    return loss, dx, f


def kernel(x, ssm_norm_w, ssm_in_w, ssm_conv_w, ssm_conv_b, ssm_dt_bias, ssm_a_log, ssm_d, ssm_gate_norm_w, ssm_out_w, kv_norm_w, w_k, w_v, attn_norm_w, w_q, w_o, ffn_norm_w, ffn_up_w, ffn_conv_w, ffn_conv_b, ffn_down_w, final_norm_w, loss_target, m_ssm_norm_w, m_ssm_in_w, m_ssm_conv_w, m_ssm_conv_b, m_ssm_dt_bias, m_ssm_a_log, m_ssm_d, m_ssm_gate_norm_w, m_ssm_out_w, m_kv_norm_w, m_w_k, m_w_v, m_attn_norm_w, m_w_q, m_w_o, m_ffn_norm_w, m_ffn_up_w, m_ffn_conv_w, m_ffn_conv_b, m_ffn_down_w, m_final_norm_w, v_ssm_norm_w, v_ssm_in_w, v_ssm_conv_w, v_ssm_conv_b, v_ssm_dt_bias, v_ssm_a_log, v_ssm_d, v_ssm_gate_norm_w, v_ssm_out_w, v_kv_norm_w, v_w_k, v_w_v, v_attn_norm_w, v_w_q, v_w_o, v_ffn_norm_w, v_ffn_up_w, v_ffn_conv_w, v_ffn_conv_b, v_ffn_down_w, v_final_norm_w):
    env = dict(locals())
    p = {n: env[n] for n in _WEIGHTS}
    mom = {n: env["m_" + n] for n in _WEIGHTS}
    var = {n: env["v_" + n] for n in _WEIGHTS}
    T = x.shape[1]
    me = 4 * lax.axis_index("x") + 2 * lax.axis_index("y") + lax.axis_index("c")
    rs = D_FF // N_DEV

    def bf2(a):
        return _as2d(a).astype(BF16)

    _T = ("ssm_in_w", "ffn_up_w")

    def t2d(a):
        return jnp.swapaxes(a, -1, -2).reshape(-1, a.shape[-2])

    def from_t2d(a, like):
        return jnp.swapaxes(a.reshape(like.shape[:-2] + (like.shape[-1], like.shape[-2])), -1, -2)

    n_in, n_up = p["ssm_in_w"].shape[-1], p["ffn_up_w"].shape[-1]

    def with_own(srcs, lands, scatter):
        out = []
        for s, l in zip(srcs, lands):
            own = lax.dynamic_index_in_dim(s, me, 0, keepdims=False) if scatter else s
            out.append(lax.dynamic_update_index_in_dim(l, own, me, 0))
        return out

    a_names = ["ssm_in_w"] + _SMALL_SHARDED
    in_shard = jnp.pad(t2d(p["ssm_in_w"]).astype(BF16), ((0, -n_in % 16), (0, 0)))
    got_a = dict(zip(a_names, _all_gather([in_shard] + [_as2d(p[n]) for n in _SMALL_SHARDED], name="gather_ssm")))
    ffn0_names = ["ssm_out_w", "up0", "down0"]
    rest_names = ["w_q", "w_k", "w_v", "w_o", "up1", "down1"]
    up_t = jnp.swapaxes(p["ffn_up_w"], -1, -2).astype(BF16)
    shard = {"up0": up_t[0], "down0": bf2(p["ffn_down_w"][0]), "up1": up_t[1],
             "down1": bf2(p["ffn_down_w"][1]), "w_q": bf2(p["w_q"]), "w_k": bf2(p["w_k"]), "w_v": bf2(p["w_v"]),
             "w_o": bf2(p["w_o"]), "ssm_out_w": bf2(p["ssm_out_w"])}
    h_ffn0 = _push_start([shard[n] for n in ffn0_names], scatter=False, name="gather_ffn0_start")
    handles = {}

    def get_w(group, after):
        if group == "ssm":
            W = {n: p[n] for n in _SMALL_REPL}
            for n in ("ssm_dt_bias", "ssm_a_log", "ssm_d", "attn_norm_w"):
                W[n] = W[n].reshape(-1)
            W["in_w"] = jnp.pad(got_a["ssm_in_w"][:, :n_in].reshape(IN_PROJ_DIM, D_MODEL),
                                ((0, IN_PROJ_PAD - IN_PROJ_DIM), (0, 0)))
            W["ssm_norm_w"] = _tie(got_a["ssm_norm_w"].reshape(D_MODEL), h_ffn0["token"])
            W["ssm_conv_w"] = _cols_to_full(got_a["ssm_conv_w"])
            W["ssm_conv_b"] = got_a["ssm_conv_b"].reshape(CONV_DIM)
            W["ssm_gate_norm_w"] = got_a["ssm_gate_norm_w"].reshape(D_INNER)
            W["ffn_conv_w"] = _cols_to_full(got_a["ffn_conv_w"]).reshape(2, FFN_CONV, 2 * D_FF)
            return W
        if group == "rest_start":
            anchor = after[0, 0]
            first = shard[rest_names[0]] + (jnp.where(jnp.isfinite(anchor), anchor, 0.0) * 0.0).astype(BF16)
            handles["rest"] = _push_start([first] + [shard[n] for n in rest_names[1:]], scatter=False,
                                          name="gather_rest_start")
            return handles["rest"]["token"]
        if group == "ffn0":
            srcs, lands = _push_wait(h_ffn0, after, name="gather_ffn0_wait")
            out, up, down = with_own(srcs, lands, False)
            return dict(ssm_out_w=out.reshape(D_INNER, D_MODEL), up=up.reshape(2 * D_FF, D_MODEL),
                        down=down.reshape(D_FF, D_MODEL))
        srcs, lands = _push_wait(handles["rest"], after, name="gather_rest_wait")
        g = dict(zip(rest_names, with_own(srcs, lands, False)))
        sq = lambda a: a.reshape(D_MODEL, D_MODEL)
        return dict(w_q=sq(g["w_q"]), w_kv=jnp.concatenate([sq(g["w_k"]), sq(g["w_v"])], axis=1), w_o=sq(g["w_o"]),
                    up=g["up1"].reshape(2 * D_FF, D_MODEL), down=g["down1"].reshape(D_FF, D_MODEL))

    pending = []

    def put_g(group, g):
        if group in ("ffn0", "ffn1"):
            keys = [("ffn_up_w", int(group[-1])), ("ffn_down_w", int(group[-1]))]
            blocks = [g["up"].reshape(N_DEV, n_up, D_MODEL), g["down"].reshape(N_DEV, rs, D_MODEL)]
        elif group == "attn":
            keys = [(n, None) for n in ("w_o", "w_q", "w_k", "w_v")]
            blocks = [g[n].reshape(N_DEV, D_MODEL // N_DEV, D_MODEL) for n, _ in keys]
        elif group == "ssm_out":
            keys = [("ssm_out_w", None)]
            blocks = [g["ssm_out_w"].reshape(N_DEV, D_INNER // N_DEV, D_MODEL)]
        else:
            keys = [("ssm_in_w", None)]
            blocks = [g["ssm_in_w"].reshape(N_DEV, n_in, D_MODEL)]
        h = _push_start(blocks, scatter=True, name=f"exchange_{group}_start")
        pending.append((group, keys, h))
        return h["token"]

    loss_row, dx, f = _local_step(x.reshape(T, D_MODEL), loss_target.reshape(T, D_MODEL), get_w, put_g)

    small_names = _SMALL_REPL + _SMALL_SHARDED
    small_full = _pack_small([f[n] for n in small_names] + [loss_row[0, 0:1]])
    small_bcast = jnp.broadcast_to(small_full[None], (N_DEV,) + small_full.shape)
    h_small = _push_start([small_bcast], scatter=True, name="exchange_small_start")
    tok = h_small["token"]

    arrived, res = {}, {}
    after = dx
    for group, keys, h in pending:
        srcs, lands = _push_wait(h, after, name=f"exchange_{group}_wait")
        arrived.update(zip(keys, with_own(srcs, lands, True)))
        for n in _BIG:
            layered = (n, 0) in arrived or (n, 1) in arrived
            if n in res or not ((n, None) in arrived or ((n, 0) in arrived and (n, 1) in arrived)):
                continue
            parts = [arrived[(n, 0)], arrived[(n, 1)]] if layered else arrived[(n, None)]
            w2, m2, v2 = ((t2d if n in _T else _as2d)(a[n]) for a in (p, mom, var))
            if not res:
                w2 = _tie(w2, tok)
            tiles = {"ffn_down_w": dict(tr=rs), "ffn_up_w": dict(tr=n_up // 2), "ssm_in_w": dict(tr=n_in, tc=256)}
            res[n] = _adamw(parts, w2, m2, v2, name=f"adamw_{n}", **tiles.get(n, dict(tr=256)))
            after = res[n][0]
    srcs, lands = _push_wait(h_small, after, name="exchange_small_wait")
    small_parts = with_own(srcs, lands, True)[0]
    out_g, out_d, out_m, out_v = {}, {}, {}, {}
    for n in _BIG:
        out_g[n], out_d[n], out_m[n], out_v[n] = (from_t2d(t, p[n]) if n in _T else t.reshape(p[n].shape) for t in res[n])

    zero = jnp.zeros_like(small_full)
    g_small_sum = _adamw(small_parts, zero, zero, zero, name="sum_small_grads", tr=small_full.shape[0])[0]
    *small_sums, loss_sum = _unpack_small(g_small_sum, [f[n].shape for n in small_names] + [(1,)])
    loss = loss_sum[0]
    g_small = dict(zip(small_names, small_sums))
    for n in _SMALL_SHARDED:
        width = p[n].shape[-1]
        g_small[n] = lax.dynamic_slice_in_dim(g_small[n], me * width, width, axis=g_small[n].ndim - 1)
    sw = _pack_small([p[n] for n in small_names])
    sm = _pack_small([mom[n] for n in small_names])
    sv = _pack_small([var[n] for n in small_names])
    sg = _pack_small([g_small[n] for n in small_names])
    _, d, nm, nv = _adamw(sg[None], sw, sm, sv, name="adamw_small", tr=sw.shape[0])
    shard_shapes = [p[n].shape for n in small_names]
    for n, dd, mm, vv in zip(small_names, _unpack_small(d, shard_shapes), _unpack_small(nm, shard_shapes),
                             _unpack_small(nv, shard_shapes)):
        out_g[n] = g_small[n].reshape(p[n].shape)
        out_d[n], out_m[n], out_v[n] = dd, mm, vv

    return (loss, dx.reshape(x.shape), *[out_g[n] for n in _WEIGHTS], *[out_d[n] for n in _WEIGHTS],
            *[out_m[n] for n in _WEIGHTS], *[out_v[n] for n in _WEIGHTS])
```

```python
import functools
import math

import jax
import jax.numpy as jnp
from jax import lax
from jax.experimental import pallas as pl
from jax.experimental.pallas import tpu as pltpu

F32 = jnp.float32
BF16 = jnp.bfloat16
EPS = 1e-6

D_MODEL = 1024
D_INNER = 2048
SSM_HEADS = 32
SSM_GROUPS = 4
SSM_STATE = 128
SSM_CONV = 4
SSM_CHUNK = 128
GN = SSM_GROUPS * SSM_STATE
CONV_DIM = D_INNER + 2 * GN
IN_PROJ_DIM = D_INNER + CONV_DIM + SSM_HEADS
IN_PROJ_PAD = 5376
SB_HEADS = 16
SB_HEAD_DIM = 64
SB_BLOCK = 128
D_FF = 2816
FFN_CONV = 3
N_DEV = 8

ADAM_LR = 0.001
ADAM_B1 = 0.9
ADAM_B2 = 0.999
ADAM_EPS = 1e-08
ADAM_WD = 0.01
ADAM_STEP = 10

_MESH = pl.DeviceIdType.MESH
_NT = (((1,), (1,)), ((), ()))
_TN = (((0,), (0,)), ((), ()))
_ANY = pl.BlockSpec(memory_space=pl.ANY)


def _cparams(sem, vmem_mb=48):
    return pltpu.CompilerParams(dimension_semantics=sem, vmem_limit_bytes=vmem_mb * 1024 * 1024)


def _sigmoid(x):
    return 0.5 * jnp.tanh(0.5 * x) + 0.5


def _softplus(x):
    return jnp.maximum(x, 0.0) + jnp.log(1.0 + jnp.exp(-jnp.abs(x)))


def _rms_fwd(xv, w):
    r = lax.rsqrt(jnp.mean(xv * xv, axis=-1, keepdims=True) + EPS)
    return xv * r * w


def _mm_fwd(x, w, *, name, norm_w=None, residual=None, out_dtype=F32, tm=512, tn=512, halves=False, w_t=False):
    M, K = x.shape
    N = w.shape[0] if w_t else w.shape[1]
    tm, tn = min(tm, M), min(tn, N)
    assert M % tm == 0 and N % tn == 0, (name, M, N, tm, tn)
    if halves:
        nbh = N // 2 // tn
        assert N // 2 % tn == 0
        out_spec = pl.BlockSpec((None, tm, tn), lambda i, j: (lax.div(j, nbh), i, lax.rem(j, nbh)))
        out_shape = jax.ShapeDtypeStruct((2, M, N // 2), out_dtype)
    else:
        out_spec = pl.BlockSpec((tm, tn), lambda i, j: (i, j))
        out_shape = jax.ShapeDtypeStruct((M, N), out_dtype)
    has_norm, has_res = norm_w is not None, residual is not None

    def body(*refs):
        x_ref, w_ref = refs[0], refs[1]
        p = 2
        nw_ref = r_ref = None
        if has_norm:
            nw_ref = refs[p]
            p += 1
        if has_res:
            r_ref = refs[p]
            p += 1
        o_ref = refs[p]
        xv = x_ref[...]
        if has_norm:
            xv = _rms_fwd(xv.astype(F32), nw_ref[...])
        acc = lax.dot_general(xv.astype(BF16), w_ref[...], _NT if w_t else (((1,), (0,)), ((), ())),
                              preferred_element_type=F32)
        if has_res:
            acc = acc + r_ref[...]
        o_ref[...] = acc.astype(out_dtype)

    w_spec = pl.BlockSpec((tn, K), lambda i, j: (j, 0)) if w_t else pl.BlockSpec((K, tn), lambda i, j: (0, j))
    in_specs = [pl.BlockSpec((tm, K), lambda i, j: (i, 0)), w_spec]
    args = [x, w]
    if has_norm:
        in_specs.append(pl.BlockSpec((1, K), lambda i, j: (0, 0)))
        args.append(norm_w.reshape(1, K))
    if has_res:
        in_specs.append(pl.BlockSpec((tm, tn), lambda i, j: (i, j)))
        args.append(residual)
    return pl.pallas_call(
        body, name=name, grid=(M // tm, N // tn), in_specs=in_specs,
        out_specs=out_spec, out_shape=out_shape,
        compiler_params=_cparams(("parallel", "parallel")))(*args)


def _mm_nt(dy, w, *, name, epi=None, out_dtype=F32, tm=512, tn=512, tk=512, w_t=False):
    halves = dy.ndim == 3
    M, K = (dy.shape[1], 2 * dy.shape[2]) if halves else dy.shape
    N = w.shape[1] if w_t else w.shape[0]
    tm, tk = min(tm, M), min(tk, K)
    tn = N if epi is not None else min(tn, N)
    assert M % tm == 0 and N % tn == 0 and K % tk == 0, (name, M, N, K, tm, tn, tk)
    nk = K // tk
    has_epi = epi is not None

    def body(*refs):
        if has_epi:
            dy_ref, w_ref, h_ref, nw_ref, r_ref, o_ref, dnw_ref, acc_ref = refs
        else:
            dy_ref, w_ref, o_ref, acc_ref = refs
        i = pl.program_id(0)
        k = pl.program_id(2)

        @pl.when(k == 0)
        def _():
            acc_ref[...] = jnp.zeros_like(acc_ref)

        acc_ref[...] += lax.dot_general(dy_ref[...].astype(BF16), w_ref[...], (((1,), (0,)), ((), ())) if w_t else _NT,
                                        preferred_element_type=F32)

        @pl.when(k == nk - 1)
        def _():
            du = acc_ref[...]
            if has_epi:
                hv = h_ref[...]
                r = lax.rsqrt(jnp.mean(hv * hv, axis=-1, keepdims=True) + EPS)
                xhat = hv * r
                dxh = du * nw_ref[...]
                dx = r * (dxh - xhat * jnp.mean(dxh * xhat, axis=-1, keepdims=True))
                o_ref[...] = (r_ref[...] + dx).astype(out_dtype)
                contrib = jnp.sum(du * xhat, axis=0, keepdims=True)

                @pl.when(i == 0)
                def _():
                    dnw_ref[...] = contrib

                @pl.when(i > 0)
                def _():
                    dnw_ref[...] += contrib
            else:
                o_ref[...] = du.astype(out_dtype)

    if halves:
        nkh = K // 2 // tk
        assert K // 2 % tk == 0
        dy_spec = pl.BlockSpec((None, tm, tk), lambda i, j, k: (lax.div(k, nkh), i, lax.rem(k, nkh)))
    else:
        dy_spec = pl.BlockSpec((tm, tk), lambda i, j, k: (i, k))
    w_spec = pl.BlockSpec((tk, tn), lambda i, j, k: (k, j)) if w_t else pl.BlockSpec((tn, tk), lambda i, j, k: (j, k))
    in_specs = [dy_spec, w_spec]
    args = [dy, w]
    out_specs = [pl.BlockSpec((tm, tn), lambda i, j, k: (i, j))]
    out_shape = [jax.ShapeDtypeStruct((M, N), out_dtype)]
    if has_epi:
        h, nw, res = epi
        in_specs += [pl.BlockSpec((tm, N), lambda i, j, k: (i, 0)), pl.BlockSpec((1, N), lambda i, j, k: (0, 0)),
                     pl.BlockSpec((tm, N), lambda i, j, k: (i, 0))]
        args += [h, nw.reshape(1, N), res]
        out_specs.append(pl.BlockSpec((1, N), lambda i, j, k: (0, 0)))
        out_shape.append(jax.ShapeDtypeStruct((1, N), F32))
    outs = pl.pallas_call(
        body, name=name, grid=(M // tm, N // tn, nk), in_specs=in_specs, out_specs=out_specs, out_shape=out_shape,
        scratch_shapes=[pltpu.VMEM((tm, tn), F32)],
        compiler_params=_cparams(("arbitrary", "arbitrary", "arbitrary")))(*args)
    return (outs[0], outs[1]) if has_epi else outs[0]


def _mm_tn(x, dy, *, name, norm_w=None, out_dtype=BF16, tk1=1024, tn=512, tt=512):
    T, K1 = x.shape
    halves = dy.ndim == 3
    N = 2 * dy.shape[2] if halves else dy.shape[1]
    tk1, tn, tt = min(tk1, K1), min(tn, N), min(tt, T)
    has_norm = norm_w is not None
    assert K1 % tk1 == 0 and N % tn == 0 and T % tt == 0, (name, K1, N, T, tk1, tn, tt)
    assert not has_norm or tk1 == K1
    nt = T // tt

    def body(*refs):
        if has_norm:
            x_ref, dy_ref, nw_ref, o_ref, acc_ref = refs
        else:
            x_ref, dy_ref, o_ref, acc_ref = refs
        t = pl.program_id(2)

        @pl.when(t == 0)
        def _():
            acc_ref[...] = jnp.zeros_like(acc_ref)

        xv = x_ref[...]
        if has_norm:
            xv = _rms_fwd(xv.astype(F32), nw_ref[...])
        acc_ref[...] += lax.dot_general(xv.astype(BF16), dy_ref[...].astype(BF16), _TN, preferred_element_type=F32)

        @pl.when(t == nt - 1)
        def _():
            o_ref[...] = acc_ref[...].astype(out_dtype)

    if halves:
        nbh = N // 2 // tn
        assert N // 2 % tn == 0
        dy_spec = pl.BlockSpec((None, tt, tn), lambda a, b, t: (lax.div(b, nbh), t, lax.rem(b, nbh)))
    else:
        dy_spec = pl.BlockSpec((tt, tn), lambda a, b, t: (t, b))
    in_specs = [pl.BlockSpec((tt, tk1), lambda a, b, t: (t, a)), dy_spec]
    args = [x, dy]
    if has_norm:
        in_specs.append(pl.BlockSpec((1, K1), lambda a, b, t: (0, 0)))
        args.append(norm_w.reshape(1, K1))
    return pl.pallas_call(
        body, name=name, grid=(K1 // tk1, N // tn, nt), in_specs=in_specs,
        out_specs=pl.BlockSpec((tk1, tn), lambda a, b, t: (a, b)),
        out_shape=jax.ShapeDtypeStruct((K1, N), out_dtype),
        scratch_shapes=[pltpu.VMEM((tk1, tn), F32)],
        compiler_params=_cparams(("parallel", "parallel", "arbitrary")))(*args)


def _mm_tn_t(dy, x, *, name, norm_w, out_dtype=BF16, tn=1408, tt=1024):
    T, K1 = x.shape
    halves = dy.ndim == 3
    N = 2 * dy.shape[2] if halves else dy.shape[1]
    tn, tt = min(tn, N), min(tt, T)
    assert N % tn == 0 and T % tt == 0, (name, N, T, tn, tt)
    nt = T // tt

    def body(dy_ref, x_ref, nw_ref, o_ref, acc_ref):
        t = pl.program_id(1)

        @pl.when(t == 0)
        def _():
            acc_ref[...] = jnp.zeros_like(acc_ref)

        xn = _rms_fwd(x_ref[...].astype(F32), nw_ref[...]).astype(BF16)
        acc_ref[...] += lax.dot_general(dy_ref[...].astype(BF16), xn, _TN, preferred_element_type=F32)

        @pl.when(t == nt - 1)
        def _():
            o_ref[...] = acc_ref[...].astype(out_dtype)

    if halves:
        nbh = N // 2 // tn
        assert N // 2 % tn == 0
        dy_spec = pl.BlockSpec((None, tt, tn), lambda b, t: (lax.div(b, nbh), t, lax.rem(b, nbh)))
    else:
        dy_spec = pl.BlockSpec((tt, tn), lambda b, t: (t, b))
    return pl.pallas_call(
        body, name=name, grid=(N // tn, nt),
        in_specs=[dy_spec, pl.BlockSpec((tt, K1), lambda b, t: (t, 0)), pl.BlockSpec((1, K1), lambda b, t: (0, 0))],
        out_specs=pl.BlockSpec((tn, K1), lambda b, t: (b, 0)),
        out_shape=jax.ShapeDtypeStruct((N, K1), out_dtype),
        scratch_shapes=[pltpu.VMEM((tn, K1), F32)],
        compiler_params=_cparams(("parallel", "arbitrary")))(dy, x, norm_w.reshape(1, K1))


def _shift_down(xb, prev8, j):
    main = pltpu.roll(xb, j, 0)
    head = pltpu.roll(xb[0:8], j, 0)
    ph = pltpu.roll(prev8, j, 0)
    row8 = lax.broadcasted_iota(jnp.int32, head.shape, 0)
    head = jnp.where(row8 < j, ph, head)
    return jnp.concatenate([head, main[8:]], axis=0)


def _shift_up(xb, next8, j):
    tt = xb.shape[0]
    main = pltpu.roll(xb, tt - j, 0)
    tail = pltpu.roll(xb[tt - 8:tt], 8 - j, 0)
    nh = pltpu.roll(next8, 8 - j, 0)
    row8 = lax.broadcasted_iota(jnp.int32, tail.shape, 0)
    tail = jnp.where(row8 + j >= 8, nh, tail)
    return jnp.concatenate([main[:tt - 8], tail], axis=0)


def _conv_hid(xb, prev8, w, b_row, K):
    out = b_row
    shifted = []
    for j in range(K):
        sh = K - 1 - j
        xs = xb if sh == 0 else _shift_down(xb, prev8, sh)
        shifted.append(xs)
        out = out + xs * w[j:j + 1, :]
    return out, shifted


def _prev_idx(i, nb8):
    return jnp.maximum(i * nb8 - 1, 0)


def _ssm_conv_fwd(zx, w, b, *, name, tt=512, tc=512):
    T = zx.shape[0]
    tt = min(tt, T)
    C, K = CONV_DIM, SSM_CONV
    cb0, nb8 = D_INNER // tc, tt // 8

    def body(x_ref, p_ref, w_ref, b_ref, o_ref):
        first = (pl.program_id(1) > 0).astype(F32)
        hid, _ = _conv_hid(x_ref[...], p_ref[...] * first, w_ref[...], b_ref[...], K)
        o_ref[...] = hid * _sigmoid(hid)

    return pl.pallas_call(
        body, name=name, grid=(C // tc, T // tt),
        in_specs=[pl.BlockSpec((tt, tc), lambda c, i: (i, c + cb0)),
                  pl.BlockSpec((8, tc), lambda c, i: (_prev_idx(i, nb8), c + cb0)),
                  pl.BlockSpec((K, tc), lambda c, i: (0, c)), pl.BlockSpec((1, tc), lambda c, i: (0, c))],
        out_specs=pl.BlockSpec((tt, tc), lambda c, i: (i, c)),
        out_shape=jax.ShapeDtypeStruct((T, C), F32),
        compiler_params=_cparams(("parallel", "parallel")))(zx, zx, w, b)


def _ssm_conv_bwd_pre(zx, w, b, dout, *, name, tt=512, tc=512):
    T = zx.shape[0]
    tt = min(tt, T)
    C, K = CONV_DIM, SSM_CONV
    cb0, nb8 = D_INNER // tc, tt // 8

    def body(x_ref, p_ref, w_ref, b_ref, d_ref, dh_ref, dw_ref, db_ref):
        t = pl.program_id(1)
        first = (t > 0).astype(F32)
        hid, shifted = _conv_hid(x_ref[...], p_ref[...] * first, w_ref[...], b_ref[...], K)
        sg = _sigmoid(hid)
        dh = d_ref[...] * (sg * (1.0 + hid * (1.0 - sg)))
        dh_ref[...] = dh

        @pl.when(t == 0)
        def _():
            dw_ref[...] = jnp.zeros_like(dw_ref)
            db_ref[...] = jnp.zeros_like(db_ref)

        db_ref[...] += jnp.sum(dh, axis=0, keepdims=True)
        for j in range(K):
            dw_ref[j:j + 1, :] += jnp.sum(dh * shifted[j], axis=0, keepdims=True)

    return pl.pallas_call(
        body, name=name, grid=(C // tc, T // tt),
        in_specs=[pl.BlockSpec((tt, tc), lambda c, i: (i, c + cb0)),
                  pl.BlockSpec((8, tc), lambda c, i: (_prev_idx(i, nb8), c + cb0)),
                  pl.BlockSpec((K, tc), lambda c, i: (0, c)), pl.BlockSpec((1, tc), lambda c, i: (0, c)),
                  pl.BlockSpec((tt, tc), lambda c, i: (i, c))],
        out_specs=[pl.BlockSpec((tt, tc), lambda c, i: (i, c)), pl.BlockSpec((K, tc), lambda c, i: (0, c)),
                   pl.BlockSpec((1, tc), lambda c, i: (0, c))],
        out_shape=[jax.ShapeDtypeStruct((T, C), F32), jax.ShapeDtypeStruct((K, C), F32),
                   jax.ShapeDtypeStruct((1, C), F32)],
        compiler_params=_cparams(("parallel", "arbitrary")))(zx, zx, w, b, dout)


def _put_cols(buf, src, col0, *, name, tt=512):
    T, C = src.shape
    tt = min(tt, T)

    def body(s_ref, _, o_ref):
        o_ref[...] = s_ref[...]

    return pl.pallas_call(
        body, name=name, grid=(T // tt,),
        in_specs=[pl.BlockSpec((tt, C), lambda i: (i, 0)), _ANY],
        out_specs=pl.BlockSpec((tt, C), lambda i: (i, col0 // C)),
        out_shape=jax.ShapeDtypeStruct(buf.shape, buf.dtype), input_output_aliases={1: 0},
        compiler_params=_cparams(("parallel",)))(src, buf)


def _conv_bwd_in(dh, w, *, name, K, tt=512, tc=512, out_dtype=BF16, into=None):
    T, C = dh.shape
    tt = min(tt, T)
    nb8, nT = tt // 8, T // tt
    last8 = T // 8 - 1
    cb0 = 0 if into is None else into[1] // tc

    def body(d_ref, n_ref, w_ref, *rest):
        o_ref = rest[-1]
        notlast = (pl.program_id(1) < nT - 1).astype(F32)
        d = d_ref[...]
        nxt = n_ref[...] * notlast
        w_ = w_ref[...]
        acc = d * w_[K - 1:K, :]
        for sh in range(1, K):
            acc = acc + _shift_up(d, nxt, sh) * w_[K - 1 - sh:K - sh, :]
        o_ref[...] = acc.astype(out_dtype)

    in_specs = [pl.BlockSpec((tt, tc), lambda c, i: (i, c)),
                pl.BlockSpec((8, tc), lambda c, i: (jnp.minimum((i + 1) * nb8, last8), c)),
                pl.BlockSpec((K, tc), lambda c, i: (0, c))]
    args = [dh, dh, w]
    if into is None:
        out_shape, alias = jax.ShapeDtypeStruct((T, C), out_dtype), {}
    else:
        assert into[0].dtype == out_dtype and into[1] % tc == 0
        in_specs.append(_ANY)
        args.append(into[0])
        out_shape, alias = jax.ShapeDtypeStruct(into[0].shape, out_dtype), {3: 0}
    return pl.pallas_call(
        body, name=name, grid=(C // tc, nT), in_specs=in_specs,
        out_specs=pl.BlockSpec((tt, tc), lambda c, i: (i, c + cb0)),
        out_shape=out_shape, input_output_aliases=alias,
        compiler_params=_cparams(("parallel", "parallel")))(*args)


def _ffn_conv_fwd3(a3, w, b, *, name, tt=256, tc=1408):
    T = a3.shape[1]
    tt = min(tt, T)
    K, nbh, n16 = FFN_CONV, D_FF // tc, tt // 16

    def body(a_ref, p_ref, wg_ref, wv_ref, bg_ref, bv_ref, o_ref):
        first = (pl.program_id(1) > 0).astype(F32)
        a = a_ref[...].astype(F32)
        prev = p_ref[...].astype(F32)[:, 8:16, :] * first
        hg, _ = _conv_hid(a[0], prev[0], wg_ref[...], bg_ref[...], K)
        hv, _ = _conv_hid(a[1], prev[1], wv_ref[...], bv_ref[...], K)
        o_ref[...] = (hg * _sigmoid(hg) * hv).astype(BF16)

    return pl.pallas_call(
        body, name=name, grid=(nbh, T // tt),
        in_specs=[pl.BlockSpec((2, tt, tc), lambda c, i: (0, i, c)),
                  pl.BlockSpec((2, 16, tc), lambda c, i: (0, _prev_idx(i, n16), c)),
                  pl.BlockSpec((K, tc), lambda c, i: (0, c)), pl.BlockSpec((K, tc), lambda c, i: (0, c + nbh)),
                  pl.BlockSpec((1, tc), lambda c, i: (0, c)), pl.BlockSpec((1, tc), lambda c, i: (0, c + nbh))],
        out_specs=pl.BlockSpec((tt, tc), lambda c, i: (i, c)),
        out_shape=jax.ShapeDtypeStruct((T, D_FF), BF16),
        compiler_params=_cparams(("parallel", "parallel")))(a3, a3, w, w, b, b)


def _ffn_conv_bwd3(a3, w, b, dp, *, name, tt=256, tc=1408):
    T = a3.shape[1]
    tt = min(tt, T)
    K, nbh, n16 = FFN_CONV, D_FF // tc, tt // 16

    def body(a_ref, p_ref, wg_ref, wv_ref, bg_ref, bv_ref, dp_ref, dh_ref, dw_ref, db_ref):
        t = pl.program_id(1)
        first = (t > 0).astype(F32)
        a = a_ref[...].astype(F32)
        prev = p_ref[...].astype(F32)[:, 8:16, :] * first
        hg, sh_g = _conv_hid(a[0], prev[0], wg_ref[...], bg_ref[...], K)
        hv, sh_v = _conv_hid(a[1], prev[1], wv_ref[...], bv_ref[...], K)
        sg = _sigmoid(hg)
        d = dp_ref[...].astype(F32)
        dhg = d * hv * (sg * (1.0 + hg * (1.0 - sg)))
        dhv = d * (hg * sg)
        dh_ref[0] = dhg.astype(BF16)
        dh_ref[1] = dhv.astype(BF16)

        @pl.when(t == 0)
        def _():
            dw_ref[...] = jnp.zeros_like(dw_ref)
            db_ref[...] = jnp.zeros_like(db_ref)

        db_ref[0] += jnp.sum(dhg, axis=0, keepdims=True)
        db_ref[1] += jnp.sum(dhv, axis=0, keepdims=True)
        for j in range(K):
            dw_ref[0, j:j + 1, :] += jnp.sum(dhg * sh_g[j], axis=0, keepdims=True)
            dw_ref[1, j:j + 1, :] += jnp.sum(dhv * sh_v[j], axis=0, keepdims=True)

    return pl.pallas_call(
        body, name=name, grid=(nbh, T // tt),
        in_specs=[pl.BlockSpec((2, tt, tc), lambda c, i: (0, i, c)),
                  pl.BlockSpec((2, 16, tc), lambda c, i: (0, _prev_idx(i, n16), c)),
                  pl.BlockSpec((K, tc), lambda c, i: (0, c)), pl.BlockSpec((K, tc), lambda c, i: (0, c + nbh)),
                  pl.BlockSpec((1, tc), lambda c, i: (0, c)), pl.BlockSpec((1, tc), lambda c, i: (0, c + nbh)),
                  pl.BlockSpec((tt, tc), lambda c, i: (i, c))],
        out_specs=[pl.BlockSpec((2, tt, tc), lambda c, i: (0, i, c)), pl.BlockSpec((2, K, tc), lambda c, i: (0, 0, c)),
                   pl.BlockSpec((2, 1, tc), lambda c, i: (0, 0, c))],
        out_shape=[jax.ShapeDtypeStruct((2, T, D_FF), BF16), jax.ShapeDtypeStruct((2, K, D_FF), F32),
                   jax.ShapeDtypeStruct((2, 1, D_FF), F32)],
        compiler_params=_cparams(("parallel", "arbitrary")))(a3, a3, w, w, b, b, dp)


def _conv_bwd_in3(dh3, w, *, name, K, tt=256, tc=1408):
    H, T, C = dh3.shape
    tt = min(tt, T)
    nb, n16, nT = C // tc, tt // 16, T // tt
    last16 = T // 16 - 1

    def body(d_ref, n_ref, w_ref, o_ref):
        notlast = (pl.program_id(2) < nT - 1).astype(F32)
        d = d_ref[...].astype(F32)
        nxt = n_ref[...].astype(F32)[0:8, :] * notlast
        w_ = w_ref[...]
        acc = d * w_[K - 1:K, :]
        for sh in range(1, K):
            acc = acc + _shift_up(d, nxt, sh) * w_[K - 1 - sh:K - sh, :]
        o_ref[...] = acc.astype(BF16)

    return pl.pallas_call(
        body, name=name, grid=(H, nb, nT),
        in_specs=[pl.BlockSpec((None, tt, tc), lambda h, c, i: (h, i, c)),
                  pl.BlockSpec((None, 16, tc), lambda h, c, i: (h, jnp.minimum((i + 1) * n16, last16), c)),
                  pl.BlockSpec((K, tc), lambda h, c, i: (0, h * nb + c))],
        out_specs=pl.BlockSpec((None, tt, tc), lambda h, c, i: (h, i, c)),
        out_shape=jax.ShapeDtypeStruct((H, T, C), BF16),
        compiler_params=_cparams(("parallel", "parallel", "parallel")))(dh3, dh3, w)


def _cumsum_rows(x):
    L = x.shape[0]
    row = lax.broadcasted_iota(jnp.int32, x.shape, 0)
    k = 1
    while k < L:
        x = x + jnp.where(row >= k, pltpu.roll(x, k, 0), 0.0)
        k *= 2
    return x


def _rcumsum_rows(x):
    L = x.shape[0]
    row = lax.broadcasted_iota(jnp.int32, x.shape, 0)
    k = 1
    while k < L:
        x = x + jnp.where(row < L - k, pltpu.roll(x, L - k, 0), 0.0)
        k *= 2
    return x


def _split_terms(m, n):
    terms, rest = [], m
    for _ in range(n):
        t = rest.astype(BF16)
        terms.append(t)
        rest = rest - t.astype(F32)
    return jnp.concatenate(terms, axis=1)


def _select_dot(m, n_terms, n_out, cond):
    K = m.shape[1]
    k = lax.broadcasted_iota(jnp.int32, (K, n_out), 0)
    j = lax.broadcasted_iota(jnp.int32, (K, n_out), 1)
    sel = cond(k, j).astype(BF16)
    return jnp.dot(_split_terms(m, n_terms), jnp.concatenate([sel] * n_terms, axis=0), preferred_element_type=F32)


def _rowsum_mxu(m):
    return _select_dot(m, 2, 128, lambda k, j: k >= 0)


def _lane_block_sums(m, width):
    shift = width.bit_length() - 1
    return _select_dot(m, 2, 128, lambda k, j: j == jnp.right_shift(k, shift))


def _heads_to_pairs(m):
    return _select_dot(m, 3, 512, lambda k, j: k == jnp.right_shift(j, 6))


def _ssd_common(dt_ref, par_ref):
    par = par_ref[...]
    raw = dt_ref[...] + par[0:1, :]
    dt = _softplus(raw)
    a = -jnp.exp(par[1:2, :])
    cs = _cumsum_rows(dt * a)
    L = cs.shape[0]
    cs_last = cs[L - 1:L, :]
    return raw, dt, a, par[2:3, :], cs, cs.T, jnp.exp(cs), jnp.exp(cs_last - cs), jnp.exp(cs_last)


def _ssd_specs(nc, rev):
    L = SSM_CHUNK

    def ci(c):
        return nc - 1 - c if rev else c

    return [pl.BlockSpec((L, D_INNER), lambda c: (ci(c), 0)),
            pl.BlockSpec((L, GN), lambda c: (ci(c), D_INNER // GN)),
            pl.BlockSpec((L, GN), lambda c: (ci(c), D_INNER // GN + 1)),
            pl.BlockSpec((SSM_GROUPS, L, 128), lambda c: (0, ci(c), 0)),
            pl.BlockSpec((SSM_GROUPS, 8, 128), lambda c: (0, 0, 0)),
            pl.BlockSpec((L, D_INNER), lambda c: (ci(c), 0)),
            pl.BlockSpec((1, D_INNER), lambda c: (0, 0))], ci


def _round_robin(gens):
    live = list(gens)
    while live:
        nxt = []
        for gen in live:
            try:
                next(gen)
                nxt.append(gen)
            except StopIteration:
                pass
        live = nxt


def _group_views(g, wide, narrow, lead):
    return ([r.at[:, g * 512:(g + 1) * 512] for r in wide], [r.at[:, g * 128:(g + 1) * 128] for r in narrow],
            [r.at[g] for r in lead])


def _ssd_fwd(xbc_c, zx, dtg, par, gnw, *, name):
    T = xbc_c.shape[0]
    L = SSM_CHUNK
    nc = T // L
    in_specs, ci = _ssd_specs(nc, False)

    def body(xs_ref, b_ref, c_ref, dt_ref, par_ref, z_ref, gnw_ref, y_ref, yn_ref, st_ref, h_ref):
        @pl.when(pl.program_id(0) == 0)
        def _():
            h_ref[...] = jnp.zeros_like(h_ref)

        gens = []
        for g in range(SSM_GROUPS):
            (xs, z, gw, y, yn), (b, c), (dt, pr, st, h) = _group_views(
                g, [xs_ref, z_ref, gnw_ref, y_ref, yn_ref], [b_ref, c_ref], [dt_ref, par_ref, st_ref, h_ref])
            gens.append(group(xs, b, c, dt, pr, z, gw, y, yn, st, h))
        _round_robin(gens)

    def group(xs_ref, b_ref, c_ref, dt_ref, par_ref, z_ref, gnw_ref, y_ref, yn_ref, st_ref, h_ref):
        _, dt, _, dsk, cs, csT, ecs, eend, dec = _ssd_common(dt_ref, par_ref)
        Bb = b_ref[...].astype(BF16)
        Cb = c_ref[...].astype(BF16)
        G = lax.dot_general(Cb, Bb, _NT, preferred_element_type=F32)
        row = lax.broadcasted_iota(jnp.int32, (L, L), 0)
        col = lax.broadcasted_iota(jnp.int32, (L, L), 1)
        tril = col <= row
        lo = lax.broadcasted_iota(jnp.int32, (L, 128), 1) < 64
        lo1 = lax.broadcasted_iota(jnp.int32, (1, 128), 1) < 64
        dt_x, ecs_x, eend_x = (_heads_to_pairs(m) for m in (dt, ecs, eend))
        for pp in range(4):
            hA, hB = 2 * pp, 2 * pp + 1
            lanes = slice(pp * 128, (pp + 1) * 128)

            def sel1(m):
                return jnp.where(lo1, m[:, hA:hA + 1], m[:, hB:hB + 1])

            X = xs_ref[:, lanes]
            xd = X * dt_x[:, lanes]
            xdb = xd.astype(BF16)
            ys = []
            for h in (hA, hB):
                Lm = jnp.where(tril, jnp.exp(jnp.minimum(cs[:, h:h + 1] - csT[h:h + 1, :], 0.0)), 0.0)
                ys.append(jnp.dot((G * Lm).astype(BF16), xdb, preferred_element_type=F32))
                yield
            Hp = h_ref[pp]
            st_ref[pp] = Hp
            yoff = jnp.dot(Cb, Hp.astype(BF16), preferred_element_type=F32) * ecs_x[:, lanes]
            y_ref[:, lanes] = jnp.where(lo, ys[0], ys[1]) + yoff + sel1(dsk) * X
            S = lax.dot_general(Bb, (xd * eend_x[:, lanes]).astype(BF16), _TN, preferred_element_type=F32)
            h_ref[pp] = Hp * sel1(dec) + S
            yield
        zv = z_ref[...]
        yg = y_ref[...] * (zv * _sigmoid(zv))
        r = jnp.tile(lax.rsqrt(_rowsum_mxu(yg * yg) * (1.0 / 512) + EPS), (1, 4))
        yn_ref[...] = (yg * r * gnw_ref[...]).astype(BF16)

    return pl.pallas_call(
        body, name=name, grid=(nc,), in_specs=in_specs,
        out_specs=[pl.BlockSpec((L, D_INNER), lambda c: (c, 0)), pl.BlockSpec((L, D_INNER), lambda c: (c, 0)),
                   pl.BlockSpec((SSM_GROUPS, None, 4, 128, 128), lambda c: (0, c, 0, 0, 0))],
        out_shape=[jax.ShapeDtypeStruct((T, D_INNER), F32), jax.ShapeDtypeStruct((T, D_INNER), BF16),
                   jax.ShapeDtypeStruct((SSM_GROUPS, nc, 4, 128, 128), F32)],
        scratch_shapes=[pltpu.VMEM((SSM_GROUPS, 4, 128, 128), F32)],
        compiler_params=_cparams(("arbitrary",)))(xbc_c, xbc_c, xbc_c, dtg, par, zx, gnw)


def _ssd_bwd(xbc_c, zx, dtg, par, gnw, y, st, dyn, *, name):
    T = xbc_c.shape[0]
    L = SSM_CHUNK
    nc = T // L
    in_specs, ci = _ssd_specs(nc, True)
    in_specs += [pl.BlockSpec((L, D_INNER), lambda c: (ci(c), 0)),
                 pl.BlockSpec((SSM_GROUPS, None, 4, 128, 128), lambda c: (0, ci(c), 0, 0, 0)),
                 pl.BlockSpec((L, D_INNER), lambda c: (ci(c), 0))]

    def body(xs_ref, b_ref, c_ref, dt_ref, par_ref, z_ref, gnw_ref, y_ref, st_ref, dyn_ref,
             dxbc_ref, dz_ref, ddt_ref, dgnw_ref, dpar_ref, dh_ref):
        @pl.when(pl.program_id(0) == 0)
        def _():
            dh_ref[...] = jnp.zeros_like(dh_ref)
            dgnw_ref[...] = jnp.zeros_like(dgnw_ref)
            dpar_ref[...] = jnp.zeros_like(dpar_ref)

        dxs_ref = dxbc_ref.at[:, 0:D_INNER]
        db_ref = dxbc_ref.at[:, D_INNER:D_INNER + GN]
        dc_ref = dxbc_ref.at[:, D_INNER + GN:CONV_DIM]

        gens = []
        for g in range(SSM_GROUPS):
            (xs, z, gw, y, dyn, dxs, dz, dgw), (b, c, db, dc), (dt, pr, st, ddt, dpr, dh) = _group_views(
                g, [xs_ref, z_ref, gnw_ref, y_ref, dyn_ref, dxs_ref, dz_ref, dgnw_ref], [b_ref, c_ref, db_ref, dc_ref],
                [dt_ref, par_ref, st_ref, ddt_ref, dpar_ref, dh_ref])
            gens.append(group(xs, b, c, dt, pr, z, gw, y, st, dyn, dxs, db, dc, dz, ddt, dgw, dpr, dh))
        _round_robin(gens)

    def group(xs_ref, b_ref, c_ref, dt_ref, par_ref, z_ref, gnw_ref, y_ref, st_ref, dyn_ref,
              dxs_ref, db_ref, dc_ref, dz_ref, ddt_ref, dgnw_ref, dpar_ref, dh_ref):
        yv = y_ref[...]
        zv = z_ref[...]
        sg = _sigmoid(zv)
        sz = zv * sg
        yg = yv * sz
        r = jnp.tile(lax.rsqrt(_rowsum_mxu(yg * yg) * (1.0 / 512) + EPS), (1, 4))
        yh = yg * r
        dyn = dyn_ref[...].astype(F32)
        dgnw_ref[...] += jnp.sum(dyn * yh, axis=0, keepdims=True)
        dyh = dyn * gnw_ref[...]
        dyg = r * (dyh - yh * jnp.tile(_rowsum_mxu(dyh * yh) * (1.0 / 512), (1, 4)))
        dY_all = dyg * sz
        dz_ref[...] = (dyg * yv * (sg * (1.0 + zv * (1.0 - sg)))).astype(dz_ref.dtype)

        yield
        raw, dt, a, dsk, cs, csT, ecs, eend, dec = _ssd_common(dt_ref, par_ref)
        Bb = b_ref[...].astype(BF16)
        Cb = c_ref[...].astype(BF16)
        G = lax.dot_general(Cb, Bb, _NT, preferred_element_type=F32)
        row = lax.broadcasted_iota(jnp.int32, (L, L), 0)
        col = lax.broadcasted_iota(jnp.int32, (L, L), 1)
        tril = col <= row
        lo = lax.broadcasted_iota(jnp.int32, (L, 128), 1) < 64
        lane1 = lax.broadcasted_iota(jnp.int32, (1, 128), 1)
        lo1 = lane1 < 64
        rowl = lax.broadcasted_iota(jnp.int32, (L, 128), 0)
        dt_x, ecs_x, eend_x = (_heads_to_pairs(m) for m in (dt, ecs, eend))
        dG = jnp.zeros((L, L), F32)
        dB = jnp.zeros((L, SSM_STATE), F32)
        dC = jnp.zeros((L, SSM_STATE), F32)
        dcs_t = jnp.zeros((L, L), F32)
        tails = jnp.zeros((1, 128), F32)
        dD_row = jnp.zeros((1, 128), F32)
        v_parts, prod_parts = [], []

        def tot(m):
            return jnp.sum(jnp.sum(m, axis=0, keepdims=True), axis=1, keepdims=True)

        for pp in range(4):
            hA, hB = 2 * pp, 2 * pp + 1
            lanes = slice(pp * 128, (pp + 1) * 128)

            def sel1(m):
                return jnp.where(lo1, m[:, hA:hA + 1], m[:, hB:hB + 1])

            X = xs_ref[:, lanes]
            dY = dY_all[:, lanes]
            dtsel = dt_x[:, lanes]
            xd = X * dtsel
            xdb = xd.astype(BF16)
            dYb = dY.astype(BF16)
            Hp = st_ref[pp]
            Hb = Hp.astype(BF16)
            dHn = dh_ref[pp]
            dHb = dHn.astype(BF16)
            ecs_sel = ecs_x[:, lanes]
            eend_sel = eend_x[:, lanes]
            dxd_state = jnp.dot(Bb, dHb, preferred_element_type=F32) * eend_sel
            yoff = jnp.dot(Cb, Hb, preferred_element_type=F32) * ecs_sel
            dYe = (dY * ecs_sel).astype(BF16)
            dC = dC + lax.dot_general(dYe, Hb, _NT, preferred_element_type=F32)
            dB = dB + lax.dot_general((xd * eend_sel).astype(BF16), dHb, _NT, preferred_element_type=F32)
            dh_ref[pp] = dHn * sel1(dec) + lax.dot_general(Cb, dYe, _TN, preferred_element_type=F32)
            q = xd * dxd_state
            dyq = dY * yoff - q
            qcol = jnp.sum(q, axis=0, keepdims=True)
            hcol = jnp.sum(dHn * Hp, axis=0, keepdims=True)
            dxd_diag = []
            for h, msk, msk1 in ((hA, lo, lo1), (hB, jnp.logical_not(lo), jnp.logical_not(lo1))):
                Lm = jnp.where(tril, jnp.exp(jnp.minimum(cs[:, h:h + 1] - csT[h:h + 1, :], 0.0)), 0.0)
                M = G * Lm
                dxd_diag.append(lax.dot_general(M.astype(BF16), dYb, _TN, preferred_element_type=F32))
                dM = lax.dot_general(jnp.where(msk, dY, 0.0).astype(BF16), xdb, _NT, preferred_element_type=F32)
                dG = dG + dM * Lm
                W = dM * M
                dcs_t = dcs_t + jnp.where(row == h, jnp.sum(W, axis=0, keepdims=True), 0.0)
                v_parts.append(W + jnp.where(msk, dyq, 0.0))
                tail = (jnp.sum(jnp.where(msk1, qcol, 0.0), axis=1, keepdims=True)
                        + dec[:, h:h + 1] * jnp.sum(jnp.where(msk1, hcol, 0.0), axis=1, keepdims=True))
                tails = tails + jnp.where(lane1 == h, tail, 0.0)
                yield
            dxd = jnp.where(lo, dxd_diag[0], dxd_diag[1]) + dxd_state
            prod_parts.append(dxd * X)
            dxs_ref[:, lanes] = dxd * dtsel + sel1(dsk) * dY
            dyx = jnp.sum(dY * X, axis=0, keepdims=True)
            sA = jnp.sum(jnp.where(lo1, dyx, 0.0), axis=1, keepdims=True)
            sB = jnp.sum(dyx, axis=1, keepdims=True) - sA
            dD_row = dD_row + jnp.where(lane1 == hA, sA, 0.0) + jnp.where(lane1 == hB, sB, 0.0)
            yield
        dGb = dG.astype(BF16)
        db_ref[...] = dB + lax.dot_general(dGb, Cb, _TN, preferred_element_type=F32)
        dc_ref[...] = dC + jnp.dot(dGb, Bb, preferred_element_type=F32)
        dcs_mat = _lane_block_sums(jnp.concatenate(v_parts, axis=1), 128) + jnp.where(rowl == L - 1, tails, 0.0)
        ddt_mat = _lane_block_sums(jnp.concatenate(prod_parts, axis=1), 64)
        dad = _rcumsum_rows(dcs_mat - dcs_t.T)
        draw = (a * dad + ddt_mat) * _sigmoid(raw)
        ddt_ref[...] = draw
        dpar_ref[0:1, :] += jnp.sum(draw, axis=0, keepdims=True)
        dpar_ref[1:2, :] += jnp.sum(dt * dad, axis=0, keepdims=True) * a
        dpar_ref[2:3, :] += dD_row

    return pl.pallas_call(
        body, name=name, grid=(nc,), in_specs=in_specs,
        out_specs=[pl.BlockSpec((L, CONV_DIM), lambda c: (ci(c), 0)),
                   pl.BlockSpec((L, D_INNER), lambda c: (ci(c), 0)),
                   pl.BlockSpec((SSM_GROUPS, L, 128), lambda c: (0, ci(c), 0)),
                   pl.BlockSpec((1, D_INNER), lambda c: (0, 0)),
                   pl.BlockSpec((SSM_GROUPS, 8, 128), lambda c: (0, 0, 0))],
        out_shape=[jax.ShapeDtypeStruct((T, CONV_DIM), F32), jax.ShapeDtypeStruct((T, IN_PROJ_PAD), BF16),
                   jax.ShapeDtypeStruct((SSM_GROUPS, T, 128), F32), jax.ShapeDtypeStruct((1, D_INNER), F32),
                   jax.ShapeDtypeStruct((SSM_GROUPS, 8, 128), F32)],
        scratch_shapes=[pltpu.VMEM((SSM_GROUPS, 4, 128, 128), F32)],
        compiler_params=_cparams(("arbitrary",)))(xbc_c, xbc_c, xbc_c, dtg, par, zx, gnw, y, st, dyn)


SB_KEYS = 512
SB_SCAN = 256
SB_STRIP = 256


def _tri(width, cond):
    kk = lax.broadcasted_iota(jnp.int32, (width, width), 0)
    jj = lax.broadcasted_iota(jnp.int32, (width, width), 1)
    return cond(kk, jj).astype(BF16)


_LOG2E = 1.4426950408889634


def _softplus2(z2):
    return jnp.maximum(z2, 0.0) + jnp.log2(1.0 + jnp.exp2(-jnp.abs(z2)))


def _sba_sub_fwd(zb, c, U, mask):
    z2 = zb * _LOG2E
    s = _softplus2(z2)
    if mask is not None:
        s = jnp.where(mask, s, 0.0)
    R = c + jnp.dot(s.astype(BF16), U, preferred_element_type=F32)
    A = jnp.exp2(z2 - s - R)
    if mask is not None:
        A = jnp.where(mask, A, 0.0)
    return A.astype(BF16), R[:, 0:1] + s[:, 0:1]


def _sba_sub_bwd(zb, dAb, Lt, pc, pe, Uincl, Uexcl, mask):
    last = zb.shape[1] - 1
    z2 = zb * _LOG2E
    s = _softplus2(z2)
    g = z2 - s
    if mask is not None:
        s = jnp.where(mask, s, 0.0)
    P = pc + jnp.dot(s.astype(BF16), Uincl, preferred_element_type=F32)
    A = jnp.exp2(g - (Lt - P))
    if mask is not None:
        A = jnp.where(mask, A, 0.0)
    E = dAb * A
    PE = pe + jnp.dot(E.astype(BF16), Uexcl, preferred_element_type=F32)
    dz = E - jnp.exp2(g) * (E + PE)
    if mask is not None:
        dz = jnp.where(mask, dz, 0.0)
    return (A.astype(BF16), dz.astype(BF16), P[:, last:last + 1], PE[:, last:last + 1] + E[:, last:last + 1])


def _stack_heads(v):
    lo = lax.broadcasted_iota(jnp.int32, v.shape, 1) < 64
    zero = jnp.zeros_like(v)
    return jnp.concatenate([jnp.where(lo, v, zero), jnp.where(lo, zero, v)], axis=0)


def _unstack_heads(v):
    lo = lax.broadcasted_iota(jnp.int32, (SB_BLOCK, 128), 1) < 64
    return jnp.where(lo, v[:SB_BLOCK], v[SB_BLOCK:])


def _sba_rows(a):
    return slice(2 * a * SB_BLOCK, 2 * (a + 1) * SB_BLOCK)


def _sba_diag_case(a, b):
    Bq = SB_BLOCK
    if b * SB_SCAN >= (a + 1) * Bq:
        return "skip"
    if (b + 1) * SB_SCAN <= a * Bq:
        return "full"
    rowi = lax.broadcasted_iota(jnp.int32, (2 * Bq, SB_SCAN), 0)
    qpos = a * Bq + jnp.where(rowi >= Bq, rowi - Bq, rowi)
    return b * SB_SCAN + lax.broadcasted_iota(jnp.int32, (2 * Bq, SB_SCAN), 1) < qpos


def _sba_fwd(q, kv, *, name):
    T = q.shape[0]
    Bq = SB_BLOCK
    nsub = SB_KEYS // Bq
    nscan = SB_KEYS // SB_SCAN
    R = 2 * SB_KEYS
    assert T % SB_KEYS == 0 and SB_STRIP == 2 * Bq
    scale = 1.0 / math.sqrt(SB_HEAD_DIM)

    def body(q_ref, k_ref, v_ref, o_ref, lt_ref, z_s, a_s, c_s, acc_s):
        i = pl.program_id(1)
        U2 = _tri(SB_SCAN, lambda k, j: k > j)
        qs_all = jnp.concatenate([_stack_heads(q_ref[a * Bq:(a + 1) * Bq, :] * scale) for a in range(nsub)], axis=0)
        c_s[...] = jnp.zeros_like(c_s)
        acc_s[...] = jnp.zeros_like(acc_s)

        def scores(J, slot):
            off = pl.multiple_of(J * SB_KEYS, SB_KEYS)
            z_s[slot] = lax.dot_general(qs_all, k_ref[pl.ds(off, SB_KEYS), :], _NT, preferred_element_type=F32)

        def weights(slot, diag):
            for a in range(nsub):
                rows = _sba_rows(a)
                c = c_s[rows, :]
                for b in reversed(range(nscan)):
                    cols = slice(b * SB_SCAN, (b + 1) * SB_SCAN)
                    case = _sba_diag_case(a, b) if diag else "full"
                    if isinstance(case, str) and case == "skip":
                        a_s[slot, rows, cols] = jnp.zeros((2 * Bq, SB_SCAN), BF16)
                        continue
                    A, c = _sba_sub_fwd(z_s[slot, rows, cols], c, U2, None if isinstance(case, str) else case)
                    a_s[slot, rows, cols] = A
                c_s[rows, :] = c

        def values(J, slot):
            off = pl.multiple_of(J * SB_KEYS, SB_KEYS)
            acc_s[...] += jnp.dot(a_s[slot], v_ref[pl.ds(off, SB_KEYS), :], preferred_element_type=F32)

        scores(i, 0)
        weights(0, True)
        scores(jnp.maximum(i - 1, 0), 1)

        def two_steps(u, _):
            t = 2 * u + 1
            weights(1, False)
            scores(jnp.maximum(i - t - 1, 0), 0)
            values(i - t + 1, 0)
            weights(0, False)
            scores(jnp.maximum(i - t - 2, 0), 1)
            values(i - t, 1)
            return 0

        lax.fori_loop(0, i // 2, two_steps, 0)
        odd = lax.rem(i, 2) == 1

        @pl.when(jnp.logical_not(odd))
        def _():
            values(0, 0)

        @pl.when(odd)
        def _():
            weights(1, False)
            values(1, 0)
            values(0, 1)
        for a in range(nsub):
            o_ref[a * Bq:(a + 1) * Bq, :] = _unstack_heads(acc_s[_sba_rows(a), :]).astype(BF16)
            lt_ref[a * Bq:(a + 1) * Bq, :] = _unstack_heads(jnp.broadcast_to(c_s[_sba_rows(a), :], (2 * Bq, 128)))

    return pl.pallas_call(
        body, name=name, grid=(SB_HEADS // 2, T // SB_KEYS),
        in_specs=[pl.BlockSpec((SB_KEYS, 128), lambda p, i: (i, p)), pl.BlockSpec((T, 128), lambda p, i: (0, p)),
                  pl.BlockSpec((T, 128), lambda p, i: (0, p + SB_HEADS // 2))],
        out_specs=[pl.BlockSpec((SB_KEYS, 128), lambda p, i: (i, p)),
                   pl.BlockSpec((None, SB_KEYS, 128), lambda p, i: (p, i, 0))],
        out_shape=[jax.ShapeDtypeStruct((T, D_MODEL), BF16), jax.ShapeDtypeStruct((SB_HEADS // 2, T, 128), F32)],
        scratch_shapes=[pltpu.VMEM((2, R, SB_KEYS), F32), pltpu.VMEM((2, R, SB_KEYS), BF16),
                        pltpu.VMEM((R, 1), F32), pltpu.VMEM((R, 128), F32)],
        compiler_params=_cparams(("parallel", "parallel")))(q, kv, kv)


def _sba_bwd(q, kv, lt, do, *, name):
    T = q.shape[0]
    Bq = SB_BLOCK
    nq = T // SB_KEYS
    nsub = SB_KEYS // Bq
    nscan = SB_KEYS // SB_SCAN
    R = 2 * SB_KEYS
    assert T % SB_KEYS == 0 and SB_STRIP == 2 * Bq
    scale = 1.0 / math.sqrt(SB_HEAD_DIM)

    def body(q_ref, k_ref, v_ref, lt_ref, do_ref, dq_ref, dk_ref, dv_ref, dk_acc, dv_acc,
             z_s, da_s, a_s, dz_s, pc_s, pe_s, lt_s, dq_s):
        i = pl.program_id(1)

        @pl.when(i == 0)
        def _():
            dk_acc[...] = jnp.zeros_like(dk_acc)
            dv_acc[...] = jnp.zeros_like(dv_acc)

        Uincl = _tri(SB_SCAN, lambda k, j: k <= j)
        Uexcl = _tri(SB_SCAN, lambda k, j: k < j)
        qs, dos = [], []
        for a in range(nsub):
            rows = slice(a * Bq, (a + 1) * Bq)
            qs.append(_stack_heads(q_ref[rows, :] * scale))
            dos.append(_stack_heads(do_ref[rows, :]))
            lt_s[_sba_rows(a), :] = jnp.concatenate([lt_ref[rows, 0:1], lt_ref[rows, 64:65]], axis=0)
        qs_all = jnp.concatenate(qs, axis=0)
        dos_all = jnp.concatenate(dos, axis=0)
        pc_s[...] = jnp.zeros_like(pc_s)
        pe_s[...] = jnp.zeros_like(pe_s)
        a_s[1] = jnp.zeros((R, SB_KEYS), BF16)
        dz_s[1] = jnp.zeros((R, SB_KEYS), BF16)

        def scores(J, slot):
            off = pl.multiple_of(J * SB_KEYS, SB_KEYS)
            z_s[slot] = lax.dot_general(qs_all, k_ref[pl.ds(off, SB_KEYS), :], _NT, preferred_element_type=F32)
            da_s[slot] = lax.dot_general(dos_all, v_ref[pl.ds(off, SB_KEYS), :], _NT, preferred_element_type=F32)

        def gradients(slot, diag):
            for a in range(nsub):
                rows = _sba_rows(a)
                pc, pe, Lt = pc_s[rows, :], pe_s[rows, :], lt_s[rows, :]
                for b in range(nscan):
                    cols = slice(b * SB_SCAN, (b + 1) * SB_SCAN)
                    case = _sba_diag_case(a, b) if diag else "full"
                    if isinstance(case, str) and case == "skip":
                        a_s[slot, rows, cols] = jnp.zeros((2 * Bq, SB_SCAN), BF16)
                        dz_s[slot, rows, cols] = jnp.zeros((2 * Bq, SB_SCAN), BF16)
                        continue
                    A, dz, pc, pe = _sba_sub_bwd(z_s[slot, rows, cols], da_s[slot, rows, cols], Lt, pc, pe, Uincl, Uexcl,
                                                 None if isinstance(case, str) else case)
                    a_s[slot, rows, cols] = A
                    dz_s[slot, rows, cols] = dz
                pc_s[rows, :] = pc
                pe_s[rows, :] = pe

        def products(J, slot):
            off = pl.multiple_of(J * SB_KEYS, SB_KEYS)
            dzt = dz_s[slot]
            dk_acc[pl.ds(off, SB_KEYS), :] += lax.dot_general(dzt, qs_all, _TN, preferred_element_type=F32)
            dv_acc[pl.ds(off, SB_KEYS), :] += lax.dot_general(a_s[slot], dos_all, _TN, preferred_element_type=F32)
            dq_s[...] += jnp.dot(dzt, k_ref[pl.ds(off, SB_KEYS), :], preferred_element_type=F32)

        dq_s[...] = jnp.zeros_like(dq_s)
        scores(0, 0)

        def two_steps(u, _):
            t = 2 * u
            gradients(0, False)
            scores(t + 1, 1)
            products(jnp.maximum(t - 1, 0), 1)
            gradients(1, False)
            scores(t + 2, 0)
            products(t, 0)
            return 0

        lax.fori_loop(0, i // 2, two_steps, 0)
        odd = lax.rem(i, 2) == 1

        @pl.when(jnp.logical_not(odd))
        def _():
            gradients(0, True)
            products(jnp.maximum(i - 1, 0), 1)
            products(i, 0)

        @pl.when(odd)
        def _():
            gradients(0, False)
            scores(i, 1)
            products(jnp.maximum(i - 2, 0), 1)
            gradients(1, True)
            products(i - 1, 0)
            products(i, 1)

        for a in range(nsub):
            dq_ref[a * Bq:(a + 1) * Bq, :] = (_unstack_heads(dq_s[_sba_rows(a), :]) * scale).astype(BF16)

        @pl.when(i == nq - 1)
        def _():
            dk_ref[...] = dk_acc[...].astype(BF16)
            dv_ref[...] = dv_acc[...].astype(BF16)

    return pl.pallas_call(
        body, name=name, grid=(SB_HEADS // 2, nq),
        in_specs=[pl.BlockSpec((SB_KEYS, 128), lambda p, i: (i, p)), pl.BlockSpec((T, 128), lambda p, i: (0, p)),
                  pl.BlockSpec((T, 128), lambda p, i: (0, p + SB_HEADS // 2)),
                  pl.BlockSpec((None, SB_KEYS, 128), lambda p, i: (p, i, 0)),
                  pl.BlockSpec((SB_KEYS, 128), lambda p, i: (i, p))],
        out_specs=[pl.BlockSpec((SB_KEYS, 128), lambda p, i: (i, p)), pl.BlockSpec((T, 128), lambda p, i: (0, p)),
                   pl.BlockSpec((T, 128), lambda p, i: (0, p))],
        out_shape=[jax.ShapeDtypeStruct((T, D_MODEL), BF16), jax.ShapeDtypeStruct((T, D_MODEL), BF16),
                   jax.ShapeDtypeStruct((T, D_MODEL), BF16)],
        scratch_shapes=[pltpu.VMEM((T, 128), F32), pltpu.VMEM((T, 128), F32),
                        pltpu.VMEM((2, R, SB_KEYS), F32), pltpu.VMEM((2, R, SB_KEYS), F32),
                        pltpu.VMEM((2, R, SB_KEYS), BF16), pltpu.VMEM((2, R, SB_KEYS), BF16),
                        pltpu.VMEM((R, 1), F32), pltpu.VMEM((R, 1), F32), pltpu.VMEM((R, 1), F32),
                        pltpu.VMEM((R, 128), F32)],
        compiler_params=_cparams(("parallel", "arbitrary")))(q, kv, kv, lt, do)


def _loss_head(h, tgt, w, *, name, tt=512):
    T, D = h.shape
    tt = min(tt, T)

    def body(h_ref, t_ref, w_ref, loss_ref, dh_ref, dw_ref):
        i = pl.program_id(0)
        hv = h_ref[...]
        wv = w_ref[...]
        r = lax.rsqrt(jnp.mean(hv * hv, axis=-1, keepdims=True) + EPS)
        xhat = hv * r
        err = xhat * wv - t_ref[...]
        part = 0.5 * jnp.sum(jnp.mean(err * err, axis=-1, keepdims=True), axis=0, keepdims=True)
        dy = err * (1.0 / D)
        dxh = dy * wv
        dh_ref[...] = r * (dxh - xhat * jnp.mean(dxh * xhat, axis=-1, keepdims=True))
        dwc = jnp.sum(dy * xhat, axis=0, keepdims=True)

        @pl.when(i == 0)
        def _():
            loss_ref[...] = jnp.broadcast_to(part, loss_ref.shape)
            dw_ref[...] = dwc

        @pl.when(i > 0)
        def _():
            loss_ref[...] += jnp.broadcast_to(part, loss_ref.shape)
            dw_ref[...] += dwc

    return pl.pallas_call(
        body, name=name, grid=(T // tt,),
        in_specs=[pl.BlockSpec((tt, D), lambda i: (i, 0)), pl.BlockSpec((tt, D), lambda i: (i, 0)),
                  pl.BlockSpec((1, D), lambda i: (0, 0))],
        out_specs=[pl.BlockSpec((1, 128), lambda i: (0, 0)), pl.BlockSpec((tt, D), lambda i: (i, 0)),
                   pl.BlockSpec((1, D), lambda i: (0, 0))],
        out_shape=[jax.ShapeDtypeStruct((1, 128), F32), jax.ShapeDtypeStruct((T, D), F32),
                   jax.ShapeDtypeStruct((1, D), F32)],
        compiler_params=_cparams(("arbitrary",)))(h, tgt, w.reshape(1, D))


def _adamw(parts, w, m, v, *, name, tr=256, tc=None):
    plist = list(parts) if isinstance(parts, (list, tuple)) else [parts]
    P, _, C = plist[0].shape
    R = sum(a.shape[1] for a in plist)
    tr = min(tr, R)
    tc = C if tc is None else tc
    assert all(a.shape[1] % tr == 0 for a in plist) and C % tc == 0, (name, R, C, tr, tc)
    nbs = [a.shape[1] // tr for a in plist]
    offs = [sum(nbs[:l]) for l in range(len(nbs))]
    c1 = 1.0 - ADAM_B1 ** ADAM_STEP
    c2 = 1.0 - ADAM_B2 ** ADAM_STEP

    def body(*refs):
        p_refs = refs[:len(plist)]
        w_ref, m_ref, v_ref, g_ref, d_ref, nm_ref, nv_ref = refs[len(plist):]
        i = pl.program_id(0)
        g = None
        for l, p_ref in enumerate(p_refs):
            gl = p_ref[0].astype(F32)
            for k in range(1, P):
                gl = gl + p_ref[k].astype(F32)
            g = gl if g is None else jnp.where(i >= offs[l], gl, g)
        mn = ADAM_B1 * m_ref[...] + (1.0 - ADAM_B1) * g
        vn = ADAM_B2 * v_ref[...] + (1.0 - ADAM_B2) * (g * g)
        g_ref[...] = g
        nm_ref[...] = mn
        nv_ref[...] = vn
        d_ref[...] = -ADAM_LR * ((mn / c1) / (jnp.sqrt(vn / c2) + ADAM_EPS) + ADAM_WD * w_ref[...])

    spec = pl.BlockSpec((tr, tc), lambda i, j: (i, j))
    sds = jax.ShapeDtypeStruct((R, C), F32)
    return pl.pallas_call(
        body, name=name, grid=(R // tr, C // tc),
        in_specs=[pl.BlockSpec((P, tr, tc), functools.partial(lambda i, j, o, n: (0, jnp.clip(i - o, 0, n - 1), j), o=o, n=n))
                  for o, n in zip(offs, nbs)] + [spec, spec, spec],
        out_specs=[spec, spec, spec, spec], out_shape=[sds, sds, sds, sds],
        compiler_params=_cparams(("parallel", "parallel")))(*plist, w, m, v)


def _all_gather(shards, *, name):
    n = len(shards)

    def body(*refs):
        ins, outs = refs[:n], refs[n:2 * n]
        send_sems, recv_sems, local_sems = refs[2 * n:]
        x, y, c = lax.axis_index("x"), lax.axis_index("y"), lax.axis_index("c")
        me, sib = (x, y, c), (x, y, 1 - c)
        chips = [(1 - x, y), (x, 1 - y), (1 - x, 1 - y)]

        def slot(p):
            return 4 * p[0] + 2 * p[1] + p[2]

        def cp(a, k, block, to, src=None):
            dst = outs[a].at[slot(block)]
            return pltpu.make_async_remote_copy(src_ref=dst if src is None else src, dst_ref=dst,
                                                send_sem=send_sems.at[a, k], recv_sem=recv_sems.at[a, k],
                                                device_id=to, device_id_type=_MESH)

        mine = [pltpu.make_async_copy(ins[a], outs[a].at[slot(me)], local_sems.at[a]) for a in range(n)]
        for m in mine:
            m.start()
        first = []
        for a in range(n):
            first.append(cp(a, 0, me, sib, src=ins[a]))
            for j, chip in enumerate(chips):
                first.append(cp(a, 1 + j, me, (*chip, c), src=ins[a]))
        for f in first:
            f.start()
        passed = []
        for j, chip in enumerate(chips):
            for a in range(n):
                cp(a, 1 + j, (*chip, c), me).wait_recv()
                f = cp(a, 4 + j, (*chip, c), sib)
                f.start()
                passed.append(f)
        for a in range(n):
            cp(a, 0, sib, me).wait_recv()
            for j, chip in enumerate(chips):
                cp(a, 4 + j, (*chip, 1 - c), me).wait_recv()
        for f in first + passed:
            f.wait_send()
        for m in mine:
            m.wait()

    return pl.pallas_call(
        body, name=name, in_specs=[_ANY] * n, out_specs=[_ANY] * n,
        out_shape=[jax.ShapeDtypeStruct((N_DEV,) + s.shape, s.dtype) for s in shards],
        scratch_shapes=[pltpu.SemaphoreType.DMA((n, 7)), pltpu.SemaphoreType.DMA((n, 7)),
                        pltpu.SemaphoreType.DMA((n,))])(*shards)


_HBM = pl.BlockSpec(memory_space=pltpu.HBM)
_SEM = pl.BlockSpec(memory_space=pltpu.SEMAPHORE)
_EFFECT = pltpu.SideEffectType.DATAFLOW_SIDE_EFFECTING


def _peers():
    x, y, c = lax.axis_index("x"), lax.axis_index("y"), lax.axis_index("c")
    out = []
    for r in range(1, N_DEV):
        px = 1 - x if (r >> 2) & 1 else x
        py = 1 - y if (r >> 1) & 1 else y
        pc = 1 - c if r & 1 else c
        out.append(((px, py, pc), 4 * px + 2 * py + pc))
    return 4 * x + 2 * y + c, out


def _push_copy(src_ref, land_ref, send_sems, recv_sems, a, k, me, peer, peer_slot, scatter, arriving):
    src = src_ref.at[peer_slot] if scatter else src_ref
    return pltpu.make_async_remote_copy(
        src_ref=src, dst_ref=land_ref.at[peer_slot if arriving else me], send_sem=send_sems.at[a * (N_DEV - 1) + k],
        recv_sem=recv_sems.at[a * (N_DEV - 1) + k], device_id=peer, device_id_type=_MESH)


def _push_start(srcs, *, scatter, name):
    n = len(srcs)
    lands = [lax.empty(s.shape if scatter else (N_DEV,) + s.shape, s.dtype) for s in srcs]

    def body(*refs):
        src_refs, land_refs = refs[:n], refs[n:2 * n]
        send_sems, recv_sems = refs[2 * n], refs[2 * n + 1]
        token = refs[-1]
        me, peers = _peers()
        for k, (peer, slot) in enumerate(peers):
            for a in range(n):
                _push_copy(src_refs[a], land_refs[a], send_sems, recv_sems, a, k, me, peer, slot, scatter, False).start()
        token[...] = jnp.zeros_like(token)

    hbm = lambda a: pltpu.HBM(a.shape, a.dtype)
    outs = pl.pallas_call(
        body, name=name,
        out_shape=(pltpu.SemaphoreType.DMA((n * (N_DEV - 1),)), pltpu.SemaphoreType.DMA((n * (N_DEV - 1),)),
                   *[hbm(s) for s in srcs], *[hbm(l) for l in lands], jax.ShapeDtypeStruct((8, 128), F32)),
        in_specs=[_HBM] * (2 * n),
        out_specs=(_SEM, _SEM, *([_HBM] * (2 * n)), pl.BlockSpec(memory_space=pltpu.VMEM)),
        input_output_aliases={i: 2 + i for i in range(2 * n)},
        compiler_params=pltpu.CompilerParams(has_side_effects=_EFFECT),
    )(*[pltpu.with_memory_space_constraint(s, pltpu.HBM) for s in srcs],
      *[pltpu.with_memory_space_constraint(l, pltpu.HBM) for l in lands])
    return dict(send=outs[0], recv=outs[1], srcs=list(outs[2:2 + n]), lands=list(outs[2 + n:2 + 2 * n]),
                token=outs[-1], scatter=scatter, n=n)


def _push_wait(h, after, *, name):
    n, scatter = h["n"], h["scatter"]

    def body(*refs):
        src_refs, land_refs = refs[:n], refs[n:2 * n]
        send_sems, recv_sems = refs[2 * n], refs[2 * n + 1]
        me, peers = _peers()
        for k, (peer, slot) in enumerate(peers):
            for a in range(n):
                cp = _push_copy(src_refs[a], land_refs[a], send_sems, recv_sems, a, k, me, peer, slot, scatter, True)
                cp.wait_send()
                cp.wait_recv()

    hbm = lambda a: pltpu.HBM(a.shape, a.dtype)
    outs = pl.pallas_call(
        body, name=name,
        out_shape=(*[hbm(s) for s in h["srcs"]], *[hbm(l) for l in h["lands"]]),
        in_specs=[_HBM] * (2 * n) + [_SEM, _SEM, _ANY], out_specs=tuple([_HBM] * (2 * n)),
        input_output_aliases={i: i for i in range(2 * n)},
        compiler_params=pltpu.CompilerParams(has_side_effects=_EFFECT),
    )(*h["srcs"], *h["lands"], h["send"], h["recv"], after)
    return list(outs[:n]), list(outs[n:])


def _ffn_fwd(h, nw, w_up, conv_w, conv_b, w_down, tag):
    a3 = _mm_fwd(h, w_up, norm_w=nw, name=f"ffn{tag}_up", out_dtype=BF16, halves=True, w_t=True, tm=1024, tn=2816)
    p = _ffn_conv_fwd3(a3, conv_w, conv_b.reshape(1, -1), name=f"ffn{tag}_conv")
    h_out = _mm_fwd(p, w_down, residual=h, name=f"ffn{tag}_down", tm=1024, tn=512)
    return h_out, (a3, p)


def _ffn_bwd(dh, h, saved, nw, w_up, conv_w, conv_b, w_down, tag):
    a3, p = saved
    g_down = _mm_tn(p, dh, name=f"ffn{tag}_down_wg", tk1=1408, tn=1024)
    dp = _mm_nt(dh, w_down, name=f"ffn{tag}_down_dg", out_dtype=BF16, tm=512, tn=2816, tk=1024)
    dhid3, dw3, db3 = _ffn_conv_bwd3(a3, conv_w, conv_b.reshape(1, -1), dp, name=f"ffn{tag}_conv_bwd")
    da3 = _conv_bwd_in3(dhid3, conv_w, K=FFN_CONV, name=f"ffn{tag}_conv_bwd_in")
    g_up = _mm_tn_t(da3, h, norm_w=nw, name=f"ffn{tag}_up_wg", tn=1408, tt=1024)
    dh_out, g_nw = _mm_nt(da3, w_up, epi=(h, nw, dh), name=f"ffn{tag}_up_dg", w_t=True, tm=1024, tk=1408)
    g_cw = jnp.concatenate([dw3[0], dw3[1]], axis=1)
    g_cb = jnp.concatenate([db3[0], db3[1]], axis=1)
    return dh_out, dict(norm=g_nw.reshape(-1), up=g_up, conv_w=g_cw, conv_b=g_cb.reshape(-1), down=g_down)


_BIG = ["ssm_in_w", "ssm_out_w", "w_k", "w_v", "w_q", "w_o", "ffn_up_w", "ffn_down_w"]
_SMALL_SHARDED = ["ssm_norm_w", "ssm_conv_w", "ssm_conv_b", "ssm_gate_norm_w", "ffn_conv_w"]
_SMALL_REPL = ["ssm_dt_bias", "ssm_a_log", "ssm_d", "kv_norm_w", "attn_norm_w", "ffn_norm_w", "ffn_conv_b",
               "final_norm_w"]
_WEIGHTS = ["ssm_norm_w", "ssm_in_w", "ssm_conv_w", "ssm_conv_b", "ssm_dt_bias", "ssm_a_log", "ssm_d",
            "ssm_gate_norm_w", "ssm_out_w", "kv_norm_w", "w_k", "w_v", "attn_norm_w", "w_q", "w_o", "ffn_norm_w",
            "ffn_up_w", "ffn_conv_w", "ffn_conv_b", "ffn_down_w", "final_norm_w"]


def _as2d(a):
    return a.reshape(-1, a.shape[-1])


def _cols_to_full(g):
    return g.transpose(1, 0, 2).reshape(g.shape[1], N_DEV * g.shape[2])


def _pack_small(vals):
    flat = jnp.concatenate([v.reshape(-1).astype(F32) for v in vals])
    n = flat.shape[0]
    rows = -(-n // 1024) * 8
    return jnp.pad(flat, (0, rows * 128 - n)).reshape(rows, 128)


def _unpack_small(packed, shapes):
    flat = packed.reshape(-1)
    out, off = [], 0
    for s in shapes:
        n = math.prod(s)
        out.append(flat[off:off + n].reshape(s))
        off += n
    return out


def _tie(a, token):
    return a + token[0, 0].astype(a.dtype)


def _local_step(x, tgt, get_w, put_g):
    T = x.shape[0]
    Ws = get_w("ssm", None)
    fnw, fcw, fcb = Ws["ffn_norm_w"], Ws["ffn_conv_w"], Ws["ffn_conv_b"]
    zx = _mm_fwd(x, Ws["in_w"], norm_w=Ws["ssm_norm_w"], name="ssm_in", w_t=True, tm=1024, tn=1792)
    xbc_c = _ssm_conv_fwd(zx, Ws["ssm_conv_w"], Ws["ssm_conv_b"].reshape(1, -1), name="ssm_conv")
    dt_raw = zx[:, D_INNER + CONV_DIM:IN_PROJ_DIM]
    dtg = jnp.pad(dt_raw.reshape(T, SSM_GROUPS, 8).transpose(1, 0, 2), ((0, 0), (0, 0), (0, 120)))
    par = jnp.stack([Ws["ssm_dt_bias"].reshape(SSM_GROUPS, 8), Ws["ssm_a_log"].reshape(SSM_GROUPS, 8),
                     Ws["ssm_d"].reshape(SSM_GROUPS, 8)], axis=1)
    par = jnp.pad(par, ((0, 0), (0, 5), (0, 120)))
    gnw = _tie(Ws["ssm_gate_norm_w"].reshape(1, D_INNER), get_w("rest_start", xbc_c))
    y, yn, st = _ssd_fwd(xbc_c, zx, dtg, par, gnw, name="ssd_fwd")
    W0 = get_w("ffn0", y)
    Ws["ssm_out_w"] = W0["ssm_out_w"]
    h1 = _mm_fwd(yn, Ws["ssm_out_w"], residual=x, name="ssm_out", tm=1024, tn=512)
    h2, ffn0 = _ffn_fwd(h1, fnw[0], W0["up"], fcw[0], fcb[0], W0["down"], "0")
    Wr = get_w("rest", h2)
    q = _mm_fwd(h2, Wr["w_q"], norm_w=Ws["attn_norm_w"], out_dtype=BF16, name="attn_q", tm=1024, tn=1024)
    kv = _mm_fwd(h2, Wr["w_kv"], norm_w=Ws["kv_norm_w"], out_dtype=BF16, name="attn_kv", tm=1024, tn=1024)
    o, lt = _sba_fwd(q, kv, name="sba_fwd")
    h3 = _mm_fwd(o, Wr["w_o"], residual=h2, name="attn_o", tm=1024, tn=512)
    h4, ffn1 = _ffn_fwd(h3, fnw[1], Wr["up"], fcw[1], fcb[1], Wr["down"], "1")
    loss, dh4, g_final = _loss_head(h4, tgt, Ws["final_norm_w"], name="loss_head")
    dh3, gf1 = _ffn_bwd(dh4, h3, ffn1, fnw[1], Wr["up"], fcw[1], fcb[1], Wr["down"], "1")
    tok = put_g("ffn1", dict(up=gf1["up"], down=gf1["down"]))
    g_wo = _mm_tn(o, dh3, name="attn_o_wg", tn=1024)
    do = _mm_nt(dh3, _tie(Wr["w_o"], tok), name="attn_o_dg", out_dtype=BF16, tn=1024, tk=1024)
    dq, dk, dv = _sba_bwd(q, kv, lt, do, name="sba_bwd")
    g_wq = _mm_tn(h2, dq, norm_w=Ws["attn_norm_w"], name="attn_q_wg", tn=1024, tt=1024)
    dh2a, g_attn_nw = _mm_nt(dq, Wr["w_q"], epi=(h2, Ws["attn_norm_w"], dh3), name="attn_q_dg", tm=1024, tk=1024)
    dkv = jnp.concatenate([dk, dv], axis=1)
    g_wkv = _mm_tn(h2, dkv, norm_w=Ws["kv_norm_w"], name="attn_kv_wg", tn=1024, tt=1024)
    dh2, g_kv_nw = _mm_nt(dkv, Wr["w_kv"], epi=(h2, Ws["kv_norm_w"], dh2a), name="attn_kv_dg", tm=1024, tk=1024)
    tok = put_g("attn", dict(w_o=g_wo, w_q=g_wq, w_k=g_wkv[:, :D_MODEL], w_v=g_wkv[:, D_MODEL:]))
    dh1, gf0 = _ffn_bwd(dh2, h1, ffn0, fnw[0], W0["up"], fcw[0], _tie(fcb[0], tok), W0["down"], "0")
    tok = put_g("ffn0", dict(up=gf0["up"], down=gf0["down"]))
    g_out = _mm_tn(yn, dh1, name="ssm_out_wg", tn=1024)
    dyn = _mm_nt(dh1, _tie(Ws["ssm_out_w"], tok), name="ssm_out_dg", out_dtype=BF16, tn=1024, tk=1024)
    tok = put_g("ssm_out", dict(ssm_out_w=g_out))
    dxbc_c, dz, ddt, g_gnw, dpar = _ssd_bwd(xbc_c, zx, dtg, par, _tie(gnw, tok), y, st, dyn, name="ssd_bwd")
    dhid, g_scw, g_scb = _ssm_conv_bwd_pre(zx, Ws["ssm_conv_w"], Ws["ssm_conv_b"].reshape(1, -1), dxbc_c,
                                           name="ssm_conv_bwd")
    dzx = _conv_bwd_in(dhid, Ws["ssm_conv_w"], K=SSM_CONV, name="ssm_conv_bwd_in", into=(dz, D_INNER))
    ddt_t = ddt[:, :, :8].transpose(1, 0, 2).reshape(T, SSM_HEADS).astype(BF16)
    dzx = _put_cols(dzx, jnp.pad(ddt_t, ((0, 0), (0, IN_PROJ_PAD - IN_PROJ_DIM))), D_INNER + CONV_DIM, name="ssm_ddt_cols")
    g_in = _mm_tn_t(dzx, x, norm_w=Ws["ssm_norm_w"], name="ssm_in_wg", tn=1792, tt=1024)
    tok = put_g("ssm_in", dict(ssm_in_w=g_in[:IN_PROJ_DIM]))
    dx, g_ssm_nw = _mm_nt(dzx, Ws["in_w"], epi=(x, _tie(Ws["ssm_norm_w"], tok), dh1), name="ssm_in_dg", w_t=True,
                          tm=1024, tk=1792)
    f = {
        "ssm_norm_w": g_ssm_nw.reshape(-1), "ssm_conv_w": g_scw,
        "ssm_conv_b": g_scb.reshape(-1), "ssm_dt_bias": dpar[:, 0, :8].reshape(-1),
        "ssm_a_log": dpar[:, 1, :8].reshape(-1), "ssm_d": dpar[:, 2, :8].reshape(-1),
        "ssm_gate_norm_w": g_gnw.reshape(-1), "kv_norm_w": g_kv_nw.reshape(-1), "attn_norm_w": g_attn_nw.reshape(-1),
        "ffn_norm_w": jnp.stack([gf0["norm"], gf1["norm"]]), "ffn_conv_w": jnp.stack([gf0["conv_w"], gf1["conv_w"]]),
        "ffn_conv_b": jnp.stack([gf0["conv_b"], gf1["conv_b"]]), "final_norm_w": g_final.reshape(-1),
    }
    return loss, dx, f


def kernel(x, ssm_norm_w, ssm_in_w, ssm_conv_w, ssm_conv_b, ssm_dt_bias, ssm_a_log, ssm_d, ssm_gate_norm_w, ssm_out_w, kv_norm_w, w_k, w_v, attn_norm_w, w_q, w_o, ffn_norm_w, ffn_up_w, ffn_conv_w, ffn_conv_b, ffn_down_w, final_norm_w, loss_target, m_ssm_norm_w, m_ssm_in_w, m_ssm_conv_w, m_ssm_conv_b, m_ssm_dt_bias, m_ssm_a_log, m_ssm_d, m_ssm_gate_norm_w, m_ssm_out_w, m_kv_norm_w, m_w_k, m_w_v, m_attn_norm_w, m_w_q, m_w_o, m_ffn_norm_w, m_ffn_up_w, m_ffn_conv_w, m_ffn_conv_b, m_ffn_down_w, m_final_norm_w, v_ssm_norm_w, v_ssm_in_w, v_ssm_conv_w, v_ssm_conv_b, v_ssm_dt_bias, v_ssm_a_log, v_ssm_d, v_ssm_gate_norm_w, v_ssm_out_w, v_kv_norm_w, v_w_k, v_w_v, v_attn_norm_w, v_w_q, v_w_o, v_ffn_norm_w, v_ffn_up_w, v_ffn_conv_w, v_ffn_conv_b, v_ffn_down_w, v_final_norm_w):
    env = dict(locals())
    p = {n: env[n] for n in _WEIGHTS}
    mom = {n: env["m_" + n] for n in _WEIGHTS}
    var = {n: env["v_" + n] for n in _WEIGHTS}
    T = x.shape[1]
    me = 4 * lax.axis_index("x") + 2 * lax.axis_index("y") + lax.axis_index("c")
    rs = D_FF // N_DEV

    def bf2(a):
        return _as2d(a).astype(BF16)

    _T = ("ssm_in_w", "ffn_up_w")

    def t2d(a):
        return jnp.swapaxes(a, -1, -2).reshape(-1, a.shape[-2])

    def from_t2d(a, like):
        return jnp.swapaxes(a.reshape(like.shape[:-2] + (like.shape[-1], like.shape[-2])), -1, -2)

    n_in, n_up = p["ssm_in_w"].shape[-1], p["ffn_up_w"].shape[-1]

    def with_own(srcs, lands, scatter):
        out = []
        for s, l in zip(srcs, lands):
            own = lax.dynamic_index_in_dim(s, me, 0, keepdims=False) if scatter else s
            out.append(lax.dynamic_update_index_in_dim(l, own, me, 0))
        return out

    a_names = ["ssm_in_w"] + _SMALL_SHARDED
    got_a = dict(zip(a_names, _all_gather([t2d(p["ssm_in_w"]).astype(BF16)] + [_as2d(p[n]) for n in _SMALL_SHARDED],
                                          name="gather_ssm")))
    ffn0_names = ["ssm_out_w", "up0", "down0"]
    rest_names = ["w_q", "w_k", "w_v", "w_o", "up1", "down1"]
    up_t = jnp.swapaxes(p["ffn_up_w"], -1, -2).astype(BF16)
    shard = {"up0": up_t[0], "down0": bf2(p["ffn_down_w"][0]), "up1": up_t[1],
             "down1": bf2(p["ffn_down_w"][1]), "w_q": bf2(p["w_q"]), "w_k": bf2(p["w_k"]), "w_v": bf2(p["w_v"]),
             "w_o": bf2(p["w_o"]), "ssm_out_w": bf2(p["ssm_out_w"])}

    def anchored(a, on):
        return a + (jnp.where(jnp.isfinite(on), on, 0.0) * 0.0).astype(a.dtype)

    h_ffn0 = _push_start([anchored(shard[ffn0_names[0]], got_a["ssm_norm_w"][0, 0, 0])]
                         + [shard[n] for n in ffn0_names[1:]], scatter=False, name="gather_ffn0_start")
    handles = {}

    def get_w(group, after):
        if group == "ssm":
            W = {n: p[n] for n in _SMALL_REPL}
            for n in ("ssm_dt_bias", "ssm_a_log", "ssm_d", "attn_norm_w"):
                W[n] = W[n].reshape(-1)
            W["in_w"] = jnp.pad(got_a["ssm_in_w"].reshape(IN_PROJ_DIM, D_MODEL), ((0, IN_PROJ_PAD - IN_PROJ_DIM), (0, 0)))
            W["ssm_norm_w"] = _tie(got_a["ssm_norm_w"].reshape(D_MODEL), h_ffn0["token"])
            W["ssm_conv_w"] = _cols_to_full(got_a["ssm_conv_w"])
            W["ssm_conv_b"] = got_a["ssm_conv_b"].reshape(CONV_DIM)
            W["ssm_gate_norm_w"] = got_a["ssm_gate_norm_w"].reshape(D_INNER)
            W["ffn_conv_w"] = _cols_to_full(got_a["ffn_conv_w"]).reshape(2, FFN_CONV, 2 * D_FF)
            return W
        if group == "rest_start":
            handles["rest"] = _push_start([anchored(shard[rest_names[0]], after[0, 0])]
                                          + [shard[n] for n in rest_names[1:]], scatter=False, name="gather_rest_start")
            return handles["rest"]["token"]
        if group == "ffn0":
            srcs, lands = _push_wait(h_ffn0, after, name="gather_ffn0_wait")
            out, up, down = with_own(srcs, lands, False)
            return dict(ssm_out_w=out.reshape(D_INNER, D_MODEL), up=up.reshape(2 * D_FF, D_MODEL),
                        down=down.reshape(D_FF, D_MODEL))
        srcs, lands = _push_wait(handles["rest"], after, name="gather_rest_wait")
        g = dict(zip(rest_names, with_own(srcs, lands, False)))
        sq = lambda a: a.reshape(D_MODEL, D_MODEL)
        return dict(w_q=sq(g["w_q"]), w_kv=jnp.concatenate([sq(g["w_k"]), sq(g["w_v"])], axis=1), w_o=sq(g["w_o"]),
                    up=g["up1"].reshape(2 * D_FF, D_MODEL), down=g["down1"].reshape(D_FF, D_MODEL))

    pending = []

    def put_g(group, g):
        if group in ("ffn0", "ffn1"):
            keys = [("ffn_up_w", int(group[-1])), ("ffn_down_w", int(group[-1]))]
            blocks = [g["up"].reshape(N_DEV, n_up, D_MODEL), g["down"].reshape(N_DEV, rs, D_MODEL)]
        elif group == "attn":
            keys = [(n, None) for n in ("w_o", "w_q", "w_k", "w_v")]
            blocks = [g[n].reshape(N_DEV, D_MODEL // N_DEV, D_MODEL) for n, _ in keys]
        elif group == "ssm_out":
            keys = [("ssm_out_w", None)]
            blocks = [g["ssm_out_w"].reshape(N_DEV, D_INNER // N_DEV, D_MODEL)]
        else:
            keys = [("ssm_in_w", None)]
            blocks = [g["ssm_in_w"].reshape(N_DEV, n_in, D_MODEL)]
        h = _push_start(blocks, scatter=True, name=f"exchange_{group}_start")
        pending.append((group, keys, h))
        return h["token"]

    loss_row, dx, f = _local_step(x.reshape(T, D_MODEL), loss_target.reshape(T, D_MODEL), get_w, put_g)

    small_names = _SMALL_REPL + _SMALL_SHARDED
    small_full = _pack_small([f[n] for n in small_names] + [loss_row[0, 0:1]])
    small_bcast = jnp.broadcast_to(small_full[None], (N_DEV,) + small_full.shape)
    h_small = _push_start([small_bcast], scatter=True, name="exchange_small_start")
    tok = h_small["token"]

    arrived, res = {}, {}
    after = dx
    for group, keys, h in pending:
        srcs, lands = _push_wait(h, after, name=f"exchange_{group}_wait")
        arrived.update(zip(keys, with_own(srcs, lands, True)))
        for n in _BIG:
            layered = (n, 0) in arrived or (n, 1) in arrived
            if n in res or not ((n, None) in arrived or ((n, 0) in arrived and (n, 1) in arrived)):
                continue
            parts = [arrived[(n, 0)], arrived[(n, 1)]] if layered else arrived[(n, None)]
            w2, m2, v2 = ((t2d if n in _T else _as2d)(a[n]) for a in (p, mom, var))
            if not res:
                w2 = _tie(w2, tok)
            tiles = {"ffn_down_w": dict(tr=rs), "ffn_up_w": dict(tr=n_up // 2), "ssm_in_w": dict(tr=n_in, tc=256)}
            res[n] = _adamw(parts, w2, m2, v2, name=f"adamw_{n}", **tiles.get(n, dict(tr=256)))
            after = res[n][0]
    srcs, lands = _push_wait(h_small, after, name="exchange_small_wait")
    small_parts = with_own(srcs, lands, True)[0]
    out_g, out_d, out_m, out_v = {}, {}, {}, {}
    for n in _BIG:
        out_g[n], out_d[n], out_m[n], out_v[n] = (from_t2d(t, p[n]) if n in _T else t.reshape(p[n].shape) for t in res[n])

    zero = jnp.zeros_like(small_full)
    g_small_sum = _adamw(small_parts, zero, zero, zero, name="sum_small_grads", tr=small_full.shape[0])[0]
    *small_sums, loss_sum = _unpack_small(g_small_sum, [f[n].shape for n in small_names] + [(1,)])
    loss = loss_sum[0]
    g_small = dict(zip(small_names, small_sums))
    for n in _SMALL_SHARDED:
        width = p[n].shape[-1]
        g_small[n] = lax.dynamic_slice_in_dim(g_small[n], me * width, width, axis=g_small[n].ndim - 1)
    sw = _pack_small([p[n] for n in small_names])
    sm = _pack_small([mom[n] for n in small_names])
    sv = _pack_small([var[n] for n in small_names])
    sg = _pack_small([g_small[n] for n in small_names])
    _, d, nm, nv = _adamw(sg[None], sw, sm, sv, name="adamw_small", tr=sw.shape[0])
    shard_shapes = [p[n].shape for n in small_names]
    for n, dd, mm, vv in zip(small_names, _unpack_small(d, shard_shapes), _unpack_small(nm, shard_shapes),
                             _unpack_small(nv, shard_shapes)):
        out_g[n] = g_small[n].reshape(p[n].shape)
        out_d[n], out_m[n], out_v[n] = dd, mm, vv

    return (loss, dx.reshape(x.shape), *[out_g[n] for n in _WEIGHTS], *[out_d[n] for n in _WEIGHTS],
            *[out_m[n] for n in _WEIGHTS], *[out_v[n] for n in _WEIGHTS])
```

```python
import functools
import math

import jax
import jax.numpy as jnp
from jax import lax
from jax.experimental import pallas as pl
from jax.experimental.pallas import tpu as pltpu

F32 = jnp.float32
BF16 = jnp.bfloat16
EPS = 1e-6

D_MODEL = 1024
D_INNER = 2048
SSM_HEADS = 32
SSM_GROUPS = 4
SSM_STATE = 128
SSM_CONV = 4
SSM_CHUNK = 128
GN = SSM_GROUPS * SSM_STATE
CONV_DIM = D_INNER + 2 * GN
IN_PROJ_DIM = D_INNER + CONV_DIM + SSM_HEADS
IN_PROJ_PAD = 5376
SB_HEADS = 16
SB_HEAD_DIM = 64
SB_BLOCK = 128
D_FF = 2816
FFN_CONV = 3
N_DEV = 8

ADAM_LR = 0.001
ADAM_B1 = 0.9
ADAM_B2 = 0.999
ADAM_EPS = 1e-08
ADAM_WD = 0.01
ADAM_STEP = 10

_MESH = pl.DeviceIdType.MESH
_NT = (((1,), (1,)), ((), ()))
_TN = (((0,), (0,)), ((), ()))
_ANY = pl.BlockSpec(memory_space=pl.ANY)


def _cparams(sem, vmem_mb=48):
    return pltpu.CompilerParams(dimension_semantics=sem, vmem_limit_bytes=vmem_mb * 1024 * 1024)


def _sigmoid(x):
    return 0.5 * jnp.tanh(0.5 * x) + 0.5


def _softplus(x):
    return jnp.maximum(x, 0.0) + jnp.log(1.0 + jnp.exp(-jnp.abs(x)))


def _rms_fwd(xv, w):
    r = lax.rsqrt(jnp.mean(xv * xv, axis=-1, keepdims=True) + EPS)
    return xv * r * w


def _mm_fwd(x, w, *, name, norm_w=None, residual=None, out_dtype=F32, tm=512, tn=512, halves=False, w_t=False):
    M, K = x.shape
    N = w.shape[0] if w_t else w.shape[1]
    tm, tn = min(tm, M), min(tn, N)
    assert M % tm == 0 and N % tn == 0, (name, M, N, tm, tn)
    if halves:
        nbh = N // 2 // tn
        assert N // 2 % tn == 0
        out_spec = pl.BlockSpec((None, tm, tn), lambda i, j: (lax.div(j, nbh), i, lax.rem(j, nbh)))
        out_shape = jax.ShapeDtypeStruct((2, M, N // 2), out_dtype)
    else:
        out_spec = pl.BlockSpec((tm, tn), lambda i, j: (i, j))
        out_shape = jax.ShapeDtypeStruct((M, N), out_dtype)
    has_norm, has_res = norm_w is not None, residual is not None

    def body(*refs):
        x_ref, w_ref = refs[0], refs[1]
        p = 2
        nw_ref = r_ref = None
        if has_norm:
            nw_ref = refs[p]
            p += 1
        if has_res:
            r_ref = refs[p]
            p += 1
        o_ref = refs[p]
        xv = x_ref[...]
        if has_norm:
            xv = _rms_fwd(xv.astype(F32), nw_ref[...])
        acc = lax.dot_general(xv.astype(BF16), w_ref[...], _NT if w_t else (((1,), (0,)), ((), ())),
                              preferred_element_type=F32)
        if has_res:
            acc = acc + r_ref[...]
        o_ref[...] = acc.astype(out_dtype)

    w_spec = pl.BlockSpec((tn, K), lambda i, j: (j, 0)) if w_t else pl.BlockSpec((K, tn), lambda i, j: (0, j))
    in_specs = [pl.BlockSpec((tm, K), lambda i, j: (i, 0)), w_spec]
    args = [x, w]
    if has_norm:
        in_specs.append(pl.BlockSpec((1, K), lambda i, j: (0, 0)))
        args.append(norm_w.reshape(1, K))
    if has_res:
        in_specs.append(pl.BlockSpec((tm, tn), lambda i, j: (i, j)))
        args.append(residual)
    return pl.pallas_call(
        body, name=name, grid=(M // tm, N // tn), in_specs=in_specs,
        out_specs=out_spec, out_shape=out_shape,
        compiler_params=_cparams(("parallel", "parallel")))(*args)


def _mm_nt(dy, w, *, name, epi=None, out_dtype=F32, tm=512, tn=512, tk=512, w_t=False):
    halves = dy.ndim == 3
    M, K = (dy.shape[1], 2 * dy.shape[2]) if halves else dy.shape
    N = w.shape[1] if w_t else w.shape[0]
    tm, tk = min(tm, M), min(tk, K)
    tn = N if epi is not None else min(tn, N)
    assert M % tm == 0 and N % tn == 0 and K % tk == 0, (name, M, N, K, tm, tn, tk)
    nk = K // tk
    has_epi = epi is not None

    def body(*refs):
        if has_epi:
            dy_ref, w_ref, h_ref, nw_ref, r_ref, o_ref, dnw_ref, acc_ref = refs
        else:
            dy_ref, w_ref, o_ref, acc_ref = refs
        i = pl.program_id(0)
        k = pl.program_id(2)

        @pl.when(k == 0)
        def _():
            acc_ref[...] = jnp.zeros_like(acc_ref)

        acc_ref[...] += lax.dot_general(dy_ref[...].astype(BF16), w_ref[...], (((1,), (0,)), ((), ())) if w_t else _NT,
                                        preferred_element_type=F32)

        @pl.when(k == nk - 1)
        def _():
            du = acc_ref[...]
            if has_epi:
                hv = h_ref[...]
                r = lax.rsqrt(jnp.mean(hv * hv, axis=-1, keepdims=True) + EPS)
                xhat = hv * r
                dxh = du * nw_ref[...]
                dx = r * (dxh - xhat * jnp.mean(dxh * xhat, axis=-1, keepdims=True))
                o_ref[...] = (r_ref[...] + dx).astype(out_dtype)
                contrib = jnp.sum(du * xhat, axis=0, keepdims=True)

                @pl.when(i == 0)
                def _():
                    dnw_ref[...] = contrib

                @pl.when(i > 0)
                def _():
                    dnw_ref[...] += contrib
            else:
                o_ref[...] = du.astype(out_dtype)

    if halves:
        nkh = K // 2 // tk
        assert K // 2 % tk == 0
        dy_spec = pl.BlockSpec((None, tm, tk), lambda i, j, k: (lax.div(k, nkh), i, lax.rem(k, nkh)))
    else:
        dy_spec = pl.BlockSpec((tm, tk), lambda i, j, k: (i, k))
    w_spec = pl.BlockSpec((tk, tn), lambda i, j, k: (k, j)) if w_t else pl.BlockSpec((tn, tk), lambda i, j, k: (j, k))
    in_specs = [dy_spec, w_spec]
    args = [dy, w]
    out_specs = [pl.BlockSpec((tm, tn), lambda i, j, k: (i, j))]
    out_shape = [jax.ShapeDtypeStruct((M, N), out_dtype)]
    if has_epi:
        h, nw, res = epi
        in_specs += [pl.BlockSpec((tm, N), lambda i, j, k: (i, 0)), pl.BlockSpec((1, N), lambda i, j, k: (0, 0)),
                     pl.BlockSpec((tm, N), lambda i, j, k: (i, 0))]
        args += [h, nw.reshape(1, N), res]
        out_specs.append(pl.BlockSpec((1, N), lambda i, j, k: (0, 0)))
        out_shape.append(jax.ShapeDtypeStruct((1, N), F32))
    outs = pl.pallas_call(
        body, name=name, grid=(M // tm, N // tn, nk), in_specs=in_specs, out_specs=out_specs, out_shape=out_shape,
        scratch_shapes=[pltpu.VMEM((tm, tn), F32)],
        compiler_params=_cparams(("arbitrary", "arbitrary", "arbitrary")))(*args)
    return (outs[0], outs[1]) if has_epi else outs[0]


def _mm_tn(x, dy, *, name, norm_w=None, out_dtype=BF16, tk1=1024, tn=512, tt=512):
    T, K1 = x.shape
    halves = dy.ndim == 3
    N = 2 * dy.shape[2] if halves else dy.shape[1]
    tk1, tn, tt = min(tk1, K1), min(tn, N), min(tt, T)
    has_norm = norm_w is not None
    assert K1 % tk1 == 0 and N % tn == 0 and T % tt == 0, (name, K1, N, T, tk1, tn, tt)
    assert not has_norm or tk1 == K1
    nt = T // tt

    def body(*refs):
        if has_norm:
            x_ref, dy_ref, nw_ref, o_ref, acc_ref = refs
        else:
            x_ref, dy_ref, o_ref, acc_ref = refs
        t = pl.program_id(2)

        @pl.when(t == 0)
        def _():
            acc_ref[...] = jnp.zeros_like(acc_ref)

        xv = x_ref[...]
        if has_norm:
            xv = _rms_fwd(xv.astype(F32), nw_ref[...])
        acc_ref[...] += lax.dot_general(xv.astype(BF16), dy_ref[...].astype(BF16), _TN, preferred_element_type=F32)

        @pl.when(t == nt - 1)
        def _():
            o_ref[...] = acc_ref[...].astype(out_dtype)

    if halves:
        nbh = N // 2 // tn
        assert N // 2 % tn == 0
        dy_spec = pl.BlockSpec((None, tt, tn), lambda a, b, t: (lax.div(b, nbh), t, lax.rem(b, nbh)))
    else:
        dy_spec = pl.BlockSpec((tt, tn), lambda a, b, t: (t, b))
    in_specs = [pl.BlockSpec((tt, tk1), lambda a, b, t: (t, a)), dy_spec]
    args = [x, dy]
    if has_norm:
        in_specs.append(pl.BlockSpec((1, K1), lambda a, b, t: (0, 0)))
        args.append(norm_w.reshape(1, K1))
    return pl.pallas_call(
        body, name=name, grid=(K1 // tk1, N // tn, nt), in_specs=in_specs,
        out_specs=pl.BlockSpec((tk1, tn), lambda a, b, t: (a, b)),
        out_shape=jax.ShapeDtypeStruct((K1, N), out_dtype),
        scratch_shapes=[pltpu.VMEM((tk1, tn), F32)],
        compiler_params=_cparams(("parallel", "parallel", "arbitrary")))(*args)


def _mm_tn_t(dy, x, *, name, norm_w, out_dtype=BF16, tn=1408, tt=1024):
    T, K1 = x.shape
    halves = dy.ndim == 3
    N = 2 * dy.shape[2] if halves else dy.shape[1]
    tn, tt = min(tn, N), min(tt, T)
    assert N % tn == 0 and T % tt == 0, (name, N, T, tn, tt)
    nt = T // tt

    def body(dy_ref, x_ref, nw_ref, o_ref, acc_ref):
        t = pl.program_id(1)

        @pl.when(t == 0)
        def _():
            acc_ref[...] = jnp.zeros_like(acc_ref)

        xn = _rms_fwd(x_ref[...].astype(F32), nw_ref[...]).astype(BF16)
        acc_ref[...] += lax.dot_general(dy_ref[...].astype(BF16), xn, _TN, preferred_element_type=F32)

        @pl.when(t == nt - 1)
        def _():
            o_ref[...] = acc_ref[...].astype(out_dtype)

    if halves:
        nbh = N // 2 // tn
        assert N // 2 % tn == 0
        dy_spec = pl.BlockSpec((None, tt, tn), lambda b, t: (lax.div(b, nbh), t, lax.rem(b, nbh)))
    else:
        dy_spec = pl.BlockSpec((tt, tn), lambda b, t: (t, b))
    return pl.pallas_call(
        body, name=name, grid=(N // tn, nt),
        in_specs=[dy_spec, pl.BlockSpec((tt, K1), lambda b, t: (t, 0)), pl.BlockSpec((1, K1), lambda b, t: (0, 0))],
        out_specs=pl.BlockSpec((tn, K1), lambda b, t: (b, 0)),
        out_shape=jax.ShapeDtypeStruct((N, K1), out_dtype),
        scratch_shapes=[pltpu.VMEM((tn, K1), F32)],
        compiler_params=_cparams(("parallel", "arbitrary")))(dy, x, norm_w.reshape(1, K1))


def _shift_down(xb, prev8, j):
    main = pltpu.roll(xb, j, 0)
    head = pltpu.roll(xb[0:8], j, 0)
    ph = pltpu.roll(prev8, j, 0)
    row8 = lax.broadcasted_iota(jnp.int32, head.shape, 0)
    head = jnp.where(row8 < j, ph, head)
    return jnp.concatenate([head, main[8:]], axis=0)


def _shift_up(xb, next8, j):
    tt = xb.shape[0]
    main = pltpu.roll(xb, tt - j, 0)
    tail = pltpu.roll(xb[tt - 8:tt], 8 - j, 0)
    nh = pltpu.roll(next8, 8 - j, 0)
    row8 = lax.broadcasted_iota(jnp.int32, tail.shape, 0)
    tail = jnp.where(row8 + j >= 8, nh, tail)
    return jnp.concatenate([main[:tt - 8], tail], axis=0)


def _conv_hid(xb, prev8, w, b_row, K):
    out = b_row
    shifted = []
    for j in range(K):
        sh = K - 1 - j
        xs = xb if sh == 0 else _shift_down(xb, prev8, sh)
        shifted.append(xs)
        out = out + xs * w[j:j + 1, :]
    return out, shifted


def _prev_idx(i, nb8):
    return jnp.maximum(i * nb8 - 1, 0)


def _ssm_conv_fwd(zx, w, b, *, name, tt=512, tc=512):
    T = zx.shape[0]
    tt = min(tt, T)
    C, K = CONV_DIM, SSM_CONV
    cb0, nb8 = D_INNER // tc, tt // 8

    def body(x_ref, p_ref, w_ref, b_ref, o_ref):
        first = (pl.program_id(1) > 0).astype(F32)
        hid, _ = _conv_hid(x_ref[...], p_ref[...] * first, w_ref[...], b_ref[...], K)
        o_ref[...] = hid * _sigmoid(hid)

    return pl.pallas_call(
        body, name=name, grid=(C // tc, T // tt),
        in_specs=[pl.BlockSpec((tt, tc), lambda c, i: (i, c + cb0)),
                  pl.BlockSpec((8, tc), lambda c, i: (_prev_idx(i, nb8), c + cb0)),
                  pl.BlockSpec((K, tc), lambda c, i: (0, c)), pl.BlockSpec((1, tc), lambda c, i: (0, c))],
        out_specs=pl.BlockSpec((tt, tc), lambda c, i: (i, c)),
        out_shape=jax.ShapeDtypeStruct((T, C), F32),
        compiler_params=_cparams(("parallel", "parallel")))(zx, zx, w, b)


def _ssm_conv_bwd_pre(zx, w, b, dout, *, name, tt=512, tc=512):
    T = zx.shape[0]
    tt = min(tt, T)
    C, K = CONV_DIM, SSM_CONV
    cb0, nb8 = D_INNER // tc, tt // 8

    def body(x_ref, p_ref, w_ref, b_ref, d_ref, dh_ref, dw_ref, db_ref):
        t = pl.program_id(1)
        first = (t > 0).astype(F32)
        hid, shifted = _conv_hid(x_ref[...], p_ref[...] * first, w_ref[...], b_ref[...], K)
        sg = _sigmoid(hid)
        dh = d_ref[...] * (sg * (1.0 + hid * (1.0 - sg)))
        dh_ref[...] = dh

        @pl.when(t == 0)
        def _():
            dw_ref[...] = jnp.zeros_like(dw_ref)
            db_ref[...] = jnp.zeros_like(db_ref)

        db_ref[...] += jnp.sum(dh, axis=0, keepdims=True)
        for j in range(K):
            dw_ref[j:j + 1, :] += jnp.sum(dh * shifted[j], axis=0, keepdims=True)

    return pl.pallas_call(
        body, name=name, grid=(C // tc, T // tt),
        in_specs=[pl.BlockSpec((tt, tc), lambda c, i: (i, c + cb0)),
                  pl.BlockSpec((8, tc), lambda c, i: (_prev_idx(i, nb8), c + cb0)),
                  pl.BlockSpec((K, tc), lambda c, i: (0, c)), pl.BlockSpec((1, tc), lambda c, i: (0, c)),
                  pl.BlockSpec((tt, tc), lambda c, i: (i, c))],
        out_specs=[pl.BlockSpec((tt, tc), lambda c, i: (i, c)), pl.BlockSpec((K, tc), lambda c, i: (0, c)),
                   pl.BlockSpec((1, tc), lambda c, i: (0, c))],
        out_shape=[jax.ShapeDtypeStruct((T, C), F32), jax.ShapeDtypeStruct((K, C), F32),
                   jax.ShapeDtypeStruct((1, C), F32)],
        compiler_params=_cparams(("parallel", "arbitrary")))(zx, zx, w, b, dout)


def _put_cols(buf, src, col0, *, name, tt=512):
    T, C = src.shape
    tt = min(tt, T)

    def body(s_ref, _, o_ref):
        o_ref[...] = s_ref[...]

    return pl.pallas_call(
        body, name=name, grid=(T // tt,),
        in_specs=[pl.BlockSpec((tt, C), lambda i: (i, 0)), _ANY],
        out_specs=pl.BlockSpec((tt, C), lambda i: (i, col0 // C)),
        out_shape=jax.ShapeDtypeStruct(buf.shape, buf.dtype), input_output_aliases={1: 0},
        compiler_params=_cparams(("parallel",)))(src, buf)


def _conv_bwd_in(dh, w, *, name, K, tt=512, tc=512, out_dtype=BF16, into=None):
    T, C = dh.shape
    tt = min(tt, T)
    nb8, nT = tt // 8, T // tt
    last8 = T // 8 - 1
    cb0 = 0 if into is None else into[1] // tc

    def body(d_ref, n_ref, w_ref, *rest):
        o_ref = rest[-1]
        notlast = (pl.program_id(1) < nT - 1).astype(F32)
        d = d_ref[...]
        nxt = n_ref[...] * notlast
        w_ = w_ref[...]
        acc = d * w_[K - 1:K, :]
        for sh in range(1, K):
            acc = acc + _shift_up(d, nxt, sh) * w_[K - 1 - sh:K - sh, :]
        o_ref[...] = acc.astype(out_dtype)

    in_specs = [pl.BlockSpec((tt, tc), lambda c, i: (i, c)),
                pl.BlockSpec((8, tc), lambda c, i: (jnp.minimum((i + 1) * nb8, last8), c)),
                pl.BlockSpec((K, tc), lambda c, i: (0, c))]
    args = [dh, dh, w]
    if into is None:
        out_shape, alias = jax.ShapeDtypeStruct((T, C), out_dtype), {}
    else:
        assert into[0].dtype == out_dtype and into[1] % tc == 0
        in_specs.append(_ANY)
        args.append(into[0])
        out_shape, alias = jax.ShapeDtypeStruct(into[0].shape, out_dtype), {3: 0}
    return pl.pallas_call(
        body, name=name, grid=(C // tc, nT), in_specs=in_specs,
        out_specs=pl.BlockSpec((tt, tc), lambda c, i: (i, c + cb0)),
        out_shape=out_shape, input_output_aliases=alias,
        compiler_params=_cparams(("parallel", "parallel")))(*args)


def _ffn_conv_fwd3(a3, w, b, *, name, tt=256, tc=1408):
    T = a3.shape[1]
    tt = min(tt, T)
    K, nbh, n16 = FFN_CONV, D_FF // tc, tt // 16

    def body(a_ref, p_ref, wg_ref, wv_ref, bg_ref, bv_ref, o_ref):
        first = (pl.program_id(1) > 0).astype(F32)
        a = a_ref[...].astype(F32)
        prev = p_ref[...].astype(F32)[:, 8:16, :] * first
        hg, _ = _conv_hid(a[0], prev[0], wg_ref[...], bg_ref[...], K)
        hv, _ = _conv_hid(a[1], prev[1], wv_ref[...], bv_ref[...], K)
        o_ref[...] = (hg * _sigmoid(hg) * hv).astype(BF16)

    return pl.pallas_call(
        body, name=name, grid=(nbh, T // tt),
        in_specs=[pl.BlockSpec((2, tt, tc), lambda c, i: (0, i, c)),
                  pl.BlockSpec((2, 16, tc), lambda c, i: (0, _prev_idx(i, n16), c)),
                  pl.BlockSpec((K, tc), lambda c, i: (0, c)), pl.BlockSpec((K, tc), lambda c, i: (0, c + nbh)),
                  pl.BlockSpec((1, tc), lambda c, i: (0, c)), pl.BlockSpec((1, tc), lambda c, i: (0, c + nbh))],
        out_specs=pl.BlockSpec((tt, tc), lambda c, i: (i, c)),
        out_shape=jax.ShapeDtypeStruct((T, D_FF), BF16),
        compiler_params=_cparams(("parallel", "parallel")))(a3, a3, w, w, b, b)


def _ffn_conv_bwd3(a3, w, b, dp, *, name, tt=256, tc=1408):
    T = a3.shape[1]
    tt = min(tt, T)
    K, nbh, n16 = FFN_CONV, D_FF // tc, tt // 16

    def body(a_ref, p_ref, wg_ref, wv_ref, bg_ref, bv_ref, dp_ref, dh_ref, dw_ref, db_ref):
        t = pl.program_id(1)
        first = (t > 0).astype(F32)
        a = a_ref[...].astype(F32)
        prev = p_ref[...].astype(F32)[:, 8:16, :] * first
        hg, sh_g = _conv_hid(a[0], prev[0], wg_ref[...], bg_ref[...], K)
        hv, sh_v = _conv_hid(a[1], prev[1], wv_ref[...], bv_ref[...], K)
        sg = _sigmoid(hg)
        d = dp_ref[...].astype(F32)
        dhg = d * hv * (sg * (1.0 + hg * (1.0 - sg)))
        dhv = d * (hg * sg)
        dh_ref[0] = dhg.astype(BF16)
        dh_ref[1] = dhv.astype(BF16)

        @pl.when(t == 0)
        def _():
            dw_ref[...] = jnp.zeros_like(dw_ref)
            db_ref[...] = jnp.zeros_like(db_ref)

        db_ref[0] += jnp.sum(dhg, axis=0, keepdims=True)
        db_ref[1] += jnp.sum(dhv, axis=0, keepdims=True)
        for j in range(K):
            dw_ref[0, j:j + 1, :] += jnp.sum(dhg * sh_g[j], axis=0, keepdims=True)
            dw_ref[1, j:j + 1, :] += jnp.sum(dhv * sh_v[j], axis=0, keepdims=True)

    return pl.pallas_call(
        body, name=name, grid=(nbh, T // tt),
        in_specs=[pl.BlockSpec((2, tt, tc), lambda c, i: (0, i, c)),
                  pl.BlockSpec((2, 16, tc), lambda c, i: (0, _prev_idx(i, n16), c)),
                  pl.BlockSpec((K, tc), lambda c, i: (0, c)), pl.BlockSpec((K, tc), lambda c, i: (0, c + nbh)),
                  pl.BlockSpec((1, tc), lambda c, i: (0, c)), pl.BlockSpec((1, tc), lambda c, i: (0, c + nbh)),
                  pl.BlockSpec((tt, tc), lambda c, i: (i, c))],
        out_specs=[pl.BlockSpec((2, tt, tc), lambda c, i: (0, i, c)), pl.BlockSpec((2, K, tc), lambda c, i: (0, 0, c)),
                   pl.BlockSpec((2, 1, tc), lambda c, i: (0, 0, c))],
        out_shape=[jax.ShapeDtypeStruct((2, T, D_FF), BF16), jax.ShapeDtypeStruct((2, K, D_FF), F32),
                   jax.ShapeDtypeStruct((2, 1, D_FF), F32)],
        compiler_params=_cparams(("parallel", "arbitrary")))(a3, a3, w, w, b, b, dp)


def _conv_bwd_in3(dh3, w, *, name, K, tt=256, tc=1408):
    H, T, C = dh3.shape
    tt = min(tt, T)
    nb, n16, nT = C // tc, tt // 16, T // tt
    last16 = T // 16 - 1

    def body(d_ref, n_ref, w_ref, o_ref):
        notlast = (pl.program_id(2) < nT - 1).astype(F32)
        d = d_ref[...].astype(F32)
        nxt = n_ref[...].astype(F32)[0:8, :] * notlast
        w_ = w_ref[...]
        acc = d * w_[K - 1:K, :]
        for sh in range(1, K):
            acc = acc + _shift_up(d, nxt, sh) * w_[K - 1 - sh:K - sh, :]
        o_ref[...] = acc.astype(BF16)

    return pl.pallas_call(
        body, name=name, grid=(H, nb, nT),
        in_specs=[pl.BlockSpec((None, tt, tc), lambda h, c, i: (h, i, c)),
                  pl.BlockSpec((None, 16, tc), lambda h, c, i: (h, jnp.minimum((i + 1) * n16, last16), c)),
                  pl.BlockSpec((K, tc), lambda h, c, i: (0, h * nb + c))],
        out_specs=pl.BlockSpec((None, tt, tc), lambda h, c, i: (h, i, c)),
        out_shape=jax.ShapeDtypeStruct((H, T, C), BF16),
        compiler_params=_cparams(("parallel", "parallel", "parallel")))(dh3, dh3, w)


def _cumsum_rows(x):
    L = x.shape[0]
    row = lax.broadcasted_iota(jnp.int32, x.shape, 0)
    k = 1
    while k < L:
        x = x + jnp.where(row >= k, pltpu.roll(x, k, 0), 0.0)
        k *= 2
    return x


def _rcumsum_rows(x):
    L = x.shape[0]
    row = lax.broadcasted_iota(jnp.int32, x.shape, 0)
    k = 1
    while k < L:
        x = x + jnp.where(row < L - k, pltpu.roll(x, L - k, 0), 0.0)
        k *= 2
    return x


def _split_terms(m, n):
    terms, rest = [], m
    for _ in range(n):
        t = rest.astype(BF16)
        terms.append(t)
        rest = rest - t.astype(F32)
    return jnp.concatenate(terms, axis=1)


def _select_dot(m, n_terms, n_out, cond):
    K = m.shape[1]
    k = lax.broadcasted_iota(jnp.int32, (K, n_out), 0)
    j = lax.broadcasted_iota(jnp.int32, (K, n_out), 1)
    sel = cond(k, j).astype(BF16)
    return jnp.dot(_split_terms(m, n_terms), jnp.concatenate([sel] * n_terms, axis=0), preferred_element_type=F32)


def _rowsum_mxu(m):
    return _select_dot(m, 2, 128, lambda k, j: k >= 0)


def _lane_block_sums(m, width):
    shift = width.bit_length() - 1
    return _select_dot(m, 2, 128, lambda k, j: j == jnp.right_shift(k, shift))


def _heads_to_pairs(m):
    return _select_dot(m, 3, 512, lambda k, j: k == jnp.right_shift(j, 6))


def _ssd_common(dt_ref, par_ref):
    par = par_ref[...]
    raw = dt_ref[...] + par[0:1, :]
    dt = _softplus(raw)
    a = -jnp.exp(par[1:2, :])
    cs = _cumsum_rows(dt * a)
    L = cs.shape[0]
    cs_last = cs[L - 1:L, :]
    return raw, dt, a, par[2:3, :], cs, cs.T, jnp.exp(cs), jnp.exp(cs_last - cs), jnp.exp(cs_last)


def _ssd_specs(nc, rev):
    L = SSM_CHUNK

    def ci(c):
        return nc - 1 - c if rev else c

    return [pl.BlockSpec((L, D_INNER), lambda c: (ci(c), 0)),
            pl.BlockSpec((L, GN), lambda c: (ci(c), D_INNER // GN)),
            pl.BlockSpec((L, GN), lambda c: (ci(c), D_INNER // GN + 1)),
            pl.BlockSpec((SSM_GROUPS, L, 128), lambda c: (0, ci(c), 0)),
            pl.BlockSpec((SSM_GROUPS, 8, 128), lambda c: (0, 0, 0)),
            pl.BlockSpec((L, D_INNER), lambda c: (ci(c), 0)),
            pl.BlockSpec((1, D_INNER), lambda c: (0, 0))], ci


def _round_robin(gens):
    live = list(gens)
    while live:
        nxt = []
        for gen in live:
            try:
                next(gen)
                nxt.append(gen)
            except StopIteration:
                pass
        live = nxt


def _group_views(g, wide, narrow, lead):
    return ([r.at[:, g * 512:(g + 1) * 512] for r in wide], [r.at[:, g * 128:(g + 1) * 128] for r in narrow],
            [r.at[g] for r in lead])


def _ssd_fwd(xbc_c, zx, dtg, par, gnw, *, name):
    T = xbc_c.shape[0]
    L = SSM_CHUNK
    nc = T // L
    in_specs, ci = _ssd_specs(nc, False)

    def body(xs_ref, b_ref, c_ref, dt_ref, par_ref, z_ref, gnw_ref, y_ref, yn_ref, st_ref, h_ref):
        @pl.when(pl.program_id(0) == 0)
        def _():
            h_ref[...] = jnp.zeros_like(h_ref)

        gens = []
        for g in range(SSM_GROUPS):
            (xs, z, gw, y, yn), (b, c), (dt, pr, st, h) = _group_views(
                g, [xs_ref, z_ref, gnw_ref, y_ref, yn_ref], [b_ref, c_ref], [dt_ref, par_ref, st_ref, h_ref])
            gens.append(group(xs, b, c, dt, pr, z, gw, y, yn, st, h))
        _round_robin(gens)

    def group(xs_ref, b_ref, c_ref, dt_ref, par_ref, z_ref, gnw_ref, y_ref, yn_ref, st_ref, h_ref):
        _, dt, _, dsk, cs, csT, ecs, eend, dec = _ssd_common(dt_ref, par_ref)
        Bb = b_ref[...].astype(BF16)
        Cb = c_ref[...].astype(BF16)
        G = lax.dot_general(Cb, Bb, _NT, preferred_element_type=F32)
        row = lax.broadcasted_iota(jnp.int32, (L, L), 0)
        col = lax.broadcasted_iota(jnp.int32, (L, L), 1)
        tril = col <= row
        lo = lax.broadcasted_iota(jnp.int32, (L, 128), 1) < 64
        lo1 = lax.broadcasted_iota(jnp.int32, (1, 128), 1) < 64
        dt_x, ecs_x, eend_x = (_heads_to_pairs(m) for m in (dt, ecs, eend))
        for pp in range(4):
            hA, hB = 2 * pp, 2 * pp + 1
            lanes = slice(pp * 128, (pp + 1) * 128)

            def sel1(m):
                return jnp.where(lo1, m[:, hA:hA + 1], m[:, hB:hB + 1])

            X = xs_ref[:, lanes]
            xd = X * dt_x[:, lanes]
            xdb = xd.astype(BF16)
            ys = []
            for h in (hA, hB):
                Lm = jnp.where(tril, jnp.exp(jnp.minimum(cs[:, h:h + 1] - csT[h:h + 1, :], 0.0)), 0.0)
                ys.append(jnp.dot((G * Lm).astype(BF16), xdb, preferred_element_type=F32))
                yield
            Hp = h_ref[pp]
            st_ref[pp] = Hp
            yoff = jnp.dot(Cb, Hp.astype(BF16), preferred_element_type=F32) * ecs_x[:, lanes]
            y_ref[:, lanes] = jnp.where(lo, ys[0], ys[1]) + yoff + sel1(dsk) * X
            S = lax.dot_general(Bb, (xd * eend_x[:, lanes]).astype(BF16), _TN, preferred_element_type=F32)
            h_ref[pp] = Hp * sel1(dec) + S
            yield
        zv = z_ref[...]
        yg = y_ref[...] * (zv * _sigmoid(zv))
        r = jnp.tile(lax.rsqrt(_rowsum_mxu(yg * yg) * (1.0 / 512) + EPS), (1, 4))
        yn_ref[...] = (yg * r * gnw_ref[...]).astype(BF16)

    return pl.pallas_call(
        body, name=name, grid=(nc,), in_specs=in_specs,
        out_specs=[pl.BlockSpec((L, D_INNER), lambda c: (c, 0)), pl.BlockSpec((L, D_INNER), lambda c: (c, 0)),
                   pl.BlockSpec((SSM_GROUPS, None, 4, 128, 128), lambda c: (0, c, 0, 0, 0))],
        out_shape=[jax.ShapeDtypeStruct((T, D_INNER), F32), jax.ShapeDtypeStruct((T, D_INNER), BF16),
                   jax.ShapeDtypeStruct((SSM_GROUPS, nc, 4, 128, 128), F32)],
        scratch_shapes=[pltpu.VMEM((SSM_GROUPS, 4, 128, 128), F32)],
        compiler_params=_cparams(("arbitrary",)))(xbc_c, xbc_c, xbc_c, dtg, par, zx, gnw)


def _ssd_bwd(xbc_c, zx, dtg, par, gnw, y, st, dyn, *, name):
    T = xbc_c.shape[0]
    L = SSM_CHUNK
    nc = T // L
    in_specs, ci = _ssd_specs(nc, True)
    in_specs += [pl.BlockSpec((L, D_INNER), lambda c: (ci(c), 0)),
                 pl.BlockSpec((SSM_GROUPS, None, 4, 128, 128), lambda c: (0, ci(c), 0, 0, 0)),
                 pl.BlockSpec((L, D_INNER), lambda c: (ci(c), 0))]

    def body(xs_ref, b_ref, c_ref, dt_ref, par_ref, z_ref, gnw_ref, y_ref, st_ref, dyn_ref,
             dxbc_ref, dz_ref, ddt_ref, dgnw_ref, dpar_ref, dh_ref):
        @pl.when(pl.program_id(0) == 0)
        def _():
            dh_ref[...] = jnp.zeros_like(dh_ref)
            dgnw_ref[...] = jnp.zeros_like(dgnw_ref)
            dpar_ref[...] = jnp.zeros_like(dpar_ref)

        dxs_ref = dxbc_ref.at[:, 0:D_INNER]
        db_ref = dxbc_ref.at[:, D_INNER:D_INNER + GN]
        dc_ref = dxbc_ref.at[:, D_INNER + GN:CONV_DIM]

        gens = []
        for g in range(SSM_GROUPS):
            (xs, z, gw, y, dyn, dxs, dz, dgw), (b, c, db, dc), (dt, pr, st, ddt, dpr, dh) = _group_views(
                g, [xs_ref, z_ref, gnw_ref, y_ref, dyn_ref, dxs_ref, dz_ref, dgnw_ref], [b_ref, c_ref, db_ref, dc_ref],
                [dt_ref, par_ref, st_ref, ddt_ref, dpar_ref, dh_ref])
            gens.append(group(xs, b, c, dt, pr, z, gw, y, st, dyn, dxs, db, dc, dz, ddt, dgw, dpr, dh))
        _round_robin(gens)

    def group(xs_ref, b_ref, c_ref, dt_ref, par_ref, z_ref, gnw_ref, y_ref, st_ref, dyn_ref,
              dxs_ref, db_ref, dc_ref, dz_ref, ddt_ref, dgnw_ref, dpar_ref, dh_ref):
        yv = y_ref[...]
        zv = z_ref[...]
        sg = _sigmoid(zv)
        sz = zv * sg
        yg = yv * sz
        r = jnp.tile(lax.rsqrt(_rowsum_mxu(yg * yg) * (1.0 / 512) + EPS), (1, 4))
        yh = yg * r
        dyn = dyn_ref[...].astype(F32)
        dgnw_ref[...] += jnp.sum(dyn * yh, axis=0, keepdims=True)
        dyh = dyn * gnw_ref[...]
        dyg = r * (dyh - yh * jnp.tile(_rowsum_mxu(dyh * yh) * (1.0 / 512), (1, 4)))
        dY_all = dyg * sz
        dz_ref[...] = (dyg * yv * (sg * (1.0 + zv * (1.0 - sg)))).astype(dz_ref.dtype)

        yield
        raw, dt, a, dsk, cs, csT, ecs, eend, dec = _ssd_common(dt_ref, par_ref)
        Bb = b_ref[...].astype(BF16)
        Cb = c_ref[...].astype(BF16)
        G = lax.dot_general(Cb, Bb, _NT, preferred_element_type=F32)
        row = lax.broadcasted_iota(jnp.int32, (L, L), 0)
        col = lax.broadcasted_iota(jnp.int32, (L, L), 1)
        tril = col <= row
        lo = lax.broadcasted_iota(jnp.int32, (L, 128), 1) < 64
        lane1 = lax.broadcasted_iota(jnp.int32, (1, 128), 1)
        lo1 = lane1 < 64
        rowl = lax.broadcasted_iota(jnp.int32, (L, 128), 0)
        dt_x, ecs_x, eend_x = (_heads_to_pairs(m) for m in (dt, ecs, eend))
        dG = jnp.zeros((L, L), F32)
        dB = jnp.zeros((L, SSM_STATE), F32)
        dC = jnp.zeros((L, SSM_STATE), F32)
        dcs_t = jnp.zeros((L, L), F32)
        tails = jnp.zeros((1, 128), F32)
        dD_row = jnp.zeros((1, 128), F32)
        v_parts, prod_parts = [], []

        def tot(m):
            return jnp.sum(jnp.sum(m, axis=0, keepdims=True), axis=1, keepdims=True)

        for pp in range(4):
            hA, hB = 2 * pp, 2 * pp + 1
            lanes = slice(pp * 128, (pp + 1) * 128)

            def sel1(m):
                return jnp.where(lo1, m[:, hA:hA + 1], m[:, hB:hB + 1])

            X = xs_ref[:, lanes]
            dY = dY_all[:, lanes]
            dtsel = dt_x[:, lanes]
            xd = X * dtsel
            xdb = xd.astype(BF16)
            dYb = dY.astype(BF16)
            Hp = st_ref[pp]
            Hb = Hp.astype(BF16)
            dHn = dh_ref[pp]
            dHb = dHn.astype(BF16)
            ecs_sel = ecs_x[:, lanes]
            eend_sel = eend_x[:, lanes]
            dxd_state = jnp.dot(Bb, dHb, preferred_element_type=F32) * eend_sel
            yoff = jnp.dot(Cb, Hb, preferred_element_type=F32) * ecs_sel
            dYe = (dY * ecs_sel).astype(BF16)
            dC = dC + lax.dot_general(dYe, Hb, _NT, preferred_element_type=F32)
            dB = dB + lax.dot_general((xd * eend_sel).astype(BF16), dHb, _NT, preferred_element_type=F32)
            dh_ref[pp] = dHn * sel1(dec) + lax.dot_general(Cb, dYe, _TN, preferred_element_type=F32)
            q = xd * dxd_state
            dyq = dY * yoff - q
            qcol = jnp.sum(q, axis=0, keepdims=True)
            hcol = jnp.sum(dHn * Hp, axis=0, keepdims=True)
            dxd_diag = []
            for h, msk, msk1 in ((hA, lo, lo1), (hB, jnp.logical_not(lo), jnp.logical_not(lo1))):
                Lm = jnp.where(tril, jnp.exp(jnp.minimum(cs[:, h:h + 1] - csT[h:h + 1, :], 0.0)), 0.0)
                M = G * Lm
                dxd_diag.append(lax.dot_general(M.astype(BF16), dYb, _TN, preferred_element_type=F32))
                dM = lax.dot_general(jnp.where(msk, dY, 0.0).astype(BF16), xdb, _NT, preferred_element_type=F32)
                dG = dG + dM * Lm
                W = dM * M
                dcs_t = dcs_t + jnp.where(row == h, jnp.sum(W, axis=0, keepdims=True), 0.0)
                v_parts.append(W + jnp.where(msk, dyq, 0.0))
                tail = (jnp.sum(jnp.where(msk1, qcol, 0.0), axis=1, keepdims=True)
                        + dec[:, h:h + 1] * jnp.sum(jnp.where(msk1, hcol, 0.0), axis=1, keepdims=True))
                tails = tails + jnp.where(lane1 == h, tail, 0.0)
                yield
            dxd = jnp.where(lo, dxd_diag[0], dxd_diag[1]) + dxd_state
            prod_parts.append(dxd * X)
            dxs_ref[:, lanes] = dxd * dtsel + sel1(dsk) * dY
            dyx = jnp.sum(dY * X, axis=0, keepdims=True)
            sA = jnp.sum(jnp.where(lo1, dyx, 0.0), axis=1, keepdims=True)
            sB = jnp.sum(dyx, axis=1, keepdims=True) - sA
            dD_row = dD_row + jnp.where(lane1 == hA, sA, 0.0) + jnp.where(lane1 == hB, sB, 0.0)
            yield
        dGb = dG.astype(BF16)
        db_ref[...] = dB + lax.dot_general(dGb, Cb, _TN, preferred_element_type=F32)
        dc_ref[...] = dC + jnp.dot(dGb, Bb, preferred_element_type=F32)
        dcs_mat = _lane_block_sums(jnp.concatenate(v_parts, axis=1), 128) + jnp.where(rowl == L - 1, tails, 0.0)
        ddt_mat = _lane_block_sums(jnp.concatenate(prod_parts, axis=1), 64)
        dad = _rcumsum_rows(dcs_mat - dcs_t.T)
        draw = (a * dad + ddt_mat) * _sigmoid(raw)
        ddt_ref[...] = draw
        dpar_ref[0:1, :] += jnp.sum(draw, axis=0, keepdims=True)
        dpar_ref[1:2, :] += jnp.sum(dt * dad, axis=0, keepdims=True) * a
        dpar_ref[2:3, :] += dD_row

    return pl.pallas_call(
        body, name=name, grid=(nc,), in_specs=in_specs,
        out_specs=[pl.BlockSpec((L, CONV_DIM), lambda c: (ci(c), 0)),
                   pl.BlockSpec((L, D_INNER), lambda c: (ci(c), 0)),
                   pl.BlockSpec((SSM_GROUPS, L, 128), lambda c: (0, ci(c), 0)),
                   pl.BlockSpec((1, D_INNER), lambda c: (0, 0)),
                   pl.BlockSpec((SSM_GROUPS, 8, 128), lambda c: (0, 0, 0))],
        out_shape=[jax.ShapeDtypeStruct((T, CONV_DIM), F32), jax.ShapeDtypeStruct((T, IN_PROJ_PAD), BF16),
                   jax.ShapeDtypeStruct((SSM_GROUPS, T, 128), F32), jax.ShapeDtypeStruct((1, D_INNER), F32),
                   jax.ShapeDtypeStruct((SSM_GROUPS, 8, 128), F32)],
        scratch_shapes=[pltpu.VMEM((SSM_GROUPS, 4, 128, 128), F32)],
        compiler_params=_cparams(("arbitrary",)))(xbc_c, xbc_c, xbc_c, dtg, par, zx, gnw, y, st, dyn)


SB_KEYS = 512
SB_SCAN = 256
SB_STRIP = 256


def _tri(width, cond):
    kk = lax.broadcasted_iota(jnp.int32, (width, width), 0)
    jj = lax.broadcasted_iota(jnp.int32, (width, width), 1)
    return cond(kk, jj).astype(BF16)


_LOG2E = 1.4426950408889634


def _softplus2(z2):
    return jnp.maximum(z2, 0.0) + jnp.log2(1.0 + jnp.exp2(-jnp.abs(z2)))


def _sba_sub_fwd(zb, c, U, mask):
    z2 = zb * _LOG2E
    s = _softplus2(z2)
    if mask is not None:
        s = jnp.where(mask, s, 0.0)
    R = c + jnp.dot(s.astype(BF16), U, preferred_element_type=F32)
    A = jnp.exp2(z2 - s - R)
    if mask is not None:
        A = jnp.where(mask, A, 0.0)
    return A.astype(BF16), R[:, 0:1] + s[:, 0:1]


def _sba_sub_bwd(zb, dAb, Lt, pc, pe, Uincl, Uexcl, mask):
    last = zb.shape[1] - 1
    z2 = zb * _LOG2E
    s = _softplus2(z2)
    g = z2 - s
    if mask is not None:
        s = jnp.where(mask, s, 0.0)
    P = pc + jnp.dot(s.astype(BF16), Uincl, preferred_element_type=F32)
    A = jnp.exp2(g - (Lt - P))
    if mask is not None:
        A = jnp.where(mask, A, 0.0)
    E = dAb * A
    PE = pe + jnp.dot(E.astype(BF16), Uexcl, preferred_element_type=F32)
    dz = E - jnp.exp2(g) * (E + PE)
    if mask is not None:
        dz = jnp.where(mask, dz, 0.0)
    return (A.astype(BF16), dz.astype(BF16), P[:, last:last + 1], PE[:, last:last + 1] + E[:, last:last + 1])


def _stack_heads(v):
    lo = lax.broadcasted_iota(jnp.int32, v.shape, 1) < 64
    zero = jnp.zeros_like(v)
    return jnp.concatenate([jnp.where(lo, v, zero), jnp.where(lo, zero, v)], axis=0)


def _unstack_heads(v):
    lo = lax.broadcasted_iota(jnp.int32, (SB_BLOCK, 128), 1) < 64
    return jnp.where(lo, v[:SB_BLOCK], v[SB_BLOCK:])


def _sba_rows(a):
    return slice(2 * a * SB_BLOCK, 2 * (a + 1) * SB_BLOCK)


def _sba_diag_case(a, b):
    Bq = SB_BLOCK
    if b * SB_SCAN >= (a + 1) * Bq:
        return "skip"
    if (b + 1) * SB_SCAN <= a * Bq:
        return "full"
    rowi = lax.broadcasted_iota(jnp.int32, (2 * Bq, SB_SCAN), 0)
    qpos = a * Bq + jnp.where(rowi >= Bq, rowi - Bq, rowi)
    return b * SB_SCAN + lax.broadcasted_iota(jnp.int32, (2 * Bq, SB_SCAN), 1) < qpos


def _sba_fwd(q, kv, *, name):
    T = q.shape[0]
    Bq = SB_BLOCK
    nsub = SB_KEYS // Bq
    nscan = SB_KEYS // SB_SCAN
    R = 2 * SB_KEYS
    assert T % SB_KEYS == 0 and SB_STRIP == 2 * Bq
    scale = 1.0 / math.sqrt(SB_HEAD_DIM)

    def body(q_ref, k_ref, v_ref, o_ref, lt_ref, z_s, a_s, c_s, acc_s):
        i = pl.program_id(1)
        U2 = _tri(SB_SCAN, lambda k, j: k > j)
        qs_all = jnp.concatenate([_stack_heads(q_ref[a * Bq:(a + 1) * Bq, :] * scale) for a in range(nsub)], axis=0)
        c_s[...] = jnp.zeros_like(c_s)
        acc_s[...] = jnp.zeros_like(acc_s)

        def scores(J, slot):
            off = pl.multiple_of(J * SB_KEYS, SB_KEYS)
            z_s[slot] = lax.dot_general(qs_all, k_ref[pl.ds(off, SB_KEYS), :], _NT, preferred_element_type=F32)

        def weights(slot, diag):
            for a in range(nsub):
                rows = _sba_rows(a)
                c = c_s[rows, :]
                for b in reversed(range(nscan)):
                    cols = slice(b * SB_SCAN, (b + 1) * SB_SCAN)
                    case = _sba_diag_case(a, b) if diag else "full"
                    if isinstance(case, str) and case == "skip":
                        a_s[slot, rows, cols] = jnp.zeros((2 * Bq, SB_SCAN), BF16)
                        continue
                    A, c = _sba_sub_fwd(z_s[slot, rows, cols], c, U2, None if isinstance(case, str) else case)
                    a_s[slot, rows, cols] = A
                c_s[rows, :] = c

        def values(J, slot):
            off = pl.multiple_of(J * SB_KEYS, SB_KEYS)
            acc_s[...] += jnp.dot(a_s[slot], v_ref[pl.ds(off, SB_KEYS), :], preferred_element_type=F32)

        scores(i, 0)
        weights(0, True)
        scores(jnp.maximum(i - 1, 0), 1)

        def two_steps(u, _):
            t = 2 * u + 1
            weights(1, False)
            scores(jnp.maximum(i - t - 1, 0), 0)
            values(i - t + 1, 0)
            weights(0, False)
            scores(jnp.maximum(i - t - 2, 0), 1)
            values(i - t, 1)
            return 0

        lax.fori_loop(0, i // 2, two_steps, 0)
        odd = lax.rem(i, 2) == 1

        @pl.when(jnp.logical_not(odd))
        def _():
            values(0, 0)

        @pl.when(odd)
        def _():
            weights(1, False)
            values(1, 0)
            values(0, 1)
        for a in range(nsub):
            o_ref[a * Bq:(a + 1) * Bq, :] = _unstack_heads(acc_s[_sba_rows(a), :]).astype(BF16)
            lt_ref[a * Bq:(a + 1) * Bq, :] = _unstack_heads(jnp.broadcast_to(c_s[_sba_rows(a), :], (2 * Bq, 128)))

    return pl.pallas_call(
        body, name=name, grid=(SB_HEADS // 2, T // SB_KEYS),
        in_specs=[pl.BlockSpec((SB_KEYS, 128), lambda p, i: (i, p)), pl.BlockSpec((T, 128), lambda p, i: (0, p)),
                  pl.BlockSpec((T, 128), lambda p, i: (0, p + SB_HEADS // 2))],
        out_specs=[pl.BlockSpec((SB_KEYS, 128), lambda p, i: (i, p)),
                   pl.BlockSpec((None, SB_KEYS, 128), lambda p, i: (p, i, 0))],
        out_shape=[jax.ShapeDtypeStruct((T, D_MODEL), BF16), jax.ShapeDtypeStruct((SB_HEADS // 2, T, 128), F32)],
        scratch_shapes=[pltpu.VMEM((2, R, SB_KEYS), F32), pltpu.VMEM((2, R, SB_KEYS), BF16),
                        pltpu.VMEM((R, 1), F32), pltpu.VMEM((R, 128), F32)],
        compiler_params=_cparams(("parallel", "parallel")))(q, kv, kv)


def _sba_bwd(q, kv, lt, do, *, name):
    T = q.shape[0]
    Bq = SB_BLOCK
    nq = T // SB_KEYS
    nsub = SB_KEYS // Bq
    nscan = SB_KEYS // SB_SCAN
    R = 2 * SB_KEYS
    assert T % SB_KEYS == 0 and SB_STRIP == 2 * Bq
    scale = 1.0 / math.sqrt(SB_HEAD_DIM)

    def body(q_ref, k_ref, v_ref, lt_ref, do_ref, dq_ref, dk_ref, dv_ref, dk_acc, dv_acc,
             z_s, da_s, a_s, dz_s, pc_s, pe_s, lt_s, dq_s):
        i = pl.program_id(1)

        @pl.when(i == 0)
        def _():
            dk_acc[...] = jnp.zeros_like(dk_acc)
            dv_acc[...] = jnp.zeros_like(dv_acc)

        Uincl = _tri(SB_SCAN, lambda k, j: k <= j)
        Uexcl = _tri(SB_SCAN, lambda k, j: k < j)
        qs, dos = [], []
        for a in range(nsub):
            rows = slice(a * Bq, (a + 1) * Bq)
            qs.append(_stack_heads(q_ref[rows, :] * scale))
            dos.append(_stack_heads(do_ref[rows, :]))
            lt_s[_sba_rows(a), :] = jnp.concatenate([lt_ref[rows, 0:1], lt_ref[rows, 64:65]], axis=0)
        qs_all = jnp.concatenate(qs, axis=0)
        dos_all = jnp.concatenate(dos, axis=0)
        pc_s[...] = jnp.zeros_like(pc_s)
        pe_s[...] = jnp.zeros_like(pe_s)
        a_s[1] = jnp.zeros((R, SB_KEYS), BF16)
        dz_s[1] = jnp.zeros((R, SB_KEYS), BF16)

        def scores(J, slot):
            off = pl.multiple_of(J * SB_KEYS, SB_KEYS)
            z_s[slot] = lax.dot_general(qs_all, k_ref[pl.ds(off, SB_KEYS), :], _NT, preferred_element_type=F32)
            da_s[slot] = lax.dot_general(dos_all, v_ref[pl.ds(off, SB_KEYS), :], _NT, preferred_element_type=F32)

        def gradients(slot, diag):
            for a in range(nsub):
                rows = _sba_rows(a)
                pc, pe, Lt = pc_s[rows, :], pe_s[rows, :], lt_s[rows, :]
                for b in range(nscan):
                    cols = slice(b * SB_SCAN, (b + 1) * SB_SCAN)
                    case = _sba_diag_case(a, b) if diag else "full"
                    if isinstance(case, str) and case == "skip":
                        a_s[slot, rows, cols] = jnp.zeros((2 * Bq, SB_SCAN), BF16)
                        dz_s[slot, rows, cols] = jnp.zeros((2 * Bq, SB_SCAN), BF16)
                        continue
                    A, dz, pc, pe = _sba_sub_bwd(z_s[slot, rows, cols], da_s[slot, rows, cols], Lt, pc, pe, Uincl, Uexcl,
                                                 None if isinstance(case, str) else case)
                    a_s[slot, rows, cols] = A
                    dz_s[slot, rows, cols] = dz
                pc_s[rows, :] = pc
                pe_s[rows, :] = pe

        def products(J, slot):
            off = pl.multiple_of(J * SB_KEYS, SB_KEYS)
            dzt = dz_s[slot]
            dk_acc[pl.ds(off, SB_KEYS), :] += lax.dot_general(dzt, qs_all, _TN, preferred_element_type=F32)
            dv_acc[pl.ds(off, SB_KEYS), :] += lax.dot_general(a_s[slot], dos_all, _TN, preferred_element_type=F32)
            dq_s[...] += jnp.dot(dzt, k_ref[pl.ds(off, SB_KEYS), :], preferred_element_type=F32)

        dq_s[...] = jnp.zeros_like(dq_s)
        scores(0, 0)

        def two_steps(u, _):
            t = 2 * u
            gradients(0, False)
            scores(t + 1, 1)
            products(jnp.maximum(t - 1, 0), 1)
            gradients(1, False)
            scores(t + 2, 0)
            products(t, 0)
            return 0

        lax.fori_loop(0, i // 2, two_steps, 0)
        odd = lax.rem(i, 2) == 1

        @pl.when(jnp.logical_not(odd))
        def _():
            gradients(0, True)
            products(jnp.maximum(i - 1, 0), 1)
            products(i, 0)

        @pl.when(odd)
        def _():
            gradients(0, False)
            scores(i, 1)
            products(jnp.maximum(i - 2, 0), 1)
            gradients(1, True)
            products(i - 1, 0)
            products(i, 1)

        for a in range(nsub):
            dq_ref[a * Bq:(a + 1) * Bq, :] = (_unstack_heads(dq_s[_sba_rows(a), :]) * scale).astype(BF16)

        @pl.when(i == nq - 1)
        def _():
            dk_ref[...] = dk_acc[...].astype(BF16)
            dv_ref[...] = dv_acc[...].astype(BF16)

    return pl.pallas_call(
        body, name=name, grid=(SB_HEADS // 2, nq),
        in_specs=[pl.BlockSpec((SB_KEYS, 128), lambda p, i: (i, p)), pl.BlockSpec((T, 128), lambda p, i: (0, p)),
                  pl.BlockSpec((T, 128), lambda p, i: (0, p + SB_HEADS // 2)),
                  pl.BlockSpec((None, SB_KEYS, 128), lambda p, i: (p, i, 0)),
                  pl.BlockSpec((SB_KEYS, 128), lambda p, i: (i, p))],
        out_specs=[pl.BlockSpec((SB_KEYS, 128), lambda p, i: (i, p)), pl.BlockSpec((T, 128), lambda p, i: (0, p)),
                   pl.BlockSpec((T, 128), lambda p, i: (0, p))],
        out_shape=[jax.ShapeDtypeStruct((T, D_MODEL), BF16), jax.ShapeDtypeStruct((T, D_MODEL), BF16),
                   jax.ShapeDtypeStruct((T, D_MODEL), BF16)],
        scratch_shapes=[pltpu.VMEM((T, 128), F32), pltpu.VMEM((T, 128), F32),
                        pltpu.VMEM((2, R, SB_KEYS), F32), pltpu.VMEM((2, R, SB_KEYS), F32),
                        pltpu.VMEM((2, R, SB_KEYS), BF16), pltpu.VMEM((2, R, SB_KEYS), BF16),
                        pltpu.VMEM((R, 1), F32), pltpu.VMEM((R, 1), F32), pltpu.VMEM((R, 1), F32),
                        pltpu.VMEM((R, 128), F32)],
        compiler_params=_cparams(("parallel", "arbitrary")))(q, kv, kv, lt, do)


def _loss_head(h, tgt, w, *, name, tt=512):
    T, D = h.shape
    tt = min(tt, T)

    def body(h_ref, t_ref, w_ref, loss_ref, dh_ref, dw_ref):
        i = pl.program_id(0)
        hv = h_ref[...]
        wv = w_ref[...]
        r = lax.rsqrt(jnp.mean(hv * hv, axis=-1, keepdims=True) + EPS)
        xhat = hv * r
        err = xhat * wv - t_ref[...]
        part = 0.5 * jnp.sum(jnp.mean(err * err, axis=-1, keepdims=True), axis=0, keepdims=True)
        dy = err * (1.0 / D)
        dxh = dy * wv
        dh_ref[...] = r * (dxh - xhat * jnp.mean(dxh * xhat, axis=-1, keepdims=True))
        dwc = jnp.sum(dy * xhat, axis=0, keepdims=True)

        @pl.when(i == 0)
        def _():
            loss_ref[...] = jnp.broadcast_to(part, loss_ref.shape)
            dw_ref[...] = dwc

        @pl.when(i > 0)
        def _():
            loss_ref[...] += jnp.broadcast_to(part, loss_ref.shape)
            dw_ref[...] += dwc

    return pl.pallas_call(
        body, name=name, grid=(T // tt,),
        in_specs=[pl.BlockSpec((tt, D), lambda i: (i, 0)), pl.BlockSpec((tt, D), lambda i: (i, 0)),
                  pl.BlockSpec((1, D), lambda i: (0, 0))],
        out_specs=[pl.BlockSpec((1, 128), lambda i: (0, 0)), pl.BlockSpec((tt, D), lambda i: (i, 0)),
                   pl.BlockSpec((1, D), lambda i: (0, 0))],
        out_shape=[jax.ShapeDtypeStruct((1, 128), F32), jax.ShapeDtypeStruct((T, D), F32),
                   jax.ShapeDtypeStruct((1, D), F32)],
        compiler_params=_cparams(("arbitrary",)))(h, tgt, w.reshape(1, D))


def _adamw(parts, w, m, v, *, name, tr=256, tc=None):
    plist = list(parts) if isinstance(parts, (list, tuple)) else [parts]
    P, _, C = plist[0].shape
    R = sum(a.shape[1] for a in plist)
    tr = min(tr, R)
    tc = C if tc is None else tc
    assert all(a.shape[1] % tr == 0 for a in plist) and C % tc == 0, (name, R, C, tr, tc)
    nbs = [a.shape[1] // tr for a in plist]
    offs = [sum(nbs[:l]) for l in range(len(nbs))]
    c1 = 1.0 - ADAM_B1 ** ADAM_STEP
    c2 = 1.0 - ADAM_B2 ** ADAM_STEP

    def body(*refs):
        p_refs = refs[:len(plist)]
        w_ref, m_ref, v_ref, g_ref, d_ref, nm_ref, nv_ref = refs[len(plist):]
        i = pl.program_id(0)
        g = None
        for l, p_ref in enumerate(p_refs):
            gl = p_ref[0].astype(F32)
            for k in range(1, P):
                gl = gl + p_ref[k].astype(F32)
            g = gl if g is None else jnp.where(i >= offs[l], gl, g)
        mn = ADAM_B1 * m_ref[...] + (1.0 - ADAM_B1) * g
        vn = ADAM_B2 * v_ref[...] + (1.0 - ADAM_B2) * (g * g)
        g_ref[...] = g
        nm_ref[...] = mn
        nv_ref[...] = vn
        d_ref[...] = -ADAM_LR * ((mn / c1) / (jnp.sqrt(vn / c2) + ADAM_EPS) + ADAM_WD * w_ref[...])

    spec = pl.BlockSpec((tr, tc), lambda i, j: (i, j))
    sds = jax.ShapeDtypeStruct((R, C), F32)
    return pl.pallas_call(
        body, name=name, grid=(R // tr, C // tc),
        in_specs=[pl.BlockSpec((P, tr, tc), functools.partial(lambda i, j, o, n: (0, jnp.clip(i - o, 0, n - 1), j), o=o, n=n))
                  for o, n in zip(offs, nbs)] + [spec, spec, spec],
        out_specs=[spec, spec, spec, spec], out_shape=[sds, sds, sds, sds],
        compiler_params=_cparams(("parallel", "parallel")))(*plist, w, m, v)


def _all_gather(shards, *, name):
    n = len(shards)

    def body(*refs):
        ins, outs = refs[:n], refs[n:2 * n]
        send_sems, recv_sems, local_sems = refs[2 * n:]
        x, y, c = lax.axis_index("x"), lax.axis_index("y"), lax.axis_index("c")
        me, sib = (x, y, c), (x, y, 1 - c)
        chips = [(1 - x, y), (x, 1 - y), (1 - x, 1 - y)]

        def slot(p):
            return 4 * p[0] + 2 * p[1] + p[2]

        def cp(a, k, block, to, src=None):
            dst = outs[a].at[slot(block)]
            return pltpu.make_async_remote_copy(src_ref=dst if src is None else src, dst_ref=dst,
                                                send_sem=send_sems.at[a, k], recv_sem=recv_sems.at[a, k],
                                                device_id=to, device_id_type=_MESH)

        mine = [pltpu.make_async_copy(ins[a], outs[a].at[slot(me)], local_sems.at[a]) for a in range(n)]
        for m in mine:
            m.start()
        first = []
        for a in range(n):
            first.append(cp(a, 0, me, sib, src=ins[a]))
            for j, chip in enumerate(chips):
                first.append(cp(a, 1 + j, me, (*chip, c), src=ins[a]))
        for f in first:
            f.start()
        passed = []
        for j, chip in enumerate(chips):
            for a in range(n):
                cp(a, 1 + j, (*chip, c), me).wait_recv()
                f = cp(a, 4 + j, (*chip, c), sib)
                f.start()
                passed.append(f)
        for a in range(n):
            cp(a, 0, sib, me).wait_recv()
            for j, chip in enumerate(chips):
                cp(a, 4 + j, (*chip, 1 - c), me).wait_recv()
        for f in first + passed:
            f.wait_send()
        for m in mine:
            m.wait()

    return pl.pallas_call(
        body, name=name, in_specs=[_ANY] * n, out_specs=[_ANY] * n,
        out_shape=[jax.ShapeDtypeStruct((N_DEV,) + s.shape, s.dtype) for s in shards],
        scratch_shapes=[pltpu.SemaphoreType.DMA((n, 7)), pltpu.SemaphoreType.DMA((n, 7)),
                        pltpu.SemaphoreType.DMA((n,))])(*shards)


_HBM = pl.BlockSpec(memory_space=pltpu.HBM)
_SEM = pl.BlockSpec(memory_space=pltpu.SEMAPHORE)
_EFFECT = pltpu.SideEffectType.DATAFLOW_SIDE_EFFECTING


def _peers():
    x, y, c = lax.axis_index("x"), lax.axis_index("y"), lax.axis_index("c")
    out = []
    for r in range(1, N_DEV):
        px = 1 - x if (r >> 2) & 1 else x
        py = 1 - y if (r >> 1) & 1 else y
        pc = 1 - c if r & 1 else c
        out.append(((px, py, pc), 4 * px + 2 * py + pc))
    return 4 * x + 2 * y + c, out


def _push_copy(src_ref, land_ref, send_sems, recv_sems, a, k, me, peer, peer_slot, scatter, arriving):
    src = src_ref.at[peer_slot] if scatter else src_ref
    return pltpu.make_async_remote_copy(
        src_ref=src, dst_ref=land_ref.at[peer_slot if arriving else me], send_sem=send_sems.at[a * (N_DEV - 1) + k],
        recv_sem=recv_sems.at[a * (N_DEV - 1) + k], device_id=peer, device_id_type=_MESH)


def _push_start(srcs, *, scatter, name):
    n = len(srcs)
    lands = [lax.empty(s.shape if scatter else (N_DEV,) + s.shape, s.dtype) for s in srcs]

    def body(*refs):
        src_refs, land_refs = refs[:n], refs[n:2 * n]
        send_sems, recv_sems = refs[2 * n], refs[2 * n + 1]
        token = refs[-1]
        me, peers = _peers()
        for k, (peer, slot) in enumerate(peers):
            for a in range(n):
                _push_copy(src_refs[a], land_refs[a], send_sems, recv_sems, a, k, me, peer, slot, scatter, False).start()
        token[...] = jnp.zeros_like(token)

    hbm = lambda a: pltpu.HBM(a.shape, a.dtype)
    outs = pl.pallas_call(
        body, name=name,
        out_shape=(pltpu.SemaphoreType.DMA((n * (N_DEV - 1),)), pltpu.SemaphoreType.DMA((n * (N_DEV - 1),)),
                   *[hbm(s) for s in srcs], *[hbm(l) for l in lands], jax.ShapeDtypeStruct((8, 128), F32)),
        in_specs=[_HBM] * (2 * n),
        out_specs=(_SEM, _SEM, *([_HBM] * (2 * n)), pl.BlockSpec(memory_space=pltpu.VMEM)),
        input_output_aliases={i: 2 + i for i in range(2 * n)},
        compiler_params=pltpu.CompilerParams(has_side_effects=_EFFECT),
    )(*[pltpu.with_memory_space_constraint(s, pltpu.HBM) for s in srcs],
      *[pltpu.with_memory_space_constraint(l, pltpu.HBM) for l in lands])
    return dict(send=outs[0], recv=outs[1], srcs=list(outs[2:2 + n]), lands=list(outs[2 + n:2 + 2 * n]),
                token=outs[-1], scatter=scatter, n=n)


def _push_wait(h, after, *, name):
    n, scatter = h["n"], h["scatter"]

    def body(*refs):
        src_refs, land_refs = refs[:n], refs[n:2 * n]
        send_sems, recv_sems = refs[2 * n], refs[2 * n + 1]
        me, peers = _peers()
        for k, (peer, slot) in enumerate(peers):
            for a in range(n):
                cp = _push_copy(src_refs[a], land_refs[a], send_sems, recv_sems, a, k, me, peer, slot, scatter, True)
                cp.wait_send()
                cp.wait_recv()

    hbm = lambda a: pltpu.HBM(a.shape, a.dtype)
    outs = pl.pallas_call(
        body, name=name,
        out_shape=(*[hbm(s) for s in h["srcs"]], *[hbm(l) for l in h["lands"]]),
        in_specs=[_HBM] * (2 * n) + [_SEM, _SEM, _ANY], out_specs=tuple([_HBM] * (2 * n)),
        input_output_aliases={i: i for i in range(2 * n)},
        compiler_params=pltpu.CompilerParams(has_side_effects=_EFFECT),
    )(*h["srcs"], *h["lands"], h["send"], h["recv"], after)
    return list(outs[:n]), list(outs[n:])


def _ffn_fwd(h, nw, w_up, conv_w, conv_b, w_down, tag):
    a3 = _mm_fwd(h, w_up, norm_w=nw, name=f"ffn{tag}_up", out_dtype=BF16, halves=True, w_t=True, tm=1024, tn=2816)
    p = _ffn_conv_fwd3(a3, conv_w, conv_b.reshape(1, -1), name=f"ffn{tag}_conv")
    h_out = _mm_fwd(p, w_down, residual=h, name=f"ffn{tag}_down", tm=1024, tn=512)
    return h_out, (a3, p)


def _ffn_bwd(dh, h, saved, nw, w_up, conv_w, conv_b, w_down, tag):
    a3, p = saved
    g_down = _mm_tn(p, dh, name=f"ffn{tag}_down_wg", tk1=1408, tn=1024)
    dp = _mm_nt(dh, w_down, name=f"ffn{tag}_down_dg", out_dtype=BF16, tm=512, tn=2816, tk=1024)
    dhid3, dw3, db3 = _ffn_conv_bwd3(a3, conv_w, conv_b.reshape(1, -1), dp, name=f"ffn{tag}_conv_bwd")
    da3 = _conv_bwd_in3(dhid3, conv_w, K=FFN_CONV, name=f"ffn{tag}_conv_bwd_in")
    g_up = _mm_tn_t(da3, h, norm_w=nw, name=f"ffn{tag}_up_wg", tn=1408, tt=1024)
    dh_out, g_nw = _mm_nt(da3, w_up, epi=(h, nw, dh), name=f"ffn{tag}_up_dg", w_t=True, tm=1024, tk=1408)
    g_cw = jnp.concatenate([dw3[0], dw3[1]], axis=1)
    g_cb = jnp.concatenate([db3[0], db3[1]], axis=1)
    return dh_out, dict(norm=g_nw.reshape(-1), up=g_up, conv_w=g_cw, conv_b=g_cb.reshape(-1), down=g_down)


_BIG = ["ssm_in_w", "ssm_out_w", "w_k", "w_v", "w_q", "w_o", "ffn_up_w", "ffn_down_w"]
_SMALL_SHARDED = ["ssm_norm_w", "ssm_conv_w", "ssm_conv_b", "ssm_gate_norm_w", "ffn_conv_w"]
_SMALL_REPL = ["ssm_dt_bias", "ssm_a_log", "ssm_d", "kv_norm_w", "attn_norm_w", "ffn_norm_w", "ffn_conv_b",
               "final_norm_w"]
_WEIGHTS = ["ssm_norm_w", "ssm_in_w", "ssm_conv_w", "ssm_conv_b", "ssm_dt_bias", "ssm_a_log", "ssm_d",
            "ssm_gate_norm_w", "ssm_out_w", "kv_norm_w", "w_k", "w_v", "attn_norm_w", "w_q", "w_o", "ffn_norm_w",
            "ffn_up_w", "ffn_conv_w", "ffn_conv_b", "ffn_down_w", "final_norm_w"]


def _as2d(a):
    return a.reshape(-1, a.shape[-1])


def _cols_to_full(g):
    return g.transpose(1, 0, 2).reshape(g.shape[1], N_DEV * g.shape[2])


def _pack_small(vals):
    flat = jnp.concatenate([v.reshape(-1).astype(F32) for v in vals])
    n = flat.shape[0]
    rows = -(-n // 1024) * 8
    return jnp.pad(flat, (0, rows * 128 - n)).reshape(rows, 128)


def _unpack_small(packed, shapes):
    flat = packed.reshape(-1)
    out, off = [], 0
    for s in shapes:
        n = math.prod(s)
        out.append(flat[off:off + n].reshape(s))
        off += n
    return out


def _tie(a, token):
    return a + token[0, 0].astype(a.dtype)


def _local_step(x, tgt, get_w, put_g):
    T = x.shape[0]
    Ws = get_w("ssm", None)
    fnw, fcw, fcb = Ws["ffn_norm_w"], Ws["ffn_conv_w"], Ws["ffn_conv_b"]
    zx = _mm_fwd(x, Ws["in_w"], norm_w=Ws["ssm_norm_w"], name="ssm_in", w_t=True, tm=1024, tn=1792)
    xbc_c = _ssm_conv_fwd(zx, Ws["ssm_conv_w"], Ws["ssm_conv_b"].reshape(1, -1), name="ssm_conv")
    dt_raw = zx[:, D_INNER + CONV_DIM:IN_PROJ_DIM]
    dtg = jnp.pad(dt_raw.reshape(T, SSM_GROUPS, 8).transpose(1, 0, 2), ((0, 0), (0, 0), (0, 120)))
    par = jnp.stack([Ws["ssm_dt_bias"].reshape(SSM_GROUPS, 8), Ws["ssm_a_log"].reshape(SSM_GROUPS, 8),
                     Ws["ssm_d"].reshape(SSM_GROUPS, 8)], axis=1)
    par = jnp.pad(par, ((0, 0), (0, 5), (0, 120)))
    gnw = _tie(Ws["ssm_gate_norm_w"].reshape(1, D_INNER), get_w("rest_start", xbc_c))
    y, yn, st = _ssd_fwd(xbc_c, zx, dtg, par, gnw, name="ssd_fwd")
    W0 = get_w("ffn0", y)
    Ws["ssm_out_w"] = W0["ssm_out_w"]
    h1 = _mm_fwd(yn, Ws["ssm_out_w"], residual=x, name="ssm_out", tm=1024, tn=512)
    h2, ffn0 = _ffn_fwd(h1, fnw[0], W0["up"], fcw[0], fcb[0], W0["down"], "0")
    Wr = get_w("rest", h2)
    q = _mm_fwd(h2, Wr["w_q"], norm_w=Ws["attn_norm_w"], out_dtype=BF16, name="attn_q", tm=1024, tn=1024)
    kv = _mm_fwd(h2, Wr["w_kv"], norm_w=Ws["kv_norm_w"], out_dtype=BF16, name="attn_kv", tm=1024, tn=1024)
    o, lt = _sba_fwd(q, kv, name="sba_fwd")
    h3 = _mm_fwd(o, Wr["w_o"], residual=h2, name="attn_o", tm=1024, tn=512)
    W1 = get_w("ffn1", h3)
    h4, ffn1 = _ffn_fwd(h3, fnw[1], W1["up"], fcw[1], fcb[1], W1["down"], "1")
    loss, dh4, g_final = _loss_head(h4, tgt, Ws["final_norm_w"], name="loss_head")
    dh3, gf1 = _ffn_bwd(dh4, h3, ffn1, fnw[1], W1["up"], fcw[1], fcb[1], W1["down"], "1")
    tok = put_g("ffn1", dict(up=gf1["up"], down=gf1["down"]))
    g_wo = _mm_tn(o, dh3, name="attn_o_wg", tn=1024)
    do = _mm_nt(dh3, _tie(Wr["w_o"], tok), name="attn_o_dg", out_dtype=BF16, tn=1024, tk=1024)
    dq, dk, dv = _sba_bwd(q, kv, lt, do, name="sba_bwd")
    g_wq = _mm_tn(h2, dq, norm_w=Ws["attn_norm_w"], name="attn_q_wg", tn=1024, tt=1024)
    dh2a, g_attn_nw = _mm_nt(dq, Wr["w_q"], epi=(h2, Ws["attn_norm_w"], dh3), name="attn_q_dg", tm=1024, tk=1024)
    dkv = jnp.concatenate([dk, dv], axis=1)
    g_wkv = _mm_tn(h2, dkv, norm_w=Ws["kv_norm_w"], name="attn_kv_wg", tn=1024, tt=1024)
    dh2, g_kv_nw = _mm_nt(dkv, Wr["w_kv"], epi=(h2, Ws["kv_norm_w"], dh2a), name="attn_kv_dg", tm=1024, tk=1024)
    tok = put_g("attn", dict(w_o=g_wo, w_q=g_wq, w_k=g_wkv[:, :D_MODEL], w_v=g_wkv[:, D_MODEL:]))
    dh1, gf0 = _ffn_bwd(dh2, h1, ffn0, fnw[0], W0["up"], fcw[0], _tie(fcb[0], tok), W0["down"], "0")
    tok = put_g("ffn0", dict(up=gf0["up"], down=gf0["down"]))
    g_out = _mm_tn(yn, dh1, name="ssm_out_wg", tn=1024)
    dyn = _mm_nt(dh1, _tie(Ws["ssm_out_w"], tok), name="ssm_out_dg", out_dtype=BF16, tn=1024, tk=1024)
    tok = put_g("ssm_out", dict(ssm_out_w=g_out))
    dxbc_c, dz, ddt, g_gnw, dpar = _ssd_bwd(xbc_c, zx, dtg, par, _tie(gnw, tok), y, st, dyn, name="ssd_bwd")
    dhid, g_scw, g_scb = _ssm_conv_bwd_pre(zx, Ws["ssm_conv_w"], Ws["ssm_conv_b"].reshape(1, -1), dxbc_c,
                                           name="ssm_conv_bwd")
    dzx = _conv_bwd_in(dhid, Ws["ssm_conv_w"], K=SSM_CONV, name="ssm_conv_bwd_in", into=(dz, D_INNER))
    ddt_t = ddt[:, :, :8].transpose(1, 0, 2).reshape(T, SSM_HEADS).astype(BF16)
    dzx = _put_cols(dzx, jnp.pad(ddt_t, ((0, 0), (0, IN_PROJ_PAD - IN_PROJ_DIM))), D_INNER + CONV_DIM, name="ssm_ddt_cols")
    g_in = _mm_tn_t(dzx, x, norm_w=Ws["ssm_norm_w"], name="ssm_in_wg", tn=1792, tt=1024)
    tok = put_g("ssm_in", dict(ssm_in_w=g_in[:IN_PROJ_DIM]))
    dx, g_ssm_nw = _mm_nt(dzx, Ws["in_w"], epi=(x, _tie(Ws["ssm_norm_w"], tok), dh1), name="ssm_in_dg", w_t=True,
                          tm=1024, tk=1792)
    f = {
        "ssm_norm_w": g_ssm_nw.reshape(-1), "ssm_conv_w": g_scw,
        "ssm_conv_b": g_scb.reshape(-1), "ssm_dt_bias": dpar[:, 0, :8].reshape(-1),
        "ssm_a_log": dpar[:, 1, :8].reshape(-1), "ssm_d": dpar[:, 2, :8].reshape(-1),
        "ssm_gate_norm_w": g_gnw.reshape(-1), "kv_norm_w": g_kv_nw.reshape(-1), "attn_norm_w": g_attn_nw.reshape(-1),
        "ffn_norm_w": jnp.stack([gf0["norm"], gf1["norm"]]), "ffn_conv_w": jnp.stack([gf0["conv_w"], gf1["conv_w"]]),
        "ffn_conv_b": jnp.stack([gf0["conv_b"], gf1["conv_b"]]), "final_norm_w": g_final.reshape(-1),
    }
    return loss, dx, f


def kernel(x, ssm_norm_w, ssm_in_w, ssm_conv_w, ssm_conv_b, ssm_dt_bias, ssm_a_log, ssm_d, ssm_gate_norm_w, ssm_out_w, kv_norm_w, w_k, w_v, attn_norm_w, w_q, w_o, ffn_norm_w, ffn_up_w, ffn_conv_w, ffn_conv_b, ffn_down_w, final_norm_w, loss_target, m_ssm_norm_w, m_ssm_in_w, m_ssm_conv_w, m_ssm_conv_b, m_ssm_dt_bias, m_ssm_a_log, m_ssm_d, m_ssm_gate_norm_w, m_ssm_out_w, m_kv_norm_w, m_w_k, m_w_v, m_attn_norm_w, m_w_q, m_w_o, m_ffn_norm_w, m_ffn_up_w, m_ffn_conv_w, m_ffn_conv_b, m_ffn_down_w, m_final_norm_w, v_ssm_norm_w, v_ssm_in_w, v_ssm_conv_w, v_ssm_conv_b, v_ssm_dt_bias, v_ssm_a_log, v_ssm_d, v_ssm_gate_norm_w, v_ssm_out_w, v_kv_norm_w, v_w_k, v_w_v, v_attn_norm_w, v_w_q, v_w_o, v_ffn_norm_w, v_ffn_up_w, v_ffn_conv_w, v_ffn_conv_b, v_ffn_down_w, v_final_norm_w):
    env = dict(locals())
    p = {n: env[n] for n in _WEIGHTS}
    mom = {n: env["m_" + n] for n in _WEIGHTS}
    var = {n: env["v_" + n] for n in _WEIGHTS}
    T = x.shape[1]
    me = 4 * lax.axis_index("x") + 2 * lax.axis_index("y") + lax.axis_index("c")
    rs = D_FF // N_DEV

    def bf2(a):
        return _as2d(a).astype(BF16)

    _T = ("ssm_in_w", "ffn_up_w")

    def t2d(a):
        return jnp.swapaxes(a, -1, -2).reshape(-1, a.shape[-2])

    def from_t2d(a, like):
        return jnp.swapaxes(a.reshape(like.shape[:-2] + (like.shape[-1], like.shape[-2])), -1, -2)

    n_in, n_up = p["ssm_in_w"].shape[-1], p["ffn_up_w"].shape[-1]

    def with_own(srcs, lands, scatter):
        out = []
        for s, l in zip(srcs, lands):
            own = lax.dynamic_index_in_dim(s, me, 0, keepdims=False) if scatter else s
            out.append(lax.dynamic_update_index_in_dim(l, own, me, 0))
        return out

    a_names = ["ssm_in_w"] + _SMALL_SHARDED
    got_a = dict(zip(a_names, _all_gather([t2d(p["ssm_in_w"]).astype(BF16)] + [_as2d(p[n]) for n in _SMALL_SHARDED],
                                          name="gather_ssm")))
    ffn0_names = ["ssm_out_w", "up0", "down0"]
    rest_names = ["w_q", "w_k", "w_v", "w_o"]
    up_t = jnp.swapaxes(p["ffn_up_w"], -1, -2).astype(BF16)
    shard = {"up0": up_t[0], "down0": bf2(p["ffn_down_w"][0]), "up1": up_t[1],
             "down1": bf2(p["ffn_down_w"][1]), "w_q": bf2(p["w_q"]), "w_k": bf2(p["w_k"]), "w_v": bf2(p["w_v"]),
             "w_o": bf2(p["w_o"]), "ssm_out_w": bf2(p["ssm_out_w"])}

    def anchored(a, on):
        return a + (jnp.where(jnp.isfinite(on), on, 0.0) * 0.0).astype(a.dtype)

    h_ffn0 = _push_start([anchored(shard[ffn0_names[0]], got_a["ssm_norm_w"][0, 0, 0])]
                         + [shard[n] for n in ffn0_names[1:]], scatter=False, name="gather_ffn0_start")
    handles = {}

    def get_w(group, after):
        if group == "ssm":
            W = {n: p[n] for n in _SMALL_REPL}
            for n in ("ssm_dt_bias", "ssm_a_log", "ssm_d", "attn_norm_w"):
                W[n] = W[n].reshape(-1)
            W["in_w"] = jnp.pad(got_a["ssm_in_w"].reshape(IN_PROJ_DIM, D_MODEL), ((0, IN_PROJ_PAD - IN_PROJ_DIM), (0, 0)))
            W["ssm_norm_w"] = _tie(got_a["ssm_norm_w"].reshape(D_MODEL), h_ffn0["token"])
            W["ssm_conv_w"] = _cols_to_full(got_a["ssm_conv_w"])
            W["ssm_conv_b"] = got_a["ssm_conv_b"].reshape(CONV_DIM)
            W["ssm_gate_norm_w"] = got_a["ssm_gate_norm_w"].reshape(D_INNER)
            W["ffn_conv_w"] = _cols_to_full(got_a["ffn_conv_w"]).reshape(2, FFN_CONV, 2 * D_FF)
            return W
        if group == "rest_start":
            handles["rest"] = _push_start([anchored(shard[rest_names[0]], after[0, 0])]
                                          + [shard[n] for n in rest_names[1:]], scatter=False, name="gather_rest_start")
            handles["ffn1"] = _push_start([anchored(shard["up1"], handles["rest"]["token"][0, 0]), shard["down1"]],
                                          scatter=False, name="gather_ffn1_start")
            return handles["ffn1"]["token"]
        if group == "ffn1":
            srcs, lands = _push_wait(handles["ffn1"], after, name="gather_ffn1_wait")
            up, down = with_own(srcs, lands, False)
            return dict(up=up.reshape(2 * D_FF, D_MODEL), down=down.reshape(D_FF, D_MODEL))
        if group == "ffn0":
            srcs, lands = _push_wait(h_ffn0, after, name="gather_ffn0_wait")
            out, up, down = with_own(srcs, lands, False)
            return dict(ssm_out_w=out.reshape(D_INNER, D_MODEL), up=up.reshape(2 * D_FF, D_MODEL),
                        down=down.reshape(D_FF, D_MODEL))
        srcs, lands = _push_wait(handles["rest"], after, name="gather_rest_wait")
        g = dict(zip(rest_names, with_own(srcs, lands, False)))
        sq = lambda a: a.reshape(D_MODEL, D_MODEL)
        return dict(w_q=sq(g["w_q"]), w_kv=jnp.concatenate([sq(g["w_k"]), sq(g["w_v"])], axis=1), w_o=sq(g["w_o"]))

    pending = []

    def put_g(group, g):
        if group in ("ffn0", "ffn1"):
            keys = [("ffn_up_w", int(group[-1])), ("ffn_down_w", int(group[-1]))]
            blocks = [g["up"].reshape(N_DEV, n_up, D_MODEL), g["down"].reshape(N_DEV, rs, D_MODEL)]
        elif group == "attn":
            keys = [(n, None) for n in ("w_o", "w_q", "w_k", "w_v")]
            blocks = [g[n].reshape(N_DEV, D_MODEL // N_DEV, D_MODEL) for n, _ in keys]
        elif group == "ssm_out":
            keys = [("ssm_out_w", None)]
            blocks = [g["ssm_out_w"].reshape(N_DEV, D_INNER // N_DEV, D_MODEL)]
        else:
            keys = [("ssm_in_w", None)]
            blocks = [g["ssm_in_w"].reshape(N_DEV, n_in, D_MODEL)]
        h = _push_start(blocks, scatter=True, name=f"exchange_{group}_start")
        pending.append((group, keys, h))
        return h["token"]

    loss_row, dx, f = _local_step(x.reshape(T, D_MODEL), loss_target.reshape(T, D_MODEL), get_w, put_g)

    small_names = _SMALL_REPL + _SMALL_SHARDED
    small_full = _pack_small([f[n] for n in small_names] + [loss_row[0, 0:1]])
    small_bcast = jnp.broadcast_to(small_full[None], (N_DEV,) + small_full.shape)
    h_small = _push_start([small_bcast], scatter=True, name="exchange_small_start")
    tok = h_small["token"]

    arrived, res = {}, {}
    after = dx
    for group, keys, h in pending:
        srcs, lands = _push_wait(h, after, name=f"exchange_{group}_wait")
        arrived.update(zip(keys, with_own(srcs, lands, True)))
        for n in _BIG:
            layered = (n, 0) in arrived or (n, 1) in arrived
            if n in res or not ((n, None) in arrived or ((n, 0) in arrived and (n, 1) in arrived)):
                continue
            parts = [arrived[(n, 0)], arrived[(n, 1)]] if layered else arrived[(n, None)]
            w2, m2, v2 = ((t2d if n in _T else _as2d)(a[n]) for a in (p, mom, var))
            if not res:
                w2 = _tie(w2, tok)
            tiles = {"ffn_down_w": dict(tr=rs), "ffn_up_w": dict(tr=n_up // 2), "ssm_in_w": dict(tr=n_in, tc=256)}
            res[n] = _adamw(parts, w2, m2, v2, name=f"adamw_{n}", **tiles.get(n, dict(tr=256)))
            after = res[n][0]
    srcs, lands = _push_wait(h_small, after, name="exchange_small_wait")
    small_parts = with_own(srcs, lands, True)[0]
    out_g, out_d, out_m, out_v = {}, {}, {}, {}
    for n in _BIG:
        out_g[n], out_d[n], out_m[n], out_v[n] = (from_t2d(t, p[n]) if n in _T else t.reshape(p[n].shape) for t in res[n])

    zero = jnp.zeros_like(small_full)
    g_small_sum = _adamw(small_parts, zero, zero, zero, name="sum_small_grads", tr=small_full.shape[0])[0]
    *small_sums, loss_sum = _unpack_small(g_small_sum, [f[n].shape for n in small_names] + [(1,)])
    loss = loss_sum[0]
    g_small = dict(zip(small_names, small_sums))
    for n in _SMALL_SHARDED:
        width = p[n].shape[-1]
        g_small[n] = lax.dynamic_slice_in_dim(g_small[n], me * width, width, axis=g_small[n].ndim - 1)
    sw = _pack_small([p[n] for n in small_names])
    sm = _pack_small([mom[n] for n in small_names])
    sv = _pack_small([var[n] for n in small_names])
    sg = _pack_small([g_small[n] for n in small_names])
    _, d, nm, nv = _adamw(sg[None], sw, sm, sv, name="adamw_small", tr=sw.shape[0])
    shard_shapes = [p[n].shape for n in small_names]
    for n, dd, mm, vv in zip(small_names, _unpack_small(d, shard_shapes), _unpack_small(nm, shard_shapes),
                             _unpack_small(nv, shard_shapes)):
        out_g[n] = g_small[n].reshape(p[n].shape)
        out_d[n], out_m[n], out_v[n] = dd, mm, vv

    return (loss, dx.reshape(x.shape), *[out_g[n] for n in _WEIGHTS], *[out_d[n] for n in _WEIGHTS],
            *[out_m[n] for n in _WEIGHTS], *[out_v[n] for n in _WEIGHTS])
```

```python
import functools
import math

import jax
import jax.numpy as jnp
from jax import lax
from jax.experimental import pallas as pl
from jax.experimental.pallas import tpu as pltpu

F32 = jnp.float32
BF16 = jnp.bfloat16
EPS = 1e-6

D_MODEL = 1024
D_INNER = 2048
SSM_HEADS = 32
SSM_GROUPS = 4
SSM_STATE = 128
SSM_CONV = 4
SSM_CHUNK = 128
GN = SSM_GROUPS * SSM_STATE
CONV_DIM = D_INNER + 2 * GN
IN_PROJ_DIM = D_INNER + CONV_DIM + SSM_HEADS
IN_PROJ_PAD = 5376
SB_HEADS = 16
SB_HEAD_DIM = 64
SB_BLOCK = 128
D_FF = 2816
FFN_CONV = 3
N_DEV = 8

ADAM_LR = 0.001
ADAM_B1 = 0.9
ADAM_B2 = 0.999
ADAM_EPS = 1e-08
ADAM_WD = 0.01
ADAM_STEP = 10

_MESH = pl.DeviceIdType.MESH
_NT = (((1,), (1,)), ((), ()))
_TN = (((0,), (0,)), ((), ()))
_ANY = pl.BlockSpec(memory_space=pl.ANY)


def _cparams(sem, vmem_mb=48):
    return pltpu.CompilerParams(dimension_semantics=sem, vmem_limit_bytes=vmem_mb * 1024 * 1024)


def _sigmoid(x):
    return 0.5 * jnp.tanh(0.5 * x) + 0.5


def _softplus(x):
    return jnp.maximum(x, 0.0) + jnp.log(1.0 + jnp.exp(-jnp.abs(x)))


def _rms_fwd(xv, w):
    r = lax.rsqrt(jnp.mean(xv * xv, axis=-1, keepdims=True) + EPS)
    return xv * r * w


def _mm_fwd(x, w, *, name, norm_w=None, residual=None, out_dtype=F32, tm=512, tn=512, halves=False, w_t=False):
    M, K = x.shape
    N = w.shape[0] if w_t else w.shape[1]
    tm, tn = min(tm, M), min(tn, N)
    assert M % tm == 0 and N % tn == 0, (name, M, N, tm, tn)
    if halves:
        nbh = N // 2 // tn
        assert N // 2 % tn == 0
        out_spec = pl.BlockSpec((None, tm, tn), lambda i, j: (lax.div(j, nbh), i, lax.rem(j, nbh)))
        out_shape = jax.ShapeDtypeStruct((2, M, N // 2), out_dtype)
    else:
        out_spec = pl.BlockSpec((tm, tn), lambda i, j: (i, j))
        out_shape = jax.ShapeDtypeStruct((M, N), out_dtype)
    has_norm, has_res = norm_w is not None, residual is not None

    def body(*refs):
        x_ref, w_ref = refs[0], refs[1]
        p = 2
        nw_ref = r_ref = None
        if has_norm:
            nw_ref = refs[p]
            p += 1
        if has_res:
            r_ref = refs[p]
            p += 1
        o_ref = refs[p]
        xv = x_ref[...]
        if has_norm:
            xv = _rms_fwd(xv.astype(F32), nw_ref[...])
        acc = lax.dot_general(xv.astype(BF16), w_ref[...], _NT if w_t else (((1,), (0,)), ((), ())),
                              preferred_element_type=F32)
        if has_res:
            acc = acc + r_ref[...]
        o_ref[...] = acc.astype(out_dtype)

    w_spec = pl.BlockSpec((tn, K), lambda i, j: (j, 0)) if w_t else pl.BlockSpec((K, tn), lambda i, j: (0, j))
    in_specs = [pl.BlockSpec((tm, K), lambda i, j: (i, 0)), w_spec]
    args = [x, w]
    if has_norm:
        in_specs.append(pl.BlockSpec((1, K), lambda i, j: (0, 0)))
        args.append(norm_w.reshape(1, K))
    if has_res:
        in_specs.append(pl.BlockSpec((tm, tn), lambda i, j: (i, j)))
        args.append(residual)
    return pl.pallas_call(
        body, name=name, grid=(M // tm, N // tn), in_specs=in_specs,
        out_specs=out_spec, out_shape=out_shape,
        compiler_params=_cparams(("parallel", "parallel")))(*args)


def _mm_nt(dy, w, *, name, epi=None, out_dtype=F32, tm=512, tn=512, tk=512, w_t=False):
    halves = dy.ndim == 3
    M, K = (dy.shape[1], 2 * dy.shape[2]) if halves else dy.shape
    N = w.shape[1] if w_t else w.shape[0]
    tm, tk = min(tm, M), min(tk, K)
    tn = N if epi is not None else min(tn, N)
    assert M % tm == 0 and N % tn == 0 and K % tk == 0, (name, M, N, K, tm, tn, tk)
    nk = K // tk
    has_epi = epi is not None

    def body(*refs):
        if has_epi:
            dy_ref, w_ref, h_ref, nw_ref, r_ref, o_ref, dnw_ref, acc_ref = refs
        else:
            dy_ref, w_ref, o_ref, acc_ref = refs
        i = pl.program_id(0)
        k = pl.program_id(2)

        @pl.when(k == 0)
        def _():
            acc_ref[...] = jnp.zeros_like(acc_ref)

        acc_ref[...] += lax.dot_general(dy_ref[...].astype(BF16), w_ref[...], (((1,), (0,)), ((), ())) if w_t else _NT,
                                        preferred_element_type=F32)

        @pl.when(k == nk - 1)
        def _():
            du = acc_ref[...]
            if has_epi:
                hv = h_ref[...]
                r = lax.rsqrt(jnp.mean(hv * hv, axis=-1, keepdims=True) + EPS)
                xhat = hv * r
                dxh = du * nw_ref[...]
                dx = r * (dxh - xhat * jnp.mean(dxh * xhat, axis=-1, keepdims=True))
                o_ref[...] = (r_ref[...] + dx).astype(out_dtype)
                contrib = jnp.sum(du * xhat, axis=0, keepdims=True)

                @pl.when(i == 0)
                def _():
                    dnw_ref[...] = contrib

                @pl.when(i > 0)
                def _():
                    dnw_ref[...] += contrib
            else:
                o_ref[...] = du.astype(out_dtype)

    if halves:
        nkh = K // 2 // tk
        assert K // 2 % tk == 0
        dy_spec = pl.BlockSpec((None, tm, tk), lambda i, j, k: (lax.div(k, nkh), i, lax.rem(k, nkh)))
    else:
        dy_spec = pl.BlockSpec((tm, tk), lambda i, j, k: (i, k))
    w_spec = pl.BlockSpec((tk, tn), lambda i, j, k: (k, j)) if w_t else pl.BlockSpec((tn, tk), lambda i, j, k: (j, k))
    in_specs = [dy_spec, w_spec]
    args = [dy, w]
    out_specs = [pl.BlockSpec((tm, tn), lambda i, j, k: (i, j))]
    out_shape = [jax.ShapeDtypeStruct((M, N), out_dtype)]
    if has_epi:
        h, nw, res = epi
        in_specs += [pl.BlockSpec((tm, N), lambda i, j, k: (i, 0)), pl.BlockSpec((1, N), lambda i, j, k: (0, 0)),
                     pl.BlockSpec((tm, N), lambda i, j, k: (i, 0))]
        args += [h, nw.reshape(1, N), res]
        out_specs.append(pl.BlockSpec((1, N), lambda i, j, k: (0, 0)))
        out_shape.append(jax.ShapeDtypeStruct((1, N), F32))
    outs = pl.pallas_call(
        body, name=name, grid=(M // tm, N // tn, nk), in_specs=in_specs, out_specs=out_specs, out_shape=out_shape,
        scratch_shapes=[pltpu.VMEM((tm, tn), F32)],
        compiler_params=_cparams(("arbitrary", "arbitrary", "arbitrary")))(*args)
    return (outs[0], outs[1]) if has_epi else outs[0]


def _mm_tn(x, dy, *, name, norm_w=None, out_dtype=BF16, tk1=1024, tn=512, tt=512):
    T, K1 = x.shape
    halves = dy.ndim == 3
    N = 2 * dy.shape[2] if halves else dy.shape[1]
    tk1, tn, tt = min(tk1, K1), min(tn, N), min(tt, T)
    has_norm = norm_w is not None
    assert K1 % tk1 == 0 and N % tn == 0 and T % tt == 0, (name, K1, N, T, tk1, tn, tt)
    assert not has_norm or tk1 == K1
    nt = T // tt

    def body(*refs):
        if has_norm:
            x_ref, dy_ref, nw_ref, o_ref, acc_ref = refs
        else:
            x_ref, dy_ref, o_ref, acc_ref = refs
        t = pl.program_id(2)

        @pl.when(t == 0)
        def _():
            acc_ref[...] = jnp.zeros_like(acc_ref)

        xv = x_ref[...]
        if has_norm:
            xv = _rms_fwd(xv.astype(F32), nw_ref[...])
        acc_ref[...] += lax.dot_general(xv.astype(BF16), dy_ref[...].astype(BF16), _TN, preferred_element_type=F32)

        @pl.when(t == nt - 1)
        def _():
            o_ref[...] = acc_ref[...].astype(out_dtype)

    if halves:
        nbh = N // 2 // tn
        assert N // 2 % tn == 0
        dy_spec = pl.BlockSpec((None, tt, tn), lambda a, b, t: (lax.div(b, nbh), t, lax.rem(b, nbh)))
    else:
        dy_spec = pl.BlockSpec((tt, tn), lambda a, b, t: (t, b))
    in_specs = [pl.BlockSpec((tt, tk1), lambda a, b, t: (t, a)), dy_spec]
    args = [x, dy]
    if has_norm:
        in_specs.append(pl.BlockSpec((1, K1), lambda a, b, t: (0, 0)))
        args.append(norm_w.reshape(1, K1))
    return pl.pallas_call(
        body, name=name, grid=(K1 // tk1, N // tn, nt), in_specs=in_specs,
        out_specs=pl.BlockSpec((tk1, tn), lambda a, b, t: (a, b)),
        out_shape=jax.ShapeDtypeStruct((K1, N), out_dtype),
        scratch_shapes=[pltpu.VMEM((tk1, tn), F32)],
        compiler_params=_cparams(("parallel", "parallel", "arbitrary")))(*args)


def _mm_tn_t(dy, x, *, name, norm_w, out_dtype=BF16, tn=1408, tt=1024):
    T, K1 = x.shape
    halves = dy.ndim == 3
    N = 2 * dy.shape[2] if halves else dy.shape[1]
    tn, tt = min(tn, N), min(tt, T)
    assert N % tn == 0 and T % tt == 0, (name, N, T, tn, tt)
    nt = T // tt

    def body(dy_ref, x_ref, nw_ref, o_ref, acc_ref):
        t = pl.program_id(1)

        @pl.when(t == 0)
        def _():
            acc_ref[...] = jnp.zeros_like(acc_ref)

        xn = _rms_fwd(x_ref[...].astype(F32), nw_ref[...]).astype(BF16)
        acc_ref[...] += lax.dot_general(dy_ref[...].astype(BF16), xn, _TN, preferred_element_type=F32)

        @pl.when(t == nt - 1)
        def _():
            o_ref[...] = acc_ref[...].astype(out_dtype)

    if halves:
        nbh = N // 2 // tn
        assert N // 2 % tn == 0
        dy_spec = pl.BlockSpec((None, tt, tn), lambda b, t: (lax.div(b, nbh), t, lax.rem(b, nbh)))
    else:
        dy_spec = pl.BlockSpec((tt, tn), lambda b, t: (t, b))
    return pl.pallas_call(
        body, name=name, grid=(N // tn, nt),
        in_specs=[dy_spec, pl.BlockSpec((tt, K1), lambda b, t: (t, 0)), pl.BlockSpec((1, K1), lambda b, t: (0, 0))],
        out_specs=pl.BlockSpec((tn, K1), lambda b, t: (b, 0)),
        out_shape=jax.ShapeDtypeStruct((N, K1), out_dtype),
        scratch_shapes=[pltpu.VMEM((tn, K1), F32)],
        compiler_params=_cparams(("parallel", "arbitrary")))(dy, x, norm_w.reshape(1, K1))


def _shift_down(xb, prev8, j):
    main = pltpu.roll(xb, j, 0)
    head = pltpu.roll(xb[0:8], j, 0)
    ph = pltpu.roll(prev8, j, 0)
    row8 = lax.broadcasted_iota(jnp.int32, head.shape, 0)
    head = jnp.where(row8 < j, ph, head)
    return jnp.concatenate([head, main[8:]], axis=0)


def _shift_up(xb, next8, j):
    tt = xb.shape[0]
    main = pltpu.roll(xb, tt - j, 0)
    tail = pltpu.roll(xb[tt - 8:tt], 8 - j, 0)
    nh = pltpu.roll(next8, 8 - j, 0)
    row8 = lax.broadcasted_iota(jnp.int32, tail.shape, 0)
    tail = jnp.where(row8 + j >= 8, nh, tail)
    return jnp.concatenate([main[:tt - 8], tail], axis=0)


def _conv_hid(xb, prev8, w, b_row, K):
    out = b_row
    shifted = []
    for j in range(K):
        sh = K - 1 - j
        xs = xb if sh == 0 else _shift_down(xb, prev8, sh)
        shifted.append(xs)
        out = out + xs * w[j:j + 1, :]
    return out, shifted


def _prev_idx(i, nb8):
    return jnp.maximum(i * nb8 - 1, 0)


def _ssm_conv_fwd(zx, w, b, *, name, tt=512, tc=512):
    T = zx.shape[0]
    tt = min(tt, T)
    C, K = CONV_DIM, SSM_CONV
    cb0, nb8 = D_INNER // tc, tt // 8

    def body(x_ref, p_ref, w_ref, b_ref, o_ref):
        first = (pl.program_id(1) > 0).astype(F32)
        hid, _ = _conv_hid(x_ref[...], p_ref[...] * first, w_ref[...], b_ref[...], K)
        o_ref[...] = hid * _sigmoid(hid)

    return pl.pallas_call(
        body, name=name, grid=(C // tc, T // tt),
        in_specs=[pl.BlockSpec((tt, tc), lambda c, i: (i, c + cb0)),
                  pl.BlockSpec((8, tc), lambda c, i: (_prev_idx(i, nb8), c + cb0)),
                  pl.BlockSpec((K, tc), lambda c, i: (0, c)), pl.BlockSpec((1, tc), lambda c, i: (0, c))],
        out_specs=pl.BlockSpec((tt, tc), lambda c, i: (i, c)),
        out_shape=jax.ShapeDtypeStruct((T, C), F32),
        compiler_params=_cparams(("parallel", "parallel")))(zx, zx, w, b)


def _ssm_conv_bwd_pre(zx, w, b, dout, *, name, tt=512, tc=512):
    T = zx.shape[0]
    tt = min(tt, T)
    C, K = CONV_DIM, SSM_CONV
    cb0, nb8 = D_INNER // tc, tt // 8

    def body(x_ref, p_ref, w_ref, b_ref, d_ref, dh_ref, dw_ref, db_ref):
        t = pl.program_id(1)
        first = (t > 0).astype(F32)
        hid, shifted = _conv_hid(x_ref[...], p_ref[...] * first, w_ref[...], b_ref[...], K)
        sg = _sigmoid(hid)
        dh = d_ref[...] * (sg * (1.0 + hid * (1.0 - sg)))
        dh_ref[...] = dh

        @pl.when(t == 0)
        def _():
            dw_ref[...] = jnp.zeros_like(dw_ref)
            db_ref[...] = jnp.zeros_like(db_ref)

        db_ref[...] += jnp.sum(dh, axis=0, keepdims=True)
        for j in range(K):
            dw_ref[j:j + 1, :] += jnp.sum(dh * shifted[j], axis=0, keepdims=True)

    return pl.pallas_call(
        body, name=name, grid=(C // tc, T // tt),
        in_specs=[pl.BlockSpec((tt, tc), lambda c, i: (i, c + cb0)),
                  pl.BlockSpec((8, tc), lambda c, i: (_prev_idx(i, nb8), c + cb0)),
                  pl.BlockSpec((K, tc), lambda c, i: (0, c)), pl.BlockSpec((1, tc), lambda c, i: (0, c)),
                  pl.BlockSpec((tt, tc), lambda c, i: (i, c))],
        out_specs=[pl.BlockSpec((tt, tc), lambda c, i: (i, c)), pl.BlockSpec((K, tc), lambda c, i: (0, c)),
                   pl.BlockSpec((1, tc), lambda c, i: (0, c))],
        out_shape=[jax.ShapeDtypeStruct((T, C), F32), jax.ShapeDtypeStruct((K, C), F32),
                   jax.ShapeDtypeStruct((1, C), F32)],
        compiler_params=_cparams(("parallel", "arbitrary")))(zx, zx, w, b, dout)


def _put_cols(buf, src, col0, *, name, tt=512):
    T, C = src.shape
    tt = min(tt, T)

    def body(s_ref, _, o_ref):
        o_ref[...] = s_ref[...]

    return pl.pallas_call(
        body, name=name, grid=(T // tt,),
        in_specs=[pl.BlockSpec((tt, C), lambda i: (i, 0)), _ANY],
        out_specs=pl.BlockSpec((tt, C), lambda i: (i, col0 // C)),
        out_shape=jax.ShapeDtypeStruct(buf.shape, buf.dtype), input_output_aliases={1: 0},
        compiler_params=_cparams(("parallel",)))(src, buf)


def _conv_bwd_in(dh, w, *, name, K, tt=512, tc=512, out_dtype=BF16, into=None):
    T, C = dh.shape
    tt = min(tt, T)
    nb8, nT = tt // 8, T // tt
    last8 = T // 8 - 1
    cb0 = 0 if into is None else into[1] // tc

    def body(d_ref, n_ref, w_ref, *rest):
        o_ref = rest[-1]
        notlast = (pl.program_id(1) < nT - 1).astype(F32)
        d = d_ref[...]
        nxt = n_ref[...] * notlast
        w_ = w_ref[...]
        acc = d * w_[K - 1:K, :]
        for sh in range(1, K):
            acc = acc + _shift_up(d, nxt, sh) * w_[K - 1 - sh:K - sh, :]
        o_ref[...] = acc.astype(out_dtype)

    in_specs = [pl.BlockSpec((tt, tc), lambda c, i: (i, c)),
                pl.BlockSpec((8, tc), lambda c, i: (jnp.minimum((i + 1) * nb8, last8), c)),
                pl.BlockSpec((K, tc), lambda c, i: (0, c))]
    args = [dh, dh, w]
    if into is None:
        out_shape, alias = jax.ShapeDtypeStruct((T, C), out_dtype), {}
    else:
        assert into[0].dtype == out_dtype and into[1] % tc == 0
        in_specs.append(_ANY)
        args.append(into[0])
        out_shape, alias = jax.ShapeDtypeStruct(into[0].shape, out_dtype), {3: 0}
    return pl.pallas_call(
        body, name=name, grid=(C // tc, nT), in_specs=in_specs,
        out_specs=pl.BlockSpec((tt, tc), lambda c, i: (i, c + cb0)),
        out_shape=out_shape, input_output_aliases=alias,
        compiler_params=_cparams(("parallel", "parallel")))(*args)


def _ffn_conv_fwd3(a3, w, b, *, name, tt=256, tc=1408):
    T = a3.shape[1]
    tt = min(tt, T)
    K, nbh, n16 = FFN_CONV, D_FF // tc, tt // 16

    def body(a_ref, p_ref, wg_ref, wv_ref, bg_ref, bv_ref, o_ref):
        first = (pl.program_id(1) > 0).astype(F32)
        a = a_ref[...].astype(F32)
        prev = p_ref[...].astype(F32)[:, 8:16, :] * first
        hg, _ = _conv_hid(a[0], prev[0], wg_ref[...], bg_ref[...], K)
        hv, _ = _conv_hid(a[1], prev[1], wv_ref[...], bv_ref[...], K)
        o_ref[...] = (hg * _sigmoid(hg) * hv).astype(BF16)

    return pl.pallas_call(
        body, name=name, grid=(nbh, T // tt),
        in_specs=[pl.BlockSpec((2, tt, tc), lambda c, i: (0, i, c)),
                  pl.BlockSpec((2, 16, tc), lambda c, i: (0, _prev_idx(i, n16), c)),
                  pl.BlockSpec((K, tc), lambda c, i: (0, c)), pl.BlockSpec((K, tc), lambda c, i: (0, c + nbh)),
                  pl.BlockSpec((1, tc), lambda c, i: (0, c)), pl.BlockSpec((1, tc), lambda c, i: (0, c + nbh))],
        out_specs=pl.BlockSpec((tt, tc), lambda c, i: (i, c)),
        out_shape=jax.ShapeDtypeStruct((T, D_FF), BF16),
        compiler_params=_cparams(("parallel", "parallel")))(a3, a3, w, w, b, b)


def _ffn_conv_bwd3(a3, w, b, dp, *, name, tt=256, tc=1408):
    T = a3.shape[1]
    tt = min(tt, T)
    K, nbh, n16 = FFN_CONV, D_FF // tc, tt // 16

    def body(a_ref, p_ref, wg_ref, wv_ref, bg_ref, bv_ref, dp_ref, dh_ref, dw_ref, db_ref):
        t = pl.program_id(1)
        first = (t > 0).astype(F32)
        a = a_ref[...].astype(F32)
        prev = p_ref[...].astype(F32)[:, 8:16, :] * first
        hg, sh_g = _conv_hid(a[0], prev[0], wg_ref[...], bg_ref[...], K)
        hv, sh_v = _conv_hid(a[1], prev[1], wv_ref[...], bv_ref[...], K)
        sg = _sigmoid(hg)
        d = dp_ref[...].astype(F32)
        dhg = d * hv * (sg * (1.0 + hg * (1.0 - sg)))
        dhv = d * (hg * sg)
        dh_ref[0] = dhg.astype(BF16)
        dh_ref[1] = dhv.astype(BF16)

        @pl.when(t == 0)
        def _():
            dw_ref[...] = jnp.zeros_like(dw_ref)
            db_ref[...] = jnp.zeros_like(db_ref)

        db_ref[0] += jnp.sum(dhg, axis=0, keepdims=True)
        db_ref[1] += jnp.sum(dhv, axis=0, keepdims=True)
        for j in range(K):
            dw_ref[0, j:j + 1, :] += jnp.sum(dhg * sh_g[j], axis=0, keepdims=True)
            dw_ref[1, j:j + 1, :] += jnp.sum(dhv * sh_v[j], axis=0, keepdims=True)

    return pl.pallas_call(
        body, name=name, grid=(nbh, T // tt),
        in_specs=[pl.BlockSpec((2, tt, tc), lambda c, i: (0, i, c)),
                  pl.BlockSpec((2, 16, tc), lambda c, i: (0, _prev_idx(i, n16), c)),
                  pl.BlockSpec((K, tc), lambda c, i: (0, c)), pl.BlockSpec((K, tc), lambda c, i: (0, c + nbh)),
                  pl.BlockSpec((1, tc), lambda c, i: (0, c)), pl.BlockSpec((1, tc), lambda c, i: (0, c + nbh)),
                  pl.BlockSpec((tt, tc), lambda c, i: (i, c))],
        out_specs=[pl.BlockSpec((2, tt, tc), lambda c, i: (0, i, c)), pl.BlockSpec((2, K, tc), lambda c, i: (0, 0, c)),
                   pl.BlockSpec((2, 1, tc), lambda c, i: (0, 0, c))],
        out_shape=[jax.ShapeDtypeStruct((2, T, D_FF), BF16), jax.ShapeDtypeStruct((2, K, D_FF), F32),
                   jax.ShapeDtypeStruct((2, 1, D_FF), F32)],
        compiler_params=_cparams(("parallel", "arbitrary")))(a3, a3, w, w, b, b, dp)


def _conv_bwd_in3(dh3, w, *, name, K, tt=256, tc=1408):
    H, T, C = dh3.shape
    tt = min(tt, T)
    nb, n16, nT = C // tc, tt // 16, T // tt
    last16 = T // 16 - 1

    def body(d_ref, n_ref, w_ref, o_ref):
        notlast = (pl.program_id(2) < nT - 1).astype(F32)
        d = d_ref[...].astype(F32)
        nxt = n_ref[...].astype(F32)[0:8, :] * notlast
        w_ = w_ref[...]
        acc = d * w_[K - 1:K, :]
        for sh in range(1, K):
            acc = acc + _shift_up(d, nxt, sh) * w_[K - 1 - sh:K - sh, :]
        o_ref[...] = acc.astype(BF16)

    return pl.pallas_call(
        body, name=name, grid=(H, nb, nT),
        in_specs=[pl.BlockSpec((None, tt, tc), lambda h, c, i: (h, i, c)),
                  pl.BlockSpec((None, 16, tc), lambda h, c, i: (h, jnp.minimum((i + 1) * n16, last16), c)),
                  pl.BlockSpec((K, tc), lambda h, c, i: (0, h * nb + c))],
        out_specs=pl.BlockSpec((None, tt, tc), lambda h, c, i: (h, i, c)),
        out_shape=jax.ShapeDtypeStruct((H, T, C), BF16),
        compiler_params=_cparams(("parallel", "parallel", "parallel")))(dh3, dh3, w)


def _cumsum_rows(x):
    L = x.shape[0]
    row = lax.broadcasted_iota(jnp.int32, x.shape, 0)
    k = 1
    while k < L:
        x = x + jnp.where(row >= k, pltpu.roll(x, k, 0), 0.0)
        k *= 2
    return x


def _rcumsum_rows(x):
    L = x.shape[0]
    row = lax.broadcasted_iota(jnp.int32, x.shape, 0)
    k = 1
    while k < L:
        x = x + jnp.where(row < L - k, pltpu.roll(x, L - k, 0), 0.0)
        k *= 2
    return x


def _split_terms(m, n):
    terms, rest = [], m
    for _ in range(n):
        t = rest.astype(BF16)
        terms.append(t)
        rest = rest - t.astype(F32)
    return jnp.concatenate(terms, axis=1)


def _select_dot(m, n_terms, n_out, cond):
    K = m.shape[1]
    k = lax.broadcasted_iota(jnp.int32, (K, n_out), 0)
    j = lax.broadcasted_iota(jnp.int32, (K, n_out), 1)
    sel = cond(k, j).astype(BF16)
    return jnp.dot(_split_terms(m, n_terms), jnp.concatenate([sel] * n_terms, axis=0), preferred_element_type=F32)


def _rowsum_mxu(m):
    return _select_dot(m, 2, 128, lambda k, j: k >= 0)


def _lane_block_sums(m, width):
    shift = width.bit_length() - 1
    return _select_dot(m, 2, 128, lambda k, j: j == jnp.right_shift(k, shift))


def _heads_to_pairs(m):
    return _select_dot(m, 3, 512, lambda k, j: k == jnp.right_shift(j, 6))


def _ssd_common(dt_ref, par_ref):
    par = par_ref[...]
    raw = dt_ref[...] + par[0:1, :]
    dt = _softplus(raw)
    a = -jnp.exp(par[1:2, :])
    cs = _cumsum_rows(dt * a)
    L = cs.shape[0]
    cs_last = cs[L - 1:L, :]
    return raw, dt, a, par[2:3, :], cs, cs.T, jnp.exp(cs), jnp.exp(cs_last - cs), jnp.exp(cs_last)


def _ssd_specs(nc, rev):
    L = SSM_CHUNK

    def ci(c):
        return nc - 1 - c if rev else c

    return [pl.BlockSpec((L, D_INNER), lambda c: (ci(c), 0)),
            pl.BlockSpec((L, GN), lambda c: (ci(c), D_INNER // GN)),
            pl.BlockSpec((L, GN), lambda c: (ci(c), D_INNER // GN + 1)),
            pl.BlockSpec((SSM_GROUPS, L, 128), lambda c: (0, ci(c), 0)),
            pl.BlockSpec((SSM_GROUPS, 8, 128), lambda c: (0, 0, 0)),
            pl.BlockSpec((L, D_INNER), lambda c: (ci(c), 0)),
            pl.BlockSpec((1, D_INNER), lambda c: (0, 0))], ci


def _round_robin(gens):
    live = list(gens)
    while live:
        nxt = []
        for gen in live:
            try:
                next(gen)
                nxt.append(gen)
            except StopIteration:
                pass
        live = nxt


def _group_views(g, wide, narrow, lead):
    return ([r.at[:, g * 512:(g + 1) * 512] for r in wide], [r.at[:, g * 128:(g + 1) * 128] for r in narrow],
            [r.at[g] for r in lead])


def _ssd_fwd(xbc_c, zx, dtg, par, gnw, *, name):
    T = xbc_c.shape[0]
    L = SSM_CHUNK
    nc = T // L
    in_specs, ci = _ssd_specs(nc, False)

    def body(xs_ref, b_ref, c_ref, dt_ref, par_ref, z_ref, gnw_ref, y_ref, yn_ref, st_ref, h_ref):
        @pl.when(pl.program_id(0) == 0)
        def _():
            h_ref[...] = jnp.zeros_like(h_ref)

        gens = []
        for g in range(SSM_GROUPS):
            (xs, z, gw, y, yn), (b, c), (dt, pr, st, h) = _group_views(
                g, [xs_ref, z_ref, gnw_ref, y_ref, yn_ref], [b_ref, c_ref], [dt_ref, par_ref, st_ref, h_ref])
            gens.append(group(xs, b, c, dt, pr, z, gw, y, yn, st, h))
        _round_robin(gens)

    def group(xs_ref, b_ref, c_ref, dt_ref, par_ref, z_ref, gnw_ref, y_ref, yn_ref, st_ref, h_ref):
        _, dt, _, dsk, cs, csT, ecs, eend, dec = _ssd_common(dt_ref, par_ref)
        Bb = b_ref[...].astype(BF16)
        Cb = c_ref[...].astype(BF16)
        G = lax.dot_general(Cb, Bb, _NT, preferred_element_type=F32)
        row = lax.broadcasted_iota(jnp.int32, (L, L), 0)
        col = lax.broadcasted_iota(jnp.int32, (L, L), 1)
        tril = col <= row
        lo = lax.broadcasted_iota(jnp.int32, (L, 128), 1) < 64
        lo1 = lax.broadcasted_iota(jnp.int32, (1, 128), 1) < 64
        dt_x, ecs_x, eend_x = (_heads_to_pairs(m) for m in (dt, ecs, eend))
        for pp in range(4):
            hA, hB = 2 * pp, 2 * pp + 1
            lanes = slice(pp * 128, (pp + 1) * 128)

            def sel1(m):
                return jnp.where(lo1, m[:, hA:hA + 1], m[:, hB:hB + 1])

            X = xs_ref[:, lanes]
            xd = X * dt_x[:, lanes]
            xdb = xd.astype(BF16)
            ys = []
            for h in (hA, hB):
                Lm = jnp.where(tril, jnp.exp(jnp.minimum(cs[:, h:h + 1] - csT[h:h + 1, :], 0.0)), 0.0)
                ys.append(jnp.dot((G * Lm).astype(BF16), xdb, preferred_element_type=F32))
                yield
            Hp = h_ref[pp]
            st_ref[pp] = Hp
            yoff = jnp.dot(Cb, Hp.astype(BF16), preferred_element_type=F32) * ecs_x[:, lanes]
            y_ref[:, lanes] = jnp.where(lo, ys[0], ys[1]) + yoff + sel1(dsk) * X
            S = lax.dot_general(Bb, (xd * eend_x[:, lanes]).astype(BF16), _TN, preferred_element_type=F32)
            h_ref[pp] = Hp * sel1(dec) + S
            yield
        zv = z_ref[...]
        yg = y_ref[...] * (zv * _sigmoid(zv))
        r = jnp.tile(lax.rsqrt(_rowsum_mxu(yg * yg) * (1.0 / 512) + EPS), (1, 4))
        yn_ref[...] = (yg * r * gnw_ref[...]).astype(BF16)

    return pl.pallas_call(
        body, name=name, grid=(nc,), in_specs=in_specs,
        out_specs=[pl.BlockSpec((L, D_INNER), lambda c: (c, 0)), pl.BlockSpec((L, D_INNER), lambda c: (c, 0)),
                   pl.BlockSpec((SSM_GROUPS, None, 4, 128, 128), lambda c: (0, c, 0, 0, 0))],
        out_shape=[jax.ShapeDtypeStruct((T, D_INNER), F32), jax.ShapeDtypeStruct((T, D_INNER), BF16),
                   jax.ShapeDtypeStruct((SSM_GROUPS, nc, 4, 128, 128), F32)],
        scratch_shapes=[pltpu.VMEM((SSM_GROUPS, 4, 128, 128), F32)],
        compiler_params=_cparams(("arbitrary",)))(xbc_c, xbc_c, xbc_c, dtg, par, zx, gnw)


def _ssd_bwd(xbc_c, zx, dtg, par, gnw, y, st, dyn, *, name):
    T = xbc_c.shape[0]
    L = SSM_CHUNK
    nc = T // L
    in_specs, ci = _ssd_specs(nc, True)
    in_specs += [pl.BlockSpec((L, D_INNER), lambda c: (ci(c), 0)),
                 pl.BlockSpec((SSM_GROUPS, None, 4, 128, 128), lambda c: (0, ci(c), 0, 0, 0)),
                 pl.BlockSpec((L, D_INNER), lambda c: (ci(c), 0))]

    def body(xs_ref, b_ref, c_ref, dt_ref, par_ref, z_ref, gnw_ref, y_ref, st_ref, dyn_ref,
             dxbc_ref, dz_ref, ddt_ref, dgnw_ref, dpar_ref, dh_ref):
        @pl.when(pl.program_id(0) == 0)
        def _():
            dh_ref[...] = jnp.zeros_like(dh_ref)
            dgnw_ref[...] = jnp.zeros_like(dgnw_ref)
            dpar_ref[...] = jnp.zeros_like(dpar_ref)

        dxs_ref = dxbc_ref.at[:, 0:D_INNER]
        db_ref = dxbc_ref.at[:, D_INNER:D_INNER + GN]
        dc_ref = dxbc_ref.at[:, D_INNER + GN:CONV_DIM]

        gens = []
        for g in range(SSM_GROUPS):
            (xs, z, gw, y, dyn, dxs, dz, dgw), (b, c, db, dc), (dt, pr, st, ddt, dpr, dh) = _group_views(
                g, [xs_ref, z_ref, gnw_ref, y_ref, dyn_ref, dxs_ref, dz_ref, dgnw_ref], [b_ref, c_ref, db_ref, dc_ref],
                [dt_ref, par_ref, st_ref, ddt_ref, dpar_ref, dh_ref])
            gens.append(group(xs, b, c, dt, pr, z, gw, y, st, dyn, dxs, db, dc, dz, ddt, dgw, dpr, dh))
        _round_robin(gens)

    def group(xs_ref, b_ref, c_ref, dt_ref, par_ref, z_ref, gnw_ref, y_ref, st_ref, dyn_ref,
              dxs_ref, db_ref, dc_ref, dz_ref, ddt_ref, dgnw_ref, dpar_ref, dh_ref):
        yv = y_ref[...]
        zv = z_ref[...]
        sg = _sigmoid(zv)
        sz = zv * sg
        yg = yv * sz
        r = jnp.tile(lax.rsqrt(_rowsum_mxu(yg * yg) * (1.0 / 512) + EPS), (1, 4))
        yh = yg * r
        dyn = dyn_ref[...].astype(F32)
        dgnw_ref[...] += jnp.sum(dyn * yh, axis=0, keepdims=True)
        dyh = dyn * gnw_ref[...]
        dyg = r * (dyh - yh * jnp.tile(_rowsum_mxu(dyh * yh) * (1.0 / 512), (1, 4)))
        dY_all = dyg * sz
        dz_ref[...] = (dyg * yv * (sg * (1.0 + zv * (1.0 - sg)))).astype(dz_ref.dtype)

        yield
        raw, dt, a, dsk, cs, csT, ecs, eend, dec = _ssd_common(dt_ref, par_ref)
        Bb = b_ref[...].astype(BF16)
        Cb = c_ref[...].astype(BF16)
        G = lax.dot_general(Cb, Bb, _NT, preferred_element_type=F32)
        row = lax.broadcasted_iota(jnp.int32, (L, L), 0)
        col = lax.broadcasted_iota(jnp.int32, (L, L), 1)
        tril = col <= row
        lo = lax.broadcasted_iota(jnp.int32, (L, 128), 1) < 64
        lane1 = lax.broadcasted_iota(jnp.int32, (1, 128), 1)
        lo1 = lane1 < 64
        rowl = lax.broadcasted_iota(jnp.int32, (L, 128), 0)
        dt_x, ecs_x, eend_x = (_heads_to_pairs(m) for m in (dt, ecs, eend))
        dG = jnp.zeros((L, L), F32)
        dB = jnp.zeros((L, SSM_STATE), F32)
        dC = jnp.zeros((L, SSM_STATE), F32)
        dcs_t = jnp.zeros((L, L), F32)
        tails = jnp.zeros((1, 128), F32)
        dD_row = jnp.zeros((1, 128), F32)
        v_parts, prod_parts = [], []

        def tot(m):
            return jnp.sum(jnp.sum(m, axis=0, keepdims=True), axis=1, keepdims=True)

        for pp in range(4):
            hA, hB = 2 * pp, 2 * pp + 1
            lanes = slice(pp * 128, (pp + 1) * 128)

            def sel1(m):
                return jnp.where(lo1, m[:, hA:hA + 1], m[:, hB:hB + 1])

            X = xs_ref[:, lanes]
            dY = dY_all[:, lanes]
            dtsel = dt_x[:, lanes]
            xd = X * dtsel
            xdb = xd.astype(BF16)
            dYb = dY.astype(BF16)
            Hp = st_ref[pp]
            Hb = Hp.astype(BF16)
            dHn = dh_ref[pp]
            dHb = dHn.astype(BF16)
            ecs_sel = ecs_x[:, lanes]
            eend_sel = eend_x[:, lanes]
            dxd_state = jnp.dot(Bb, dHb, preferred_element_type=F32) * eend_sel
            yoff = jnp.dot(Cb, Hb, preferred_element_type=F32) * ecs_sel
            dYe = (dY * ecs_sel).astype(BF16)
            dC = dC + lax.dot_general(dYe, Hb, _NT, preferred_element_type=F32)
            dB = dB + lax.dot_general((xd * eend_sel).astype(BF16), dHb, _NT, preferred_element_type=F32)
            dh_ref[pp] = dHn * sel1(dec) + lax.dot_general(Cb, dYe, _TN, preferred_element_type=F32)
            q = xd * dxd_state
            dyq = dY * yoff - q
            qcol = jnp.sum(q, axis=0, keepdims=True)
            hcol = jnp.sum(dHn * Hp, axis=0, keepdims=True)
            dxd_diag = []
            for h, msk, msk1 in ((hA, lo, lo1), (hB, jnp.logical_not(lo), jnp.logical_not(lo1))):
                Lm = jnp.where(tril, jnp.exp(jnp.minimum(cs[:, h:h + 1] - csT[h:h + 1, :], 0.0)), 0.0)
                M = G * Lm
                dxd_diag.append(lax.dot_general(M.astype(BF16), dYb, _TN, preferred_element_type=F32))
                dM = lax.dot_general(jnp.where(msk, dY, 0.0).astype(BF16), xdb, _NT, preferred_element_type=F32)
                dG = dG + dM * Lm
                W = dM * M
                dcs_t = dcs_t + jnp.where(row == h, jnp.sum(W, axis=0, keepdims=True), 0.0)
                v_parts.append(W + jnp.where(msk, dyq, 0.0))
                tail = (jnp.sum(jnp.where(msk1, qcol, 0.0), axis=1, keepdims=True)
                        + dec[:, h:h + 1] * jnp.sum(jnp.where(msk1, hcol, 0.0), axis=1, keepdims=True))
                tails = tails + jnp.where(lane1 == h, tail, 0.0)
                yield
            dxd = jnp.where(lo, dxd_diag[0], dxd_diag[1]) + dxd_state
            prod_parts.append(dxd * X)
            dxs_ref[:, lanes] = dxd * dtsel + sel1(dsk) * dY
            dyx = jnp.sum(dY * X, axis=0, keepdims=True)
            sA = jnp.sum(jnp.where(lo1, dyx, 0.0), axis=1, keepdims=True)
            sB = jnp.sum(dyx, axis=1, keepdims=True) - sA
            dD_row = dD_row + jnp.where(lane1 == hA, sA, 0.0) + jnp.where(lane1 == hB, sB, 0.0)
            yield
        dGb = dG.astype(BF16)
        db_ref[...] = dB + lax.dot_general(dGb, Cb, _TN, preferred_element_type=F32)
        dc_ref[...] = dC + jnp.dot(dGb, Bb, preferred_element_type=F32)
        dcs_mat = _lane_block_sums(jnp.concatenate(v_parts, axis=1), 128) + jnp.where(rowl == L - 1, tails, 0.0)
        ddt_mat = _lane_block_sums(jnp.concatenate(prod_parts, axis=1), 64)
        dad = _rcumsum_rows(dcs_mat - dcs_t.T)
        draw = (a * dad + ddt_mat) * _sigmoid(raw)
        ddt_ref[...] = draw
        dpar_ref[0:1, :] += jnp.sum(draw, axis=0, keepdims=True)
        dpar_ref[1:2, :] += jnp.sum(dt * dad, axis=0, keepdims=True) * a
        dpar_ref[2:3, :] += dD_row

    return pl.pallas_call(
        body, name=name, grid=(nc,), in_specs=in_specs,
        out_specs=[pl.BlockSpec((L, CONV_DIM), lambda c: (ci(c), 0)),
                   pl.BlockSpec((L, D_INNER), lambda c: (ci(c), 0)),
                   pl.BlockSpec((SSM_GROUPS, L, 128), lambda c: (0, ci(c), 0)),
                   pl.BlockSpec((1, D_INNER), lambda c: (0, 0)),
                   pl.BlockSpec((SSM_GROUPS, 8, 128), lambda c: (0, 0, 0))],
        out_shape=[jax.ShapeDtypeStruct((T, CONV_DIM), F32), jax.ShapeDtypeStruct((T, IN_PROJ_PAD), BF16),
                   jax.ShapeDtypeStruct((SSM_GROUPS, T, 128), F32), jax.ShapeDtypeStruct((1, D_INNER), F32),
                   jax.ShapeDtypeStruct((SSM_GROUPS, 8, 128), F32)],
        scratch_shapes=[pltpu.VMEM((SSM_GROUPS, 4, 128, 128), F32)],
        compiler_params=_cparams(("arbitrary",)))(xbc_c, xbc_c, xbc_c, dtg, par, zx, gnw, y, st, dyn)


SB_KEYS = 512
SB_SCAN = 256
SB_STRIP = 256


def _tri(width, cond):
    kk = lax.broadcasted_iota(jnp.int32, (width, width), 0)
    jj = lax.broadcasted_iota(jnp.int32, (width, width), 1)
    return cond(kk, jj).astype(BF16)


_LOG2E = 1.4426950408889634


def _softplus2(z2):
    return jnp.maximum(z2, 0.0) + jnp.log2(1.0 + jnp.exp2(-jnp.abs(z2)))


def _sba_sub_fwd(zb, c, U, mask):
    z2 = zb * _LOG2E
    s = _softplus2(z2)
    if mask is not None:
        s = jnp.where(mask, s, 0.0)
    R = c + jnp.dot(s.astype(BF16), U, preferred_element_type=F32)
    A = jnp.exp2(z2 - s - R)
    if mask is not None:
        A = jnp.where(mask, A, 0.0)
    return A.astype(BF16), R[:, 0:1] + s[:, 0:1]


def _sba_sub_bwd(zb, dAb, Lt, pc, pe, Uincl, Uexcl, mask):
    last = zb.shape[1] - 1
    z2 = zb * _LOG2E
    s = _softplus2(z2)
    g = z2 - s
    if mask is not None:
        s = jnp.where(mask, s, 0.0)
    P = pc + jnp.dot(s.astype(BF16), Uincl, preferred_element_type=F32)
    A = jnp.exp2(g - (Lt - P))
    if mask is not None:
        A = jnp.where(mask, A, 0.0)
    E = dAb * A
    PE = pe + jnp.dot(E.astype(BF16), Uexcl, preferred_element_type=F32)
    dz = E - jnp.exp2(g) * (E + PE)
    if mask is not None:
        dz = jnp.where(mask, dz, 0.0)
    return (A.astype(BF16), dz.astype(BF16), P[:, last:last + 1], PE[:, last:last + 1] + E[:, last:last + 1])


def _stack_heads(v):
    lo = lax.broadcasted_iota(jnp.int32, v.shape, 1) < 64
    zero = jnp.zeros_like(v)
    return jnp.concatenate([jnp.where(lo, v, zero), jnp.where(lo, zero, v)], axis=0)


def _unstack_heads(v):
    lo = lax.broadcasted_iota(jnp.int32, (SB_BLOCK, 128), 1) < 64
    return jnp.where(lo, v[:SB_BLOCK], v[SB_BLOCK:])


def _sba_rows(a):
    return slice(2 * a * SB_BLOCK, 2 * (a + 1) * SB_BLOCK)


def _sba_diag_case(a, b):
    Bq = SB_BLOCK
    if b * SB_SCAN >= (a + 1) * Bq:
        return "skip"
    if (b + 1) * SB_SCAN <= a * Bq:
        return "full"
    rowi = lax.broadcasted_iota(jnp.int32, (2 * Bq, SB_SCAN), 0)
    qpos = a * Bq + jnp.where(rowi >= Bq, rowi - Bq, rowi)
    return b * SB_SCAN + lax.broadcasted_iota(jnp.int32, (2 * Bq, SB_SCAN), 1) < qpos


def _sba_fwd(q, kv, *, name):
    T = q.shape[0]
    Bq = SB_BLOCK
    nsub = SB_KEYS // Bq
    nscan = SB_KEYS // SB_SCAN
    R = 2 * SB_KEYS
    assert T % SB_KEYS == 0 and SB_STRIP == 2 * Bq
    scale = 1.0 / math.sqrt(SB_HEAD_DIM)

    def body(q_ref, k_ref, v_ref, o_ref, lt_ref, z_s, a_s, c_s, acc_s):
        i = pl.program_id(1)
        U2 = _tri(SB_SCAN, lambda k, j: k > j)
        qs_all = jnp.concatenate([_stack_heads(q_ref[a * Bq:(a + 1) * Bq, :] * scale) for a in range(nsub)], axis=0)
        c_s[...] = jnp.zeros_like(c_s)
        acc_s[...] = jnp.zeros_like(acc_s)

        def scores(J, slot):
            off = pl.multiple_of(J * SB_KEYS, SB_KEYS)
            z_s[slot] = lax.dot_general(qs_all, k_ref[pl.ds(off, SB_KEYS), :], _NT, preferred_element_type=F32)

        def weights(slot, diag):
            for a in range(nsub):
                rows = _sba_rows(a)
                c = c_s[rows, :]
                for b in reversed(range(nscan)):
                    cols = slice(b * SB_SCAN, (b + 1) * SB_SCAN)
                    case = _sba_diag_case(a, b) if diag else "full"
                    if isinstance(case, str) and case == "skip":
                        a_s[slot, rows, cols] = jnp.zeros((2 * Bq, SB_SCAN), BF16)
                        continue
                    A, c = _sba_sub_fwd(z_s[slot, rows, cols], c, U2, None if isinstance(case, str) else case)
                    a_s[slot, rows, cols] = A
                c_s[rows, :] = c

        def values(J, slot):
            off = pl.multiple_of(J * SB_KEYS, SB_KEYS)
            acc_s[...] += jnp.dot(a_s[slot], v_ref[pl.ds(off, SB_KEYS), :], preferred_element_type=F32)

        scores(i, 0)
        weights(0, True)
        scores(jnp.maximum(i - 1, 0), 1)

        def two_steps(u, _):
            t = 2 * u + 1
            weights(1, False)
            scores(jnp.maximum(i - t - 1, 0), 0)
            values(i - t + 1, 0)
            weights(0, False)
            scores(jnp.maximum(i - t - 2, 0), 1)
            values(i - t, 1)
            return 0

        lax.fori_loop(0, i // 2, two_steps, 0)
        odd = lax.rem(i, 2) == 1

        @pl.when(jnp.logical_not(odd))
        def _():
            values(0, 0)

        @pl.when(odd)
        def _():
            weights(1, False)
            values(1, 0)
            values(0, 1)
        for a in range(nsub):
            o_ref[a * Bq:(a + 1) * Bq, :] = _unstack_heads(acc_s[_sba_rows(a), :]).astype(BF16)
            lt_ref[a * Bq:(a + 1) * Bq, :] = _unstack_heads(jnp.broadcast_to(c_s[_sba_rows(a), :], (2 * Bq, 128)))

    return pl.pallas_call(
        body, name=name, grid=(SB_HEADS // 2, T // SB_KEYS),
        in_specs=[pl.BlockSpec((SB_KEYS, 128), lambda p, i: (i, p)), pl.BlockSpec((T, 128), lambda p, i: (0, 2 * p)),
                  pl.BlockSpec((T, 128), lambda p, i: (0, 2 * p + 1))],
        out_specs=[pl.BlockSpec((SB_KEYS, 128), lambda p, i: (i, p)),
                   pl.BlockSpec((None, SB_KEYS, 128), lambda p, i: (p, i, 0))],
        out_shape=[jax.ShapeDtypeStruct((T, D_MODEL), BF16), jax.ShapeDtypeStruct((SB_HEADS // 2, T, 128), F32)],
        scratch_shapes=[pltpu.VMEM((2, R, SB_KEYS), F32), pltpu.VMEM((2, R, SB_KEYS), BF16),
                        pltpu.VMEM((R, 1), F32), pltpu.VMEM((R, 128), F32)],
        compiler_params=_cparams(("parallel", "parallel")))(q, kv, kv)


def _sba_bwd(q, kv, lt, do, *, name):
    T = q.shape[0]
    Bq = SB_BLOCK
    nq = T // SB_KEYS
    nsub = SB_KEYS // Bq
    nscan = SB_KEYS // SB_SCAN
    R = 2 * SB_KEYS
    assert T % SB_KEYS == 0 and SB_STRIP == 2 * Bq
    scale = 1.0 / math.sqrt(SB_HEAD_DIM)

    def body(q_ref, k_ref, v_ref, lt_ref, do_ref, dq_ref, dkv_ref, dk_acc, dv_acc,
             z_s, da_s, a_s, dz_s, pc_s, pe_s, lt_s, dq_s):
        i = pl.program_id(1)

        @pl.when(i == 0)
        def _():
            dk_acc[...] = jnp.zeros_like(dk_acc)
            dv_acc[...] = jnp.zeros_like(dv_acc)

        Uincl = _tri(SB_SCAN, lambda k, j: k <= j)
        Uexcl = _tri(SB_SCAN, lambda k, j: k < j)
        qs, dos = [], []
        for a in range(nsub):
            rows = slice(a * Bq, (a + 1) * Bq)
            qs.append(_stack_heads(q_ref[rows, :] * scale))
            dos.append(_stack_heads(do_ref[rows, :]))
            lt_s[_sba_rows(a), :] = jnp.concatenate([lt_ref[rows, 0:1], lt_ref[rows, 64:65]], axis=0)
        qs_all = jnp.concatenate(qs, axis=0)
        dos_all = jnp.concatenate(dos, axis=0)
        pc_s[...] = jnp.zeros_like(pc_s)
        pe_s[...] = jnp.zeros_like(pe_s)
        a_s[1] = jnp.zeros((R, SB_KEYS), BF16)
        dz_s[1] = jnp.zeros((R, SB_KEYS), BF16)

        def scores(J, slot):
            off = pl.multiple_of(J * SB_KEYS, SB_KEYS)
            z_s[slot] = lax.dot_general(qs_all, k_ref[pl.ds(off, SB_KEYS), :], _NT, preferred_element_type=F32)
            da_s[slot] = lax.dot_general(dos_all, v_ref[pl.ds(off, SB_KEYS), :], _NT, preferred_element_type=F32)

        def gradients(slot, diag):
            for a in range(nsub):
                rows = _sba_rows(a)
                pc, pe, Lt = pc_s[rows, :], pe_s[rows, :], lt_s[rows, :]
                for b in range(nscan):
                    cols = slice(b * SB_SCAN, (b + 1) * SB_SCAN)
                    case = _sba_diag_case(a, b) if diag else "full"
                    if isinstance(case, str) and case == "skip":
                        a_s[slot, rows, cols] = jnp.zeros((2 * Bq, SB_SCAN), BF16)
                        dz_s[slot, rows, cols] = jnp.zeros((2 * Bq, SB_SCAN), BF16)
                        continue
                    A, dz, pc, pe = _sba_sub_bwd(z_s[slot, rows, cols], da_s[slot, rows, cols], Lt, pc, pe, Uincl, Uexcl,
                                                 None if isinstance(case, str) else case)
                    a_s[slot, rows, cols] = A
                    dz_s[slot, rows, cols] = dz
                pc_s[rows, :] = pc
                pe_s[rows, :] = pe

        def products(J, slot):
            off = pl.multiple_of(J * SB_KEYS, SB_KEYS)
            dzt = dz_s[slot]
            dk_acc[pl.ds(off, SB_KEYS), :] += lax.dot_general(dzt, qs_all, _TN, preferred_element_type=F32)
            dv_acc[pl.ds(off, SB_KEYS), :] += lax.dot_general(a_s[slot], dos_all, _TN, preferred_element_type=F32)
            dq_s[...] += jnp.dot(dzt, k_ref[pl.ds(off, SB_KEYS), :], preferred_element_type=F32)

        dq_s[...] = jnp.zeros_like(dq_s)
        scores(0, 0)

        def two_steps(u, _):
            t = 2 * u
            gradients(0, False)
            scores(t + 1, 1)
            products(jnp.maximum(t - 1, 0), 1)
            gradients(1, False)
            scores(t + 2, 0)
            products(t, 0)
            return 0

        lax.fori_loop(0, i // 2, two_steps, 0)
        odd = lax.rem(i, 2) == 1

        @pl.when(jnp.logical_not(odd))
        def _():
            gradients(0, True)
            products(jnp.maximum(i - 1, 0), 1)
            products(i, 0)

        @pl.when(odd)
        def _():
            gradients(0, False)
            scores(i, 1)
            products(jnp.maximum(i - 2, 0), 1)
            gradients(1, True)
            products(i - 1, 0)
            products(i, 1)

        for a in range(nsub):
            dq_ref[a * Bq:(a + 1) * Bq, :] = (_unstack_heads(dq_s[_sba_rows(a), :]) * scale).astype(BF16)

        @pl.when(i == nq - 1)
        def _():
            dkv_ref[:, 0:128] = dk_acc[...].astype(BF16)
            dkv_ref[:, 128:256] = dv_acc[...].astype(BF16)

    return pl.pallas_call(
        body, name=name, grid=(SB_HEADS // 2, nq),
        in_specs=[pl.BlockSpec((SB_KEYS, 128), lambda p, i: (i, p)), pl.BlockSpec((T, 128), lambda p, i: (0, 2 * p)),
                  pl.BlockSpec((T, 128), lambda p, i: (0, 2 * p + 1)),
                  pl.BlockSpec((None, SB_KEYS, 128), lambda p, i: (p, i, 0)),
                  pl.BlockSpec((SB_KEYS, 128), lambda p, i: (i, p))],
        out_specs=[pl.BlockSpec((SB_KEYS, 128), lambda p, i: (i, p)), pl.BlockSpec((T, 256), lambda p, i: (0, p))],
        out_shape=[jax.ShapeDtypeStruct((T, D_MODEL), BF16), jax.ShapeDtypeStruct((T, 2 * D_MODEL), BF16)],
        scratch_shapes=[pltpu.VMEM((T, 128), F32), pltpu.VMEM((T, 128), F32),
                        pltpu.VMEM((2, R, SB_KEYS), F32), pltpu.VMEM((2, R, SB_KEYS), F32),
                        pltpu.VMEM((2, R, SB_KEYS), BF16), pltpu.VMEM((2, R, SB_KEYS), BF16),
                        pltpu.VMEM((R, 1), F32), pltpu.VMEM((R, 1), F32), pltpu.VMEM((R, 1), F32),
                        pltpu.VMEM((R, 128), F32)],
        compiler_params=_cparams(("parallel", "arbitrary")))(q, kv, kv, lt, do)


def _loss_head(h, tgt, w, *, name, tt=512):
    T, D = h.shape
    tt = min(tt, T)

    def body(h_ref, t_ref, w_ref, loss_ref, dh_ref, dw_ref):
        i = pl.program_id(0)
        hv = h_ref[...]
        wv = w_ref[...]
        r = lax.rsqrt(jnp.mean(hv * hv, axis=-1, keepdims=True) + EPS)
        xhat = hv * r
        err = xhat * wv - t_ref[...]
        part = 0.5 * jnp.sum(jnp.mean(err * err, axis=-1, keepdims=True), axis=0, keepdims=True)
        dy = err * (1.0 / D)
        dxh = dy * wv
        dh_ref[...] = r * (dxh - xhat * jnp.mean(dxh * xhat, axis=-1, keepdims=True))
        dwc = jnp.sum(dy * xhat, axis=0, keepdims=True)

        @pl.when(i == 0)
        def _():
            loss_ref[...] = jnp.broadcast_to(part, loss_ref.shape)
            dw_ref[...] = dwc

        @pl.when(i > 0)
        def _():
            loss_ref[...] += jnp.broadcast_to(part, loss_ref.shape)
            dw_ref[...] += dwc

    return pl.pallas_call(
        body, name=name, grid=(T // tt,),
        in_specs=[pl.BlockSpec((tt, D), lambda i: (i, 0)), pl.BlockSpec((tt, D), lambda i: (i, 0)),
                  pl.BlockSpec((1, D), lambda i: (0, 0))],
        out_specs=[pl.BlockSpec((1, 128), lambda i: (0, 0)), pl.BlockSpec((tt, D), lambda i: (i, 0)),
                   pl.BlockSpec((1, D), lambda i: (0, 0))],
        out_shape=[jax.ShapeDtypeStruct((1, 128), F32), jax.ShapeDtypeStruct((T, D), F32),
                   jax.ShapeDtypeStruct((1, D), F32)],
        compiler_params=_cparams(("arbitrary",)))(h, tgt, w.reshape(1, D))


def _adamw(parts, w, m, v, *, name, tr=256, tc=None):
    plist = list(parts) if isinstance(parts, (list, tuple)) else [parts]
    P, _, C = plist[0].shape
    R = sum(a.shape[1] for a in plist)
    tr = min(tr, R)
    tc = C if tc is None else tc
    assert all(a.shape[1] % tr == 0 for a in plist) and C % tc == 0, (name, R, C, tr, tc)
    nbs = [a.shape[1] // tr for a in plist]
    offs = [sum(nbs[:l]) for l in range(len(nbs))]
    c1 = 1.0 - ADAM_B1 ** ADAM_STEP
    c2 = 1.0 - ADAM_B2 ** ADAM_STEP

    def body(*refs):
        p_refs = refs[:len(plist)]
        w_ref, m_ref, v_ref, g_ref, d_ref, nm_ref, nv_ref = refs[len(plist):]
        i = pl.program_id(0)
        g = None
        for l, p_ref in enumerate(p_refs):
            gl = p_ref[0].astype(F32)
            for k in range(1, P):
                gl = gl + p_ref[k].astype(F32)
            g = gl if g is None else jnp.where(i >= offs[l], gl, g)
        mn = ADAM_B1 * m_ref[...] + (1.0 - ADAM_B1) * g
        vn = ADAM_B2 * v_ref[...] + (1.0 - ADAM_B2) * (g * g)
        g_ref[...] = g
        nm_ref[...] = mn
        nv_ref[...] = vn
        d_ref[...] = -ADAM_LR * ((mn / c1) / (jnp.sqrt(vn / c2) + ADAM_EPS) + ADAM_WD * w_ref[...])

    spec = pl.BlockSpec((tr, tc), lambda i, j: (i, j))
    sds = jax.ShapeDtypeStruct((R, C), F32)
    return pl.pallas_call(
        body, name=name, grid=(R // tr, C // tc),
        in_specs=[pl.BlockSpec((P, tr, tc), functools.partial(lambda i, j, o, n: (0, jnp.clip(i - o, 0, n - 1), j), o=o, n=n))
                  for o, n in zip(offs, nbs)] + [spec, spec, spec],
        out_specs=[spec, spec, spec, spec], out_shape=[sds, sds, sds, sds],
        compiler_params=_cparams(("parallel", "parallel")))(*plist, w, m, v)


def _all_gather(shards, *, name):
    n = len(shards)

    def body(*refs):
        ins, outs = refs[:n], refs[n:2 * n]
        send_sems, recv_sems, local_sems = refs[2 * n:]
        x, y, c = lax.axis_index("x"), lax.axis_index("y"), lax.axis_index("c")
        me, sib = (x, y, c), (x, y, 1 - c)
        chips = [(1 - x, y), (x, 1 - y), (1 - x, 1 - y)]

        def slot(p):
            return 4 * p[0] + 2 * p[1] + p[2]

        def cp(a, k, block, to, src=None):
            dst = outs[a].at[slot(block)]
            return pltpu.make_async_remote_copy(src_ref=dst if src is None else src, dst_ref=dst,
                                                send_sem=send_sems.at[a, k], recv_sem=recv_sems.at[a, k],
                                                device_id=to, device_id_type=_MESH)

        mine = [pltpu.make_async_copy(ins[a], outs[a].at[slot(me)], local_sems.at[a]) for a in range(n)]
        for m in mine:
            m.start()
        first = []
        for a in range(n):
            first.append(cp(a, 0, me, sib, src=ins[a]))
            for j, chip in enumerate(chips):
                first.append(cp(a, 1 + j, me, (*chip, c), src=ins[a]))
        for f in first:
            f.start()
        passed = []
        for j, chip in enumerate(chips):
            for a in range(n):
                cp(a, 1 + j, (*chip, c), me).wait_recv()
                f = cp(a, 4 + j, (*chip, c), sib)
                f.start()
                passed.append(f)
        for a in range(n):
            cp(a, 0, sib, me).wait_recv()
            for j, chip in enumerate(chips):
                cp(a, 4 + j, (*chip, 1 - c), me).wait_recv()
        for f in first + passed:
            f.wait_send()
        for m in mine:
            m.wait()

    return pl.pallas_call(
        body, name=name, in_specs=[_ANY] * n, out_specs=[_ANY] * n,
        out_shape=[jax.ShapeDtypeStruct((N_DEV,) + s.shape, s.dtype) for s in shards],
        scratch_shapes=[pltpu.SemaphoreType.DMA((n, 7)), pltpu.SemaphoreType.DMA((n, 7)),
                        pltpu.SemaphoreType.DMA((n,))])(*shards)


_HBM = pl.BlockSpec(memory_space=pltpu.HBM)
_SEM = pl.BlockSpec(memory_space=pltpu.SEMAPHORE)
_EFFECT = pltpu.SideEffectType.DATAFLOW_SIDE_EFFECTING


def _peers():
    x, y, c = lax.axis_index("x"), lax.axis_index("y"), lax.axis_index("c")
    out = []
    for r in range(1, N_DEV):
        px = 1 - x if (r >> 2) & 1 else x
        py = 1 - y if (r >> 1) & 1 else y
        pc = 1 - c if r & 1 else c
        out.append(((px, py, pc), 4 * px + 2 * py + pc))
    return 4 * x + 2 * y + c, out


def _push_copy(src_ref, land_ref, send_sems, recv_sems, a, k, me, peer, peer_slot, scatter, arriving):
    src = src_ref.at[peer_slot] if scatter else src_ref
    return pltpu.make_async_remote_copy(
        src_ref=src, dst_ref=land_ref.at[peer_slot if arriving else me], send_sem=send_sems.at[a * (N_DEV - 1) + k],
        recv_sem=recv_sems.at[a * (N_DEV - 1) + k], device_id=peer, device_id_type=_MESH)


def _push_start(srcs, *, scatter, name):
    n = len(srcs)
    lands = [lax.empty(s.shape if scatter else (N_DEV,) + s.shape, s.dtype) for s in srcs]

    def body(*refs):
        src_refs, land_refs = refs[:n], refs[n:2 * n]
        send_sems, recv_sems = refs[2 * n], refs[2 * n + 1]
        token = refs[-1]
        me, peers = _peers()
        for k, (peer, slot) in enumerate(peers):
            for a in range(n):
                _push_copy(src_refs[a], land_refs[a], send_sems, recv_sems, a, k, me, peer, slot, scatter, False).start()
        token[...] = jnp.zeros_like(token)

    hbm = lambda a: pltpu.HBM(a.shape, a.dtype)
    outs = pl.pallas_call(
        body, name=name,
        out_shape=(pltpu.SemaphoreType.DMA((n * (N_DEV - 1),)), pltpu.SemaphoreType.DMA((n * (N_DEV - 1),)),
                   *[hbm(s) for s in srcs], *[hbm(l) for l in lands], jax.ShapeDtypeStruct((8, 128), F32)),
        in_specs=[_HBM] * (2 * n),
        out_specs=(_SEM, _SEM, *([_HBM] * (2 * n)), pl.BlockSpec(memory_space=pltpu.VMEM)),
        input_output_aliases={i: 2 + i for i in range(2 * n)},
        compiler_params=pltpu.CompilerParams(has_side_effects=_EFFECT),
    )(*[pltpu.with_memory_space_constraint(s, pltpu.HBM) for s in srcs],
      *[pltpu.with_memory_space_constraint(l, pltpu.HBM) for l in lands])
    return dict(send=outs[0], recv=outs[1], srcs=list(outs[2:2 + n]), lands=list(outs[2 + n:2 + 2 * n]),
                token=outs[-1], scatter=scatter, n=n)


def _push_wait(h, after, *, name):
    n, scatter = h["n"], h["scatter"]

    def body(*refs):
        src_refs, land_refs = refs[:n], refs[n:2 * n]
        send_sems, recv_sems = refs[2 * n], refs[2 * n + 1]
        me, peers = _peers()
        for k, (peer, slot) in enumerate(peers):
            for a in range(n):
                cp = _push_copy(src_refs[a], land_refs[a], send_sems, recv_sems, a, k, me, peer, slot, scatter, True)
                cp.wait_send()
                cp.wait_recv()

    hbm = lambda a: pltpu.HBM(a.shape, a.dtype)
    outs = pl.pallas_call(
        body, name=name,
        out_shape=(*[hbm(s) for s in h["srcs"]], *[hbm(l) for l in h["lands"]]),
        in_specs=[_HBM] * (2 * n) + [_SEM, _SEM, _ANY], out_specs=tuple([_HBM] * (2 * n)),
        input_output_aliases={i: i for i in range(2 * n)},
        compiler_params=pltpu.CompilerParams(has_side_effects=_EFFECT),
    )(*h["srcs"], *h["lands"], h["send"], h["recv"], after)
    return list(outs[:n]), list(outs[n:])


def _ffn_fwd(h, nw, w_up, conv_w, conv_b, w_down, tag):
    a3 = _mm_fwd(h, w_up, norm_w=nw, name=f"ffn{tag}_up", out_dtype=BF16, halves=True, w_t=True, tm=1024, tn=2816)
    p = _ffn_conv_fwd3(a3, conv_w, conv_b.reshape(1, -1), name=f"ffn{tag}_conv")
    h_out = _mm_fwd(p, w_down, residual=h, name=f"ffn{tag}_down", tm=1024, tn=512)
    return h_out, (a3, p)


def _ffn_bwd(dh, h, saved, nw, w_up, conv_w, conv_b, w_down, tag):
    a3, p = saved
    g_down = _mm_tn(p, dh, name=f"ffn{tag}_down_wg", tk1=1408, tn=1024)
    dp = _mm_nt(dh, w_down, name=f"ffn{tag}_down_dg", out_dtype=BF16, tm=512, tn=2816, tk=1024)
    dhid3, dw3, db3 = _ffn_conv_bwd3(a3, conv_w, conv_b.reshape(1, -1), dp, name=f"ffn{tag}_conv_bwd")
    da3 = _conv_bwd_in3(dhid3, conv_w, K=FFN_CONV, name=f"ffn{tag}_conv_bwd_in")
    g_up = _mm_tn_t(da3, h, norm_w=nw, name=f"ffn{tag}_up_wg", tn=1408, tt=1024)
    dh_out, g_nw = _mm_nt(da3, w_up, epi=(h, nw, dh), name=f"ffn{tag}_up_dg", w_t=True, tm=1024, tk=1408)
    g_cw = jnp.concatenate([dw3[0], dw3[1]], axis=1)
    g_cb = jnp.concatenate([db3[0], db3[1]], axis=1)
    return dh_out, dict(norm=g_nw.reshape(-1), up=g_up, conv_w=g_cw, conv_b=g_cb.reshape(-1), down=g_down)


_BIG = ["ssm_in_w", "ssm_out_w", "w_k", "w_v", "w_q", "w_o", "ffn_up_w", "ffn_down_w"]
_SMALL_SHARDED = ["ssm_norm_w", "ssm_conv_w", "ssm_conv_b", "ssm_gate_norm_w", "ffn_conv_w"]
_SMALL_REPL = ["ssm_dt_bias", "ssm_a_log", "ssm_d", "kv_norm_w", "attn_norm_w", "ffn_norm_w", "ffn_conv_b",
               "final_norm_w"]
_WEIGHTS = ["ssm_norm_w", "ssm_in_w", "ssm_conv_w", "ssm_conv_b", "ssm_dt_bias", "ssm_a_log", "ssm_d",
            "ssm_gate_norm_w", "ssm_out_w", "kv_norm_w", "w_k", "w_v", "attn_norm_w", "w_q", "w_o", "ffn_norm_w",
            "ffn_up_w", "ffn_conv_w", "ffn_conv_b", "ffn_down_w", "final_norm_w"]


def _as2d(a):
    return a.reshape(-1, a.shape[-1])


def _cols_to_full(g):
    return g.transpose(1, 0, 2).reshape(g.shape[1], N_DEV * g.shape[2])


def _pack_small(vals):
    flat = jnp.concatenate([v.reshape(-1).astype(F32) for v in vals])
    n = flat.shape[0]
    rows = -(-n // 1024) * 8
    return jnp.pad(flat, (0, rows * 128 - n)).reshape(rows, 128)


def _unpack_small(packed, shapes):
    flat = packed.reshape(-1)
    out, off = [], 0
    for s in shapes:
        n = math.prod(s)
        out.append(flat[off:off + n].reshape(s))
        off += n
    return out


def _tie(a, token):
    return a + token[0, 0].astype(a.dtype)


def _local_step(x, tgt, get_w, put_g):
    T = x.shape[0]
    Ws = get_w("ssm", None)
    fnw, fcw, fcb = Ws["ffn_norm_w"], Ws["ffn_conv_w"], Ws["ffn_conv_b"]
    zx = _mm_fwd(x, Ws["in_w"], norm_w=Ws["ssm_norm_w"], name="ssm_in", w_t=True, tm=1024, tn=1792)
    xbc_c = _ssm_conv_fwd(zx, Ws["ssm_conv_w"], Ws["ssm_conv_b"].reshape(1, -1), name="ssm_conv")
    dt_raw = zx[:, D_INNER + CONV_DIM:IN_PROJ_DIM]
    dtg = jnp.pad(dt_raw.reshape(T, SSM_GROUPS, 8).transpose(1, 0, 2), ((0, 0), (0, 0), (0, 120)))
    par = jnp.stack([Ws["ssm_dt_bias"].reshape(SSM_GROUPS, 8), Ws["ssm_a_log"].reshape(SSM_GROUPS, 8),
                     Ws["ssm_d"].reshape(SSM_GROUPS, 8)], axis=1)
    par = jnp.pad(par, ((0, 0), (0, 5), (0, 120)))
    gnw = _tie(Ws["ssm_gate_norm_w"].reshape(1, D_INNER), get_w("rest_start", xbc_c))
    y, yn, st = _ssd_fwd(xbc_c, zx, dtg, par, gnw, name="ssd_fwd")
    W0 = get_w("ffn0", y)
    Ws["ssm_out_w"] = W0["ssm_out_w"]
    h1 = _mm_fwd(yn, Ws["ssm_out_w"], residual=x, name="ssm_out", tm=1024, tn=512)
    h2, ffn0 = _ffn_fwd(h1, fnw[0], W0["up"], fcw[0], fcb[0], W0["down"], "0")
    Wr = get_w("rest", h2)
    q = _mm_fwd(h2, Wr["w_q"], norm_w=Ws["attn_norm_w"], out_dtype=BF16, name="attn_q", tm=1024, tn=1024)
    kv = _mm_fwd(h2, Wr["w_kv"], norm_w=Ws["kv_norm_w"], out_dtype=BF16, name="attn_kv", tm=1024, tn=1024)
    o, lt = _sba_fwd(q, kv, name="sba_fwd")
    h3 = _mm_fwd(o, Wr["w_o"], residual=h2, name="attn_o", tm=1024, tn=512)
    W1 = get_w("ffn1", h3)
    h4, ffn1 = _ffn_fwd(h3, fnw[1], W1["up"], fcw[1], fcb[1], W1["down"], "1")
    loss, dh4, g_final = _loss_head(h4, tgt, Ws["final_norm_w"], name="loss_head")
    dh3, gf1 = _ffn_bwd(dh4, h3, ffn1, fnw[1], W1["up"], fcw[1], fcb[1], W1["down"], "1")
    tok = put_g("ffn1", dict(up=gf1["up"], down=gf1["down"]))
    g_wo = _mm_tn(o, dh3, name="attn_o_wg", tn=1024)
    do = _mm_nt(dh3, _tie(Wr["w_o"], tok), name="attn_o_dg", out_dtype=BF16, tn=1024, tk=1024)
    dq, dkv = _sba_bwd(q, kv, lt, do, name="sba_bwd")
    g_wq = _mm_tn(h2, dq, norm_w=Ws["attn_norm_w"], name="attn_q_wg", tn=1024, tt=1024)
    dh2a, g_attn_nw = _mm_nt(dq, Wr["w_q"], epi=(h2, Ws["attn_norm_w"], dh3), name="attn_q_dg", tm=1024, tk=1024)
    g_wkv = _mm_tn(h2, dkv, norm_w=Ws["kv_norm_w"], name="attn_kv_wg", tn=1024, tt=1024)
    dh2, g_kv_nw = _mm_nt(dkv, Wr["w_kv"], epi=(h2, Ws["kv_norm_w"], dh2a), name="attn_kv_dg", tm=1024, tk=1024)
    g_wkv = g_wkv.reshape(D_MODEL, SB_HEADS // 2, 2, 128)
    tok = put_g("attn", dict(w_o=g_wo, w_q=g_wq, w_k=g_wkv[:, :, 0].reshape(D_MODEL, D_MODEL),
                             w_v=g_wkv[:, :, 1].reshape(D_MODEL, D_MODEL)))
    dh1, gf0 = _ffn_bwd(dh2, h1, ffn0, fnw[0], W0["up"], fcw[0], _tie(fcb[0], tok), W0["down"], "0")
    tok = put_g("ffn0", dict(up=gf0["up"], down=gf0["down"]))
    g_out = _mm_tn(yn, dh1, name="ssm_out_wg", tn=1024)
    dyn = _mm_nt(dh1, _tie(Ws["ssm_out_w"], tok), name="ssm_out_dg", out_dtype=BF16, tn=1024, tk=1024)
    tok = put_g("ssm_out", dict(ssm_out_w=g_out))
    dxbc_c, dz, ddt, g_gnw, dpar = _ssd_bwd(xbc_c, zx, dtg, par, _tie(gnw, tok), y, st, dyn, name="ssd_bwd")
    dhid, g_scw, g_scb = _ssm_conv_bwd_pre(zx, Ws["ssm_conv_w"], Ws["ssm_conv_b"].reshape(1, -1), dxbc_c,
                                           name="ssm_conv_bwd")
    dzx = _conv_bwd_in(dhid, Ws["ssm_conv_w"], K=SSM_CONV, name="ssm_conv_bwd_in", into=(dz, D_INNER))
    ddt_t = ddt[:, :, :8].transpose(1, 0, 2).reshape(T, SSM_HEADS).astype(BF16)
    dzx = _put_cols(dzx, jnp.pad(ddt_t, ((0, 0), (0, IN_PROJ_PAD - IN_PROJ_DIM))), D_INNER + CONV_DIM, name="ssm_ddt_cols")
    g_in = _mm_tn_t(dzx, x, norm_w=Ws["ssm_norm_w"], name="ssm_in_wg", tn=1792, tt=1024)
    tok = put_g("ssm_in", dict(ssm_in_w=g_in[:IN_PROJ_DIM]))
    dx, g_ssm_nw = _mm_nt(dzx, Ws["in_w"], epi=(x, _tie(Ws["ssm_norm_w"], tok), dh1), name="ssm_in_dg", w_t=True,
                          tm=1024, tk=1792)
    f = {
        "ssm_norm_w": g_ssm_nw.reshape(-1), "ssm_conv_w": g_scw,
        "ssm_conv_b": g_scb.reshape(-1), "ssm_dt_bias": dpar[:, 0, :8].reshape(-1),
        "ssm_a_log": dpar[:, 1, :8].reshape(-1), "ssm_d": dpar[:, 2, :8].reshape(-1),
        "ssm_gate_norm_w": g_gnw.reshape(-1), "kv_norm_w": g_kv_nw.reshape(-1), "attn_norm_w": g_attn_nw.reshape(-1),
        "ffn_norm_w": jnp.stack([gf0["norm"], gf1["norm"]]), "ffn_conv_w": jnp.stack([gf0["conv_w"], gf1["conv_w"]]),
        "ffn_conv_b": jnp.stack([gf0["conv_b"], gf1["conv_b"]]), "final_norm_w": g_final.reshape(-1),
    }
    return loss, dx, f


def kernel(x, ssm_norm_w, ssm_in_w, ssm_conv_w, ssm_conv_b, ssm_dt_bias, ssm_a_log, ssm_d, ssm_gate_norm_w, ssm_out_w, kv_norm_w, w_k, w_v, attn_norm_w, w_q, w_o, ffn_norm_w, ffn_up_w, ffn_conv_w, ffn_conv_b, ffn_down_w, final_norm_w, loss_target, m_ssm_norm_w, m_ssm_in_w, m_ssm_conv_w, m_ssm_conv_b, m_ssm_dt_bias, m_ssm_a_log, m_ssm_d, m_ssm_gate_norm_w, m_ssm_out_w, m_kv_norm_w, m_w_k, m_w_v, m_attn_norm_w, m_w_q, m_w_o, m_ffn_norm_w, m_ffn_up_w, m_ffn_conv_w, m_ffn_conv_b, m_ffn_down_w, m_final_norm_w, v_ssm_norm_w, v_ssm_in_w, v_ssm_conv_w, v_ssm_conv_b, v_ssm_dt_bias, v_ssm_a_log, v_ssm_d, v_ssm_gate_norm_w, v_ssm_out_w, v_kv_norm_w, v_w_k, v_w_v, v_attn_norm_w, v_w_q, v_w_o, v_ffn_norm_w, v_ffn_up_w, v_ffn_conv_w, v_ffn_conv_b, v_ffn_down_w, v_final_norm_w):
    env = dict(locals())
    p = {n: env[n] for n in _WEIGHTS}
    mom = {n: env["m_" + n] for n in _WEIGHTS}
    var = {n: env["v_" + n] for n in _WEIGHTS}
    T = x.shape[1]
    me = 4 * lax.axis_index("x") + 2 * lax.axis_index("y") + lax.axis_index("c")
    rs = D_FF // N_DEV

    def bf2(a):
        return _as2d(a).astype(BF16)

    _T = ("ssm_in_w", "ffn_up_w")

    def t2d(a):
        return jnp.swapaxes(a, -1, -2).reshape(-1, a.shape[-2])

    def from_t2d(a, like):
        return jnp.swapaxes(a.reshape(like.shape[:-2] + (like.shape[-1], like.shape[-2])), -1, -2)

    n_in, n_up = p["ssm_in_w"].shape[-1], p["ffn_up_w"].shape[-1]

    def with_own(srcs, lands, scatter):
        out = []
        for s, l in zip(srcs, lands):
            own = lax.dynamic_index_in_dim(s, me, 0, keepdims=False) if scatter else s
            out.append(lax.dynamic_update_index_in_dim(l, own, me, 0))
        return out

    a_names = ["ssm_in_w"] + _SMALL_SHARDED
    got_a = dict(zip(a_names, _all_gather([t2d(p["ssm_in_w"]).astype(BF16)] + [_as2d(p[n]) for n in _SMALL_SHARDED],
                                          name="gather_ssm")))
    ffn0_names = ["ssm_out_w", "up0", "down0"]
    rest_names = ["w_q", "w_k", "w_v", "w_o"]
    up_t = jnp.swapaxes(p["ffn_up_w"], -1, -2).astype(BF16)
    shard = {"up0": up_t[0], "down0": bf2(p["ffn_down_w"][0]), "up1": up_t[1],
             "down1": bf2(p["ffn_down_w"][1]), "w_q": bf2(p["w_q"]), "w_k": bf2(p["w_k"]), "w_v": bf2(p["w_v"]),
             "w_o": bf2(p["w_o"]), "ssm_out_w": bf2(p["ssm_out_w"])}

    def anchored(a, on):
        return a + (jnp.where(jnp.isfinite(on), on, 0.0) * 0.0).astype(a.dtype)

    h_ffn0 = _push_start([anchored(shard[ffn0_names[0]], got_a["ssm_norm_w"][0, 0, 0])]
                         + [shard[n] for n in ffn0_names[1:]], scatter=False, name="gather_ffn0_start")
    handles = {}

    def get_w(group, after):
        if group == "ssm":
            W = {n: p[n] for n in _SMALL_REPL}
            for n in ("ssm_dt_bias", "ssm_a_log", "ssm_d", "attn_norm_w"):
                W[n] = W[n].reshape(-1)
            W["in_w"] = jnp.pad(got_a["ssm_in_w"].reshape(IN_PROJ_DIM, D_MODEL), ((0, IN_PROJ_PAD - IN_PROJ_DIM), (0, 0)))
            W["ssm_norm_w"] = _tie(got_a["ssm_norm_w"].reshape(D_MODEL), h_ffn0["token"])
            W["ssm_conv_w"] = _cols_to_full(got_a["ssm_conv_w"])
            W["ssm_conv_b"] = got_a["ssm_conv_b"].reshape(CONV_DIM)
            W["ssm_gate_norm_w"] = got_a["ssm_gate_norm_w"].reshape(D_INNER)
            W["ffn_conv_w"] = _cols_to_full(got_a["ffn_conv_w"]).reshape(2, FFN_CONV, 2 * D_FF)
            return W
        if group == "rest_start":
            handles["rest"] = _push_start([anchored(shard[rest_names[0]], after[0, 0])]
                                          + [shard[n] for n in rest_names[1:]], scatter=False, name="gather_rest_start")
            handles["ffn1"] = _push_start([anchored(shard["up1"], handles["rest"]["token"][0, 0]), shard["down1"]],
                                          scatter=False, name="gather_ffn1_start")
            return handles["ffn1"]["token"]
        if group == "ffn1":
            srcs, lands = _push_wait(handles["ffn1"], after, name="gather_ffn1_wait")
            up, down = with_own(srcs, lands, False)
            return dict(up=up.reshape(2 * D_FF, D_MODEL), down=down.reshape(D_FF, D_MODEL))
        if group == "ffn0":
            srcs, lands = _push_wait(h_ffn0, after, name="gather_ffn0_wait")
            out, up, down = with_own(srcs, lands, False)
            return dict(ssm_out_w=out.reshape(D_INNER, D_MODEL), up=up.reshape(2 * D_FF, D_MODEL),
                        down=down.reshape(D_FF, D_MODEL))
        srcs, lands = _push_wait(handles["rest"], after, name="gather_rest_wait")
        g = dict(zip(rest_names, with_own(srcs, lands, False)))
        sq = lambda a: a.reshape(D_MODEL, D_MODEL)
        w_kv = jnp.stack([sq(g["w_k"]).reshape(D_MODEL, SB_HEADS // 2, 128),
                          sq(g["w_v"]).reshape(D_MODEL, SB_HEADS // 2, 128)], axis=2).reshape(D_MODEL, 2 * D_MODEL)
        return dict(w_q=sq(g["w_q"]), w_kv=w_kv, w_o=sq(g["w_o"]))

    pending = []

    def put_g(group, g):
        if group in ("ffn0", "ffn1"):
            keys = [("ffn_up_w", int(group[-1])), ("ffn_down_w", int(group[-1]))]
            blocks = [g["up"].reshape(N_DEV, n_up, D_MODEL), g["down"].reshape(N_DEV, rs, D_MODEL)]
        elif group == "attn":
            keys = [(n, None) for n in ("w_o", "w_q", "w_k", "w_v")]
            blocks = [g[n].reshape(N_DEV, D_MODEL // N_DEV, D_MODEL) for n, _ in keys]
        elif group == "ssm_out":
            keys = [("ssm_out_w", None)]
            blocks = [g["ssm_out_w"].reshape(N_DEV, D_INNER // N_DEV, D_MODEL)]
        else:
            keys = [("ssm_in_w", None)]
            blocks = [g["ssm_in_w"].reshape(N_DEV, n_in, D_MODEL)]
        h = _push_start(blocks, scatter=True, name=f"exchange_{group}_start")
        pending.append((group, keys, h))
        return h["token"]

    loss_row, dx, f = _local_step(x.reshape(T, D_MODEL), loss_target.reshape(T, D_MODEL), get_w, put_g)

    small_names = _SMALL_REPL + _SMALL_SHARDED
    small_full = _pack_small([f[n] for n in small_names] + [loss_row[0, 0:1]])
    small_bcast = jnp.broadcast_to(small_full[None], (N_DEV,) + small_full.shape)
    h_small = _push_start([small_bcast], scatter=True, name="exchange_small_start")
    tok = h_small["token"]

    arrived, res = {}, {}
    after = dx
    for group, keys, h in pending:
        srcs, lands = _push_wait(h, after, name=f"exchange_{group}_wait")
        arrived.update(zip(keys, with_own(srcs, lands, True)))
        for n in _BIG:
            layered = (n, 0) in arrived or (n, 1) in arrived
            if n in res or not ((n, None) in arrived or ((n, 0) in arrived and (n, 1) in arrived)):
                continue
            parts = [arrived[(n, 0)], arrived[(n, 1)]] if layered else arrived[(n, None)]
            w2, m2, v2 = ((t2d if n in _T else _as2d)(a[n]) for a in (p, mom, var))
            if not res:
                w2 = _tie(w2, tok)
            tiles = {"ffn_down_w": dict(tr=rs), "ffn_up_w": dict(tr=n_up // 2), "ssm_in_w": dict(tr=n_in, tc=256)}
            res[n] = _adamw(parts, w2, m2, v2, name=f"adamw_{n}", **tiles.get(n, dict(tr=256)))
            after = res[n][0]
    srcs, lands = _push_wait(h_small, after, name="exchange_small_wait")
    small_parts = with_own(srcs, lands, True)[0]
    out_g, out_d, out_m, out_v = {}, {}, {}, {}
    for n in _BIG:
        out_g[n], out_d[n], out_m[n], out_v[n] = (from_t2d(t, p[n]) if n in _T else t.reshape(p[n].shape) for t in res[n])

    zero = jnp.zeros_like(small_full)
    g_small_sum = _adamw(small_parts, zero, zero, zero, name="sum_small_grads", tr=small_full.shape[0])[0]
    *small_sums, loss_sum = _unpack_small(g_small_sum, [f[n].shape for n in small_names] + [(1,)])
    loss = loss_sum[0]
    g_small = dict(zip(small_names, small_sums))
    for n in _SMALL_SHARDED:
        width = p[n].shape[-1]
        g_small[n] = lax.dynamic_slice_in_dim(g_small[n], me * width, width, axis=g_small[n].ndim - 1)
    sw = _pack_small([p[n] for n in small_names])
    sm = _pack_small([mom[n] for n in small_names])
    sv = _pack_small([var[n] for n in small_names])
    sg = _pack_small([g_small[n] for n in small_names])
    _, d, nm, nv = _adamw(sg[None], sw, sm, sv, name="adamw_small", tr=sw.shape[0])
    shard_shapes = [p[n].shape for n in small_names]
    for n, dd, mm, vv in zip(small_names, _unpack_small(d, shard_shapes), _unpack_small(nm, shard_shapes),
                             _unpack_small(nv, shard_shapes)):
        out_g[n] = g_small[n].reshape(p[n].shape)
        out_d[n], out_m[n], out_v[n] = dd, mm, vv

    return (loss, dx.reshape(x.shape), *[out_g[n] for n in _WEIGHTS], *[out_d[n] for n in _WEIGHTS],
            *[out_m[n] for n in _WEIGHTS], *[out_v[n] for n in _WEIGHTS])
```

```python
import functools
import math

import jax
import jax.numpy as jnp
from jax import lax
from jax.experimental import pallas as pl
from jax.experimental.pallas import tpu as pltpu

F32 = jnp.float32
BF16 = jnp.bfloat16
EPS = 1e-6

D_MODEL = 1024
D_INNER = 2048
SSM_HEADS = 32
SSM_GROUPS = 4
SSM_STATE = 128
SSM_CONV = 4
SSM_CHUNK = 128
GN = SSM_GROUPS * SSM_STATE
CONV_DIM = D_INNER + 2 * GN
IN_PROJ_DIM = D_INNER + CONV_DIM + SSM_HEADS
IN_PROJ_PAD = 5376
SB_HEADS = 16
SB_HEAD_DIM = 64
SB_BLOCK = 128
D_FF = 2816
FFN_CONV = 3
N_DEV = 8

ADAM_LR = 0.001
ADAM_B1 = 0.9
ADAM_B2 = 0.999
ADAM_EPS = 1e-08
ADAM_WD = 0.01
ADAM_STEP = 10

_MESH = pl.DeviceIdType.MESH
_NT = (((1,), (1,)), ((), ()))
_TN = (((0,), (0,)), ((), ()))
_ANY = pl.BlockSpec(memory_space=pl.ANY)


def _cparams(sem, vmem_mb=48):
    return pltpu.CompilerParams(dimension_semantics=sem, vmem_limit_bytes=vmem_mb * 1024 * 1024)


def _sigmoid(x):
    return 0.5 * jnp.tanh(0.5 * x) + 0.5


def _softplus(x):
    return jnp.maximum(x, 0.0) + jnp.log(1.0 + jnp.exp(-jnp.abs(x)))


def _rms_fwd(xv, w):
    r = lax.rsqrt(jnp.mean(xv * xv, axis=-1, keepdims=True) + EPS)
    return xv * r * w


def _mm_fwd(x, w, *, name, norm_w=None, residual=None, out_dtype=F32, tm=512, tn=512, halves=False, w_t=False):
    M, K = x.shape
    N = w.shape[0] if w_t else w.shape[1]
    tm, tn = min(tm, M), min(tn, N)
    assert M % tm == 0 and N % tn == 0, (name, M, N, tm, tn)
    if halves:
        nbh = N // 2 // tn
        assert N // 2 % tn == 0
        out_spec = pl.BlockSpec((None, tm, tn), lambda i, j: (lax.div(j, nbh), i, lax.rem(j, nbh)))
        out_shape = jax.ShapeDtypeStruct((2, M, N // 2), out_dtype)
    else:
        out_spec = pl.BlockSpec((tm, tn), lambda i, j: (i, j))
        out_shape = jax.ShapeDtypeStruct((M, N), out_dtype)
    has_norm, has_res = norm_w is not None, residual is not None

    def body(*refs):
        x_ref, w_ref = refs[0], refs[1]
        p = 2
        nw_ref = r_ref = None
        if has_norm:
            nw_ref = refs[p]
            p += 1
        if has_res:
            r_ref = refs[p]
            p += 1
        o_ref = refs[p]
        xv = x_ref[...]
        if has_norm:
            xv = _rms_fwd(xv.astype(F32), nw_ref[...])
        acc = lax.dot_general(xv.astype(BF16), w_ref[...], _NT if w_t else (((1,), (0,)), ((), ())),
                              preferred_element_type=F32)
        if has_res:
            acc = acc + r_ref[...]
        o_ref[...] = acc.astype(out_dtype)

    w_spec = pl.BlockSpec((tn, K), lambda i, j: (j, 0)) if w_t else pl.BlockSpec((K, tn), lambda i, j: (0, j))
    in_specs = [pl.BlockSpec((tm, K), lambda i, j: (i, 0)), w_spec]
    args = [x, w]
    if has_norm:
        in_specs.append(pl.BlockSpec((1, K), lambda i, j: (0, 0)))
        args.append(norm_w.reshape(1, K))
    if has_res:
        in_specs.append(pl.BlockSpec((tm, tn), lambda i, j: (i, j)))
        args.append(residual)
    return pl.pallas_call(
        body, name=name, grid=(M // tm, N // tn), in_specs=in_specs,
        out_specs=out_spec, out_shape=out_shape,
        compiler_params=_cparams(("parallel", "parallel")))(*args)


def _mm_nt(dy, w, *, name, epi=None, out_dtype=F32, tm=512, tn=512, tk=512, w_t=False):
    halves = dy.ndim == 3
    M, K = (dy.shape[1], 2 * dy.shape[2]) if halves else dy.shape
    N = w.shape[1] if w_t else w.shape[0]
    tm, tk = min(tm, M), min(tk, K)
    tn = N if epi is not None else min(tn, N)
    assert M % tm == 0 and N % tn == 0 and K % tk == 0, (name, M, N, K, tm, tn, tk)
    nk = K // tk
    has_epi = epi is not None

    def body(*refs):
        if has_epi:
            dy_ref, w_ref, h_ref, nw_ref, r_ref, o_ref, dnw_ref, acc_ref = refs
        else:
            dy_ref, w_ref, o_ref, acc_ref = refs
        i = pl.program_id(0)
        k = pl.program_id(2)

        @pl.when(k == 0)
        def _():
            acc_ref[...] = jnp.zeros_like(acc_ref)

        acc_ref[...] += lax.dot_general(dy_ref[...].astype(BF16), w_ref[...], (((1,), (0,)), ((), ())) if w_t else _NT,
                                        preferred_element_type=F32)

        @pl.when(k == nk - 1)
        def _():
            du = acc_ref[...]
            if has_epi:
                hv = h_ref[...]
                r = lax.rsqrt(jnp.mean(hv * hv, axis=-1, keepdims=True) + EPS)
                xhat = hv * r
                dxh = du * nw_ref[...]
                dx = r * (dxh - xhat * jnp.mean(dxh * xhat, axis=-1, keepdims=True))
                o_ref[...] = (r_ref[...] + dx).astype(out_dtype)
                contrib = jnp.sum(du * xhat, axis=0, keepdims=True)

                @pl.when(i == 0)
                def _():
                    dnw_ref[...] = contrib

                @pl.when(i > 0)
                def _():
                    dnw_ref[...] += contrib
            else:
                o_ref[...] = du.astype(out_dtype)

    if halves:
        nkh = K // 2 // tk
        assert K // 2 % tk == 0
        dy_spec = pl.BlockSpec((None, tm, tk), lambda i, j, k: (lax.div(k, nkh), i, lax.rem(k, nkh)))
    else:
        dy_spec = pl.BlockSpec((tm, tk), lambda i, j, k: (i, k))
    w_spec = pl.BlockSpec((tk, tn), lambda i, j, k: (k, j)) if w_t else pl.BlockSpec((tn, tk), lambda i, j, k: (j, k))
    in_specs = [dy_spec, w_spec]
    args = [dy, w]
    out_specs = [pl.BlockSpec((tm, tn), lambda i, j, k: (i, j))]
    out_shape = [jax.ShapeDtypeStruct((M, N), out_dtype)]
    if has_epi:
        h, nw, res = epi
        in_specs += [pl.BlockSpec((tm, N), lambda i, j, k: (i, 0)), pl.BlockSpec((1, N), lambda i, j, k: (0, 0)),
                     pl.BlockSpec((tm, N), lambda i, j, k: (i, 0))]
        args += [h, nw.reshape(1, N), res]
        out_specs.append(pl.BlockSpec((1, N), lambda i, j, k: (0, 0)))
        out_shape.append(jax.ShapeDtypeStruct((1, N), F32))
    outs = pl.pallas_call(
        body, name=name, grid=(M // tm, N // tn, nk), in_specs=in_specs, out_specs=out_specs, out_shape=out_shape,
        scratch_shapes=[pltpu.VMEM((tm, tn), F32)],
        compiler_params=_cparams(("arbitrary", "arbitrary", "arbitrary")))(*args)
    return (outs[0], outs[1]) if has_epi else outs[0]


def _mm_tn(x, dy, *, name, norm_w=None, out_dtype=BF16, tk1=1024, tn=512, tt=512):
    T, K1 = x.shape
    halves = dy.ndim == 3
    N = 2 * dy.shape[2] if halves else dy.shape[1]
    tk1, tn, tt = min(tk1, K1), min(tn, N), min(tt, T)
    has_norm = norm_w is not None
    assert K1 % tk1 == 0 and N % tn == 0 and T % tt == 0, (name, K1, N, T, tk1, tn, tt)
    assert not has_norm or tk1 == K1
    nt = T // tt

    def body(*refs):
        if has_norm:
            x_ref, dy_ref, nw_ref, o_ref, acc_ref = refs
        else:
            x_ref, dy_ref, o_ref, acc_ref = refs
        t = pl.program_id(2)

        @pl.when(t == 0)
        def _():
            acc_ref[...] = jnp.zeros_like(acc_ref)

        xv = x_ref[...]
        if has_norm:
            xv = _rms_fwd(xv.astype(F32), nw_ref[...])
        acc_ref[...] += lax.dot_general(xv.astype(BF16), dy_ref[...].astype(BF16), _TN, preferred_element_type=F32)

        @pl.when(t == nt - 1)
        def _():
            o_ref[...] = acc_ref[...].astype(out_dtype)

    if halves:
        nbh = N // 2 // tn
        assert N // 2 % tn == 0
        dy_spec = pl.BlockSpec((None, tt, tn), lambda a, b, t: (lax.div(b, nbh), t, lax.rem(b, nbh)))
    else:
        dy_spec = pl.BlockSpec((tt, tn), lambda a, b, t: (t, b))
    in_specs = [pl.BlockSpec((tt, tk1), lambda a, b, t: (t, a)), dy_spec]
    args = [x, dy]
    if has_norm:
        in_specs.append(pl.BlockSpec((1, K1), lambda a, b, t: (0, 0)))
        args.append(norm_w.reshape(1, K1))
    return pl.pallas_call(
        body, name=name, grid=(K1 // tk1, N // tn, nt), in_specs=in_specs,
        out_specs=pl.BlockSpec((tk1, tn), lambda a, b, t: (a, b)),
        out_shape=jax.ShapeDtypeStruct((K1, N), out_dtype),
        scratch_shapes=[pltpu.VMEM((tk1, tn), F32)],
        compiler_params=_cparams(("parallel", "parallel", "arbitrary")))(*args)


def _mm_tn_t(dy, x, *, name, norm_w, out_dtype=BF16, tn=1408, tt=1024, vmem_mb=48):
    T, K1 = x.shape
    halves = dy.ndim == 3
    N = 2 * dy.shape[2] if halves else dy.shape[1]
    tn, tt = min(tn, N), min(tt, T)
    assert N % tn == 0 and T % tt == 0, (name, N, T, tn, tt)
    nt = T // tt

    def body(dy_ref, x_ref, nw_ref, o_ref, acc_ref):
        t = pl.program_id(1)

        @pl.when(t == 0)
        def _():
            acc_ref[...] = jnp.zeros_like(acc_ref)

        xn = _rms_fwd(x_ref[...].astype(F32), nw_ref[...]).astype(BF16)
        acc_ref[...] += lax.dot_general(dy_ref[...].astype(BF16), xn, _TN, preferred_element_type=F32)

        @pl.when(t == nt - 1)
        def _():
            o_ref[...] = acc_ref[...].astype(out_dtype)

    if halves:
        nbh = N // 2 // tn
        assert N // 2 % tn == 0
        dy_spec = pl.BlockSpec((None, tt, tn), lambda b, t: (lax.div(b, nbh), t, lax.rem(b, nbh)))
    else:
        dy_spec = pl.BlockSpec((tt, tn), lambda b, t: (t, b))
    return pl.pallas_call(
        body, name=name, grid=(N // tn, nt),
        in_specs=[dy_spec, pl.BlockSpec((tt, K1), lambda b, t: (t, 0)), pl.BlockSpec((1, K1), lambda b, t: (0, 0))],
        out_specs=pl.BlockSpec((tn, K1), lambda b, t: (b, 0)),
        out_shape=jax.ShapeDtypeStruct((N, K1), out_dtype),
        scratch_shapes=[pltpu.VMEM((tn, K1), F32)],
        compiler_params=_cparams(("parallel", "arbitrary"), vmem_mb))(dy, x, norm_w.reshape(1, K1))


def _shift_down(xb, prev8, j):
    main = pltpu.roll(xb, j, 0)
    head = pltpu.roll(xb[0:8], j, 0)
    ph = pltpu.roll(prev8, j, 0)
    row8 = lax.broadcasted_iota(jnp.int32, head.shape, 0)
    head = jnp.where(row8 < j, ph, head)
    return jnp.concatenate([head, main[8:]], axis=0)


def _shift_up(xb, next8, j):
    tt = xb.shape[0]
    main = pltpu.roll(xb, tt - j, 0)
    tail = pltpu.roll(xb[tt - 8:tt], 8 - j, 0)
    nh = pltpu.roll(next8, 8 - j, 0)
    row8 = lax.broadcasted_iota(jnp.int32, tail.shape, 0)
    tail = jnp.where(row8 + j >= 8, nh, tail)
    return jnp.concatenate([main[:tt - 8], tail], axis=0)


def _conv_hid(xb, prev8, w, b_row, K):
    out = b_row
    shifted = []
    for j in range(K):
        sh = K - 1 - j
        xs = xb if sh == 0 else _shift_down(xb, prev8, sh)
        shifted.append(xs)
        out = out + xs * w[j:j + 1, :]
    return out, shifted


def _prev_idx(i, nb8):
    return jnp.maximum(i * nb8 - 1, 0)


def _ssm_conv_fwd(zx, w, b, *, name, tt=512, tc=512):
    T = zx.shape[0]
    tt = min(tt, T)
    C, K = CONV_DIM, SSM_CONV
    cb0, nb8 = D_INNER // tc, tt // 8

    def body(x_ref, p_ref, w_ref, b_ref, o_ref):
        first = (pl.program_id(1) > 0).astype(F32)
        hid, _ = _conv_hid(x_ref[...], p_ref[...] * first, w_ref[...], b_ref[...], K)
        o_ref[...] = hid * _sigmoid(hid)

    return pl.pallas_call(
        body, name=name, grid=(C // tc, T // tt),
        in_specs=[pl.BlockSpec((tt, tc), lambda c, i: (i, c + cb0)),
                  pl.BlockSpec((8, tc), lambda c, i: (_prev_idx(i, nb8), c + cb0)),
                  pl.BlockSpec((K, tc), lambda c, i: (0, c)), pl.BlockSpec((1, tc), lambda c, i: (0, c))],
        out_specs=pl.BlockSpec((tt, tc), lambda c, i: (i, c)),
        out_shape=jax.ShapeDtypeStruct((T, C), F32),
        compiler_params=_cparams(("parallel", "parallel")))(zx, zx, w, b)


def _ssm_conv_bwd_pre(zx, w, b, dout, *, name, tt=512, tc=512):
    T = zx.shape[0]
    tt = min(tt, T)
    C, K = CONV_DIM, SSM_CONV
    cb0, nb8 = D_INNER // tc, tt // 8

    def body(x_ref, p_ref, w_ref, b_ref, d_ref, dh_ref, dw_ref, db_ref):
        t = pl.program_id(1)
        first = (t > 0).astype(F32)
        hid, shifted = _conv_hid(x_ref[...], p_ref[...] * first, w_ref[...], b_ref[...], K)
        sg = _sigmoid(hid)
        dh = d_ref[...] * (sg * (1.0 + hid * (1.0 - sg)))
        dh_ref[...] = dh

        @pl.when(t == 0)
        def _():
            dw_ref[...] = jnp.zeros_like(dw_ref)
            db_ref[...] = jnp.zeros_like(db_ref)

        db_ref[...] += jnp.sum(dh, axis=0, keepdims=True)
        for j in range(K):
            dw_ref[j:j + 1, :] += jnp.sum(dh * shifted[j], axis=0, keepdims=True)

    return pl.pallas_call(
        body, name=name, grid=(C // tc, T // tt),
        in_specs=[pl.BlockSpec((tt, tc), lambda c, i: (i, c + cb0)),
                  pl.BlockSpec((8, tc), lambda c, i: (_prev_idx(i, nb8), c + cb0)),
                  pl.BlockSpec((K, tc), lambda c, i: (0, c)), pl.BlockSpec((1, tc), lambda c, i: (0, c)),
                  pl.BlockSpec((tt, tc), lambda c, i: (i, c))],
        out_specs=[pl.BlockSpec((tt, tc), lambda c, i: (i, c)), pl.BlockSpec((K, tc), lambda c, i: (0, c)),
                   pl.BlockSpec((1, tc), lambda c, i: (0, c))],
        out_shape=[jax.ShapeDtypeStruct((T, C), F32), jax.ShapeDtypeStruct((K, C), F32),
                   jax.ShapeDtypeStruct((1, C), F32)],
        compiler_params=_cparams(("parallel", "arbitrary")))(zx, zx, w, b, dout)


def _put_cols(buf, src, col0, *, name, tt=512):
    T, C = src.shape
    tt = min(tt, T)

    def body(s_ref, _, o_ref):
        o_ref[...] = s_ref[...]

    return pl.pallas_call(
        body, name=name, grid=(T // tt,),
        in_specs=[pl.BlockSpec((tt, C), lambda i: (i, 0)), _ANY],
        out_specs=pl.BlockSpec((tt, C), lambda i: (i, col0 // C)),
        out_shape=jax.ShapeDtypeStruct(buf.shape, buf.dtype), input_output_aliases={1: 0},
        compiler_params=_cparams(("parallel",)))(src, buf)


def _conv_bwd_in(dh, w, *, name, K, tt=512, tc=512, out_dtype=BF16, into=None):
    T, C = dh.shape
    tt = min(tt, T)
    nb8, nT = tt // 8, T // tt
    last8 = T // 8 - 1
    cb0 = 0 if into is None else into[1] // tc

    def body(d_ref, n_ref, w_ref, *rest):
        o_ref = rest[-1]
        notlast = (pl.program_id(1) < nT - 1).astype(F32)
        d = d_ref[...]
        nxt = n_ref[...] * notlast
        w_ = w_ref[...]
        acc = d * w_[K - 1:K, :]
        for sh in range(1, K):
            acc = acc + _shift_up(d, nxt, sh) * w_[K - 1 - sh:K - sh, :]
        o_ref[...] = acc.astype(out_dtype)

    in_specs = [pl.BlockSpec((tt, tc), lambda c, i: (i, c)),
                pl.BlockSpec((8, tc), lambda c, i: (jnp.minimum((i + 1) * nb8, last8), c)),
                pl.BlockSpec((K, tc), lambda c, i: (0, c))]
    args = [dh, dh, w]
    if into is None:
        out_shape, alias = jax.ShapeDtypeStruct((T, C), out_dtype), {}
    else:
        assert into[0].dtype == out_dtype and into[1] % tc == 0
        in_specs.append(_ANY)
        args.append(into[0])
        out_shape, alias = jax.ShapeDtypeStruct(into[0].shape, out_dtype), {3: 0}
    return pl.pallas_call(
        body, name=name, grid=(C // tc, nT), in_specs=in_specs,
        out_specs=pl.BlockSpec((tt, tc), lambda c, i: (i, c + cb0)),
        out_shape=out_shape, input_output_aliases=alias,
        compiler_params=_cparams(("parallel", "parallel")))(*args)


def _ffn_conv_fwd3(a3, w, b, *, name, tt=256, tc=1408):
    T = a3.shape[1]
    tt = min(tt, T)
    K, nbh, n16 = FFN_CONV, D_FF // tc, tt // 16

    def body(a_ref, p_ref, wg_ref, wv_ref, bg_ref, bv_ref, o_ref):
        first = (pl.program_id(1) > 0).astype(F32)
        a = a_ref[...].astype(F32)
        prev = p_ref[...].astype(F32)[:, 8:16, :] * first
        hg, _ = _conv_hid(a[0], prev[0], wg_ref[...], bg_ref[...], K)
        hv, _ = _conv_hid(a[1], prev[1], wv_ref[...], bv_ref[...], K)
        o_ref[...] = (hg * _sigmoid(hg) * hv).astype(BF16)

    return pl.pallas_call(
        body, name=name, grid=(nbh, T // tt),
        in_specs=[pl.BlockSpec((2, tt, tc), lambda c, i: (0, i, c)),
                  pl.BlockSpec((2, 16, tc), lambda c, i: (0, _prev_idx(i, n16), c)),
                  pl.BlockSpec((K, tc), lambda c, i: (0, c)), pl.BlockSpec((K, tc), lambda c, i: (0, c + nbh)),
                  pl.BlockSpec((1, tc), lambda c, i: (0, c)), pl.BlockSpec((1, tc), lambda c, i: (0, c + nbh))],
        out_specs=pl.BlockSpec((tt, tc), lambda c, i: (i, c)),
        out_shape=jax.ShapeDtypeStruct((T, D_FF), BF16),
        compiler_params=_cparams(("parallel", "parallel")))(a3, a3, w, w, b, b)


def _ffn_conv_bwd3(a3, w, b, dp, *, name, tt=256, tc=1408):
    T = a3.shape[1]
    tt = min(tt, T)
    K, nbh, n16 = FFN_CONV, D_FF // tc, tt // 16

    def body(a_ref, p_ref, wg_ref, wv_ref, bg_ref, bv_ref, dp_ref, dh_ref, dw_ref, db_ref):
        t = pl.program_id(1)
        first = (t > 0).astype(F32)
        a = a_ref[...].astype(F32)
        prev = p_ref[...].astype(F32)[:, 8:16, :] * first
        hg, sh_g = _conv_hid(a[0], prev[0], wg_ref[...], bg_ref[...], K)
        hv, sh_v = _conv_hid(a[1], prev[1], wv_ref[...], bv_ref[...], K)
        sg = _sigmoid(hg)
        d = dp_ref[...].astype(F32)
        dhg = d * hv * (sg * (1.0 + hg * (1.0 - sg)))
        dhv = d * (hg * sg)
        dh_ref[0] = dhg.astype(BF16)
        dh_ref[1] = dhv.astype(BF16)

        @pl.when(t == 0)
        def _():
            dw_ref[...] = jnp.zeros_like(dw_ref)
            db_ref[...] = jnp.zeros_like(db_ref)

        db_ref[0] += jnp.sum(dhg, axis=0, keepdims=True)
        db_ref[1] += jnp.sum(dhv, axis=0, keepdims=True)
        for j in range(K):
            dw_ref[0, j:j + 1, :] += jnp.sum(dhg * sh_g[j], axis=0, keepdims=True)
            dw_ref[1, j:j + 1, :] += jnp.sum(dhv * sh_v[j], axis=0, keepdims=True)

    return pl.pallas_call(
        body, name=name, grid=(nbh, T // tt),
        in_specs=[pl.BlockSpec((2, tt, tc), lambda c, i: (0, i, c)),
                  pl.BlockSpec((2, 16, tc), lambda c, i: (0, _prev_idx(i, n16), c)),
                  pl.BlockSpec((K, tc), lambda c, i: (0, c)), pl.BlockSpec((K, tc), lambda c, i: (0, c + nbh)),
                  pl.BlockSpec((1, tc), lambda c, i: (0, c)), pl.BlockSpec((1, tc), lambda c, i: (0, c + nbh)),
                  pl.BlockSpec((tt, tc), lambda c, i: (i, c))],
        out_specs=[pl.BlockSpec((2, tt, tc), lambda c, i: (0, i, c)), pl.BlockSpec((2, K, tc), lambda c, i: (0, 0, c)),
                   pl.BlockSpec((2, 1, tc), lambda c, i: (0, 0, c))],
        out_shape=[jax.ShapeDtypeStruct((2, T, D_FF), BF16), jax.ShapeDtypeStruct((2, K, D_FF), F32),
                   jax.ShapeDtypeStruct((2, 1, D_FF), F32)],
        compiler_params=_cparams(("parallel", "arbitrary")))(a3, a3, w, w, b, b, dp)


def _conv_bwd_in3(dh3, w, *, name, K, tt=256, tc=1408):
    H, T, C = dh3.shape
    tt = min(tt, T)
    nb, n16, nT = C // tc, tt // 16, T // tt
    last16 = T // 16 - 1

    def body(d_ref, n_ref, w_ref, o_ref):
        notlast = (pl.program_id(2) < nT - 1).astype(F32)
        d = d_ref[...].astype(F32)
        nxt = n_ref[...].astype(F32)[0:8, :] * notlast
        w_ = w_ref[...]
        acc = d * w_[K - 1:K, :]
        for sh in range(1, K):
            acc = acc + _shift_up(d, nxt, sh) * w_[K - 1 - sh:K - sh, :]
        o_ref[...] = acc.astype(BF16)

    return pl.pallas_call(
        body, name=name, grid=(H, nb, nT),
        in_specs=[pl.BlockSpec((None, tt, tc), lambda h, c, i: (h, i, c)),
                  pl.BlockSpec((None, 16, tc), lambda h, c, i: (h, jnp.minimum((i + 1) * n16, last16), c)),
                  pl.BlockSpec((K, tc), lambda h, c, i: (0, h * nb + c))],
        out_specs=pl.BlockSpec((None, tt, tc), lambda h, c, i: (h, i, c)),
        out_shape=jax.ShapeDtypeStruct((H, T, C), BF16),
        compiler_params=_cparams(("parallel", "parallel", "parallel")))(dh3, dh3, w)


def _cumsum_rows(x):
    L = x.shape[0]
    row = lax.broadcasted_iota(jnp.int32, x.shape, 0)
    k = 1
    while k < L:
        x = x + jnp.where(row >= k, pltpu.roll(x, k, 0), 0.0)
        k *= 2
    return x


def _rcumsum_rows(x):
    L = x.shape[0]
    row = lax.broadcasted_iota(jnp.int32, x.shape, 0)
    k = 1
    while k < L:
        x = x + jnp.where(row < L - k, pltpu.roll(x, L - k, 0), 0.0)
        k *= 2
    return x


def _split_terms(m, n):
    terms, rest = [], m
    for _ in range(n):
        t = rest.astype(BF16)
        terms.append(t)
        rest = rest - t.astype(F32)
    return jnp.concatenate(terms, axis=1)


def _select_dot(m, n_terms, n_out, cond):
    K = m.shape[1]
    k = lax.broadcasted_iota(jnp.int32, (K, n_out), 0)
    j = lax.broadcasted_iota(jnp.int32, (K, n_out), 1)
    sel = cond(k, j).astype(BF16)
    return jnp.dot(_split_terms(m, n_terms), jnp.concatenate([sel] * n_terms, axis=0), preferred_element_type=F32)


def _rowsum_mxu(m):
    return _select_dot(m, 2, 128, lambda k, j: k >= 0)


def _lane_block_sums(m, width):
    shift = width.bit_length() - 1
    return _select_dot(m, 2, 128, lambda k, j: j == jnp.right_shift(k, shift))


def _heads_to_pairs(m):
    return _select_dot(m, 3, 512, lambda k, j: k == jnp.right_shift(j, 6))


def _ssd_common(dt_ref, par_ref):
    par = par_ref[...]
    raw = dt_ref[...] + par[0:1, :]
    dt = _softplus(raw)
    a = -jnp.exp(par[1:2, :])
    cs = _cumsum_rows(dt * a)
    L = cs.shape[0]
    cs_last = cs[L - 1:L, :]
    return raw, dt, a, par[2:3, :], cs, cs.T, jnp.exp(cs), jnp.exp(cs_last - cs), jnp.exp(cs_last)


def _ssd_specs(nc, rev):
    L = SSM_CHUNK

    def ci(c):
        return nc - 1 - c if rev else c

    return [pl.BlockSpec((L, D_INNER), lambda c: (ci(c), 0)),
            pl.BlockSpec((L, GN), lambda c: (ci(c), D_INNER // GN)),
            pl.BlockSpec((L, GN), lambda c: (ci(c), D_INNER // GN + 1)),
            pl.BlockSpec((SSM_GROUPS, L, 128), lambda c: (0, ci(c), 0)),
            pl.BlockSpec((SSM_GROUPS, 8, 128), lambda c: (0, 0, 0)),
            pl.BlockSpec((L, D_INNER), lambda c: (ci(c), 0)),
            pl.BlockSpec((1, D_INNER), lambda c: (0, 0))], ci


def _round_robin(gens):
    live = list(gens)
    while live:
        nxt = []
        for gen in live:
            try:
                next(gen)
                nxt.append(gen)
            except StopIteration:
                pass
        live = nxt


def _group_views(g, wide, narrow, lead):
    return ([r.at[:, g * 512:(g + 1) * 512] for r in wide], [r.at[:, g * 128:(g + 1) * 128] for r in narrow],
            [r.at[g] for r in lead])


def _ssd_fwd(xbc_c, zx, dtg, par, gnw, *, name):
    T = xbc_c.shape[0]
    L = SSM_CHUNK
    nc = T // L
    in_specs, ci = _ssd_specs(nc, False)

    def body(xs_ref, b_ref, c_ref, dt_ref, par_ref, z_ref, gnw_ref, y_ref, yn_ref, st_ref, h_ref):
        @pl.when(pl.program_id(0) == 0)
        def _():
            h_ref[...] = jnp.zeros_like(h_ref)

        gens = []
        for g in range(SSM_GROUPS):
            (xs, z, gw, y, yn), (b, c), (dt, pr, st, h) = _group_views(
                g, [xs_ref, z_ref, gnw_ref, y_ref, yn_ref], [b_ref, c_ref], [dt_ref, par_ref, st_ref, h_ref])
            gens.append(group(xs, b, c, dt, pr, z, gw, y, yn, st, h))
        _round_robin(gens)

    def group(xs_ref, b_ref, c_ref, dt_ref, par_ref, z_ref, gnw_ref, y_ref, yn_ref, st_ref, h_ref):
        _, dt, _, dsk, cs, csT, ecs, eend, dec = _ssd_common(dt_ref, par_ref)
        Bb = b_ref[...].astype(BF16)
        Cb = c_ref[...].astype(BF16)
        G = lax.dot_general(Cb, Bb, _NT, preferred_element_type=F32)
        row = lax.broadcasted_iota(jnp.int32, (L, L), 0)
        col = lax.broadcasted_iota(jnp.int32, (L, L), 1)
        tril = col <= row
        lo = lax.broadcasted_iota(jnp.int32, (L, 128), 1) < 64
        lo1 = lax.broadcasted_iota(jnp.int32, (1, 128), 1) < 64
        dt_x, ecs_x, eend_x = (_heads_to_pairs(m) for m in (dt, ecs, eend))
        for pp in range(4):
            hA, hB = 2 * pp, 2 * pp + 1
            lanes = slice(pp * 128, (pp + 1) * 128)

            def sel1(m):
                return jnp.where(lo1, m[:, hA:hA + 1], m[:, hB:hB + 1])

            X = xs_ref[:, lanes]
            xd = X * dt_x[:, lanes]
            xdb = xd.astype(BF16)
            ys = []
            for h in (hA, hB):
                Lm = jnp.where(tril, jnp.exp(jnp.minimum(cs[:, h:h + 1] - csT[h:h + 1, :], 0.0)), 0.0)
                ys.append(jnp.dot((G * Lm).astype(BF16), xdb, preferred_element_type=F32))
                yield
            Hp = h_ref[pp]
            st_ref[pp] = Hp
            yoff = jnp.dot(Cb, Hp.astype(BF16), preferred_element_type=F32) * ecs_x[:, lanes]
            y_ref[:, lanes] = jnp.where(lo, ys[0], ys[1]) + yoff + sel1(dsk) * X
            S = lax.dot_general(Bb, (xd * eend_x[:, lanes]).astype(BF16), _TN, preferred_element_type=F32)
            h_ref[pp] = Hp * sel1(dec) + S
            yield
        zv = z_ref[...]
        yg = y_ref[...] * (zv * _sigmoid(zv))
        r = jnp.tile(lax.rsqrt(_rowsum_mxu(yg * yg) * (1.0 / 512) + EPS), (1, 4))
        yn_ref[...] = (yg * r * gnw_ref[...]).astype(BF16)

    return pl.pallas_call(
        body, name=name, grid=(nc,), in_specs=in_specs,
        out_specs=[pl.BlockSpec((L, D_INNER), lambda c: (c, 0)), pl.BlockSpec((L, D_INNER), lambda c: (c, 0)),
                   pl.BlockSpec((SSM_GROUPS, None, 4, 128, 128), lambda c: (0, c, 0, 0, 0))],
        out_shape=[jax.ShapeDtypeStruct((T, D_INNER), F32), jax.ShapeDtypeStruct((T, D_INNER), BF16),
                   jax.ShapeDtypeStruct((SSM_GROUPS, nc, 4, 128, 128), F32)],
        scratch_shapes=[pltpu.VMEM((SSM_GROUPS, 4, 128, 128), F32)],
        compiler_params=_cparams(("arbitrary",)))(xbc_c, xbc_c, xbc_c, dtg, par, zx, gnw)


def _ssd_bwd(xbc_c, zx, dtg, par, gnw, y, st, dyn, *, name):
    T = xbc_c.shape[0]
    L = SSM_CHUNK
    nc = T // L
    in_specs, ci = _ssd_specs(nc, True)
    in_specs += [pl.BlockSpec((L, D_INNER), lambda c: (ci(c), 0)),
                 pl.BlockSpec((SSM_GROUPS, None, 4, 128, 128), lambda c: (0, ci(c), 0, 0, 0)),
                 pl.BlockSpec((L, D_INNER), lambda c: (ci(c), 0))]

    def body(xs_ref, b_ref, c_ref, dt_ref, par_ref, z_ref, gnw_ref, y_ref, st_ref, dyn_ref,
             dxbc_ref, dz_ref, ddt_ref, dgnw_ref, dpar_ref, dh_ref):
        @pl.when(pl.program_id(0) == 0)
        def _():
            dh_ref[...] = jnp.zeros_like(dh_ref)
            dgnw_ref[...] = jnp.zeros_like(dgnw_ref)
            dpar_ref[...] = jnp.zeros_like(dpar_ref)

        dxs_ref = dxbc_ref.at[:, 0:D_INNER]
        db_ref = dxbc_ref.at[:, D_INNER:D_INNER + GN]
        dc_ref = dxbc_ref.at[:, D_INNER + GN:CONV_DIM]

        gens = []
        for g in range(SSM_GROUPS):
            (xs, z, gw, y, dyn, dxs, dz, dgw), (b, c, db, dc), (dt, pr, st, ddt, dpr, dh) = _group_views(
                g, [xs_ref, z_ref, gnw_ref, y_ref, dyn_ref, dxs_ref, dz_ref, dgnw_ref], [b_ref, c_ref, db_ref, dc_ref],
                [dt_ref, par_ref, st_ref, ddt_ref, dpar_ref, dh_ref])
            gens.append(group(xs, b, c, dt, pr, z, gw, y, st, dyn, dxs, db, dc, dz, ddt, dgw, dpr, dh))
        _round_robin(gens)

    def group(xs_ref, b_ref, c_ref, dt_ref, par_ref, z_ref, gnw_ref, y_ref, st_ref, dyn_ref,
              dxs_ref, db_ref, dc_ref, dz_ref, ddt_ref, dgnw_ref, dpar_ref, dh_ref):
        yv = y_ref[...]
        zv = z_ref[...]
        sg = _sigmoid(zv)
        sz = zv * sg
        yg = yv * sz
        r = jnp.tile(lax.rsqrt(_rowsum_mxu(yg * yg) * (1.0 / 512) + EPS), (1, 4))
        yh = yg * r
        dyn = dyn_ref[...].astype(F32)
        dgnw_ref[...] += jnp.sum(dyn * yh, axis=0, keepdims=True)
        dyh = dyn * gnw_ref[...]
        dyg = r * (dyh - yh * jnp.tile(_rowsum_mxu(dyh * yh) * (1.0 / 512), (1, 4)))
        dY_all = dyg * sz
        dz_ref[...] = (dyg * yv * (sg * (1.0 + zv * (1.0 - sg)))).astype(dz_ref.dtype)

        yield
        raw, dt, a, dsk, cs, csT, ecs, eend, dec = _ssd_common(dt_ref, par_ref)
        Bb = b_ref[...].astype(BF16)
        Cb = c_ref[...].astype(BF16)
        G = lax.dot_general(Cb, Bb, _NT, preferred_element_type=F32)
        row = lax.broadcasted_iota(jnp.int32, (L, L), 0)
        col = lax.broadcasted_iota(jnp.int32, (L, L), 1)
        tril = col <= row
        lo = lax.broadcasted_iota(jnp.int32, (L, 128), 1) < 64
        lane1 = lax.broadcasted_iota(jnp.int32, (1, 128), 1)
        lo1 = lane1 < 64
        rowl = lax.broadcasted_iota(jnp.int32, (L, 128), 0)
        dt_x, ecs_x, eend_x = (_heads_to_pairs(m) for m in (dt, ecs, eend))
        dG = jnp.zeros((L, L), F32)
        dB = jnp.zeros((L, SSM_STATE), F32)
        dC = jnp.zeros((L, SSM_STATE), F32)
        dcs_t = jnp.zeros((L, L), F32)
        tails = jnp.zeros((1, 128), F32)
        dD_row = jnp.zeros((1, 128), F32)
        v_parts, prod_parts = [], []

        def tot(m):
            return jnp.sum(jnp.sum(m, axis=0, keepdims=True), axis=1, keepdims=True)

        for pp in range(4):
            hA, hB = 2 * pp, 2 * pp + 1
            lanes = slice(pp * 128, (pp + 1) * 128)

            def sel1(m):
                return jnp.where(lo1, m[:, hA:hA + 1], m[:, hB:hB + 1])

            X = xs_ref[:, lanes]
            dY = dY_all[:, lanes]
            dtsel = dt_x[:, lanes]
            xd = X * dtsel
            xdb = xd.astype(BF16)
            dYb = dY.astype(BF16)
            Hp = st_ref[pp]
            Hb = Hp.astype(BF16)
            dHn = dh_ref[pp]
            dHb = dHn.astype(BF16)
            ecs_sel = ecs_x[:, lanes]
            eend_sel = eend_x[:, lanes]
            dxd_state = jnp.dot(Bb, dHb, preferred_element_type=F32) * eend_sel
            yoff = jnp.dot(Cb, Hb, preferred_element_type=F32) * ecs_sel
            dYe = (dY * ecs_sel).astype(BF16)
            dC = dC + lax.dot_general(dYe, Hb, _NT, preferred_element_type=F32)
            dB = dB + lax.dot_general((xd * eend_sel).astype(BF16), dHb, _NT, preferred_element_type=F32)
            dh_ref[pp] = dHn * sel1(dec) + lax.dot_general(Cb, dYe, _TN, preferred_element_type=F32)
            q = xd * dxd_state
            dyq = dY * yoff - q
            qcol = jnp.sum(q, axis=0, keepdims=True)
            hcol = jnp.sum(dHn * Hp, axis=0, keepdims=True)
            dxd_diag = []
            for h, msk, msk1 in ((hA, lo, lo1), (hB, jnp.logical_not(lo), jnp.logical_not(lo1))):
                Lm = jnp.where(tril, jnp.exp(jnp.minimum(cs[:, h:h + 1] - csT[h:h + 1, :], 0.0)), 0.0)
                M = G * Lm
                dxd_diag.append(lax.dot_general(M.astype(BF16), dYb, _TN, preferred_element_type=F32))
                dM = lax.dot_general(jnp.where(msk, dY, 0.0).astype(BF16), xdb, _NT, preferred_element_type=F32)
                dG = dG + dM * Lm
                W = dM * M
                dcs_t = dcs_t + jnp.where(row == h, jnp.sum(W, axis=0, keepdims=True), 0.0)
                v_parts.append(W + jnp.where(msk, dyq, 0.0))
                tail = (jnp.sum(jnp.where(msk1, qcol, 0.0), axis=1, keepdims=True)
                        + dec[:, h:h + 1] * jnp.sum(jnp.where(msk1, hcol, 0.0), axis=1, keepdims=True))
                tails = tails + jnp.where(lane1 == h, tail, 0.0)
                yield
            dxd = jnp.where(lo, dxd_diag[0], dxd_diag[1]) + dxd_state
            prod_parts.append(dxd * X)
            dxs_ref[:, lanes] = dxd * dtsel + sel1(dsk) * dY
            dyx = jnp.sum(dY * X, axis=0, keepdims=True)
            sA = jnp.sum(jnp.where(lo1, dyx, 0.0), axis=1, keepdims=True)
            sB = jnp.sum(dyx, axis=1, keepdims=True) - sA
            dD_row = dD_row + jnp.where(lane1 == hA, sA, 0.0) + jnp.where(lane1 == hB, sB, 0.0)
            yield
        dGb = dG.astype(BF16)
        db_ref[...] = dB + lax.dot_general(dGb, Cb, _TN, preferred_element_type=F32)
        dc_ref[...] = dC + jnp.dot(dGb, Bb, preferred_element_type=F32)
        dcs_mat = _lane_block_sums(jnp.concatenate(v_parts, axis=1), 128) + jnp.where(rowl == L - 1, tails, 0.0)
        ddt_mat = _lane_block_sums(jnp.concatenate(prod_parts, axis=1), 64)
        dad = _rcumsum_rows(dcs_mat - dcs_t.T)
        draw = (a * dad + ddt_mat) * _sigmoid(raw)
        ddt_ref[...] = draw
        dpar_ref[0:1, :] += jnp.sum(draw, axis=0, keepdims=True)
        dpar_ref[1:2, :] += jnp.sum(dt * dad, axis=0, keepdims=True) * a
        dpar_ref[2:3, :] += dD_row

    return pl.pallas_call(
        body, name=name, grid=(nc,), in_specs=in_specs,
        out_specs=[pl.BlockSpec((L, CONV_DIM), lambda c: (ci(c), 0)),
                   pl.BlockSpec((L, D_INNER), lambda c: (ci(c), 0)),
                   pl.BlockSpec((SSM_GROUPS, L, 128), lambda c: (0, ci(c), 0)),
                   pl.BlockSpec((1, D_INNER), lambda c: (0, 0)),
                   pl.BlockSpec((SSM_GROUPS, 8, 128), lambda c: (0, 0, 0))],
        out_shape=[jax.ShapeDtypeStruct((T, CONV_DIM), F32), jax.ShapeDtypeStruct((T, IN_PROJ_PAD), BF16),
                   jax.ShapeDtypeStruct((SSM_GROUPS, T, 128), F32), jax.ShapeDtypeStruct((1, D_INNER), F32),
                   jax.ShapeDtypeStruct((SSM_GROUPS, 8, 128), F32)],
        scratch_shapes=[pltpu.VMEM((SSM_GROUPS, 4, 128, 128), F32)],
        compiler_params=_cparams(("arbitrary",)))(xbc_c, xbc_c, xbc_c, dtg, par, zx, gnw, y, st, dyn)


SB_KEYS = 512
SB_SCAN = 256
SB_STRIP = 256


def _tri(width, cond):
    kk = lax.broadcasted_iota(jnp.int32, (width, width), 0)
    jj = lax.broadcasted_iota(jnp.int32, (width, width), 1)
    return cond(kk, jj).astype(BF16)


_LOG2E = 1.4426950408889634


def _softplus2(z2):
    return jnp.maximum(z2, 0.0) + jnp.log2(1.0 + jnp.exp2(-jnp.abs(z2)))


def _sba_sub_fwd(zb, c, U, mask):
    z2 = zb * _LOG2E
    s = _softplus2(z2)
    if mask is not None:
        s = jnp.where(mask, s, 0.0)
    R = c + jnp.dot(s.astype(BF16), U, preferred_element_type=F32)
    A = jnp.exp2(z2 - s - R)
    if mask is not None:
        A = jnp.where(mask, A, 0.0)
    return A.astype(BF16), R[:, 0:1] + s[:, 0:1]


def _sba_sub_bwd(zb, dAb, Lt, pc, pe, Uincl, Uexcl, mask):
    last = zb.shape[1] - 1
    z2 = zb * _LOG2E
    s = _softplus2(z2)
    g = z2 - s
    if mask is not None:
        s = jnp.where(mask, s, 0.0)
    P = pc + jnp.dot(s.astype(BF16), Uincl, preferred_element_type=F32)
    A = jnp.exp2(g - (Lt - P))
    if mask is not None:
        A = jnp.where(mask, A, 0.0)
    E = dAb * A
    PE = pe + jnp.dot(E.astype(BF16), Uexcl, preferred_element_type=F32)
    dz = E - jnp.exp2(g) * (E + PE)
    if mask is not None:
        dz = jnp.where(mask, dz, 0.0)
    return (A.astype(BF16), dz.astype(BF16), P[:, last:last + 1], PE[:, last:last + 1] + E[:, last:last + 1])


def _stack_heads(v):
    lo = lax.broadcasted_iota(jnp.int32, v.shape, 1) < 64
    zero = jnp.zeros_like(v)
    return jnp.concatenate([jnp.where(lo, v, zero), jnp.where(lo, zero, v)], axis=0)


def _unstack_heads(v):
    lo = lax.broadcasted_iota(jnp.int32, (SB_BLOCK, 128), 1) < 64
    return jnp.where(lo, v[:SB_BLOCK], v[SB_BLOCK:])


def _sba_rows(a):
    return slice(2 * a * SB_BLOCK, 2 * (a + 1) * SB_BLOCK)


def _sba_diag_case(a, b):
    Bq = SB_BLOCK
    if b * SB_SCAN >= (a + 1) * Bq:
        return "skip"
    if (b + 1) * SB_SCAN <= a * Bq:
        return "full"
    rowi = lax.broadcasted_iota(jnp.int32, (2 * Bq, SB_SCAN), 0)
    qpos = a * Bq + jnp.where(rowi >= Bq, rowi - Bq, rowi)
    return b * SB_SCAN + lax.broadcasted_iota(jnp.int32, (2 * Bq, SB_SCAN), 1) < qpos


def _sba_fwd(q, kv, *, name):
    T = q.shape[0]
    Bq = SB_BLOCK
    nsub = SB_KEYS // Bq
    nscan = SB_KEYS // SB_SCAN
    R = 2 * SB_KEYS
    assert T % SB_KEYS == 0 and SB_STRIP == 2 * Bq
    scale = 1.0 / math.sqrt(SB_HEAD_DIM)

    def body(q_ref, k_ref, v_ref, o_ref, lt_ref, z_s, a_s, c_s, acc_s):
        i = pl.program_id(1)
        U2 = _tri(SB_SCAN, lambda k, j: k > j)
        qs_all = jnp.concatenate([_stack_heads(q_ref[a * Bq:(a + 1) * Bq, :] * scale) for a in range(nsub)], axis=0)
        c_s[...] = jnp.zeros_like(c_s)
        acc_s[...] = jnp.zeros_like(acc_s)

        def scores(J, slot):
            off = pl.multiple_of(J * SB_KEYS, SB_KEYS)
            z_s[slot] = lax.dot_general(qs_all, k_ref[pl.ds(off, SB_KEYS), :], _NT, preferred_element_type=F32)

        def weights(slot, diag):
            for a in range(nsub):
                rows = _sba_rows(a)
                c = c_s[rows, :]
                for b in reversed(range(nscan)):
                    cols = slice(b * SB_SCAN, (b + 1) * SB_SCAN)
                    case = _sba_diag_case(a, b) if diag else "full"
                    if isinstance(case, str) and case == "skip":
                        a_s[slot, rows, cols] = jnp.zeros((2 * Bq, SB_SCAN), BF16)
                        continue
                    A, c = _sba_sub_fwd(z_s[slot, rows, cols], c, U2, None if isinstance(case, str) else case)
                    a_s[slot, rows, cols] = A
                c_s[rows, :] = c

        def values(J, slot):
            off = pl.multiple_of(J * SB_KEYS, SB_KEYS)
            acc_s[...] += jnp.dot(a_s[slot], v_ref[pl.ds(off, SB_KEYS), :], preferred_element_type=F32)

        scores(i, 0)
        weights(0, True)
        scores(jnp.maximum(i - 1, 0), 1)

        def two_steps(u, _):
            t = 2 * u + 1
            weights(1, False)
            scores(jnp.maximum(i - t - 1, 0), 0)
            values(i - t + 1, 0)
            weights(0, False)
            scores(jnp.maximum(i - t - 2, 0), 1)
            values(i - t, 1)
            return 0

        lax.fori_loop(0, i // 2, two_steps, 0)
        odd = lax.rem(i, 2) == 1

        @pl.when(jnp.logical_not(odd))
        def _():
            values(0, 0)

        @pl.when(odd)
        def _():
            weights(1, False)
            values(1, 0)
            values(0, 1)
        for a in range(nsub):
            o_ref[a * Bq:(a + 1) * Bq, :] = _unstack_heads(acc_s[_sba_rows(a), :]).astype(BF16)
            lt_ref[a * Bq:(a + 1) * Bq, :] = _unstack_heads(jnp.broadcast_to(c_s[_sba_rows(a), :], (2 * Bq, 128)))

    return pl.pallas_call(
        body, name=name, grid=(SB_HEADS // 2, T // SB_KEYS),
        in_specs=[pl.BlockSpec((SB_KEYS, 128), lambda p, i: (i, p)), pl.BlockSpec((T, 128), lambda p, i: (0, p)),
                  pl.BlockSpec((T, 128), lambda p, i: (0, p + SB_HEADS // 2))],
        out_specs=[pl.BlockSpec((SB_KEYS, 128), lambda p, i: (i, p)),
                   pl.BlockSpec((None, SB_KEYS, 128), lambda p, i: (p, i, 0))],
        out_shape=[jax.ShapeDtypeStruct((T, D_MODEL), BF16), jax.ShapeDtypeStruct((SB_HEADS // 2, T, 128), F32)],
        scratch_shapes=[pltpu.VMEM((2, R, SB_KEYS), F32), pltpu.VMEM((2, R, SB_KEYS), BF16),
                        pltpu.VMEM((R, 1), F32), pltpu.VMEM((R, 128), F32)],
        compiler_params=_cparams(("parallel", "parallel")))(q, kv, kv)


def _sba_bwd(q, kv, lt, do, *, name):
    T = q.shape[0]
    Bq = SB_BLOCK
    nq = T // SB_KEYS
    nsub = SB_KEYS // Bq
    nscan = SB_KEYS // SB_SCAN
    R = 2 * SB_KEYS
    assert T % SB_KEYS == 0 and SB_STRIP == 2 * Bq
    scale = 1.0 / math.sqrt(SB_HEAD_DIM)

    def body(q_ref, k_ref, v_ref, lt_ref, do_ref, dq_ref, dk_ref, dv_ref, dk_acc, dv_acc,
             z_s, da_s, a_s, dz_s, pc_s, pe_s, lt_s, dq_s):
        i = pl.program_id(1)

        @pl.when(i == 0)
        def _():
            dk_acc[...] = jnp.zeros_like(dk_acc)
            dv_acc[...] = jnp.zeros_like(dv_acc)

        Uincl = _tri(SB_SCAN, lambda k, j: k <= j)
        Uexcl = _tri(SB_SCAN, lambda k, j: k < j)
        qs, dos = [], []
        for a in range(nsub):
            rows = slice(a * Bq, (a + 1) * Bq)
            qs.append(_stack_heads(q_ref[rows, :] * scale))
            dos.append(_stack_heads(do_ref[rows, :]))
            lt_s[_sba_rows(a), :] = jnp.concatenate([lt_ref[rows, 0:1], lt_ref[rows, 64:65]], axis=0)
        qs_all = jnp.concatenate(qs, axis=0)
        dos_all = jnp.concatenate(dos, axis=0)
        pc_s[...] = jnp.zeros_like(pc_s)
        pe_s[...] = jnp.zeros_like(pe_s)
        a_s[1] = jnp.zeros((R, SB_KEYS), BF16)
        dz_s[1] = jnp.zeros((R, SB_KEYS), BF16)

        def scores(J, slot):
            off = pl.multiple_of(J * SB_KEYS, SB_KEYS)
            z_s[slot] = lax.dot_general(qs_all, k_ref[pl.ds(off, SB_KEYS), :], _NT, preferred_element_type=F32)
            da_s[slot] = lax.dot_general(dos_all, v_ref[pl.ds(off, SB_KEYS), :], _NT, preferred_element_type=F32)

        def gradients(slot, diag):
            for a in range(nsub):
                rows = _sba_rows(a)
                pc, pe, Lt = pc_s[rows, :], pe_s[rows, :], lt_s[rows, :]
                for b in range(nscan):
                    cols = slice(b * SB_SCAN, (b + 1) * SB_SCAN)
                    case = _sba_diag_case(a, b) if diag else "full"
                    if isinstance(case, str) and case == "skip":
                        a_s[slot, rows, cols] = jnp.zeros((2 * Bq, SB_SCAN), BF16)
                        dz_s[slot, rows, cols] = jnp.zeros((2 * Bq, SB_SCAN), BF16)
                        continue
                    A, dz, pc, pe = _sba_sub_bwd(z_s[slot, rows, cols], da_s[slot, rows, cols], Lt, pc, pe, Uincl, Uexcl,
                                                 None if isinstance(case, str) else case)
                    a_s[slot, rows, cols] = A
                    dz_s[slot, rows, cols] = dz
                pc_s[rows, :] = pc
                pe_s[rows, :] = pe

        def products(J, slot):
            off = pl.multiple_of(J * SB_KEYS, SB_KEYS)
            dzt = dz_s[slot]
            dk_acc[pl.ds(off, SB_KEYS), :] += lax.dot_general(dzt, qs_all, _TN, preferred_element_type=F32)
            dv_acc[pl.ds(off, SB_KEYS), :] += lax.dot_general(a_s[slot], dos_all, _TN, preferred_element_type=F32)
            dq_s[...] += jnp.dot(dzt, k_ref[pl.ds(off, SB_KEYS), :], preferred_element_type=F32)

        dq_s[...] = jnp.zeros_like(dq_s)
        scores(0, 0)

        def two_steps(u, _):
            t = 2 * u
            gradients(0, False)
            scores(t + 1, 1)
            products(jnp.maximum(t - 1, 0), 1)
            gradients(1, False)
            scores(t + 2, 0)
            products(t, 0)
            return 0

        lax.fori_loop(0, i // 2, two_steps, 0)
        odd = lax.rem(i, 2) == 1

        @pl.when(jnp.logical_not(odd))
        def _():
            gradients(0, True)
            products(jnp.maximum(i - 1, 0), 1)
            products(i, 0)

        @pl.when(odd)
        def _():
            gradients(0, False)
            scores(i, 1)
            products(jnp.maximum(i - 2, 0), 1)
            gradients(1, True)
            products(i - 1, 0)
            products(i, 1)

        for a in range(nsub):
            dq_ref[a * Bq:(a + 1) * Bq, :] = (_unstack_heads(dq_s[_sba_rows(a), :]) * scale).astype(BF16)

        @pl.when(i == nq - 1)
        def _():
            dk_ref[...] = dk_acc[...].astype(BF16)
            dv_ref[...] = dv_acc[...].astype(BF16)

    return pl.pallas_call(
        body, name=name, grid=(SB_HEADS // 2, nq),
        in_specs=[pl.BlockSpec((SB_KEYS, 128), lambda p, i: (i, p)), pl.BlockSpec((T, 128), lambda p, i: (0, p)),
                  pl.BlockSpec((T, 128), lambda p, i: (0, p + SB_HEADS // 2)),
                  pl.BlockSpec((None, SB_KEYS, 128), lambda p, i: (p, i, 0)),
                  pl.BlockSpec((SB_KEYS, 128), lambda p, i: (i, p))],
        out_specs=[pl.BlockSpec((SB_KEYS, 128), lambda p, i: (i, p)), pl.BlockSpec((T, 128), lambda p, i: (0, p)),
                   pl.BlockSpec((T, 128), lambda p, i: (0, p))],
        out_shape=[jax.ShapeDtypeStruct((T, D_MODEL), BF16), jax.ShapeDtypeStruct((T, D_MODEL), BF16),
                   jax.ShapeDtypeStruct((T, D_MODEL), BF16)],
        scratch_shapes=[pltpu.VMEM((T, 128), F32), pltpu.VMEM((T, 128), F32),
                        pltpu.VMEM((2, R, SB_KEYS), F32), pltpu.VMEM((2, R, SB_KEYS), F32),
                        pltpu.VMEM((2, R, SB_KEYS), BF16), pltpu.VMEM((2, R, SB_KEYS), BF16),
                        pltpu.VMEM((R, 1), F32), pltpu.VMEM((R, 1), F32), pltpu.VMEM((R, 1), F32),
                        pltpu.VMEM((R, 128), F32)],
        compiler_params=_cparams(("parallel", "arbitrary")))(q, kv, kv, lt, do)


def _loss_head(h, tgt, w, *, name, tt=512):
    T, D = h.shape
    tt = min(tt, T)

    def body(h_ref, t_ref, w_ref, loss_ref, dh_ref, dw_ref):
        i = pl.program_id(0)
        hv = h_ref[...]
        wv = w_ref[...]
        r = lax.rsqrt(jnp.mean(hv * hv, axis=-1, keepdims=True) + EPS)
        xhat = hv * r
        err = xhat * wv - t_ref[...]
        part = 0.5 * jnp.sum(jnp.mean(err * err, axis=-1, keepdims=True), axis=0, keepdims=True)
        dy = err * (1.0 / D)
        dxh = dy * wv
        dh_ref[...] = r * (dxh - xhat * jnp.mean(dxh * xhat, axis=-1, keepdims=True))
        dwc = jnp.sum(dy * xhat, axis=0, keepdims=True)

        @pl.when(i == 0)
        def _():
            loss_ref[...] = jnp.broadcast_to(part, loss_ref.shape)
            dw_ref[...] = dwc

        @pl.when(i > 0)
        def _():
            loss_ref[...] += jnp.broadcast_to(part, loss_ref.shape)
            dw_ref[...] += dwc

    return pl.pallas_call(
        body, name=name, grid=(T // tt,),
        in_specs=[pl.BlockSpec((tt, D), lambda i: (i, 0)), pl.BlockSpec((tt, D), lambda i: (i, 0)),
                  pl.BlockSpec((1, D), lambda i: (0, 0))],
        out_specs=[pl.BlockSpec((1, 128), lambda i: (0, 0)), pl.BlockSpec((tt, D), lambda i: (i, 0)),
                   pl.BlockSpec((1, D), lambda i: (0, 0))],
        out_shape=[jax.ShapeDtypeStruct((1, 128), F32), jax.ShapeDtypeStruct((T, D), F32),
                   jax.ShapeDtypeStruct((1, D), F32)],
        compiler_params=_cparams(("arbitrary",)))(h, tgt, w.reshape(1, D))


def _adamw(parts, w, m, v, *, name, tr=256, tc=None):
    plist = list(parts) if isinstance(parts, (list, tuple)) else [parts]
    P, _, C = plist[0].shape
    R = sum(a.shape[1] for a in plist)
    tr = min(tr, R)
    tc = C if tc is None else tc
    assert all(a.shape[1] % tr == 0 for a in plist) and C % tc == 0, (name, R, C, tr, tc)
    nbs = [a.shape[1] // tr for a in plist]
    offs = [sum(nbs[:l]) for l in range(len(nbs))]
    c1 = 1.0 - ADAM_B1 ** ADAM_STEP
    c2 = 1.0 - ADAM_B2 ** ADAM_STEP

    def body(*refs):
        p_refs = refs[:len(plist)]
        w_ref, m_ref, v_ref, g_ref, d_ref, nm_ref, nv_ref = refs[len(plist):]
        i = pl.program_id(0)
        g = None
        for l, p_ref in enumerate(p_refs):
            gl = p_ref[0].astype(F32)
            for k in range(1, P):
                gl = gl + p_ref[k].astype(F32)
            g = gl if g is None else jnp.where(i >= offs[l], gl, g)
        mn = ADAM_B1 * m_ref[...] + (1.0 - ADAM_B1) * g
        vn = ADAM_B2 * v_ref[...] + (1.0 - ADAM_B2) * (g * g)
        g_ref[...] = g
        nm_ref[...] = mn
        nv_ref[...] = vn
        d_ref[...] = -ADAM_LR * ((mn / c1) / (jnp.sqrt(vn / c2) + ADAM_EPS) + ADAM_WD * w_ref[...])

    spec = pl.BlockSpec((tr, tc), lambda i, j: (i, j))
    sds = jax.ShapeDtypeStruct((R, C), F32)
    return pl.pallas_call(
        body, name=name, grid=(R // tr, C // tc),
        in_specs=[pl.BlockSpec((P, tr, tc), functools.partial(lambda i, j, o, n: (0, jnp.clip(i - o, 0, n - 1), j), o=o, n=n))
                  for o, n in zip(offs, nbs)] + [spec, spec, spec],
        out_specs=[spec, spec, spec, spec], out_shape=[sds, sds, sds, sds],
        compiler_params=_cparams(("parallel", "parallel")))(*plist, w, m, v)


def _all_gather(shards, *, name):
    n = len(shards)

    def body(*refs):
        ins, outs = refs[:n], refs[n:2 * n]
        send_sems, recv_sems, local_sems = refs[2 * n:]
        x, y, c = lax.axis_index("x"), lax.axis_index("y"), lax.axis_index("c")
        me, sib = (x, y, c), (x, y, 1 - c)
        chips = [(1 - x, y), (x, 1 - y), (1 - x, 1 - y)]

        def slot(p):
            return 4 * p[0] + 2 * p[1] + p[2]

        def cp(a, k, block, to, src=None):
            dst = outs[a].at[slot(block)]
            return pltpu.make_async_remote_copy(src_ref=dst if src is None else src, dst_ref=dst,
                                                send_sem=send_sems.at[a, k], recv_sem=recv_sems.at[a, k],
                                                device_id=to, device_id_type=_MESH)

        mine = [pltpu.make_async_copy(ins[a], outs[a].at[slot(me)], local_sems.at[a]) for a in range(n)]
        for m in mine:
            m.start()
        first = []
        for a in range(n):
            first.append(cp(a, 0, me, sib, src=ins[a]))
            for j, chip in enumerate(chips):
                first.append(cp(a, 1 + j, me, (*chip, c), src=ins[a]))
        for f in first:
            f.start()
        passed = []
        for j, chip in enumerate(chips):
            for a in range(n):
                cp(a, 1 + j, (*chip, c), me).wait_recv()
                f = cp(a, 4 + j, (*chip, c), sib)
                f.start()
                passed.append(f)
        for a in range(n):
            cp(a, 0, sib, me).wait_recv()
            for j, chip in enumerate(chips):
                cp(a, 4 + j, (*chip, 1 - c), me).wait_recv()
        for f in first + passed:
            f.wait_send()
        for m in mine:
            m.wait()

    return pl.pallas_call(
        body, name=name, in_specs=[_ANY] * n, out_specs=[_ANY] * n,
        out_shape=[jax.ShapeDtypeStruct((N_DEV,) + s.shape, s.dtype) for s in shards],
        scratch_shapes=[pltpu.SemaphoreType.DMA((n, 7)), pltpu.SemaphoreType.DMA((n, 7)),
                        pltpu.SemaphoreType.DMA((n,))])(*shards)


_HBM = pl.BlockSpec(memory_space=pltpu.HBM)
_SEM = pl.BlockSpec(memory_space=pltpu.SEMAPHORE)
_EFFECT = pltpu.SideEffectType.DATAFLOW_SIDE_EFFECTING


def _peers():
    x, y, c = lax.axis_index("x"), lax.axis_index("y"), lax.axis_index("c")
    out = []
    for r in range(1, N_DEV):
        px = 1 - x if (r >> 2) & 1 else x
        py = 1 - y if (r >> 1) & 1 else y
        pc = 1 - c if r & 1 else c
        out.append(((px, py, pc), 4 * px + 2 * py + pc))
    return 4 * x + 2 * y + c, out


def _push_copy(src_ref, land_ref, send_sems, recv_sems, a, k, me, peer, peer_slot, scatter, arriving):
    src = src_ref.at[peer_slot] if scatter else src_ref
    return pltpu.make_async_remote_copy(
        src_ref=src, dst_ref=land_ref.at[peer_slot if arriving else me], send_sem=send_sems.at[a * (N_DEV - 1) + k],
        recv_sem=recv_sems.at[a * (N_DEV - 1) + k], device_id=peer, device_id_type=_MESH)


def _push_start(srcs, *, scatter, name):
    n = len(srcs)
    lands = [lax.empty(s.shape if scatter else (N_DEV,) + s.shape, s.dtype) for s in srcs]

    def body(*refs):
        src_refs, land_refs = refs[:n], refs[n:2 * n]
        send_sems, recv_sems = refs[2 * n], refs[2 * n + 1]
        token = refs[-1]
        me, peers = _peers()
        for k, (peer, slot) in enumerate(peers):
            for a in range(n):
                _push_copy(src_refs[a], land_refs[a], send_sems, recv_sems, a, k, me, peer, slot, scatter, False).start()
        token[...] = jnp.zeros_like(token)

    hbm = lambda a: pltpu.HBM(a.shape, a.dtype)
    outs = pl.pallas_call(
        body, name=name,
        out_shape=(pltpu.SemaphoreType.DMA((n * (N_DEV - 1),)), pltpu.SemaphoreType.DMA((n * (N_DEV - 1),)),
                   *[hbm(s) for s in srcs], *[hbm(l) for l in lands], jax.ShapeDtypeStruct((8, 128), F32)),
        in_specs=[_HBM] * (2 * n),
        out_specs=(_SEM, _SEM, *([_HBM] * (2 * n)), pl.BlockSpec(memory_space=pltpu.VMEM)),
        input_output_aliases={i: 2 + i for i in range(2 * n)},
        compiler_params=pltpu.CompilerParams(has_side_effects=_EFFECT),
    )(*[pltpu.with_memory_space_constraint(s, pltpu.HBM) for s in srcs],
      *[pltpu.with_memory_space_constraint(l, pltpu.HBM) for l in lands])
    return dict(send=outs[0], recv=outs[1], srcs=list(outs[2:2 + n]), lands=list(outs[2 + n:2 + 2 * n]),
                token=outs[-1], scatter=scatter, n=n)


def _push_wait(h, after, *, name):
    n, scatter = h["n"], h["scatter"]

    def body(*refs):
        src_refs, land_refs = refs[:n], refs[n:2 * n]
        send_sems, recv_sems = refs[2 * n], refs[2 * n + 1]
        me, peers = _peers()
        for k, (peer, slot) in enumerate(peers):
            for a in range(n):
                cp = _push_copy(src_refs[a], land_refs[a], send_sems, recv_sems, a, k, me, peer, slot, scatter, True)
                cp.wait_send()
                cp.wait_recv()

    hbm = lambda a: pltpu.HBM(a.shape, a.dtype)
    outs = pl.pallas_call(
        body, name=name,
        out_shape=(*[hbm(s) for s in h["srcs"]], *[hbm(l) for l in h["lands"]]),
        in_specs=[_HBM] * (2 * n) + [_SEM, _SEM, _ANY], out_specs=tuple([_HBM] * (2 * n)),
        input_output_aliases={i: i for i in range(2 * n)},
        compiler_params=pltpu.CompilerParams(has_side_effects=_EFFECT),
    )(*h["srcs"], *h["lands"], h["send"], h["recv"], after)
    return list(outs[:n]), list(outs[n:])


def _ffn_fwd(h, nw, w_up, conv_w, conv_b, w_down, tag):
    a3 = _mm_fwd(h, w_up, norm_w=nw, name=f"ffn{tag}_up", out_dtype=BF16, halves=True, w_t=True, tm=1024, tn=2816)
    p = _ffn_conv_fwd3(a3, conv_w, conv_b.reshape(1, -1), name=f"ffn{tag}_conv")
    h_out = _mm_fwd(p, w_down, residual=h, name=f"ffn{tag}_down", tm=1024, tn=512)
    return h_out, (a3, p)


def _ffn_bwd(dh, h, saved, nw, w_up, conv_w, conv_b, w_down, tag):
    a3, p = saved
    g_down = _mm_tn(p, dh, name=f"ffn{tag}_down_wg", tk1=1408, tn=1024)
    dp = _mm_nt(dh, w_down, name=f"ffn{tag}_down_dg", out_dtype=BF16, tm=512, tn=2816, tk=1024)
    dhid3, dw3, db3 = _ffn_conv_bwd3(a3, conv_w, conv_b.reshape(1, -1), dp, name=f"ffn{tag}_conv_bwd")
    da3 = _conv_bwd_in3(dhid3, conv_w, K=FFN_CONV, name=f"ffn{tag}_conv_bwd_in")
    g_up = _mm_tn_t(da3, h, norm_w=nw, name=f"ffn{tag}_up_wg", tn=2816, tt=1024, vmem_mb=58)
    dh_out, g_nw = _mm_nt(da3, w_up, epi=(h, nw, dh), name=f"ffn{tag}_up_dg", w_t=True, tm=1024, tk=1408)
    g_cw = jnp.concatenate([dw3[0], dw3[1]], axis=1)
    g_cb = jnp.concatenate([db3[0], db3[1]], axis=1)
    return dh_out, dict(norm=g_nw.reshape(-1), up=g_up, conv_w=g_cw, conv_b=g_cb.reshape(-1), down=g_down)


_BIG = ["ssm_in_w", "ssm_out_w", "w_k", "w_v", "w_q", "w_o", "ffn_up_w", "ffn_down_w"]
_SMALL_SHARDED = ["ssm_norm_w", "ssm_conv_w", "ssm_conv_b", "ssm_gate_norm_w", "ffn_conv_w"]
_SMALL_REPL = ["ssm_dt_bias", "ssm_a_log", "ssm_d", "kv_norm_w", "attn_norm_w", "ffn_norm_w", "ffn_conv_b",
               "final_norm_w"]
_WEIGHTS = ["ssm_norm_w", "ssm_in_w", "ssm_conv_w", "ssm_conv_b", "ssm_dt_bias", "ssm_a_log", "ssm_d",
            "ssm_gate_norm_w", "ssm_out_w", "kv_norm_w", "w_k", "w_v", "attn_norm_w", "w_q", "w_o", "ffn_norm_w",
            "ffn_up_w", "ffn_conv_w", "ffn_conv_b", "ffn_down_w", "final_norm_w"]


def _as2d(a):
    return a.reshape(-1, a.shape[-1])


def _cols_to_full(g):
    return g.transpose(1, 0, 2).reshape(g.shape[1], N_DEV * g.shape[2])


def _pack_small(vals):
    flat = jnp.concatenate([v.reshape(-1).astype(F32) for v in vals])
    n = flat.shape[0]
    rows = -(-n // 1024) * 8
    return jnp.pad(flat, (0, rows * 128 - n)).reshape(rows, 128)


def _unpack_small(packed, shapes):
    flat = packed.reshape(-1)
    out, off = [], 0
    for s in shapes:
        n = math.prod(s)
        out.append(flat[off:off + n].reshape(s))
        off += n
    return out


def _tie(a, token):
    return a + token[0, 0].astype(a.dtype)


def _local_step(x, tgt, get_w, put_g):
    T = x.shape[0]
    Ws = get_w("ssm", None)
    fnw, fcw, fcb = Ws["ffn_norm_w"], Ws["ffn_conv_w"], Ws["ffn_conv_b"]
    zx = _mm_fwd(x, Ws["in_w"], norm_w=Ws["ssm_norm_w"], name="ssm_in", w_t=True, tm=1024, tn=1792)
    xbc_c = _ssm_conv_fwd(zx, Ws["ssm_conv_w"], Ws["ssm_conv_b"].reshape(1, -1), name="ssm_conv")
    dt_raw = zx[:, D_INNER + CONV_DIM:IN_PROJ_DIM]
    dtg = jnp.pad(dt_raw.reshape(T, SSM_GROUPS, 8).transpose(1, 0, 2), ((0, 0), (0, 0), (0, 120)))
    par = jnp.stack([Ws["ssm_dt_bias"].reshape(SSM_GROUPS, 8), Ws["ssm_a_log"].reshape(SSM_GROUPS, 8),
                     Ws["ssm_d"].reshape(SSM_GROUPS, 8)], axis=1)
    par = jnp.pad(par, ((0, 0), (0, 5), (0, 120)))
    gnw = _tie(Ws["ssm_gate_norm_w"].reshape(1, D_INNER), get_w("rest_start", xbc_c))
    y, yn, st = _ssd_fwd(xbc_c, zx, dtg, par, gnw, name="ssd_fwd")
    W0 = get_w("ffn0", y)
    Ws["ssm_out_w"] = W0["ssm_out_w"]
    h1 = _mm_fwd(yn, Ws["ssm_out_w"], residual=x, name="ssm_out", tm=1024, tn=512)
    h2, ffn0 = _ffn_fwd(h1, fnw[0], W0["up"], fcw[0], fcb[0], W0["down"], "0")
    Wr = get_w("rest", h2)
    q = _mm_fwd(h2, Wr["w_q"], norm_w=Ws["attn_norm_w"], out_dtype=BF16, name="attn_q", tm=1024, tn=1024)
    kv = _mm_fwd(h2, Wr["w_kv"], norm_w=Ws["kv_norm_w"], out_dtype=BF16, name="attn_kv", tm=1024, tn=1024)
    o, lt = _sba_fwd(q, kv, name="sba_fwd")
    h3 = _mm_fwd(o, Wr["w_o"], residual=h2, name="attn_o", tm=1024, tn=512)
    W1 = get_w("ffn1", h3)
    h4, ffn1 = _ffn_fwd(h3, fnw[1], W1["up"], fcw[1], fcb[1], W1["down"], "1")
    loss, dh4, g_final = _loss_head(h4, tgt, Ws["final_norm_w"], name="loss_head")
    dh3, gf1 = _ffn_bwd(dh4, h3, ffn1, fnw[1], W1["up"], fcw[1], fcb[1], W1["down"], "1")
    tok = put_g("ffn1", dict(up=gf1["up"], down=gf1["down"]))
    g_wo = _mm_tn(o, dh3, name="attn_o_wg", tn=1024)
    do = _mm_nt(dh3, _tie(Wr["w_o"], tok), name="attn_o_dg", out_dtype=BF16, tn=1024, tk=1024)
    dq, dk, dv = _sba_bwd(q, kv, lt, do, name="sba_bwd")
    g_wq = _mm_tn(h2, dq, norm_w=Ws["attn_norm_w"], name="attn_q_wg", tn=1024, tt=1024)
    dh2a, g_attn_nw = _mm_nt(dq, Wr["w_q"], epi=(h2, Ws["attn_norm_w"], dh3), name="attn_q_dg", tm=1024, tk=1024)
    dkv = jnp.concatenate([dk, dv], axis=1)
    g_wkv = _mm_tn(h2, dkv, norm_w=Ws["kv_norm_w"], name="attn_kv_wg", tn=1024, tt=1024)
    dh2, g_kv_nw = _mm_nt(dkv, Wr["w_kv"], epi=(h2, Ws["kv_norm_w"], dh2a), name="attn_kv_dg", tm=1024, tk=1024)
    tok = put_g("attn", dict(w_o=g_wo, w_q=g_wq, w_k=g_wkv[:, :D_MODEL], w_v=g_wkv[:, D_MODEL:]))
    dh1, gf0 = _ffn_bwd(dh2, h1, ffn0, fnw[0], W0["up"], fcw[0], _tie(fcb[0], tok), W0["down"], "0")
    tok = put_g("ffn0", dict(up=gf0["up"], down=gf0["down"]))
    g_out = _mm_tn(yn, dh1, name="ssm_out_wg", tn=1024)
    dyn = _mm_nt(dh1, _tie(Ws["ssm_out_w"], tok), name="ssm_out_dg", out_dtype=BF16, tn=1024, tk=1024)
    tok = put_g("ssm_out", dict(ssm_out_w=g_out))
    dxbc_c, dz, ddt, g_gnw, dpar = _ssd_bwd(xbc_c, zx, dtg, par, _tie(gnw, tok), y, st, dyn, name="ssd_bwd")
    dhid, g_scw, g_scb = _ssm_conv_bwd_pre(zx, Ws["ssm_conv_w"], Ws["ssm_conv_b"].reshape(1, -1), dxbc_c,
                                           name="ssm_conv_bwd")
    dzx = _conv_bwd_in(dhid, Ws["ssm_conv_w"], K=SSM_CONV, name="ssm_conv_bwd_in", into=(dz, D_INNER))
    ddt_t = ddt[:, :, :8].transpose(1, 0, 2).reshape(T, SSM_HEADS).astype(BF16)
    dzx = _put_cols(dzx, jnp.pad(ddt_t, ((0, 0), (0, IN_PROJ_PAD - IN_PROJ_DIM))), D_INNER + CONV_DIM, name="ssm_ddt_cols")
    g_in = _mm_tn_t(dzx, x, norm_w=Ws["ssm_norm_w"], name="ssm_in_wg", tn=1792, tt=1024)
    tok = put_g("ssm_in", dict(ssm_in_w=g_in[:IN_PROJ_DIM]))
    dx, g_ssm_nw = _mm_nt(dzx, Ws["in_w"], epi=(x, _tie(Ws["ssm_norm_w"], tok), dh1), name="ssm_in_dg", w_t=True,
                          tm=1024, tk=1792)
    f = {
        "ssm_norm_w": g_ssm_nw.reshape(-1), "ssm_conv_w": g_scw,
        "ssm_conv_b": g_scb.reshape(-1), "ssm_dt_bias": dpar[:, 0, :8].reshape(-1),
        "ssm_a_log": dpar[:, 1, :8].reshape(-1), "ssm_d": dpar[:, 2, :8].reshape(-1),
        "ssm_gate_norm_w": g_gnw.reshape(-1), "kv_norm_w": g_kv_nw.reshape(-1), "attn_norm_w": g_attn_nw.reshape(-1),
        "ffn_norm_w": jnp.stack([gf0["norm"], gf1["norm"]]), "ffn_conv_w": jnp.stack([gf0["conv_w"], gf1["conv_w"]]),
        "ffn_conv_b": jnp.stack([gf0["conv_b"], gf1["conv_b"]]), "final_norm_w": g_final.reshape(-1),
    }
    return loss, dx, f


def kernel(x, ssm_norm_w, ssm_in_w, ssm_conv_w, ssm_conv_b, ssm_dt_bias, ssm_a_log, ssm_d, ssm_gate_norm_w, ssm_out_w, kv_norm_w, w_k, w_v, attn_norm_w, w_q, w_o, ffn_norm_w, ffn_up_w, ffn_conv_w, ffn_conv_b, ffn_down_w, final_norm_w, loss_target, m_ssm_norm_w, m_ssm_in_w, m_ssm_conv_w, m_ssm_conv_b, m_ssm_dt_bias, m_ssm_a_log, m_ssm_d, m_ssm_gate_norm_w, m_ssm_out_w, m_kv_norm_w, m_w_k, m_w_v, m_attn_norm_w, m_w_q, m_w_o, m_ffn_norm_w, m_ffn_up_w, m_ffn_conv_w, m_ffn_conv_b, m_ffn_down_w, m_final_norm_w, v_ssm_norm_w, v_ssm_in_w, v_ssm_conv_w, v_ssm_conv_b, v_ssm_dt_bias, v_ssm_a_log, v_ssm_d, v_ssm_gate_norm_w, v_ssm_out_w, v_kv_norm_w, v_w_k, v_w_v, v_attn_norm_w, v_w_q, v_w_o, v_ffn_norm_w, v_ffn_up_w, v_ffn_conv_w, v_ffn_conv_b, v_ffn_down_w, v_final_norm_w):
    env = dict(locals())
    p = {n: env[n] for n in _WEIGHTS}
    mom = {n: env["m_" + n] for n in _WEIGHTS}
    var = {n: env["v_" + n] for n in _WEIGHTS}
    T = x.shape[1]
    me = 4 * lax.axis_index("x") + 2 * lax.axis_index("y") + lax.axis_index("c")
    rs = D_FF // N_DEV

    def bf2(a):
        return _as2d(a).astype(BF16)

    _T = ("ssm_in_w", "ffn_up_w")

    def t2d(a):
        return jnp.swapaxes(a, -1, -2).reshape(-1, a.shape[-2])

    def from_t2d(a, like):
        return jnp.swapaxes(a.reshape(like.shape[:-2] + (like.shape[-1], like.shape[-2])), -1, -2)

    n_in, n_up = p["ssm_in_w"].shape[-1], p["ffn_up_w"].shape[-1]

    def with_own(srcs, lands, scatter):
        out = []
        for s, l in zip(srcs, lands):
            own = lax.dynamic_index_in_dim(s, me, 0, keepdims=False) if scatter else s
            out.append(lax.dynamic_update_index_in_dim(l, own, me, 0))
        return out

    a_names = ["ssm_in_w"] + _SMALL_SHARDED
    got_a = dict(zip(a_names, _all_gather([t2d(p["ssm_in_w"]).astype(BF16)] + [_as2d(p[n]) for n in _SMALL_SHARDED],
                                          name="gather_ssm")))
    ffn0_names = ["ssm_out_w", "up0", "down0"]
    rest_names = ["w_q", "w_k", "w_v", "w_o"]
    up_t = jnp.swapaxes(p["ffn_up_w"], -1, -2).astype(BF16)
    shard = {"up0": up_t[0], "down0": bf2(p["ffn_down_w"][0]), "up1": up_t[1],
             "down1": bf2(p["ffn_down_w"][1]), "w_q": bf2(p["w_q"]), "w_k": bf2(p["w_k"]), "w_v": bf2(p["w_v"]),
             "w_o": bf2(p["w_o"]), "ssm_out_w": bf2(p["ssm_out_w"])}

    def anchored(a, on):
        return a + (jnp.where(jnp.isfinite(on), on, 0.0) * 0.0).astype(a.dtype)

    h_ffn0 = _push_start([anchored(shard[ffn0_names[0]], got_a["ssm_norm_w"][0, 0, 0])]
                         + [shard[n] for n in ffn0_names[1:]], scatter=False, name="gather_ffn0_start")
    handles = {}

    def get_w(group, after):
        if group == "ssm":
            W = {n: p[n] for n in _SMALL_REPL}
            for n in ("ssm_dt_bias", "ssm_a_log", "ssm_d", "attn_norm_w"):
                W[n] = W[n].reshape(-1)
            W["in_w"] = jnp.pad(got_a["ssm_in_w"].reshape(IN_PROJ_DIM, D_MODEL), ((0, IN_PROJ_PAD - IN_PROJ_DIM), (0, 0)))
            W["ssm_norm_w"] = _tie(got_a["ssm_norm_w"].reshape(D_MODEL), h_ffn0["token"])
            W["ssm_conv_w"] = _cols_to_full(got_a["ssm_conv_w"])
            W["ssm_conv_b"] = got_a["ssm_conv_b"].reshape(CONV_DIM)
            W["ssm_gate_norm_w"] = got_a["ssm_gate_norm_w"].reshape(D_INNER)
            W["ffn_conv_w"] = _cols_to_full(got_a["ffn_conv_w"]).reshape(2, FFN_CONV, 2 * D_FF)
            return W
        if group == "rest_start":
            handles["rest"] = _push_start([anchored(shard[rest_names[0]], after[0, 0])]
                                          + [shard[n] for n in rest_names[1:]], scatter=False, name="gather_rest_start")
            handles["ffn1"] = _push_start([anchored(shard["up1"], handles["rest"]["token"][0, 0]), shard["down1"]],
                                          scatter=False, name="gather_ffn1_start")
            return handles["ffn1"]["token"]
        if group == "ffn1":
            srcs, lands = _push_wait(handles["ffn1"], after, name="gather_ffn1_wait")
            up, down = with_own(srcs, lands, False)
            return dict(up=up.reshape(2 * D_FF, D_MODEL), down=down.reshape(D_FF, D_MODEL))
        if group == "ffn0":
            srcs, lands = _push_wait(h_ffn0, after, name="gather_ffn0_wait")
            out, up, down = with_own(srcs, lands, False)
            return dict(ssm_out_w=out.reshape(D_INNER, D_MODEL), up=up.reshape(2 * D_FF, D_MODEL),
                        down=down.reshape(D_FF, D_MODEL))
        srcs, lands = _push_wait(handles["rest"], after, name="gather_rest_wait")
        g = dict(zip(rest_names, with_own(srcs, lands, False)))
        sq = lambda a: a.reshape(D_MODEL, D_MODEL)
        return dict(w_q=sq(g["w_q"]), w_kv=jnp.concatenate([sq(g["w_k"]), sq(g["w_v"])], axis=1), w_o=sq(g["w_o"]))

    pending = []

    def put_g(group, g):
        if group in ("ffn0", "ffn1"):
            keys = [("ffn_up_w", int(group[-1])), ("ffn_down_w", int(group[-1]))]
            blocks = [g["up"].reshape(N_DEV, n_up, D_MODEL), g["down"].reshape(N_DEV, rs, D_MODEL)]
        elif group == "attn":
            keys = [(n, None) for n in ("w_o", "w_q", "w_k", "w_v")]
            blocks = [g[n].reshape(N_DEV, D_MODEL // N_DEV, D_MODEL) for n, _ in keys]
        elif group == "ssm_out":
            keys = [("ssm_out_w", None)]
            blocks = [g["ssm_out_w"].reshape(N_DEV, D_INNER // N_DEV, D_MODEL)]
        else:
            keys = [("ssm_in_w", None)]
            blocks = [g["ssm_in_w"].reshape(N_DEV, n_in, D_MODEL)]
        h = _push_start(blocks, scatter=True, name=f"exchange_{group}_start")
        pending.append((group, keys, h))
        return h["token"]

    loss_row, dx, f = _local_step(x.reshape(T, D_MODEL), loss_target.reshape(T, D_MODEL), get_w, put_g)

    small_names = _SMALL_REPL + _SMALL_SHARDED
    small_full = _pack_small([f[n] for n in small_names] + [loss_row[0, 0:1]])
    small_bcast = jnp.broadcast_to(small_full[None], (N_DEV,) + small_full.shape)
    h_small = _push_start([small_bcast], scatter=True, name="exchange_small_start")
    tok = h_small["token"]

    arrived, res = {}, {}
    after = dx
    for group, keys, h in pending:
        srcs, lands = _push_wait(h, after, name=f"exchange_{group}_wait")
        arrived.update(zip(keys, with_own(srcs, lands, True)))
        for n in _BIG:
            layered = (n, 0) in arrived or (n, 1) in arrived
            if n in res or not ((n, None) in arrived or ((n, 0) in arrived and (n, 1) in arrived)):
                continue
            parts = [arrived[(n, 0)], arrived[(n, 1)]] if layered else arrived[(n, None)]
            w2, m2, v2 = ((t2d if n in _T else _as2d)(a[n]) for a in (p, mom, var))
            if not res:
                w2 = _tie(w2, tok)
            tiles = {"ffn_down_w": dict(tr=rs), "ffn_up_w": dict(tr=n_up // 2), "ssm_in_w": dict(tr=n_in, tc=256)}
            res[n] = _adamw(parts, w2, m2, v2, name=f"adamw_{n}", **tiles.get(n, dict(tr=256)))
            after = res[n][0]
    srcs, lands = _push_wait(h_small, after, name="exchange_small_wait")
    small_parts = with_own(srcs, lands, True)[0]
    out_g, out_d, out_m, out_v = {}, {}, {}, {}
    for n in _BIG:
        out_g[n], out_d[n], out_m[n], out_v[n] = (from_t2d(t, p[n]) if n in _T else t.reshape(p[n].shape) for t in res[n])

    zero = jnp.zeros_like(small_full)
    g_small_sum = _adamw(small_parts, zero, zero, zero, name="sum_small_grads", tr=small_full.shape[0])[0]
    *small_sums, loss_sum = _unpack_small(g_small_sum, [f[n].shape for n in small_names] + [(1,)])
    loss = loss_sum[0]
    g_small = dict(zip(small_names, small_sums))
    for n in _SMALL_SHARDED:
        width = p[n].shape[-1]
        g_small[n] = lax.dynamic_slice_in_dim(g_small[n], me * width, width, axis=g_small[n].ndim - 1)
    sw = _pack_small([p[n] for n in small_names])
    sm = _pack_small([mom[n] for n in small_names])
    sv = _pack_small([var[n] for n in small_names])
    sg = _pack_small([g_small[n] for n in small_names])
    _, d, nm, nv = _adamw(sg[None], sw, sm, sv, name="adamw_small", tr=sw.shape[0])
    shard_shapes = [p[n].shape for n in small_names]
    for n, dd, mm, vv in zip(small_names, _unpack_small(d, shard_shapes), _unpack_small(nm, shard_shapes),
                             _unpack_small(nv, shard_shapes)):
        out_g[n] = g_small[n].reshape(p[n].shape)
        out_d[n], out_m[n], out_v[n] = dd, mm, vv

    return (loss, dx.reshape(x.shape), *[out_g[n] for n in _WEIGHTS], *[out_d[n] for n in _WEIGHTS],
            *[out_m[n] for n in _WEIGHTS], *[out_v[n] for n in _WEIGHTS])
```

```python
import functools
import math

import jax
import jax.numpy as jnp
from jax import lax
from jax.experimental import pallas as pl
from jax.experimental.pallas import tpu as pltpu

F32 = jnp.float32
BF16 = jnp.bfloat16
EPS = 1e-6

D_MODEL = 1024
D_INNER = 2048
SSM_HEADS = 32
SSM_GROUPS = 4
SSM_STATE = 128
SSM_CONV = 4
SSM_CHUNK = 128
GN = SSM_GROUPS * SSM_STATE
CONV_DIM = D_INNER + 2 * GN
IN_PROJ_DIM = D_INNER + CONV_DIM + SSM_HEADS
IN_PROJ_PAD = 5376
SB_HEADS = 16
SB_HEAD_DIM = 64
SB_BLOCK = 128
D_FF = 2816
FFN_CONV = 3
N_DEV = 8

ADAM_LR = 0.001
ADAM_B1 = 0.9
ADAM_B2 = 0.999
ADAM_EPS = 1e-08
ADAM_WD = 0.01
ADAM_STEP = 10

_MESH = pl.DeviceIdType.MESH
_NT = (((1,), (1,)), ((), ()))
_TN = (((0,), (0,)), ((), ()))
_ANY = pl.BlockSpec(memory_space=pl.ANY)


def _cparams(sem, vmem_mb=48):
    return pltpu.CompilerParams(dimension_semantics=sem, vmem_limit_bytes=vmem_mb * 1024 * 1024)


def _sigmoid(x):
    return 0.5 * jnp.tanh(0.5 * x) + 0.5


def _softplus(x):
    return jnp.maximum(x, 0.0) + jnp.log(1.0 + jnp.exp(-jnp.abs(x)))


def _rms_fwd(xv, w):
    r = lax.rsqrt(jnp.mean(xv * xv, axis=-1, keepdims=True) + EPS)
    return xv * r * w


def _mm_fwd(x, w, *, name, norm_w=None, residual=None, out_dtype=F32, tm=512, tn=512, halves=False, w_t=False):
    M, K = x.shape
    N = w.shape[0] if w_t else w.shape[1]
    tm, tn = min(tm, M), min(tn, N)
    assert M % tm == 0 and N % tn == 0, (name, M, N, tm, tn)
    if halves:
        nbh = N // 2 // tn
        assert N // 2 % tn == 0
        out_spec = pl.BlockSpec((None, tm, tn), lambda i, j: (lax.div(j, nbh), i, lax.rem(j, nbh)))
        out_shape = jax.ShapeDtypeStruct((2, M, N // 2), out_dtype)
    else:
        out_spec = pl.BlockSpec((tm, tn), lambda i, j: (i, j))
        out_shape = jax.ShapeDtypeStruct((M, N), out_dtype)
    has_norm, has_res = norm_w is not None, residual is not None

    def body(*refs):
        x_ref, w_ref = refs[0], refs[1]
        p = 2
        nw_ref = r_ref = None
        if has_norm:
            nw_ref = refs[p]
            p += 1
        if has_res:
            r_ref = refs[p]
            p += 1
        o_ref = refs[p]
        xv = x_ref[...]
        if has_norm:
            xv = _rms_fwd(xv.astype(F32), nw_ref[...])
        acc = lax.dot_general(xv.astype(BF16), w_ref[...], _NT if w_t else (((1,), (0,)), ((), ())),
                              preferred_element_type=F32)
        if has_res:
            acc = acc + r_ref[...]
        o_ref[...] = acc.astype(out_dtype)

    w_spec = pl.BlockSpec((tn, K), lambda i, j: (j, 0)) if w_t else pl.BlockSpec((K, tn), lambda i, j: (0, j))
    in_specs = [pl.BlockSpec((tm, K), lambda i, j: (i, 0)), w_spec]
    args = [x, w]
    if has_norm:
        in_specs.append(pl.BlockSpec((1, K), lambda i, j: (0, 0)))
        args.append(norm_w.reshape(1, K))
    if has_res:
        in_specs.append(pl.BlockSpec((tm, tn), lambda i, j: (i, j)))
        args.append(residual)
    return pl.pallas_call(
        body, name=name, grid=(M // tm, N // tn), in_specs=in_specs,
        out_specs=out_spec, out_shape=out_shape,
        compiler_params=_cparams(("parallel", "parallel")))(*args)


def _mm_nt(dy, w, *, name, epi=None, out_dtype=F32, tm=512, tn=512, tk=512, w_t=False):
    halves = dy.ndim == 3
    M, K = (dy.shape[1], 2 * dy.shape[2]) if halves else dy.shape
    N = w.shape[1] if w_t else w.shape[0]
    tm, tk = min(tm, M), min(tk, K)
    tn = N if epi is not None else min(tn, N)
    assert M % tm == 0 and N % tn == 0 and K % tk == 0, (name, M, N, K, tm, tn, tk)
    nk = K // tk
    has_epi = epi is not None

    def body(*refs):
        if has_epi:
            dy_ref, w_ref, h_ref, nw_ref, r_ref, o_ref, dnw_ref, acc_ref = refs
        else:
            dy_ref, w_ref, o_ref, acc_ref = refs
        i = pl.program_id(0)
        k = pl.program_id(2)

        @pl.when(k == 0)
        def _():
            acc_ref[...] = jnp.zeros_like(acc_ref)

        acc_ref[...] += lax.dot_general(dy_ref[...].astype(BF16), w_ref[...], (((1,), (0,)), ((), ())) if w_t else _NT,
                                        preferred_element_type=F32)

        @pl.when(k == nk - 1)
        def _():
            du = acc_ref[...]
            if has_epi:
                hv = h_ref[...]
                r = lax.rsqrt(jnp.mean(hv * hv, axis=-1, keepdims=True) + EPS)
                xhat = hv * r
                dxh = du * nw_ref[...]
                dx = r * (dxh - xhat * jnp.mean(dxh * xhat, axis=-1, keepdims=True))
                o_ref[...] = (r_ref[...] + dx).astype(out_dtype)
                contrib = jnp.sum(du * xhat, axis=0, keepdims=True)

                @pl.when(i == 0)
                def _():
                    dnw_ref[...] = contrib

                @pl.when(i > 0)
                def _():
                    dnw_ref[...] += contrib
            else:
                o_ref[...] = du.astype(out_dtype)

    if halves:
        nkh = K // 2 // tk
        assert K // 2 % tk == 0
        dy_spec = pl.BlockSpec((None, tm, tk), lambda i, j, k: (lax.div(k, nkh), i, lax.rem(k, nkh)))
    else:
        dy_spec = pl.BlockSpec((tm, tk), lambda i, j, k: (i, k))
    w_spec = pl.BlockSpec((tk, tn), lambda i, j, k: (k, j)) if w_t else pl.BlockSpec((tn, tk), lambda i, j, k: (j, k))
    in_specs = [dy_spec, w_spec]
    args = [dy, w]
    out_specs = [pl.BlockSpec((tm, tn), lambda i, j, k: (i, j))]
    out_shape = [jax.ShapeDtypeStruct((M, N), out_dtype)]
    if has_epi:
        h, nw, res = epi
        in_specs += [pl.BlockSpec((tm, N), lambda i, j, k: (i, 0)), pl.BlockSpec((1, N), lambda i, j, k: (0, 0)),
                     pl.BlockSpec((tm, N), lambda i, j, k: (i, 0))]
        args += [h, nw.reshape(1, N), res]
        out_specs.append(pl.BlockSpec((1, N), lambda i, j, k: (0, 0)))
        out_shape.append(jax.ShapeDtypeStruct((1, N), F32))
    outs = pl.pallas_call(
        body, name=name, grid=(M // tm, N // tn, nk), in_specs=in_specs, out_specs=out_specs, out_shape=out_shape,
        scratch_shapes=[pltpu.VMEM((tm, tn), F32)],
        compiler_params=_cparams(("arbitrary", "arbitrary", "arbitrary")))(*args)
    return (outs[0], outs[1]) if has_epi else outs[0]


def _mm_tn(x, dy, *, name, norm_w=None, out_dtype=BF16, tk1=1024, tn=512, tt=512):
    T, K1 = x.shape
    halves = dy.ndim == 3
    N = 2 * dy.shape[2] if halves else dy.shape[1]
    tk1, tn, tt = min(tk1, K1), min(tn, N), min(tt, T)
    has_norm = norm_w is not None
    assert K1 % tk1 == 0 and N % tn == 0 and T % tt == 0, (name, K1, N, T, tk1, tn, tt)
    assert not has_norm or tk1 == K1
    nt = T // tt

    def body(*refs):
        if has_norm:
            x_ref, dy_ref, nw_ref, o_ref, acc_ref = refs
        else:
            x_ref, dy_ref, o_ref, acc_ref = refs
        t = pl.program_id(2)

        @pl.when(t == 0)
        def _():
            acc_ref[...] = jnp.zeros_like(acc_ref)

        xv = x_ref[...]
        if has_norm:
            xv = _rms_fwd(xv.astype(F32), nw_ref[...])
        acc_ref[...] += lax.dot_general(xv.astype(BF16), dy_ref[...].astype(BF16), _TN, preferred_element_type=F32)

        @pl.when(t == nt - 1)
        def _():
            o_ref[...] = acc_ref[...].astype(out_dtype)

    if halves:
        nbh = N // 2 // tn
        assert N // 2 % tn == 0
        dy_spec = pl.BlockSpec((None, tt, tn), lambda a, b, t: (lax.div(b, nbh), t, lax.rem(b, nbh)))
    else:
        dy_spec = pl.BlockSpec((tt, tn), lambda a, b, t: (t, b))
    in_specs = [pl.BlockSpec((tt, tk1), lambda a, b, t: (t, a)), dy_spec]
    args = [x, dy]
    if has_norm:
        in_specs.append(pl.BlockSpec((1, K1), lambda a, b, t: (0, 0)))
        args.append(norm_w.reshape(1, K1))
    return pl.pallas_call(
        body, name=name, grid=(K1 // tk1, N // tn, nt), in_specs=in_specs,
        out_specs=pl.BlockSpec((tk1, tn), lambda a, b, t: (a, b)),
        out_shape=jax.ShapeDtypeStruct((K1, N), out_dtype),
        scratch_shapes=[pltpu.VMEM((tk1, tn), F32)],
        compiler_params=_cparams(("parallel", "parallel", "arbitrary")))(*args)


def _mm_tn_t(dy, x, *, name, norm_w, out_dtype=BF16, tn=1408, tt=1024, vmem_mb=48):
    T, K1 = x.shape
    halves = dy.ndim == 3
    N = 2 * dy.shape[2] if halves else dy.shape[1]
    tn, tt = min(tn, N), min(tt, T)
    assert N % tn == 0 and T % tt == 0, (name, N, T, tn, tt)
    nt = T // tt

    def body(dy_ref, x_ref, nw_ref, o_ref, acc_ref):
        t = pl.program_id(1)

        @pl.when(t == 0)
        def _():
            acc_ref[...] = jnp.zeros_like(acc_ref)

        xn = _rms_fwd(x_ref[...].astype(F32), nw_ref[...]).astype(BF16)
        acc_ref[...] += lax.dot_general(dy_ref[...].astype(BF16), xn, _TN, preferred_element_type=F32)

        @pl.when(t == nt - 1)
        def _():
            o_ref[...] = acc_ref[...].astype(out_dtype)

    if halves:
        nbh = N // 2 // tn
        assert N // 2 % tn == 0
        dy_spec = pl.BlockSpec((None, tt, tn), lambda b, t: (lax.div(b, nbh), t, lax.rem(b, nbh)))
    else:
        dy_spec = pl.BlockSpec((tt, tn), lambda b, t: (t, b))
    return pl.pallas_call(
        body, name=name, grid=(N // tn, nt),
        in_specs=[dy_spec, pl.BlockSpec((tt, K1), lambda b, t: (t, 0)), pl.BlockSpec((1, K1), lambda b, t: (0, 0))],
        out_specs=pl.BlockSpec((tn, K1), lambda b, t: (b, 0)),
        out_shape=jax.ShapeDtypeStruct((N, K1), out_dtype),
        scratch_shapes=[pltpu.VMEM((tn, K1), F32)],
        compiler_params=_cparams(("parallel", "arbitrary"), vmem_mb))(dy, x, norm_w.reshape(1, K1))


def _shift_down(xb, prev8, j):
    main = pltpu.roll(xb, j, 0)
    head = pltpu.roll(xb[0:8], j, 0)
    ph = pltpu.roll(prev8, j, 0)
    row8 = lax.broadcasted_iota(jnp.int32, head.shape, 0)
    head = jnp.where(row8 < j, ph, head)
    return jnp.concatenate([head, main[8:]], axis=0)


def _shift_up(xb, next8, j):
    tt = xb.shape[0]
    main = pltpu.roll(xb, tt - j, 0)
    tail = pltpu.roll(xb[tt - 8:tt], 8 - j, 0)
    nh = pltpu.roll(next8, 8 - j, 0)
    row8 = lax.broadcasted_iota(jnp.int32, tail.shape, 0)
    tail = jnp.where(row8 + j >= 8, nh, tail)
    return jnp.concatenate([main[:tt - 8], tail], axis=0)


def _conv_hid(xb, prev8, w, b_row, K):
    out = b_row
    shifted = []
    for j in range(K):
        sh = K - 1 - j
        xs = xb if sh == 0 else _shift_down(xb, prev8, sh)
        shifted.append(xs)
        out = out + xs * w[j:j + 1, :]
    return out, shifted


def _prev_idx(i, nb8):
    return jnp.maximum(i * nb8 - 1, 0)


def _ssm_conv_fwd(zx, w, b, *, name, tt=512, tc=512):
    T = zx.shape[0]
    tt = min(tt, T)
    C, K = CONV_DIM, SSM_CONV
    cb0, nb8 = D_INNER // tc, tt // 8

    def body(x_ref, p_ref, w_ref, b_ref, o_ref):
        first = (pl.program_id(1) > 0).astype(F32)
        hid, _ = _conv_hid(x_ref[...], p_ref[...] * first, w_ref[...], b_ref[...], K)
        o_ref[...] = hid * _sigmoid(hid)

    return pl.pallas_call(
        body, name=name, grid=(C // tc, T // tt),
        in_specs=[pl.BlockSpec((tt, tc), lambda c, i: (i, c + cb0)),
                  pl.BlockSpec((8, tc), lambda c, i: (_prev_idx(i, nb8), c + cb0)),
                  pl.BlockSpec((K, tc), lambda c, i: (0, c)), pl.BlockSpec((1, tc), lambda c, i: (0, c))],
        out_specs=pl.BlockSpec((tt, tc), lambda c, i: (i, c)),
        out_shape=jax.ShapeDtypeStruct((T, C), F32),
        compiler_params=_cparams(("parallel", "parallel")))(zx, zx, w, b)


def _ssm_conv_bwd_pre(zx, w, b, dout, *, name, tt=512, tc=512):
    T = zx.shape[0]
    tt = min(tt, T)
    C, K = CONV_DIM, SSM_CONV
    cb0, nb8 = D_INNER // tc, tt // 8

    def body(x_ref, p_ref, w_ref, b_ref, d_ref, dh_ref, dw_ref, db_ref):
        t = pl.program_id(1)
        first = (t > 0).astype(F32)
        hid, shifted = _conv_hid(x_ref[...], p_ref[...] * first, w_ref[...], b_ref[...], K)
        sg = _sigmoid(hid)
        dh = d_ref[...] * (sg * (1.0 + hid * (1.0 - sg)))
        dh_ref[...] = dh

        @pl.when(t == 0)
        def _():
            dw_ref[...] = jnp.zeros_like(dw_ref)
            db_ref[...] = jnp.zeros_like(db_ref)

        db_ref[...] += jnp.sum(dh, axis=0, keepdims=True)
        for j in range(K):
            dw_ref[j:j + 1, :] += jnp.sum(dh * shifted[j], axis=0, keepdims=True)

    return pl.pallas_call(
        body, name=name, grid=(C // tc, T // tt),
        in_specs=[pl.BlockSpec((tt, tc), lambda c, i: (i, c + cb0)),
                  pl.BlockSpec((8, tc), lambda c, i: (_prev_idx(i, nb8), c + cb0)),
                  pl.BlockSpec((K, tc), lambda c, i: (0, c)), pl.BlockSpec((1, tc), lambda c, i: (0, c)),
                  pl.BlockSpec((tt, tc), lambda c, i: (i, c))],
        out_specs=[pl.BlockSpec((tt, tc), lambda c, i: (i, c)), pl.BlockSpec((K, tc), lambda c, i: (0, c)),
                   pl.BlockSpec((1, tc), lambda c, i: (0, c))],
        out_shape=[jax.ShapeDtypeStruct((T, C), F32), jax.ShapeDtypeStruct((K, C), F32),
                   jax.ShapeDtypeStruct((1, C), F32)],
        compiler_params=_cparams(("parallel", "arbitrary")))(zx, zx, w, b, dout)


def _put_cols(buf, src, col0, *, name, tt=512):
    T, C = src.shape
    tt = min(tt, T)

    def body(s_ref, _, o_ref):
        o_ref[...] = s_ref[...]

    return pl.pallas_call(
        body, name=name, grid=(T // tt,),
        in_specs=[pl.BlockSpec((tt, C), lambda i: (i, 0)), _ANY],
        out_specs=pl.BlockSpec((tt, C), lambda i: (i, col0 // C)),
        out_shape=jax.ShapeDtypeStruct(buf.shape, buf.dtype), input_output_aliases={1: 0},
        compiler_params=_cparams(("parallel",)))(src, buf)


def _conv_bwd_in(dh, w, *, name, K, tt=512, tc=512, out_dtype=BF16, into=None):
    T, C = dh.shape
    tt = min(tt, T)
    nb8, nT = tt // 8, T // tt
    last8 = T // 8 - 1
    cb0 = 0 if into is None else into[1] // tc

    def body(d_ref, n_ref, w_ref, *rest):
        o_ref = rest[-1]
        notlast = (pl.program_id(1) < nT - 1).astype(F32)
        d = d_ref[...]
        nxt = n_ref[...] * notlast
        w_ = w_ref[...]
        acc = d * w_[K - 1:K, :]
        for sh in range(1, K):
            acc = acc + _shift_up(d, nxt, sh) * w_[K - 1 - sh:K - sh, :]
        o_ref[...] = acc.astype(out_dtype)

    in_specs = [pl.BlockSpec((tt, tc), lambda c, i: (i, c)),
                pl.BlockSpec((8, tc), lambda c, i: (jnp.minimum((i + 1) * nb8, last8), c)),
                pl.BlockSpec((K, tc), lambda c, i: (0, c))]
    args = [dh, dh, w]
    if into is None:
        out_shape, alias = jax.ShapeDtypeStruct((T, C), out_dtype), {}
    else:
        assert into[0].dtype == out_dtype and into[1] % tc == 0
        in_specs.append(_ANY)
        args.append(into[0])
        out_shape, alias = jax.ShapeDtypeStruct(into[0].shape, out_dtype), {3: 0}
    return pl.pallas_call(
        body, name=name, grid=(C // tc, nT), in_specs=in_specs,
        out_specs=pl.BlockSpec((tt, tc), lambda c, i: (i, c + cb0)),
        out_shape=out_shape, input_output_aliases=alias,
        compiler_params=_cparams(("parallel", "parallel")))(*args)


def _ffn_conv_fwd3(a3, w, b, *, name, tt=256, tc=1408):
    T = a3.shape[1]
    tt = min(tt, T)
    K, nbh, n16 = FFN_CONV, D_FF // tc, tt // 16

    def body(a_ref, p_ref, wg_ref, wv_ref, bg_ref, bv_ref, o_ref):
        first = (pl.program_id(1) > 0).astype(F32)
        a = a_ref[...].astype(F32)
        prev = p_ref[...].astype(F32)[:, 8:16, :] * first
        hg, _ = _conv_hid(a[0], prev[0], wg_ref[...], bg_ref[...], K)
        hv, _ = _conv_hid(a[1], prev[1], wv_ref[...], bv_ref[...], K)
        o_ref[...] = (hg * _sigmoid(hg) * hv).astype(BF16)

    return pl.pallas_call(
        body, name=name, grid=(nbh, T // tt),
        in_specs=[pl.BlockSpec((2, tt, tc), lambda c, i: (0, i, c)),
                  pl.BlockSpec((2, 16, tc), lambda c, i: (0, _prev_idx(i, n16), c)),
                  pl.BlockSpec((K, tc), lambda c, i: (0, c)), pl.BlockSpec((K, tc), lambda c, i: (0, c + nbh)),
                  pl.BlockSpec((1, tc), lambda c, i: (0, c)), pl.BlockSpec((1, tc), lambda c, i: (0, c + nbh))],
        out_specs=pl.BlockSpec((tt, tc), lambda c, i: (i, c)),
        out_shape=jax.ShapeDtypeStruct((T, D_FF), BF16),
        compiler_params=_cparams(("parallel", "parallel")))(a3, a3, w, w, b, b)


def _ffn_conv_bwd3(a3, w, b, dp, *, name, tt=256, tc=1408):
    T = a3.shape[1]
    tt = min(tt, T)
    K, nbh, n16 = FFN_CONV, D_FF // tc, tt // 16

    def body(a_ref, p_ref, wg_ref, wv_ref, bg_ref, bv_ref, dp_ref, dh_ref, dw_ref, db_ref):
        t = pl.program_id(1)
        first = (t > 0).astype(F32)
        a = a_ref[...].astype(F32)
        prev = p_ref[...].astype(F32)[:, 8:16, :] * first
        hg, sh_g = _conv_hid(a[0], prev[0], wg_ref[...], bg_ref[...], K)
        hv, sh_v = _conv_hid(a[1], prev[1], wv_ref[...], bv_ref[...], K)
        sg = _sigmoid(hg)
        d = dp_ref[...].astype(F32)
        dhg = d * hv * (sg * (1.0 + hg * (1.0 - sg)))
        dhv = d * (hg * sg)
        dh_ref[0] = dhg.astype(BF16)
        dh_ref[1] = dhv.astype(BF16)

        @pl.when(t == 0)
        def _():
            dw_ref[...] = jnp.zeros_like(dw_ref)
            db_ref[...] = jnp.zeros_like(db_ref)

        db_ref[0] += jnp.sum(dhg, axis=0, keepdims=True)
        db_ref[1] += jnp.sum(dhv, axis=0, keepdims=True)
        for j in range(K):
            dw_ref[0, j:j + 1, :] += jnp.sum(dhg * sh_g[j], axis=0, keepdims=True)
            dw_ref[1, j:j + 1, :] += jnp.sum(dhv * sh_v[j], axis=0, keepdims=True)

    return pl.pallas_call(
        body, name=name, grid=(nbh, T // tt),
        in_specs=[pl.BlockSpec((2, tt, tc), lambda c, i: (0, i, c)),
                  pl.BlockSpec((2, 16, tc), lambda c, i: (0, _prev_idx(i, n16), c)),
                  pl.BlockSpec((K, tc), lambda c, i: (0, c)), pl.BlockSpec((K, tc), lambda c, i: (0, c + nbh)),
                  pl.BlockSpec((1, tc), lambda c, i: (0, c)), pl.BlockSpec((1, tc), lambda c, i: (0, c + nbh)),
                  pl.BlockSpec((tt, tc), lambda c, i: (i, c))],
        out_specs=[pl.BlockSpec((2, tt, tc), lambda c, i: (0, i, c)), pl.BlockSpec((2, K, tc), lambda c, i: (0, 0, c)),
                   pl.BlockSpec((2, 1, tc), lambda c, i: (0, 0, c))],
        out_shape=[jax.ShapeDtypeStruct((2, T, D_FF), BF16), jax.ShapeDtypeStruct((2, K, D_FF), F32),
                   jax.ShapeDtypeStruct((2, 1, D_FF), F32)],
        compiler_params=_cparams(("parallel", "arbitrary")))(a3, a3, w, w, b, b, dp)


def _conv_bwd_in3(dh3, w, *, name, K, tt=256, tc=1408):
    H, T, C = dh3.shape
    tt = min(tt, T)
    nb, n16, nT = C // tc, tt // 16, T // tt
    last16 = T // 16 - 1

    def body(d_ref, n_ref, w_ref, o_ref):
        notlast = (pl.program_id(2) < nT - 1).astype(F32)
        d = d_ref[...].astype(F32)
        nxt = n_ref[...].astype(F32)[0:8, :] * notlast
        w_ = w_ref[...]
        acc = d * w_[K - 1:K, :]
        for sh in range(1, K):
            acc = acc + _shift_up(d, nxt, sh) * w_[K - 1 - sh:K - sh, :]
        o_ref[...] = acc.astype(BF16)

    return pl.pallas_call(
        body, name=name, grid=(H, nb, nT),
        in_specs=[pl.BlockSpec((None, tt, tc), lambda h, c, i: (h, i, c)),
                  pl.BlockSpec((None, 16, tc), lambda h, c, i: (h, jnp.minimum((i + 1) * n16, last16), c)),
                  pl.BlockSpec((K, tc), lambda h, c, i: (0, h * nb + c))],
        out_specs=pl.BlockSpec((None, tt, tc), lambda h, c, i: (h, i, c)),
        out_shape=jax.ShapeDtypeStruct((H, T, C), BF16),
        compiler_params=_cparams(("parallel", "parallel", "parallel")))(dh3, dh3, w)


def _cumsum_rows(x):
    L = x.shape[0]
    row = lax.broadcasted_iota(jnp.int32, x.shape, 0)
    k = 1
    while k < L:
        x = x + jnp.where(row >= k, pltpu.roll(x, k, 0), 0.0)
        k *= 2
    return x


def _rcumsum_rows(x):
    L = x.shape[0]
    row = lax.broadcasted_iota(jnp.int32, x.shape, 0)
    k = 1
    while k < L:
        x = x + jnp.where(row < L - k, pltpu.roll(x, L - k, 0), 0.0)
        k *= 2
    return x


def _split_terms(m, n):
    terms, rest = [], m
    for _ in range(n):
        t = rest.astype(BF16)
        terms.append(t)
        rest = rest - t.astype(F32)
    return jnp.concatenate(terms, axis=1)


def _select_dot(m, n_terms, n_out, cond):
    K = m.shape[1]
    k = lax.broadcasted_iota(jnp.int32, (K, n_out), 0)
    j = lax.broadcasted_iota(jnp.int32, (K, n_out), 1)
    sel = cond(k, j).astype(BF16)
    return jnp.dot(_split_terms(m, n_terms), jnp.concatenate([sel] * n_terms, axis=0), preferred_element_type=F32)


def _rowsum_mxu(m):
    return _select_dot(m, 2, 128, lambda k, j: k >= 0)


def _lane_block_sums(m, width):
    shift = width.bit_length() - 1
    return _select_dot(m, 2, 128, lambda k, j: j == jnp.right_shift(k, shift))


def _heads_to_pairs(m):
    return _select_dot(m, 3, 512, lambda k, j: k == jnp.right_shift(j, 6))


def _ssd_common(dt_ref, par_ref):
    par = par_ref[...]
    raw = dt_ref[...] + par[0:1, :]
    dt = _softplus(raw)
    a = -jnp.exp(par[1:2, :])
    cs = _cumsum_rows(dt * a)
    L = cs.shape[0]
    cs_last = cs[L - 1:L, :]
    return raw, dt, a, par[2:3, :], cs, cs.T, jnp.exp(cs), jnp.exp(cs_last - cs), jnp.exp(cs_last)


def _ssd_specs(nc, rev):
    L = SSM_CHUNK

    def ci(c):
        return nc - 1 - c if rev else c

    return [pl.BlockSpec((L, D_INNER), lambda c: (ci(c), 0)),
            pl.BlockSpec((L, GN), lambda c: (ci(c), D_INNER // GN)),
            pl.BlockSpec((L, GN), lambda c: (ci(c), D_INNER // GN + 1)),
            pl.BlockSpec((SSM_GROUPS, L, 128), lambda c: (0, ci(c), 0)),
            pl.BlockSpec((SSM_GROUPS, 8, 128), lambda c: (0, 0, 0)),
            pl.BlockSpec((L, D_INNER), lambda c: (ci(c), 0)),
            pl.BlockSpec((1, D_INNER), lambda c: (0, 0))], ci


def _round_robin(gens):
    live = list(gens)
    while live:
        nxt = []
        for gen in live:
            try:
                next(gen)
                nxt.append(gen)
            except StopIteration:
                pass
        live = nxt


def _group_views(g, wide, narrow, lead):
    return ([r.at[:, g * 512:(g + 1) * 512] for r in wide], [r.at[:, g * 128:(g + 1) * 128] for r in narrow],
            [r.at[g] for r in lead])


def _ssd_fwd(xbc_c, zx, dtg, par, gnw, *, name):
    T = xbc_c.shape[0]
    L = SSM_CHUNK
    nc = T // L
    in_specs, ci = _ssd_specs(nc, False)

    def body(xs_ref, b_ref, c_ref, dt_ref, par_ref, z_ref, gnw_ref, y_ref, yn_ref, st_ref, h_ref):
        @pl.when(pl.program_id(0) == 0)
        def _():
            h_ref[...] = jnp.zeros_like(h_ref)

        gens = []
        for g in range(SSM_GROUPS):
            (xs, z, gw, y, yn), (b, c), (dt, pr, st, h) = _group_views(
                g, [xs_ref, z_ref, gnw_ref, y_ref, yn_ref], [b_ref, c_ref], [dt_ref, par_ref, st_ref, h_ref])
            gens.append(group(xs, b, c, dt, pr, z, gw, y, yn, st, h))
        _round_robin(gens)

    def group(xs_ref, b_ref, c_ref, dt_ref, par_ref, z_ref, gnw_ref, y_ref, yn_ref, st_ref, h_ref):
        _, dt, _, dsk, cs, csT, ecs, eend, dec = _ssd_common(dt_ref, par_ref)
        Bb = b_ref[...].astype(BF16)
        Cb = c_ref[...].astype(BF16)
        G = lax.dot_general(Cb, Bb, _NT, preferred_element_type=F32)
        row = lax.broadcasted_iota(jnp.int32, (L, L), 0)
        col = lax.broadcasted_iota(jnp.int32, (L, L), 1)
        tril = col <= row
        lo = lax.broadcasted_iota(jnp.int32, (L, 128), 1) < 64
        lo1 = lax.broadcasted_iota(jnp.int32, (1, 128), 1) < 64
        dt_x, ecs_x, eend_x = (_heads_to_pairs(m) for m in (dt, ecs, eend))
        for pp in range(4):
            hA, hB = 2 * pp, 2 * pp + 1
            lanes = slice(pp * 128, (pp + 1) * 128)

            def sel1(m):
                return jnp.where(lo1, m[:, hA:hA + 1], m[:, hB:hB + 1])

            X = xs_ref[:, lanes]
            xd = X * dt_x[:, lanes]
            xdb = xd.astype(BF16)
            ys = []
            for h in (hA, hB):
                Lm = jnp.where(tril, jnp.exp(jnp.minimum(cs[:, h:h + 1] - csT[h:h + 1, :], 0.0)), 0.0)
                ys.append(jnp.dot((G * Lm).astype(BF16), xdb, preferred_element_type=F32))
                yield
            Hp = h_ref[pp]
            st_ref[pp] = Hp
            yoff = jnp.dot(Cb, Hp.astype(BF16), preferred_element_type=F32) * ecs_x[:, lanes]
            y_ref[:, lanes] = jnp.where(lo, ys[0], ys[1]) + yoff + sel1(dsk) * X
            S = lax.dot_general(Bb, (xd * eend_x[:, lanes]).astype(BF16), _TN, preferred_element_type=F32)
            h_ref[pp] = Hp * sel1(dec) + S
            yield
        zv = z_ref[...]
        yg = y_ref[...] * (zv * _sigmoid(zv))
        r = jnp.tile(lax.rsqrt(_rowsum_mxu(yg * yg) * (1.0 / 512) + EPS), (1, 4))
        yn_ref[...] = (yg * r * gnw_ref[...]).astype(BF16)

    return pl.pallas_call(
        body, name=name, grid=(nc,), in_specs=in_specs,
        out_specs=[pl.BlockSpec((L, D_INNER), lambda c: (c, 0)), pl.BlockSpec((L, D_INNER), lambda c: (c, 0)),
                   pl.BlockSpec((SSM_GROUPS, None, 4, 128, 128), lambda c: (0, c, 0, 0, 0))],
        out_shape=[jax.ShapeDtypeStruct((T, D_INNER), F32), jax.ShapeDtypeStruct((T, D_INNER), BF16),
                   jax.ShapeDtypeStruct((SSM_GROUPS, nc, 4, 128, 128), F32)],
        scratch_shapes=[pltpu.VMEM((SSM_GROUPS, 4, 128, 128), F32)],
        compiler_params=_cparams(("arbitrary",)))(xbc_c, xbc_c, xbc_c, dtg, par, zx, gnw)


def _ssd_bwd(xbc_c, zx, dtg, par, gnw, y, st, dyn, *, name):
    T = xbc_c.shape[0]
    L = SSM_CHUNK
    nc = T // L
    in_specs, ci = _ssd_specs(nc, True)
    in_specs += [pl.BlockSpec((L, D_INNER), lambda c: (ci(c), 0)),
                 pl.BlockSpec((SSM_GROUPS, None, 4, 128, 128), lambda c: (0, ci(c), 0, 0, 0)),
                 pl.BlockSpec((L, D_INNER), lambda c: (ci(c), 0))]

    def body(xs_ref, b_ref, c_ref, dt_ref, par_ref, z_ref, gnw_ref, y_ref, st_ref, dyn_ref,
             dxbc_ref, dz_ref, ddt_ref, dgnw_ref, dpar_ref, dh_ref):
        @pl.when(pl.program_id(0) == 0)
        def _():
            dh_ref[...] = jnp.zeros_like(dh_ref)
            dgnw_ref[...] = jnp.zeros_like(dgnw_ref)
            dpar_ref[...] = jnp.zeros_like(dpar_ref)

        dxs_ref = dxbc_ref.at[:, 0:D_INNER]
        db_ref = dxbc_ref.at[:, D_INNER:D_INNER + GN]
        dc_ref = dxbc_ref.at[:, D_INNER + GN:CONV_DIM]

        gens = []
        for g in range(SSM_GROUPS):
            (xs, z, gw, y, dyn, dxs, dz, dgw), (b, c, db, dc), (dt, pr, st, ddt, dpr, dh) = _group_views(
                g, [xs_ref, z_ref, gnw_ref, y_ref, dyn_ref, dxs_ref, dz_ref, dgnw_ref], [b_ref, c_ref, db_ref, dc_ref],
                [dt_ref, par_ref, st_ref, ddt_ref, dpar_ref, dh_ref])
            gens.append(group(xs, b, c, dt, pr, z, gw, y, st, dyn, dxs, db, dc, dz, ddt, dgw, dpr, dh))
        _round_robin(gens)

    def group(xs_ref, b_ref, c_ref, dt_ref, par_ref, z_ref, gnw_ref, y_ref, st_ref, dyn_ref,
              dxs_ref, db_ref, dc_ref, dz_ref, ddt_ref, dgnw_ref, dpar_ref, dh_ref):
        yv = y_ref[...]
        zv = z_ref[...]
        sg = _sigmoid(zv)
        sz = zv * sg
        yg = yv * sz
        r = jnp.tile(lax.rsqrt(_rowsum_mxu(yg * yg) * (1.0 / 512) + EPS), (1, 4))
        yh = yg * r
        dyn = dyn_ref[...].astype(F32)
        dgnw_ref[...] += jnp.sum(dyn * yh, axis=0, keepdims=True)
        dyh = dyn * gnw_ref[...]
        dyg = r * (dyh - yh * jnp.tile(_rowsum_mxu(dyh * yh) * (1.0 / 512), (1, 4)))
        dY_all = dyg * sz
        dz_ref[...] = (dyg * yv * (sg * (1.0 + zv * (1.0 - sg)))).astype(dz_ref.dtype)

        yield
        raw, dt, a, dsk, cs, csT, ecs, eend, dec = _ssd_common(dt_ref, par_ref)
        Bb = b_ref[...].astype(BF16)
        Cb = c_ref[...].astype(BF16)
        G = lax.dot_general(Cb, Bb, _NT, preferred_element_type=F32)
        row = lax.broadcasted_iota(jnp.int32, (L, L), 0)
        col = lax.broadcasted_iota(jnp.int32, (L, L), 1)
        tril = col <= row
        lo = lax.broadcasted_iota(jnp.int32, (L, 128), 1) < 64
        lane1 = lax.broadcasted_iota(jnp.int32, (1, 128), 1)
        lo1 = lane1 < 64
        rowl = lax.broadcasted_iota(jnp.int32, (L, 128), 0)
        dt_x, ecs_x, eend_x = (_heads_to_pairs(m) for m in (dt, ecs, eend))
        dG = jnp.zeros((L, L), F32)
        dB = jnp.zeros((L, SSM_STATE), F32)
        dC = jnp.zeros((L, SSM_STATE), F32)
        dcs_t = jnp.zeros((L, L), F32)
        tails = jnp.zeros((1, 128), F32)
        dD_row = jnp.zeros((1, 128), F32)
        v_parts, prod_parts = [], []

        def tot(m):
            return jnp.sum(jnp.sum(m, axis=0, keepdims=True), axis=1, keepdims=True)

        for pp in range(4):
            hA, hB = 2 * pp, 2 * pp + 1
            lanes = slice(pp * 128, (pp + 1) * 128)

            def sel1(m):
                return jnp.where(lo1, m[:, hA:hA + 1], m[:, hB:hB + 1])

            X = xs_ref[:, lanes]
            dY = dY_all[:, lanes]
            dtsel = dt_x[:, lanes]
            xd = X * dtsel
            xdb = xd.astype(BF16)
            dYb = dY.astype(BF16)
            Hp = st_ref[pp]
            Hb = Hp.astype(BF16)
            dHn = dh_ref[pp]
            dHb = dHn.astype(BF16)
            ecs_sel = ecs_x[:, lanes]
            eend_sel = eend_x[:, lanes]
            dxd_state = jnp.dot(Bb, dHb, preferred_element_type=F32) * eend_sel
            yoff = jnp.dot(Cb, Hb, preferred_element_type=F32) * ecs_sel
            dYe = (dY * ecs_sel).astype(BF16)
            dC = dC + lax.dot_general(dYe, Hb, _NT, preferred_element_type=F32)
            dB = dB + lax.dot_general((xd * eend_sel).astype(BF16), dHb, _NT, preferred_element_type=F32)
            dh_ref[pp] = dHn * sel1(dec) + lax.dot_general(Cb, dYe, _TN, preferred_element_type=F32)
            q = xd * dxd_state
            dyq = dY * yoff - q
            qcol = jnp.sum(q, axis=0, keepdims=True)
            hcol = jnp.sum(dHn * Hp, axis=0, keepdims=True)
            dxd_diag = []
            for h, msk, msk1 in ((hA, lo, lo1), (hB, jnp.logical_not(lo), jnp.logical_not(lo1))):
                Lm = jnp.where(tril, jnp.exp(jnp.minimum(cs[:, h:h + 1] - csT[h:h + 1, :], 0.0)), 0.0)
                M = G * Lm
                dxd_diag.append(lax.dot_general(M.astype(BF16), dYb, _TN, preferred_element_type=F32))
                dM = lax.dot_general(jnp.where(msk, dY, 0.0).astype(BF16), xdb, _NT, preferred_element_type=F32)
                dG = dG + dM * Lm
                W = dM * M
                dcs_t = dcs_t + jnp.where(row == h, jnp.sum(W, axis=0, keepdims=True), 0.0)
                v_parts.append(W + jnp.where(msk, dyq, 0.0))
                tail = (jnp.sum(jnp.where(msk1, qcol, 0.0), axis=1, keepdims=True)
                        + dec[:, h:h + 1] * jnp.sum(jnp.where(msk1, hcol, 0.0), axis=1, keepdims=True))
                tails = tails + jnp.where(lane1 == h, tail, 0.0)
                yield
            dxd = jnp.where(lo, dxd_diag[0], dxd_diag[1]) + dxd_state
            prod_parts.append(dxd * X)
            dxs_ref[:, lanes] = dxd * dtsel + sel1(dsk) * dY
            dyx = jnp.sum(dY * X, axis=0, keepdims=True)
            sA = jnp.sum(jnp.where(lo1, dyx, 0.0), axis=1, keepdims=True)
            sB = jnp.sum(dyx, axis=1, keepdims=True) - sA
            dD_row = dD_row + jnp.where(lane1 == hA, sA, 0.0) + jnp.where(lane1 == hB, sB, 0.0)
            yield
        dGb = dG.astype(BF16)
        db_ref[...] = dB + lax.dot_general(dGb, Cb, _TN, preferred_element_type=F32)
        dc_ref[...] = dC + jnp.dot(dGb, Bb, preferred_element_type=F32)
        dcs_mat = _lane_block_sums(jnp.concatenate(v_parts, axis=1), 128) + jnp.where(rowl == L - 1, tails, 0.0)
        ddt_mat = _lane_block_sums(jnp.concatenate(prod_parts, axis=1), 64)
        dad = _rcumsum_rows(dcs_mat - dcs_t.T)
        draw = (a * dad + ddt_mat) * _sigmoid(raw)
        ddt_ref[...] = draw
        dpar_ref[0:1, :] += jnp.sum(draw, axis=0, keepdims=True)
        dpar_ref[1:2, :] += jnp.sum(dt * dad, axis=0, keepdims=True) * a
        dpar_ref[2:3, :] += dD_row

    return pl.pallas_call(
        body, name=name, grid=(nc,), in_specs=in_specs,
        out_specs=[pl.BlockSpec((L, CONV_DIM), lambda c: (ci(c), 0)),
                   pl.BlockSpec((L, D_INNER), lambda c: (ci(c), 0)),
                   pl.BlockSpec((SSM_GROUPS, L, 128), lambda c: (0, ci(c), 0)),
                   pl.BlockSpec((1, D_INNER), lambda c: (0, 0)),
                   pl.BlockSpec((SSM_GROUPS, 8, 128), lambda c: (0, 0, 0))],
        out_shape=[jax.ShapeDtypeStruct((T, CONV_DIM), F32), jax.ShapeDtypeStruct((T, IN_PROJ_PAD), BF16),
                   jax.ShapeDtypeStruct((SSM_GROUPS, T, 128), F32), jax.ShapeDtypeStruct((1, D_INNER), F32),
                   jax.ShapeDtypeStruct((SSM_GROUPS, 8, 128), F32)],
        scratch_shapes=[pltpu.VMEM((SSM_GROUPS, 4, 128, 128), F32)],
        compiler_params=_cparams(("arbitrary",)))(xbc_c, xbc_c, xbc_c, dtg, par, zx, gnw, y, st, dyn)


SB_KEYS = 512
SB_SCAN = 256
SB_STRIP = 256


def _tri(width, cond):
    kk = lax.broadcasted_iota(jnp.int32, (width, width), 0)
    jj = lax.broadcasted_iota(jnp.int32, (width, width), 1)
    return cond(kk, jj).astype(BF16)


_LOG2E = 1.4426950408889634


def _softplus2(z2):
    return jnp.maximum(z2, 0.0) + jnp.log2(1.0 + jnp.exp2(-jnp.abs(z2)))


def _sba_sub_fwd(zb, c, U, mask):
    z2 = zb * _LOG2E
    s = _softplus2(z2)
    if mask is not None:
        s = jnp.where(mask, s, 0.0)
    R = c + jnp.dot(s.astype(BF16), U, preferred_element_type=F32)
    A = jnp.exp2(z2 - s - R)
    if mask is not None:
        A = jnp.where(mask, A, 0.0)
    return A.astype(BF16), R[:, 0:1] + s[:, 0:1]


def _sba_sub_bwd(zb, dAb, Lt, pc, pe, Uincl, Uexcl, mask):
    last = zb.shape[1] - 1
    z2 = zb * _LOG2E
    s = _softplus2(z2)
    g = z2 - s
    if mask is not None:
        s = jnp.where(mask, s, 0.0)
    P = pc + jnp.dot(s.astype(BF16), Uincl, preferred_element_type=F32)
    A = jnp.exp2(g - (Lt - P))
    if mask is not None:
        A = jnp.where(mask, A, 0.0)
    E = dAb * A
    PE = pe + jnp.dot(E.astype(BF16), Uexcl, preferred_element_type=F32)
    dz = E - jnp.exp2(g) * (E + PE)
    if mask is not None:
        dz = jnp.where(mask, dz, 0.0)
    return (A.astype(BF16), dz.astype(BF16), P[:, last:last + 1], PE[:, last:last + 1] + E[:, last:last + 1])


def _stack_heads(v):
    lo = lax.broadcasted_iota(jnp.int32, v.shape, 1) < 64
    zero = jnp.zeros_like(v)
    return jnp.concatenate([jnp.where(lo, v, zero), jnp.where(lo, zero, v)], axis=0)


def _unstack_heads(v):
    lo = lax.broadcasted_iota(jnp.int32, (SB_BLOCK, 128), 1) < 64
    return jnp.where(lo, v[:SB_BLOCK], v[SB_BLOCK:])


def _sba_rows(a):
    return slice(2 * a * SB_BLOCK, 2 * (a + 1) * SB_BLOCK)


def _sba_diag_case(a, b):
    Bq = SB_BLOCK
    if b * SB_SCAN >= (a + 1) * Bq:
        return "skip"
    if (b + 1) * SB_SCAN <= a * Bq:
        return "full"
    rowi = lax.broadcasted_iota(jnp.int32, (2 * Bq, SB_SCAN), 0)
    qpos = a * Bq + jnp.where(rowi >= Bq, rowi - Bq, rowi)
    return b * SB_SCAN + lax.broadcasted_iota(jnp.int32, (2 * Bq, SB_SCAN), 1) < qpos


def _sba_fwd(q, kv, *, name):
    T = q.shape[0]
    Bq = SB_BLOCK
    nsub = SB_KEYS // Bq
    nscan = SB_KEYS // SB_SCAN
    R = 2 * SB_KEYS
    assert T % SB_KEYS == 0 and SB_STRIP == 2 * Bq
    scale = 1.0 / math.sqrt(SB_HEAD_DIM)

    def body(q_ref, k_ref, v_ref, o_ref, lt_ref, z_s, a_s, c_s, acc_s):
        i = pl.program_id(1)
        U2 = _tri(SB_SCAN, lambda k, j: k > j)
        qs_all = jnp.concatenate([_stack_heads(q_ref[a * Bq:(a + 1) * Bq, :] * scale) for a in range(nsub)], axis=0)
        c_s[...] = jnp.zeros_like(c_s)
        acc_s[...] = jnp.zeros_like(acc_s)

        def scores(J, slot):
            off = pl.multiple_of(J * SB_KEYS, SB_KEYS)
            z_s[slot] = lax.dot_general(qs_all, k_ref[pl.ds(off, SB_KEYS), :], _NT, preferred_element_type=F32)

        def weights(slot, diag):
            for a in range(nsub):
                rows = _sba_rows(a)
                c = c_s[rows, :]
                for b in reversed(range(nscan)):
                    cols = slice(b * SB_SCAN, (b + 1) * SB_SCAN)
                    case = _sba_diag_case(a, b) if diag else "full"
                    if isinstance(case, str) and case == "skip":
                        a_s[slot, rows, cols] = jnp.zeros((2 * Bq, SB_SCAN), BF16)
                        continue
                    A, c = _sba_sub_fwd(z_s[slot, rows, cols], c, U2, None if isinstance(case, str) else case)
                    a_s[slot, rows, cols] = A
                c_s[rows, :] = c

        def values(J, slot):
            off = pl.multiple_of(J * SB_KEYS, SB_KEYS)
            acc_s[...] += jnp.dot(a_s[slot], v_ref[pl.ds(off, SB_KEYS), :], preferred_element_type=F32)

        scores(i, 0)
        weights(0, True)
        scores(jnp.maximum(i - 1, 0), 1)

        def two_steps(u, _):
            t = 2 * u + 1
            weights(1, False)
            scores(jnp.maximum(i - t - 1, 0), 0)
            values(i - t + 1, 0)
            weights(0, False)
            scores(jnp.maximum(i - t - 2, 0), 1)
            values(i - t, 1)
            return 0

        lax.fori_loop(0, i // 2, two_steps, 0)
        odd = lax.rem(i, 2) == 1

        @pl.when(jnp.logical_not(odd))
        def _():
            values(0, 0)

        @pl.when(odd)
        def _():
            weights(1, False)
            values(1, 0)
            values(0, 1)
        for a in range(nsub):
            o_ref[a * Bq:(a + 1) * Bq, :] = _unstack_heads(acc_s[_sba_rows(a), :]).astype(BF16)
            lt_ref[a * Bq:(a + 1) * Bq, :] = _unstack_heads(jnp.broadcast_to(c_s[_sba_rows(a), :], (2 * Bq, 128)))

    return pl.pallas_call(
        body, name=name, grid=(SB_HEADS // 2, T // SB_KEYS),
        in_specs=[pl.BlockSpec((SB_KEYS, 128), lambda p, i: (i, p)), pl.BlockSpec((T, 128), lambda p, i: (0, p)),
                  pl.BlockSpec((T, 128), lambda p, i: (0, p + SB_HEADS // 2))],
        out_specs=[pl.BlockSpec((SB_KEYS, 128), lambda p, i: (i, p)),
                   pl.BlockSpec((None, SB_KEYS, 128), lambda p, i: (p, i, 0))],
        out_shape=[jax.ShapeDtypeStruct((T, D_MODEL), BF16), jax.ShapeDtypeStruct((SB_HEADS // 2, T, 128), F32)],
        scratch_shapes=[pltpu.VMEM((2, R, SB_KEYS), F32), pltpu.VMEM((2, R, SB_KEYS), BF16),
                        pltpu.VMEM((R, 1), F32), pltpu.VMEM((R, 128), F32)],
        compiler_params=_cparams(("parallel", "parallel")))(q, kv, kv)


def _sba_bwd(q, kv, lt, do, *, name):
    T = q.shape[0]
    Bq = SB_BLOCK
    nq = T // SB_KEYS
    nsub = SB_KEYS // Bq
    nscan = SB_KEYS // SB_SCAN
    R = 2 * SB_KEYS
    assert T % SB_KEYS == 0 and SB_STRIP == 2 * Bq
    scale = 1.0 / math.sqrt(SB_HEAD_DIM)

    def body(q_ref, k_ref, v_ref, lt_ref, do_ref, dq_ref, dk_ref, dv_ref, dk_acc, dv_acc,
             z_s, da_s, a_s, dz_s, pc_s, pe_s, lt_s, dq_s):
        i = pl.program_id(1)

        @pl.when(i == 0)
        def _():
            dk_acc[...] = jnp.zeros_like(dk_acc)
            dv_acc[...] = jnp.zeros_like(dv_acc)

        Uincl = _tri(SB_SCAN, lambda k, j: k <= j)
        Uexcl = _tri(SB_SCAN, lambda k, j: k < j)
        qs, dos = [], []
        for a in range(nsub):
            rows = slice(a * Bq, (a + 1) * Bq)
            qs.append(_stack_heads(q_ref[rows, :] * scale))
            dos.append(_stack_heads(do_ref[rows, :]))
            lt_s[_sba_rows(a), :] = jnp.concatenate([lt_ref[rows, 0:1], lt_ref[rows, 64:65]], axis=0)
        qs_all = jnp.concatenate(qs, axis=0)
        dos_all = jnp.concatenate(dos, axis=0)
        pc_s[...] = jnp.zeros_like(pc_s)
        pe_s[...] = jnp.zeros_like(pe_s)
        a_s[1] = jnp.zeros((R, SB_KEYS), BF16)
        dz_s[1] = jnp.zeros((R, SB_KEYS), BF16)

        def scores(J, slot):
            off = pl.multiple_of(J * SB_KEYS, SB_KEYS)
            z_s[slot] = lax.dot_general(qs_all, k_ref[pl.ds(off, SB_KEYS), :], _NT, preferred_element_type=F32)
            da_s[slot] = lax.dot_general(dos_all, v_ref[pl.ds(off, SB_KEYS), :], _NT, preferred_element_type=F32)

        def gradients(slot, diag):
            for a in range(nsub):
                rows = _sba_rows(a)
                pc, pe, Lt = pc_s[rows, :], pe_s[rows, :], lt_s[rows, :]
                for b in range(nscan):
                    cols = slice(b * SB_SCAN, (b + 1) * SB_SCAN)
                    case = _sba_diag_case(a, b) if diag else "full"
                    if isinstance(case, str) and case == "skip":
                        a_s[slot, rows, cols] = jnp.zeros((2 * Bq, SB_SCAN), BF16)
                        dz_s[slot, rows, cols] = jnp.zeros((2 * Bq, SB_SCAN), BF16)
                        continue
                    A, dz, pc, pe = _sba_sub_bwd(z_s[slot, rows, cols], da_s[slot, rows, cols], Lt, pc, pe, Uincl, Uexcl,
                                                 None if isinstance(case, str) else case)
                    a_s[slot, rows, cols] = A
                    dz_s[slot, rows, cols] = dz
                pc_s[rows, :] = pc
                pe_s[rows, :] = pe

        def products(J, slot):
            off = pl.multiple_of(J * SB_KEYS, SB_KEYS)
            dzt = dz_s[slot]
            dk_acc[pl.ds(off, SB_KEYS), :] += lax.dot_general(dzt, qs_all, _TN, preferred_element_type=F32)
            dv_acc[pl.ds(off, SB_KEYS), :] += lax.dot_general(a_s[slot], dos_all, _TN, preferred_element_type=F32)
            dq_s[...] += jnp.dot(dzt, k_ref[pl.ds(off, SB_KEYS), :], preferred_element_type=F32)

        dq_s[...] = jnp.zeros_like(dq_s)
        scores(0, 0)

        def two_steps(u, _):
            t = 2 * u
            gradients(0, False)
            scores(t + 1, 1)
            products(jnp.maximum(t - 1, 0), 1)
            gradients(1, False)
            scores(t + 2, 0)
            products(t, 0)
            return 0

        lax.fori_loop(0, i // 2, two_steps, 0)
        odd = lax.rem(i, 2) == 1

        @pl.when(jnp.logical_not(odd))
        def _():
            gradients(0, True)
            products(jnp.maximum(i - 1, 0), 1)
            products(i, 0)

        @pl.when(odd)
        def _():
            gradients(0, False)
            scores(i, 1)
            products(jnp.maximum(i - 2, 0), 1)
            gradients(1, True)
            products(i - 1, 0)
            products(i, 1)

        for a in range(nsub):
            dq_ref[a * Bq:(a + 1) * Bq, :] = (_unstack_heads(dq_s[_sba_rows(a), :]) * scale).astype(BF16)

        @pl.when(i == nq - 1)
        def _():
            dk_ref[...] = dk_acc[...].astype(BF16)
            dv_ref[...] = dv_acc[...].astype(BF16)

    return pl.pallas_call(
        body, name=name, grid=(SB_HEADS // 2, nq),
        in_specs=[pl.BlockSpec((SB_KEYS, 128), lambda p, i: (i, p)), pl.BlockSpec((T, 128), lambda p, i: (0, p)),
                  pl.BlockSpec((T, 128), lambda p, i: (0, p + SB_HEADS // 2)),
                  pl.BlockSpec((None, SB_KEYS, 128), lambda p, i: (p, i, 0)),
                  pl.BlockSpec((SB_KEYS, 128), lambda p, i: (i, p))],
        out_specs=[pl.BlockSpec((SB_KEYS, 128), lambda p, i: (i, p)), pl.BlockSpec((T, 128), lambda p, i: (0, p)),
                   pl.BlockSpec((T, 128), lambda p, i: (0, p))],
        out_shape=[jax.ShapeDtypeStruct((T, D_MODEL), BF16), jax.ShapeDtypeStruct((T, D_MODEL), BF16),
                   jax.ShapeDtypeStruct((T, D_MODEL), BF16)],
        scratch_shapes=[pltpu.VMEM((T, 128), F32), pltpu.VMEM((T, 128), F32),
                        pltpu.VMEM((2, R, SB_KEYS), F32), pltpu.VMEM((2, R, SB_KEYS), F32),
                        pltpu.VMEM((2, R, SB_KEYS), BF16), pltpu.VMEM((2, R, SB_KEYS), BF16),
                        pltpu.VMEM((R, 1), F32), pltpu.VMEM((R, 1), F32), pltpu.VMEM((R, 1), F32),
                        pltpu.VMEM((R, 128), F32)],
        compiler_params=_cparams(("parallel", "arbitrary")))(q, kv, kv, lt, do)


def _loss_head(h, tgt, w, *, name, tt=512):
    T, D = h.shape
    tt = min(tt, T)

    def body(h_ref, t_ref, w_ref, loss_ref, dh_ref, dw_ref):
        i = pl.program_id(0)
        hv = h_ref[...]
        wv = w_ref[...]
        r = lax.rsqrt(jnp.mean(hv * hv, axis=-1, keepdims=True) + EPS)
        xhat = hv * r
        err = xhat * wv - t_ref[...]
        part = 0.5 * jnp.sum(jnp.mean(err * err, axis=-1, keepdims=True), axis=0, keepdims=True)
        dy = err * (1.0 / D)
        dxh = dy * wv
        dh_ref[...] = r * (dxh - xhat * jnp.mean(dxh * xhat, axis=-1, keepdims=True))
        dwc = jnp.sum(dy * xhat, axis=0, keepdims=True)

        @pl.when(i == 0)
        def _():
            loss_ref[...] = jnp.broadcast_to(part, loss_ref.shape)
            dw_ref[...] = dwc

        @pl.when(i > 0)
        def _():
            loss_ref[...] += jnp.broadcast_to(part, loss_ref.shape)
            dw_ref[...] += dwc

    return pl.pallas_call(
        body, name=name, grid=(T // tt,),
        in_specs=[pl.BlockSpec((tt, D), lambda i: (i, 0)), pl.BlockSpec((tt, D), lambda i: (i, 0)),
                  pl.BlockSpec((1, D), lambda i: (0, 0))],
        out_specs=[pl.BlockSpec((1, 128), lambda i: (0, 0)), pl.BlockSpec((tt, D), lambda i: (i, 0)),
                   pl.BlockSpec((1, D), lambda i: (0, 0))],
        out_shape=[jax.ShapeDtypeStruct((1, 128), F32), jax.ShapeDtypeStruct((T, D), F32),
                   jax.ShapeDtypeStruct((1, D), F32)],
        compiler_params=_cparams(("arbitrary",)))(h, tgt, w.reshape(1, D))


def _adamw(parts, w, m, v, *, name, tr=256, tc=None):
    plist = list(parts) if isinstance(parts, (list, tuple)) else [parts]
    P, _, C = plist[0].shape
    R = sum(a.shape[1] for a in plist)
    tr = min(tr, R)
    tc = C if tc is None else tc
    assert all(a.shape[1] % tr == 0 for a in plist) and C % tc == 0, (name, R, C, tr, tc)
    nbs = [a.shape[1] // tr for a in plist]
    offs = [sum(nbs[:l]) for l in range(len(nbs))]
    c1 = 1.0 - ADAM_B1 ** ADAM_STEP
    c2 = 1.0 - ADAM_B2 ** ADAM_STEP

    def body(*refs):
        p_refs = refs[:len(plist)]
        w_ref, m_ref, v_ref, g_ref, d_ref, nm_ref, nv_ref = refs[len(plist):]
        i = pl.program_id(0)
        g = None
        for l, p_ref in enumerate(p_refs):
            gl = p_ref[0].astype(F32)
            for k in range(1, P):
                gl = gl + p_ref[k].astype(F32)
            g = gl if g is None else jnp.where(i >= offs[l], gl, g)
        mn = ADAM_B1 * m_ref[...] + (1.0 - ADAM_B1) * g
        vn = ADAM_B2 * v_ref[...] + (1.0 - ADAM_B2) * (g * g)
        g_ref[...] = g
        nm_ref[...] = mn
        nv_ref[...] = vn
        d_ref[...] = -ADAM_LR * ((mn / c1) / (jnp.sqrt(vn / c2) + ADAM_EPS) + ADAM_WD * w_ref[...])

    spec = pl.BlockSpec((tr, tc), lambda i, j: (i, j))
    sds = jax.ShapeDtypeStruct((R, C), F32)
    return pl.pallas_call(
        body, name=name, grid=(R // tr, C // tc),
        in_specs=[pl.BlockSpec((P, tr, tc), functools.partial(lambda i, j, o, n: (0, jnp.clip(i - o, 0, n - 1), j), o=o, n=n))
                  for o, n in zip(offs, nbs)] + [spec, spec, spec],
        out_specs=[spec, spec, spec, spec], out_shape=[sds, sds, sds, sds],
        compiler_params=_cparams(("parallel", "parallel")))(*plist, w, m, v)


def _all_gather(shards, *, name):
    n = len(shards)

    def body(*refs):
        ins, outs = refs[:n], refs[n:2 * n]
        send_sems, recv_sems, local_sems = refs[2 * n:]
        x, y, c = lax.axis_index("x"), lax.axis_index("y"), lax.axis_index("c")
        me, sib = (x, y, c), (x, y, 1 - c)
        chips = [(1 - x, y), (x, 1 - y), (1 - x, 1 - y)]

        def slot(p):
            return 4 * p[0] + 2 * p[1] + p[2]

        def cp(a, k, block, to, src=None):
            dst = outs[a].at[slot(block)]
            return pltpu.make_async_remote_copy(src_ref=dst if src is None else src, dst_ref=dst,
                                                send_sem=send_sems.at[a, k], recv_sem=recv_sems.at[a, k],
                                                device_id=to, device_id_type=_MESH)

        mine = [pltpu.make_async_copy(ins[a], outs[a].at[slot(me)], local_sems.at[a]) for a in range(n)]
        for m in mine:
            m.start()
        first = []
        for a in range(n):
            first.append(cp(a, 0, me, sib, src=ins[a]))
            for j, chip in enumerate(chips):
                first.append(cp(a, 1 + j, me, (*chip, c), src=ins[a]))
        for f in first:
            f.start()
        passed = []
        for j, chip in enumerate(chips):
            for a in range(n):
                cp(a, 1 + j, (*chip, c), me).wait_recv()
                f = cp(a, 4 + j, (*chip, c), sib)
                f.start()
                passed.append(f)
        for a in range(n):
            cp(a, 0, sib, me).wait_recv()
            for j, chip in enumerate(chips):
                cp(a, 4 + j, (*chip, 1 - c), me).wait_recv()
        for f in first + passed:
            f.wait_send()
        for m in mine:
            m.wait()

    return pl.pallas_call(
        body, name=name, in_specs=[_ANY] * n, out_specs=[_ANY] * n,
        out_shape=[jax.ShapeDtypeStruct((N_DEV,) + s.shape, s.dtype) for s in shards],
        scratch_shapes=[pltpu.SemaphoreType.DMA((n, 7)), pltpu.SemaphoreType.DMA((n, 7)),
                        pltpu.SemaphoreType.DMA((n,))])(*shards)


_HBM = pl.BlockSpec(memory_space=pltpu.HBM)
_SEM = pl.BlockSpec(memory_space=pltpu.SEMAPHORE)
_EFFECT = pltpu.SideEffectType.DATAFLOW_SIDE_EFFECTING


def _peers():
    x, y, c = lax.axis_index("x"), lax.axis_index("y"), lax.axis_index("c")
    out = []
    for r in range(1, N_DEV):
        px = 1 - x if (r >> 2) & 1 else x
        py = 1 - y if (r >> 1) & 1 else y
        pc = 1 - c if r & 1 else c
        out.append(((px, py, pc), 4 * px + 2 * py + pc))
    return 4 * x + 2 * y + c, out


def _push_copy(src_ref, land_ref, send_sems, recv_sems, a, k, me, peer, peer_slot, scatter, arriving):
    src = src_ref.at[peer_slot] if scatter else src_ref
    return pltpu.make_async_remote_copy(
        src_ref=src, dst_ref=land_ref.at[peer_slot if arriving else me], send_sem=send_sems.at[a * (N_DEV - 1) + k],
        recv_sem=recv_sems.at[a * (N_DEV - 1) + k], device_id=peer, device_id_type=_MESH)


def _push_start(srcs, *, scatter, name):
    n = len(srcs)
    lands = [lax.empty(s.shape if scatter else (N_DEV,) + s.shape, s.dtype) for s in srcs]

    def body(*refs):
        src_refs, land_refs = refs[:n], refs[n:2 * n]
        send_sems, recv_sems = refs[2 * n], refs[2 * n + 1]
        token = refs[-1]
        me, peers = _peers()
        for k, (peer, slot) in enumerate(peers):
            for a in range(n):
                _push_copy(src_refs[a], land_refs[a], send_sems, recv_sems, a, k, me, peer, slot, scatter, False).start()
        token[...] = jnp.zeros_like(token)

    hbm = lambda a: pltpu.HBM(a.shape, a.dtype)
    outs = pl.pallas_call(
        body, name=name,
        out_shape=(pltpu.SemaphoreType.DMA((n * (N_DEV - 1),)), pltpu.SemaphoreType.DMA((n * (N_DEV - 1),)),
                   *[hbm(s) for s in srcs], *[hbm(l) for l in lands], jax.ShapeDtypeStruct((8, 128), F32)),
        in_specs=[_HBM] * (2 * n),
        out_specs=(_SEM, _SEM, *([_HBM] * (2 * n)), pl.BlockSpec(memory_space=pltpu.VMEM)),
        input_output_aliases={i: 2 + i for i in range(2 * n)},
        compiler_params=pltpu.CompilerParams(has_side_effects=_EFFECT),
    )(*[pltpu.with_memory_space_constraint(s, pltpu.HBM) for s in srcs],
      *[pltpu.with_memory_space_constraint(l, pltpu.HBM) for l in lands])
    return dict(send=outs[0], recv=outs[1], srcs=list(outs[2:2 + n]), lands=list(outs[2 + n:2 + 2 * n]),
                token=outs[-1], scatter=scatter, n=n)


def _push_wait(h, after, *, name):
    n, scatter = h["n"], h["scatter"]

    def body(*refs):
        src_refs, land_refs = refs[:n], refs[n:2 * n]
        send_sems, recv_sems = refs[2 * n], refs[2 * n + 1]
        me, peers = _peers()
        for k, (peer, slot) in enumerate(peers):
            for a in range(n):
                cp = _push_copy(src_refs[a], land_refs[a], send_sems, recv_sems, a, k, me, peer, slot, scatter, True)
                cp.wait_send()
                cp.wait_recv()

    hbm = lambda a: pltpu.HBM(a.shape, a.dtype)
    outs = pl.pallas_call(
        body, name=name,
        out_shape=(*[hbm(s) for s in h["srcs"]], *[hbm(l) for l in h["lands"]]),
        in_specs=[_HBM] * (2 * n) + [_SEM, _SEM, _ANY], out_specs=tuple([_HBM] * (2 * n)),
        input_output_aliases={i: i for i in range(2 * n)},
        compiler_params=pltpu.CompilerParams(has_side_effects=_EFFECT),
    )(*h["srcs"], *h["lands"], h["send"], h["recv"], after)
    return list(outs[:n]), list(outs[n:])


def _ffn_fwd(h, nw, w_up, conv_w, conv_b, w_down, tag):
    a3 = _mm_fwd(h, w_up, norm_w=nw, name=f"ffn{tag}_up", out_dtype=BF16, halves=True, w_t=True, tm=1024, tn=2816)
    p = _ffn_conv_fwd3(a3, conv_w, conv_b.reshape(1, -1), name=f"ffn{tag}_conv")
    h_out = _mm_fwd(p, w_down, residual=h, name=f"ffn{tag}_down", tm=1024, tn=512)
    return h_out, (a3, p)


def _ffn_bwd(dh, h, saved, nw, w_up, conv_w, conv_b, w_down, tag):
    a3, p = saved
    g_down = _mm_tn(p, dh, name=f"ffn{tag}_down_wg", tk1=1408, tn=1024, tt=1024)
    dp = _mm_nt(dh, w_down, name=f"ffn{tag}_down_dg", out_dtype=BF16, tm=512, tn=2816, tk=1024)
    dhid3, dw3, db3 = _ffn_conv_bwd3(a3, conv_w, conv_b.reshape(1, -1), dp, name=f"ffn{tag}_conv_bwd")
    da3 = _conv_bwd_in3(dhid3, conv_w, K=FFN_CONV, name=f"ffn{tag}_conv_bwd_in")
    g_up = _mm_tn_t(da3, h, norm_w=nw, name=f"ffn{tag}_up_wg", tn=2816, tt=1024, vmem_mb=58)
    dh_out, g_nw = _mm_nt(da3, w_up, epi=(h, nw, dh), name=f"ffn{tag}_up_dg", w_t=True, tm=1024, tk=1408)
    g_cw = jnp.concatenate([dw3[0], dw3[1]], axis=1)
    g_cb = jnp.concatenate([db3[0], db3[1]], axis=1)
    return dh_out, dict(norm=g_nw.reshape(-1), up=g_up, conv_w=g_cw, conv_b=g_cb.reshape(-1), down=g_down)


_BIG = ["ssm_in_w", "ssm_out_w", "w_k", "w_v", "w_q", "w_o", "ffn_up_w", "ffn_down_w"]
_SMALL_SHARDED = ["ssm_norm_w", "ssm_conv_w", "ssm_conv_b", "ssm_gate_norm_w", "ffn_conv_w"]
_SMALL_REPL = ["ssm_dt_bias", "ssm_a_log", "ssm_d", "kv_norm_w", "attn_norm_w", "ffn_norm_w", "ffn_conv_b",
               "final_norm_w"]
_WEIGHTS = ["ssm_norm_w", "ssm_in_w", "ssm_conv_w", "ssm_conv_b", "ssm_dt_bias", "ssm_a_log", "ssm_d",
            "ssm_gate_norm_w", "ssm_out_w", "kv_norm_w", "w_k", "w_v", "attn_norm_w", "w_q", "w_o", "ffn_norm_w",
            "ffn_up_w", "ffn_conv_w", "ffn_conv_b", "ffn_down_w", "final_norm_w"]


def _as2d(a):
    return a.reshape(-1, a.shape[-1])


def _cols_to_full(g):
    return g.transpose(1, 0, 2).reshape(g.shape[1], N_DEV * g.shape[2])


def _pack_small(vals):
    flat = jnp.concatenate([v.reshape(-1).astype(F32) for v in vals])
    n = flat.shape[0]
    rows = -(-n // 1024) * 8
    return jnp.pad(flat, (0, rows * 128 - n)).reshape(rows, 128)


def _unpack_small(packed, shapes):
    flat = packed.reshape(-1)
    out, off = [], 0
    for s in shapes:
        n = math.prod(s)
        out.append(flat[off:off + n].reshape(s))
        off += n
    return out


def _tie(a, token):
    return a + token[0, 0].astype(a.dtype)


def _local_step(x, tgt, get_w, put_g):
    T = x.shape[0]
    Ws = get_w("ssm", None)
    fnw, fcw, fcb = Ws["ffn_norm_w"], Ws["ffn_conv_w"], Ws["ffn_conv_b"]
    zx = _mm_fwd(x, Ws["in_w"], norm_w=Ws["ssm_norm_w"], name="ssm_in", w_t=True, tm=1024, tn=1792)
    xbc_c = _ssm_conv_fwd(zx, Ws["ssm_conv_w"], Ws["ssm_conv_b"].reshape(1, -1), name="ssm_conv")
    dt_raw = zx[:, D_INNER + CONV_DIM:IN_PROJ_DIM]
    dtg = jnp.pad(dt_raw.reshape(T, SSM_GROUPS, 8).transpose(1, 0, 2), ((0, 0), (0, 0), (0, 120)))
    par = jnp.stack([Ws["ssm_dt_bias"].reshape(SSM_GROUPS, 8), Ws["ssm_a_log"].reshape(SSM_GROUPS, 8),
                     Ws["ssm_d"].reshape(SSM_GROUPS, 8)], axis=1)
    par = jnp.pad(par, ((0, 0), (0, 5), (0, 120)))
    gnw = _tie(Ws["ssm_gate_norm_w"].reshape(1, D_INNER), get_w("rest_start", xbc_c))
    y, yn, st = _ssd_fwd(xbc_c, zx, dtg, par, gnw, name="ssd_fwd")
    W0 = get_w("ffn0", y)
    Ws["ssm_out_w"] = W0["ssm_out_w"]
    h1 = _mm_fwd(yn, Ws["ssm_out_w"], residual=x, name="ssm_out", tm=1024, tn=512)
    h2, ffn0 = _ffn_fwd(h1, fnw[0], W0["up"], fcw[0], fcb[0], W0["down"], "0")
    Wr = get_w("rest", h2)
    q = _mm_fwd(h2, Wr["w_q"], norm_w=Ws["attn_norm_w"], out_dtype=BF16, name="attn_q", tm=1024, tn=1024)
    kv = _mm_fwd(h2, Wr["w_kv"], norm_w=Ws["kv_norm_w"], out_dtype=BF16, name="attn_kv", tm=1024, tn=1024)
    o, lt = _sba_fwd(q, kv, name="sba_fwd")
    h3 = _mm_fwd(o, Wr["w_o"], residual=h2, name="attn_o", tm=1024, tn=512)
    W1 = get_w("ffn1", h3)
    h4, ffn1 = _ffn_fwd(h3, fnw[1], W1["up"], fcw[1], fcb[1], W1["down"], "1")
    loss, dh4, g_final = _loss_head(h4, tgt, Ws["final_norm_w"], name="loss_head")
    dh3, gf1 = _ffn_bwd(dh4, h3, ffn1, fnw[1], W1["up"], fcw[1], fcb[1], W1["down"], "1")
    tok = put_g("ffn1", dict(up=gf1["up"], down=gf1["down"]))
    g_wo = _mm_tn(o, dh3, name="attn_o_wg", tn=1024, tt=1024)
    do = _mm_nt(dh3, _tie(Wr["w_o"], tok), name="attn_o_dg", out_dtype=BF16, tm=1024, tn=1024, tk=1024)
    dq, dk, dv = _sba_bwd(q, kv, lt, do, name="sba_bwd")
    g_wq = _mm_tn(h2, dq, norm_w=Ws["attn_norm_w"], name="attn_q_wg", tn=1024, tt=1024)
    dh2a, g_attn_nw = _mm_nt(dq, Wr["w_q"], epi=(h2, Ws["attn_norm_w"], dh3), name="attn_q_dg", tm=1024, tk=1024)
    dkv = jnp.concatenate([dk, dv], axis=1)
    g_wkv = _mm_tn(h2, dkv, norm_w=Ws["kv_norm_w"], name="attn_kv_wg", tn=1024, tt=1024)
    dh2, g_kv_nw = _mm_nt(dkv, Wr["w_kv"], epi=(h2, Ws["kv_norm_w"], dh2a), name="attn_kv_dg", tm=1024, tk=1024)
    tok = put_g("attn", dict(w_o=g_wo, w_q=g_wq, w_k=g_wkv[:, :D_MODEL], w_v=g_wkv[:, D_MODEL:]))
    dh1, gf0 = _ffn_bwd(dh2, h1, ffn0, fnw[0], W0["up"], fcw[0], _tie(fcb[0], tok), W0["down"], "0")
    tok = put_g("ffn0", dict(up=gf0["up"], down=gf0["down"]))
    g_out = _mm_tn(yn, dh1, name="ssm_out_wg", tn=1024, tt=1024)
    dyn = _mm_nt(dh1, _tie(Ws["ssm_out_w"], tok), name="ssm_out_dg", out_dtype=BF16, tm=1024, tn=1024, tk=1024)
    tok = put_g("ssm_out", dict(ssm_out_w=g_out))
    dxbc_c, dz, ddt, g_gnw, dpar = _ssd_bwd(xbc_c, zx, dtg, par, _tie(gnw, tok), y, st, dyn, name="ssd_bwd")
    dhid, g_scw, g_scb = _ssm_conv_bwd_pre(zx, Ws["ssm_conv_w"], Ws["ssm_conv_b"].reshape(1, -1), dxbc_c,
                                           name="ssm_conv_bwd")
    dzx = _conv_bwd_in(dhid, Ws["ssm_conv_w"], K=SSM_CONV, name="ssm_conv_bwd_in", into=(dz, D_INNER))
    ddt_t = ddt[:, :, :8].transpose(1, 0, 2).reshape(T, SSM_HEADS).astype(BF16)
    dzx = _put_cols(dzx, jnp.pad(ddt_t, ((0, 0), (0, IN_PROJ_PAD - IN_PROJ_DIM))), D_INNER + CONV_DIM, name="ssm_ddt_cols")
    g_in = _mm_tn_t(dzx, x, norm_w=Ws["ssm_norm_w"], name="ssm_in_wg", tn=1792, tt=1024)
    tok = put_g("ssm_in", dict(ssm_in_w=g_in[:IN_PROJ_DIM]))
    dx, g_ssm_nw = _mm_nt(dzx, Ws["in_w"], epi=(x, _tie(Ws["ssm_norm_w"], tok), dh1), name="ssm_in_dg", w_t=True,
                          tm=1024, tk=1792)
    f = {
        "ssm_norm_w": g_ssm_nw.reshape(-1), "ssm_conv_w": g_scw,
        "ssm_conv_b": g_scb.reshape(-1), "ssm_dt_bias": dpar[:, 0, :8].reshape(-1),
        "ssm_a_log": dpar[:, 1, :8].reshape(-1), "ssm_d": dpar[:, 2, :8].reshape(-1),
        "ssm_gate_norm_w": g_gnw.reshape(-1), "kv_norm_w": g_kv_nw.reshape(-1), "attn_norm_w": g_attn_nw.reshape(-1),
        "ffn_norm_w": jnp.stack([gf0["norm"], gf1["norm"]]), "ffn_conv_w": jnp.stack([gf0["conv_w"], gf1["conv_w"]]),
        "ffn_conv_b": jnp.stack([gf0["conv_b"], gf1["conv_b"]]), "final_norm_w": g_final.reshape(-1),
    }
    return loss, dx, f


def kernel(x, ssm_norm_w, ssm_in_w, ssm_conv_w, ssm_conv_b, ssm_dt_bias, ssm_a_log, ssm_d, ssm_gate_norm_w, ssm_out_w, kv_norm_w, w_k, w_v, attn_norm_w, w_q, w_o, ffn_norm_w, ffn_up_w, ffn_conv_w, ffn_conv_b, ffn_down_w, final_norm_w, loss_target, m_ssm_norm_w, m_ssm_in_w, m_ssm_conv_w, m_ssm_conv_b, m_ssm_dt_bias, m_ssm_a_log, m_ssm_d, m_ssm_gate_norm_w, m_ssm_out_w, m_kv_norm_w, m_w_k, m_w_v, m_attn_norm_w, m_w_q, m_w_o, m_ffn_norm_w, m_ffn_up_w, m_ffn_conv_w, m_ffn_conv_b, m_ffn_down_w, m_final_norm_w, v_ssm_norm_w, v_ssm_in_w, v_ssm_conv_w, v_ssm_conv_b, v_ssm_dt_bias, v_ssm_a_log, v_ssm_d, v_ssm_gate_norm_w, v_ssm_out_w, v_kv_norm_w, v_w_k, v_w_v, v_attn_norm_w, v_w_q, v_w_o, v_ffn_norm_w, v_ffn_up_w, v_ffn_conv_w, v_ffn_conv_b, v_ffn_down_w, v_final_norm_w):
    env = dict(locals())
    p = {n: env[n] for n in _WEIGHTS}
    mom = {n: env["m_" + n] for n in _WEIGHTS}
    var = {n: env["v_" + n] for n in _WEIGHTS}
    T = x.shape[1]
    me = 4 * lax.axis_index("x") + 2 * lax.axis_index("y") + lax.axis_index("c")
    rs = D_FF // N_DEV

    def bf2(a):
        return _as2d(a).astype(BF16)

    _T = ("ssm_in_w", "ffn_up_w")

    def t2d(a):
        return jnp.swapaxes(a, -1, -2).reshape(-1, a.shape[-2])

    def from_t2d(a, like):
        return jnp.swapaxes(a.reshape(like.shape[:-2] + (like.shape[-1], like.shape[-2])), -1, -2)

    n_in, n_up = p["ssm_in_w"].shape[-1], p["ffn_up_w"].shape[-1]

    def with_own(srcs, lands, scatter):
        out = []
        for s, l in zip(srcs, lands):
            own = lax.dynamic_index_in_dim(s, me, 0, keepdims=False) if scatter else s
            out.append(lax.dynamic_update_index_in_dim(l, own, me, 0))
        return out

    a_names = ["ssm_in_w"] + _SMALL_SHARDED
    got_a = dict(zip(a_names, _all_gather([t2d(p["ssm_in_w"]).astype(BF16)] + [_as2d(p[n]) for n in _SMALL_SHARDED],
                                          name="gather_ssm")))
    ffn0_names = ["ssm_out_w", "up0", "down0"]
    rest_names = ["w_q", "w_k", "w_v", "w_o"]
    up_t = jnp.swapaxes(p["ffn_up_w"], -1, -2).astype(BF16)
    shard = {"up0": up_t[0], "down0": bf2(p["ffn_down_w"][0]), "up1": up_t[1],
             "down1": bf2(p["ffn_down_w"][1]), "w_q": bf2(p["w_q"]), "w_k": bf2(p["w_k"]), "w_v": bf2(p["w_v"]),
             "w_o": bf2(p["w_o"]), "ssm_out_w": bf2(p["ssm_out_w"])}

    def anchored(a, on):
        return a + (jnp.where(jnp.isfinite(on), on, 0.0) * 0.0).astype(a.dtype)

    h_ffn0 = _push_start([anchored(shard[ffn0_names[0]], got_a["ssm_norm_w"][0, 0, 0])]
                         + [shard[n] for n in ffn0_names[1:]], scatter=False, name="gather_ffn0_start")
    handles = {}

    def get_w(group, after):
        if group == "ssm":
            W = {n: p[n] for n in _SMALL_REPL}
            for n in ("ssm_dt_bias", "ssm_a_log", "ssm_d", "attn_norm_w"):
                W[n] = W[n].reshape(-1)
            W["in_w"] = jnp.pad(got_a["ssm_in_w"].reshape(IN_PROJ_DIM, D_MODEL), ((0, IN_PROJ_PAD - IN_PROJ_DIM), (0, 0)))
            W["ssm_norm_w"] = _tie(got_a["ssm_norm_w"].reshape(D_MODEL), h_ffn0["token"])
            W["ssm_conv_w"] = _cols_to_full(got_a["ssm_conv_w"])
            W["ssm_conv_b"] = got_a["ssm_conv_b"].reshape(CONV_DIM)
            W["ssm_gate_norm_w"] = got_a["ssm_gate_norm_w"].reshape(D_INNER)
            W["ffn_conv_w"] = _cols_to_full(got_a["ffn_conv_w"]).reshape(2, FFN_CONV, 2 * D_FF)
            return W
        if group == "rest_start":
            handles["rest"] = _push_start([anchored(shard[rest_names[0]], after[0, 0])]
                                          + [shard[n] for n in rest_names[1:]], scatter=False, name="gather_rest_start")
            handles["ffn1"] = _push_start([anchored(shard["up1"], handles["rest"]["token"][0, 0]), shard["down1"]],
                                          scatter=False, name="gather_ffn1_start")
            return handles["ffn1"]["token"]
        if group == "ffn1":
            srcs, lands = _push_wait(handles["ffn1"], after, name="gather_ffn1_wait")
            up, down = with_own(srcs, lands, False)
            return dict(up=up.reshape(2 * D_FF, D_MODEL), down=down.reshape(D_FF, D_MODEL))
        if group == "ffn0":
            srcs, lands = _push_wait(h_ffn0, after, name="gather_ffn0_wait")
            out, up, down = with_own(srcs, lands, False)
            return dict(ssm_out_w=out.reshape(D_INNER, D_MODEL), up=up.reshape(2 * D_FF, D_MODEL),
                        down=down.reshape(D_FF, D_MODEL))
        srcs, lands = _push_wait(handles["rest"], after, name="gather_rest_wait")
        g = dict(zip(rest_names, with_own(srcs, lands, False)))
        sq = lambda a: a.reshape(D_MODEL, D_MODEL)
        return dict(w_q=sq(g["w_q"]), w_kv=jnp.concatenate([sq(g["w_k"]), sq(g["w_v"])], axis=1), w_o=sq(g["w_o"]))

    pending = []

    def put_g(group, g):
        if group in ("ffn0", "ffn1"):
            keys = [("ffn_up_w", int(group[-1])), ("ffn_down_w", int(group[-1]))]
            blocks = [g["up"].reshape(N_DEV, n_up, D_MODEL), g["down"].reshape(N_DEV, rs, D_MODEL)]
        elif group == "attn":
            keys = [(n, None) for n in ("w_o", "w_q", "w_k", "w_v")]
            blocks = [g[n].reshape(N_DEV, D_MODEL // N_DEV, D_MODEL) for n, _ in keys]
        elif group == "ssm_out":
            keys = [("ssm_out_w", None)]
            blocks = [g["ssm_out_w"].reshape(N_DEV, D_INNER // N_DEV, D_MODEL)]
        else:
            keys = [("ssm_in_w", None)]
            blocks = [g["ssm_in_w"].reshape(N_DEV, n_in, D_MODEL)]
        h = _push_start(blocks, scatter=True, name=f"exchange_{group}_start")
        pending.append((group, keys, h))
        return h["token"]

    loss_row, dx, f = _local_step(x.reshape(T, D_MODEL), loss_target.reshape(T, D_MODEL), get_w, put_g)

    small_names = _SMALL_REPL + _SMALL_SHARDED
    small_full = _pack_small([f[n] for n in small_names] + [loss_row[0, 0:1]])
    small_bcast = jnp.broadcast_to(small_full[None], (N_DEV,) + small_full.shape)
    h_small = _push_start([small_bcast], scatter=True, name="exchange_small_start")
    tok = h_small["token"]

    arrived, res = {}, {}
    after = dx
    for group, keys, h in pending:
        srcs, lands = _push_wait(h, after, name=f"exchange_{group}_wait")
        arrived.update(zip(keys, with_own(srcs, lands, True)))
        for n in _BIG:
            layered = (n, 0) in arrived or (n, 1) in arrived
            if n in res or not ((n, None) in arrived or ((n, 0) in arrived and (n, 1) in arrived)):
                continue
            parts = [arrived[(n, 0)], arrived[(n, 1)]] if layered else arrived[(n, None)]
            w2, m2, v2 = ((t2d if n in _T else _as2d)(a[n]) for a in (p, mom, var))
            if not res:
                w2 = _tie(w2, tok)
            tiles = {"ffn_down_w": dict(tr=rs), "ffn_up_w": dict(tr=n_up // 2), "ssm_in_w": dict(tr=n_in, tc=256)}
            res[n] = _adamw(parts, w2, m2, v2, name=f"adamw_{n}", **tiles.get(n, dict(tr=256)))
            after = res[n][0]
    srcs, lands = _push_wait(h_small, after, name="exchange_small_wait")
    small_parts = with_own(srcs, lands, True)[0]
    out_g, out_d, out_m, out_v = {}, {}, {}, {}
    for n in _BIG:
        out_g[n], out_d[n], out_m[n], out_v[n] = (from_t2d(t, p[n]) if n in _T else t.reshape(p[n].shape) for t in res[n])

    zero = jnp.zeros_like(small_full)
    g_small_sum = _adamw(small_parts, zero, zero, zero, name="sum_small_grads", tr=small_full.shape[0])[0]
    *small_sums, loss_sum = _unpack_small(g_small_sum, [f[n].shape for n in small_names] + [(1,)])
    loss = loss_sum[0]
    g_small = dict(zip(small_names, small_sums))
    for n in _SMALL_SHARDED:
        width = p[n].shape[-1]
        g_small[n] = lax.dynamic_slice_in_dim(g_small[n], me * width, width, axis=g_small[n].ndim - 1)
    sw = _pack_small([p[n] for n in small_names])
    sm = _pack_small([mom[n] for n in small_names])
    sv = _pack_small([var[n] for n in small_names])
    sg = _pack_small([g_small[n] for n in small_names])
    _, d, nm, nv = _adamw(sg[None], sw, sm, sv, name="adamw_small", tr=sw.shape[0])
    shard_shapes = [p[n].shape for n in small_names]
    for n, dd, mm, vv in zip(small_names, _unpack_small(d, shard_shapes), _unpack_small(nm, shard_shapes),
                             _unpack_small(nv, shard_shapes)):
        out_g[n] = g_small[n].reshape(p[n].shape)
        out_d[n], out_m[n], out_v[n] = dd, mm, vv

    return (loss, dx.reshape(x.shape), *[out_g[n] for n in _WEIGHTS], *[out_d[n] for n in _WEIGHTS],
            *[out_m[n] for n in _WEIGHTS], *[out_v[n] for n in _WEIGHTS])
```

```python
import functools
import math

import jax
import jax.numpy as jnp
from jax import lax
from jax.experimental import pallas as pl
from jax.experimental.pallas import tpu as pltpu

F32 = jnp.float32
BF16 = jnp.bfloat16
EPS = 1e-6

D_MODEL = 1024
D_INNER = 2048
SSM_HEADS = 32
SSM_GROUPS = 4
SSM_STATE = 128
SSM_CONV = 4
SSM_CHUNK = 128
GN = SSM_GROUPS * SSM_STATE
CONV_DIM = D_INNER + 2 * GN
IN_PROJ_DIM = D_INNER + CONV_DIM + SSM_HEADS
IN_PROJ_PAD = 5376
SB_HEADS = 16
SB_HEAD_DIM = 64
SB_BLOCK = 128
D_FF = 2816
FFN_CONV = 3
N_DEV = 8

ADAM_LR = 0.001
ADAM_B1 = 0.9
ADAM_B2 = 0.999
ADAM_EPS = 1e-08
ADAM_WD = 0.01
ADAM_STEP = 10

_MESH = pl.DeviceIdType.MESH
_NT = (((1,), (1,)), ((), ()))
_TN = (((0,), (0,)), ((), ()))
_ANY = pl.BlockSpec(memory_space=pl.ANY)


def _cparams(sem, vmem_mb=48):
    return pltpu.CompilerParams(dimension_semantics=sem, vmem_limit_bytes=vmem_mb * 1024 * 1024)


def _sigmoid(x):
    return 0.5 * jnp.tanh(0.5 * x) + 0.5


def _softplus(x):
    return jnp.maximum(x, 0.0) + jnp.log(1.0 + jnp.exp(-jnp.abs(x)))


def _rms_fwd(xv, w):
    r = lax.rsqrt(jnp.mean(xv * xv, axis=-1, keepdims=True) + EPS)
    return xv * r * w


def _mm_fwd(x, w, *, name, norm_w=None, residual=None, out_dtype=F32, tm=512, tn=512, halves=False, w_t=False):
    M, K = x.shape
    N = w.shape[0] if w_t else w.shape[1]
    tm, tn = min(tm, M), min(tn, N)
    assert M % tm == 0 and N % tn == 0, (name, M, N, tm, tn)
    if halves:
        nbh = N // 2 // tn
        assert N // 2 % tn == 0
        out_spec = pl.BlockSpec((None, tm, tn), lambda i, j: (lax.div(j, nbh), i, lax.rem(j, nbh)))
        out_shape = jax.ShapeDtypeStruct((2, M, N // 2), out_dtype)
    else:
        out_spec = pl.BlockSpec((tm, tn), lambda i, j: (i, j))
        out_shape = jax.ShapeDtypeStruct((M, N), out_dtype)
    has_norm, has_res = norm_w is not None, residual is not None

    def body(*refs):
        x_ref, w_ref = refs[0], refs[1]
        p = 2
        nw_ref = r_ref = None
        if has_norm:
            nw_ref = refs[p]
            p += 1
        if has_res:
            r_ref = refs[p]
            p += 1
        o_ref = refs[p]
        xv = x_ref[...]
        if has_norm:
            xv = _rms_fwd(xv.astype(F32), nw_ref[...])
        acc = lax.dot_general(xv.astype(BF16), w_ref[...], _NT if w_t else (((1,), (0,)), ((), ())),
                              preferred_element_type=F32)
        if has_res:
            acc = acc + r_ref[...]
        o_ref[...] = acc.astype(out_dtype)

    w_spec = pl.BlockSpec((tn, K), lambda i, j: (j, 0)) if w_t else pl.BlockSpec((K, tn), lambda i, j: (0, j))
    in_specs = [pl.BlockSpec((tm, K), lambda i, j: (i, 0)), w_spec]
    args = [x, w]
    if has_norm:
        in_specs.append(pl.BlockSpec((1, K), lambda i, j: (0, 0)))
        args.append(norm_w.reshape(1, K))
    if has_res:
        in_specs.append(pl.BlockSpec((tm, tn), lambda i, j: (i, j)))
        args.append(residual)
    return pl.pallas_call(
        body, name=name, grid=(M // tm, N // tn), in_specs=in_specs,
        out_specs=out_spec, out_shape=out_shape,
        compiler_params=_cparams(("parallel", "parallel")))(*args)


def _mm_nt(dy, w, *, name, epi=None, out_dtype=F32, tm=512, tn=512, tk=512, w_t=False):
    halves = dy.ndim == 3
    M, K = (dy.shape[1], 2 * dy.shape[2]) if halves else dy.shape
    N = w.shape[1] if w_t else w.shape[0]
    tm, tk = min(tm, M), min(tk, K)
    tn = N if epi is not None else min(tn, N)
    assert M % tm == 0 and N % tn == 0 and K % tk == 0, (name, M, N, K, tm, tn, tk)
    nk = K // tk
    has_epi = epi is not None

    def body(*refs):
        if has_epi:
            dy_ref, w_ref, h_ref, nw_ref, r_ref, o_ref, dnw_ref, acc_ref = refs
        else:
            dy_ref, w_ref, o_ref, acc_ref = refs
        i = pl.program_id(0)
        k = pl.program_id(2)

        @pl.when(k == 0)
        def _():
            acc_ref[...] = jnp.zeros_like(acc_ref)

        acc_ref[...] += lax.dot_general(dy_ref[...].astype(BF16), w_ref[...], (((1,), (0,)), ((), ())) if w_t else _NT,
                                        preferred_element_type=F32)

        @pl.when(k == nk - 1)
        def _():
            du = acc_ref[...]
            if has_epi:
                hv = h_ref[...]
                r = lax.rsqrt(jnp.mean(hv * hv, axis=-1, keepdims=True) + EPS)
                xhat = hv * r
                dxh = du * nw_ref[...]
                dx = r * (dxh - xhat * jnp.mean(dxh * xhat, axis=-1, keepdims=True))
                o_ref[...] = (r_ref[...] + dx).astype(out_dtype)
                contrib = jnp.sum(du * xhat, axis=0, keepdims=True)

                @pl.when(i == 0)
                def _():
                    dnw_ref[...] = contrib

                @pl.when(i > 0)
                def _():
                    dnw_ref[...] += contrib
            else:
                o_ref[...] = du.astype(out_dtype)

    if halves:
        nkh = K // 2 // tk
        assert K // 2 % tk == 0
        dy_spec = pl.BlockSpec((None, tm, tk), lambda i, j, k: (lax.div(k, nkh), i, lax.rem(k, nkh)))
    else:
        dy_spec = pl.BlockSpec((tm, tk), lambda i, j, k: (i, k))
    w_spec = pl.BlockSpec((tk, tn), lambda i, j, k: (k, j)) if w_t else pl.BlockSpec((tn, tk), lambda i, j, k: (j, k))
    in_specs = [dy_spec, w_spec]
    args = [dy, w]
    out_specs = [pl.BlockSpec((tm, tn), lambda i, j, k: (i, j))]
    out_shape = [jax.ShapeDtypeStruct((M, N), out_dtype)]
    if has_epi:
        h, nw, res = epi
        in_specs += [pl.BlockSpec((tm, N), lambda i, j, k: (i, 0)), pl.BlockSpec((1, N), lambda i, j, k: (0, 0)),
                     pl.BlockSpec((tm, N), lambda i, j, k: (i, 0))]
        args += [h, nw.reshape(1, N), res]
        out_specs.append(pl.BlockSpec((1, N), lambda i, j, k: (0, 0)))
        out_shape.append(jax.ShapeDtypeStruct((1, N), F32))
    outs = pl.pallas_call(
        body, name=name, grid=(M // tm, N // tn, nk), in_specs=in_specs, out_specs=out_specs, out_shape=out_shape,
        scratch_shapes=[pltpu.VMEM((tm, tn), F32)],
        compiler_params=_cparams(("arbitrary", "arbitrary", "arbitrary")))(*args)
    return (outs[0], outs[1]) if has_epi else outs[0]


def _mm_tn(x, dy, *, name, norm_w=None, out_dtype=BF16, tk1=1024, tn=512, tt=512):
    T, K1 = x.shape
    halves = dy.ndim == 3
    N = 2 * dy.shape[2] if halves else dy.shape[1]
    tk1, tn, tt = min(tk1, K1), min(tn, N), min(tt, T)
    has_norm = norm_w is not None
    assert K1 % tk1 == 0 and N % tn == 0 and T % tt == 0, (name, K1, N, T, tk1, tn, tt)
    assert not has_norm or tk1 == K1
    nt = T // tt

    def body(*refs):
        if has_norm:
            x_ref, dy_ref, nw_ref, o_ref, acc_ref = refs
        else:
            x_ref, dy_ref, o_ref, acc_ref = refs
        t = pl.program_id(2)

        @pl.when(t == 0)
        def _():
            acc_ref[...] = jnp.zeros_like(acc_ref)

        xv = x_ref[...]
        if has_norm:
            xv = _rms_fwd(xv.astype(F32), nw_ref[...])
        acc_ref[...] += lax.dot_general(xv.astype(BF16), dy_ref[...].astype(BF16), _TN, preferred_element_type=F32)

        @pl.when(t == nt - 1)
        def _():
            o_ref[...] = acc_ref[...].astype(out_dtype)

    if halves:
        nbh = N // 2 // tn
        assert N // 2 % tn == 0
        dy_spec = pl.BlockSpec((None, tt, tn), lambda a, b, t: (lax.div(b, nbh), t, lax.rem(b, nbh)))
    else:
        dy_spec = pl.BlockSpec((tt, tn), lambda a, b, t: (t, b))
    in_specs = [pl.BlockSpec((tt, tk1), lambda a, b, t: (t, a)), dy_spec]
    args = [x, dy]
    if has_norm:
        in_specs.append(pl.BlockSpec((1, K1), lambda a, b, t: (0, 0)))
        args.append(norm_w.reshape(1, K1))
    return pl.pallas_call(
        body, name=name, grid=(K1 // tk1, N // tn, nt), in_specs=in_specs,
        out_specs=pl.BlockSpec((tk1, tn), lambda a, b, t: (a, b)),
        out_shape=jax.ShapeDtypeStruct((K1, N), out_dtype),
        scratch_shapes=[pltpu.VMEM((tk1, tn), F32)],
        compiler_params=_cparams(("parallel", "parallel", "arbitrary")))(*args)


def _mm_tn_t(dy, x, *, name, norm_w, out_dtype=BF16, tn=1408, tt=1024, vmem_mb=48):
    T, K1 = x.shape
    halves = dy.ndim == 3
    N = 2 * dy.shape[2] if halves else dy.shape[1]
    tn, tt = min(tn, N), min(tt, T)
    assert N % tn == 0 and T % tt == 0, (name, N, T, tn, tt)
    nt = T // tt

    def body(dy_ref, x_ref, nw_ref, o_ref, acc_ref):
        t = pl.program_id(1)

        @pl.when(t == 0)
        def _():
            acc_ref[...] = jnp.zeros_like(acc_ref)

        xn = _rms_fwd(x_ref[...].astype(F32), nw_ref[...]).astype(BF16)
        acc_ref[...] += lax.dot_general(dy_ref[...].astype(BF16), xn, _TN, preferred_element_type=F32)

        @pl.when(t == nt - 1)
        def _():
            o_ref[...] = acc_ref[...].astype(out_dtype)

    if halves:
        nbh = N // 2 // tn
        assert N // 2 % tn == 0
        dy_spec = pl.BlockSpec((None, tt, tn), lambda b, t: (lax.div(b, nbh), t, lax.rem(b, nbh)))
    else:
        dy_spec = pl.BlockSpec((tt, tn), lambda b, t: (t, b))
    return pl.pallas_call(
        body, name=name, grid=(N // tn, nt),
        in_specs=[dy_spec, pl.BlockSpec((tt, K1), lambda b, t: (t, 0)), pl.BlockSpec((1, K1), lambda b, t: (0, 0))],
        out_specs=pl.BlockSpec((tn, K1), lambda b, t: (b, 0)),
        out_shape=jax.ShapeDtypeStruct((N, K1), out_dtype),
        scratch_shapes=[pltpu.VMEM((tn, K1), F32)],
        compiler_params=_cparams(("parallel", "arbitrary"), vmem_mb))(dy, x, norm_w.reshape(1, K1))


def _shift_down(xb, prev8, j):
    main = pltpu.roll(xb, j, 0)
    head = pltpu.roll(xb[0:8], j, 0)
    ph = pltpu.roll(prev8, j, 0)
    row8 = lax.broadcasted_iota(jnp.int32, head.shape, 0)
    head = jnp.where(row8 < j, ph, head)
    return jnp.concatenate([head, main[8:]], axis=0)


def _shift_up(xb, next8, j):
    tt = xb.shape[0]
    main = pltpu.roll(xb, tt - j, 0)
    tail = pltpu.roll(xb[tt - 8:tt], 8 - j, 0)
    nh = pltpu.roll(next8, 8 - j, 0)
    row8 = lax.broadcasted_iota(jnp.int32, tail.shape, 0)
    tail = jnp.where(row8 + j >= 8, nh, tail)
    return jnp.concatenate([main[:tt - 8], tail], axis=0)


def _conv_hid(xb, prev8, w, b_row, K):
    out = b_row
    shifted = []
    for j in range(K):
        sh = K - 1 - j
        xs = xb if sh == 0 else _shift_down(xb, prev8, sh)
        shifted.append(xs)
        out = out + xs * w[j:j + 1, :]
    return out, shifted


def _prev_idx(i, nb8):
    return jnp.maximum(i * nb8 - 1, 0)


def _ssm_conv_fwd(zx, w, b, *, name, tt=512, tc=512):
    T = zx.shape[0]
    tt = min(tt, T)
    C, K = CONV_DIM, SSM_CONV
    cb0, nb8 = D_INNER // tc, tt // 8

    def body(x_ref, p_ref, w_ref, b_ref, o_ref):
        first = (pl.program_id(1) > 0).astype(F32)
        hid, _ = _conv_hid(x_ref[...], p_ref[...] * first, w_ref[...], b_ref[...], K)
        o_ref[...] = hid * _sigmoid(hid)

    return pl.pallas_call(
        body, name=name, grid=(C // tc, T // tt),
        in_specs=[pl.BlockSpec((tt, tc), lambda c, i: (i, c + cb0)),
                  pl.BlockSpec((8, tc), lambda c, i: (_prev_idx(i, nb8), c + cb0)),
                  pl.BlockSpec((K, tc), lambda c, i: (0, c)), pl.BlockSpec((1, tc), lambda c, i: (0, c))],
        out_specs=pl.BlockSpec((tt, tc), lambda c, i: (i, c)),
        out_shape=jax.ShapeDtypeStruct((T, C), F32),
        compiler_params=_cparams(("parallel", "parallel")))(zx, zx, w, b)


def _ssm_conv_bwd_pre(zx, w, b, dout, *, name, tt=512, tc=512):
    T = zx.shape[0]
    tt = min(tt, T)
    C, K = CONV_DIM, SSM_CONV
    cb0, nb8 = D_INNER // tc, tt // 8

    def body(x_ref, p_ref, w_ref, b_ref, d_ref, dh_ref, dw_ref, db_ref):
        t = pl.program_id(1)
        first = (t > 0).astype(F32)
        hid, shifted = _conv_hid(x_ref[...], p_ref[...] * first, w_ref[...], b_ref[...], K)
        sg = _sigmoid(hid)
        dh = d_ref[...] * (sg * (1.0 + hid * (1.0 - sg)))
        dh_ref[...] = dh

        @pl.when(t == 0)
        def _():
            dw_ref[...] = jnp.zeros_like(dw_ref)
            db_ref[...] = jnp.zeros_like(db_ref)

        db_ref[...] += jnp.sum(dh, axis=0, keepdims=True)
        for j in range(K):
            dw_ref[j:j + 1, :] += jnp.sum(dh * shifted[j], axis=0, keepdims=True)

    return pl.pallas_call(
        body, name=name, grid=(C // tc, T // tt),
        in_specs=[pl.BlockSpec((tt, tc), lambda c, i: (i, c + cb0)),
                  pl.BlockSpec((8, tc), lambda c, i: (_prev_idx(i, nb8), c + cb0)),
                  pl.BlockSpec((K, tc), lambda c, i: (0, c)), pl.BlockSpec((1, tc), lambda c, i: (0, c)),
                  pl.BlockSpec((tt, tc), lambda c, i: (i, c))],
        out_specs=[pl.BlockSpec((tt, tc), lambda c, i: (i, c)), pl.BlockSpec((K, tc), lambda c, i: (0, c)),
                   pl.BlockSpec((1, tc), lambda c, i: (0, c))],
        out_shape=[jax.ShapeDtypeStruct((T, C), F32), jax.ShapeDtypeStruct((K, C), F32),
                   jax.ShapeDtypeStruct((1, C), F32)],
        compiler_params=_cparams(("parallel", "arbitrary")))(zx, zx, w, b, dout)


def _put_cols(buf, src, col0, *, name, tt=512):
    T, C = src.shape
    tt = min(tt, T)

    def body(s_ref, _, o_ref):
        o_ref[...] = s_ref[...]

    return pl.pallas_call(
        body, name=name, grid=(T // tt,),
        in_specs=[pl.BlockSpec((tt, C), lambda i: (i, 0)), _ANY],
        out_specs=pl.BlockSpec((tt, C), lambda i: (i, col0 // C)),
        out_shape=jax.ShapeDtypeStruct(buf.shape, buf.dtype), input_output_aliases={1: 0},
        compiler_params=_cparams(("parallel",)))(src, buf)


def _conv_bwd_in(dh, w, *, name, K, tt=512, tc=512, out_dtype=BF16, into=None):
    T, C = dh.shape
    tt = min(tt, T)
    nb8, nT = tt // 8, T // tt
    last8 = T // 8 - 1
    cb0 = 0 if into is None else into[1] // tc

    def body(d_ref, n_ref, w_ref, *rest):
        o_ref = rest[-1]
        notlast = (pl.program_id(1) < nT - 1).astype(F32)
        d = d_ref[...]
        nxt = n_ref[...] * notlast
        w_ = w_ref[...]
        acc = d * w_[K - 1:K, :]
        for sh in range(1, K):
            acc = acc + _shift_up(d, nxt, sh) * w_[K - 1 - sh:K - sh, :]
        o_ref[...] = acc.astype(out_dtype)

    in_specs = [pl.BlockSpec((tt, tc), lambda c, i: (i, c)),
                pl.BlockSpec((8, tc), lambda c, i: (jnp.minimum((i + 1) * nb8, last8), c)),
                pl.BlockSpec((K, tc), lambda c, i: (0, c))]
    args = [dh, dh, w]
    if into is None:
        out_shape, alias = jax.ShapeDtypeStruct((T, C), out_dtype), {}
    else:
        assert into[0].dtype == out_dtype and into[1] % tc == 0
        in_specs.append(_ANY)
        args.append(into[0])
        out_shape, alias = jax.ShapeDtypeStruct(into[0].shape, out_dtype), {3: 0}
    return pl.pallas_call(
        body, name=name, grid=(C // tc, nT), in_specs=in_specs,
        out_specs=pl.BlockSpec((tt, tc), lambda c, i: (i, c + cb0)),
        out_shape=out_shape, input_output_aliases=alias,
        compiler_params=_cparams(("parallel", "parallel")))(*args)


def _ffn_conv_fwd3(a3, w, b, *, name, tt=256, tc=1408):
    T = a3.shape[1]
    tt = min(tt, T)
    K, nbh, n16 = FFN_CONV, D_FF // tc, tt // 16

    def body(a_ref, p_ref, wg_ref, wv_ref, bg_ref, bv_ref, o_ref):
        first = (pl.program_id(1) > 0).astype(F32)
        a = a_ref[...].astype(F32)
        prev = p_ref[...].astype(F32)[:, 8:16, :] * first
        hg, _ = _conv_hid(a[0], prev[0], wg_ref[...], bg_ref[...], K)
        hv, _ = _conv_hid(a[1], prev[1], wv_ref[...], bv_ref[...], K)
        o_ref[...] = (hg * _sigmoid(hg) * hv).astype(BF16)

    return pl.pallas_call(
        body, name=name, grid=(nbh, T // tt),
        in_specs=[pl.BlockSpec((2, tt, tc), lambda c, i: (0, i, c)),
                  pl.BlockSpec((2, 16, tc), lambda c, i: (0, _prev_idx(i, n16), c)),
                  pl.BlockSpec((K, tc), lambda c, i: (0, c)), pl.BlockSpec((K, tc), lambda c, i: (0, c + nbh)),
                  pl.BlockSpec((1, tc), lambda c, i: (0, c)), pl.BlockSpec((1, tc), lambda c, i: (0, c + nbh))],
        out_specs=pl.BlockSpec((tt, tc), lambda c, i: (i, c)),
        out_shape=jax.ShapeDtypeStruct((T, D_FF), BF16),
        compiler_params=_cparams(("parallel", "parallel")))(a3, a3, w, w, b, b)


def _ffn_conv_bwd3(a3, w, b, dp, *, name, tt=256, tc=1408):
    T = a3.shape[1]
    tt = min(tt, T)
    K, nbh, n16 = FFN_CONV, D_FF // tc, tt // 16

    def body(a_ref, p_ref, wg_ref, wv_ref, bg_ref, bv_ref, dp_ref, dh_ref, dw_ref, db_ref):
        t = pl.program_id(1)
        first = (t > 0).astype(F32)
        a = a_ref[...].astype(F32)
        prev = p_ref[...].astype(F32)[:, 8:16, :] * first
        hg, sh_g = _conv_hid(a[0], prev[0], wg_ref[...], bg_ref[...], K)
        hv, sh_v = _conv_hid(a[1], prev[1], wv_ref[...], bv_ref[...], K)
        sg = _sigmoid(hg)
        d = dp_ref[...].astype(F32)
        dhg = d * hv * (sg * (1.0 + hg * (1.0 - sg)))
        dhv = d * (hg * sg)
        dh_ref[0] = dhg.astype(BF16)
        dh_ref[1] = dhv.astype(BF16)

        @pl.when(t == 0)
        def _():
            dw_ref[...] = jnp.zeros_like(dw_ref)
            db_ref[...] = jnp.zeros_like(db_ref)

        db_ref[0] += jnp.sum(dhg, axis=0, keepdims=True)
        db_ref[1] += jnp.sum(dhv, axis=0, keepdims=True)
        for j in range(K):
            dw_ref[0, j:j + 1, :] += jnp.sum(dhg * sh_g[j], axis=0, keepdims=True)
            dw_ref[1, j:j + 1, :] += jnp.sum(dhv * sh_v[j], axis=0, keepdims=True)

    return pl.pallas_call(
        body, name=name, grid=(nbh, T // tt),
        in_specs=[pl.BlockSpec((2, tt, tc), lambda c, i: (0, i, c)),
                  pl.BlockSpec((2, 16, tc), lambda c, i: (0, _prev_idx(i, n16), c)),
                  pl.BlockSpec((K, tc), lambda c, i: (0, c)), pl.BlockSpec((K, tc), lambda c, i: (0, c + nbh)),
                  pl.BlockSpec((1, tc), lambda c, i: (0, c)), pl.BlockSpec((1, tc), lambda c, i: (0, c + nbh)),
                  pl.BlockSpec((tt, tc), lambda c, i: (i, c))],
        out_specs=[pl.BlockSpec((2, tt, tc), lambda c, i: (0, i, c)), pl.BlockSpec((2, K, tc), lambda c, i: (0, 0, c)),
                   pl.BlockSpec((2, 1, tc), lambda c, i: (0, 0, c))],
        out_shape=[jax.ShapeDtypeStruct((2, T, D_FF), BF16), jax.ShapeDtypeStruct((2, K, D_FF), F32),
                   jax.ShapeDtypeStruct((2, 1, D_FF), F32)],
        compiler_params=_cparams(("parallel", "arbitrary")))(a3, a3, w, w, b, b, dp)


def _conv_bwd_in3(dh3, w, *, name, K, tt=256, tc=1408):
    H, T, C = dh3.shape
    tt = min(tt, T)
    nb, n16, nT = C // tc, tt // 16, T // tt
    last16 = T // 16 - 1

    def body(d_ref, n_ref, w_ref, o_ref):
        notlast = (pl.program_id(2) < nT - 1).astype(F32)
        d = d_ref[...].astype(F32)
        nxt = n_ref[...].astype(F32)[0:8, :] * notlast
        w_ = w_ref[...]
        acc = d * w_[K - 1:K, :]
        for sh in range(1, K):
            acc = acc + _shift_up(d, nxt, sh) * w_[K - 1 - sh:K - sh, :]
        o_ref[...] = acc.astype(BF16)

    return pl.pallas_call(
        body, name=name, grid=(H, nb, nT),
        in_specs=[pl.BlockSpec((None, tt, tc), lambda h, c, i: (h, i, c)),
                  pl.BlockSpec((None, 16, tc), lambda h, c, i: (h, jnp.minimum((i + 1) * n16, last16), c)),
                  pl.BlockSpec((K, tc), lambda h, c, i: (0, h * nb + c))],
        out_specs=pl.BlockSpec((None, tt, tc), lambda h, c, i: (h, i, c)),
        out_shape=jax.ShapeDtypeStruct((H, T, C), BF16),
        compiler_params=_cparams(("parallel", "parallel", "parallel")))(dh3, dh3, w)


def _cumsum_rows(x):
    L = x.shape[0]
    row = lax.broadcasted_iota(jnp.int32, x.shape, 0)
    k = 1
    while k < L:
        x = x + jnp.where(row >= k, pltpu.roll(x, k, 0), 0.0)
        k *= 2
    return x


def _rcumsum_rows(x):
    L = x.shape[0]
    row = lax.broadcasted_iota(jnp.int32, x.shape, 0)
    k = 1
    while k < L:
        x = x + jnp.where(row < L - k, pltpu.roll(x, L - k, 0), 0.0)
        k *= 2
    return x


def _split_terms(m, n):
    terms, rest = [], m
    for _ in range(n):
        t = rest.astype(BF16)
        terms.append(t)
        rest = rest - t.astype(F32)
    return jnp.concatenate(terms, axis=1)


def _select_dot(m, n_terms, n_out, cond):
    K = m.shape[1]
    k = lax.broadcasted_iota(jnp.int32, (K, n_out), 0)
    j = lax.broadcasted_iota(jnp.int32, (K, n_out), 1)
    sel = cond(k, j).astype(BF16)
    return jnp.dot(_split_terms(m, n_terms), jnp.concatenate([sel] * n_terms, axis=0), preferred_element_type=F32)


def _rowsum_mxu(m):
    return _select_dot(m, 2, 128, lambda k, j: k >= 0)


def _lane_block_sums(m, width):
    shift = width.bit_length() - 1
    return _select_dot(m, 2, 128, lambda k, j: j == jnp.right_shift(k, shift))


def _heads_to_pairs(m):
    return _select_dot(m, 3, 512, lambda k, j: k == jnp.right_shift(j, 6))


def _ssd_common(dt_ref, par_ref):
    par = par_ref[...]
    raw = dt_ref[...] + par[0:1, :]
    dt = _softplus(raw)
    a = -jnp.exp(par[1:2, :])
    cs = _cumsum_rows(dt * a)
    L = cs.shape[0]
    cs_last = cs[L - 1:L, :]
    return raw, dt, a, par[2:3, :], cs, cs.T, jnp.exp(cs), jnp.exp(cs_last - cs), jnp.exp(cs_last)


def _ssd_specs(nc, rev):
    L = SSM_CHUNK

    def ci(c):
        return nc - 1 - c if rev else c

    return [pl.BlockSpec((L, D_INNER), lambda c: (ci(c), 0)),
            pl.BlockSpec((L, GN), lambda c: (ci(c), D_INNER // GN)),
            pl.BlockSpec((L, GN), lambda c: (ci(c), D_INNER // GN + 1)),
            pl.BlockSpec((SSM_GROUPS, L, 128), lambda c: (0, ci(c), 0)),
            pl.BlockSpec((SSM_GROUPS, 8, 128), lambda c: (0, 0, 0)),
            pl.BlockSpec((L, D_INNER), lambda c: (ci(c), 0)),
            pl.BlockSpec((1, D_INNER), lambda c: (0, 0))], ci


def _round_robin(gens):
    live = list(gens)
    while live:
        nxt = []
        for gen in live:
            try:
                next(gen)
                nxt.append(gen)
            except StopIteration:
                pass
        live = nxt


def _group_views(g, wide, narrow, lead):
    return ([r.at[:, g * 512:(g + 1) * 512] for r in wide], [r.at[:, g * 128:(g + 1) * 128] for r in narrow],
            [r.at[g] for r in lead])


def _ssd_fwd(xbc_c, zx, dtg, par, gnw, *, name):
    T = xbc_c.shape[0]
    L = SSM_CHUNK
    nc = T // L
    in_specs, ci = _ssd_specs(nc, False)

    def body(xs_ref, b_ref, c_ref, dt_ref, par_ref, z_ref, gnw_ref, y_ref, yn_ref, st_ref, h_ref):
        @pl.when(pl.program_id(0) == 0)
        def _():
            h_ref[...] = jnp.zeros_like(h_ref)

        gens = []
        for g in range(SSM_GROUPS):
            (xs, z, gw, y, yn), (b, c), (dt, pr, st, h) = _group_views(
                g, [xs_ref, z_ref, gnw_ref, y_ref, yn_ref], [b_ref, c_ref], [dt_ref, par_ref, st_ref, h_ref])
            gens.append(group(xs, b, c, dt, pr, z, gw, y, yn, st, h))
        _round_robin(gens)

    def group(xs_ref, b_ref, c_ref, dt_ref, par_ref, z_ref, gnw_ref, y_ref, yn_ref, st_ref, h_ref):
        _, dt, _, dsk, cs, csT, ecs, eend, dec = _ssd_common(dt_ref, par_ref)
        Bb = b_ref[...].astype(BF16)
        Cb = c_ref[...].astype(BF16)
        G = lax.dot_general(Cb, Bb, _NT, preferred_element_type=F32)
        row = lax.broadcasted_iota(jnp.int32, (L, L), 0)
        col = lax.broadcasted_iota(jnp.int32, (L, L), 1)
        tril = col <= row
        lo = lax.broadcasted_iota(jnp.int32, (L, 128), 1) < 64
        lo1 = lax.broadcasted_iota(jnp.int32, (1, 128), 1) < 64
        dt_x, ecs_x, eend_x = (_heads_to_pairs(m) for m in (dt, ecs, eend))
        for pp in range(4):
            hA, hB = 2 * pp, 2 * pp + 1
            lanes = slice(pp * 128, (pp + 1) * 128)

            def sel1(m):
                return jnp.where(lo1, m[:, hA:hA + 1], m[:, hB:hB + 1])

            X = xs_ref[:, lanes]
            xd = X * dt_x[:, lanes]
            xdb = xd.astype(BF16)
            ys = []
            for h in (hA, hB):
                Lm = jnp.where(tril, jnp.exp(jnp.minimum(cs[:, h:h + 1] - csT[h:h + 1, :], 0.0)), 0.0)
                ys.append(jnp.dot((G * Lm).astype(BF16), xdb, preferred_element_type=F32))
                yield
            Hp = h_ref[pp]
            st_ref[pp] = Hp
            yoff = jnp.dot(Cb, Hp.astype(BF16), preferred_element_type=F32) * ecs_x[:, lanes]
            y_ref[:, lanes] = jnp.where(lo, ys[0], ys[1]) + yoff + sel1(dsk) * X
            S = lax.dot_general(Bb, (xd * eend_x[:, lanes]).astype(BF16), _TN, preferred_element_type=F32)
            h_ref[pp] = Hp * sel1(dec) + S
            yield
        zv = z_ref[...]
        yg = y_ref[...] * (zv * _sigmoid(zv))
        r = jnp.tile(lax.rsqrt(_rowsum_mxu(yg * yg) * (1.0 / 512) + EPS), (1, 4))
        yn_ref[...] = (yg * r * gnw_ref[...]).astype(BF16)

    return pl.pallas_call(
        body, name=name, grid=(nc,), in_specs=in_specs,
        out_specs=[pl.BlockSpec((L, D_INNER), lambda c: (c, 0)), pl.BlockSpec((L, D_INNER), lambda c: (c, 0)),
                   pl.BlockSpec((SSM_GROUPS, None, 4, 128, 128), lambda c: (0, c, 0, 0, 0))],
        out_shape=[jax.ShapeDtypeStruct((T, D_INNER), F32), jax.ShapeDtypeStruct((T, D_INNER), BF16),
                   jax.ShapeDtypeStruct((SSM_GROUPS, nc, 4, 128, 128), F32)],
        scratch_shapes=[pltpu.VMEM((SSM_GROUPS, 4, 128, 128), F32)],
        compiler_params=_cparams(("arbitrary",)))(xbc_c, xbc_c, xbc_c, dtg, par, zx, gnw)


def _ssd_bwd(xbc_c, zx, dtg, par, gnw, y, st, dyn, *, name):
    T = xbc_c.shape[0]
    L = SSM_CHUNK
    nc = T // L
    in_specs, ci = _ssd_specs(nc, True)
    in_specs += [pl.BlockSpec((L, D_INNER), lambda c: (ci(c), 0)),
                 pl.BlockSpec((SSM_GROUPS, None, 4, 128, 128), lambda c: (0, ci(c), 0, 0, 0)),
                 pl.BlockSpec((L, D_INNER), lambda c: (ci(c), 0))]

    def body(xs_ref, b_ref, c_ref, dt_ref, par_ref, z_ref, gnw_ref, y_ref, st_ref, dyn_ref,
             dxbc_ref, dz_ref, ddt_ref, dgnw_ref, dpar_ref, dh_ref):
        @pl.when(pl.program_id(0) == 0)
        def _():
            dh_ref[...] = jnp.zeros_like(dh_ref)
            dgnw_ref[...] = jnp.zeros_like(dgnw_ref)
            dpar_ref[...] = jnp.zeros_like(dpar_ref)

        dxs_ref = dxbc_ref.at[:, 0:D_INNER]
        db_ref = dxbc_ref.at[:, D_INNER:D_INNER + GN]
        dc_ref = dxbc_ref.at[:, D_INNER + GN:CONV_DIM]

        gens = []
        for g in range(SSM_GROUPS):
            (xs, z, gw, y, dyn, dxs, dz, dgw), (b, c, db, dc), (dt, pr, st, ddt, dpr, dh) = _group_views(
                g, [xs_ref, z_ref, gnw_ref, y_ref, dyn_ref, dxs_ref, dz_ref, dgnw_ref], [b_ref, c_ref, db_ref, dc_ref],
                [dt_ref, par_ref, st_ref, ddt_ref, dpar_ref, dh_ref])
            gens.append(group(xs, b, c, dt, pr, z, gw, y, st, dyn, dxs, db, dc, dz, ddt, dgw, dpr, dh))
        _round_robin(gens)

    def group(xs_ref, b_ref, c_ref, dt_ref, par_ref, z_ref, gnw_ref, y_ref, st_ref, dyn_ref,
              dxs_ref, db_ref, dc_ref, dz_ref, ddt_ref, dgnw_ref, dpar_ref, dh_ref):
        yv = y_ref[...]
        zv = z_ref[...]
        sg = _sigmoid(zv)
        sz = zv * sg
        yg = yv * sz
        r = jnp.tile(lax.rsqrt(_rowsum_mxu(yg * yg) * (1.0 / 512) + EPS), (1, 4))
        yh = yg * r
        dyn = dyn_ref[...].astype(F32)
        dgnw_ref[...] += jnp.sum(dyn * yh, axis=0, keepdims=True)
        dyh = dyn * gnw_ref[...]
        dyg = r * (dyh - yh * jnp.tile(_rowsum_mxu(dyh * yh) * (1.0 / 512), (1, 4)))
        dY_all = dyg * sz
        dz_ref[...] = (dyg * yv * (sg * (1.0 + zv * (1.0 - sg)))).astype(dz_ref.dtype)

        yield
        raw, dt, a, dsk, cs, csT, ecs, eend, dec = _ssd_common(dt_ref, par_ref)
        Bb = b_ref[...].astype(BF16)
        Cb = c_ref[...].astype(BF16)
        G = lax.dot_general(Cb, Bb, _NT, preferred_element_type=F32)
        row = lax.broadcasted_iota(jnp.int32, (L, L), 0)
        col = lax.broadcasted_iota(jnp.int32, (L, L), 1)
        tril = col <= row
        lo = lax.broadcasted_iota(jnp.int32, (L, 128), 1) < 64
        lane1 = lax.broadcasted_iota(jnp.int32, (1, 128), 1)
        lo1 = lane1 < 64
        rowl = lax.broadcasted_iota(jnp.int32, (L, 128), 0)
        dt_x, ecs_x, eend_x = (_heads_to_pairs(m) for m in (dt, ecs, eend))
        dG = jnp.zeros((L, L), F32)
        dB = jnp.zeros((L, SSM_STATE), F32)
        dC = jnp.zeros((L, SSM_STATE), F32)
        dcs_t = jnp.zeros((L, L), F32)
        tails = jnp.zeros((1, 128), F32)
        dD_row = jnp.zeros((1, 128), F32)
        v_parts, prod_parts = [], []

        def tot(m):
            return jnp.sum(jnp.sum(m, axis=0, keepdims=True), axis=1, keepdims=True)

        for pp in range(4):
            hA, hB = 2 * pp, 2 * pp + 1
            lanes = slice(pp * 128, (pp + 1) * 128)

            def sel1(m):
                return jnp.where(lo1, m[:, hA:hA + 1], m[:, hB:hB + 1])

            X = xs_ref[:, lanes]
            dY = dY_all[:, lanes]
            dtsel = dt_x[:, lanes]
            xd = X * dtsel
            xdb = xd.astype(BF16)
            dYb = dY.astype(BF16)
            Hp = st_ref[pp]
            Hb = Hp.astype(BF16)
            dHn = dh_ref[pp]
            dHb = dHn.astype(BF16)
            ecs_sel = ecs_x[:, lanes]
            eend_sel = eend_x[:, lanes]
            dxd_state = jnp.dot(Bb, dHb, preferred_element_type=F32) * eend_sel
            yoff = jnp.dot(Cb, Hb, preferred_element_type=F32) * ecs_sel
            dYe = (dY * ecs_sel).astype(BF16)
            dC = dC + lax.dot_general(dYe, Hb, _NT, preferred_element_type=F32)
            dB = dB + lax.dot_general((xd * eend_sel).astype(BF16), dHb, _NT, preferred_element_type=F32)
            dh_ref[pp] = dHn * sel1(dec) + lax.dot_general(Cb, dYe, _TN, preferred_element_type=F32)
            q = xd * dxd_state
            dyq = dY * yoff - q
            qcol = jnp.sum(q, axis=0, keepdims=True)
            hcol = jnp.sum(dHn * Hp, axis=0, keepdims=True)
            dxd_diag = []
            for h, msk, msk1 in ((hA, lo, lo1), (hB, jnp.logical_not(lo), jnp.logical_not(lo1))):
                Lm = jnp.where(tril, jnp.exp(jnp.minimum(cs[:, h:h + 1] - csT[h:h + 1, :], 0.0)), 0.0)
                M = G * Lm
                dxd_diag.append(lax.dot_general(M.astype(BF16), dYb, _TN, preferred_element_type=F32))
                dM = lax.dot_general(jnp.where(msk, dY, 0.0).astype(BF16), xdb, _NT, preferred_element_type=F32)
                dG = dG + dM * Lm
                W = dM * M
                dcs_t = dcs_t + jnp.where(row == h, jnp.sum(W, axis=0, keepdims=True), 0.0)
                v_parts.append(W + jnp.where(msk, dyq, 0.0))
                tail = (jnp.sum(jnp.where(msk1, qcol, 0.0), axis=1, keepdims=True)
                        + dec[:, h:h + 1] * jnp.sum(jnp.where(msk1, hcol, 0.0), axis=1, keepdims=True))
                tails = tails + jnp.where(lane1 == h, tail, 0.0)
                yield
            dxd = jnp.where(lo, dxd_diag[0], dxd_diag[1]) + dxd_state
            prod_parts.append(dxd * X)
            dxs_ref[:, lanes] = dxd * dtsel + sel1(dsk) * dY
            dyx = jnp.sum(dY * X, axis=0, keepdims=True)
            sA = jnp.sum(jnp.where(lo1, dyx, 0.0), axis=1, keepdims=True)
            sB = jnp.sum(dyx, axis=1, keepdims=True) - sA
            dD_row = dD_row + jnp.where(lane1 == hA, sA, 0.0) + jnp.where(lane1 == hB, sB, 0.0)
            yield
        dGb = dG.astype(BF16)
        db_ref[...] = dB + lax.dot_general(dGb, Cb, _TN, preferred_element_type=F32)
        dc_ref[...] = dC + jnp.dot(dGb, Bb, preferred_element_type=F32)
        dcs_mat = _lane_block_sums(jnp.concatenate(v_parts, axis=1), 128) + jnp.where(rowl == L - 1, tails, 0.0)
        ddt_mat = _lane_block_sums(jnp.concatenate(prod_parts, axis=1), 64)
        dad = _rcumsum_rows(dcs_mat - dcs_t.T)
        draw = (a * dad + ddt_mat) * _sigmoid(raw)
        ddt_ref[...] = draw
        dpar_ref[0:1, :] += jnp.sum(draw, axis=0, keepdims=True)
        dpar_ref[1:2, :] += jnp.sum(dt * dad, axis=0, keepdims=True) * a
        dpar_ref[2:3, :] += dD_row

    return pl.pallas_call(
        body, name=name, grid=(nc,), in_specs=in_specs,
        out_specs=[pl.BlockSpec((L, CONV_DIM), lambda c: (ci(c), 0)),
                   pl.BlockSpec((L, D_INNER), lambda c: (ci(c), 0)),
                   pl.BlockSpec((SSM_GROUPS, L, 128), lambda c: (0, ci(c), 0)),
                   pl.BlockSpec((1, D_INNER), lambda c: (0, 0)),
                   pl.BlockSpec((SSM_GROUPS, 8, 128), lambda c: (0, 0, 0))],
        out_shape=[jax.ShapeDtypeStruct((T, CONV_DIM), F32), jax.ShapeDtypeStruct((T, IN_PROJ_PAD), BF16),
                   jax.ShapeDtypeStruct((SSM_GROUPS, T, 128), F32), jax.ShapeDtypeStruct((1, D_INNER), F32),
                   jax.ShapeDtypeStruct((SSM_GROUPS, 8, 128), F32)],
        scratch_shapes=[pltpu.VMEM((SSM_GROUPS, 4, 128, 128), F32)],
        compiler_params=_cparams(("arbitrary",)))(xbc_c, xbc_c, xbc_c, dtg, par, zx, gnw, y, st, dyn)


SB_KEYS = 512
SB_SCAN = 256
SB_STRIP = 256


def _tri(width, cond):
    kk = lax.broadcasted_iota(jnp.int32, (width, width), 0)
    jj = lax.broadcasted_iota(jnp.int32, (width, width), 1)
    return cond(kk, jj).astype(BF16)


_LOG2E = 1.4426950408889634


def _softplus2(z2):
    return jnp.maximum(z2, 0.0) + jnp.log2(1.0 + jnp.exp2(-jnp.abs(z2)))


def _sba_sub_fwd(zb, c, U, mask):
    z2 = zb * _LOG2E
    s = _softplus2(z2)
    if mask is not None:
        s = jnp.where(mask, s, 0.0)
    R = c + jnp.dot(s.astype(BF16), U, preferred_element_type=F32)
    A = jnp.exp2(z2 - s - R)
    if mask is not None:
        A = jnp.where(mask, A, 0.0)
    return A.astype(BF16), R[:, 0:1] + s[:, 0:1]


def _sba_sub_bwd(zb, dAb, Lt, pc, pe, Uincl, Uexcl, mask):
    last = zb.shape[1] - 1
    z2 = zb * _LOG2E
    s = _softplus2(z2)
    g = z2 - s
    if mask is not None:
        s = jnp.where(mask, s, 0.0)
    P = pc + jnp.dot(s.astype(BF16), Uincl, preferred_element_type=F32)
    A = jnp.exp2(g - (Lt - P))
    if mask is not None:
        A = jnp.where(mask, A, 0.0)
    E = dAb * A
    PE = pe + jnp.dot(E.astype(BF16), Uexcl, preferred_element_type=F32)
    dz = E - jnp.exp2(g) * (E + PE)
    if mask is not None:
        dz = jnp.where(mask, dz, 0.0)
    return (A.astype(BF16), dz.astype(BF16), P[:, last:last + 1], PE[:, last:last + 1] + E[:, last:last + 1])


def _stack_heads(v):
    lo = lax.broadcasted_iota(jnp.int32, v.shape, 1) < 64
    zero = jnp.zeros_like(v)
    return jnp.concatenate([jnp.where(lo, v, zero), jnp.where(lo, zero, v)], axis=0)


def _unstack_heads(v):
    lo = lax.broadcasted_iota(jnp.int32, (SB_BLOCK, 128), 1) < 64
    return jnp.where(lo, v[:SB_BLOCK], v[SB_BLOCK:])


def _sba_rows(a):
    return slice(2 * a * SB_BLOCK, 2 * (a + 1) * SB_BLOCK)


def _sba_diag_case(a, b):
    Bq = SB_BLOCK
    if b * SB_SCAN >= (a + 1) * Bq:
        return "skip"
    if (b + 1) * SB_SCAN <= a * Bq:
        return "full"
    rowi = lax.broadcasted_iota(jnp.int32, (2 * Bq, SB_SCAN), 0)
    qpos = a * Bq + jnp.where(rowi >= Bq, rowi - Bq, rowi)
    return b * SB_SCAN + lax.broadcasted_iota(jnp.int32, (2 * Bq, SB_SCAN), 1) < qpos


def _sba_fwd(q, kv, *, name):
    T = q.shape[0]
    Bq = SB_BLOCK
    nsub = SB_KEYS // Bq
    nscan = SB_KEYS // SB_SCAN
    R = 2 * SB_KEYS
    assert T % SB_KEYS == 0 and SB_STRIP == 2 * Bq
    scale = 1.0 / math.sqrt(SB_HEAD_DIM)

    def body(q_ref, k_ref, v_ref, o_ref, lt_ref, z_s, a_s, c_s, acc_s):
        i = pl.program_id(1)
        U2 = _tri(SB_SCAN, lambda k, j: k > j)
        qs_all = jnp.concatenate([_stack_heads(q_ref[a * Bq:(a + 1) * Bq, :] * scale) for a in range(nsub)], axis=0)
        c_s[...] = jnp.zeros_like(c_s)
        acc_s[...] = jnp.zeros_like(acc_s)

        def scores(J, slot):
            off = pl.multiple_of(J * SB_KEYS, SB_KEYS)
            z_s[slot] = lax.dot_general(qs_all, k_ref[pl.ds(off, SB_KEYS), :], _NT, preferred_element_type=F32)

        def weights(slot, diag):
            for a in range(nsub):
                rows = _sba_rows(a)
                c = c_s[rows, :]
                for b in reversed(range(nscan)):
                    cols = slice(b * SB_SCAN, (b + 1) * SB_SCAN)
                    case = _sba_diag_case(a, b) if diag else "full"
                    if isinstance(case, str) and case == "skip":
                        a_s[slot, rows, cols] = jnp.zeros((2 * Bq, SB_SCAN), BF16)
                        continue
                    A, c = _sba_sub_fwd(z_s[slot, rows, cols], c, U2, None if isinstance(case, str) else case)
                    a_s[slot, rows, cols] = A
                c_s[rows, :] = c

        def values(J, slot):
            off = pl.multiple_of(J * SB_KEYS, SB_KEYS)
            acc_s[...] += jnp.dot(a_s[slot], v_ref[pl.ds(off, SB_KEYS), :], preferred_element_type=F32)

        scores(i, 0)
        weights(0, True)
        scores(jnp.maximum(i - 1, 0), 1)

        def two_steps(u, _):
            t = 2 * u + 1
            weights(1, False)
            scores(jnp.maximum(i - t - 1, 0), 0)
            values(i - t + 1, 0)
            weights(0, False)
            scores(jnp.maximum(i - t - 2, 0), 1)
            values(i - t, 1)
            return 0

        lax.fori_loop(0, i // 2, two_steps, 0)
        odd = lax.rem(i, 2) == 1

        @pl.when(jnp.logical_not(odd))
        def _():
            values(0, 0)

        @pl.when(odd)
        def _():
            weights(1, False)
            values(1, 0)
            values(0, 1)
        for a in range(nsub):
            o_ref[a * Bq:(a + 1) * Bq, :] = _unstack_heads(acc_s[_sba_rows(a), :]).astype(BF16)
            lt_ref[a * Bq:(a + 1) * Bq, :] = _unstack_heads(jnp.broadcast_to(c_s[_sba_rows(a), :], (2 * Bq, 128)))

    return pl.pallas_call(
        body, name=name, grid=(SB_HEADS // 2, T // SB_KEYS),
        in_specs=[pl.BlockSpec((SB_KEYS, 128), lambda p, i: (i, p)), pl.BlockSpec((T, 128), lambda p, i: (0, p)),
                  pl.BlockSpec((T, 128), lambda p, i: (0, p + SB_HEADS // 2))],
        out_specs=[pl.BlockSpec((SB_KEYS, 128), lambda p, i: (i, p)),
                   pl.BlockSpec((None, SB_KEYS, 128), lambda p, i: (p, i, 0))],
        out_shape=[jax.ShapeDtypeStruct((T, D_MODEL), BF16), jax.ShapeDtypeStruct((SB_HEADS // 2, T, 128), F32)],
        scratch_shapes=[pltpu.VMEM((2, R, SB_KEYS), F32), pltpu.VMEM((2, R, SB_KEYS), BF16),
                        pltpu.VMEM((R, 1), F32), pltpu.VMEM((R, 128), F32)],
        compiler_params=_cparams(("parallel", "parallel")))(q, kv, kv)


def _sba_bwd(q, kv, lt, do, *, name):
    T = q.shape[0]
    Bq = SB_BLOCK
    nq = T // SB_KEYS
    nsub = SB_KEYS // Bq
    nscan = SB_KEYS // SB_SCAN
    R = 2 * SB_KEYS
    assert T % SB_KEYS == 0 and SB_STRIP == 2 * Bq
    scale = 1.0 / math.sqrt(SB_HEAD_DIM)

    def body(q_ref, k_ref, v_ref, lt_ref, do_ref, dq_ref, dk_ref, dv_ref, dk_acc, dv_acc,
             z_s, da_s, a_s, dz_s, pc_s, pe_s, lt_s, dq_s):
        i = pl.program_id(1)

        @pl.when(i == 0)
        def _():
            dk_acc[...] = jnp.zeros_like(dk_acc)
            dv_acc[...] = jnp.zeros_like(dv_acc)

        Uincl = _tri(SB_SCAN, lambda k, j: k <= j)
        Uexcl = _tri(SB_SCAN, lambda k, j: k < j)
        qs, dos = [], []
        for a in range(nsub):
            rows = slice(a * Bq, (a + 1) * Bq)
            qs.append(_stack_heads(q_ref[rows, :] * scale))
            dos.append(_stack_heads(do_ref[rows, :]))
            lt_s[_sba_rows(a), :] = jnp.concatenate([lt_ref[rows, 0:1], lt_ref[rows, 64:65]], axis=0)
        qs_all = jnp.concatenate(qs, axis=0)
        dos_all = jnp.concatenate(dos, axis=0)
        pc_s[...] = jnp.zeros_like(pc_s)
        pe_s[...] = jnp.zeros_like(pe_s)
        a_s[1] = jnp.zeros((R, SB_KEYS), BF16)
        dz_s[1] = jnp.zeros((R, SB_KEYS), BF16)

        def scores(J, slot):
            off = pl.multiple_of(J * SB_KEYS, SB_KEYS)
            z_s[slot] = lax.dot_general(qs_all, k_ref[pl.ds(off, SB_KEYS), :], _NT, preferred_element_type=F32)
            da_s[slot] = lax.dot_general(dos_all, v_ref[pl.ds(off, SB_KEYS), :], _NT, preferred_element_type=F32)

        def gradients(slot, diag):
            for a in range(nsub):
                rows = _sba_rows(a)
                pc, pe, Lt = pc_s[rows, :], pe_s[rows, :], lt_s[rows, :]
                for b in range(nscan):
                    cols = slice(b * SB_SCAN, (b + 1) * SB_SCAN)
                    case = _sba_diag_case(a, b) if diag else "full"
                    if isinstance(case, str) and case == "skip":
                        a_s[slot, rows, cols] = jnp.zeros((2 * Bq, SB_SCAN), BF16)
                        dz_s[slot, rows, cols] = jnp.zeros((2 * Bq, SB_SCAN), BF16)
                        continue
                    A, dz, pc, pe = _sba_sub_bwd(z_s[slot, rows, cols], da_s[slot, rows, cols], Lt, pc, pe, Uincl, Uexcl,
                                                 None if isinstance(case, str) else case)
                    a_s[slot, rows, cols] = A
                    dz_s[slot, rows, cols] = dz
                pc_s[rows, :] = pc
                pe_s[rows, :] = pe

        def products(J, slot):
            off = pl.multiple_of(J * SB_KEYS, SB_KEYS)
            dzt = dz_s[slot]
            dk_acc[pl.ds(off, SB_KEYS), :] += lax.dot_general(dzt, qs_all, _TN, preferred_element_type=F32)
            dv_acc[pl.ds(off, SB_KEYS), :] += lax.dot_general(a_s[slot], dos_all, _TN, preferred_element_type=F32)
            dq_s[...] += jnp.dot(dzt, k_ref[pl.ds(off, SB_KEYS), :], preferred_element_type=F32)

        dq_s[...] = jnp.zeros_like(dq_s)
        scores(0, 0)

        def two_steps(u, _):
            t = 2 * u
            gradients(0, False)
            scores(t + 1, 1)
            products(jnp.maximum(t - 1, 0), 1)
            gradients(1, False)
            scores(t + 2, 0)
            products(t, 0)
            return 0

        lax.fori_loop(0, i // 2, two_steps, 0)
        odd = lax.rem(i, 2) == 1

        @pl.when(jnp.logical_not(odd))
        def _():
            gradients(0, True)
            products(jnp.maximum(i - 1, 0), 1)
            products(i, 0)

        @pl.when(odd)
        def _():
            gradients(0, False)
            scores(i, 1)
            products(jnp.maximum(i - 2, 0), 1)
            gradients(1, True)
            products(i - 1, 0)
            products(i, 1)

        for a in range(nsub):
            dq_ref[a * Bq:(a + 1) * Bq, :] = (_unstack_heads(dq_s[_sba_rows(a), :]) * scale).astype(BF16)

        @pl.when(i == nq - 1)
        def _():
            dk_ref[...] = dk_acc[...].astype(BF16)
            dv_ref[...] = dv_acc[...].astype(BF16)

    return pl.pallas_call(
        body, name=name, grid=(SB_HEADS // 2, nq),
        in_specs=[pl.BlockSpec((SB_KEYS, 128), lambda p, i: (i, p)), pl.BlockSpec((T, 128), lambda p, i: (0, p)),
                  pl.BlockSpec((T, 128), lambda p, i: (0, p + SB_HEADS // 2)),
                  pl.BlockSpec((None, SB_KEYS, 128), lambda p, i: (p, i, 0)),
                  pl.BlockSpec((SB_KEYS, 128), lambda p, i: (i, p))],
        out_specs=[pl.BlockSpec((SB_KEYS, 128), lambda p, i: (i, p)), pl.BlockSpec((T, 128), lambda p, i: (0, p)),
                   pl.BlockSpec((T, 128), lambda p, i: (0, p))],
        out_shape=[jax.ShapeDtypeStruct((T, D_MODEL), BF16), jax.ShapeDtypeStruct((T, D_MODEL), BF16),
                   jax.ShapeDtypeStruct((T, D_MODEL), BF16)],
        scratch_shapes=[pltpu.VMEM((T, 128), F32), pltpu.VMEM((T, 128), F32),
                        pltpu.VMEM((2, R, SB_KEYS), F32), pltpu.VMEM((2, R, SB_KEYS), F32),
                        pltpu.VMEM((2, R, SB_KEYS), BF16), pltpu.VMEM((2, R, SB_KEYS), BF16),
                        pltpu.VMEM((R, 1), F32), pltpu.VMEM((R, 1), F32), pltpu.VMEM((R, 1), F32),
                        pltpu.VMEM((R, 128), F32)],
        compiler_params=_cparams(("parallel", "arbitrary")))(q, kv, kv, lt, do)


def _loss_head(h, tgt, w, *, name, tt=512):
    T, D = h.shape
    tt = min(tt, T)

    def body(h_ref, t_ref, w_ref, loss_ref, dh_ref, dw_ref):
        i = pl.program_id(0)
        hv = h_ref[...]
        wv = w_ref[...]
        r = lax.rsqrt(jnp.mean(hv * hv, axis=-1, keepdims=True) + EPS)
        xhat = hv * r
        err = xhat * wv - t_ref[...]
        part = 0.5 * jnp.sum(jnp.mean(err * err, axis=-1, keepdims=True), axis=0, keepdims=True)
        dy = err * (1.0 / D)
        dxh = dy * wv
        dh_ref[...] = r * (dxh - xhat * jnp.mean(dxh * xhat, axis=-1, keepdims=True))
        dwc = jnp.sum(dy * xhat, axis=0, keepdims=True)

        @pl.when(i == 0)
        def _():
            loss_ref[...] = jnp.broadcast_to(part, loss_ref.shape)
            dw_ref[...] = dwc

        @pl.when(i > 0)
        def _():
            loss_ref[...] += jnp.broadcast_to(part, loss_ref.shape)
            dw_ref[...] += dwc

    return pl.pallas_call(
        body, name=name, grid=(T // tt,),
        in_specs=[pl.BlockSpec((tt, D), lambda i: (i, 0)), pl.BlockSpec((tt, D), lambda i: (i, 0)),
                  pl.BlockSpec((1, D), lambda i: (0, 0))],
        out_specs=[pl.BlockSpec((1, 128), lambda i: (0, 0)), pl.BlockSpec((tt, D), lambda i: (i, 0)),
                   pl.BlockSpec((1, D), lambda i: (0, 0))],
        out_shape=[jax.ShapeDtypeStruct((1, 128), F32), jax.ShapeDtypeStruct((T, D), F32),
                   jax.ShapeDtypeStruct((1, D), F32)],
        compiler_params=_cparams(("arbitrary",)))(h, tgt, w.reshape(1, D))


def _adamw(parts, w, m, v, *, name, tr=256, tc=None):
    plist = list(parts) if isinstance(parts, (list, tuple)) else [parts]
    P, _, C = plist[0].shape
    R = sum(a.shape[1] for a in plist)
    tr = min(tr, R)
    tc = C if tc is None else tc
    assert all(a.shape[1] % tr == 0 for a in plist) and C % tc == 0, (name, R, C, tr, tc)
    nbs = [a.shape[1] // tr for a in plist]
    offs = [sum(nbs[:l]) for l in range(len(nbs))]
    c1 = 1.0 - ADAM_B1 ** ADAM_STEP
    c2 = 1.0 - ADAM_B2 ** ADAM_STEP

    def body(*refs):
        p_refs = refs[:len(plist)]
        w_ref, m_ref, v_ref, g_ref, d_ref, nm_ref, nv_ref = refs[len(plist):]
        i = pl.program_id(0)
        g = None
        for l, p_ref in enumerate(p_refs):
            gl = p_ref[0].astype(F32)
            for k in range(1, P):
                gl = gl + p_ref[k].astype(F32)
            g = gl if g is None else jnp.where(i >= offs[l], gl, g)
        mn = ADAM_B1 * m_ref[...] + (1.0 - ADAM_B1) * g
        vn = ADAM_B2 * v_ref[...] + (1.0 - ADAM_B2) * (g * g)
        g_ref[...] = g
        nm_ref[...] = mn
        nv_ref[...] = vn
        d_ref[...] = -ADAM_LR * ((mn / c1) / (jnp.sqrt(vn / c2) + ADAM_EPS) + ADAM_WD * w_ref[...])

    spec = pl.BlockSpec((tr, tc), lambda i, j: (i, j))
    sds = jax.ShapeDtypeStruct((R, C), F32)
    return pl.pallas_call(
        body, name=name, grid=(R // tr, C // tc),
        in_specs=[pl.BlockSpec((P, tr, tc), functools.partial(lambda i, j, o, n: (0, jnp.clip(i - o, 0, n - 1), j), o=o, n=n))
                  for o, n in zip(offs, nbs)] + [spec, spec, spec],
        out_specs=[spec, spec, spec, spec], out_shape=[sds, sds, sds, sds],
        compiler_params=_cparams(("parallel", "parallel")))(*plist, w, m, v)


def _all_gather(shards, *, name):
    n = len(shards)

    def body(*refs):
        ins, outs = refs[:n], refs[n:2 * n]
        send_sems, recv_sems, local_sems = refs[2 * n:]
        x, y, c = lax.axis_index("x"), lax.axis_index("y"), lax.axis_index("c")
        me, sib = (x, y, c), (x, y, 1 - c)
        chips = [(1 - x, y), (x, 1 - y), (1 - x, 1 - y)]

        def slot(p):
            return 4 * p[0] + 2 * p[1] + p[2]

        def cp(a, k, block, to, src=None):
            dst = outs[a].at[slot(block)]
            return pltpu.make_async_remote_copy(src_ref=dst if src is None else src, dst_ref=dst,
                                                send_sem=send_sems.at[a, k], recv_sem=recv_sems.at[a, k],
                                                device_id=to, device_id_type=_MESH)

        mine = [pltpu.make_async_copy(ins[a], outs[a].at[slot(me)], local_sems.at[a]) for a in range(n)]
        for m in mine:
            m.start()
        first = []
        for a in range(n):
            first.append(cp(a, 0, me, sib, src=ins[a]))
            for j, chip in enumerate(chips):
                first.append(cp(a, 1 + j, me, (*chip, c), src=ins[a]))
        for f in first:
            f.start()
        passed = []
        for j, chip in enumerate(chips):
            for a in range(n):
                cp(a, 1 + j, (*chip, c), me).wait_recv()
                f = cp(a, 4 + j, (*chip, c), sib)
                f.start()
                passed.append(f)
        for a in range(n):
            cp(a, 0, sib, me).wait_recv()
            for j, chip in enumerate(chips):
                cp(a, 4 + j, (*chip, 1 - c), me).wait_recv()
        for f in first + passed:
            f.wait_send()
        for m in mine:
            m.wait()

    return pl.pallas_call(
        body, name=name, in_specs=[_ANY] * n, out_specs=[_ANY] * n,
        out_shape=[jax.ShapeDtypeStruct((N_DEV,) + s.shape, s.dtype) for s in shards],
        scratch_shapes=[pltpu.SemaphoreType.DMA((n, 7)), pltpu.SemaphoreType.DMA((n, 7)),
                        pltpu.SemaphoreType.DMA((n,))])(*shards)


_HBM = pl.BlockSpec(memory_space=pltpu.HBM)
_SEM = pl.BlockSpec(memory_space=pltpu.SEMAPHORE)
_EFFECT = pltpu.SideEffectType.DATAFLOW_SIDE_EFFECTING


def _peers():
    x, y, c = lax.axis_index("x"), lax.axis_index("y"), lax.axis_index("c")
    out = []
    for r in range(1, N_DEV):
        px = 1 - x if (r >> 2) & 1 else x
        py = 1 - y if (r >> 1) & 1 else y
        pc = 1 - c if r & 1 else c
        out.append(((px, py, pc), 4 * px + 2 * py + pc))
    return 4 * x + 2 * y + c, out


def _push_copy(src_ref, land_ref, send_sems, recv_sems, a, k, me, peer, peer_slot, scatter, arriving):
    src = src_ref.at[peer_slot] if scatter else src_ref
    return pltpu.make_async_remote_copy(
        src_ref=src, dst_ref=land_ref.at[peer_slot if arriving else me], send_sem=send_sems.at[a * (N_DEV - 1) + k],
        recv_sem=recv_sems.at[a * (N_DEV - 1) + k], device_id=peer, device_id_type=_MESH)


def _push_start(srcs, *, scatter, name):
    n = len(srcs)
    lands = [lax.empty(s.shape if scatter else (N_DEV,) + s.shape, s.dtype) for s in srcs]

    def body(*refs):
        src_refs, land_refs = refs[:n], refs[n:2 * n]
        send_sems, recv_sems = refs[2 * n], refs[2 * n + 1]
        token = refs[-1]
        me, peers = _peers()
        for k, (peer, slot) in enumerate(peers):
            for a in range(n):
                _push_copy(src_refs[a], land_refs[a], send_sems, recv_sems, a, k, me, peer, slot, scatter, False).start()
        token[...] = jnp.zeros_like(token)

    hbm = lambda a: pltpu.HBM(a.shape, a.dtype)
    outs = pl.pallas_call(
        body, name=name,
        out_shape=(pltpu.SemaphoreType.DMA((n * (N_DEV - 1),)), pltpu.SemaphoreType.DMA((n * (N_DEV - 1),)),
                   *[hbm(s) for s in srcs], *[hbm(l) for l in lands], jax.ShapeDtypeStruct((8, 128), F32)),
        in_specs=[_HBM] * (2 * n),
        out_specs=(_SEM, _SEM, *([_HBM] * (2 * n)), pl.BlockSpec(memory_space=pltpu.VMEM)),
        input_output_aliases={i: 2 + i for i in range(2 * n)},
        compiler_params=pltpu.CompilerParams(has_side_effects=_EFFECT),
    )(*[pltpu.with_memory_space_constraint(s, pltpu.HBM) for s in srcs],
      *[pltpu.with_memory_space_constraint(l, pltpu.HBM) for l in lands])
    return dict(send=outs[0], recv=outs[1], srcs=list(outs[2:2 + n]), lands=list(outs[2 + n:2 + 2 * n]),
                token=outs[-1], scatter=scatter, n=n)


def _push_wait(h, after, *, name):
    n, scatter = h["n"], h["scatter"]

    def body(*refs):
        src_refs, land_refs = refs[:n], refs[n:2 * n]
        send_sems, recv_sems = refs[2 * n], refs[2 * n + 1]
        me, peers = _peers()
        for k, (peer, slot) in enumerate(peers):
            for a in range(n):
                cp = _push_copy(src_refs[a], land_refs[a], send_sems, recv_sems, a, k, me, peer, slot, scatter, True)
                cp.wait_send()
                cp.wait_recv()

    hbm = lambda a: pltpu.HBM(a.shape, a.dtype)
    outs = pl.pallas_call(
        body, name=name,
        out_shape=(*[hbm(s) for s in h["srcs"]], *[hbm(l) for l in h["lands"]]),
        in_specs=[_HBM] * (2 * n) + [_SEM, _SEM, _ANY], out_specs=tuple([_HBM] * (2 * n)),
        input_output_aliases={i: i for i in range(2 * n)},
        compiler_params=pltpu.CompilerParams(has_side_effects=_EFFECT),
    )(*h["srcs"], *h["lands"], h["send"], h["recv"], after)
    return list(outs[:n]), list(outs[n:])


def _ffn_fwd(h, nw, w_up, conv_w, conv_b, w_down, tag):
    a3 = _mm_fwd(h, w_up, norm_w=nw, name=f"ffn{tag}_up", out_dtype=BF16, halves=True, w_t=True, tm=1024, tn=2816)
    p = _ffn_conv_fwd3(a3, conv_w, conv_b.reshape(1, -1), name=f"ffn{tag}_conv")
    h_out = _mm_fwd(p, w_down, residual=h, name=f"ffn{tag}_down", tm=1024, tn=1024)
    return h_out, (a3, p)


def _ffn_bwd(dh, h, saved, nw, w_up, conv_w, conv_b, w_down, tag):
    a3, p = saved
    g_down = _mm_tn(p, dh, name=f"ffn{tag}_down_wg", tk1=1408, tn=1024, tt=1024)
    dp = _mm_nt(dh, w_down, name=f"ffn{tag}_down_dg", out_dtype=BF16, tm=512, tn=2816, tk=1024)
    dhid3, dw3, db3 = _ffn_conv_bwd3(a3, conv_w, conv_b.reshape(1, -1), dp, name=f"ffn{tag}_conv_bwd")
    da3 = _conv_bwd_in3(dhid3, conv_w, K=FFN_CONV, name=f"ffn{tag}_conv_bwd_in")
    g_up = _mm_tn_t(da3, h, norm_w=nw, name=f"ffn{tag}_up_wg", tn=2816, tt=1024, vmem_mb=58)
    dh_out, g_nw = _mm_nt(da3, w_up, epi=(h, nw, dh), name=f"ffn{tag}_up_dg", w_t=True, tm=1024, tk=1408)
    g_cw = jnp.concatenate([dw3[0], dw3[1]], axis=1)
    g_cb = jnp.concatenate([db3[0], db3[1]], axis=1)
    return dh_out, dict(norm=g_nw.reshape(-1), up=g_up, conv_w=g_cw, conv_b=g_cb.reshape(-1), down=g_down)


_BIG = ["ssm_in_w", "ssm_out_w", "w_k", "w_v", "w_q", "w_o", "ffn_up_w", "ffn_down_w"]
_SMALL_SHARDED = ["ssm_norm_w", "ssm_conv_w", "ssm_conv_b", "ssm_gate_norm_w", "ffn_conv_w"]
_SMALL_REPL = ["ssm_dt_bias", "ssm_a_log", "ssm_d", "kv_norm_w", "attn_norm_w", "ffn_norm_w", "ffn_conv_b",
               "final_norm_w"]
_WEIGHTS = ["ssm_norm_w", "ssm_in_w", "ssm_conv_w", "ssm_conv_b", "ssm_dt_bias", "ssm_a_log", "ssm_d",
            "ssm_gate_norm_w", "ssm_out_w", "kv_norm_w", "w_k", "w_v", "attn_norm_w", "w_q", "w_o", "ffn_norm_w",
            "ffn_up_w", "ffn_conv_w", "ffn_conv_b", "ffn_down_w", "final_norm_w"]


def _as2d(a):
    return a.reshape(-1, a.shape[-1])


def _cols_to_full(g):
    return g.transpose(1, 0, 2).reshape(g.shape[1], N_DEV * g.shape[2])


def _pack_small(vals):
    flat = jnp.concatenate([v.reshape(-1).astype(F32) for v in vals])
    n = flat.shape[0]
    rows = -(-n // 1024) * 8
    return jnp.pad(flat, (0, rows * 128 - n)).reshape(rows, 128)


def _unpack_small(packed, shapes):
    flat = packed.reshape(-1)
    out, off = [], 0
    for s in shapes:
        n = math.prod(s)
        out.append(flat[off:off + n].reshape(s))
        off += n
    return out


def _tie(a, token):
    return a + token[0, 0].astype(a.dtype)


def _local_step(x, tgt, get_w, put_g):
    T = x.shape[0]
    Ws = get_w("ssm", None)
    fnw, fcw, fcb = Ws["ffn_norm_w"], Ws["ffn_conv_w"], Ws["ffn_conv_b"]
    zx = _mm_fwd(x, Ws["in_w"], norm_w=Ws["ssm_norm_w"], name="ssm_in", w_t=True, tm=1024, tn=1792)
    xbc_c = _ssm_conv_fwd(zx, Ws["ssm_conv_w"], Ws["ssm_conv_b"].reshape(1, -1), name="ssm_conv")
    dt_raw = zx[:, D_INNER + CONV_DIM:IN_PROJ_DIM]
    dtg = jnp.pad(dt_raw.reshape(T, SSM_GROUPS, 8).transpose(1, 0, 2), ((0, 0), (0, 0), (0, 120)))
    par = jnp.stack([Ws["ssm_dt_bias"].reshape(SSM_GROUPS, 8), Ws["ssm_a_log"].reshape(SSM_GROUPS, 8),
                     Ws["ssm_d"].reshape(SSM_GROUPS, 8)], axis=1)
    par = jnp.pad(par, ((0, 0), (0, 5), (0, 120)))
    gnw = _tie(Ws["ssm_gate_norm_w"].reshape(1, D_INNER), get_w("rest_start", xbc_c))
    y, yn, st = _ssd_fwd(xbc_c, zx, dtg, par, gnw, name="ssd_fwd")
    W0 = get_w("ffn0", y)
    Ws["ssm_out_w"] = W0["ssm_out_w"]
    h1 = _mm_fwd(yn, Ws["ssm_out_w"], residual=x, name="ssm_out", tm=1024, tn=1024)
    h2, ffn0 = _ffn_fwd(h1, fnw[0], W0["up"], fcw[0], fcb[0], W0["down"], "0")
    Wr = get_w("rest", h2)
    q = _mm_fwd(h2, Wr["w_q"], norm_w=Ws["attn_norm_w"], out_dtype=BF16, name="attn_q", tm=1024, tn=1024)
    kv = _mm_fwd(h2, Wr["w_kv"], norm_w=Ws["kv_norm_w"], out_dtype=BF16, name="attn_kv", tm=1024, tn=1024)
    o, lt = _sba_fwd(q, kv, name="sba_fwd")
    h3 = _mm_fwd(o, Wr["w_o"], residual=h2, name="attn_o", tm=1024, tn=1024)
    W1 = get_w("ffn1", h3)
    h4, ffn1 = _ffn_fwd(h3, fnw[1], W1["up"], fcw[1], fcb[1], W1["down"], "1")
    loss, dh4, g_final = _loss_head(h4, tgt, Ws["final_norm_w"], name="loss_head")
    dh3, gf1 = _ffn_bwd(dh4, h3, ffn1, fnw[1], W1["up"], fcw[1], fcb[1], W1["down"], "1")
    tok = put_g("ffn1", dict(up=gf1["up"], down=gf1["down"]))
    g_wo = _mm_tn(o, dh3, name="attn_o_wg", tn=1024, tt=1024)
    do = _mm_nt(dh3, _tie(Wr["w_o"], tok), name="attn_o_dg", out_dtype=BF16, tm=1024, tn=1024, tk=1024)
    dq, dk, dv = _sba_bwd(q, kv, lt, do, name="sba_bwd")
    g_wq = _mm_tn(h2, dq, norm_w=Ws["attn_norm_w"], name="attn_q_wg", tn=1024, tt=1024)
    dh2a, g_attn_nw = _mm_nt(dq, Wr["w_q"], epi=(h2, Ws["attn_norm_w"], dh3), name="attn_q_dg", tm=1024, tk=1024)
    dkv = jnp.concatenate([dk, dv], axis=1)
    g_wkv = _mm_tn(h2, dkv, norm_w=Ws["kv_norm_w"], name="attn_kv_wg", tn=1024, tt=1024)
    dh2, g_kv_nw = _mm_nt(dkv, Wr["w_kv"], epi=(h2, Ws["kv_norm_w"], dh2a), name="attn_kv_dg", tm=1024, tk=1024)
    tok = put_g("attn", dict(w_o=g_wo, w_q=g_wq, w_k=g_wkv[:, :D_MODEL], w_v=g_wkv[:, D_MODEL:]))
    dh1, gf0 = _ffn_bwd(dh2, h1, ffn0, fnw[0], W0["up"], fcw[0], _tie(fcb[0], tok), W0["down"], "0")
    tok = put_g("ffn0", dict(up=gf0["up"], down=gf0["down"]))
    g_out = _mm_tn(yn, dh1, name="ssm_out_wg", tn=1024, tt=1024)
    dyn = _mm_nt(dh1, _tie(Ws["ssm_out_w"], tok), name="ssm_out_dg", out_dtype=BF16, tm=1024, tn=1024, tk=1024)
    tok = put_g("ssm_out", dict(ssm_out_w=g_out))
    dxbc_c, dz, ddt, g_gnw, dpar = _ssd_bwd(xbc_c, zx, dtg, par, _tie(gnw, tok), y, st, dyn, name="ssd_bwd")
    dhid, g_scw, g_scb = _ssm_conv_bwd_pre(zx, Ws["ssm_conv_w"], Ws["ssm_conv_b"].reshape(1, -1), dxbc_c,
                                           name="ssm_conv_bwd")
    dzx = _conv_bwd_in(dhid, Ws["ssm_conv_w"], K=SSM_CONV, name="ssm_conv_bwd_in", into=(dz, D_INNER))
    ddt_t = ddt[:, :, :8].transpose(1, 0, 2).reshape(T, SSM_HEADS).astype(BF16)
    dzx = _put_cols(dzx, jnp.pad(ddt_t, ((0, 0), (0, IN_PROJ_PAD - IN_PROJ_DIM))), D_INNER + CONV_DIM, name="ssm_ddt_cols")
    g_in = _mm_tn_t(dzx, x, norm_w=Ws["ssm_norm_w"], name="ssm_in_wg", tn=1792, tt=1024)
    tok = put_g("ssm_in", dict(ssm_in_w=g_in[:IN_PROJ_DIM]))
    dx, g_ssm_nw = _mm_nt(dzx, Ws["in_w"], epi=(x, _tie(Ws["ssm_norm_w"], tok), dh1), name="ssm_in_dg", w_t=True,
                          tm=1024, tk=1792)
    f = {
        "ssm_norm_w": g_ssm_nw.reshape(-1), "ssm_conv_w": g_scw,
        "ssm_conv_b": g_scb.reshape(-1), "ssm_dt_bias": dpar[:, 0, :8].reshape(-1),
        "ssm_a_log": dpar[:, 1, :8].reshape(-1), "ssm_d": dpar[:, 2, :8].reshape(-1),
        "ssm_gate_norm_w": g_gnw.reshape(-1), "kv_norm_w": g_kv_nw.reshape(-1), "attn_norm_w": g_attn_nw.reshape(-1),
        "ffn_norm_w": jnp.stack([gf0["norm"], gf1["norm"]]), "ffn_conv_w": jnp.stack([gf0["conv_w"], gf1["conv_w"]]),
        "ffn_conv_b": jnp.stack([gf0["conv_b"], gf1["conv_b"]]), "final_norm_w": g_final.reshape(-1),
    }
    return loss, dx, f


def kernel(x, ssm_norm_w, ssm_in_w, ssm_conv_w, ssm_conv_b, ssm_dt_bias, ssm_a_log, ssm_d, ssm_gate_norm_w, ssm_out_w, kv_norm_w, w_k, w_v, attn_norm_w, w_q, w_o, ffn_norm_w, ffn_up_w, ffn_conv_w, ffn_conv_b, ffn_down_w, final_norm_w, loss_target, m_ssm_norm_w, m_ssm_in_w, m_ssm_conv_w, m_ssm_conv_b, m_ssm_dt_bias, m_ssm_a_log, m_ssm_d, m_ssm_gate_norm_w, m_ssm_out_w, m_kv_norm_w, m_w_k, m_w_v, m_attn_norm_w, m_w_q, m_w_o, m_ffn_norm_w, m_ffn_up_w, m_ffn_conv_w, m_ffn_conv_b, m_ffn_down_w, m_final_norm_w, v_ssm_norm_w, v_ssm_in_w, v_ssm_conv_w, v_ssm_conv_b, v_ssm_dt_bias, v_ssm_a_log, v_ssm_d, v_ssm_gate_norm_w, v_ssm_out_w, v_kv_norm_w, v_w_k, v_w_v, v_attn_norm_w, v_w_q, v_w_o, v_ffn_norm_w, v_ffn_up_w, v_ffn_conv_w, v_ffn_conv_b, v_ffn_down_w, v_final_norm_w):
    env = dict(locals())
    p = {n: env[n] for n in _WEIGHTS}
    mom = {n: env["m_" + n] for n in _WEIGHTS}
    var = {n: env["v_" + n] for n in _WEIGHTS}
    T = x.shape[1]
    me = 4 * lax.axis_index("x") + 2 * lax.axis_index("y") + lax.axis_index("c")
    rs = D_FF // N_DEV

    def bf2(a):
        return _as2d(a).astype(BF16)

    _T = ("ssm_in_w", "ffn_up_w")

    def t2d(a):
        return jnp.swapaxes(a, -1, -2).reshape(-1, a.shape[-2])

    def from_t2d(a, like):
        return jnp.swapaxes(a.reshape(like.shape[:-2] + (like.shape[-1], like.shape[-2])), -1, -2)

    n_in, n_up = p["ssm_in_w"].shape[-1], p["ffn_up_w"].shape[-1]

    def with_own(srcs, lands, scatter):
        out = []
        for s, l in zip(srcs, lands):
            own = lax.dynamic_index_in_dim(s, me, 0, keepdims=False) if scatter else s
            out.append(lax.dynamic_update_index_in_dim(l, own, me, 0))
        return out

    a_names = ["ssm_in_w"] + _SMALL_SHARDED
    got_a = dict(zip(a_names, _all_gather([t2d(p["ssm_in_w"]).astype(BF16)] + [_as2d(p[n]) for n in _SMALL_SHARDED],
                                          name="gather_ssm")))
    ffn0_names = ["ssm_out_w", "up0", "down0"]
    rest_names = ["w_q", "w_k", "w_v", "w_o"]
    up_t = jnp.swapaxes(p["ffn_up_w"], -1, -2).astype(BF16)
    shard = {"up0": up_t[0], "down0": bf2(p["ffn_down_w"][0]), "up1": up_t[1],
             "down1": bf2(p["ffn_down_w"][1]), "w_q": bf2(p["w_q"]), "w_k": bf2(p["w_k"]), "w_v": bf2(p["w_v"]),
             "w_o": bf2(p["w_o"]), "ssm_out_w": bf2(p["ssm_out_w"])}

    def anchored(a, on):
        return a + (jnp.where(jnp.isfinite(on), on, 0.0) * 0.0).astype(a.dtype)

    h_ffn0 = _push_start([anchored(shard[ffn0_names[0]], got_a["ssm_norm_w"][0, 0, 0])]
                         + [shard[n] for n in ffn0_names[1:]], scatter=False, name="gather_ffn0_start")
    handles = {}

    def get_w(group, after):
        if group == "ssm":
            W = {n: p[n] for n in _SMALL_REPL}
            for n in ("ssm_dt_bias", "ssm_a_log", "ssm_d", "attn_norm_w"):
                W[n] = W[n].reshape(-1)
            W["in_w"] = jnp.pad(got_a["ssm_in_w"].reshape(IN_PROJ_DIM, D_MODEL), ((0, IN_PROJ_PAD - IN_PROJ_DIM), (0, 0)))
            W["ssm_norm_w"] = _tie(got_a["ssm_norm_w"].reshape(D_MODEL), h_ffn0["token"])
            W["ssm_conv_w"] = _cols_to_full(got_a["ssm_conv_w"])
            W["ssm_conv_b"] = got_a["ssm_conv_b"].reshape(CONV_DIM)
            W["ssm_gate_norm_w"] = got_a["ssm_gate_norm_w"].reshape(D_INNER)
            W["ffn_conv_w"] = _cols_to_full(got_a["ffn_conv_w"]).reshape(2, FFN_CONV, 2 * D_FF)
            return W
        if group == "rest_start":
            handles["rest"] = _push_start([anchored(shard[rest_names[0]], after[0, 0])]
                                          + [shard[n] for n in rest_names[1:]], scatter=False, name="gather_rest_start")
            handles["ffn1"] = _push_start([anchored(shard["up1"], handles["rest"]["token"][0, 0]), shard["down1"]],
                                          scatter=False, name="gather_ffn1_start")
            return handles["ffn1"]["token"]
        if group == "ffn1":
            srcs, lands = _push_wait(handles["ffn1"], after, name="gather_ffn1_wait")
            up, down = with_own(srcs, lands, False)
            return dict(up=up.reshape(2 * D_FF, D_MODEL), down=down.reshape(D_FF, D_MODEL))
        if group == "ffn0":
            srcs, lands = _push_wait(h_ffn0, after, name="gather_ffn0_wait")
            out, up, down = with_own(srcs, lands, False)
            return dict(ssm_out_w=out.reshape(D_INNER, D_MODEL), up=up.reshape(2 * D_FF, D_MODEL),
                        down=down.reshape(D_FF, D_MODEL))
        srcs, lands = _push_wait(handles["rest"], after, name="gather_rest_wait")
        g = dict(zip(rest_names, with_own(srcs, lands, False)))
        sq = lambda a: a.reshape(D_MODEL, D_MODEL)
        return dict(w_q=sq(g["w_q"]), w_kv=jnp.concatenate([sq(g["w_k"]), sq(g["w_v"])], axis=1), w_o=sq(g["w_o"]))

    pending = []

    def put_g(group, g):
        if group in ("ffn0", "ffn1"):
            keys = [("ffn_up_w", int(group[-1])), ("ffn_down_w", int(group[-1]))]
            blocks = [g["up"].reshape(N_DEV, n_up, D_MODEL), g["down"].reshape(N_DEV, rs, D_MODEL)]
        elif group == "attn":
            keys = [(n, None) for n in ("w_o", "w_q", "w_k", "w_v")]
            blocks = [g[n].reshape(N_DEV, D_MODEL // N_DEV, D_MODEL) for n, _ in keys]
        elif group == "ssm_out":
            keys = [("ssm_out_w", None)]
            blocks = [g["ssm_out_w"].reshape(N_DEV, D_INNER // N_DEV, D_MODEL)]
        else:
            keys = [("ssm_in_w", None)]
            blocks = [g["ssm_in_w"].reshape(N_DEV, n_in, D_MODEL)]
        h = _push_start(blocks, scatter=True, name=f"exchange_{group}_start")
        pending.append((group, keys, h))
        return h["token"]

    loss_row, dx, f = _local_step(x.reshape(T, D_MODEL), loss_target.reshape(T, D_MODEL), get_w, put_g)

    small_names = _SMALL_REPL + _SMALL_SHARDED
    small_full = _pack_small([f[n] for n in small_names] + [loss_row[0, 0:1]])
    small_bcast = jnp.broadcast_to(small_full[None], (N_DEV,) + small_full.shape)
    h_small = _push_start([small_bcast], scatter=True, name="exchange_small_start")
    tok = h_small["token"]

    arrived, res = {}, {}
    after = dx
    for group, keys, h in pending:
        srcs, lands = _push_wait(h, after, name=f"exchange_{group}_wait")
        arrived.update(zip(keys, with_own(srcs, lands, True)))
        for n in _BIG:
            layered = (n, 0) in arrived or (n, 1) in arrived
            if n in res or not ((n, None) in arrived or ((n, 0) in arrived and (n, 1) in arrived)):
                continue
            parts = [arrived[(n, 0)], arrived[(n, 1)]] if layered else arrived[(n, None)]
            w2, m2, v2 = ((t2d if n in _T else _as2d)(a[n]) for a in (p, mom, var))
            if not res:
                w2 = _tie(w2, tok)
            tiles = {"ffn_down_w": dict(tr=rs), "ffn_up_w": dict(tr=n_up // 2), "ssm_in_w": dict(tr=n_in, tc=256)}
            res[n] = _adamw(parts, w2, m2, v2, name=f"adamw_{n}", **tiles.get(n, dict(tr=256)))
            after = res[n][0]
    srcs, lands = _push_wait(h_small, after, name="exchange_small_wait")
    small_parts = with_own(srcs, lands, True)[0]
    out_g, out_d, out_m, out_v = {}, {}, {}, {}
    for n in _BIG:
        out_g[n], out_d[n], out_m[n], out_v[n] = (from_t2d(t, p[n]) if n in _T else t.reshape(p[n].shape) for t in res[n])

    zero = jnp.zeros_like(small_full)
    g_small_sum = _adamw(small_parts, zero, zero, zero, name="sum_small_grads", tr=small_full.shape[0])[0]
    *small_sums, loss_sum = _unpack_small(g_small_sum, [f[n].shape for n in small_names] + [(1,)])
    loss = loss_sum[0]
    g_small = dict(zip(small_names, small_sums))
    for n in _SMALL_SHARDED:
        width = p[n].shape[-1]
        g_small[n] = lax.dynamic_slice_in_dim(g_small[n], me * width, width, axis=g_small[n].ndim - 1)
    sw = _pack_small([p[n] for n in small_names])
    sm = _pack_small([mom[n] for n in small_names])
    sv = _pack_small([var[n] for n in small_names])
    sg = _pack_small([g_small[n] for n in small_names])
    _, d, nm, nv = _adamw(sg[None], sw, sm, sv, name="adamw_small", tr=sw.shape[0])
    shard_shapes = [p[n].shape for n in small_names]
    for n, dd, mm, vv in zip(small_names, _unpack_small(d, shard_shapes), _unpack_small(nm, shard_shapes),
                             _unpack_small(nv, shard_shapes)):
        out_g[n] = g_small[n].reshape(p[n].shape)
        out_d[n], out_m[n], out_v[n] = dd, mm, vv

    return (loss, dx.reshape(x.shape), *[out_g[n] for n in _WEIGHTS], *[out_d[n] for n in _WEIGHTS],
            *[out_m[n] for n in _WEIGHTS], *[out_v[n] for n in _WEIGHTS])
```

```python
import functools
import math

import jax
import jax.numpy as jnp
from jax import lax
from jax.experimental import pallas as pl
from jax.experimental.pallas import tpu as pltpu

F32 = jnp.float32
BF16 = jnp.bfloat16
EPS = 1e-6

D_MODEL = 1024
D_INNER = 2048
SSM_HEADS = 32
SSM_GROUPS = 4
SSM_STATE = 128
SSM_CONV = 4
SSM_CHUNK = 128
GN = SSM_GROUPS * SSM_STATE
CONV_DIM = D_INNER + 2 * GN
IN_PROJ_DIM = D_INNER + CONV_DIM + SSM_HEADS
IN_PROJ_PAD = 5376
SB_HEADS = 16
SB_HEAD_DIM = 64
SB_BLOCK = 128
D_FF = 2816
FFN_CONV = 3
N_DEV = 8

ADAM_LR = 0.001
ADAM_B1 = 0.9
ADAM_B2 = 0.999
ADAM_EPS = 1e-08
ADAM_WD = 0.01
ADAM_STEP = 10

_MESH = pl.DeviceIdType.MESH
_NT = (((1,), (1,)), ((), ()))
_TN = (((0,), (0,)), ((), ()))
_ANY = pl.BlockSpec(memory_space=pl.ANY)


def _cparams(sem, vmem_mb=48):
    return pltpu.CompilerParams(dimension_semantics=sem, vmem_limit_bytes=vmem_mb * 1024 * 1024)


def _sigmoid(x):
    return 0.5 * jnp.tanh(0.5 * x) + 0.5


def _softplus(x):
    return jnp.maximum(x, 0.0) + jnp.log(1.0 + jnp.exp(-jnp.abs(x)))


def _rms_fwd(xv, w):
    r = lax.rsqrt(jnp.mean(xv * xv, axis=-1, keepdims=True) + EPS)
    return xv * r * w


def _mm_fwd(x, w, *, name, norm_w=None, residual=None, out_dtype=F32, tm=512, tn=512, halves=False, w_t=False):
    M, K = x.shape
    N = w.shape[0] if w_t else w.shape[1]
    tm, tn = min(tm, M), min(tn, N)
    assert M % tm == 0 and N % tn == 0, (name, M, N, tm, tn)
    if halves:
        nbh = N // 2 // tn
        assert N // 2 % tn == 0
        out_spec = pl.BlockSpec((None, tm, tn), lambda i, j: (lax.div(j, nbh), i, lax.rem(j, nbh)))
        out_shape = jax.ShapeDtypeStruct((2, M, N // 2), out_dtype)
    else:
        out_spec = pl.BlockSpec((tm, tn), lambda i, j: (i, j))
        out_shape = jax.ShapeDtypeStruct((M, N), out_dtype)
    has_norm, has_res = norm_w is not None, residual is not None

    def body(*refs):
        x_ref, w_ref = refs[0], refs[1]
        p = 2
        nw_ref = r_ref = None
        if has_norm:
            nw_ref = refs[p]
            p += 1
        if has_res:
            r_ref = refs[p]
            p += 1
        o_ref = refs[p]
        xv = x_ref[...]
        if has_norm:
            xv = _rms_fwd(xv.astype(F32), nw_ref[...])
        acc = lax.dot_general(xv.astype(BF16), w_ref[...], _NT if w_t else (((1,), (0,)), ((), ())),
                              preferred_element_type=F32)
        if has_res:
            acc = acc + r_ref[...]
        o_ref[...] = acc.astype(out_dtype)

    w_spec = pl.BlockSpec((tn, K), lambda i, j: (j, 0)) if w_t else pl.BlockSpec((K, tn), lambda i, j: (0, j))
    in_specs = [pl.BlockSpec((tm, K), lambda i, j: (i, 0)), w_spec]
    args = [x, w]
    if has_norm:
        in_specs.append(pl.BlockSpec((1, K), lambda i, j: (0, 0)))
        args.append(norm_w.reshape(1, K))
    if has_res:
        in_specs.append(pl.BlockSpec((tm, tn), lambda i, j: (i, j)))
        args.append(residual)
    return pl.pallas_call(
        body, name=name, grid=(M // tm, N // tn), in_specs=in_specs,
        out_specs=out_spec, out_shape=out_shape,
        compiler_params=_cparams(("parallel", "parallel")))(*args)


def _mm_nt(dy, w, *, name, epi=None, out_dtype=F32, tm=512, tn=512, tk=512, w_t=False, vmem_mb=48):
    halves = dy.ndim == 3
    M, K = (dy.shape[1], 2 * dy.shape[2]) if halves else dy.shape
    N = w.shape[1] if w_t else w.shape[0]
    tm, tk = min(tm, M), min(tk, K)
    tn = N if epi is not None else min(tn, N)
    assert M % tm == 0 and N % tn == 0 and K % tk == 0, (name, M, N, K, tm, tn, tk)
    nk = K // tk
    has_epi = epi is not None

    def body(*refs):
        if has_epi:
            dy_ref, w_ref, h_ref, nw_ref, r_ref, o_ref, dnw_ref, acc_ref = refs
        else:
            dy_ref, w_ref, o_ref, acc_ref = refs
        i = pl.program_id(0)
        k = pl.program_id(2)

        @pl.when(k == 0)
        def _():
            acc_ref[...] = jnp.zeros_like(acc_ref)

        acc_ref[...] += lax.dot_general(dy_ref[...].astype(BF16), w_ref[...], (((1,), (0,)), ((), ())) if w_t else _NT,
                                        preferred_element_type=F32)

        @pl.when(k == nk - 1)
        def _():
            du = acc_ref[...]
            if has_epi:
                hv = h_ref[...]
                r = lax.rsqrt(jnp.mean(hv * hv, axis=-1, keepdims=True) + EPS)
                xhat = hv * r
                dxh = du * nw_ref[...]
                dx = r * (dxh - xhat * jnp.mean(dxh * xhat, axis=-1, keepdims=True))
                o_ref[...] = (r_ref[...] + dx).astype(out_dtype)
                contrib = jnp.sum(du * xhat, axis=0, keepdims=True)

                @pl.when(i == 0)
                def _():
                    dnw_ref[...] = contrib

                @pl.when(i > 0)
                def _():
                    dnw_ref[...] += contrib
            else:
                o_ref[...] = du.astype(out_dtype)

    if halves:
        nkh = K // 2 // tk
        assert K // 2 % tk == 0
        dy_spec = pl.BlockSpec((None, tm, tk), lambda i, j, k: (lax.div(k, nkh), i, lax.rem(k, nkh)))
    else:
        dy_spec = pl.BlockSpec((tm, tk), lambda i, j, k: (i, k))
    w_spec = pl.BlockSpec((tk, tn), lambda i, j, k: (k, j)) if w_t else pl.BlockSpec((tn, tk), lambda i, j, k: (j, k))
    in_specs = [dy_spec, w_spec]
    args = [dy, w]
    out_specs = [pl.BlockSpec((tm, tn), lambda i, j, k: (i, j))]
    out_shape = [jax.ShapeDtypeStruct((M, N), out_dtype)]
    if has_epi:
        h, nw, res = epi
        in_specs += [pl.BlockSpec((tm, N), lambda i, j, k: (i, 0)), pl.BlockSpec((1, N), lambda i, j, k: (0, 0)),
                     pl.BlockSpec((tm, N), lambda i, j, k: (i, 0))]
        args += [h, nw.reshape(1, N), res]
        out_specs.append(pl.BlockSpec((1, N), lambda i, j, k: (0, 0)))
        out_shape.append(jax.ShapeDtypeStruct((1, N), F32))
    outs = pl.pallas_call(
        body, name=name, grid=(M // tm, N // tn, nk), in_specs=in_specs, out_specs=out_specs, out_shape=out_shape,
        scratch_shapes=[pltpu.VMEM((tm, tn), F32)],
        compiler_params=_cparams(("arbitrary", "arbitrary", "arbitrary"), vmem_mb))(*args)
    return (outs[0], outs[1]) if has_epi else outs[0]


def _mm_tn(x, dy, *, name, norm_w=None, out_dtype=BF16, tk1=1024, tn=512, tt=512):
    T, K1 = x.shape
    halves = dy.ndim == 3
    N = 2 * dy.shape[2] if halves else dy.shape[1]
    tk1, tn, tt = min(tk1, K1), min(tn, N), min(tt, T)
    has_norm = norm_w is not None
    assert K1 % tk1 == 0 and N % tn == 0 and T % tt == 0, (name, K1, N, T, tk1, tn, tt)
    assert not has_norm or tk1 == K1
    nt = T // tt

    def body(*refs):
        if has_norm:
            x_ref, dy_ref, nw_ref, o_ref, acc_ref = refs
        else:
            x_ref, dy_ref, o_ref, acc_ref = refs
        t = pl.program_id(2)

        @pl.when(t == 0)
        def _():
            acc_ref[...] = jnp.zeros_like(acc_ref)

        xv = x_ref[...]
        if has_norm:
            xv = _rms_fwd(xv.astype(F32), nw_ref[...])
        acc_ref[...] += lax.dot_general(xv.astype(BF16), dy_ref[...].astype(BF16), _TN, preferred_element_type=F32)

        @pl.when(t == nt - 1)
        def _():
            o_ref[...] = acc_ref[...].astype(out_dtype)

    if halves:
        nbh = N // 2 // tn
        assert N // 2 % tn == 0
        dy_spec = pl.BlockSpec((None, tt, tn), lambda a, b, t: (lax.div(b, nbh), t, lax.rem(b, nbh)))
    else:
        dy_spec = pl.BlockSpec((tt, tn), lambda a, b, t: (t, b))
    in_specs = [pl.BlockSpec((tt, tk1), lambda a, b, t: (t, a)), dy_spec]
    args = [x, dy]
    if has_norm:
        in_specs.append(pl.BlockSpec((1, K1), lambda a, b, t: (0, 0)))
        args.append(norm_w.reshape(1, K1))
    return pl.pallas_call(
        body, name=name, grid=(K1 // tk1, N // tn, nt), in_specs=in_specs,
        out_specs=pl.BlockSpec((tk1, tn), lambda a, b, t: (a, b)),
        out_shape=jax.ShapeDtypeStruct((K1, N), out_dtype),
        scratch_shapes=[pltpu.VMEM((tk1, tn), F32)],
        compiler_params=_cparams(("parallel", "parallel", "arbitrary")))(*args)


def _mm_tn_t(dy, x, *, name, norm_w, out_dtype=BF16, tn=1408, tt=1024, vmem_mb=48):
    T, K1 = x.shape
    halves = dy.ndim == 3
    N = 2 * dy.shape[2] if halves else dy.shape[1]
    tn, tt = min(tn, N), min(tt, T)
    assert N % tn == 0 and T % tt == 0, (name, N, T, tn, tt)
    nt = T // tt

    def body(dy_ref, x_ref, nw_ref, o_ref, acc_ref):
        t = pl.program_id(1)

        @pl.when(t == 0)
        def _():
            acc_ref[...] = jnp.zeros_like(acc_ref)

        xn = _rms_fwd(x_ref[...].astype(F32), nw_ref[...]).astype(BF16)
        acc_ref[...] += lax.dot_general(dy_ref[...].astype(BF16), xn, _TN, preferred_element_type=F32)

        @pl.when(t == nt - 1)
        def _():
            o_ref[...] = acc_ref[...].astype(out_dtype)

    if halves:
        nbh = N // 2 // tn
        assert N // 2 % tn == 0
        dy_spec = pl.BlockSpec((None, tt, tn), lambda b, t: (lax.div(b, nbh), t, lax.rem(b, nbh)))
    else:
        dy_spec = pl.BlockSpec((tt, tn), lambda b, t: (t, b))
    return pl.pallas_call(
        body, name=name, grid=(N // tn, nt),
        in_specs=[dy_spec, pl.BlockSpec((tt, K1), lambda b, t: (t, 0)), pl.BlockSpec((1, K1), lambda b, t: (0, 0))],
        out_specs=pl.BlockSpec((tn, K1), lambda b, t: (b, 0)),
        out_shape=jax.ShapeDtypeStruct((N, K1), out_dtype),
        scratch_shapes=[pltpu.VMEM((tn, K1), F32)],
        compiler_params=_cparams(("parallel", "arbitrary"), vmem_mb))(dy, x, norm_w.reshape(1, K1))


def _shift_down(xb, prev8, j):
    main = pltpu.roll(xb, j, 0)
    head = pltpu.roll(xb[0:8], j, 0)
    ph = pltpu.roll(prev8, j, 0)
    row8 = lax.broadcasted_iota(jnp.int32, head.shape, 0)
    head = jnp.where(row8 < j, ph, head)
    return jnp.concatenate([head, main[8:]], axis=0)


def _shift_up(xb, next8, j):
    tt = xb.shape[0]
    main = pltpu.roll(xb, tt - j, 0)
    tail = pltpu.roll(xb[tt - 8:tt], 8 - j, 0)
    nh = pltpu.roll(next8, 8 - j, 0)
    row8 = lax.broadcasted_iota(jnp.int32, tail.shape, 0)
    tail = jnp.where(row8 + j >= 8, nh, tail)
    return jnp.concatenate([main[:tt - 8], tail], axis=0)


def _conv_hid(xb, prev8, w, b_row, K):
    out = b_row
    shifted = []
    for j in range(K):
        sh = K - 1 - j
        xs = xb if sh == 0 else _shift_down(xb, prev8, sh)
        shifted.append(xs)
        out = out + xs * w[j:j + 1, :]
    return out, shifted


def _prev_idx(i, nb8):
    return jnp.maximum(i * nb8 - 1, 0)


def _ssm_conv_fwd(zx, w, b, *, name, tt=512, tc=512):
    T = zx.shape[0]
    tt = min(tt, T)
    C, K = CONV_DIM, SSM_CONV
    cb0, nb8 = D_INNER // tc, tt // 8

    def body(x_ref, p_ref, w_ref, b_ref, o_ref):
        first = (pl.program_id(1) > 0).astype(F32)
        hid, _ = _conv_hid(x_ref[...], p_ref[...] * first, w_ref[...], b_ref[...], K)
        o_ref[...] = hid * _sigmoid(hid)

    return pl.pallas_call(
        body, name=name, grid=(C // tc, T // tt),
        in_specs=[pl.BlockSpec((tt, tc), lambda c, i: (i, c + cb0)),
                  pl.BlockSpec((8, tc), lambda c, i: (_prev_idx(i, nb8), c + cb0)),
                  pl.BlockSpec((K, tc), lambda c, i: (0, c)), pl.BlockSpec((1, tc), lambda c, i: (0, c))],
        out_specs=pl.BlockSpec((tt, tc), lambda c, i: (i, c)),
        out_shape=jax.ShapeDtypeStruct((T, C), F32),
        compiler_params=_cparams(("parallel", "parallel")))(zx, zx, w, b)


def _ssm_conv_bwd_pre(zx, w, b, dout, *, name, tt=512, tc=512):
    T = zx.shape[0]
    tt = min(tt, T)
    C, K = CONV_DIM, SSM_CONV
    cb0, nb8 = D_INNER // tc, tt // 8

    def body(x_ref, p_ref, w_ref, b_ref, d_ref, dh_ref, dw_ref, db_ref):
        t = pl.program_id(1)
        first = (t > 0).astype(F32)
        hid, shifted = _conv_hid(x_ref[...], p_ref[...] * first, w_ref[...], b_ref[...], K)
        sg = _sigmoid(hid)
        dh = d_ref[...] * (sg * (1.0 + hid * (1.0 - sg)))
        dh_ref[...] = dh

        @pl.when(t == 0)
        def _():
            dw_ref[...] = jnp.zeros_like(dw_ref)
            db_ref[...] = jnp.zeros_like(db_ref)

        db_ref[...] += jnp.sum(dh, axis=0, keepdims=True)
        for j in range(K):
            dw_ref[j:j + 1, :] += jnp.sum(dh * shifted[j], axis=0, keepdims=True)

    return pl.pallas_call(
        body, name=name, grid=(C // tc, T // tt),
        in_specs=[pl.BlockSpec((tt, tc), lambda c, i: (i, c + cb0)),
                  pl.BlockSpec((8, tc), lambda c, i: (_prev_idx(i, nb8), c + cb0)),
                  pl.BlockSpec((K, tc), lambda c, i: (0, c)), pl.BlockSpec((1, tc), lambda c, i: (0, c)),
                  pl.BlockSpec((tt, tc), lambda c, i: (i, c))],
        out_specs=[pl.BlockSpec((tt, tc), lambda c, i: (i, c)), pl.BlockSpec((K, tc), lambda c, i: (0, c)),
                   pl.BlockSpec((1, tc), lambda c, i: (0, c))],
        out_shape=[jax.ShapeDtypeStruct((T, C), F32), jax.ShapeDtypeStruct((K, C), F32),
                   jax.ShapeDtypeStruct((1, C), F32)],
        compiler_params=_cparams(("parallel", "arbitrary")))(zx, zx, w, b, dout)


def _put_cols(buf, src, col0, *, name, tt=512):
    T, C = src.shape
    tt = min(tt, T)

    def body(s_ref, _, o_ref):
        o_ref[...] = s_ref[...]

    return pl.pallas_call(
        body, name=name, grid=(T // tt,),
        in_specs=[pl.BlockSpec((tt, C), lambda i: (i, 0)), _ANY],
        out_specs=pl.BlockSpec((tt, C), lambda i: (i, col0 // C)),
        out_shape=jax.ShapeDtypeStruct(buf.shape, buf.dtype), input_output_aliases={1: 0},
        compiler_params=_cparams(("parallel",)))(src, buf)


def _conv_bwd_in(dh, w, *, name, K, tt=512, tc=512, out_dtype=BF16, into=None):
    T, C = dh.shape
    tt = min(tt, T)
    nb8, nT = tt // 8, T // tt
    last8 = T // 8 - 1
    cb0 = 0 if into is None else into[1] // tc

    def body(d_ref, n_ref, w_ref, *rest):
        o_ref = rest[-1]
        notlast = (pl.program_id(1) < nT - 1).astype(F32)
        d = d_ref[...]
        nxt = n_ref[...] * notlast
        w_ = w_ref[...]
        acc = d * w_[K - 1:K, :]
        for sh in range(1, K):
            acc = acc + _shift_up(d, nxt, sh) * w_[K - 1 - sh:K - sh, :]
        o_ref[...] = acc.astype(out_dtype)

    in_specs = [pl.BlockSpec((tt, tc), lambda c, i: (i, c)),
                pl.BlockSpec((8, tc), lambda c, i: (jnp.minimum((i + 1) * nb8, last8), c)),
                pl.BlockSpec((K, tc), lambda c, i: (0, c))]
    args = [dh, dh, w]
    if into is None:
        out_shape, alias = jax.ShapeDtypeStruct((T, C), out_dtype), {}
    else:
        assert into[0].dtype == out_dtype and into[1] % tc == 0
        in_specs.append(_ANY)
        args.append(into[0])
        out_shape, alias = jax.ShapeDtypeStruct(into[0].shape, out_dtype), {3: 0}
    return pl.pallas_call(
        body, name=name, grid=(C // tc, nT), in_specs=in_specs,
        out_specs=pl.BlockSpec((tt, tc), lambda c, i: (i, c + cb0)),
        out_shape=out_shape, input_output_aliases=alias,
        compiler_params=_cparams(("parallel", "parallel")))(*args)


def _ffn_conv_fwd3(a3, w, b, *, name, tt=256, tc=1408):
    T = a3.shape[1]
    tt = min(tt, T)
    K, nbh, n16 = FFN_CONV, D_FF // tc, tt // 16

    def body(a_ref, p_ref, wg_ref, wv_ref, bg_ref, bv_ref, o_ref):
        first = (pl.program_id(1) > 0).astype(F32)
        a = a_ref[...].astype(F32)
        prev = p_ref[...].astype(F32)[:, 8:16, :] * first
        hg, _ = _conv_hid(a[0], prev[0], wg_ref[...], bg_ref[...], K)
        hv, _ = _conv_hid(a[1], prev[1], wv_ref[...], bv_ref[...], K)
        o_ref[...] = (hg * _sigmoid(hg) * hv).astype(BF16)

    return pl.pallas_call(
        body, name=name, grid=(nbh, T // tt),
        in_specs=[pl.BlockSpec((2, tt, tc), lambda c, i: (0, i, c)),
                  pl.BlockSpec((2, 16, tc), lambda c, i: (0, _prev_idx(i, n16), c)),
                  pl.BlockSpec((K, tc), lambda c, i: (0, c)), pl.BlockSpec((K, tc), lambda c, i: (0, c + nbh)),
                  pl.BlockSpec((1, tc), lambda c, i: (0, c)), pl.BlockSpec((1, tc), lambda c, i: (0, c + nbh))],
        out_specs=pl.BlockSpec((tt, tc), lambda c, i: (i, c)),
        out_shape=jax.ShapeDtypeStruct((T, D_FF), BF16),
        compiler_params=_cparams(("parallel", "parallel")))(a3, a3, w, w, b, b)


def _ffn_conv_bwd3(a3, w, b, dp, *, name, tt=256, tc=1408):
    T = a3.shape[1]
    tt = min(tt, T)
    K, nbh, n16 = FFN_CONV, D_FF // tc, tt // 16

    def body(a_ref, p_ref, wg_ref, wv_ref, bg_ref, bv_ref, dp_ref, dh_ref, dw_ref, db_ref):
        t = pl.program_id(1)
        first = (t > 0).astype(F32)
        a = a_ref[...].astype(F32)
        prev = p_ref[...].astype(F32)[:, 8:16, :] * first
        hg, sh_g = _conv_hid(a[0], prev[0], wg_ref[...], bg_ref[...], K)
        hv, sh_v = _conv_hid(a[1], prev[1], wv_ref[...], bv_ref[...], K)
        sg = _sigmoid(hg)
        d = dp_ref[...].astype(F32)
        dhg = d * hv * (sg * (1.0 + hg * (1.0 - sg)))
        dhv = d * (hg * sg)
        dh_ref[0] = dhg.astype(BF16)
        dh_ref[1] = dhv.astype(BF16)

        @pl.when(t == 0)
        def _():
            dw_ref[...] = jnp.zeros_like(dw_ref)
            db_ref[...] = jnp.zeros_like(db_ref)

        db_ref[0] += jnp.sum(dhg, axis=0, keepdims=True)
        db_ref[1] += jnp.sum(dhv, axis=0, keepdims=True)
        for j in range(K):
            dw_ref[0, j:j + 1, :] += jnp.sum(dhg * sh_g[j], axis=0, keepdims=True)
            dw_ref[1, j:j + 1, :] += jnp.sum(dhv * sh_v[j], axis=0, keepdims=True)

    return pl.pallas_call(
        body, name=name, grid=(nbh, T // tt),
        in_specs=[pl.BlockSpec((2, tt, tc), lambda c, i: (0, i, c)),
                  pl.BlockSpec((2, 16, tc), lambda c, i: (0, _prev_idx(i, n16), c)),
                  pl.BlockSpec((K, tc), lambda c, i: (0, c)), pl.BlockSpec((K, tc), lambda c, i: (0, c + nbh)),
                  pl.BlockSpec((1, tc), lambda c, i: (0, c)), pl.BlockSpec((1, tc), lambda c, i: (0, c + nbh)),
                  pl.BlockSpec((tt, tc), lambda c, i: (i, c))],
        out_specs=[pl.BlockSpec((2, tt, tc), lambda c, i: (0, i, c)), pl.BlockSpec((2, K, tc), lambda c, i: (0, 0, c)),
                   pl.BlockSpec((2, 1, tc), lambda c, i: (0, 0, c))],
        out_shape=[jax.ShapeDtypeStruct((2, T, D_FF), BF16), jax.ShapeDtypeStruct((2, K, D_FF), F32),
                   jax.ShapeDtypeStruct((2, 1, D_FF), F32)],
        compiler_params=_cparams(("parallel", "arbitrary")))(a3, a3, w, w, b, b, dp)


def _conv_bwd_in3(dh3, w, *, name, K, tt=256, tc=1408):
    H, T, C = dh3.shape
    tt = min(tt, T)
    nb, n16, nT = C // tc, tt // 16, T // tt
    last16 = T // 16 - 1

    def body(d_ref, n_ref, w_ref, o_ref):
        notlast = (pl.program_id(2) < nT - 1).astype(F32)
        d = d_ref[...].astype(F32)
        nxt = n_ref[...].astype(F32)[0:8, :] * notlast
        w_ = w_ref[...]
        acc = d * w_[K - 1:K, :]
        for sh in range(1, K):
            acc = acc + _shift_up(d, nxt, sh) * w_[K - 1 - sh:K - sh, :]
        o_ref[...] = acc.astype(BF16)

    return pl.pallas_call(
        body, name=name, grid=(H, nb, nT),
        in_specs=[pl.BlockSpec((None, tt, tc), lambda h, c, i: (h, i, c)),
                  pl.BlockSpec((None, 16, tc), lambda h, c, i: (h, jnp.minimum((i + 1) * n16, last16), c)),
                  pl.BlockSpec((K, tc), lambda h, c, i: (0, h * nb + c))],
        out_specs=pl.BlockSpec((None, tt, tc), lambda h, c, i: (h, i, c)),
        out_shape=jax.ShapeDtypeStruct((H, T, C), BF16),
        compiler_params=_cparams(("parallel", "parallel", "parallel")))(dh3, dh3, w)


def _cumsum_rows(x):
    L = x.shape[0]
    row = lax.broadcasted_iota(jnp.int32, x.shape, 0)
    k = 1
    while k < L:
        x = x + jnp.where(row >= k, pltpu.roll(x, k, 0), 0.0)
        k *= 2
    return x


def _rcumsum_rows(x):
    L = x.shape[0]
    row = lax.broadcasted_iota(jnp.int32, x.shape, 0)
    k = 1
    while k < L:
        x = x + jnp.where(row < L - k, pltpu.roll(x, L - k, 0), 0.0)
        k *= 2
    return x


def _split_terms(m, n):
    terms, rest = [], m
    for _ in range(n):
        t = rest.astype(BF16)
        terms.append(t)
        rest = rest - t.astype(F32)
    return jnp.concatenate(terms, axis=1)


def _select_dot(m, n_terms, n_out, cond):
    K = m.shape[1]
    k = lax.broadcasted_iota(jnp.int32, (K, n_out), 0)
    j = lax.broadcasted_iota(jnp.int32, (K, n_out), 1)
    sel = cond(k, j).astype(BF16)
    return jnp.dot(_split_terms(m, n_terms), jnp.concatenate([sel] * n_terms, axis=0), preferred_element_type=F32)


def _rowsum_mxu(m):
    return _select_dot(m, 2, 128, lambda k, j: k >= 0)


def _lane_block_sums(m, width):
    shift = width.bit_length() - 1
    return _select_dot(m, 2, 128, lambda k, j: j == jnp.right_shift(k, shift))


def _heads_to_pairs(m):
    return _select_dot(m, 3, 512, lambda k, j: k == jnp.right_shift(j, 6))


def _ssd_common(dt_ref, par_ref):
    par = par_ref[...]
    raw = dt_ref[...] + par[0:1, :]
    dt = _softplus(raw)
    a = -jnp.exp(par[1:2, :])
    cs = _cumsum_rows(dt * a)
    L = cs.shape[0]
    cs_last = cs[L - 1:L, :]
    return raw, dt, a, par[2:3, :], cs, cs.T, jnp.exp(cs), jnp.exp(cs_last - cs), jnp.exp(cs_last)


def _ssd_specs(nc, rev):
    L = SSM_CHUNK

    def ci(c):
        return nc - 1 - c if rev else c

    return [pl.BlockSpec((L, D_INNER), lambda c: (ci(c), 0)),
            pl.BlockSpec((L, GN), lambda c: (ci(c), D_INNER // GN)),
            pl.BlockSpec((L, GN), lambda c: (ci(c), D_INNER // GN + 1)),
            pl.BlockSpec((SSM_GROUPS, L, 128), lambda c: (0, ci(c), 0)),
            pl.BlockSpec((SSM_GROUPS, 8, 128), lambda c: (0, 0, 0)),
            pl.BlockSpec((L, D_INNER), lambda c: (ci(c), 0)),
            pl.BlockSpec((1, D_INNER), lambda c: (0, 0))], ci


def _round_robin(gens):
    live = list(gens)
    while live:
        nxt = []
        for gen in live:
            try:
                next(gen)
                nxt.append(gen)
            except StopIteration:
                pass
        live = nxt


def _group_views(g, wide, narrow, lead):
    return ([r.at[:, g * 512:(g + 1) * 512] for r in wide], [r.at[:, g * 128:(g + 1) * 128] for r in narrow],
            [r.at[g] for r in lead])


def _ssd_fwd(xbc_c, zx, dtg, par, gnw, *, name):
    T = xbc_c.shape[0]
    L = SSM_CHUNK
    nc = T // L
    in_specs, ci = _ssd_specs(nc, False)

    def body(xs_ref, b_ref, c_ref, dt_ref, par_ref, z_ref, gnw_ref, y_ref, yn_ref, st_ref, h_ref):
        @pl.when(pl.program_id(0) == 0)
        def _():
            h_ref[...] = jnp.zeros_like(h_ref)

        gens = []
        for g in range(SSM_GROUPS):
            (xs, z, gw, y, yn), (b, c), (dt, pr, st, h) = _group_views(
                g, [xs_ref, z_ref, gnw_ref, y_ref, yn_ref], [b_ref, c_ref], [dt_ref, par_ref, st_ref, h_ref])
            gens.append(group(xs, b, c, dt, pr, z, gw, y, yn, st, h))
        _round_robin(gens)

    def group(xs_ref, b_ref, c_ref, dt_ref, par_ref, z_ref, gnw_ref, y_ref, yn_ref, st_ref, h_ref):
        _, dt, _, dsk, cs, csT, ecs, eend, dec = _ssd_common(dt_ref, par_ref)
        Bb = b_ref[...].astype(BF16)
        Cb = c_ref[...].astype(BF16)
        G = lax.dot_general(Cb, Bb, _NT, preferred_element_type=F32)
        row = lax.broadcasted_iota(jnp.int32, (L, L), 0)
        col = lax.broadcasted_iota(jnp.int32, (L, L), 1)
        tril = col <= row
        lo = lax.broadcasted_iota(jnp.int32, (L, 128), 1) < 64
        lo1 = lax.broadcasted_iota(jnp.int32, (1, 128), 1) < 64
        dt_x, ecs_x, eend_x = (_heads_to_pairs(m) for m in (dt, ecs, eend))
        for pp in range(4):
            hA, hB = 2 * pp, 2 * pp + 1
            lanes = slice(pp * 128, (pp + 1) * 128)

            def sel1(m):
                return jnp.where(lo1, m[:, hA:hA + 1], m[:, hB:hB + 1])

            X = xs_ref[:, lanes]
            xd = X * dt_x[:, lanes]
            xdb = xd.astype(BF16)
            ys = []
            for h in (hA, hB):
                Lm = jnp.where(tril, jnp.exp(jnp.minimum(cs[:, h:h + 1] - csT[h:h + 1, :], 0.0)), 0.0)
                ys.append(jnp.dot((G * Lm).astype(BF16), xdb, preferred_element_type=F32))
                yield
            Hp = h_ref[pp]
            st_ref[pp] = Hp
            yoff = jnp.dot(Cb, Hp.astype(BF16), preferred_element_type=F32) * ecs_x[:, lanes]
            y_ref[:, lanes] = jnp.where(lo, ys[0], ys[1]) + yoff + sel1(dsk) * X
            S = lax.dot_general(Bb, (xd * eend_x[:, lanes]).astype(BF16), _TN, preferred_element_type=F32)
            h_ref[pp] = Hp * sel1(dec) + S
            yield
        zv = z_ref[...]
        yg = y_ref[...] * (zv * _sigmoid(zv))
        r = jnp.tile(lax.rsqrt(_rowsum_mxu(yg * yg) * (1.0 / 512) + EPS), (1, 4))
        yn_ref[...] = (yg * r * gnw_ref[...]).astype(BF16)

    return pl.pallas_call(
        body, name=name, grid=(nc,), in_specs=in_specs,
        out_specs=[pl.BlockSpec((L, D_INNER), lambda c: (c, 0)), pl.BlockSpec((L, D_INNER), lambda c: (c, 0)),
                   pl.BlockSpec((SSM_GROUPS, None, 4, 128, 128), lambda c: (0, c, 0, 0, 0))],
        out_shape=[jax.ShapeDtypeStruct((T, D_INNER), F32), jax.ShapeDtypeStruct((T, D_INNER), BF16),
                   jax.ShapeDtypeStruct((SSM_GROUPS, nc, 4, 128, 128), F32)],
        scratch_shapes=[pltpu.VMEM((SSM_GROUPS, 4, 128, 128), F32)],
        compiler_params=_cparams(("arbitrary",)))(xbc_c, xbc_c, xbc_c, dtg, par, zx, gnw)


def _ssd_bwd(xbc_c, zx, dtg, par, gnw, y, st, dyn, *, name):
    T = xbc_c.shape[0]
    L = SSM_CHUNK
    nc = T // L
    in_specs, ci = _ssd_specs(nc, True)
    in_specs += [pl.BlockSpec((L, D_INNER), lambda c: (ci(c), 0)),
                 pl.BlockSpec((SSM_GROUPS, None, 4, 128, 128), lambda c: (0, ci(c), 0, 0, 0)),
                 pl.BlockSpec((L, D_INNER), lambda c: (ci(c), 0))]

    def body(xs_ref, b_ref, c_ref, dt_ref, par_ref, z_ref, gnw_ref, y_ref, st_ref, dyn_ref,
             dxbc_ref, dz_ref, ddt_ref, dgnw_ref, dpar_ref, dh_ref):
        @pl.when(pl.program_id(0) == 0)
        def _():
            dh_ref[...] = jnp.zeros_like(dh_ref)
            dgnw_ref[...] = jnp.zeros_like(dgnw_ref)
            dpar_ref[...] = jnp.zeros_like(dpar_ref)

        dxs_ref = dxbc_ref.at[:, 0:D_INNER]
        db_ref = dxbc_ref.at[:, D_INNER:D_INNER + GN]
        dc_ref = dxbc_ref.at[:, D_INNER + GN:CONV_DIM]

        gens = []
        for g in range(SSM_GROUPS):
            (xs, z, gw, y, dyn, dxs, dz, dgw), (b, c, db, dc), (dt, pr, st, ddt, dpr, dh) = _group_views(
                g, [xs_ref, z_ref, gnw_ref, y_ref, dyn_ref, dxs_ref, dz_ref, dgnw_ref], [b_ref, c_ref, db_ref, dc_ref],
                [dt_ref, par_ref, st_ref, ddt_ref, dpar_ref, dh_ref])
            gens.append(group(xs, b, c, dt, pr, z, gw, y, st, dyn, dxs, db, dc, dz, ddt, dgw, dpr, dh))
        _round_robin(gens)

    def group(xs_ref, b_ref, c_ref, dt_ref, par_ref, z_ref, gnw_ref, y_ref, st_ref, dyn_ref,
              dxs_ref, db_ref, dc_ref, dz_ref, ddt_ref, dgnw_ref, dpar_ref, dh_ref):
        yv = y_ref[...]
        zv = z_ref[...]
        sg = _sigmoid(zv)
        sz = zv * sg
        yg = yv * sz
        r = jnp.tile(lax.rsqrt(_rowsum_mxu(yg * yg) * (1.0 / 512) + EPS), (1, 4))
        yh = yg * r
        dyn = dyn_ref[...].astype(F32)
        dgnw_ref[...] += jnp.sum(dyn * yh, axis=0, keepdims=True)
        dyh = dyn * gnw_ref[...]
        dyg = r * (dyh - yh * jnp.tile(_rowsum_mxu(dyh * yh) * (1.0 / 512), (1, 4)))
        dY_all = dyg * sz
        dz_ref[...] = (dyg * yv * (sg * (1.0 + zv * (1.0 - sg)))).astype(dz_ref.dtype)

        yield
        raw, dt, a, dsk, cs, csT, ecs, eend, dec = _ssd_common(dt_ref, par_ref)
        Bb = b_ref[...].astype(BF16)
        Cb = c_ref[...].astype(BF16)
        G = lax.dot_general(Cb, Bb, _NT, preferred_element_type=F32)
        row = lax.broadcasted_iota(jnp.int32, (L, L), 0)
        col = lax.broadcasted_iota(jnp.int32, (L, L), 1)
        tril = col <= row
        lo = lax.broadcasted_iota(jnp.int32, (L, 128), 1) < 64
        lane1 = lax.broadcasted_iota(jnp.int32, (1, 128), 1)
        lo1 = lane1 < 64
        rowl = lax.broadcasted_iota(jnp.int32, (L, 128), 0)
        dt_x, ecs_x, eend_x = (_heads_to_pairs(m) for m in (dt, ecs, eend))
        dG = jnp.zeros((L, L), F32)
        dB = jnp.zeros((L, SSM_STATE), F32)
        dC = jnp.zeros((L, SSM_STATE), F32)
        dcs_t = jnp.zeros((L, L), F32)
        tails = jnp.zeros((1, 128), F32)
        dD_row = jnp.zeros((1, 128), F32)
        v_parts, prod_parts = [], []

        def tot(m):
            return jnp.sum(jnp.sum(m, axis=0, keepdims=True), axis=1, keepdims=True)

        for pp in range(4):
            hA, hB = 2 * pp, 2 * pp + 1
            lanes = slice(pp * 128, (pp + 1) * 128)

            def sel1(m):
                return jnp.where(lo1, m[:, hA:hA + 1], m[:, hB:hB + 1])

            X = xs_ref[:, lanes]
            dY = dY_all[:, lanes]
            dtsel = dt_x[:, lanes]
            xd = X * dtsel
            xdb = xd.astype(BF16)
            dYb = dY.astype(BF16)
            Hp = st_ref[pp]
            Hb = Hp.astype(BF16)
            dHn = dh_ref[pp]
            dHb = dHn.astype(BF16)
            ecs_sel = ecs_x[:, lanes]
            eend_sel = eend_x[:, lanes]
            dxd_state = jnp.dot(Bb, dHb, preferred_element_type=F32) * eend_sel
            yoff = jnp.dot(Cb, Hb, preferred_element_type=F32) * ecs_sel
            dYe = (dY * ecs_sel).astype(BF16)
            dC = dC + lax.dot_general(dYe, Hb, _NT, preferred_element_type=F32)
            dB = dB + lax.dot_general((xd * eend_sel).astype(BF16), dHb, _NT, preferred_element_type=F32)
            dh_ref[pp] = dHn * sel1(dec) + lax.dot_general(Cb, dYe, _TN, preferred_element_type=F32)
            q = xd * dxd_state
            dyq = dY * yoff - q
            qcol = jnp.sum(q, axis=0, keepdims=True)
            hcol = jnp.sum(dHn * Hp, axis=0, keepdims=True)
            dxd_diag = []
            for h, msk, msk1 in ((hA, lo, lo1), (hB, jnp.logical_not(lo), jnp.logical_not(lo1))):
                Lm = jnp.where(tril, jnp.exp(jnp.minimum(cs[:, h:h + 1] - csT[h:h + 1, :], 0.0)), 0.0)
                M = G * Lm
                dxd_diag.append(lax.dot_general(M.astype(BF16), dYb, _TN, preferred_element_type=F32))
                dM = lax.dot_general(jnp.where(msk, dY, 0.0).astype(BF16), xdb, _NT, preferred_element_type=F32)
                dG = dG + dM * Lm
                W = dM * M
                dcs_t = dcs_t + jnp.where(row == h, jnp.sum(W, axis=0, keepdims=True), 0.0)
                v_parts.append(W + jnp.where(msk, dyq, 0.0))
                tail = (jnp.sum(jnp.where(msk1, qcol, 0.0), axis=1, keepdims=True)
                        + dec[:, h:h + 1] * jnp.sum(jnp.where(msk1, hcol, 0.0), axis=1, keepdims=True))
                tails = tails + jnp.where(lane1 == h, tail, 0.0)
                yield
            dxd = jnp.where(lo, dxd_diag[0], dxd_diag[1]) + dxd_state
            prod_parts.append(dxd * X)
            dxs_ref[:, lanes] = dxd * dtsel + sel1(dsk) * dY
            dyx = jnp.sum(dY * X, axis=0, keepdims=True)
            sA = jnp.sum(jnp.where(lo1, dyx, 0.0), axis=1, keepdims=True)
            sB = jnp.sum(dyx, axis=1, keepdims=True) - sA
            dD_row = dD_row + jnp.where(lane1 == hA, sA, 0.0) + jnp.where(lane1 == hB, sB, 0.0)
            yield
        dGb = dG.astype(BF16)
        db_ref[...] = dB + lax.dot_general(dGb, Cb, _TN, preferred_element_type=F32)
        dc_ref[...] = dC + jnp.dot(dGb, Bb, preferred_element_type=F32)
        dcs_mat = _lane_block_sums(jnp.concatenate(v_parts, axis=1), 128) + jnp.where(rowl == L - 1, tails, 0.0)
        ddt_mat = _lane_block_sums(jnp.concatenate(prod_parts, axis=1), 64)
        dad = _rcumsum_rows(dcs_mat - dcs_t.T)
        draw = (a * dad + ddt_mat) * _sigmoid(raw)
        ddt_ref[...] = draw
        dpar_ref[0:1, :] += jnp.sum(draw, axis=0, keepdims=True)
        dpar_ref[1:2, :] += jnp.sum(dt * dad, axis=0, keepdims=True) * a
        dpar_ref[2:3, :] += dD_row

    return pl.pallas_call(
        body, name=name, grid=(nc,), in_specs=in_specs,
        out_specs=[pl.BlockSpec((L, CONV_DIM), lambda c: (ci(c), 0)),
                   pl.BlockSpec((L, D_INNER), lambda c: (ci(c), 0)),
                   pl.BlockSpec((SSM_GROUPS, L, 128), lambda c: (0, ci(c), 0)),
                   pl.BlockSpec((1, D_INNER), lambda c: (0, 0)),
                   pl.BlockSpec((SSM_GROUPS, 8, 128), lambda c: (0, 0, 0))],
        out_shape=[jax.ShapeDtypeStruct((T, CONV_DIM), F32), jax.ShapeDtypeStruct((T, IN_PROJ_PAD), BF16),
                   jax.ShapeDtypeStruct((SSM_GROUPS, T, 128), F32), jax.ShapeDtypeStruct((1, D_INNER), F32),
                   jax.ShapeDtypeStruct((SSM_GROUPS, 8, 128), F32)],
        scratch_shapes=[pltpu.VMEM((SSM_GROUPS, 4, 128, 128), F32)],
        compiler_params=_cparams(("arbitrary",)))(xbc_c, xbc_c, xbc_c, dtg, par, zx, gnw, y, st, dyn)


SB_KEYS = 512
SB_SCAN = 256
SB_STRIP = 256


def _tri(width, cond):
    kk = lax.broadcasted_iota(jnp.int32, (width, width), 0)
    jj = lax.broadcasted_iota(jnp.int32, (width, width), 1)
    return cond(kk, jj).astype(BF16)


_LOG2E = 1.4426950408889634


def _softplus2(z2):
    return jnp.maximum(z2, 0.0) + jnp.log2(1.0 + jnp.exp2(-jnp.abs(z2)))


def _sba_sub_fwd(zb, c, U, mask):
    z2 = zb * _LOG2E
    s = _softplus2(z2)
    if mask is not None:
        s = jnp.where(mask, s, 0.0)
    R = c + jnp.dot(s.astype(BF16), U, preferred_element_type=F32)
    A = jnp.exp2(z2 - s - R)
    if mask is not None:
        A = jnp.where(mask, A, 0.0)
    return A.astype(BF16), R[:, 0:1] + s[:, 0:1]


def _sba_sub_bwd(zb, dAb, Lt, pc, pe, Uincl, Uexcl, mask):
    last = zb.shape[1] - 1
    z2 = zb * _LOG2E
    s = _softplus2(z2)
    g = z2 - s
    if mask is not None:
        s = jnp.where(mask, s, 0.0)
    P = pc + jnp.dot(s.astype(BF16), Uincl, preferred_element_type=F32)
    A = jnp.exp2(g - (Lt - P))
    if mask is not None:
        A = jnp.where(mask, A, 0.0)
    E = dAb * A
    PE = pe + jnp.dot(E.astype(BF16), Uexcl, preferred_element_type=F32)
    dz = E - jnp.exp2(g) * (E + PE)
    if mask is not None:
        dz = jnp.where(mask, dz, 0.0)
    return (A.astype(BF16), dz.astype(BF16), P[:, last:last + 1], PE[:, last:last + 1] + E[:, last:last + 1])


def _stack_heads(v):
    lo = lax.broadcasted_iota(jnp.int32, v.shape, 1) < 64
    zero = jnp.zeros_like(v)
    return jnp.concatenate([jnp.where(lo, v, zero), jnp.where(lo, zero, v)], axis=0)


def _unstack_heads(v):
    lo = lax.broadcasted_iota(jnp.int32, (SB_BLOCK, 128), 1) < 64
    return jnp.where(lo, v[:SB_BLOCK], v[SB_BLOCK:])


def _sba_rows(a):
    return slice(2 * a * SB_BLOCK, 2 * (a + 1) * SB_BLOCK)


def _sba_diag_case(a, b):
    Bq = SB_BLOCK
    if b * SB_SCAN >= (a + 1) * Bq:
        return "skip"
    if (b + 1) * SB_SCAN <= a * Bq:
        return "full"
    rowi = lax.broadcasted_iota(jnp.int32, (2 * Bq, SB_SCAN), 0)
    qpos = a * Bq + jnp.where(rowi >= Bq, rowi - Bq, rowi)
    return b * SB_SCAN + lax.broadcasted_iota(jnp.int32, (2 * Bq, SB_SCAN), 1) < qpos


def _sba_fwd(q, kv, *, name):
    T = q.shape[0]
    Bq = SB_BLOCK
    nsub = SB_KEYS // Bq
    nscan = SB_KEYS // SB_SCAN
    R = 2 * SB_KEYS
    assert T % SB_KEYS == 0 and SB_STRIP == 2 * Bq
    scale = 1.0 / math.sqrt(SB_HEAD_DIM)

    def body(q_ref, k_ref, v_ref, o_ref, lt_ref, z_s, a_s, c_s, acc_s):
        i = pl.program_id(1)
        U2 = _tri(SB_SCAN, lambda k, j: k > j)
        qs_all = jnp.concatenate([_stack_heads(q_ref[a * Bq:(a + 1) * Bq, :] * scale) for a in range(nsub)], axis=0)
        c_s[...] = jnp.zeros_like(c_s)
        acc_s[...] = jnp.zeros_like(acc_s)

        def scores(J, slot):
            off = pl.multiple_of(J * SB_KEYS, SB_KEYS)
            z_s[slot] = lax.dot_general(qs_all, k_ref[pl.ds(off, SB_KEYS), :], _NT, preferred_element_type=F32)

        def weights(slot, diag):
            for a in range(nsub):
                rows = _sba_rows(a)
                c = c_s[rows, :]
                for b in reversed(range(nscan)):
                    cols = slice(b * SB_SCAN, (b + 1) * SB_SCAN)
                    case = _sba_diag_case(a, b) if diag else "full"
                    if isinstance(case, str) and case == "skip":
                        a_s[slot, rows, cols] = jnp.zeros((2 * Bq, SB_SCAN), BF16)
                        continue
                    A, c = _sba_sub_fwd(z_s[slot, rows, cols], c, U2, None if isinstance(case, str) else case)
                    a_s[slot, rows, cols] = A
                c_s[rows, :] = c

        def values(J, slot):
            off = pl.multiple_of(J * SB_KEYS, SB_KEYS)
            acc_s[...] += jnp.dot(a_s[slot], v_ref[pl.ds(off, SB_KEYS), :], preferred_element_type=F32)

        scores(i, 0)
        weights(0, True)
        scores(jnp.maximum(i - 1, 0), 1)

        def two_steps(u, _):
            t = 2 * u + 1
            weights(1, False)
            scores(jnp.maximum(i - t - 1, 0), 0)
            values(i - t + 1, 0)
            weights(0, False)
            scores(jnp.maximum(i - t - 2, 0), 1)
            values(i - t, 1)
            return 0

        lax.fori_loop(0, i // 2, two_steps, 0)
        odd = lax.rem(i, 2) == 1

        @pl.when(jnp.logical_not(odd))
        def _():
            values(0, 0)

        @pl.when(odd)
        def _():
            weights(1, False)
            values(1, 0)
            values(0, 1)
        for a in range(nsub):
            o_ref[a * Bq:(a + 1) * Bq, :] = _unstack_heads(acc_s[_sba_rows(a), :]).astype(BF16)
            lt_ref[a * Bq:(a + 1) * Bq, :] = _unstack_heads(jnp.broadcast_to(c_s[_sba_rows(a), :], (2 * Bq, 128)))

    return pl.pallas_call(
        body, name=name, grid=(SB_HEADS // 2, T // SB_KEYS),
        in_specs=[pl.BlockSpec((SB_KEYS, 128), lambda p, i: (i, p)), pl.BlockSpec((T, 128), lambda p, i: (0, p)),
                  pl.BlockSpec((T, 128), lambda p, i: (0, p + SB_HEADS // 2))],
        out_specs=[pl.BlockSpec((SB_KEYS, 128), lambda p, i: (i, p)),
                   pl.BlockSpec((None, SB_KEYS, 128), lambda p, i: (p, i, 0))],
        out_shape=[jax.ShapeDtypeStruct((T, D_MODEL), BF16), jax.ShapeDtypeStruct((SB_HEADS // 2, T, 128), F32)],
        scratch_shapes=[pltpu.VMEM((2, R, SB_KEYS), F32), pltpu.VMEM((2, R, SB_KEYS), BF16),
                        pltpu.VMEM((R, 1), F32), pltpu.VMEM((R, 128), F32)],
        compiler_params=_cparams(("parallel", "parallel")))(q, kv, kv)


def _sba_bwd(q, kv, lt, do, *, name):
    T = q.shape[0]
    Bq = SB_BLOCK
    nq = T // SB_KEYS
    nsub = SB_KEYS // Bq
    nscan = SB_KEYS // SB_SCAN
    R = 2 * SB_KEYS
    assert T % SB_KEYS == 0 and SB_STRIP == 2 * Bq
    scale = 1.0 / math.sqrt(SB_HEAD_DIM)

    def body(q_ref, k_ref, v_ref, lt_ref, do_ref, dq_ref, dk_ref, dv_ref, dk_acc, dv_acc,
             z_s, da_s, a_s, dz_s, pc_s, pe_s, lt_s, dq_s):
        i = pl.program_id(1)

        @pl.when(i == 0)
        def _():
            dk_acc[...] = jnp.zeros_like(dk_acc)
            dv_acc[...] = jnp.zeros_like(dv_acc)

        Uincl = _tri(SB_SCAN, lambda k, j: k <= j)
        Uexcl = _tri(SB_SCAN, lambda k, j: k < j)
        qs, dos = [], []
        for a in range(nsub):
            rows = slice(a * Bq, (a + 1) * Bq)
            qs.append(_stack_heads(q_ref[rows, :] * scale))
            dos.append(_stack_heads(do_ref[rows, :]))
            lt_s[_sba_rows(a), :] = jnp.concatenate([lt_ref[rows, 0:1], lt_ref[rows, 64:65]], axis=0)
        qs_all = jnp.concatenate(qs, axis=0)
        dos_all = jnp.concatenate(dos, axis=0)
        pc_s[...] = jnp.zeros_like(pc_s)
        pe_s[...] = jnp.zeros_like(pe_s)
        a_s[1] = jnp.zeros((R, SB_KEYS), BF16)
        dz_s[1] = jnp.zeros((R, SB_KEYS), BF16)

        def scores(J, slot):
            off = pl.multiple_of(J * SB_KEYS, SB_KEYS)
            z_s[slot] = lax.dot_general(qs_all, k_ref[pl.ds(off, SB_KEYS), :], _NT, preferred_element_type=F32)
            da_s[slot] = lax.dot_general(dos_all, v_ref[pl.ds(off, SB_KEYS), :], _NT, preferred_element_type=F32)

        def gradients(slot, diag):
            for a in range(nsub):
                rows = _sba_rows(a)
                pc, pe, Lt = pc_s[rows, :], pe_s[rows, :], lt_s[rows, :]
                for b in range(nscan):
                    cols = slice(b * SB_SCAN, (b + 1) * SB_SCAN)
                    case = _sba_diag_case(a, b) if diag else "full"
                    if isinstance(case, str) and case == "skip":
                        a_s[slot, rows, cols] = jnp.zeros((2 * Bq, SB_SCAN), BF16)
                        dz_s[slot, rows, cols] = jnp.zeros((2 * Bq, SB_SCAN), BF16)
                        continue
                    A, dz, pc, pe = _sba_sub_bwd(z_s[slot, rows, cols], da_s[slot, rows, cols], Lt, pc, pe, Uincl, Uexcl,
                                                 None if isinstance(case, str) else case)
                    a_s[slot, rows, cols] = A
                    dz_s[slot, rows, cols] = dz
                pc_s[rows, :] = pc
                pe_s[rows, :] = pe

        def products(J, slot):
            off = pl.multiple_of(J * SB_KEYS, SB_KEYS)
            dzt = dz_s[slot]
            dk_acc[pl.ds(off, SB_KEYS), :] += lax.dot_general(dzt, qs_all, _TN, preferred_element_type=F32)
            dv_acc[pl.ds(off, SB_KEYS), :] += lax.dot_general(a_s[slot], dos_all, _TN, preferred_element_type=F32)
            dq_s[...] += jnp.dot(dzt, k_ref[pl.ds(off, SB_KEYS), :], preferred_element_type=F32)

        dq_s[...] = jnp.zeros_like(dq_s)
        scores(0, 0)

        def two_steps(u, _):
            t = 2 * u
            gradients(0, False)
            scores(t + 1, 1)
            products(jnp.maximum(t - 1, 0), 1)
            gradients(1, False)
            scores(t + 2, 0)
            products(t, 0)
            return 0

        lax.fori_loop(0, i // 2, two_steps, 0)
        odd = lax.rem(i, 2) == 1

        @pl.when(jnp.logical_not(odd))
        def _():
            gradients(0, True)
            products(jnp.maximum(i - 1, 0), 1)
            products(i, 0)

        @pl.when(odd)
        def _():
            gradients(0, False)
            scores(i, 1)
            products(jnp.maximum(i - 2, 0), 1)
            gradients(1, True)
            products(i - 1, 0)
            products(i, 1)

        for a in range(nsub):
            dq_ref[a * Bq:(a + 1) * Bq, :] = (_unstack_heads(dq_s[_sba_rows(a), :]) * scale).astype(BF16)

        @pl.when(i == nq - 1)
        def _():
            dk_ref[...] = dk_acc[...].astype(BF16)
            dv_ref[...] = dv_acc[...].astype(BF16)

    return pl.pallas_call(
        body, name=name, grid=(SB_HEADS // 2, nq),
        in_specs=[pl.BlockSpec((SB_KEYS, 128), lambda p, i: (i, p)), pl.BlockSpec((T, 128), lambda p, i: (0, p)),
                  pl.BlockSpec((T, 128), lambda p, i: (0, p + SB_HEADS // 2)),
                  pl.BlockSpec((None, SB_KEYS, 128), lambda p, i: (p, i, 0)),
                  pl.BlockSpec((SB_KEYS, 128), lambda p, i: (i, p))],
        out_specs=[pl.BlockSpec((SB_KEYS, 128), lambda p, i: (i, p)), pl.BlockSpec((T, 128), lambda p, i: (0, p)),
                   pl.BlockSpec((T, 128), lambda p, i: (0, p))],
        out_shape=[jax.ShapeDtypeStruct((T, D_MODEL), BF16), jax.ShapeDtypeStruct((T, D_MODEL), BF16),
                   jax.ShapeDtypeStruct((T, D_MODEL), BF16)],
        scratch_shapes=[pltpu.VMEM((T, 128), F32), pltpu.VMEM((T, 128), F32),
                        pltpu.VMEM((2, R, SB_KEYS), F32), pltpu.VMEM((2, R, SB_KEYS), F32),
                        pltpu.VMEM((2, R, SB_KEYS), BF16), pltpu.VMEM((2, R, SB_KEYS), BF16),
                        pltpu.VMEM((R, 1), F32), pltpu.VMEM((R, 1), F32), pltpu.VMEM((R, 1), F32),
                        pltpu.VMEM((R, 128), F32)],
        compiler_params=_cparams(("parallel", "arbitrary")))(q, kv, kv, lt, do)


def _loss_head(h, tgt, w, *, name, tt=512):
    T, D = h.shape
    tt = min(tt, T)

    def body(h_ref, t_ref, w_ref, loss_ref, dh_ref, dw_ref):
        i = pl.program_id(0)
        hv = h_ref[...]
        wv = w_ref[...]
        r = lax.rsqrt(jnp.mean(hv * hv, axis=-1, keepdims=True) + EPS)
        xhat = hv * r
        err = xhat * wv - t_ref[...]
        part = 0.5 * jnp.sum(jnp.mean(err * err, axis=-1, keepdims=True), axis=0, keepdims=True)
        dy = err * (1.0 / D)
        dxh = dy * wv
        dh_ref[...] = r * (dxh - xhat * jnp.mean(dxh * xhat, axis=-1, keepdims=True))
        dwc = jnp.sum(dy * xhat, axis=0, keepdims=True)

        @pl.when(i == 0)
        def _():
            loss_ref[...] = jnp.broadcast_to(part, loss_ref.shape)
            dw_ref[...] = dwc

        @pl.when(i > 0)
        def _():
            loss_ref[...] += jnp.broadcast_to(part, loss_ref.shape)
            dw_ref[...] += dwc

    return pl.pallas_call(
        body, name=name, grid=(T // tt,),
        in_specs=[pl.BlockSpec((tt, D), lambda i: (i, 0)), pl.BlockSpec((tt, D), lambda i: (i, 0)),
                  pl.BlockSpec((1, D), lambda i: (0, 0))],
        out_specs=[pl.BlockSpec((1, 128), lambda i: (0, 0)), pl.BlockSpec((tt, D), lambda i: (i, 0)),
                   pl.BlockSpec((1, D), lambda i: (0, 0))],
        out_shape=[jax.ShapeDtypeStruct((1, 128), F32), jax.ShapeDtypeStruct((T, D), F32),
                   jax.ShapeDtypeStruct((1, D), F32)],
        compiler_params=_cparams(("arbitrary",)))(h, tgt, w.reshape(1, D))


def _adamw(parts, w, m, v, *, name, tr=256, tc=None):
    plist = list(parts) if isinstance(parts, (list, tuple)) else [parts]
    P, _, C = plist[0].shape
    R = sum(a.shape[1] for a in plist)
    tr = min(tr, R)
    tc = C if tc is None else tc
    assert all(a.shape[1] % tr == 0 for a in plist) and C % tc == 0, (name, R, C, tr, tc)
    nbs = [a.shape[1] // tr for a in plist]
    offs = [sum(nbs[:l]) for l in range(len(nbs))]
    c1 = 1.0 - ADAM_B1 ** ADAM_STEP
    c2 = 1.0 - ADAM_B2 ** ADAM_STEP

    def body(*refs):
        p_refs = refs[:len(plist)]
        w_ref, m_ref, v_ref, g_ref, d_ref, nm_ref, nv_ref = refs[len(plist):]
        i = pl.program_id(0)
        g = None
        for l, p_ref in enumerate(p_refs):
            gl = p_ref[0].astype(F32)
            for k in range(1, P):
                gl = gl + p_ref[k].astype(F32)
            g = gl if g is None else jnp.where(i >= offs[l], gl, g)
        mn = ADAM_B1 * m_ref[...] + (1.0 - ADAM_B1) * g
        vn = ADAM_B2 * v_ref[...] + (1.0 - ADAM_B2) * (g * g)
        g_ref[...] = g
        nm_ref[...] = mn
        nv_ref[...] = vn
        d_ref[...] = -ADAM_LR * ((mn / c1) / (jnp.sqrt(vn / c2) + ADAM_EPS) + ADAM_WD * w_ref[...])

    spec = pl.BlockSpec((tr, tc), lambda i, j: (i, j))
    sds = jax.ShapeDtypeStruct((R, C), F32)
    return pl.pallas_call(
        body, name=name, grid=(R // tr, C // tc),
        in_specs=[pl.BlockSpec((P, tr, tc), functools.partial(lambda i, j, o, n: (0, jnp.clip(i - o, 0, n - 1), j), o=o, n=n))
                  for o, n in zip(offs, nbs)] + [spec, spec, spec],
        out_specs=[spec, spec, spec, spec], out_shape=[sds, sds, sds, sds],
        compiler_params=_cparams(("parallel", "parallel")))(*plist, w, m, v)


def _all_gather(shards, *, name):
    n = len(shards)

    def body(*refs):
        ins, outs = refs[:n], refs[n:2 * n]
        send_sems, recv_sems, local_sems = refs[2 * n:]
        x, y, c = lax.axis_index("x"), lax.axis_index("y"), lax.axis_index("c")
        me, sib = (x, y, c), (x, y, 1 - c)
        chips = [(1 - x, y), (x, 1 - y), (1 - x, 1 - y)]

        def slot(p):
            return 4 * p[0] + 2 * p[1] + p[2]

        def cp(a, k, block, to, src=None):
            dst = outs[a].at[slot(block)]
            return pltpu.make_async_remote_copy(src_ref=dst if src is None else src, dst_ref=dst,
                                                send_sem=send_sems.at[a, k], recv_sem=recv_sems.at[a, k],
                                                device_id=to, device_id_type=_MESH)

        mine = [pltpu.make_async_copy(ins[a], outs[a].at[slot(me)], local_sems.at[a]) for a in range(n)]
        for m in mine:
            m.start()
        first = []
        for a in range(n):
            first.append(cp(a, 0, me, sib, src=ins[a]))
            for j, chip in enumerate(chips):
                first.append(cp(a, 1 + j, me, (*chip, c), src=ins[a]))
        for f in first:
            f.start()
        passed = []
        for j, chip in enumerate(chips):
            for a in range(n):
                cp(a, 1 + j, (*chip, c), me).wait_recv()
                f = cp(a, 4 + j, (*chip, c), sib)
                f.start()
                passed.append(f)
        for a in range(n):
            cp(a, 0, sib, me).wait_recv()
            for j, chip in enumerate(chips):
                cp(a, 4 + j, (*chip, 1 - c), me).wait_recv()
        for f in first + passed:
            f.wait_send()
        for m in mine:
            m.wait()

    return pl.pallas_call(
        body, name=name, in_specs=[_ANY] * n, out_specs=[_ANY] * n,
        out_shape=[jax.ShapeDtypeStruct((N_DEV,) + s.shape, s.dtype) for s in shards],
        scratch_shapes=[pltpu.SemaphoreType.DMA((n, 7)), pltpu.SemaphoreType.DMA((n, 7)),
                        pltpu.SemaphoreType.DMA((n,))])(*shards)


_HBM = pl.BlockSpec(memory_space=pltpu.HBM)
_SEM = pl.BlockSpec(memory_space=pltpu.SEMAPHORE)
_EFFECT = pltpu.SideEffectType.DATAFLOW_SIDE_EFFECTING


def _peers():
    x, y, c = lax.axis_index("x"), lax.axis_index("y"), lax.axis_index("c")
    out = []
    for r in range(1, N_DEV):
        px = 1 - x if (r >> 2) & 1 else x
        py = 1 - y if (r >> 1) & 1 else y
        pc = 1 - c if r & 1 else c
        out.append(((px, py, pc), 4 * px + 2 * py + pc))
    return 4 * x + 2 * y + c, out


def _push_copy(src_ref, land_ref, send_sems, recv_sems, a, k, me, peer, peer_slot, scatter, arriving):
    src = src_ref.at[peer_slot] if scatter else src_ref
    return pltpu.make_async_remote_copy(
        src_ref=src, dst_ref=land_ref.at[peer_slot if arriving else me], send_sem=send_sems.at[a * (N_DEV - 1) + k],
        recv_sem=recv_sems.at[a * (N_DEV - 1) + k], device_id=peer, device_id_type=_MESH)


def _push_start(srcs, *, scatter, name):
    n = len(srcs)
    lands = [lax.empty(s.shape if scatter else (N_DEV,) + s.shape, s.dtype) for s in srcs]

    def body(*refs):
        src_refs, land_refs = refs[:n], refs[n:2 * n]
        send_sems, recv_sems = refs[2 * n], refs[2 * n + 1]
        token = refs[-1]
        me, peers = _peers()
        for k, (peer, slot) in enumerate(peers):
            for a in range(n):
                _push_copy(src_refs[a], land_refs[a], send_sems, recv_sems, a, k, me, peer, slot, scatter, False).start()
        token[...] = jnp.zeros_like(token)

    hbm = lambda a: pltpu.HBM(a.shape, a.dtype)
    outs = pl.pallas_call(
        body, name=name,
        out_shape=(pltpu.SemaphoreType.DMA((n * (N_DEV - 1),)), pltpu.SemaphoreType.DMA((n * (N_DEV - 1),)),
                   *[hbm(s) for s in srcs], *[hbm(l) for l in lands], jax.ShapeDtypeStruct((8, 128), F32)),
        in_specs=[_HBM] * (2 * n),
        out_specs=(_SEM, _SEM, *([_HBM] * (2 * n)), pl.BlockSpec(memory_space=pltpu.VMEM)),
        input_output_aliases={i: 2 + i for i in range(2 * n)},
        compiler_params=pltpu.CompilerParams(has_side_effects=_EFFECT),
    )(*[pltpu.with_memory_space_constraint(s, pltpu.HBM) for s in srcs],
      *[pltpu.with_memory_space_constraint(l, pltpu.HBM) for l in lands])
    return dict(send=outs[0], recv=outs[1], srcs=list(outs[2:2 + n]), lands=list(outs[2 + n:2 + 2 * n]),
                token=outs[-1], scatter=scatter, n=n)


def _push_wait(h, after, *, name):
    n, scatter = h["n"], h["scatter"]

    def body(*refs):
        src_refs, land_refs = refs[:n], refs[n:2 * n]
        send_sems, recv_sems = refs[2 * n], refs[2 * n + 1]
        me, peers = _peers()
        for k, (peer, slot) in enumerate(peers):
            for a in range(n):
                cp = _push_copy(src_refs[a], land_refs[a], send_sems, recv_sems, a, k, me, peer, slot, scatter, True)
                cp.wait_send()
                cp.wait_recv()

    hbm = lambda a: pltpu.HBM(a.shape, a.dtype)
    outs = pl.pallas_call(
        body, name=name,
        out_shape=(*[hbm(s) for s in h["srcs"]], *[hbm(l) for l in h["lands"]]),
        in_specs=[_HBM] * (2 * n) + [_SEM, _SEM, _ANY], out_specs=tuple([_HBM] * (2 * n)),
        input_output_aliases={i: i for i in range(2 * n)},
        compiler_params=pltpu.CompilerParams(has_side_effects=_EFFECT),
    )(*h["srcs"], *h["lands"], h["send"], h["recv"], after)
    return list(outs[:n]), list(outs[n:])


def _ffn_fwd(h, nw, w_up, conv_w, conv_b, w_down, tag):
    a3 = _mm_fwd(h, w_up, norm_w=nw, name=f"ffn{tag}_up", out_dtype=BF16, halves=True, w_t=True, tm=1024, tn=2816)
    p = _ffn_conv_fwd3(a3, conv_w, conv_b.reshape(1, -1), name=f"ffn{tag}_conv")
    h_out = _mm_fwd(p, w_down, residual=h, name=f"ffn{tag}_down", tm=1024, tn=1024)
    return h_out, (a3, p)


def _ffn_bwd(dh, h, saved, nw, w_up, conv_w, conv_b, w_down, tag):
    a3, p = saved
    g_down = _mm_tn(p, dh, name=f"ffn{tag}_down_wg", tk1=1408, tn=1024, tt=1024)
    dp = _mm_nt(dh, w_down, name=f"ffn{tag}_down_dg", out_dtype=BF16, tm=1024, tn=2816, tk=1024, vmem_mb=58)
    dhid3, dw3, db3 = _ffn_conv_bwd3(a3, conv_w, conv_b.reshape(1, -1), dp, name=f"ffn{tag}_conv_bwd")
    da3 = _conv_bwd_in3(dhid3, conv_w, K=FFN_CONV, name=f"ffn{tag}_conv_bwd_in")
    g_up = _mm_tn_t(da3, h, norm_w=nw, name=f"ffn{tag}_up_wg", tn=2816, tt=1024, vmem_mb=58)
    dh_out, g_nw = _mm_nt(da3, w_up, epi=(h, nw, dh), name=f"ffn{tag}_up_dg", w_t=True, tm=1024, tk=1408)
    g_cw = jnp.concatenate([dw3[0], dw3[1]], axis=1)
    g_cb = jnp.concatenate([db3[0], db3[1]], axis=1)
    return dh_out, dict(norm=g_nw.reshape(-1), up=g_up, conv_w=g_cw, conv_b=g_cb.reshape(-1), down=g_down)


_BIG = ["ssm_in_w", "ssm_out_w", "w_k", "w_v", "w_q", "w_o", "ffn_up_w", "ffn_down_w"]
_SMALL_SHARDED = ["ssm_norm_w", "ssm_conv_w", "ssm_conv_b", "ssm_gate_norm_w", "ffn_conv_w"]
_SMALL_REPL = ["ssm_dt_bias", "ssm_a_log", "ssm_d", "kv_norm_w", "attn_norm_w", "ffn_norm_w", "ffn_conv_b",
               "final_norm_w"]
_WEIGHTS = ["ssm_norm_w", "ssm_in_w", "ssm_conv_w", "ssm_conv_b", "ssm_dt_bias", "ssm_a_log", "ssm_d",
            "ssm_gate_norm_w", "ssm_out_w", "kv_norm_w", "w_k", "w_v", "attn_norm_w", "w_q", "w_o", "ffn_norm_w",
            "ffn_up_w", "ffn_conv_w", "ffn_conv_b", "ffn_down_w", "final_norm_w"]


def _as2d(a):
    return a.reshape(-1, a.shape[-1])


def _cols_to_full(g):
    return g.transpose(1, 0, 2).reshape(g.shape[1], N_DEV * g.shape[2])


def _pack_small(vals):
    flat = jnp.concatenate([v.reshape(-1).astype(F32) for v in vals])
    n = flat.shape[0]
    rows = -(-n // 1024) * 8
    return jnp.pad(flat, (0, rows * 128 - n)).reshape(rows, 128)


def _unpack_small(packed, shapes):
    flat = packed.reshape(-1)
    out, off = [], 0
    for s in shapes:
        n = math.prod(s)
        out.append(flat[off:off + n].reshape(s))
        off += n
    return out


def _tie(a, token):
    return a + token[0, 0].astype(a.dtype)


def _local_step(x, tgt, get_w, put_g):
    T = x.shape[0]
    Ws = get_w("ssm", None)
    fnw, fcw, fcb = Ws["ffn_norm_w"], Ws["ffn_conv_w"], Ws["ffn_conv_b"]
    zx = _mm_fwd(x, Ws["in_w"], norm_w=Ws["ssm_norm_w"], name="ssm_in", w_t=True, tm=1024, tn=1792)
    xbc_c = _ssm_conv_fwd(zx, Ws["ssm_conv_w"], Ws["ssm_conv_b"].reshape(1, -1), name="ssm_conv")
    dt_raw = zx[:, D_INNER + CONV_DIM:IN_PROJ_DIM]
    dtg = jnp.pad(dt_raw.reshape(T, SSM_GROUPS, 8).transpose(1, 0, 2), ((0, 0), (0, 0), (0, 120)))
    par = jnp.stack([Ws["ssm_dt_bias"].reshape(SSM_GROUPS, 8), Ws["ssm_a_log"].reshape(SSM_GROUPS, 8),
                     Ws["ssm_d"].reshape(SSM_GROUPS, 8)], axis=1)
    par = jnp.pad(par, ((0, 0), (0, 5), (0, 120)))
    gnw = _tie(Ws["ssm_gate_norm_w"].reshape(1, D_INNER), get_w("rest_start", xbc_c))
    y, yn, st = _ssd_fwd(xbc_c, zx, dtg, par, gnw, name="ssd_fwd")
    W0 = get_w("ffn0", y)
    Ws["ssm_out_w"] = W0["ssm_out_w"]
    h1 = _mm_fwd(yn, Ws["ssm_out_w"], residual=x, name="ssm_out", tm=1024, tn=1024)
    h2, ffn0 = _ffn_fwd(h1, fnw[0], W0["up"], fcw[0], fcb[0], W0["down"], "0")
    Wr = get_w("rest", h2)
    q = _mm_fwd(h2, Wr["w_q"], norm_w=Ws["attn_norm_w"], out_dtype=BF16, name="attn_q", tm=1024, tn=1024)
    kv = _mm_fwd(h2, Wr["w_kv"], norm_w=Ws["kv_norm_w"], out_dtype=BF16, name="attn_kv", tm=1024, tn=1024)
    o, lt = _sba_fwd(q, kv, name="sba_fwd")
    h3 = _mm_fwd(o, Wr["w_o"], residual=h2, name="attn_o", tm=1024, tn=1024)
    W1 = get_w("ffn1", h3)
    h4, ffn1 = _ffn_fwd(h3, fnw[1], W1["up"], fcw[1], fcb[1], W1["down"], "1")
    loss, dh4, g_final = _loss_head(h4, tgt, Ws["final_norm_w"], name="loss_head")
    dh3, gf1 = _ffn_bwd(dh4, h3, ffn1, fnw[1], W1["up"], fcw[1], fcb[1], W1["down"], "1")
    tok = put_g("ffn1", dict(up=gf1["up"], down=gf1["down"]))
    g_wo = _mm_tn(o, dh3, name="attn_o_wg", tn=1024, tt=1024)
    do = _mm_nt(dh3, _tie(Wr["w_o"], tok), name="attn_o_dg", out_dtype=BF16, tm=1024, tn=1024, tk=1024)
    dq, dk, dv = _sba_bwd(q, kv, lt, do, name="sba_bwd")
    g_wq = _mm_tn(h2, dq, norm_w=Ws["attn_norm_w"], name="attn_q_wg", tn=1024, tt=1024)
    dh2a, g_attn_nw = _mm_nt(dq, Wr["w_q"], epi=(h2, Ws["attn_norm_w"], dh3), name="attn_q_dg", tm=1024, tk=1024)
    dkv = jnp.concatenate([dk, dv], axis=1)
    g_wkv = _mm_tn(h2, dkv, norm_w=Ws["kv_norm_w"], name="attn_kv_wg", tn=1024, tt=1024)
    dh2, g_kv_nw = _mm_nt(dkv, Wr["w_kv"], epi=(h2, Ws["kv_norm_w"], dh2a), name="attn_kv_dg", tm=1024, tk=1024)
    tok = put_g("attn", dict(w_o=g_wo, w_q=g_wq, w_k=g_wkv[:, :D_MODEL], w_v=g_wkv[:, D_MODEL:]))
    dh1, gf0 = _ffn_bwd(dh2, h1, ffn0, fnw[0], W0["up"], fcw[0], _tie(fcb[0], tok), W0["down"], "0")
    tok = put_g("ffn0", dict(up=gf0["up"], down=gf0["down"]))
    g_out = _mm_tn(yn, dh1, name="ssm_out_wg", tn=1024, tt=1024)
    dyn = _mm_nt(dh1, _tie(Ws["ssm_out_w"], tok), name="ssm_out_dg", out_dtype=BF16, tm=1024, tn=1024, tk=1024)
    tok = put_g("ssm_out", dict(ssm_out_w=g_out))
    dxbc_c, dz, ddt, g_gnw, dpar = _ssd_bwd(xbc_c, zx, dtg, par, _tie(gnw, tok), y, st, dyn, name="ssd_bwd")
    dhid, g_scw, g_scb = _ssm_conv_bwd_pre(zx, Ws["ssm_conv_w"], Ws["ssm_conv_b"].reshape(1, -1), dxbc_c,
                                           name="ssm_conv_bwd")
    dzx = _conv_bwd_in(dhid, Ws["ssm_conv_w"], K=SSM_CONV, name="ssm_conv_bwd_in", into=(dz, D_INNER))
    ddt_t = ddt[:, :, :8].transpose(1, 0, 2).reshape(T, SSM_HEADS).astype(BF16)
    dzx = _put_cols(dzx, jnp.pad(ddt_t, ((0, 0), (0, IN_PROJ_PAD - IN_PROJ_DIM))), D_INNER + CONV_DIM, name="ssm_ddt_cols")
    g_in = _mm_tn_t(dzx, x, norm_w=Ws["ssm_norm_w"], name="ssm_in_wg", tn=1792, tt=1024)
    tok = put_g("ssm_in", dict(ssm_in_w=g_in[:IN_PROJ_DIM]))
    dx, g_ssm_nw = _mm_nt(dzx, Ws["in_w"], epi=(x, _tie(Ws["ssm_norm_w"], tok), dh1), name="ssm_in_dg", w_t=True,
                          tm=1024, tk=1792)
    f = {
        "ssm_norm_w": g_ssm_nw.reshape(-1), "ssm_conv_w": g_scw,
        "ssm_conv_b": g_scb.reshape(-1), "ssm_dt_bias": dpar[:, 0, :8].reshape(-1),
        "ssm_a_log": dpar[:, 1, :8].reshape(-1), "ssm_d": dpar[:, 2, :8].reshape(-1),
        "ssm_gate_norm_w": g_gnw.reshape(-1), "kv_norm_w": g_kv_nw.reshape(-1), "attn_norm_w": g_attn_nw.reshape(-1),
        "ffn_norm_w": jnp.stack([gf0["norm"], gf1["norm"]]), "ffn_conv_w": jnp.stack([gf0["conv_w"], gf1["conv_w"]]),
        "ffn_conv_b": jnp.stack([gf0["conv_b"], gf1["conv_b"]]), "final_norm_w": g_final.reshape(-1),
    }
    return loss, dx, f


def kernel(x, ssm_norm_w, ssm_in_w, ssm_conv_w, ssm_conv_b, ssm_dt_bias, ssm_a_log, ssm_d, ssm_gate_norm_w, ssm_out_w, kv_norm_w, w_k, w_v, attn_norm_w, w_q, w_o, ffn_norm_w, ffn_up_w, ffn_conv_w, ffn_conv_b, ffn_down_w, final_norm_w, loss_target, m_ssm_norm_w, m_ssm_in_w, m_ssm_conv_w, m_ssm_conv_b, m_ssm_dt_bias, m_ssm_a_log, m_ssm_d, m_ssm_gate_norm_w, m_ssm_out_w, m_kv_norm_w, m_w_k, m_w_v, m_attn_norm_w, m_w_q, m_w_o, m_ffn_norm_w, m_ffn_up_w, m_ffn_conv_w, m_ffn_conv_b, m_ffn_down_w, m_final_norm_w, v_ssm_norm_w, v_ssm_in_w, v_ssm_conv_w, v_ssm_conv_b, v_ssm_dt_bias, v_ssm_a_log, v_ssm_d, v_ssm_gate_norm_w, v_ssm_out_w, v_kv_norm_w, v_w_k, v_w_v, v_attn_norm_w, v_w_q, v_w_o, v_ffn_norm_w, v_ffn_up_w, v_ffn_conv_w, v_ffn_conv_b, v_ffn_down_w, v_final_norm_w):
    env = dict(locals())
    p = {n: env[n] for n in _WEIGHTS}
    mom = {n: env["m_" + n] for n in _WEIGHTS}
    var = {n: env["v_" + n] for n in _WEIGHTS}
    T = x.shape[1]
    me = 4 * lax.axis_index("x") + 2 * lax.axis_index("y") + lax.axis_index("c")
    rs = D_FF // N_DEV

    def bf2(a):
        return _as2d(a).astype(BF16)

    _T = ("ssm_in_w", "ffn_up_w")

    def t2d(a):
        return jnp.swapaxes(a, -1, -2).reshape(-1, a.shape[-2])

    def from_t2d(a, like):
        return jnp.swapaxes(a.reshape(like.shape[:-2] + (like.shape[-1], like.shape[-2])), -1, -2)

    n_in, n_up = p["ssm_in_w"].shape[-1], p["ffn_up_w"].shape[-1]

    def with_own(srcs, lands, scatter):
        out = []
        for s, l in zip(srcs, lands):
            own = lax.dynamic_index_in_dim(s, me, 0, keepdims=False) if scatter else s
            out.append(lax.dynamic_update_index_in_dim(l, own, me, 0))
        return out

    a_names = ["ssm_in_w"] + _SMALL_SHARDED
    got_a = dict(zip(a_names, _all_gather([t2d(p["ssm_in_w"]).astype(BF16)] + [_as2d(p[n]) for n in _SMALL_SHARDED],
                                          name="gather_ssm")))
    ffn0_names = ["ssm_out_w", "up0", "down0"]
    rest_names = ["w_q", "w_k", "w_v", "w_o"]
    up_t = jnp.swapaxes(p["ffn_up_w"], -1, -2).astype(BF16)
    shard = {"up0": up_t[0], "down0": bf2(p["ffn_down_w"][0]), "up1": up_t[1],
             "down1": bf2(p["ffn_down_w"][1]), "w_q": bf2(p["w_q"]), "w_k": bf2(p["w_k"]), "w_v": bf2(p["w_v"]),
             "w_o": bf2(p["w_o"]), "ssm_out_w": bf2(p["ssm_out_w"])}

    def anchored(a, on):
        return a + (jnp.where(jnp.isfinite(on), on, 0.0) * 0.0).astype(a.dtype)

    h_ffn0 = _push_start([anchored(shard[ffn0_names[0]], got_a["ssm_norm_w"][0, 0, 0])]
                         + [shard[n] for n in ffn0_names[1:]], scatter=False, name="gather_ffn0_start")
    handles = {}

    def get_w(group, after):
        if group == "ssm":
            W = {n: p[n] for n in _SMALL_REPL}
            for n in ("ssm_dt_bias", "ssm_a_log", "ssm_d", "attn_norm_w"):
                W[n] = W[n].reshape(-1)
            W["in_w"] = jnp.pad(got_a["ssm_in_w"].reshape(IN_PROJ_DIM, D_MODEL), ((0, IN_PROJ_PAD - IN_PROJ_DIM), (0, 0)))
            W["ssm_norm_w"] = _tie(got_a["ssm_norm_w"].reshape(D_MODEL), h_ffn0["token"])
            W["ssm_conv_w"] = _cols_to_full(got_a["ssm_conv_w"])
            W["ssm_conv_b"] = got_a["ssm_conv_b"].reshape(CONV_DIM)
            W["ssm_gate_norm_w"] = got_a["ssm_gate_norm_w"].reshape(D_INNER)
            W["ffn_conv_w"] = _cols_to_full(got_a["ffn_conv_w"]).reshape(2, FFN_CONV, 2 * D_FF)
            return W
        if group == "rest_start":
            handles["rest"] = _push_start([anchored(shard[rest_names[0]], after[0, 0])]
                                          + [shard[n] for n in rest_names[1:]], scatter=False, name="gather_rest_start")
            handles["ffn1"] = _push_start([anchored(shard["up1"], handles["rest"]["token"][0, 0]), shard["down1"]],
                                          scatter=False, name="gather_ffn1_start")
            return handles["ffn1"]["token"]
        if group == "ffn1":
            srcs, lands = _push_wait(handles["ffn1"], after, name="gather_ffn1_wait")
            up, down = with_own(srcs, lands, False)
            return dict(up=up.reshape(2 * D_FF, D_MODEL), down=down.reshape(D_FF, D_MODEL))
        if group == "ffn0":
            srcs, lands = _push_wait(h_ffn0, after, name="gather_ffn0_wait")
            out, up, down = with_own(srcs, lands, False)
            return dict(ssm_out_w=out.reshape(D_INNER, D_MODEL), up=up.reshape(2 * D_FF, D_MODEL),
                        down=down.reshape(D_FF, D_MODEL))
        srcs, lands = _push_wait(handles["rest"], after, name="gather_rest_wait")
        g = dict(zip(rest_names, with_own(srcs, lands, False)))
        sq = lambda a: a.reshape(D_MODEL, D_MODEL)
        return dict(w_q=sq(g["w_q"]), w_kv=jnp.concatenate([sq(g["w_k"]), sq(g["w_v"])], axis=1), w_o=sq(g["w_o"]))

    pending = []

    def put_g(group, g):
        if group in ("ffn0", "ffn1"):
            keys = [("ffn_up_w", int(group[-1])), ("ffn_down_w", int(group[-1]))]
            blocks = [g["up"].reshape(N_DEV, n_up, D_MODEL), g["down"].reshape(N_DEV, rs, D_MODEL)]
        elif group == "attn":
            keys = [(n, None) for n in ("w_o", "w_q", "w_k", "w_v")]
            blocks = [g[n].reshape(N_DEV, D_MODEL // N_DEV, D_MODEL) for n, _ in keys]
        elif group == "ssm_out":
            keys = [("ssm_out_w", None)]
            blocks = [g["ssm_out_w"].reshape(N_DEV, D_INNER // N_DEV, D_MODEL)]
        else:
            keys = [("ssm_in_w", None)]
            blocks = [g["ssm_in_w"].reshape(N_DEV, n_in, D_MODEL)]
        h = _push_start(blocks, scatter=True, name=f"exchange_{group}_start")
        pending.append((group, keys, h))
        return h["token"]

    loss_row, dx, f = _local_step(x.reshape(T, D_MODEL), loss_target.reshape(T, D_MODEL), get_w, put_g)

    small_names = _SMALL_REPL + _SMALL_SHARDED
    small_full = _pack_small([f[n] for n in small_names] + [loss_row[0, 0:1]])
    small_bcast = jnp.broadcast_to(small_full[None], (N_DEV,) + small_full.shape)
    h_small = _push_start([small_bcast], scatter=True, name="exchange_small_start")
    tok = h_small["token"]

    arrived, res = {}, {}
    after = dx
    for group, keys, h in pending:
        srcs, lands = _push_wait(h, after, name=f"exchange_{group}_wait")
        arrived.update(zip(keys, with_own(srcs, lands, True)))
        for n in _BIG:
            layered = (n, 0) in arrived or (n, 1) in arrived
            if n in res or not ((n, None) in arrived or ((n, 0) in arrived and (n, 1) in arrived)):
                continue
            parts = [arrived[(n, 0)], arrived[(n, 1)]] if layered else arrived[(n, None)]
            w2, m2, v2 = ((t2d if n in _T else _as2d)(a[n]) for a in (p, mom, var))
            if not res:
                w2 = _tie(w2, tok)
            tiles = {"ffn_down_w": dict(tr=rs), "ffn_up_w": dict(tr=n_up // 2), "ssm_in_w": dict(tr=n_in, tc=256)}
            res[n] = _adamw(parts, w2, m2, v2, name=f"adamw_{n}", **tiles.get(n, dict(tr=256)))
            after = res[n][0]
    srcs, lands = _push_wait(h_small, after, name="exchange_small_wait")
    small_parts = with_own(srcs, lands, True)[0]
    out_g, out_d, out_m, out_v = {}, {}, {}, {}
    for n in _BIG:
        out_g[n], out_d[n], out_m[n], out_v[n] = (from_t2d(t, p[n]) if n in _T else t.reshape(p[n].shape) for t in res[n])

    zero = jnp.zeros_like(small_full)
    g_small_sum = _adamw(small_parts, zero, zero, zero, name="sum_small_grads", tr=small_full.shape[0])[0]
    *small_sums, loss_sum = _unpack_small(g_small_sum, [f[n].shape for n in small_names] + [(1,)])
    loss = loss_sum[0]
    g_small = dict(zip(small_names, small_sums))
    for n in _SMALL_SHARDED:
        width = p[n].shape[-1]
        g_small[n] = lax.dynamic_slice_in_dim(g_small[n], me * width, width, axis=g_small[n].ndim - 1)
    sw = _pack_small([p[n] for n in small_names])
    sm = _pack_small([mom[n] for n in small_names])
    sv = _pack_small([var[n] for n in small_names])
    sg = _pack_small([g_small[n] for n in small_names])
    _, d, nm, nv = _adamw(sg[None], sw, sm, sv, name="adamw_small", tr=sw.shape[0])
    shard_shapes = [p[n].shape for n in small_names]
    for n, dd, mm, vv in zip(small_names, _unpack_small(d, shard_shapes), _unpack_small(nm, shard_shapes),
                             _unpack_small(nv, shard_shapes)):
        out_g[n] = g_small[n].reshape(p[n].shape)
        out_d[n], out_m[n], out_v[n] = dd, mm, vv

    return (loss, dx.reshape(x.shape), *[out_g[n] for n in _WEIGHTS], *[out_d[n] for n in _WEIGHTS],
            *[out_m[n] for n in _WEIGHTS], *[out_v[n] for n in _WEIGHTS])
```

```python
import functools
import math

import jax
import jax.numpy as jnp
from jax import lax
from jax.experimental import pallas as pl
from jax.experimental.pallas import tpu as pltpu

F32 = jnp.float32
BF16 = jnp.bfloat16
EPS = 1e-6

D_MODEL = 1024
D_INNER = 2048
SSM_HEADS = 32
SSM_GROUPS = 4
SSM_STATE = 128
SSM_CONV = 4
SSM_CHUNK = 128
GN = SSM_GROUPS * SSM_STATE
CONV_DIM = D_INNER + 2 * GN
IN_PROJ_DIM = D_INNER + CONV_DIM + SSM_HEADS
IN_PROJ_PAD = 5376
SB_HEADS = 16
SB_HEAD_DIM = 64
SB_BLOCK = 128
D_FF = 2816
FFN_CONV = 3
N_DEV = 8

ADAM_LR = 0.001
ADAM_B1 = 0.9
ADAM_B2 = 0.999
ADAM_EPS = 1e-08
ADAM_WD = 0.01
ADAM_STEP = 10

_MESH = pl.DeviceIdType.MESH
_NT = (((1,), (1,)), ((), ()))
_TN = (((0,), (0,)), ((), ()))
_ANY = pl.BlockSpec(memory_space=pl.ANY)


def _cparams(sem, vmem_mb=48):
    return pltpu.CompilerParams(dimension_semantics=sem, vmem_limit_bytes=vmem_mb * 1024 * 1024)


def _sigmoid(x):
    return 0.5 * jnp.tanh(0.5 * x) + 0.5


def _softplus(x):
    return jnp.maximum(x, 0.0) + jnp.log(1.0 + jnp.exp(-jnp.abs(x)))


def _rms_fwd(xv, w):
    r = lax.rsqrt(jnp.mean(xv * xv, axis=-1, keepdims=True) + EPS)
    return xv * r * w


def _mm_fwd(x, w, *, name, norm_w=None, residual=None, out_dtype=F32, tm=512, tn=512, halves=False, w_t=False):
    M, K = x.shape
    N = w.shape[0] if w_t else w.shape[1]
    tm, tn = min(tm, M), min(tn, N)
    assert M % tm == 0 and N % tn == 0, (name, M, N, tm, tn)
    if halves:
        nbh = N // 2 // tn
        assert N // 2 % tn == 0
        out_spec = pl.BlockSpec((None, tm, tn), lambda i, j: (lax.div(j, nbh), i, lax.rem(j, nbh)))
        out_shape = jax.ShapeDtypeStruct((2, M, N // 2), out_dtype)
    else:
        out_spec = pl.BlockSpec((tm, tn), lambda i, j: (i, j))
        out_shape = jax.ShapeDtypeStruct((M, N), out_dtype)
    has_norm, has_res = norm_w is not None, residual is not None

    def body(*refs):
        x_ref, w_ref = refs[0], refs[1]
        p = 2
        nw_ref = r_ref = None
        if has_norm:
            nw_ref = refs[p]
            p += 1
        if has_res:
            r_ref = refs[p]
            p += 1
        o_ref = refs[p]
        xv = x_ref[...]
        if has_norm:
            xv = _rms_fwd(xv.astype(F32), nw_ref[...])
        acc = lax.dot_general(xv.astype(BF16), w_ref[...], _NT if w_t else (((1,), (0,)), ((), ())),
                              preferred_element_type=F32)
        if has_res:
            acc = acc + r_ref[...]
        o_ref[...] = acc.astype(out_dtype)

    w_spec = pl.BlockSpec((tn, K), lambda i, j: (j, 0)) if w_t else pl.BlockSpec((K, tn), lambda i, j: (0, j))
    in_specs = [pl.BlockSpec((tm, K), lambda i, j: (i, 0)), w_spec]
    args = [x, w]
    if has_norm:
        in_specs.append(pl.BlockSpec((1, K), lambda i, j: (0, 0)))
        args.append(norm_w.reshape(1, K))
    if has_res:
        in_specs.append(pl.BlockSpec((tm, tn), lambda i, j: (i, j)))
        args.append(residual)
    return pl.pallas_call(
        body, name=name, grid=(M // tm, N // tn), in_specs=in_specs,
        out_specs=out_spec, out_shape=out_shape,
        compiler_params=_cparams(("parallel", "parallel")))(*args)


def _mm_nt(dy, w, *, name, epi=None, out_dtype=F32, tm=512, tn=512, tk=512, w_t=False):
    halves = dy.ndim == 3
    M, K = (dy.shape[1], 2 * dy.shape[2]) if halves else dy.shape
    N = w.shape[1] if w_t else w.shape[0]
    tm, tk = min(tm, M), min(tk, K)
    tn = N if epi is not None else min(tn, N)
    assert M % tm == 0 and N % tn == 0 and K % tk == 0, (name, M, N, K, tm, tn, tk)
    nk = K // tk
    has_epi = epi is not None

    def body(*refs):
        if has_epi:
            dy_ref, w_ref, h_ref, nw_ref, r_ref, o_ref, dnw_ref, acc_ref = refs
        else:
            dy_ref, w_ref, o_ref, acc_ref = refs
        i = pl.program_id(0)
        k = pl.program_id(2)

        @pl.when(k == 0)
        def _():
            acc_ref[...] = jnp.zeros_like(acc_ref)

        acc_ref[...] += lax.dot_general(dy_ref[...].astype(BF16), w_ref[...], (((1,), (0,)), ((), ())) if w_t else _NT,
                                        preferred_element_type=F32)

        @pl.when(k == nk - 1)
        def _():
            du = acc_ref[...]
            if has_epi:
                hv = h_ref[...]
                r = lax.rsqrt(jnp.mean(hv * hv, axis=-1, keepdims=True) + EPS)
                xhat = hv * r
                dxh = du * nw_ref[...]
                dx = r * (dxh - xhat * jnp.mean(dxh * xhat, axis=-1, keepdims=True))
                o_ref[...] = (r_ref[...] + dx).astype(out_dtype)
                contrib = jnp.sum(du * xhat, axis=0, keepdims=True)

                @pl.when(i == 0)
                def _():
                    dnw_ref[...] = contrib

                @pl.when(i > 0)
                def _():
                    dnw_ref[...] += contrib
            else:
                o_ref[...] = du.astype(out_dtype)

    if halves:
        nkh = K // 2 // tk
        assert K // 2 % tk == 0
        dy_spec = pl.BlockSpec((None, tm, tk), lambda i, j, k: (lax.div(k, nkh), i, lax.rem(k, nkh)))
    else:
        dy_spec = pl.BlockSpec((tm, tk), lambda i, j, k: (i, k))
    w_spec = pl.BlockSpec((tk, tn), lambda i, j, k: (k, j)) if w_t else pl.BlockSpec((tn, tk), lambda i, j, k: (j, k))
    in_specs = [dy_spec, w_spec]
    args = [dy, w]
    out_specs = [pl.BlockSpec((tm, tn), lambda i, j, k: (i, j))]
    out_shape = [jax.ShapeDtypeStruct((M, N), out_dtype)]
    if has_epi:
        h, nw, res = epi
        in_specs += [pl.BlockSpec((tm, N), lambda i, j, k: (i, 0)), pl.BlockSpec((1, N), lambda i, j, k: (0, 0)),
                     pl.BlockSpec((tm, N), lambda i, j, k: (i, 0))]
        args += [h, nw.reshape(1, N), res]
        out_specs.append(pl.BlockSpec((1, N), lambda i, j, k: (0, 0)))
        out_shape.append(jax.ShapeDtypeStruct((1, N), F32))
    outs = pl.pallas_call(
        body, name=name, grid=(M // tm, N // tn, nk), in_specs=in_specs, out_specs=out_specs, out_shape=out_shape,
        scratch_shapes=[pltpu.VMEM((tm, tn), F32)],
        compiler_params=_cparams(("arbitrary", "arbitrary", "arbitrary")))(*args)
    return (outs[0], outs[1]) if has_epi else outs[0]


def _mm_tn(x, dy, *, name, norm_w=None, out_dtype=BF16, tk1=1024, tn=512, tt=512):
    T, K1 = x.shape
    halves = dy.ndim == 3
    N = 2 * dy.shape[2] if halves else dy.shape[1]
    tk1, tn, tt = min(tk1, K1), min(tn, N), min(tt, T)
    has_norm = norm_w is not None
    assert K1 % tk1 == 0 and N % tn == 0 and T % tt == 0, (name, K1, N, T, tk1, tn, tt)
    assert not has_norm or tk1 == K1
    nt = T // tt

    def body(*refs):
        if has_norm:
            x_ref, dy_ref, nw_ref, o_ref, acc_ref = refs
        else:
            x_ref, dy_ref, o_ref, acc_ref = refs
        t = pl.program_id(2)

        @pl.when(t == 0)
        def _():
            acc_ref[...] = jnp.zeros_like(acc_ref)

        xv = x_ref[...]
        if has_norm:
            xv = _rms_fwd(xv.astype(F32), nw_ref[...])
        acc_ref[...] += lax.dot_general(xv.astype(BF16), dy_ref[...].astype(BF16), _TN, preferred_element_type=F32)

        @pl.when(t == nt - 1)
        def _():
            o_ref[...] = acc_ref[...].astype(out_dtype)

    if halves:
        nbh = N // 2 // tn
        assert N // 2 % tn == 0
        dy_spec = pl.BlockSpec((None, tt, tn), lambda a, b, t: (lax.div(b, nbh), t, lax.rem(b, nbh)))
    else:
        dy_spec = pl.BlockSpec((tt, tn), lambda a, b, t: (t, b))
    in_specs = [pl.BlockSpec((tt, tk1), lambda a, b, t: (t, a)), dy_spec]
    args = [x, dy]
    if has_norm:
        in_specs.append(pl.BlockSpec((1, K1), lambda a, b, t: (0, 0)))
        args.append(norm_w.reshape(1, K1))
    return pl.pallas_call(
        body, name=name, grid=(K1 // tk1, N // tn, nt), in_specs=in_specs,
        out_specs=pl.BlockSpec((tk1, tn), lambda a, b, t: (a, b)),
        out_shape=jax.ShapeDtypeStruct((K1, N), out_dtype),
        scratch_shapes=[pltpu.VMEM((tk1, tn), F32)],
        compiler_params=_cparams(("parallel", "parallel", "arbitrary")))(*args)


def _mm_tn_t(dy, x, *, name, norm_w, out_dtype=BF16, tn=1408, tt=1024, vmem_mb=48):
    T, K1 = x.shape
    halves = dy.ndim == 3
    N = 2 * dy.shape[2] if halves else dy.shape[1]
    tn, tt = min(tn, N), min(tt, T)
    assert N % tn == 0 and T % tt == 0, (name, N, T, tn, tt)
    nt = T // tt

    def body(dy_ref, x_ref, nw_ref, o_ref, acc_ref):
        t = pl.program_id(1)

        @pl.when(t == 0)
        def _():
            acc_ref[...] = jnp.zeros_like(acc_ref)

        xn = _rms_fwd(x_ref[...].astype(F32), nw_ref[...]).astype(BF16)
        acc_ref[...] += lax.dot_general(dy_ref[...].astype(BF16), xn, _TN, preferred_element_type=F32)

        @pl.when(t == nt - 1)
        def _():
            o_ref[...] = acc_ref[...].astype(out_dtype)

    if halves:
        nbh = N // 2 // tn
        assert N // 2 % tn == 0
        dy_spec = pl.BlockSpec((None, tt, tn), lambda b, t: (lax.div(b, nbh), t, lax.rem(b, nbh)))
    else:
        dy_spec = pl.BlockSpec((tt, tn), lambda b, t: (t, b))
    return pl.pallas_call(
        body, name=name, grid=(N // tn, nt),
        in_specs=[dy_spec, pl.BlockSpec((tt, K1), lambda b, t: (t, 0)), pl.BlockSpec((1, K1), lambda b, t: (0, 0))],
        out_specs=pl.BlockSpec((tn, K1), lambda b, t: (b, 0)),
        out_shape=jax.ShapeDtypeStruct((N, K1), out_dtype),
        scratch_shapes=[pltpu.VMEM((tn, K1), F32)],
        compiler_params=_cparams(("parallel", "arbitrary"), vmem_mb))(dy, x, norm_w.reshape(1, K1))


def _shift_down(xb, prev8, j):
    main = pltpu.roll(xb, j, 0)
    head = pltpu.roll(xb[0:8], j, 0)
    ph = pltpu.roll(prev8, j, 0)
    row8 = lax.broadcasted_iota(jnp.int32, head.shape, 0)
    head = jnp.where(row8 < j, ph, head)
    return jnp.concatenate([head, main[8:]], axis=0)


def _shift_up(xb, next8, j):
    tt = xb.shape[0]
    main = pltpu.roll(xb, tt - j, 0)
    tail = pltpu.roll(xb[tt - 8:tt], 8 - j, 0)
    nh = pltpu.roll(next8, 8 - j, 0)
    row8 = lax.broadcasted_iota(jnp.int32, tail.shape, 0)
    tail = jnp.where(row8 + j >= 8, nh, tail)
    return jnp.concatenate([main[:tt - 8], tail], axis=0)


def _conv_hid(xb, prev8, w, b_row, K):
    out = b_row
    shifted = []
    for j in range(K):
        sh = K - 1 - j
        xs = xb if sh == 0 else _shift_down(xb, prev8, sh)
        shifted.append(xs)
        out = out + xs * w[j:j + 1, :]
    return out, shifted


def _prev_idx(i, nb8):
    return jnp.maximum(i * nb8 - 1, 0)


def _ssm_conv_fwd(zx, w, b, *, name, tt=512, tc=512):
    T = zx.shape[0]
    tt = min(tt, T)
    C, K = CONV_DIM, SSM_CONV
    cb0, nb8 = D_INNER // tc, tt // 8

    def body(x_ref, p_ref, w_ref, b_ref, o_ref):
        first = (pl.program_id(1) > 0).astype(F32)
        hid, _ = _conv_hid(x_ref[...], p_ref[...] * first, w_ref[...], b_ref[...], K)
        o_ref[...] = hid * _sigmoid(hid)

    return pl.pallas_call(
        body, name=name, grid=(C // tc, T // tt),
        in_specs=[pl.BlockSpec((tt, tc), lambda c, i: (i, c + cb0)),
                  pl.BlockSpec((8, tc), lambda c, i: (_prev_idx(i, nb8), c + cb0)),
                  pl.BlockSpec((K, tc), lambda c, i: (0, c)), pl.BlockSpec((1, tc), lambda c, i: (0, c))],
        out_specs=pl.BlockSpec((tt, tc), lambda c, i: (i, c)),
        out_shape=jax.ShapeDtypeStruct((T, C), F32),
        compiler_params=_cparams(("parallel", "parallel")))(zx, zx, w, b)


def _ssm_conv_bwd_pre(zx, w, b, dout, *, name, tt=512, tc=512):
    T = zx.shape[0]
    tt = min(tt, T)
    C, K = CONV_DIM, SSM_CONV
    cb0, nb8 = D_INNER // tc, tt // 8

    def body(x_ref, p_ref, w_ref, b_ref, d_ref, dh_ref, dw_ref, db_ref):
        t = pl.program_id(1)
        first = (t > 0).astype(F32)
        hid, shifted = _conv_hid(x_ref[...], p_ref[...] * first, w_ref[...], b_ref[...], K)
        sg = _sigmoid(hid)
        dh = d_ref[...] * (sg * (1.0 + hid * (1.0 - sg)))
        dh_ref[...] = dh

        @pl.when(t == 0)
        def _():
            dw_ref[...] = jnp.zeros_like(dw_ref)
            db_ref[...] = jnp.zeros_like(db_ref)

        db_ref[...] += jnp.sum(dh, axis=0, keepdims=True)
        for j in range(K):
            dw_ref[j:j + 1, :] += jnp.sum(dh * shifted[j], axis=0, keepdims=True)

    return pl.pallas_call(
        body, name=name, grid=(C // tc, T // tt),
        in_specs=[pl.BlockSpec((tt, tc), lambda c, i: (i, c + cb0)),
                  pl.BlockSpec((8, tc), lambda c, i: (_prev_idx(i, nb8), c + cb0)),
                  pl.BlockSpec((K, tc), lambda c, i: (0, c)), pl.BlockSpec((1, tc), lambda c, i: (0, c)),
                  pl.BlockSpec((tt, tc), lambda c, i: (i, c))],
        out_specs=[pl.BlockSpec((tt, tc), lambda c, i: (i, c)), pl.BlockSpec((K, tc), lambda c, i: (0, c)),
                   pl.BlockSpec((1, tc), lambda c, i: (0, c))],
        out_shape=[jax.ShapeDtypeStruct((T, C), F32), jax.ShapeDtypeStruct((K, C), F32),
                   jax.ShapeDtypeStruct((1, C), F32)],
        compiler_params=_cparams(("parallel", "arbitrary")))(zx, zx, w, b, dout)


def _put_cols(buf, src, col0, *, name, tt=512):
    T, C = src.shape
    tt = min(tt, T)

    def body(s_ref, _, o_ref):
        o_ref[...] = s_ref[...]

    return pl.pallas_call(
        body, name=name, grid=(T // tt,),
        in_specs=[pl.BlockSpec((tt, C), lambda i: (i, 0)), _ANY],
        out_specs=pl.BlockSpec((tt, C), lambda i: (i, col0 // C)),
        out_shape=jax.ShapeDtypeStruct(buf.shape, buf.dtype), input_output_aliases={1: 0},
        compiler_params=_cparams(("parallel",)))(src, buf)


def _conv_bwd_in(dh, w, *, name, K, tt=512, tc=512, out_dtype=BF16, into=None):
    T, C = dh.shape
    tt = min(tt, T)
    nb8, nT = tt // 8, T // tt
    last8 = T // 8 - 1
    cb0 = 0 if into is None else into[1] // tc

    def body(d_ref, n_ref, w_ref, *rest):
        o_ref = rest[-1]
        notlast = (pl.program_id(1) < nT - 1).astype(F32)
        d = d_ref[...]
        nxt = n_ref[...] * notlast
        w_ = w_ref[...]
        acc = d * w_[K - 1:K, :]
        for sh in range(1, K):
            acc = acc + _shift_up(d, nxt, sh) * w_[K - 1 - sh:K - sh, :]
        o_ref[...] = acc.astype(out_dtype)

    in_specs = [pl.BlockSpec((tt, tc), lambda c, i: (i, c)),
                pl.BlockSpec((8, tc), lambda c, i: (jnp.minimum((i + 1) * nb8, last8), c)),
                pl.BlockSpec((K, tc), lambda c, i: (0, c))]
    args = [dh, dh, w]
    if into is None:
        out_shape, alias = jax.ShapeDtypeStruct((T, C), out_dtype), {}
    else:
        assert into[0].dtype == out_dtype and into[1] % tc == 0
        in_specs.append(_ANY)
        args.append(into[0])
        out_shape, alias = jax.ShapeDtypeStruct(into[0].shape, out_dtype), {3: 0}
    return pl.pallas_call(
        body, name=name, grid=(C // tc, nT), in_specs=in_specs,
        out_specs=pl.BlockSpec((tt, tc), lambda c, i: (i, c + cb0)),
        out_shape=out_shape, input_output_aliases=alias,
        compiler_params=_cparams(("parallel", "parallel")))(*args)


def _ffn_conv_fwd3(a3, w, b, *, name, tt=256, tc=1408):
    T = a3.shape[1]
    tt = min(tt, T)
    K, nbh, n16 = FFN_CONV, D_FF // tc, tt // 16

    def body(a_ref, p_ref, wg_ref, wv_ref, bg_ref, bv_ref, o_ref):
        first = (pl.program_id(1) > 0).astype(F32)
        a = a_ref[...].astype(F32)
        prev = p_ref[...].astype(F32)[:, 8:16, :] * first
        hg, _ = _conv_hid(a[0], prev[0], wg_ref[...], bg_ref[...], K)
        hv, _ = _conv_hid(a[1], prev[1], wv_ref[...], bv_ref[...], K)
        o_ref[...] = (hg * _sigmoid(hg) * hv).astype(BF16)

    return pl.pallas_call(
        body, name=name, grid=(nbh, T // tt),
        in_specs=[pl.BlockSpec((2, tt, tc), lambda c, i: (0, i, c)),
                  pl.BlockSpec((2, 16, tc), lambda c, i: (0, _prev_idx(i, n16), c)),
                  pl.BlockSpec((K, tc), lambda c, i: (0, c)), pl.BlockSpec((K, tc), lambda c, i: (0, c + nbh)),
                  pl.BlockSpec((1, tc), lambda c, i: (0, c)), pl.BlockSpec((1, tc), lambda c, i: (0, c + nbh))],
        out_specs=pl.BlockSpec((tt, tc), lambda c, i: (i, c)),
        out_shape=jax.ShapeDtypeStruct((T, D_FF), BF16),
        compiler_params=_cparams(("parallel", "parallel")))(a3, a3, w, w, b, b)


def _ffn_conv_bwd3(a3, w, b, dp, *, name, tt=256, tc=1408):
    T = a3.shape[1]
    tt = min(tt, T)
    K, nbh, n16 = FFN_CONV, D_FF // tc, tt // 16

    def body(a_ref, p_ref, wg_ref, wv_ref, bg_ref, bv_ref, dp_ref, dh_ref, dw_ref, db_ref):
        t = pl.program_id(1)
        first = (t > 0).astype(F32)
        a = a_ref[...].astype(F32)
        prev = p_ref[...].astype(F32)[:, 8:16, :] * first
        hg, sh_g = _conv_hid(a[0], prev[0], wg_ref[...], bg_ref[...], K)
        hv, sh_v = _conv_hid(a[1], prev[1], wv_ref[...], bv_ref[...], K)
        sg = _sigmoid(hg)
        d = dp_ref[...].astype(F32)
        dhg = d * hv * (sg * (1.0 + hg * (1.0 - sg)))
        dhv = d * (hg * sg)
        dh_ref[0] = dhg.astype(BF16)
        dh_ref[1] = dhv.astype(BF16)

        @pl.when(t == 0)
        def _():
            dw_ref[...] = jnp.zeros_like(dw_ref)
            db_ref[...] = jnp.zeros_like(db_ref)

        db_ref[0] += jnp.sum(dhg, axis=0, keepdims=True)
        db_ref[1] += jnp.sum(dhv, axis=0, keepdims=True)
        for j in range(K):
            dw_ref[0, j:j + 1, :] += jnp.sum(dhg * sh_g[j], axis=0, keepdims=True)
            dw_ref[1, j:j + 1, :] += jnp.sum(dhv * sh_v[j], axis=0, keepdims=True)

    return pl.pallas_call(
        body, name=name, grid=(nbh, T // tt),
        in_specs=[pl.BlockSpec((2, tt, tc), lambda c, i: (0, i, c)),
                  pl.BlockSpec((2, 16, tc), lambda c, i: (0, _prev_idx(i, n16), c)),
                  pl.BlockSpec((K, tc), lambda c, i: (0, c)), pl.BlockSpec((K, tc), lambda c, i: (0, c + nbh)),
                  pl.BlockSpec((1, tc), lambda c, i: (0, c)), pl.BlockSpec((1, tc), lambda c, i: (0, c + nbh)),
                  pl.BlockSpec((tt, tc), lambda c, i: (i, c))],
        out_specs=[pl.BlockSpec((2, tt, tc), lambda c, i: (0, i, c)), pl.BlockSpec((2, K, tc), lambda c, i: (0, 0, c)),
                   pl.BlockSpec((2, 1, tc), lambda c, i: (0, 0, c))],
        out_shape=[jax.ShapeDtypeStruct((2, T, D_FF), BF16), jax.ShapeDtypeStruct((2, K, D_FF), F32),
                   jax.ShapeDtypeStruct((2, 1, D_FF), F32)],
        compiler_params=_cparams(("parallel", "arbitrary")))(a3, a3, w, w, b, b, dp)


def _conv_bwd_in3(dh3, w, *, name, K, tt=256, tc=1408):
    H, T, C = dh3.shape
    tt = min(tt, T)
    nb, n16, nT = C // tc, tt // 16, T // tt
    last16 = T // 16 - 1

    def body(d_ref, n_ref, w_ref, o_ref):
        notlast = (pl.program_id(2) < nT - 1).astype(F32)
        d = d_ref[...].astype(F32)
        nxt = n_ref[...].astype(F32)[0:8, :] * notlast
        w_ = w_ref[...]
        acc = d * w_[K - 1:K, :]
        for sh in range(1, K):
            acc = acc + _shift_up(d, nxt, sh) * w_[K - 1 - sh:K - sh, :]
        o_ref[...] = acc.astype(BF16)

    return pl.pallas_call(
        body, name=name, grid=(H, nb, nT),
        in_specs=[pl.BlockSpec((None, tt, tc), lambda h, c, i: (h, i, c)),
                  pl.BlockSpec((None, 16, tc), lambda h, c, i: (h, jnp.minimum((i + 1) * n16, last16), c)),
                  pl.BlockSpec((K, tc), lambda h, c, i: (0, h * nb + c))],
        out_specs=pl.BlockSpec((None, tt, tc), lambda h, c, i: (h, i, c)),
        out_shape=jax.ShapeDtypeStruct((H, T, C), BF16),
        compiler_params=_cparams(("parallel", "parallel", "parallel")))(dh3, dh3, w)


def _cumsum_rows(x):
    L = x.shape[0]
    row = lax.broadcasted_iota(jnp.int32, x.shape, 0)
    k = 1
    while k < L:
        x = x + jnp.where(row >= k, pltpu.roll(x, k, 0), 0.0)
        k *= 2
    return x


def _rcumsum_rows(x):
    L = x.shape[0]
    row = lax.broadcasted_iota(jnp.int32, x.shape, 0)
    k = 1
    while k < L:
        x = x + jnp.where(row < L - k, pltpu.roll(x, L - k, 0), 0.0)
        k *= 2
    return x


def _split_terms(m, n):
    terms, rest = [], m
    for _ in range(n):
        t = rest.astype(BF16)
        terms.append(t)
        rest = rest - t.astype(F32)
    return jnp.concatenate(terms, axis=1)


def _select_dot(m, n_terms, n_out, cond):
    K = m.shape[1]
    k = lax.broadcasted_iota(jnp.int32, (K, n_out), 0)
    j = lax.broadcasted_iota(jnp.int32, (K, n_out), 1)
    sel = cond(k, j).astype(BF16)
    return jnp.dot(_split_terms(m, n_terms), jnp.concatenate([sel] * n_terms, axis=0), preferred_element_type=F32)


def _rowsum_mxu(m):
    return _select_dot(m, 2, 128, lambda k, j: k >= 0)


def _lane_block_sums(m, width):
    shift = width.bit_length() - 1
    return _select_dot(m, 2, 128, lambda k, j: j == jnp.right_shift(k, shift))


def _heads_to_pairs(m):
    return _select_dot(m, 3, 512, lambda k, j: k == jnp.right_shift(j, 6))


def _ssd_common(dt_ref, par_ref):
    par = par_ref[...]
    raw = dt_ref[...] + par[0:1, :]
    dt = _softplus(raw)
    a = -jnp.exp(par[1:2, :])
    cs = _cumsum_rows(dt * a)
    L = cs.shape[0]
    cs_last = cs[L - 1:L, :]
    return raw, dt, a, par[2:3, :], cs, cs.T, jnp.exp(cs), jnp.exp(cs_last - cs), jnp.exp(cs_last)


def _ssd_specs(nc, rev):
    L = SSM_CHUNK

    def ci(c):
        return nc - 1 - c if rev else c

    return [pl.BlockSpec((L, D_INNER), lambda c: (ci(c), 0)),
            pl.BlockSpec((L, GN), lambda c: (ci(c), D_INNER // GN)),
            pl.BlockSpec((L, GN), lambda c: (ci(c), D_INNER // GN + 1)),
            pl.BlockSpec((SSM_GROUPS, L, 128), lambda c: (0, ci(c), 0)),
            pl.BlockSpec((SSM_GROUPS, 8, 128), lambda c: (0, 0, 0)),
            pl.BlockSpec((L, D_INNER), lambda c: (ci(c), 0)),
            pl.BlockSpec((1, D_INNER), lambda c: (0, 0))], ci


def _round_robin(gens):
    live = list(gens)
    while live:
        nxt = []
        for gen in live:
            try:
                next(gen)
                nxt.append(gen)
            except StopIteration:
                pass
        live = nxt


def _group_views(g, wide, narrow, lead):
    return ([r.at[:, g * 512:(g + 1) * 512] for r in wide], [r.at[:, g * 128:(g + 1) * 128] for r in narrow],
            [r.at[g] for r in lead])


def _ssd_fwd(xbc_c, zx, dtg, par, gnw, *, name):
    T = xbc_c.shape[0]
    L = SSM_CHUNK
    nc = T // L
    in_specs, ci = _ssd_specs(nc, False)

    def body(xs_ref, b_ref, c_ref, dt_ref, par_ref, z_ref, gnw_ref, y_ref, yn_ref, st_ref, h_ref):
        @pl.when(pl.program_id(0) == 0)
        def _():
            h_ref[...] = jnp.zeros_like(h_ref)

        gens = []
        for g in range(SSM_GROUPS):
            (xs, z, gw, y, yn), (b, c), (dt, pr, st, h) = _group_views(
                g, [xs_ref, z_ref, gnw_ref, y_ref, yn_ref], [b_ref, c_ref], [dt_ref, par_ref, st_ref, h_ref])
            gens.append(group(xs, b, c, dt, pr, z, gw, y, yn, st, h))
        _round_robin(gens)

    def group(xs_ref, b_ref, c_ref, dt_ref, par_ref, z_ref, gnw_ref, y_ref, yn_ref, st_ref, h_ref):
        _, dt, _, dsk, cs, csT, ecs, eend, dec = _ssd_common(dt_ref, par_ref)
        Bb = b_ref[...].astype(BF16)
        Cb = c_ref[...].astype(BF16)
        G = lax.dot_general(Cb, Bb, _NT, preferred_element_type=F32)
        row = lax.broadcasted_iota(jnp.int32, (L, L), 0)
        col = lax.broadcasted_iota(jnp.int32, (L, L), 1)
        tril = col <= row
        lo = lax.broadcasted_iota(jnp.int32, (L, 128), 1) < 64
        lo1 = lax.broadcasted_iota(jnp.int32, (1, 128), 1) < 64
        dt_x, ecs_x, eend_x = (_heads_to_pairs(m) for m in (dt, ecs, eend))
        for pp in range(4):
            hA, hB = 2 * pp, 2 * pp + 1
            lanes = slice(pp * 128, (pp + 1) * 128)

            def sel1(m):
                return jnp.where(lo1, m[:, hA:hA + 1], m[:, hB:hB + 1])

            X = xs_ref[:, lanes]
            xd = X * dt_x[:, lanes]
            xdb = xd.astype(BF16)
            ys = []
            for h in (hA, hB):
                Lm = jnp.where(tril, jnp.exp(jnp.minimum(cs[:, h:h + 1] - csT[h:h + 1, :], 0.0)), 0.0)
                ys.append(jnp.dot((G * Lm).astype(BF16), xdb, preferred_element_type=F32))
                yield
            Hp = h_ref[pp]
            st_ref[pp] = Hp
            yoff = jnp.dot(Cb, Hp.astype(BF16), preferred_element_type=F32) * ecs_x[:, lanes]
            y_ref[:, lanes] = jnp.where(lo, ys[0], ys[1]) + yoff + sel1(dsk) * X
            S = lax.dot_general(Bb, (xd * eend_x[:, lanes]).astype(BF16), _TN, preferred_element_type=F32)
            h_ref[pp] = Hp * sel1(dec) + S
            yield
        zv = z_ref[...]
        yg = y_ref[...] * (zv * _sigmoid(zv))
        r = jnp.tile(lax.rsqrt(_rowsum_mxu(yg * yg) * (1.0 / 512) + EPS), (1, 4))
        yn_ref[...] = (yg * r * gnw_ref[...]).astype(BF16)

    return pl.pallas_call(
        body, name=name, grid=(nc,), in_specs=in_specs,
        out_specs=[pl.BlockSpec((L, D_INNER), lambda c: (c, 0)), pl.BlockSpec((L, D_INNER), lambda c: (c, 0)),
                   pl.BlockSpec((SSM_GROUPS, None, 4, 128, 128), lambda c: (0, c, 0, 0, 0))],
        out_shape=[jax.ShapeDtypeStruct((T, D_INNER), F32), jax.ShapeDtypeStruct((T, D_INNER), BF16),
                   jax.ShapeDtypeStruct((SSM_GROUPS, nc, 4, 128, 128), F32)],
        scratch_shapes=[pltpu.VMEM((SSM_GROUPS, 4, 128, 128), F32)],
        compiler_params=_cparams(("arbitrary",)))(xbc_c, xbc_c, xbc_c, dtg, par, zx, gnw)


def _ssd_bwd(xbc_c, zx, dtg, par, gnw, y, st, dyn, *, name):
    T = xbc_c.shape[0]
    L = SSM_CHUNK
    nc = T // L
    in_specs, ci = _ssd_specs(nc, True)
    in_specs += [pl.BlockSpec((L, D_INNER), lambda c: (ci(c), 0)),
                 pl.BlockSpec((SSM_GROUPS, None, 4, 128, 128), lambda c: (0, ci(c), 0, 0, 0)),
                 pl.BlockSpec((L, D_INNER), lambda c: (ci(c), 0))]

    def body(xs_ref, b_ref, c_ref, dt_ref, par_ref, z_ref, gnw_ref, y_ref, st_ref, dyn_ref,
             dxbc_ref, dz_ref, ddt_ref, dgnw_ref, dpar_ref, dh_ref):
        @pl.when(pl.program_id(0) == 0)
        def _():
            dh_ref[...] = jnp.zeros_like(dh_ref)
            dgnw_ref[...] = jnp.zeros_like(dgnw_ref)
            dpar_ref[...] = jnp.zeros_like(dpar_ref)

        dxs_ref = dxbc_ref.at[:, 0:D_INNER]
        db_ref = dxbc_ref.at[:, D_INNER:D_INNER + GN]
        dc_ref = dxbc_ref.at[:, D_INNER + GN:CONV_DIM]

        gens = []
        for g in range(SSM_GROUPS):
            (xs, z, gw, y, dyn, dxs, dz, dgw), (b, c, db, dc), (dt, pr, st, ddt, dpr, dh) = _group_views(
                g, [xs_ref, z_ref, gnw_ref, y_ref, dyn_ref, dxs_ref, dz_ref, dgnw_ref], [b_ref, c_ref, db_ref, dc_ref],
                [dt_ref, par_ref, st_ref, ddt_ref, dpar_ref, dh_ref])
            gens.append(group(xs, b, c, dt, pr, z, gw, y, st, dyn, dxs, db, dc, dz, ddt, dgw, dpr, dh))
        _round_robin(gens)

    def group(xs_ref, b_ref, c_ref, dt_ref, par_ref, z_ref, gnw_ref, y_ref, st_ref, dyn_ref,
              dxs_ref, db_ref, dc_ref, dz_ref, ddt_ref, dgnw_ref, dpar_ref, dh_ref):
        yv = y_ref[...]
        zv = z_ref[...]
        sg = _sigmoid(zv)
        sz = zv * sg
        yg = yv * sz
        r = jnp.tile(lax.rsqrt(_rowsum_mxu(yg * yg) * (1.0 / 512) + EPS), (1, 4))
        yh = yg * r
        dyn = dyn_ref[...].astype(F32)
        dgnw_ref[...] += jnp.sum(dyn * yh, axis=0, keepdims=True)
        dyh = dyn * gnw_ref[...]
        dyg = r * (dyh - yh * jnp.tile(_rowsum_mxu(dyh * yh) * (1.0 / 512), (1, 4)))
        dY_all = dyg * sz
        dz_ref[...] = (dyg * yv * (sg * (1.0 + zv * (1.0 - sg)))).astype(dz_ref.dtype)

        yield
        raw, dt, a, dsk, cs, csT, ecs, eend, dec = _ssd_common(dt_ref, par_ref)
        Bb = b_ref[...].astype(BF16)
        Cb = c_ref[...].astype(BF16)
        G = lax.dot_general(Cb, Bb, _NT, preferred_element_type=F32)
        row = lax.broadcasted_iota(jnp.int32, (L, L), 0)
        col = lax.broadcasted_iota(jnp.int32, (L, L), 1)
        tril = col <= row
        lo = lax.broadcasted_iota(jnp.int32, (L, 128), 1) < 64
        lane1 = lax.broadcasted_iota(jnp.int32, (1, 128), 1)
        lo1 = lane1 < 64
        rowl = lax.broadcasted_iota(jnp.int32, (L, 128), 0)
        dt_x, ecs_x, eend_x = (_heads_to_pairs(m) for m in (dt, ecs, eend))
        dG = jnp.zeros((L, L), F32)
        dB = jnp.zeros((L, SSM_STATE), F32)
        dC = jnp.zeros((L, SSM_STATE), F32)
        dcs_t = jnp.zeros((L, L), F32)
        tails = jnp.zeros((1, 128), F32)
        dD_row = jnp.zeros((1, 128), F32)
        v_parts, prod_parts = [], []

        def tot(m):
            return jnp.sum(jnp.sum(m, axis=0, keepdims=True), axis=1, keepdims=True)

        for pp in range(4):
            hA, hB = 2 * pp, 2 * pp + 1
            lanes = slice(pp * 128, (pp + 1) * 128)

            def sel1(m):
                return jnp.where(lo1, m[:, hA:hA + 1], m[:, hB:hB + 1])

            X = xs_ref[:, lanes]
            dY = dY_all[:, lanes]
            dtsel = dt_x[:, lanes]
            xd = X * dtsel
            xdb = xd.astype(BF16)
            dYb = dY.astype(BF16)
            Hp = st_ref[pp]
            Hb = Hp.astype(BF16)
            dHn = dh_ref[pp]
            dHb = dHn.astype(BF16)
            ecs_sel = ecs_x[:, lanes]
            eend_sel = eend_x[:, lanes]
            dxd_state = jnp.dot(Bb, dHb, preferred_element_type=F32) * eend_sel
            yoff = jnp.dot(Cb, Hb, preferred_element_type=F32) * ecs_sel
            dYe = (dY * ecs_sel).astype(BF16)
            dC = dC + lax.dot_general(dYe, Hb, _NT, preferred_element_type=F32)
            dB = dB + lax.dot_general((xd * eend_sel).astype(BF16), dHb, _NT, preferred_element_type=F32)
            dh_ref[pp] = dHn * sel1(dec) + lax.dot_general(Cb, dYe, _TN, preferred_element_type=F32)
            q = xd * dxd_state
            dyq = dY * yoff - q
            qcol = jnp.sum(q, axis=0, keepdims=True)
            hcol = jnp.sum(dHn * Hp, axis=0, keepdims=True)
            dxd_diag = []
            for h, msk, msk1 in ((hA, lo, lo1), (hB, jnp.logical_not(lo), jnp.logical_not(lo1))):
                Lm = jnp.where(tril, jnp.exp(jnp.minimum(cs[:, h:h + 1] - csT[h:h + 1, :], 0.0)), 0.0)
                M = G * Lm
                dxd_diag.append(lax.dot_general(M.astype(BF16), dYb, _TN, preferred_element_type=F32))
                dM = lax.dot_general(jnp.where(msk, dY, 0.0).astype(BF16), xdb, _NT, preferred_element_type=F32)
                dG = dG + dM * Lm
                W = dM * M
                dcs_t = dcs_t + jnp.where(row == h, jnp.sum(W, axis=0, keepdims=True), 0.0)
                v_parts.append(W + jnp.where(msk, dyq, 0.0))
                tail = (jnp.sum(jnp.where(msk1, qcol, 0.0), axis=1, keepdims=True)
                        + dec[:, h:h + 1] * jnp.sum(jnp.where(msk1, hcol, 0.0), axis=1, keepdims=True))
                tails = tails + jnp.where(lane1 == h, tail, 0.0)
                yield
            dxd = jnp.where(lo, dxd_diag[0], dxd_diag[1]) + dxd_state
            prod_parts.append(dxd * X)
            dxs_ref[:, lanes] = dxd * dtsel + sel1(dsk) * dY
            dyx = jnp.sum(dY * X, axis=0, keepdims=True)
            sA = jnp.sum(jnp.where(lo1, dyx, 0.0), axis=1, keepdims=True)
            sB = jnp.sum(dyx, axis=1, keepdims=True) - sA
            dD_row = dD_row + jnp.where(lane1 == hA, sA, 0.0) + jnp.where(lane1 == hB, sB, 0.0)
            yield
        dGb = dG.astype(BF16)
        db_ref[...] = dB + lax.dot_general(dGb, Cb, _TN, preferred_element_type=F32)
        dc_ref[...] = dC + jnp.dot(dGb, Bb, preferred_element_type=F32)
        dcs_mat = _lane_block_sums(jnp.concatenate(v_parts, axis=1), 128) + jnp.where(rowl == L - 1, tails, 0.0)
        ddt_mat = _lane_block_sums(jnp.concatenate(prod_parts, axis=1), 64)
        dad = _rcumsum_rows(dcs_mat - dcs_t.T)
        draw = (a * dad + ddt_mat) * _sigmoid(raw)
        ddt_ref[...] = draw
        dpar_ref[0:1, :] += jnp.sum(draw, axis=0, keepdims=True)
        dpar_ref[1:2, :] += jnp.sum(dt * dad, axis=0, keepdims=True) * a
        dpar_ref[2:3, :] += dD_row

    return pl.pallas_call(
        body, name=name, grid=(nc,), in_specs=in_specs,
        out_specs=[pl.BlockSpec((L, CONV_DIM), lambda c: (ci(c), 0)),
                   pl.BlockSpec((L, D_INNER), lambda c: (ci(c), 0)),
                   pl.BlockSpec((SSM_GROUPS, L, 128), lambda c: (0, ci(c), 0)),
                   pl.BlockSpec((1, D_INNER), lambda c: (0, 0)),
                   pl.BlockSpec((SSM_GROUPS, 8, 128), lambda c: (0, 0, 0))],
        out_shape=[jax.ShapeDtypeStruct((T, CONV_DIM), F32), jax.ShapeDtypeStruct((T, IN_PROJ_PAD), BF16),
                   jax.ShapeDtypeStruct((SSM_GROUPS, T, 128), F32), jax.ShapeDtypeStruct((1, D_INNER), F32),
                   jax.ShapeDtypeStruct((SSM_GROUPS, 8, 128), F32)],
        scratch_shapes=[pltpu.VMEM((SSM_GROUPS, 4, 128, 128), F32)],
        compiler_params=_cparams(("arbitrary",)))(xbc_c, xbc_c, xbc_c, dtg, par, zx, gnw, y, st, dyn)


SB_KEYS = 512
SB_SCAN = 256
SB_STRIP = 256


def _tri(width, cond):
    kk = lax.broadcasted_iota(jnp.int32, (width, width), 0)
    jj = lax.broadcasted_iota(jnp.int32, (width, width), 1)
    return cond(kk, jj).astype(BF16)


_LOG2E = 1.4426950408889634


def _softplus2(z2):
    return jnp.maximum(z2, 0.0) + jnp.log2(1.0 + jnp.exp2(-jnp.abs(z2)))


def _sba_sub_fwd(zb, c, U, mask):
    z2 = zb * _LOG2E
    s = _softplus2(z2)
    if mask is not None:
        s = jnp.where(mask, s, 0.0)
    R = c + jnp.dot(s.astype(BF16), U, preferred_element_type=F32)
    A = jnp.exp2(z2 - s - R)
    if mask is not None:
        A = jnp.where(mask, A, 0.0)
    return A.astype(BF16), R[:, 0:1] + s[:, 0:1]


def _sba_sub_bwd(zb, dAb, Lt, pc, pe, Uincl, Uexcl, mask):
    last = zb.shape[1] - 1
    z2 = zb * _LOG2E
    s = _softplus2(z2)
    g = z2 - s
    if mask is not None:
        s = jnp.where(mask, s, 0.0)
    P = pc + jnp.dot(s.astype(BF16), Uincl, preferred_element_type=F32)
    A = jnp.exp2(g - (Lt - P))
    if mask is not None:
        A = jnp.where(mask, A, 0.0)
    E = dAb * A
    PE = pe + jnp.dot(E.astype(BF16), Uexcl, preferred_element_type=F32)
    dz = E - jnp.exp2(g) * (E + PE)
    if mask is not None:
        dz = jnp.where(mask, dz, 0.0)
    return (A.astype(BF16), dz.astype(BF16), P[:, last:last + 1], PE[:, last:last + 1] + E[:, last:last + 1])


def _stack_heads(v):
    lo = lax.broadcasted_iota(jnp.int32, v.shape, 1) < 64
    zero = jnp.zeros_like(v)
    return jnp.concatenate([jnp.where(lo, v, zero), jnp.where(lo, zero, v)], axis=0)


def _unstack_heads(v):
    lo = lax.broadcasted_iota(jnp.int32, (SB_BLOCK, 128), 1) < 64
    return jnp.where(lo, v[:SB_BLOCK], v[SB_BLOCK:])


def _sba_rows(a):
    return slice(2 * a * SB_BLOCK, 2 * (a + 1) * SB_BLOCK)


def _sba_diag_case(a, b):
    Bq = SB_BLOCK
    if b * SB_SCAN >= (a + 1) * Bq:
        return "skip"
    if (b + 1) * SB_SCAN <= a * Bq:
        return "full"
    rowi = lax.broadcasted_iota(jnp.int32, (2 * Bq, SB_SCAN), 0)
    qpos = a * Bq + jnp.where(rowi >= Bq, rowi - Bq, rowi)
    return b * SB_SCAN + lax.broadcasted_iota(jnp.int32, (2 * Bq, SB_SCAN), 1) < qpos


def _sba_fwd(q, kv, *, name):
    T = q.shape[0]
    Bq = SB_BLOCK
    nsub = SB_KEYS // Bq
    nscan = SB_KEYS // SB_SCAN
    R = 2 * SB_KEYS
    assert T % SB_KEYS == 0 and SB_STRIP == 2 * Bq
    scale = 1.0 / math.sqrt(SB_HEAD_DIM)

    def body(q_ref, k_ref, v_ref, o_ref, lt_ref, z_s, a_s, c_s, acc_s):
        i = pl.program_id(1)
        U2 = _tri(SB_SCAN, lambda k, j: k > j)
        qs_all = jnp.concatenate([_stack_heads(q_ref[a * Bq:(a + 1) * Bq, :] * scale) for a in range(nsub)], axis=0)
        c_s[...] = jnp.zeros_like(c_s)
        acc_s[...] = jnp.zeros_like(acc_s)

        def scores(J, slot):
            off = pl.multiple_of(J * SB_KEYS, SB_KEYS)
            z_s[slot] = lax.dot_general(qs_all, k_ref[pl.ds(off, SB_KEYS), :], _NT, preferred_element_type=F32)

        def weights(slot, diag):
            for a in range(nsub):
                rows = _sba_rows(a)
                c = c_s[rows, :]
                for b in reversed(range(nscan)):
                    cols = slice(b * SB_SCAN, (b + 1) * SB_SCAN)
                    case = _sba_diag_case(a, b) if diag else "full"
                    if isinstance(case, str) and case == "skip":
                        a_s[slot, rows, cols] = jnp.zeros((2 * Bq, SB_SCAN), BF16)
                        continue
                    A, c = _sba_sub_fwd(z_s[slot, rows, cols], c, U2, None if isinstance(case, str) else case)
                    a_s[slot, rows, cols] = A
                c_s[rows, :] = c

        def values(J, slot):
            off = pl.multiple_of(J * SB_KEYS, SB_KEYS)
            acc_s[...] += jnp.dot(a_s[slot], v_ref[pl.ds(off, SB_KEYS), :], preferred_element_type=F32)

        scores(i, 0)
        weights(0, True)
        scores(jnp.maximum(i - 1, 0), 1)

        def two_steps(u, _):
            t = 2 * u + 1
            weights(1, False)
            scores(jnp.maximum(i - t - 1, 0), 0)
            values(i - t + 1, 0)
            weights(0, False)
            scores(jnp.maximum(i - t - 2, 0), 1)
            values(i - t, 1)
            return 0

        lax.fori_loop(0, i // 2, two_steps, 0)
        odd = lax.rem(i, 2) == 1

        @pl.when(jnp.logical_not(odd))
        def _():
            values(0, 0)

        @pl.when(odd)
        def _():
            weights(1, False)
            values(1, 0)
            values(0, 1)
        for a in range(nsub):
            o_ref[a * Bq:(a + 1) * Bq, :] = _unstack_heads(acc_s[_sba_rows(a), :]).astype(BF16)
            lt_ref[a * Bq:(a + 1) * Bq, :] = _unstack_heads(jnp.broadcast_to(c_s[_sba_rows(a), :], (2 * Bq, 128)))

    return pl.pallas_call(
        body, name=name, grid=(SB_HEADS // 2, T // SB_KEYS),
        in_specs=[pl.BlockSpec((SB_KEYS, 128), lambda p, i: (i, p)), pl.BlockSpec((T, 128), lambda p, i: (0, p)),
                  pl.BlockSpec((T, 128), lambda p, i: (0, p + SB_HEADS // 2))],
        out_specs=[pl.BlockSpec((SB_KEYS, 128), lambda p, i: (i, p)),
                   pl.BlockSpec((None, SB_KEYS, 128), lambda p, i: (p, i, 0))],
        out_shape=[jax.ShapeDtypeStruct((T, D_MODEL), BF16), jax.ShapeDtypeStruct((SB_HEADS // 2, T, 128), F32)],
        scratch_shapes=[pltpu.VMEM((2, R, SB_KEYS), F32), pltpu.VMEM((2, R, SB_KEYS), BF16),
                        pltpu.VMEM((R, 1), F32), pltpu.VMEM((R, 128), F32)],
        compiler_params=_cparams(("parallel", "parallel")))(q, kv, kv)


def _sba_bwd(q, kv, lt, do, *, name):
    T = q.shape[0]
    Bq = SB_BLOCK
    nq = T // SB_KEYS
    nsub = SB_KEYS // Bq
    nscan = SB_KEYS // SB_SCAN
    R = 2 * SB_KEYS
    assert T % SB_KEYS == 0 and SB_STRIP == 2 * Bq
    scale = 1.0 / math.sqrt(SB_HEAD_DIM)

    def body(q_ref, k_ref, v_ref, lt_ref, do_ref, dq_ref, dk_ref, dv_ref, dk_acc, dv_acc,
             z_s, da_s, a_s, dz_s, pc_s, pe_s, lt_s, dq_s):
        i = pl.program_id(1)

        @pl.when(i == 0)
        def _():
            dk_acc[...] = jnp.zeros_like(dk_acc)
            dv_acc[...] = jnp.zeros_like(dv_acc)

        Uincl = _tri(SB_SCAN, lambda k, j: k <= j)
        Uexcl = _tri(SB_SCAN, lambda k, j: k < j)
        qs, dos = [], []
        for a in range(nsub):
            rows = slice(a * Bq, (a + 1) * Bq)
            qs.append(_stack_heads(q_ref[rows, :] * scale))
            dos.append(_stack_heads(do_ref[rows, :]))
            lt_s[_sba_rows(a), :] = jnp.concatenate([lt_ref[rows, 0:1], lt_ref[rows, 64:65]], axis=0)
        qs_all = jnp.concatenate(qs, axis=0)
        dos_all = jnp.concatenate(dos, axis=0)
        pc_s[...] = jnp.zeros_like(pc_s)
        pe_s[...] = jnp.zeros_like(pe_s)
        a_s[1] = jnp.zeros((R, SB_KEYS), BF16)
        dz_s[1] = jnp.zeros((R, SB_KEYS), BF16)

        def scores(J, slot):
            off = pl.multiple_of(J * SB_KEYS, SB_KEYS)
            z_s[slot] = lax.dot_general(qs_all, k_ref[pl.ds(off, SB_KEYS), :], _NT, preferred_element_type=F32)
            da_s[slot] = lax.dot_general(dos_all, v_ref[pl.ds(off, SB_KEYS), :], _NT, preferred_element_type=F32)

        def gradients(slot, diag):
            for a in range(nsub):
                rows = _sba_rows(a)
                pc, pe, Lt = pc_s[rows, :], pe_s[rows, :], lt_s[rows, :]
                for b in range(nscan):
                    cols = slice(b * SB_SCAN, (b + 1) * SB_SCAN)
                    case = _sba_diag_case(a, b) if diag else "full"
                    if isinstance(case, str) and case == "skip":
                        a_s[slot, rows, cols] = jnp.zeros((2 * Bq, SB_SCAN), BF16)
                        dz_s[slot, rows, cols] = jnp.zeros((2 * Bq, SB_SCAN), BF16)
                        continue
                    A, dz, pc, pe = _sba_sub_bwd(z_s[slot, rows, cols], da_s[slot, rows, cols], Lt, pc, pe, Uincl, Uexcl,
                                                 None if isinstance(case, str) else case)
                    a_s[slot, rows, cols] = A
                    dz_s[slot, rows, cols] = dz
                pc_s[rows, :] = pc
                pe_s[rows, :] = pe

        def products(J, slot):
            off = pl.multiple_of(J * SB_KEYS, SB_KEYS)
            dzt = dz_s[slot]
            dk_acc[pl.ds(off, SB_KEYS), :] += lax.dot_general(dzt, qs_all, _TN, preferred_element_type=F32)
            dv_acc[pl.ds(off, SB_KEYS), :] += lax.dot_general(a_s[slot], dos_all, _TN, preferred_element_type=F32)
            dq_s[...] += jnp.dot(dzt, k_ref[pl.ds(off, SB_KEYS), :], preferred_element_type=F32)

        dq_s[...] = jnp.zeros_like(dq_s)
        scores(0, 0)

        def two_steps(u, _):
            t = 2 * u
            gradients(0, False)
            scores(t + 1, 1)
            products(jnp.maximum(t - 1, 0), 1)
            gradients(1, False)
            scores(t + 2, 0)
            products(t, 0)
            return 0

        lax.fori_loop(0, i // 2, two_steps, 0)
        odd = lax.rem(i, 2) == 1

        @pl.when(jnp.logical_not(odd))
        def _():
            gradients(0, True)
            products(jnp.maximum(i - 1, 0), 1)
            products(i, 0)

        @pl.when(odd)
        def _():
            gradients(0, False)
            scores(i, 1)
            products(jnp.maximum(i - 2, 0), 1)
            gradients(1, True)
            products(i - 1, 0)
            products(i, 1)

        for a in range(nsub):
            dq_ref[a * Bq:(a + 1) * Bq, :] = (_unstack_heads(dq_s[_sba_rows(a), :]) * scale).astype(BF16)

        @pl.when(i == nq - 1)
        def _():
            dk_ref[...] = dk_acc[...].astype(BF16)
            dv_ref[...] = dv_acc[...].astype(BF16)

    return pl.pallas_call(
        body, name=name, grid=(SB_HEADS // 2, nq),
        in_specs=[pl.BlockSpec((SB_KEYS, 128), lambda p, i: (i, p)), pl.BlockSpec((T, 128), lambda p, i: (0, p)),
                  pl.BlockSpec((T, 128), lambda p, i: (0, p + SB_HEADS // 2)),
                  pl.BlockSpec((None, SB_KEYS, 128), lambda p, i: (p, i, 0)),
                  pl.BlockSpec((SB_KEYS, 128), lambda p, i: (i, p))],
        out_specs=[pl.BlockSpec((SB_KEYS, 128), lambda p, i: (i, p)), pl.BlockSpec((T, 128), lambda p, i: (0, p)),
                   pl.BlockSpec((T, 128), lambda p, i: (0, p))],
        out_shape=[jax.ShapeDtypeStruct((T, D_MODEL), BF16), jax.ShapeDtypeStruct((T, D_MODEL), BF16),
                   jax.ShapeDtypeStruct((T, D_MODEL), BF16)],
        scratch_shapes=[pltpu.VMEM((T, 128), F32), pltpu.VMEM((T, 128), F32),
                        pltpu.VMEM((2, R, SB_KEYS), F32), pltpu.VMEM((2, R, SB_KEYS), F32),
                        pltpu.VMEM((2, R, SB_KEYS), BF16), pltpu.VMEM((2, R, SB_KEYS), BF16),
                        pltpu.VMEM((R, 1), F32), pltpu.VMEM((R, 1), F32), pltpu.VMEM((R, 1), F32),
                        pltpu.VMEM((R, 128), F32)],
        compiler_params=_cparams(("parallel", "arbitrary")))(q, kv, kv, lt, do)


def _loss_head(h, tgt, w, *, name, tt=512):
    T, D = h.shape
    tt = min(tt, T)

    def body(h_ref, t_ref, w_ref, loss_ref, dh_ref, dw_ref):
        i = pl.program_id(0)
        hv = h_ref[...]
        wv = w_ref[...]
        r = lax.rsqrt(jnp.mean(hv * hv, axis=-1, keepdims=True) + EPS)
        xhat = hv * r
        err = xhat * wv - t_ref[...]
        part = 0.5 * jnp.sum(jnp.mean(err * err, axis=-1, keepdims=True), axis=0, keepdims=True)
        dy = err * (1.0 / D)
        dxh = dy * wv
        dh_ref[...] = r * (dxh - xhat * jnp.mean(dxh * xhat, axis=-1, keepdims=True))
        dwc = jnp.sum(dy * xhat, axis=0, keepdims=True)

        @pl.when(i == 0)
        def _():
            loss_ref[...] = jnp.broadcast_to(part, loss_ref.shape)
            dw_ref[...] = dwc

        @pl.when(i > 0)
        def _():
            loss_ref[...] += jnp.broadcast_to(part, loss_ref.shape)
            dw_ref[...] += dwc

    return pl.pallas_call(
        body, name=name, grid=(T // tt,),
        in_specs=[pl.BlockSpec((tt, D), lambda i: (i, 0)), pl.BlockSpec((tt, D), lambda i: (i, 0)),
                  pl.BlockSpec((1, D), lambda i: (0, 0))],
        out_specs=[pl.BlockSpec((1, 128), lambda i: (0, 0)), pl.BlockSpec((tt, D), lambda i: (i, 0)),
                   pl.BlockSpec((1, D), lambda i: (0, 0))],
        out_shape=[jax.ShapeDtypeStruct((1, 128), F32), jax.ShapeDtypeStruct((T, D), F32),
                   jax.ShapeDtypeStruct((1, D), F32)],
        compiler_params=_cparams(("arbitrary",)))(h, tgt, w.reshape(1, D))


def _adamw(parts, w, m, v, *, name, tr=256, tc=None):
    plist = list(parts) if isinstance(parts, (list, tuple)) else [parts]
    P, _, C = plist[0].shape
    R = sum(a.shape[1] for a in plist)
    tr = min(tr, R)
    tc = C if tc is None else tc
    assert all(a.shape[1] % tr == 0 for a in plist) and C % tc == 0, (name, R, C, tr, tc)
    nbs = [a.shape[1] // tr for a in plist]
    offs = [sum(nbs[:l]) for l in range(len(nbs))]
    c1 = 1.0 - ADAM_B1 ** ADAM_STEP
    c2 = 1.0 - ADAM_B2 ** ADAM_STEP

    def body(*refs):
        p_refs = refs[:len(plist)]
        w_ref, m_ref, v_ref, g_ref, d_ref, nm_ref, nv_ref = refs[len(plist):]
        i = pl.program_id(0)
        g = None
        for l, p_ref in enumerate(p_refs):
            gl = p_ref[0].astype(F32)
            for k in range(1, P):
                gl = gl + p_ref[k].astype(F32)
            g = gl if g is None else jnp.where(i >= offs[l], gl, g)
        mn = ADAM_B1 * m_ref[...] + (1.0 - ADAM_B1) * g
        vn = ADAM_B2 * v_ref[...] + (1.0 - ADAM_B2) * (g * g)
        g_ref[...] = g
        nm_ref[...] = mn
        nv_ref[...] = vn
        d_ref[...] = -ADAM_LR * ((mn / c1) / (jnp.sqrt(vn / c2) + ADAM_EPS) + ADAM_WD * w_ref[...])

    spec = pl.BlockSpec((tr, tc), lambda i, j: (i, j))
    sds = jax.ShapeDtypeStruct((R, C), F32)
    return pl.pallas_call(
        body, name=name, grid=(R // tr, C // tc),
        in_specs=[pl.BlockSpec((P, tr, tc), functools.partial(lambda i, j, o, n: (0, jnp.clip(i - o, 0, n - 1), j), o=o, n=n))
                  for o, n in zip(offs, nbs)] + [spec, spec, spec],
        out_specs=[spec, spec, spec, spec], out_shape=[sds, sds, sds, sds],
        compiler_params=_cparams(("parallel", "parallel")))(*plist, w, m, v)


def _all_gather(shards, *, name):
    n = len(shards)

    def body(*refs):
        ins, outs = refs[:n], refs[n:2 * n]
        send_sems, recv_sems, local_sems = refs[2 * n:]
        x, y, c = lax.axis_index("x"), lax.axis_index("y"), lax.axis_index("c")
        me, sib = (x, y, c), (x, y, 1 - c)
        chips = [(1 - x, y), (x, 1 - y), (1 - x, 1 - y)]

        def slot(p):
            return 4 * p[0] + 2 * p[1] + p[2]

        def cp(a, k, block, to, src=None):
            dst = outs[a].at[slot(block)]
            return pltpu.make_async_remote_copy(src_ref=dst if src is None else src, dst_ref=dst,
                                                send_sem=send_sems.at[a, k], recv_sem=recv_sems.at[a, k],
                                                device_id=to, device_id_type=_MESH)

        mine = [pltpu.make_async_copy(ins[a], outs[a].at[slot(me)], local_sems.at[a]) for a in range(n)]
        for m in mine:
            m.start()
        first = []
        for a in range(n):
            first.append(cp(a, 0, me, sib, src=ins[a]))
            for j, chip in enumerate(chips):
                first.append(cp(a, 1 + j, me, (*chip, c), src=ins[a]))
        for f in first:
            f.start()
        passed = []
        for j, chip in enumerate(chips):
            for a in range(n):
                cp(a, 1 + j, (*chip, c), me).wait_recv()
                f = cp(a, 4 + j, (*chip, c), sib)
                f.start()
                passed.append(f)
        for a in range(n):
            cp(a, 0, sib, me).wait_recv()
            for j, chip in enumerate(chips):
                cp(a, 4 + j, (*chip, 1 - c), me).wait_recv()
        for f in first + passed:
            f.wait_send()
        for m in mine:
            m.wait()

    return pl.pallas_call(
        body, name=name, in_specs=[_ANY] * n, out_specs=[_ANY] * n,
        out_shape=[jax.ShapeDtypeStruct((N_DEV,) + s.shape, s.dtype) for s in shards],
        scratch_shapes=[pltpu.SemaphoreType.DMA((n, 7)), pltpu.SemaphoreType.DMA((n, 7)),
                        pltpu.SemaphoreType.DMA((n,))])(*shards)


_HBM = pl.BlockSpec(memory_space=pltpu.HBM)
_SEM = pl.BlockSpec(memory_space=pltpu.SEMAPHORE)
_EFFECT = pltpu.SideEffectType.DATAFLOW_SIDE_EFFECTING


def _peers():
    x, y, c = lax.axis_index("x"), lax.axis_index("y"), lax.axis_index("c")
    out = []
    for r in range(1, N_DEV):
        px = 1 - x if (r >> 2) & 1 else x
        py = 1 - y if (r >> 1) & 1 else y
        pc = 1 - c if r & 1 else c
        out.append(((px, py, pc), 4 * px + 2 * py + pc))
    return 4 * x + 2 * y + c, out


def _push_copy(src_ref, land_ref, send_sems, recv_sems, a, k, me, peer, peer_slot, scatter, arriving):
    src = src_ref.at[peer_slot] if scatter else src_ref
    return pltpu.make_async_remote_copy(
        src_ref=src, dst_ref=land_ref.at[peer_slot if arriving else me], send_sem=send_sems.at[a * (N_DEV - 1) + k],
        recv_sem=recv_sems.at[a * (N_DEV - 1) + k], device_id=peer, device_id_type=_MESH)


def _push_start(srcs, *, scatter, name):
    n = len(srcs)
    lands = [lax.empty(s.shape if scatter else (N_DEV,) + s.shape, s.dtype) for s in srcs]

    def body(*refs):
        src_refs, land_refs = refs[:n], refs[n:2 * n]
        send_sems, recv_sems = refs[2 * n], refs[2 * n + 1]
        token = refs[-1]
        me, peers = _peers()
        for k, (peer, slot) in enumerate(peers):
            for a in range(n):
                _push_copy(src_refs[a], land_refs[a], send_sems, recv_sems, a, k, me, peer, slot, scatter, False).start()
        token[...] = jnp.zeros_like(token)

    hbm = lambda a: pltpu.HBM(a.shape, a.dtype)
    outs = pl.pallas_call(
        body, name=name,
        out_shape=(pltpu.SemaphoreType.DMA((n * (N_DEV - 1),)), pltpu.SemaphoreType.DMA((n * (N_DEV - 1),)),
                   *[hbm(s) for s in srcs], *[hbm(l) for l in lands], jax.ShapeDtypeStruct((8, 128), F32)),
        in_specs=[_HBM] * (2 * n),
        out_specs=(_SEM, _SEM, *([_HBM] * (2 * n)), pl.BlockSpec(memory_space=pltpu.VMEM)),
        input_output_aliases={i: 2 + i for i in range(2 * n)},
        compiler_params=pltpu.CompilerParams(has_side_effects=_EFFECT),
    )(*[pltpu.with_memory_space_constraint(s, pltpu.HBM) for s in srcs],
      *[pltpu.with_memory_space_constraint(l, pltpu.HBM) for l in lands])
    return dict(send=outs[0], recv=outs[1], srcs=list(outs[2:2 + n]), lands=list(outs[2 + n:2 + 2 * n]),
                token=outs[-1], scatter=scatter, n=n)


def _push_wait(h, after, *, name):
    n, scatter = h["n"], h["scatter"]

    def body(*refs):
        src_refs, land_refs = refs[:n], refs[n:2 * n]
        send_sems, recv_sems = refs[2 * n], refs[2 * n + 1]
        me, peers = _peers()
        for k, (peer, slot) in enumerate(peers):
            for a in range(n):
                cp = _push_copy(src_refs[a], land_refs[a], send_sems, recv_sems, a, k, me, peer, slot, scatter, True)
                cp.wait_send()
                cp.wait_recv()

    hbm = lambda a: pltpu.HBM(a.shape, a.dtype)
    outs = pl.pallas_call(
        body, name=name,
        out_shape=(*[hbm(s) for s in h["srcs"]], *[hbm(l) for l in h["lands"]]),
        in_specs=[_HBM] * (2 * n) + [_SEM, _SEM, _ANY], out_specs=tuple([_HBM] * (2 * n)),
        input_output_aliases={i: i for i in range(2 * n)},
        compiler_params=pltpu.CompilerParams(has_side_effects=_EFFECT),
    )(*h["srcs"], *h["lands"], h["send"], h["recv"], after)
    return list(outs[:n]), list(outs[n:])


def _ffn_fwd(h, nw, w_up, conv_w, conv_b, w_down, tag):
    a3 = _mm_fwd(h, w_up, norm_w=nw, name=f"ffn{tag}_up", out_dtype=BF16, halves=True, w_t=True, tm=1024, tn=2816)
    p = _ffn_conv_fwd3(a3, conv_w, conv_b.reshape(1, -1), name=f"ffn{tag}_conv")
    h_out = _mm_fwd(p, w_down, residual=h, name=f"ffn{tag}_down", tm=1024, tn=1024)
    return h_out, (a3, p)


def _ffn_bwd(dh, h, saved, nw, w_up, conv_w, conv_b, w_down, tag):
    a3, p = saved
    g_down = _mm_tn(p, dh, name=f"ffn{tag}_down_wg", tk1=1408, tn=1024, tt=1024)
    dp = _mm_nt(dh, w_down, name=f"ffn{tag}_down_dg", out_dtype=BF16, tm=512, tn=2816, tk=1024)
    dhid3, dw3, db3 = _ffn_conv_bwd3(a3, conv_w, conv_b.reshape(1, -1), dp, name=f"ffn{tag}_conv_bwd")
    da3 = _conv_bwd_in3(dhid3, conv_w, K=FFN_CONV, name=f"ffn{tag}_conv_bwd_in")
    g_up = _mm_tn_t(da3, h, norm_w=nw, name=f"ffn{tag}_up_wg", tn=2816, tt=1024, vmem_mb=58)
    dh_out, g_nw = _mm_nt(da3, w_up, epi=(h, nw, dh), name=f"ffn{tag}_up_dg", w_t=True, tm=1024, tk=1408)
    g_cw = jnp.concatenate([dw3[0], dw3[1]], axis=1)
    g_cb = jnp.concatenate([db3[0], db3[1]], axis=1)
    return dh_out, dict(norm=g_nw.reshape(-1), up=g_up, conv_w=g_cw, conv_b=g_cb.reshape(-1), down=g_down)


_BIG = ["ssm_in_w", "ssm_out_w", "w_k", "w_v", "w_q", "w_o", "ffn_up_w", "ffn_down_w"]
_SMALL_SHARDED = ["ssm_norm_w", "ssm_conv_w", "ssm_conv_b", "ssm_gate_norm_w", "ffn_conv_w"]
_SMALL_REPL = ["ssm_dt_bias", "ssm_a_log", "ssm_d", "kv_norm_w", "attn_norm_w", "ffn_norm_w", "ffn_conv_b",
               "final_norm_w"]
_WEIGHTS = ["ssm_norm_w", "ssm_in_w", "ssm_conv_w", "ssm_conv_b", "ssm_dt_bias", "ssm_a_log", "ssm_d",
            "ssm_gate_norm_w", "ssm_out_w", "kv_norm_w", "w_k", "w_v", "attn_norm_w", "w_q", "w_o", "ffn_norm_w",
            "ffn_up_w", "ffn_conv_w", "ffn_conv_b", "ffn_down_w", "final_norm_w"]


def _as2d(a):
    return a.reshape(-1, a.shape[-1])


def _cols_to_full(g):
    return g.transpose(1, 0, 2).reshape(g.shape[1], N_DEV * g.shape[2])


def _pack_small(vals):
    flat = jnp.concatenate([v.reshape(-1).astype(F32) for v in vals])
    n = flat.shape[0]
    rows = -(-n // 1024) * 8
    return jnp.pad(flat, (0, rows * 128 - n)).reshape(rows, 128)


def _unpack_small(packed, shapes):
    flat = packed.reshape(-1)
    out, off = [], 0
    for s in shapes:
        n = math.prod(s)
        out.append(flat[off:off + n].reshape(s))
        off += n
    return out


def _tie(a, token):
    return a + token[0, 0].astype(a.dtype)


def _local_step(x, tgt, get_w, put_g):
    T = x.shape[0]
    Ws = get_w("ssm", None)
    fnw, fcw, fcb = Ws["ffn_norm_w"], Ws["ffn_conv_w"], Ws["ffn_conv_b"]
    zx = _mm_fwd(x, Ws["in_w"], norm_w=Ws["ssm_norm_w"], name="ssm_in", w_t=True, tm=1024, tn=1792)
    xbc_c = _ssm_conv_fwd(zx, Ws["ssm_conv_w"], Ws["ssm_conv_b"].reshape(1, -1), name="ssm_conv")
    dt_raw = zx[:, D_INNER + CONV_DIM:IN_PROJ_DIM]
    dtg = jnp.pad(dt_raw.reshape(T, SSM_GROUPS, 8).transpose(1, 0, 2), ((0, 0), (0, 0), (0, 120)))
    par = jnp.stack([Ws["ssm_dt_bias"].reshape(SSM_GROUPS, 8), Ws["ssm_a_log"].reshape(SSM_GROUPS, 8),
                     Ws["ssm_d"].reshape(SSM_GROUPS, 8)], axis=1)
    par = jnp.pad(par, ((0, 0), (0, 5), (0, 120)))
    gnw = _tie(Ws["ssm_gate_norm_w"].reshape(1, D_INNER), get_w("rest_start", xbc_c))
    y, yn, st = _ssd_fwd(xbc_c, zx, dtg, par, gnw, name="ssd_fwd")
    W0 = get_w("ffn0", y)
    Ws["ssm_out_w"] = W0["ssm_out_w"]
    h1 = _mm_fwd(yn, Ws["ssm_out_w"], residual=x, name="ssm_out", tm=1024, tn=1024)
    h2, ffn0 = _ffn_fwd(h1, fnw[0], W0["up"], fcw[0], fcb[0], W0["down"], "0")
    Wr = get_w("rest", h2)
    q = _mm_fwd(h2, Wr["w_q"], norm_w=Ws["attn_norm_w"], out_dtype=BF16, name="attn_q", tm=1024, tn=1024)
    kv = _mm_fwd(h2, Wr["w_kv"], norm_w=Ws["kv_norm_w"], out_dtype=BF16, name="attn_kv", tm=1024, tn=1024)
    o, lt = _sba_fwd(q, kv, name="sba_fwd")
    h3 = _mm_fwd(o, Wr["w_o"], residual=h2, name="attn_o", tm=1024, tn=1024)
    W1 = get_w("ffn1", h3)
    h4, ffn1 = _ffn_fwd(h3, fnw[1], W1["up"], fcw[1], fcb[1], W1["down"], "1")
    loss, dh4, g_final = _loss_head(h4, tgt, Ws["final_norm_w"], name="loss_head")
    dh3, gf1 = _ffn_bwd(dh4, h3, ffn1, fnw[1], W1["up"], fcw[1], fcb[1], W1["down"], "1")
    tok = put_g("ffn1", dict(up=gf1["up"], down=gf1["down"]))
    g_wo = _mm_tn(o, dh3, name="attn_o_wg", tn=1024, tt=1024)
    do = _mm_nt(dh3, _tie(Wr["w_o"], tok), name="attn_o_dg", out_dtype=BF16, tm=1024, tn=1024, tk=1024)
    dq, dk, dv = _sba_bwd(q, kv, lt, do, name="sba_bwd")
    g_wq = _mm_tn(h2, dq, norm_w=Ws["attn_norm_w"], name="attn_q_wg", tn=1024, tt=1024)
    dh2a, g_attn_nw = _mm_nt(dq, Wr["w_q"], epi=(h2, Ws["attn_norm_w"], dh3), name="attn_q_dg", tm=1024, tk=1024)
    dkv = jnp.concatenate([dk, dv], axis=1)
    g_wkv = _mm_tn(h2, dkv, norm_w=Ws["kv_norm_w"], name="attn_kv_wg", tn=1024, tt=1024)
    dh2, g_kv_nw = _mm_nt(dkv, Wr["w_kv"], epi=(h2, Ws["kv_norm_w"], dh2a), name="attn_kv_dg", tm=1024, tk=1024)
    tok = put_g("attn", dict(w_o=g_wo, w_q=g_wq, w_k=g_wkv[:, :D_MODEL], w_v=g_wkv[:, D_MODEL:]))
    dh1, gf0 = _ffn_bwd(dh2, h1, ffn0, fnw[0], W0["up"], fcw[0], _tie(fcb[0], tok), W0["down"], "0")
    tok = put_g("ffn0", dict(up=gf0["up"], down=gf0["down"]))
    g_out = _mm_tn(yn, dh1, name="ssm_out_wg", tn=1024, tt=1024)
    dyn = _mm_nt(dh1, _tie(Ws["ssm_out_w"], tok), name="ssm_out_dg", out_dtype=BF16, tm=1024, tn=1024, tk=1024)
    tok = put_g("ssm_out", dict(ssm_out_w=g_out))
    dxbc_c, dz, ddt, g_gnw, dpar = _ssd_bwd(xbc_c, zx, dtg, par, _tie(gnw, tok), y, st, dyn, name="ssd_bwd")
    dhid, g_scw, g_scb = _ssm_conv_bwd_pre(zx, Ws["ssm_conv_w"], Ws["ssm_conv_b"].reshape(1, -1), dxbc_c,
                                           name="ssm_conv_bwd")
    dzx = _conv_bwd_in(dhid, Ws["ssm_conv_w"], K=SSM_CONV, name="ssm_conv_bwd_in", into=(dz, D_INNER))
    ddt_t = ddt[:, :, :8].transpose(1, 0, 2).reshape(T, SSM_HEADS).astype(BF16)
    dzx = _put_cols(dzx, jnp.pad(ddt_t, ((0, 0), (0, IN_PROJ_PAD - IN_PROJ_DIM))), D_INNER + CONV_DIM, name="ssm_ddt_cols")
    g_in = _mm_tn_t(dzx, x, norm_w=Ws["ssm_norm_w"], name="ssm_in_wg", tn=1792, tt=1024)
    tok = put_g("ssm_in", dict(ssm_in_w=g_in[:IN_PROJ_DIM]))
    dx, g_ssm_nw = _mm_nt(dzx, Ws["in_w"], epi=(x, _tie(Ws["ssm_norm_w"], tok), dh1), name="ssm_in_dg", w_t=True,
                          tm=1024, tk=1792)
    f = {
        "ssm_norm_w": g_ssm_nw.reshape(-1), "ssm_conv_w": g_scw,
        "ssm_conv_b": g_scb.reshape(-1), "ssm_dt_bias": dpar[:, 0, :8].reshape(-1),
        "ssm_a_log": dpar[:, 1, :8].reshape(-1), "ssm_d": dpar[:, 2, :8].reshape(-1),
        "ssm_gate_norm_w": g_gnw.reshape(-1), "kv_norm_w": g_kv_nw.reshape(-1), "attn_norm_w": g_attn_nw.reshape(-1),
        "ffn_norm_w": jnp.stack([gf0["norm"], gf1["norm"]]), "ffn_conv_w": jnp.stack([gf0["conv_w"], gf1["conv_w"]]),
        "ffn_conv_b": jnp.stack([gf0["conv_b"], gf1["conv_b"]]), "final_norm_w": g_final.reshape(-1),
    }
    return loss, dx, f


def kernel(x, ssm_norm_w, ssm_in_w, ssm_conv_w, ssm_conv_b, ssm_dt_bias, ssm_a_log, ssm_d, ssm_gate_norm_w, ssm_out_w, kv_norm_w, w_k, w_v, attn_norm_w, w_q, w_o, ffn_norm_w, ffn_up_w, ffn_conv_w, ffn_conv_b, ffn_down_w, final_norm_w, loss_target, m_ssm_norm_w, m_ssm_in_w, m_ssm_conv_w, m_ssm_conv_b, m_ssm_dt_bias, m_ssm_a_log, m_ssm_d, m_ssm_gate_norm_w, m_ssm_out_w, m_kv_norm_w, m_w_k, m_w_v, m_attn_norm_w, m_w_q, m_w_o, m_ffn_norm_w, m_ffn_up_w, m_ffn_conv_w, m_ffn_conv_b, m_ffn_down_w, m_final_norm_w, v_ssm_norm_w, v_ssm_in_w, v_ssm_conv_w, v_ssm_conv_b, v_ssm_dt_bias, v_ssm_a_log, v_ssm_d, v_ssm_gate_norm_w, v_ssm_out_w, v_kv_norm_w, v_w_k, v_w_v, v_attn_norm_w, v_w_q, v_w_o, v_ffn_norm_w, v_ffn_up_w, v_ffn_conv_w, v_ffn_conv_b, v_ffn_down_w, v_final_norm_w):
    env = dict(locals())
    p = {n: env[n] for n in _WEIGHTS}
    mom = {n: env["m_" + n] for n in _WEIGHTS}
    var = {n: env["v_" + n] for n in _WEIGHTS}
    T = x.shape[1]
    me = 4 * lax.axis_index("x") + 2 * lax.axis_index("y") + lax.axis_index("c")
    rs = D_FF // N_DEV

    def bf2(a):
        return _as2d(a).astype(BF16)

    _T = ("ssm_in_w", "ffn_up_w")

    def t2d(a):
        return jnp.swapaxes(a, -1, -2).reshape(-1, a.shape[-2])

    def from_t2d(a, like):
        return jnp.swapaxes(a.reshape(like.shape[:-2] + (like.shape[-1], like.shape[-2])), -1, -2)

    n_in, n_up = p["ssm_in_w"].shape[-1], p["ffn_up_w"].shape[-1]

    def with_own(srcs, lands, scatter):
        out = []
        for s, l in zip(srcs, lands):
            own = lax.dynamic_index_in_dim(s, me, 0, keepdims=False) if scatter else s
            out.append(lax.dynamic_update_index_in_dim(l, own, me, 0))
        return out

    a_names = ["ssm_in_w"] + _SMALL_SHARDED
    got_a = dict(zip(a_names, _all_gather([t2d(p["ssm_in_w"]).astype(BF16)] + [_as2d(p[n]) for n in _SMALL_SHARDED],
                                          name="gather_ssm")))
    ffn0_names = ["ssm_out_w", "up0", "down0"]
    rest_names = ["w_q", "w_k", "w_v", "w_o"]
    up_t = jnp.swapaxes(p["ffn_up_w"], -1, -2).astype(BF16)
    shard = {"up0": up_t[0], "down0": bf2(p["ffn_down_w"][0]), "up1": up_t[1],
             "down1": bf2(p["ffn_down_w"][1]), "w_q": bf2(p["w_q"]), "w_k": bf2(p["w_k"]), "w_v": bf2(p["w_v"]),
             "w_o": bf2(p["w_o"]), "ssm_out_w": bf2(p["ssm_out_w"])}

    def anchored(a, on):
        return a + (jnp.where(jnp.isfinite(on), on, 0.0) * 0.0).astype(a.dtype)

    h_ffn0 = _push_start([anchored(shard[ffn0_names[0]], got_a["ssm_norm_w"][0, 0, 0])]
                         + [shard[n] for n in ffn0_names[1:]], scatter=False, name="gather_ffn0_start")
    handles = {}

    def get_w(group, after):
        if group == "ssm":
            W = {n: p[n] for n in _SMALL_REPL}
            for n in ("ssm_dt_bias", "ssm_a_log", "ssm_d", "attn_norm_w"):
                W[n] = W[n].reshape(-1)
            W["in_w"] = jnp.concatenate([got_a["ssm_in_w"][j] for j in range(N_DEV)]
                                        + [jnp.zeros((IN_PROJ_PAD - IN_PROJ_DIM, D_MODEL), BF16)], axis=0)
            W["ssm_norm_w"] = _tie(got_a["ssm_norm_w"].reshape(D_MODEL), h_ffn0["token"])
            W["ssm_conv_w"] = _cols_to_full(got_a["ssm_conv_w"])
            W["ssm_conv_b"] = got_a["ssm_conv_b"].reshape(CONV_DIM)
            W["ssm_gate_norm_w"] = got_a["ssm_gate_norm_w"].reshape(D_INNER)
            W["ffn_conv_w"] = _cols_to_full(got_a["ffn_conv_w"]).reshape(2, FFN_CONV, 2 * D_FF)
            return W
        if group == "rest_start":
            handles["rest"] = _push_start([anchored(shard[rest_names[0]], after[0, 0])]
                                          + [shard[n] for n in rest_names[1:]], scatter=False, name="gather_rest_start")
            handles["ffn1"] = _push_start([anchored(shard["up1"], handles["rest"]["token"][0, 0]), shard["down1"]],
                                          scatter=False, name="gather_ffn1_start")
            return handles["ffn1"]["token"]
        if group == "ffn1":
            srcs, lands = _push_wait(handles["ffn1"], after, name="gather_ffn1_wait")
            up, down = with_own(srcs, lands, False)
            return dict(up=up.reshape(2 * D_FF, D_MODEL), down=down.reshape(D_FF, D_MODEL))
        if group == "ffn0":
            srcs, lands = _push_wait(h_ffn0, after, name="gather_ffn0_wait")
            out, up, down = with_own(srcs, lands, False)
            return dict(ssm_out_w=out.reshape(D_INNER, D_MODEL), up=up.reshape(2 * D_FF, D_MODEL),
                        down=down.reshape(D_FF, D_MODEL))
        srcs, lands = _push_wait(handles["rest"], after, name="gather_rest_wait")
        g = dict(zip(rest_names, with_own(srcs, lands, False)))
        sq = lambda a: a.reshape(D_MODEL, D_MODEL)
        return dict(w_q=sq(g["w_q"]), w_kv=jnp.concatenate([sq(g["w_k"]), sq(g["w_v"])], axis=1), w_o=sq(g["w_o"]))

    pending = []

    def put_g(group, g):
        if group in ("ffn0", "ffn1"):
            keys = [("ffn_up_w", int(group[-1])), ("ffn_down_w", int(group[-1]))]
            blocks = [g["up"].reshape(N_DEV, n_up, D_MODEL), g["down"].reshape(N_DEV, rs, D_MODEL)]
        elif group == "attn":
            keys = [(n, None) for n in ("w_o", "w_q", "w_k", "w_v")]
            blocks = [g[n].reshape(N_DEV, D_MODEL // N_DEV, D_MODEL) for n, _ in keys]
        elif group == "ssm_out":
            keys = [("ssm_out_w", None)]
            blocks = [g["ssm_out_w"].reshape(N_DEV, D_INNER // N_DEV, D_MODEL)]
        else:
            keys = [("ssm_in_w", None)]
            blocks = [jnp.stack([g["ssm_in_w"][j * n_in:(j + 1) * n_in] for j in range(N_DEV)])]
        h = _push_start(blocks, scatter=True, name=f"exchange_{group}_start")
        pending.append((group, keys, h))
        return h["token"]

    loss_row, dx, f = _local_step(x.reshape(T, D_MODEL), loss_target.reshape(T, D_MODEL), get_w, put_g)

    small_names = _SMALL_REPL + _SMALL_SHARDED
    small_full = _pack_small([f[n] for n in small_names] + [loss_row[0, 0:1]])
    small_bcast = jnp.broadcast_to(small_full[None], (N_DEV,) + small_full.shape)
    h_small = _push_start([small_bcast], scatter=True, name="exchange_small_start")
    tok = h_small["token"]

    arrived, res = {}, {}
    after = dx
    for group, keys, h in pending:
        srcs, lands = _push_wait(h, after, name=f"exchange_{group}_wait")
        arrived.update(zip(keys, with_own(srcs, lands, True)))
        for n in _BIG:
            layered = (n, 0) in arrived or (n, 1) in arrived
            if n in res or not ((n, None) in arrived or ((n, 0) in arrived and (n, 1) in arrived)):
                continue
            parts = [arrived[(n, 0)], arrived[(n, 1)]] if layered else arrived[(n, None)]
            w2, m2, v2 = ((t2d if n in _T else _as2d)(a[n]) for a in (p, mom, var))
            if not res:
                w2 = _tie(w2, tok)
            tiles = {"ffn_down_w": dict(tr=rs), "ffn_up_w": dict(tr=n_up // 2), "ssm_in_w": dict(tr=n_in, tc=256)}
            res[n] = _adamw(parts, w2, m2, v2, name=f"adamw_{n}", **tiles.get(n, dict(tr=256)))
            after = res[n][0]
    srcs, lands = _push_wait(h_small, after, name="exchange_small_wait")
    small_parts = with_own(srcs, lands, True)[0]
    out_g, out_d, out_m, out_v = {}, {}, {}, {}
    for n in _BIG:
        out_g[n], out_d[n], out_m[n], out_v[n] = (from_t2d(t, p[n]) if n in _T else t.reshape(p[n].shape) for t in res[n])

    zero = jnp.zeros_like(small_full)
    g_small_sum = _adamw(small_parts, zero, zero, zero, name="sum_small_grads", tr=small_full.shape[0])[0]
    *small_sums, loss_sum = _unpack_small(g_small_sum, [f[n].shape for n in small_names] + [(1,)])
    loss = loss_sum[0]
    g_small = dict(zip(small_names, small_sums))
    for n in _SMALL_SHARDED:
        width = p[n].shape[-1]
        g_small[n] = lax.dynamic_slice_in_dim(g_small[n], me * width, width, axis=g_small[n].ndim - 1)
    sw = _pack_small([p[n] for n in small_names])
    sm = _pack_small([mom[n] for n in small_names])
    sv = _pack_small([var[n] for n in small_names])
    sg = _pack_small([g_small[n] for n in small_names])
    _, d, nm, nv = _adamw(sg[None], sw, sm, sv, name="adamw_small", tr=sw.shape[0])
    shard_shapes = [p[n].shape for n in small_names]
    for n, dd, mm, vv in zip(small_names, _unpack_small(d, shard_shapes), _unpack_small(nm, shard_shapes),
                             _unpack_small(nv, shard_shapes)):
        out_g[n] = g_small[n].reshape(p[n].shape)
        out_d[n], out_m[n], out_v[n] = dd, mm, vv

    return (loss, dx.reshape(x.shape), *[out_g[n] for n in _WEIGHTS], *[out_d[n] for n in _WEIGHTS],
            *[out_m[n] for n in _WEIGHTS], *[out_v[n] for n in _WEIGHTS])
```

```python
import functools
import math

import jax
import jax.numpy as jnp
from jax import lax
from jax.experimental import pallas as pl
from jax.experimental.pallas import tpu as pltpu

F32 = jnp.float32
BF16 = jnp.bfloat16
EPS = 1e-6

D_MODEL = 1024
D_INNER = 2048
SSM_HEADS = 32
SSM_GROUPS = 4
SSM_STATE = 128
SSM_CONV = 4
SSM_CHUNK = 128
GN = SSM_GROUPS * SSM_STATE
CONV_DIM = D_INNER + 2 * GN
IN_PROJ_DIM = D_INNER + CONV_DIM + SSM_HEADS
IN_PROJ_PAD = 5376
SB_HEADS = 16
SB_HEAD_DIM = 64
SB_BLOCK = 128
D_FF = 2816
FFN_CONV = 3
N_DEV = 8

ADAM_LR = 0.001
ADAM_B1 = 0.9
ADAM_B2 = 0.999
ADAM_EPS = 1e-08
ADAM_WD = 0.01
ADAM_STEP = 10

_MESH = pl.DeviceIdType.MESH
_NT = (((1,), (1,)), ((), ()))
_TN = (((0,), (0,)), ((), ()))
_ANY = pl.BlockSpec(memory_space=pl.ANY)


def _cparams(sem, vmem_mb=48):
    return pltpu.CompilerParams(dimension_semantics=sem, vmem_limit_bytes=vmem_mb * 1024 * 1024)


def _sigmoid(x):
    return 0.5 * jnp.tanh(0.5 * x) + 0.5


def _softplus(x):
    return jnp.maximum(x, 0.0) + jnp.log(1.0 + jnp.exp(-jnp.abs(x)))


def _rms_fwd(xv, w):
    r = lax.rsqrt(jnp.mean(xv * xv, axis=-1, keepdims=True) + EPS)
    return xv * r * w


def _mm_fwd(x, w, *, name, norm_w=None, residual=None, out_dtype=F32, tm=512, tn=512, halves=False, w_t=False):
    M, K = x.shape
    N = w.shape[0] if w_t else w.shape[1]
    tm, tn = min(tm, M), min(tn, N)
    assert M % tm == 0 and N % tn == 0, (name, M, N, tm, tn)
    if halves:
        nbh = N // 2 // tn
        assert N // 2 % tn == 0
        out_spec = pl.BlockSpec((None, tm, tn), lambda i, j: (lax.div(j, nbh), i, lax.rem(j, nbh)))
        out_shape = jax.ShapeDtypeStruct((2, M, N // 2), out_dtype)
    else:
        out_spec = pl.BlockSpec((tm, tn), lambda i, j: (i, j))
        out_shape = jax.ShapeDtypeStruct((M, N), out_dtype)
    has_norm, has_res = norm_w is not None, residual is not None

    def body(*refs):
        x_ref, w_ref = refs[0], refs[1]
        p = 2
        nw_ref = r_ref = None
        if has_norm:
            nw_ref = refs[p]
            p += 1
        if has_res:
            r_ref = refs[p]
            p += 1
        o_ref = refs[p]
        xv = x_ref[...]
        if has_norm:
            xv = _rms_fwd(xv.astype(F32), nw_ref[...])
        acc = lax.dot_general(xv.astype(BF16), w_ref[...], _NT if w_t else (((1,), (0,)), ((), ())),
                              preferred_element_type=F32)
        if has_res:
            acc = acc + r_ref[...]
        o_ref[...] = acc.astype(out_dtype)

    w_spec = pl.BlockSpec((tn, K), lambda i, j: (j, 0)) if w_t else pl.BlockSpec((K, tn), lambda i, j: (0, j))
    in_specs = [pl.BlockSpec((tm, K), lambda i, j: (i, 0)), w_spec]
    args = [x, w]
    if has_norm:
        in_specs.append(pl.BlockSpec((1, K), lambda i, j: (0, 0)))
        args.append(norm_w.reshape(1, K))
    if has_res:
        in_specs.append(pl.BlockSpec((tm, tn), lambda i, j: (i, j)))
        args.append(residual)
    return pl.pallas_call(
        body, name=name, grid=(M // tm, N // tn), in_specs=in_specs,
        out_specs=out_spec, out_shape=out_shape,
        compiler_params=_cparams(("parallel", "parallel")))(*args)


def _mm_nt(dy, w, *, name, epi=None, out_dtype=F32, tm=512, tn=512, tk=512, w_t=False):
    halves = dy.ndim == 3
    M, K = (dy.shape[1], 2 * dy.shape[2]) if halves else dy.shape
    N = w.shape[1] if w_t else w.shape[0]
    tm, tk = min(tm, M), min(tk, K)
    tn = N if epi is not None else min(tn, N)
    assert M % tm == 0 and N % tn == 0 and K % tk == 0, (name, M, N, K, tm, tn, tk)
    nk = K // tk
    has_epi = epi is not None

    def body(*refs):
        if has_epi:
            dy_ref, w_ref, h_ref, nw_ref, r_ref, o_ref, dnw_ref, acc_ref = refs
        else:
            dy_ref, w_ref, o_ref, acc_ref = refs
        i = pl.program_id(0)
        k = pl.program_id(2)

        @pl.when(k == 0)
        def _():
            acc_ref[...] = jnp.zeros_like(acc_ref)

        acc_ref[...] += lax.dot_general(dy_ref[...].astype(BF16), w_ref[...], (((1,), (0,)), ((), ())) if w_t else _NT,
                                        preferred_element_type=F32)

        @pl.when(k == nk - 1)
        def _():
            du = acc_ref[...]
            if has_epi:
                hv = h_ref[...]
                r = lax.rsqrt(jnp.mean(hv * hv, axis=-1, keepdims=True) + EPS)
                xhat = hv * r
                dxh = du * nw_ref[...]
                dx = r * (dxh - xhat * jnp.mean(dxh * xhat, axis=-1, keepdims=True))
                o_ref[...] = (r_ref[...] + dx).astype(out_dtype)
                contrib = jnp.sum(du * xhat, axis=0, keepdims=True)

                @pl.when(i == 0)
                def _():
                    dnw_ref[...] = contrib

                @pl.when(i > 0)
                def _():
                    dnw_ref[...] += contrib
            else:
                o_ref[...] = du.astype(out_dtype)

    if halves:
        nkh = K // 2 // tk
        assert K // 2 % tk == 0
        dy_spec = pl.BlockSpec((None, tm, tk), lambda i, j, k: (lax.div(k, nkh), i, lax.rem(k, nkh)))
    else:
        dy_spec = pl.BlockSpec((tm, tk), lambda i, j, k: (i, k))
    w_spec = pl.BlockSpec((tk, tn), lambda i, j, k: (k, j)) if w_t else pl.BlockSpec((tn, tk), lambda i, j, k: (j, k))
    in_specs = [dy_spec, w_spec]
    args = [dy, w]
    out_specs = [pl.BlockSpec((tm, tn), lambda i, j, k: (i, j))]
    out_shape = [jax.ShapeDtypeStruct((M, N), out_dtype)]
    if has_epi:
        h, nw, res = epi
        in_specs += [pl.BlockSpec((tm, N), lambda i, j, k: (i, 0)), pl.BlockSpec((1, N), lambda i, j, k: (0, 0)),
                     pl.BlockSpec((tm, N), lambda i, j, k: (i, 0))]
        args += [h, nw.reshape(1, N), res]
        out_specs.append(pl.BlockSpec((1, N), lambda i, j, k: (0, 0)))
        out_shape.append(jax.ShapeDtypeStruct((1, N), F32))
    outs = pl.pallas_call(
        body, name=name, grid=(M // tm, N // tn, nk), in_specs=in_specs, out_specs=out_specs, out_shape=out_shape,
        scratch_shapes=[pltpu.VMEM((tm, tn), F32)],
        compiler_params=_cparams(("arbitrary", "arbitrary", "arbitrary")))(*args)
    return (outs[0], outs[1]) if has_epi else outs[0]


def _mm_tn(x, dy, *, name, norm_w=None, out_dtype=BF16, tk1=1024, tn=512, tt=512):
    T, K1 = x.shape
    halves = dy.ndim == 3
    N = 2 * dy.shape[2] if halves else dy.shape[1]
    tk1, tn, tt = min(tk1, K1), min(tn, N), min(tt, T)
    has_norm = norm_w is not None
    assert K1 % tk1 == 0 and N % tn == 0 and T % tt == 0, (name, K1, N, T, tk1, tn, tt)
    assert not has_norm or tk1 == K1
    nt = T // tt

    def body(*refs):
        if has_norm:
            x_ref, dy_ref, nw_ref, o_ref, acc_ref = refs
        else:
            x_ref, dy_ref, o_ref, acc_ref = refs
        t = pl.program_id(2)

        @pl.when(t == 0)
        def _():
            acc_ref[...] = jnp.zeros_like(acc_ref)

        xv = x_ref[...]
        if has_norm:
            xv = _rms_fwd(xv.astype(F32), nw_ref[...])
        acc_ref[...] += lax.dot_general(xv.astype(BF16), dy_ref[...].astype(BF16), _TN, preferred_element_type=F32)

        @pl.when(t == nt - 1)
        def _():
            o_ref[...] = acc_ref[...].astype(out_dtype)

    if halves:
        nbh = N // 2 // tn
        assert N // 2 % tn == 0
        dy_spec = pl.BlockSpec((None, tt, tn), lambda a, b, t: (lax.div(b, nbh), t, lax.rem(b, nbh)))
    else:
        dy_spec = pl.BlockSpec((tt, tn), lambda a, b, t: (t, b))
    in_specs = [pl.BlockSpec((tt, tk1), lambda a, b, t: (t, a)), dy_spec]
    args = [x, dy]
    if has_norm:
        in_specs.append(pl.BlockSpec((1, K1), lambda a, b, t: (0, 0)))
        args.append(norm_w.reshape(1, K1))
    return pl.pallas_call(
        body, name=name, grid=(K1 // tk1, N // tn, nt), in_specs=in_specs,
        out_specs=pl.BlockSpec((tk1, tn), lambda a, b, t: (a, b)),
        out_shape=jax.ShapeDtypeStruct((K1, N), out_dtype),
        scratch_shapes=[pltpu.VMEM((tk1, tn), F32)],
        compiler_params=_cparams(("parallel", "parallel", "arbitrary")))(*args)


def _mm_tn_t(dy, x, *, name, norm_w, out_dtype=BF16, tn=1408, tt=1024, vmem_mb=48):
    T, K1 = x.shape
    halves = dy.ndim == 3
    N = 2 * dy.shape[2] if halves else dy.shape[1]
    tn, tt = min(tn, N), min(tt, T)
    assert N % tn == 0 and T % tt == 0, (name, N, T, tn, tt)
    nt = T // tt

    def body(dy_ref, x_ref, nw_ref, o_ref, acc_ref):
        t = pl.program_id(1)

        @pl.when(t == 0)
        def _():
            acc_ref[...] = jnp.zeros_like(acc_ref)

        xn = _rms_fwd(x_ref[...].astype(F32), nw_ref[...]).astype(BF16)
        acc_ref[...] += lax.dot_general(dy_ref[...].astype(BF16), xn, _TN, preferred_element_type=F32)

        @pl.when(t == nt - 1)
        def _():
            o_ref[...] = acc_ref[...].astype(out_dtype)

    if halves:
        nbh = N // 2 // tn
        assert N // 2 % tn == 0
        dy_spec = pl.BlockSpec((None, tt, tn), lambda b, t: (lax.div(b, nbh), t, lax.rem(b, nbh)))
    else:
        dy_spec = pl.BlockSpec((tt, tn), lambda b, t: (t, b))
    return pl.pallas_call(
        body, name=name, grid=(N // tn, nt),
        in_specs=[dy_spec, pl.BlockSpec((tt, K1), lambda b, t: (t, 0)), pl.BlockSpec((1, K1), lambda b, t: (0, 0))],
        out_specs=pl.BlockSpec((tn, K1), lambda b, t: (b, 0)),
        out_shape=jax.ShapeDtypeStruct((N, K1), out_dtype),
        scratch_shapes=[pltpu.VMEM((tn, K1), F32)],
        compiler_params=_cparams(("parallel", "arbitrary"), vmem_mb))(dy, x, norm_w.reshape(1, K1))


def _shift_down(xb, prev8, j):
    main = pltpu.roll(xb, j, 0)
    head = pltpu.roll(xb[0:8], j, 0)
    ph = pltpu.roll(prev8, j, 0)
    row8 = lax.broadcasted_iota(jnp.int32, head.shape, 0)
    head = jnp.where(row8 < j, ph, head)
    return jnp.concatenate([head, main[8:]], axis=0)


def _shift_up(xb, next8, j):
    tt = xb.shape[0]
    main = pltpu.roll(xb, tt - j, 0)
    tail = pltpu.roll(xb[tt - 8:tt], 8 - j, 0)
    nh = pltpu.roll(next8, 8 - j, 0)
    row8 = lax.broadcasted_iota(jnp.int32, tail.shape, 0)
    tail = jnp.where(row8 + j >= 8, nh, tail)
    return jnp.concatenate([main[:tt - 8], tail], axis=0)


def _conv_hid(xb, prev8, w, b_row, K):
    out = b_row
    shifted = []
    for j in range(K):
        sh = K - 1 - j
        xs = xb if sh == 0 else _shift_down(xb, prev8, sh)
        shifted.append(xs)
        out = out + xs * w[j:j + 1, :]
    return out, shifted


def _prev_idx(i, nb8):
    return jnp.maximum(i * nb8 - 1, 0)


def _ssm_conv_fwd(zx, w, b, *, name, tt=512, tc=512):
    T = zx.shape[0]
    tt = min(tt, T)
    C, K = CONV_DIM, SSM_CONV
    cb0, nb8 = D_INNER // tc, tt // 8

    def body(x_ref, p_ref, w_ref, b_ref, o_ref):
        first = (pl.program_id(1) > 0).astype(F32)
        hid, _ = _conv_hid(x_ref[...], p_ref[...] * first, w_ref[...], b_ref[...], K)
        o_ref[...] = hid * _sigmoid(hid)

    return pl.pallas_call(
        body, name=name, grid=(C // tc, T // tt),
        in_specs=[pl.BlockSpec((tt, tc), lambda c, i: (i, c + cb0)),
                  pl.BlockSpec((8, tc), lambda c, i: (_prev_idx(i, nb8), c + cb0)),
                  pl.BlockSpec((K, tc), lambda c, i: (0, c)), pl.BlockSpec((1, tc), lambda c, i: (0, c))],
        out_specs=pl.BlockSpec((tt, tc), lambda c, i: (i, c)),
        out_shape=jax.ShapeDtypeStruct((T, C), F32),
        compiler_params=_cparams(("parallel", "parallel")))(zx, zx, w, b)


def _ssm_conv_bwd_pre(zx, w, b, dout, *, name, tt=512, tc=512):
    T = zx.shape[0]
    tt = min(tt, T)
    C, K = CONV_DIM, SSM_CONV
    cb0, nb8 = D_INNER // tc, tt // 8

    def body(x_ref, p_ref, w_ref, b_ref, d_ref, dh_ref, dw_ref, db_ref):
        t = pl.program_id(1)
        first = (t > 0).astype(F32)
        hid, shifted = _conv_hid(x_ref[...], p_ref[...] * first, w_ref[...], b_ref[...], K)
        sg = _sigmoid(hid)
        dh = d_ref[...] * (sg * (1.0 + hid * (1.0 - sg)))
        dh_ref[...] = dh

        @pl.when(t == 0)
        def _():
            dw_ref[...] = jnp.zeros_like(dw_ref)
            db_ref[...] = jnp.zeros_like(db_ref)

        db_ref[...] += jnp.sum(dh, axis=0, keepdims=True)
        for j in range(K):
            dw_ref[j:j + 1, :] += jnp.sum(dh * shifted[j], axis=0, keepdims=True)

    return pl.pallas_call(
        body, name=name, grid=(C // tc, T // tt),
        in_specs=[pl.BlockSpec((tt, tc), lambda c, i: (i, c + cb0)),
                  pl.BlockSpec((8, tc), lambda c, i: (_prev_idx(i, nb8), c + cb0)),
                  pl.BlockSpec((K, tc), lambda c, i: (0, c)), pl.BlockSpec((1, tc), lambda c, i: (0, c)),
                  pl.BlockSpec((tt, tc), lambda c, i: (i, c))],
        out_specs=[pl.BlockSpec((tt, tc), lambda c, i: (i, c)), pl.BlockSpec((K, tc), lambda c, i: (0, c)),
                   pl.BlockSpec((1, tc), lambda c, i: (0, c))],
        out_shape=[jax.ShapeDtypeStruct((T, C), F32), jax.ShapeDtypeStruct((K, C), F32),
                   jax.ShapeDtypeStruct((1, C), F32)],
        compiler_params=_cparams(("parallel", "arbitrary")))(zx, zx, w, b, dout)


def _put_cols(buf, src, col0, *, name, tt=512):
    T, C = src.shape
    tt = min(tt, T)

    def body(s_ref, _, o_ref):
        o_ref[...] = s_ref[...]

    return pl.pallas_call(
        body, name=name, grid=(T // tt,),
        in_specs=[pl.BlockSpec((tt, C), lambda i: (i, 0)), _ANY],
        out_specs=pl.BlockSpec((tt, C), lambda i: (i, col0 // C)),
        out_shape=jax.ShapeDtypeStruct(buf.shape, buf.dtype), input_output_aliases={1: 0},
        compiler_params=_cparams(("parallel",)))(src, buf)


def _stack_rows(g, rows, *, name, tc=256):
    n, r, C = g.shape

    def body(g_ref, o_ref):
        for j in range(n):
            o_ref[j * r:(j + 1) * r, :] = g_ref[j]
        o_ref[n * r:, :] = jnp.zeros((rows - n * r, tc), g.dtype)

    return pl.pallas_call(
        body, name=name, grid=(C // tc,),
        in_specs=[pl.BlockSpec((n, r, tc), lambda i: (0, 0, i))],
        out_specs=pl.BlockSpec((rows, tc), lambda i: (0, i)),
        out_shape=jax.ShapeDtypeStruct((rows, C), g.dtype),
        compiler_params=_cparams(("parallel",)))(g)


def _conv_bwd_in(dh, w, *, name, K, tt=512, tc=512, out_dtype=BF16, into=None):
    T, C = dh.shape
    tt = min(tt, T)
    nb8, nT = tt // 8, T // tt
    last8 = T // 8 - 1
    cb0 = 0 if into is None else into[1] // tc

    def body(d_ref, n_ref, w_ref, *rest):
        o_ref = rest[-1]
        notlast = (pl.program_id(1) < nT - 1).astype(F32)
        d = d_ref[...]
        nxt = n_ref[...] * notlast
        w_ = w_ref[...]
        acc = d * w_[K - 1:K, :]
        for sh in range(1, K):
            acc = acc + _shift_up(d, nxt, sh) * w_[K - 1 - sh:K - sh, :]
        o_ref[...] = acc.astype(out_dtype)

    in_specs = [pl.BlockSpec((tt, tc), lambda c, i: (i, c)),
                pl.BlockSpec((8, tc), lambda c, i: (jnp.minimum((i + 1) * nb8, last8), c)),
                pl.BlockSpec((K, tc), lambda c, i: (0, c))]
    args = [dh, dh, w]
    if into is None:
        out_shape, alias = jax.ShapeDtypeStruct((T, C), out_dtype), {}
    else:
        assert into[0].dtype == out_dtype and into[1] % tc == 0
        in_specs.append(_ANY)
        args.append(into[0])
        out_shape, alias = jax.ShapeDtypeStruct(into[0].shape, out_dtype), {3: 0}
    return pl.pallas_call(
        body, name=name, grid=(C // tc, nT), in_specs=in_specs,
        out_specs=pl.BlockSpec((tt, tc), lambda c, i: (i, c + cb0)),
        out_shape=out_shape, input_output_aliases=alias,
        compiler_params=_cparams(("parallel", "parallel")))(*args)


def _ffn_conv_fwd3(a3, w, b, *, name, tt=256, tc=1408):
    T = a3.shape[1]
    tt = min(tt, T)
    K, nbh, n16 = FFN_CONV, D_FF // tc, tt // 16

    def body(a_ref, p_ref, wg_ref, wv_ref, bg_ref, bv_ref, o_ref):
        first = (pl.program_id(1) > 0).astype(F32)
        a = a_ref[...].astype(F32)
        prev = p_ref[...].astype(F32)[:, 8:16, :] * first
        hg, _ = _conv_hid(a[0], prev[0], wg_ref[...], bg_ref[...], K)
        hv, _ = _conv_hid(a[1], prev[1], wv_ref[...], bv_ref[...], K)
        o_ref[...] = (hg * _sigmoid(hg) * hv).astype(BF16)

    return pl.pallas_call(
        body, name=name, grid=(nbh, T // tt),
        in_specs=[pl.BlockSpec((2, tt, tc), lambda c, i: (0, i, c)),
                  pl.BlockSpec((2, 16, tc), lambda c, i: (0, _prev_idx(i, n16), c)),
                  pl.BlockSpec((K, tc), lambda c, i: (0, c)), pl.BlockSpec((K, tc), lambda c, i: (0, c + nbh)),
                  pl.BlockSpec((1, tc), lambda c, i: (0, c)), pl.BlockSpec((1, tc), lambda c, i: (0, c + nbh))],
        out_specs=pl.BlockSpec((tt, tc), lambda c, i: (i, c)),
        out_shape=jax.ShapeDtypeStruct((T, D_FF), BF16),
        compiler_params=_cparams(("parallel", "parallel")))(a3, a3, w, w, b, b)


def _ffn_conv_bwd3(a3, w, b, dp, *, name, tt=256, tc=1408):
    T = a3.shape[1]
    tt = min(tt, T)
    K, nbh, n16 = FFN_CONV, D_FF // tc, tt // 16

    def body(a_ref, p_ref, wg_ref, wv_ref, bg_ref, bv_ref, dp_ref, dh_ref, dw_ref, db_ref):
        t = pl.program_id(1)
        first = (t > 0).astype(F32)
        a = a_ref[...].astype(F32)
        prev = p_ref[...].astype(F32)[:, 8:16, :] * first
        hg, sh_g = _conv_hid(a[0], prev[0], wg_ref[...], bg_ref[...], K)
        hv, sh_v = _conv_hid(a[1], prev[1], wv_ref[...], bv_ref[...], K)
        sg = _sigmoid(hg)
        d = dp_ref[...].astype(F32)
        dhg = d * hv * (sg * (1.0 + hg * (1.0 - sg)))
        dhv = d * (hg * sg)
        dh_ref[0] = dhg.astype(BF16)
        dh_ref[1] = dhv.astype(BF16)

        @pl.when(t == 0)
        def _():
            dw_ref[...] = jnp.zeros_like(dw_ref)
            db_ref[...] = jnp.zeros_like(db_ref)

        db_ref[0] += jnp.sum(dhg, axis=0, keepdims=True)
        db_ref[1] += jnp.sum(dhv, axis=0, keepdims=True)
        for j in range(K):
            dw_ref[0, j:j + 1, :] += jnp.sum(dhg * sh_g[j], axis=0, keepdims=True)
            dw_ref[1, j:j + 1, :] += jnp.sum(dhv * sh_v[j], axis=0, keepdims=True)

    return pl.pallas_call(
        body, name=name, grid=(nbh, T // tt),
        in_specs=[pl.BlockSpec((2, tt, tc), lambda c, i: (0, i, c)),
                  pl.BlockSpec((2, 16, tc), lambda c, i: (0, _prev_idx(i, n16), c)),
                  pl.BlockSpec((K, tc), lambda c, i: (0, c)), pl.BlockSpec((K, tc), lambda c, i: (0, c + nbh)),
                  pl.BlockSpec((1, tc), lambda c, i: (0, c)), pl.BlockSpec((1, tc), lambda c, i: (0, c + nbh)),
                  pl.BlockSpec((tt, tc), lambda c, i: (i, c))],
        out_specs=[pl.BlockSpec((2, tt, tc), lambda c, i: (0, i, c)), pl.BlockSpec((2, K, tc), lambda c, i: (0, 0, c)),
                   pl.BlockSpec((2, 1, tc), lambda c, i: (0, 0, c))],
        out_shape=[jax.ShapeDtypeStruct((2, T, D_FF), BF16), jax.ShapeDtypeStruct((2, K, D_FF), F32),
                   jax.ShapeDtypeStruct((2, 1, D_FF), F32)],
        compiler_params=_cparams(("parallel", "arbitrary")))(a3, a3, w, w, b, b, dp)


def _conv_bwd_in3(dh3, w, *, name, K, tt=256, tc=1408):
    H, T, C = dh3.shape
    tt = min(tt, T)
    nb, n16, nT = C // tc, tt // 16, T // tt
    last16 = T // 16 - 1

    def body(d_ref, n_ref, w_ref, o_ref):
        notlast = (pl.program_id(2) < nT - 1).astype(F32)
        d = d_ref[...].astype(F32)
        nxt = n_ref[...].astype(F32)[0:8, :] * notlast
        w_ = w_ref[...]
        acc = d * w_[K - 1:K, :]
        for sh in range(1, K):
            acc = acc + _shift_up(d, nxt, sh) * w_[K - 1 - sh:K - sh, :]
        o_ref[...] = acc.astype(BF16)

    return pl.pallas_call(
        body, name=name, grid=(H, nb, nT),
        in_specs=[pl.BlockSpec((None, tt, tc), lambda h, c, i: (h, i, c)),
                  pl.BlockSpec((None, 16, tc), lambda h, c, i: (h, jnp.minimum((i + 1) * n16, last16), c)),
                  pl.BlockSpec((K, tc), lambda h, c, i: (0, h * nb + c))],
        out_specs=pl.BlockSpec((None, tt, tc), lambda h, c, i: (h, i, c)),
        out_shape=jax.ShapeDtypeStruct((H, T, C), BF16),
        compiler_params=_cparams(("parallel", "parallel", "parallel")))(dh3, dh3, w)


def _cumsum_rows(x):
    L = x.shape[0]
    row = lax.broadcasted_iota(jnp.int32, x.shape, 0)
    k = 1
    while k < L:
        x = x + jnp.where(row >= k, pltpu.roll(x, k, 0), 0.0)
        k *= 2
    return x


def _rcumsum_rows(x):
    L = x.shape[0]
    row = lax.broadcasted_iota(jnp.int32, x.shape, 0)
    k = 1
    while k < L:
        x = x + jnp.where(row < L - k, pltpu.roll(x, L - k, 0), 0.0)
        k *= 2
    return x


def _split_terms(m, n):
    terms, rest = [], m
    for _ in range(n):
        t = rest.astype(BF16)
        terms.append(t)
        rest = rest - t.astype(F32)
    return jnp.concatenate(terms, axis=1)


def _select_dot(m, n_terms, n_out, cond):
    K = m.shape[1]
    k = lax.broadcasted_iota(jnp.int32, (K, n_out), 0)
    j = lax.broadcasted_iota(jnp.int32, (K, n_out), 1)
    sel = cond(k, j).astype(BF16)
    return jnp.dot(_split_terms(m, n_terms), jnp.concatenate([sel] * n_terms, axis=0), preferred_element_type=F32)


def _rowsum_mxu(m):
    return _select_dot(m, 2, 128, lambda k, j: k >= 0)


def _lane_block_sums(m, width):
    shift = width.bit_length() - 1
    return _select_dot(m, 2, 128, lambda k, j: j == jnp.right_shift(k, shift))


def _heads_to_pairs(m):
    return _select_dot(m, 3, 512, lambda k, j: k == jnp.right_shift(j, 6))


def _ssd_common(dt_ref, par_ref):
    par = par_ref[...]
    raw = dt_ref[...] + par[0:1, :]
    dt = _softplus(raw)
    a = -jnp.exp(par[1:2, :])
    cs = _cumsum_rows(dt * a)
    L = cs.shape[0]
    cs_last = cs[L - 1:L, :]
    return raw, dt, a, par[2:3, :], cs, cs.T, jnp.exp(cs), jnp.exp(cs_last - cs), jnp.exp(cs_last)


def _ssd_specs(nc, rev):
    L = SSM_CHUNK

    def ci(c):
        return nc - 1 - c if rev else c

    return [pl.BlockSpec((L, D_INNER), lambda c: (ci(c), 0)),
            pl.BlockSpec((L, GN), lambda c: (ci(c), D_INNER // GN)),
            pl.BlockSpec((L, GN), lambda c: (ci(c), D_INNER // GN + 1)),
            pl.BlockSpec((SSM_GROUPS, L, 128), lambda c: (0, ci(c), 0)),
            pl.BlockSpec((SSM_GROUPS, 8, 128), lambda c: (0, 0, 0)),
            pl.BlockSpec((L, D_INNER), lambda c: (ci(c), 0)),
            pl.BlockSpec((1, D_INNER), lambda c: (0, 0))], ci


def _round_robin(gens):
    live = list(gens)
    while live:
        nxt = []
        for gen in live:
            try:
                next(gen)
                nxt.append(gen)
            except StopIteration:
                pass
        live = nxt


def _group_views(g, wide, narrow, lead):
    return ([r.at[:, g * 512:(g + 1) * 512] for r in wide], [r.at[:, g * 128:(g + 1) * 128] for r in narrow],
            [r.at[g] for r in lead])


def _ssd_fwd(xbc_c, zx, dtg, par, gnw, *, name):
    T = xbc_c.shape[0]
    L = SSM_CHUNK
    nc = T // L
    in_specs, ci = _ssd_specs(nc, False)

    def body(xs_ref, b_ref, c_ref, dt_ref, par_ref, z_ref, gnw_ref, y_ref, yn_ref, st_ref, h_ref):
        @pl.when(pl.program_id(0) == 0)
        def _():
            h_ref[...] = jnp.zeros_like(h_ref)

        gens = []
        for g in range(SSM_GROUPS):
            (xs, z, gw, y, yn), (b, c), (dt, pr, st, h) = _group_views(
                g, [xs_ref, z_ref, gnw_ref, y_ref, yn_ref], [b_ref, c_ref], [dt_ref, par_ref, st_ref, h_ref])
            gens.append(group(xs, b, c, dt, pr, z, gw, y, yn, st, h))
        _round_robin(gens)

    def group(xs_ref, b_ref, c_ref, dt_ref, par_ref, z_ref, gnw_ref, y_ref, yn_ref, st_ref, h_ref):
        _, dt, _, dsk, cs, csT, ecs, eend, dec = _ssd_common(dt_ref, par_ref)
        Bb = b_ref[...].astype(BF16)
        Cb = c_ref[...].astype(BF16)
        G = lax.dot_general(Cb, Bb, _NT, preferred_element_type=F32)
        row = lax.broadcasted_iota(jnp.int32, (L, L), 0)
        col = lax.broadcasted_iota(jnp.int32, (L, L), 1)
        tril = col <= row
        lo = lax.broadcasted_iota(jnp.int32, (L, 128), 1) < 64
        lo1 = lax.broadcasted_iota(jnp.int32, (1, 128), 1) < 64
        dt_x, ecs_x, eend_x = (_heads_to_pairs(m) for m in (dt, ecs, eend))
        for pp in range(4):
            hA, hB = 2 * pp, 2 * pp + 1
            lanes = slice(pp * 128, (pp + 1) * 128)

            def sel1(m):
                return jnp.where(lo1, m[:, hA:hA + 1], m[:, hB:hB + 1])

            X = xs_ref[:, lanes]
            xd = X * dt_x[:, lanes]
            xdb = xd.astype(BF16)
            ys = []
            for h in (hA, hB):
                Lm = jnp.where(tril, jnp.exp(jnp.minimum(cs[:, h:h + 1] - csT[h:h + 1, :], 0.0)), 0.0)
                ys.append(jnp.dot((G * Lm).astype(BF16), xdb, preferred_element_type=F32))
                yield
            Hp = h_ref[pp]
            st_ref[pp] = Hp
            yoff = jnp.dot(Cb, Hp.astype(BF16), preferred_element_type=F32) * ecs_x[:, lanes]
            y_ref[:, lanes] = jnp.where(lo, ys[0], ys[1]) + yoff + sel1(dsk) * X
            S = lax.dot_general(Bb, (xd * eend_x[:, lanes]).astype(BF16), _TN, preferred_element_type=F32)
            h_ref[pp] = Hp * sel1(dec) + S
            yield
        zv = z_ref[...]
        yg = y_ref[...] * (zv * _sigmoid(zv))
        r = jnp.tile(lax.rsqrt(_rowsum_mxu(yg * yg) * (1.0 / 512) + EPS), (1, 4))
        yn_ref[...] = (yg * r * gnw_ref[...]).astype(BF16)

    return pl.pallas_call(
        body, name=name, grid=(nc,), in_specs=in_specs,
        out_specs=[pl.BlockSpec((L, D_INNER), lambda c: (c, 0)), pl.BlockSpec((L, D_INNER), lambda c: (c, 0)),
                   pl.BlockSpec((SSM_GROUPS, None, 4, 128, 128), lambda c: (0, c, 0, 0, 0))],
        out_shape=[jax.ShapeDtypeStruct((T, D_INNER), F32), jax.ShapeDtypeStruct((T, D_INNER), BF16),
                   jax.ShapeDtypeStruct((SSM_GROUPS, nc, 4, 128, 128), F32)],
        scratch_shapes=[pltpu.VMEM((SSM_GROUPS, 4, 128, 128), F32)],
        compiler_params=_cparams(("arbitrary",)))(xbc_c, xbc_c, xbc_c, dtg, par, zx, gnw)


def _ssd_bwd(xbc_c, zx, dtg, par, gnw, y, st, dyn, *, name):
    T = xbc_c.shape[0]
    L = SSM_CHUNK
    nc = T // L
    in_specs, ci = _ssd_specs(nc, True)
    in_specs += [pl.BlockSpec((L, D_INNER), lambda c: (ci(c), 0)),
                 pl.BlockSpec((SSM_GROUPS, None, 4, 128, 128), lambda c: (0, ci(c), 0, 0, 0)),
                 pl.BlockSpec((L, D_INNER), lambda c: (ci(c), 0))]

    def body(xs_ref, b_ref, c_ref, dt_ref, par_ref, z_ref, gnw_ref, y_ref, st_ref, dyn_ref,
             dxbc_ref, dz_ref, ddt_ref, dgnw_ref, dpar_ref, dh_ref):
        @pl.when(pl.program_id(0) == 0)
        def _():
            dh_ref[...] = jnp.zeros_like(dh_ref)
            dgnw_ref[...] = jnp.zeros_like(dgnw_ref)
            dpar_ref[...] = jnp.zeros_like(dpar_ref)

        dxs_ref = dxbc_ref.at[:, 0:D_INNER]
        db_ref = dxbc_ref.at[:, D_INNER:D_INNER + GN]
        dc_ref = dxbc_ref.at[:, D_INNER + GN:CONV_DIM]

        gens = []
        for g in range(SSM_GROUPS):
            (xs, z, gw, y, dyn, dxs, dz, dgw), (b, c, db, dc), (dt, pr, st, ddt, dpr, dh) = _group_views(
                g, [xs_ref, z_ref, gnw_ref, y_ref, dyn_ref, dxs_ref, dz_ref, dgnw_ref], [b_ref, c_ref, db_ref, dc_ref],
                [dt_ref, par_ref, st_ref, ddt_ref, dpar_ref, dh_ref])
            gens.append(group(xs, b, c, dt, pr, z, gw, y, st, dyn, dxs, db, dc, dz, ddt, dgw, dpr, dh))
        _round_robin(gens)

    def group(xs_ref, b_ref, c_ref, dt_ref, par_ref, z_ref, gnw_ref, y_ref, st_ref, dyn_ref,
              dxs_ref, db_ref, dc_ref, dz_ref, ddt_ref, dgnw_ref, dpar_ref, dh_ref):
        yv = y_ref[...]
        zv = z_ref[...]
        sg = _sigmoid(zv)
        sz = zv * sg
        yg = yv * sz
        r = jnp.tile(lax.rsqrt(_rowsum_mxu(yg * yg) * (1.0 / 512) + EPS), (1, 4))
        yh = yg * r
        dyn = dyn_ref[...].astype(F32)
        dgnw_ref[...] += jnp.sum(dyn * yh, axis=0, keepdims=True)
        dyh = dyn * gnw_ref[...]
        dyg = r * (dyh - yh * jnp.tile(_rowsum_mxu(dyh * yh) * (1.0 / 512), (1, 4)))
        dY_all = dyg * sz
        dz_ref[...] = (dyg * yv * (sg * (1.0 + zv * (1.0 - sg)))).astype(dz_ref.dtype)

        yield
        raw, dt, a, dsk, cs, csT, ecs, eend, dec = _ssd_common(dt_ref, par_ref)
        Bb = b_ref[...].astype(BF16)
        Cb = c_ref[...].astype(BF16)
        G = lax.dot_general(Cb, Bb, _NT, preferred_element_type=F32)
        row = lax.broadcasted_iota(jnp.int32, (L, L), 0)
        col = lax.broadcasted_iota(jnp.int32, (L, L), 1)
        tril = col <= row
        lo = lax.broadcasted_iota(jnp.int32, (L, 128), 1) < 64
        lane1 = lax.broadcasted_iota(jnp.int32, (1, 128), 1)
        lo1 = lane1 < 64
        rowl = lax.broadcasted_iota(jnp.int32, (L, 128), 0)
        dt_x, ecs_x, eend_x = (_heads_to_pairs(m) for m in (dt, ecs, eend))
        dG = jnp.zeros((L, L), F32)
        dB = jnp.zeros((L, SSM_STATE), F32)
        dC = jnp.zeros((L, SSM_STATE), F32)
        dcs_t = jnp.zeros((L, L), F32)
        tails = jnp.zeros((1, 128), F32)
        dD_row = jnp.zeros((1, 128), F32)
        v_parts, prod_parts = [], []

        def tot(m):
            return jnp.sum(jnp.sum(m, axis=0, keepdims=True), axis=1, keepdims=True)

        for pp in range(4):
            hA, hB = 2 * pp, 2 * pp + 1
            lanes = slice(pp * 128, (pp + 1) * 128)

            def sel1(m):
                return jnp.where(lo1, m[:, hA:hA + 1], m[:, hB:hB + 1])

            X = xs_ref[:, lanes]
            dY = dY_all[:, lanes]
            dtsel = dt_x[:, lanes]
            xd = X * dtsel
            xdb = xd.astype(BF16)
            dYb = dY.astype(BF16)
            Hp = st_ref[pp]
            Hb = Hp.astype(BF16)
            dHn = dh_ref[pp]
            dHb = dHn.astype(BF16)
            ecs_sel = ecs_x[:, lanes]
            eend_sel = eend_x[:, lanes]
            dxd_state = jnp.dot(Bb, dHb, preferred_element_type=F32) * eend_sel
            yoff = jnp.dot(Cb, Hb, preferred_element_type=F32) * ecs_sel
            dYe = (dY * ecs_sel).astype(BF16)
            dC = dC + lax.dot_general(dYe, Hb, _NT, preferred_element_type=F32)
            dB = dB + lax.dot_general((xd * eend_sel).astype(BF16), dHb, _NT, preferred_element_type=F32)
            dh_ref[pp] = dHn * sel1(dec) + lax.dot_general(Cb, dYe, _TN, preferred_element_type=F32)
            q = xd * dxd_state
            dyq = dY * yoff - q
            qcol = jnp.sum(q, axis=0, keepdims=True)
            hcol = jnp.sum(dHn * Hp, axis=0, keepdims=True)
            dxd_diag = []
            for h, msk, msk1 in ((hA, lo, lo1), (hB, jnp.logical_not(lo), jnp.logical_not(lo1))):
                Lm = jnp.where(tril, jnp.exp(jnp.minimum(cs[:, h:h + 1] - csT[h:h + 1, :], 0.0)), 0.0)
                M = G * Lm
                dxd_diag.append(lax.dot_general(M.astype(BF16), dYb, _TN, preferred_element_type=F32))
                dM = lax.dot_general(jnp.where(msk, dY, 0.0).astype(BF16), xdb, _NT, preferred_element_type=F32)
                dG = dG + dM * Lm
                W = dM * M
                dcs_t = dcs_t + jnp.where(row == h, jnp.sum(W, axis=0, keepdims=True), 0.0)
                v_parts.append(W + jnp.where(msk, dyq, 0.0))
                tail = (jnp.sum(jnp.where(msk1, qcol, 0.0), axis=1, keepdims=True)
                        + dec[:, h:h + 1] * jnp.sum(jnp.where(msk1, hcol, 0.0), axis=1, keepdims=True))
                tails = tails + jnp.where(lane1 == h, tail, 0.0)
                yield
            dxd = jnp.where(lo, dxd_diag[0], dxd_diag[1]) + dxd_state
            prod_parts.append(dxd * X)
            dxs_ref[:, lanes] = dxd * dtsel + sel1(dsk) * dY
            dyx = jnp.sum(dY * X, axis=0, keepdims=True)
            sA = jnp.sum(jnp.where(lo1, dyx, 0.0), axis=1, keepdims=True)
            sB = jnp.sum(dyx, axis=1, keepdims=True) - sA
            dD_row = dD_row + jnp.where(lane1 == hA, sA, 0.0) + jnp.where(lane1 == hB, sB, 0.0)
            yield
        dGb = dG.astype(BF16)
        db_ref[...] = dB + lax.dot_general(dGb, Cb, _TN, preferred_element_type=F32)
        dc_ref[...] = dC + jnp.dot(dGb, Bb, preferred_element_type=F32)
        dcs_mat = _lane_block_sums(jnp.concatenate(v_parts, axis=1), 128) + jnp.where(rowl == L - 1, tails, 0.0)
        ddt_mat = _lane_block_sums(jnp.concatenate(prod_parts, axis=1), 64)
        dad = _rcumsum_rows(dcs_mat - dcs_t.T)
        draw = (a * dad + ddt_mat) * _sigmoid(raw)
        ddt_ref[...] = draw
        dpar_ref[0:1, :] += jnp.sum(draw, axis=0, keepdims=True)
        dpar_ref[1:2, :] += jnp.sum(dt * dad, axis=0, keepdims=True) * a
        dpar_ref[2:3, :] += dD_row

    return pl.pallas_call(
        body, name=name, grid=(nc,), in_specs=in_specs,
        out_specs=[pl.BlockSpec((L, CONV_DIM), lambda c: (ci(c), 0)),
                   pl.BlockSpec((L, D_INNER), lambda c: (ci(c), 0)),
                   pl.BlockSpec((SSM_GROUPS, L, 128), lambda c: (0, ci(c), 0)),
                   pl.BlockSpec((1, D_INNER), lambda c: (0, 0)),
                   pl.BlockSpec((SSM_GROUPS, 8, 128), lambda c: (0, 0, 0))],
        out_shape=[jax.ShapeDtypeStruct((T, CONV_DIM), F32), jax.ShapeDtypeStruct((T, IN_PROJ_PAD), BF16),
                   jax.ShapeDtypeStruct((SSM_GROUPS, T, 128), F32), jax.ShapeDtypeStruct((1, D_INNER), F32),
                   jax.ShapeDtypeStruct((SSM_GROUPS, 8, 128), F32)],
        scratch_shapes=[pltpu.VMEM((SSM_GROUPS, 4, 128, 128), F32)],
        compiler_params=_cparams(("arbitrary",)))(xbc_c, xbc_c, xbc_c, dtg, par, zx, gnw, y, st, dyn)


SB_KEYS = 512
SB_SCAN = 256
SB_STRIP = 256


def _tri(width, cond):
    kk = lax.broadcasted_iota(jnp.int32, (width, width), 0)
    jj = lax.broadcasted_iota(jnp.int32, (width, width), 1)
    return cond(kk, jj).astype(BF16)


_LOG2E = 1.4426950408889634


def _softplus2(z2):
    return jnp.maximum(z2, 0.0) + jnp.log2(1.0 + jnp.exp2(-jnp.abs(z2)))


def _sba_sub_fwd(zb, c, U, mask):
    z2 = zb * _LOG2E
    s = _softplus2(z2)
    if mask is not None:
        s = jnp.where(mask, s, 0.0)
    R = c + jnp.dot(s.astype(BF16), U, preferred_element_type=F32)
    A = jnp.exp2(z2 - s - R)
    if mask is not None:
        A = jnp.where(mask, A, 0.0)
    return A.astype(BF16), R[:, 0:1] + s[:, 0:1]


def _sba_sub_bwd(zb, dAb, Lt, pc, pe, Uincl, Uexcl, mask):
    last = zb.shape[1] - 1
    z2 = zb * _LOG2E
    s = _softplus2(z2)
    g = z2 - s
    if mask is not None:
        s = jnp.where(mask, s, 0.0)
    P = pc + jnp.dot(s.astype(BF16), Uincl, preferred_element_type=F32)
    A = jnp.exp2(g - (Lt - P))
    if mask is not None:
        A = jnp.where(mask, A, 0.0)
    E = dAb * A
    PE = pe + jnp.dot(E.astype(BF16), Uexcl, preferred_element_type=F32)
    dz = E - jnp.exp2(g) * (E + PE)
    if mask is not None:
        dz = jnp.where(mask, dz, 0.0)
    return (A.astype(BF16), dz.astype(BF16), P[:, last:last + 1], PE[:, last:last + 1] + E[:, last:last + 1])


def _stack_heads(v):
    lo = lax.broadcasted_iota(jnp.int32, v.shape, 1) < 64
    zero = jnp.zeros_like(v)
    return jnp.concatenate([jnp.where(lo, v, zero), jnp.where(lo, zero, v)], axis=0)


def _unstack_heads(v):
    lo = lax.broadcasted_iota(jnp.int32, (SB_BLOCK, 128), 1) < 64
    return jnp.where(lo, v[:SB_BLOCK], v[SB_BLOCK:])


def _sba_rows(a):
    return slice(2 * a * SB_BLOCK, 2 * (a + 1) * SB_BLOCK)


def _sba_diag_case(a, b):
    Bq = SB_BLOCK
    if b * SB_SCAN >= (a + 1) * Bq:
        return "skip"
    if (b + 1) * SB_SCAN <= a * Bq:
        return "full"
    rowi = lax.broadcasted_iota(jnp.int32, (2 * Bq, SB_SCAN), 0)
    qpos = a * Bq + jnp.where(rowi >= Bq, rowi - Bq, rowi)
    return b * SB_SCAN + lax.broadcasted_iota(jnp.int32, (2 * Bq, SB_SCAN), 1) < qpos


def _sba_fwd(q, kv, *, name):
    T = q.shape[0]
    Bq = SB_BLOCK
    nsub = SB_KEYS // Bq
    nscan = SB_KEYS // SB_SCAN
    R = 2 * SB_KEYS
    assert T % SB_KEYS == 0 and SB_STRIP == 2 * Bq
    scale = 1.0 / math.sqrt(SB_HEAD_DIM)

    def body(q_ref, k_ref, v_ref, o_ref, lt_ref, z_s, a_s, c_s, acc_s):
        i = pl.program_id(1)
        U2 = _tri(SB_SCAN, lambda k, j: k > j)
        qs_all = jnp.concatenate([_stack_heads(q_ref[a * Bq:(a + 1) * Bq, :] * scale) for a in range(nsub)], axis=0)
        c_s[...] = jnp.zeros_like(c_s)
        acc_s[...] = jnp.zeros_like(acc_s)

        def scores(J, slot):
            off = pl.multiple_of(J * SB_KEYS, SB_KEYS)
            z_s[slot] = lax.dot_general(qs_all, k_ref[pl.ds(off, SB_KEYS), :], _NT, preferred_element_type=F32)

        def weights(slot, diag):
            for a in range(nsub):
                rows = _sba_rows(a)
                c = c_s[rows, :]
                for b in reversed(range(nscan)):
                    cols = slice(b * SB_SCAN, (b + 1) * SB_SCAN)
                    case = _sba_diag_case(a, b) if diag else "full"
                    if isinstance(case, str) and case == "skip":
                        a_s[slot, rows, cols] = jnp.zeros((2 * Bq, SB_SCAN), BF16)
                        continue
                    A, c = _sba_sub_fwd(z_s[slot, rows, cols], c, U2, None if isinstance(case, str) else case)
                    a_s[slot, rows, cols] = A
                c_s[rows, :] = c

        def values(J, slot):
            off = pl.multiple_of(J * SB_KEYS, SB_KEYS)
            acc_s[...] += jnp.dot(a_s[slot], v_ref[pl.ds(off, SB_KEYS), :], preferred_element_type=F32)

        scores(i, 0)
        weights(0, True)
        scores(jnp.maximum(i - 1, 0), 1)

        def two_steps(u, _):
            t = 2 * u + 1
            weights(1, False)
            scores(jnp.maximum(i - t - 1, 0), 0)
            values(i - t + 1, 0)
            weights(0, False)
            scores(jnp.maximum(i - t - 2, 0), 1)
            values(i - t, 1)
            return 0

        lax.fori_loop(0, i // 2, two_steps, 0)
        odd = lax.rem(i, 2) == 1

        @pl.when(jnp.logical_not(odd))
        def _():
            values(0, 0)

        @pl.when(odd)
        def _():
            weights(1, False)
            values(1, 0)
            values(0, 1)
        for a in range(nsub):
            o_ref[a * Bq:(a + 1) * Bq, :] = _unstack_heads(acc_s[_sba_rows(a), :]).astype(BF16)
            lt_ref[a * Bq:(a + 1) * Bq, :] = _unstack_heads(jnp.broadcast_to(c_s[_sba_rows(a), :], (2 * Bq, 128)))

    return pl.pallas_call(
        body, name=name, grid=(SB_HEADS // 2, T // SB_KEYS),
        in_specs=[pl.BlockSpec((SB_KEYS, 128), lambda p, i: (i, p)), pl.BlockSpec((T, 128), lambda p, i: (0, p)),
                  pl.BlockSpec((T, 128), lambda p, i: (0, p + SB_HEADS // 2))],
        out_specs=[pl.BlockSpec((SB_KEYS, 128), lambda p, i: (i, p)),
                   pl.BlockSpec((None, SB_KEYS, 128), lambda p, i: (p, i, 0))],
        out_shape=[jax.ShapeDtypeStruct((T, D_MODEL), BF16), jax.ShapeDtypeStruct((SB_HEADS // 2, T, 128), F32)],
        scratch_shapes=[pltpu.VMEM((2, R, SB_KEYS), F32), pltpu.VMEM((2, R, SB_KEYS), BF16),
                        pltpu.VMEM((R, 1), F32), pltpu.VMEM((R, 128), F32)],
        compiler_params=_cparams(("parallel", "parallel")))(q, kv, kv)


def _sba_bwd(q, kv, lt, do, *, name):
    T = q.shape[0]
    Bq = SB_BLOCK
    nq = T // SB_KEYS
    nsub = SB_KEYS // Bq
    nscan = SB_KEYS // SB_SCAN
    R = 2 * SB_KEYS
    assert T % SB_KEYS == 0 and SB_STRIP == 2 * Bq
    scale = 1.0 / math.sqrt(SB_HEAD_DIM)

    def body(q_ref, k_ref, v_ref, lt_ref, do_ref, dq_ref, dk_ref, dv_ref, dk_acc, dv_acc,
             z_s, da_s, a_s, dz_s, pc_s, pe_s, lt_s, dq_s):
        i = pl.program_id(1)

        @pl.when(i == 0)
        def _():
            dk_acc[...] = jnp.zeros_like(dk_acc)
            dv_acc[...] = jnp.zeros_like(dv_acc)

        Uincl = _tri(SB_SCAN, lambda k, j: k <= j)
        Uexcl = _tri(SB_SCAN, lambda k, j: k < j)
        qs, dos = [], []
        for a in range(nsub):
            rows = slice(a * Bq, (a + 1) * Bq)
            qs.append(_stack_heads(q_ref[rows, :] * scale))
            dos.append(_stack_heads(do_ref[rows, :]))
            lt_s[_sba_rows(a), :] = jnp.concatenate([lt_ref[rows, 0:1], lt_ref[rows, 64:65]], axis=0)
        qs_all = jnp.concatenate(qs, axis=0)
        dos_all = jnp.concatenate(dos, axis=0)
        pc_s[...] = jnp.zeros_like(pc_s)
        pe_s[...] = jnp.zeros_like(pe_s)
        a_s[1] = jnp.zeros((R, SB_KEYS), BF16)
        dz_s[1] = jnp.zeros((R, SB_KEYS), BF16)

        def scores(J, slot):
            off = pl.multiple_of(J * SB_KEYS, SB_KEYS)
            z_s[slot] = lax.dot_general(qs_all, k_ref[pl.ds(off, SB_KEYS), :], _NT, preferred_element_type=F32)
            da_s[slot] = lax.dot_general(dos_all, v_ref[pl.ds(off, SB_KEYS), :], _NT, preferred_element_type=F32)

        def gradients(slot, diag):
            for a in range(nsub):
                rows = _sba_rows(a)
                pc, pe, Lt = pc_s[rows, :], pe_s[rows, :], lt_s[rows, :]
                for b in range(nscan):
                    cols = slice(b * SB_SCAN, (b + 1) * SB_SCAN)
                    case = _sba_diag_case(a, b) if diag else "full"
                    if isinstance(case, str) and case == "skip":
                        a_s[slot, rows, cols] = jnp.zeros((2 * Bq, SB_SCAN), BF16)
                        dz_s[slot, rows, cols] = jnp.zeros((2 * Bq, SB_SCAN), BF16)
                        continue
                    A, dz, pc, pe = _sba_sub_bwd(z_s[slot, rows, cols], da_s[slot, rows, cols], Lt, pc, pe, Uincl, Uexcl,
                                                 None if isinstance(case, str) else case)
                    a_s[slot, rows, cols] = A
                    dz_s[slot, rows, cols] = dz
                pc_s[rows, :] = pc
                pe_s[rows, :] = pe

        def products(J, slot):
            off = pl.multiple_of(J * SB_KEYS, SB_KEYS)
            dzt = dz_s[slot]
            dk_acc[pl.ds(off, SB_KEYS), :] += lax.dot_general(dzt, qs_all, _TN, preferred_element_type=F32)
            dv_acc[pl.ds(off, SB_KEYS), :] += lax.dot_general(a_s[slot], dos_all, _TN, preferred_element_type=F32)
            dq_s[...] += jnp.dot(dzt, k_ref[pl.ds(off, SB_KEYS), :], preferred_element_type=F32)

        dq_s[...] = jnp.zeros_like(dq_s)
        scores(0, 0)

        def two_steps(u, _):
            t = 2 * u
            gradients(0, False)
            scores(t + 1, 1)
            products(jnp.maximum(t - 1, 0), 1)
            gradients(1, False)
            scores(t + 2, 0)
            products(t, 0)
            return 0

        lax.fori_loop(0, i // 2, two_steps, 0)
        odd = lax.rem(i, 2) == 1

        @pl.when(jnp.logical_not(odd))
        def _():
            gradients(0, True)
            products(jnp.maximum(i - 1, 0), 1)
            products(i, 0)

        @pl.when(odd)
        def _():
            gradients(0, False)
            scores(i, 1)
            products(jnp.maximum(i - 2, 0), 1)
            gradients(1, True)
            products(i - 1, 0)
            products(i, 1)

        for a in range(nsub):
            dq_ref[a * Bq:(a + 1) * Bq, :] = (_unstack_heads(dq_s[_sba_rows(a), :]) * scale).astype(BF16)

        @pl.when(i == nq - 1)
        def _():
            dk_ref[...] = dk_acc[...].astype(BF16)
            dv_ref[...] = dv_acc[...].astype(BF16)

    return pl.pallas_call(
        body, name=name, grid=(SB_HEADS // 2, nq),
        in_specs=[pl.BlockSpec((SB_KEYS, 128), lambda p, i: (i, p)), pl.BlockSpec((T, 128), lambda p, i: (0, p)),
                  pl.BlockSpec((T, 128), lambda p, i: (0, p + SB_HEADS // 2)),
                  pl.BlockSpec((None, SB_KEYS, 128), lambda p, i: (p, i, 0)),
                  pl.BlockSpec((SB_KEYS, 128), lambda p, i: (i, p))],
        out_specs=[pl.BlockSpec((SB_KEYS, 128), lambda p, i: (i, p)), pl.BlockSpec((T, 128), lambda p, i: (0, p)),
                   pl.BlockSpec((T, 128), lambda p, i: (0, p))],
        out_shape=[jax.ShapeDtypeStruct((T, D_MODEL), BF16), jax.ShapeDtypeStruct((T, D_MODEL), BF16),
                   jax.ShapeDtypeStruct((T, D_MODEL), BF16)],
        scratch_shapes=[pltpu.VMEM((T, 128), F32), pltpu.VMEM((T, 128), F32),
                        pltpu.VMEM((2, R, SB_KEYS), F32), pltpu.VMEM((2, R, SB_KEYS), F32),
                        pltpu.VMEM((2, R, SB_KEYS), BF16), pltpu.VMEM((2, R, SB_KEYS), BF16),
                        pltpu.VMEM((R, 1), F32), pltpu.VMEM((R, 1), F32), pltpu.VMEM((R, 1), F32),
                        pltpu.VMEM((R, 128), F32)],
        compiler_params=_cparams(("parallel", "arbitrary")))(q, kv, kv, lt, do)


def _loss_head(h, tgt, w, *, name, tt=512):
    T, D = h.shape
    tt = min(tt, T)

    def body(h_ref, t_ref, w_ref, loss_ref, dh_ref, dw_ref):
        i = pl.program_id(0)
        hv = h_ref[...]
        wv = w_ref[...]
        r = lax.rsqrt(jnp.mean(hv * hv, axis=-1, keepdims=True) + EPS)
        xhat = hv * r
        err = xhat * wv - t_ref[...]
        part = 0.5 * jnp.sum(jnp.mean(err * err, axis=-1, keepdims=True), axis=0, keepdims=True)
        dy = err * (1.0 / D)
        dxh = dy * wv
        dh_ref[...] = r * (dxh - xhat * jnp.mean(dxh * xhat, axis=-1, keepdims=True))
        dwc = jnp.sum(dy * xhat, axis=0, keepdims=True)

        @pl.when(i == 0)
        def _():
            loss_ref[...] = jnp.broadcast_to(part, loss_ref.shape)
            dw_ref[...] = dwc

        @pl.when(i > 0)
        def _():
            loss_ref[...] += jnp.broadcast_to(part, loss_ref.shape)
            dw_ref[...] += dwc

    return pl.pallas_call(
        body, name=name, grid=(T // tt,),
        in_specs=[pl.BlockSpec((tt, D), lambda i: (i, 0)), pl.BlockSpec((tt, D), lambda i: (i, 0)),
                  pl.BlockSpec((1, D), lambda i: (0, 0))],
        out_specs=[pl.BlockSpec((1, 128), lambda i: (0, 0)), pl.BlockSpec((tt, D), lambda i: (i, 0)),
                   pl.BlockSpec((1, D), lambda i: (0, 0))],
        out_shape=[jax.ShapeDtypeStruct((1, 128), F32), jax.ShapeDtypeStruct((T, D), F32),
                   jax.ShapeDtypeStruct((1, D), F32)],
        compiler_params=_cparams(("arbitrary",)))(h, tgt, w.reshape(1, D))


def _adamw(parts, w, m, v, *, name, tr=256, tc=None):
    plist = list(parts) if isinstance(parts, (list, tuple)) else [parts]
    P, _, C = plist[0].shape
    R = sum(a.shape[1] for a in plist)
    tr = min(tr, R)
    tc = C if tc is None else tc
    assert all(a.shape[1] % tr == 0 for a in plist) and C % tc == 0, (name, R, C, tr, tc)
    nbs = [a.shape[1] // tr for a in plist]
    offs = [sum(nbs[:l]) for l in range(len(nbs))]
    c1 = 1.0 - ADAM_B1 ** ADAM_STEP
    c2 = 1.0 - ADAM_B2 ** ADAM_STEP

    def body(*refs):
        p_refs = refs[:len(plist)]
        w_ref, m_ref, v_ref, g_ref, d_ref, nm_ref, nv_ref = refs[len(plist):]
        i = pl.program_id(0)
        g = None
        for l, p_ref in enumerate(p_refs):
            gl = p_ref[0].astype(F32)
            for k in range(1, P):
                gl = gl + p_ref[k].astype(F32)
            g = gl if g is None else jnp.where(i >= offs[l], gl, g)
        mn = ADAM_B1 * m_ref[...] + (1.0 - ADAM_B1) * g
        vn = ADAM_B2 * v_ref[...] + (1.0 - ADAM_B2) * (g * g)
        g_ref[...] = g
        nm_ref[...] = mn
        nv_ref[...] = vn
        d_ref[...] = -ADAM_LR * ((mn / c1) / (jnp.sqrt(vn / c2) + ADAM_EPS) + ADAM_WD * w_ref[...])

    spec = pl.BlockSpec((tr, tc), lambda i, j: (i, j))
    sds = jax.ShapeDtypeStruct((R, C), F32)
    return pl.pallas_call(
        body, name=name, grid=(R // tr, C // tc),
        in_specs=[pl.BlockSpec((P, tr, tc), functools.partial(lambda i, j, o, n: (0, jnp.clip(i - o, 0, n - 1), j), o=o, n=n))
                  for o, n in zip(offs, nbs)] + [spec, spec, spec],
        out_specs=[spec, spec, spec, spec], out_shape=[sds, sds, sds, sds],
        compiler_params=_cparams(("parallel", "parallel")))(*plist, w, m, v)


def _all_gather(shards, *, name):
    n = len(shards)

    def body(*refs):
        ins, outs = refs[:n], refs[n:2 * n]
        send_sems, recv_sems, local_sems = refs[2 * n:]
        x, y, c = lax.axis_index("x"), lax.axis_index("y"), lax.axis_index("c")
        me, sib = (x, y, c), (x, y, 1 - c)
        chips = [(1 - x, y), (x, 1 - y), (1 - x, 1 - y)]

        def slot(p):
            return 4 * p[0] + 2 * p[1] + p[2]

        def cp(a, k, block, to, src=None):
            dst = outs[a].at[slot(block)]
            return pltpu.make_async_remote_copy(src_ref=dst if src is None else src, dst_ref=dst,
                                                send_sem=send_sems.at[a, k], recv_sem=recv_sems.at[a, k],
                                                device_id=to, device_id_type=_MESH)

        mine = [pltpu.make_async_copy(ins[a], outs[a].at[slot(me)], local_sems.at[a]) for a in range(n)]
        for m in mine:
            m.start()
        first = []
        for a in range(n):
            first.append(cp(a, 0, me, sib, src=ins[a]))
            for j, chip in enumerate(chips):
                first.append(cp(a, 1 + j, me, (*chip, c), src=ins[a]))
        for f in first:
            f.start()
        passed = []
        for j, chip in enumerate(chips):
            for a in range(n):
                cp(a, 1 + j, (*chip, c), me).wait_recv()
                f = cp(a, 4 + j, (*chip, c), sib)
                f.start()
                passed.append(f)
        for a in range(n):
            cp(a, 0, sib, me).wait_recv()
            for j, chip in enumerate(chips):
                cp(a, 4 + j, (*chip, 1 - c), me).wait_recv()
        for f in first + passed:
            f.wait_send()
        for m in mine:
            m.wait()

    return pl.pallas_call(
        body, name=name, in_specs=[_ANY] * n, out_specs=[_ANY] * n,
        out_shape=[jax.ShapeDtypeStruct((N_DEV,) + s.shape, s.dtype) for s in shards],
        scratch_shapes=[pltpu.SemaphoreType.DMA((n, 7)), pltpu.SemaphoreType.DMA((n, 7)),
                        pltpu.SemaphoreType.DMA((n,))])(*shards)


_HBM = pl.BlockSpec(memory_space=pltpu.HBM)
_SEM = pl.BlockSpec(memory_space=pltpu.SEMAPHORE)
_EFFECT = pltpu.SideEffectType.DATAFLOW_SIDE_EFFECTING


def _peers():
    x, y, c = lax.axis_index("x"), lax.axis_index("y"), lax.axis_index("c")
    out = []
    for r in range(1, N_DEV):
        px = 1 - x if (r >> 2) & 1 else x
        py = 1 - y if (r >> 1) & 1 else y
        pc = 1 - c if r & 1 else c
        out.append(((px, py, pc), 4 * px + 2 * py + pc))
    return 4 * x + 2 * y + c, out


def _push_copy(src_ref, land_ref, send_sems, recv_sems, a, k, me, peer, peer_slot, scatter, arriving):
    src = src_ref.at[peer_slot] if scatter else src_ref
    return pltpu.make_async_remote_copy(
        src_ref=src, dst_ref=land_ref.at[peer_slot if arriving else me], send_sem=send_sems.at[a * (N_DEV - 1) + k],
        recv_sem=recv_sems.at[a * (N_DEV - 1) + k], device_id=peer, device_id_type=_MESH)


def _push_start(srcs, *, scatter, name):
    n = len(srcs)
    lands = [lax.empty(s.shape if scatter else (N_DEV,) + s.shape, s.dtype) for s in srcs]

    def body(*refs):
        src_refs, land_refs = refs[:n], refs[n:2 * n]
        send_sems, recv_sems = refs[2 * n], refs[2 * n + 1]
        token = refs[-1]
        me, peers = _peers()
        for k, (peer, slot) in enumerate(peers):
            for a in range(n):
                _push_copy(src_refs[a], land_refs[a], send_sems, recv_sems, a, k, me, peer, slot, scatter, False).start()
        token[...] = jnp.zeros_like(token)

    hbm = lambda a: pltpu.HBM(a.shape, a.dtype)
    outs = pl.pallas_call(
        body, name=name,
        out_shape=(pltpu.SemaphoreType.DMA((n * (N_DEV - 1),)), pltpu.SemaphoreType.DMA((n * (N_DEV - 1),)),
                   *[hbm(s) for s in srcs], *[hbm(l) for l in lands], jax.ShapeDtypeStruct((8, 128), F32)),
        in_specs=[_HBM] * (2 * n),
        out_specs=(_SEM, _SEM, *([_HBM] * (2 * n)), pl.BlockSpec(memory_space=pltpu.VMEM)),
        input_output_aliases={i: 2 + i for i in range(2 * n)},
        compiler_params=pltpu.CompilerParams(has_side_effects=_EFFECT),
    )(*[pltpu.with_memory_space_constraint(s, pltpu.HBM) for s in srcs],
      *[pltpu.with_memory_space_constraint(l, pltpu.HBM) for l in lands])
    return dict(send=outs[0], recv=outs[1], srcs=list(outs[2:2 + n]), lands=list(outs[2 + n:2 + 2 * n]),
                token=outs[-1], scatter=scatter, n=n)


def _push_wait(h, after, *, name):
    n, scatter = h["n"], h["scatter"]

    def body(*refs):
        src_refs, land_refs = refs[:n], refs[n:2 * n]
        send_sems, recv_sems = refs[2 * n], refs[2 * n + 1]
        me, peers = _peers()
        for k, (peer, slot) in enumerate(peers):
            for a in range(n):
                cp = _push_copy(src_refs[a], land_refs[a], send_sems, recv_sems, a, k, me, peer, slot, scatter, True)
                cp.wait_send()
                cp.wait_recv()

    hbm = lambda a: pltpu.HBM(a.shape, a.dtype)
    outs = pl.pallas_call(
        body, name=name,
        out_shape=(*[hbm(s) for s in h["srcs"]], *[hbm(l) for l in h["lands"]]),
        in_specs=[_HBM] * (2 * n) + [_SEM, _SEM, _ANY], out_specs=tuple([_HBM] * (2 * n)),
        input_output_aliases={i: i for i in range(2 * n)},
        compiler_params=pltpu.CompilerParams(has_side_effects=_EFFECT),
    )(*h["srcs"], *h["lands"], h["send"], h["recv"], after)
    return list(outs[:n]), list(outs[n:])


def _ffn_fwd(h, nw, w_up, conv_w, conv_b, w_down, tag):
    a3 = _mm_fwd(h, w_up, norm_w=nw, name=f"ffn{tag}_up", out_dtype=BF16, halves=True, w_t=True, tm=1024, tn=2816)
    p = _ffn_conv_fwd3(a3, conv_w, conv_b.reshape(1, -1), name=f"ffn{tag}_conv")
    h_out = _mm_fwd(p, w_down, residual=h, name=f"ffn{tag}_down", tm=1024, tn=1024)
    return h_out, (a3, p)


def _ffn_bwd(dh, h, saved, nw, w_up, conv_w, conv_b, w_down, tag):
    a3, p = saved
    g_down = _mm_tn(p, dh, name=f"ffn{tag}_down_wg", tk1=1408, tn=1024, tt=1024)
    dp = _mm_nt(dh, w_down, name=f"ffn{tag}_down_dg", out_dtype=BF16, tm=512, tn=2816, tk=1024)
    dhid3, dw3, db3 = _ffn_conv_bwd3(a3, conv_w, conv_b.reshape(1, -1), dp, name=f"ffn{tag}_conv_bwd")
    da3 = _conv_bwd_in3(dhid3, conv_w, K=FFN_CONV, name=f"ffn{tag}_conv_bwd_in")
    g_up = _mm_tn_t(da3, h, norm_w=nw, name=f"ffn{tag}_up_wg", tn=2816, tt=1024, vmem_mb=58)
    dh_out, g_nw = _mm_nt(da3, w_up, epi=(h, nw, dh), name=f"ffn{tag}_up_dg", w_t=True, tm=1024, tk=1408)
    g_cw = jnp.concatenate([dw3[0], dw3[1]], axis=1)
    g_cb = jnp.concatenate([db3[0], db3[1]], axis=1)
    return dh_out, dict(norm=g_nw.reshape(-1), up=g_up, conv_w=g_cw, conv_b=g_cb.reshape(-1), down=g_down)


_BIG = ["ssm_in_w", "ssm_out_w", "w_k", "w_v", "w_q", "w_o", "ffn_up_w", "ffn_down_w"]
_SMALL_SHARDED = ["ssm_norm_w", "ssm_conv_w", "ssm_conv_b", "ssm_gate_norm_w", "ffn_conv_w"]
_SMALL_REPL = ["ssm_dt_bias", "ssm_a_log", "ssm_d", "kv_norm_w", "attn_norm_w", "ffn_norm_w", "ffn_conv_b",
               "final_norm_w"]
_WEIGHTS = ["ssm_norm_w", "ssm_in_w", "ssm_conv_w", "ssm_conv_b", "ssm_dt_bias", "ssm_a_log", "ssm_d",
            "ssm_gate_norm_w", "ssm_out_w", "kv_norm_w", "w_k", "w_v", "attn_norm_w", "w_q", "w_o", "ffn_norm_w",
            "ffn_up_w", "ffn_conv_w", "ffn_conv_b", "ffn_down_w", "final_norm_w"]


def _as2d(a):
    return a.reshape(-1, a.shape[-1])


def _cols_to_full(g):
    return g.transpose(1, 0, 2).reshape(g.shape[1], N_DEV * g.shape[2])


def _pack_small(vals):
    flat = jnp.concatenate([v.reshape(-1).astype(F32) for v in vals])
    n = flat.shape[0]
    rows = -(-n // 1024) * 8
    return jnp.pad(flat, (0, rows * 128 - n)).reshape(rows, 128)


def _unpack_small(packed, shapes):
    flat = packed.reshape(-1)
    out, off = [], 0
    for s in shapes:
        n = math.prod(s)
        out.append(flat[off:off + n].reshape(s))
        off += n
    return out


def _tie(a, token):
    return a + token[0, 0].astype(a.dtype)


def _local_step(x, tgt, get_w, put_g):
    T = x.shape[0]
    Ws = get_w("ssm", None)
    fnw, fcw, fcb = Ws["ffn_norm_w"], Ws["ffn_conv_w"], Ws["ffn_conv_b"]
    zx = _mm_fwd(x, Ws["in_w"], norm_w=Ws["ssm_norm_w"], name="ssm_in", w_t=True, tm=1024, tn=1792)
    xbc_c = _ssm_conv_fwd(zx, Ws["ssm_conv_w"], Ws["ssm_conv_b"].reshape(1, -1), name="ssm_conv")
    dt_raw = zx[:, D_INNER + CONV_DIM:IN_PROJ_DIM]
    dtg = jnp.pad(dt_raw.reshape(T, SSM_GROUPS, 8).transpose(1, 0, 2), ((0, 0), (0, 0), (0, 120)))
    par = jnp.stack([Ws["ssm_dt_bias"].reshape(SSM_GROUPS, 8), Ws["ssm_a_log"].reshape(SSM_GROUPS, 8),
                     Ws["ssm_d"].reshape(SSM_GROUPS, 8)], axis=1)
    par = jnp.pad(par, ((0, 0), (0, 5), (0, 120)))
    gnw = _tie(Ws["ssm_gate_norm_w"].reshape(1, D_INNER), get_w("rest_start", xbc_c))
    y, yn, st = _ssd_fwd(xbc_c, zx, dtg, par, gnw, name="ssd_fwd")
    W0 = get_w("ffn0", y)
    Ws["ssm_out_w"] = W0["ssm_out_w"]
    h1 = _mm_fwd(yn, Ws["ssm_out_w"], residual=x, name="ssm_out", tm=1024, tn=1024)
    h2, ffn0 = _ffn_fwd(h1, fnw[0], W0["up"], fcw[0], fcb[0], W0["down"], "0")
    Wr = get_w("rest", h2)
    q = _mm_fwd(h2, Wr["w_q"], norm_w=Ws["attn_norm_w"], out_dtype=BF16, name="attn_q", tm=1024, tn=1024)
    kv = _mm_fwd(h2, Wr["w_kv"], norm_w=Ws["kv_norm_w"], out_dtype=BF16, name="attn_kv", tm=1024, tn=1024)
    o, lt = _sba_fwd(q, kv, name="sba_fwd")
    h3 = _mm_fwd(o, Wr["w_o"], residual=h2, name="attn_o", tm=1024, tn=1024)
    W1 = get_w("ffn1", h3)
    h4, ffn1 = _ffn_fwd(h3, fnw[1], W1["up"], fcw[1], fcb[1], W1["down"], "1")
    loss, dh4, g_final = _loss_head(h4, tgt, Ws["final_norm_w"], name="loss_head")
    dh3, gf1 = _ffn_bwd(dh4, h3, ffn1, fnw[1], W1["up"], fcw[1], fcb[1], W1["down"], "1")
    tok = put_g("ffn1", dict(up=gf1["up"], down=gf1["down"]))
    g_wo = _mm_tn(o, dh3, name="attn_o_wg", tn=1024, tt=1024)
    do = _mm_nt(dh3, _tie(Wr["w_o"], tok), name="attn_o_dg", out_dtype=BF16, tm=1024, tn=1024, tk=1024)
    dq, dk, dv = _sba_bwd(q, kv, lt, do, name="sba_bwd")
    g_wq = _mm_tn(h2, dq, norm_w=Ws["attn_norm_w"], name="attn_q_wg", tn=1024, tt=1024)
    dh2a, g_attn_nw = _mm_nt(dq, Wr["w_q"], epi=(h2, Ws["attn_norm_w"], dh3), name="attn_q_dg", tm=1024, tk=1024)
    dkv = jnp.concatenate([dk, dv], axis=1)
    g_wkv = _mm_tn(h2, dkv, norm_w=Ws["kv_norm_w"], name="attn_kv_wg", tn=1024, tt=1024)
    dh2, g_kv_nw = _mm_nt(dkv, Wr["w_kv"], epi=(h2, Ws["kv_norm_w"], dh2a), name="attn_kv_dg", tm=1024, tk=1024)
    tok = put_g("attn", dict(w_o=g_wo, w_q=g_wq, w_k=g_wkv[:, :D_MODEL], w_v=g_wkv[:, D_MODEL:]))
    dh1, gf0 = _ffn_bwd(dh2, h1, ffn0, fnw[0], W0["up"], fcw[0], _tie(fcb[0], tok), W0["down"], "0")
    tok = put_g("ffn0", dict(up=gf0["up"], down=gf0["down"]))
    g_out = _mm_tn(yn, dh1, name="ssm_out_wg", tn=1024, tt=1024)
    dyn = _mm_nt(dh1, _tie(Ws["ssm_out_w"], tok), name="ssm_out_dg", out_dtype=BF16, tm=1024, tn=1024, tk=1024)
    tok = put_g("ssm_out", dict(ssm_out_w=g_out))
    dxbc_c, dz, ddt, g_gnw, dpar = _ssd_bwd(xbc_c, zx, dtg, par, _tie(gnw, tok), y, st, dyn, name="ssd_bwd")
    dhid, g_scw, g_scb = _ssm_conv_bwd_pre(zx, Ws["ssm_conv_w"], Ws["ssm_conv_b"].reshape(1, -1), dxbc_c,
                                           name="ssm_conv_bwd")
    dzx = _conv_bwd_in(dhid, Ws["ssm_conv_w"], K=SSM_CONV, name="ssm_conv_bwd_in", into=(dz, D_INNER))
    ddt_t = ddt[:, :, :8].transpose(1, 0, 2).reshape(T, SSM_HEADS).astype(BF16)
    dzx = _put_cols(dzx, jnp.pad(ddt_t, ((0, 0), (0, IN_PROJ_PAD - IN_PROJ_DIM))), D_INNER + CONV_DIM, name="ssm_ddt_cols")
    g_in = _mm_tn_t(dzx, x, norm_w=Ws["ssm_norm_w"], name="ssm_in_wg", tn=1792, tt=1024)
    tok = put_g("ssm_in", dict(ssm_in_w=g_in[:IN_PROJ_DIM]))
    dx, g_ssm_nw = _mm_nt(dzx, Ws["in_w"], epi=(x, _tie(Ws["ssm_norm_w"], tok), dh1), name="ssm_in_dg", w_t=True,
                          tm=1024, tk=1792)
    f = {
        "ssm_norm_w": g_ssm_nw.reshape(-1), "ssm_conv_w": g_scw,
        "ssm_conv_b": g_scb.reshape(-1), "ssm_dt_bias": dpar[:, 0, :8].reshape(-1),
        "ssm_a_log": dpar[:, 1, :8].reshape(-1), "ssm_d": dpar[:, 2, :8].reshape(-1),
        "ssm_gate_norm_w": g_gnw.reshape(-1), "kv_norm_w": g_kv_nw.reshape(-1), "attn_norm_w": g_attn_nw.reshape(-1),
        "ffn_norm_w": jnp.stack([gf0["norm"], gf1["norm"]]), "ffn_conv_w": jnp.stack([gf0["conv_w"], gf1["conv_w"]]),
        "ffn_conv_b": jnp.stack([gf0["conv_b"], gf1["conv_b"]]), "final_norm_w": g_final.reshape(-1),
    }
    return loss, dx, f


def kernel(x, ssm_norm_w, ssm_in_w, ssm_conv_w, ssm_conv_b, ssm_dt_bias, ssm_a_log, ssm_d, ssm_gate_norm_w, ssm_out_w, kv_norm_w, w_k, w_v, attn_norm_w, w_q, w_o, ffn_norm_w, ffn_up_w, ffn_conv_w, ffn_conv_b, ffn_down_w, final_norm_w, loss_target, m_ssm_norm_w, m_ssm_in_w, m_ssm_conv_w, m_ssm_conv_b, m_ssm_dt_bias, m_ssm_a_log, m_ssm_d, m_ssm_gate_norm_w, m_ssm_out_w, m_kv_norm_w, m_w_k, m_w_v, m_attn_norm_w, m_w_q, m_w_o, m_ffn_norm_w, m_ffn_up_w, m_ffn_conv_w, m_ffn_conv_b, m_ffn_down_w, m_final_norm_w, v_ssm_norm_w, v_ssm_in_w, v_ssm_conv_w, v_ssm_conv_b, v_ssm_dt_bias, v_ssm_a_log, v_ssm_d, v_ssm_gate_norm_w, v_ssm_out_w, v_kv_norm_w, v_w_k, v_w_v, v_attn_norm_w, v_w_q, v_w_o, v_ffn_norm_w, v_ffn_up_w, v_ffn_conv_w, v_ffn_conv_b, v_ffn_down_w, v_final_norm_w):
    env = dict(locals())
    p = {n: env[n] for n in _WEIGHTS}
    mom = {n: env["m_" + n] for n in _WEIGHTS}
    var = {n: env["v_" + n] for n in _WEIGHTS}
    T = x.shape[1]
    me = 4 * lax.axis_index("x") + 2 * lax.axis_index("y") + lax.axis_index("c")
    rs = D_FF // N_DEV

    def bf2(a):
        return _as2d(a).astype(BF16)

    _T = ("ssm_in_w", "ffn_up_w")

    def t2d(a):
        return jnp.swapaxes(a, -1, -2).reshape(-1, a.shape[-2])

    def from_t2d(a, like):
        return jnp.swapaxes(a.reshape(like.shape[:-2] + (like.shape[-1], like.shape[-2])), -1, -2)

    n_in, n_up = p["ssm_in_w"].shape[-1], p["ffn_up_w"].shape[-1]

    def with_own(srcs, lands, scatter):
        out = []
        for s, l in zip(srcs, lands):
            own = lax.dynamic_index_in_dim(s, me, 0, keepdims=False) if scatter else s
            out.append(lax.dynamic_update_index_in_dim(l, own, me, 0))
        return out

    a_names = ["ssm_in_w"] + _SMALL_SHARDED
    got_a = dict(zip(a_names, _all_gather([t2d(p["ssm_in_w"]).astype(BF16)] + [_as2d(p[n]) for n in _SMALL_SHARDED],
                                          name="gather_ssm")))
    ffn0_names = ["ssm_out_w", "up0", "down0"]
    rest_names = ["w_q", "w_k", "w_v", "w_o"]
    up_t = jnp.swapaxes(p["ffn_up_w"], -1, -2).astype(BF16)
    shard = {"up0": up_t[0], "down0": bf2(p["ffn_down_w"][0]), "up1": up_t[1],
             "down1": bf2(p["ffn_down_w"][1]), "w_q": bf2(p["w_q"]), "w_k": bf2(p["w_k"]), "w_v": bf2(p["w_v"]),
             "w_o": bf2(p["w_o"]), "ssm_out_w": bf2(p["ssm_out_w"])}

    def anchored(a, on):
        return a + (jnp.where(jnp.isfinite(on), on, 0.0) * 0.0).astype(a.dtype)

    h_ffn0 = _push_start([anchored(shard[ffn0_names[0]], got_a["ssm_norm_w"][0, 0, 0])]
                         + [shard[n] for n in ffn0_names[1:]], scatter=False, name="gather_ffn0_start")
    handles = {}

    def get_w(group, after):
        if group == "ssm":
            W = {n: p[n] for n in _SMALL_REPL}
            for n in ("ssm_dt_bias", "ssm_a_log", "ssm_d", "attn_norm_w"):
                W[n] = W[n].reshape(-1)
            W["in_w"] = _stack_rows(got_a["ssm_in_w"], IN_PROJ_PAD, name="ssm_in_w_rows")
            W["ssm_norm_w"] = _tie(got_a["ssm_norm_w"].reshape(D_MODEL), h_ffn0["token"])
            W["ssm_conv_w"] = _cols_to_full(got_a["ssm_conv_w"])
            W["ssm_conv_b"] = got_a["ssm_conv_b"].reshape(CONV_DIM)
            W["ssm_gate_norm_w"] = got_a["ssm_gate_norm_w"].reshape(D_INNER)
            W["ffn_conv_w"] = _cols_to_full(got_a["ffn_conv_w"]).reshape(2, FFN_CONV, 2 * D_FF)
            return W
        if group == "rest_start":
            handles["rest"] = _push_start([anchored(shard[rest_names[0]], after[0, 0])]
                                          + [shard[n] for n in rest_names[1:]], scatter=False, name="gather_rest_start")
            handles["ffn1"] = _push_start([anchored(shard["up1"], handles["rest"]["token"][0, 0]), shard["down1"]],
                                          scatter=False, name="gather_ffn1_start")
            return handles["ffn1"]["token"]
        if group == "ffn1":
            srcs, lands = _push_wait(handles["ffn1"], after, name="gather_ffn1_wait")
            up, down = with_own(srcs, lands, False)
            return dict(up=up.reshape(2 * D_FF, D_MODEL), down=down.reshape(D_FF, D_MODEL))
        if group == "ffn0":
            srcs, lands = _push_wait(h_ffn0, after, name="gather_ffn0_wait")
            out, up, down = with_own(srcs, lands, False)
            return dict(ssm_out_w=out.reshape(D_INNER, D_MODEL), up=up.reshape(2 * D_FF, D_MODEL),
                        down=down.reshape(D_FF, D_MODEL))
        srcs, lands = _push_wait(handles["rest"], after, name="gather_rest_wait")
        g = dict(zip(rest_names, with_own(srcs, lands, False)))
        sq = lambda a: a.reshape(D_MODEL, D_MODEL)
        return dict(w_q=sq(g["w_q"]), w_kv=jnp.concatenate([sq(g["w_k"]), sq(g["w_v"])], axis=1), w_o=sq(g["w_o"]))

    pending = []

    def put_g(group, g):
        if group in ("ffn0", "ffn1"):
            keys = [("ffn_up_w", int(group[-1])), ("ffn_down_w", int(group[-1]))]
            blocks = [g["up"].reshape(N_DEV, n_up, D_MODEL), g["down"].reshape(N_DEV, rs, D_MODEL)]
        elif group == "attn":
            keys = [(n, None) for n in ("w_o", "w_q", "w_k", "w_v")]
            blocks = [g[n].reshape(N_DEV, D_MODEL // N_DEV, D_MODEL) for n, _ in keys]
        elif group == "ssm_out":
            keys = [("ssm_out_w", None)]
            blocks = [g["ssm_out_w"].reshape(N_DEV, D_INNER // N_DEV, D_MODEL)]
        else:
            keys = [("ssm_in_w", None)]
            blocks = [g["ssm_in_w"].reshape(N_DEV, n_in, D_MODEL)]
        h = _push_start(blocks, scatter=True, name=f"exchange_{group}_start")
        pending.append((group, keys, h))
        return h["token"]

    loss_row, dx, f = _local_step(x.reshape(T, D_MODEL), loss_target.reshape(T, D_MODEL), get_w, put_g)

    small_names = _SMALL_REPL + _SMALL_SHARDED
    small_full = _pack_small([f[n] for n in small_names] + [loss_row[0, 0:1]])
    small_bcast = jnp.broadcast_to(small_full[None], (N_DEV,) + small_full.shape)
    h_small = _push_start([small_bcast], scatter=True, name="exchange_small_start")
    tok = h_small["token"]

    arrived, res = {}, {}
    after = dx
    for group, keys, h in pending:
        srcs, lands = _push_wait(h, after, name=f"exchange_{group}_wait")
        arrived.update(zip(keys, with_own(srcs, lands, True)))
        for n in _BIG:
            layered = (n, 0) in arrived or (n, 1) in arrived
            if n in res or not ((n, None) in arrived or ((n, 0) in arrived and (n, 1) in arrived)):
                continue
            parts = [arrived[(n, 0)], arrived[(n, 1)]] if layered else arrived[(n, None)]
            w2, m2, v2 = ((t2d if n in _T else _as2d)(a[n]) for a in (p, mom, var))
            if not res:
                w2 = _tie(w2, tok)
            tiles = {"ffn_down_w": dict(tr=rs), "ffn_up_w": dict(tr=n_up // 2), "ssm_in_w": dict(tr=n_in, tc=256)}
            res[n] = _adamw(parts, w2, m2, v2, name=f"adamw_{n}", **tiles.get(n, dict(tr=256)))
            after = res[n][0]
    srcs, lands = _push_wait(h_small, after, name="exchange_small_wait")
    small_parts = with_own(srcs, lands, True)[0]
    out_g, out_d, out_m, out_v = {}, {}, {}, {}
    for n in _BIG:
        out_g[n], out_d[n], out_m[n], out_v[n] = (from_t2d(t, p[n]) if n in _T else t.reshape(p[n].shape) for t in res[n])

    zero = jnp.zeros_like(small_full)
    g_small_sum = _adamw(small_parts, zero, zero, zero, name="sum_small_grads", tr=small_full.shape[0])[0]
    *small_sums, loss_sum = _unpack_small(g_small_sum, [f[n].shape for n in small_names] + [(1,)])
    loss = loss_sum[0]
    g_small = dict(zip(small_names, small_sums))
    for n in _SMALL_SHARDED:
        width = p[n].shape[-1]
        g_small[n] = lax.dynamic_slice_in_dim(g_small[n], me * width, width, axis=g_small[n].ndim - 1)
    sw = _pack_small([p[n] for n in small_names])
    sm = _pack_small([mom[n] for n in small_names])
    sv = _pack_small([var[n] for n in small_names])
    sg = _pack_small([g_small[n] for n in small_names])
    _, d, nm, nv = _adamw(sg[None], sw, sm, sv, name="adamw_small", tr=sw.shape[0])
    shard_shapes = [p[n].shape for n in small_names]
    for n, dd, mm, vv in zip(small_names, _unpack_small(d, shard_shapes), _unpack_small(nm, shard_shapes),
                             _unpack_small(nv, shard_shapes)):
        out_g[n] = g_small[n].reshape(p[n].shape)
        out_d[n], out_m[n], out_v[n] = dd, mm, vv

    return (loss, dx.reshape(x.shape), *[out_g[n] for n in _WEIGHTS], *[out_d[n] for n in _WEIGHTS],
            *[out_m[n] for n in _WEIGHTS], *[out_v[n] for n in _WEIGHTS])
```

```python
import functools
import math

import jax
import jax.numpy as jnp
from jax import lax
from jax.experimental import pallas as pl
from jax.experimental.pallas import tpu as pltpu

F32 = jnp.float32
BF16 = jnp.bfloat16
EPS = 1e-6

D_MODEL = 1024
D_INNER = 2048
SSM_HEADS = 32
SSM_GROUPS = 4
SSM_STATE = 128
SSM_CONV = 4
SSM_CHUNK = 128
GN = SSM_GROUPS * SSM_STATE
CONV_DIM = D_INNER + 2 * GN
IN_PROJ_DIM = D_INNER + CONV_DIM + SSM_HEADS
IN_PROJ_PAD = 5376
SB_HEADS = 16
SB_HEAD_DIM = 64
SB_BLOCK = 128
D_FF = 2816
FFN_CONV = 3
N_DEV = 8

ADAM_LR = 0.001
ADAM_B1 = 0.9
ADAM_B2 = 0.999
ADAM_EPS = 1e-08
ADAM_WD = 0.01
ADAM_STEP = 10

_MESH = pl.DeviceIdType.MESH
_NT = (((1,), (1,)), ((), ()))
_TN = (((0,), (0,)), ((), ()))
_ANY = pl.BlockSpec(memory_space=pl.ANY)


def _cparams(sem, vmem_mb=48):
    return pltpu.CompilerParams(dimension_semantics=sem, vmem_limit_bytes=vmem_mb * 1024 * 1024)


def _sigmoid(x):
    return 0.5 * jnp.tanh(0.5 * x) + 0.5


def _softplus(x):
    return jnp.maximum(x, 0.0) + jnp.log(1.0 + jnp.exp(-jnp.abs(x)))


def _rms_fwd(xv, w):
    r = lax.rsqrt(jnp.mean(xv * xv, axis=-1, keepdims=True) + EPS)
    return xv * r * w


def _mm_fwd(x, w, *, name, norm_w=None, residual=None, out_dtype=F32, tm=512, tn=512, halves=False, w_t=False):
    M, K = x.shape
    N = w.shape[0] if w_t else w.shape[1]
    tm, tn = min(tm, M), min(tn, N)
    assert M % tm == 0 and N % tn == 0, (name, M, N, tm, tn)
    if halves:
        nbh = N // 2 // tn
        assert N // 2 % tn == 0
        out_spec = pl.BlockSpec((None, tm, tn), lambda i, j: (lax.div(j, nbh), i, lax.rem(j, nbh)))
        out_shape = jax.ShapeDtypeStruct((2, M, N // 2), out_dtype)
    else:
        out_spec = pl.BlockSpec((tm, tn), lambda i, j: (i, j))
        out_shape = jax.ShapeDtypeStruct((M, N), out_dtype)
    has_norm, has_res = norm_w is not None, residual is not None

    def body(*refs):
        x_ref, w_ref = refs[0], refs[1]
        p = 2
        nw_ref = r_ref = None
        if has_norm:
            nw_ref = refs[p]
            p += 1
        if has_res:
            r_ref = refs[p]
            p += 1
        o_ref = refs[p]
        xv = x_ref[...]
        if has_norm:
            xv = _rms_fwd(xv.astype(F32), nw_ref[...])
        acc = lax.dot_general(xv.astype(BF16), w_ref[...], _NT if w_t else (((1,), (0,)), ((), ())),
                              preferred_element_type=F32)
        if has_res:
            acc = acc + r_ref[...]
        o_ref[...] = acc.astype(out_dtype)

    w_spec = pl.BlockSpec((tn, K), lambda i, j: (j, 0)) if w_t else pl.BlockSpec((K, tn), lambda i, j: (0, j))
    in_specs = [pl.BlockSpec((tm, K), lambda i, j: (i, 0)), w_spec]
    args = [x, w]
    if has_norm:
        in_specs.append(pl.BlockSpec((1, K), lambda i, j: (0, 0)))
        args.append(norm_w.reshape(1, K))
    if has_res:
        in_specs.append(pl.BlockSpec((tm, tn), lambda i, j: (i, j)))
        args.append(residual)
    return pl.pallas_call(
        body, name=name, grid=(M // tm, N // tn), in_specs=in_specs,
        out_specs=out_spec, out_shape=out_shape,
        compiler_params=_cparams(("parallel", "parallel")))(*args)


def _mm_nt(dy, w, *, name, epi=None, out_dtype=F32, tm=512, tn=512, tk=512, w_t=False):
    halves = dy.ndim == 3
    M, K = (dy.shape[1], 2 * dy.shape[2]) if halves else dy.shape
    N = w.shape[1] if w_t else w.shape[0]
    tm, tk = min(tm, M), min(tk, K)
    tn = N if epi is not None else min(tn, N)
    assert M % tm == 0 and N % tn == 0 and K % tk == 0, (name, M, N, K, tm, tn, tk)
    nk = K // tk
    has_epi = epi is not None

    def body(*refs):
        if has_epi:
            dy_ref, w_ref, h_ref, nw_ref, r_ref, o_ref, dnw_ref, acc_ref = refs
        else:
            dy_ref, w_ref, o_ref, acc_ref = refs
        i = pl.program_id(0)
        k = pl.program_id(2)

        @pl.when(k == 0)
        def _():
            acc_ref[...] = jnp.zeros_like(acc_ref)

        acc_ref[...] += lax.dot_general(dy_ref[...].astype(BF16), w_ref[...], (((1,), (0,)), ((), ())) if w_t else _NT,
                                        preferred_element_type=F32)

        @pl.when(k == nk - 1)
        def _():
            du = acc_ref[...]
            if has_epi:
                hv = h_ref[...]
                r = lax.rsqrt(jnp.mean(hv * hv, axis=-1, keepdims=True) + EPS)
                xhat = hv * r
                dxh = du * nw_ref[...]
                dx = r * (dxh - xhat * jnp.mean(dxh * xhat, axis=-1, keepdims=True))
                o_ref[...] = (r_ref[...] + dx).astype(out_dtype)
                contrib = jnp.sum(du * xhat, axis=0, keepdims=True)

                @pl.when(i == 0)
                def _():
                    dnw_ref[...] = contrib

                @pl.when(i > 0)
                def _():
                    dnw_ref[...] += contrib
            else:
                o_ref[...] = du.astype(out_dtype)

    if halves:
        nkh = K // 2 // tk
        assert K // 2 % tk == 0
        dy_spec = pl.BlockSpec((None, tm, tk), lambda i, j, k: (lax.div(k, nkh), i, lax.rem(k, nkh)))
    else:
        dy_spec = pl.BlockSpec((tm, tk), lambda i, j, k: (i, k))
    w_spec = pl.BlockSpec((tk, tn), lambda i, j, k: (k, j)) if w_t else pl.BlockSpec((tn, tk), lambda i, j, k: (j, k))
    in_specs = [dy_spec, w_spec]
    args = [dy, w]
    out_specs = [pl.BlockSpec((tm, tn), lambda i, j, k: (i, j))]
    out_shape = [jax.ShapeDtypeStruct((M, N), out_dtype)]
    if has_epi:
        h, nw, res = epi
        in_specs += [pl.BlockSpec((tm, N), lambda i, j, k: (i, 0)), pl.BlockSpec((1, N), lambda i, j, k: (0, 0)),
                     pl.BlockSpec((tm, N), lambda i, j, k: (i, 0))]
        args += [h, nw.reshape(1, N), res]
        out_specs.append(pl.BlockSpec((1, N), lambda i, j, k: (0, 0)))
        out_shape.append(jax.ShapeDtypeStruct((1, N), F32))
    outs = pl.pallas_call(
        body, name=name, grid=(M // tm, N // tn, nk), in_specs=in_specs, out_specs=out_specs, out_shape=out_shape,
        scratch_shapes=[pltpu.VMEM((tm, tn), F32)],
        compiler_params=_cparams(("arbitrary", "arbitrary", "arbitrary")))(*args)
    return (outs[0], outs[1]) if has_epi else outs[0]


def _mm_tn(x, dy, *, name, norm_w=None, out_dtype=BF16, tk1=1024, tn=512, tt=512):
    T, K1 = x.shape
    halves = dy.ndim == 3
    N = 2 * dy.shape[2] if halves else dy.shape[1]
    tk1, tn, tt = min(tk1, K1), min(tn, N), min(tt, T)
    has_norm = norm_w is not None
    assert K1 % tk1 == 0 and N % tn == 0 and T % tt == 0, (name, K1, N, T, tk1, tn, tt)
    assert not has_norm or tk1 == K1
    nt = T // tt

    def body(*refs):
        if has_norm:
            x_ref, dy_ref, nw_ref, o_ref, acc_ref = refs
        else:
            x_ref, dy_ref, o_ref, acc_ref = refs
        t = pl.program_id(2)

        @pl.when(t == 0)
        def _():
            acc_ref[...] = jnp.zeros_like(acc_ref)

        xv = x_ref[...]
        if has_norm:
            xv = _rms_fwd(xv.astype(F32), nw_ref[...])
        acc_ref[...] += lax.dot_general(xv.astype(BF16), dy_ref[...].astype(BF16), _TN, preferred_element_type=F32)

        @pl.when(t == nt - 1)
        def _():
            o_ref[...] = acc_ref[...].astype(out_dtype)

    if halves:
        nbh = N // 2 // tn
        assert N // 2 % tn == 0
        dy_spec = pl.BlockSpec((None, tt, tn), lambda a, b, t: (lax.div(b, nbh), t, lax.rem(b, nbh)))
    else:
        dy_spec = pl.BlockSpec((tt, tn), lambda a, b, t: (t, b))
    in_specs = [pl.BlockSpec((tt, tk1), lambda a, b, t: (t, a)), dy_spec]
    args = [x, dy]
    if has_norm:
        in_specs.append(pl.BlockSpec((1, K1), lambda a, b, t: (0, 0)))
        args.append(norm_w.reshape(1, K1))
    return pl.pallas_call(
        body, name=name, grid=(K1 // tk1, N // tn, nt), in_specs=in_specs,
        out_specs=pl.BlockSpec((tk1, tn), lambda a, b, t: (a, b)),
        out_shape=jax.ShapeDtypeStruct((K1, N), out_dtype),
        scratch_shapes=[pltpu.VMEM((tk1, tn), F32)],
        compiler_params=_cparams(("parallel", "parallel", "arbitrary")))(*args)


def _mm_tn_t(dy, x, *, name, norm_w, out_dtype=BF16, tn=1408, tt=1024, vmem_mb=48):
    T, K1 = x.shape
    halves = dy.ndim == 3
    N = 2 * dy.shape[2] if halves else dy.shape[1]
    tn, tt = min(tn, N), min(tt, T)
    assert N % tn == 0 and T % tt == 0, (name, N, T, tn, tt)
    nt = T // tt

    def body(dy_ref, x_ref, nw_ref, o_ref, acc_ref):
        t = pl.program_id(1)

        @pl.when(t == 0)
        def _():
            acc_ref[...] = jnp.zeros_like(acc_ref)

        xn = _rms_fwd(x_ref[...].astype(F32), nw_ref[...]).astype(BF16)
        acc_ref[...] += lax.dot_general(dy_ref[...].astype(BF16), xn, _TN, preferred_element_type=F32)

        @pl.when(t == nt - 1)
        def _():
            o_ref[...] = acc_ref[...].astype(out_dtype)

    if halves:
        nbh = N // 2 // tn
        assert N // 2 % tn == 0
        dy_spec = pl.BlockSpec((None, tt, tn), lambda b, t: (lax.div(b, nbh), t, lax.rem(b, nbh)))
    else:
        dy_spec = pl.BlockSpec((tt, tn), lambda b, t: (t, b))
    return pl.pallas_call(
        body, name=name, grid=(N // tn, nt),
        in_specs=[dy_spec, pl.BlockSpec((tt, K1), lambda b, t: (t, 0)), pl.BlockSpec((1, K1), lambda b, t: (0, 0))],
        out_specs=pl.BlockSpec((tn, K1), lambda b, t: (b, 0)),
        out_shape=jax.ShapeDtypeStruct((N, K1), out_dtype),
        scratch_shapes=[pltpu.VMEM((tn, K1), F32)],
        compiler_params=_cparams(("parallel", "arbitrary"), vmem_mb))(dy, x, norm_w.reshape(1, K1))


def _shift_down(xb, prev8, j):
    main = pltpu.roll(xb, j, 0)
    head = pltpu.roll(xb[0:8], j, 0)
    ph = pltpu.roll(prev8, j, 0)
    row8 = lax.broadcasted_iota(jnp.int32, head.shape, 0)
    head = jnp.where(row8 < j, ph, head)
    return jnp.concatenate([head, main[8:]], axis=0)


def _shift_up(xb, next8, j):
    tt = xb.shape[0]
    main = pltpu.roll(xb, tt - j, 0)
    tail = pltpu.roll(xb[tt - 8:tt], 8 - j, 0)
    nh = pltpu.roll(next8, 8 - j, 0)
    row8 = lax.broadcasted_iota(jnp.int32, tail.shape, 0)
    tail = jnp.where(row8 + j >= 8, nh, tail)
    return jnp.concatenate([main[:tt - 8], tail], axis=0)


def _conv_hid(xb, prev8, w, b_row, K):
    out = b_row
    shifted = []
    for j in range(K):
        sh = K - 1 - j
        xs = xb if sh == 0 else _shift_down(xb, prev8, sh)
        shifted.append(xs)
        out = out + xs * w[j:j + 1, :]
    return out, shifted


def _prev_idx(i, nb8):
    return jnp.maximum(i * nb8 - 1, 0)


def _ssm_conv_fwd(zx, w, b, *, name, tt=512, tc=512):
    T = zx.shape[0]
    tt = min(tt, T)
    C, K = CONV_DIM, SSM_CONV
    cb0, nb8 = D_INNER // tc, tt // 8

    def body(x_ref, p_ref, w_ref, b_ref, o_ref):
        first = (pl.program_id(1) > 0).astype(F32)
        hid, _ = _conv_hid(x_ref[...], p_ref[...] * first, w_ref[...], b_ref[...], K)
        o_ref[...] = hid * _sigmoid(hid)

    return pl.pallas_call(
        body, name=name, grid=(C // tc, T // tt),
        in_specs=[pl.BlockSpec((tt, tc), lambda c, i: (i, c + cb0)),
                  pl.BlockSpec((8, tc), lambda c, i: (_prev_idx(i, nb8), c + cb0)),
                  pl.BlockSpec((K, tc), lambda c, i: (0, c)), pl.BlockSpec((1, tc), lambda c, i: (0, c))],
        out_specs=pl.BlockSpec((tt, tc), lambda c, i: (i, c)),
        out_shape=jax.ShapeDtypeStruct((T, C), F32),
        compiler_params=_cparams(("parallel", "parallel")))(zx, zx, w, b)


def _ssm_conv_bwd_pre(zx, w, b, dout, *, name, tt=512, tc=512):
    T = zx.shape[0]
    tt = min(tt, T)
    C, K = CONV_DIM, SSM_CONV
    cb0, nb8 = D_INNER // tc, tt // 8

    def body(x_ref, p_ref, w_ref, b_ref, d_ref, dh_ref, dw_ref, db_ref):
        t = pl.program_id(1)
        first = (t > 0).astype(F32)
        hid, shifted = _conv_hid(x_ref[...], p_ref[...] * first, w_ref[...], b_ref[...], K)
        sg = _sigmoid(hid)
        dh = d_ref[...] * (sg * (1.0 + hid * (1.0 - sg)))
        dh_ref[...] = dh

        @pl.when(t == 0)
        def _():
            dw_ref[...] = jnp.zeros_like(dw_ref)
            db_ref[...] = jnp.zeros_like(db_ref)

        db_ref[...] += jnp.sum(dh, axis=0, keepdims=True)
        for j in range(K):
            dw_ref[j:j + 1, :] += jnp.sum(dh * shifted[j], axis=0, keepdims=True)

    return pl.pallas_call(
        body, name=name, grid=(C // tc, T // tt),
        in_specs=[pl.BlockSpec((tt, tc), lambda c, i: (i, c + cb0)),
                  pl.BlockSpec((8, tc), lambda c, i: (_prev_idx(i, nb8), c + cb0)),
                  pl.BlockSpec((K, tc), lambda c, i: (0, c)), pl.BlockSpec((1, tc), lambda c, i: (0, c)),
                  pl.BlockSpec((tt, tc), lambda c, i: (i, c))],
        out_specs=[pl.BlockSpec((tt, tc), lambda c, i: (i, c)), pl.BlockSpec((K, tc), lambda c, i: (0, c)),
                   pl.BlockSpec((1, tc), lambda c, i: (0, c))],
        out_shape=[jax.ShapeDtypeStruct((T, C), F32), jax.ShapeDtypeStruct((K, C), F32),
                   jax.ShapeDtypeStruct((1, C), F32)],
        compiler_params=_cparams(("parallel", "arbitrary")))(zx, zx, w, b, dout)


def _put_cols(buf, src, col0, *, name, tt=512):
    T, C = src.shape
    tt = min(tt, T)

    def body(s_ref, _, o_ref):
        o_ref[...] = s_ref[...]

    return pl.pallas_call(
        body, name=name, grid=(T // tt,),
        in_specs=[pl.BlockSpec((tt, C), lambda i: (i, 0)), _ANY],
        out_specs=pl.BlockSpec((tt, C), lambda i: (i, col0 // C)),
        out_shape=jax.ShapeDtypeStruct(buf.shape, buf.dtype), input_output_aliases={1: 0},
        compiler_params=_cparams(("parallel",)))(src, buf)


def _stack_rows(g, rows, *, name, tc=256):
    n, r, C = g.shape

    def body(g_ref, o_ref):
        for j in range(n):
            o_ref[j * r:(j + 1) * r, :] = g_ref[j]
        o_ref[n * r:, :] = jnp.zeros((rows - n * r, tc), g.dtype)

    return pl.pallas_call(
        body, name=name, grid=(C // tc,),
        in_specs=[pl.BlockSpec((n, r, tc), lambda i: (0, 0, i))],
        out_specs=pl.BlockSpec((rows, tc), lambda i: (0, i)),
        out_shape=jax.ShapeDtypeStruct((rows, C), g.dtype),
        compiler_params=_cparams(("parallel",)))(g)


def _unstack_rows(g, n, r, *, name, tc=256):
    rows, C = g.shape

    def body(g_ref, o_ref):
        for j in range(n):
            o_ref[j] = g_ref[j * r:(j + 1) * r, :]

    return pl.pallas_call(
        body, name=name, grid=(C // tc,),
        in_specs=[pl.BlockSpec((rows, tc), lambda i: (0, i))],
        out_specs=pl.BlockSpec((n, r, tc), lambda i: (0, 0, i)),
        out_shape=jax.ShapeDtypeStruct((n, r, C), g.dtype),
        compiler_params=_cparams(("parallel",)))(g)


def _conv_bwd_in(dh, w, *, name, K, tt=512, tc=512, out_dtype=BF16, into=None):
    T, C = dh.shape
    tt = min(tt, T)
    nb8, nT = tt // 8, T // tt
    last8 = T // 8 - 1
    cb0 = 0 if into is None else into[1] // tc

    def body(d_ref, n_ref, w_ref, *rest):
        o_ref = rest[-1]
        notlast = (pl.program_id(1) < nT - 1).astype(F32)
        d = d_ref[...]
        nxt = n_ref[...] * notlast
        w_ = w_ref[...]
        acc = d * w_[K - 1:K, :]
        for sh in range(1, K):
            acc = acc + _shift_up(d, nxt, sh) * w_[K - 1 - sh:K - sh, :]
        o_ref[...] = acc.astype(out_dtype)

    in_specs = [pl.BlockSpec((tt, tc), lambda c, i: (i, c)),
                pl.BlockSpec((8, tc), lambda c, i: (jnp.minimum((i + 1) * nb8, last8), c)),
                pl.BlockSpec((K, tc), lambda c, i: (0, c))]
    args = [dh, dh, w]
    if into is None:
        out_shape, alias = jax.ShapeDtypeStruct((T, C), out_dtype), {}
    else:
        assert into[0].dtype == out_dtype and into[1] % tc == 0
        in_specs.append(_ANY)
        args.append(into[0])
        out_shape, alias = jax.ShapeDtypeStruct(into[0].shape, out_dtype), {3: 0}
    return pl.pallas_call(
        body, name=name, grid=(C // tc, nT), in_specs=in_specs,
        out_specs=pl.BlockSpec((tt, tc), lambda c, i: (i, c + cb0)),
        out_shape=out_shape, input_output_aliases=alias,
        compiler_params=_cparams(("parallel", "parallel")))(*args)


def _ffn_conv_fwd3(a3, w, b, *, name, tt=256, tc=1408):
    T = a3.shape[1]
    tt = min(tt, T)
    K, nbh, n16 = FFN_CONV, D_FF // tc, tt // 16

    def body(a_ref, p_ref, wg_ref, wv_ref, bg_ref, bv_ref, o_ref):
        first = (pl.program_id(1) > 0).astype(F32)
        a = a_ref[...].astype(F32)
        prev = p_ref[...].astype(F32)[:, 8:16, :] * first
        hg, _ = _conv_hid(a[0], prev[0], wg_ref[...], bg_ref[...], K)
        hv, _ = _conv_hid(a[1], prev[1], wv_ref[...], bv_ref[...], K)
        o_ref[...] = (hg * _sigmoid(hg) * hv).astype(BF16)

    return pl.pallas_call(
        body, name=name, grid=(nbh, T // tt),
        in_specs=[pl.BlockSpec((2, tt, tc), lambda c, i: (0, i, c)),
                  pl.BlockSpec((2, 16, tc), lambda c, i: (0, _prev_idx(i, n16), c)),
                  pl.BlockSpec((K, tc), lambda c, i: (0, c)), pl.BlockSpec((K, tc), lambda c, i: (0, c + nbh)),
                  pl.BlockSpec((1, tc), lambda c, i: (0, c)), pl.BlockSpec((1, tc), lambda c, i: (0, c + nbh))],
        out_specs=pl.BlockSpec((tt, tc), lambda c, i: (i, c)),
        out_shape=jax.ShapeDtypeStruct((T, D_FF), BF16),
        compiler_params=_cparams(("parallel", "parallel")))(a3, a3, w, w, b, b)


def _ffn_conv_bwd3(a3, w, b, dp, *, name, tt=256, tc=1408):
    T = a3.shape[1]
    tt = min(tt, T)
    K, nbh, n16 = FFN_CONV, D_FF // tc, tt // 16

    def body(a_ref, p_ref, wg_ref, wv_ref, bg_ref, bv_ref, dp_ref, dh_ref, dw_ref, db_ref):
        t = pl.program_id(1)
        first = (t > 0).astype(F32)
        a = a_ref[...].astype(F32)
        prev = p_ref[...].astype(F32)[:, 8:16, :] * first
        hg, sh_g = _conv_hid(a[0], prev[0], wg_ref[...], bg_ref[...], K)
        hv, sh_v = _conv_hid(a[1], prev[1], wv_ref[...], bv_ref[...], K)
        sg = _sigmoid(hg)
        d = dp_ref[...].astype(F32)
        dhg = d * hv * (sg * (1.0 + hg * (1.0 - sg)))
        dhv = d * (hg * sg)
        dh_ref[0] = dhg.astype(BF16)
        dh_ref[1] = dhv.astype(BF16)

        @pl.when(t == 0)
        def _():
            dw_ref[...] = jnp.zeros_like(dw_ref)
            db_ref[...] = jnp.zeros_like(db_ref)

        db_ref[0] += jnp.sum(dhg, axis=0, keepdims=True)
        db_ref[1] += jnp.sum(dhv, axis=0, keepdims=True)
        for j in range(K):
            dw_ref[0, j:j + 1, :] += jnp.sum(dhg * sh_g[j], axis=0, keepdims=True)
            dw_ref[1, j:j + 1, :] += jnp.sum(dhv * sh_v[j], axis=0, keepdims=True)

    return pl.pallas_call(
        body, name=name, grid=(nbh, T // tt),
        in_specs=[pl.BlockSpec((2, tt, tc), lambda c, i: (0, i, c)),
                  pl.BlockSpec((2, 16, tc), lambda c, i: (0, _prev_idx(i, n16), c)),
                  pl.BlockSpec((K, tc), lambda c, i: (0, c)), pl.BlockSpec((K, tc), lambda c, i: (0, c + nbh)),
                  pl.BlockSpec((1, tc), lambda c, i: (0, c)), pl.BlockSpec((1, tc), lambda c, i: (0, c + nbh)),
                  pl.BlockSpec((tt, tc), lambda c, i: (i, c))],
        out_specs=[pl.BlockSpec((2, tt, tc), lambda c, i: (0, i, c)), pl.BlockSpec((2, K, tc), lambda c, i: (0, 0, c)),
                   pl.BlockSpec((2, 1, tc), lambda c, i: (0, 0, c))],
        out_shape=[jax.ShapeDtypeStruct((2, T, D_FF), BF16), jax.ShapeDtypeStruct((2, K, D_FF), F32),
                   jax.ShapeDtypeStruct((2, 1, D_FF), F32)],
        compiler_params=_cparams(("parallel", "arbitrary")))(a3, a3, w, w, b, b, dp)


def _conv_bwd_in3(dh3, w, *, name, K, tt=256, tc=1408):
    H, T, C = dh3.shape
    tt = min(tt, T)
    nb, n16, nT = C // tc, tt // 16, T // tt
    last16 = T // 16 - 1

    def body(d_ref, n_ref, w_ref, o_ref):
        notlast = (pl.program_id(2) < nT - 1).astype(F32)
        d = d_ref[...].astype(F32)
        nxt = n_ref[...].astype(F32)[0:8, :] * notlast
        w_ = w_ref[...]
        acc = d * w_[K - 1:K, :]
        for sh in range(1, K):
            acc = acc + _shift_up(d, nxt, sh) * w_[K - 1 - sh:K - sh, :]
        o_ref[...] = acc.astype(BF16)

    return pl.pallas_call(
        body, name=name, grid=(H, nb, nT),
        in_specs=[pl.BlockSpec((None, tt, tc), lambda h, c, i: (h, i, c)),
                  pl.BlockSpec((None, 16, tc), lambda h, c, i: (h, jnp.minimum((i + 1) * n16, last16), c)),
                  pl.BlockSpec((K, tc), lambda h, c, i: (0, h * nb + c))],
        out_specs=pl.BlockSpec((None, tt, tc), lambda h, c, i: (h, i, c)),
        out_shape=jax.ShapeDtypeStruct((H, T, C), BF16),
        compiler_params=_cparams(("parallel", "parallel", "parallel")))(dh3, dh3, w)


def _cumsum_rows(x):
    L = x.shape[0]
    row = lax.broadcasted_iota(jnp.int32, x.shape, 0)
    k = 1
    while k < L:
        x = x + jnp.where(row >= k, pltpu.roll(x, k, 0), 0.0)
        k *= 2
    return x


def _rcumsum_rows(x):
    L = x.shape[0]
    row = lax.broadcasted_iota(jnp.int32, x.shape, 0)
    k = 1
    while k < L:
        x = x + jnp.where(row < L - k, pltpu.roll(x, L - k, 0), 0.0)
        k *= 2
    return x


def _split_terms(m, n):
    terms, rest = [], m
    for _ in range(n):
        t = rest.astype(BF16)
        terms.append(t)
        rest = rest - t.astype(F32)
    return jnp.concatenate(terms, axis=1)


def _select_dot(m, n_terms, n_out, cond):
    K = m.shape[1]
    k = lax.broadcasted_iota(jnp.int32, (K, n_out), 0)
    j = lax.broadcasted_iota(jnp.int32, (K, n_out), 1)
    sel = cond(k, j).astype(BF16)
    return jnp.dot(_split_terms(m, n_terms), jnp.concatenate([sel] * n_terms, axis=0), preferred_element_type=F32)


def _rowsum_mxu(m):
    return _select_dot(m, 2, 128, lambda k, j: k >= 0)


def _lane_block_sums(m, width):
    shift = width.bit_length() - 1
    return _select_dot(m, 2, 128, lambda k, j: j == jnp.right_shift(k, shift))


def _heads_to_pairs(m):
    return _select_dot(m, 3, 512, lambda k, j: k == jnp.right_shift(j, 6))


def _ssd_common(dt_ref, par_ref):
    par = par_ref[...]
    raw = dt_ref[...] + par[0:1, :]
    dt = _softplus(raw)
    a = -jnp.exp(par[1:2, :])
    cs = _cumsum_rows(dt * a)
    L = cs.shape[0]
    cs_last = cs[L - 1:L, :]
    return raw, dt, a, par[2:3, :], cs, cs.T, jnp.exp(cs), jnp.exp(cs_last - cs), jnp.exp(cs_last)


def _ssd_specs(nc, rev):
    L = SSM_CHUNK

    def ci(c):
        return nc - 1 - c if rev else c

    return [pl.BlockSpec((L, D_INNER), lambda c: (ci(c), 0)),
            pl.BlockSpec((L, GN), lambda c: (ci(c), D_INNER // GN)),
            pl.BlockSpec((L, GN), lambda c: (ci(c), D_INNER // GN + 1)),
            pl.BlockSpec((SSM_GROUPS, L, 128), lambda c: (0, ci(c), 0)),
            pl.BlockSpec((SSM_GROUPS, 8, 128), lambda c: (0, 0, 0)),
            pl.BlockSpec((L, D_INNER), lambda c: (ci(c), 0)),
            pl.BlockSpec((1, D_INNER), lambda c: (0, 0))], ci


def _round_robin(gens):
    live = list(gens)
    while live:
        nxt = []
        for gen in live:
            try:
                next(gen)
                nxt.append(gen)
            except StopIteration:
                pass
        live = nxt


def _group_views(g, wide, narrow, lead):
    return ([r.at[:, g * 512:(g + 1) * 512] for r in wide], [r.at[:, g * 128:(g + 1) * 128] for r in narrow],
            [r.at[g] for r in lead])


def _ssd_fwd(xbc_c, zx, dtg, par, gnw, *, name):
    T = xbc_c.shape[0]
    L = SSM_CHUNK
    nc = T // L
    in_specs, ci = _ssd_specs(nc, False)

    def body(xs_ref, b_ref, c_ref, dt_ref, par_ref, z_ref, gnw_ref, y_ref, yn_ref, st_ref, h_ref):
        @pl.when(pl.program_id(0) == 0)
        def _():
            h_ref[...] = jnp.zeros_like(h_ref)

        gens = []
        for g in range(SSM_GROUPS):
            (xs, z, gw, y, yn), (b, c), (dt, pr, st, h) = _group_views(
                g, [xs_ref, z_ref, gnw_ref, y_ref, yn_ref], [b_ref, c_ref], [dt_ref, par_ref, st_ref, h_ref])
            gens.append(group(xs, b, c, dt, pr, z, gw, y, yn, st, h))
        _round_robin(gens)

    def group(xs_ref, b_ref, c_ref, dt_ref, par_ref, z_ref, gnw_ref, y_ref, yn_ref, st_ref, h_ref):
        _, dt, _, dsk, cs, csT, ecs, eend, dec = _ssd_common(dt_ref, par_ref)
        Bb = b_ref[...].astype(BF16)
        Cb = c_ref[...].astype(BF16)
        G = lax.dot_general(Cb, Bb, _NT, preferred_element_type=F32)
        row = lax.broadcasted_iota(jnp.int32, (L, L), 0)
        col = lax.broadcasted_iota(jnp.int32, (L, L), 1)
        tril = col <= row
        lo = lax.broadcasted_iota(jnp.int32, (L, 128), 1) < 64
        lo1 = lax.broadcasted_iota(jnp.int32, (1, 128), 1) < 64
        dt_x, ecs_x, eend_x = (_heads_to_pairs(m) for m in (dt, ecs, eend))
        for pp in range(4):
            hA, hB = 2 * pp, 2 * pp + 1
            lanes = slice(pp * 128, (pp + 1) * 128)

            def sel1(m):
                return jnp.where(lo1, m[:, hA:hA + 1], m[:, hB:hB + 1])

            X = xs_ref[:, lanes]
            xd = X * dt_x[:, lanes]
            xdb = xd.astype(BF16)
            ys = []
            for h in (hA, hB):
                Lm = jnp.where(tril, jnp.exp(jnp.minimum(cs[:, h:h + 1] - csT[h:h + 1, :], 0.0)), 0.0)
                ys.append(jnp.dot((G * Lm).astype(BF16), xdb, preferred_element_type=F32))
                yield
            Hp = h_ref[pp]
            st_ref[pp] = Hp
            yoff = jnp.dot(Cb, Hp.astype(BF16), preferred_element_type=F32) * ecs_x[:, lanes]
            y_ref[:, lanes] = jnp.where(lo, ys[0], ys[1]) + yoff + sel1(dsk) * X
            S = lax.dot_general(Bb, (xd * eend_x[:, lanes]).astype(BF16), _TN, preferred_element_type=F32)
            h_ref[pp] = Hp * sel1(dec) + S
            yield
        zv = z_ref[...]
        yg = y_ref[...] * (zv * _sigmoid(zv))
        r = jnp.tile(lax.rsqrt(_rowsum_mxu(yg * yg) * (1.0 / 512) + EPS), (1, 4))
        yn_ref[...] = (yg * r * gnw_ref[...]).astype(BF16)

    return pl.pallas_call(
        body, name=name, grid=(nc,), in_specs=in_specs,
        out_specs=[pl.BlockSpec((L, D_INNER), lambda c: (c, 0)), pl.BlockSpec((L, D_INNER), lambda c: (c, 0)),
                   pl.BlockSpec((SSM_GROUPS, None, 4, 128, 128), lambda c: (0, c, 0, 0, 0))],
        out_shape=[jax.ShapeDtypeStruct((T, D_INNER), F32), jax.ShapeDtypeStruct((T, D_INNER), BF16),
                   jax.ShapeDtypeStruct((SSM_GROUPS, nc, 4, 128, 128), F32)],
        scratch_shapes=[pltpu.VMEM((SSM_GROUPS, 4, 128, 128), F32)],
        compiler_params=_cparams(("arbitrary",)))(xbc_c, xbc_c, xbc_c, dtg, par, zx, gnw)


def _ssd_bwd(xbc_c, zx, dtg, par, gnw, y, st, dyn, *, name):
    T = xbc_c.shape[0]
    L = SSM_CHUNK
    nc = T // L
    in_specs, ci = _ssd_specs(nc, True)
    in_specs += [pl.BlockSpec((L, D_INNER), lambda c: (ci(c), 0)),
                 pl.BlockSpec((SSM_GROUPS, None, 4, 128, 128), lambda c: (0, ci(c), 0, 0, 0)),
                 pl.BlockSpec((L, D_INNER), lambda c: (ci(c), 0))]

    def body(xs_ref, b_ref, c_ref, dt_ref, par_ref, z_ref, gnw_ref, y_ref, st_ref, dyn_ref,
             dxbc_ref, dz_ref, ddt_ref, dgnw_ref, dpar_ref, dh_ref):
        @pl.when(pl.program_id(0) == 0)
        def _():
            dh_ref[...] = jnp.zeros_like(dh_ref)
            dgnw_ref[...] = jnp.zeros_like(dgnw_ref)
            dpar_ref[...] = jnp.zeros_like(dpar_ref)

        dxs_ref = dxbc_ref.at[:, 0:D_INNER]
        db_ref = dxbc_ref.at[:, D_INNER:D_INNER + GN]
        dc_ref = dxbc_ref.at[:, D_INNER + GN:CONV_DIM]

        gens = []
        for g in range(SSM_GROUPS):
            (xs, z, gw, y, dyn, dxs, dz, dgw), (b, c, db, dc), (dt, pr, st, ddt, dpr, dh) = _group_views(
                g, [xs_ref, z_ref, gnw_ref, y_ref, dyn_ref, dxs_ref, dz_ref, dgnw_ref], [b_ref, c_ref, db_ref, dc_ref],
                [dt_ref, par_ref, st_ref, ddt_ref, dpar_ref, dh_ref])
            gens.append(group(xs, b, c, dt, pr, z, gw, y, st, dyn, dxs, db, dc, dz, ddt, dgw, dpr, dh))
        _round_robin(gens)

    def group(xs_ref, b_ref, c_ref, dt_ref, par_ref, z_ref, gnw_ref, y_ref, st_ref, dyn_ref,
              dxs_ref, db_ref, dc_ref, dz_ref, ddt_ref, dgnw_ref, dpar_ref, dh_ref):
        yv = y_ref[...]
        zv = z_ref[...]
        sg = _sigmoid(zv)
        sz = zv * sg
        yg = yv * sz
        r = jnp.tile(lax.rsqrt(_rowsum_mxu(yg * yg) * (1.0 / 512) + EPS), (1, 4))
        yh = yg * r
        dyn = dyn_ref[...].astype(F32)
        dgnw_ref[...] += jnp.sum(dyn * yh, axis=0, keepdims=True)
        dyh = dyn * gnw_ref[...]
        dyg = r * (dyh - yh * jnp.tile(_rowsum_mxu(dyh * yh) * (1.0 / 512), (1, 4)))
        dY_all = dyg * sz
        dz_ref[...] = (dyg * yv * (sg * (1.0 + zv * (1.0 - sg)))).astype(dz_ref.dtype)

        yield
        raw, dt, a, dsk, cs, csT, ecs, eend, dec = _ssd_common(dt_ref, par_ref)
        Bb = b_ref[...].astype(BF16)
        Cb = c_ref[...].astype(BF16)
        G = lax.dot_general(Cb, Bb, _NT, preferred_element_type=F32)
        row = lax.broadcasted_iota(jnp.int32, (L, L), 0)
        col = lax.broadcasted_iota(jnp.int32, (L, L), 1)
        tril = col <= row
        lo = lax.broadcasted_iota(jnp.int32, (L, 128), 1) < 64
        lane1 = lax.broadcasted_iota(jnp.int32, (1, 128), 1)
        lo1 = lane1 < 64
        rowl = lax.broadcasted_iota(jnp.int32, (L, 128), 0)
        dt_x, ecs_x, eend_x = (_heads_to_pairs(m) for m in (dt, ecs, eend))
        dG = jnp.zeros((L, L), F32)
        dB = jnp.zeros((L, SSM_STATE), F32)
        dC = jnp.zeros((L, SSM_STATE), F32)
        dcs_t = jnp.zeros((L, L), F32)
        tails = jnp.zeros((1, 128), F32)
        dD_row = jnp.zeros((1, 128), F32)
        v_parts, prod_parts = [], []

        def tot(m):
            return jnp.sum(jnp.sum(m, axis=0, keepdims=True), axis=1, keepdims=True)

        for pp in range(4):
            hA, hB = 2 * pp, 2 * pp + 1
            lanes = slice(pp * 128, (pp + 1) * 128)

            def sel1(m):
                return jnp.where(lo1, m[:, hA:hA + 1], m[:, hB:hB + 1])

            X = xs_ref[:, lanes]
            dY = dY_all[:, lanes]
            dtsel = dt_x[:, lanes]
            xd = X * dtsel
            xdb = xd.astype(BF16)
            dYb = dY.astype(BF16)
            Hp = st_ref[pp]
            Hb = Hp.astype(BF16)
            dHn = dh_ref[pp]
            dHb = dHn.astype(BF16)
            ecs_sel = ecs_x[:, lanes]
            eend_sel = eend_x[:, lanes]
            dxd_state = jnp.dot(Bb, dHb, preferred_element_type=F32) * eend_sel
            yoff = jnp.dot(Cb, Hb, preferred_element_type=F32) * ecs_sel
            dYe = (dY * ecs_sel).astype(BF16)
            dC = dC + lax.dot_general(dYe, Hb, _NT, preferred_element_type=F32)
            dB = dB + lax.dot_general((xd * eend_sel).astype(BF16), dHb, _NT, preferred_element_type=F32)
            dh_ref[pp] = dHn * sel1(dec) + lax.dot_general(Cb, dYe, _TN, preferred_element_type=F32)
            q = xd * dxd_state
            dyq = dY * yoff - q
            qcol = jnp.sum(q, axis=0, keepdims=True)
            hcol = jnp.sum(dHn * Hp, axis=0, keepdims=True)
            dxd_diag = []
            for h, msk, msk1 in ((hA, lo, lo1), (hB, jnp.logical_not(lo), jnp.logical_not(lo1))):
                Lm = jnp.where(tril, jnp.exp(jnp.minimum(cs[:, h:h + 1] - csT[h:h + 1, :], 0.0)), 0.0)
                M = G * Lm
                dxd_diag.append(lax.dot_general(M.astype(BF16), dYb, _TN, preferred_element_type=F32))
                dM = lax.dot_general(jnp.where(msk, dY, 0.0).astype(BF16), xdb, _NT, preferred_element_type=F32)
                dG = dG + dM * Lm
                W = dM * M
                dcs_t = dcs_t + jnp.where(row == h, jnp.sum(W, axis=0, keepdims=True), 0.0)
                v_parts.append(W + jnp.where(msk, dyq, 0.0))
                tail = (jnp.sum(jnp.where(msk1, qcol, 0.0), axis=1, keepdims=True)
                        + dec[:, h:h + 1] * jnp.sum(jnp.where(msk1, hcol, 0.0), axis=1, keepdims=True))
                tails = tails + jnp.where(lane1 == h, tail, 0.0)
                yield
            dxd = jnp.where(lo, dxd_diag[0], dxd_diag[1]) + dxd_state
            prod_parts.append(dxd * X)
            dxs_ref[:, lanes] = dxd * dtsel + sel1(dsk) * dY
            dyx = jnp.sum(dY * X, axis=0, keepdims=True)
            sA = jnp.sum(jnp.where(lo1, dyx, 0.0), axis=1, keepdims=True)
            sB = jnp.sum(dyx, axis=1, keepdims=True) - sA
            dD_row = dD_row + jnp.where(lane1 == hA, sA, 0.0) + jnp.where(lane1 == hB, sB, 0.0)
            yield
        dGb = dG.astype(BF16)
        db_ref[...] = dB + lax.dot_general(dGb, Cb, _TN, preferred_element_type=F32)
        dc_ref[...] = dC + jnp.dot(dGb, Bb, preferred_element_type=F32)
        dcs_mat = _lane_block_sums(jnp.concatenate(v_parts, axis=1), 128) + jnp.where(rowl == L - 1, tails, 0.0)
        ddt_mat = _lane_block_sums(jnp.concatenate(prod_parts, axis=1), 64)
        dad = _rcumsum_rows(dcs_mat - dcs_t.T)
        draw = (a * dad + ddt_mat) * _sigmoid(raw)
        ddt_ref[...] = draw
        dpar_ref[0:1, :] += jnp.sum(draw, axis=0, keepdims=True)
        dpar_ref[1:2, :] += jnp.sum(dt * dad, axis=0, keepdims=True) * a
        dpar_ref[2:3, :] += dD_row

    return pl.pallas_call(
        body, name=name, grid=(nc,), in_specs=in_specs,
        out_specs=[pl.BlockSpec((L, CONV_DIM), lambda c: (ci(c), 0)),
                   pl.BlockSpec((L, D_INNER), lambda c: (ci(c), 0)),
                   pl.BlockSpec((SSM_GROUPS, L, 128), lambda c: (0, ci(c), 0)),
                   pl.BlockSpec((1, D_INNER), lambda c: (0, 0)),
                   pl.BlockSpec((SSM_GROUPS, 8, 128), lambda c: (0, 0, 0))],
        out_shape=[jax.ShapeDtypeStruct((T, CONV_DIM), F32), jax.ShapeDtypeStruct((T, IN_PROJ_PAD), BF16),
                   jax.ShapeDtypeStruct((SSM_GROUPS, T, 128), F32), jax.ShapeDtypeStruct((1, D_INNER), F32),
                   jax.ShapeDtypeStruct((SSM_GROUPS, 8, 128), F32)],
        scratch_shapes=[pltpu.VMEM((SSM_GROUPS, 4, 128, 128), F32)],
        compiler_params=_cparams(("arbitrary",)))(xbc_c, xbc_c, xbc_c, dtg, par, zx, gnw, y, st, dyn)


SB_KEYS = 512
SB_SCAN = 256
SB_STRIP = 256


def _tri(width, cond):
    kk = lax.broadcasted_iota(jnp.int32, (width, width), 0)
    jj = lax.broadcasted_iota(jnp.int32, (width, width), 1)
    return cond(kk, jj).astype(BF16)


_LOG2E = 1.4426950408889634


def _softplus2(z2):
    return jnp.maximum(z2, 0.0) + jnp.log2(1.0 + jnp.exp2(-jnp.abs(z2)))


def _sba_sub_fwd(zb, c, U, mask):
    z2 = zb * _LOG2E
    s = _softplus2(z2)
    if mask is not None:
        s = jnp.where(mask, s, 0.0)
    R = c + jnp.dot(s.astype(BF16), U, preferred_element_type=F32)
    A = jnp.exp2(z2 - s - R)
    if mask is not None:
        A = jnp.where(mask, A, 0.0)
    return A.astype(BF16), R[:, 0:1] + s[:, 0:1]


def _sba_sub_bwd(zb, dAb, Lt, pc, pe, Uincl, Uexcl, mask):
    last = zb.shape[1] - 1
    z2 = zb * _LOG2E
    s = _softplus2(z2)
    g = z2 - s
    if mask is not None:
        s = jnp.where(mask, s, 0.0)
    P = pc + jnp.dot(s.astype(BF16), Uincl, preferred_element_type=F32)
    A = jnp.exp2(g - (Lt - P))
    if mask is not None:
        A = jnp.where(mask, A, 0.0)
    E = dAb * A
    PE = pe + jnp.dot(E.astype(BF16), Uexcl, preferred_element_type=F32)
    dz = E - jnp.exp2(g) * (E + PE)
    if mask is not None:
        dz = jnp.where(mask, dz, 0.0)
    return (A.astype(BF16), dz.astype(BF16), P[:, last:last + 1], PE[:, last:last + 1] + E[:, last:last + 1])


def _stack_heads(v):
    lo = lax.broadcasted_iota(jnp.int32, v.shape, 1) < 64
    zero = jnp.zeros_like(v)
    return jnp.concatenate([jnp.where(lo, v, zero), jnp.where(lo, zero, v)], axis=0)


def _unstack_heads(v):
    lo = lax.broadcasted_iota(jnp.int32, (SB_BLOCK, 128), 1) < 64
    return jnp.where(lo, v[:SB_BLOCK], v[SB_BLOCK:])


def _sba_rows(a):
    return slice(2 * a * SB_BLOCK, 2 * (a + 1) * SB_BLOCK)


def _sba_diag_case(a, b):
    Bq = SB_BLOCK
    if b * SB_SCAN >= (a + 1) * Bq:
        return "skip"
    if (b + 1) * SB_SCAN <= a * Bq:
        return "full"
    rowi = lax.broadcasted_iota(jnp.int32, (2 * Bq, SB_SCAN), 0)
    qpos = a * Bq + jnp.where(rowi >= Bq, rowi - Bq, rowi)
    return b * SB_SCAN + lax.broadcasted_iota(jnp.int32, (2 * Bq, SB_SCAN), 1) < qpos


def _sba_fwd(q, kv, *, name):
    T = q.shape[0]
    Bq = SB_BLOCK
    nsub = SB_KEYS // Bq
    nscan = SB_KEYS // SB_SCAN
    R = 2 * SB_KEYS
    assert T % SB_KEYS == 0 and SB_STRIP == 2 * Bq
    scale = 1.0 / math.sqrt(SB_HEAD_DIM)

    def body(q_ref, k_ref, v_ref, o_ref, lt_ref, z_s, a_s, c_s, acc_s):
        i = pl.program_id(1)
        U2 = _tri(SB_SCAN, lambda k, j: k > j)
        qs_all = jnp.concatenate([_stack_heads(q_ref[a * Bq:(a + 1) * Bq, :] * scale) for a in range(nsub)], axis=0)
        c_s[...] = jnp.zeros_like(c_s)
        acc_s[...] = jnp.zeros_like(acc_s)

        def scores(J, slot):
            off = pl.multiple_of(J * SB_KEYS, SB_KEYS)
            z_s[slot] = lax.dot_general(qs_all, k_ref[pl.ds(off, SB_KEYS), :], _NT, preferred_element_type=F32)

        def weights(slot, diag):
            for a in range(nsub):
                rows = _sba_rows(a)
                c = c_s[rows, :]
                for b in reversed(range(nscan)):
                    cols = slice(b * SB_SCAN, (b + 1) * SB_SCAN)
                    case = _sba_diag_case(a, b) if diag else "full"
                    if isinstance(case, str) and case == "skip":
                        a_s[slot, rows, cols] = jnp.zeros((2 * Bq, SB_SCAN), BF16)
                        continue
                    A, c = _sba_sub_fwd(z_s[slot, rows, cols], c, U2, None if isinstance(case, str) else case)
                    a_s[slot, rows, cols] = A
                c_s[rows, :] = c

        def values(J, slot):
            off = pl.multiple_of(J * SB_KEYS, SB_KEYS)
            acc_s[...] += jnp.dot(a_s[slot], v_ref[pl.ds(off, SB_KEYS), :], preferred_element_type=F32)

        scores(i, 0)
        weights(0, True)
        scores(jnp.maximum(i - 1, 0), 1)

        def two_steps(u, _):
            t = 2 * u + 1
            weights(1, False)
            scores(jnp.maximum(i - t - 1, 0), 0)
            values(i - t + 1, 0)
            weights(0, False)
            scores(jnp.maximum(i - t - 2, 0), 1)
            values(i - t, 1)
            return 0

        lax.fori_loop(0, i // 2, two_steps, 0)
        odd = lax.rem(i, 2) == 1

        @pl.when(jnp.logical_not(odd))
        def _():
            values(0, 0)

        @pl.when(odd)
        def _():
            weights(1, False)
            values(1, 0)
            values(0, 1)
        for a in range(nsub):
            o_ref[a * Bq:(a + 1) * Bq, :] = _unstack_heads(acc_s[_sba_rows(a), :]).astype(BF16)
            lt_ref[a * Bq:(a + 1) * Bq, :] = _unstack_heads(jnp.broadcast_to(c_s[_sba_rows(a), :], (2 * Bq, 128)))

    return pl.pallas_call(
        body, name=name, grid=(SB_HEADS // 2, T // SB_KEYS),
        in_specs=[pl.BlockSpec((SB_KEYS, 128), lambda p, i: (i, p)), pl.BlockSpec((T, 128), lambda p, i: (0, p)),
                  pl.BlockSpec((T, 128), lambda p, i: (0, p + SB_HEADS // 2))],
        out_specs=[pl.BlockSpec((SB_KEYS, 128), lambda p, i: (i, p)),
                   pl.BlockSpec((None, SB_KEYS, 128), lambda p, i: (p, i, 0))],
        out_shape=[jax.ShapeDtypeStruct((T, D_MODEL), BF16), jax.ShapeDtypeStruct((SB_HEADS // 2, T, 128), F32)],
        scratch_shapes=[pltpu.VMEM((2, R, SB_KEYS), F32), pltpu.VMEM((2, R, SB_KEYS), BF16),
                        pltpu.VMEM((R, 1), F32), pltpu.VMEM((R, 128), F32)],
        compiler_params=_cparams(("parallel", "parallel")))(q, kv, kv)


def _sba_bwd(q, kv, lt, do, *, name):
    T = q.shape[0]
    Bq = SB_BLOCK
    nq = T // SB_KEYS
    nsub = SB_KEYS // Bq
    nscan = SB_KEYS // SB_SCAN
    R = 2 * SB_KEYS
    assert T % SB_KEYS == 0 and SB_STRIP == 2 * Bq
    scale = 1.0 / math.sqrt(SB_HEAD_DIM)

    def body(q_ref, k_ref, v_ref, lt_ref, do_ref, dq_ref, dk_ref, dv_ref, dk_acc, dv_acc,
             z_s, da_s, a_s, dz_s, pc_s, pe_s, lt_s, dq_s):
        i = pl.program_id(1)

        @pl.when(i == 0)
        def _():
            dk_acc[...] = jnp.zeros_like(dk_acc)
            dv_acc[...] = jnp.zeros_like(dv_acc)

        Uincl = _tri(SB_SCAN, lambda k, j: k <= j)
        Uexcl = _tri(SB_SCAN, lambda k, j: k < j)
        qs, dos = [], []
        for a in range(nsub):
            rows = slice(a * Bq, (a + 1) * Bq)
            qs.append(_stack_heads(q_ref[rows, :] * scale))
            dos.append(_stack_heads(do_ref[rows, :]))
            lt_s[_sba_rows(a), :] = jnp.concatenate([lt_ref[rows, 0:1], lt_ref[rows, 64:65]], axis=0)
        qs_all = jnp.concatenate(qs, axis=0)
        dos_all = jnp.concatenate(dos, axis=0)
        pc_s[...] = jnp.zeros_like(pc_s)
        pe_s[...] = jnp.zeros_like(pe_s)
        a_s[1] = jnp.zeros((R, SB_KEYS), BF16)
        dz_s[1] = jnp.zeros((R, SB_KEYS), BF16)

        def scores(J, slot):
            off = pl.multiple_of(J * SB_KEYS, SB_KEYS)
            z_s[slot] = lax.dot_general(qs_all, k_ref[pl.ds(off, SB_KEYS), :], _NT, preferred_element_type=F32)
            da_s[slot] = lax.dot_general(dos_all, v_ref[pl.ds(off, SB_KEYS), :], _NT, preferred_element_type=F32)

        def gradients(slot, diag):
            for a in range(nsub):
                rows = _sba_rows(a)
                pc, pe, Lt = pc_s[rows, :], pe_s[rows, :], lt_s[rows, :]
                for b in range(nscan):
                    cols = slice(b * SB_SCAN, (b + 1) * SB_SCAN)
                    case = _sba_diag_case(a, b) if diag else "full"
                    if isinstance(case, str) and case == "skip":
                        a_s[slot, rows, cols] = jnp.zeros((2 * Bq, SB_SCAN), BF16)
                        dz_s[slot, rows, cols] = jnp.zeros((2 * Bq, SB_SCAN), BF16)
                        continue
                    A, dz, pc, pe = _sba_sub_bwd(z_s[slot, rows, cols], da_s[slot, rows, cols], Lt, pc, pe, Uincl, Uexcl,
                                                 None if isinstance(case, str) else case)
                    a_s[slot, rows, cols] = A
                    dz_s[slot, rows, cols] = dz
                pc_s[rows, :] = pc
                pe_s[rows, :] = pe

        def products(J, slot):
            off = pl.multiple_of(J * SB_KEYS, SB_KEYS)
            dzt = dz_s[slot]
            dk_acc[pl.ds(off, SB_KEYS), :] += lax.dot_general(dzt, qs_all, _TN, preferred_element_type=F32)
            dv_acc[pl.ds(off, SB_KEYS), :] += lax.dot_general(a_s[slot], dos_all, _TN, preferred_element_type=F32)
            dq_s[...] += jnp.dot(dzt, k_ref[pl.ds(off, SB_KEYS), :], preferred_element_type=F32)

        dq_s[...] = jnp.zeros_like(dq_s)
        scores(0, 0)

        def two_steps(u, _):
            t = 2 * u
            gradients(0, False)
            scores(t + 1, 1)
            products(jnp.maximum(t - 1, 0), 1)
            gradients(1, False)
            scores(t + 2, 0)
            products(t, 0)
            return 0

        lax.fori_loop(0, i // 2, two_steps, 0)
        odd = lax.rem(i, 2) == 1

        @pl.when(jnp.logical_not(odd))
        def _():
            gradients(0, True)
            products(jnp.maximum(i - 1, 0), 1)
            products(i, 0)

        @pl.when(odd)
        def _():
            gradients(0, False)
            scores(i, 1)
            products(jnp.maximum(i - 2, 0), 1)
            gradients(1, True)
            products(i - 1, 0)
            products(i, 1)

        for a in range(nsub):
            dq_ref[a * Bq:(a + 1) * Bq, :] = (_unstack_heads(dq_s[_sba_rows(a), :]) * scale).astype(BF16)

        @pl.when(i == nq - 1)
        def _():
            dk_ref[...] = dk_acc[...].astype(BF16)
            dv_ref[...] = dv_acc[...].astype(BF16)

    return pl.pallas_call(
        body, name=name, grid=(SB_HEADS // 2, nq),
        in_specs=[pl.BlockSpec((SB_KEYS, 128), lambda p, i: (i, p)), pl.BlockSpec((T, 128), lambda p, i: (0, p)),
                  pl.BlockSpec((T, 128), lambda p, i: (0, p + SB_HEADS // 2)),
                  pl.BlockSpec((None, SB_KEYS, 128), lambda p, i: (p, i, 0)),
                  pl.BlockSpec((SB_KEYS, 128), lambda p, i: (i, p))],
        out_specs=[pl.BlockSpec((SB_KEYS, 128), lambda p, i: (i, p)), pl.BlockSpec((T, 128), lambda p, i: (0, p)),
                   pl.BlockSpec((T, 128), lambda p, i: (0, p))],
        out_shape=[jax.ShapeDtypeStruct((T, D_MODEL), BF16), jax.ShapeDtypeStruct((T, D_MODEL), BF16),
                   jax.ShapeDtypeStruct((T, D_MODEL), BF16)],
        scratch_shapes=[pltpu.VMEM((T, 128), F32), pltpu.VMEM((T, 128), F32),
                        pltpu.VMEM((2, R, SB_KEYS), F32), pltpu.VMEM((2, R, SB_KEYS), F32),
                        pltpu.VMEM((2, R, SB_KEYS), BF16), pltpu.VMEM((2, R, SB_KEYS), BF16),
                        pltpu.VMEM((R, 1), F32), pltpu.VMEM((R, 1), F32), pltpu.VMEM((R, 1), F32),
                        pltpu.VMEM((R, 128), F32)],
        compiler_params=_cparams(("parallel", "arbitrary")))(q, kv, kv, lt, do)


def _loss_head(h, tgt, w, *, name, tt=512):
    T, D = h.shape
    tt = min(tt, T)

    def body(h_ref, t_ref, w_ref, loss_ref, dh_ref, dw_ref):
        i = pl.program_id(0)
        hv = h_ref[...]
        wv = w_ref[...]
        r = lax.rsqrt(jnp.mean(hv * hv, axis=-1, keepdims=True) + EPS)
        xhat = hv * r
        err = xhat * wv - t_ref[...]
        part = 0.5 * jnp.sum(jnp.mean(err * err, axis=-1, keepdims=True), axis=0, keepdims=True)
        dy = err * (1.0 / D)
        dxh = dy * wv
        dh_ref[...] = r * (dxh - xhat * jnp.mean(dxh * xhat, axis=-1, keepdims=True))
        dwc = jnp.sum(dy * xhat, axis=0, keepdims=True)

        @pl.when(i == 0)
        def _():
            loss_ref[...] = jnp.broadcast_to(part, loss_ref.shape)
            dw_ref[...] = dwc

        @pl.when(i > 0)
        def _():
            loss_ref[...] += jnp.broadcast_to(part, loss_ref.shape)
            dw_ref[...] += dwc

    return pl.pallas_call(
        body, name=name, grid=(T // tt,),
        in_specs=[pl.BlockSpec((tt, D), lambda i: (i, 0)), pl.BlockSpec((tt, D), lambda i: (i, 0)),
                  pl.BlockSpec((1, D), lambda i: (0, 0))],
        out_specs=[pl.BlockSpec((1, 128), lambda i: (0, 0)), pl.BlockSpec((tt, D), lambda i: (i, 0)),
                   pl.BlockSpec((1, D), lambda i: (0, 0))],
        out_shape=[jax.ShapeDtypeStruct((1, 128), F32), jax.ShapeDtypeStruct((T, D), F32),
                   jax.ShapeDtypeStruct((1, D), F32)],
        compiler_params=_cparams(("arbitrary",)))(h, tgt, w.reshape(1, D))


def _adamw(parts, w, m, v, *, name, tr=256, tc=None):
    plist = list(parts) if isinstance(parts, (list, tuple)) else [parts]
    P, _, C = plist[0].shape
    R = sum(a.shape[1] for a in plist)
    tr = min(tr, R)
    tc = C if tc is None else tc
    assert all(a.shape[1] % tr == 0 for a in plist) and C % tc == 0, (name, R, C, tr, tc)
    nbs = [a.shape[1] // tr for a in plist]
    offs = [sum(nbs[:l]) for l in range(len(nbs))]
    c1 = 1.0 - ADAM_B1 ** ADAM_STEP
    c2 = 1.0 - ADAM_B2 ** ADAM_STEP

    def body(*refs):
        p_refs = refs[:len(plist)]
        w_ref, m_ref, v_ref, g_ref, d_ref, nm_ref, nv_ref = refs[len(plist):]
        i = pl.program_id(0)
        g = None
        for l, p_ref in enumerate(p_refs):
            gl = p_ref[0].astype(F32)
            for k in range(1, P):
                gl = gl + p_ref[k].astype(F32)
            g = gl if g is None else jnp.where(i >= offs[l], gl, g)
        mn = ADAM_B1 * m_ref[...] + (1.0 - ADAM_B1) * g
        vn = ADAM_B2 * v_ref[...] + (1.0 - ADAM_B2) * (g * g)
        g_ref[...] = g
        nm_ref[...] = mn
        nv_ref[...] = vn
        d_ref[...] = -ADAM_LR * ((mn / c1) / (jnp.sqrt(vn / c2) + ADAM_EPS) + ADAM_WD * w_ref[...])

    spec = pl.BlockSpec((tr, tc), lambda i, j: (i, j))
    sds = jax.ShapeDtypeStruct((R, C), F32)
    return pl.pallas_call(
        body, name=name, grid=(R // tr, C // tc),
        in_specs=[pl.BlockSpec((P, tr, tc), functools.partial(lambda i, j, o, n: (0, jnp.clip(i - o, 0, n - 1), j), o=o, n=n))
                  for o, n in zip(offs, nbs)] + [spec, spec, spec],
        out_specs=[spec, spec, spec, spec], out_shape=[sds, sds, sds, sds],
        compiler_params=_cparams(("parallel", "parallel")))(*plist, w, m, v)


def _all_gather(shards, *, name):
    n = len(shards)

    def body(*refs):
        ins, outs = refs[:n], refs[n:2 * n]
        send_sems, recv_sems, local_sems = refs[2 * n:]
        x, y, c = lax.axis_index("x"), lax.axis_index("y"), lax.axis_index("c")
        me, sib = (x, y, c), (x, y, 1 - c)
        chips = [(1 - x, y), (x, 1 - y), (1 - x, 1 - y)]

        def slot(p):
            return 4 * p[0] + 2 * p[1] + p[2]

        def cp(a, k, block, to, src=None):
            dst = outs[a].at[slot(block)]
            return pltpu.make_async_remote_copy(src_ref=dst if src is None else src, dst_ref=dst,
                                                send_sem=send_sems.at[a, k], recv_sem=recv_sems.at[a, k],
                                                device_id=to, device_id_type=_MESH)

        mine = [pltpu.make_async_copy(ins[a], outs[a].at[slot(me)], local_sems.at[a]) for a in range(n)]
        for m in mine:
            m.start()
        first = []
        for a in range(n):
            first.append(cp(a, 0, me, sib, src=ins[a]))
            for j, chip in enumerate(chips):
                first.append(cp(a, 1 + j, me, (*chip, c), src=ins[a]))
        for f in first:
            f.start()
        passed = []
        for j, chip in enumerate(chips):
            for a in range(n):
                cp(a, 1 + j, (*chip, c), me).wait_recv()
                f = cp(a, 4 + j, (*chip, c), sib)
                f.start()
                passed.append(f)
        for a in range(n):
            cp(a, 0, sib, me).wait_recv()
            for j, chip in enumerate(chips):
                cp(a, 4 + j, (*chip, 1 - c), me).wait_recv()
        for f in first + passed:
            f.wait_send()
        for m in mine:
            m.wait()

    return pl.pallas_call(
        body, name=name, in_specs=[_ANY] * n, out_specs=[_ANY] * n,
        out_shape=[jax.ShapeDtypeStruct((N_DEV,) + s.shape, s.dtype) for s in shards],
        scratch_shapes=[pltpu.SemaphoreType.DMA((n, 7)), pltpu.SemaphoreType.DMA((n, 7)),
                        pltpu.SemaphoreType.DMA((n,))])(*shards)


_HBM = pl.BlockSpec(memory_space=pltpu.HBM)
_SEM = pl.BlockSpec(memory_space=pltpu.SEMAPHORE)
_EFFECT = pltpu.SideEffectType.DATAFLOW_SIDE_EFFECTING


def _peers():
    x, y, c = lax.axis_index("x"), lax.axis_index("y"), lax.axis_index("c")
    out = []
    for r in range(1, N_DEV):
        px = 1 - x if (r >> 2) & 1 else x
        py = 1 - y if (r >> 1) & 1 else y
        pc = 1 - c if r & 1 else c
        out.append(((px, py, pc), 4 * px + 2 * py + pc))
    return 4 * x + 2 * y + c, out


def _push_copy(src_ref, land_ref, send_sems, recv_sems, a, k, me, peer, peer_slot, scatter, arriving):
    src = src_ref.at[peer_slot] if scatter else src_ref
    return pltpu.make_async_remote_copy(
        src_ref=src, dst_ref=land_ref.at[peer_slot if arriving else me], send_sem=send_sems.at[a * (N_DEV - 1) + k],
        recv_sem=recv_sems.at[a * (N_DEV - 1) + k], device_id=peer, device_id_type=_MESH)


def _push_start(srcs, *, scatter, name):
    n = len(srcs)
    lands = [lax.empty(s.shape if scatter else (N_DEV,) + s.shape, s.dtype) for s in srcs]

    def body(*refs):
        src_refs, land_refs = refs[:n], refs[n:2 * n]
        send_sems, recv_sems = refs[2 * n], refs[2 * n + 1]
        token = refs[-1]
        me, peers = _peers()
        for k, (peer, slot) in enumerate(peers):
            for a in range(n):
                _push_copy(src_refs[a], land_refs[a], send_sems, recv_sems, a, k, me, peer, slot, scatter, False).start()
        token[...] = jnp.zeros_like(token)

    hbm = lambda a: pltpu.HBM(a.shape, a.dtype)
    outs = pl.pallas_call(
        body, name=name,
        out_shape=(pltpu.SemaphoreType.DMA((n * (N_DEV - 1),)), pltpu.SemaphoreType.DMA((n * (N_DEV - 1),)),
                   *[hbm(s) for s in srcs], *[hbm(l) for l in lands], jax.ShapeDtypeStruct((8, 128), F32)),
        in_specs=[_HBM] * (2 * n),
        out_specs=(_SEM, _SEM, *([_HBM] * (2 * n)), pl.BlockSpec(memory_space=pltpu.VMEM)),
        input_output_aliases={i: 2 + i for i in range(2 * n)},
        compiler_params=pltpu.CompilerParams(has_side_effects=_EFFECT),
    )(*[pltpu.with_memory_space_constraint(s, pltpu.HBM) for s in srcs],
      *[pltpu.with_memory_space_constraint(l, pltpu.HBM) for l in lands])
    return dict(send=outs[0], recv=outs[1], srcs=list(outs[2:2 + n]), lands=list(outs[2 + n:2 + 2 * n]),
                token=outs[-1], scatter=scatter, n=n)


def _push_wait(h, after, *, name):
    n, scatter = h["n"], h["scatter"]

    def body(*refs):
        src_refs, land_refs = refs[:n], refs[n:2 * n]
        send_sems, recv_sems = refs[2 * n], refs[2 * n + 1]
        me, peers = _peers()
        for k, (peer, slot) in enumerate(peers):
            for a in range(n):
                cp = _push_copy(src_refs[a], land_refs[a], send_sems, recv_sems, a, k, me, peer, slot, scatter, True)
                cp.wait_send()
                cp.wait_recv()

    hbm = lambda a: pltpu.HBM(a.shape, a.dtype)
    outs = pl.pallas_call(
        body, name=name,
        out_shape=(*[hbm(s) for s in h["srcs"]], *[hbm(l) for l in h["lands"]]),
        in_specs=[_HBM] * (2 * n) + [_SEM, _SEM, _ANY], out_specs=tuple([_HBM] * (2 * n)),
        input_output_aliases={i: i for i in range(2 * n)},
        compiler_params=pltpu.CompilerParams(has_side_effects=_EFFECT),
    )(*h["srcs"], *h["lands"], h["send"], h["recv"], after)
    return list(outs[:n]), list(outs[n:])


def _ffn_fwd(h, nw, w_up, conv_w, conv_b, w_down, tag):
    a3 = _mm_fwd(h, w_up, norm_w=nw, name=f"ffn{tag}_up", out_dtype=BF16, halves=True, w_t=True, tm=1024, tn=2816)
    p = _ffn_conv_fwd3(a3, conv_w, conv_b.reshape(1, -1), name=f"ffn{tag}_conv")
    h_out = _mm_fwd(p, w_down, residual=h, name=f"ffn{tag}_down", tm=1024, tn=1024)
    return h_out, (a3, p)


def _ffn_bwd(dh, h, saved, nw, w_up, conv_w, conv_b, w_down, tag):
    a3, p = saved
    g_down = _mm_tn(p, dh, name=f"ffn{tag}_down_wg", tk1=1408, tn=1024, tt=1024)
    dp = _mm_nt(dh, w_down, name=f"ffn{tag}_down_dg", out_dtype=BF16, tm=512, tn=2816, tk=1024)
    dhid3, dw3, db3 = _ffn_conv_bwd3(a3, conv_w, conv_b.reshape(1, -1), dp, name=f"ffn{tag}_conv_bwd")
    da3 = _conv_bwd_in3(dhid3, conv_w, K=FFN_CONV, name=f"ffn{tag}_conv_bwd_in")
    g_up = _mm_tn_t(da3, h, norm_w=nw, name=f"ffn{tag}_up_wg", tn=2816, tt=1024, vmem_mb=58)
    dh_out, g_nw = _mm_nt(da3, w_up, epi=(h, nw, dh), name=f"ffn{tag}_up_dg", w_t=True, tm=1024, tk=1408)
    g_cw = jnp.concatenate([dw3[0], dw3[1]], axis=1)
    g_cb = jnp.concatenate([db3[0], db3[1]], axis=1)
    return dh_out, dict(norm=g_nw.reshape(-1), up=g_up, conv_w=g_cw, conv_b=g_cb.reshape(-1), down=g_down)


_BIG = ["ssm_in_w", "ssm_out_w", "w_k", "w_v", "w_q", "w_o", "ffn_up_w", "ffn_down_w"]
_SMALL_SHARDED = ["ssm_norm_w", "ssm_conv_w", "ssm_conv_b", "ssm_gate_norm_w", "ffn_conv_w"]
_SMALL_REPL = ["ssm_dt_bias", "ssm_a_log", "ssm_d", "kv_norm_w", "attn_norm_w", "ffn_norm_w", "ffn_conv_b",
               "final_norm_w"]
_WEIGHTS = ["ssm_norm_w", "ssm_in_w", "ssm_conv_w", "ssm_conv_b", "ssm_dt_bias", "ssm_a_log", "ssm_d",
            "ssm_gate_norm_w", "ssm_out_w", "kv_norm_w", "w_k", "w_v", "attn_norm_w", "w_q", "w_o", "ffn_norm_w",
            "ffn_up_w", "ffn_conv_w", "ffn_conv_b", "ffn_down_w", "final_norm_w"]


def _as2d(a):
    return a.reshape(-1, a.shape[-1])


def _cols_to_full(g):
    return g.transpose(1, 0, 2).reshape(g.shape[1], N_DEV * g.shape[2])


def _pack_small(vals):
    flat = jnp.concatenate([v.reshape(-1).astype(F32) for v in vals])
    n = flat.shape[0]
    rows = -(-n // 1024) * 8
    return jnp.pad(flat, (0, rows * 128 - n)).reshape(rows, 128)


def _unpack_small(packed, shapes):
    flat = packed.reshape(-1)
    out, off = [], 0
    for s in shapes:
        n = math.prod(s)
        out.append(flat[off:off + n].reshape(s))
        off += n
    return out


def _tie(a, token):
    return a + token[0, 0].astype(a.dtype)


def _local_step(x, tgt, get_w, put_g):
    T = x.shape[0]
    Ws = get_w("ssm", None)
    fnw, fcw, fcb = Ws["ffn_norm_w"], Ws["ffn_conv_w"], Ws["ffn_conv_b"]
    zx = _mm_fwd(x, Ws["in_w"], norm_w=Ws["ssm_norm_w"], name="ssm_in", w_t=True, tm=1024, tn=1792)
    xbc_c = _ssm_conv_fwd(zx, Ws["ssm_conv_w"], Ws["ssm_conv_b"].reshape(1, -1), name="ssm_conv")
    dt_raw = zx[:, D_INNER + CONV_DIM:IN_PROJ_DIM]
    dtg = jnp.pad(dt_raw.reshape(T, SSM_GROUPS, 8).transpose(1, 0, 2), ((0, 0), (0, 0), (0, 120)))
    par = jnp.stack([Ws["ssm_dt_bias"].reshape(SSM_GROUPS, 8), Ws["ssm_a_log"].reshape(SSM_GROUPS, 8),
                     Ws["ssm_d"].reshape(SSM_GROUPS, 8)], axis=1)
    par = jnp.pad(par, ((0, 0), (0, 5), (0, 120)))
    gnw = _tie(Ws["ssm_gate_norm_w"].reshape(1, D_INNER), get_w("rest_start", xbc_c))
    y, yn, st = _ssd_fwd(xbc_c, zx, dtg, par, gnw, name="ssd_fwd")
    W0 = get_w("ffn0", y)
    Ws["ssm_out_w"] = W0["ssm_out_w"]
    h1 = _mm_fwd(yn, Ws["ssm_out_w"], residual=x, name="ssm_out", tm=1024, tn=1024)
    h2, ffn0 = _ffn_fwd(h1, fnw[0], W0["up"], fcw[0], fcb[0], W0["down"], "0")
    Wr = get_w("rest", h2)
    q = _mm_fwd(h2, Wr["w_q"], norm_w=Ws["attn_norm_w"], out_dtype=BF16, name="attn_q", tm=1024, tn=1024)
    kv = _mm_fwd(h2, Wr["w_kv"], norm_w=Ws["kv_norm_w"], out_dtype=BF16, name="attn_kv", tm=1024, tn=1024)
    o, lt = _sba_fwd(q, kv, name="sba_fwd")
    h3 = _mm_fwd(o, Wr["w_o"], residual=h2, name="attn_o", tm=1024, tn=1024)
    W1 = get_w("ffn1", h3)
    h4, ffn1 = _ffn_fwd(h3, fnw[1], W1["up"], fcw[1], fcb[1], W1["down"], "1")
    loss, dh4, g_final = _loss_head(h4, tgt, Ws["final_norm_w"], name="loss_head")
    dh3, gf1 = _ffn_bwd(dh4, h3, ffn1, fnw[1], W1["up"], fcw[1], fcb[1], W1["down"], "1")
    tok = put_g("ffn1", dict(up=gf1["up"], down=gf1["down"]))
    g_wo = _mm_tn(o, dh3, name="attn_o_wg", tn=1024, tt=1024)
    do = _mm_nt(dh3, _tie(Wr["w_o"], tok), name="attn_o_dg", out_dtype=BF16, tm=1024, tn=1024, tk=1024)
    dq, dk, dv = _sba_bwd(q, kv, lt, do, name="sba_bwd")
    g_wq = _mm_tn(h2, dq, norm_w=Ws["attn_norm_w"], name="attn_q_wg", tn=1024, tt=1024)
    dh2a, g_attn_nw = _mm_nt(dq, Wr["w_q"], epi=(h2, Ws["attn_norm_w"], dh3), name="attn_q_dg", tm=1024, tk=1024)
    dkv = jnp.concatenate([dk, dv], axis=1)
    g_wkv = _mm_tn(h2, dkv, norm_w=Ws["kv_norm_w"], name="attn_kv_wg", tn=1024, tt=1024)
    dh2, g_kv_nw = _mm_nt(dkv, Wr["w_kv"], epi=(h2, Ws["kv_norm_w"], dh2a), name="attn_kv_dg", tm=1024, tk=1024)
    tok = put_g("attn", dict(w_o=g_wo, w_q=g_wq, w_k=g_wkv[:, :D_MODEL], w_v=g_wkv[:, D_MODEL:]))
    dh1, gf0 = _ffn_bwd(dh2, h1, ffn0, fnw[0], W0["up"], fcw[0], _tie(fcb[0], tok), W0["down"], "0")
    tok = put_g("ffn0", dict(up=gf0["up"], down=gf0["down"]))
    g_out = _mm_tn(yn, dh1, name="ssm_out_wg", tn=1024, tt=1024)
    dyn = _mm_nt(dh1, _tie(Ws["ssm_out_w"], tok), name="ssm_out_dg", out_dtype=BF16, tm=1024, tn=1024, tk=1024)
    tok = put_g("ssm_out", dict(ssm_out_w=g_out))
    dxbc_c, dz, ddt, g_gnw, dpar = _ssd_bwd(xbc_c, zx, dtg, par, _tie(gnw, tok), y, st, dyn, name="ssd_bwd")
    dhid, g_scw, g_scb = _ssm_conv_bwd_pre(zx, Ws["ssm_conv_w"], Ws["ssm_conv_b"].reshape(1, -1), dxbc_c,
                                           name="ssm_conv_bwd")
    dzx = _conv_bwd_in(dhid, Ws["ssm_conv_w"], K=SSM_CONV, name="ssm_conv_bwd_in", into=(dz, D_INNER))
    ddt_t = ddt[:, :, :8].transpose(1, 0, 2).reshape(T, SSM_HEADS).astype(BF16)
    dzx = _put_cols(dzx, jnp.pad(ddt_t, ((0, 0), (0, IN_PROJ_PAD - IN_PROJ_DIM))), D_INNER + CONV_DIM, name="ssm_ddt_cols")
    g_in = _mm_tn_t(dzx, x, norm_w=Ws["ssm_norm_w"], name="ssm_in_wg", tn=1792, tt=1024)
    tok = put_g("ssm_in", dict(ssm_in_w=g_in))
    dx, g_ssm_nw = _mm_nt(dzx, Ws["in_w"], epi=(x, _tie(Ws["ssm_norm_w"], tok), dh1), name="ssm_in_dg", w_t=True,
                          tm=1024, tk=1792)
    f = {
        "ssm_norm_w": g_ssm_nw.reshape(-1), "ssm_conv_w": g_scw,
        "ssm_conv_b": g_scb.reshape(-1), "ssm_dt_bias": dpar[:, 0, :8].reshape(-1),
        "ssm_a_log": dpar[:, 1, :8].reshape(-1), "ssm_d": dpar[:, 2, :8].reshape(-1),
        "ssm_gate_norm_w": g_gnw.reshape(-1), "kv_norm_w": g_kv_nw.reshape(-1), "attn_norm_w": g_attn_nw.reshape(-1),
        "ffn_norm_w": jnp.stack([gf0["norm"], gf1["norm"]]), "ffn_conv_w": jnp.stack([gf0["conv_w"], gf1["conv_w"]]),
        "ffn_conv_b": jnp.stack([gf0["conv_b"], gf1["conv_b"]]), "final_norm_w": g_final.reshape(-1),
    }
    return loss, dx, f


def kernel(x, ssm_norm_w, ssm_in_w, ssm_conv_w, ssm_conv_b, ssm_dt_bias, ssm_a_log, ssm_d, ssm_gate_norm_w, ssm_out_w, kv_norm_w, w_k, w_v, attn_norm_w, w_q, w_o, ffn_norm_w, ffn_up_w, ffn_conv_w, ffn_conv_b, ffn_down_w, final_norm_w, loss_target, m_ssm_norm_w, m_ssm_in_w, m_ssm_conv_w, m_ssm_conv_b, m_ssm_dt_bias, m_ssm_a_log, m_ssm_d, m_ssm_gate_norm_w, m_ssm_out_w, m_kv_norm_w, m_w_k, m_w_v, m_attn_norm_w, m_w_q, m_w_o, m_ffn_norm_w, m_ffn_up_w, m_ffn_conv_w, m_ffn_conv_b, m_ffn_down_w, m_final_norm_w, v_ssm_norm_w, v_ssm_in_w, v_ssm_conv_w, v_ssm_conv_b, v_ssm_dt_bias, v_ssm_a_log, v_ssm_d, v_ssm_gate_norm_w, v_ssm_out_w, v_kv_norm_w, v_w_k, v_w_v, v_attn_norm_w, v_w_q, v_w_o, v_ffn_norm_w, v_ffn_up_w, v_ffn_conv_w, v_ffn_conv_b, v_ffn_down_w, v_final_norm_w):
    env = dict(locals())
    p = {n: env[n] for n in _WEIGHTS}
    mom = {n: env["m_" + n] for n in _WEIGHTS}
    var = {n: env["v_" + n] for n in _WEIGHTS}
    T = x.shape[1]
    me = 4 * lax.axis_index("x") + 2 * lax.axis_index("y") + lax.axis_index("c")
    rs = D_FF // N_DEV

    def bf2(a):
        return _as2d(a).astype(BF16)

    _T = ("ssm_in_w", "ffn_up_w")

    def t2d(a):
        return jnp.swapaxes(a, -1, -2).reshape(-1, a.shape[-2])

    def from_t2d(a, like):
        return jnp.swapaxes(a.reshape(like.shape[:-2] + (like.shape[-1], like.shape[-2])), -1, -2)

    n_in, n_up = p["ssm_in_w"].shape[-1], p["ffn_up_w"].shape[-1]

    def with_own(srcs, lands, scatter):
        out = []
        for s, l in zip(srcs, lands):
            own = lax.dynamic_index_in_dim(s, me, 0, keepdims=False) if scatter else s
            out.append(lax.dynamic_update_index_in_dim(l, own, me, 0))
        return out

    a_names = ["ssm_in_w"] + _SMALL_SHARDED
    got_a = dict(zip(a_names, _all_gather([t2d(p["ssm_in_w"]).astype(BF16)] + [_as2d(p[n]) for n in _SMALL_SHARDED],
                                          name="gather_ssm")))
    ffn0_names = ["ssm_out_w", "up0", "down0"]
    rest_names = ["w_q", "w_k", "w_v", "w_o"]
    up_t = jnp.swapaxes(p["ffn_up_w"], -1, -2).astype(BF16)
    shard = {"up0": up_t[0], "down0": bf2(p["ffn_down_w"][0]), "up1": up_t[1],
             "down1": bf2(p["ffn_down_w"][1]), "w_q": bf2(p["w_q"]), "w_k": bf2(p["w_k"]), "w_v": bf2(p["w_v"]),
             "w_o": bf2(p["w_o"]), "ssm_out_w": bf2(p["ssm_out_w"])}

    def anchored(a, on):
        return a + (jnp.where(jnp.isfinite(on), on, 0.0) * 0.0).astype(a.dtype)

    h_ffn0 = _push_start([anchored(shard[ffn0_names[0]], got_a["ssm_norm_w"][0, 0, 0])]
                         + [shard[n] for n in ffn0_names[1:]], scatter=False, name="gather_ffn0_start")
    handles = {}

    def get_w(group, after):
        if group == "ssm":
            W = {n: p[n] for n in _SMALL_REPL}
            for n in ("ssm_dt_bias", "ssm_a_log", "ssm_d", "attn_norm_w"):
                W[n] = W[n].reshape(-1)
            W["in_w"] = _stack_rows(got_a["ssm_in_w"], IN_PROJ_PAD, name="ssm_in_w_rows")
            W["ssm_norm_w"] = _tie(got_a["ssm_norm_w"].reshape(D_MODEL), h_ffn0["token"])
            W["ssm_conv_w"] = _cols_to_full(got_a["ssm_conv_w"])
            W["ssm_conv_b"] = got_a["ssm_conv_b"].reshape(CONV_DIM)
            W["ssm_gate_norm_w"] = got_a["ssm_gate_norm_w"].reshape(D_INNER)
            W["ffn_conv_w"] = _cols_to_full(got_a["ffn_conv_w"]).reshape(2, FFN_CONV, 2 * D_FF)
            return W
        if group == "rest_start":
            handles["rest"] = _push_start([anchored(shard[rest_names[0]], after[0, 0])]
                                          + [shard[n] for n in rest_names[1:]], scatter=False, name="gather_rest_start")
            handles["ffn1"] = _push_start([anchored(shard["up1"], handles["rest"]["token"][0, 0]), shard["down1"]],
                                          scatter=False, name="gather_ffn1_start")
            return handles["ffn1"]["token"]
        if group == "ffn1":
            srcs, lands = _push_wait(handles["ffn1"], after, name="gather_ffn1_wait")
            up, down = with_own(srcs, lands, False)
            return dict(up=up.reshape(2 * D_FF, D_MODEL), down=down.reshape(D_FF, D_MODEL))
        if group == "ffn0":
            srcs, lands = _push_wait(h_ffn0, after, name="gather_ffn0_wait")
            out, up, down = with_own(srcs, lands, False)
            return dict(ssm_out_w=out.reshape(D_INNER, D_MODEL), up=up.reshape(2 * D_FF, D_MODEL),
                        down=down.reshape(D_FF, D_MODEL))
        srcs, lands = _push_wait(handles["rest"], after, name="gather_rest_wait")
        g = dict(zip(rest_names, with_own(srcs, lands, False)))
        sq = lambda a: a.reshape(D_MODEL, D_MODEL)
        return dict(w_q=sq(g["w_q"]), w_kv=jnp.concatenate([sq(g["w_k"]), sq(g["w_v"])], axis=1), w_o=sq(g["w_o"]))

    pending = []

    def put_g(group, g):
        if group in ("ffn0", "ffn1"):
            keys = [("ffn_up_w", int(group[-1])), ("ffn_down_w", int(group[-1]))]
            blocks = [g["up"].reshape(N_DEV, n_up, D_MODEL), g["down"].reshape(N_DEV, rs, D_MODEL)]
        elif group == "attn":
            keys = [(n, None) for n in ("w_o", "w_q", "w_k", "w_v")]
            blocks = [g[n].reshape(N_DEV, D_MODEL // N_DEV, D_MODEL) for n, _ in keys]
        elif group == "ssm_out":
            keys = [("ssm_out_w", None)]
            blocks = [g["ssm_out_w"].reshape(N_DEV, D_INNER // N_DEV, D_MODEL)]
        else:
            keys = [("ssm_in_w", None)]
            blocks = [_unstack_rows(g["ssm_in_w"], N_DEV, n_in, name="ssm_in_g_rows")]
        h = _push_start(blocks, scatter=True, name=f"exchange_{group}_start")
        pending.append((group, keys, h))
        return h["token"]

    loss_row, dx, f = _local_step(x.reshape(T, D_MODEL), loss_target.reshape(T, D_MODEL), get_w, put_g)

    small_names = _SMALL_REPL + _SMALL_SHARDED
    small_full = _pack_small([f[n] for n in small_names] + [loss_row[0, 0:1]])
    small_bcast = jnp.broadcast_to(small_full[None], (N_DEV,) + small_full.shape)
    h_small = _push_start([small_bcast], scatter=True, name="exchange_small_start")
    tok = h_small["token"]

    arrived, res = {}, {}
    after = dx
    for group, keys, h in pending:
        srcs, lands = _push_wait(h, after, name=f"exchange_{group}_wait")
        arrived.update(zip(keys, with_own(srcs, lands, True)))
        for n in _BIG:
            layered = (n, 0) in arrived or (n, 1) in arrived
            if n in res or not ((n, None) in arrived or ((n, 0) in arrived and (n, 1) in arrived)):
                continue
            parts = [arrived[(n, 0)], arrived[(n, 1)]] if layered else arrived[(n, None)]
            w2, m2, v2 = ((t2d if n in _T else _as2d)(a[n]) for a in (p, mom, var))
            if not res:
                w2 = _tie(w2, tok)
            tiles = {"ffn_down_w": dict(tr=rs), "ffn_up_w": dict(tr=n_up // 2), "ssm_in_w": dict(tr=n_in, tc=256)}
            res[n] = _adamw(parts, w2, m2, v2, name=f"adamw_{n}", **tiles.get(n, dict(tr=256)))
            after = res[n][0]
    srcs, lands = _push_wait(h_small, after, name="exchange_small_wait")
    small_parts = with_own(srcs, lands, True)[0]
    out_g, out_d, out_m, out_v = {}, {}, {}, {}
    for n in _BIG:
        out_g[n], out_d[n], out_m[n], out_v[n] = (from_t2d(t, p[n]) if n in _T else t.reshape(p[n].shape) for t in res[n])

    zero = jnp.zeros_like(small_full)
    g_small_sum = _adamw(small_parts, zero, zero, zero, name="sum_small_grads", tr=small_full.shape[0])[0]
    *small_sums, loss_sum = _unpack_small(g_small_sum, [f[n].shape for n in small_names] + [(1,)])
    loss = loss_sum[0]
    g_small = dict(zip(small_names, small_sums))
    for n in _SMALL_SHARDED:
        width = p[n].shape[-1]
        g_small[n] = lax.dynamic_slice_in_dim(g_small[n], me * width, width, axis=g_small[n].ndim - 1)
    sw = _pack_small([p[n] for n in small_names])
    sm = _pack_small([mom[n] for n in small_names])
    sv = _pack_small([var[n] for n in small_names])
    sg = _pack_small([g_small[n] for n in small_names])
    _, d, nm, nv = _adamw(sg[None], sw, sm, sv, name="adamw_small", tr=sw.shape[0])
    shard_shapes = [p[n].shape for n in small_names]
    for n, dd, mm, vv in zip(small_names, _unpack_small(d, shard_shapes), _unpack_small(nm, shard_shapes),
                             _unpack_small(nv, shard_shapes)):
        out_g[n] = g_small[n].reshape(p[n].shape)
        out_d[n], out_m[n], out_v[n] = dd, mm, vv

    return (loss, dx.reshape(x.shape), *[out_g[n] for n in _WEIGHTS], *[out_d[n] for n in _WEIGHTS],
            *[out_m[n] for n in _WEIGHTS], *[out_v[n] for n in _WEIGHTS])
```
